```python
import math
import jax, jax.numpy as jnp
from jax import lax
import numpy as np

D_MODEL = 1024
BATCH = 16
SEQ = 2048
DEPTH = 2

N_MIXERS = 2
N_A_LAYERS = (DEPTH + 1) // 2
N_B_LAYERS = DEPTH // 2
EPS = 1e-6

A_HEADS = D_MODEL // 128
A_DK = 128
A_DV = 256
A_QK = A_HEADS * A_DK
A_VW = A_HEADS * A_DV
A_CONV = 4
A_CHUNK = 64
A_IN = 2 * A_QK + 2 * A_VW + 2 * A_HEADS

B_WINDOWS = (128, 512, 2048)
B_DILATIONS = (1, 4, 16)
B_GROUPS = len(B_WINDOWS)
B_HEADS = D_MODEL // 128
B_DH = 128
B_W = B_HEADS * B_DH
B_BLOCK = 128
B_IN = 3 * B_GROUPS * B_W + B_W
ROPE_THETA = 500000.0
ROPE_DIMS = B_DH // 4

kernel_name = "hybrid_gdn_dilated_swa"


def rms_norm(x, g):
    xf = x.astype(jnp.float32)
    y = xf * lax.rsqrt(jnp.mean(xf * xf, axis=-1, keepdims=True) + EPS)
    return (y * g.astype(jnp.float32)).astype(x.dtype)


def l2_norm(x):
    xf = x.astype(jnp.float32)
    return xf * lax.rsqrt(jnp.sum(xf * xf, axis=-1, keepdims=True) + EPS)


def causal_dwconv(x, w):
    K = w.shape[0]
    return lax.conv_general_dilated(
        x, w[:, None, :].astype(x.dtype), window_strides=(1,), padding=((K - 1, 0),),
        dimension_numbers=("NWC", "WIO", "NWC"), feature_group_count=x.shape[-1])


def gated_delta_rule(q, k, v, g, beta):
    Bn, H, S, DK = q.shape
    DV = v.shape[-1]
    C = A_CHUNK
    N = S // C
    f32 = jnp.float32
    q, k, v, g, beta = (t.astype(f32) for t in (q, k, v, g, beta))
    q = q * (DK ** -0.5)

    def chunk(t):
        return t.reshape((Bn, H, N, C) + t.shape[3:])

    q, k, v, g, beta = map(chunk, (q, k, v, g, beta))
    g = jnp.cumsum(g, axis=-1)
    idx = jnp.arange(C)
    causal = idx[:, None] >= idx[None, :]
    strict = idx[:, None] > idx[None, :]
    decay = jnp.exp(jnp.where(causal, g[..., :, None] - g[..., None, :], -jnp.inf))
    k_beta = k * beta[..., None]
    lower = jnp.where(strict, jnp.einsum("bhnid,bhnjd->bhnij", k_beta, k) * decay, 0.0)
    rhs = jnp.concatenate([v * beta[..., None], k_beta * jnp.exp(g)[..., None]], axis=-1)
    sol = lax.linalg.triangular_solve(lower + jnp.eye(C, dtype=f32), rhs,
                                      left_side=True, lower=True, unit_diagonal=True)
    u, w = sol[..., :DV], sol[..., DV:]
    qk = jnp.einsum("bhnid,bhnjd->bhnij", q, k) * decay

    def step(state, inp):
        q_c, k_c, u_c, w_c, g_c, qk_c = inp
        v_new = u_c - jnp.einsum("bhck,bhkv->bhcv", w_c, state)
        o = (jnp.einsum("bhck,bhkv->bhcv", q_c * jnp.exp(g_c)[..., None], state)
             + jnp.einsum("bhij,bhjv->bhiv", qk_c, v_new))
        g_last = g_c[..., -1:]
        state = (state * jnp.exp(g_last)[..., None]
                 + jnp.einsum("bhck,bhcv->bhkv", k_c * jnp.exp(g_last - g_c)[..., None], v_new))
        return state, o

    xs = tuple(jnp.moveaxis(t, 2, 0) for t in (q, k, u, w, g, qk))
    state0 = jnp.zeros((Bn, H, DK, DV), f32)
    _, o = lax.scan(step, state0, xs)
    return jnp.moveaxis(o, 0, 2).reshape(Bn, H, S, DV)


def gated_deltanet_mixer(h, w_in, conv_w, a_log, dt_bias, out_norm_g, w_out):
    Bn, S, _ = h.shape
    proj = h @ w_in
    n_qkv = 2 * A_QK + A_VW
    qkv = proj[..., :n_qkv]
    z = proj[..., n_qkv:n_qkv + A_VW]
    b_logit = proj[..., n_qkv + A_VW:n_qkv + A_VW + A_HEADS]
    a_logit = proj[..., n_qkv + A_VW + A_HEADS:]
    qkv = jax.nn.silu(causal_dwconv(qkv, conv_w))
    q = qkv[..., :A_QK].reshape(Bn, S, A_HEADS, A_DK).transpose(0, 2, 1, 3)
    k = qkv[..., A_QK:2 * A_QK].reshape(Bn, S, A_HEADS, A_DK).transpose(0, 2, 1, 3)
    v = qkv[..., 2 * A_QK:].reshape(Bn, S, A_HEADS, A_DV).transpose(0, 2, 1, 3)
    q, k = l2_norm(q), l2_norm(k)
    beta = jax.nn.sigmoid(b_logit.astype(jnp.float32)).transpose(0, 2, 1)
    g = (-jnp.exp(a_log.astype(jnp.float32))
         * jax.nn.softplus(a_logit.astype(jnp.float32) + dt_bias.astype(jnp.float32))).transpose(0, 2, 1)
    o = gated_delta_rule(q, k, v, g, beta).transpose(0, 2, 1, 3)
    o = rms_norm(o, out_norm_g).astype(h.dtype)
    o = o * jax.nn.silu(z.reshape(Bn, S, A_HEADS, A_DV))
    return o.reshape(Bn, S, A_VW) @ w_out


def partial_rope(x, cos, sin):
    half = ROPE_DIMS // 2
    x1, x2, xp = x[..., :half], x[..., half:ROPE_DIMS], x[..., ROPE_DIMS:]
    xf1, xf2 = x1.astype(jnp.float32), x2.astype(jnp.float32)
    r1 = (xf1 * cos - xf2 * sin).astype(x.dtype)
    r2 = (xf2 * cos + xf1 * sin).astype(x.dtype)
    return jnp.concatenate([r1, r2, xp], axis=-1)


def dilated_window_attention(q, k, v, dilation, span):
    Bn, S, H, DH = q.shape
    L = S // dilation
    nb = -(-L // B_BLOCK)
    Lp = nb * B_BLOCK

    def to_sub(t):
        t = t.reshape(Bn, L, dilation, H, DH).transpose(0, 2, 3, 1, 4)
        return jnp.pad(t, ((0, 0), (0, 0), (0, 0), (0, Lp - L), (0, 0)))

    qs, ks, vs = map(to_sub, (q, k, v))
    qb = qs.reshape(Bn, dilation, H, nb, B_BLOCK, DH)

    def band(t):
        tp = jnp.pad(t, ((0, 0), (0, 0), (0, 0), (B_BLOCK, 0), (0, 0)))
        prev = tp[..., :Lp, :].reshape(Bn, dilation, H, nb, B_BLOCK, DH)
        cur = tp[..., B_BLOCK:, :].reshape(Bn, dilation, H, nb, B_BLOCK, DH)
        return jnp.concatenate([prev, cur], axis=-2)

    kb, vb = band(ks), band(vs)
    qi = jnp.arange(B_BLOCK)[:, None]
    kj = jnp.arange(2 * B_BLOCK)[None, :]
    dist = qi + B_BLOCK - kj
    key_pos = jnp.arange(nb)[:, None, None] * B_BLOCK + kj[None] - B_BLOCK
    mask = (dist >= 0) & (dist <= span) & (key_pos >= 0)
    s = jnp.einsum("bdhnqe,bdhnke->bdhnqk", qb, kb,
                   preferred_element_type=jnp.float32) * (DH ** -0.5)
    s = jnp.where(mask, s, -jnp.inf)
    lse = jax.nn.logsumexp(s, axis=-1)
    p = jnp.exp(s - lse[..., None])
    o = jnp.einsum("bdhnqk,bdhnke->bdhnqe", p.astype(v.dtype), vb,
                   preferred_element_type=jnp.float32)

    def from_sub(t):
        t = t.reshape((Bn, dilation, H, Lp) + t.shape[5:])[:, :, :, :L]
        t = jnp.moveaxis(t, 3, 1)
        return t.reshape((Bn, S, H) + t.shape[4:])

    return from_sub(o), from_sub(lse)


def dilated_attention_mixer(h, positions, w_in, q_norm_g, k_norm_g, w_out):
    Bn, S, _ = h.shape
    proj = h @ w_in
    n_qkv = 3 * B_GROUPS * B_W
    qkv = proj[..., :n_qkv].reshape(Bn, S, 3, B_GROUPS, B_HEADS, B_DH)
    z = proj[..., n_qkv:]
    q, k, v = qkv[:, :, 0], qkv[:, :, 1], qkv[:, :, 2]
    q = rms_norm(q, q_norm_g[:, None, :])
    k = rms_norm(k, k_norm_g[:, None, :])
    inv_freq = ROPE_THETA ** (-jnp.arange(0, ROPE_DIMS, 2, dtype=jnp.float32) / ROPE_DIMS)
    ang = positions.astype(jnp.float32)[..., None] * inv_freq
    cos = jnp.cos(ang)[:, :, None, None, :]
    sin = jnp.sin(ang)[:, :, None, None, :]
    q, k = partial_rope(q, cos, sin), partial_rope(k, cos, sin)
    outs, lses = [], []
    for gi in range(B_GROUPS):
        o_g, lse_g = dilated_window_attention(q[:, :, gi], k[:, :, gi], v[:, :, gi],
                                              B_DILATIONS[gi], B_WINDOWS[gi] // B_DILATIONS[gi])
        outs.append(o_g)
        lses.append(lse_g)
    wts = jax.nn.softmax(jnp.stack(lses, axis=0), axis=0)
    o = jnp.sum(wts[..., None] * jnp.stack(outs, axis=0), axis=0).astype(h.dtype)
    o = o.reshape(Bn, S, B_W) * jax.nn.silu(z)
    return o @ w_out


def _fwd_setup_inputs(seed: int = 0) -> dict:
    key = jax.random.key(seed)
    ks = jax.random.split(key, 16)
    f32 = jnp.float32
    x = jax.random.normal(ks[0], (BATCH, SEQ, D_MODEL), f32)
    positions = jnp.broadcast_to(jnp.arange(SEQ, dtype=jnp.int32)[None, :], (BATCH, SEQ))
    norm_g = 1.0 + 0.02 * jax.random.normal(ks[1], (DEPTH, D_MODEL), f32)
    a_w_in = jax.random.normal(ks[2], (N_A_LAYERS, D_MODEL, A_IN), f32) * D_MODEL ** -0.5
    a_conv_w = jax.random.normal(ks[3], (N_A_LAYERS, A_CONV, 2 * A_QK + A_VW), f32) * A_CONV ** -0.5
    a_log = jnp.log(jax.random.uniform(ks[4], (N_A_LAYERS, A_HEADS), f32, 1.0, 16.0))
    dt = jnp.exp(jax.random.uniform(ks[5], (N_A_LAYERS, A_HEADS), f32,
                                    math.log(1e-3), math.log(1e-1)))
    a_dt_bias = dt + jnp.log(-jnp.expm1(-dt))
    a_norm_g = 1.0 + 0.02 * jax.random.normal(ks[6], (N_A_LAYERS, A_DV), f32)
    a_w_out = jax.random.normal(ks[7], (N_A_LAYERS, A_VW, D_MODEL), f32) * A_VW ** -0.5
    b_w_in = jax.random.normal(ks[8], (N_B_LAYERS, D_MODEL, B_IN), f32) * D_MODEL ** -0.5
    b_q_norm_g = 1.0 + 0.02 * jax.random.normal(ks[9], (N_B_LAYERS, B_GROUPS, B_DH), f32)
    b_k_norm_g = 1.0 + 0.02 * jax.random.normal(ks[10], (N_B_LAYERS, B_GROUPS, B_DH), f32)
    b_w_out = jax.random.normal(ks[11], (N_B_LAYERS, B_W, D_MODEL), f32) * B_W ** -0.5
    return {"x": x, "positions": positions, "norm_g": norm_g,
            "a_w_in": a_w_in, "a_conv_w": a_conv_w, "a_log": a_log, "a_dt_bias": a_dt_bias,
            "a_norm_g": a_norm_g, "a_w_out": a_w_out,
            "b_w_in": b_w_in, "b_q_norm_g": b_q_norm_g, "b_k_norm_g": b_k_norm_g,
            "b_w_out": b_w_out}


def _fwd_reference(x, positions, norm_g, a_w_in, a_conv_w, a_log, a_dt_bias, a_norm_g, a_w_out,
              b_w_in, b_q_norm_g, b_k_norm_g, b_w_out):
    for i in range(DEPTH):
        h = rms_norm(x, norm_g[i])
        j = i // N_MIXERS
        if i % N_MIXERS == 0:
            y = gated_deltanet_mixer(h, a_w_in[j], a_conv_w[j], a_log[j], a_dt_bias[j],
                                     a_norm_g[j], a_w_out[j])
        else:
            y = dilated_attention_mixer(h, positions, b_w_in[j], b_q_norm_g[j],
                                        b_k_norm_g[j], b_w_out[j])
        x = x + y.astype(x.dtype)
    return x


import jax as _jax
import jax.numpy as _jnp

TWIN_FORMAT = 'train_step'
FWD_PARAMS = ['x', 'positions', 'norm_g', 'a_w_in', 'a_conv_w', 'a_log', 'a_dt_bias', 'a_norm_g', 'a_w_out', 'b_w_in', 'b_q_norm_g', 'b_k_norm_g', 'b_w_out']
TWIN_WEIGHTS = ['norm_g', 'a_w_in', 'a_conv_w', 'a_log', 'a_dt_bias', 'a_norm_g', 'a_w_out', 'b_w_in', 'b_q_norm_g', 'b_k_norm_g', 'b_w_out']
TWIN_DIFF_INPUT = 'x'
TWIN_INPUTS = ['x', 'positions', 'norm_g', 'a_w_in', 'a_conv_w', 'a_log', 'a_dt_bias', 'a_norm_g', 'a_w_out', 'b_w_in', 'b_q_norm_g', 'b_k_norm_g', 'b_w_out', 'loss_target', 'm_norm_g', 'm_a_w_in', 'm_a_conv_w', 'm_a_log', 'm_a_dt_bias', 'm_a_norm_g', 'm_a_w_out', 'm_b_w_in', 'm_b_q_norm_g', 'm_b_k_norm_g', 'm_b_w_out', 'v_norm_g', 'v_a_w_in', 'v_a_conv_w', 'v_a_log', 'v_a_dt_bias', 'v_a_norm_g', 'v_a_w_out', 'v_b_w_in', 'v_b_q_norm_g', 'v_b_k_norm_g', 'v_b_w_out']
TWIN_OUTPUTS = ['loss', 'grad_x', 'grad_norm_g', 'grad_a_w_in', 'grad_a_conv_w', 'grad_a_log', 'grad_a_dt_bias', 'grad_a_norm_g', 'grad_a_w_out', 'grad_b_w_in', 'grad_b_q_norm_g', 'grad_b_k_norm_g', 'grad_b_w_out', 'delta_norm_g', 'delta_a_w_in', 'delta_a_conv_w', 'delta_a_log', 'delta_a_dt_bias', 'delta_a_norm_g', 'delta_a_w_out', 'delta_b_w_in', 'delta_b_q_norm_g', 'delta_b_k_norm_g', 'delta_b_w_out', 'new_m_norm_g', 'new_m_a_w_in', 'new_m_a_conv_w', 'new_m_a_log', 'new_m_a_dt_bias', 'new_m_a_norm_g', 'new_m_a_w_out', 'new_m_b_w_in', 'new_m_b_q_norm_g', 'new_m_b_k_norm_g', 'new_m_b_w_out', 'new_v_norm_g', 'new_v_a_w_in', 'new_v_a_conv_w', 'new_v_a_log', 'new_v_a_dt_bias', 'new_v_a_norm_g', 'new_v_a_w_out', 'new_v_b_w_in', 'new_v_b_q_norm_g', 'new_v_b_k_norm_g', 'new_v_b_w_out']
TWIN_LEAF_KINDS = {'loss': 'loss', 'grad_x': 'grad_x', 'grad_norm_g': 'grad_w', 'grad_a_w_in': 'grad_w', 'grad_a_conv_w': 'grad_w', 'grad_a_log': 'grad_w', 'grad_a_dt_bias': 'grad_w', 'grad_a_norm_g': 'grad_w', 'grad_a_w_out': 'grad_w', 'grad_b_w_in': 'grad_w', 'grad_b_q_norm_g': 'grad_w', 'grad_b_k_norm_g': 'grad_w', 'grad_b_w_out': 'grad_w', 'delta_norm_g': 'delta_w', 'delta_a_w_in': 'delta_w', 'delta_a_conv_w': 'delta_w', 'delta_a_log': 'delta_w', 'delta_a_dt_bias': 'delta_w', 'delta_a_norm_g': 'delta_w', 'delta_a_w_out': 'delta_w', 'delta_b_w_in': 'delta_w', 'delta_b_q_norm_g': 'delta_w', 'delta_b_k_norm_g': 'delta_w', 'delta_b_w_out': 'delta_w', 'new_m_norm_g': 'new_m', 'new_m_a_w_in': 'new_m', 'new_m_a_conv_w': 'new_m', 'new_m_a_log': 'new_m', 'new_m_a_dt_bias': 'new_m', 'new_m_a_norm_g': 'new_m', 'new_m_a_w_out': 'new_m', 'new_m_b_w_in': 'new_m', 'new_m_b_q_norm_g': 'new_m', 'new_m_b_k_norm_g': 'new_m', 'new_m_b_w_out': 'new_m', 'new_v_norm_g': 'new_v', 'new_v_a_w_in': 'new_v', 'new_v_a_conv_w': 'new_v', 'new_v_a_log': 'new_v', 'new_v_a_dt_bias': 'new_v', 'new_v_a_norm_g': 'new_v', 'new_v_a_w_out': 'new_v', 'new_v_b_w_in': 'new_v', 'new_v_b_q_norm_g': 'new_v', 'new_v_b_k_norm_g': 'new_v', 'new_v_b_w_out': 'new_v'}


def _forward(args):
    return _fwd_reference(*[args[k] for k in FWD_PARAMS])


def _output_shape():
    out = _jax.eval_shape(lambda: _forward(_fwd_setup_inputs(0)))
    return out.shape, out.dtype

N_MICROBATCH = 1
ADAM_LR = 0.001
ADAM_B1 = 0.9
ADAM_B2 = 0.999
ADAM_EPS = 1e-08
ADAM_WD = 0.01
ADAM_STEP = 10
PER_EXAMPLE_BATCH_AXIS = {'x': 0, 'positions': 0, 'loss_target': 0}
SHARED_INPUTS = []
_WEIGHT_DTYPES = {'norm_g': _jnp.float32, 'a_w_in': _jnp.float32, 'a_conv_w': _jnp.float32, 'a_log': _jnp.float32, 'a_dt_bias': _jnp.float32, 'a_norm_g': _jnp.float32, 'a_w_out': _jnp.float32, 'b_w_in': _jnp.float32, 'b_q_norm_g': _jnp.float32, 'b_k_norm_g': _jnp.float32, 'b_w_out': _jnp.float32}
MOMENT_SCALE = {'norm_g': 9.499114e+00, 'a_w_in': 2.043422e-01, 'a_conv_w': 2.881479e-01, 'a_log': 4.117151e+01, 'a_dt_bias': 3.923641e+01, 'a_norm_g': 4.580041e+01, 'a_w_out': 5.227350e-01, 'b_w_in': 2.263073e-02, 'b_q_norm_g': 1.672374e-01, 'b_k_norm_g': 1.671981e-01, 'b_w_out': 3.678405e-02}


def _to_microbatches(a, axis):
    t = _jnp.moveaxis(a, axis, 0)
    t = t.reshape((N_MICROBATCH, t.shape[0] // N_MICROBATCH) + t.shape[1:])
    return _jnp.moveaxis(t, 1, axis + 1)


def setup_inputs(seed: int = 0) -> dict:
    inp = _fwd_setup_inputs(seed)
    key = _jax.random.fold_in(_jax.random.key(seed), 7919)
    shape, _ = _output_shape()
    out = dict(inp)
    out["loss_target"] = _jax.random.normal(_jax.random.fold_in(key, 0), shape, _jnp.float32)
    for i, name in enumerate(TWIN_WEIGHTS):
        w = inp[name].astype(_jnp.float32)
        if MOMENT_SCALE is None:
            s = _jnp.sqrt(_jnp.mean(_jnp.square(w)) + 1e-30)
        else:
            s = MOMENT_SCALE[name]
        km, kv = _jax.random.split(_jax.random.fold_in(key, i + 1))
        out[name] = w
        out["m_" + name] = s * _jax.random.normal(km, w.shape, _jnp.float32)
        out["v_" + name] = (s * s) * _jax.random.uniform(kv, w.shape, _jnp.float32, 0.5, 1.5)
    if N_MICROBATCH > 1:
        for name, axis in PER_EXAMPLE_BATCH_AXIS.items():
            out[name] = _to_microbatches(out[name], axis)
    return {'x': out['x'], 'positions': out['positions'], 'norm_g': out['norm_g'], 'a_w_in': out['a_w_in'], 'a_conv_w': out['a_conv_w'], 'a_log': out['a_log'], 'a_dt_bias': out['a_dt_bias'], 'a_norm_g': out['a_norm_g'], 'a_w_out': out['a_w_out'], 'b_w_in': out['b_w_in'], 'b_q_norm_g': out['b_q_norm_g'], 'b_k_norm_g': out['b_k_norm_g'], 'b_w_out': out['b_w_out'], 'loss_target': out['loss_target'], 'm_norm_g': out['m_norm_g'], 'm_a_w_in': out['m_a_w_in'], 'm_a_conv_w': out['m_a_conv_w'], 'm_a_log': out['m_a_log'], 'm_a_dt_bias': out['m_a_dt_bias'], 'm_a_norm_g': out['m_a_norm_g'], 'm_a_w_out': out['m_a_w_out'], 'm_b_w_in': out['m_b_w_in'], 'm_b_q_norm_g': out['m_b_q_norm_g'], 'm_b_k_norm_g': out['m_b_k_norm_g'], 'm_b_w_out': out['m_b_w_out'], 'v_norm_g': out['v_norm_g'], 'v_a_w_in': out['v_a_w_in'], 'v_a_conv_w': out['v_a_conv_w'], 'v_a_log': out['v_a_log'], 'v_a_dt_bias': out['v_a_dt_bias'], 'v_a_norm_g': out['v_a_norm_g'], 'v_a_w_out': out['v_a_w_out'], 'v_b_w_in': out['v_b_w_in'], 'v_b_q_norm_g': out['v_b_q_norm_g'], 'v_b_k_norm_g': out['v_b_k_norm_g'], 'v_b_w_out': out['v_b_w_out']}


def _loss(weights, diff, rest, loss_target):
    with _jax.named_scope("forward"):
        args = {**rest, TWIN_DIFF_INPUT: diff, **{k: w.astype(_WEIGHT_DTYPES[k]) for k, w in weights.items()}}
        y = _forward(args)
    with _jax.named_scope("loss_head"):
        err = _jnp.square(y.astype(_jnp.float32) - loss_target)
        return 0.5 * _jnp.sum(_jnp.mean(err, axis=-1)) if err.ndim else 0.5 * err


def _adamw(w, g, m, v):
    m = ADAM_B1 * m + (1.0 - ADAM_B1) * g
    v = ADAM_B2 * v + (1.0 - ADAM_B2) * _jnp.square(g)
    m_hat = m / (1.0 - ADAM_B1 ** ADAM_STEP)
    v_hat = v / (1.0 - ADAM_B2 ** ADAM_STEP)
    delta = -ADAM_LR * (m_hat / (_jnp.sqrt(v_hat) + ADAM_EPS) + ADAM_WD * w)
    return delta, m, v


def reference(x, positions, norm_g, a_w_in, a_conv_w, a_log, a_dt_bias, a_norm_g, a_w_out, b_w_in, b_q_norm_g, b_k_norm_g, b_w_out, loss_target, m_norm_g, m_a_w_in, m_a_conv_w, m_a_log, m_a_dt_bias, m_a_norm_g, m_a_w_out, m_b_w_in, m_b_q_norm_g, m_b_k_norm_g, m_b_w_out, v_norm_g, v_a_w_in, v_a_conv_w, v_a_log, v_a_dt_bias, v_a_norm_g, v_a_w_out, v_b_w_in, v_b_q_norm_g, v_b_k_norm_g, v_b_w_out):
    given = dict(x=x, positions=positions, norm_g=norm_g, a_w_in=a_w_in, a_conv_w=a_conv_w, a_log=a_log, a_dt_bias=a_dt_bias, a_norm_g=a_norm_g, a_w_out=a_w_out, b_w_in=b_w_in, b_q_norm_g=b_q_norm_g, b_k_norm_g=b_k_norm_g, b_w_out=b_w_out, loss_target=loss_target, m_norm_g=m_norm_g, m_a_w_in=m_a_w_in, m_a_conv_w=m_a_conv_w, m_a_log=m_a_log, m_a_dt_bias=m_a_dt_bias, m_a_norm_g=m_a_norm_g, m_a_w_out=m_a_w_out, m_b_w_in=m_b_w_in, m_b_q_norm_g=m_b_q_norm_g, m_b_k_norm_g=m_b_k_norm_g, m_b_w_out=m_b_w_out, v_norm_g=v_norm_g, v_a_w_in=v_a_w_in, v_a_conv_w=v_a_conv_w, v_a_log=v_a_log, v_a_dt_bias=v_a_dt_bias, v_a_norm_g=v_a_norm_g, v_a_w_out=v_a_w_out, v_b_w_in=v_b_w_in, v_b_q_norm_g=v_b_q_norm_g, v_b_k_norm_g=v_b_k_norm_g, v_b_w_out=v_b_w_out)
    weights = {n: given[n] for n in TWIN_WEIGHTS}
    shared = {n: given[n] for n in SHARED_INPUTS}
    per_example = {n: given[n] for n in ['x', 'positions']}
    grad_fn = _jax.value_and_grad(_loss, argnums=(0, 1))

    def one_microbatch(ex, loss_target):
        ex = dict(ex)
        diff = ex.pop(TWIN_DIFF_INPUT)
        return grad_fn(weights, diff, {**shared, **ex}, loss_target)

    if N_MICROBATCH == 1:
        loss, (grad_w, grad_x) = one_microbatch(per_example, given["loss_target"])
    else:
        def body(carry, xs):
            loss_sum, grad_sum = carry
            l_k, (gw_k, gx_k) = one_microbatch(xs[0], xs[1])
            with _jax.named_scope("update"):
                return (loss_sum + l_k, _jax.tree.map(_jnp.add, grad_sum, gw_k)), gx_k

        init = (_jnp.zeros((), _jnp.float32), _jax.tree.map(_jnp.zeros_like, weights))
        (loss, grad_w), grad_x = _jax.lax.scan(body, init, (per_example, given["loss_target"]))
    with _jax.named_scope("update"):
        delta_w, new_m, new_v = {}, {}, {}
        for n in TWIN_WEIGHTS:
            delta_w[n], new_m[n], new_v[n] = _adamw(weights[n], grad_w[n], given["m_" + n], given["v_" + n])
    return (loss, grad_x, *[grad_w[n] for n in TWIN_WEIGHTS], *[delta_w[n] for n in TWIN_WEIGHTS],
            *[new_m[n] for n in TWIN_WEIGHTS], *[new_v[n] for n in TWIN_WEIGHTS])
```

```python
import functools
import math

import jax
import jax.numpy as jnp
import numpy as np
from jax import lax
from jax.experimental import pallas as pl
from jax.experimental.pallas import tpu as pltpu

F32 = jnp.float32
BF16 = jnp.bfloat16
MESH = pl.DeviceIdType.MESH

EPS = 1e-6
D_MODEL = 1024
A_HEADS = 8
A_DK = 128
A_DV = 256
A_QK = A_HEADS * A_DK
A_VW = A_HEADS * A_DV
A_MAIN = 2 * A_QK + 2 * A_VW
A_HEAD_COLS = 2 * A_DK + 2 * A_DV
A_CONV_COLS = 2 * A_DK + A_DV
A_CHUNK = 64
A_CONV = 4
B_GROUPS = 3
B_HEADS = 8
B_DH = 128
B_W = B_HEADS * B_DH
B_DIL = (1, 4, 16)
B_BLK = 128
ROPE_THETA = 500000.0
ROPE_DIMS = B_DH // 4
ADAM_LR, ADAM_B1, ADAM_B2, ADAM_EPS, ADAM_WD, ADAM_STEP = 0.001, 0.9, 0.999, 1e-08, 0.01, 10
N_CHIPS = 4
VMEM_BIG = 56 * 1024 * 1024


def _params(sem=None, vmem=None):
    return pltpu.CompilerParams(dimension_semantics=sem, vmem_limit_bytes=vmem)


def _dot(a, b, ca, cb):
    return lax.dot_general(a.astype(BF16), b.astype(BF16), (((ca,), (cb,)), ((), ())),
                           preferred_element_type=F32)


def _split3(a):
    hi = a.astype(BF16)
    r = a - hi.astype(F32)
    mid = r.astype(BF16)
    lo = (r - mid.astype(F32)).astype(BF16)
    return hi, mid, lo


def _dot_hi(a, b, ca, cb):
    a_hi, a_lo, _ = _split3(a)
    b_hi, b_lo, _ = _split3(b)
    dn = (((ca,), (cb,)), ((), ()))
    out = lax.dot_general(a_hi, b_hi, dn, preferred_element_type=F32)
    out = out + lax.dot_general(a_hi, b_lo, dn, preferred_element_type=F32)
    return out + lax.dot_general(a_lo, b_hi, dn, preferred_element_type=F32)


def _sigmoid(y):
    return 1.0 / (1.0 + jnp.exp(-y))


def _silu(y):
    return y * _sigmoid(y)


def _dsilu(y):
    s = _sigmoid(y)
    return s * (1.0 + y * (1.0 - s))


def _matmul(a, b, mode, out_dtype, name, res=None, tm=512, tn=512, tk=1024):
    if mode == "nn":
        (m, k), (_, n) = a.shape, b.shape
    elif mode == "nt":
        (m, k), (n, _) = a.shape, b.shape
    else:
        (k, m), (_, n) = a.shape, b.shape
    tm, tn, tk = min(tm, m), min(tn, n), min(tk, k)
    assert m % tm == 0 and n % tn == 0 and k % tk == 0, (name, a.shape, b.shape)
    nk = k // tk
    dims = {"nn": ((1,), (0,)), "nt": ((1,), (1,)), "tn": ((0,), (0,))}[mode]

    def body(*refs):
        if res is None:
            a_ref, b_ref, o_ref, acc = refs
        else:
            a_ref, b_ref, r_ref, o_ref, acc = refs
        kk = pl.program_id(2)

        @pl.when(kk == 0)
        def _():
            acc[...] = jnp.zeros_like(acc)

        acc[...] += lax.dot_general(a_ref[...], b_ref[...], (dims, ((), ())),
                                    preferred_element_type=F32)

        @pl.when(kk == nk - 1)
        def _():
            r = acc[...]
            if res is not None:
                r = r + r_ref[...]
            o_ref[...] = r.astype(out_dtype)

    if mode == "tn":
        a_spec = pl.BlockSpec((tk, tm), lambda i, j, kk: (kk, i))
    else:
        a_spec = pl.BlockSpec((tm, tk), lambda i, j, kk: (i, kk))
    if mode == "nt":
        b_spec = pl.BlockSpec((tn, tk), lambda i, j, kk: (j, kk))
    else:
        b_spec = pl.BlockSpec((tk, tn), lambda i, j, kk: (kk, j))
    in_specs = [a_spec, b_spec]
    args = [a, b]
    if res is not None:
        in_specs.append(pl.BlockSpec((tm, tn), lambda i, j, kk: (i, j)))
        args.append(res)
    return pl.pallas_call(
        body, name=name, grid=(m // tm, n // tn, nk),
        in_specs=in_specs, out_specs=pl.BlockSpec((tm, tn), lambda i, j, kk: (i, j)),
        out_shape=jax.ShapeDtypeStruct((m, n), out_dtype),
        scratch_shapes=[pltpu.VMEM((tm, tn), F32)],
        compiler_params=_params(("parallel", "parallel", "arbitrary"), 40 * 1024 * 1024),
    )(*args)


def _rms_fwd(x, g, name, tm=256):
    t, d = x.shape

    def body(x_ref, g_ref, h_ref):
        xv = x_ref[...]
        r = lax.rsqrt(jnp.mean(xv * xv, axis=-1, keepdims=True) + EPS)
        h_ref[...] = (xv * r * g_ref[...]).astype(BF16)

    return pl.pallas_call(
        body, name=name, grid=(t // tm,),
        in_specs=[pl.BlockSpec((tm, d), lambda i: (i, 0)), pl.BlockSpec((1, d), lambda i: (0, 0))],
        out_specs=pl.BlockSpec((tm, d), lambda i: (i, 0)),
        out_shape=jax.ShapeDtypeStruct((t, d), BF16),
        compiler_params=_params(("parallel",)),
    )(x, g)


def _rms_bwd(x, g, dhs, dres, name, tm=256):
    t, d = x.shape
    n_dh = len(dhs)

    def body(*refs):
        x_ref, g_ref = refs[0], refs[1]
        dh_refs = refs[2:2 + n_dh]
        dres_ref, dx_ref, dg_ref = refs[2 + n_dh:]
        i = pl.program_id(0)

        @pl.when(i == 0)
        def _():
            dg_ref[...] = jnp.zeros_like(dg_ref)

        xv = x_ref[...]
        r = lax.rsqrt(jnp.mean(xv * xv, axis=-1, keepdims=True) + EPS)
        xh = xv * r
        dh = dh_refs[0][...]
        for ref in dh_refs[1:]:
            dh = dh + ref[...]
        dg_ref[0:1, :] += jnp.sum(dh * xh, axis=0, keepdims=True)
        dxh = dh * g_ref[...]
        dx = r * (dxh - xh * jnp.mean(dxh * xh, axis=-1, keepdims=True))
        dx_ref[...] = dx + dres_ref[...]

    row = pl.BlockSpec((tm, d), lambda i: (i, 0))
    dx, dg = pl.pallas_call(
        body, name=name, grid=(t // tm,),
        in_specs=[row, pl.BlockSpec((1, d), lambda i: (0, 0))] + [row] * n_dh + [row],
        out_specs=[row, pl.BlockSpec((8, d), lambda i: (0, 0))],
        out_shape=[jax.ShapeDtypeStruct((t, d), F32), jax.ShapeDtypeStruct((8, d), F32)],
        compiler_params=_params(("arbitrary",)),
    )(x, g, *dhs, dres)
    return dx, dg[0:1]


def _loss_grad(y, target, name="loss_grad", tm=256):
    t, d = y.shape
    nb = t // tm

    def body(y_ref, t_ref, dy_ref, part_ref):
        e = y_ref[...] - t_ref[...]
        dy_ref[...] = e * (1.0 / d)
        s = jnp.sum(jnp.sum(e * e, axis=1, keepdims=True), axis=0, keepdims=True) * (0.5 / d)
        part_ref[...] = jnp.broadcast_to(s, (8, 128))

    row = pl.BlockSpec((tm, d), lambda i: (i, 0))
    dy, part = pl.pallas_call(
        body, name=name, grid=(nb,), in_specs=[row, row],
        out_specs=[row, pl.BlockSpec((None, 8, 128), lambda i: (i, 0, 0))],
        out_shape=[jax.ShapeDtypeStruct((t, d), F32), jax.ShapeDtypeStruct((nb, 8, 128), F32)],
        compiler_params=_params(("parallel",)),
    )(y, target)
    return dy, part[:, 0, 0]


def _softplus(x):
    t = jnp.exp(-jnp.abs(x))
    return jnp.maximum(x, 0.0) + jnp.where(t < 1e-3, t * (1.0 - 0.5 * t), jnp.log(1.0 + t))


def _tri(rows_le_cols):
    r = lax.broadcasted_iota(jnp.int32, (A_CHUNK, A_CHUNK), 0)
    c = lax.broadcasted_iota(jnp.int32, (A_CHUNK, A_CHUNK), 1)
    return jnp.where((r <= c) if rows_le_cols else (r >= c), 1.0, 0.0).astype(BF16)


def _dot_exact_rhs(a, ones_bf16):
    dn = (((1,), (0,)), ((), ()))
    hi, mid, lo = _split3(a)
    out = lax.dot_general(hi, ones_bf16, dn, preferred_element_type=F32)
    out = out + lax.dot_general(mid, ones_bf16, dn, preferred_element_type=F32)
    return out + lax.dot_general(lo, ones_bf16, dn, preferred_element_type=F32)


def _gdn_prep(tail_t, a_log, dt_bias):
    bn, _, n, c = tail_t.shape

    def body(t_ref, alog_ref, dtb_ref, beta_ref, gc_ref):
        upper = _tri(True)
        for h in range(A_HEADS):
            beta_ref[h] = _sigmoid(t_ref[h])
            ea = jnp.exp(jnp.full((n, c), alog_ref[h], F32))
            g = -ea * _softplus(t_ref[A_HEADS + h] + dtb_ref[h])
            gc_ref[h] = _dot_exact_rhs(g, upper)

    smem = pl.BlockSpec(memory_space=pltpu.SMEM)
    blk = pl.BlockSpec((None, A_HEADS, n, c), lambda b: (b, 0, 0, 0))
    return pl.pallas_call(
        body, name="gdn_prep", grid=(bn,),
        in_specs=[pl.BlockSpec((None, 2 * A_HEADS, n, c), lambda b: (b, 0, 0, 0)), smem, smem],
        out_specs=[blk, blk],
        out_shape=[jax.ShapeDtypeStruct((bn, A_HEADS, n, c), F32)] * 2,
        compiler_params=_params(("parallel",)),
    )(tail_t, a_log, dt_bias)


def _gdn_prep_bwd(tail_t, a_log, dt_bias, d_gc, d_beta):
    bn, _, n, c = tail_t.shape

    def body(t_ref, alog_ref, dtb_ref, dgc_ref, dbeta_ref, dt_ref, dal_ref, ddt_ref):
        lower = _tri(False)
        for h in range(A_HEADS):
            beta = _sigmoid(t_ref[h])
            dt_ref[h] = dbeta_ref[h] * beta * (1.0 - beta)
            dg = _dot_exact_rhs(dgc_ref[h], lower)
            ea = jnp.exp(jnp.full((n, c), alog_ref[h], F32))
            xa = t_ref[A_HEADS + h] + dtb_ref[h]
            g = -ea * _softplus(xa)
            dxa = -ea * dg * _sigmoid(xa)
            dt_ref[A_HEADS + h] = dxa
            s1 = jnp.sum(jnp.sum(g * dg, axis=1, keepdims=True), axis=0, keepdims=True)
            s2 = jnp.sum(jnp.sum(dxa, axis=1, keepdims=True), axis=0, keepdims=True)
            dal_ref[h:h + 1, :] = jnp.broadcast_to(s1, (1, 128))
            ddt_ref[h:h + 1, :] = jnp.broadcast_to(s2, (1, 128))

    smem = pl.BlockSpec(memory_space=pltpu.SMEM)
    blk8 = pl.BlockSpec((None, A_HEADS, n, c), lambda b: (b, 0, 0, 0))
    blk16 = pl.BlockSpec((None, 2 * A_HEADS, n, c), lambda b: (b, 0, 0, 0))
    sm = pl.BlockSpec((None, A_HEADS, 128), lambda b: (b, 0, 0))
    return pl.pallas_call(
        body, name="gdn_prep_bwd", grid=(bn,),
        in_specs=[blk16, smem, smem, blk8, blk8],
        out_specs=[blk16, sm, sm],
        out_shape=[jax.ShapeDtypeStruct((bn, 2 * A_HEADS, n, c), F32),
                   jax.ShapeDtypeStruct((bn, A_HEADS, 128), F32),
                   jax.ShapeDtypeStruct((bn, A_HEADS, 128), F32)],
        compiler_params=_params(("parallel",)),
    )(tail_t, a_log, dt_bias, d_gc, d_beta)


HALO = 8


def _conv_window(x_ref, n, first, lo, width):
    if first:
        return jnp.concatenate([jnp.zeros((HALO, width), F32), x_ref[0:A_CHUNK, lo:lo + width]], axis=0)
    start = pl.multiple_of(n * A_CHUNK - HALO, HALO)
    return x_ref[pl.ds(start, A_CHUNK + HALO), lo:lo + width]


def _conv_taps(xw, w):
    y = w[A_CONV - 1:A_CONV, :] * xw
    for j in range(1, A_CONV):
        y = y + w[A_CONV - 1 - j:A_CONV - j, :] * pltpu.roll(xw, j, 0)
    return y[HALO:, :]


def _row_to_col(row, eye):
    c = eye.shape[0]
    return jnp.sum(jnp.where(eye, jnp.broadcast_to(row, (c, c)), 0.0), axis=1, keepdims=True)


def _col_to_row(col, eye):
    c = eye.shape[0]
    return jnp.sum(jnp.where(eye, jnp.broadcast_to(col, (c, c)), 0.0), axis=0, keepdims=True)


def _unit_lower_inverse(a, ri, ci):
    eye = jnp.where(ri == ci, 1.0, 0.0)
    a8 = jnp.where((ri >> 3) == (ci >> 3), a, 0.0)
    a2 = _dot_hi(a8, a8, 1, 0)
    a4 = _dot_hi(a2, a2, 1, 0)
    t = eye - a8
    t = t + _dot_hi(t, a2, 1, 0)
    t = t + _dot_hi(t, a4, 1, 0)
    for sh in (3, 4, 5):
        off = jnp.where(((ri >> (sh + 1)) == (ci >> (sh + 1))) & ((ri >> sh) != (ci >> sh)), a, 0.0)
        t = t - _dot_hi(_dot_hi(t, off, 1, 0), t, 1, 0)
    return t


def _gdn_chunk_inputs(x_ref, cw, n, first):
    xw = _conv_window(x_ref, n, first, 0, A_CONV_COLS)
    y = _conv_taps(xw, cw)
    a = _silu(y)
    aq, ak, v = a[:, 0:A_DK], a[:, A_DK:2 * A_DK], a[:, 2 * A_DK:]
    rq = lax.rsqrt(jnp.sum(aq * aq, axis=1, keepdims=True) + EPS)
    rk = lax.rsqrt(jnp.sum(ak * ak, axis=1, keepdims=True) + EPS)
    return dict(xw=xw, y=y, aq=aq, ak=ak, rq=rq, rk=rk,
                q=aq * rq * (A_DK ** -0.5), k=ak * rk, v=v)


def _gdn_chunk_core(q, k, v, g_row, b_row, t_mat, ri, ci):
    eye = ri == ci
    g_col = _row_to_col(g_row, eye)
    b_col = _row_to_col(b_row, eye)
    causal = ri >= ci
    strict = ri > ci
    dec = jnp.where(causal, jnp.exp(jnp.where(causal, g_col - g_row, 0.0)), 0.0)
    gam = jnp.exp(g_col)
    g_last = g_row[:, A_CHUNK - 1:A_CHUNK]
    gam_last = jnp.exp(g_last)
    e = jnp.exp(g_last - g_col)
    kb = k * b_col
    bv = v * b_col
    kbg = kb * gam
    kk = _dot(kb, k, 1, 1)
    a_mat = jnp.where(strict, kk * dec, 0.0)
    if t_mat is None:
        t_mat = _unit_lower_inverse(a_mat, ri, ci)
    u = _dot(t_mat, bv, 1, 0)
    w = _dot(t_mat, kbg, 1, 0)
    p = _dot(q, k, 1, 1) * dec
    return dict(eye=eye, g_col=g_col, b_col=b_col, dec=dec, strict=strict, causal=causal, gam=gam,
                gam_last=gam_last, e=e, kb=kb, bv=bv, kbg=kbg, a_mat=a_mat, t_mat=t_mat, u=u, w=w, p=p,
                qg=q * gam, kd=k * e)


def _gdn_fwd(proj_hm, cw_hm, beta, gc, norm_g):
    bn, s, _ = proj_hm.shape
    n = s // A_CHUNK

    def body(x_ref, cw_ref, beta_ref, gc_ref, ng_ref, og_ref, oraw_ref, st_ref, t_ref, state):
        ri = lax.broadcasted_iota(jnp.int32, (A_CHUNK, A_CHUNK), 0)
        ci = lax.broadcasted_iota(jnp.int32, (A_CHUNK, A_CHUNK), 1)
        cw = cw_ref[...]
        ng = ng_ref[...]
        state[...] = jnp.zeros_like(state)

        def chunk(i, first):
            rows = pl.ds(0 if first else pl.multiple_of(i * A_CHUNK, A_CHUNK), A_CHUNK)
            cin = _gdn_chunk_inputs(x_ref, cw, i, first)
            core = _gdn_chunk_core(cin["q"], cin["k"], cin["v"], gc_ref[pl.ds(i, 1), :],
                                   beta_ref[pl.ds(i, 1), :], None, ri, ci)
            st = state[...]
            st_ref[i] = st
            t_ref[i] = core["t_mat"]
            vn = core["u"] - _dot(core["w"], st, 1, 0)
            o = _dot(core["qg"], st, 1, 0) + _dot(core["p"], vn, 1, 0)
            state[...] = st * core["gam_last"] + _dot(core["kd"], vn, 0, 0)
            oraw_ref[rows, :] = o
            r = lax.rsqrt(jnp.mean(o * o, axis=1, keepdims=True) + EPS)
            z = x_ref[rows, A_CONV_COLS:A_HEAD_COLS]
            og_ref[rows, :] = (o * r * ng * _silu(z)).astype(BF16)

        chunk(0, True)
        lax.fori_loop(1, n, lambda i, c: (chunk(i, False), c)[1], 0)

    return pl.pallas_call(
        body, name="gdn_fwd", grid=(bn, A_HEADS),
        in_specs=[pl.BlockSpec((None, s, A_HEAD_COLS), lambda b, h: (b, 0, h)),
                  pl.BlockSpec((A_CONV, A_CONV_COLS), lambda b, h: (0, h)),
                  pl.BlockSpec((None, None, n, A_CHUNK), lambda b, h: (b, h, 0, 0)),
                  pl.BlockSpec((None, None, n, A_CHUNK), lambda b, h: (b, h, 0, 0)),
                  pl.BlockSpec((1, A_DV), lambda b, h: (0, 0))],
        out_specs=[pl.BlockSpec((None, s, A_DV), lambda b, h: (b, 0, h)),
                   pl.BlockSpec((None, s, A_DV), lambda b, h: (b, 0, h)),
                   pl.BlockSpec((None, None, n, A_DK, A_DV), lambda b, h: (b, h, 0, 0, 0)),
                   pl.BlockSpec((None, None, n, A_CHUNK, A_CHUNK), lambda b, h: (b, h, 0, 0, 0))],
        out_shape=[jax.ShapeDtypeStruct((bn, s, A_VW), BF16),
                   jax.ShapeDtypeStruct((bn, s, A_VW), F32),
                   jax.ShapeDtypeStruct((bn, A_HEADS, n, A_DK, A_DV), F32),
                   jax.ShapeDtypeStruct((bn, A_HEADS, n, A_CHUNK, A_CHUNK), F32)],
        scratch_shapes=[pltpu.VMEM((A_DK, A_DV), F32)],
        compiler_params=_params(("parallel", "parallel"), VMEM_BIG),
    )(proj_hm, cw_hm, beta, gc, norm_g)


def _gdn_bwd(proj_hm, cw_hm, beta, gc, norm_g, oraw, states, t_mats, dog):
    bn, s, _ = proj_hm.shape
    n = s // A_CHUNK

    def body(x_ref, cw_ref, beta_ref, gc_ref, ng_ref, oraw_ref, st_ref, t_ref, dog_ref,
             dx_ref, dgc_ref, dbeta_ref, dcw_ref, dng_ref, dstate, dy_next):
        ri = lax.broadcasted_iota(jnp.int32, (A_CHUNK, A_CHUNK), 0)
        ci = lax.broadcasted_iota(jnp.int32, (A_CHUNK, A_CHUNK), 1)
        cw = cw_ref[...]
        ng = ng_ref[...]
        dstate[...] = jnp.zeros_like(dstate)
        dy_next[...] = jnp.zeros_like(dy_next)
        dcw_ref[...] = jnp.zeros_like(dcw_ref)
        dng_ref[...] = jnp.zeros_like(dng_ref)

        def chunk(i, first):
            rows = pl.ds(0 if first else pl.multiple_of(i * A_CHUNK, A_CHUNK), A_CHUNK)
            cin = _gdn_chunk_inputs(x_ref, cw, i, first)
            q, k, v = cin["q"], cin["k"], cin["v"]
            g_row = gc_ref[pl.ds(i, 1), :]
            b_row = beta_ref[pl.ds(i, 1), :]
            cr = _gdn_chunk_core(q, k, v, g_row, b_row, t_ref[i], ri, ci)
            eye, dec, gam, e = cr["eye"], cr["dec"], cr["gam"], cr["e"]
            b_col, t_mat, u, w, p = cr["b_col"], cr["t_mat"], cr["u"], cr["w"], cr["p"]
            st = st_ref[i]
            ds_out = dstate[...]

            o = oraw_ref[rows, :]
            z = x_ref[rows, A_CONV_COLS:A_HEAD_COLS]
            d_og = dog_ref[rows, :]
            r = lax.rsqrt(jnp.mean(o * o, axis=1, keepdims=True) + EPS)
            oh = o * r
            d_on = d_og * _silu(z)
            dz = d_og * oh * ng * _dsilu(z)
            dng_ref[0:1, :] += jnp.sum(d_on * oh, axis=0, keepdims=True)
            d_oh = d_on * ng
            d_o = r * (d_oh - oh * jnp.mean(d_oh * oh, axis=1, keepdims=True))

            vn = u - _dot(w, st, 1, 0)
            d_vn = _dot(p, d_o, 0, 0) + _dot(cr["kd"], ds_out, 1, 0)
            d_p = jnp.where(cr["causal"], _dot(d_o, vn, 1, 1), 0.0)
            d_qg = _dot(d_o, st, 1, 1)
            d_kd = _dot(vn, ds_out, 1, 1)
            d_gam_last = jnp.sum(jnp.sum(st * ds_out, axis=1, keepdims=True), axis=0, keepdims=True)
            d_w = -_dot(d_vn, st, 1, 1)
            dstate[...] = _dot(cr["qg"], d_o, 0, 0) + ds_out * cr["gam_last"] - _dot(w, d_vn, 0, 0)
            d_bv = _dot(t_mat, d_vn, 0, 0)
            d_kbg = _dot(t_mat, d_w, 0, 0)
            d_a = jnp.where(cr["strict"], -(_dot(d_bv, u, 1, 1) + _dot(d_kbg, w, 1, 1)), 0.0)
            m_a = d_a * dec
            n_p = d_p * dec
            d_kb = _dot(m_a, k, 1, 0) + d_kbg * gam
            d_q = _dot(n_p, k, 1, 0) + d_qg * gam
            d_k = (_dot(m_a, cr["kb"], 0, 0) + _dot(n_p, q, 0, 0) + d_kd * e + d_kb * b_col)
            d_v = d_bv * b_col
            d_beta_col = (jnp.sum(d_bv * v, axis=1, keepdims=True)
                          + jnp.sum(d_kb * k, axis=1, keepdims=True))
            gterm = d_a * cr["a_mat"] + d_p * p
            d_e = jnp.sum(d_kd * k, axis=1, keepdims=True) * e
            d_g_col = (jnp.sum(gterm, axis=1, keepdims=True)
                       + (jnp.sum(d_qg * q, axis=1, keepdims=True)
                          + jnp.sum(d_kbg * cr["kb"], axis=1, keepdims=True)) * gam
                       - d_e)
            d_g_last = jnp.sum(d_e, axis=0, keepdims=True) + d_gam_last * cr["gam_last"]
            lane = lax.broadcasted_iota(jnp.int32, (1, A_CHUNK), 1)
            d_g_row = (_col_to_row(d_g_col, eye) - jnp.sum(gterm, axis=0, keepdims=True)
                       + jnp.where(lane == A_CHUNK - 1, d_g_last, 0.0))
            dgc_ref[pl.ds(i, 1), :] = d_g_row
            dbeta_ref[pl.ds(i, 1), :] = _col_to_row(d_beta_col, eye)

            qh = cin["aq"] * cin["rq"]
            kh = cin["ak"] * cin["rk"]
            d_qh = d_q * (A_DK ** -0.5)
            d_aq = cin["rq"] * (d_qh - qh * jnp.sum(d_qh * qh, axis=1, keepdims=True))
            d_ak = cin["rk"] * (d_k - kh * jnp.sum(d_k * kh, axis=1, keepdims=True))
            d_y = jnp.concatenate([d_aq, d_ak, d_v], axis=1) * _dsilu(cin["y"])
            xw = cin["xw"]
            dyw = jnp.concatenate([d_y, dy_next[...]], axis=0)
            d_x = cw[A_CONV - 1:A_CONV, :] * dyw
            for j in range(1, A_CONV):
                d_x = d_x + cw[A_CONV - 1 - j:A_CONV - j, :] * pltpu.roll(dyw, A_CHUNK + HALO - j, 0)
            d_x = d_x[0:A_CHUNK, :]
            dy_pad = jnp.concatenate([jnp.zeros((HALO, A_CONV_COLS), F32), d_y], axis=0)
            for j in range(A_CONV):
                xs = xw if j == 0 else pltpu.roll(xw, j, 0)
                dcw_ref[A_CONV - 1 - j:A_CONV - j, :] += jnp.sum(dy_pad * xs, axis=0, keepdims=True)
            dy_next[...] = d_y[0:HALO, :]
            dx_ref[rows, 0:A_CONV_COLS] = d_x.astype(BF16)
            dx_ref[rows, A_CONV_COLS:A_HEAD_COLS] = dz.astype(BF16)

        lax.fori_loop(0, n - 1, lambda i, c: (chunk(n - 1 - i, False), c)[1], 0)
        chunk(0, True)

    hn = lambda b, h: (b, h, 0, 0)
    return pl.pallas_call(
        body, name="gdn_bwd", grid=(bn, A_HEADS),
        in_specs=[pl.BlockSpec((None, s, A_HEAD_COLS), lambda b, h: (b, 0, h)),
                  pl.BlockSpec((A_CONV, A_CONV_COLS), lambda b, h: (0, h)),
                  pl.BlockSpec((None, None, n, A_CHUNK), hn),
                  pl.BlockSpec((None, None, n, A_CHUNK), hn),
                  pl.BlockSpec((1, A_DV), lambda b, h: (0, 0)),
                  pl.BlockSpec((None, s, A_DV), lambda b, h: (b, 0, h)),
                  pl.BlockSpec((None, None, n, A_DK, A_DV), lambda b, h: (b, h, 0, 0, 0)),
                  pl.BlockSpec((None, None, n, A_CHUNK, A_CHUNK), lambda b, h: (b, h, 0, 0, 0)),
                  pl.BlockSpec((None, s, A_DV), lambda b, h: (b, 0, h))],
        out_specs=[pl.BlockSpec((None, s, A_HEAD_COLS), lambda b, h: (b, 0, h)),
                   pl.BlockSpec((None, None, n, A_CHUNK), hn),
                   pl.BlockSpec((None, None, n, A_CHUNK), hn),
                   pl.BlockSpec((None, A_CONV, A_CONV_COLS), lambda b, h: (b, 0, h)),
                   pl.BlockSpec((None, None, 8, A_DV), hn)],
        out_shape=[jax.ShapeDtypeStruct((bn, s, A_HEADS * A_HEAD_COLS), BF16),
                   jax.ShapeDtypeStruct((bn, A_HEADS, n, A_CHUNK), F32),
                   jax.ShapeDtypeStruct((bn, A_HEADS, n, A_CHUNK), F32),
                   jax.ShapeDtypeStruct((bn, A_CONV, A_HEADS * A_CONV_COLS), F32),
                   jax.ShapeDtypeStruct((bn, A_HEADS, 8, A_DV), F32)],
        scratch_shapes=[pltpu.VMEM((A_DK, A_DV), F32), pltpu.VMEM((HALO, A_CONV_COLS), F32)],
        compiler_params=_params(("parallel", "parallel"), VMEM_BIG),
    )(proj_hm, cw_hm, beta, gc, norm_g, oraw, states, t_mats, dog)


def _rope_tables(posf, inv_freq_row):
    t = posf.shape[0]
    tm = 512

    def body(p_ref, f_ref, c_ref, sa_ref, sb_ref):
        ang = p_ref[...] * f_ref[...]
        lane = lax.broadcasted_iota(jnp.int32, ang.shape, 1)
        half = ROPE_DIMS // 2
        c_ref[...] = jnp.where(lane < ROPE_DIMS, jnp.cos(ang), 1.0)
        sn = jnp.sin(ang)
        sa_ref[...] = jnp.where(lane < half, -sn, 0.0)
        sb_ref[...] = jnp.where((lane >= half) & (lane < ROPE_DIMS), sn, 0.0)

    row = pl.BlockSpec((tm, 128), lambda i: (i, 0))
    return pl.pallas_call(
        body, name="rope_tables", grid=(t // tm,),
        in_specs=[row, pl.BlockSpec((1, 128), lambda i: (0, 0))], out_specs=[row] * 3,
        out_shape=[jax.ShapeDtypeStruct((t, 128), F32)] * 3,
        compiler_params=_params(("parallel",)),
    )(posf, inv_freq_row)


def _rope(x, c, sa, sb):
    half = ROPE_DIMS // 2
    return x * c + pltpu.roll(x, 128 - half, 1) * sa + pltpu.roll(x, half, 1) * sb


def _rope_t(d, c, sa, sb):
    half = ROPE_DIMS // 2
    return d * c + pltpu.roll(d * sa, half, 1) + pltpu.roll(d * sb, 128 - half, 1)


def _qk_prep(proj, c, sa, sb, qg, kg, name, tm=256):
    t = proj.shape[0]
    wide = proj.shape[1]

    def body(x_ref, c_ref, sa_ref, sb_ref, qg_ref, kg_ref, o_ref):
        cc, s1, s2 = c_ref[...], sa_ref[...], sb_ref[...]
        for which, g_ref in ((0, qg_ref), (1, kg_ref)):
            g = g_ref[...]
            for h in range(B_HEADS):
                lo = which * B_W + h * B_DH
                xv = x_ref[:, lo:lo + B_DH]
                r = lax.rsqrt(jnp.mean(xv * xv, axis=1, keepdims=True) + EPS)
                o_ref[:, lo:lo + B_DH] = _rope(xv * r * g, cc, s1, s2).astype(BF16)
        o_ref[:, 2 * B_W:3 * B_W] = x_ref[:, 2 * B_W:3 * B_W].astype(BF16)

    tab = pl.BlockSpec((tm, 128), lambda i: (i, 0))
    gain = pl.BlockSpec((1, B_DH), lambda i: (0, 0))
    return pl.pallas_call(
        body, name=name, grid=(t // tm,),
        in_specs=[pl.BlockSpec((tm, wide), lambda i: (i, 0)), tab, tab, tab, gain, gain],
        out_specs=pl.BlockSpec((tm, 3 * B_W), lambda i: (i, 0)),
        out_shape=jax.ShapeDtypeStruct((t, 3 * B_W), BF16),
        compiler_params=_params(("parallel",), 40 * 1024 * 1024),
    )(proj, c, sa, sb, qg, kg)


def _qk_prep_bwd(proj, c, sa, sb, qg, kg, dq, dk, dv, dz, name, tm=256):
    t = proj.shape[0]
    wide = proj.shape[1]
    out_w = 3 * B_W + (B_W if dz is not None else 0)

    def body(*refs):
        x_ref, c_ref, sa_ref, sb_ref, qg_ref, kg_ref, dq_ref, dk_ref, dv_ref = refs[:9]
        if dz is not None:
            dz_ref, o_ref, dgain_ref = refs[9:]
        else:
            o_ref, dgain_ref = refs[9:]
        i = pl.program_id(0)

        @pl.when(i == 0)
        def _():
            dgain_ref[...] = jnp.zeros_like(dgain_ref)

        cc, s1, s2 = c_ref[...], sa_ref[...], sb_ref[...]
        for which, g_ref, d_ref in ((0, qg_ref, dq_ref), (1, kg_ref, dk_ref)):
            g = g_ref[...]
            acc = jnp.zeros((1, B_DH), F32)
            for h in range(B_HEADS):
                lo = which * B_W + h * B_DH
                xv = x_ref[:, lo:lo + B_DH]
                r = lax.rsqrt(jnp.mean(xv * xv, axis=1, keepdims=True) + EPS)
                xh = xv * r
                d_xn = _rope_t(d_ref[:, h * B_DH:(h + 1) * B_DH], cc, s1, s2)
                acc = acc + jnp.sum(d_xn * xh, axis=0, keepdims=True)
                d_xh = d_xn * g
                d_x = r * (d_xh - xh * jnp.mean(d_xh * xh, axis=1, keepdims=True))
                o_ref[:, lo:lo + B_DH] = d_x.astype(BF16)
            dgain_ref[which:which + 1, :] += acc
        o_ref[:, 2 * B_W:3 * B_W] = dv_ref[...].astype(BF16)
        if dz is not None:
            o_ref[:, 3 * B_W:4 * B_W] = dz_ref[...]

    tab = pl.BlockSpec((tm, 128), lambda i: (i, 0))
    gain = pl.BlockSpec((1, B_DH), lambda i: (0, 0))
    grad = pl.BlockSpec((tm, B_W), lambda i: (i, 0))
    in_specs = [pl.BlockSpec((tm, wide), lambda i: (i, 0)), tab, tab, tab, gain, gain, grad, grad, grad]
    args = [proj, c, sa, sb, qg, kg, dq, dk, dv]
    if dz is not None:
        in_specs.append(grad)
        args.append(dz)
    return pl.pallas_call(
        body, name=name, grid=(t // tm,), in_specs=in_specs,
        out_specs=[pl.BlockSpec((tm, out_w), lambda i: (i, 0)), pl.BlockSpec((8, B_DH), lambda i: (0, 0))],
        out_shape=[jax.ShapeDtypeStruct((t, out_w), BF16), jax.ShapeDtypeStruct((8, B_DH), F32)],
        compiler_params=_params(("arbitrary",), 40 * 1024 * 1024),
    )(*args)


def _attn_masks():
    qi = lax.broadcasted_iota(jnp.int32, (B_BLK, 2 * B_BLK), 0)
    kj = lax.broadcasted_iota(jnp.int32, (B_BLK, 2 * B_BLK), 1)
    two = (kj >= qi) & (kj <= qi + B_BLK)
    q1 = lax.broadcasted_iota(jnp.int32, (B_BLK, B_BLK), 0)
    k1 = lax.broadcasted_iota(jnp.int32, (B_BLK, B_BLK), 1)
    return k1 <= q1, two


def _lane_pick(ref_rows, h):
    lane = lax.broadcasted_iota(jnp.int32, ref_rows.shape, 1)
    return jnp.sum(jnp.where(lane == h, ref_rows, 0.0), axis=1, keepdims=True)


def _attn_fwd(qkv, name):
    ns, ln, _ = qkv.shape
    nb = ln // B_BLK
    scale = B_DH ** -0.5

    def body(q_ref, k_ref, v_ref, o_ref, lse_ref):
        h = pl.program_id(1)
        mask1, mask2 = _attn_masks()

        @pl.when(h == 0)
        def _():
            lse_ref[...] = jnp.zeros_like(lse_ref)

        def block(i, first):
            rows = pl.ds(pl.multiple_of(i * B_BLK, B_BLK), B_BLK)
            if first:
                win, mask = pl.ds(0, B_BLK), mask1
            else:
                win, mask = pl.ds(pl.multiple_of((i - 1) * B_BLK, B_BLK), 2 * B_BLK), mask2
            sc = jnp.where(mask, _dot(q_ref[rows, :], k_ref[win, :], 1, 1) * scale, -1e30)
            m = jnp.max(sc, axis=1, keepdims=True)
            p = jnp.exp(sc - m)
            l = jnp.sum(p, axis=1, keepdims=True)
            o_ref[rows, :] = _dot(p, v_ref[win, :], 1, 0) / l
            lane = lax.broadcasted_iota(jnp.int32, (B_BLK, B_HEADS), 1)
            lse_ref[rows, :] = jnp.where(lane == h, m + jnp.log(l), lse_ref[rows, :])

        block(0, True)
        if nb > 1:
            lax.fori_loop(1, nb, lambda i, c: (block(i, False), c)[1], 0)

    return pl.pallas_call(
        body, name=name, grid=(ns, B_HEADS),
        in_specs=[pl.BlockSpec((None, ln, B_DH), lambda s, h: (s, 0, h)),
                  pl.BlockSpec((None, ln, B_DH), lambda s, h: (s, 0, B_HEADS + h)),
                  pl.BlockSpec((None, ln, B_DH), lambda s, h: (s, 0, 2 * B_HEADS + h))],
        out_specs=[pl.BlockSpec((None, ln, B_DH), lambda s, h: (s, 0, h)),
                   pl.BlockSpec((None, ln, B_HEADS), lambda s, h: (s, 0, 0))],
        out_shape=[jax.ShapeDtypeStruct((ns, ln, B_W), F32), jax.ShapeDtypeStruct((ns, ln, B_HEADS), F32)],
        compiler_params=_params(("parallel", "arbitrary")),
    )(qkv, qkv, qkv)


def _attn_bwd(qkv, d_o, lse_joint, delta, name):
    ns, ln, _ = qkv.shape
    nb = ln // B_BLK
    scale = B_DH ** -0.5

    def body(q_ref, k_ref, v_ref, do_ref, lj_ref, dl_ref, dq_ref, dk_ref, dv_ref):
        h = pl.program_id(1)
        mask1, mask2 = _attn_masks()
        dk_ref[...] = jnp.zeros_like(dk_ref)
        dv_ref[...] = jnp.zeros_like(dv_ref)

        def block(i, first):
            rows = pl.ds(pl.multiple_of(i * B_BLK, B_BLK), B_BLK)
            if first:
                win, mask = pl.ds(0, B_BLK), mask1
            else:
                win, mask = pl.ds(pl.multiple_of((i - 1) * B_BLK, B_BLK), 2 * B_BLK), mask2
            q = q_ref[rows, :]
            d_out = do_ref[rows, :]
            l_col = _lane_pick(lj_ref[rows, :], h)
            d_col = _lane_pick(dl_ref[rows, :], h)
            sc = _dot(q, k_ref[win, :], 1, 1) * scale
            p = jnp.exp(jnp.where(mask, sc - l_col, -1e30))
            d_p = _dot(d_out, v_ref[win, :], 1, 1)
            d_s = p * (d_p - d_col) * scale
            dq_ref[rows, :] = _dot(d_s, k_ref[win, :], 1, 0)
            dk_ref[win, :] += _dot(d_s, q, 0, 0)
            dv_ref[win, :] += _dot(p, d_out, 0, 0)

        block(0, True)
        if nb > 1:
            lax.fori_loop(1, nb, lambda i, c: (block(i, False), c)[1], 0)

    head = lambda off: pl.BlockSpec((None, ln, B_DH), lambda s, h: (s, 0, off + h))
    small = pl.BlockSpec((None, ln, B_HEADS), lambda s, h: (s, 0, 0))
    return pl.pallas_call(
        body, name=name, grid=(ns, B_HEADS),
        in_specs=[head(0), head(B_HEADS), head(2 * B_HEADS), head(0), small, small],
        out_specs=[head(0)] * 3,
        out_shape=[jax.ShapeDtypeStruct((ns, ln, B_W), F32)] * 3,
        compiler_params=_params(("parallel", "parallel")),
    )(qkv, qkv, qkv, d_o, lse_joint, delta)


def _merge_weights(lse_refs):
    ls = [r[...] for r in lse_refs]
    m = jnp.maximum(jnp.maximum(ls[0], ls[1]), ls[2])
    es = [jnp.exp(l - m) for l in ls]
    tot = es[0] + es[1] + es[2]
    return [e / tot for e in es], m + jnp.log(tot)


def _merge_fwd(outs, lses, proj0, tm=256):
    t = outs[0].shape[0]

    def body(o0, o1, o2, l0, l1, l2, z_ref, og_ref):
        wts, _ = _merge_weights((l0, l1, l2))
        for h in range(B_HEADS):
            cols = slice(h * B_DH, (h + 1) * B_DH)
            o = (wts[0][:, h:h + 1] * o0[:, cols] + wts[1][:, h:h + 1] * o1[:, cols]
                 + wts[2][:, h:h + 1] * o2[:, cols])
            og_ref[:, cols] = (o * _silu(z_ref[:, cols])).astype(BF16)

    wide = pl.BlockSpec((tm, B_W), lambda i: (i, 0))
    small = pl.BlockSpec((tm, B_HEADS), lambda i: (i, 0))
    return pl.pallas_call(
        body, name="merge_fwd", grid=(t // tm,),
        in_specs=[wide] * 3 + [small] * 3 + [pl.BlockSpec((tm, B_W), lambda i: (i, 3))],
        out_specs=wide, out_shape=jax.ShapeDtypeStruct((t, B_W), BF16),
        compiler_params=_params(("parallel",)),
    )(*outs, *lses, proj0)


def _merge_bwd(outs, lses, proj0, d_og, tm=256):
    t = outs[0].shape[0]

    def body(o0, o1, o2, l0, l1, l2, z_ref, dog_ref, do_ref, lj_ref, dl_ref, dz_ref):
        wts, lj = _merge_weights((l0, l1, l2))
        lj_ref[...] = lj
        lane = lax.broadcasted_iota(jnp.int32, (tm, B_HEADS), 1)
        delta = jnp.zeros((tm, B_HEADS), F32)
        for h in range(B_HEADS):
            cols = slice(h * B_DH, (h + 1) * B_DH)
            o = (wts[0][:, h:h + 1] * o0[:, cols] + wts[1][:, h:h + 1] * o1[:, cols]
                 + wts[2][:, h:h + 1] * o2[:, cols])
            z = z_ref[:, cols]
            d_g = dog_ref[:, cols]
            d_out = d_g * _silu(z)
            dz_ref[:, cols] = (d_g * o * _dsilu(z)).astype(BF16)
            do_ref[:, cols] = d_out.astype(BF16)
            delta = jnp.where(lane == h, jnp.sum(d_out * o, axis=1, keepdims=True), delta)
        dl_ref[...] = delta

    wide = pl.BlockSpec((tm, B_W), lambda i: (i, 0))
    small = pl.BlockSpec((tm, B_HEADS), lambda i: (i, 0))
    return pl.pallas_call(
        body, name="merge_bwd", grid=(t // tm,),
        in_specs=[wide] * 3 + [small] * 3 + [pl.BlockSpec((tm, B_W), lambda i: (i, 3)), wide],
        out_specs=[wide, small, small, wide],
        out_shape=[jax.ShapeDtypeStruct((t, B_W), BF16), jax.ShapeDtypeStruct((t, B_HEADS), F32),
                   jax.ShapeDtypeStruct((t, B_HEADS), F32), jax.ShapeDtypeStruct((t, B_W), BF16)],
        compiler_params=_params(("parallel",)),
    )(*outs, *lses, proj0, d_og)


def _adamw(w, g, m, v, name):
    r, c = w.shape
    tr = r
    for cand in (256, 128, 64, 32, 16, 8):
        if r % cand == 0:
            tr = cand
            break

    def body(w_ref, g_ref, m_ref, v_ref, d_ref, nm_ref, nv_ref):
        gv = g_ref[...]
        nm = ADAM_B1 * m_ref[...] + (1.0 - ADAM_B1) * gv
        nv = ADAM_B2 * v_ref[...] + (1.0 - ADAM_B2) * (gv * gv)
        m_hat = nm / (1.0 - ADAM_B1 ** ADAM_STEP)
        v_hat = nv / (1.0 - ADAM_B2 ** ADAM_STEP)
        d_ref[...] = -ADAM_LR * (m_hat / (jnp.sqrt(v_hat) + ADAM_EPS) + ADAM_WD * w_ref[...])
        nm_ref[...] = nm
        nv_ref[...] = nv

    blk = pl.BlockSpec((tr, c), lambda i: (i, 0))
    return pl.pallas_call(
        body, name=name, grid=(r // tr,), in_specs=[blk] * 4, out_specs=[blk] * 3,
        out_shape=[jax.ShapeDtypeStruct((r, c), F32)] * 3,
        compiler_params=_params(("parallel",)),
    )(w, g, m, v)


def _pair_sum(own, other, half_index, name, tr=256):
    _, r, c = own.shape
    rh = r // 2
    tr = min(tr, rh)
    nrb = rh // tr

    def body(c_ref, own_ref, oth_ref, out_ref):
        out_ref[...] = (own_ref[...] + oth_ref[...].astype(F32)).astype(BF16)

    return pl.pallas_call(
        body, name=name,
        grid_spec=pltpu.PrefetchScalarGridSpec(
            num_scalar_prefetch=1, grid=(N_CHIPS, nrb),
            in_specs=[pl.BlockSpec((None, tr, c), lambda k, i, cc: (k, cc[0] * nrb + i, 0)),
                      pl.BlockSpec((None, tr, c), lambda k, i, cc: (k, i, 0))],
            out_specs=pl.BlockSpec((None, tr, c), lambda k, i, cc: (k, i, 0))),
        out_shape=jax.ShapeDtypeStruct((N_CHIPS, rh, c), BF16),
        compiler_params=_params(("parallel", "parallel")),
    )(half_index, own, other)


def _chip_sum(parts, name, tr=256):
    _, r, c = parts.shape
    tr = min(tr, r)

    def body(p_ref, out_ref):
        acc = p_ref[0].astype(F32)
        for k in range(1, N_CHIPS):
            acc = acc + p_ref[k].astype(F32)
        out_ref[...] = acc

    return pl.pallas_call(
        body, name=name, grid=(r // tr,),
        in_specs=[pl.BlockSpec((N_CHIPS, tr, c), lambda i: (0, i, 0))],
        out_specs=pl.BlockSpec((tr, c), lambda i: (i, 0)),
        out_shape=jax.ShapeDtypeStruct((r, c), F32),
        compiler_params=_params(("parallel",)),
    )(parts)


HBM = pl.BlockSpec(memory_space=pltpu.HBM)


def _place():
    x, y, c = lax.axis_index("x"), lax.axis_index("y"), lax.axis_index("c")
    chips = [(1 - x, y), (x, 1 - y), (1 - x, 1 - y)]
    return x, y, c, chips


def _weight_allgather(shards, conv_shard):
    na = len(shards)

    def body(*refs):
        ins = refs[:na]
        conv_in = refs[na]
        outs = refs[na + 1:2 * na + 1]
        conv_out = refs[2 * na + 1]
        send, recv, fsend, frecv, lsem, csend, crecv = refs[2 * na + 2:]
        x, y, c, chips = _place()
        me = 2 * x + y
        sib = (x, y, 1 - c)
        local, first, conv_cp = [], [], []
        for i in range(na):
            rh = ins[i].shape[0] // 2
            mine = pl.ds(c * rh, rh)
            cp = pltpu.make_async_copy(ins[i], outs[i].at[me], lsem.at[i])
            cp.start()
            local.append(cp)
            for j, (px, py) in enumerate(chips):
                cp = pltpu.make_async_remote_copy(
                    src_ref=ins[i].at[mine], dst_ref=outs[i].at[me, mine],
                    send_sem=send.at[3 * i + j], recv_sem=recv.at[3 * i + j],
                    device_id=(px, py, c), device_id_type=MESH)
                cp.start()
                first.append(cp)
        cp = pltpu.make_async_copy(conv_in, conv_out.at[me], lsem.at[na])
        cp.start()
        local.append(cp)
        for j, (px, py) in enumerate(chips):
            cp = pltpu.make_async_remote_copy(
                src_ref=conv_in, dst_ref=conv_out.at[me], send_sem=csend.at[j], recv_sem=crecv.at[j],
                device_id=(px, py, c), device_id_type=MESH)
            cp.start()
            conv_cp.append(cp)
        passed = []
        for i in range(na):
            rh = ins[i].shape[0] // 2
            mine = pl.ds(c * rh, rh)
            for j, (px, py) in enumerate(chips):
                slot = outs[i].at[2 * px + py, mine]
                pltpu.make_async_remote_copy(
                    src_ref=slot, dst_ref=slot, send_sem=send.at[3 * i + j], recv_sem=recv.at[3 * i + j],
                    device_id=(px, py, c), device_id_type=MESH).wait_recv()
                cp = pltpu.make_async_remote_copy(
                    src_ref=slot, dst_ref=slot, send_sem=fsend.at[3 * i + j], recv_sem=frecv.at[3 * i + j],
                    device_id=sib, device_id_type=MESH)
                cp.start()
                passed.append(cp)
        for i in range(na):
            rh = ins[i].shape[0] // 2
            theirs = pl.ds((1 - c) * rh, rh)
            for j, (px, py) in enumerate(chips):
                slot = outs[i].at[2 * px + py, theirs]
                pltpu.make_async_remote_copy(
                    src_ref=slot, dst_ref=slot, send_sem=fsend.at[3 * i + j], recv_sem=frecv.at[3 * i + j],
                    device_id=sib, device_id_type=MESH).wait_recv()
        for j, (px, py) in enumerate(chips):
            slot = conv_out.at[2 * px + py]
            pltpu.make_async_remote_copy(
                src_ref=slot, dst_ref=slot, send_sem=csend.at[j], recv_sem=crecv.at[j],
                device_id=(px, py, c), device_id_type=MESH).wait_recv()
        for cp in first + passed + conv_cp:
            cp.wait_send()
        for cp in local:
            cp.wait()

    out_shape = [jax.ShapeDtypeStruct((N_CHIPS,) + s.shape, s.dtype) for s in shards]
    out_shape.append(jax.ShapeDtypeStruct((N_CHIPS,) + conv_shard.shape, conv_shard.dtype))
    res = pl.pallas_call(
        body, name="weight_allgather",
        in_specs=[HBM] * (na + 1), out_specs=[HBM] * (na + 1), out_shape=out_shape,
        scratch_shapes=[pltpu.SemaphoreType.DMA((3 * na,)), pltpu.SemaphoreType.DMA((3 * na,)),
                        pltpu.SemaphoreType.DMA((3 * na,)), pltpu.SemaphoreType.DMA((3 * na,)),
                        pltpu.SemaphoreType.DMA((na + 1,)), pltpu.SemaphoreType.DMA((3,)),
                        pltpu.SemaphoreType.DMA((3,))],
    )(*shards, conv_shard)
    return res[:na], res[na]


def _sibling_swap_halves(grads):
    na = len(grads)

    def body(*refs):
        ins, outs = refs[:na], refs[na:2 * na]
        send, recv = refs[2 * na:]
        x, y, c, _ = _place()
        sib = (x, y, 1 - c)
        cps = []
        for i in range(na):
            rh = ins[i].shape[1] // 2
            cp = pltpu.make_async_remote_copy(
                src_ref=ins[i].at[:, pl.ds((1 - c) * rh, rh), :], dst_ref=outs[i],
                send_sem=send.at[i], recv_sem=recv.at[i], device_id=sib, device_id_type=MESH)
            cp.start()
            cps.append(cp)
        for cp in cps:
            cp.wait()

    out_shape = [jax.ShapeDtypeStruct((g.shape[0], g.shape[1] // 2, g.shape[2]), g.dtype) for g in grads]
    return pl.pallas_call(
        body, name="grad_sibling_swap", in_specs=[HBM] * na, out_specs=[HBM] * na, out_shape=out_shape,
        scratch_shapes=[pltpu.SemaphoreType.DMA((na,)), pltpu.SemaphoreType.DMA((na,))],
    )(*grads)


def _chip_exchange(sums):
    na = len(sums)

    def body(*refs):
        ins, outs = refs[:na], refs[na:2 * na]
        send, recv, lsem = refs[2 * na:]
        x, y, c, chips = _place()
        me = 2 * x + y
        cps, local = [], []
        for i in range(na):
            cp = pltpu.make_async_copy(ins[i].at[me], outs[i].at[me], lsem.at[i])
            cp.start()
            local.append(cp)
            for j, (px, py) in enumerate(chips):
                cp = pltpu.make_async_remote_copy(
                    src_ref=ins[i].at[2 * px + py], dst_ref=outs[i].at[me],
                    send_sem=send.at[3 * i + j], recv_sem=recv.at[3 * i + j],
                    device_id=(px, py, c), device_id_type=MESH)
                cp.start()
                cps.append(cp)
        for i in range(na):
            for j, (px, py) in enumerate(chips):
                slot = outs[i].at[2 * px + py]
                pltpu.make_async_remote_copy(
                    src_ref=slot, dst_ref=slot, send_sem=send.at[3 * i + j], recv_sem=recv.at[3 * i + j],
                    device_id=(px, py, c), device_id_type=MESH).wait_recv()
        for cp in cps:
            cp.wait_send()
        for cp in local:
            cp.wait()

    out_shape = [jax.ShapeDtypeStruct(s.shape, s.dtype) for s in sums]
    return pl.pallas_call(
        body, name="grad_chip_exchange", in_specs=[HBM] * na, out_specs=[HBM] * na, out_shape=out_shape,
        scratch_shapes=[pltpu.SemaphoreType.DMA((3 * na,)), pltpu.SemaphoreType.DMA((3 * na,)),
                        pltpu.SemaphoreType.DMA((na,))],
    )(*sums)


def _sibling_join_halves(halves):
    na = len(halves)

    def body(*refs):
        ins, outs = refs[:na], refs[na:2 * na]
        send, recv, lsem = refs[2 * na:]
        x, y, c, _ = _place()
        sib = (x, y, 1 - c)
        cps, local = [], []
        for i in range(na):
            rh = ins[i].shape[0]
            mine = pl.ds(c * rh, rh)
            cp = pltpu.make_async_copy(ins[i], outs[i].at[mine], lsem.at[i])
            cp.start()
            local.append(cp)
            cp = pltpu.make_async_remote_copy(
                src_ref=ins[i], dst_ref=outs[i].at[mine], send_sem=send.at[i], recv_sem=recv.at[i],
                device_id=sib, device_id_type=MESH)
            cp.start()
            cps.append(cp)
        for i in range(na):
            rh = ins[i].shape[0]
            theirs = outs[i].at[pl.ds((1 - c) * rh, rh)]
            pltpu.make_async_remote_copy(
                src_ref=theirs, dst_ref=theirs, send_sem=send.at[i], recv_sem=recv.at[i],
                device_id=sib, device_id_type=MESH).wait_recv()
        for cp in cps:
            cp.wait_send()
        for cp in local:
            cp.wait()

    out_shape = [jax.ShapeDtypeStruct((2 * h.shape[0], h.shape[1]), h.dtype) for h in halves]
    return pl.pallas_call(
        body, name="grad_sibling_join", in_specs=[HBM] * na, out_specs=[HBM] * na, out_shape=out_shape,
        scratch_shapes=[pltpu.SemaphoreType.DMA((na,)), pltpu.SemaphoreType.DMA((na,)),
                        pltpu.SemaphoreType.DMA((na,))],
    )(*halves)


def _small_allreduce(vec):
    r, cdim = vec.shape
    n_dev = 8

    def body(v_ref, out_ref, buf, send, recv):
        x, y, c, _ = _place()
        me = 4 * x + 2 * y + c
        buf[me] = v_ref[...]
        cps = []
        for k in range(1, n_dev):
            dx, dy, dc = (k >> 2) & 1, (k >> 1) & 1, k & 1
            peer = (x ^ dx, y ^ dy, c ^ dc)
            cp = pltpu.make_async_remote_copy(
                src_ref=v_ref, dst_ref=buf.at[me], send_sem=send.at[k - 1], recv_sem=recv.at[k - 1],
                device_id=peer, device_id_type=MESH)
            cp.start()
            cps.append(cp)
        for k in range(1, n_dev):
            dx, dy, dc = (k >> 2) & 1, (k >> 1) & 1, k & 1
            src = 4 * (x ^ dx) + 2 * (y ^ dy) + (c ^ dc)
            slot = buf.at[src]
            pltpu.make_async_remote_copy(
                src_ref=slot, dst_ref=slot, send_sem=send.at[k - 1], recv_sem=recv.at[k - 1],
                device_id=(x ^ dx, y ^ dy, c ^ dc), device_id_type=MESH).wait_recv()
        for cp in cps:
            cp.wait_send()
        acc = buf[0]
        for k in range(1, n_dev):
            acc = acc + buf[k]
        out_ref[...] = acc

    vm = pl.BlockSpec(memory_space=pltpu.VMEM)
    return pl.pallas_call(
        body, name="small_allreduce", in_specs=[vm], out_specs=vm,
        out_shape=jax.ShapeDtypeStruct((r, cdim), F32),
        scratch_shapes=[pltpu.VMEM((n_dev, r, cdim), F32), pltpu.SemaphoreType.DMA((n_dev - 1,)),
                        pltpu.SemaphoreType.DMA((n_dev - 1,))],
    )(vec)


def _a_cols_to_head_major(w):
    lead = w.shape[:-1]
    q = w[..., :A_QK].reshape(lead + (A_HEADS, A_DK))
    k = w[..., A_QK:2 * A_QK].reshape(lead + (A_HEADS, A_DK))
    v = w[..., 2 * A_QK:2 * A_QK + A_VW].reshape(lead + (A_HEADS, A_DV))
    z = w[..., 2 * A_QK + A_VW:].reshape(lead + (A_HEADS, A_DV))
    return jnp.concatenate([q, k, v, z], axis=-1).reshape(lead + (A_HEADS * A_HEAD_COLS,))


def _a_cols_from_head_major(w):
    lead = w.shape[:-1]
    w = w.reshape(lead + (A_HEADS, A_HEAD_COLS))
    parts = [w[..., :A_DK], w[..., A_DK:2 * A_DK], w[..., 2 * A_DK:2 * A_DK + A_DV], w[..., 2 * A_DK + A_DV:]]
    return jnp.concatenate([p.reshape(lead + (-1,)) for p in parts], axis=-1)


def _conv_cols_to_head_major(w):
    lead = w.shape[:-1]
    q = w[..., :A_QK].reshape(lead + (A_HEADS, A_DK))
    k = w[..., A_QK:2 * A_QK].reshape(lead + (A_HEADS, A_DK))
    v = w[..., 2 * A_QK:].reshape(lead + (A_HEADS, A_DV))
    return jnp.concatenate([q, k, v], axis=-1).reshape(lead + (A_HEADS * A_CONV_COLS,))


def _conv_cols_from_head_major(w):
    lead = w.shape[:-1]
    w = w.reshape(lead + (A_HEADS, A_CONV_COLS))
    parts = [w[..., :A_DK], w[..., A_DK:2 * A_DK], w[..., 2 * A_DK:]]
    return jnp.concatenate([p.reshape(lead + (-1,)) for p in parts], axis=-1)


def _to_stream(a, bn, d):
    rest = a.shape[1:]
    s = a.shape[0] // bn
    a = a.reshape((bn, s // d, d) + rest)
    a = jnp.swapaxes(a, 1, 2)
    return a.reshape((bn * d, s // d) + rest)


def _from_stream(a, bn, d):
    rest = a.shape[2:]
    ln = a.shape[1]
    a = a.reshape((bn, d, ln) + rest)
    a = jnp.swapaxes(a, 1, 2)
    return a.reshape((bn * ln * d,) + rest)


def _b_group_cols(w, gi):
    n_qkv = 3 * B_GROUPS * B_W
    qkv = w[..., :n_qkv].reshape(w.shape[:-1] + (3, B_GROUPS, B_W))
    return qkv[..., :, gi, :].reshape(w.shape[:-1] + (3 * B_W,))


def _shard_major(g, ncols):
    r = g.shape[0]
    return jnp.swapaxes(g.reshape(r, N_CHIPS, ncols), 0, 1)


def _pack_rows(items):
    rows, offs = [], []
    at = 0
    for a in items:
        flat = a.reshape(-1).astype(F32)
        nr = -(-flat.shape[0] // 128)
        flat = jnp.pad(flat, (0, nr * 128 - flat.shape[0]))
        rows.append(flat.reshape(nr, 128))
        offs.append((at, nr, a.shape))
        at += nr
    pad = -at % 8
    if pad:
        rows.append(jnp.zeros((pad, 128), F32))
    return jnp.concatenate(rows, axis=0), offs


def _unpack_rows(packed, offs):
    out = []
    for at, nr, shape in offs:
        size = int(np.prod(shape)) if len(shape) else 1
        out.append(packed[at:at + nr].reshape(-1)[:size].reshape(shape))
    return out


def _local_step(x, positions, loss_target, norm_g, wa_in, conv_w, a_log, a_dt_bias, a_norm_g, wa_out,
                wb_in, b_q_norm_g, b_k_norm_g, wb_out):
    bn, s, d = x.shape
    t = bn * s
    n_chunks = s // A_CHUNK
    wa_main = _a_cols_to_head_major(wa_in[:, :A_MAIN])
    wa_tail = jnp.pad(wa_in[:, A_MAIN:], ((0, 0), (0, 128 - 2 * A_HEADS)))
    cw_hm = _conv_cols_to_head_major(conv_w)
    wb_groups = [_b_group_cols(wb_in, gi) for gi in range(B_GROUPS)]
    wb_groups[0] = jnp.concatenate([wb_groups[0], wb_in[:, 3 * B_GROUPS * B_W:]], axis=1)

    x0 = x.reshape(t, d)
    h0 = _rms_fwd(x0, norm_g[0:1], "rms0_fwd")
    proj_a = _matmul(h0, wa_main, "nn", F32, "a_in_main")
    tail_a = _matmul(h0, wa_tail, "nn", F32, "a_in_tail")
    tail_t = jnp.swapaxes(tail_a[:, :2 * A_HEADS].reshape(bn, s, 2 * A_HEADS), 1, 2)
    tail_t = tail_t.reshape(bn, 2 * A_HEADS, n_chunks, A_CHUNK)
    beta, gc = _gdn_prep(tail_t, a_log[0], a_dt_bias[0])
    proj_a3 = proj_a.reshape(bn, s, A_MAIN)
    og_a, oraw_a, states, t_mats = _gdn_fwd(proj_a3, cw_hm, beta, gc, a_norm_g)
    x1 = _matmul(og_a.reshape(t, A_VW), wa_out, "nn", F32, "a_out", res=x0, tk=2048)

    h1 = _rms_fwd(x1, norm_g[1:2], "rms1_fwd")
    inv_freq = ROPE_THETA ** (-jnp.arange(0, ROPE_DIMS, 2, dtype=F32) / ROPE_DIMS)
    freq_row = jnp.concatenate([inv_freq, inv_freq, jnp.zeros((128 - ROPE_DIMS,), F32)]).reshape(1, 128)
    posf = jnp.broadcast_to(positions.astype(F32).reshape(t, 1), (t, 128))
    tabs = _rope_tables(posf, freq_row)
    h1_s, tabs_s, proj_b, qkv_b, o_b, lse_b = [], [], [], [], [], []
    for gi, dil in enumerate(B_DIL):
        hs = h1 if dil == 1 else _to_stream(h1, bn, dil).reshape(t, d)
        ts = tabs if dil == 1 else [_to_stream(tb, bn, dil).reshape(t, 128) for tb in tabs]
        pj = _matmul(hs, wb_groups[gi], "nn", F32, f"b_in_g{gi}")
        qkv = _qk_prep(pj, *ts, b_q_norm_g[0, gi:gi + 1], b_k_norm_g[0, gi:gi + 1], f"qk_prep_g{gi}")
        o_s, lse_s = _attn_fwd(qkv.reshape(bn * dil, s // dil, 3 * B_W), f"attn_fwd_g{gi}")
        h1_s.append(hs), tabs_s.append(ts), proj_b.append(pj), qkv_b.append(qkv)
        o_b.append(o_s.reshape(t, B_W) if dil == 1 else _from_stream(o_s, bn, dil))
        lse_b.append(lse_s.reshape(t, B_HEADS) if dil == 1 else _from_stream(lse_s, bn, dil))
    og_b = _merge_fwd(o_b, lse_b, proj_b[0])
    x2 = _matmul(og_b, wb_out, "nn", F32, "b_out", res=x1)

    d_x2, loss_parts = _loss_grad(x2, loss_target.reshape(t, d))
    loss_local = jnp.sum(loss_parts)

    d_x2b = d_x2.astype(BF16)
    g_wb_out = _matmul(og_b, d_x2b, "tn", F32, "b_out_dw", tk=512)
    d_og_b = _matmul(d_x2b, wb_out, "nt", F32, "b_out_dx")
    d_o, lse_joint, delta, d_z = _merge_bwd(o_b, lse_b, proj_b[0], d_og_b)
    d_h1, g_wb_cols, g_qn, g_kn = [], [], [], []
    for gi, dil in enumerate(B_DIL):
        if dil == 1:
            do_s, lj_s, dl_s = d_o, lse_joint, delta
        else:
            do_s, lj_s, dl_s = (_to_stream(a, bn, dil).reshape(t, -1) for a in (d_o, lse_joint, delta))
        ns, ln = bn * dil, s // dil
        dq, dk, dv = _attn_bwd(qkv_b[gi].reshape(ns, ln, 3 * B_W), do_s.reshape(ns, ln, B_W),
                               lj_s.reshape(ns, ln, B_HEADS), dl_s.reshape(ns, ln, B_HEADS), f"attn_bwd_g{gi}")
        d_pj, d_gain = _qk_prep_bwd(proj_b[gi], *tabs_s[gi], b_q_norm_g[0, gi:gi + 1], b_k_norm_g[0, gi:gi + 1],
                                    dq.reshape(t, B_W), dk.reshape(t, B_W), dv.reshape(t, B_W),
                                    d_z if gi == 0 else None, f"qk_prep_bwd_g{gi}")
        g_wb_cols.append(_matmul(h1_s[gi], d_pj, "tn", F32, f"b_in_dw_g{gi}", tk=512))
        dh = _matmul(d_pj, wb_groups[gi], "nt", F32, f"b_in_dx_g{gi}")
        d_h1.append(dh if dil == 1 else _from_stream(dh.reshape(ns, ln, d), bn, dil))
        g_qn.append(d_gain[0]), g_kn.append(d_gain[1])
    d_x1, g_norm1 = _rms_bwd(x1, norm_g[1:2], d_h1, d_x2, "rms1_bwd")
    pieces = [g_wb_cols[gi][:, w * B_W:(w + 1) * B_W] for w in range(3) for gi in range(B_GROUPS)]
    g_wb_in = jnp.concatenate(pieces + [g_wb_cols[0][:, 3 * B_W:]], axis=1)

    d_x1b = d_x1.astype(BF16)
    g_wa_out = _matmul(og_a.reshape(t, A_VW), d_x1b, "tn", F32, "a_out_dw", tk=512)
    d_og_a = _matmul(d_x1b, wa_out, "nt", F32, "a_out_dx")
    d_pa, d_gc, d_beta, d_cw, d_ng = _gdn_bwd(proj_a3, cw_hm, beta, gc, a_norm_g, oraw_a, states, t_mats,
                                              d_og_a.reshape(bn, s, A_VW))
    d_tail_t, d_alog, d_dtb = _gdn_prep_bwd(tail_t, a_log[0], a_dt_bias[0], d_gc, d_beta)
    d_tail = jnp.swapaxes(d_tail_t.reshape(bn, 2 * A_HEADS, s), 1, 2).reshape(t, 2 * A_HEADS)
    d_tail = jnp.pad(d_tail, ((0, 0), (0, 128 - 2 * A_HEADS))).astype(BF16)
    d_pa = d_pa.reshape(t, A_MAIN)
    g_wa_main = _matmul(h0, d_pa, "tn", F32, "a_in_dw_main", tk=512)
    g_wa_tail = _matmul(h0, d_tail, "tn", F32, "a_in_dw_tail", tk=512)
    d_h0 = _matmul(d_pa, wa_main, "nt", F32, "a_in_dx_main")
    d_h0t = _matmul(d_tail, wa_tail, "nt", F32, "a_in_dx_tail")
    d_x0, g_norm0 = _rms_bwd(x0, norm_g[0:1], [d_h0, d_h0t], d_x1, "rms0_bwd")
    g_wa_in = jnp.concatenate([_a_cols_from_head_major(g_wa_main), g_wa_tail[:, :2 * A_HEADS]], axis=1)

    gfull = {
        "norm_g": jnp.concatenate([g_norm0, g_norm1], axis=0), "a_w_in": g_wa_in,
        "a_conv_w": _conv_cols_from_head_major(jnp.sum(d_cw, axis=0)),
        "a_log": jnp.sum(d_alog[:, :, 0], axis=0), "a_dt_bias": jnp.sum(d_dtb[:, :, 0], axis=0),
        "a_norm_g": jnp.sum(d_ng[:, :, 0, :], axis=(0, 1)), "a_w_out": g_wa_out, "b_w_in": g_wb_in,
        "b_q_norm_g": jnp.stack(g_qn), "b_k_norm_g": jnp.stack(g_kn), "b_w_out": g_wb_out}
    return loss_local, d_x0.reshape(bn, s, d), gfull


def kernel(x, positions, norm_g, a_w_in, a_conv_w, a_log, a_dt_bias, a_norm_g, a_w_out, b_w_in, b_q_norm_g, b_k_norm_g, b_w_out, loss_target, m_norm_g, m_a_w_in, m_a_conv_w, m_a_log, m_a_dt_bias, m_a_norm_g, m_a_w_out, m_b_w_in, m_b_q_norm_g, m_b_k_norm_g, m_b_w_out, v_norm_g, v_a_w_in, v_a_conv_w, v_a_log, v_a_dt_bias, v_a_norm_g, v_a_w_out, v_b_w_in, v_b_q_norm_g, v_b_k_norm_g, v_b_w_out):
    d = x.shape[2]
    my_c = lax.axis_index("c")
    my_chip = 2 * lax.axis_index("x") + lax.axis_index("y")

    shards = [a_w_in[0].astype(BF16), a_w_out[0].astype(BF16), b_w_in[0].astype(BF16), b_w_out[0].astype(BF16)]
    (ga_in, ga_out, gb_in, gb_out), g_conv = _weight_allgather(shards, a_conv_w[0])
    wa_in = jnp.concatenate([ga_in[k] for k in range(N_CHIPS)], axis=1)
    wa_out = ga_out.reshape(A_VW, d)
    wb_in = jnp.concatenate([gb_in[k] for k in range(N_CHIPS)], axis=1)
    wb_out = gb_out.reshape(B_W, d)
    conv_w = jnp.concatenate([g_conv[k] for k in range(N_CHIPS)], axis=1)

    loss_local, d_x0, gfull = _local_step(x, positions, loss_target, norm_g, wa_in, conv_w, a_log, a_dt_bias,
                                          a_norm_g, wa_out, wb_in, b_q_norm_g, b_k_norm_g, wb_out)

    g_full = [_shard_major(gfull["a_w_in"], a_w_in.shape[2]), gfull["a_w_out"].reshape(N_CHIPS, -1, d),
              _shard_major(gfull["b_w_in"], b_w_in.shape[2]), gfull["b_w_out"].reshape(N_CHIPS, -1, d)]
    recv_sib = _sibling_swap_halves([g.astype(BF16) for g in g_full])
    half_index = jnp.reshape(my_c, (1,)).astype(jnp.int32)
    chip_sums = [_pair_sum(g, r, half_index, f"grad_pair_sum_{i}") for i, (g, r) in enumerate(zip(g_full, recv_sib))]
    by_source = _chip_exchange(chip_sums)
    halves = [_chip_sum(p, f"grad_chip_sum_{i}") for i, p in enumerate(by_source)]
    g_a_w_in, g_a_w_out, g_b_w_in, g_b_w_out = _sibling_join_halves(halves)

    small = [gfull["norm_g"], gfull["a_conv_w"], gfull["a_log"], gfull["a_dt_bias"], gfull["a_norm_g"],
             gfull["b_q_norm_g"], gfull["b_k_norm_g"], loss_local]
    packed, offs = _pack_rows(small)
    red = _unpack_rows(_small_allreduce(packed), offs)
    g_norm, g_conv_all, g_alog, g_dtb, g_ang, g_q, g_k, loss = red
    g_conv_mine = lax.dynamic_slice_in_dim(g_conv_all, my_chip * a_conv_w.shape[2], a_conv_w.shape[2], axis=1)

    grads = {
        "norm_g": g_norm, "a_w_in": g_a_w_in[None], "a_conv_w": g_conv_mine[None], "a_log": g_alog[None],
        "a_dt_bias": g_dtb[None], "a_norm_g": g_ang[None], "a_w_out": g_a_w_out[None], "b_w_in": g_b_w_in[None],
        "b_q_norm_g": g_q[None], "b_k_norm_g": g_k[None], "b_w_out": g_b_w_out[None]}
    weights = {"norm_g": norm_g, "a_w_in": a_w_in, "a_conv_w": a_conv_w, "a_log": a_log, "a_dt_bias": a_dt_bias,
               "a_norm_g": a_norm_g, "a_w_out": a_w_out, "b_w_in": b_w_in, "b_q_norm_g": b_q_norm_g,
               "b_k_norm_g": b_k_norm_g, "b_w_out": b_w_out}
    m_in = {"norm_g": m_norm_g, "a_w_in": m_a_w_in, "a_conv_w": m_a_conv_w, "a_log": m_a_log,
            "a_dt_bias": m_a_dt_bias, "a_norm_g": m_a_norm_g, "a_w_out": m_a_w_out, "b_w_in": m_b_w_in,
            "b_q_norm_g": m_b_q_norm_g, "b_k_norm_g": m_b_k_norm_g, "b_w_out": m_b_w_out}
    v_in = {"norm_g": v_norm_g, "a_w_in": v_a_w_in, "a_conv_w": v_a_conv_w, "a_log": v_a_log,
            "a_dt_bias": v_a_dt_bias, "a_norm_g": v_a_norm_g, "a_w_out": v_a_w_out, "b_w_in": v_b_w_in,
            "b_q_norm_g": v_b_q_norm_g, "b_k_norm_g": v_b_k_norm_g, "b_w_out": v_b_w_out}
    names = list(weights)

    big = ("a_w_in", "a_w_out", "b_w_in", "b_w_out")
    delta_w, new_m, new_v = {}, {}, {}
    for nm in big:
        shp = weights[nm].shape
        two = lambda a: a.reshape(shp[-2], shp[-1])
        dl, m2, v2 = _adamw(two(weights[nm]), two(grads[nm]), two(m_in[nm]), two(v_in[nm]), f"adamw_{nm}")
        delta_w[nm], new_m[nm], new_v[nm] = dl.reshape(shp), m2.reshape(shp), v2.reshape(shp)
    small_names = [nm for nm in names if nm not in big]
    packs = [_pack_rows([src[nm] for nm in small_names]) for src in (weights, grads, m_in, v_in)]
    offs = packs[0][1]
    dl, m2, v2 = _adamw(packs[0][0], packs[1][0], packs[2][0], packs[3][0], "adamw_small")
    for nm, a, b, c2 in zip(small_names, _unpack_rows(dl, offs), _unpack_rows(m2, offs), _unpack_rows(v2, offs)):
        delta_w[nm], new_m[nm], new_v[nm] = a, b, c2

    return (loss, d_x0, *[grads[nm] for nm in names], *[delta_w[nm] for nm in names],
            *[new_m[nm] for nm in names], *[new_v[nm] for nm in names])
```

```python
import functools
import math

import jax
import jax.numpy as jnp
import numpy as np
from jax import lax
from jax.experimental import pallas as pl
from jax.experimental.pallas import tpu as pltpu

F32 = jnp.float32
BF16 = jnp.bfloat16
MESH = pl.DeviceIdType.MESH

EPS = 1e-6
D_MODEL = 1024
A_HEADS = 8
A_DK = 128
A_DV = 256
A_QK = A_HEADS * A_DK
A_VW = A_HEADS * A_DV
A_MAIN = 2 * A_QK + 2 * A_VW
A_HEAD_COLS = 2 * A_DK + 2 * A_DV
A_CONV_COLS = 2 * A_DK + A_DV
A_CHUNK = 64
A_CONV = 4
B_GROUPS = 3
B_HEADS = 8
B_DH = 128
B_W = B_HEADS * B_DH
B_DIL = (1, 4, 16)
B_BLK = 128
ROPE_THETA = 500000.0
ROPE_DIMS = B_DH // 4
ADAM_LR, ADAM_B1, ADAM_B2, ADAM_EPS, ADAM_WD, ADAM_STEP = 0.001, 0.9, 0.999, 1e-08, 0.01, 10
N_CHIPS = 4
VMEM_BIG = 56 * 1024 * 1024


def _params(sem=None, vmem=None):
    return pltpu.CompilerParams(dimension_semantics=sem, vmem_limit_bytes=vmem)


def _dot(a, b, ca, cb):
    return lax.dot_general(a.astype(BF16), b.astype(BF16), (((ca,), (cb,)), ((), ())),
                           preferred_element_type=F32)


def _split3(a):
    hi = a.astype(BF16)
    r = a - hi.astype(F32)
    mid = r.astype(BF16)
    lo = (r - mid.astype(F32)).astype(BF16)
    return hi, mid, lo


def _dot_hi(a, b, ca, cb):
    a_hi, a_lo, _ = _split3(a)
    b_hi, b_lo, _ = _split3(b)
    dn = (((ca,), (cb,)), ((), ()))
    out = lax.dot_general(a_hi, b_hi, dn, preferred_element_type=F32)
    out = out + lax.dot_general(a_hi, b_lo, dn, preferred_element_type=F32)
    return out + lax.dot_general(a_lo, b_hi, dn, preferred_element_type=F32)


def _sigmoid(y):
    return 1.0 / (1.0 + jnp.exp(-y))


def _silu(y):
    return y * _sigmoid(y)


def _dsilu(y):
    s = _sigmoid(y)
    return s * (1.0 + y * (1.0 - s))


def _matmul(a, b, mode, out_dtype, name, res=None, tm=1024, tn=1024, tk=1024):
    if mode == "nn":
        (m, k), (_, n) = a.shape, b.shape
    else:
        (m, k), (n, _) = a.shape, b.shape
    tm, tn, tk = min(tm, m), min(tn, n), min(tk, k)
    assert m % tm == 0 and n % tn == 0 and k % tk == 0, (name, a.shape, b.shape)
    nk = k // tk
    dims = {"nn": ((1,), (0,)), "nt": ((1,), (1,))}[mode]

    def body(*refs):
        a_ref, b_ref = refs[0], refs[1]
        r_ref = refs[2] if res is not None else None
        o_ref = refs[3] if res is not None else refs[2]
        prod = lax.dot_general(a_ref[...], b_ref[...], (dims, ((), ())), preferred_element_type=F32)

        def finish(r):
            if res is not None:
                r = r + r_ref[...]
            o_ref[...] = r.astype(out_dtype)

        if nk == 1:
            finish(prod)
            return
        acc = refs[-1]
        kk = pl.program_id(2)

        @pl.when(kk == 0)
        def _():
            acc[...] = prod

        @pl.when((kk > 0) & (kk < nk - 1))
        def _():
            acc[...] += prod

        @pl.when(kk == nk - 1)
        def _():
            finish(acc[...] + prod)

    a_spec = pl.BlockSpec((tm, tk), lambda i, j, kk: (i, kk))
    if mode == "nt":
        b_spec = pl.BlockSpec((tn, tk), lambda i, j, kk: (j, kk))
    else:
        b_spec = pl.BlockSpec((tk, tn), lambda i, j, kk: (kk, j))
    in_specs = [a_spec, b_spec]
    args = [a, b]
    if res is not None:
        in_specs.append(pl.BlockSpec((tm, tn), lambda i, j, kk: (i, j)))
        args.append(res)
    return pl.pallas_call(
        body, name=name, grid=(m // tm, n // tn, nk),
        in_specs=in_specs, out_specs=pl.BlockSpec((tm, tn), lambda i, j, kk: (i, j)),
        out_shape=jax.ShapeDtypeStruct((m, n), out_dtype),
        scratch_shapes=[pltpu.VMEM((tm, tn), F32)] if nk > 1 else [],
        compiler_params=_params(("parallel", "parallel", "arbitrary"), 48 * 1024 * 1024),
    )(*args)


def _rms_fwd(x, g, name, tm=256):
    t, d = x.shape

    def body(x_ref, g_ref, h_ref):
        xv = x_ref[...]
        r = lax.rsqrt(jnp.mean(xv * xv, axis=-1, keepdims=True) + EPS)
        h_ref[...] = (xv * r * g_ref[...]).astype(BF16)

    return pl.pallas_call(
        body, name=name, grid=(t // tm,),
        in_specs=[pl.BlockSpec((tm, d), lambda i: (i, 0)), pl.BlockSpec((1, d), lambda i: (0, 0))],
        out_specs=pl.BlockSpec((tm, d), lambda i: (i, 0)),
        out_shape=jax.ShapeDtypeStruct((t, d), BF16),
        compiler_params=_params(("parallel",)),
    )(x, g)


def _rms_bwd(x, g, dhs, dres, name, tm=256):
    t, d = x.shape
    n_dh = len(dhs)

    def body(*refs):
        x_ref, g_ref = refs[0], refs[1]
        dh_refs = refs[2:2 + n_dh]
        dres_ref, dx_ref, dg_ref = refs[2 + n_dh:]
        i = pl.program_id(0)

        @pl.when(i == 0)
        def _():
            dg_ref[...] = jnp.zeros_like(dg_ref)

        xv = x_ref[...]
        r = lax.rsqrt(jnp.mean(xv * xv, axis=-1, keepdims=True) + EPS)
        xh = xv * r
        dh = dh_refs[0][...]
        for ref in dh_refs[1:]:
            dh = dh + ref[...]
        dg_ref[0:1, :] += jnp.sum(dh * xh, axis=0, keepdims=True)
        dxh = dh * g_ref[...]
        dx = r * (dxh - xh * jnp.mean(dxh * xh, axis=-1, keepdims=True))
        dx_ref[...] = dx + dres_ref[...]

    row = pl.BlockSpec((tm, d), lambda i: (i, 0))
    dx, dg = pl.pallas_call(
        body, name=name, grid=(t // tm,),
        in_specs=[row, pl.BlockSpec((1, d), lambda i: (0, 0))] + [row] * n_dh + [row],
        out_specs=[row, pl.BlockSpec((8, d), lambda i: (0, 0))],
        out_shape=[jax.ShapeDtypeStruct((t, d), F32), jax.ShapeDtypeStruct((8, d), F32)],
        compiler_params=_params(("arbitrary",)),
    )(x, g, *dhs, dres)
    return dx, dg[0:1]


def _loss_grad(y, target, name="loss_grad", tm=256):
    t, d = y.shape
    nb = t // tm

    def body(y_ref, t_ref, dy_ref, part_ref):
        e = y_ref[...] - t_ref[...]
        dy_ref[...] = e * (1.0 / d)
        s = jnp.sum(jnp.sum(e * e, axis=1, keepdims=True), axis=0, keepdims=True) * (0.5 / d)
        part_ref[...] = jnp.broadcast_to(s, (8, 128))

    row = pl.BlockSpec((tm, d), lambda i: (i, 0))
    dy, part = pl.pallas_call(
        body, name=name, grid=(nb,), in_specs=[row, row],
        out_specs=[row, pl.BlockSpec((None, 8, 128), lambda i: (i, 0, 0))],
        out_shape=[jax.ShapeDtypeStruct((t, d), F32), jax.ShapeDtypeStruct((nb, 8, 128), F32)],
        compiler_params=_params(("parallel",)),
    )(y, target)
    return dy, part[:, 0, 0]


def _softplus(x):
    t = jnp.exp(-jnp.abs(x))
    return jnp.maximum(x, 0.0) + jnp.where(t < 1e-3, t * (1.0 - 0.5 * t), jnp.log(1.0 + t))


def _tri(rows_le_cols):
    r = lax.broadcasted_iota(jnp.int32, (A_CHUNK, A_CHUNK), 0)
    c = lax.broadcasted_iota(jnp.int32, (A_CHUNK, A_CHUNK), 1)
    return jnp.where((r <= c) if rows_le_cols else (r >= c), 1.0, 0.0).astype(BF16)


def _dot_exact_rhs(a, ones_bf16):
    dn = (((1,), (0,)), ((), ()))
    hi, mid, lo = _split3(a)
    out = lax.dot_general(hi, ones_bf16, dn, preferred_element_type=F32)
    out = out + lax.dot_general(mid, ones_bf16, dn, preferred_element_type=F32)
    return out + lax.dot_general(lo, ones_bf16, dn, preferred_element_type=F32)


def _gdn_prep(tail_t, a_log, dt_bias):
    bn, _, n, c = tail_t.shape

    def body(t_ref, alog_ref, dtb_ref, beta_ref, gc_ref):
        upper = _tri(True)
        for h in range(A_HEADS):
            beta_ref[h] = _sigmoid(t_ref[h])
            ea = jnp.exp(jnp.full((n, c), alog_ref[h], F32))
            g = -ea * _softplus(t_ref[A_HEADS + h] + dtb_ref[h])
            gc_ref[h] = _dot_exact_rhs(g, upper)

    smem = pl.BlockSpec(memory_space=pltpu.SMEM)
    blk = pl.BlockSpec((None, A_HEADS, n, c), lambda b: (b, 0, 0, 0))
    return pl.pallas_call(
        body, name="gdn_prep", grid=(bn,),
        in_specs=[pl.BlockSpec((None, 2 * A_HEADS, n, c), lambda b: (b, 0, 0, 0)), smem, smem],
        out_specs=[blk, blk],
        out_shape=[jax.ShapeDtypeStruct((bn, A_HEADS, n, c), F32)] * 2,
        compiler_params=_params(("parallel",)),
    )(tail_t, a_log, dt_bias)


def _gdn_prep_bwd(tail_t, a_log, dt_bias, d_gc, d_beta):
    bn, _, n, c = tail_t.shape

    def body(t_ref, alog_ref, dtb_ref, dgc_ref, dbeta_ref, dt_ref, dal_ref, ddt_ref):
        lower = _tri(False)
        for h in range(A_HEADS):
            beta = _sigmoid(t_ref[h])
            dt_ref[h] = dbeta_ref[h] * beta * (1.0 - beta)
            dg = _dot_exact_rhs(dgc_ref[h], lower)
            ea = jnp.exp(jnp.full((n, c), alog_ref[h], F32))
            xa = t_ref[A_HEADS + h] + dtb_ref[h]
            g = -ea * _softplus(xa)
            dxa = -ea * dg * _sigmoid(xa)
            dt_ref[A_HEADS + h] = dxa
            s1 = jnp.sum(jnp.sum(g * dg, axis=1, keepdims=True), axis=0, keepdims=True)
            s2 = jnp.sum(jnp.sum(dxa, axis=1, keepdims=True), axis=0, keepdims=True)
            dal_ref[h:h + 1, :] = jnp.broadcast_to(s1, (1, 128))
            ddt_ref[h:h + 1, :] = jnp.broadcast_to(s2, (1, 128))

    smem = pl.BlockSpec(memory_space=pltpu.SMEM)
    blk8 = pl.BlockSpec((None, A_HEADS, n, c), lambda b: (b, 0, 0, 0))
    blk16 = pl.BlockSpec((None, 2 * A_HEADS, n, c), lambda b: (b, 0, 0, 0))
    sm = pl.BlockSpec((None, A_HEADS, 128), lambda b: (b, 0, 0))
    return pl.pallas_call(
        body, name="gdn_prep_bwd", grid=(bn,),
        in_specs=[blk16, smem, smem, blk8, blk8],
        out_specs=[blk16, sm, sm],
        out_shape=[jax.ShapeDtypeStruct((bn, 2 * A_HEADS, n, c), F32),
                   jax.ShapeDtypeStruct((bn, A_HEADS, 128), F32),
                   jax.ShapeDtypeStruct((bn, A_HEADS, 128), F32)],
        compiler_params=_params(("parallel",)),
    )(tail_t, a_log, dt_bias, d_gc, d_beta)


HALO = 8


def _conv_window(x_ref, n, first, lo, width):
    if first:
        return jnp.concatenate([jnp.zeros((HALO, width), F32), x_ref[0:A_CHUNK, lo:lo + width]], axis=0)
    start = pl.multiple_of(n * A_CHUNK - HALO, HALO)
    return x_ref[pl.ds(start, A_CHUNK + HALO), lo:lo + width]


def _conv_taps(xw, w):
    y = w[A_CONV - 1:A_CONV, :] * xw
    for j in range(1, A_CONV):
        y = y + w[A_CONV - 1 - j:A_CONV - j, :] * pltpu.roll(xw, j, 0)
    return y[HALO:, :]


def _row_to_col(row, eye):
    c = eye.shape[0]
    return jnp.sum(jnp.where(eye, jnp.broadcast_to(row, (c, c)), 0.0), axis=1, keepdims=True)


def _col_to_row(col, eye):
    c = eye.shape[0]
    return jnp.sum(jnp.where(eye, jnp.broadcast_to(col, (c, c)), 0.0), axis=0, keepdims=True)


def _unit_lower_inverse(a, ri, ci):
    eye = jnp.where(ri == ci, 1.0, 0.0)
    a8 = jnp.where((ri >> 3) == (ci >> 3), a, 0.0)
    a2 = _dot_hi(a8, a8, 1, 0)
    a4 = _dot_hi(a2, a2, 1, 0)
    t = eye - a8
    t = t + _dot_hi(t, a2, 1, 0)
    t = t + _dot_hi(t, a4, 1, 0)
    for sh in (3, 4, 5):
        off = jnp.where(((ri >> (sh + 1)) == (ci >> (sh + 1))) & ((ri >> sh) != (ci >> sh)), a, 0.0)
        t = t - _dot_hi(_dot_hi(t, off, 1, 0), t, 1, 0)
    return t


def _gdn_chunk_inputs(x_ref, cw, n, first):
    xw = _conv_window(x_ref, n, first, 0, A_CONV_COLS)
    y = _conv_taps(xw, cw)
    a = _silu(y)
    aq, ak, v = a[:, 0:A_DK], a[:, A_DK:2 * A_DK], a[:, 2 * A_DK:]
    rq = lax.rsqrt(jnp.sum(aq * aq, axis=1, keepdims=True) + EPS)
    rk = lax.rsqrt(jnp.sum(ak * ak, axis=1, keepdims=True) + EPS)
    return dict(xw=xw, y=y, aq=aq, ak=ak, rq=rq, rk=rk,
                q=aq * rq * (A_DK ** -0.5), k=ak * rk, v=v)


def _gdn_chunk_core(q, k, v, g_row, b_row, t_mat, ri, ci):
    eye = ri == ci
    g_col = _row_to_col(g_row, eye)
    b_col = _row_to_col(b_row, eye)
    causal = ri >= ci
    strict = ri > ci
    dec = jnp.where(causal, jnp.exp(jnp.where(causal, g_col - g_row, 0.0)), 0.0)
    gam = jnp.exp(g_col)
    g_last = g_row[:, A_CHUNK - 1:A_CHUNK]
    gam_last = jnp.exp(g_last)
    e = jnp.exp(g_last - g_col)
    kb = k * b_col
    bv = v * b_col
    kbg = kb * gam
    kk = _dot(kb, k, 1, 1)
    a_mat = jnp.where(strict, kk * dec, 0.0)
    if t_mat is None:
        t_mat = _unit_lower_inverse(a_mat, ri, ci)
    u = _dot(t_mat, bv, 1, 0)
    w = _dot(t_mat, kbg, 1, 0)
    p = _dot(q, k, 1, 1) * dec
    return dict(eye=eye, g_col=g_col, b_col=b_col, dec=dec, strict=strict, causal=causal, gam=gam,
                gam_last=gam_last, e=e, kb=kb, bv=bv, kbg=kbg, a_mat=a_mat, t_mat=t_mat, u=u, w=w, p=p,
                qg=q * gam, kd=k * e)


def _gdn_fwd(proj_hm, cw_hm, beta, gc, norm_g):
    bn, s, _ = proj_hm.shape
    n = s // A_CHUNK

    def body(x_ref, cw_ref, beta_ref, gc_ref, ng_ref, og_ref, oraw_ref, st_ref, t_ref, state):
        ri = lax.broadcasted_iota(jnp.int32, (A_CHUNK, A_CHUNK), 0)
        ci = lax.broadcasted_iota(jnp.int32, (A_CHUNK, A_CHUNK), 1)
        cw = cw_ref[...]
        ng = ng_ref[...]
        state[...] = jnp.zeros_like(state)

        def chunk(i, first):
            rows = pl.ds(0 if first else pl.multiple_of(i * A_CHUNK, A_CHUNK), A_CHUNK)
            cin = _gdn_chunk_inputs(x_ref, cw, i, first)
            core = _gdn_chunk_core(cin["q"], cin["k"], cin["v"], gc_ref[pl.ds(i, 1), :],
                                   beta_ref[pl.ds(i, 1), :], None, ri, ci)
            st = state[...]
            st_ref[i] = st
            t_ref[i] = core["t_mat"]
            vn = core["u"] - _dot(core["w"], st, 1, 0)
            o = _dot(core["qg"], st, 1, 0) + _dot(core["p"], vn, 1, 0)
            state[...] = st * core["gam_last"] + _dot(core["kd"], vn, 0, 0)
            oraw_ref[rows, :] = o
            r = lax.rsqrt(jnp.mean(o * o, axis=1, keepdims=True) + EPS)
            z = x_ref[rows, A_CONV_COLS:A_HEAD_COLS]
            og_ref[rows, :] = (o * r * ng * _silu(z)).astype(BF16)

        chunk(0, True)
        lax.fori_loop(1, n, lambda i, c: (chunk(i, False), c)[1], 0)

    return pl.pallas_call(
        body, name="gdn_fwd", grid=(bn, A_HEADS),
        in_specs=[pl.BlockSpec((None, s, A_HEAD_COLS), lambda b, h: (b, 0, h)),
                  pl.BlockSpec((A_CONV, A_CONV_COLS), lambda b, h: (0, h)),
                  pl.BlockSpec((None, None, n, A_CHUNK), lambda b, h: (b, h, 0, 0)),
                  pl.BlockSpec((None, None, n, A_CHUNK), lambda b, h: (b, h, 0, 0)),
                  pl.BlockSpec((1, A_DV), lambda b, h: (0, 0))],
        out_specs=[pl.BlockSpec((None, s, A_DV), lambda b, h: (b, 0, h)),
                   pl.BlockSpec((None, s, A_DV), lambda b, h: (b, 0, h)),
                   pl.BlockSpec((None, None, n, A_DK, A_DV), lambda b, h: (b, h, 0, 0, 0)),
                   pl.BlockSpec((None, None, n, A_CHUNK, A_CHUNK), lambda b, h: (b, h, 0, 0, 0))],
        out_shape=[jax.ShapeDtypeStruct((bn, s, A_VW), BF16),
                   jax.ShapeDtypeStruct((bn, s, A_VW), F32),
                   jax.ShapeDtypeStruct((bn, A_HEADS, n, A_DK, A_DV), F32),
                   jax.ShapeDtypeStruct((bn, A_HEADS, n, A_CHUNK, A_CHUNK), F32)],
        scratch_shapes=[pltpu.VMEM((A_DK, A_DV), F32)],
        compiler_params=_params(("parallel", "parallel"), VMEM_BIG),
    )(proj_hm, cw_hm, beta, gc, norm_g)


def _gdn_bwd(proj_hm, cw_hm, beta, gc, norm_g, oraw, states, t_mats, dog):
    bn, s, _ = proj_hm.shape
    n = s // A_CHUNK

    def body(x_ref, cw_ref, beta_ref, gc_ref, ng_ref, oraw_ref, st_ref, t_ref, dog_ref,
             dx_ref, dgc_ref, dbeta_ref, dcw_ref, dng_ref, dstate, dy_next):
        ri = lax.broadcasted_iota(jnp.int32, (A_CHUNK, A_CHUNK), 0)
        ci = lax.broadcasted_iota(jnp.int32, (A_CHUNK, A_CHUNK), 1)
        cw = cw_ref[...]
        ng = ng_ref[...]
        dstate[...] = jnp.zeros_like(dstate)
        dy_next[...] = jnp.zeros_like(dy_next)
        dcw_ref[...] = jnp.zeros_like(dcw_ref)
        dng_ref[...] = jnp.zeros_like(dng_ref)

        def chunk(i, first):
            rows = pl.ds(0 if first else pl.multiple_of(i * A_CHUNK, A_CHUNK), A_CHUNK)
            cin = _gdn_chunk_inputs(x_ref, cw, i, first)
            q, k, v = cin["q"], cin["k"], cin["v"]
            g_row = gc_ref[pl.ds(i, 1), :]
            b_row = beta_ref[pl.ds(i, 1), :]
            cr = _gdn_chunk_core(q, k, v, g_row, b_row, t_ref[i], ri, ci)
            eye, dec, gam, e = cr["eye"], cr["dec"], cr["gam"], cr["e"]
            b_col, t_mat, u, w, p = cr["b_col"], cr["t_mat"], cr["u"], cr["w"], cr["p"]
            st = st_ref[i]
            ds_out = dstate[...]

            o = oraw_ref[rows, :]
            z = x_ref[rows, A_CONV_COLS:A_HEAD_COLS]
            d_og = dog_ref[rows, :]
            r = lax.rsqrt(jnp.mean(o * o, axis=1, keepdims=True) + EPS)
            oh = o * r
            d_on = d_og * _silu(z)
            dz = d_og * oh * ng * _dsilu(z)
            dng_ref[0:1, :] += jnp.sum(d_on * oh, axis=0, keepdims=True)
            d_oh = d_on * ng
            d_o = r * (d_oh - oh * jnp.mean(d_oh * oh, axis=1, keepdims=True))

            vn = u - _dot(w, st, 1, 0)
            d_vn = _dot(p, d_o, 0, 0) + _dot(cr["kd"], ds_out, 1, 0)
            d_p = jnp.where(cr["causal"], _dot(d_o, vn, 1, 1), 0.0)
            d_qg = _dot(d_o, st, 1, 1)
            d_kd = _dot(vn, ds_out, 1, 1)
            d_gam_last = jnp.sum(jnp.sum(st * ds_out, axis=1, keepdims=True), axis=0, keepdims=True)
            d_w = -_dot(d_vn, st, 1, 1)
            dstate[...] = _dot(cr["qg"], d_o, 0, 0) + ds_out * cr["gam_last"] - _dot(w, d_vn, 0, 0)
            d_bv = _dot(t_mat, d_vn, 0, 0)
            d_kbg = _dot(t_mat, d_w, 0, 0)
            d_a = jnp.where(cr["strict"], -(_dot(d_bv, u, 1, 1) + _dot(d_kbg, w, 1, 1)), 0.0)
            m_a = d_a * dec
            n_p = d_p * dec
            d_kb = _dot(m_a, k, 1, 0) + d_kbg * gam
            d_q = _dot(n_p, k, 1, 0) + d_qg * gam
            d_k = (_dot(m_a, cr["kb"], 0, 0) + _dot(n_p, q, 0, 0) + d_kd * e + d_kb * b_col)
            d_v = d_bv * b_col
            d_beta_col = (jnp.sum(d_bv * v, axis=1, keepdims=True)
                          + jnp.sum(d_kb * k, axis=1, keepdims=True))
            gterm = d_a * cr["a_mat"] + d_p * p
            d_e = jnp.sum(d_kd * k, axis=1, keepdims=True) * e
            d_g_col = (jnp.sum(gterm, axis=1, keepdims=True)
                       + (jnp.sum(d_qg * q, axis=1, keepdims=True)
                          + jnp.sum(d_kbg * cr["kb"], axis=1, keepdims=True)) * gam
                       - d_e)
            d_g_last = jnp.sum(d_e, axis=0, keepdims=True) + d_gam_last * cr["gam_last"]
            lane = lax.broadcasted_iota(jnp.int32, (1, A_CHUNK), 1)
            d_g_row = (_col_to_row(d_g_col, eye) - jnp.sum(gterm, axis=0, keepdims=True)
                       + jnp.where(lane == A_CHUNK - 1, d_g_last, 0.0))
            dgc_ref[pl.ds(i, 1), :] = d_g_row
            dbeta_ref[pl.ds(i, 1), :] = _col_to_row(d_beta_col, eye)

            qh = cin["aq"] * cin["rq"]
            kh = cin["ak"] * cin["rk"]
            d_qh = d_q * (A_DK ** -0.5)
            d_aq = cin["rq"] * (d_qh - qh * jnp.sum(d_qh * qh, axis=1, keepdims=True))
            d_ak = cin["rk"] * (d_k - kh * jnp.sum(d_k * kh, axis=1, keepdims=True))
            d_y = jnp.concatenate([d_aq, d_ak, d_v], axis=1) * _dsilu(cin["y"])
            xw = cin["xw"]
            dyw = jnp.concatenate([d_y, dy_next[...]], axis=0)
            d_x = cw[A_CONV - 1:A_CONV, :] * dyw
            for j in range(1, A_CONV):
                d_x = d_x + cw[A_CONV - 1 - j:A_CONV - j, :] * pltpu.roll(dyw, A_CHUNK + HALO - j, 0)
            d_x = d_x[0:A_CHUNK, :]
            dy_pad = jnp.concatenate([jnp.zeros((HALO, A_CONV_COLS), F32), d_y], axis=0)
            for j in range(A_CONV):
                xs = xw if j == 0 else pltpu.roll(xw, j, 0)
                dcw_ref[A_CONV - 1 - j:A_CONV - j, :] += jnp.sum(dy_pad * xs, axis=0, keepdims=True)
            dy_next[...] = d_y[0:HALO, :]
            dx_ref[rows, 0:A_CONV_COLS] = d_x.astype(BF16)
            dx_ref[rows, A_CONV_COLS:A_HEAD_COLS] = dz.astype(BF16)

        lax.fori_loop(0, n - 1, lambda i, c: (chunk(n - 1 - i, False), c)[1], 0)
        chunk(0, True)

    hn = lambda b, h: (b, h, 0, 0)
    return pl.pallas_call(
        body, name="gdn_bwd", grid=(bn, A_HEADS),
        in_specs=[pl.BlockSpec((None, s, A_HEAD_COLS), lambda b, h: (b, 0, h)),
                  pl.BlockSpec((A_CONV, A_CONV_COLS), lambda b, h: (0, h)),
                  pl.BlockSpec((None, None, n, A_CHUNK), hn),
                  pl.BlockSpec((None, None, n, A_CHUNK), hn),
                  pl.BlockSpec((1, A_DV), lambda b, h: (0, 0)),
                  pl.BlockSpec((None, s, A_DV), lambda b, h: (b, 0, h)),
                  pl.BlockSpec((None, None, n, A_DK, A_DV), lambda b, h: (b, h, 0, 0, 0)),
                  pl.BlockSpec((None, None, n, A_CHUNK, A_CHUNK), lambda b, h: (b, h, 0, 0, 0)),
                  pl.BlockSpec((None, s, A_DV), lambda b, h: (b, 0, h))],
        out_specs=[pl.BlockSpec((None, s, A_HEAD_COLS), lambda b, h: (b, 0, h)),
                   pl.BlockSpec((None, None, n, A_CHUNK), hn),
                   pl.BlockSpec((None, None, n, A_CHUNK), hn),
                   pl.BlockSpec((None, A_CONV, A_CONV_COLS), lambda b, h: (b, 0, h)),
                   pl.BlockSpec((None, None, 8, A_DV), hn)],
        out_shape=[jax.ShapeDtypeStruct((bn, s, A_HEADS * A_HEAD_COLS), BF16),
                   jax.ShapeDtypeStruct((bn, A_HEADS, n, A_CHUNK), F32),
                   jax.ShapeDtypeStruct((bn, A_HEADS, n, A_CHUNK), F32),
                   jax.ShapeDtypeStruct((bn, A_CONV, A_HEADS * A_CONV_COLS), F32),
                   jax.ShapeDtypeStruct((bn, A_HEADS, 8, A_DV), F32)],
        scratch_shapes=[pltpu.VMEM((A_DK, A_DV), F32), pltpu.VMEM((HALO, A_CONV_COLS), F32)],
        compiler_params=_params(("parallel", "parallel"), VMEM_BIG),
    )(proj_hm, cw_hm, beta, gc, norm_g, oraw, states, t_mats, dog)


A_SEQ_BLK = 512
A_BLK_CHUNKS = A_SEQ_BLK // A_CHUNK


def _gdn_halo(proj_hm):
    bn, s, w = proj_hm.shape
    last = proj_hm.reshape(bn, s // A_SEQ_BLK, A_SEQ_BLK, w)[:, :, A_SEQ_BLK - HALO:, :]
    return jnp.concatenate([jnp.zeros((bn, 1, HALO, w), proj_hm.dtype), last[:, :-1]], axis=1)


def _gdn_window(x_ref, halo_ref, ci, first, lo):
    if first:
        return jnp.concatenate([halo_ref[:, lo:lo + A_CONV_COLS], x_ref[0:A_CHUNK, lo:lo + A_CONV_COLS]], axis=0)
    start = pl.multiple_of(ci * A_CHUNK - HALO, HALO)
    return x_ref[pl.ds(start, A_CHUNK + HALO), lo:lo + A_CONV_COLS]


def _gdn_chunk_prep(xw, cw):
    y = _conv_taps(xw, cw)
    a = _silu(y)
    aq, ak, v = a[:, 0:A_DK], a[:, A_DK:2 * A_DK], a[:, 2 * A_DK:]
    rq = lax.rsqrt(jnp.sum(aq * aq, axis=1, keepdims=True) + EPS)
    rk = lax.rsqrt(jnp.sum(ak * ak, axis=1, keepdims=True) + EPS)
    return dict(xw=xw, y=y, aq=aq, ak=ak, rq=rq, rk=rk, q=aq * rq * (A_DK ** -0.5), k=ak * rk, v=v)


def _gdn_fwd(proj_hm, cw_hm, beta, gc, norm_g, hp=4):
    bn, s, _ = proj_hm.shape
    n = s // A_CHUNK
    nsb = s // A_SEQ_BLK
    halo = _gdn_halo(proj_hm)

    def body(x_ref, halo_ref, cw_ref, beta_ref, gc_ref, ng_ref, og_ref, oraw_ref, st_ref, t_ref, state):
        ri = lax.broadcasted_iota(jnp.int32, (A_CHUNK, A_CHUNK), 0)
        ci_ = lax.broadcasted_iota(jnp.int32, (A_CHUNK, A_CHUNK), 1)
        ng = ng_ref[...]

        @pl.when(pl.program_id(2) == 0)
        def _():
            state[...] = jnp.zeros_like(state)

        def chunk(ci, first):
            rows = pl.ds(0 if first else pl.multiple_of(ci * A_CHUNK, A_CHUNK), A_CHUNK)
            for hh in range(hp):
                lo = hh * A_HEAD_COLS
                cw = cw_ref[:, hh * A_CONV_COLS:(hh + 1) * A_CONV_COLS]
                cin = _gdn_chunk_prep(_gdn_window(x_ref, halo_ref, ci, first, lo), cw)
                core = _gdn_chunk_core(cin["q"], cin["k"], cin["v"], gc_ref[hh, pl.ds(ci, 1), :],
                                       beta_ref[hh, pl.ds(ci, 1), :], None, ri, ci_)
                st = state[hh]
                st_ref[hh, ci] = st
                t_ref[hh, ci] = core["t_mat"]
                vn = core["u"] - _dot(core["w"], st, 1, 0)
                o = _dot(core["qg"], st, 1, 0) + _dot(core["p"], vn, 1, 0)
                state[hh] = st * core["gam_last"] + _dot(core["kd"], vn, 0, 0)
                ocols = slice(hh * A_DV, (hh + 1) * A_DV)
                oraw_ref[rows, ocols] = o
                r = lax.rsqrt(jnp.mean(o * o, axis=1, keepdims=True) + EPS)
                z = x_ref[rows, lo + A_CONV_COLS:lo + A_HEAD_COLS]
                og_ref[rows, ocols] = (o * r * ng * _silu(z)).astype(BF16)

        chunk(0, True)
        lax.fori_loop(1, A_BLK_CHUNKS, lambda i, c: (chunk(i, False), c)[1], 0)

    small = pl.BlockSpec((None, hp, A_BLK_CHUNKS, A_CHUNK), lambda b, h, j: (b, h, j, 0))
    return pl.pallas_call(
        body, name="gdn_fwd", grid=(bn, A_HEADS // hp, nsb),
        in_specs=[pl.BlockSpec((None, A_SEQ_BLK, hp * A_HEAD_COLS), lambda b, h, j: (b, j, h)),
                  pl.BlockSpec((None, None, HALO, hp * A_HEAD_COLS), lambda b, h, j: (b, j, 0, h)),
                  pl.BlockSpec((A_CONV, hp * A_CONV_COLS), lambda b, h, j: (0, h)),
                  small, small,
                  pl.BlockSpec((1, A_DV), lambda b, h, j: (0, 0))],
        out_specs=[pl.BlockSpec((None, A_SEQ_BLK, hp * A_DV), lambda b, h, j: (b, j, h)),
                   pl.BlockSpec((None, A_SEQ_BLK, hp * A_DV), lambda b, h, j: (b, j, h)),
                   pl.BlockSpec((None, hp, A_BLK_CHUNKS, A_DK, A_DV), lambda b, h, j: (b, h, j, 0, 0)),
                   pl.BlockSpec((None, hp, A_BLK_CHUNKS, A_CHUNK, A_CHUNK), lambda b, h, j: (b, h, j, 0, 0))],
        out_shape=[jax.ShapeDtypeStruct((bn, s, A_VW), BF16),
                   jax.ShapeDtypeStruct((bn, s, A_VW), F32),
                   jax.ShapeDtypeStruct((bn, A_HEADS, n, A_DK, A_DV), F32),
                   jax.ShapeDtypeStruct((bn, A_HEADS, n, A_CHUNK, A_CHUNK), F32)],
        scratch_shapes=[pltpu.VMEM((hp, A_DK, A_DV), F32)],
        compiler_params=_params(("parallel", "parallel", "arbitrary"), VMEM_BIG),
    )(proj_hm, halo, cw_hm, beta, gc, norm_g)


def _gdn_bwd(proj_hm, cw_hm, beta, gc, norm_g, oraw, states, t_mats, dog, hp=2):
    bn, s, _ = proj_hm.shape
    n = s // A_CHUNK
    nsb = s // A_SEQ_BLK
    halo = _gdn_halo(proj_hm)

    def body(x_ref, halo_ref, cw_ref, beta_ref, gc_ref, ng_ref, oraw_ref, st_ref, t_ref, dog_ref,
             dx_ref, dgc_ref, dbeta_ref, dcw_ref, dng_ref, dstate, dy_next):
        ri = lax.broadcasted_iota(jnp.int32, (A_CHUNK, A_CHUNK), 0)
        ci_ = lax.broadcasted_iota(jnp.int32, (A_CHUNK, A_CHUNK), 1)
        lane = lax.broadcasted_iota(jnp.int32, (1, A_CHUNK), 1)
        ng = ng_ref[...]

        @pl.when(pl.program_id(2) == 0)
        def _():
            dstate[...] = jnp.zeros_like(dstate)
            dy_next[...] = jnp.zeros_like(dy_next)
            dcw_ref[...] = jnp.zeros_like(dcw_ref)
            dng_ref[...] = jnp.zeros_like(dng_ref)

        def one_head(hh, ci, first, rows):
            lo = hh * A_HEAD_COLS
            ccols = slice(hh * A_CONV_COLS, (hh + 1) * A_CONV_COLS)
            ocols = slice(hh * A_DV, (hh + 1) * A_DV)
            cw = cw_ref[:, ccols]
            cin = _gdn_chunk_prep(_gdn_window(x_ref, halo_ref, ci, first, lo), cw)
            q, k, v = cin["q"], cin["k"], cin["v"]
            cr = _gdn_chunk_core(q, k, v, gc_ref[hh, pl.ds(ci, 1), :], beta_ref[hh, pl.ds(ci, 1), :],
                                 t_ref[hh, ci], ri, ci_)
            eye, dec, gam, e = cr["eye"], cr["dec"], cr["gam"], cr["e"]
            b_col, t_mat, u, w, p = cr["b_col"], cr["t_mat"], cr["u"], cr["w"], cr["p"]
            st = st_ref[hh, ci]
            ds_out = dstate[hh]

            o = oraw_ref[rows, ocols]
            z = x_ref[rows, lo + A_CONV_COLS:lo + A_HEAD_COLS]
            d_og = dog_ref[rows, ocols]
            r = lax.rsqrt(jnp.mean(o * o, axis=1, keepdims=True) + EPS)
            oh = o * r
            d_on = d_og * _silu(z)
            dz = d_og * oh * ng * _dsilu(z)
            dng_ref[hh, 0:1, :] += jnp.sum(d_on * oh, axis=0, keepdims=True)
            d_oh = d_on * ng
            d_o = r * (d_oh - oh * jnp.mean(d_oh * oh, axis=1, keepdims=True))

            vn = u - _dot(w, st, 1, 0)
            d_vn = _dot(p, d_o, 0, 0) + _dot(cr["kd"], ds_out, 1, 0)
            d_p = jnp.where(cr["causal"], _dot(d_o, vn, 1, 1), 0.0)
            d_qg = _dot(d_o, st, 1, 1)
            d_kd = _dot(vn, ds_out, 1, 1)
            d_gam_last = jnp.sum(jnp.sum(st * ds_out, axis=1, keepdims=True), axis=0, keepdims=True)
            d_w = -_dot(d_vn, st, 1, 1)
            dstate[hh] = _dot(cr["qg"], d_o, 0, 0) + ds_out * cr["gam_last"] - _dot(w, d_vn, 0, 0)
            d_bv = _dot(t_mat, d_vn, 0, 0)
            d_kbg = _dot(t_mat, d_w, 0, 0)
            d_a = jnp.where(cr["strict"], -(_dot(d_bv, u, 1, 1) + _dot(d_kbg, w, 1, 1)), 0.0)
            m_a = d_a * dec
            n_p = d_p * dec
            d_kb = _dot(m_a, k, 1, 0) + d_kbg * gam
            d_q = _dot(n_p, k, 1, 0) + d_qg * gam
            d_k = (_dot(m_a, cr["kb"], 0, 0) + _dot(n_p, q, 0, 0) + d_kd * e + d_kb * b_col)
            d_v = d_bv * b_col
            d_beta_col = (jnp.sum(d_bv * v, axis=1, keepdims=True)
                          + jnp.sum(d_kb * k, axis=1, keepdims=True))
            gterm = d_a * cr["a_mat"] + d_p * p
            d_e = jnp.sum(d_kd * k, axis=1, keepdims=True) * e
            d_g_col = (jnp.sum(gterm, axis=1, keepdims=True)
                       + (jnp.sum(d_qg * q, axis=1, keepdims=True)
                          + jnp.sum(d_kbg * cr["kb"], axis=1, keepdims=True)) * gam
                       - d_e)
            d_g_last = jnp.sum(d_e, axis=0, keepdims=True) + d_gam_last * cr["gam_last"]
            d_g_row = (_col_to_row(d_g_col, eye) - jnp.sum(gterm, axis=0, keepdims=True)
                       + jnp.where(lane == A_CHUNK - 1, d_g_last, 0.0))
            dgc_ref[hh, pl.ds(ci, 1), :] = d_g_row
            dbeta_ref[hh, pl.ds(ci, 1), :] = _col_to_row(d_beta_col, eye)

            qh = cin["aq"] * cin["rq"]
            kh = cin["ak"] * cin["rk"]
            d_qh = d_q * (A_DK ** -0.5)
            d_aq = cin["rq"] * (d_qh - qh * jnp.sum(d_qh * qh, axis=1, keepdims=True))
            d_ak = cin["rk"] * (d_k - kh * jnp.sum(d_k * kh, axis=1, keepdims=True))
            d_y = jnp.concatenate([d_aq, d_ak, d_v], axis=1) * _dsilu(cin["y"])
            xw = cin["xw"]
            dyw = jnp.concatenate([d_y, dy_next[hh]], axis=0)
            d_x = cw[A_CONV - 1:A_CONV, :] * dyw
            for j in range(1, A_CONV):
                d_x = d_x + cw[A_CONV - 1 - j:A_CONV - j, :] * pltpu.roll(dyw, A_CHUNK + HALO - j, 0)
            dy_pad = jnp.concatenate([jnp.zeros((HALO, A_CONV_COLS), F32), d_y], axis=0)
            for j in range(A_CONV):
                xs = xw if j == 0 else pltpu.roll(xw, j, 0)
                dcw_ref[A_CONV - 1 - j:A_CONV - j, ccols] += jnp.sum(dy_pad * xs, axis=0, keepdims=True)
            dy_next[hh] = d_y[0:HALO, :]
            dx_ref[rows, lo:lo + A_CONV_COLS] = d_x[0:A_CHUNK, :].astype(BF16)
            dx_ref[rows, lo + A_CONV_COLS:lo + A_HEAD_COLS] = dz.astype(BF16)

        def chunk(ci, first):
            rows = pl.ds(0 if first else pl.multiple_of(ci * A_CHUNK, A_CHUNK), A_CHUNK)
            for hh in range(hp):
                one_head(hh, ci, first, rows)

        lax.fori_loop(0, A_BLK_CHUNKS - 1, lambda i, c: (chunk(A_BLK_CHUNKS - 1 - i, False), c)[1], 0)
        chunk(0, True)

    rev = lambda j: nsb - 1 - j
    small = pl.BlockSpec((None, hp, A_BLK_CHUNKS, A_CHUNK), lambda b, h, j: (b, h, rev(j), 0))
    wide = pl.BlockSpec((None, A_SEQ_BLK, hp * A_HEAD_COLS), lambda b, h, j: (b, rev(j), h))
    val = pl.BlockSpec((None, A_SEQ_BLK, hp * A_DV), lambda b, h, j: (b, rev(j), h))
    return pl.pallas_call(
        body, name="gdn_bwd", grid=(bn, A_HEADS // hp, nsb),
        in_specs=[wide,
                  pl.BlockSpec((None, None, HALO, hp * A_HEAD_COLS), lambda b, h, j: (b, rev(j), 0, h)),
                  pl.BlockSpec((A_CONV, hp * A_CONV_COLS), lambda b, h, j: (0, h)),
                  small, small,
                  pl.BlockSpec((1, A_DV), lambda b, h, j: (0, 0)),
                  val,
                  pl.BlockSpec((None, hp, A_BLK_CHUNKS, A_DK, A_DV), lambda b, h, j: (b, h, rev(j), 0, 0)),
                  pl.BlockSpec((None, hp, A_BLK_CHUNKS, A_CHUNK, A_CHUNK), lambda b, h, j: (b, h, rev(j), 0, 0)),
                  val],
        out_specs=[wide, small, small,
                   pl.BlockSpec((None, A_CONV, hp * A_CONV_COLS), lambda b, h, j: (b, 0, h)),
                   pl.BlockSpec((None, hp, 8, A_DV), lambda b, h, j: (b, h, 0, 0))],
        out_shape=[jax.ShapeDtypeStruct((bn, s, A_HEADS * A_HEAD_COLS), BF16),
                   jax.ShapeDtypeStruct((bn, A_HEADS, n, A_CHUNK), F32),
                   jax.ShapeDtypeStruct((bn, A_HEADS, n, A_CHUNK), F32),
                   jax.ShapeDtypeStruct((bn, A_CONV, A_HEADS * A_CONV_COLS), F32),
                   jax.ShapeDtypeStruct((bn, A_HEADS, 8, A_DV), F32)],
        scratch_shapes=[pltpu.VMEM((hp, A_DK, A_DV), F32), pltpu.VMEM((hp, HALO, A_CONV_COLS), F32)],
        compiler_params=_params(("parallel", "parallel", "arbitrary"), VMEM_BIG),
    )(proj_hm, halo, cw_hm, beta, gc, norm_g, oraw, states, t_mats, dog)


def _rope_tables(posf, inv_freq_row):
    t = posf.shape[0]
    tm = 512

    def body(p_ref, f_ref, c_ref, sa_ref, sb_ref):
        ang = p_ref[...] * f_ref[...]
        lane = lax.broadcasted_iota(jnp.int32, ang.shape, 1)
        half = ROPE_DIMS // 2
        c_ref[...] = jnp.where(lane < ROPE_DIMS, jnp.cos(ang), 1.0)
        sn = jnp.sin(ang)
        sa_ref[...] = jnp.where(lane < half, -sn, 0.0)
        sb_ref[...] = jnp.where((lane >= half) & (lane < ROPE_DIMS), sn, 0.0)

    row = pl.BlockSpec((tm, 128), lambda i: (i, 0))
    return pl.pallas_call(
        body, name="rope_tables", grid=(t // tm,),
        in_specs=[row, pl.BlockSpec((1, 128), lambda i: (0, 0))], out_specs=[row] * 3,
        out_shape=[jax.ShapeDtypeStruct((t, 128), F32)] * 3,
        compiler_params=_params(("parallel",)),
    )(posf, inv_freq_row)


def _rope(x, c, sa, sb):
    half = ROPE_DIMS // 2
    return x * c + pltpu.roll(x, 128 - half, 1) * sa + pltpu.roll(x, half, 1) * sb


def _rope_t(d, c, sa, sb):
    half = ROPE_DIMS // 2
    return d * c + pltpu.roll(d * sa, half, 1) + pltpu.roll(d * sb, 128 - half, 1)


def _qk_prep(proj, c, sa, sb, qg, kg, name, tm=256):
    t = proj.shape[0]
    wide = proj.shape[1]

    def body(x_ref, c_ref, sa_ref, sb_ref, qg_ref, kg_ref, o_ref):
        cc, s1, s2 = c_ref[...], sa_ref[...], sb_ref[...]
        for which, g_ref in ((0, qg_ref), (1, kg_ref)):
            g = g_ref[...]
            for h in range(B_HEADS):
                lo = which * B_W + h * B_DH
                xv = x_ref[:, lo:lo + B_DH]
                r = lax.rsqrt(jnp.mean(xv * xv, axis=1, keepdims=True) + EPS)
                o_ref[:, lo:lo + B_DH] = _rope(xv * r * g, cc, s1, s2).astype(BF16)
        o_ref[:, 2 * B_W:3 * B_W] = x_ref[:, 2 * B_W:3 * B_W].astype(BF16)

    tab = pl.BlockSpec((tm, 128), lambda i: (i, 0))
    gain = pl.BlockSpec((1, B_DH), lambda i: (0, 0))
    return pl.pallas_call(
        body, name=name, grid=(t // tm,),
        in_specs=[pl.BlockSpec((tm, wide), lambda i: (i, 0)), tab, tab, tab, gain, gain],
        out_specs=pl.BlockSpec((tm, 3 * B_W), lambda i: (i, 0)),
        out_shape=jax.ShapeDtypeStruct((t, 3 * B_W), BF16),
        compiler_params=_params(("parallel",), 40 * 1024 * 1024),
    )(proj, c, sa, sb, qg, kg)


def _qk_prep_bwd(proj, c, sa, sb, qg, kg, dq, dk, dv, dz, name, tm=256):
    t = proj.shape[0]
    wide = proj.shape[1]
    out_w = 3 * B_W + (B_W if dz is not None else 0)

    def body(*refs):
        x_ref, c_ref, sa_ref, sb_ref, qg_ref, kg_ref, dq_ref, dk_ref, dv_ref = refs[:9]
        if dz is not None:
            dz_ref, o_ref, dgain_ref = refs[9:]
        else:
            o_ref, dgain_ref = refs[9:]
        i = pl.program_id(0)

        @pl.when(i == 0)
        def _():
            dgain_ref[...] = jnp.zeros_like(dgain_ref)

        cc, s1, s2 = c_ref[...], sa_ref[...], sb_ref[...]
        for which, g_ref, d_ref in ((0, qg_ref, dq_ref), (1, kg_ref, dk_ref)):
            g = g_ref[...]
            acc = jnp.zeros((1, B_DH), F32)
            for h in range(B_HEADS):
                lo = which * B_W + h * B_DH
                xv = x_ref[:, lo:lo + B_DH]
                r = lax.rsqrt(jnp.mean(xv * xv, axis=1, keepdims=True) + EPS)
                xh = xv * r
                d_xn = _rope_t(d_ref[:, h * B_DH:(h + 1) * B_DH], cc, s1, s2)
                acc = acc + jnp.sum(d_xn * xh, axis=0, keepdims=True)
                d_xh = d_xn * g
                d_x = r * (d_xh - xh * jnp.mean(d_xh * xh, axis=1, keepdims=True))
                o_ref[:, lo:lo + B_DH] = d_x.astype(BF16)
            dgain_ref[which:which + 1, :] += acc
        o_ref[:, 2 * B_W:3 * B_W] = dv_ref[...].astype(BF16)
        if dz is not None:
            o_ref[:, 3 * B_W:4 * B_W] = dz_ref[...]

    tab = pl.BlockSpec((tm, 128), lambda i: (i, 0))
    gain = pl.BlockSpec((1, B_DH), lambda i: (0, 0))
    grad = pl.BlockSpec((tm, B_W), lambda i: (i, 0))
    in_specs = [pl.BlockSpec((tm, wide), lambda i: (i, 0)), tab, tab, tab, gain, gain, grad, grad, grad]
    args = [proj, c, sa, sb, qg, kg, dq, dk, dv]
    if dz is not None:
        in_specs.append(grad)
        args.append(dz)
    return pl.pallas_call(
        body, name=name, grid=(t // tm,), in_specs=in_specs,
        out_specs=[pl.BlockSpec((tm, out_w), lambda i: (i, 0)), pl.BlockSpec((8, B_DH), lambda i: (0, 0))],
        out_shape=[jax.ShapeDtypeStruct((t, out_w), BF16), jax.ShapeDtypeStruct((8, B_DH), F32)],
        compiler_params=_params(("arbitrary",), 40 * 1024 * 1024),
    )(*args)


def _attn_masks():
    qi = lax.broadcasted_iota(jnp.int32, (B_BLK, 2 * B_BLK), 0)
    kj = lax.broadcasted_iota(jnp.int32, (B_BLK, 2 * B_BLK), 1)
    two = (kj >= qi) & (kj <= qi + B_BLK)
    q1 = lax.broadcasted_iota(jnp.int32, (B_BLK, B_BLK), 0)
    k1 = lax.broadcasted_iota(jnp.int32, (B_BLK, B_BLK), 1)
    return k1 <= q1, two


def _lane_pick(ref_rows, h):
    lane = lax.broadcasted_iota(jnp.int32, ref_rows.shape, 1)
    return jnp.sum(jnp.where(lane == h, ref_rows, 0.0), axis=1, keepdims=True)


def _attn_fwd(qkv, name):
    ns, ln, _ = qkv.shape
    nb = ln // B_BLK
    scale = B_DH ** -0.5

    def body(q_ref, k_ref, v_ref, o_ref, lse_ref):
        h = pl.program_id(1)
        mask1, mask2 = _attn_masks()

        @pl.when(h == 0)
        def _():
            lse_ref[...] = jnp.zeros_like(lse_ref)

        def block(i, first):
            rows = pl.ds(pl.multiple_of(i * B_BLK, B_BLK), B_BLK)
            if first:
                win, mask = pl.ds(0, B_BLK), mask1
            else:
                win, mask = pl.ds(pl.multiple_of((i - 1) * B_BLK, B_BLK), 2 * B_BLK), mask2
            sc = jnp.where(mask, _dot(q_ref[rows, :], k_ref[win, :], 1, 1) * scale, -1e30)
            m = jnp.max(sc, axis=1, keepdims=True)
            p = jnp.exp(sc - m)
            l = jnp.sum(p, axis=1, keepdims=True)
            o_ref[rows, :] = _dot(p, v_ref[win, :], 1, 0) / l
            lane = lax.broadcasted_iota(jnp.int32, (B_BLK, B_HEADS), 1)
            lse_ref[rows, :] = jnp.where(lane == h, m + jnp.log(l), lse_ref[rows, :])

        block(0, True)
        if nb > 1:
            lax.fori_loop(1, nb, lambda i, c: (block(i, False), c)[1], 0)

    return pl.pallas_call(
        body, name=name, grid=(ns, B_HEADS),
        in_specs=[pl.BlockSpec((None, ln, B_DH), lambda s, h: (s, 0, h)),
                  pl.BlockSpec((None, ln, B_DH), lambda s, h: (s, 0, B_HEADS + h)),
                  pl.BlockSpec((None, ln, B_DH), lambda s, h: (s, 0, 2 * B_HEADS + h))],
        out_specs=[pl.BlockSpec((None, ln, B_DH), lambda s, h: (s, 0, h)),
                   pl.BlockSpec((None, ln, B_HEADS), lambda s, h: (s, 0, 0))],
        out_shape=[jax.ShapeDtypeStruct((ns, ln, B_W), F32), jax.ShapeDtypeStruct((ns, ln, B_HEADS), F32)],
        compiler_params=_params(("parallel", "arbitrary")),
    )(qkv, qkv, qkv)


def _attn_bwd(qkv, d_o, lse_joint, delta, name):
    ns, ln, _ = qkv.shape
    nb = ln // B_BLK
    scale = B_DH ** -0.5

    def body(q_ref, k_ref, v_ref, do_ref, lj_ref, dl_ref, dq_ref, dk_ref, dv_ref):
        h = pl.program_id(1)
        mask1, mask2 = _attn_masks()
        dk_ref[...] = jnp.zeros_like(dk_ref)
        dv_ref[...] = jnp.zeros_like(dv_ref)

        def block(i, first):
            rows = pl.ds(pl.multiple_of(i * B_BLK, B_BLK), B_BLK)
            if first:
                win, mask = pl.ds(0, B_BLK), mask1
            else:
                win, mask = pl.ds(pl.multiple_of((i - 1) * B_BLK, B_BLK), 2 * B_BLK), mask2
            q = q_ref[rows, :]
            d_out = do_ref[rows, :]
            l_col = _lane_pick(lj_ref[rows, :], h)
            d_col = _lane_pick(dl_ref[rows, :], h)
            sc = _dot(q, k_ref[win, :], 1, 1) * scale
            p = jnp.exp(jnp.where(mask, sc - l_col, -1e30))
            d_p = _dot(d_out, v_ref[win, :], 1, 1)
            d_s = p * (d_p - d_col) * scale
            dq_ref[rows, :] = _dot(d_s, k_ref[win, :], 1, 0)
            dk_ref[win, :] += _dot(d_s, q, 0, 0)
            dv_ref[win, :] += _dot(p, d_out, 0, 0)

        block(0, True)
        if nb > 1:
            lax.fori_loop(1, nb, lambda i, c: (block(i, False), c)[1], 0)

    head = lambda off: pl.BlockSpec((None, ln, B_DH), lambda s, h: (s, 0, off + h))
    small = pl.BlockSpec((None, ln, B_HEADS), lambda s, h: (s, 0, 0))
    return pl.pallas_call(
        body, name=name, grid=(ns, B_HEADS),
        in_specs=[head(0), head(B_HEADS), head(2 * B_HEADS), head(0), small, small],
        out_specs=[head(0)] * 3,
        out_shape=[jax.ShapeDtypeStruct((ns, ln, B_W), F32)] * 3,
        compiler_params=_params(("parallel", "parallel")),
    )(qkv, qkv, qkv, d_o, lse_joint, delta)


B_ROWS = 2048


def _attn_schedule(nb, sb, block):
    for si in range(sb):
        block(si, 0, True)
    if nb == 1:
        return
    per = 1 if sb > 1 else 2
    lead = 1 + (nb - 1) % per
    for i in range(1, lead):
        for si in range(sb):
            block(si, i, False)

    def step(it, carry):
        for u in range(per):
            for si in range(sb):
                block(si, lead + it * per + u, False)
        return carry

    lax.fori_loop(0, (nb - lead) // per, step, 0)


def _attn_rows(i, first):
    if first:
        return pl.ds(0, B_BLK), pl.ds(0, B_BLK)
    rows = pl.ds(pl.multiple_of(i * B_BLK, B_BLK), B_BLK)
    return rows, pl.ds(pl.multiple_of((i - 1) * B_BLK, B_BLK), 2 * B_BLK)


def _attn_fwd(qkv, name):
    ns, ln, _ = qkv.shape
    nb = ln // B_BLK
    sb = B_ROWS // ln
    scale = B_DH ** -0.5

    def body(q_ref, k_ref, v_ref, o_ref, lse_ref):
        h = pl.program_id(1)
        mask1, mask2 = _attn_masks()
        lane = lax.broadcasted_iota(jnp.int32, (B_BLK, B_HEADS), 1)

        @pl.when(h == 0)
        def _():
            lse_ref[...] = jnp.zeros_like(lse_ref)

        def block(si, i, first):
            rows, win = _attn_rows(i, first)
            mask = mask1 if first else mask2
            sc = jnp.where(mask, _dot(q_ref[si, rows, :], k_ref[si, win, :], 1, 1) * scale, -1e30)
            m = jnp.max(sc, axis=1, keepdims=True)
            p = jnp.exp(sc - m)
            l = jnp.sum(p, axis=1, keepdims=True)
            o_ref[si, rows, :] = _dot(p, v_ref[si, win, :], 1, 0) / l
            lse_ref[si, rows, :] = jnp.where(lane == h, m + jnp.log(l), lse_ref[si, rows, :])

        _attn_schedule(nb, sb, block)

    head = lambda off: pl.BlockSpec((sb, ln, B_DH), lambda s, h: (s, 0, off + h))
    return pl.pallas_call(
        body, name=name, grid=(ns // sb, B_HEADS),
        in_specs=[head(0), head(B_HEADS), head(2 * B_HEADS)],
        out_specs=[head(0), pl.BlockSpec((sb, ln, B_HEADS), lambda s, h: (s, 0, 0))],
        out_shape=[jax.ShapeDtypeStruct((ns, ln, B_W), F32), jax.ShapeDtypeStruct((ns, ln, B_HEADS), F32)],
        compiler_params=_params(("parallel", "arbitrary")),
    )(qkv, qkv, qkv)


def _attn_bwd(qkv, d_o, lse_joint, delta, name):
    ns, ln, _ = qkv.shape
    nb = ln // B_BLK
    sb = B_ROWS // ln
    scale = B_DH ** -0.5

    def body(q_ref, k_ref, v_ref, do_ref, lj_ref, dl_ref, dq_ref, dk_ref, dv_ref):
        h = pl.program_id(1)
        mask1, mask2 = _attn_masks()
        dk_ref[...] = jnp.zeros_like(dk_ref)
        dv_ref[...] = jnp.zeros_like(dv_ref)

        def block(si, i, first):
            rows, win = _attn_rows(i, first)
            mask = mask1 if first else mask2
            q = q_ref[si, rows, :]
            d_out = do_ref[si, rows, :]
            l_col = _lane_pick(lj_ref[si, rows, :], h)
            d_col = _lane_pick(dl_ref[si, rows, :], h)
            sc = _dot(q, k_ref[si, win, :], 1, 1) * scale
            p = jnp.exp(jnp.where(mask, sc - l_col, -1e30))
            d_p = _dot(d_out, v_ref[si, win, :], 1, 1)
            d_s = p * (d_p - d_col) * scale
            dq_ref[si, rows, :] = _dot(d_s, k_ref[si, win, :], 1, 0)
            dk_ref[si, win, :] += _dot(d_s, q, 0, 0)
            dv_ref[si, win, :] += _dot(p, d_out, 0, 0)

        _attn_schedule(nb, sb, block)

    head = lambda off: pl.BlockSpec((sb, ln, B_DH), lambda s, h: (s, 0, off + h))
    small = pl.BlockSpec((sb, ln, B_HEADS), lambda s, h: (s, 0, 0))
    return pl.pallas_call(
        body, name=name, grid=(ns // sb, B_HEADS),
        in_specs=[head(0), head(B_HEADS), head(2 * B_HEADS), head(0), small, small],
        out_specs=[head(0)] * 3,
        out_shape=[jax.ShapeDtypeStruct((ns, ln, B_W), F32)] * 3,
        compiler_params=_params(("parallel", "parallel")),
    )(qkv, qkv, qkv, d_o, lse_joint, delta)


def _merge_weights(lse_refs):
    ls = [r[...] for r in lse_refs]
    m = jnp.maximum(jnp.maximum(ls[0], ls[1]), ls[2])
    es = [jnp.exp(l - m) for l in ls]
    tot = es[0] + es[1] + es[2]
    return [e / tot for e in es], m + jnp.log(tot)


def _merge_fwd(outs, lses, proj0, tm=256):
    t = outs[0].shape[0]

    def body(o0, o1, o2, l0, l1, l2, z_ref, og_ref):
        wts, _ = _merge_weights((l0, l1, l2))
        for h in range(B_HEADS):
            cols = slice(h * B_DH, (h + 1) * B_DH)
            o = (wts[0][:, h:h + 1] * o0[:, cols] + wts[1][:, h:h + 1] * o1[:, cols]
                 + wts[2][:, h:h + 1] * o2[:, cols])
            og_ref[:, cols] = (o * _silu(z_ref[:, cols])).astype(BF16)

    wide = pl.BlockSpec((tm, B_W), lambda i: (i, 0))
    small = pl.BlockSpec((tm, B_HEADS), lambda i: (i, 0))
    return pl.pallas_call(
        body, name="merge_fwd", grid=(t // tm,),
        in_specs=[wide] * 3 + [small] * 3 + [pl.BlockSpec((tm, B_W), lambda i: (i, 3))],
        out_specs=wide, out_shape=jax.ShapeDtypeStruct((t, B_W), BF16),
        compiler_params=_params(("parallel",)),
    )(*outs, *lses, proj0)


def _merge_bwd(outs, lses, proj0, d_og, tm=256):
    t = outs[0].shape[0]

    def body(o0, o1, o2, l0, l1, l2, z_ref, dog_ref, do_ref, lj_ref, dl_ref, dz_ref):
        wts, lj = _merge_weights((l0, l1, l2))
        lj_ref[...] = lj
        lane = lax.broadcasted_iota(jnp.int32, (tm, B_HEADS), 1)
        delta = jnp.zeros((tm, B_HEADS), F32)
        for h in range(B_HEADS):
            cols = slice(h * B_DH, (h + 1) * B_DH)
            o = (wts[0][:, h:h + 1] * o0[:, cols] + wts[1][:, h:h + 1] * o1[:, cols]
                 + wts[2][:, h:h + 1] * o2[:, cols])
            z = z_ref[:, cols]
            d_g = dog_ref[:, cols]
            d_out = d_g * _silu(z)
            dz_ref[:, cols] = (d_g * o * _dsilu(z)).astype(BF16)
            do_ref[:, cols] = d_out.astype(BF16)
            delta = jnp.where(lane == h, jnp.sum(d_out * o, axis=1, keepdims=True), delta)
        dl_ref[...] = delta

    wide = pl.BlockSpec((tm, B_W), lambda i: (i, 0))
    small = pl.BlockSpec((tm, B_HEADS), lambda i: (i, 0))
    return pl.pallas_call(
        body, name="merge_bwd", grid=(t // tm,),
        in_specs=[wide] * 3 + [small] * 3 + [pl.BlockSpec((tm, B_W), lambda i: (i, 3)), wide],
        out_specs=[wide, small, small, wide],
        out_shape=[jax.ShapeDtypeStruct((t, B_W), BF16), jax.ShapeDtypeStruct((t, B_HEADS), F32),
                   jax.ShapeDtypeStruct((t, B_HEADS), F32), jax.ShapeDtypeStruct((t, B_W), BF16)],
        compiler_params=_params(("parallel",)),
    )(*outs, *lses, proj0, d_og)


def _adamw(w, g, m, v, name):
    r, c = w.shape
    tr = r
    for cand in (256, 128, 64, 32, 16, 8):
        if r % cand == 0:
            tr = cand
            break

    def body(w_ref, g_ref, m_ref, v_ref, d_ref, nm_ref, nv_ref):
        gv = g_ref[...]
        nm = ADAM_B1 * m_ref[...] + (1.0 - ADAM_B1) * gv
        nv = ADAM_B2 * v_ref[...] + (1.0 - ADAM_B2) * (gv * gv)
        m_hat = nm / (1.0 - ADAM_B1 ** ADAM_STEP)
        v_hat = nv / (1.0 - ADAM_B2 ** ADAM_STEP)
        d_ref[...] = -ADAM_LR * (m_hat / (jnp.sqrt(v_hat) + ADAM_EPS) + ADAM_WD * w_ref[...])
        nm_ref[...] = nm
        nv_ref[...] = nv

    blk = pl.BlockSpec((tr, c), lambda i: (i, 0))
    return pl.pallas_call(
        body, name=name, grid=(r // tr,), in_specs=[blk] * 4, out_specs=[blk] * 3,
        out_shape=[jax.ShapeDtypeStruct((r, c), F32)] * 3,
        compiler_params=_params(("parallel",)),
    )(w, g, m, v)


def _pair_sum(own, other, half_index, name, tr=256):
    _, r, c = own.shape
    rh = r // 2
    tr = min(tr, rh)
    nrb = rh // tr

    def body(c_ref, own_ref, oth_ref, out_ref):
        out_ref[...] = (own_ref[...] + oth_ref[...].astype(F32)).astype(BF16)

    return pl.pallas_call(
        body, name=name,
        grid_spec=pltpu.PrefetchScalarGridSpec(
            num_scalar_prefetch=1, grid=(N_CHIPS, nrb),
            in_specs=[pl.BlockSpec((None, tr, c), lambda k, i, cc: (k, cc[0] * nrb + i, 0)),
                      pl.BlockSpec((None, tr, c), lambda k, i, cc: (k, i, 0))],
            out_specs=pl.BlockSpec((None, tr, c), lambda k, i, cc: (k, i, 0))),
        out_shape=jax.ShapeDtypeStruct((N_CHIPS, rh, c), BF16),
        compiler_params=_params(("parallel", "parallel")),
    )(half_index, own, other)


def _chip_sum(parts, name, tr=256):
    _, r, c = parts.shape
    tr = min(tr, r)

    def body(p_ref, out_ref):
        acc = p_ref[0].astype(F32)
        for k in range(1, N_CHIPS):
            acc = acc + p_ref[k].astype(F32)
        out_ref[...] = acc

    return pl.pallas_call(
        body, name=name, grid=(r // tr,),
        in_specs=[pl.BlockSpec((N_CHIPS, tr, c), lambda i: (0, i, 0))],
        out_specs=pl.BlockSpec((tr, c), lambda i: (i, 0)),
        out_shape=jax.ShapeDtypeStruct((r, c), F32),
        compiler_params=_params(("parallel",)),
    )(parts)


HBM = pl.BlockSpec(memory_space=pltpu.HBM)


def _place():
    x, y, c = lax.axis_index("x"), lax.axis_index("y"), lax.axis_index("c")
    chips = [(1 - x, y), (x, 1 - y), (1 - x, 1 - y)]
    return x, y, c, chips


def _weight_allgather(shards, conv_shard):
    na = len(shards)

    def body(*refs):
        ins = refs[:na]
        conv_in = refs[na]
        outs = refs[na + 1:2 * na + 1]
        conv_out = refs[2 * na + 1]
        send, recv, fsend, frecv, lsem, csend, crecv = refs[2 * na + 2:]
        x, y, c, chips = _place()
        me = 2 * x + y
        sib = (x, y, 1 - c)
        local, first, conv_cp = [], [], []
        for i in range(na):
            rh = ins[i].shape[0] // 2
            mine = pl.ds(c * rh, rh)
            cp = pltpu.make_async_copy(ins[i], outs[i].at[me], lsem.at[i])
            cp.start()
            local.append(cp)
            for j, (px, py) in enumerate(chips):
                cp = pltpu.make_async_remote_copy(
                    src_ref=ins[i].at[mine], dst_ref=outs[i].at[me, mine],
                    send_sem=send.at[3 * i + j], recv_sem=recv.at[3 * i + j],
                    device_id=(px, py, c), device_id_type=MESH)
                cp.start()
                first.append(cp)
        cp = pltpu.make_async_copy(conv_in, conv_out.at[me], lsem.at[na])
        cp.start()
        local.append(cp)
        for j, (px, py) in enumerate(chips):
            cp = pltpu.make_async_remote_copy(
                src_ref=conv_in, dst_ref=conv_out.at[me], send_sem=csend.at[j], recv_sem=crecv.at[j],
                device_id=(px, py, c), device_id_type=MESH)
            cp.start()
            conv_cp.append(cp)
        passed = []
        for i in range(na):
            rh = ins[i].shape[0] // 2
            mine = pl.ds(c * rh, rh)
            for j, (px, py) in enumerate(chips):
                slot = outs[i].at[2 * px + py, mine]
                pltpu.make_async_remote_copy(
                    src_ref=slot, dst_ref=slot, send_sem=send.at[3 * i + j], recv_sem=recv.at[3 * i + j],
                    device_id=(px, py, c), device_id_type=MESH).wait_recv()
                cp = pltpu.make_async_remote_copy(
                    src_ref=slot, dst_ref=slot, send_sem=fsend.at[3 * i + j], recv_sem=frecv.at[3 * i + j],
                    device_id=sib, device_id_type=MESH)
                cp.start()
                passed.append(cp)
        for i in range(na):
            rh = ins[i].shape[0] // 2
            theirs = pl.ds((1 - c) * rh, rh)
            for j, (px, py) in enumerate(chips):
                slot = outs[i].at[2 * px + py, theirs]
                pltpu.make_async_remote_copy(
                    src_ref=slot, dst_ref=slot, send_sem=fsend.at[3 * i + j], recv_sem=frecv.at[3 * i + j],
                    device_id=sib, device_id_type=MESH).wait_recv()
        for j, (px, py) in enumerate(chips):
            slot = conv_out.at[2 * px + py]
            pltpu.make_async_remote_copy(
                src_ref=slot, dst_ref=slot, send_sem=csend.at[j], recv_sem=crecv.at[j],
                device_id=(px, py, c), device_id_type=MESH).wait_recv()
        for cp in first + passed + conv_cp:
            cp.wait_send()
        for cp in local:
            cp.wait()

    out_shape = [jax.ShapeDtypeStruct((N_CHIPS,) + s.shape, s.dtype) for s in shards]
    out_shape.append(jax.ShapeDtypeStruct((N_CHIPS,) + conv_shard.shape, conv_shard.dtype))
    res = pl.pallas_call(
        body, name="weight_allgather",
        in_specs=[HBM] * (na + 1), out_specs=[HBM] * (na + 1), out_shape=out_shape,
        scratch_shapes=[pltpu.SemaphoreType.DMA((3 * na,)), pltpu.SemaphoreType.DMA((3 * na,)),
                        pltpu.SemaphoreType.DMA((3 * na,)), pltpu.SemaphoreType.DMA((3 * na,)),
                        pltpu.SemaphoreType.DMA((na + 1,)), pltpu.SemaphoreType.DMA((3,)),
                        pltpu.SemaphoreType.DMA((3,))],
    )(*shards, conv_shard)
    return res[:na], res[na]


def _sibling_swap_halves(grads):
    na = len(grads)

    def body(*refs):
        ins, outs = refs[:na], refs[na:2 * na]
        send, recv = refs[2 * na:]
        x, y, c, _ = _place()
        sib = (x, y, 1 - c)
        cps = []
        for i in range(na):
            rh = ins[i].shape[1] // 2
            cp = pltpu.make_async_remote_copy(
                src_ref=ins[i].at[:, pl.ds((1 - c) * rh, rh), :], dst_ref=outs[i],
                send_sem=send.at[i], recv_sem=recv.at[i], device_id=sib, device_id_type=MESH)
            cp.start()
            cps.append(cp)
        for cp in cps:
            cp.wait()

    out_shape = [jax.ShapeDtypeStruct((g.shape[0], g.shape[1] // 2, g.shape[2]), g.dtype) for g in grads]
    return pl.pallas_call(
        body, name="grad_sibling_swap", in_specs=[HBM] * na, out_specs=[HBM] * na, out_shape=out_shape,
        scratch_shapes=[pltpu.SemaphoreType.DMA((na,)), pltpu.SemaphoreType.DMA((na,))],
    )(*grads)


def _chip_exchange(sums):
    na = len(sums)

    def body(*refs):
        ins, outs = refs[:na], refs[na:2 * na]
        send, recv, lsem = refs[2 * na:]
        x, y, c, chips = _place()
        me = 2 * x + y
        cps, local = [], []
        for i in range(na):
            cp = pltpu.make_async_copy(ins[i].at[me], outs[i].at[me], lsem.at[i])
            cp.start()
            local.append(cp)
            for j, (px, py) in enumerate(chips):
                cp = pltpu.make_async_remote_copy(
                    src_ref=ins[i].at[2 * px + py], dst_ref=outs[i].at[me],
                    send_sem=send.at[3 * i + j], recv_sem=recv.at[3 * i + j],
                    device_id=(px, py, c), device_id_type=MESH)
                cp.start()
                cps.append(cp)
        for i in range(na):
            for j, (px, py) in enumerate(chips):
                slot = outs[i].at[2 * px + py]
                pltpu.make_async_remote_copy(
                    src_ref=slot, dst_ref=slot, send_sem=send.at[3 * i + j], recv_sem=recv.at[3 * i + j],
                    device_id=(px, py, c), device_id_type=MESH).wait_recv()
        for cp in cps:
            cp.wait_send()
        for cp in local:
            cp.wait()

    out_shape = [jax.ShapeDtypeStruct(s.shape, s.dtype) for s in sums]
    return pl.pallas_call(
        body, name="grad_chip_exchange", in_specs=[HBM] * na, out_specs=[HBM] * na, out_shape=out_shape,
        scratch_shapes=[pltpu.SemaphoreType.DMA((3 * na,)), pltpu.SemaphoreType.DMA((3 * na,)),
                        pltpu.SemaphoreType.DMA((na,))],
    )(*sums)


def _sibling_join_halves(halves):
    na = len(halves)

    def body(*refs):
        ins, outs = refs[:na], refs[na:2 * na]
        send, recv, lsem = refs[2 * na:]
        x, y, c, _ = _place()
        sib = (x, y, 1 - c)
        cps, local = [], []
        for i in range(na):
            rh = ins[i].shape[0]
            mine = pl.ds(c * rh, rh)
            cp = pltpu.make_async_copy(ins[i], outs[i].at[mine], lsem.at[i])
            cp.start()
            local.append(cp)
            cp = pltpu.make_async_remote_copy(
                src_ref=ins[i], dst_ref=outs[i].at[mine], send_sem=send.at[i], recv_sem=recv.at[i],
                device_id=sib, device_id_type=MESH)
            cp.start()
            cps.append(cp)
        for i in range(na):
            rh = ins[i].shape[0]
            theirs = outs[i].at[pl.ds((1 - c) * rh, rh)]
            pltpu.make_async_remote_copy(
                src_ref=theirs, dst_ref=theirs, send_sem=send.at[i], recv_sem=recv.at[i],
                device_id=sib, device_id_type=MESH).wait_recv()
        for cp in cps:
            cp.wait_send()
        for cp in local:
            cp.wait()

    out_shape = [jax.ShapeDtypeStruct((2 * h.shape[0], h.shape[1]), h.dtype) for h in halves]
    return pl.pallas_call(
        body, name="grad_sibling_join", in_specs=[HBM] * na, out_specs=[HBM] * na, out_shape=out_shape,
        scratch_shapes=[pltpu.SemaphoreType.DMA((na,)), pltpu.SemaphoreType.DMA((na,)),
                        pltpu.SemaphoreType.DMA((na,))],
    )(*halves)


def _small_allreduce(vec):
    r, cdim = vec.shape
    n_dev = 8

    def body(v_ref, out_ref, buf, send, recv):
        x, y, c, _ = _place()
        me = 4 * x + 2 * y + c
        buf[me] = v_ref[...]
        cps = []
        for k in range(1, n_dev):
            dx, dy, dc = (k >> 2) & 1, (k >> 1) & 1, k & 1
            peer = (x ^ dx, y ^ dy, c ^ dc)
            cp = pltpu.make_async_remote_copy(
                src_ref=v_ref, dst_ref=buf.at[me], send_sem=send.at[k - 1], recv_sem=recv.at[k - 1],
                device_id=peer, device_id_type=MESH)
            cp.start()
            cps.append(cp)
        for k in range(1, n_dev):
            dx, dy, dc = (k >> 2) & 1, (k >> 1) & 1, k & 1
            src = 4 * (x ^ dx) + 2 * (y ^ dy) + (c ^ dc)
            slot = buf.at[src]
            pltpu.make_async_remote_copy(
                src_ref=slot, dst_ref=slot, send_sem=send.at[k - 1], recv_sem=recv.at[k - 1],
                device_id=(x ^ dx, y ^ dy, c ^ dc), device_id_type=MESH).wait_recv()
        for cp in cps:
            cp.wait_send()
        acc = buf[0]
        for k in range(1, n_dev):
            acc = acc + buf[k]
        out_ref[...] = acc

    vm = pl.BlockSpec(memory_space=pltpu.VMEM)
    return pl.pallas_call(
        body, name="small_allreduce", in_specs=[vm], out_specs=vm,
        out_shape=jax.ShapeDtypeStruct((r, cdim), F32),
        scratch_shapes=[pltpu.VMEM((n_dev, r, cdim), F32), pltpu.SemaphoreType.DMA((n_dev - 1,)),
                        pltpu.SemaphoreType.DMA((n_dev - 1,))],
    )(vec)


def _a_cols_to_head_major(w):
    lead = w.shape[:-1]
    q = w[..., :A_QK].reshape(lead + (A_HEADS, A_DK))
    k = w[..., A_QK:2 * A_QK].reshape(lead + (A_HEADS, A_DK))
    v = w[..., 2 * A_QK:2 * A_QK + A_VW].reshape(lead + (A_HEADS, A_DV))
    z = w[..., 2 * A_QK + A_VW:].reshape(lead + (A_HEADS, A_DV))
    return jnp.concatenate([q, k, v, z], axis=-1).reshape(lead + (A_HEADS * A_HEAD_COLS,))


def _a_cols_from_head_major(w):
    lead = w.shape[:-1]
    w = w.reshape(lead + (A_HEADS, A_HEAD_COLS))
    parts = [w[..., :A_DK], w[..., A_DK:2 * A_DK], w[..., 2 * A_DK:2 * A_DK + A_DV], w[..., 2 * A_DK + A_DV:]]
    return jnp.concatenate([p.reshape(lead + (-1,)) for p in parts], axis=-1)


def _conv_cols_to_head_major(w):
    lead = w.shape[:-1]
    q = w[..., :A_QK].reshape(lead + (A_HEADS, A_DK))
    k = w[..., A_QK:2 * A_QK].reshape(lead + (A_HEADS, A_DK))
    v = w[..., 2 * A_QK:].reshape(lead + (A_HEADS, A_DV))
    return jnp.concatenate([q, k, v], axis=-1).reshape(lead + (A_HEADS * A_CONV_COLS,))


def _conv_cols_from_head_major(w):
    lead = w.shape[:-1]
    w = w.reshape(lead + (A_HEADS, A_CONV_COLS))
    parts = [w[..., :A_DK], w[..., A_DK:2 * A_DK], w[..., 2 * A_DK:]]
    return jnp.concatenate([p.reshape(lead + (-1,)) for p in parts], axis=-1)


def _to_stream(a, bn, d):
    rest = a.shape[1:]
    s = a.shape[0] // bn
    a = a.reshape((bn, s // d, d) + rest)
    a = jnp.swapaxes(a, 1, 2)
    return a.reshape((bn * d, s // d) + rest)


def _from_stream(a, bn, d):
    rest = a.shape[2:]
    ln = a.shape[1]
    a = a.reshape((bn, d, ln) + rest)
    a = jnp.swapaxes(a, 1, 2)
    return a.reshape((bn * ln * d,) + rest)


def _b_group_cols(w, gi):
    n_qkv = 3 * B_GROUPS * B_W
    qkv = w[..., :n_qkv].reshape(w.shape[:-1] + (3, B_GROUPS, B_W))
    return qkv[..., :, gi, :].reshape(w.shape[:-1] + (3 * B_W,))


def _shard_major(g, ncols):
    r = g.shape[0]
    return jnp.swapaxes(g.reshape(r, N_CHIPS, ncols), 0, 1)


def _pack_rows(items):
    rows, offs = [], []
    at = 0
    for a in items:
        flat = a.reshape(-1).astype(F32)
        nr = -(-flat.shape[0] // 128)
        flat = jnp.pad(flat, (0, nr * 128 - flat.shape[0]))
        rows.append(flat.reshape(nr, 128))
        offs.append((at, nr, a.shape))
        at += nr
    pad = -at % 8
    if pad:
        rows.append(jnp.zeros((pad, 128), F32))
    return jnp.concatenate(rows, axis=0), offs


def _unpack_rows(packed, offs):
    out = []
    for at, nr, shape in offs:
        size = int(np.prod(shape)) if len(shape) else 1
        out.append(packed[at:at + nr].reshape(-1)[:size].reshape(shape))
    return out


def _local_step(x, positions, loss_target, norm_g, wa_in, conv_w, a_log, a_dt_bias, a_norm_g, wa_out,
                wb_in, b_q_norm_g, b_k_norm_g, wb_out):
    bn, s, d = x.shape
    t = bn * s
    n_chunks = s // A_CHUNK
    wa_main = _a_cols_to_head_major(wa_in[:, :A_MAIN])
    wa_tail = jnp.pad(wa_in[:, A_MAIN:], ((0, 0), (0, 128 - 2 * A_HEADS)))
    cw_hm = _conv_cols_to_head_major(conv_w)
    wb_groups = [_b_group_cols(wb_in, gi) for gi in range(B_GROUPS)]
    wb_groups[0] = jnp.concatenate([wb_groups[0], wb_in[:, 3 * B_GROUPS * B_W:]], axis=1)

    x0 = x.reshape(t, d)
    h0 = _rms_fwd(x0, norm_g[0:1], "rms0_fwd")
    proj_a = _matmul(h0, wa_main, "nn", F32, "a_in_main")
    tail_a = _matmul(h0, wa_tail, "nn", F32, "a_in_tail")
    tail_t = jnp.swapaxes(tail_a[:, :2 * A_HEADS].reshape(bn, s, 2 * A_HEADS), 1, 2)
    tail_t = tail_t.reshape(bn, 2 * A_HEADS, n_chunks, A_CHUNK)
    beta, gc = _gdn_prep(tail_t, a_log[0], a_dt_bias[0])
    proj_a3 = proj_a.reshape(bn, s, A_MAIN)
    og_a, oraw_a, states, t_mats = _gdn_fwd(proj_a3, cw_hm, beta, gc, a_norm_g)
    x1 = _matmul(og_a.reshape(t, A_VW), wa_out, "nn", F32, "a_out", res=x0, tk=2048)

    h1 = _rms_fwd(x1, norm_g[1:2], "rms1_fwd")
    inv_freq = ROPE_THETA ** (-jnp.arange(0, ROPE_DIMS, 2, dtype=F32) / ROPE_DIMS)
    freq_row = jnp.concatenate([inv_freq, inv_freq, jnp.zeros((128 - ROPE_DIMS,), F32)]).reshape(1, 128)
    posf = jnp.broadcast_to(positions.astype(F32).reshape(t, 1), (t, 128))
    tabs = _rope_tables(posf, freq_row)
    h1_s, tabs_s, proj_b, qkv_b, o_b, lse_b = [], [], [], [], [], []
    for gi, dil in enumerate(B_DIL):
        hs = h1 if dil == 1 else _to_stream(h1, bn, dil).reshape(t, d)
        ts = tabs if dil == 1 else [_to_stream(tb, bn, dil).reshape(t, 128) for tb in tabs]
        pj = _matmul(hs, wb_groups[gi], "nn", F32, f"b_in_g{gi}")
        qkv = _qk_prep(pj, *ts, b_q_norm_g[0, gi:gi + 1], b_k_norm_g[0, gi:gi + 1], f"qk_prep_g{gi}")
        o_s, lse_s = _attn_fwd(qkv.reshape(bn * dil, s // dil, 3 * B_W), f"attn_fwd_g{gi}")
        h1_s.append(hs), tabs_s.append(ts), proj_b.append(pj), qkv_b.append(qkv)
        o_b.append(o_s.reshape(t, B_W) if dil == 1 else _from_stream(o_s, bn, dil))
        lse_b.append(lse_s.reshape(t, B_HEADS) if dil == 1 else _from_stream(lse_s, bn, dil))
    og_b = _merge_fwd(o_b, lse_b, proj_b[0])
    x2 = _matmul(og_b, wb_out, "nn", F32, "b_out", res=x1)

    d_x2, loss_parts = _loss_grad(x2, loss_target.reshape(t, d))
    loss_local = jnp.sum(loss_parts)

    d_x2b = d_x2.astype(BF16)
    g_wb_out = _matmul(og_b.T, d_x2b, "nn", F32, "b_out_dw")
    d_og_b = _matmul(d_x2b, wb_out, "nt", F32, "b_out_dx")
    d_o, lse_joint, delta, d_z = _merge_bwd(o_b, lse_b, proj_b[0], d_og_b)
    d_h1, g_wb_cols, g_qn, g_kn = [], [], [], []
    for gi, dil in enumerate(B_DIL):
        if dil == 1:
            do_s, lj_s, dl_s = d_o, lse_joint, delta
        else:
            do_s, lj_s, dl_s = (_to_stream(a, bn, dil).reshape(t, -1) for a in (d_o, lse_joint, delta))
        ns, ln = bn * dil, s // dil
        dq, dk, dv = _attn_bwd(qkv_b[gi].reshape(ns, ln, 3 * B_W), do_s.reshape(ns, ln, B_W),
                               lj_s.reshape(ns, ln, B_HEADS), dl_s.reshape(ns, ln, B_HEADS), f"attn_bwd_g{gi}")
        d_pj, d_gain = _qk_prep_bwd(proj_b[gi], *tabs_s[gi], b_q_norm_g[0, gi:gi + 1], b_k_norm_g[0, gi:gi + 1],
                                    dq.reshape(t, B_W), dk.reshape(t, B_W), dv.reshape(t, B_W),
                                    d_z if gi == 0 else None, f"qk_prep_bwd_g{gi}")
        g_wb_cols.append(_matmul(h1_s[gi].T, d_pj, "nn", F32, f"b_in_dw_g{gi}"))
        dh = _matmul(d_pj, wb_groups[gi], "nt", F32, f"b_in_dx_g{gi}")
        d_h1.append(dh if dil == 1 else _from_stream(dh.reshape(ns, ln, d), bn, dil))
        g_qn.append(d_gain[0]), g_kn.append(d_gain[1])
    d_x1, g_norm1 = _rms_bwd(x1, norm_g[1:2], d_h1, d_x2, "rms1_bwd")
    pieces = [g_wb_cols[gi][:, w * B_W:(w + 1) * B_W] for w in range(3) for gi in range(B_GROUPS)]
    g_wb_in = jnp.concatenate(pieces + [g_wb_cols[0][:, 3 * B_W:]], axis=1)

    d_x1b = d_x1.astype(BF16)
    g_wa_out = _matmul(og_a.reshape(t, A_VW).T, d_x1b, "nn", F32, "a_out_dw")
    d_og_a = _matmul(d_x1b, wa_out, "nt", F32, "a_out_dx")
    d_pa, d_gc, d_beta, d_cw, d_ng = _gdn_bwd(proj_a3, cw_hm, beta, gc, a_norm_g, oraw_a, states, t_mats,
                                              d_og_a.reshape(bn, s, A_VW))
    d_tail_t, d_alog, d_dtb = _gdn_prep_bwd(tail_t, a_log[0], a_dt_bias[0], d_gc, d_beta)
    d_tail = jnp.swapaxes(d_tail_t.reshape(bn, 2 * A_HEADS, s), 1, 2).reshape(t, 2 * A_HEADS)
    d_tail = jnp.pad(d_tail, ((0, 0), (0, 128 - 2 * A_HEADS))).astype(BF16)
    d_pa = d_pa.reshape(t, A_MAIN)
    h0_t = h0.T
    g_wa_main = _matmul(h0_t, d_pa, "nn", F32, "a_in_dw_main")
    g_wa_tail = _matmul(h0_t, d_tail, "nn", F32, "a_in_dw_tail")
    d_h0 = _matmul(d_pa, wa_main, "nt", F32, "a_in_dx_main")
    d_h0t = _matmul(d_tail, wa_tail, "nt", F32, "a_in_dx_tail")
    d_x0, g_norm0 = _rms_bwd(x0, norm_g[0:1], [d_h0, d_h0t], d_x1, "rms0_bwd")
    g_wa_in = jnp.concatenate([_a_cols_from_head_major(g_wa_main), g_wa_tail[:, :2 * A_HEADS]], axis=1)

    gfull = {
        "norm_g": jnp.concatenate([g_norm0, g_norm1], axis=0), "a_w_in": g_wa_in,
        "a_conv_w": _conv_cols_from_head_major(jnp.sum(d_cw, axis=0)),
        "a_log": jnp.sum(d_alog[:, :, 0], axis=0), "a_dt_bias": jnp.sum(d_dtb[:, :, 0], axis=0),
        "a_norm_g": jnp.sum(d_ng[:, :, 0, :], axis=(0, 1)), "a_w_out": g_wa_out, "b_w_in": g_wb_in,
        "b_q_norm_g": jnp.stack(g_qn), "b_k_norm_g": jnp.stack(g_kn), "b_w_out": g_wb_out}
    return loss_local, d_x0.reshape(bn, s, d), gfull


def kernel(x, positions, norm_g, a_w_in, a_conv_w, a_log, a_dt_bias, a_norm_g, a_w_out, b_w_in, b_q_norm_g, b_k_norm_g, b_w_out, loss_target, m_norm_g, m_a_w_in, m_a_conv_w, m_a_log, m_a_dt_bias, m_a_norm_g, m_a_w_out, m_b_w_in, m_b_q_norm_g, m_b_k_norm_g, m_b_w_out, v_norm_g, v_a_w_in, v_a_conv_w, v_a_log, v_a_dt_bias, v_a_norm_g, v_a_w_out, v_b_w_in, v_b_q_norm_g, v_b_k_norm_g, v_b_w_out):
    d = x.shape[2]
    my_c = lax.axis_index("c")
    my_chip = 2 * lax.axis_index("x") + lax.axis_index("y")

    shards = [a_w_in[0].astype(BF16), a_w_out[0].astype(BF16), b_w_in[0].astype(BF16), b_w_out[0].astype(BF16)]
    (ga_in, ga_out, gb_in, gb_out), g_conv = _weight_allgather(shards, a_conv_w[0])
    wa_in = jnp.concatenate([ga_in[k] for k in range(N_CHIPS)], axis=1)
    wa_out = ga_out.reshape(A_VW, d)
    wb_in = jnp.concatenate([gb_in[k] for k in range(N_CHIPS)], axis=1)
    wb_out = gb_out.reshape(B_W, d)
    conv_w = jnp.concatenate([g_conv[k] for k in range(N_CHIPS)], axis=1)

    loss_local, d_x0, gfull = _local_step(x, positions, loss_target, norm_g, wa_in, conv_w, a_log, a_dt_bias,
                                          a_norm_g, wa_out, wb_in, b_q_norm_g, b_k_norm_g, wb_out)

    g_full = [_shard_major(gfull["a_w_in"], a_w_in.shape[2]), gfull["a_w_out"].reshape(N_CHIPS, -1, d),
              _shard_major(gfull["b_w_in"], b_w_in.shape[2]), gfull["b_w_out"].reshape(N_CHIPS, -1, d)]
    recv_sib = _sibling_swap_halves([g.astype(BF16) for g in g_full])
    half_index = jnp.reshape(my_c, (1,)).astype(jnp.int32)
    chip_sums = [_pair_sum(g, r, half_index, f"grad_pair_sum_{i}") for i, (g, r) in enumerate(zip(g_full, recv_sib))]
    by_source = _chip_exchange(chip_sums)
    halves = [_chip_sum(p, f"grad_chip_sum_{i}") for i, p in enumerate(by_source)]
    g_a_w_in, g_a_w_out, g_b_w_in, g_b_w_out = _sibling_join_halves(halves)

    small = [gfull["norm_g"], gfull["a_conv_w"], gfull["a_log"], gfull["a_dt_bias"], gfull["a_norm_g"],
             gfull["b_q_norm_g"], gfull["b_k_norm_g"], loss_local]
    packed, offs = _pack_rows(small)
    red = _unpack_rows(_small_allreduce(packed), offs)
    g_norm, g_conv_all, g_alog, g_dtb, g_ang, g_q, g_k, loss = red
    g_conv_mine = lax.dynamic_slice_in_dim(g_conv_all, my_chip * a_conv_w.shape[2], a_conv_w.shape[2], axis=1)

    grads = {
        "norm_g": g_norm, "a_w_in": g_a_w_in[None], "a_conv_w": g_conv_mine[None], "a_log": g_alog[None],
        "a_dt_bias": g_dtb[None], "a_norm_g": g_ang[None], "a_w_out": g_a_w_out[None], "b_w_in": g_b_w_in[None],
        "b_q_norm_g": g_q[None], "b_k_norm_g": g_k[None], "b_w_out": g_b_w_out[None]}
    weights = {"norm_g": norm_g, "a_w_in": a_w_in, "a_conv_w": a_conv_w, "a_log": a_log, "a_dt_bias": a_dt_bias,
               "a_norm_g": a_norm_g, "a_w_out": a_w_out, "b_w_in": b_w_in, "b_q_norm_g": b_q_norm_g,
               "b_k_norm_g": b_k_norm_g, "b_w_out": b_w_out}
    m_in = {"norm_g": m_norm_g, "a_w_in": m_a_w_in, "a_conv_w": m_a_conv_w, "a_log": m_a_log,
            "a_dt_bias": m_a_dt_bias, "a_norm_g": m_a_norm_g, "a_w_out": m_a_w_out, "b_w_in": m_b_w_in,
            "b_q_norm_g": m_b_q_norm_g, "b_k_norm_g": m_b_k_norm_g, "b_w_out": m_b_w_out}
    v_in = {"norm_g": v_norm_g, "a_w_in": v_a_w_in, "a_conv_w": v_a_conv_w, "a_log": v_a_log,
            "a_dt_bias": v_a_dt_bias, "a_norm_g": v_a_norm_g, "a_w_out": v_a_w_out, "b_w_in": v_b_w_in,
            "b_q_norm_g": v_b_q_norm_g, "b_k_norm_g": v_b_k_norm_g, "b_w_out": v_b_w_out}
    names = list(weights)

    big = ("a_w_in", "a_w_out", "b_w_in", "b_w_out")
    delta_w, new_m, new_v = {}, {}, {}
    for nm in big:
        shp = weights[nm].shape
        two = lambda a: a.reshape(shp[-2], shp[-1])
        dl, m2, v2 = _adamw(two(weights[nm]), two(grads[nm]), two(m_in[nm]), two(v_in[nm]), f"adamw_{nm}")
        delta_w[nm], new_m[nm], new_v[nm] = dl.reshape(shp), m2.reshape(shp), v2.reshape(shp)
    small_names = [nm for nm in names if nm not in big]
    packs = [_pack_rows([src[nm] for nm in small_names]) for src in (weights, grads, m_in, v_in)]
    offs = packs[0][1]
    dl, m2, v2 = _adamw(packs[0][0], packs[1][0], packs[2][0], packs[3][0], "adamw_small")
    for nm, a, b, c2 in zip(small_names, _unpack_rows(dl, offs), _unpack_rows(m2, offs), _unpack_rows(v2, offs)):
        delta_w[nm], new_m[nm], new_v[nm] = a, b, c2

    return (loss, d_x0, *[grads[nm] for nm in names], *[delta_w[nm] for nm in names],
            *[new_m[nm] for nm in names], *[new_v[nm] for nm in names])
```

```python
import functools
import math

import jax
import jax.numpy as jnp
import numpy as np
from jax import lax
from jax.experimental import pallas as pl
from jax.experimental.pallas import tpu as pltpu

F32 = jnp.float32
BF16 = jnp.bfloat16
MESH = pl.DeviceIdType.MESH

EPS = 1e-6
D_MODEL = 1024
A_HEADS = 8
A_DK = 128
A_DV = 256
A_QK = A_HEADS * A_DK
A_VW = A_HEADS * A_DV
A_MAIN = 2 * A_QK + 2 * A_VW
A_HEAD_COLS = 2 * A_DK + 2 * A_DV
A_CONV_COLS = 2 * A_DK + A_DV
A_CHUNK = 64
A_CONV = 4
B_GROUPS = 3
B_HEADS = 8
B_DH = 128
B_W = B_HEADS * B_DH
B_DIL = (1, 4, 16)
B_BLK = 128
ROPE_THETA = 500000.0
ROPE_DIMS = B_DH // 4
ADAM_LR, ADAM_B1, ADAM_B2, ADAM_EPS, ADAM_WD, ADAM_STEP = 0.001, 0.9, 0.999, 1e-08, 0.01, 10
N_CHIPS = 4
VMEM_BIG = 56 * 1024 * 1024


def _params(sem=None, vmem=None):
    return pltpu.CompilerParams(dimension_semantics=sem, vmem_limit_bytes=vmem)


def _dot(a, b, ca, cb):
    return lax.dot_general(a.astype(BF16), b.astype(BF16), (((ca,), (cb,)), ((), ())),
                           preferred_element_type=F32)


def _split3(a):
    hi = a.astype(BF16)
    r = a - hi.astype(F32)
    mid = r.astype(BF16)
    lo = (r - mid.astype(F32)).astype(BF16)
    return hi, mid, lo


def _dot_hi(a, b, ca, cb):
    a_hi, a_lo, _ = _split3(a)
    b_hi, b_lo, _ = _split3(b)
    dn = (((ca,), (cb,)), ((), ()))
    out = lax.dot_general(a_hi, b_hi, dn, preferred_element_type=F32)
    out = out + lax.dot_general(a_hi, b_lo, dn, preferred_element_type=F32)
    return out + lax.dot_general(a_lo, b_hi, dn, preferred_element_type=F32)


def _sigmoid(y):
    return 1.0 / (1.0 + jnp.exp(-y))


def _silu(y):
    return y * _sigmoid(y)


def _dsilu(y):
    s = _sigmoid(y)
    return s * (1.0 + y * (1.0 - s))


def _matmul(a, b, mode, out_dtype, name, res=None, tm=1024, tn=1024, tk=1024):
    if mode == "nn":
        (m, k), (_, n) = a.shape, b.shape
    else:
        (m, k), (n, _) = a.shape, b.shape
    tm, tn, tk = min(tm, m), min(tn, n), min(tk, k)
    assert m % tm == 0 and n % tn == 0 and k % tk == 0, (name, a.shape, b.shape)
    nk = k // tk
    dims = {"nn": ((1,), (0,)), "nt": ((1,), (1,))}[mode]

    def body(*refs):
        a_ref, b_ref = refs[0], refs[1]
        r_ref = refs[2] if res is not None else None
        o_ref = refs[3] if res is not None else refs[2]
        prod = lax.dot_general(a_ref[...], b_ref[...], (dims, ((), ())), preferred_element_type=F32)

        def finish(r):
            if res is not None:
                r = r + r_ref[...]
            o_ref[...] = r.astype(out_dtype)

        if nk == 1:
            finish(prod)
            return
        acc = refs[-1]
        kk = pl.program_id(2)

        @pl.when(kk == 0)
        def _():
            acc[...] = prod

        @pl.when((kk > 0) & (kk < nk - 1))
        def _():
            acc[...] += prod

        @pl.when(kk == nk - 1)
        def _():
            finish(acc[...] + prod)

    a_spec = pl.BlockSpec((tm, tk), lambda i, j, kk: (i, kk))
    if mode == "nt":
        b_spec = pl.BlockSpec((tn, tk), lambda i, j, kk: (j, kk))
    else:
        b_spec = pl.BlockSpec((tk, tn), lambda i, j, kk: (kk, j))
    in_specs = [a_spec, b_spec]
    args = [a, b]
    if res is not None:
        in_specs.append(pl.BlockSpec((tm, tn), lambda i, j, kk: (i, j)))
        args.append(res)
    return pl.pallas_call(
        body, name=name, grid=(m // tm, n // tn, nk),
        in_specs=in_specs, out_specs=pl.BlockSpec((tm, tn), lambda i, j, kk: (i, j)),
        out_shape=jax.ShapeDtypeStruct((m, n), out_dtype),
        scratch_shapes=[pltpu.VMEM((tm, tn), F32)] if nk > 1 else [],
        compiler_params=_params(("parallel", "parallel", "arbitrary"), 48 * 1024 * 1024),
    )(*args)


def _rms_fwd(x, g, name, tm=256):
    t, d = x.shape

    def body(x_ref, g_ref, h_ref):
        xv = x_ref[...]
        r = lax.rsqrt(jnp.mean(xv * xv, axis=-1, keepdims=True) + EPS)
        h_ref[...] = (xv * r * g_ref[...]).astype(BF16)

    return pl.pallas_call(
        body, name=name, grid=(t // tm,),
        in_specs=[pl.BlockSpec((tm, d), lambda i: (i, 0)), pl.BlockSpec((1, d), lambda i: (0, 0))],
        out_specs=pl.BlockSpec((tm, d), lambda i: (i, 0)),
        out_shape=jax.ShapeDtypeStruct((t, d), BF16),
        compiler_params=_params(("parallel",)),
    )(x, g)


def _rms_bwd(x, g, dhs, dres, name, tm=256):
    t, d = x.shape
    n_dh = len(dhs)

    def body(*refs):
        x_ref, g_ref = refs[0], refs[1]
        dh_refs = refs[2:2 + n_dh]
        dres_ref, dx_ref, dg_ref = refs[2 + n_dh:]
        i = pl.program_id(0)

        @pl.when(i == 0)
        def _():
            dg_ref[...] = jnp.zeros_like(dg_ref)

        xv = x_ref[...]
        r = lax.rsqrt(jnp.mean(xv * xv, axis=-1, keepdims=True) + EPS)
        xh = xv * r
        dh = dh_refs[0][...]
        for ref in dh_refs[1:]:
            dh = dh + ref[...]
        dg_ref[0:1, :] += jnp.sum(dh * xh, axis=0, keepdims=True)
        dxh = dh * g_ref[...]
        dx = r * (dxh - xh * jnp.mean(dxh * xh, axis=-1, keepdims=True))
        dx_ref[...] = dx + dres_ref[...]

    row = pl.BlockSpec((tm, d), lambda i: (i, 0))
    dx, dg = pl.pallas_call(
        body, name=name, grid=(t // tm,),
        in_specs=[row, pl.BlockSpec((1, d), lambda i: (0, 0))] + [row] * n_dh + [row],
        out_specs=[row, pl.BlockSpec((8, d), lambda i: (0, 0))],
        out_shape=[jax.ShapeDtypeStruct((t, d), F32), jax.ShapeDtypeStruct((8, d), F32)],
        compiler_params=_params(("arbitrary",)),
    )(x, g, *dhs, dres)
    return dx, dg[0:1]


def _loss_grad(y, target, name="loss_grad", tm=256):
    t, d = y.shape
    nb = t // tm

    def body(y_ref, t_ref, dy_ref, part_ref):
        e = y_ref[...] - t_ref[...]
        dy_ref[...] = e * (1.0 / d)
        s = jnp.sum(jnp.sum(e * e, axis=1, keepdims=True), axis=0, keepdims=True) * (0.5 / d)
        part_ref[...] = jnp.broadcast_to(s, (8, 128))

    row = pl.BlockSpec((tm, d), lambda i: (i, 0))
    dy, part = pl.pallas_call(
        body, name=name, grid=(nb,), in_specs=[row, row],
        out_specs=[row, pl.BlockSpec((None, 8, 128), lambda i: (i, 0, 0))],
        out_shape=[jax.ShapeDtypeStruct((t, d), F32), jax.ShapeDtypeStruct((nb, 8, 128), F32)],
        compiler_params=_params(("parallel",)),
    )(y, target)
    return dy, part[:, 0, 0]


def _softplus(x):
    t = jnp.exp(-jnp.abs(x))
    return jnp.maximum(x, 0.0) + jnp.where(t < 1e-3, t * (1.0 - 0.5 * t), jnp.log(1.0 + t))


def _tri(rows_le_cols):
    r = lax.broadcasted_iota(jnp.int32, (A_CHUNK, A_CHUNK), 0)
    c = lax.broadcasted_iota(jnp.int32, (A_CHUNK, A_CHUNK), 1)
    return jnp.where((r <= c) if rows_le_cols else (r >= c), 1.0, 0.0).astype(BF16)


def _dot_exact_rhs(a, ones_bf16):
    dn = (((1,), (0,)), ((), ()))
    hi, mid, lo = _split3(a)
    out = lax.dot_general(hi, ones_bf16, dn, preferred_element_type=F32)
    out = out + lax.dot_general(mid, ones_bf16, dn, preferred_element_type=F32)
    return out + lax.dot_general(lo, ones_bf16, dn, preferred_element_type=F32)


def _gdn_prep(tail_t, a_log, dt_bias):
    bn, _, n, c = tail_t.shape

    def body(t_ref, alog_ref, dtb_ref, beta_ref, gc_ref):
        upper = _tri(True)
        for h in range(A_HEADS):
            beta_ref[h] = _sigmoid(t_ref[h])
            ea = jnp.exp(jnp.full((n, c), alog_ref[h], F32))
            g = -ea * _softplus(t_ref[A_HEADS + h] + dtb_ref[h])
            gc_ref[h] = _dot_exact_rhs(g, upper)

    smem = pl.BlockSpec(memory_space=pltpu.SMEM)
    blk = pl.BlockSpec((None, A_HEADS, n, c), lambda b: (b, 0, 0, 0))
    return pl.pallas_call(
        body, name="gdn_prep", grid=(bn,),
        in_specs=[pl.BlockSpec((None, 2 * A_HEADS, n, c), lambda b: (b, 0, 0, 0)), smem, smem],
        out_specs=[blk, blk],
        out_shape=[jax.ShapeDtypeStruct((bn, A_HEADS, n, c), F32)] * 2,
        compiler_params=_params(("parallel",)),
    )(tail_t, a_log, dt_bias)


def _gdn_prep_bwd(tail_t, a_log, dt_bias, d_gc, d_beta):
    bn, _, n, c = tail_t.shape

    def body(t_ref, alog_ref, dtb_ref, dgc_ref, dbeta_ref, dt_ref, dal_ref, ddt_ref):
        lower = _tri(False)
        for h in range(A_HEADS):
            beta = _sigmoid(t_ref[h])
            dt_ref[h] = dbeta_ref[h] * beta * (1.0 - beta)
            dg = _dot_exact_rhs(dgc_ref[h], lower)
            ea = jnp.exp(jnp.full((n, c), alog_ref[h], F32))
            xa = t_ref[A_HEADS + h] + dtb_ref[h]
            g = -ea * _softplus(xa)
            dxa = -ea * dg * _sigmoid(xa)
            dt_ref[A_HEADS + h] = dxa
            s1 = jnp.sum(jnp.sum(g * dg, axis=1, keepdims=True), axis=0, keepdims=True)
            s2 = jnp.sum(jnp.sum(dxa, axis=1, keepdims=True), axis=0, keepdims=True)
            dal_ref[h:h + 1, :] = jnp.broadcast_to(s1, (1, 128))
            ddt_ref[h:h + 1, :] = jnp.broadcast_to(s2, (1, 128))

    smem = pl.BlockSpec(memory_space=pltpu.SMEM)
    blk8 = pl.BlockSpec((None, A_HEADS, n, c), lambda b: (b, 0, 0, 0))
    blk16 = pl.BlockSpec((None, 2 * A_HEADS, n, c), lambda b: (b, 0, 0, 0))
    sm = pl.BlockSpec((None, A_HEADS, 128), lambda b: (b, 0, 0))
    return pl.pallas_call(
        body, name="gdn_prep_bwd", grid=(bn,),
        in_specs=[blk16, smem, smem, blk8, blk8],
        out_specs=[blk16, sm, sm],
        out_shape=[jax.ShapeDtypeStruct((bn, 2 * A_HEADS, n, c), F32),
                   jax.ShapeDtypeStruct((bn, A_HEADS, 128), F32),
                   jax.ShapeDtypeStruct((bn, A_HEADS, 128), F32)],
        compiler_params=_params(("parallel",)),
    )(tail_t, a_log, dt_bias, d_gc, d_beta)


HALO = 8


def _conv_window(x_ref, n, first, lo, width):
    if first:
        return jnp.concatenate([jnp.zeros((HALO, width), F32), x_ref[0:A_CHUNK, lo:lo + width]], axis=0)
    start = pl.multiple_of(n * A_CHUNK - HALO, HALO)
    return x_ref[pl.ds(start, A_CHUNK + HALO), lo:lo + width]


def _conv_taps(xw, w):
    y = w[A_CONV - 1:A_CONV, :] * xw
    for j in range(1, A_CONV):
        y = y + w[A_CONV - 1 - j:A_CONV - j, :] * pltpu.roll(xw, j, 0)
    return y[HALO:, :]


def _row_to_col(row, eye):
    c = eye.shape[0]
    return jnp.sum(jnp.where(eye, jnp.broadcast_to(row, (c, c)), 0.0), axis=1, keepdims=True)


def _col_to_row(col, eye):
    c = eye.shape[0]
    return jnp.sum(jnp.where(eye, jnp.broadcast_to(col, (c, c)), 0.0), axis=0, keepdims=True)


def _unit_lower_inverse(a, ri, ci):
    eye = jnp.where(ri == ci, 1.0, 0.0)
    a8 = jnp.where((ri >> 3) == (ci >> 3), a, 0.0)
    a2 = _dot_hi(a8, a8, 1, 0)
    yield
    a4 = _dot_hi(a2, a2, 1, 0)
    t = eye - a8
    t = t + _dot_hi(t, a2, 1, 0)
    yield
    t = t + _dot_hi(t, a4, 1, 0)
    yield
    for sh in (3, 4, 5):
        off = jnp.where(((ri >> (sh + 1)) == (ci >> (sh + 1))) & ((ri >> sh) != (ci >> sh)), a, 0.0)
        left = _dot_hi(t, off, 1, 0)
        yield
        t = t - _dot_hi(left, t, 1, 0)
        yield
    return t


def _round_robin(gens):
    live = list(gens)
    while live:
        nxt = []
        for g in live:
            try:
                next(g)
                nxt.append(g)
            except StopIteration:
                pass
        live = nxt


def _gdn_chunk_inputs(x_ref, cw, n, first):
    xw = _conv_window(x_ref, n, first, 0, A_CONV_COLS)
    y = _conv_taps(xw, cw)
    a = _silu(y)
    aq, ak, v = a[:, 0:A_DK], a[:, A_DK:2 * A_DK], a[:, 2 * A_DK:]
    rq = lax.rsqrt(jnp.sum(aq * aq, axis=1, keepdims=True) + EPS)
    rk = lax.rsqrt(jnp.sum(ak * ak, axis=1, keepdims=True) + EPS)
    return dict(xw=xw, y=y, aq=aq, ak=ak, rq=rq, rk=rk,
                q=aq * rq * (A_DK ** -0.5), k=ak * rk, v=v)


def _gdn_chunk_core(q, k, v, g_row, b_row, t_mat, ri, ci):
    eye = ri == ci
    g_col = _row_to_col(g_row, eye)
    b_col = _row_to_col(b_row, eye)
    causal = ri >= ci
    strict = ri > ci
    dec = jnp.where(causal, jnp.exp(jnp.where(causal, g_col - g_row, 0.0)), 0.0)
    gam = jnp.exp(g_col)
    g_last = g_row[:, A_CHUNK - 1:A_CHUNK]
    gam_last = jnp.exp(g_last)
    e = jnp.exp(g_last - g_col)
    kb = k * b_col
    bv = v * b_col
    kbg = kb * gam
    kk = _dot(kb, k, 1, 1)
    p = _dot(q, k, 1, 1) * dec
    yield
    a_mat = jnp.where(strict, kk * dec, 0.0)
    if t_mat is None:
        t_mat = yield from _unit_lower_inverse(a_mat, ri, ci)
    u = _dot(t_mat, bv, 1, 0)
    w = _dot(t_mat, kbg, 1, 0)
    yield
    return dict(eye=eye, g_col=g_col, b_col=b_col, dec=dec, strict=strict, causal=causal, gam=gam,
                gam_last=gam_last, e=e, kb=kb, bv=bv, kbg=kbg, a_mat=a_mat, t_mat=t_mat, u=u, w=w, p=p,
                qg=q * gam, kd=k * e)


def _gdn_fwd(proj_hm, cw_hm, beta, gc, norm_g):
    bn, s, _ = proj_hm.shape
    n = s // A_CHUNK

    def body(x_ref, cw_ref, beta_ref, gc_ref, ng_ref, og_ref, oraw_ref, st_ref, t_ref, state):
        ri = lax.broadcasted_iota(jnp.int32, (A_CHUNK, A_CHUNK), 0)
        ci = lax.broadcasted_iota(jnp.int32, (A_CHUNK, A_CHUNK), 1)
        cw = cw_ref[...]
        ng = ng_ref[...]
        state[...] = jnp.zeros_like(state)

        def chunk(i, first):
            rows = pl.ds(0 if first else pl.multiple_of(i * A_CHUNK, A_CHUNK), A_CHUNK)
            cin = _gdn_chunk_inputs(x_ref, cw, i, first)
            core = _gdn_chunk_core(cin["q"], cin["k"], cin["v"], gc_ref[pl.ds(i, 1), :],
                                   beta_ref[pl.ds(i, 1), :], None, ri, ci)
            st = state[...]
            st_ref[i] = st
            t_ref[i] = core["t_mat"]
            vn = core["u"] - _dot(core["w"], st, 1, 0)
            o = _dot(core["qg"], st, 1, 0) + _dot(core["p"], vn, 1, 0)
            state[...] = st * core["gam_last"] + _dot(core["kd"], vn, 0, 0)
            oraw_ref[rows, :] = o
            r = lax.rsqrt(jnp.mean(o * o, axis=1, keepdims=True) + EPS)
            z = x_ref[rows, A_CONV_COLS:A_HEAD_COLS]
            og_ref[rows, :] = (o * r * ng * _silu(z)).astype(BF16)

        chunk(0, True)
        lax.fori_loop(1, n, lambda i, c: (chunk(i, False), c)[1], 0)

    return pl.pallas_call(
        body, name="gdn_fwd", grid=(bn, A_HEADS),
        in_specs=[pl.BlockSpec((None, s, A_HEAD_COLS), lambda b, h: (b, 0, h)),
                  pl.BlockSpec((A_CONV, A_CONV_COLS), lambda b, h: (0, h)),
                  pl.BlockSpec((None, None, n, A_CHUNK), lambda b, h: (b, h, 0, 0)),
                  pl.BlockSpec((None, None, n, A_CHUNK), lambda b, h: (b, h, 0, 0)),
                  pl.BlockSpec((1, A_DV), lambda b, h: (0, 0))],
        out_specs=[pl.BlockSpec((None, s, A_DV), lambda b, h: (b, 0, h)),
                   pl.BlockSpec((None, s, A_DV), lambda b, h: (b, 0, h)),
                   pl.BlockSpec((None, None, n, A_DK, A_DV), lambda b, h: (b, h, 0, 0, 0)),
                   pl.BlockSpec((None, None, n, A_CHUNK, A_CHUNK), lambda b, h: (b, h, 0, 0, 0))],
        out_shape=[jax.ShapeDtypeStruct((bn, s, A_VW), BF16),
                   jax.ShapeDtypeStruct((bn, s, A_VW), F32),
                   jax.ShapeDtypeStruct((bn, A_HEADS, n, A_DK, A_DV), F32),
                   jax.ShapeDtypeStruct((bn, A_HEADS, n, A_CHUNK, A_CHUNK), F32)],
        scratch_shapes=[pltpu.VMEM((A_DK, A_DV), F32)],
        compiler_params=_params(("parallel", "parallel"), VMEM_BIG),
    )(proj_hm, cw_hm, beta, gc, norm_g)


def _gdn_bwd(proj_hm, cw_hm, beta, gc, norm_g, oraw, states, t_mats, dog):
    bn, s, _ = proj_hm.shape
    n = s // A_CHUNK

    def body(x_ref, cw_ref, beta_ref, gc_ref, ng_ref, oraw_ref, st_ref, t_ref, dog_ref,
             dx_ref, dgc_ref, dbeta_ref, dcw_ref, dng_ref, dstate, dy_next):
        ri = lax.broadcasted_iota(jnp.int32, (A_CHUNK, A_CHUNK), 0)
        ci = lax.broadcasted_iota(jnp.int32, (A_CHUNK, A_CHUNK), 1)
        cw = cw_ref[...]
        ng = ng_ref[...]
        dstate[...] = jnp.zeros_like(dstate)
        dy_next[...] = jnp.zeros_like(dy_next)
        dcw_ref[...] = jnp.zeros_like(dcw_ref)
        dng_ref[...] = jnp.zeros_like(dng_ref)

        def chunk(i, first):
            rows = pl.ds(0 if first else pl.multiple_of(i * A_CHUNK, A_CHUNK), A_CHUNK)
            cin = _gdn_chunk_inputs(x_ref, cw, i, first)
            q, k, v = cin["q"], cin["k"], cin["v"]
            g_row = gc_ref[pl.ds(i, 1), :]
            b_row = beta_ref[pl.ds(i, 1), :]
            cr = _gdn_chunk_core(q, k, v, g_row, b_row, t_ref[i], ri, ci)
            eye, dec, gam, e = cr["eye"], cr["dec"], cr["gam"], cr["e"]
            b_col, t_mat, u, w, p = cr["b_col"], cr["t_mat"], cr["u"], cr["w"], cr["p"]
            st = st_ref[i]
            ds_out = dstate[...]

            o = oraw_ref[rows, :]
            z = x_ref[rows, A_CONV_COLS:A_HEAD_COLS]
            d_og = dog_ref[rows, :]
            r = lax.rsqrt(jnp.mean(o * o, axis=1, keepdims=True) + EPS)
            oh = o * r
            d_on = d_og * _silu(z)
            dz = d_og * oh * ng * _dsilu(z)
            dng_ref[0:1, :] += jnp.sum(d_on * oh, axis=0, keepdims=True)
            d_oh = d_on * ng
            d_o = r * (d_oh - oh * jnp.mean(d_oh * oh, axis=1, keepdims=True))

            vn = u - _dot(w, st, 1, 0)
            d_vn = _dot(p, d_o, 0, 0) + _dot(cr["kd"], ds_out, 1, 0)
            d_p = jnp.where(cr["causal"], _dot(d_o, vn, 1, 1), 0.0)
            d_qg = _dot(d_o, st, 1, 1)
            d_kd = _dot(vn, ds_out, 1, 1)
            d_gam_last = jnp.sum(jnp.sum(st * ds_out, axis=1, keepdims=True), axis=0, keepdims=True)
            d_w = -_dot(d_vn, st, 1, 1)
            dstate[...] = _dot(cr["qg"], d_o, 0, 0) + ds_out * cr["gam_last"] - _dot(w, d_vn, 0, 0)
            d_bv = _dot(t_mat, d_vn, 0, 0)
            d_kbg = _dot(t_mat, d_w, 0, 0)
            d_a = jnp.where(cr["strict"], -(_dot(d_bv, u, 1, 1) + _dot(d_kbg, w, 1, 1)), 0.0)
            m_a = d_a * dec
            n_p = d_p * dec
            d_kb = _dot(m_a, k, 1, 0) + d_kbg * gam
            d_q = _dot(n_p, k, 1, 0) + d_qg * gam
            d_k = (_dot(m_a, cr["kb"], 0, 0) + _dot(n_p, q, 0, 0) + d_kd * e + d_kb * b_col)
            d_v = d_bv * b_col
            d_beta_col = (jnp.sum(d_bv * v, axis=1, keepdims=True)
                          + jnp.sum(d_kb * k, axis=1, keepdims=True))
            gterm = d_a * cr["a_mat"] + d_p * p
            d_e = jnp.sum(d_kd * k, axis=1, keepdims=True) * e
            d_g_col = (jnp.sum(gterm, axis=1, keepdims=True)
                       + (jnp.sum(d_qg * q, axis=1, keepdims=True)
                          + jnp.sum(d_kbg * cr["kb"], axis=1, keepdims=True)) * gam
                       - d_e)
            d_g_last = jnp.sum(d_e, axis=0, keepdims=True) + d_gam_last * cr["gam_last"]
            lane = lax.broadcasted_iota(jnp.int32, (1, A_CHUNK), 1)
            d_g_row = (_col_to_row(d_g_col, eye) - jnp.sum(gterm, axis=0, keepdims=True)
                       + jnp.where(lane == A_CHUNK - 1, d_g_last, 0.0))
            dgc_ref[pl.ds(i, 1), :] = d_g_row
            dbeta_ref[pl.ds(i, 1), :] = _col_to_row(d_beta_col, eye)

            qh = cin["aq"] * cin["rq"]
            kh = cin["ak"] * cin["rk"]
            d_qh = d_q * (A_DK ** -0.5)
            d_aq = cin["rq"] * (d_qh - qh * jnp.sum(d_qh * qh, axis=1, keepdims=True))
            d_ak = cin["rk"] * (d_k - kh * jnp.sum(d_k * kh, axis=1, keepdims=True))
            d_y = jnp.concatenate([d_aq, d_ak, d_v], axis=1) * _dsilu(cin["y"])
            xw = cin["xw"]
            dyw = jnp.concatenate([d_y, dy_next[...]], axis=0)
            d_x = cw[A_CONV - 1:A_CONV, :] * dyw
            for j in range(1, A_CONV):
                d_x = d_x + cw[A_CONV - 1 - j:A_CONV - j, :] * pltpu.roll(dyw, A_CHUNK + HALO - j, 0)
            d_x = d_x[0:A_CHUNK, :]
            dy_pad = jnp.concatenate([jnp.zeros((HALO, A_CONV_COLS), F32), d_y], axis=0)
            for j in range(A_CONV):
                xs = xw if j == 0 else pltpu.roll(xw, j, 0)
                dcw_ref[A_CONV - 1 - j:A_CONV - j, :] += jnp.sum(dy_pad * xs, axis=0, keepdims=True)
            dy_next[...] = d_y[0:HALO, :]
            dx_ref[rows, 0:A_CONV_COLS] = d_x.astype(BF16)
            dx_ref[rows, A_CONV_COLS:A_HEAD_COLS] = dz.astype(BF16)

        lax.fori_loop(0, n - 1, lambda i, c: (chunk(n - 1 - i, False), c)[1], 0)
        chunk(0, True)

    hn = lambda b, h: (b, h, 0, 0)
    return pl.pallas_call(
        body, name="gdn_bwd", grid=(bn, A_HEADS),
        in_specs=[pl.BlockSpec((None, s, A_HEAD_COLS), lambda b, h: (b, 0, h)),
                  pl.BlockSpec((A_CONV, A_CONV_COLS), lambda b, h: (0, h)),
                  pl.BlockSpec((None, None, n, A_CHUNK), hn),
                  pl.BlockSpec((None, None, n, A_CHUNK), hn),
                  pl.BlockSpec((1, A_DV), lambda b, h: (0, 0)),
                  pl.BlockSpec((None, s, A_DV), lambda b, h: (b, 0, h)),
                  pl.BlockSpec((None, None, n, A_DK, A_DV), lambda b, h: (b, h, 0, 0, 0)),
                  pl.BlockSpec((None, None, n, A_CHUNK, A_CHUNK), lambda b, h: (b, h, 0, 0, 0)),
                  pl.BlockSpec((None, s, A_DV), lambda b, h: (b, 0, h))],
        out_specs=[pl.BlockSpec((None, s, A_HEAD_COLS), lambda b, h: (b, 0, h)),
                   pl.BlockSpec((None, None, n, A_CHUNK), hn),
                   pl.BlockSpec((None, None, n, A_CHUNK), hn),
                   pl.BlockSpec((None, A_CONV, A_CONV_COLS), lambda b, h: (b, 0, h)),
                   pl.BlockSpec((None, None, 8, A_DV), hn)],
        out_shape=[jax.ShapeDtypeStruct((bn, s, A_HEADS * A_HEAD_COLS), BF16),
                   jax.ShapeDtypeStruct((bn, A_HEADS, n, A_CHUNK), F32),
                   jax.ShapeDtypeStruct((bn, A_HEADS, n, A_CHUNK), F32),
                   jax.ShapeDtypeStruct((bn, A_CONV, A_HEADS * A_CONV_COLS), F32),
                   jax.ShapeDtypeStruct((bn, A_HEADS, 8, A_DV), F32)],
        scratch_shapes=[pltpu.VMEM((A_DK, A_DV), F32), pltpu.VMEM((HALO, A_CONV_COLS), F32)],
        compiler_params=_params(("parallel", "parallel"), VMEM_BIG),
    )(proj_hm, cw_hm, beta, gc, norm_g, oraw, states, t_mats, dog)


A_SEQ_BLK = 512
A_BLK_CHUNKS = A_SEQ_BLK // A_CHUNK


def _gdn_halo(proj_hm):
    bn, s, w = proj_hm.shape
    last = proj_hm.reshape(bn, s // A_SEQ_BLK, A_SEQ_BLK, w)[:, :, A_SEQ_BLK - HALO:, :]
    return jnp.concatenate([jnp.zeros((bn, 1, HALO, w), proj_hm.dtype), last[:, :-1]], axis=1)


def _gdn_window(x_ref, halo_ref, ci, first, lo):
    if first:
        return jnp.concatenate([halo_ref[:, lo:lo + A_CONV_COLS], x_ref[0:A_CHUNK, lo:lo + A_CONV_COLS]], axis=0)
    start = pl.multiple_of(ci * A_CHUNK - HALO, HALO)
    return x_ref[pl.ds(start, A_CHUNK + HALO), lo:lo + A_CONV_COLS]


def _gdn_chunk_prep(xw, cw):
    y = _conv_taps(xw, cw)
    a = _silu(y)
    aq, ak, v = a[:, 0:A_DK], a[:, A_DK:2 * A_DK], a[:, 2 * A_DK:]
    rq = lax.rsqrt(jnp.sum(aq * aq, axis=1, keepdims=True) + EPS)
    rk = lax.rsqrt(jnp.sum(ak * ak, axis=1, keepdims=True) + EPS)
    return dict(xw=xw, y=y, aq=aq, ak=ak, rq=rq, rk=rk, q=aq * rq * (A_DK ** -0.5), k=ak * rk, v=v)


def _gdn_fwd(proj_hm, cw_hm, beta, gc, norm_g, hp=4):
    bn, s, _ = proj_hm.shape
    n = s // A_CHUNK
    nsb = s // A_SEQ_BLK
    halo = _gdn_halo(proj_hm)

    def body(x_ref, halo_ref, cw_ref, beta_ref, gc_ref, ng_ref, og_ref, oraw_ref, st_ref, t_ref, state):
        ri = lax.broadcasted_iota(jnp.int32, (A_CHUNK, A_CHUNK), 0)
        ci_ = lax.broadcasted_iota(jnp.int32, (A_CHUNK, A_CHUNK), 1)
        ng = ng_ref[...]

        @pl.when(pl.program_id(2) == 0)
        def _():
            state[...] = jnp.zeros_like(state)

        def one_head(hh, ci, first, rows):
            lo = hh * A_HEAD_COLS
            cw = cw_ref[:, hh * A_CONV_COLS:(hh + 1) * A_CONV_COLS]
            cin = _gdn_chunk_prep(_gdn_window(x_ref, halo_ref, ci, first, lo), cw)
            core = yield from _gdn_chunk_core(cin["q"], cin["k"], cin["v"], gc_ref[hh, pl.ds(ci, 1), :],
                                              beta_ref[hh, pl.ds(ci, 1), :], None, ri, ci_)
            st = state[hh]
            st_ref[hh, ci] = st
            t_ref[hh, ci] = core["t_mat"]
            vn = core["u"] - _dot(core["w"], st, 1, 0)
            qs = _dot(core["qg"], st, 1, 0)
            yield
            o = qs + _dot(core["p"], vn, 1, 0)
            state[hh] = st * core["gam_last"] + _dot(core["kd"], vn, 0, 0)
            yield
            ocols = slice(hh * A_DV, (hh + 1) * A_DV)
            oraw_ref[rows, ocols] = o
            r = lax.rsqrt(jnp.mean(o * o, axis=1, keepdims=True) + EPS)
            z = x_ref[rows, lo + A_CONV_COLS:lo + A_HEAD_COLS]
            og_ref[rows, ocols] = (o * r * ng * _silu(z)).astype(BF16)

        def chunk(ci, first):
            rows = pl.ds(0 if first else pl.multiple_of(ci * A_CHUNK, A_CHUNK), A_CHUNK)
            _round_robin([one_head(hh, ci, first, rows) for hh in range(hp)])

        chunk(0, True)
        lax.fori_loop(1, A_BLK_CHUNKS, lambda i, c: (chunk(i, False), c)[1], 0)

    small = pl.BlockSpec((None, hp, A_BLK_CHUNKS, A_CHUNK), lambda b, h, j: (b, h, j, 0))
    return pl.pallas_call(
        body, name="gdn_fwd", grid=(bn, A_HEADS // hp, nsb),
        in_specs=[pl.BlockSpec((None, A_SEQ_BLK, hp * A_HEAD_COLS), lambda b, h, j: (b, j, h)),
                  pl.BlockSpec((None, None, HALO, hp * A_HEAD_COLS), lambda b, h, j: (b, j, 0, h)),
                  pl.BlockSpec((A_CONV, hp * A_CONV_COLS), lambda b, h, j: (0, h)),
                  small, small,
                  pl.BlockSpec((1, A_DV), lambda b, h, j: (0, 0))],
        out_specs=[pl.BlockSpec((None, A_SEQ_BLK, hp * A_DV), lambda b, h, j: (b, j, h)),
                   pl.BlockSpec((None, A_SEQ_BLK, hp * A_DV), lambda b, h, j: (b, j, h)),
                   pl.BlockSpec((None, hp, A_BLK_CHUNKS, A_DK, A_DV), lambda b, h, j: (b, h, j, 0, 0)),
                   pl.BlockSpec((None, hp, A_BLK_CHUNKS, A_CHUNK, A_CHUNK), lambda b, h, j: (b, h, j, 0, 0))],
        out_shape=[jax.ShapeDtypeStruct((bn, s, A_VW), BF16),
                   jax.ShapeDtypeStruct((bn, s, A_VW), F32),
                   jax.ShapeDtypeStruct((bn, A_HEADS, n, A_DK, A_DV), F32),
                   jax.ShapeDtypeStruct((bn, A_HEADS, n, A_CHUNK, A_CHUNK), F32)],
        scratch_shapes=[pltpu.VMEM((hp, A_DK, A_DV), F32)],
        compiler_params=_params(("parallel", "parallel", "arbitrary"), VMEM_BIG),
    )(proj_hm, halo, cw_hm, beta, gc, norm_g)


def _gdn_bwd(proj_hm, cw_hm, beta, gc, norm_g, oraw, states, t_mats, dog, hp=4):
    bn, s, _ = proj_hm.shape
    n = s // A_CHUNK
    nsb = s // A_SEQ_BLK
    halo = _gdn_halo(proj_hm)

    def body(x_ref, halo_ref, cw_ref, beta_ref, gc_ref, ng_ref, oraw_ref, st_ref, t_ref, dog_ref,
             dx_ref, dgc_ref, dbeta_ref, dcw_ref, dng_ref, dstate, dy_next):
        ri = lax.broadcasted_iota(jnp.int32, (A_CHUNK, A_CHUNK), 0)
        ci_ = lax.broadcasted_iota(jnp.int32, (A_CHUNK, A_CHUNK), 1)
        lane = lax.broadcasted_iota(jnp.int32, (1, A_CHUNK), 1)
        ng = ng_ref[...]

        @pl.when(pl.program_id(2) == 0)
        def _():
            dstate[...] = jnp.zeros_like(dstate)
            dy_next[...] = jnp.zeros_like(dy_next)
            dcw_ref[...] = jnp.zeros_like(dcw_ref)
            dng_ref[...] = jnp.zeros_like(dng_ref)

        def one_head(hh, ci, first, rows):
            lo = hh * A_HEAD_COLS
            ccols = slice(hh * A_CONV_COLS, (hh + 1) * A_CONV_COLS)
            ocols = slice(hh * A_DV, (hh + 1) * A_DV)
            cw = cw_ref[:, ccols]
            cin = _gdn_chunk_prep(_gdn_window(x_ref, halo_ref, ci, first, lo), cw)
            q, k, v = cin["q"], cin["k"], cin["v"]
            cr = yield from _gdn_chunk_core(q, k, v, gc_ref[hh, pl.ds(ci, 1), :], beta_ref[hh, pl.ds(ci, 1), :],
                                            t_ref[hh, ci], ri, ci_)
            eye, dec, gam, e = cr["eye"], cr["dec"], cr["gam"], cr["e"]
            b_col, t_mat, u, w, p = cr["b_col"], cr["t_mat"], cr["u"], cr["w"], cr["p"]
            st = st_ref[hh, ci]
            ds_out = dstate[hh]

            o = oraw_ref[rows, ocols]
            z = x_ref[rows, lo + A_CONV_COLS:lo + A_HEAD_COLS]
            d_og = dog_ref[rows, ocols]
            r = lax.rsqrt(jnp.mean(o * o, axis=1, keepdims=True) + EPS)
            oh = o * r
            d_on = d_og * _silu(z)
            dz = d_og * oh * ng * _dsilu(z)
            dng_ref[hh, 0:1, :] += jnp.sum(d_on * oh, axis=0, keepdims=True)
            d_oh = d_on * ng
            d_o = r * (d_oh - oh * jnp.mean(d_oh * oh, axis=1, keepdims=True))

            vn = u - _dot(w, st, 1, 0)
            d_vn = _dot(p, d_o, 0, 0) + _dot(cr["kd"], ds_out, 1, 0)
            d_qg = _dot(d_o, st, 1, 1)
            qgdo = _dot(cr["qg"], d_o, 0, 0)
            yield
            d_p = jnp.where(cr["causal"], _dot(d_o, vn, 1, 1), 0.0)
            d_kd = _dot(vn, ds_out, 1, 1)
            d_gam_last = jnp.sum(jnp.sum(st * ds_out, axis=1, keepdims=True), axis=0, keepdims=True)
            d_w = -_dot(d_vn, st, 1, 1)
            dstate[hh] = qgdo + ds_out * cr["gam_last"] - _dot(w, d_vn, 0, 0)
            d_bv = _dot(t_mat, d_vn, 0, 0)
            yield
            d_kbg = _dot(t_mat, d_w, 0, 0)
            n_p = d_p * dec
            d_q = _dot(n_p, k, 1, 0) + d_qg * gam
            npq = _dot(n_p, q, 0, 0)
            yield
            d_a = jnp.where(cr["strict"], -(_dot(d_bv, u, 1, 1) + _dot(d_kbg, w, 1, 1)), 0.0)
            yield
            m_a = d_a * dec
            d_kb = _dot(m_a, k, 1, 0) + d_kbg * gam
            d_k = (_dot(m_a, cr["kb"], 0, 0) + npq + d_kd * e + d_kb * b_col)
            yield
            d_v = d_bv * b_col
            d_beta_col = (jnp.sum(d_bv * v, axis=1, keepdims=True)
                          + jnp.sum(d_kb * k, axis=1, keepdims=True))
            gterm = d_a * cr["a_mat"] + d_p * p
            d_e = jnp.sum(d_kd * k, axis=1, keepdims=True) * e
            d_g_col = (jnp.sum(gterm, axis=1, keepdims=True)
                       + (jnp.sum(d_qg * q, axis=1, keepdims=True)
                          + jnp.sum(d_kbg * cr["kb"], axis=1, keepdims=True)) * gam
                       - d_e)
            d_g_last = jnp.sum(d_e, axis=0, keepdims=True) + d_gam_last * cr["gam_last"]
            d_g_row = (_col_to_row(d_g_col, eye) - jnp.sum(gterm, axis=0, keepdims=True)
                       + jnp.where(lane == A_CHUNK - 1, d_g_last, 0.0))
            dgc_ref[hh, pl.ds(ci, 1), :] = d_g_row
            dbeta_ref[hh, pl.ds(ci, 1), :] = _col_to_row(d_beta_col, eye)

            qh = cin["aq"] * cin["rq"]
            kh = cin["ak"] * cin["rk"]
            d_qh = d_q * (A_DK ** -0.5)
            d_aq = cin["rq"] * (d_qh - qh * jnp.sum(d_qh * qh, axis=1, keepdims=True))
            d_ak = cin["rk"] * (d_k - kh * jnp.sum(d_k * kh, axis=1, keepdims=True))
            d_y = jnp.concatenate([d_aq, d_ak, d_v], axis=1) * _dsilu(cin["y"])
            xw = cin["xw"]
            dyw = jnp.concatenate([d_y, dy_next[hh]], axis=0)
            d_x = cw[A_CONV - 1:A_CONV, :] * dyw
            for j in range(1, A_CONV):
                d_x = d_x + cw[A_CONV - 1 - j:A_CONV - j, :] * pltpu.roll(dyw, A_CHUNK + HALO - j, 0)
            dy_pad = jnp.concatenate([jnp.zeros((HALO, A_CONV_COLS), F32), d_y], axis=0)
            for j in range(A_CONV):
                xs = xw if j == 0 else pltpu.roll(xw, j, 0)
                dcw_ref[A_CONV - 1 - j:A_CONV - j, ccols] += jnp.sum(dy_pad * xs, axis=0, keepdims=True)
            dy_next[hh] = d_y[0:HALO, :]
            dx_ref[rows, lo:lo + A_CONV_COLS] = d_x[0:A_CHUNK, :].astype(BF16)
            dx_ref[rows, lo + A_CONV_COLS:lo + A_HEAD_COLS] = dz.astype(BF16)

        def chunk(ci, first):
            rows = pl.ds(0 if first else pl.multiple_of(ci * A_CHUNK, A_CHUNK), A_CHUNK)
            _round_robin([one_head(hh, ci, first, rows) for hh in range(hp)])

        lax.fori_loop(0, A_BLK_CHUNKS - 1, lambda i, c: (chunk(A_BLK_CHUNKS - 1 - i, False), c)[1], 0)
        chunk(0, True)

    rev = lambda j: nsb - 1 - j
    small = pl.BlockSpec((None, hp, A_BLK_CHUNKS, A_CHUNK), lambda b, h, j: (b, h, rev(j), 0))
    wide = pl.BlockSpec((None, A_SEQ_BLK, hp * A_HEAD_COLS), lambda b, h, j: (b, rev(j), h))
    val = pl.BlockSpec((None, A_SEQ_BLK, hp * A_DV), lambda b, h, j: (b, rev(j), h))
    return pl.pallas_call(
        body, name="gdn_bwd", grid=(bn, A_HEADS // hp, nsb),
        in_specs=[wide,
                  pl.BlockSpec((None, None, HALO, hp * A_HEAD_COLS), lambda b, h, j: (b, rev(j), 0, h)),
                  pl.BlockSpec((A_CONV, hp * A_CONV_COLS), lambda b, h, j: (0, h)),
                  small, small,
                  pl.BlockSpec((1, A_DV), lambda b, h, j: (0, 0)),
                  val,
                  pl.BlockSpec((None, hp, A_BLK_CHUNKS, A_DK, A_DV), lambda b, h, j: (b, h, rev(j), 0, 0)),
                  pl.BlockSpec((None, hp, A_BLK_CHUNKS, A_CHUNK, A_CHUNK), lambda b, h, j: (b, h, rev(j), 0, 0)),
                  val],
        out_specs=[wide, small, small,
                   pl.BlockSpec((None, A_CONV, hp * A_CONV_COLS), lambda b, h, j: (b, 0, h)),
                   pl.BlockSpec((None, hp, 8, A_DV), lambda b, h, j: (b, h, 0, 0))],
        out_shape=[jax.ShapeDtypeStruct((bn, s, A_HEADS * A_HEAD_COLS), BF16),
                   jax.ShapeDtypeStruct((bn, A_HEADS, n, A_CHUNK), F32),
                   jax.ShapeDtypeStruct((bn, A_HEADS, n, A_CHUNK), F32),
                   jax.ShapeDtypeStruct((bn, A_CONV, A_HEADS * A_CONV_COLS), F32),
                   jax.ShapeDtypeStruct((bn, A_HEADS, 8, A_DV), F32)],
        scratch_shapes=[pltpu.VMEM((hp, A_DK, A_DV), F32), pltpu.VMEM((hp, HALO, A_CONV_COLS), F32)],
        compiler_params=_params(("parallel", "parallel", "arbitrary"), VMEM_BIG),
    )(proj_hm, halo, cw_hm, beta, gc, norm_g, oraw, states, t_mats, dog)


def _rope_tables(posf, inv_freq_row):
    t = posf.shape[0]
    tm = 512

    def body(p_ref, f_ref, c_ref, sa_ref, sb_ref):
        ang = p_ref[...] * f_ref[...]
        lane = lax.broadcasted_iota(jnp.int32, ang.shape, 1)
        half = ROPE_DIMS // 2
        c_ref[...] = jnp.where(lane < ROPE_DIMS, jnp.cos(ang), 1.0)
        sn = jnp.sin(ang)
        sa_ref[...] = jnp.where(lane < half, -sn, 0.0)
        sb_ref[...] = jnp.where((lane >= half) & (lane < ROPE_DIMS), sn, 0.0)

    row = pl.BlockSpec((tm, 128), lambda i: (i, 0))
    return pl.pallas_call(
        body, name="rope_tables", grid=(t // tm,),
        in_specs=[row, pl.BlockSpec((1, 128), lambda i: (0, 0))], out_specs=[row] * 3,
        out_shape=[jax.ShapeDtypeStruct((t, 128), F32)] * 3,
        compiler_params=_params(("parallel",)),
    )(posf, inv_freq_row)


def _rope(x, c, sa, sb):
    half = ROPE_DIMS // 2
    return x * c + pltpu.roll(x, 128 - half, 1) * sa + pltpu.roll(x, half, 1) * sb


def _rope_t(d, c, sa, sb):
    half = ROPE_DIMS // 2
    return d * c + pltpu.roll(d * sa, half, 1) + pltpu.roll(d * sb, 128 - half, 1)


def _qk_prep(proj, c, sa, sb, qg, kg, name, tm=256):
    t = proj.shape[0]
    wide = proj.shape[1]

    def body(x_ref, c_ref, sa_ref, sb_ref, qg_ref, kg_ref, o_ref):
        cc, s1, s2 = c_ref[...], sa_ref[...], sb_ref[...]
        for which, g_ref in ((0, qg_ref), (1, kg_ref)):
            g = g_ref[...]
            for h in range(B_HEADS):
                lo = which * B_W + h * B_DH
                xv = x_ref[:, lo:lo + B_DH]
                r = lax.rsqrt(jnp.mean(xv * xv, axis=1, keepdims=True) + EPS)
                o_ref[:, lo:lo + B_DH] = _rope(xv * r * g, cc, s1, s2).astype(BF16)
        o_ref[:, 2 * B_W:3 * B_W] = x_ref[:, 2 * B_W:3 * B_W].astype(BF16)

    tab = pl.BlockSpec((tm, 128), lambda i: (i, 0))
    gain = pl.BlockSpec((1, B_DH), lambda i: (0, 0))
    return pl.pallas_call(
        body, name=name, grid=(t // tm,),
        in_specs=[pl.BlockSpec((tm, wide), lambda i: (i, 0)), tab, tab, tab, gain, gain],
        out_specs=pl.BlockSpec((tm, 3 * B_W), lambda i: (i, 0)),
        out_shape=jax.ShapeDtypeStruct((t, 3 * B_W), BF16),
        compiler_params=_params(("parallel",), 40 * 1024 * 1024),
    )(proj, c, sa, sb, qg, kg)


def _qk_prep_bwd(proj, c, sa, sb, qg, kg, dq, dk, dv, dz, name, tm=256):
    t = proj.shape[0]
    wide = proj.shape[1]
    out_w = 3 * B_W + (B_W if dz is not None else 0)

    def body(*refs):
        x_ref, c_ref, sa_ref, sb_ref, qg_ref, kg_ref, dq_ref, dk_ref, dv_ref = refs[:9]
        if dz is not None:
            dz_ref, o_ref, dgain_ref = refs[9:]
        else:
            o_ref, dgain_ref = refs[9:]
        i = pl.program_id(0)

        @pl.when(i == 0)
        def _():
            dgain_ref[...] = jnp.zeros_like(dgain_ref)

        cc, s1, s2 = c_ref[...], sa_ref[...], sb_ref[...]
        for which, g_ref, d_ref in ((0, qg_ref, dq_ref), (1, kg_ref, dk_ref)):
            g = g_ref[...]
            acc = jnp.zeros((1, B_DH), F32)
            for h in range(B_HEADS):
                lo = which * B_W + h * B_DH
                xv = x_ref[:, lo:lo + B_DH]
                r = lax.rsqrt(jnp.mean(xv * xv, axis=1, keepdims=True) + EPS)
                xh = xv * r
                d_xn = _rope_t(d_ref[:, h * B_DH:(h + 1) * B_DH], cc, s1, s2)
                acc = acc + jnp.sum(d_xn * xh, axis=0, keepdims=True)
                d_xh = d_xn * g
                d_x = r * (d_xh - xh * jnp.mean(d_xh * xh, axis=1, keepdims=True))
                o_ref[:, lo:lo + B_DH] = d_x.astype(BF16)
            dgain_ref[which:which + 1, :] += acc
        o_ref[:, 2 * B_W:3 * B_W] = dv_ref[...].astype(BF16)
        if dz is not None:
            o_ref[:, 3 * B_W:4 * B_W] = dz_ref[...]

    tab = pl.BlockSpec((tm, 128), lambda i: (i, 0))
    gain = pl.BlockSpec((1, B_DH), lambda i: (0, 0))
    grad = pl.BlockSpec((tm, B_W), lambda i: (i, 0))
    in_specs = [pl.BlockSpec((tm, wide), lambda i: (i, 0)), tab, tab, tab, gain, gain, grad, grad, grad]
    args = [proj, c, sa, sb, qg, kg, dq, dk, dv]
    if dz is not None:
        in_specs.append(grad)
        args.append(dz)
    return pl.pallas_call(
        body, name=name, grid=(t // tm,), in_specs=in_specs,
        out_specs=[pl.BlockSpec((tm, out_w), lambda i: (i, 0)), pl.BlockSpec((8, B_DH), lambda i: (0, 0))],
        out_shape=[jax.ShapeDtypeStruct((t, out_w), BF16), jax.ShapeDtypeStruct((8, B_DH), F32)],
        compiler_params=_params(("arbitrary",), 40 * 1024 * 1024),
    )(*args)


def _attn_masks():
    qi = lax.broadcasted_iota(jnp.int32, (B_BLK, 2 * B_BLK), 0)
    kj = lax.broadcasted_iota(jnp.int32, (B_BLK, 2 * B_BLK), 1)
    two = (kj >= qi) & (kj <= qi + B_BLK)
    q1 = lax.broadcasted_iota(jnp.int32, (B_BLK, B_BLK), 0)
    k1 = lax.broadcasted_iota(jnp.int32, (B_BLK, B_BLK), 1)
    return k1 <= q1, two


def _lane_pick(ref_rows, h):
    lane = lax.broadcasted_iota(jnp.int32, ref_rows.shape, 1)
    return jnp.sum(jnp.where(lane == h, ref_rows, 0.0), axis=1, keepdims=True)


def _attn_fwd(qkv, name):
    ns, ln, _ = qkv.shape
    nb = ln // B_BLK
    scale = B_DH ** -0.5

    def body(q_ref, k_ref, v_ref, o_ref, lse_ref):
        h = pl.program_id(1)
        mask1, mask2 = _attn_masks()

        @pl.when(h == 0)
        def _():
            lse_ref[...] = jnp.zeros_like(lse_ref)

        def block(i, first):
            rows = pl.ds(pl.multiple_of(i * B_BLK, B_BLK), B_BLK)
            if first:
                win, mask = pl.ds(0, B_BLK), mask1
            else:
                win, mask = pl.ds(pl.multiple_of((i - 1) * B_BLK, B_BLK), 2 * B_BLK), mask2
            sc = jnp.where(mask, _dot(q_ref[rows, :], k_ref[win, :], 1, 1) * scale, -1e30)
            m = jnp.max(sc, axis=1, keepdims=True)
            p = jnp.exp(sc - m)
            l = jnp.sum(p, axis=1, keepdims=True)
            o_ref[rows, :] = _dot(p, v_ref[win, :], 1, 0) / l
            lane = lax.broadcasted_iota(jnp.int32, (B_BLK, B_HEADS), 1)
            lse_ref[rows, :] = jnp.where(lane == h, m + jnp.log(l), lse_ref[rows, :])

        block(0, True)
        if nb > 1:
            lax.fori_loop(1, nb, lambda i, c: (block(i, False), c)[1], 0)

    return pl.pallas_call(
        body, name=name, grid=(ns, B_HEADS),
        in_specs=[pl.BlockSpec((None, ln, B_DH), lambda s, h: (s, 0, h)),
                  pl.BlockSpec((None, ln, B_DH), lambda s, h: (s, 0, B_HEADS + h)),
                  pl.BlockSpec((None, ln, B_DH), lambda s, h: (s, 0, 2 * B_HEADS + h))],
        out_specs=[pl.BlockSpec((None, ln, B_DH), lambda s, h: (s, 0, h)),
                   pl.BlockSpec((None, ln, B_HEADS), lambda s, h: (s, 0, 0))],
        out_shape=[jax.ShapeDtypeStruct((ns, ln, B_W), F32), jax.ShapeDtypeStruct((ns, ln, B_HEADS), F32)],
        compiler_params=_params(("parallel", "arbitrary")),
    )(qkv, qkv, qkv)


def _attn_bwd(qkv, d_o, lse_joint, delta, name):
    ns, ln, _ = qkv.shape
    nb = ln // B_BLK
    scale = B_DH ** -0.5

    def body(q_ref, k_ref, v_ref, do_ref, lj_ref, dl_ref, dq_ref, dk_ref, dv_ref):
        h = pl.program_id(1)
        mask1, mask2 = _attn_masks()
        dk_ref[...] = jnp.zeros_like(dk_ref)
        dv_ref[...] = jnp.zeros_like(dv_ref)

        def block(i, first):
            rows = pl.ds(pl.multiple_of(i * B_BLK, B_BLK), B_BLK)
            if first:
                win, mask = pl.ds(0, B_BLK), mask1
            else:
                win, mask = pl.ds(pl.multiple_of((i - 1) * B_BLK, B_BLK), 2 * B_BLK), mask2
            q = q_ref[rows, :]
            d_out = do_ref[rows, :]
            l_col = _lane_pick(lj_ref[rows, :], h)
            d_col = _lane_pick(dl_ref[rows, :], h)
            sc = _dot(q, k_ref[win, :], 1, 1) * scale
            p = jnp.exp(jnp.where(mask, sc - l_col, -1e30))
            d_p = _dot(d_out, v_ref[win, :], 1, 1)
            d_s = p * (d_p - d_col) * scale
            dq_ref[rows, :] = _dot(d_s, k_ref[win, :], 1, 0)
            dk_ref[win, :] += _dot(d_s, q, 0, 0)
            dv_ref[win, :] += _dot(p, d_out, 0, 0)

        block(0, True)
        if nb > 1:
            lax.fori_loop(1, nb, lambda i, c: (block(i, False), c)[1], 0)

    head = lambda off: pl.BlockSpec((None, ln, B_DH), lambda s, h: (s, 0, off + h))
    small = pl.BlockSpec((None, ln, B_HEADS), lambda s, h: (s, 0, 0))
    return pl.pallas_call(
        body, name=name, grid=(ns, B_HEADS),
        in_specs=[head(0), head(B_HEADS), head(2 * B_HEADS), head(0), small, small],
        out_specs=[head(0)] * 3,
        out_shape=[jax.ShapeDtypeStruct((ns, ln, B_W), F32)] * 3,
        compiler_params=_params(("parallel", "parallel")),
    )(qkv, qkv, qkv, d_o, lse_joint, delta)


B_ROWS = 2048


def _attn_schedule(nb, sb, block):
    for si in range(sb):
        block(si, 0, True)
    if nb == 1:
        return
    per = 1 if sb > 1 else 2
    lead = 1 + (nb - 1) % per
    for i in range(1, lead):
        for si in range(sb):
            block(si, i, False)

    def step(it, carry):
        for u in range(per):
            for si in range(sb):
                block(si, lead + it * per + u, False)
        return carry

    lax.fori_loop(0, (nb - lead) // per, step, 0)


def _attn_rows(i, first):
    if first:
        return pl.ds(0, B_BLK), pl.ds(0, B_BLK)
    rows = pl.ds(pl.multiple_of(i * B_BLK, B_BLK), B_BLK)
    return rows, pl.ds(pl.multiple_of((i - 1) * B_BLK, B_BLK), 2 * B_BLK)


def _attn_fwd(qkv, name):
    ns, ln, _ = qkv.shape
    nb = ln // B_BLK
    sb = B_ROWS // ln
    scale = B_DH ** -0.5

    def body(q_ref, k_ref, v_ref, o_ref, lse_ref):
        h = pl.program_id(1)
        mask1, mask2 = _attn_masks()
        lane = lax.broadcasted_iota(jnp.int32, (B_BLK, B_HEADS), 1)

        @pl.when(h == 0)
        def _():
            lse_ref[...] = jnp.zeros_like(lse_ref)

        def block(si, i, first):
            rows, win = _attn_rows(i, first)
            mask = mask1 if first else mask2
            sc = jnp.where(mask, _dot(q_ref[si, rows, :], k_ref[si, win, :], 1, 1) * scale, -1e30)
            m = jnp.max(sc, axis=1, keepdims=True)
            p = jnp.exp(sc - m)
            l = jnp.sum(p, axis=1, keepdims=True)
            o_ref[si, rows, :] = _dot(p, v_ref[si, win, :], 1, 0) / l
            lse_ref[si, rows, :] = jnp.where(lane == h, m + jnp.log(l), lse_ref[si, rows, :])

        _attn_schedule(nb, sb, block)

    head = lambda off: pl.BlockSpec((sb, ln, B_DH), lambda s, h: (s, 0, off + h))
    return pl.pallas_call(
        body, name=name, grid=(ns // sb, B_HEADS),
        in_specs=[head(0), head(B_HEADS), head(2 * B_HEADS)],
        out_specs=[head(0), pl.BlockSpec((sb, ln, B_HEADS), lambda s, h: (s, 0, 0))],
        out_shape=[jax.ShapeDtypeStruct((ns, ln, B_W), F32), jax.ShapeDtypeStruct((ns, ln, B_HEADS), F32)],
        compiler_params=_params(("parallel", "arbitrary")),
    )(qkv, qkv, qkv)


def _attn_bwd(qkv, d_o, lse_joint, delta, name):
    ns, ln, _ = qkv.shape
    nb = ln // B_BLK
    sb = B_ROWS // ln
    scale = B_DH ** -0.5

    def body(q_ref, k_ref, v_ref, do_ref, lj_ref, dl_ref, dq_ref, dk_ref, dv_ref):
        h = pl.program_id(1)
        mask1, mask2 = _attn_masks()
        dk_ref[...] = jnp.zeros_like(dk_ref)
        dv_ref[...] = jnp.zeros_like(dv_ref)

        def block(si, i, first):
            rows, win = _attn_rows(i, first)
            mask = mask1 if first else mask2
            q = q_ref[si, rows, :]
            d_out = do_ref[si, rows, :]
            l_col = _lane_pick(lj_ref[si, rows, :], h)
            d_col = _lane_pick(dl_ref[si, rows, :], h)
            sc = _dot(q, k_ref[si, win, :], 1, 1) * scale
            p = jnp.exp(jnp.where(mask, sc - l_col, -1e30))
            d_p = _dot(d_out, v_ref[si, win, :], 1, 1)
            d_s = p * (d_p - d_col) * scale
            dq_ref[si, rows, :] = _dot(d_s, k_ref[si, win, :], 1, 0)
            dk_ref[si, win, :] += _dot(d_s, q, 0, 0)
            dv_ref[si, win, :] += _dot(p, d_out, 0, 0)

        _attn_schedule(nb, sb, block)

    head = lambda off: pl.BlockSpec((sb, ln, B_DH), lambda s, h: (s, 0, off + h))
    small = pl.BlockSpec((sb, ln, B_HEADS), lambda s, h: (s, 0, 0))
    return pl.pallas_call(
        body, name=name, grid=(ns // sb, B_HEADS),
        in_specs=[head(0), head(B_HEADS), head(2 * B_HEADS), head(0), small, small],
        out_specs=[head(0)] * 3,
        out_shape=[jax.ShapeDtypeStruct((ns, ln, B_W), F32)] * 3,
        compiler_params=_params(("parallel", "parallel")),
    )(qkv, qkv, qkv, d_o, lse_joint, delta)


def _merge_weights(lse_refs):
    ls = [r[...] for r in lse_refs]
    m = jnp.maximum(jnp.maximum(ls[0], ls[1]), ls[2])
    es = [jnp.exp(l - m) for l in ls]
    tot = es[0] + es[1] + es[2]
    return [e / tot for e in es], m + jnp.log(tot)


def _merge_fwd(outs, lses, proj0, tm=256):
    t = outs[0].shape[0]

    def body(o0, o1, o2, l0, l1, l2, z_ref, og_ref):
        wts, _ = _merge_weights((l0, l1, l2))
        for h in range(B_HEADS):
            cols = slice(h * B_DH, (h + 1) * B_DH)
            o = (wts[0][:, h:h + 1] * o0[:, cols] + wts[1][:, h:h + 1] * o1[:, cols]
                 + wts[2][:, h:h + 1] * o2[:, cols])
            og_ref[:, cols] = (o * _silu(z_ref[:, cols])).astype(BF16)

    wide = pl.BlockSpec((tm, B_W), lambda i: (i, 0))
    small = pl.BlockSpec((tm, B_HEADS), lambda i: (i, 0))
    return pl.pallas_call(
        body, name="merge_fwd", grid=(t // tm,),
        in_specs=[wide] * 3 + [small] * 3 + [pl.BlockSpec((tm, B_W), lambda i: (i, 3))],
        out_specs=wide, out_shape=jax.ShapeDtypeStruct((t, B_W), BF16),
        compiler_params=_params(("parallel",)),
    )(*outs, *lses, proj0)


def _merge_bwd(outs, lses, proj0, d_og, tm=256):
    t = outs[0].shape[0]

    def body(o0, o1, o2, l0, l1, l2, z_ref, dog_ref, do_ref, lj_ref, dl_ref, dz_ref):
        wts, lj = _merge_weights((l0, l1, l2))
        lj_ref[...] = lj
        lane = lax.broadcasted_iota(jnp.int32, (tm, B_HEADS), 1)
        delta = jnp.zeros((tm, B_HEADS), F32)
        for h in range(B_HEADS):
            cols = slice(h * B_DH, (h + 1) * B_DH)
            o = (wts[0][:, h:h + 1] * o0[:, cols] + wts[1][:, h:h + 1] * o1[:, cols]
                 + wts[2][:, h:h + 1] * o2[:, cols])
            z = z_ref[:, cols]
            d_g = dog_ref[:, cols]
            d_out = d_g * _silu(z)
            dz_ref[:, cols] = (d_g * o * _dsilu(z)).astype(BF16)
            do_ref[:, cols] = d_out.astype(BF16)
            delta = jnp.where(lane == h, jnp.sum(d_out * o, axis=1, keepdims=True), delta)
        dl_ref[...] = delta

    wide = pl.BlockSpec((tm, B_W), lambda i: (i, 0))
    small = pl.BlockSpec((tm, B_HEADS), lambda i: (i, 0))
    return pl.pallas_call(
        body, name="merge_bwd", grid=(t // tm,),
        in_specs=[wide] * 3 + [small] * 3 + [pl.BlockSpec((tm, B_W), lambda i: (i, 3)), wide],
        out_specs=[wide, small, small, wide],
        out_shape=[jax.ShapeDtypeStruct((t, B_W), BF16), jax.ShapeDtypeStruct((t, B_HEADS), F32),
                   jax.ShapeDtypeStruct((t, B_HEADS), F32), jax.ShapeDtypeStruct((t, B_W), BF16)],
        compiler_params=_params(("parallel",)),
    )(*outs, *lses, proj0, d_og)


def _adamw(w, g, m, v, name):
    r, c = w.shape
    tr = r
    for cand in (256, 128, 64, 32, 16, 8):
        if r % cand == 0:
            tr = cand
            break

    def body(w_ref, g_ref, m_ref, v_ref, d_ref, nm_ref, nv_ref):
        gv = g_ref[...]
        nm = ADAM_B1 * m_ref[...] + (1.0 - ADAM_B1) * gv
        nv = ADAM_B2 * v_ref[...] + (1.0 - ADAM_B2) * (gv * gv)
        m_hat = nm / (1.0 - ADAM_B1 ** ADAM_STEP)
        v_hat = nv / (1.0 - ADAM_B2 ** ADAM_STEP)
        d_ref[...] = -ADAM_LR * (m_hat / (jnp.sqrt(v_hat) + ADAM_EPS) + ADAM_WD * w_ref[...])
        nm_ref[...] = nm
        nv_ref[...] = nv

    blk = pl.BlockSpec((tr, c), lambda i: (i, 0))
    return pl.pallas_call(
        body, name=name, grid=(r // tr,), in_specs=[blk] * 4, out_specs=[blk] * 3,
        out_shape=[jax.ShapeDtypeStruct((r, c), F32)] * 3,
        compiler_params=_params(("parallel",)),
    )(w, g, m, v)


def _pair_sum(own, other, half_index, name, tr=256):
    _, r, c = own.shape
    rh = r // 2
    tr = min(tr, rh)
    nrb = rh // tr

    def body(c_ref, own_ref, oth_ref, out_ref):
        out_ref[...] = (own_ref[...] + oth_ref[...].astype(F32)).astype(BF16)

    return pl.pallas_call(
        body, name=name,
        grid_spec=pltpu.PrefetchScalarGridSpec(
            num_scalar_prefetch=1, grid=(N_CHIPS, nrb),
            in_specs=[pl.BlockSpec((None, tr, c), lambda k, i, cc: (k, cc[0] * nrb + i, 0)),
                      pl.BlockSpec((None, tr, c), lambda k, i, cc: (k, i, 0))],
            out_specs=pl.BlockSpec((None, tr, c), lambda k, i, cc: (k, i, 0))),
        out_shape=jax.ShapeDtypeStruct((N_CHIPS, rh, c), BF16),
        compiler_params=_params(("parallel", "parallel")),
    )(half_index, own, other)


def _chip_sum(parts, name, tr=256):
    _, r, c = parts.shape
    tr = min(tr, r)

    def body(p_ref, out_ref):
        acc = p_ref[0].astype(F32)
        for k in range(1, N_CHIPS):
            acc = acc + p_ref[k].astype(F32)
        out_ref[...] = acc

    return pl.pallas_call(
        body, name=name, grid=(r // tr,),
        in_specs=[pl.BlockSpec((N_CHIPS, tr, c), lambda i: (0, i, 0))],
        out_specs=pl.BlockSpec((tr, c), lambda i: (i, 0)),
        out_shape=jax.ShapeDtypeStruct((r, c), F32),
        compiler_params=_params(("parallel",)),
    )(parts)


HBM = pl.BlockSpec(memory_space=pltpu.HBM)


def _place():
    x, y, c = lax.axis_index("x"), lax.axis_index("y"), lax.axis_index("c")
    chips = [(1 - x, y), (x, 1 - y), (1 - x, 1 - y)]
    return x, y, c, chips


def _weight_allgather(shards, conv_shard):
    na = len(shards)

    def body(*refs):
        ins = refs[:na]
        conv_in = refs[na]
        outs = refs[na + 1:2 * na + 1]
        conv_out = refs[2 * na + 1]
        send, recv, fsend, frecv, lsem, csend, crecv = refs[2 * na + 2:]
        x, y, c, chips = _place()
        me = 2 * x + y
        sib = (x, y, 1 - c)
        local, first, conv_cp = [], [], []
        for i in range(na):
            rh = ins[i].shape[0] // 2
            mine = pl.ds(c * rh, rh)
            cp = pltpu.make_async_copy(ins[i], outs[i].at[me], lsem.at[i])
            cp.start()
            local.append(cp)
            for j, (px, py) in enumerate(chips):
                cp = pltpu.make_async_remote_copy(
                    src_ref=ins[i].at[mine], dst_ref=outs[i].at[me, mine],
                    send_sem=send.at[3 * i + j], recv_sem=recv.at[3 * i + j],
                    device_id=(px, py, c), device_id_type=MESH)
                cp.start()
                first.append(cp)
        cp = pltpu.make_async_copy(conv_in, conv_out.at[me], lsem.at[na])
        cp.start()
        local.append(cp)
        for j, (px, py) in enumerate(chips):
            cp = pltpu.make_async_remote_copy(
                src_ref=conv_in, dst_ref=conv_out.at[me], send_sem=csend.at[j], recv_sem=crecv.at[j],
                device_id=(px, py, c), device_id_type=MESH)
            cp.start()
            conv_cp.append(cp)
        passed = []
        for i in range(na):
            rh = ins[i].shape[0] // 2
            mine = pl.ds(c * rh, rh)
            for j, (px, py) in enumerate(chips):
                slot = outs[i].at[2 * px + py, mine]
                pltpu.make_async_remote_copy(
                    src_ref=slot, dst_ref=slot, send_sem=send.at[3 * i + j], recv_sem=recv.at[3 * i + j],
                    device_id=(px, py, c), device_id_type=MESH).wait_recv()
                cp = pltpu.make_async_remote_copy(
                    src_ref=slot, dst_ref=slot, send_sem=fsend.at[3 * i + j], recv_sem=frecv.at[3 * i + j],
                    device_id=sib, device_id_type=MESH)
                cp.start()
                passed.append(cp)
        for i in range(na):
            rh = ins[i].shape[0] // 2
            theirs = pl.ds((1 - c) * rh, rh)
            for j, (px, py) in enumerate(chips):
                slot = outs[i].at[2 * px + py, theirs]
                pltpu.make_async_remote_copy(
                    src_ref=slot, dst_ref=slot, send_sem=fsend.at[3 * i + j], recv_sem=frecv.at[3 * i + j],
                    device_id=sib, device_id_type=MESH).wait_recv()
        for j, (px, py) in enumerate(chips):
            slot = conv_out.at[2 * px + py]
            pltpu.make_async_remote_copy(
                src_ref=slot, dst_ref=slot, send_sem=csend.at[j], recv_sem=crecv.at[j],
                device_id=(px, py, c), device_id_type=MESH).wait_recv()
        for cp in first + passed + conv_cp:
            cp.wait_send()
        for cp in local:
            cp.wait()

    out_shape = [jax.ShapeDtypeStruct((N_CHIPS,) + s.shape, s.dtype) for s in shards]
    out_shape.append(jax.ShapeDtypeStruct((N_CHIPS,) + conv_shard.shape, conv_shard.dtype))
    res = pl.pallas_call(
        body, name="weight_allgather",
        in_specs=[HBM] * (na + 1), out_specs=[HBM] * (na + 1), out_shape=out_shape,
        scratch_shapes=[pltpu.SemaphoreType.DMA((3 * na,)), pltpu.SemaphoreType.DMA((3 * na,)),
                        pltpu.SemaphoreType.DMA((3 * na,)), pltpu.SemaphoreType.DMA((3 * na,)),
                        pltpu.SemaphoreType.DMA((na + 1,)), pltpu.SemaphoreType.DMA((3,)),
                        pltpu.SemaphoreType.DMA((3,))],
    )(*shards, conv_shard)
    return res[:na], res[na]


def _sibling_swap_halves(grads):
    na = len(grads)

    def body(*refs):
        ins, outs = refs[:na], refs[na:2 * na]
        send, recv = refs[2 * na:]
        x, y, c, _ = _place()
        sib = (x, y, 1 - c)
        cps = []
        for i in range(na):
            rh = ins[i].shape[1] // 2
            cp = pltpu.make_async_remote_copy(
                src_ref=ins[i].at[:, pl.ds((1 - c) * rh, rh), :], dst_ref=outs[i],
                send_sem=send.at[i], recv_sem=recv.at[i], device_id=sib, device_id_type=MESH)
            cp.start()
            cps.append(cp)
        for cp in cps:
            cp.wait()

    out_shape = [jax.ShapeDtypeStruct((g.shape[0], g.shape[1] // 2, g.shape[2]), g.dtype) for g in grads]
    return pl.pallas_call(
        body, name="grad_sibling_swap", in_specs=[HBM] * na, out_specs=[HBM] * na, out_shape=out_shape,
        scratch_shapes=[pltpu.SemaphoreType.DMA((na,)), pltpu.SemaphoreType.DMA((na,))],
    )(*grads)


def _chip_exchange(sums):
    na = len(sums)

    def body(*refs):
        ins, outs = refs[:na], refs[na:2 * na]
        send, recv, lsem = refs[2 * na:]
        x, y, c, chips = _place()
        me = 2 * x + y
        cps, local = [], []
        for i in range(na):
            cp = pltpu.make_async_copy(ins[i].at[me], outs[i].at[me], lsem.at[i])
            cp.start()
            local.append(cp)
            for j, (px, py) in enumerate(chips):
                cp = pltpu.make_async_remote_copy(
                    src_ref=ins[i].at[2 * px + py], dst_ref=outs[i].at[me],
                    send_sem=send.at[3 * i + j], recv_sem=recv.at[3 * i + j],
                    device_id=(px, py, c), device_id_type=MESH)
                cp.start()
                cps.append(cp)
        for i in range(na):
            for j, (px, py) in enumerate(chips):
                slot = outs[i].at[2 * px + py]
                pltpu.make_async_remote_copy(
                    src_ref=slot, dst_ref=slot, send_sem=send.at[3 * i + j], recv_sem=recv.at[3 * i + j],
                    device_id=(px, py, c), device_id_type=MESH).wait_recv()
        for cp in cps:
            cp.wait_send()
        for cp in local:
            cp.wait()

    out_shape = [jax.ShapeDtypeStruct(s.shape, s.dtype) for s in sums]
    return pl.pallas_call(
        body, name="grad_chip_exchange", in_specs=[HBM] * na, out_specs=[HBM] * na, out_shape=out_shape,
        scratch_shapes=[pltpu.SemaphoreType.DMA((3 * na,)), pltpu.SemaphoreType.DMA((3 * na,)),
                        pltpu.SemaphoreType.DMA((na,))],
    )(*sums)


def _sibling_join_halves(halves):
    na = len(halves)

    def body(*refs):
        ins, outs = refs[:na], refs[na:2 * na]
        send, recv, lsem = refs[2 * na:]
        x, y, c, _ = _place()
        sib = (x, y, 1 - c)
        cps, local = [], []
        for i in range(na):
            rh = ins[i].shape[0]
            mine = pl.ds(c * rh, rh)
            cp = pltpu.make_async_copy(ins[i], outs[i].at[mine], lsem.at[i])
            cp.start()
            local.append(cp)
            cp = pltpu.make_async_remote_copy(
                src_ref=ins[i], dst_ref=outs[i].at[mine], send_sem=send.at[i], recv_sem=recv.at[i],
                device_id=sib, device_id_type=MESH)
            cp.start()
            cps.append(cp)
        for i in range(na):
            rh = ins[i].shape[0]
            theirs = outs[i].at[pl.ds((1 - c) * rh, rh)]
            pltpu.make_async_remote_copy(
                src_ref=theirs, dst_ref=theirs, send_sem=send.at[i], recv_sem=recv.at[i],
                device_id=sib, device_id_type=MESH).wait_recv()
        for cp in cps:
            cp.wait_send()
        for cp in local:
            cp.wait()

    out_shape = [jax.ShapeDtypeStruct((2 * h.shape[0], h.shape[1]), h.dtype) for h in halves]
    return pl.pallas_call(
        body, name="grad_sibling_join", in_specs=[HBM] * na, out_specs=[HBM] * na, out_shape=out_shape,
        scratch_shapes=[pltpu.SemaphoreType.DMA((na,)), pltpu.SemaphoreType.DMA((na,)),
                        pltpu.SemaphoreType.DMA((na,))],
    )(*halves)


def _small_allreduce(vec):
    r, cdim = vec.shape
    n_dev = 8

    def body(v_ref, out_ref, buf, send, recv):
        x, y, c, _ = _place()
        me = 4 * x + 2 * y + c
        buf[me] = v_ref[...]
        cps = []
        for k in range(1, n_dev):
            dx, dy, dc = (k >> 2) & 1, (k >> 1) & 1, k & 1
            peer = (x ^ dx, y ^ dy, c ^ dc)
            cp = pltpu.make_async_remote_copy(
                src_ref=v_ref, dst_ref=buf.at[me], send_sem=send.at[k - 1], recv_sem=recv.at[k - 1],
                device_id=peer, device_id_type=MESH)
            cp.start()
            cps.append(cp)
        for k in range(1, n_dev):
            dx, dy, dc = (k >> 2) & 1, (k >> 1) & 1, k & 1
            src = 4 * (x ^ dx) + 2 * (y ^ dy) + (c ^ dc)
            slot = buf.at[src]
            pltpu.make_async_remote_copy(
                src_ref=slot, dst_ref=slot, send_sem=send.at[k - 1], recv_sem=recv.at[k - 1],
                device_id=(x ^ dx, y ^ dy, c ^ dc), device_id_type=MESH).wait_recv()
        for cp in cps:
            cp.wait_send()
        acc = buf[0]
        for k in range(1, n_dev):
            acc = acc + buf[k]
        out_ref[...] = acc

    vm = pl.BlockSpec(memory_space=pltpu.VMEM)
    return pl.pallas_call(
        body, name="small_allreduce", in_specs=[vm], out_specs=vm,
        out_shape=jax.ShapeDtypeStruct((r, cdim), F32),
        scratch_shapes=[pltpu.VMEM((n_dev, r, cdim), F32), pltpu.SemaphoreType.DMA((n_dev - 1,)),
                        pltpu.SemaphoreType.DMA((n_dev - 1,))],
    )(vec)


def _a_cols_to_head_major(w):
    lead = w.shape[:-1]
    q = w[..., :A_QK].reshape(lead + (A_HEADS, A_DK))
    k = w[..., A_QK:2 * A_QK].reshape(lead + (A_HEADS, A_DK))
    v = w[..., 2 * A_QK:2 * A_QK + A_VW].reshape(lead + (A_HEADS, A_DV))
    z = w[..., 2 * A_QK + A_VW:].reshape(lead + (A_HEADS, A_DV))
    return jnp.concatenate([q, k, v, z], axis=-1).reshape(lead + (A_HEADS * A_HEAD_COLS,))


def _a_cols_from_head_major(w):
    lead = w.shape[:-1]
    w = w.reshape(lead + (A_HEADS, A_HEAD_COLS))
    parts = [w[..., :A_DK], w[..., A_DK:2 * A_DK], w[..., 2 * A_DK:2 * A_DK + A_DV], w[..., 2 * A_DK + A_DV:]]
    return jnp.concatenate([p.reshape(lead + (-1,)) for p in parts], axis=-1)


def _conv_cols_to_head_major(w):
    lead = w.shape[:-1]
    q = w[..., :A_QK].reshape(lead + (A_HEADS, A_DK))
    k = w[..., A_QK:2 * A_QK].reshape(lead + (A_HEADS, A_DK))
    v = w[..., 2 * A_QK:].reshape(lead + (A_HEADS, A_DV))
    return jnp.concatenate([q, k, v], axis=-1).reshape(lead + (A_HEADS * A_CONV_COLS,))


def _conv_cols_from_head_major(w):
    lead = w.shape[:-1]
    w = w.reshape(lead + (A_HEADS, A_CONV_COLS))
    parts = [w[..., :A_DK], w[..., A_DK:2 * A_DK], w[..., 2 * A_DK:]]
    return jnp.concatenate([p.reshape(lead + (-1,)) for p in parts], axis=-1)


def _to_stream(a, bn, d):
    rest = a.shape[1:]
    s = a.shape[0] // bn
    a = a.reshape((bn, s // d, d) + rest)
    a = jnp.swapaxes(a, 1, 2)
    return a.reshape((bn * d, s // d) + rest)


def _from_stream(a, bn, d):
    rest = a.shape[2:]
    ln = a.shape[1]
    a = a.reshape((bn, d, ln) + rest)
    a = jnp.swapaxes(a, 1, 2)
    return a.reshape((bn * ln * d,) + rest)


def _b_group_cols(w, gi):
    n_qkv = 3 * B_GROUPS * B_W
    qkv = w[..., :n_qkv].reshape(w.shape[:-1] + (3, B_GROUPS, B_W))
    return qkv[..., :, gi, :].reshape(w.shape[:-1] + (3 * B_W,))


def _shard_major(g, ncols):
    r = g.shape[0]
    return jnp.swapaxes(g.reshape(r, N_CHIPS, ncols), 0, 1)


def _pack_rows(items):
    rows, offs = [], []
    at = 0
    for a in items:
        flat = a.reshape(-1).astype(F32)
        nr = -(-flat.shape[0] // 128)
        flat = jnp.pad(flat, (0, nr * 128 - flat.shape[0]))
        rows.append(flat.reshape(nr, 128))
        offs.append((at, nr, a.shape))
        at += nr
    pad = -at % 8
    if pad:
        rows.append(jnp.zeros((pad, 128), F32))
    return jnp.concatenate(rows, axis=0), offs


def _unpack_rows(packed, offs):
    out = []
    for at, nr, shape in offs:
        size = int(np.prod(shape)) if len(shape) else 1
        out.append(packed[at:at + nr].reshape(-1)[:size].reshape(shape))
    return out


def _local_step(x, positions, loss_target, norm_g, wa_in, conv_w, a_log, a_dt_bias, a_norm_g, wa_out,
                wb_in, b_q_norm_g, b_k_norm_g, wb_out):
    bn, s, d = x.shape
    t = bn * s
    n_chunks = s // A_CHUNK
    wa_main = _a_cols_to_head_major(wa_in[:, :A_MAIN])
    wa_tail = jnp.pad(wa_in[:, A_MAIN:], ((0, 0), (0, 128 - 2 * A_HEADS)))
    cw_hm = _conv_cols_to_head_major(conv_w)
    wb_groups = [_b_group_cols(wb_in, gi) for gi in range(B_GROUPS)]
    wb_groups[0] = jnp.concatenate([wb_groups[0], wb_in[:, 3 * B_GROUPS * B_W:]], axis=1)

    x0 = x.reshape(t, d)
    h0 = _rms_fwd(x0, norm_g[0:1], "rms0_fwd")
    proj_a = _matmul(h0, wa_main, "nn", F32, "a_in_main")
    tail_a = _matmul(h0, wa_tail, "nn", F32, "a_in_tail")
    tail_t = jnp.swapaxes(tail_a[:, :2 * A_HEADS].reshape(bn, s, 2 * A_HEADS), 1, 2)
    tail_t = tail_t.reshape(bn, 2 * A_HEADS, n_chunks, A_CHUNK)
    beta, gc = _gdn_prep(tail_t, a_log[0], a_dt_bias[0])
    proj_a3 = proj_a.reshape(bn, s, A_MAIN)
    og_a, oraw_a, states, t_mats = _gdn_fwd(proj_a3, cw_hm, beta, gc, a_norm_g)
    x1 = _matmul(og_a.reshape(t, A_VW), wa_out, "nn", F32, "a_out", res=x0, tk=2048)

    h1 = _rms_fwd(x1, norm_g[1:2], "rms1_fwd")
    inv_freq = ROPE_THETA ** (-jnp.arange(0, ROPE_DIMS, 2, dtype=F32) / ROPE_DIMS)
    freq_row = jnp.concatenate([inv_freq, inv_freq, jnp.zeros((128 - ROPE_DIMS,), F32)]).reshape(1, 128)
    posf = jnp.broadcast_to(positions.astype(F32).reshape(t, 1), (t, 128))
    tabs = _rope_tables(posf, freq_row)
    h1_s, tabs_s, proj_b, qkv_b, o_b, lse_b = [], [], [], [], [], []
    for gi, dil in enumerate(B_DIL):
        hs = h1 if dil == 1 else _to_stream(h1, bn, dil).reshape(t, d)
        ts = tabs if dil == 1 else [_to_stream(tb, bn, dil).reshape(t, 128) for tb in tabs]
        pj = _matmul(hs, wb_groups[gi], "nn", F32, f"b_in_g{gi}")
        qkv = _qk_prep(pj, *ts, b_q_norm_g[0, gi:gi + 1], b_k_norm_g[0, gi:gi + 1], f"qk_prep_g{gi}")
        o_s, lse_s = _attn_fwd(qkv.reshape(bn * dil, s // dil, 3 * B_W), f"attn_fwd_g{gi}")
        h1_s.append(hs), tabs_s.append(ts), proj_b.append(pj), qkv_b.append(qkv)
        o_b.append(o_s.reshape(t, B_W) if dil == 1 else _from_stream(o_s, bn, dil))
        lse_b.append(lse_s.reshape(t, B_HEADS) if dil == 1 else _from_stream(lse_s, bn, dil))
    og_b = _merge_fwd(o_b, lse_b, proj_b[0])
    x2 = _matmul(og_b, wb_out, "nn", F32, "b_out", res=x1)

    d_x2, loss_parts = _loss_grad(x2, loss_target.reshape(t, d))
    loss_local = jnp.sum(loss_parts)

    d_x2b = d_x2.astype(BF16)
    g_wb_out = _matmul(og_b.T, d_x2b, "nn", F32, "b_out_dw")
    d_og_b = _matmul(d_x2b, wb_out, "nt", F32, "b_out_dx")
    d_o, lse_joint, delta, d_z = _merge_bwd(o_b, lse_b, proj_b[0], d_og_b)
    d_h1, g_wb_cols, g_qn, g_kn = [], [], [], []
    for gi, dil in enumerate(B_DIL):
        if dil == 1:
            do_s, lj_s, dl_s = d_o, lse_joint, delta
        else:
            do_s, lj_s, dl_s = (_to_stream(a, bn, dil).reshape(t, -1) for a in (d_o, lse_joint, delta))
        ns, ln = bn * dil, s // dil
        dq, dk, dv = _attn_bwd(qkv_b[gi].reshape(ns, ln, 3 * B_W), do_s.reshape(ns, ln, B_W),
                               lj_s.reshape(ns, ln, B_HEADS), dl_s.reshape(ns, ln, B_HEADS), f"attn_bwd_g{gi}")
        d_pj, d_gain = _qk_prep_bwd(proj_b[gi], *tabs_s[gi], b_q_norm_g[0, gi:gi + 1], b_k_norm_g[0, gi:gi + 1],
                                    dq.reshape(t, B_W), dk.reshape(t, B_W), dv.reshape(t, B_W),
                                    d_z if gi == 0 else None, f"qk_prep_bwd_g{gi}")
        g_wb_cols.append(_matmul(h1_s[gi].T, d_pj, "nn", F32, f"b_in_dw_g{gi}"))
        dh = _matmul(d_pj, wb_groups[gi], "nt", F32, f"b_in_dx_g{gi}")
        d_h1.append(dh if dil == 1 else _from_stream(dh.reshape(ns, ln, d), bn, dil))
        g_qn.append(d_gain[0]), g_kn.append(d_gain[1])
    d_x1, g_norm1 = _rms_bwd(x1, norm_g[1:2], d_h1, d_x2, "rms1_bwd")
    pieces = [g_wb_cols[gi][:, w * B_W:(w + 1) * B_W] for w in range(3) for gi in range(B_GROUPS)]
    g_wb_in = jnp.concatenate(pieces + [g_wb_cols[0][:, 3 * B_W:]], axis=1)

    d_x1b = d_x1.astype(BF16)
    g_wa_out = _matmul(og_a.reshape(t, A_VW).T, d_x1b, "nn", F32, "a_out_dw")
    d_og_a = _matmul(d_x1b, wa_out, "nt", F32, "a_out_dx")
    d_pa, d_gc, d_beta, d_cw, d_ng = _gdn_bwd(proj_a3, cw_hm, beta, gc, a_norm_g, oraw_a, states, t_mats,
                                              d_og_a.reshape(bn, s, A_VW))
    d_tail_t, d_alog, d_dtb = _gdn_prep_bwd(tail_t, a_log[0], a_dt_bias[0], d_gc, d_beta)
    d_tail = jnp.swapaxes(d_tail_t.reshape(bn, 2 * A_HEADS, s), 1, 2).reshape(t, 2 * A_HEADS)
    d_tail = jnp.pad(d_tail, ((0, 0), (0, 128 - 2 * A_HEADS))).astype(BF16)
    d_pa = d_pa.reshape(t, A_MAIN)
    h0_t = h0.T
    g_wa_main = _matmul(h0_t, d_pa, "nn", F32, "a_in_dw_main")
    g_wa_tail = _matmul(h0_t, d_tail, "nn", F32, "a_in_dw_tail")
    d_h0 = _matmul(d_pa, wa_main, "nt", F32, "a_in_dx_main")
    d_h0t = _matmul(d_tail, wa_tail, "nt", F32, "a_in_dx_tail")
    d_x0, g_norm0 = _rms_bwd(x0, norm_g[0:1], [d_h0, d_h0t], d_x1, "rms0_bwd")
    g_wa_in = jnp.concatenate([_a_cols_from_head_major(g_wa_main), g_wa_tail[:, :2 * A_HEADS]], axis=1)

    gfull = {
        "norm_g": jnp.concatenate([g_norm0, g_norm1], axis=0), "a_w_in": g_wa_in,
        "a_conv_w": _conv_cols_from_head_major(jnp.sum(d_cw, axis=0)),
        "a_log": jnp.sum(d_alog[:, :, 0], axis=0), "a_dt_bias": jnp.sum(d_dtb[:, :, 0], axis=0),
        "a_norm_g": jnp.sum(d_ng[:, :, 0, :], axis=(0, 1)), "a_w_out": g_wa_out, "b_w_in": g_wb_in,
        "b_q_norm_g": jnp.stack(g_qn), "b_k_norm_g": jnp.stack(g_kn), "b_w_out": g_wb_out}
    return loss_local, d_x0.reshape(bn, s, d), gfull


def kernel(x, positions, norm_g, a_w_in, a_conv_w, a_log, a_dt_bias, a_norm_g, a_w_out, b_w_in, b_q_norm_g, b_k_norm_g, b_w_out, loss_target, m_norm_g, m_a_w_in, m_a_conv_w, m_a_log, m_a_dt_bias, m_a_norm_g, m_a_w_out, m_b_w_in, m_b_q_norm_g, m_b_k_norm_g, m_b_w_out, v_norm_g, v_a_w_in, v_a_conv_w, v_a_log, v_a_dt_bias, v_a_norm_g, v_a_w_out, v_b_w_in, v_b_q_norm_g, v_b_k_norm_g, v_b_w_out):
    d = x.shape[2]
    my_c = lax.axis_index("c")
    my_chip = 2 * lax.axis_index("x") + lax.axis_index("y")

    shards = [a_w_in[0].astype(BF16), a_w_out[0].astype(BF16), b_w_in[0].astype(BF16), b_w_out[0].astype(BF16)]
    (ga_in, ga_out, gb_in, gb_out), g_conv = _weight_allgather(shards, a_conv_w[0])
    wa_in = jnp.concatenate([ga_in[k] for k in range(N_CHIPS)], axis=1)
    wa_out = ga_out.reshape(A_VW, d)
    wb_in = jnp.concatenate([gb_in[k] for k in range(N_CHIPS)], axis=1)
    wb_out = gb_out.reshape(B_W, d)
    conv_w = jnp.concatenate([g_conv[k] for k in range(N_CHIPS)], axis=1)

    loss_local, d_x0, gfull = _local_step(x, positions, loss_target, norm_g, wa_in, conv_w, a_log, a_dt_bias,
                                          a_norm_g, wa_out, wb_in, b_q_norm_g, b_k_norm_g, wb_out)

    g_full = [_shard_major(gfull["a_w_in"], a_w_in.shape[2]), gfull["a_w_out"].reshape(N_CHIPS, -1, d),
              _shard_major(gfull["b_w_in"], b_w_in.shape[2]), gfull["b_w_out"].reshape(N_CHIPS, -1, d)]
    recv_sib = _sibling_swap_halves([g.astype(BF16) for g in g_full])
    half_index = jnp.reshape(my_c, (1,)).astype(jnp.int32)
    chip_sums = [_pair_sum(g, r, half_index, f"grad_pair_sum_{i}") for i, (g, r) in enumerate(zip(g_full, recv_sib))]
    by_source = _chip_exchange(chip_sums)
    halves = [_chip_sum(p, f"grad_chip_sum_{i}") for i, p in enumerate(by_source)]
    g_a_w_in, g_a_w_out, g_b_w_in, g_b_w_out = _sibling_join_halves(halves)

    small = [gfull["norm_g"], gfull["a_conv_w"], gfull["a_log"], gfull["a_dt_bias"], gfull["a_norm_g"],
             gfull["b_q_norm_g"], gfull["b_k_norm_g"], loss_local]
    packed, offs = _pack_rows(small)
    red = _unpack_rows(_small_allreduce(packed), offs)
    g_norm, g_conv_all, g_alog, g_dtb, g_ang, g_q, g_k, loss = red
    g_conv_mine = lax.dynamic_slice_in_dim(g_conv_all, my_chip * a_conv_w.shape[2], a_conv_w.shape[2], axis=1)

    grads = {
        "norm_g": g_norm, "a_w_in": g_a_w_in[None], "a_conv_w": g_conv_mine[None], "a_log": g_alog[None],
        "a_dt_bias": g_dtb[None], "a_norm_g": g_ang[None], "a_w_out": g_a_w_out[None], "b_w_in": g_b_w_in[None],
        "b_q_norm_g": g_q[None], "b_k_norm_g": g_k[None], "b_w_out": g_b_w_out[None]}
    weights = {"norm_g": norm_g, "a_w_in": a_w_in, "a_conv_w": a_conv_w, "a_log": a_log, "a_dt_bias": a_dt_bias,
               "a_norm_g": a_norm_g, "a_w_out": a_w_out, "b_w_in": b_w_in, "b_q_norm_g": b_q_norm_g,
               "b_k_norm_g": b_k_norm_g, "b_w_out": b_w_out}
    m_in = {"norm_g": m_norm_g, "a_w_in": m_a_w_in, "a_conv_w": m_a_conv_w, "a_log": m_a_log,
            "a_dt_bias": m_a_dt_bias, "a_norm_g": m_a_norm_g, "a_w_out": m_a_w_out, "b_w_in": m_b_w_in,
            "b_q_norm_g": m_b_q_norm_g, "b_k_norm_g": m_b_k_norm_g, "b_w_out": m_b_w_out}
    v_in = {"norm_g": v_norm_g, "a_w_in": v_a_w_in, "a_conv_w": v_a_conv_w, "a_log": v_a_log,
            "a_dt_bias": v_a_dt_bias, "a_norm_g": v_a_norm_g, "a_w_out": v_a_w_out, "b_w_in": v_b_w_in,
            "b_q_norm_g": v_b_q_norm_g, "b_k_norm_g": v_b_k_norm_g, "b_w_out": v_b_w_out}
    names = list(weights)

    big = ("a_w_in", "a_w_out", "b_w_in", "b_w_out")
    delta_w, new_m, new_v = {}, {}, {}
    for nm in big:
        shp = weights[nm].shape
        two = lambda a: a.reshape(shp[-2], shp[-1])
        dl, m2, v2 = _adamw(two(weights[nm]), two(grads[nm]), two(m_in[nm]), two(v_in[nm]), f"adamw_{nm}")
        delta_w[nm], new_m[nm], new_v[nm] = dl.reshape(shp), m2.reshape(shp), v2.reshape(shp)
    small_names = [nm for nm in names if nm not in big]
    packs = [_pack_rows([src[nm] for nm in small_names]) for src in (weights, grads, m_in, v_in)]
    offs = packs[0][1]
    dl, m2, v2 = _adamw(packs[0][0], packs[1][0], packs[2][0], packs[3][0], "adamw_small")
    for nm, a, b, c2 in zip(small_names, _unpack_rows(dl, offs), _unpack_rows(m2, offs), _unpack_rows(v2, offs)):
        delta_w[nm], new_m[nm], new_v[nm] = a, b, c2

    return (loss, d_x0, *[grads[nm] for nm in names], *[delta_w[nm] for nm in names],
            *[new_m[nm] for nm in names], *[new_v[nm] for nm in names])
```

```python
import functools
import math

import jax
import jax.numpy as jnp
import numpy as np
from jax import lax
from jax.experimental import pallas as pl
from jax.experimental.pallas import tpu as pltpu

F32 = jnp.float32
BF16 = jnp.bfloat16
MESH = pl.DeviceIdType.MESH

EPS = 1e-6
D_MODEL = 1024
A_HEADS = 8
A_DK = 128
A_DV = 256
A_QK = A_HEADS * A_DK
A_VW = A_HEADS * A_DV
A_MAIN = 2 * A_QK + 2 * A_VW
A_HEAD_COLS = 2 * A_DK + 2 * A_DV
A_CONV_COLS = 2 * A_DK + A_DV
A_CHUNK = 64
A_CONV = 4
B_GROUPS = 3
B_HEADS = 8
B_DH = 128
B_W = B_HEADS * B_DH
B_DIL = (1, 4, 16)
B_BLK = 128
ROPE_THETA = 500000.0
ROPE_DIMS = B_DH // 4
ADAM_LR, ADAM_B1, ADAM_B2, ADAM_EPS, ADAM_WD, ADAM_STEP = 0.001, 0.9, 0.999, 1e-08, 0.01, 10
N_CHIPS = 4
VMEM_BIG = 56 * 1024 * 1024


def _params(sem=None, vmem=None):
    return pltpu.CompilerParams(dimension_semantics=sem, vmem_limit_bytes=vmem)


def _dot(a, b, ca, cb):
    return lax.dot_general(a.astype(BF16), b.astype(BF16), (((ca,), (cb,)), ((), ())),
                           preferred_element_type=F32)


def _split3(a):
    hi = a.astype(BF16)
    r = a - hi.astype(F32)
    mid = r.astype(BF16)
    lo = (r - mid.astype(F32)).astype(BF16)
    return hi, mid, lo


def _dot_hi(a, b, ca, cb):
    a_hi, a_lo, _ = _split3(a)
    b_hi, b_lo, _ = _split3(b)
    dn = (((ca,), (cb,)), ((), ()))
    out = lax.dot_general(a_hi, b_hi, dn, preferred_element_type=F32)
    out = out + lax.dot_general(a_hi, b_lo, dn, preferred_element_type=F32)
    return out + lax.dot_general(a_lo, b_hi, dn, preferred_element_type=F32)


def _sigmoid(y):
    return 1.0 / (1.0 + jnp.exp(-y))


def _silu(y):
    return y * _sigmoid(y)


def _dsilu(y):
    s = _sigmoid(y)
    return s * (1.0 + y * (1.0 - s))


def _matmul(a, b, mode, out_dtype, name, res=None, tm=1024, tn=1024, tk=1024):
    if mode == "nn":
        (m, k), (_, n) = a.shape, b.shape
    else:
        (m, k), (n, _) = a.shape, b.shape
    tm, tn, tk = min(tm, m), min(tn, n), min(tk, k)
    assert m % tm == 0 and n % tn == 0 and k % tk == 0, (name, a.shape, b.shape)
    nk = k // tk
    dims = {"nn": ((1,), (0,)), "nt": ((1,), (1,))}[mode]

    def body(*refs):
        a_ref, b_ref = refs[0], refs[1]
        r_ref = refs[2] if res is not None else None
        o_ref = refs[3] if res is not None else refs[2]
        prod = lax.dot_general(a_ref[...], b_ref[...], (dims, ((), ())), preferred_element_type=F32)

        def finish(r):
            if res is not None:
                r = r + r_ref[...]
            o_ref[...] = r.astype(out_dtype)

        if nk == 1:
            finish(prod)
            return
        acc = refs[-1]
        kk = pl.program_id(2)

        @pl.when(kk == 0)
        def _():
            acc[...] = prod

        @pl.when((kk > 0) & (kk < nk - 1))
        def _():
            acc[...] += prod

        @pl.when(kk == nk - 1)
        def _():
            finish(acc[...] + prod)

    a_spec = pl.BlockSpec((tm, tk), lambda i, j, kk: (i, kk))
    if mode == "nt":
        b_spec = pl.BlockSpec((tn, tk), lambda i, j, kk: (j, kk))
    else:
        b_spec = pl.BlockSpec((tk, tn), lambda i, j, kk: (kk, j))
    in_specs = [a_spec, b_spec]
    args = [a, b]
    if res is not None:
        in_specs.append(pl.BlockSpec((tm, tn), lambda i, j, kk: (i, j)))
        args.append(res)
    return pl.pallas_call(
        body, name=name, grid=(m // tm, n // tn, nk),
        in_specs=in_specs, out_specs=pl.BlockSpec((tm, tn), lambda i, j, kk: (i, j)),
        out_shape=jax.ShapeDtypeStruct((m, n), out_dtype),
        scratch_shapes=[pltpu.VMEM((tm, tn), F32)] if nk > 1 else [],
        compiler_params=_params(("parallel", "parallel", "arbitrary"), 48 * 1024 * 1024),
    )(*args)


def _rms_fwd(x, g, name, tm=256):
    t, d = x.shape

    def body(x_ref, g_ref, h_ref):
        xv = x_ref[...]
        r = lax.rsqrt(jnp.mean(xv * xv, axis=-1, keepdims=True) + EPS)
        h_ref[...] = (xv * r * g_ref[...]).astype(BF16)

    return pl.pallas_call(
        body, name=name, grid=(t // tm,),
        in_specs=[pl.BlockSpec((tm, d), lambda i: (i, 0)), pl.BlockSpec((1, d), lambda i: (0, 0))],
        out_specs=pl.BlockSpec((tm, d), lambda i: (i, 0)),
        out_shape=jax.ShapeDtypeStruct((t, d), BF16),
        compiler_params=_params(("parallel",)),
    )(x, g)


def _rms_bwd(x, g, dhs, dres, name, tm=256):
    t, d = x.shape
    n_dh = len(dhs)

    def body(*refs):
        x_ref, g_ref = refs[0], refs[1]
        dh_refs = refs[2:2 + n_dh]
        dres_ref, dx_ref, dg_ref = refs[2 + n_dh:]
        i = pl.program_id(0)

        @pl.when(i == 0)
        def _():
            dg_ref[...] = jnp.zeros_like(dg_ref)

        xv = x_ref[...]
        r = lax.rsqrt(jnp.mean(xv * xv, axis=-1, keepdims=True) + EPS)
        xh = xv * r
        dh = dh_refs[0][...]
        for ref in dh_refs[1:]:
            dh = dh + ref[...]
        dg_ref[0:1, :] += jnp.sum(dh * xh, axis=0, keepdims=True)
        dxh = dh * g_ref[...]
        dx = r * (dxh - xh * jnp.mean(dxh * xh, axis=-1, keepdims=True))
        dx_ref[...] = dx + dres_ref[...]

    row = pl.BlockSpec((tm, d), lambda i: (i, 0))
    dx, dg = pl.pallas_call(
        body, name=name, grid=(t // tm,),
        in_specs=[row, pl.BlockSpec((1, d), lambda i: (0, 0))] + [row] * n_dh + [row],
        out_specs=[row, pl.BlockSpec((8, d), lambda i: (0, 0))],
        out_shape=[jax.ShapeDtypeStruct((t, d), F32), jax.ShapeDtypeStruct((8, d), F32)],
        compiler_params=_params(("arbitrary",)),
    )(x, g, *dhs, dres)
    return dx, dg[0:1]


def _loss_grad(y, target, name="loss_grad", tm=256):
    t, d = y.shape
    nb = t // tm

    def body(y_ref, t_ref, dy_ref, part_ref):
        e = y_ref[...] - t_ref[...]
        dy_ref[...] = e * (1.0 / d)
        s = jnp.sum(jnp.sum(e * e, axis=1, keepdims=True), axis=0, keepdims=True) * (0.5 / d)
        part_ref[...] = jnp.broadcast_to(s, (8, 128))

    row = pl.BlockSpec((tm, d), lambda i: (i, 0))
    dy, part = pl.pallas_call(
        body, name=name, grid=(nb,), in_specs=[row, row],
        out_specs=[row, pl.BlockSpec((None, 8, 128), lambda i: (i, 0, 0))],
        out_shape=[jax.ShapeDtypeStruct((t, d), F32), jax.ShapeDtypeStruct((nb, 8, 128), F32)],
        compiler_params=_params(("parallel",)),
    )(y, target)
    return dy, part[:, 0, 0]


def _softplus(x):
    t = jnp.exp(-jnp.abs(x))
    return jnp.maximum(x, 0.0) + jnp.where(t < 1e-3, t * (1.0 - 0.5 * t), jnp.log(1.0 + t))


def _tri(rows_le_cols):
    r = lax.broadcasted_iota(jnp.int32, (A_CHUNK, A_CHUNK), 0)
    c = lax.broadcasted_iota(jnp.int32, (A_CHUNK, A_CHUNK), 1)
    return jnp.where((r <= c) if rows_le_cols else (r >= c), 1.0, 0.0).astype(BF16)


def _dot_exact_rhs(a, ones_bf16):
    dn = (((1,), (0,)), ((), ()))
    hi, mid, lo = _split3(a)
    out = lax.dot_general(hi, ones_bf16, dn, preferred_element_type=F32)
    out = out + lax.dot_general(mid, ones_bf16, dn, preferred_element_type=F32)
    return out + lax.dot_general(lo, ones_bf16, dn, preferred_element_type=F32)


def _gdn_prep(tail_t, a_log, dt_bias):
    bn, _, n, c = tail_t.shape

    def body(t_ref, alog_ref, dtb_ref, beta_ref, gc_ref):
        upper = _tri(True)
        for h in range(A_HEADS):
            beta_ref[h] = _sigmoid(t_ref[h])
            ea = jnp.exp(jnp.full((n, c), alog_ref[h], F32))
            g = -ea * _softplus(t_ref[A_HEADS + h] + dtb_ref[h])
            gc_ref[h] = _dot_exact_rhs(g, upper)

    smem = pl.BlockSpec(memory_space=pltpu.SMEM)
    blk = pl.BlockSpec((None, A_HEADS, n, c), lambda b: (b, 0, 0, 0))
    return pl.pallas_call(
        body, name="gdn_prep", grid=(bn,),
        in_specs=[pl.BlockSpec((None, 2 * A_HEADS, n, c), lambda b: (b, 0, 0, 0)), smem, smem],
        out_specs=[blk, blk],
        out_shape=[jax.ShapeDtypeStruct((bn, A_HEADS, n, c), F32)] * 2,
        compiler_params=_params(("parallel",)),
    )(tail_t, a_log, dt_bias)


def _gdn_prep_bwd(tail_t, a_log, dt_bias, d_gc, d_beta):
    bn, _, n, c = tail_t.shape

    def body(t_ref, alog_ref, dtb_ref, dgc_ref, dbeta_ref, dt_ref, dal_ref, ddt_ref):
        lower = _tri(False)
        for h in range(A_HEADS):
            beta = _sigmoid(t_ref[h])
            dt_ref[h] = dbeta_ref[h] * beta * (1.0 - beta)
            dg = _dot_exact_rhs(dgc_ref[h], lower)
            ea = jnp.exp(jnp.full((n, c), alog_ref[h], F32))
            xa = t_ref[A_HEADS + h] + dtb_ref[h]
            g = -ea * _softplus(xa)
            dxa = -ea * dg * _sigmoid(xa)
            dt_ref[A_HEADS + h] = dxa
            s1 = jnp.sum(jnp.sum(g * dg, axis=1, keepdims=True), axis=0, keepdims=True)
            s2 = jnp.sum(jnp.sum(dxa, axis=1, keepdims=True), axis=0, keepdims=True)
            dal_ref[h:h + 1, :] = jnp.broadcast_to(s1, (1, 128))
            ddt_ref[h:h + 1, :] = jnp.broadcast_to(s2, (1, 128))

    smem = pl.BlockSpec(memory_space=pltpu.SMEM)
    blk8 = pl.BlockSpec((None, A_HEADS, n, c), lambda b: (b, 0, 0, 0))
    blk16 = pl.BlockSpec((None, 2 * A_HEADS, n, c), lambda b: (b, 0, 0, 0))
    sm = pl.BlockSpec((None, A_HEADS, 128), lambda b: (b, 0, 0))
    return pl.pallas_call(
        body, name="gdn_prep_bwd", grid=(bn,),
        in_specs=[blk16, smem, smem, blk8, blk8],
        out_specs=[blk16, sm, sm],
        out_shape=[jax.ShapeDtypeStruct((bn, 2 * A_HEADS, n, c), F32),
                   jax.ShapeDtypeStruct((bn, A_HEADS, 128), F32),
                   jax.ShapeDtypeStruct((bn, A_HEADS, 128), F32)],
        compiler_params=_params(("parallel",)),
    )(tail_t, a_log, dt_bias, d_gc, d_beta)


HALO = 8


def _conv_window(x_ref, n, first, lo, width):
    if first:
        return jnp.concatenate([jnp.zeros((HALO, width), F32), x_ref[0:A_CHUNK, lo:lo + width]], axis=0)
    start = pl.multiple_of(n * A_CHUNK - HALO, HALO)
    return x_ref[pl.ds(start, A_CHUNK + HALO), lo:lo + width]


def _conv_taps(xw, w):
    y = w[A_CONV - 1:A_CONV, :] * xw
    for j in range(1, A_CONV):
        y = y + w[A_CONV - 1 - j:A_CONV - j, :] * pltpu.roll(xw, j, 0)
    return y[HALO:, :]


def _row_to_col(row, eye):
    c = eye.shape[0]
    return jnp.sum(jnp.where(eye, jnp.broadcast_to(row, (c, c)), 0.0), axis=1, keepdims=True)


def _col_to_row(col, eye):
    c = eye.shape[0]
    return jnp.sum(jnp.where(eye, jnp.broadcast_to(col, (c, c)), 0.0), axis=0, keepdims=True)


def _unit_lower_inverse(a, ri, ci):
    eye = jnp.where(ri == ci, 1.0, 0.0)
    a8 = jnp.where((ri >> 3) == (ci >> 3), a, 0.0)
    a2 = _dot_hi(a8, a8, 1, 0)
    yield
    a4 = _dot_hi(a2, a2, 1, 0)
    t = eye - a8
    t = t + _dot_hi(t, a2, 1, 0)
    yield
    t = t + _dot_hi(t, a4, 1, 0)
    yield
    for sh in (3, 4, 5):
        off = jnp.where(((ri >> (sh + 1)) == (ci >> (sh + 1))) & ((ri >> sh) != (ci >> sh)), a, 0.0)
        left = _dot_hi(t, off, 1, 0)
        yield
        t = t - _dot_hi(left, t, 1, 0)
        yield
    return t


def _round_robin(gens):
    live = list(gens)
    while live:
        nxt = []
        for g in live:
            try:
                next(g)
                nxt.append(g)
            except StopIteration:
                pass
        live = nxt


def _gdn_chunk_inputs(x_ref, cw, n, first):
    xw = _conv_window(x_ref, n, first, 0, A_CONV_COLS)
    y = _conv_taps(xw, cw)
    a = _silu(y)
    aq, ak, v = a[:, 0:A_DK], a[:, A_DK:2 * A_DK], a[:, 2 * A_DK:]
    rq = lax.rsqrt(jnp.sum(aq * aq, axis=1, keepdims=True) + EPS)
    rk = lax.rsqrt(jnp.sum(ak * ak, axis=1, keepdims=True) + EPS)
    return dict(xw=xw, y=y, aq=aq, ak=ak, rq=rq, rk=rk,
                q=aq * rq * (A_DK ** -0.5), k=ak * rk, v=v)


def _gdn_chunk_core(q, k, v, g_row, b_row, t_mat, ri, ci):
    eye = ri == ci
    g_col = _row_to_col(g_row, eye)
    b_col = _row_to_col(b_row, eye)
    causal = ri >= ci
    strict = ri > ci
    dec = jnp.where(causal, jnp.exp(jnp.where(causal, g_col - g_row, 0.0)), 0.0)
    gam = jnp.exp(g_col)
    g_last = g_row[:, A_CHUNK - 1:A_CHUNK]
    gam_last = jnp.exp(g_last)
    e = jnp.exp(g_last - g_col)
    kb = k * b_col
    bv = v * b_col
    kbg = kb * gam
    kk = _dot(kb, k, 1, 1)
    p = _dot(q, k, 1, 1) * dec
    yield
    a_mat = jnp.where(strict, kk * dec, 0.0)
    if t_mat is None:
        t_mat = yield from _unit_lower_inverse(a_mat, ri, ci)
    u = _dot(t_mat, bv, 1, 0)
    w = _dot(t_mat, kbg, 1, 0)
    yield
    return dict(eye=eye, g_col=g_col, b_col=b_col, dec=dec, strict=strict, causal=causal, gam=gam,
                gam_last=gam_last, e=e, kb=kb, bv=bv, kbg=kbg, a_mat=a_mat, t_mat=t_mat, u=u, w=w, p=p,
                qg=q * gam, kd=k * e)


def _gdn_fwd(proj_hm, cw_hm, beta, gc, norm_g):
    bn, s, _ = proj_hm.shape
    n = s // A_CHUNK

    def body(x_ref, cw_ref, beta_ref, gc_ref, ng_ref, og_ref, oraw_ref, st_ref, t_ref, state):
        ri = lax.broadcasted_iota(jnp.int32, (A_CHUNK, A_CHUNK), 0)
        ci = lax.broadcasted_iota(jnp.int32, (A_CHUNK, A_CHUNK), 1)
        cw = cw_ref[...]
        ng = ng_ref[...]
        state[...] = jnp.zeros_like(state)

        def chunk(i, first):
            rows = pl.ds(0 if first else pl.multiple_of(i * A_CHUNK, A_CHUNK), A_CHUNK)
            cin = _gdn_chunk_inputs(x_ref, cw, i, first)
            core = _gdn_chunk_core(cin["q"], cin["k"], cin["v"], gc_ref[pl.ds(i, 1), :],
                                   beta_ref[pl.ds(i, 1), :], None, ri, ci)
            st = state[...]
            st_ref[i] = st
            t_ref[i] = core["t_mat"]
            vn = core["u"] - _dot(core["w"], st, 1, 0)
            o = _dot(core["qg"], st, 1, 0) + _dot(core["p"], vn, 1, 0)
            state[...] = st * core["gam_last"] + _dot(core["kd"], vn, 0, 0)
            oraw_ref[rows, :] = o
            r = lax.rsqrt(jnp.mean(o * o, axis=1, keepdims=True) + EPS)
            z = x_ref[rows, A_CONV_COLS:A_HEAD_COLS]
            og_ref[rows, :] = (o * r * ng * _silu(z)).astype(BF16)

        chunk(0, True)
        lax.fori_loop(1, n, lambda i, c: (chunk(i, False), c)[1], 0)

    return pl.pallas_call(
        body, name="gdn_fwd", grid=(bn, A_HEADS),
        in_specs=[pl.BlockSpec((None, s, A_HEAD_COLS), lambda b, h: (b, 0, h)),
                  pl.BlockSpec((A_CONV, A_CONV_COLS), lambda b, h: (0, h)),
                  pl.BlockSpec((None, None, n, A_CHUNK), lambda b, h: (b, h, 0, 0)),
                  pl.BlockSpec((None, None, n, A_CHUNK), lambda b, h: (b, h, 0, 0)),
                  pl.BlockSpec((1, A_DV), lambda b, h: (0, 0))],
        out_specs=[pl.BlockSpec((None, s, A_DV), lambda b, h: (b, 0, h)),
                   pl.BlockSpec((None, s, A_DV), lambda b, h: (b, 0, h)),
                   pl.BlockSpec((None, None, n, A_DK, A_DV), lambda b, h: (b, h, 0, 0, 0)),
                   pl.BlockSpec((None, None, n, A_CHUNK, A_CHUNK), lambda b, h: (b, h, 0, 0, 0))],
        out_shape=[jax.ShapeDtypeStruct((bn, s, A_VW), BF16),
                   jax.ShapeDtypeStruct((bn, s, A_VW), F32),
                   jax.ShapeDtypeStruct((bn, A_HEADS, n, A_DK, A_DV), F32),
                   jax.ShapeDtypeStruct((bn, A_HEADS, n, A_CHUNK, A_CHUNK), F32)],
        scratch_shapes=[pltpu.VMEM((A_DK, A_DV), F32)],
        compiler_params=_params(("parallel", "parallel"), VMEM_BIG),
    )(proj_hm, cw_hm, beta, gc, norm_g)


def _gdn_bwd(proj_hm, cw_hm, beta, gc, norm_g, oraw, states, t_mats, dog):
    bn, s, _ = proj_hm.shape
    n = s // A_CHUNK

    def body(x_ref, cw_ref, beta_ref, gc_ref, ng_ref, oraw_ref, st_ref, t_ref, dog_ref,
             dx_ref, dgc_ref, dbeta_ref, dcw_ref, dng_ref, dstate, dy_next):
        ri = lax.broadcasted_iota(jnp.int32, (A_CHUNK, A_CHUNK), 0)
        ci = lax.broadcasted_iota(jnp.int32, (A_CHUNK, A_CHUNK), 1)
        cw = cw_ref[...]
        ng = ng_ref[...]
        dstate[...] = jnp.zeros_like(dstate)
        dy_next[...] = jnp.zeros_like(dy_next)
        dcw_ref[...] = jnp.zeros_like(dcw_ref)
        dng_ref[...] = jnp.zeros_like(dng_ref)

        def chunk(i, first):
            rows = pl.ds(0 if first else pl.multiple_of(i * A_CHUNK, A_CHUNK), A_CHUNK)
            cin = _gdn_chunk_inputs(x_ref, cw, i, first)
            q, k, v = cin["q"], cin["k"], cin["v"]
            g_row = gc_ref[pl.ds(i, 1), :]
            b_row = beta_ref[pl.ds(i, 1), :]
            cr = _gdn_chunk_core(q, k, v, g_row, b_row, t_ref[i], ri, ci)
            eye, dec, gam, e = cr["eye"], cr["dec"], cr["gam"], cr["e"]
            b_col, t_mat, u, w, p = cr["b_col"], cr["t_mat"], cr["u"], cr["w"], cr["p"]
            st = st_ref[i]
            ds_out = dstate[...]

            o = oraw_ref[rows, :]
            z = x_ref[rows, A_CONV_COLS:A_HEAD_COLS]
            d_og = dog_ref[rows, :]
            r = lax.rsqrt(jnp.mean(o * o, axis=1, keepdims=True) + EPS)
            oh = o * r
            d_on = d_og * _silu(z)
            dz = d_og * oh * ng * _dsilu(z)
            dng_ref[0:1, :] += jnp.sum(d_on * oh, axis=0, keepdims=True)
            d_oh = d_on * ng
            d_o = r * (d_oh - oh * jnp.mean(d_oh * oh, axis=1, keepdims=True))

            vn = u - _dot(w, st, 1, 0)
            d_vn = _dot(p, d_o, 0, 0) + _dot(cr["kd"], ds_out, 1, 0)
            d_p = jnp.where(cr["causal"], _dot(d_o, vn, 1, 1), 0.0)
            d_qg = _dot(d_o, st, 1, 1)
            d_kd = _dot(vn, ds_out, 1, 1)
            d_gam_last = jnp.sum(jnp.sum(st * ds_out, axis=1, keepdims=True), axis=0, keepdims=True)
            d_w = -_dot(d_vn, st, 1, 1)
            dstate[...] = _dot(cr["qg"], d_o, 0, 0) + ds_out * cr["gam_last"] - _dot(w, d_vn, 0, 0)
            d_bv = _dot(t_mat, d_vn, 0, 0)
            d_kbg = _dot(t_mat, d_w, 0, 0)
            d_a = jnp.where(cr["strict"], -(_dot(d_bv, u, 1, 1) + _dot(d_kbg, w, 1, 1)), 0.0)
            m_a = d_a * dec
            n_p = d_p * dec
            d_kb = _dot(m_a, k, 1, 0) + d_kbg * gam
            d_q = _dot(n_p, k, 1, 0) + d_qg * gam
            d_k = (_dot(m_a, cr["kb"], 0, 0) + _dot(n_p, q, 0, 0) + d_kd * e + d_kb * b_col)
            d_v = d_bv * b_col
            d_beta_col = (jnp.sum(d_bv * v, axis=1, keepdims=True)
                          + jnp.sum(d_kb * k, axis=1, keepdims=True))
            gterm = d_a * cr["a_mat"] + d_p * p
            d_e = jnp.sum(d_kd * k, axis=1, keepdims=True) * e
            d_g_col = (jnp.sum(gterm, axis=1, keepdims=True)
                       + (jnp.sum(d_qg * q, axis=1, keepdims=True)
                          + jnp.sum(d_kbg * cr["kb"], axis=1, keepdims=True)) * gam
                       - d_e)
            d_g_last = jnp.sum(d_e, axis=0, keepdims=True) + d_gam_last * cr["gam_last"]
            lane = lax.broadcasted_iota(jnp.int32, (1, A_CHUNK), 1)
            d_g_row = (_col_to_row(d_g_col, eye) - jnp.sum(gterm, axis=0, keepdims=True)
                       + jnp.where(lane == A_CHUNK - 1, d_g_last, 0.0))
            dgc_ref[pl.ds(i, 1), :] = d_g_row
            dbeta_ref[pl.ds(i, 1), :] = _col_to_row(d_beta_col, eye)

            qh = cin["aq"] * cin["rq"]
            kh = cin["ak"] * cin["rk"]
            d_qh = d_q * (A_DK ** -0.5)
            d_aq = cin["rq"] * (d_qh - qh * jnp.sum(d_qh * qh, axis=1, keepdims=True))
            d_ak = cin["rk"] * (d_k - kh * jnp.sum(d_k * kh, axis=1, keepdims=True))
            d_y = jnp.concatenate([d_aq, d_ak, d_v], axis=1) * _dsilu(cin["y"])
            xw = cin["xw"]
            dyw = jnp.concatenate([d_y, dy_next[...]], axis=0)
            d_x = cw[A_CONV - 1:A_CONV, :] * dyw
            for j in range(1, A_CONV):
                d_x = d_x + cw[A_CONV - 1 - j:A_CONV - j, :] * pltpu.roll(dyw, A_CHUNK + HALO - j, 0)
            d_x = d_x[0:A_CHUNK, :]
            dy_pad = jnp.concatenate([jnp.zeros((HALO, A_CONV_COLS), F32), d_y], axis=0)
            for j in range(A_CONV):
                xs = xw if j == 0 else pltpu.roll(xw, j, 0)
                dcw_ref[A_CONV - 1 - j:A_CONV - j, :] += jnp.sum(dy_pad * xs, axis=0, keepdims=True)
            dy_next[...] = d_y[0:HALO, :]
            dx_ref[rows, 0:A_CONV_COLS] = d_x.astype(BF16)
            dx_ref[rows, A_CONV_COLS:A_HEAD_COLS] = dz.astype(BF16)

        lax.fori_loop(0, n - 1, lambda i, c: (chunk(n - 1 - i, False), c)[1], 0)
        chunk(0, True)

    hn = lambda b, h: (b, h, 0, 0)
    return pl.pallas_call(
        body, name="gdn_bwd", grid=(bn, A_HEADS),
        in_specs=[pl.BlockSpec((None, s, A_HEAD_COLS), lambda b, h: (b, 0, h)),
                  pl.BlockSpec((A_CONV, A_CONV_COLS), lambda b, h: (0, h)),
                  pl.BlockSpec((None, None, n, A_CHUNK), hn),
                  pl.BlockSpec((None, None, n, A_CHUNK), hn),
                  pl.BlockSpec((1, A_DV), lambda b, h: (0, 0)),
                  pl.BlockSpec((None, s, A_DV), lambda b, h: (b, 0, h)),
                  pl.BlockSpec((None, None, n, A_DK, A_DV), lambda b, h: (b, h, 0, 0, 0)),
                  pl.BlockSpec((None, None, n, A_CHUNK, A_CHUNK), lambda b, h: (b, h, 0, 0, 0)),
                  pl.BlockSpec((None, s, A_DV), lambda b, h: (b, 0, h))],
        out_specs=[pl.BlockSpec((None, s, A_HEAD_COLS), lambda b, h: (b, 0, h)),
                   pl.BlockSpec((None, None, n, A_CHUNK), hn),
                   pl.BlockSpec((None, None, n, A_CHUNK), hn),
                   pl.BlockSpec((None, A_CONV, A_CONV_COLS), lambda b, h: (b, 0, h)),
                   pl.BlockSpec((None, None, 8, A_DV), hn)],
        out_shape=[jax.ShapeDtypeStruct((bn, s, A_HEADS * A_HEAD_COLS), BF16),
                   jax.ShapeDtypeStruct((bn, A_HEADS, n, A_CHUNK), F32),
                   jax.ShapeDtypeStruct((bn, A_HEADS, n, A_CHUNK), F32),
                   jax.ShapeDtypeStruct((bn, A_CONV, A_HEADS * A_CONV_COLS), F32),
                   jax.ShapeDtypeStruct((bn, A_HEADS, 8, A_DV), F32)],
        scratch_shapes=[pltpu.VMEM((A_DK, A_DV), F32), pltpu.VMEM((HALO, A_CONV_COLS), F32)],
        compiler_params=_params(("parallel", "parallel"), VMEM_BIG),
    )(proj_hm, cw_hm, beta, gc, norm_g, oraw, states, t_mats, dog)


A_SEQ_BLK = 512
A_BLK_CHUNKS = A_SEQ_BLK // A_CHUNK


def _gdn_halo(proj_hm):
    bn, s, w = proj_hm.shape
    last = proj_hm.reshape(bn, s // A_SEQ_BLK, A_SEQ_BLK, w)[:, :, A_SEQ_BLK - HALO:, :]
    return jnp.concatenate([jnp.zeros((bn, 1, HALO, w), proj_hm.dtype), last[:, :-1]], axis=1)


def _gdn_window(x_ref, halo_ref, ci, first, lo):
    if first:
        return jnp.concatenate([halo_ref[:, lo:lo + A_CONV_COLS], x_ref[0:A_CHUNK, lo:lo + A_CONV_COLS]], axis=0)
    start = pl.multiple_of(ci * A_CHUNK - HALO, HALO)
    return x_ref[pl.ds(start, A_CHUNK + HALO), lo:lo + A_CONV_COLS]


def _gdn_chunk_prep(xw, cw):
    y = _conv_taps(xw, cw)
    a = _silu(y)
    aq, ak, v = a[:, 0:A_DK], a[:, A_DK:2 * A_DK], a[:, 2 * A_DK:]
    rq = lax.rsqrt(jnp.sum(aq * aq, axis=1, keepdims=True) + EPS)
    rk = lax.rsqrt(jnp.sum(ak * ak, axis=1, keepdims=True) + EPS)
    return dict(xw=xw, y=y, aq=aq, ak=ak, rq=rq, rk=rk, q=aq * rq * (A_DK ** -0.5), k=ak * rk, v=v)


def _gdn_fwd(proj_hm, cw_hm, beta, gc, norm_g, hp=4):
    bn, s, _ = proj_hm.shape
    n = s // A_CHUNK
    nsb = s // A_SEQ_BLK
    halo = _gdn_halo(proj_hm)

    def body(x_ref, halo_ref, cw_ref, beta_ref, gc_ref, ng_ref, og_ref, oraw_ref, st_ref, t_ref, state):
        ri = lax.broadcasted_iota(jnp.int32, (A_CHUNK, A_CHUNK), 0)
        ci_ = lax.broadcasted_iota(jnp.int32, (A_CHUNK, A_CHUNK), 1)
        ng = ng_ref[...]

        @pl.when(pl.program_id(2) == 0)
        def _():
            state[...] = jnp.zeros_like(state)

        def one_head(hh, ci, first, rows):
            lo = hh * A_HEAD_COLS
            cw = cw_ref[:, hh * A_CONV_COLS:(hh + 1) * A_CONV_COLS]
            cin = _gdn_chunk_prep(_gdn_window(x_ref, halo_ref, ci, first, lo), cw)
            core = yield from _gdn_chunk_core(cin["q"], cin["k"], cin["v"], gc_ref[hh, pl.ds(ci, 1), :],
                                              beta_ref[hh, pl.ds(ci, 1), :], None, ri, ci_)
            st = state[hh]
            st_ref[hh, ci] = st
            t_ref[hh, ci] = core["t_mat"]
            vn = core["u"] - _dot(core["w"], st, 1, 0)
            qs = _dot(core["qg"], st, 1, 0)
            yield
            o = qs + _dot(core["p"], vn, 1, 0)
            state[hh] = st * core["gam_last"] + _dot(core["kd"], vn, 0, 0)
            yield
            ocols = slice(hh * A_DV, (hh + 1) * A_DV)
            oraw_ref[rows, ocols] = o
            r = lax.rsqrt(jnp.mean(o * o, axis=1, keepdims=True) + EPS)
            z = x_ref[rows, lo + A_CONV_COLS:lo + A_HEAD_COLS]
            og_ref[rows, ocols] = (o * r * ng * _silu(z)).astype(BF16)

        def chunk(ci, first):
            rows = pl.ds(0 if first else pl.multiple_of(ci * A_CHUNK, A_CHUNK), A_CHUNK)
            _round_robin([one_head(hh, ci, first, rows) for hh in range(hp)])

        chunk(0, True)
        lax.fori_loop(1, A_BLK_CHUNKS, lambda i, c: (chunk(i, False), c)[1], 0)

    small = pl.BlockSpec((None, hp, A_BLK_CHUNKS, A_CHUNK), lambda b, h, j: (b, h, j, 0))
    return pl.pallas_call(
        body, name="gdn_fwd", grid=(bn, A_HEADS // hp, nsb),
        in_specs=[pl.BlockSpec((None, A_SEQ_BLK, hp * A_HEAD_COLS), lambda b, h, j: (b, j, h)),
                  pl.BlockSpec((None, None, HALO, hp * A_HEAD_COLS), lambda b, h, j: (b, j, 0, h)),
                  pl.BlockSpec((A_CONV, hp * A_CONV_COLS), lambda b, h, j: (0, h)),
                  small, small,
                  pl.BlockSpec((1, A_DV), lambda b, h, j: (0, 0))],
        out_specs=[pl.BlockSpec((None, A_SEQ_BLK, hp * A_DV), lambda b, h, j: (b, j, h)),
                   pl.BlockSpec((None, A_SEQ_BLK, hp * A_DV), lambda b, h, j: (b, j, h)),
                   pl.BlockSpec((None, hp, A_BLK_CHUNKS, A_DK, A_DV), lambda b, h, j: (b, h, j, 0, 0)),
                   pl.BlockSpec((None, hp, A_BLK_CHUNKS, A_CHUNK, A_CHUNK), lambda b, h, j: (b, h, j, 0, 0))],
        out_shape=[jax.ShapeDtypeStruct((bn, s, A_VW), BF16),
                   jax.ShapeDtypeStruct((bn, s, A_VW), F32),
                   jax.ShapeDtypeStruct((bn, A_HEADS, n, A_DK, A_DV), F32),
                   jax.ShapeDtypeStruct((bn, A_HEADS, n, A_CHUNK, A_CHUNK), F32)],
        scratch_shapes=[pltpu.VMEM((hp, A_DK, A_DV), F32)],
        compiler_params=_params(("parallel", "parallel", "arbitrary"), VMEM_BIG),
    )(proj_hm, halo, cw_hm, beta, gc, norm_g)


def _gdn_bwd(proj_hm, cw_hm, beta, gc, norm_g, oraw, states, t_mats, dog, hp=4):
    bn, s, _ = proj_hm.shape
    n = s // A_CHUNK
    nsb = s // A_SEQ_BLK
    halo = _gdn_halo(proj_hm)

    def body(x_ref, halo_ref, cw_ref, beta_ref, gc_ref, ng_ref, oraw_ref, st_ref, t_ref, dog_ref,
             dx_ref, dgc_ref, dbeta_ref, dcw_ref, dng_ref, dstate, dy_next):
        ri = lax.broadcasted_iota(jnp.int32, (A_CHUNK, A_CHUNK), 0)
        ci_ = lax.broadcasted_iota(jnp.int32, (A_CHUNK, A_CHUNK), 1)
        lane = lax.broadcasted_iota(jnp.int32, (1, A_CHUNK), 1)
        ng = ng_ref[...]

        @pl.when(pl.program_id(2) == 0)
        def _():
            dstate[...] = jnp.zeros_like(dstate)
            dy_next[...] = jnp.zeros_like(dy_next)
            dcw_ref[...] = jnp.zeros_like(dcw_ref)
            dng_ref[...] = jnp.zeros_like(dng_ref)

        def one_head(hh, ci, first, rows):
            lo = hh * A_HEAD_COLS
            ccols = slice(hh * A_CONV_COLS, (hh + 1) * A_CONV_COLS)
            ocols = slice(hh * A_DV, (hh + 1) * A_DV)
            cw = cw_ref[:, ccols]
            cin = _gdn_chunk_prep(_gdn_window(x_ref, halo_ref, ci, first, lo), cw)
            q, k, v = cin["q"], cin["k"], cin["v"]
            cr = yield from _gdn_chunk_core(q, k, v, gc_ref[hh, pl.ds(ci, 1), :], beta_ref[hh, pl.ds(ci, 1), :],
                                            t_ref[hh, ci], ri, ci_)
            eye, dec, gam, e = cr["eye"], cr["dec"], cr["gam"], cr["e"]
            b_col, t_mat, u, w, p = cr["b_col"], cr["t_mat"], cr["u"], cr["w"], cr["p"]
            st = st_ref[hh, ci]
            ds_out = dstate[hh]

            o = oraw_ref[rows, ocols]
            z = x_ref[rows, lo + A_CONV_COLS:lo + A_HEAD_COLS]
            d_og = dog_ref[rows, ocols]
            r = lax.rsqrt(jnp.mean(o * o, axis=1, keepdims=True) + EPS)
            oh = o * r
            d_on = d_og * _silu(z)
            dz = d_og * oh * ng * _dsilu(z)
            dng_ref[hh, 0:1, :] += jnp.sum(d_on * oh, axis=0, keepdims=True)
            d_oh = d_on * ng
            d_o = r * (d_oh - oh * jnp.mean(d_oh * oh, axis=1, keepdims=True))

            vn = u - _dot(w, st, 1, 0)
            d_vn = _dot(p, d_o, 0, 0) + _dot(cr["kd"], ds_out, 1, 0)
            d_qg = _dot(d_o, st, 1, 1)
            qgdo = _dot(cr["qg"], d_o, 0, 0)
            yield
            d_p = jnp.where(cr["causal"], _dot(d_o, vn, 1, 1), 0.0)
            d_kd = _dot(vn, ds_out, 1, 1)
            d_gam_last = jnp.sum(jnp.sum(st * ds_out, axis=1, keepdims=True), axis=0, keepdims=True)
            d_w = -_dot(d_vn, st, 1, 1)
            dstate[hh] = qgdo + ds_out * cr["gam_last"] - _dot(w, d_vn, 0, 0)
            d_bv = _dot(t_mat, d_vn, 0, 0)
            yield
            d_kbg = _dot(t_mat, d_w, 0, 0)
            n_p = d_p * dec
            d_q = _dot(n_p, k, 1, 0) + d_qg * gam
            npq = _dot(n_p, q, 0, 0)
            yield
            d_a = jnp.where(cr["strict"], -(_dot(d_bv, u, 1, 1) + _dot(d_kbg, w, 1, 1)), 0.0)
            yield
            m_a = d_a * dec
            d_kb = _dot(m_a, k, 1, 0) + d_kbg * gam
            d_k = (_dot(m_a, cr["kb"], 0, 0) + npq + d_kd * e + d_kb * b_col)
            yield
            d_v = d_bv * b_col
            d_beta_col = (jnp.sum(d_bv * v, axis=1, keepdims=True)
                          + jnp.sum(d_kb * k, axis=1, keepdims=True))
            gterm = d_a * cr["a_mat"] + d_p * p
            d_e = jnp.sum(d_kd * k, axis=1, keepdims=True) * e
            d_g_col = (jnp.sum(gterm, axis=1, keepdims=True)
                       + (jnp.sum(d_qg * q, axis=1, keepdims=True)
                          + jnp.sum(d_kbg * cr["kb"], axis=1, keepdims=True)) * gam
                       - d_e)
            d_g_last = jnp.sum(d_e, axis=0, keepdims=True) + d_gam_last * cr["gam_last"]
            d_g_row = (_col_to_row(d_g_col, eye) - jnp.sum(gterm, axis=0, keepdims=True)
                       + jnp.where(lane == A_CHUNK - 1, d_g_last, 0.0))
            dgc_ref[hh, pl.ds(ci, 1), :] = d_g_row
            dbeta_ref[hh, pl.ds(ci, 1), :] = _col_to_row(d_beta_col, eye)

            qh = cin["aq"] * cin["rq"]
            kh = cin["ak"] * cin["rk"]
            d_qh = d_q * (A_DK ** -0.5)
            d_aq = cin["rq"] * (d_qh - qh * jnp.sum(d_qh * qh, axis=1, keepdims=True))
            d_ak = cin["rk"] * (d_k - kh * jnp.sum(d_k * kh, axis=1, keepdims=True))
            d_y = jnp.concatenate([d_aq, d_ak, d_v], axis=1) * _dsilu(cin["y"])
            xw = cin["xw"]
            dyw = jnp.concatenate([d_y, dy_next[hh]], axis=0)
            d_x = cw[A_CONV - 1:A_CONV, :] * dyw
            for j in range(1, A_CONV):
                d_x = d_x + cw[A_CONV - 1 - j:A_CONV - j, :] * pltpu.roll(dyw, A_CHUNK + HALO - j, 0)
            dy_pad = jnp.concatenate([jnp.zeros((HALO, A_CONV_COLS), F32), d_y], axis=0)
            for j in range(A_CONV):
                xs = xw if j == 0 else pltpu.roll(xw, j, 0)
                dcw_ref[A_CONV - 1 - j:A_CONV - j, ccols] += jnp.sum(dy_pad * xs, axis=0, keepdims=True)
            dy_next[hh] = d_y[0:HALO, :]
            dx_ref[rows, lo:lo + A_CONV_COLS] = d_x[0:A_CHUNK, :].astype(BF16)
            dx_ref[rows, lo + A_CONV_COLS:lo + A_HEAD_COLS] = dz.astype(BF16)

        def chunk(ci, first):
            rows = pl.ds(0 if first else pl.multiple_of(ci * A_CHUNK, A_CHUNK), A_CHUNK)
            _round_robin([one_head(hh, ci, first, rows) for hh in range(hp)])

        lax.fori_loop(0, A_BLK_CHUNKS - 1, lambda i, c: (chunk(A_BLK_CHUNKS - 1 - i, False), c)[1], 0)
        chunk(0, True)

    rev = lambda j: nsb - 1 - j
    small = pl.BlockSpec((None, hp, A_BLK_CHUNKS, A_CHUNK), lambda b, h, j: (b, h, rev(j), 0))
    wide = pl.BlockSpec((None, A_SEQ_BLK, hp * A_HEAD_COLS), lambda b, h, j: (b, rev(j), h))
    val = pl.BlockSpec((None, A_SEQ_BLK, hp * A_DV), lambda b, h, j: (b, rev(j), h))
    return pl.pallas_call(
        body, name="gdn_bwd", grid=(bn, A_HEADS // hp, nsb),
        in_specs=[wide,
                  pl.BlockSpec((None, None, HALO, hp * A_HEAD_COLS), lambda b, h, j: (b, rev(j), 0, h)),
                  pl.BlockSpec((A_CONV, hp * A_CONV_COLS), lambda b, h, j: (0, h)),
                  small, small,
                  pl.BlockSpec((1, A_DV), lambda b, h, j: (0, 0)),
                  val,
                  pl.BlockSpec((None, hp, A_BLK_CHUNKS, A_DK, A_DV), lambda b, h, j: (b, h, rev(j), 0, 0)),
                  pl.BlockSpec((None, hp, A_BLK_CHUNKS, A_CHUNK, A_CHUNK), lambda b, h, j: (b, h, rev(j), 0, 0)),
                  val],
        out_specs=[wide, small, small,
                   pl.BlockSpec((None, A_CONV, hp * A_CONV_COLS), lambda b, h, j: (b, 0, h)),
                   pl.BlockSpec((None, hp, 8, A_DV), lambda b, h, j: (b, h, 0, 0))],
        out_shape=[jax.ShapeDtypeStruct((bn, s, A_HEADS * A_HEAD_COLS), BF16),
                   jax.ShapeDtypeStruct((bn, A_HEADS, n, A_CHUNK), F32),
                   jax.ShapeDtypeStruct((bn, A_HEADS, n, A_CHUNK), F32),
                   jax.ShapeDtypeStruct((bn, A_CONV, A_HEADS * A_CONV_COLS), F32),
                   jax.ShapeDtypeStruct((bn, A_HEADS, 8, A_DV), F32)],
        scratch_shapes=[pltpu.VMEM((hp, A_DK, A_DV), F32), pltpu.VMEM((hp, HALO, A_CONV_COLS), F32)],
        compiler_params=_params(("parallel", "parallel", "arbitrary"), VMEM_BIG),
    )(proj_hm, halo, cw_hm, beta, gc, norm_g, oraw, states, t_mats, dog)


def _rope_tables(posf, inv_freq_row):
    t = posf.shape[0]
    tm = 512

    def body(p_ref, f_ref, c_ref, sa_ref, sb_ref):
        ang = p_ref[...] * f_ref[...]
        lane = lax.broadcasted_iota(jnp.int32, ang.shape, 1)
        half = ROPE_DIMS // 2
        c_ref[...] = jnp.where(lane < ROPE_DIMS, jnp.cos(ang), 1.0)
        sn = jnp.sin(ang)
        sa_ref[...] = jnp.where(lane < half, -sn, 0.0)
        sb_ref[...] = jnp.where((lane >= half) & (lane < ROPE_DIMS), sn, 0.0)

    row = pl.BlockSpec((tm, 128), lambda i: (i, 0))
    return pl.pallas_call(
        body, name="rope_tables", grid=(t // tm,),
        in_specs=[row, pl.BlockSpec((1, 128), lambda i: (0, 0))], out_specs=[row] * 3,
        out_shape=[jax.ShapeDtypeStruct((t, 128), F32)] * 3,
        compiler_params=_params(("parallel",)),
    )(posf, inv_freq_row)


def _rope(x, c, sa, sb):
    half = ROPE_DIMS // 2
    return x * c + pltpu.roll(x, 128 - half, 1) * sa + pltpu.roll(x, half, 1) * sb


def _rope_t(d, c, sa, sb):
    half = ROPE_DIMS // 2
    return d * c + pltpu.roll(d * sa, half, 1) + pltpu.roll(d * sb, 128 - half, 1)


def _qk_prep(proj, c, sa, sb, qg, kg, name, tm=256):
    t = proj.shape[0]
    wide = proj.shape[1]

    def body(x_ref, c_ref, sa_ref, sb_ref, qg_ref, kg_ref, o_ref):
        cc, s1, s2 = c_ref[...], sa_ref[...], sb_ref[...]
        for which, g_ref in ((0, qg_ref), (1, kg_ref)):
            g = g_ref[...]
            for h in range(B_HEADS):
                lo = which * B_W + h * B_DH
                xv = x_ref[:, lo:lo + B_DH]
                r = lax.rsqrt(jnp.mean(xv * xv, axis=1, keepdims=True) + EPS)
                o_ref[:, lo:lo + B_DH] = _rope(xv * r * g, cc, s1, s2).astype(BF16)
        o_ref[:, 2 * B_W:3 * B_W] = x_ref[:, 2 * B_W:3 * B_W].astype(BF16)

    tab = pl.BlockSpec((tm, 128), lambda i: (i, 0))
    gain = pl.BlockSpec((1, B_DH), lambda i: (0, 0))
    return pl.pallas_call(
        body, name=name, grid=(t // tm,),
        in_specs=[pl.BlockSpec((tm, wide), lambda i: (i, 0)), tab, tab, tab, gain, gain],
        out_specs=pl.BlockSpec((tm, 3 * B_W), lambda i: (i, 0)),
        out_shape=jax.ShapeDtypeStruct((t, 3 * B_W), BF16),
        compiler_params=_params(("parallel",), 40 * 1024 * 1024),
    )(proj, c, sa, sb, qg, kg)


def _qk_prep_bwd(proj, c, sa, sb, qg, kg, dq, dk, dv, dz, name, tm=256):
    t = proj.shape[0]
    wide = proj.shape[1]
    out_w = 3 * B_W + (B_W if dz is not None else 0)

    def body(*refs):
        x_ref, c_ref, sa_ref, sb_ref, qg_ref, kg_ref, dq_ref, dk_ref, dv_ref = refs[:9]
        if dz is not None:
            dz_ref, o_ref, dgain_ref = refs[9:]
        else:
            o_ref, dgain_ref = refs[9:]
        i = pl.program_id(0)

        @pl.when(i == 0)
        def _():
            dgain_ref[...] = jnp.zeros_like(dgain_ref)

        cc, s1, s2 = c_ref[...], sa_ref[...], sb_ref[...]
        for which, g_ref, d_ref in ((0, qg_ref, dq_ref), (1, kg_ref, dk_ref)):
            g = g_ref[...]
            acc = jnp.zeros((1, B_DH), F32)
            for h in range(B_HEADS):
                lo = which * B_W + h * B_DH
                xv = x_ref[:, lo:lo + B_DH]
                r = lax.rsqrt(jnp.mean(xv * xv, axis=1, keepdims=True) + EPS)
                xh = xv * r
                d_xn = _rope_t(d_ref[:, h * B_DH:(h + 1) * B_DH], cc, s1, s2)
                acc = acc + jnp.sum(d_xn * xh, axis=0, keepdims=True)
                d_xh = d_xn * g
                d_x = r * (d_xh - xh * jnp.mean(d_xh * xh, axis=1, keepdims=True))
                o_ref[:, lo:lo + B_DH] = d_x.astype(BF16)
            dgain_ref[which:which + 1, :] += acc
        o_ref[:, 2 * B_W:3 * B_W] = dv_ref[...].astype(BF16)
        if dz is not None:
            o_ref[:, 3 * B_W:4 * B_W] = dz_ref[...]

    tab = pl.BlockSpec((tm, 128), lambda i: (i, 0))
    gain = pl.BlockSpec((1, B_DH), lambda i: (0, 0))
    grad = pl.BlockSpec((tm, B_W), lambda i: (i, 0))
    in_specs = [pl.BlockSpec((tm, wide), lambda i: (i, 0)), tab, tab, tab, gain, gain, grad, grad, grad]
    args = [proj, c, sa, sb, qg, kg, dq, dk, dv]
    if dz is not None:
        in_specs.append(grad)
        args.append(dz)
    return pl.pallas_call(
        body, name=name, grid=(t // tm,), in_specs=in_specs,
        out_specs=[pl.BlockSpec((tm, out_w), lambda i: (i, 0)), pl.BlockSpec((8, B_DH), lambda i: (0, 0))],
        out_shape=[jax.ShapeDtypeStruct((t, out_w), BF16), jax.ShapeDtypeStruct((8, B_DH), F32)],
        compiler_params=_params(("arbitrary",), 40 * 1024 * 1024),
    )(*args)


def _attn_masks():
    qi = lax.broadcasted_iota(jnp.int32, (B_BLK, 2 * B_BLK), 0)
    kj = lax.broadcasted_iota(jnp.int32, (B_BLK, 2 * B_BLK), 1)
    two = (kj >= qi) & (kj <= qi + B_BLK)
    q1 = lax.broadcasted_iota(jnp.int32, (B_BLK, B_BLK), 0)
    k1 = lax.broadcasted_iota(jnp.int32, (B_BLK, B_BLK), 1)
    return k1 <= q1, two


def _lane_pick(ref_rows, h):
    lane = lax.broadcasted_iota(jnp.int32, ref_rows.shape, 1)
    return jnp.sum(jnp.where(lane == h, ref_rows, 0.0), axis=1, keepdims=True)


def _attn_fwd(qkv, name):
    ns, ln, _ = qkv.shape
    nb = ln // B_BLK
    scale = B_DH ** -0.5

    def body(q_ref, k_ref, v_ref, o_ref, lse_ref):
        h = pl.program_id(1)
        mask1, mask2 = _attn_masks()

        @pl.when(h == 0)
        def _():
            lse_ref[...] = jnp.zeros_like(lse_ref)

        def block(i, first):
            rows = pl.ds(pl.multiple_of(i * B_BLK, B_BLK), B_BLK)
            if first:
                win, mask = pl.ds(0, B_BLK), mask1
            else:
                win, mask = pl.ds(pl.multiple_of((i - 1) * B_BLK, B_BLK), 2 * B_BLK), mask2
            sc = jnp.where(mask, _dot(q_ref[rows, :], k_ref[win, :], 1, 1) * scale, -1e30)
            m = jnp.max(sc, axis=1, keepdims=True)
            p = jnp.exp(sc - m)
            l = jnp.sum(p, axis=1, keepdims=True)
            o_ref[rows, :] = _dot(p, v_ref[win, :], 1, 0) / l
            lane = lax.broadcasted_iota(jnp.int32, (B_BLK, B_HEADS), 1)
            lse_ref[rows, :] = jnp.where(lane == h, m + jnp.log(l), lse_ref[rows, :])

        block(0, True)
        if nb > 1:
            lax.fori_loop(1, nb, lambda i, c: (block(i, False), c)[1], 0)

    return pl.pallas_call(
        body, name=name, grid=(ns, B_HEADS),
        in_specs=[pl.BlockSpec((None, ln, B_DH), lambda s, h: (s, 0, h)),
                  pl.BlockSpec((None, ln, B_DH), lambda s, h: (s, 0, B_HEADS + h)),
                  pl.BlockSpec((None, ln, B_DH), lambda s, h: (s, 0, 2 * B_HEADS + h))],
        out_specs=[pl.BlockSpec((None, ln, B_DH), lambda s, h: (s, 0, h)),
                   pl.BlockSpec((None, ln, B_HEADS), lambda s, h: (s, 0, 0))],
        out_shape=[jax.ShapeDtypeStruct((ns, ln, B_W), F32), jax.ShapeDtypeStruct((ns, ln, B_HEADS), F32)],
        compiler_params=_params(("parallel", "arbitrary")),
    )(qkv, qkv, qkv)


def _attn_bwd(qkv, d_o, lse_joint, delta, name):
    ns, ln, _ = qkv.shape
    nb = ln // B_BLK
    scale = B_DH ** -0.5

    def body(q_ref, k_ref, v_ref, do_ref, lj_ref, dl_ref, dq_ref, dk_ref, dv_ref):
        h = pl.program_id(1)
        mask1, mask2 = _attn_masks()
        dk_ref[...] = jnp.zeros_like(dk_ref)
        dv_ref[...] = jnp.zeros_like(dv_ref)

        def block(i, first):
            rows = pl.ds(pl.multiple_of(i * B_BLK, B_BLK), B_BLK)
            if first:
                win, mask = pl.ds(0, B_BLK), mask1
            else:
                win, mask = pl.ds(pl.multiple_of((i - 1) * B_BLK, B_BLK), 2 * B_BLK), mask2
            q = q_ref[rows, :]
            d_out = do_ref[rows, :]
            l_col = _lane_pick(lj_ref[rows, :], h)
            d_col = _lane_pick(dl_ref[rows, :], h)
            sc = _dot(q, k_ref[win, :], 1, 1) * scale
            p = jnp.exp(jnp.where(mask, sc - l_col, -1e30))
            d_p = _dot(d_out, v_ref[win, :], 1, 1)
            d_s = p * (d_p - d_col) * scale
            dq_ref[rows, :] = _dot(d_s, k_ref[win, :], 1, 0)
            dk_ref[win, :] += _dot(d_s, q, 0, 0)
            dv_ref[win, :] += _dot(p, d_out, 0, 0)

        block(0, True)
        if nb > 1:
            lax.fori_loop(1, nb, lambda i, c: (block(i, False), c)[1], 0)

    head = lambda off: pl.BlockSpec((None, ln, B_DH), lambda s, h: (s, 0, off + h))
    small = pl.BlockSpec((None, ln, B_HEADS), lambda s, h: (s, 0, 0))
    return pl.pallas_call(
        body, name=name, grid=(ns, B_HEADS),
        in_specs=[head(0), head(B_HEADS), head(2 * B_HEADS), head(0), small, small],
        out_specs=[head(0)] * 3,
        out_shape=[jax.ShapeDtypeStruct((ns, ln, B_W), F32)] * 3,
        compiler_params=_params(("parallel", "parallel")),
    )(qkv, qkv, qkv, d_o, lse_joint, delta)


B_ROWS = 2048


def _attn_schedule(nb, sb, block):
    for si in range(sb):
        block(si, 0, True)
    if nb == 1:
        return
    per = 1 if sb > 1 else 2
    lead = 1 + (nb - 1) % per
    for i in range(1, lead):
        for si in range(sb):
            block(si, i, False)

    def step(it, carry):
        for u in range(per):
            for si in range(sb):
                block(si, lead + it * per + u, False)
        return carry

    lax.fori_loop(0, (nb - lead) // per, step, 0)


def _attn_rows(i, first):
    if first:
        return pl.ds(0, B_BLK), pl.ds(0, B_BLK)
    rows = pl.ds(pl.multiple_of(i * B_BLK, B_BLK), B_BLK)
    return rows, pl.ds(pl.multiple_of((i - 1) * B_BLK, B_BLK), 2 * B_BLK)


def _attn_fwd(qkv, name):
    ns, ln, _ = qkv.shape
    nb = ln // B_BLK
    sb = B_ROWS // ln
    scale = B_DH ** -0.5

    def body(q_ref, k_ref, v_ref, o_ref, lse_ref):
        h = pl.program_id(1)
        mask1, mask2 = _attn_masks()
        lane = lax.broadcasted_iota(jnp.int32, (B_BLK, B_HEADS), 1)

        @pl.when(h == 0)
        def _():
            lse_ref[...] = jnp.zeros_like(lse_ref)

        def block(si, i, first):
            rows, win = _attn_rows(i, first)
            mask = mask1 if first else mask2
            sc = jnp.where(mask, _dot(q_ref[si, rows, :], k_ref[si, win, :], 1, 1) * scale, -1e30)
            m = jnp.max(sc, axis=1, keepdims=True)
            p = jnp.exp(sc - m)
            l = jnp.sum(p, axis=1, keepdims=True)
            o_ref[si, rows, :] = _dot(p, v_ref[si, win, :], 1, 0) / l
            lse_ref[si, rows, :] = jnp.where(lane == h, m + jnp.log(l), lse_ref[si, rows, :])

        _attn_schedule(nb, sb, block)

    head = lambda off: pl.BlockSpec((sb, ln, B_DH), lambda s, h: (s, 0, off + h))
    return pl.pallas_call(
        body, name=name, grid=(ns // sb, B_HEADS),
        in_specs=[head(0), head(B_HEADS), head(2 * B_HEADS)],
        out_specs=[head(0), pl.BlockSpec((sb, ln, B_HEADS), lambda s, h: (s, 0, 0))],
        out_shape=[jax.ShapeDtypeStruct((ns, ln, B_W), F32), jax.ShapeDtypeStruct((ns, ln, B_HEADS), F32)],
        compiler_params=_params(("parallel", "arbitrary")),
    )(qkv, qkv, qkv)


def _attn_bwd(qkv, d_o, lse_joint, delta, name):
    ns, ln, _ = qkv.shape
    nb = ln // B_BLK
    sb = B_ROWS // ln
    scale = B_DH ** -0.5

    def body(q_ref, k_ref, v_ref, do_ref, lj_ref, dl_ref, dq_ref, dk_ref, dv_ref):
        h = pl.program_id(1)
        mask1, mask2 = _attn_masks()
        dk_ref[...] = jnp.zeros_like(dk_ref)
        dv_ref[...] = jnp.zeros_like(dv_ref)

        def block(si, i, first):
            rows, win = _attn_rows(i, first)
            mask = mask1 if first else mask2
            q = q_ref[si, rows, :]
            d_out = do_ref[si, rows, :]
            l_col = _lane_pick(lj_ref[si, rows, :], h)
            d_col = _lane_pick(dl_ref[si, rows, :], h)
            sc = _dot(q, k_ref[si, win, :], 1, 1) * scale
            p = jnp.exp(jnp.where(mask, sc - l_col, -1e30))
            d_p = _dot(d_out, v_ref[si, win, :], 1, 1)
            d_s = p * (d_p - d_col) * scale
            dq_ref[si, rows, :] = _dot(d_s, k_ref[si, win, :], 1, 0)
            dk_ref[si, win, :] += _dot(d_s, q, 0, 0)
            dv_ref[si, win, :] += _dot(p, d_out, 0, 0)

        _attn_schedule(nb, sb, block)

    head = lambda off: pl.BlockSpec((sb, ln, B_DH), lambda s, h: (s, 0, off + h))
    small = pl.BlockSpec((sb, ln, B_HEADS), lambda s, h: (s, 0, 0))
    return pl.pallas_call(
        body, name=name, grid=(ns // sb, B_HEADS),
        in_specs=[head(0), head(B_HEADS), head(2 * B_HEADS), head(0), small, small],
        out_specs=[head(0)] * 3,
        out_shape=[jax.ShapeDtypeStruct((ns, ln, B_W), F32)] * 3,
        compiler_params=_params(("parallel", "parallel")),
    )(qkv, qkv, qkv, d_o, lse_joint, delta)


def _merge_weights(lse_refs):
    ls = [r[...] for r in lse_refs]
    m = jnp.maximum(jnp.maximum(ls[0], ls[1]), ls[2])
    es = [jnp.exp(l - m) for l in ls]
    tot = es[0] + es[1] + es[2]
    return [e / tot for e in es], m + jnp.log(tot)


def _merge_fwd(outs, lses, proj0, tm=256):
    t = outs[0].shape[0]

    def body(o0, o1, o2, l0, l1, l2, z_ref, og_ref):
        wts, _ = _merge_weights((l0, l1, l2))
        for h in range(B_HEADS):
            cols = slice(h * B_DH, (h + 1) * B_DH)
            o = (wts[0][:, h:h + 1] * o0[:, cols] + wts[1][:, h:h + 1] * o1[:, cols]
                 + wts[2][:, h:h + 1] * o2[:, cols])
            og_ref[:, cols] = (o * _silu(z_ref[:, cols])).astype(BF16)

    wide = pl.BlockSpec((tm, B_W), lambda i: (i, 0))
    small = pl.BlockSpec((tm, B_HEADS), lambda i: (i, 0))
    return pl.pallas_call(
        body, name="merge_fwd", grid=(t // tm,),
        in_specs=[wide] * 3 + [small] * 3 + [pl.BlockSpec((tm, B_W), lambda i: (i, 3))],
        out_specs=wide, out_shape=jax.ShapeDtypeStruct((t, B_W), BF16),
        compiler_params=_params(("parallel",)),
    )(*outs, *lses, proj0)


def _merge_bwd(outs, lses, proj0, d_og, tm=256):
    t = outs[0].shape[0]

    def body(o0, o1, o2, l0, l1, l2, z_ref, dog_ref, do_ref, lj_ref, dl_ref, dz_ref):
        wts, lj = _merge_weights((l0, l1, l2))
        lj_ref[...] = lj
        lane = lax.broadcasted_iota(jnp.int32, (tm, B_HEADS), 1)
        delta = jnp.zeros((tm, B_HEADS), F32)
        for h in range(B_HEADS):
            cols = slice(h * B_DH, (h + 1) * B_DH)
            o = (wts[0][:, h:h + 1] * o0[:, cols] + wts[1][:, h:h + 1] * o1[:, cols]
                 + wts[2][:, h:h + 1] * o2[:, cols])
            z = z_ref[:, cols]
            d_g = dog_ref[:, cols]
            d_out = d_g * _silu(z)
            dz_ref[:, cols] = (d_g * o * _dsilu(z)).astype(BF16)
            do_ref[:, cols] = d_out.astype(BF16)
            delta = jnp.where(lane == h, jnp.sum(d_out * o, axis=1, keepdims=True), delta)
        dl_ref[...] = delta

    wide = pl.BlockSpec((tm, B_W), lambda i: (i, 0))
    small = pl.BlockSpec((tm, B_HEADS), lambda i: (i, 0))
    return pl.pallas_call(
        body, name="merge_bwd", grid=(t // tm,),
        in_specs=[wide] * 3 + [small] * 3 + [pl.BlockSpec((tm, B_W), lambda i: (i, 3)), wide],
        out_specs=[wide, small, small, wide],
        out_shape=[jax.ShapeDtypeStruct((t, B_W), BF16), jax.ShapeDtypeStruct((t, B_HEADS), F32),
                   jax.ShapeDtypeStruct((t, B_HEADS), F32), jax.ShapeDtypeStruct((t, B_W), BF16)],
        compiler_params=_params(("parallel",)),
    )(*outs, *lses, proj0, d_og)


def _adamw(w, g, m, v, name):
    r, c = w.shape
    tr = r
    for cand in (256, 128, 64, 32, 16, 8):
        if r % cand == 0:
            tr = cand
            break

    def body(w_ref, g_ref, m_ref, v_ref, d_ref, nm_ref, nv_ref):
        gv = g_ref[...]
        nm = ADAM_B1 * m_ref[...] + (1.0 - ADAM_B1) * gv
        nv = ADAM_B2 * v_ref[...] + (1.0 - ADAM_B2) * (gv * gv)
        m_hat = nm / (1.0 - ADAM_B1 ** ADAM_STEP)
        v_hat = nv / (1.0 - ADAM_B2 ** ADAM_STEP)
        d_ref[...] = -ADAM_LR * (m_hat / (jnp.sqrt(v_hat) + ADAM_EPS) + ADAM_WD * w_ref[...])
        nm_ref[...] = nm
        nv_ref[...] = nv

    blk = pl.BlockSpec((tr, c), lambda i: (i, 0))
    return pl.pallas_call(
        body, name=name, grid=(r // tr,), in_specs=[blk] * 4, out_specs=[blk] * 3,
        out_shape=[jax.ShapeDtypeStruct((r, c), F32)] * 3,
        compiler_params=_params(("parallel",)),
    )(w, g, m, v)


def _pair_sum(own, other, half_index, name, tr=256):
    _, r, c = own.shape
    rh = r // 2
    tr = min(tr, rh)
    nrb = rh // tr

    def body(c_ref, own_ref, oth_ref, out_ref):
        out_ref[...] = (own_ref[...] + oth_ref[...].astype(F32)).astype(BF16)

    return pl.pallas_call(
        body, name=name,
        grid_spec=pltpu.PrefetchScalarGridSpec(
            num_scalar_prefetch=1, grid=(N_CHIPS, nrb),
            in_specs=[pl.BlockSpec((None, tr, c), lambda k, i, cc: (k, cc[0] * nrb + i, 0)),
                      pl.BlockSpec((None, tr, c), lambda k, i, cc: (k, i, 0))],
            out_specs=pl.BlockSpec((None, tr, c), lambda k, i, cc: (k, i, 0))),
        out_shape=jax.ShapeDtypeStruct((N_CHIPS, rh, c), BF16),
        compiler_params=_params(("parallel", "parallel")),
    )(half_index, own, other)


def _chip_sum(sums, others, chip_index, name, tr=256):
    _, r, c = sums.shape
    tr = min(tr, r)

    def body(k_ref, own_ref, oth_ref, out_ref):
        acc = own_ref[...].astype(F32)
        for j in range(N_CHIPS - 1):
            acc = acc + oth_ref[j].astype(F32)
        out_ref[...] = acc

    return pl.pallas_call(
        body, name=name,
        grid_spec=pltpu.PrefetchScalarGridSpec(
            num_scalar_prefetch=1, grid=(r // tr,),
            in_specs=[pl.BlockSpec((None, tr, c), lambda i, kk: (kk[0], i, 0)),
                      pl.BlockSpec((N_CHIPS - 1, tr, c), lambda i, kk: (0, i, 0))],
            out_specs=pl.BlockSpec((tr, c), lambda i, kk: (i, 0))),
        out_shape=jax.ShapeDtypeStruct((r, c), F32),
        compiler_params=_params(("parallel",)),
    )(chip_index, sums, others)


HBM = pl.BlockSpec(memory_space=pltpu.HBM)


def _place():
    x, y, c = lax.axis_index("x"), lax.axis_index("y"), lax.axis_index("c")
    chips = [(1 - x, y), (x, 1 - y), (1 - x, 1 - y)]
    return x, y, c, chips


def _weight_allgather(shards, conv_shard):
    na = len(shards)

    def body(*refs):
        ins = refs[:na]
        conv_in = refs[na]
        outs = refs[na + 1:2 * na + 1]
        conv_out = refs[2 * na + 1]
        send, recv, fsend, frecv, csend, crecv = refs[2 * na + 2:]
        x, y, c, chips = _place()
        me = 2 * x + y
        sib = (x, y, 1 - c)
        first, conv_cp = [], []
        for i in range(na):
            rh = ins[i].shape[0] // 2
            mine = pl.ds(c * rh, rh)
            for j, (px, py) in enumerate(chips):
                cp = pltpu.make_async_remote_copy(
                    src_ref=ins[i].at[mine], dst_ref=outs[i].at[me, mine],
                    send_sem=send.at[3 * i + j], recv_sem=recv.at[3 * i + j],
                    device_id=(px, py, c), device_id_type=MESH)
                cp.start()
                first.append(cp)
        for j, (px, py) in enumerate(chips):
            cp = pltpu.make_async_remote_copy(
                src_ref=conv_in, dst_ref=conv_out.at[me], send_sem=csend.at[j], recv_sem=crecv.at[j],
                device_id=(px, py, c), device_id_type=MESH)
            cp.start()
            conv_cp.append(cp)
        passed = []
        for i in range(na):
            rh = ins[i].shape[0] // 2
            mine = pl.ds(c * rh, rh)
            for j, (px, py) in enumerate(chips):
                slot = outs[i].at[2 * px + py, mine]
                pltpu.make_async_remote_copy(
                    src_ref=slot, dst_ref=slot, send_sem=send.at[3 * i + j], recv_sem=recv.at[3 * i + j],
                    device_id=(px, py, c), device_id_type=MESH).wait_recv()
                cp = pltpu.make_async_remote_copy(
                    src_ref=slot, dst_ref=slot, send_sem=fsend.at[3 * i + j], recv_sem=frecv.at[3 * i + j],
                    device_id=sib, device_id_type=MESH)
                cp.start()
                passed.append(cp)
        for i in range(na):
            rh = ins[i].shape[0] // 2
            theirs = pl.ds((1 - c) * rh, rh)
            for j, (px, py) in enumerate(chips):
                slot = outs[i].at[2 * px + py, theirs]
                pltpu.make_async_remote_copy(
                    src_ref=slot, dst_ref=slot, send_sem=fsend.at[3 * i + j], recv_sem=frecv.at[3 * i + j],
                    device_id=sib, device_id_type=MESH).wait_recv()
        for j, (px, py) in enumerate(chips):
            slot = conv_out.at[2 * px + py]
            pltpu.make_async_remote_copy(
                src_ref=slot, dst_ref=slot, send_sem=csend.at[j], recv_sem=crecv.at[j],
                device_id=(px, py, c), device_id_type=MESH).wait_recv()
        for cp in first + passed + conv_cp:
            cp.wait_send()

    out_shape = [jax.ShapeDtypeStruct((N_CHIPS,) + s.shape, s.dtype) for s in shards]
    out_shape.append(jax.ShapeDtypeStruct((N_CHIPS,) + conv_shard.shape, conv_shard.dtype))
    res = pl.pallas_call(
        body, name="weight_allgather",
        in_specs=[HBM] * (na + 1), out_specs=[HBM] * (na + 1), out_shape=out_shape,
        scratch_shapes=[pltpu.SemaphoreType.DMA((3 * na,)), pltpu.SemaphoreType.DMA((3 * na,)),
                        pltpu.SemaphoreType.DMA((3 * na,)), pltpu.SemaphoreType.DMA((3 * na,)),
                        pltpu.SemaphoreType.DMA((3,)), pltpu.SemaphoreType.DMA((3,))],
    )(*shards, conv_shard)
    my_chip = 2 * lax.axis_index("x") + lax.axis_index("y")
    pick = lambda got, own: [jnp.where(my_chip == k, own, got[k]) for k in range(N_CHIPS)]
    return [pick(g, s) for g, s in zip(res[:na], shards)], pick(res[na], conv_shard)


def _sibling_swap_halves(grads):
    na = len(grads)

    def body(*refs):
        ins, outs = refs[:na], refs[na:2 * na]
        send, recv = refs[2 * na:]
        x, y, c, _ = _place()
        sib = (x, y, 1 - c)
        cps = []
        for i in range(na):
            rh = ins[i].shape[1] // 2
            cp = pltpu.make_async_remote_copy(
                src_ref=ins[i].at[:, pl.ds((1 - c) * rh, rh), :], dst_ref=outs[i],
                send_sem=send.at[i], recv_sem=recv.at[i], device_id=sib, device_id_type=MESH)
            cp.start()
            cps.append(cp)
        for cp in cps:
            cp.wait()

    out_shape = [jax.ShapeDtypeStruct((g.shape[0], g.shape[1] // 2, g.shape[2]), g.dtype) for g in grads]
    return pl.pallas_call(
        body, name="grad_sibling_swap", in_specs=[HBM] * na, out_specs=[HBM] * na, out_shape=out_shape,
        scratch_shapes=[pltpu.SemaphoreType.DMA((na,)), pltpu.SemaphoreType.DMA((na,))],
    )(*grads)


def _chip_exchange(sums):
    na = len(sums)

    def body(*refs):
        ins, outs = refs[:na], refs[na:2 * na]
        send, recv = refs[2 * na:]
        x, y, c, chips = _place()
        cps = []
        for i in range(na):
            for j, (px, py) in enumerate(chips):
                cp = pltpu.make_async_remote_copy(
                    src_ref=ins[i].at[2 * px + py], dst_ref=outs[i].at[j],
                    send_sem=send.at[3 * i + j], recv_sem=recv.at[3 * i + j],
                    device_id=(px, py, c), device_id_type=MESH)
                cp.start()
                cps.append(cp)
        for i in range(na):
            for j, (px, py) in enumerate(chips):
                slot = outs[i].at[j]
                pltpu.make_async_remote_copy(
                    src_ref=slot, dst_ref=slot, send_sem=send.at[3 * i + j], recv_sem=recv.at[3 * i + j],
                    device_id=(px, py, c), device_id_type=MESH).wait_recv()
        for cp in cps:
            cp.wait_send()

    out_shape = [jax.ShapeDtypeStruct((3,) + s.shape[1:], s.dtype) for s in sums]
    return pl.pallas_call(
        body, name="grad_chip_exchange", in_specs=[HBM] * na, out_specs=[HBM] * na, out_shape=out_shape,
        scratch_shapes=[pltpu.SemaphoreType.DMA((3 * na,)), pltpu.SemaphoreType.DMA((3 * na,))],
    )(*sums)


def _sibling_swap_whole(halves):
    na = len(halves)

    def body(*refs):
        ins, outs = refs[:na], refs[na:2 * na]
        send, recv = refs[2 * na:]
        x, y, c, _ = _place()
        cps = []
        for i in range(na):
            cp = pltpu.make_async_remote_copy(
                src_ref=ins[i], dst_ref=outs[i], send_sem=send.at[i], recv_sem=recv.at[i],
                device_id=(x, y, 1 - c), device_id_type=MESH)
            cp.start()
            cps.append(cp)
        for cp in cps:
            cp.wait()

    out_shape = [jax.ShapeDtypeStruct(h.shape, h.dtype) for h in halves]
    return pl.pallas_call(
        body, name="grad_sibling_join", in_specs=[HBM] * na, out_specs=[HBM] * na, out_shape=out_shape,
        scratch_shapes=[pltpu.SemaphoreType.DMA((na,)), pltpu.SemaphoreType.DMA((na,))],
    )(*halves)


def _small_allreduce(vec):
    r, cdim = vec.shape
    n_dev = 8

    def body(v_ref, out_ref, buf, send, recv):
        x, y, c, _ = _place()
        me = 4 * x + 2 * y + c
        buf[me] = v_ref[...]
        cps = []
        for k in range(1, n_dev):
            dx, dy, dc = (k >> 2) & 1, (k >> 1) & 1, k & 1
            peer = (x ^ dx, y ^ dy, c ^ dc)
            cp = pltpu.make_async_remote_copy(
                src_ref=v_ref, dst_ref=buf.at[me], send_sem=send.at[k - 1], recv_sem=recv.at[k - 1],
                device_id=peer, device_id_type=MESH)
            cp.start()
            cps.append(cp)
        for k in range(1, n_dev):
            dx, dy, dc = (k >> 2) & 1, (k >> 1) & 1, k & 1
            src = 4 * (x ^ dx) + 2 * (y ^ dy) + (c ^ dc)
            slot = buf.at[src]
            pltpu.make_async_remote_copy(
                src_ref=slot, dst_ref=slot, send_sem=send.at[k - 1], recv_sem=recv.at[k - 1],
                device_id=(x ^ dx, y ^ dy, c ^ dc), device_id_type=MESH).wait_recv()
        for cp in cps:
            cp.wait_send()
        acc = buf[0]
        for k in range(1, n_dev):
            acc = acc + buf[k]
        out_ref[...] = acc

    vm = pl.BlockSpec(memory_space=pltpu.VMEM)
    return pl.pallas_call(
        body, name="small_allreduce", in_specs=[vm], out_specs=vm,
        out_shape=jax.ShapeDtypeStruct((r, cdim), F32),
        scratch_shapes=[pltpu.VMEM((n_dev, r, cdim), F32), pltpu.SemaphoreType.DMA((n_dev - 1,)),
                        pltpu.SemaphoreType.DMA((n_dev - 1,))],
    )(vec)


def _a_cols_to_head_major(w):
    lead = w.shape[:-1]
    q = w[..., :A_QK].reshape(lead + (A_HEADS, A_DK))
    k = w[..., A_QK:2 * A_QK].reshape(lead + (A_HEADS, A_DK))
    v = w[..., 2 * A_QK:2 * A_QK + A_VW].reshape(lead + (A_HEADS, A_DV))
    z = w[..., 2 * A_QK + A_VW:].reshape(lead + (A_HEADS, A_DV))
    return jnp.concatenate([q, k, v, z], axis=-1).reshape(lead + (A_HEADS * A_HEAD_COLS,))


def _a_cols_from_head_major(w):
    lead = w.shape[:-1]
    w = w.reshape(lead + (A_HEADS, A_HEAD_COLS))
    parts = [w[..., :A_DK], w[..., A_DK:2 * A_DK], w[..., 2 * A_DK:2 * A_DK + A_DV], w[..., 2 * A_DK + A_DV:]]
    return jnp.concatenate([p.reshape(lead + (-1,)) for p in parts], axis=-1)


def _conv_cols_to_head_major(w):
    lead = w.shape[:-1]
    q = w[..., :A_QK].reshape(lead + (A_HEADS, A_DK))
    k = w[..., A_QK:2 * A_QK].reshape(lead + (A_HEADS, A_DK))
    v = w[..., 2 * A_QK:].reshape(lead + (A_HEADS, A_DV))
    return jnp.concatenate([q, k, v], axis=-1).reshape(lead + (A_HEADS * A_CONV_COLS,))


def _conv_cols_from_head_major(w):
    lead = w.shape[:-1]
    w = w.reshape(lead + (A_HEADS, A_CONV_COLS))
    parts = [w[..., :A_DK], w[..., A_DK:2 * A_DK], w[..., 2 * A_DK:]]
    return jnp.concatenate([p.reshape(lead + (-1,)) for p in parts], axis=-1)


def _to_stream(a, bn, d):
    rest = a.shape[1:]
    s = a.shape[0] // bn
    a = a.reshape((bn, s // d, d) + rest)
    a = jnp.swapaxes(a, 1, 2)
    return a.reshape((bn * d, s // d) + rest)


def _from_stream(a, bn, d):
    rest = a.shape[2:]
    ln = a.shape[1]
    a = a.reshape((bn, d, ln) + rest)
    a = jnp.swapaxes(a, 1, 2)
    return a.reshape((bn * ln * d,) + rest)


def _b_group_cols(w, gi):
    n_qkv = 3 * B_GROUPS * B_W
    qkv = w[..., :n_qkv].reshape(w.shape[:-1] + (3, B_GROUPS, B_W))
    return qkv[..., :, gi, :].reshape(w.shape[:-1] + (3 * B_W,))


def _shard_major(g, ncols):
    r = g.shape[0]
    return jnp.swapaxes(g.reshape(r, N_CHIPS, ncols), 0, 1)


def _pack_rows(items):
    rows, offs = [], []
    at = 0
    for a in items:
        flat = a.reshape(-1).astype(F32)
        nr = -(-flat.shape[0] // 128)
        flat = jnp.pad(flat, (0, nr * 128 - flat.shape[0]))
        rows.append(flat.reshape(nr, 128))
        offs.append((at, nr, a.shape))
        at += nr
    pad = -at % 8
    if pad:
        rows.append(jnp.zeros((pad, 128), F32))
    return jnp.concatenate(rows, axis=0), offs


def _unpack_rows(packed, offs):
    out = []
    for at, nr, shape in offs:
        size = int(np.prod(shape)) if len(shape) else 1
        out.append(packed[at:at + nr].reshape(-1)[:size].reshape(shape))
    return out


def _local_step(x, positions, loss_target, norm_g, wa_in, conv_w, a_log, a_dt_bias, a_norm_g, wa_out,
                wb_in, b_q_norm_g, b_k_norm_g, wb_out):
    bn, s, d = x.shape
    t = bn * s
    n_chunks = s // A_CHUNK
    wa_main = _a_cols_to_head_major(wa_in[:, :A_MAIN])
    wa_tail = jnp.pad(wa_in[:, A_MAIN:], ((0, 0), (0, 128 - 2 * A_HEADS)))
    cw_hm = _conv_cols_to_head_major(conv_w)
    wb_groups = [_b_group_cols(wb_in, gi) for gi in range(B_GROUPS)]
    wb_groups[0] = jnp.concatenate([wb_groups[0], wb_in[:, 3 * B_GROUPS * B_W:]], axis=1)

    x0 = x.reshape(t, d)
    h0 = _rms_fwd(x0, norm_g[0:1], "rms0_fwd")
    proj_a = _matmul(h0, wa_main, "nn", F32, "a_in_main")
    tail_a = _matmul(h0, wa_tail, "nn", F32, "a_in_tail")
    tail_t = jnp.swapaxes(tail_a[:, :2 * A_HEADS].reshape(bn, s, 2 * A_HEADS), 1, 2)
    tail_t = tail_t.reshape(bn, 2 * A_HEADS, n_chunks, A_CHUNK)
    beta, gc = _gdn_prep(tail_t, a_log[0], a_dt_bias[0])
    proj_a3 = proj_a.reshape(bn, s, A_MAIN)
    og_a, oraw_a, states, t_mats = _gdn_fwd(proj_a3, cw_hm, beta, gc, a_norm_g)
    x1 = _matmul(og_a.reshape(t, A_VW), wa_out, "nn", F32, "a_out", res=x0, tk=2048)

    h1 = _rms_fwd(x1, norm_g[1:2], "rms1_fwd")
    inv_freq = ROPE_THETA ** (-jnp.arange(0, ROPE_DIMS, 2, dtype=F32) / ROPE_DIMS)
    freq_row = jnp.concatenate([inv_freq, inv_freq, jnp.zeros((128 - ROPE_DIMS,), F32)]).reshape(1, 128)
    posf = jnp.broadcast_to(positions.astype(F32).reshape(t, 1), (t, 128))
    tabs = _rope_tables(posf, freq_row)
    h1_s, tabs_s, proj_b, qkv_b, o_b, lse_b = [], [], [], [], [], []
    for gi, dil in enumerate(B_DIL):
        hs = h1 if dil == 1 else _to_stream(h1, bn, dil).reshape(t, d)
        ts = tabs if dil == 1 else [_to_stream(tb, bn, dil).reshape(t, 128) for tb in tabs]
        pj = _matmul(hs, wb_groups[gi], "nn", F32, f"b_in_g{gi}")
        qkv = _qk_prep(pj, *ts, b_q_norm_g[0, gi:gi + 1], b_k_norm_g[0, gi:gi + 1], f"qk_prep_g{gi}")
        o_s, lse_s = _attn_fwd(qkv.reshape(bn * dil, s // dil, 3 * B_W), f"attn_fwd_g{gi}")
        h1_s.append(hs), tabs_s.append(ts), proj_b.append(pj), qkv_b.append(qkv)
        o_b.append(o_s.reshape(t, B_W) if dil == 1 else _from_stream(o_s, bn, dil))
        lse_b.append(lse_s.reshape(t, B_HEADS) if dil == 1 else _from_stream(lse_s, bn, dil))
    og_b = _merge_fwd(o_b, lse_b, proj_b[0])
    x2 = _matmul(og_b, wb_out, "nn", F32, "b_out", res=x1)

    d_x2, loss_parts = _loss_grad(x2, loss_target.reshape(t, d))
    loss_local = jnp.sum(loss_parts)

    d_x2b = d_x2.astype(BF16)
    g_wb_out = _matmul(og_b.T, d_x2b, "nn", F32, "b_out_dw")
    d_og_b = _matmul(d_x2b, wb_out, "nt", F32, "b_out_dx")
    d_o, lse_joint, delta, d_z = _merge_bwd(o_b, lse_b, proj_b[0], d_og_b)
    d_h1, g_wb_cols, g_qn, g_kn = [], [], [], []
    for gi, dil in enumerate(B_DIL):
        if dil == 1:
            do_s, lj_s, dl_s = d_o, lse_joint, delta
        else:
            do_s, lj_s, dl_s = (_to_stream(a, bn, dil).reshape(t, -1) for a in (d_o, lse_joint, delta))
        ns, ln = bn * dil, s // dil
        dq, dk, dv = _attn_bwd(qkv_b[gi].reshape(ns, ln, 3 * B_W), do_s.reshape(ns, ln, B_W),
                               lj_s.reshape(ns, ln, B_HEADS), dl_s.reshape(ns, ln, B_HEADS), f"attn_bwd_g{gi}")
        d_pj, d_gain = _qk_prep_bwd(proj_b[gi], *tabs_s[gi], b_q_norm_g[0, gi:gi + 1], b_k_norm_g[0, gi:gi + 1],
                                    dq.reshape(t, B_W), dk.reshape(t, B_W), dv.reshape(t, B_W),
                                    d_z if gi == 0 else None, f"qk_prep_bwd_g{gi}")
        g_wb_cols.append(_matmul(h1_s[gi].T, d_pj, "nn", F32, f"b_in_dw_g{gi}"))
        dh = _matmul(d_pj, wb_groups[gi], "nt", F32, f"b_in_dx_g{gi}")
        d_h1.append(dh if dil == 1 else _from_stream(dh.reshape(ns, ln, d), bn, dil))
        g_qn.append(d_gain[0]), g_kn.append(d_gain[1])
    d_x1, g_norm1 = _rms_bwd(x1, norm_g[1:2], d_h1, d_x2, "rms1_bwd")
    pieces = [g_wb_cols[gi][:, w * B_W:(w + 1) * B_W] for w in range(3) for gi in range(B_GROUPS)]
    g_wb_in = jnp.concatenate(pieces + [g_wb_cols[0][:, 3 * B_W:]], axis=1)

    d_x1b = d_x1.astype(BF16)
    g_wa_out = _matmul(og_a.reshape(t, A_VW).T, d_x1b, "nn", F32, "a_out_dw")
    d_og_a = _matmul(d_x1b, wa_out, "nt", F32, "a_out_dx")
    d_pa, d_gc, d_beta, d_cw, d_ng = _gdn_bwd(proj_a3, cw_hm, beta, gc, a_norm_g, oraw_a, states, t_mats,
                                              d_og_a.reshape(bn, s, A_VW))
    d_tail_t, d_alog, d_dtb = _gdn_prep_bwd(tail_t, a_log[0], a_dt_bias[0], d_gc, d_beta)
    d_tail = jnp.swapaxes(d_tail_t.reshape(bn, 2 * A_HEADS, s), 1, 2).reshape(t, 2 * A_HEADS)
    d_tail = jnp.pad(d_tail, ((0, 0), (0, 128 - 2 * A_HEADS))).astype(BF16)
    d_pa = d_pa.reshape(t, A_MAIN)
    h0_t = h0.T
    g_wa_main = _matmul(h0_t, d_pa, "nn", F32, "a_in_dw_main")
    g_wa_tail = _matmul(h0_t, d_tail, "nn", F32, "a_in_dw_tail")
    d_h0 = _matmul(d_pa, wa_main, "nt", F32, "a_in_dx_main")
    d_h0t = _matmul(d_tail, wa_tail, "nt", F32, "a_in_dx_tail")
    d_x0, g_norm0 = _rms_bwd(x0, norm_g[0:1], [d_h0, d_h0t], d_x1, "rms0_bwd")
    g_wa_in = jnp.concatenate([_a_cols_from_head_major(g_wa_main), g_wa_tail[:, :2 * A_HEADS]], axis=1)

    gfull = {
        "norm_g": jnp.concatenate([g_norm0, g_norm1], axis=0), "a_w_in": g_wa_in,
        "a_conv_w": _conv_cols_from_head_major(jnp.sum(d_cw, axis=0)),
        "a_log": jnp.sum(d_alog[:, :, 0], axis=0), "a_dt_bias": jnp.sum(d_dtb[:, :, 0], axis=0),
        "a_norm_g": jnp.sum(d_ng[:, :, 0, :], axis=(0, 1)), "a_w_out": g_wa_out, "b_w_in": g_wb_in,
        "b_q_norm_g": jnp.stack(g_qn), "b_k_norm_g": jnp.stack(g_kn), "b_w_out": g_wb_out}
    return loss_local, d_x0.reshape(bn, s, d), gfull


def kernel(x, positions, norm_g, a_w_in, a_conv_w, a_log, a_dt_bias, a_norm_g, a_w_out, b_w_in, b_q_norm_g, b_k_norm_g, b_w_out, loss_target, m_norm_g, m_a_w_in, m_a_conv_w, m_a_log, m_a_dt_bias, m_a_norm_g, m_a_w_out, m_b_w_in, m_b_q_norm_g, m_b_k_norm_g, m_b_w_out, v_norm_g, v_a_w_in, v_a_conv_w, v_a_log, v_a_dt_bias, v_a_norm_g, v_a_w_out, v_b_w_in, v_b_q_norm_g, v_b_k_norm_g, v_b_w_out):
    d = x.shape[2]
    my_c = lax.axis_index("c")
    my_chip = 2 * lax.axis_index("x") + lax.axis_index("y")

    shards = [a_w_in[0].astype(BF16), a_w_out[0].astype(BF16), b_w_in[0].astype(BF16), b_w_out[0].astype(BF16)]
    (ga_in, ga_out, gb_in, gb_out), g_conv = _weight_allgather(shards, a_conv_w[0])
    wa_in = jnp.concatenate(ga_in, axis=1)
    wa_out = jnp.concatenate(ga_out, axis=0)
    wb_in = jnp.concatenate(gb_in, axis=1)
    wb_out = jnp.concatenate(gb_out, axis=0)
    conv_w = jnp.concatenate(g_conv, axis=1)

    loss_local, d_x0, gfull = _local_step(x, positions, loss_target, norm_g, wa_in, conv_w, a_log, a_dt_bias,
                                          a_norm_g, wa_out, wb_in, b_q_norm_g, b_k_norm_g, wb_out)

    g_full = [_shard_major(gfull["a_w_in"], a_w_in.shape[2]), gfull["a_w_out"].reshape(N_CHIPS, -1, d),
              _shard_major(gfull["b_w_in"], b_w_in.shape[2]), gfull["b_w_out"].reshape(N_CHIPS, -1, d)]
    recv_sib = _sibling_swap_halves([g.astype(BF16) for g in g_full])
    half_index = jnp.reshape(my_c, (1,)).astype(jnp.int32)
    chip_sums = [_pair_sum(g, r, half_index, f"grad_pair_sum_{i}") for i, (g, r) in enumerate(zip(g_full, recv_sib))]
    received = _chip_exchange(chip_sums)
    chip_index = jnp.reshape(my_chip, (1,)).astype(jnp.int32)
    halves = [_chip_sum(s, r, chip_index, f"grad_chip_sum_{i}") for i, (s, r) in enumerate(zip(chip_sums, received))]
    theirs = _sibling_swap_whole(halves)
    join = lambda mine, other: jnp.where(my_c == 0, jnp.concatenate([mine, other], axis=0),
                                         jnp.concatenate([other, mine], axis=0))
    g_a_w_in, g_a_w_out, g_b_w_in, g_b_w_out = [join(m, o) for m, o in zip(halves, theirs)]

    small = [gfull["norm_g"], gfull["a_conv_w"], gfull["a_log"], gfull["a_dt_bias"], gfull["a_norm_g"],
             gfull["b_q_norm_g"], gfull["b_k_norm_g"], loss_local]
    packed, offs = _pack_rows(small)
    red = _unpack_rows(_small_allreduce(packed), offs)
    g_norm, g_conv_all, g_alog, g_dtb, g_ang, g_q, g_k, loss = red
    g_conv_mine = lax.dynamic_slice_in_dim(g_conv_all, my_chip * a_conv_w.shape[2], a_conv_w.shape[2], axis=1)

    grads = {
        "norm_g": g_norm, "a_w_in": g_a_w_in[None], "a_conv_w": g_conv_mine[None], "a_log": g_alog[None],
        "a_dt_bias": g_dtb[None], "a_norm_g": g_ang[None], "a_w_out": g_a_w_out[None], "b_w_in": g_b_w_in[None],
        "b_q_norm_g": g_q[None], "b_k_norm_g": g_k[None], "b_w_out": g_b_w_out[None]}
    weights = {"norm_g": norm_g, "a_w_in": a_w_in, "a_conv_w": a_conv_w, "a_log": a_log, "a_dt_bias": a_dt_bias,
               "a_norm_g": a_norm_g, "a_w_out": a_w_out, "b_w_in": b_w_in, "b_q_norm_g": b_q_norm_g,
               "b_k_norm_g": b_k_norm_g, "b_w_out": b_w_out}
    m_in = {"norm_g": m_norm_g, "a_w_in": m_a_w_in, "a_conv_w": m_a_conv_w, "a_log": m_a_log,
            "a_dt_bias": m_a_dt_bias, "a_norm_g": m_a_norm_g, "a_w_out": m_a_w_out, "b_w_in": m_b_w_in,
            "b_q_norm_g": m_b_q_norm_g, "b_k_norm_g": m_b_k_norm_g, "b_w_out": m_b_w_out}
    v_in = {"norm_g": v_norm_g, "a_w_in": v_a_w_in, "a_conv_w": v_a_conv_w, "a_log": v_a_log,
            "a_dt_bias": v_a_dt_bias, "a_norm_g": v_a_norm_g, "a_w_out": v_a_w_out, "b_w_in": v_b_w_in,
            "b_q_norm_g": v_b_q_norm_g, "b_k_norm_g": v_b_k_norm_g, "b_w_out": v_b_w_out}
    names = list(weights)

    big = ("a_w_in", "a_w_out", "b_w_in", "b_w_out")
    delta_w, new_m, new_v = {}, {}, {}
    for nm in big:
        shp = weights[nm].shape
        two = lambda a: a.reshape(shp[-2], shp[-1])
        dl, m2, v2 = _adamw(two(weights[nm]), two(grads[nm]), two(m_in[nm]), two(v_in[nm]), f"adamw_{nm}")
        delta_w[nm], new_m[nm], new_v[nm] = dl.reshape(shp), m2.reshape(shp), v2.reshape(shp)
    small_names = [nm for nm in names if nm not in big]
    packs = [_pack_rows([src[nm] for nm in small_names]) for src in (weights, grads, m_in, v_in)]
    offs = packs[0][1]
    dl, m2, v2 = _adamw(packs[0][0], packs[1][0], packs[2][0], packs[3][0], "adamw_small")
    for nm, a, b, c2 in zip(small_names, _unpack_rows(dl, offs), _unpack_rows(m2, offs), _unpack_rows(v2, offs)):
        delta_w[nm], new_m[nm], new_v[nm] = a, b, c2

    return (loss, d_x0, *[grads[nm] for nm in names], *[delta_w[nm] for nm in names],
            *[new_m[nm] for nm in names], *[new_v[nm] for nm in names])
```

```python
import functools
import math

import jax
import jax.numpy as jnp
import numpy as np
from jax import lax
from jax.experimental import pallas as pl
from jax.experimental.pallas import tpu as pltpu

F32 = jnp.float32
BF16 = jnp.bfloat16
MESH = pl.DeviceIdType.MESH

EPS = 1e-6
D_MODEL = 1024
A_HEADS = 8
A_DK = 128
A_DV = 256
A_QK = A_HEADS * A_DK
A_VW = A_HEADS * A_DV
A_MAIN = 2 * A_QK + 2 * A_VW
A_HEAD_COLS = 2 * A_DK + 2 * A_DV
A_CONV_COLS = 2 * A_DK + A_DV
A_CHUNK = 64
A_CONV = 4
B_GROUPS = 3
B_HEADS = 8
B_DH = 128
B_W = B_HEADS * B_DH
B_DIL = (1, 4, 16)
B_BLK = 128
ROPE_THETA = 500000.0
ROPE_DIMS = B_DH // 4
ADAM_LR, ADAM_B1, ADAM_B2, ADAM_EPS, ADAM_WD, ADAM_STEP = 0.001, 0.9, 0.999, 1e-08, 0.01, 10
N_CHIPS = 4
VMEM_BIG = 56 * 1024 * 1024


def _params(sem=None, vmem=None):
    return pltpu.CompilerParams(dimension_semantics=sem, vmem_limit_bytes=vmem)


def _dot(a, b, ca, cb):
    return lax.dot_general(a.astype(BF16), b.astype(BF16), (((ca,), (cb,)), ((), ())),
                           preferred_element_type=F32)


def _split3(a):
    hi = a.astype(BF16)
    r = a - hi.astype(F32)
    mid = r.astype(BF16)
    lo = (r - mid.astype(F32)).astype(BF16)
    return hi, mid, lo


def _dot_hi(a, b, ca, cb):
    a_hi, a_lo, _ = _split3(a)
    b_hi, b_lo, _ = _split3(b)
    dn = (((ca,), (cb,)), ((), ()))
    out = lax.dot_general(a_hi, b_hi, dn, preferred_element_type=F32)
    out = out + lax.dot_general(a_hi, b_lo, dn, preferred_element_type=F32)
    return out + lax.dot_general(a_lo, b_hi, dn, preferred_element_type=F32)


def _sigmoid(y):
    return 1.0 / (1.0 + jnp.exp(-y))


def _silu(y):
    return y * _sigmoid(y)


def _dsilu(y):
    s = _sigmoid(y)
    return s * (1.0 + y * (1.0 - s))


def _matmul(a, b, mode, out_dtype, name, res=None, tm=1024, tn=1024, tk=1024):
    if mode == "nn":
        (m, k), (_, n) = a.shape, b.shape
    else:
        (m, k), (n, _) = a.shape, b.shape
    tm, tn, tk = min(tm, m), min(tn, n), min(tk, k)
    assert m % tm == 0 and n % tn == 0 and k % tk == 0, (name, a.shape, b.shape)
    nk = k // tk
    dims = {"nn": ((1,), (0,)), "nt": ((1,), (1,))}[mode]

    def body(*refs):
        a_ref, b_ref = refs[0], refs[1]
        r_ref = refs[2] if res is not None else None
        o_ref = refs[3] if res is not None else refs[2]
        prod = lax.dot_general(a_ref[...], b_ref[...], (dims, ((), ())), preferred_element_type=F32)

        def finish(r):
            if res is not None:
                r = r + r_ref[...]
            o_ref[...] = r.astype(out_dtype)

        if nk == 1:
            finish(prod)
            return
        acc = refs[-1]
        kk = pl.program_id(2)

        @pl.when(kk == 0)
        def _():
            acc[...] = prod

        @pl.when((kk > 0) & (kk < nk - 1))
        def _():
            acc[...] += prod

        @pl.when(kk == nk - 1)
        def _():
            finish(acc[...] + prod)

    a_spec = pl.BlockSpec((tm, tk), lambda i, j, kk: (i, kk))
    if mode == "nt":
        b_spec = pl.BlockSpec((tn, tk), lambda i, j, kk: (j, kk))
    else:
        b_spec = pl.BlockSpec((tk, tn), lambda i, j, kk: (kk, j))
    in_specs = [a_spec, b_spec]
    args = [a, b]
    if res is not None:
        in_specs.append(pl.BlockSpec((tm, tn), lambda i, j, kk: (i, j)))
        args.append(res)
    return pl.pallas_call(
        body, name=name, grid=(m // tm, n // tn, nk),
        in_specs=in_specs, out_specs=pl.BlockSpec((tm, tn), lambda i, j, kk: (i, j)),
        out_shape=jax.ShapeDtypeStruct((m, n), out_dtype),
        scratch_shapes=[pltpu.VMEM((tm, tn), F32)] if nk > 1 else [],
        compiler_params=_params(("parallel", "parallel", "arbitrary"), 48 * 1024 * 1024),
    )(*args)


def _rms_fwd(x, g, name, tm=256):
    t, d = x.shape

    def body(x_ref, g_ref, h_ref):
        xv = x_ref[...]
        r = lax.rsqrt(jnp.mean(xv * xv, axis=-1, keepdims=True) + EPS)
        h_ref[...] = (xv * r * g_ref[...]).astype(BF16)

    return pl.pallas_call(
        body, name=name, grid=(t // tm,),
        in_specs=[pl.BlockSpec((tm, d), lambda i: (i, 0)), pl.BlockSpec((1, d), lambda i: (0, 0))],
        out_specs=pl.BlockSpec((tm, d), lambda i: (i, 0)),
        out_shape=jax.ShapeDtypeStruct((t, d), BF16),
        compiler_params=_params(("parallel",)),
    )(x, g)


def _rms_bwd(x, g, dhs, dres, name, tm=256):
    t, d = x.shape
    n_dh = len(dhs)

    def body(*refs):
        x_ref, g_ref = refs[0], refs[1]
        dh_refs = refs[2:2 + n_dh]
        dres_ref, dx_ref, dg_ref = refs[2 + n_dh:]
        i = pl.program_id(0)

        @pl.when(i == 0)
        def _():
            dg_ref[...] = jnp.zeros_like(dg_ref)

        xv = x_ref[...]
        r = lax.rsqrt(jnp.mean(xv * xv, axis=-1, keepdims=True) + EPS)
        xh = xv * r
        dh = dh_refs[0][...]
        for ref in dh_refs[1:]:
            dh = dh + ref[...]
        dg_ref[0:1, :] += jnp.sum(dh * xh, axis=0, keepdims=True)
        dxh = dh * g_ref[...]
        dx = r * (dxh - xh * jnp.mean(dxh * xh, axis=-1, keepdims=True))
        dx_ref[...] = dx + dres_ref[...]

    row = pl.BlockSpec((tm, d), lambda i: (i, 0))
    dx, dg = pl.pallas_call(
        body, name=name, grid=(t // tm,),
        in_specs=[row, pl.BlockSpec((1, d), lambda i: (0, 0))] + [row] * n_dh + [row],
        out_specs=[row, pl.BlockSpec((8, d), lambda i: (0, 0))],
        out_shape=[jax.ShapeDtypeStruct((t, d), F32), jax.ShapeDtypeStruct((8, d), F32)],
        compiler_params=_params(("arbitrary",)),
    )(x, g, *dhs, dres)
    return dx, dg[0:1]


def _loss_grad(y, target, name="loss_grad", tm=256):
    t, d = y.shape
    nb = t // tm

    def body(y_ref, t_ref, dy_ref, part_ref):
        e = y_ref[...] - t_ref[...]
        dy_ref[...] = e * (1.0 / d)
        s = jnp.sum(jnp.sum(e * e, axis=1, keepdims=True), axis=0, keepdims=True) * (0.5 / d)
        part_ref[...] = jnp.broadcast_to(s, (8, 128))

    row = pl.BlockSpec((tm, d), lambda i: (i, 0))
    dy, part = pl.pallas_call(
        body, name=name, grid=(nb,), in_specs=[row, row],
        out_specs=[row, pl.BlockSpec((None, 8, 128), lambda i: (i, 0, 0))],
        out_shape=[jax.ShapeDtypeStruct((t, d), F32), jax.ShapeDtypeStruct((nb, 8, 128), F32)],
        compiler_params=_params(("parallel",)),
    )(y, target)
    return dy, part[:, 0, 0]


def _softplus(x):
    t = jnp.exp(-jnp.abs(x))
    return jnp.maximum(x, 0.0) + jnp.where(t < 1e-3, t * (1.0 - 0.5 * t), jnp.log(1.0 + t))


def _tri(rows_le_cols):
    r = lax.broadcasted_iota(jnp.int32, (A_CHUNK, A_CHUNK), 0)
    c = lax.broadcasted_iota(jnp.int32, (A_CHUNK, A_CHUNK), 1)
    return jnp.where((r <= c) if rows_le_cols else (r >= c), 1.0, 0.0).astype(BF16)


def _dot_exact_rhs(a, ones_bf16):
    dn = (((1,), (0,)), ((), ()))
    hi, mid, lo = _split3(a)
    out = lax.dot_general(hi, ones_bf16, dn, preferred_element_type=F32)
    out = out + lax.dot_general(mid, ones_bf16, dn, preferred_element_type=F32)
    return out + lax.dot_general(lo, ones_bf16, dn, preferred_element_type=F32)


def _gdn_prep(tail_t, a_log, dt_bias):
    bn, _, n, c = tail_t.shape

    def body(t_ref, alog_ref, dtb_ref, beta_ref, gc_ref):
        upper = _tri(True)
        for h in range(A_HEADS):
            beta_ref[h] = _sigmoid(t_ref[h])
            ea = jnp.exp(jnp.full((n, c), alog_ref[h], F32))
            g = -ea * _softplus(t_ref[A_HEADS + h] + dtb_ref[h])
            gc_ref[h] = _dot_exact_rhs(g, upper)

    smem = pl.BlockSpec(memory_space=pltpu.SMEM)
    blk = pl.BlockSpec((None, A_HEADS, n, c), lambda b: (b, 0, 0, 0))
    return pl.pallas_call(
        body, name="gdn_prep", grid=(bn,),
        in_specs=[pl.BlockSpec((None, 2 * A_HEADS, n, c), lambda b: (b, 0, 0, 0)), smem, smem],
        out_specs=[blk, blk],
        out_shape=[jax.ShapeDtypeStruct((bn, A_HEADS, n, c), F32)] * 2,
        compiler_params=_params(("parallel",)),
    )(tail_t, a_log, dt_bias)


def _gdn_prep_bwd(tail_t, a_log, dt_bias, d_gc, d_beta):
    bn, _, n, c = tail_t.shape

    def body(t_ref, alog_ref, dtb_ref, dgc_ref, dbeta_ref, dt_ref, dal_ref, ddt_ref):
        lower = _tri(False)
        for h in range(A_HEADS):
            beta = _sigmoid(t_ref[h])
            dt_ref[h] = dbeta_ref[h] * beta * (1.0 - beta)
            dg = _dot_exact_rhs(dgc_ref[h], lower)
            ea = jnp.exp(jnp.full((n, c), alog_ref[h], F32))
            xa = t_ref[A_HEADS + h] + dtb_ref[h]
            g = -ea * _softplus(xa)
            dxa = -ea * dg * _sigmoid(xa)
            dt_ref[A_HEADS + h] = dxa
            s1 = jnp.sum(jnp.sum(g * dg, axis=1, keepdims=True), axis=0, keepdims=True)
            s2 = jnp.sum(jnp.sum(dxa, axis=1, keepdims=True), axis=0, keepdims=True)
            dal_ref[h:h + 1, :] = jnp.broadcast_to(s1, (1, 128))
            ddt_ref[h:h + 1, :] = jnp.broadcast_to(s2, (1, 128))

    smem = pl.BlockSpec(memory_space=pltpu.SMEM)
    blk8 = pl.BlockSpec((None, A_HEADS, n, c), lambda b: (b, 0, 0, 0))
    blk16 = pl.BlockSpec((None, 2 * A_HEADS, n, c), lambda b: (b, 0, 0, 0))
    sm = pl.BlockSpec((None, A_HEADS, 128), lambda b: (b, 0, 0))
    return pl.pallas_call(
        body, name="gdn_prep_bwd", grid=(bn,),
        in_specs=[blk16, smem, smem, blk8, blk8],
        out_specs=[blk16, sm, sm],
        out_shape=[jax.ShapeDtypeStruct((bn, 2 * A_HEADS, n, c), F32),
                   jax.ShapeDtypeStruct((bn, A_HEADS, 128), F32),
                   jax.ShapeDtypeStruct((bn, A_HEADS, 128), F32)],
        compiler_params=_params(("parallel",)),
    )(tail_t, a_log, dt_bias, d_gc, d_beta)


HALO = 8


def _conv_window(x_ref, n, first, lo, width):
    if first:
        return jnp.concatenate([jnp.zeros((HALO, width), F32), x_ref[0:A_CHUNK, lo:lo + width]], axis=0)
    start = pl.multiple_of(n * A_CHUNK - HALO, HALO)
    return x_ref[pl.ds(start, A_CHUNK + HALO), lo:lo + width]


def _conv_taps(xw, w):
    y = w[A_CONV - 1:A_CONV, :] * xw
    for j in range(1, A_CONV):
        y = y + w[A_CONV - 1 - j:A_CONV - j, :] * pltpu.roll(xw, j, 0)
    return y[HALO:, :]


def _row_to_col(row, eye):
    c = eye.shape[0]
    return jnp.sum(jnp.where(eye, jnp.broadcast_to(row, (c, c)), 0.0), axis=1, keepdims=True)


def _col_to_row(col, eye):
    c = eye.shape[0]
    return jnp.sum(jnp.where(eye, jnp.broadcast_to(col, (c, c)), 0.0), axis=0, keepdims=True)


def _unit_lower_inverse(a, ri, ci):
    eye = jnp.where(ri == ci, 1.0, 0.0)
    a8 = jnp.where((ri >> 3) == (ci >> 3), a, 0.0)
    a2 = _dot_hi(a8, a8, 1, 0)
    yield
    a4 = _dot_hi(a2, a2, 1, 0)
    t = eye - a8
    t = t + _dot_hi(t, a2, 1, 0)
    yield
    t = t + _dot_hi(t, a4, 1, 0)
    yield
    for sh in (3, 4, 5):
        off = jnp.where(((ri >> (sh + 1)) == (ci >> (sh + 1))) & ((ri >> sh) != (ci >> sh)), a, 0.0)
        left = _dot_hi(t, off, 1, 0)
        yield
        t = t - _dot_hi(left, t, 1, 0)
        yield
    return t


def _round_robin(gens):
    live = list(gens)
    while live:
        nxt = []
        for g in live:
            try:
                next(g)
                nxt.append(g)
            except StopIteration:
                pass
        live = nxt


def _gdn_chunk_inputs(x_ref, cw, n, first):
    xw = _conv_window(x_ref, n, first, 0, A_CONV_COLS)
    y = _conv_taps(xw, cw)
    a = _silu(y)
    aq, ak, v = a[:, 0:A_DK], a[:, A_DK:2 * A_DK], a[:, 2 * A_DK:]
    rq = lax.rsqrt(jnp.sum(aq * aq, axis=1, keepdims=True) + EPS)
    rk = lax.rsqrt(jnp.sum(ak * ak, axis=1, keepdims=True) + EPS)
    return dict(xw=xw, y=y, aq=aq, ak=ak, rq=rq, rk=rk,
                q=aq * rq * (A_DK ** -0.5), k=ak * rk, v=v)


def _gdn_chunk_core(q, k, v, g_row, b_row, t_mat, ri, ci):
    eye = ri == ci
    g_col = _row_to_col(g_row, eye)
    b_col = _row_to_col(b_row, eye)
    causal = ri >= ci
    strict = ri > ci
    dec = jnp.where(causal, jnp.exp(jnp.where(causal, g_col - g_row, 0.0)), 0.0)
    gam = jnp.exp(g_col)
    g_last = g_row[:, A_CHUNK - 1:A_CHUNK]
    gam_last = jnp.exp(g_last)
    e = jnp.exp(g_last - g_col)
    kb = k * b_col
    bv = v * b_col
    kbg = kb * gam
    kk = _dot(kb, k, 1, 1)
    p = _dot(q, k, 1, 1) * dec
    yield
    a_mat = jnp.where(strict, kk * dec, 0.0)
    if t_mat is None:
        t_mat = yield from _unit_lower_inverse(a_mat, ri, ci)
    u = _dot(t_mat, bv, 1, 0)
    w = _dot(t_mat, kbg, 1, 0)
    yield
    return dict(eye=eye, g_col=g_col, b_col=b_col, dec=dec, strict=strict, causal=causal, gam=gam,
                gam_last=gam_last, e=e, kb=kb, bv=bv, kbg=kbg, a_mat=a_mat, t_mat=t_mat, u=u, w=w, p=p,
                qg=q * gam, kd=k * e)


def _gdn_fwd(proj_hm, cw_hm, beta, gc, norm_g):
    bn, s, _ = proj_hm.shape
    n = s // A_CHUNK

    def body(x_ref, cw_ref, beta_ref, gc_ref, ng_ref, og_ref, oraw_ref, st_ref, t_ref, state):
        ri = lax.broadcasted_iota(jnp.int32, (A_CHUNK, A_CHUNK), 0)
        ci = lax.broadcasted_iota(jnp.int32, (A_CHUNK, A_CHUNK), 1)
        cw = cw_ref[...]
        ng = ng_ref[...]
        state[...] = jnp.zeros_like(state)

        def chunk(i, first):
            rows = pl.ds(0 if first else pl.multiple_of(i * A_CHUNK, A_CHUNK), A_CHUNK)
            cin = _gdn_chunk_inputs(x_ref, cw, i, first)
            core = _gdn_chunk_core(cin["q"], cin["k"], cin["v"], gc_ref[pl.ds(i, 1), :],
                                   beta_ref[pl.ds(i, 1), :], None, ri, ci)
            st = state[...]
            st_ref[i] = st
            t_ref[i] = core["t_mat"]
            vn = core["u"] - _dot(core["w"], st, 1, 0)
            o = _dot(core["qg"], st, 1, 0) + _dot(core["p"], vn, 1, 0)
            state[...] = st * core["gam_last"] + _dot(core["kd"], vn, 0, 0)
            oraw_ref[rows, :] = o
            r = lax.rsqrt(jnp.mean(o * o, axis=1, keepdims=True) + EPS)
            z = x_ref[rows, A_CONV_COLS:A_HEAD_COLS]
            og_ref[rows, :] = (o * r * ng * _silu(z)).astype(BF16)

        chunk(0, True)
        lax.fori_loop(1, n, lambda i, c: (chunk(i, False), c)[1], 0)

    return pl.pallas_call(
        body, name="gdn_fwd", grid=(bn, A_HEADS),
        in_specs=[pl.BlockSpec((None, s, A_HEAD_COLS), lambda b, h: (b, 0, h)),
                  pl.BlockSpec((A_CONV, A_CONV_COLS), lambda b, h: (0, h)),
                  pl.BlockSpec((None, None, n, A_CHUNK), lambda b, h: (b, h, 0, 0)),
                  pl.BlockSpec((None, None, n, A_CHUNK), lambda b, h: (b, h, 0, 0)),
                  pl.BlockSpec((1, A_DV), lambda b, h: (0, 0))],
        out_specs=[pl.BlockSpec((None, s, A_DV), lambda b, h: (b, 0, h)),
                   pl.BlockSpec((None, s, A_DV), lambda b, h: (b, 0, h)),
                   pl.BlockSpec((None, None, n, A_DK, A_DV), lambda b, h: (b, h, 0, 0, 0)),
                   pl.BlockSpec((None, None, n, A_CHUNK, A_CHUNK), lambda b, h: (b, h, 0, 0, 0))],
        out_shape=[jax.ShapeDtypeStruct((bn, s, A_VW), BF16),
                   jax.ShapeDtypeStruct((bn, s, A_VW), F32),
                   jax.ShapeDtypeStruct((bn, A_HEADS, n, A_DK, A_DV), F32),
                   jax.ShapeDtypeStruct((bn, A_HEADS, n, A_CHUNK, A_CHUNK), F32)],
        scratch_shapes=[pltpu.VMEM((A_DK, A_DV), F32)],
        compiler_params=_params(("parallel", "parallel"), VMEM_BIG),
    )(proj_hm, cw_hm, beta, gc, norm_g)


def _gdn_bwd(proj_hm, cw_hm, beta, gc, norm_g, oraw, states, t_mats, dog):
    bn, s, _ = proj_hm.shape
    n = s // A_CHUNK

    def body(x_ref, cw_ref, beta_ref, gc_ref, ng_ref, oraw_ref, st_ref, t_ref, dog_ref,
             dx_ref, dgc_ref, dbeta_ref, dcw_ref, dng_ref, dstate, dy_next):
        ri = lax.broadcasted_iota(jnp.int32, (A_CHUNK, A_CHUNK), 0)
        ci = lax.broadcasted_iota(jnp.int32, (A_CHUNK, A_CHUNK), 1)
        cw = cw_ref[...]
        ng = ng_ref[...]
        dstate[...] = jnp.zeros_like(dstate)
        dy_next[...] = jnp.zeros_like(dy_next)
        dcw_ref[...] = jnp.zeros_like(dcw_ref)
        dng_ref[...] = jnp.zeros_like(dng_ref)

        def chunk(i, first):
            rows = pl.ds(0 if first else pl.multiple_of(i * A_CHUNK, A_CHUNK), A_CHUNK)
            cin = _gdn_chunk_inputs(x_ref, cw, i, first)
            q, k, v = cin["q"], cin["k"], cin["v"]
            g_row = gc_ref[pl.ds(i, 1), :]
            b_row = beta_ref[pl.ds(i, 1), :]
            cr = _gdn_chunk_core(q, k, v, g_row, b_row, t_ref[i], ri, ci)
            eye, dec, gam, e = cr["eye"], cr["dec"], cr["gam"], cr["e"]
            b_col, t_mat, u, w, p = cr["b_col"], cr["t_mat"], cr["u"], cr["w"], cr["p"]
            st = st_ref[i]
            ds_out = dstate[...]

            o = oraw_ref[rows, :]
            z = x_ref[rows, A_CONV_COLS:A_HEAD_COLS]
            d_og = dog_ref[rows, :]
            r = lax.rsqrt(jnp.mean(o * o, axis=1, keepdims=True) + EPS)
            oh = o * r
            d_on = d_og * _silu(z)
            dz = d_og * oh * ng * _dsilu(z)
            dng_ref[0:1, :] += jnp.sum(d_on * oh, axis=0, keepdims=True)
            d_oh = d_on * ng
            d_o = r * (d_oh - oh * jnp.mean(d_oh * oh, axis=1, keepdims=True))

            vn = u - _dot(w, st, 1, 0)
            d_vn = _dot(p, d_o, 0, 0) + _dot(cr["kd"], ds_out, 1, 0)
            d_p = jnp.where(cr["causal"], _dot(d_o, vn, 1, 1), 0.0)
            d_qg = _dot(d_o, st, 1, 1)
            d_kd = _dot(vn, ds_out, 1, 1)
            d_gam_last = jnp.sum(jnp.sum(st * ds_out, axis=1, keepdims=True), axis=0, keepdims=True)
            d_w = -_dot(d_vn, st, 1, 1)
            dstate[...] = _dot(cr["qg"], d_o, 0, 0) + ds_out * cr["gam_last"] - _dot(w, d_vn, 0, 0)
            d_bv = _dot(t_mat, d_vn, 0, 0)
            d_kbg = _dot(t_mat, d_w, 0, 0)
            d_a = jnp.where(cr["strict"], -(_dot(d_bv, u, 1, 1) + _dot(d_kbg, w, 1, 1)), 0.0)
            m_a = d_a * dec
            n_p = d_p * dec
            d_kb = _dot(m_a, k, 1, 0) + d_kbg * gam
            d_q = _dot(n_p, k, 1, 0) + d_qg * gam
            d_k = (_dot(m_a, cr["kb"], 0, 0) + _dot(n_p, q, 0, 0) + d_kd * e + d_kb * b_col)
            d_v = d_bv * b_col
            d_beta_col = (jnp.sum(d_bv * v, axis=1, keepdims=True)
                          + jnp.sum(d_kb * k, axis=1, keepdims=True))
            gterm = d_a * cr["a_mat"] + d_p * p
            d_e = jnp.sum(d_kd * k, axis=1, keepdims=True) * e
            d_g_col = (jnp.sum(gterm, axis=1, keepdims=True)
                       + (jnp.sum(d_qg * q, axis=1, keepdims=True)
                          + jnp.sum(d_kbg * cr["kb"], axis=1, keepdims=True)) * gam
                       - d_e)
            d_g_last = jnp.sum(d_e, axis=0, keepdims=True) + d_gam_last * cr["gam_last"]
            lane = lax.broadcasted_iota(jnp.int32, (1, A_CHUNK), 1)
            d_g_row = (_col_to_row(d_g_col, eye) - jnp.sum(gterm, axis=0, keepdims=True)
                       + jnp.where(lane == A_CHUNK - 1, d_g_last, 0.0))
            dgc_ref[pl.ds(i, 1), :] = d_g_row
            dbeta_ref[pl.ds(i, 1), :] = _col_to_row(d_beta_col, eye)

            qh = cin["aq"] * cin["rq"]
            kh = cin["ak"] * cin["rk"]
            d_qh = d_q * (A_DK ** -0.5)
            d_aq = cin["rq"] * (d_qh - qh * jnp.sum(d_qh * qh, axis=1, keepdims=True))
            d_ak = cin["rk"] * (d_k - kh * jnp.sum(d_k * kh, axis=1, keepdims=True))
            d_y = jnp.concatenate([d_aq, d_ak, d_v], axis=1) * _dsilu(cin["y"])
            xw = cin["xw"]
            dyw = jnp.concatenate([d_y, dy_next[...]], axis=0)
            d_x = cw[A_CONV - 1:A_CONV, :] * dyw
            for j in range(1, A_CONV):
                d_x = d_x + cw[A_CONV - 1 - j:A_CONV - j, :] * pltpu.roll(dyw, A_CHUNK + HALO - j, 0)
            d_x = d_x[0:A_CHUNK, :]
            dy_pad = jnp.concatenate([jnp.zeros((HALO, A_CONV_COLS), F32), d_y], axis=0)
            for j in range(A_CONV):
                xs = xw if j == 0 else pltpu.roll(xw, j, 0)
                dcw_ref[A_CONV - 1 - j:A_CONV - j, :] += jnp.sum(dy_pad * xs, axis=0, keepdims=True)
            dy_next[...] = d_y[0:HALO, :]
            dx_ref[rows, 0:A_CONV_COLS] = d_x.astype(BF16)
            dx_ref[rows, A_CONV_COLS:A_HEAD_COLS] = dz.astype(BF16)

        lax.fori_loop(0, n - 1, lambda i, c: (chunk(n - 1 - i, False), c)[1], 0)
        chunk(0, True)

    hn = lambda b, h: (b, h, 0, 0)
    return pl.pallas_call(
        body, name="gdn_bwd", grid=(bn, A_HEADS),
        in_specs=[pl.BlockSpec((None, s, A_HEAD_COLS), lambda b, h: (b, 0, h)),
                  pl.BlockSpec((A_CONV, A_CONV_COLS), lambda b, h: (0, h)),
                  pl.BlockSpec((None, None, n, A_CHUNK), hn),
                  pl.BlockSpec((None, None, n, A_CHUNK), hn),
                  pl.BlockSpec((1, A_DV), lambda b, h: (0, 0)),
                  pl.BlockSpec((None, s, A_DV), lambda b, h: (b, 0, h)),
                  pl.BlockSpec((None, None, n, A_DK, A_DV), lambda b, h: (b, h, 0, 0, 0)),
                  pl.BlockSpec((None, None, n, A_CHUNK, A_CHUNK), lambda b, h: (b, h, 0, 0, 0)),
                  pl.BlockSpec((None, s, A_DV), lambda b, h: (b, 0, h))],
        out_specs=[pl.BlockSpec((None, s, A_HEAD_COLS), lambda b, h: (b, 0, h)),
                   pl.BlockSpec((None, None, n, A_CHUNK), hn),
                   pl.BlockSpec((None, None, n, A_CHUNK), hn),
                   pl.BlockSpec((None, A_CONV, A_CONV_COLS), lambda b, h: (b, 0, h)),
                   pl.BlockSpec((None, None, 8, A_DV), hn)],
        out_shape=[jax.ShapeDtypeStruct((bn, s, A_HEADS * A_HEAD_COLS), BF16),
                   jax.ShapeDtypeStruct((bn, A_HEADS, n, A_CHUNK), F32),
                   jax.ShapeDtypeStruct((bn, A_HEADS, n, A_CHUNK), F32),
                   jax.ShapeDtypeStruct((bn, A_CONV, A_HEADS * A_CONV_COLS), F32),
                   jax.ShapeDtypeStruct((bn, A_HEADS, 8, A_DV), F32)],
        scratch_shapes=[pltpu.VMEM((A_DK, A_DV), F32), pltpu.VMEM((HALO, A_CONV_COLS), F32)],
        compiler_params=_params(("parallel", "parallel"), VMEM_BIG),
    )(proj_hm, cw_hm, beta, gc, norm_g, oraw, states, t_mats, dog)


A_SEQ_BLK = 512
A_BLK_CHUNKS = A_SEQ_BLK // A_CHUNK


def _gdn_halo(proj_hm):
    bn, s, w = proj_hm.shape
    last = proj_hm.reshape(bn, s // A_SEQ_BLK, A_SEQ_BLK, w)[:, :, A_SEQ_BLK - HALO:, :]
    return jnp.concatenate([jnp.zeros((bn, 1, HALO, w), proj_hm.dtype), last[:, :-1]], axis=1)


def _gdn_window(x_ref, halo_ref, ci, first, lo):
    if first:
        return jnp.concatenate([halo_ref[:, lo:lo + A_CONV_COLS], x_ref[0:A_CHUNK, lo:lo + A_CONV_COLS]], axis=0)
    start = pl.multiple_of(ci * A_CHUNK - HALO, HALO)
    return x_ref[pl.ds(start, A_CHUNK + HALO), lo:lo + A_CONV_COLS]


def _gdn_chunk_prep(xw, cw):
    y = _conv_taps(xw, cw)
    a = _silu(y)
    aq, ak, v = a[:, 0:A_DK], a[:, A_DK:2 * A_DK], a[:, 2 * A_DK:]
    rq = lax.rsqrt(jnp.sum(aq * aq, axis=1, keepdims=True) + EPS)
    rk = lax.rsqrt(jnp.sum(ak * ak, axis=1, keepdims=True) + EPS)
    return dict(xw=xw, y=y, aq=aq, ak=ak, rq=rq, rk=rk, q=aq * rq * (A_DK ** -0.5), k=ak * rk, v=v)


def _gdn_fwd(proj_hm, cw_hm, beta, gc, norm_g, hp=4):
    bn, s, _ = proj_hm.shape
    n = s // A_CHUNK
    nsb = s // A_SEQ_BLK
    halo = _gdn_halo(proj_hm)

    def body(x_ref, halo_ref, cw_ref, beta_ref, gc_ref, ng_ref, og_ref, oraw_ref, st_ref, t_ref, state):
        ri = lax.broadcasted_iota(jnp.int32, (A_CHUNK, A_CHUNK), 0)
        ci_ = lax.broadcasted_iota(jnp.int32, (A_CHUNK, A_CHUNK), 1)
        ng = ng_ref[...]

        @pl.when(pl.program_id(2) == 0)
        def _():
            state[...] = jnp.zeros_like(state)

        def one_head(hh, ci, first, rows):
            lo = hh * A_HEAD_COLS
            cw = cw_ref[:, hh * A_CONV_COLS:(hh + 1) * A_CONV_COLS]
            cin = _gdn_chunk_prep(_gdn_window(x_ref, halo_ref, ci, first, lo), cw)
            core = yield from _gdn_chunk_core(cin["q"], cin["k"], cin["v"], gc_ref[hh, pl.ds(ci, 1), :],
                                              beta_ref[hh, pl.ds(ci, 1), :], None, ri, ci_)
            st = state[hh]
            st_ref[hh, ci] = st
            t_ref[hh, ci] = core["t_mat"]
            vn = core["u"] - _dot(core["w"], st, 1, 0)
            qs = _dot(core["qg"], st, 1, 0)
            yield
            o = qs + _dot(core["p"], vn, 1, 0)
            state[hh] = st * core["gam_last"] + _dot(core["kd"], vn, 0, 0)
            yield
            ocols = slice(hh * A_DV, (hh + 1) * A_DV)
            oraw_ref[rows, ocols] = o
            r = lax.rsqrt(jnp.mean(o * o, axis=1, keepdims=True) + EPS)
            z = x_ref[rows, lo + A_CONV_COLS:lo + A_HEAD_COLS]
            og_ref[rows, ocols] = (o * r * ng * _silu(z)).astype(BF16)

        def chunk(ci, first):
            rows = pl.ds(0 if first else pl.multiple_of(ci * A_CHUNK, A_CHUNK), A_CHUNK)
            _round_robin([one_head(hh, ci, first, rows) for hh in range(hp)])

        chunk(0, True)
        lax.fori_loop(1, A_BLK_CHUNKS, lambda i, c: (chunk(i, False), c)[1], 0)

    small = pl.BlockSpec((None, hp, A_BLK_CHUNKS, A_CHUNK), lambda b, h, j: (b, h, j, 0))
    return pl.pallas_call(
        body, name="gdn_fwd", grid=(bn, A_HEADS // hp, nsb),
        in_specs=[pl.BlockSpec((None, A_SEQ_BLK, hp * A_HEAD_COLS), lambda b, h, j: (b, j, h)),
                  pl.BlockSpec((None, None, HALO, hp * A_HEAD_COLS), lambda b, h, j: (b, j, 0, h)),
                  pl.BlockSpec((A_CONV, hp * A_CONV_COLS), lambda b, h, j: (0, h)),
                  small, small,
                  pl.BlockSpec((1, A_DV), lambda b, h, j: (0, 0))],
        out_specs=[pl.BlockSpec((None, A_SEQ_BLK, hp * A_DV), lambda b, h, j: (b, j, h)),
                   pl.BlockSpec((None, A_SEQ_BLK, hp * A_DV), lambda b, h, j: (b, j, h)),
                   pl.BlockSpec((None, hp, A_BLK_CHUNKS, A_DK, A_DV), lambda b, h, j: (b, h, j, 0, 0)),
                   pl.BlockSpec((None, hp, A_BLK_CHUNKS, A_CHUNK, A_CHUNK), lambda b, h, j: (b, h, j, 0, 0))],
        out_shape=[jax.ShapeDtypeStruct((bn, s, A_VW), BF16),
                   jax.ShapeDtypeStruct((bn, s, A_VW), F32),
                   jax.ShapeDtypeStruct((bn, A_HEADS, n, A_DK, A_DV), F32),
                   jax.ShapeDtypeStruct((bn, A_HEADS, n, A_CHUNK, A_CHUNK), F32)],
        scratch_shapes=[pltpu.VMEM((hp, A_DK, A_DV), F32)],
        compiler_params=_params(("parallel", "parallel", "arbitrary"), VMEM_BIG),
    )(proj_hm, halo, cw_hm, beta, gc, norm_g)


def _gdn_bwd(proj_hm, cw_hm, beta, gc, norm_g, oraw, states, t_mats, dog, hp=4):
    bn, s, _ = proj_hm.shape
    n = s // A_CHUNK
    nsb = s // A_SEQ_BLK
    halo = _gdn_halo(proj_hm)

    def body(x_ref, halo_ref, cw_ref, beta_ref, gc_ref, ng_ref, oraw_ref, st_ref, t_ref, dog_ref,
             dx_ref, dgc_ref, dbeta_ref, dcw_ref, dng_ref, dstate, dy_next):
        ri = lax.broadcasted_iota(jnp.int32, (A_CHUNK, A_CHUNK), 0)
        ci_ = lax.broadcasted_iota(jnp.int32, (A_CHUNK, A_CHUNK), 1)
        lane = lax.broadcasted_iota(jnp.int32, (1, A_CHUNK), 1)
        ng = ng_ref[...]

        @pl.when(pl.program_id(2) == 0)
        def _():
            dstate[...] = jnp.zeros_like(dstate)
            dy_next[...] = jnp.zeros_like(dy_next)
            dcw_ref[...] = jnp.zeros_like(dcw_ref)
            dng_ref[...] = jnp.zeros_like(dng_ref)

        def one_head(hh, ci, first, rows):
            lo = hh * A_HEAD_COLS
            ccols = slice(hh * A_CONV_COLS, (hh + 1) * A_CONV_COLS)
            ocols = slice(hh * A_DV, (hh + 1) * A_DV)
            cw = cw_ref[:, ccols]
            cin = _gdn_chunk_prep(_gdn_window(x_ref, halo_ref, ci, first, lo), cw)
            q, k, v = cin["q"], cin["k"], cin["v"]
            cr = yield from _gdn_chunk_core(q, k, v, gc_ref[hh, pl.ds(ci, 1), :], beta_ref[hh, pl.ds(ci, 1), :],
                                            t_ref[hh, ci], ri, ci_)
            eye, dec, gam, e = cr["eye"], cr["dec"], cr["gam"], cr["e"]
            b_col, t_mat, u, w, p = cr["b_col"], cr["t_mat"], cr["u"], cr["w"], cr["p"]
            st = st_ref[hh, ci]
            ds_out = dstate[hh]

            o = oraw_ref[rows, ocols]
            z = x_ref[rows, lo + A_CONV_COLS:lo + A_HEAD_COLS]
            d_og = dog_ref[rows, ocols]
            r = lax.rsqrt(jnp.mean(o * o, axis=1, keepdims=True) + EPS)
            oh = o * r
            d_on = d_og * _silu(z)
            dz = d_og * oh * ng * _dsilu(z)
            dng_ref[hh, 0:1, :] += jnp.sum(d_on * oh, axis=0, keepdims=True)
            d_oh = d_on * ng
            d_o = r * (d_oh - oh * jnp.mean(d_oh * oh, axis=1, keepdims=True))

            vn = u - _dot(w, st, 1, 0)
            d_vn = _dot(p, d_o, 0, 0) + _dot(cr["kd"], ds_out, 1, 0)
            d_qg = _dot(d_o, st, 1, 1)
            qgdo = _dot(cr["qg"], d_o, 0, 0)
            yield
            d_p = jnp.where(cr["causal"], _dot(d_o, vn, 1, 1), 0.0)
            d_kd = _dot(vn, ds_out, 1, 1)
            d_gam_last = jnp.sum(jnp.sum(st * ds_out, axis=1, keepdims=True), axis=0, keepdims=True)
            d_w = -_dot(d_vn, st, 1, 1)
            dstate[hh] = qgdo + ds_out * cr["gam_last"] - _dot(w, d_vn, 0, 0)
            d_bv = _dot(t_mat, d_vn, 0, 0)
            yield
            d_kbg = _dot(t_mat, d_w, 0, 0)
            n_p = d_p * dec
            d_q = _dot(n_p, k, 1, 0) + d_qg * gam
            npq = _dot(n_p, q, 0, 0)
            yield
            d_a = jnp.where(cr["strict"], -(_dot(d_bv, u, 1, 1) + _dot(d_kbg, w, 1, 1)), 0.0)
            yield
            m_a = d_a * dec
            d_kb = _dot(m_a, k, 1, 0) + d_kbg * gam
            d_k = (_dot(m_a, cr["kb"], 0, 0) + npq + d_kd * e + d_kb * b_col)
            yield
            d_v = d_bv * b_col
            d_beta_col = (jnp.sum(d_bv * v, axis=1, keepdims=True)
                          + jnp.sum(d_kb * k, axis=1, keepdims=True))
            gterm = d_a * cr["a_mat"] + d_p * p
            d_e = jnp.sum(d_kd * k, axis=1, keepdims=True) * e
            d_g_col = (jnp.sum(gterm, axis=1, keepdims=True)
                       + (jnp.sum(d_qg * q, axis=1, keepdims=True)
                          + jnp.sum(d_kbg * cr["kb"], axis=1, keepdims=True)) * gam
                       - d_e)
            d_g_last = jnp.sum(d_e, axis=0, keepdims=True) + d_gam_last * cr["gam_last"]
            d_g_row = (_col_to_row(d_g_col, eye) - jnp.sum(gterm, axis=0, keepdims=True)
                       + jnp.where(lane == A_CHUNK - 1, d_g_last, 0.0))
            dgc_ref[hh, pl.ds(ci, 1), :] = d_g_row
            dbeta_ref[hh, pl.ds(ci, 1), :] = _col_to_row(d_beta_col, eye)

            qh = cin["aq"] * cin["rq"]
            kh = cin["ak"] * cin["rk"]
            d_qh = d_q * (A_DK ** -0.5)
            d_aq = cin["rq"] * (d_qh - qh * jnp.sum(d_qh * qh, axis=1, keepdims=True))
            d_ak = cin["rk"] * (d_k - kh * jnp.sum(d_k * kh, axis=1, keepdims=True))
            d_y = jnp.concatenate([d_aq, d_ak, d_v], axis=1) * _dsilu(cin["y"])
            xw = cin["xw"]
            dyw = jnp.concatenate([d_y, dy_next[hh]], axis=0)
            d_x = cw[A_CONV - 1:A_CONV, :] * dyw
            for j in range(1, A_CONV):
                d_x = d_x + cw[A_CONV - 1 - j:A_CONV - j, :] * pltpu.roll(dyw, A_CHUNK + HALO - j, 0)
            dy_pad = jnp.concatenate([jnp.zeros((HALO, A_CONV_COLS), F32), d_y], axis=0)
            for j in range(A_CONV):
                xs = xw if j == 0 else pltpu.roll(xw, j, 0)
                dcw_ref[A_CONV - 1 - j:A_CONV - j, ccols] += jnp.sum(dy_pad * xs, axis=0, keepdims=True)
            dy_next[hh] = d_y[0:HALO, :]
            dx_ref[rows, lo:lo + A_CONV_COLS] = d_x[0:A_CHUNK, :].astype(BF16)
            dx_ref[rows, lo + A_CONV_COLS:lo + A_HEAD_COLS] = dz.astype(BF16)

        def chunk(ci, first):
            rows = pl.ds(0 if first else pl.multiple_of(ci * A_CHUNK, A_CHUNK), A_CHUNK)
            _round_robin([one_head(hh, ci, first, rows) for hh in range(hp)])

        lax.fori_loop(0, A_BLK_CHUNKS - 1, lambda i, c: (chunk(A_BLK_CHUNKS - 1 - i, False), c)[1], 0)
        chunk(0, True)

    rev = lambda j: nsb - 1 - j
    small = pl.BlockSpec((None, hp, A_BLK_CHUNKS, A_CHUNK), lambda b, h, j: (b, h, rev(j), 0))
    wide = pl.BlockSpec((None, A_SEQ_BLK, hp * A_HEAD_COLS), lambda b, h, j: (b, rev(j), h))
    val = pl.BlockSpec((None, A_SEQ_BLK, hp * A_DV), lambda b, h, j: (b, rev(j), h))
    return pl.pallas_call(
        body, name="gdn_bwd", grid=(bn, A_HEADS // hp, nsb),
        in_specs=[wide,
                  pl.BlockSpec((None, None, HALO, hp * A_HEAD_COLS), lambda b, h, j: (b, rev(j), 0, h)),
                  pl.BlockSpec((A_CONV, hp * A_CONV_COLS), lambda b, h, j: (0, h)),
                  small, small,
                  pl.BlockSpec((1, A_DV), lambda b, h, j: (0, 0)),
                  val,
                  pl.BlockSpec((None, hp, A_BLK_CHUNKS, A_DK, A_DV), lambda b, h, j: (b, h, rev(j), 0, 0)),
                  pl.BlockSpec((None, hp, A_BLK_CHUNKS, A_CHUNK, A_CHUNK), lambda b, h, j: (b, h, rev(j), 0, 0)),
                  val],
        out_specs=[wide, small, small,
                   pl.BlockSpec((None, A_CONV, hp * A_CONV_COLS), lambda b, h, j: (b, 0, h)),
                   pl.BlockSpec((None, hp, 8, A_DV), lambda b, h, j: (b, h, 0, 0))],
        out_shape=[jax.ShapeDtypeStruct((bn, s, A_HEADS * A_HEAD_COLS), BF16),
                   jax.ShapeDtypeStruct((bn, A_HEADS, n, A_CHUNK), F32),
                   jax.ShapeDtypeStruct((bn, A_HEADS, n, A_CHUNK), F32),
                   jax.ShapeDtypeStruct((bn, A_CONV, A_HEADS * A_CONV_COLS), F32),
                   jax.ShapeDtypeStruct((bn, A_HEADS, 8, A_DV), F32)],
        scratch_shapes=[pltpu.VMEM((hp, A_DK, A_DV), F32), pltpu.VMEM((hp, HALO, A_CONV_COLS), F32)],
        compiler_params=_params(("parallel", "parallel", "arbitrary"), VMEM_BIG),
    )(proj_hm, halo, cw_hm, beta, gc, norm_g, oraw, states, t_mats, dog)


def _rope_tables(posf, inv_freq_row):
    t = posf.shape[0]
    tm = 512

    def body(p_ref, f_ref, c_ref, sa_ref, sb_ref):
        ang = p_ref[...] * f_ref[...]
        lane = lax.broadcasted_iota(jnp.int32, ang.shape, 1)
        half = ROPE_DIMS // 2
        c_ref[...] = jnp.where(lane < ROPE_DIMS, jnp.cos(ang), 1.0)
        sn = jnp.sin(ang)
        sa_ref[...] = jnp.where(lane < half, -sn, 0.0)
        sb_ref[...] = jnp.where((lane >= half) & (lane < ROPE_DIMS), sn, 0.0)

    row = pl.BlockSpec((tm, 128), lambda i: (i, 0))
    return pl.pallas_call(
        body, name="rope_tables", grid=(t // tm,),
        in_specs=[row, pl.BlockSpec((1, 128), lambda i: (0, 0))], out_specs=[row] * 3,
        out_shape=[jax.ShapeDtypeStruct((t, 128), F32)] * 3,
        compiler_params=_params(("parallel",)),
    )(posf, inv_freq_row)


def _rope(x, c, sa, sb):
    half = ROPE_DIMS // 2
    return x * c + pltpu.roll(x, 128 - half, 1) * sa + pltpu.roll(x, half, 1) * sb


def _rope_t(d, c, sa, sb):
    half = ROPE_DIMS // 2
    return d * c + pltpu.roll(d * sa, half, 1) + pltpu.roll(d * sb, 128 - half, 1)


def _qk_prep(proj, c, sa, sb, qg, kg, name, tm=256):
    t = proj.shape[0]
    wide = proj.shape[1]

    def body(x_ref, c_ref, sa_ref, sb_ref, qg_ref, kg_ref, o_ref):
        cc, s1, s2 = c_ref[...], sa_ref[...], sb_ref[...]
        for which, g_ref in ((0, qg_ref), (1, kg_ref)):
            g = g_ref[...]
            for h in range(B_HEADS):
                lo = which * B_W + h * B_DH
                xv = x_ref[:, lo:lo + B_DH]
                r = lax.rsqrt(jnp.mean(xv * xv, axis=1, keepdims=True) + EPS)
                o_ref[:, lo:lo + B_DH] = _rope(xv * r * g, cc, s1, s2).astype(BF16)
        o_ref[:, 2 * B_W:3 * B_W] = x_ref[:, 2 * B_W:3 * B_W].astype(BF16)

    tab = pl.BlockSpec((tm, 128), lambda i: (i, 0))
    gain = pl.BlockSpec((1, B_DH), lambda i: (0, 0))
    return pl.pallas_call(
        body, name=name, grid=(t // tm,),
        in_specs=[pl.BlockSpec((tm, wide), lambda i: (i, 0)), tab, tab, tab, gain, gain],
        out_specs=pl.BlockSpec((tm, 3 * B_W), lambda i: (i, 0)),
        out_shape=jax.ShapeDtypeStruct((t, 3 * B_W), BF16),
        compiler_params=_params(("parallel",), 40 * 1024 * 1024),
    )(proj, c, sa, sb, qg, kg)


def _qk_prep_bwd(proj, c, sa, sb, qg, kg, dq, dk, dv, dz, name, tm=256):
    t = proj.shape[0]
    wide = proj.shape[1]
    out_w = 3 * B_W + (B_W if dz is not None else 0)

    def body(*refs):
        x_ref, c_ref, sa_ref, sb_ref, qg_ref, kg_ref, dq_ref, dk_ref, dv_ref = refs[:9]
        if dz is not None:
            dz_ref, o_ref, dgain_ref = refs[9:]
        else:
            o_ref, dgain_ref = refs[9:]
        i = pl.program_id(0)

        @pl.when(i == 0)
        def _():
            dgain_ref[...] = jnp.zeros_like(dgain_ref)

        cc, s1, s2 = c_ref[...], sa_ref[...], sb_ref[...]
        for which, g_ref, d_ref in ((0, qg_ref, dq_ref), (1, kg_ref, dk_ref)):
            g = g_ref[...]
            acc = jnp.zeros((1, B_DH), F32)
            for h in range(B_HEADS):
                lo = which * B_W + h * B_DH
                xv = x_ref[:, lo:lo + B_DH]
                r = lax.rsqrt(jnp.mean(xv * xv, axis=1, keepdims=True) + EPS)
                xh = xv * r
                d_xn = _rope_t(d_ref[:, h * B_DH:(h + 1) * B_DH], cc, s1, s2)
                acc = acc + jnp.sum(d_xn * xh, axis=0, keepdims=True)
                d_xh = d_xn * g
                d_x = r * (d_xh - xh * jnp.mean(d_xh * xh, axis=1, keepdims=True))
                o_ref[:, lo:lo + B_DH] = d_x.astype(BF16)
            dgain_ref[which:which + 1, :] += acc
        o_ref[:, 2 * B_W:3 * B_W] = dv_ref[...].astype(BF16)
        if dz is not None:
            o_ref[:, 3 * B_W:4 * B_W] = dz_ref[...]

    tab = pl.BlockSpec((tm, 128), lambda i: (i, 0))
    gain = pl.BlockSpec((1, B_DH), lambda i: (0, 0))
    grad = pl.BlockSpec((tm, B_W), lambda i: (i, 0))
    in_specs = [pl.BlockSpec((tm, wide), lambda i: (i, 0)), tab, tab, tab, gain, gain, grad, grad, grad]
    args = [proj, c, sa, sb, qg, kg, dq, dk, dv]
    if dz is not None:
        in_specs.append(grad)
        args.append(dz)
    return pl.pallas_call(
        body, name=name, grid=(t // tm,), in_specs=in_specs,
        out_specs=[pl.BlockSpec((tm, out_w), lambda i: (i, 0)), pl.BlockSpec((8, B_DH), lambda i: (0, 0))],
        out_shape=[jax.ShapeDtypeStruct((t, out_w), BF16), jax.ShapeDtypeStruct((8, B_DH), F32)],
        compiler_params=_params(("arbitrary",), 40 * 1024 * 1024),
    )(*args)


def _attn_masks():
    qi = lax.broadcasted_iota(jnp.int32, (B_BLK, 2 * B_BLK), 0)
    kj = lax.broadcasted_iota(jnp.int32, (B_BLK, 2 * B_BLK), 1)
    two = (kj >= qi) & (kj <= qi + B_BLK)
    q1 = lax.broadcasted_iota(jnp.int32, (B_BLK, B_BLK), 0)
    k1 = lax.broadcasted_iota(jnp.int32, (B_BLK, B_BLK), 1)
    return k1 <= q1, two


def _lane_pick(ref_rows, h):
    lane = lax.broadcasted_iota(jnp.int32, ref_rows.shape, 1)
    return jnp.sum(jnp.where(lane == h, ref_rows, 0.0), axis=1, keepdims=True)


def _attn_fwd(qkv, name):
    ns, ln, _ = qkv.shape
    nb = ln // B_BLK
    scale = B_DH ** -0.5

    def body(q_ref, k_ref, v_ref, o_ref, lse_ref):
        h = pl.program_id(1)
        mask1, mask2 = _attn_masks()

        @pl.when(h == 0)
        def _():
            lse_ref[...] = jnp.zeros_like(lse_ref)

        def block(i, first):
            rows = pl.ds(pl.multiple_of(i * B_BLK, B_BLK), B_BLK)
            if first:
                win, mask = pl.ds(0, B_BLK), mask1
            else:
                win, mask = pl.ds(pl.multiple_of((i - 1) * B_BLK, B_BLK), 2 * B_BLK), mask2
            sc = jnp.where(mask, _dot(q_ref[rows, :], k_ref[win, :], 1, 1) * scale, -1e30)
            m = jnp.max(sc, axis=1, keepdims=True)
            p = jnp.exp(sc - m)
            l = jnp.sum(p, axis=1, keepdims=True)
            o_ref[rows, :] = _dot(p, v_ref[win, :], 1, 0) / l
            lane = lax.broadcasted_iota(jnp.int32, (B_BLK, B_HEADS), 1)
            lse_ref[rows, :] = jnp.where(lane == h, m + jnp.log(l), lse_ref[rows, :])

        block(0, True)
        if nb > 1:
            lax.fori_loop(1, nb, lambda i, c: (block(i, False), c)[1], 0)

    return pl.pallas_call(
        body, name=name, grid=(ns, B_HEADS),
        in_specs=[pl.BlockSpec((None, ln, B_DH), lambda s, h: (s, 0, h)),
                  pl.BlockSpec((None, ln, B_DH), lambda s, h: (s, 0, B_HEADS + h)),
                  pl.BlockSpec((None, ln, B_DH), lambda s, h: (s, 0, 2 * B_HEADS + h))],
        out_specs=[pl.BlockSpec((None, ln, B_DH), lambda s, h: (s, 0, h)),
                   pl.BlockSpec((None, ln, B_HEADS), lambda s, h: (s, 0, 0))],
        out_shape=[jax.ShapeDtypeStruct((ns, ln, B_W), F32), jax.ShapeDtypeStruct((ns, ln, B_HEADS), F32)],
        compiler_params=_params(("parallel", "arbitrary")),
    )(qkv, qkv, qkv)


def _attn_bwd(qkv, d_o, lse_joint, delta, name):
    ns, ln, _ = qkv.shape
    nb = ln // B_BLK
    scale = B_DH ** -0.5

    def body(q_ref, k_ref, v_ref, do_ref, lj_ref, dl_ref, dq_ref, dk_ref, dv_ref):
        h = pl.program_id(1)
        mask1, mask2 = _attn_masks()
        dk_ref[...] = jnp.zeros_like(dk_ref)
        dv_ref[...] = jnp.zeros_like(dv_ref)

        def block(i, first):
            rows = pl.ds(pl.multiple_of(i * B_BLK, B_BLK), B_BLK)
            if first:
                win, mask = pl.ds(0, B_BLK), mask1
            else:
                win, mask = pl.ds(pl.multiple_of((i - 1) * B_BLK, B_BLK), 2 * B_BLK), mask2
            q = q_ref[rows, :]
            d_out = do_ref[rows, :]
            l_col = _lane_pick(lj_ref[rows, :], h)
            d_col = _lane_pick(dl_ref[rows, :], h)
            sc = _dot(q, k_ref[win, :], 1, 1) * scale
            p = jnp.exp(jnp.where(mask, sc - l_col, -1e30))
            d_p = _dot(d_out, v_ref[win, :], 1, 1)
            d_s = p * (d_p - d_col) * scale
            dq_ref[rows, :] = _dot(d_s, k_ref[win, :], 1, 0)
            dk_ref[win, :] += _dot(d_s, q, 0, 0)
            dv_ref[win, :] += _dot(p, d_out, 0, 0)

        block(0, True)
        if nb > 1:
            lax.fori_loop(1, nb, lambda i, c: (block(i, False), c)[1], 0)

    head = lambda off: pl.BlockSpec((None, ln, B_DH), lambda s, h: (s, 0, off + h))
    small = pl.BlockSpec((None, ln, B_HEADS), lambda s, h: (s, 0, 0))
    return pl.pallas_call(
        body, name=name, grid=(ns, B_HEADS),
        in_specs=[head(0), head(B_HEADS), head(2 * B_HEADS), head(0), small, small],
        out_specs=[head(0)] * 3,
        out_shape=[jax.ShapeDtypeStruct((ns, ln, B_W), F32)] * 3,
        compiler_params=_params(("parallel", "parallel")),
    )(qkv, qkv, qkv, d_o, lse_joint, delta)


B_ROWS = 2048


def _attn_schedule(nb, sb, block):
    way = 4

    def run(items):
        for at in range(0, len(items), way):
            _round_robin([block(*it) for it in items[at:at + way]])

    run([(si, 0, True) for si in range(sb)])
    if nb == 1:
        return
    per = max(1, way // sb)
    lead = 1 + (nb - 1) % per
    if lead > 1:
        run([(si, i, False) for i in range(1, lead) for si in range(sb)])

    def step(it, carry):
        run([(si, lead + it * per + u, False) for u in range(per) for si in range(sb)])
        return carry

    lax.fori_loop(0, (nb - lead) // per, step, 0)


def _attn_rows(i, first):
    if first:
        return pl.ds(0, B_BLK), pl.ds(0, B_BLK)
    rows = pl.ds(pl.multiple_of(i * B_BLK, B_BLK), B_BLK)
    return rows, pl.ds(pl.multiple_of((i - 1) * B_BLK, B_BLK), 2 * B_BLK)


def _attn_fwd(qkv, name):
    ns, ln, _ = qkv.shape
    nb = ln // B_BLK
    sb = B_ROWS // ln
    scale = B_DH ** -0.5

    def body(q_ref, k_ref, v_ref, o_ref, lse_ref):
        h = pl.program_id(1)
        mask1, mask2 = _attn_masks()
        lane = lax.broadcasted_iota(jnp.int32, (B_BLK, B_HEADS), 1)

        @pl.when(h == 0)
        def _():
            lse_ref[...] = jnp.zeros_like(lse_ref)

        def block(si, i, first):
            rows, win = _attn_rows(i, first)
            mask = mask1 if first else mask2
            sc = jnp.where(mask, _dot(q_ref[si, rows, :], k_ref[si, win, :], 1, 1) * scale, -1e30)
            yield
            m = jnp.max(sc, axis=1, keepdims=True)
            p = jnp.exp(sc - m)
            l = jnp.sum(p, axis=1, keepdims=True)
            pv = _dot(p, v_ref[si, win, :], 1, 0)
            yield
            o_ref[si, rows, :] = pv / l
            lse_ref[si, rows, :] = jnp.where(lane == h, m + jnp.log(l), lse_ref[si, rows, :])

        _attn_schedule(nb, sb, block)

    head = lambda off: pl.BlockSpec((sb, ln, B_DH), lambda s, h: (s, 0, off + h))
    return pl.pallas_call(
        body, name=name, grid=(ns // sb, B_HEADS),
        in_specs=[head(0), head(B_HEADS), head(2 * B_HEADS)],
        out_specs=[head(0), pl.BlockSpec((sb, ln, B_HEADS), lambda s, h: (s, 0, 0))],
        out_shape=[jax.ShapeDtypeStruct((ns, ln, B_W), F32), jax.ShapeDtypeStruct((ns, ln, B_HEADS), F32)],
        compiler_params=_params(("parallel", "arbitrary")),
    )(qkv, qkv, qkv)


def _attn_bwd(qkv, d_o, lse_joint, delta, name):
    ns, ln, _ = qkv.shape
    nb = ln // B_BLK
    sb = B_ROWS // ln
    scale = B_DH ** -0.5

    def body(q_ref, k_ref, v_ref, do_ref, lj_ref, dl_ref, dq_ref, dk_ref, dv_ref):
        h = pl.program_id(1)
        mask1, mask2 = _attn_masks()
        dk_ref[...] = jnp.zeros_like(dk_ref)
        dv_ref[...] = jnp.zeros_like(dv_ref)

        def block(si, i, first):
            rows, win = _attn_rows(i, first)
            mask = mask1 if first else mask2
            q = q_ref[si, rows, :]
            d_out = do_ref[si, rows, :]
            l_col = _lane_pick(lj_ref[si, rows, :], h)
            d_col = _lane_pick(dl_ref[si, rows, :], h)
            sc = _dot(q, k_ref[si, win, :], 1, 1) * scale
            d_p = _dot(d_out, v_ref[si, win, :], 1, 1)
            yield
            p = jnp.exp(jnp.where(mask, sc - l_col, -1e30))
            d_s = p * (d_p - d_col) * scale
            d_q = _dot(d_s, k_ref[si, win, :], 1, 0)
            d_k = _dot(d_s, q, 0, 0)
            d_v = _dot(p, d_out, 0, 0)
            yield
            dq_ref[si, rows, :] = d_q
            dk_ref[si, win, :] += d_k
            dv_ref[si, win, :] += d_v

        _attn_schedule(nb, sb, block)

    head = lambda off: pl.BlockSpec((sb, ln, B_DH), lambda s, h: (s, 0, off + h))
    small = pl.BlockSpec((sb, ln, B_HEADS), lambda s, h: (s, 0, 0))
    return pl.pallas_call(
        body, name=name, grid=(ns // sb, B_HEADS),
        in_specs=[head(0), head(B_HEADS), head(2 * B_HEADS), head(0), small, small],
        out_specs=[head(0)] * 3,
        out_shape=[jax.ShapeDtypeStruct((ns, ln, B_W), F32)] * 3,
        compiler_params=_params(("parallel", "parallel")),
    )(qkv, qkv, qkv, d_o, lse_joint, delta)


def _merge_weights(lse_refs):
    ls = [r[...] for r in lse_refs]
    m = jnp.maximum(jnp.maximum(ls[0], ls[1]), ls[2])
    es = [jnp.exp(l - m) for l in ls]
    tot = es[0] + es[1] + es[2]
    return [e / tot for e in es], m + jnp.log(tot)


def _merge_fwd(outs, lses, proj0, tm=256):
    t = outs[0].shape[0]

    def body(o0, o1, o2, l0, l1, l2, z_ref, og_ref):
        wts, _ = _merge_weights((l0, l1, l2))
        for h in range(B_HEADS):
            cols = slice(h * B_DH, (h + 1) * B_DH)
            o = (wts[0][:, h:h + 1] * o0[:, cols] + wts[1][:, h:h + 1] * o1[:, cols]
                 + wts[2][:, h:h + 1] * o2[:, cols])
            og_ref[:, cols] = (o * _silu(z_ref[:, cols])).astype(BF16)

    wide = pl.BlockSpec((tm, B_W), lambda i: (i, 0))
    small = pl.BlockSpec((tm, B_HEADS), lambda i: (i, 0))
    return pl.pallas_call(
        body, name="merge_fwd", grid=(t // tm,),
        in_specs=[wide] * 3 + [small] * 3 + [pl.BlockSpec((tm, B_W), lambda i: (i, 3))],
        out_specs=wide, out_shape=jax.ShapeDtypeStruct((t, B_W), BF16),
        compiler_params=_params(("parallel",)),
    )(*outs, *lses, proj0)


def _merge_bwd(outs, lses, proj0, d_og, tm=256):
    t = outs[0].shape[0]

    def body(o0, o1, o2, l0, l1, l2, z_ref, dog_ref, do_ref, lj_ref, dl_ref, dz_ref):
        wts, lj = _merge_weights((l0, l1, l2))
        lj_ref[...] = lj
        lane = lax.broadcasted_iota(jnp.int32, (tm, B_HEADS), 1)
        delta = jnp.zeros((tm, B_HEADS), F32)
        for h in range(B_HEADS):
            cols = slice(h * B_DH, (h + 1) * B_DH)
            o = (wts[0][:, h:h + 1] * o0[:, cols] + wts[1][:, h:h + 1] * o1[:, cols]
                 + wts[2][:, h:h + 1] * o2[:, cols])
            z = z_ref[:, cols]
            d_g = dog_ref[:, cols]
            d_out = d_g * _silu(z)
            dz_ref[:, cols] = (d_g * o * _dsilu(z)).astype(BF16)
            do_ref[:, cols] = d_out.astype(BF16)
            delta = jnp.where(lane == h, jnp.sum(d_out * o, axis=1, keepdims=True), delta)
        dl_ref[...] = delta

    wide = pl.BlockSpec((tm, B_W), lambda i: (i, 0))
    small = pl.BlockSpec((tm, B_HEADS), lambda i: (i, 0))
    return pl.pallas_call(
        body, name="merge_bwd", grid=(t // tm,),
        in_specs=[wide] * 3 + [small] * 3 + [pl.BlockSpec((tm, B_W), lambda i: (i, 3)), wide],
        out_specs=[wide, small, small, wide],
        out_shape=[jax.ShapeDtypeStruct((t, B_W), BF16), jax.ShapeDtypeStruct((t, B_HEADS), F32),
                   jax.ShapeDtypeStruct((t, B_HEADS), F32), jax.ShapeDtypeStruct((t, B_W), BF16)],
        compiler_params=_params(("parallel",)),
    )(*outs, *lses, proj0, d_og)


def _adamw(w, g, m, v, name):
    r, c = w.shape
    tr = r
    for cand in (256, 128, 64, 32, 16, 8):
        if r % cand == 0:
            tr = cand
            break

    def body(w_ref, g_ref, m_ref, v_ref, d_ref, nm_ref, nv_ref):
        gv = g_ref[...]
        nm = ADAM_B1 * m_ref[...] + (1.0 - ADAM_B1) * gv
        nv = ADAM_B2 * v_ref[...] + (1.0 - ADAM_B2) * (gv * gv)
        m_hat = nm / (1.0 - ADAM_B1 ** ADAM_STEP)
        v_hat = nv / (1.0 - ADAM_B2 ** ADAM_STEP)
        d_ref[...] = -ADAM_LR * (m_hat / (jnp.sqrt(v_hat) + ADAM_EPS) + ADAM_WD * w_ref[...])
        nm_ref[...] = nm
        nv_ref[...] = nv

    blk = pl.BlockSpec((tr, c), lambda i: (i, 0))
    return pl.pallas_call(
        body, name=name, grid=(r // tr,), in_specs=[blk] * 4, out_specs=[blk] * 3,
        out_shape=[jax.ShapeDtypeStruct((r, c), F32)] * 3,
        compiler_params=_params(("parallel",)),
    )(w, g, m, v)


def _adam_update(w, gv, m, v):
    nm = ADAM_B1 * m + (1.0 - ADAM_B1) * gv
    nv = ADAM_B2 * v + (1.0 - ADAM_B2) * (gv * gv)
    m_hat = nm / (1.0 - ADAM_B1 ** ADAM_STEP)
    v_hat = nv / (1.0 - ADAM_B2 ** ADAM_STEP)
    return -ADAM_LR * (m_hat / (jnp.sqrt(v_hat) + ADAM_EPS) + ADAM_WD * w), nm, nv


def _adamw_shard(w, mine, theirs, m, v, half_index, name, tr=128):
    _, r, c = w.shape
    nhb = (r // 2) // tr

    def body(c_ref, w_ref, mine_ref, theirs_ref, m_ref, v_ref, g_ref, d_ref, nm_ref, nv_ref):
        is_mine = (pl.program_id(0) // nhb) == c_ref[0]
        gv = jnp.where(is_mine, mine_ref[...], theirs_ref[...])
        d, nm, nv = _adam_update(w_ref[...], gv, m_ref[...], v_ref[...])
        g_ref[...] = gv
        d_ref[...] = d
        nm_ref[...] = nm
        nv_ref[...] = nv

    full = pl.BlockSpec((None, tr, c), lambda i, cc: (0, i, 0))
    half = pl.BlockSpec((tr, c), lambda i, cc: (i % nhb, 0))
    return pl.pallas_call(
        body, name=name,
        grid_spec=pltpu.PrefetchScalarGridSpec(
            num_scalar_prefetch=1, grid=(2 * nhb,),
            in_specs=[full, half, half, full, full], out_specs=[full] * 4),
        out_shape=[jax.ShapeDtypeStruct(w.shape, F32)] * 4,
        compiler_params=_params(("parallel",), 40 * 1024 * 1024),
    )(half_index, w, mine, theirs, m, v)


def _pair_sum(own, other, half_index, name, tr=256):
    _, r, c = own.shape
    rh = r // 2
    tr = min(tr, rh)
    nrb = rh // tr

    def body(c_ref, own_ref, oth_ref, out_ref):
        out_ref[...] = (own_ref[...] + oth_ref[...].astype(F32)).astype(BF16)

    return pl.pallas_call(
        body, name=name,
        grid_spec=pltpu.PrefetchScalarGridSpec(
            num_scalar_prefetch=1, grid=(N_CHIPS, nrb),
            in_specs=[pl.BlockSpec((None, tr, c), lambda k, i, cc: (k, cc[0] * nrb + i, 0)),
                      pl.BlockSpec((None, tr, c), lambda k, i, cc: (k, i, 0))],
            out_specs=pl.BlockSpec((None, tr, c), lambda k, i, cc: (k, i, 0))),
        out_shape=jax.ShapeDtypeStruct((N_CHIPS, rh, c), BF16),
        compiler_params=_params(("parallel", "parallel")),
    )(half_index, own, other)


def _chip_sum(sums, others, chip_index, name, tr=256):
    _, r, c = sums.shape
    tr = min(tr, r)

    def body(k_ref, own_ref, oth_ref, out_ref):
        acc = own_ref[...].astype(F32)
        for j in range(N_CHIPS - 1):
            acc = acc + oth_ref[j].astype(F32)
        out_ref[...] = acc

    return pl.pallas_call(
        body, name=name,
        grid_spec=pltpu.PrefetchScalarGridSpec(
            num_scalar_prefetch=1, grid=(r // tr,),
            in_specs=[pl.BlockSpec((None, tr, c), lambda i, kk: (kk[0], i, 0)),
                      pl.BlockSpec((N_CHIPS - 1, tr, c), lambda i, kk: (0, i, 0))],
            out_specs=pl.BlockSpec((tr, c), lambda i, kk: (i, 0))),
        out_shape=jax.ShapeDtypeStruct((r, c), F32),
        compiler_params=_params(("parallel",)),
    )(chip_index, sums, others)


HBM = pl.BlockSpec(memory_space=pltpu.HBM)


def _place():
    x, y, c = lax.axis_index("x"), lax.axis_index("y"), lax.axis_index("c")
    chips = [(1 - x, y), (x, 1 - y), (1 - x, 1 - y)]
    return x, y, c, chips


def _weight_allgather(shards, conv_shard):
    na = len(shards)

    def body(*refs):
        ins = refs[:na]
        conv_in = refs[na]
        outs = refs[na + 1:2 * na + 1]
        conv_out = refs[2 * na + 1]
        send, recv, fsend, frecv, csend, crecv = refs[2 * na + 2:]
        x, y, c, chips = _place()
        me = 2 * x + y
        sib = (x, y, 1 - c)
        first, conv_cp = [], []
        for i in range(na):
            rh = ins[i].shape[0] // 2
            mine = pl.ds(c * rh, rh)
            for j, (px, py) in enumerate(chips):
                cp = pltpu.make_async_remote_copy(
                    src_ref=ins[i].at[mine], dst_ref=outs[i].at[me, mine],
                    send_sem=send.at[3 * i + j], recv_sem=recv.at[3 * i + j],
                    device_id=(px, py, c), device_id_type=MESH)
                cp.start()
                first.append(cp)
        for j, (px, py) in enumerate(chips):
            cp = pltpu.make_async_remote_copy(
                src_ref=conv_in, dst_ref=conv_out.at[me], send_sem=csend.at[j], recv_sem=crecv.at[j],
                device_id=(px, py, c), device_id_type=MESH)
            cp.start()
            conv_cp.append(cp)
        passed = []
        for i in range(na):
            rh = ins[i].shape[0] // 2
            mine = pl.ds(c * rh, rh)
            for j, (px, py) in enumerate(chips):
                slot = outs[i].at[2 * px + py, mine]
                pltpu.make_async_remote_copy(
                    src_ref=slot, dst_ref=slot, send_sem=send.at[3 * i + j], recv_sem=recv.at[3 * i + j],
                    device_id=(px, py, c), device_id_type=MESH).wait_recv()
                cp = pltpu.make_async_remote_copy(
                    src_ref=slot, dst_ref=slot, send_sem=fsend.at[3 * i + j], recv_sem=frecv.at[3 * i + j],
                    device_id=sib, device_id_type=MESH)
                cp.start()
                passed.append(cp)
        for i in range(na):
            rh = ins[i].shape[0] // 2
            theirs = pl.ds((1 - c) * rh, rh)
            for j, (px, py) in enumerate(chips):
                slot = outs[i].at[2 * px + py, theirs]
                pltpu.make_async_remote_copy(
                    src_ref=slot, dst_ref=slot, send_sem=fsend.at[3 * i + j], recv_sem=frecv.at[3 * i + j],
                    device_id=sib, device_id_type=MESH).wait_recv()
        for j, (px, py) in enumerate(chips):
            slot = conv_out.at[2 * px + py]
            pltpu.make_async_remote_copy(
                src_ref=slot, dst_ref=slot, send_sem=csend.at[j], recv_sem=crecv.at[j],
                device_id=(px, py, c), device_id_type=MESH).wait_recv()
        for cp in first + passed + conv_cp:
            cp.wait_send()

    out_shape = [jax.ShapeDtypeStruct((N_CHIPS,) + s.shape, s.dtype) for s in shards]
    out_shape.append(jax.ShapeDtypeStruct((N_CHIPS,) + conv_shard.shape, conv_shard.dtype))
    res = pl.pallas_call(
        body, name="weight_allgather",
        in_specs=[HBM] * (na + 1), out_specs=[HBM] * (na + 1), out_shape=out_shape,
        scratch_shapes=[pltpu.SemaphoreType.DMA((3 * na,)), pltpu.SemaphoreType.DMA((3 * na,)),
                        pltpu.SemaphoreType.DMA((3 * na,)), pltpu.SemaphoreType.DMA((3 * na,)),
                        pltpu.SemaphoreType.DMA((3,)), pltpu.SemaphoreType.DMA((3,))],
    )(*shards, conv_shard)
    my_chip = 2 * lax.axis_index("x") + lax.axis_index("y")
    pick = lambda got, own: [jnp.where(my_chip == k, own, got[k]) for k in range(N_CHIPS)]
    return [pick(g, s) for g, s in zip(res[:na], shards)], pick(res[na], conv_shard)


def _sibling_swap_halves(grads):
    na = len(grads)

    def body(*refs):
        ins, outs = refs[:na], refs[na:2 * na]
        send, recv = refs[2 * na:]
        x, y, c, _ = _place()
        sib = (x, y, 1 - c)
        cps = []
        for i in range(na):
            rh = ins[i].shape[1] // 2
            cp = pltpu.make_async_remote_copy(
                src_ref=ins[i].at[:, pl.ds((1 - c) * rh, rh), :], dst_ref=outs[i],
                send_sem=send.at[i], recv_sem=recv.at[i], device_id=sib, device_id_type=MESH)
            cp.start()
            cps.append(cp)
        for cp in cps:
            cp.wait()

    out_shape = [jax.ShapeDtypeStruct((g.shape[0], g.shape[1] // 2, g.shape[2]), g.dtype) for g in grads]
    return pl.pallas_call(
        body, name="grad_sibling_swap", in_specs=[HBM] * na, out_specs=[HBM] * na, out_shape=out_shape,
        scratch_shapes=[pltpu.SemaphoreType.DMA((na,)), pltpu.SemaphoreType.DMA((na,))],
    )(*grads)


def _chip_exchange(sums):
    na = len(sums)

    def body(*refs):
        ins, outs = refs[:na], refs[na:2 * na]
        send, recv = refs[2 * na:]
        x, y, c, chips = _place()
        cps = []
        for i in range(na):
            for j, (px, py) in enumerate(chips):
                cp = pltpu.make_async_remote_copy(
                    src_ref=ins[i].at[2 * px + py], dst_ref=outs[i].at[j],
                    send_sem=send.at[3 * i + j], recv_sem=recv.at[3 * i + j],
                    device_id=(px, py, c), device_id_type=MESH)
                cp.start()
                cps.append(cp)
        for i in range(na):
            for j, (px, py) in enumerate(chips):
                slot = outs[i].at[j]
                pltpu.make_async_remote_copy(
                    src_ref=slot, dst_ref=slot, send_sem=send.at[3 * i + j], recv_sem=recv.at[3 * i + j],
                    device_id=(px, py, c), device_id_type=MESH).wait_recv()
        for cp in cps:
            cp.wait_send()

    out_shape = [jax.ShapeDtypeStruct((3,) + s.shape[1:], s.dtype) for s in sums]
    return pl.pallas_call(
        body, name="grad_chip_exchange", in_specs=[HBM] * na, out_specs=[HBM] * na, out_shape=out_shape,
        scratch_shapes=[pltpu.SemaphoreType.DMA((3 * na,)), pltpu.SemaphoreType.DMA((3 * na,))],
    )(*sums)


def _sibling_swap_whole(halves):
    na = len(halves)

    def body(*refs):
        ins, outs = refs[:na], refs[na:2 * na]
        send, recv = refs[2 * na:]
        x, y, c, _ = _place()
        cps = []
        for i in range(na):
            cp = pltpu.make_async_remote_copy(
                src_ref=ins[i], dst_ref=outs[i], send_sem=send.at[i], recv_sem=recv.at[i],
                device_id=(x, y, 1 - c), device_id_type=MESH)
            cp.start()
            cps.append(cp)
        for cp in cps:
            cp.wait()

    out_shape = [jax.ShapeDtypeStruct(h.shape, h.dtype) for h in halves]
    return pl.pallas_call(
        body, name="grad_sibling_join", in_specs=[HBM] * na, out_specs=[HBM] * na, out_shape=out_shape,
        scratch_shapes=[pltpu.SemaphoreType.DMA((na,)), pltpu.SemaphoreType.DMA((na,))],
    )(*halves)


def _small_allreduce(vec):
    r, cdim = vec.shape
    n_dev = 8

    def body(v_ref, out_ref, buf, send, recv):
        x, y, c, _ = _place()
        me = 4 * x + 2 * y + c
        buf[me] = v_ref[...]
        cps = []
        for k in range(1, n_dev):
            dx, dy, dc = (k >> 2) & 1, (k >> 1) & 1, k & 1
            peer = (x ^ dx, y ^ dy, c ^ dc)
            cp = pltpu.make_async_remote_copy(
                src_ref=v_ref, dst_ref=buf.at[me], send_sem=send.at[k - 1], recv_sem=recv.at[k - 1],
                device_id=peer, device_id_type=MESH)
            cp.start()
            cps.append(cp)
        for k in range(1, n_dev):
            dx, dy, dc = (k >> 2) & 1, (k >> 1) & 1, k & 1
            src = 4 * (x ^ dx) + 2 * (y ^ dy) + (c ^ dc)
            slot = buf.at[src]
            pltpu.make_async_remote_copy(
                src_ref=slot, dst_ref=slot, send_sem=send.at[k - 1], recv_sem=recv.at[k - 1],
                device_id=(x ^ dx, y ^ dy, c ^ dc), device_id_type=MESH).wait_recv()
        for cp in cps:
            cp.wait_send()
        acc = buf[0]
        for k in range(1, n_dev):
            acc = acc + buf[k]
        out_ref[...] = acc

    vm = pl.BlockSpec(memory_space=pltpu.VMEM)
    return pl.pallas_call(
        body, name="small_allreduce", in_specs=[vm], out_specs=vm,
        out_shape=jax.ShapeDtypeStruct((r, cdim), F32),
        scratch_shapes=[pltpu.VMEM((n_dev, r, cdim), F32), pltpu.SemaphoreType.DMA((n_dev - 1,)),
                        pltpu.SemaphoreType.DMA((n_dev - 1,))],
    )(vec)


def _a_cols_to_head_major(w):
    lead = w.shape[:-1]
    q = w[..., :A_QK].reshape(lead + (A_HEADS, A_DK))
    k = w[..., A_QK:2 * A_QK].reshape(lead + (A_HEADS, A_DK))
    v = w[..., 2 * A_QK:2 * A_QK + A_VW].reshape(lead + (A_HEADS, A_DV))
    z = w[..., 2 * A_QK + A_VW:].reshape(lead + (A_HEADS, A_DV))
    return jnp.concatenate([q, k, v, z], axis=-1).reshape(lead + (A_HEADS * A_HEAD_COLS,))


def _a_cols_from_head_major(w):
    lead = w.shape[:-1]
    w = w.reshape(lead + (A_HEADS, A_HEAD_COLS))
    parts = [w[..., :A_DK], w[..., A_DK:2 * A_DK], w[..., 2 * A_DK:2 * A_DK + A_DV], w[..., 2 * A_DK + A_DV:]]
    return jnp.concatenate([p.reshape(lead + (-1,)) for p in parts], axis=-1)


def _conv_cols_to_head_major(w):
    lead = w.shape[:-1]
    q = w[..., :A_QK].reshape(lead + (A_HEADS, A_DK))
    k = w[..., A_QK:2 * A_QK].reshape(lead + (A_HEADS, A_DK))
    v = w[..., 2 * A_QK:].reshape(lead + (A_HEADS, A_DV))
    return jnp.concatenate([q, k, v], axis=-1).reshape(lead + (A_HEADS * A_CONV_COLS,))


def _conv_cols_from_head_major(w):
    lead = w.shape[:-1]
    w = w.reshape(lead + (A_HEADS, A_CONV_COLS))
    parts = [w[..., :A_DK], w[..., A_DK:2 * A_DK], w[..., 2 * A_DK:]]
    return jnp.concatenate([p.reshape(lead + (-1,)) for p in parts], axis=-1)


def _to_stream(a, bn, d):
    rest = a.shape[1:]
    s = a.shape[0] // bn
    a = a.reshape((bn, s // d, d) + rest)
    a = jnp.swapaxes(a, 1, 2)
    return a.reshape((bn * d, s // d) + rest)


def _from_stream(a, bn, d):
    rest = a.shape[2:]
    ln = a.shape[1]
    a = a.reshape((bn, d, ln) + rest)
    a = jnp.swapaxes(a, 1, 2)
    return a.reshape((bn * ln * d,) + rest)


def _b_group_cols(w, gi):
    n_qkv = 3 * B_GROUPS * B_W
    qkv = w[..., :n_qkv].reshape(w.shape[:-1] + (3, B_GROUPS, B_W))
    return qkv[..., :, gi, :].reshape(w.shape[:-1] + (3 * B_W,))


def _shard_major(g, ncols):
    r = g.shape[0]
    return jnp.swapaxes(g.reshape(r, N_CHIPS, ncols), 0, 1)


def _pack_rows(items):
    rows, offs = [], []
    at = 0
    for a in items:
        flat = a.reshape(-1).astype(F32)
        nr = -(-flat.shape[0] // 1024) * 8
        flat = jnp.pad(flat, (0, nr * 128 - flat.shape[0]))
        rows.append(flat.reshape(nr, 128))
        offs.append((at, nr, a.shape))
        at += nr
    return jnp.concatenate(rows, axis=0), offs


def _unpack_rows(packed, offs):
    out = []
    for at, nr, shape in offs:
        size = int(np.prod(shape)) if len(shape) else 1
        out.append(packed[at:at + nr].reshape(-1)[:size].reshape(shape))
    return out


def _local_step(x, positions, loss_target, norm_g, wa_in, conv_w, a_log, a_dt_bias, a_norm_g, wa_out,
                wb_in, b_q_norm_g, b_k_norm_g, wb_out):
    bn, s, d = x.shape
    t = bn * s
    n_chunks = s // A_CHUNK
    wa_main = _a_cols_to_head_major(wa_in[:, :A_MAIN])
    wa_tail = jnp.pad(wa_in[:, A_MAIN:], ((0, 0), (0, 128 - 2 * A_HEADS)))
    cw_hm = _conv_cols_to_head_major(conv_w)
    wb_groups = [_b_group_cols(wb_in, gi) for gi in range(B_GROUPS)]
    wb_groups[0] = jnp.concatenate([wb_groups[0], wb_in[:, 3 * B_GROUPS * B_W:]], axis=1)

    x0 = x.reshape(t, d)
    h0 = _rms_fwd(x0, norm_g[0:1], "rms0_fwd")
    proj_a = _matmul(h0, wa_main, "nn", F32, "a_in_main")
    tail_a = _matmul(h0, wa_tail, "nn", F32, "a_in_tail")
    tail_t = jnp.swapaxes(tail_a[:, :2 * A_HEADS].reshape(bn, s, 2 * A_HEADS), 1, 2)
    tail_t = tail_t.reshape(bn, 2 * A_HEADS, n_chunks, A_CHUNK)
    beta, gc = _gdn_prep(tail_t, a_log[0], a_dt_bias[0])
    proj_a3 = proj_a.reshape(bn, s, A_MAIN)
    og_a, oraw_a, states, t_mats = _gdn_fwd(proj_a3, cw_hm, beta, gc, a_norm_g)
    x1 = _matmul(og_a.reshape(t, A_VW), wa_out, "nn", F32, "a_out", res=x0, tk=2048)

    h1 = _rms_fwd(x1, norm_g[1:2], "rms1_fwd")
    inv_freq = ROPE_THETA ** (-jnp.arange(0, ROPE_DIMS, 2, dtype=F32) / ROPE_DIMS)
    freq_row = jnp.concatenate([inv_freq, inv_freq, jnp.zeros((128 - ROPE_DIMS,), F32)]).reshape(1, 128)
    posf = jnp.broadcast_to(positions.astype(F32).reshape(t, 1), (t, 128))
    tabs = _rope_tables(posf, freq_row)
    h1_s, tabs_s, proj_b, qkv_b, o_b, lse_b = [], [], [], [], [], []
    for gi, dil in enumerate(B_DIL):
        hs = h1 if dil == 1 else _to_stream(h1, bn, dil).reshape(t, d)
        ts = tabs if dil == 1 else [_to_stream(tb, bn, dil).reshape(t, 128) for tb in tabs]
        pj = _matmul(hs, wb_groups[gi], "nn", F32, f"b_in_g{gi}")
        qkv = _qk_prep(pj, *ts, b_q_norm_g[0, gi:gi + 1], b_k_norm_g[0, gi:gi + 1], f"qk_prep_g{gi}")
        o_s, lse_s = _attn_fwd(qkv.reshape(bn * dil, s // dil, 3 * B_W), f"attn_fwd_g{gi}")
        h1_s.append(hs), tabs_s.append(ts), proj_b.append(pj), qkv_b.append(qkv)
        o_b.append(o_s.reshape(t, B_W) if dil == 1 else _from_stream(o_s, bn, dil))
        lse_b.append(lse_s.reshape(t, B_HEADS) if dil == 1 else _from_stream(lse_s, bn, dil))
    og_b = _merge_fwd(o_b, lse_b, proj_b[0])
    x2 = _matmul(og_b, wb_out, "nn", F32, "b_out", res=x1)

    d_x2, loss_parts = _loss_grad(x2, loss_target.reshape(t, d))
    loss_local = jnp.sum(loss_parts)

    d_x2b = d_x2.astype(BF16)
    g_wb_out = _matmul(og_b.T, d_x2b, "nn", F32, "b_out_dw")
    d_og_b = _matmul(d_x2b, wb_out, "nt", F32, "b_out_dx")
    d_o, lse_joint, delta, d_z = _merge_bwd(o_b, lse_b, proj_b[0], d_og_b)
    d_h1, g_wb_cols, g_qn, g_kn = [], [], [], []
    for gi, dil in enumerate(B_DIL):
        if dil == 1:
            do_s, lj_s, dl_s = d_o, lse_joint, delta
        else:
            do_s, lj_s, dl_s = (_to_stream(a, bn, dil).reshape(t, -1) for a in (d_o, lse_joint, delta))
        ns, ln = bn * dil, s // dil
        dq, dk, dv = _attn_bwd(qkv_b[gi].reshape(ns, ln, 3 * B_W), do_s.reshape(ns, ln, B_W),
                               lj_s.reshape(ns, ln, B_HEADS), dl_s.reshape(ns, ln, B_HEADS), f"attn_bwd_g{gi}")
        d_pj, d_gain = _qk_prep_bwd(proj_b[gi], *tabs_s[gi], b_q_norm_g[0, gi:gi + 1], b_k_norm_g[0, gi:gi + 1],
                                    dq.reshape(t, B_W), dk.reshape(t, B_W), dv.reshape(t, B_W),
                                    d_z if gi == 0 else None, f"qk_prep_bwd_g{gi}")
        g_wb_cols.append(_matmul(h1_s[gi].T, d_pj, "nn", F32, f"b_in_dw_g{gi}"))
        dh = _matmul(d_pj, wb_groups[gi], "nt", F32, f"b_in_dx_g{gi}")
        d_h1.append(dh if dil == 1 else _from_stream(dh.reshape(ns, ln, d), bn, dil))
        g_qn.append(d_gain[0]), g_kn.append(d_gain[1])
    d_x1, g_norm1 = _rms_bwd(x1, norm_g[1:2], d_h1, d_x2, "rms1_bwd")
    pieces = [g_wb_cols[gi][:, w * B_W:(w + 1) * B_W] for w in range(3) for gi in range(B_GROUPS)]
    g_wb_in = jnp.concatenate(pieces + [g_wb_cols[0][:, 3 * B_W:]], axis=1)

    d_x1b = d_x1.astype(BF16)
    g_wa_out = _matmul(og_a.reshape(t, A_VW).T, d_x1b, "nn", F32, "a_out_dw")
    d_og_a = _matmul(d_x1b, wa_out, "nt", F32, "a_out_dx")
    d_pa, d_gc, d_beta, d_cw, d_ng = _gdn_bwd(proj_a3, cw_hm, beta, gc, a_norm_g, oraw_a, states, t_mats,
                                              d_og_a.reshape(bn, s, A_VW))
    d_tail_t, d_alog, d_dtb = _gdn_prep_bwd(tail_t, a_log[0], a_dt_bias[0], d_gc, d_beta)
    d_tail = jnp.swapaxes(d_tail_t.reshape(bn, 2 * A_HEADS, s), 1, 2).reshape(t, 2 * A_HEADS)
    d_tail = jnp.pad(d_tail, ((0, 0), (0, 128 - 2 * A_HEADS))).astype(BF16)
    d_pa = d_pa.reshape(t, A_MAIN)
    h0_t = h0.T
    g_wa_main = _matmul(h0_t, d_pa, "nn", F32, "a_in_dw_main")
    g_wa_tail = _matmul(h0_t, d_tail, "nn", F32, "a_in_dw_tail")
    d_h0 = _matmul(d_pa, wa_main, "nt", F32, "a_in_dx_main")
    d_h0t = _matmul(d_tail, wa_tail, "nt", F32, "a_in_dx_tail")
    d_x0, g_norm0 = _rms_bwd(x0, norm_g[0:1], [d_h0, d_h0t], d_x1, "rms0_bwd")
    g_wa_in = jnp.concatenate([_a_cols_from_head_major(g_wa_main), g_wa_tail[:, :2 * A_HEADS]], axis=1)

    gfull = {
        "norm_g": jnp.concatenate([g_norm0, g_norm1], axis=0), "a_w_in": g_wa_in,
        "a_conv_w": _conv_cols_from_head_major(jnp.sum(d_cw, axis=0)),
        "a_log": jnp.sum(d_alog[:, :, 0], axis=0), "a_dt_bias": jnp.sum(d_dtb[:, :, 0], axis=0),
        "a_norm_g": jnp.sum(d_ng[:, :, 0, :], axis=(0, 1)), "a_w_out": g_wa_out, "b_w_in": g_wb_in,
        "b_q_norm_g": jnp.stack(g_qn), "b_k_norm_g": jnp.stack(g_kn), "b_w_out": g_wb_out}
    return loss_local, d_x0.reshape(bn, s, d), gfull


def kernel(x, positions, norm_g, a_w_in, a_conv_w, a_log, a_dt_bias, a_norm_g, a_w_out, b_w_in, b_q_norm_g, b_k_norm_g, b_w_out, loss_target, m_norm_g, m_a_w_in, m_a_conv_w, m_a_log, m_a_dt_bias, m_a_norm_g, m_a_w_out, m_b_w_in, m_b_q_norm_g, m_b_k_norm_g, m_b_w_out, v_norm_g, v_a_w_in, v_a_conv_w, v_a_log, v_a_dt_bias, v_a_norm_g, v_a_w_out, v_b_w_in, v_b_q_norm_g, v_b_k_norm_g, v_b_w_out):
    d = x.shape[2]
    my_c = lax.axis_index("c")
    my_chip = 2 * lax.axis_index("x") + lax.axis_index("y")

    shards = [a_w_in[0].astype(BF16), a_w_out[0].astype(BF16), b_w_in[0].astype(BF16), b_w_out[0].astype(BF16)]
    (ga_in, ga_out, gb_in, gb_out), g_conv = _weight_allgather(shards, a_conv_w[0])
    wa_in = jnp.concatenate(ga_in, axis=1)
    wa_out = jnp.concatenate(ga_out, axis=0)
    wb_in = jnp.concatenate(gb_in, axis=1)
    wb_out = jnp.concatenate(gb_out, axis=0)
    conv_w = jnp.concatenate(g_conv, axis=1)

    loss_local, d_x0, gfull = _local_step(x, positions, loss_target, norm_g, wa_in, conv_w, a_log, a_dt_bias,
                                          a_norm_g, wa_out, wb_in, b_q_norm_g, b_k_norm_g, wb_out)

    g_full = [_shard_major(gfull["a_w_in"], a_w_in.shape[2]), gfull["a_w_out"].reshape(N_CHIPS, -1, d),
              _shard_major(gfull["b_w_in"], b_w_in.shape[2]), gfull["b_w_out"].reshape(N_CHIPS, -1, d)]
    recv_sib = _sibling_swap_halves([g.astype(BF16) for g in g_full])
    half_index = jnp.reshape(my_c, (1,)).astype(jnp.int32)
    chip_sums = [_pair_sum(g, r, half_index, f"grad_pair_sum_{i}") for i, (g, r) in enumerate(zip(g_full, recv_sib))]
    received = _chip_exchange(chip_sums)
    chip_index = jnp.reshape(my_chip, (1,)).astype(jnp.int32)
    halves = [_chip_sum(s, r, chip_index, f"grad_chip_sum_{i}") for i, (s, r) in enumerate(zip(chip_sums, received))]
    theirs = _sibling_swap_whole(halves)
    big = ("a_w_in", "a_w_out", "b_w_in", "b_w_out")
    big_halves = dict(zip(big, zip(halves, theirs)))

    small = [gfull["norm_g"], gfull["a_conv_w"], gfull["a_log"], gfull["a_dt_bias"], gfull["a_norm_g"],
             gfull["b_q_norm_g"], gfull["b_k_norm_g"], loss_local]
    packed, offs = _pack_rows(small)
    red = _unpack_rows(_small_allreduce(packed), offs)
    g_norm, g_conv_all, g_alog, g_dtb, g_ang, g_q, g_k, loss = red
    g_conv_mine = lax.dynamic_slice_in_dim(g_conv_all, my_chip * a_conv_w.shape[2], a_conv_w.shape[2], axis=1)

    grads = {
        "norm_g": g_norm, "a_conv_w": g_conv_mine[None], "a_log": g_alog[None], "a_dt_bias": g_dtb[None],
        "a_norm_g": g_ang[None], "b_q_norm_g": g_q[None], "b_k_norm_g": g_k[None]}
    weights = {"norm_g": norm_g, "a_w_in": a_w_in, "a_conv_w": a_conv_w, "a_log": a_log, "a_dt_bias": a_dt_bias,
               "a_norm_g": a_norm_g, "a_w_out": a_w_out, "b_w_in": b_w_in, "b_q_norm_g": b_q_norm_g,
               "b_k_norm_g": b_k_norm_g, "b_w_out": b_w_out}
    m_in = {"norm_g": m_norm_g, "a_w_in": m_a_w_in, "a_conv_w": m_a_conv_w, "a_log": m_a_log,
            "a_dt_bias": m_a_dt_bias, "a_norm_g": m_a_norm_g, "a_w_out": m_a_w_out, "b_w_in": m_b_w_in,
            "b_q_norm_g": m_b_q_norm_g, "b_k_norm_g": m_b_k_norm_g, "b_w_out": m_b_w_out}
    v_in = {"norm_g": v_norm_g, "a_w_in": v_a_w_in, "a_conv_w": v_a_conv_w, "a_log": v_a_log,
            "a_dt_bias": v_a_dt_bias, "a_norm_g": v_a_norm_g, "a_w_out": v_a_w_out, "b_w_in": v_b_w_in,
            "b_q_norm_g": v_b_q_norm_g, "b_k_norm_g": v_b_k_norm_g, "b_w_out": v_b_w_out}
    names = list(weights)

    delta_w, new_m, new_v = {}, {}, {}
    for nm in big:
        mine, other = big_halves[nm]
        grads[nm], delta_w[nm], new_m[nm], new_v[nm] = _adamw_shard(
            weights[nm], mine, other, m_in[nm], v_in[nm], half_index, f"adamw_{nm}")
    small_names = [nm for nm in names if nm not in big]
    packs = [_pack_rows([src[nm] for nm in small_names]) for src in (weights, grads, m_in, v_in)]
    offs = packs[0][1]
    dl, m2, v2 = _adamw(packs[0][0], packs[1][0], packs[2][0], packs[3][0], "adamw_small")
    for nm, a, b, c2 in zip(small_names, _unpack_rows(dl, offs), _unpack_rows(m2, offs), _unpack_rows(v2, offs)):
        delta_w[nm], new_m[nm], new_v[nm] = a, b, c2

    return (loss, d_x0, *[grads[nm] for nm in names], *[delta_w[nm] for nm in names],
            *[new_m[nm] for nm in names], *[new_v[nm] for nm in names])
```

```python
import functools
import math

import jax
import jax.numpy as jnp
import numpy as np
from jax import lax
from jax.experimental import pallas as pl
from jax.experimental.pallas import tpu as pltpu

F32 = jnp.float32
BF16 = jnp.bfloat16
MESH = pl.DeviceIdType.MESH

EPS = 1e-6
D_MODEL = 1024
A_HEADS = 8
A_DK = 128
A_DV = 256
A_QK = A_HEADS * A_DK
A_VW = A_HEADS * A_DV
A_MAIN = 2 * A_QK + 2 * A_VW
A_HEAD_COLS = 2 * A_DK + 2 * A_DV
A_CONV_COLS = 2 * A_DK + A_DV
A_CHUNK = 64
A_CONV = 4
B_GROUPS = 3
B_HEADS = 8
B_DH = 128
B_W = B_HEADS * B_DH
B_DIL = (1, 4, 16)
B_BLK = 128
ROPE_THETA = 500000.0
ROPE_DIMS = B_DH // 4
ADAM_LR, ADAM_B1, ADAM_B2, ADAM_EPS, ADAM_WD, ADAM_STEP = 0.001, 0.9, 0.999, 1e-08, 0.01, 10
N_CHIPS = 4
VMEM_BIG = 56 * 1024 * 1024


def _params(sem=None, vmem=None):
    return pltpu.CompilerParams(dimension_semantics=sem, vmem_limit_bytes=vmem)


def _dot(a, b, ca, cb):
    return lax.dot_general(a.astype(BF16), b.astype(BF16), (((ca,), (cb,)), ((), ())),
                           preferred_element_type=F32)


def _split3(a):
    hi = a.astype(BF16)
    r = a - hi.astype(F32)
    mid = r.astype(BF16)
    lo = (r - mid.astype(F32)).astype(BF16)
    return hi, mid, lo


def _dot_hi(a, b, ca, cb):
    a_hi, a_lo, _ = _split3(a)
    b_hi, b_lo, _ = _split3(b)
    dn = (((ca,), (cb,)), ((), ()))
    out = lax.dot_general(a_hi, b_hi, dn, preferred_element_type=F32)
    out = out + lax.dot_general(a_hi, b_lo, dn, preferred_element_type=F32)
    return out + lax.dot_general(a_lo, b_hi, dn, preferred_element_type=F32)


def _sigmoid(y):
    return 1.0 / (1.0 + jnp.exp(-y))


def _silu(y):
    return y * _sigmoid(y)


def _dsilu(y):
    s = _sigmoid(y)
    return s * (1.0 + y * (1.0 - s))


def _matmul(a, b, mode, out_dtype, name, res=None, tm=1024, tn=1024, tk=1024):
    if mode == "nn":
        (m, k), (_, n) = a.shape, b.shape
    else:
        (m, k), (n, _) = a.shape, b.shape
    tm, tn, tk = min(tm, m), min(tn, n), min(tk, k)
    assert m % tm == 0 and n % tn == 0 and k % tk == 0, (name, a.shape, b.shape)
    nk = k // tk
    dims = {"nn": ((1,), (0,)), "nt": ((1,), (1,))}[mode]

    def body(*refs):
        a_ref, b_ref = refs[0], refs[1]
        r_ref = refs[2] if res is not None else None
        o_ref = refs[3] if res is not None else refs[2]
        prod = lax.dot_general(a_ref[...], b_ref[...], (dims, ((), ())), preferred_element_type=F32)

        def finish(r):
            if res is not None:
                r = r + r_ref[...]
            o_ref[...] = r.astype(out_dtype)

        if nk == 1:
            finish(prod)
            return
        acc = refs[-1]
        kk = pl.program_id(2)

        @pl.when(kk == 0)
        def _():
            acc[...] = prod

        @pl.when((kk > 0) & (kk < nk - 1))
        def _():
            acc[...] += prod

        @pl.when(kk == nk - 1)
        def _():
            finish(acc[...] + prod)

    a_spec = pl.BlockSpec((tm, tk), lambda i, j, kk: (i, kk))
    if mode == "nt":
        b_spec = pl.BlockSpec((tn, tk), lambda i, j, kk: (j, kk))
    else:
        b_spec = pl.BlockSpec((tk, tn), lambda i, j, kk: (kk, j))
    in_specs = [a_spec, b_spec]
    args = [a, b]
    if res is not None:
        in_specs.append(pl.BlockSpec((tm, tn), lambda i, j, kk: (i, j)))
        args.append(res)
    return pl.pallas_call(
        body, name=name, grid=(m // tm, n // tn, nk),
        in_specs=in_specs, out_specs=pl.BlockSpec((tm, tn), lambda i, j, kk: (i, j)),
        out_shape=jax.ShapeDtypeStruct((m, n), out_dtype),
        scratch_shapes=[pltpu.VMEM((tm, tn), F32)] if nk > 1 else [],
        compiler_params=_params(("parallel", "parallel", "arbitrary"), 48 * 1024 * 1024),
    )(*args)


def _rms_fwd(x, g, name, tm=256):
    t, d = x.shape

    def body(x_ref, g_ref, h_ref):
        xv = x_ref[...]
        r = lax.rsqrt(jnp.mean(xv * xv, axis=-1, keepdims=True) + EPS)
        h_ref[...] = (xv * r * g_ref[...]).astype(BF16)

    return pl.pallas_call(
        body, name=name, grid=(t // tm,),
        in_specs=[pl.BlockSpec((tm, d), lambda i: (i, 0)), pl.BlockSpec((1, d), lambda i: (0, 0))],
        out_specs=pl.BlockSpec((tm, d), lambda i: (i, 0)),
        out_shape=jax.ShapeDtypeStruct((t, d), BF16),
        compiler_params=_params(("parallel",)),
    )(x, g)


def _rms_bwd(x, g, dhs, dres, name, tm=256):
    t, d = x.shape
    n_dh = len(dhs)

    def body(*refs):
        x_ref, g_ref = refs[0], refs[1]
        dh_refs = refs[2:2 + n_dh]
        dres_ref, dx_ref, dg_ref = refs[2 + n_dh:]
        i = pl.program_id(0)

        @pl.when(i == 0)
        def _():
            dg_ref[...] = jnp.zeros_like(dg_ref)

        xv = x_ref[...]
        r = lax.rsqrt(jnp.mean(xv * xv, axis=-1, keepdims=True) + EPS)
        xh = xv * r
        dh = dh_refs[0][...]
        for ref in dh_refs[1:]:
            dh = dh + ref[...]
        dg_ref[0:1, :] += jnp.sum(dh * xh, axis=0, keepdims=True)
        dxh = dh * g_ref[...]
        dx = r * (dxh - xh * jnp.mean(dxh * xh, axis=-1, keepdims=True))
        dx_ref[...] = dx + dres_ref[...]

    row = pl.BlockSpec((tm, d), lambda i: (i, 0))
    dx, dg = pl.pallas_call(
        body, name=name, grid=(t // tm,),
        in_specs=[row, pl.BlockSpec((1, d), lambda i: (0, 0))] + [row] * n_dh + [row],
        out_specs=[row, pl.BlockSpec((8, d), lambda i: (0, 0))],
        out_shape=[jax.ShapeDtypeStruct((t, d), F32), jax.ShapeDtypeStruct((8, d), F32)],
        compiler_params=_params(("arbitrary",)),
    )(x, g, *dhs, dres)
    return dx, dg[0:1]


def _loss_grad(y, target, name="loss_grad", tm=256):
    t, d = y.shape
    nb = t // tm

    def body(y_ref, t_ref, dy_ref, part_ref):
        e = y_ref[...] - t_ref[...]
        dy_ref[...] = e * (1.0 / d)
        s = jnp.sum(jnp.sum(e * e, axis=1, keepdims=True), axis=0, keepdims=True) * (0.5 / d)
        part_ref[...] = jnp.broadcast_to(s, (8, 128))

    row = pl.BlockSpec((tm, d), lambda i: (i, 0))
    dy, part = pl.pallas_call(
        body, name=name, grid=(nb,), in_specs=[row, row],
        out_specs=[row, pl.BlockSpec((None, 8, 128), lambda i: (i, 0, 0))],
        out_shape=[jax.ShapeDtypeStruct((t, d), F32), jax.ShapeDtypeStruct((nb, 8, 128), F32)],
        compiler_params=_params(("parallel",)),
    )(y, target)
    return dy, part[:, 0, 0]


def _softplus(x):
    t = jnp.exp(-jnp.abs(x))
    return jnp.maximum(x, 0.0) + jnp.where(t < 1e-3, t * (1.0 - 0.5 * t), jnp.log(1.0 + t))


def _tri(rows_le_cols):
    r = lax.broadcasted_iota(jnp.int32, (A_CHUNK, A_CHUNK), 0)
    c = lax.broadcasted_iota(jnp.int32, (A_CHUNK, A_CHUNK), 1)
    return jnp.where((r <= c) if rows_le_cols else (r >= c), 1.0, 0.0).astype(BF16)


def _dot_exact_rhs(a, ones_bf16):
    dn = (((1,), (0,)), ((), ()))
    hi, mid, lo = _split3(a)
    out = lax.dot_general(hi, ones_bf16, dn, preferred_element_type=F32)
    out = out + lax.dot_general(mid, ones_bf16, dn, preferred_element_type=F32)
    return out + lax.dot_general(lo, ones_bf16, dn, preferred_element_type=F32)


def _gdn_prep(tail_t, a_log, dt_bias):
    bn, _, n, c = tail_t.shape

    def body(t_ref, alog_ref, dtb_ref, beta_ref, gc_ref):
        upper = _tri(True)
        for h in range(A_HEADS):
            beta_ref[h] = _sigmoid(t_ref[h])
            ea = jnp.exp(jnp.full((n, c), alog_ref[h], F32))
            g = -ea * _softplus(t_ref[A_HEADS + h] + dtb_ref[h])
            gc_ref[h] = _dot_exact_rhs(g, upper)

    smem = pl.BlockSpec(memory_space=pltpu.SMEM)
    blk = pl.BlockSpec((None, A_HEADS, n, c), lambda b: (b, 0, 0, 0))
    return pl.pallas_call(
        body, name="gdn_prep", grid=(bn,),
        in_specs=[pl.BlockSpec((None, 2 * A_HEADS, n, c), lambda b: (b, 0, 0, 0)), smem, smem],
        out_specs=[blk, blk],
        out_shape=[jax.ShapeDtypeStruct((bn, A_HEADS, n, c), F32)] * 2,
        compiler_params=_params(("parallel",)),
    )(tail_t, a_log, dt_bias)


def _gdn_prep_bwd(tail_t, a_log, dt_bias, d_gc, d_beta):
    bn, _, n, c = tail_t.shape

    def body(t_ref, alog_ref, dtb_ref, dgc_ref, dbeta_ref, dt_ref, dal_ref, ddt_ref):
        lower = _tri(False)
        for h in range(A_HEADS):
            beta = _sigmoid(t_ref[h])
            dt_ref[h] = dbeta_ref[h] * beta * (1.0 - beta)
            dg = _dot_exact_rhs(dgc_ref[h], lower)
            ea = jnp.exp(jnp.full((n, c), alog_ref[h], F32))
            xa = t_ref[A_HEADS + h] + dtb_ref[h]
            g = -ea * _softplus(xa)
            dxa = -ea * dg * _sigmoid(xa)
            dt_ref[A_HEADS + h] = dxa
            s1 = jnp.sum(jnp.sum(g * dg, axis=1, keepdims=True), axis=0, keepdims=True)
            s2 = jnp.sum(jnp.sum(dxa, axis=1, keepdims=True), axis=0, keepdims=True)
            dal_ref[h:h + 1, :] = jnp.broadcast_to(s1, (1, 128))
            ddt_ref[h:h + 1, :] = jnp.broadcast_to(s2, (1, 128))

    smem = pl.BlockSpec(memory_space=pltpu.SMEM)
    blk8 = pl.BlockSpec((None, A_HEADS, n, c), lambda b: (b, 0, 0, 0))
    blk16 = pl.BlockSpec((None, 2 * A_HEADS, n, c), lambda b: (b, 0, 0, 0))
    sm = pl.BlockSpec((None, A_HEADS, 128), lambda b: (b, 0, 0))
    return pl.pallas_call(
        body, name="gdn_prep_bwd", grid=(bn,),
        in_specs=[blk16, smem, smem, blk8, blk8],
        out_specs=[blk16, sm, sm],
        out_shape=[jax.ShapeDtypeStruct((bn, 2 * A_HEADS, n, c), F32),
                   jax.ShapeDtypeStruct((bn, A_HEADS, 128), F32),
                   jax.ShapeDtypeStruct((bn, A_HEADS, 128), F32)],
        compiler_params=_params(("parallel",)),
    )(tail_t, a_log, dt_bias, d_gc, d_beta)


HALO = 8


def _conv_window(x_ref, n, first, lo, width):
    if first:
        return jnp.concatenate([jnp.zeros((HALO, width), F32), x_ref[0:A_CHUNK, lo:lo + width]], axis=0)
    start = pl.multiple_of(n * A_CHUNK - HALO, HALO)
    return x_ref[pl.ds(start, A_CHUNK + HALO), lo:lo + width]


def _conv_taps(xw, w):
    y = w[A_CONV - 1:A_CONV, :] * xw
    for j in range(1, A_CONV):
        y = y + w[A_CONV - 1 - j:A_CONV - j, :] * pltpu.roll(xw, j, 0)
    return y[HALO:, :]


def _row_to_col(row, eye):
    c = eye.shape[0]
    return jnp.sum(jnp.where(eye, jnp.broadcast_to(row, (c, c)), 0.0), axis=1, keepdims=True)


def _col_to_row(col, eye):
    c = eye.shape[0]
    return jnp.sum(jnp.where(eye, jnp.broadcast_to(col, (c, c)), 0.0), axis=0, keepdims=True)


def _unit_lower_inverse(a, ri, ci):
    eye = jnp.where(ri == ci, 1.0, 0.0)
    a8 = jnp.where((ri >> 3) == (ci >> 3), a, 0.0)
    a2 = _dot_hi(a8, a8, 1, 0)
    yield
    a4 = _dot_hi(a2, a2, 1, 0)
    t = eye - a8
    t = t + _dot_hi(t, a2, 1, 0)
    yield
    t = t + _dot_hi(t, a4, 1, 0)
    yield
    for sh in (3, 4, 5):
        off = jnp.where(((ri >> (sh + 1)) == (ci >> (sh + 1))) & ((ri >> sh) != (ci >> sh)), a, 0.0)
        left = _dot_hi(t, off, 1, 0)
        yield
        t = t - _dot_hi(left, t, 1, 0)
        yield
    return t


def _round_robin(gens):
    live = list(gens)
    while live:
        nxt = []
        for g in live:
            try:
                next(g)
                nxt.append(g)
            except StopIteration:
                pass
        live = nxt


def _gdn_chunk_inputs(x_ref, cw, n, first):
    xw = _conv_window(x_ref, n, first, 0, A_CONV_COLS)
    y = _conv_taps(xw, cw)
    a = _silu(y)
    aq, ak, v = a[:, 0:A_DK], a[:, A_DK:2 * A_DK], a[:, 2 * A_DK:]
    rq = lax.rsqrt(jnp.sum(aq * aq, axis=1, keepdims=True) + EPS)
    rk = lax.rsqrt(jnp.sum(ak * ak, axis=1, keepdims=True) + EPS)
    return dict(xw=xw, y=y, aq=aq, ak=ak, rq=rq, rk=rk,
                q=aq * rq * (A_DK ** -0.5), k=ak * rk, v=v)


def _gdn_chunk_core(q, k, v, g_row, b_row, t_mat, ri, ci):
    eye = ri == ci
    g_col = _row_to_col(g_row, eye)
    b_col = _row_to_col(b_row, eye)
    causal = ri >= ci
    strict = ri > ci
    dec = jnp.where(causal, jnp.exp(jnp.where(causal, g_col - g_row, 0.0)), 0.0)
    gam = jnp.exp(g_col)
    g_last = g_row[:, A_CHUNK - 1:A_CHUNK]
    gam_last = jnp.exp(g_last)
    e = jnp.exp(g_last - g_col)
    kb = k * b_col
    bv = v * b_col
    kbg = kb * gam
    kk = _dot(kb, k, 1, 1)
    p = _dot(q, k, 1, 1) * dec
    yield
    a_mat = jnp.where(strict, kk * dec, 0.0)
    if t_mat is None:
        t_mat = yield from _unit_lower_inverse(a_mat, ri, ci)
    u = _dot(t_mat, bv, 1, 0)
    w = _dot(t_mat, kbg, 1, 0)
    yield
    return dict(eye=eye, g_col=g_col, b_col=b_col, dec=dec, strict=strict, causal=causal, gam=gam,
                gam_last=gam_last, e=e, kb=kb, bv=bv, kbg=kbg, a_mat=a_mat, t_mat=t_mat, u=u, w=w, p=p,
                qg=q * gam, kd=k * e)


def _gdn_fwd(proj_hm, cw_hm, beta, gc, norm_g):
    bn, s, _ = proj_hm.shape
    n = s // A_CHUNK

    def body(x_ref, cw_ref, beta_ref, gc_ref, ng_ref, og_ref, oraw_ref, st_ref, t_ref, state):
        ri = lax.broadcasted_iota(jnp.int32, (A_CHUNK, A_CHUNK), 0)
        ci = lax.broadcasted_iota(jnp.int32, (A_CHUNK, A_CHUNK), 1)
        cw = cw_ref[...]
        ng = ng_ref[...]
        state[...] = jnp.zeros_like(state)

        def chunk(i, first):
            rows = pl.ds(0 if first else pl.multiple_of(i * A_CHUNK, A_CHUNK), A_CHUNK)
            cin = _gdn_chunk_inputs(x_ref, cw, i, first)
            core = _gdn_chunk_core(cin["q"], cin["k"], cin["v"], gc_ref[pl.ds(i, 1), :],
                                   beta_ref[pl.ds(i, 1), :], None, ri, ci)
            st = state[...]
            st_ref[i] = st
            t_ref[i] = core["t_mat"]
            vn = core["u"] - _dot(core["w"], st, 1, 0)
            o = _dot(core["qg"], st, 1, 0) + _dot(core["p"], vn, 1, 0)
            state[...] = st * core["gam_last"] + _dot(core["kd"], vn, 0, 0)
            oraw_ref[rows, :] = o
            r = lax.rsqrt(jnp.mean(o * o, axis=1, keepdims=True) + EPS)
            z = x_ref[rows, A_CONV_COLS:A_HEAD_COLS]
            og_ref[rows, :] = (o * r * ng * _silu(z)).astype(BF16)

        chunk(0, True)
        lax.fori_loop(1, n, lambda i, c: (chunk(i, False), c)[1], 0)

    return pl.pallas_call(
        body, name="gdn_fwd", grid=(bn, A_HEADS),
        in_specs=[pl.BlockSpec((None, s, A_HEAD_COLS), lambda b, h: (b, 0, h)),
                  pl.BlockSpec((A_CONV, A_CONV_COLS), lambda b, h: (0, h)),
                  pl.BlockSpec((None, None, n, A_CHUNK), lambda b, h: (b, h, 0, 0)),
                  pl.BlockSpec((None, None, n, A_CHUNK), lambda b, h: (b, h, 0, 0)),
                  pl.BlockSpec((1, A_DV), lambda b, h: (0, 0))],
        out_specs=[pl.BlockSpec((None, s, A_DV), lambda b, h: (b, 0, h)),
                   pl.BlockSpec((None, s, A_DV), lambda b, h: (b, 0, h)),
                   pl.BlockSpec((None, None, n, A_DK, A_DV), lambda b, h: (b, h, 0, 0, 0)),
                   pl.BlockSpec((None, None, n, A_CHUNK, A_CHUNK), lambda b, h: (b, h, 0, 0, 0))],
        out_shape=[jax.ShapeDtypeStruct((bn, s, A_VW), BF16),
                   jax.ShapeDtypeStruct((bn, s, A_VW), F32),
                   jax.ShapeDtypeStruct((bn, A_HEADS, n, A_DK, A_DV), F32),
                   jax.ShapeDtypeStruct((bn, A_HEADS, n, A_CHUNK, A_CHUNK), F32)],
        scratch_shapes=[pltpu.VMEM((A_DK, A_DV), F32)],
        compiler_params=_params(("parallel", "parallel"), VMEM_BIG),
    )(proj_hm, cw_hm, beta, gc, norm_g)


def _gdn_bwd(proj_hm, cw_hm, beta, gc, norm_g, oraw, states, t_mats, dog):
    bn, s, _ = proj_hm.shape
    n = s // A_CHUNK

    def body(x_ref, cw_ref, beta_ref, gc_ref, ng_ref, oraw_ref, st_ref, t_ref, dog_ref,
             dx_ref, dgc_ref, dbeta_ref, dcw_ref, dng_ref, dstate, dy_next):
        ri = lax.broadcasted_iota(jnp.int32, (A_CHUNK, A_CHUNK), 0)
        ci = lax.broadcasted_iota(jnp.int32, (A_CHUNK, A_CHUNK), 1)
        cw = cw_ref[...]
        ng = ng_ref[...]
        dstate[...] = jnp.zeros_like(dstate)
        dy_next[...] = jnp.zeros_like(dy_next)
        dcw_ref[...] = jnp.zeros_like(dcw_ref)
        dng_ref[...] = jnp.zeros_like(dng_ref)

        def chunk(i, first):
            rows = pl.ds(0 if first else pl.multiple_of(i * A_CHUNK, A_CHUNK), A_CHUNK)
            cin = _gdn_chunk_inputs(x_ref, cw, i, first)
            q, k, v = cin["q"], cin["k"], cin["v"]
            g_row = gc_ref[pl.ds(i, 1), :]
            b_row = beta_ref[pl.ds(i, 1), :]
            cr = _gdn_chunk_core(q, k, v, g_row, b_row, t_ref[i], ri, ci)
            eye, dec, gam, e = cr["eye"], cr["dec"], cr["gam"], cr["e"]
            b_col, t_mat, u, w, p = cr["b_col"], cr["t_mat"], cr["u"], cr["w"], cr["p"]
            st = st_ref[i]
            ds_out = dstate[...]

            o = oraw_ref[rows, :]
            z = x_ref[rows, A_CONV_COLS:A_HEAD_COLS]
            d_og = dog_ref[rows, :]
            r = lax.rsqrt(jnp.mean(o * o, axis=1, keepdims=True) + EPS)
            oh = o * r
            d_on = d_og * _silu(z)
            dz = d_og * oh * ng * _dsilu(z)
            dng_ref[0:1, :] += jnp.sum(d_on * oh, axis=0, keepdims=True)
            d_oh = d_on * ng
            d_o = r * (d_oh - oh * jnp.mean(d_oh * oh, axis=1, keepdims=True))

            vn = u - _dot(w, st, 1, 0)
            d_vn = _dot(p, d_o, 0, 0) + _dot(cr["kd"], ds_out, 1, 0)
            d_p = jnp.where(cr["causal"], _dot(d_o, vn, 1, 1), 0.0)
            d_qg = _dot(d_o, st, 1, 1)
            d_kd = _dot(vn, ds_out, 1, 1)
            d_gam_last = jnp.sum(jnp.sum(st * ds_out, axis=1, keepdims=True), axis=0, keepdims=True)
            d_w = -_dot(d_vn, st, 1, 1)
            dstate[...] = _dot(cr["qg"], d_o, 0, 0) + ds_out * cr["gam_last"] - _dot(w, d_vn, 0, 0)
            d_bv = _dot(t_mat, d_vn, 0, 0)
            d_kbg = _dot(t_mat, d_w, 0, 0)
            d_a = jnp.where(cr["strict"], -(_dot(d_bv, u, 1, 1) + _dot(d_kbg, w, 1, 1)), 0.0)
            m_a = d_a * dec
            n_p = d_p * dec
            d_kb = _dot(m_a, k, 1, 0) + d_kbg * gam
            d_q = _dot(n_p, k, 1, 0) + d_qg * gam
            d_k = (_dot(m_a, cr["kb"], 0, 0) + _dot(n_p, q, 0, 0) + d_kd * e + d_kb * b_col)
            d_v = d_bv * b_col
            d_beta_col = (jnp.sum(d_bv * v, axis=1, keepdims=True)
                          + jnp.sum(d_kb * k, axis=1, keepdims=True))
            gterm = d_a * cr["a_mat"] + d_p * p
            d_e = jnp.sum(d_kd * k, axis=1, keepdims=True) * e
            d_g_col = (jnp.sum(gterm, axis=1, keepdims=True)
                       + (jnp.sum(d_qg * q, axis=1, keepdims=True)
                          + jnp.sum(d_kbg * cr["kb"], axis=1, keepdims=True)) * gam
                       - d_e)
            d_g_last = jnp.sum(d_e, axis=0, keepdims=True) + d_gam_last * cr["gam_last"]
            lane = lax.broadcasted_iota(jnp.int32, (1, A_CHUNK), 1)
            d_g_row = (_col_to_row(d_g_col, eye) - jnp.sum(gterm, axis=0, keepdims=True)
                       + jnp.where(lane == A_CHUNK - 1, d_g_last, 0.0))
            dgc_ref[pl.ds(i, 1), :] = d_g_row
            dbeta_ref[pl.ds(i, 1), :] = _col_to_row(d_beta_col, eye)

            qh = cin["aq"] * cin["rq"]
            kh = cin["ak"] * cin["rk"]
            d_qh = d_q * (A_DK ** -0.5)
            d_aq = cin["rq"] * (d_qh - qh * jnp.sum(d_qh * qh, axis=1, keepdims=True))
            d_ak = cin["rk"] * (d_k - kh * jnp.sum(d_k * kh, axis=1, keepdims=True))
            d_y = jnp.concatenate([d_aq, d_ak, d_v], axis=1) * _dsilu(cin["y"])
            xw = cin["xw"]
            dyw = jnp.concatenate([d_y, dy_next[...]], axis=0)
            d_x = cw[A_CONV - 1:A_CONV, :] * dyw
            for j in range(1, A_CONV):
                d_x = d_x + cw[A_CONV - 1 - j:A_CONV - j, :] * pltpu.roll(dyw, A_CHUNK + HALO - j, 0)
            d_x = d_x[0:A_CHUNK, :]
            dy_pad = jnp.concatenate([jnp.zeros((HALO, A_CONV_COLS), F32), d_y], axis=0)
            for j in range(A_CONV):
                xs = xw if j == 0 else pltpu.roll(xw, j, 0)
                dcw_ref[A_CONV - 1 - j:A_CONV - j, :] += jnp.sum(dy_pad * xs, axis=0, keepdims=True)
            dy_next[...] = d_y[0:HALO, :]
            dx_ref[rows, 0:A_CONV_COLS] = d_x.astype(BF16)
            dx_ref[rows, A_CONV_COLS:A_HEAD_COLS] = dz.astype(BF16)

        lax.fori_loop(0, n - 1, lambda i, c: (chunk(n - 1 - i, False), c)[1], 0)
        chunk(0, True)

    hn = lambda b, h: (b, h, 0, 0)
    return pl.pallas_call(
        body, name="gdn_bwd", grid=(bn, A_HEADS),
        in_specs=[pl.BlockSpec((None, s, A_HEAD_COLS), lambda b, h: (b, 0, h)),
                  pl.BlockSpec((A_CONV, A_CONV_COLS), lambda b, h: (0, h)),
                  pl.BlockSpec((None, None, n, A_CHUNK), hn),
                  pl.BlockSpec((None, None, n, A_CHUNK), hn),
                  pl.BlockSpec((1, A_DV), lambda b, h: (0, 0)),
                  pl.BlockSpec((None, s, A_DV), lambda b, h: (b, 0, h)),
                  pl.BlockSpec((None, None, n, A_DK, A_DV), lambda b, h: (b, h, 0, 0, 0)),
                  pl.BlockSpec((None, None, n, A_CHUNK, A_CHUNK), lambda b, h: (b, h, 0, 0, 0)),
                  pl.BlockSpec((None, s, A_DV), lambda b, h: (b, 0, h))],
        out_specs=[pl.BlockSpec((None, s, A_HEAD_COLS), lambda b, h: (b, 0, h)),
                   pl.BlockSpec((None, None, n, A_CHUNK), hn),
                   pl.BlockSpec((None, None, n, A_CHUNK), hn),
                   pl.BlockSpec((None, A_CONV, A_CONV_COLS), lambda b, h: (b, 0, h)),
                   pl.BlockSpec((None, None, 8, A_DV), hn)],
        out_shape=[jax.ShapeDtypeStruct((bn, s, A_HEADS * A_HEAD_COLS), BF16),
                   jax.ShapeDtypeStruct((bn, A_HEADS, n, A_CHUNK), F32),
                   jax.ShapeDtypeStruct((bn, A_HEADS, n, A_CHUNK), F32),
                   jax.ShapeDtypeStruct((bn, A_CONV, A_HEADS * A_CONV_COLS), F32),
                   jax.ShapeDtypeStruct((bn, A_HEADS, 8, A_DV), F32)],
        scratch_shapes=[pltpu.VMEM((A_DK, A_DV), F32), pltpu.VMEM((HALO, A_CONV_COLS), F32)],
        compiler_params=_params(("parallel", "parallel"), VMEM_BIG),
    )(proj_hm, cw_hm, beta, gc, norm_g, oraw, states, t_mats, dog)


A_SEQ_BLK = 512
A_BLK_CHUNKS = A_SEQ_BLK // A_CHUNK


def _gdn_halo(proj_hm):
    bn, s, w = proj_hm.shape
    last = proj_hm.reshape(bn, s // A_SEQ_BLK, A_SEQ_BLK, w)[:, :, A_SEQ_BLK - HALO:, :]
    return jnp.concatenate([jnp.zeros((bn, 1, HALO, w), proj_hm.dtype), last[:, :-1]], axis=1)


def _gdn_window(x_ref, halo_ref, ci, first, lo):
    if first:
        return jnp.concatenate([halo_ref[:, lo:lo + A_CONV_COLS], x_ref[0:A_CHUNK, lo:lo + A_CONV_COLS]], axis=0)
    start = pl.multiple_of(ci * A_CHUNK - HALO, HALO)
    return x_ref[pl.ds(start, A_CHUNK + HALO), lo:lo + A_CONV_COLS]


def _gdn_chunk_prep(xw, cw):
    y = _conv_taps(xw, cw)
    a = _silu(y)
    aq, ak, v = a[:, 0:A_DK], a[:, A_DK:2 * A_DK], a[:, 2 * A_DK:]
    rq = lax.rsqrt(jnp.sum(aq * aq, axis=1, keepdims=True) + EPS)
    rk = lax.rsqrt(jnp.sum(ak * ak, axis=1, keepdims=True) + EPS)
    return dict(xw=xw, y=y, aq=aq, ak=ak, rq=rq, rk=rk, q=aq * rq * (A_DK ** -0.5), k=ak * rk, v=v)


def _gdn_fwd(proj_hm, cw_hm, beta, gc, norm_g, hp=4):
    bn, s, _ = proj_hm.shape
    n = s // A_CHUNK
    nsb = s // A_SEQ_BLK
    halo = _gdn_halo(proj_hm)

    def body(x_ref, halo_ref, cw_ref, beta_ref, gc_ref, ng_ref, og_ref, oraw_ref, st_ref, t_ref, state):
        ri = lax.broadcasted_iota(jnp.int32, (A_CHUNK, A_CHUNK), 0)
        ci_ = lax.broadcasted_iota(jnp.int32, (A_CHUNK, A_CHUNK), 1)
        ng = ng_ref[...]

        @pl.when(pl.program_id(2) == 0)
        def _():
            state[...] = jnp.zeros_like(state)

        def one_head(hh, ci, first, rows):
            lo = hh * A_HEAD_COLS
            cw = cw_ref[:, hh * A_CONV_COLS:(hh + 1) * A_CONV_COLS]
            cin = _gdn_chunk_prep(_gdn_window(x_ref, halo_ref, ci, first, lo), cw)
            core = yield from _gdn_chunk_core(cin["q"], cin["k"], cin["v"], gc_ref[hh, pl.ds(ci, 1), :],
                                              beta_ref[hh, pl.ds(ci, 1), :], None, ri, ci_)
            st = state[hh]
            st_ref[hh, ci] = st
            t_ref[hh, ci] = core["t_mat"]
            vn = core["u"] - _dot(core["w"], st, 1, 0)
            qs = _dot(core["qg"], st, 1, 0)
            yield
            o = qs + _dot(core["p"], vn, 1, 0)
            state[hh] = st * core["gam_last"] + _dot(core["kd"], vn, 0, 0)
            yield
            ocols = slice(hh * A_DV, (hh + 1) * A_DV)
            oraw_ref[rows, ocols] = o
            r = lax.rsqrt(jnp.mean(o * o, axis=1, keepdims=True) + EPS)
            z = x_ref[rows, lo + A_CONV_COLS:lo + A_HEAD_COLS]
            og_ref[rows, ocols] = (o * r * ng * _silu(z)).astype(BF16)

        def chunk(ci, first):
            rows = pl.ds(0 if first else pl.multiple_of(ci * A_CHUNK, A_CHUNK), A_CHUNK)
            _round_robin([one_head(hh, ci, first, rows) for hh in range(hp)])

        chunk(0, True)
        lax.fori_loop(1, A_BLK_CHUNKS, lambda i, c: (chunk(i, False), c)[1], 0)

    small = pl.BlockSpec((None, hp, A_BLK_CHUNKS, A_CHUNK), lambda b, h, j: (b, h, j, 0))
    return pl.pallas_call(
        body, name="gdn_fwd", grid=(bn, A_HEADS // hp, nsb),
        in_specs=[pl.BlockSpec((None, A_SEQ_BLK, hp * A_HEAD_COLS), lambda b, h, j: (b, j, h)),
                  pl.BlockSpec((None, None, HALO, hp * A_HEAD_COLS), lambda b, h, j: (b, j, 0, h)),
                  pl.BlockSpec((A_CONV, hp * A_CONV_COLS), lambda b, h, j: (0, h)),
                  small, small,
                  pl.BlockSpec((1, A_DV), lambda b, h, j: (0, 0))],
        out_specs=[pl.BlockSpec((None, A_SEQ_BLK, hp * A_DV), lambda b, h, j: (b, j, h)),
                   pl.BlockSpec((None, A_SEQ_BLK, hp * A_DV), lambda b, h, j: (b, j, h)),
                   pl.BlockSpec((None, hp, A_BLK_CHUNKS, A_DK, A_DV), lambda b, h, j: (b, h, j, 0, 0)),
                   pl.BlockSpec((None, hp, A_BLK_CHUNKS, A_CHUNK, A_CHUNK), lambda b, h, j: (b, h, j, 0, 0))],
        out_shape=[jax.ShapeDtypeStruct((bn, s, A_VW), BF16),
                   jax.ShapeDtypeStruct((bn, s, A_VW), F32),
                   jax.ShapeDtypeStruct((bn, A_HEADS, n, A_DK, A_DV), F32),
                   jax.ShapeDtypeStruct((bn, A_HEADS, n, A_CHUNK, A_CHUNK), F32)],
        scratch_shapes=[pltpu.VMEM((hp, A_DK, A_DV), F32)],
        compiler_params=_params(("parallel", "parallel", "arbitrary"), VMEM_BIG),
    )(proj_hm, halo, cw_hm, beta, gc, norm_g)


def _gdn_bwd(proj_hm, cw_hm, beta, gc, norm_g, oraw, states, t_mats, dog, hp=4):
    bn, s, _ = proj_hm.shape
    n = s // A_CHUNK
    nsb = s // A_SEQ_BLK
    halo = _gdn_halo(proj_hm)

    def body(x_ref, halo_ref, cw_ref, beta_ref, gc_ref, ng_ref, oraw_ref, st_ref, t_ref, dog_ref,
             dx_ref, dgc_ref, dbeta_ref, dcw_ref, dng_ref, dstate, dy_next):
        ri = lax.broadcasted_iota(jnp.int32, (A_CHUNK, A_CHUNK), 0)
        ci_ = lax.broadcasted_iota(jnp.int32, (A_CHUNK, A_CHUNK), 1)
        lane = lax.broadcasted_iota(jnp.int32, (1, A_CHUNK), 1)
        ng = ng_ref[...]

        @pl.when(pl.program_id(2) == 0)
        def _():
            dstate[...] = jnp.zeros_like(dstate)
            dy_next[...] = jnp.zeros_like(dy_next)
            dcw_ref[...] = jnp.zeros_like(dcw_ref)
            dng_ref[...] = jnp.zeros_like(dng_ref)

        def one_head(hh, ci, first, rows):
            lo = hh * A_HEAD_COLS
            ccols = slice(hh * A_CONV_COLS, (hh + 1) * A_CONV_COLS)
            ocols = slice(hh * A_DV, (hh + 1) * A_DV)
            cw = cw_ref[:, ccols]
            cin = _gdn_chunk_prep(_gdn_window(x_ref, halo_ref, ci, first, lo), cw)
            q, k, v = cin["q"], cin["k"], cin["v"]
            cr = yield from _gdn_chunk_core(q, k, v, gc_ref[hh, pl.ds(ci, 1), :], beta_ref[hh, pl.ds(ci, 1), :],
                                            t_ref[hh, ci], ri, ci_)
            eye, dec, gam, e = cr["eye"], cr["dec"], cr["gam"], cr["e"]
            b_col, t_mat, u, w, p = cr["b_col"], cr["t_mat"], cr["u"], cr["w"], cr["p"]
            st = st_ref[hh, ci]
            ds_out = dstate[hh]

            o = oraw_ref[rows, ocols]
            z = x_ref[rows, lo + A_CONV_COLS:lo + A_HEAD_COLS]
            d_og = dog_ref[rows, ocols]
            r = lax.rsqrt(jnp.mean(o * o, axis=1, keepdims=True) + EPS)
            oh = o * r
            d_on = d_og * _silu(z)
            dz = d_og * oh * ng * _dsilu(z)
            dng_ref[hh, 0:1, :] += jnp.sum(d_on * oh, axis=0, keepdims=True)
            d_oh = d_on * ng
            d_o = r * (d_oh - oh * jnp.mean(d_oh * oh, axis=1, keepdims=True))

            vn = u - _dot(w, st, 1, 0)
            d_vn = _dot(p, d_o, 0, 0) + _dot(cr["kd"], ds_out, 1, 0)
            d_qg = _dot(d_o, st, 1, 1)
            qgdo = _dot(cr["qg"], d_o, 0, 0)
            yield
            d_p = jnp.where(cr["causal"], _dot(d_o, vn, 1, 1), 0.0)
            d_kd = _dot(vn, ds_out, 1, 1)
            d_gam_last = jnp.sum(jnp.sum(st * ds_out, axis=1, keepdims=True), axis=0, keepdims=True)
            d_w = -_dot(d_vn, st, 1, 1)
            dstate[hh] = qgdo + ds_out * cr["gam_last"] - _dot(w, d_vn, 0, 0)
            d_bv = _dot(t_mat, d_vn, 0, 0)
            yield
            d_kbg = _dot(t_mat, d_w, 0, 0)
            n_p = d_p * dec
            d_q = _dot(n_p, k, 1, 0) + d_qg * gam
            npq = _dot(n_p, q, 0, 0)
            yield
            d_a = jnp.where(cr["strict"], -(_dot(d_bv, u, 1, 1) + _dot(d_kbg, w, 1, 1)), 0.0)
            yield
            m_a = d_a * dec
            d_kb = _dot(m_a, k, 1, 0) + d_kbg * gam
            d_k = (_dot(m_a, cr["kb"], 0, 0) + npq + d_kd * e + d_kb * b_col)
            yield
            d_v = d_bv * b_col
            d_beta_col = (jnp.sum(d_bv * v, axis=1, keepdims=True)
                          + jnp.sum(d_kb * k, axis=1, keepdims=True))
            gterm = d_a * cr["a_mat"] + d_p * p
            d_e = jnp.sum(d_kd * k, axis=1, keepdims=True) * e
            d_g_col = (jnp.sum(gterm, axis=1, keepdims=True)
                       + (jnp.sum(d_qg * q, axis=1, keepdims=True)
                          + jnp.sum(d_kbg * cr["kb"], axis=1, keepdims=True)) * gam
                       - d_e)
            d_g_last = jnp.sum(d_e, axis=0, keepdims=True) + d_gam_last * cr["gam_last"]
            d_g_row = (_col_to_row(d_g_col, eye) - jnp.sum(gterm, axis=0, keepdims=True)
                       + jnp.where(lane == A_CHUNK - 1, d_g_last, 0.0))
            dgc_ref[hh, pl.ds(ci, 1), :] = d_g_row
            dbeta_ref[hh, pl.ds(ci, 1), :] = _col_to_row(d_beta_col, eye)

            qh = cin["aq"] * cin["rq"]
            kh = cin["ak"] * cin["rk"]
            d_qh = d_q * (A_DK ** -0.5)
            d_aq = cin["rq"] * (d_qh - qh * jnp.sum(d_qh * qh, axis=1, keepdims=True))
            d_ak = cin["rk"] * (d_k - kh * jnp.sum(d_k * kh, axis=1, keepdims=True))
            d_y = jnp.concatenate([d_aq, d_ak, d_v], axis=1) * _dsilu(cin["y"])
            xw = cin["xw"]
            dyw = jnp.concatenate([d_y, dy_next[hh]], axis=0)
            d_x = cw[A_CONV - 1:A_CONV, :] * dyw
            for j in range(1, A_CONV):
                d_x = d_x + cw[A_CONV - 1 - j:A_CONV - j, :] * pltpu.roll(dyw, A_CHUNK + HALO - j, 0)
            dy_pad = jnp.concatenate([jnp.zeros((HALO, A_CONV_COLS), F32), d_y], axis=0)
            for j in range(A_CONV):
                xs = xw if j == 0 else pltpu.roll(xw, j, 0)
                dcw_ref[A_CONV - 1 - j:A_CONV - j, ccols] += jnp.sum(dy_pad * xs, axis=0, keepdims=True)
            dy_next[hh] = d_y[0:HALO, :]
            dx_ref[rows, lo:lo + A_CONV_COLS] = d_x[0:A_CHUNK, :].astype(BF16)
            dx_ref[rows, lo + A_CONV_COLS:lo + A_HEAD_COLS] = dz.astype(BF16)

        def chunk(ci, first):
            rows = pl.ds(0 if first else pl.multiple_of(ci * A_CHUNK, A_CHUNK), A_CHUNK)
            _round_robin([one_head(hh, ci, first, rows) for hh in range(hp)])

        lax.fori_loop(0, A_BLK_CHUNKS - 1, lambda i, c: (chunk(A_BLK_CHUNKS - 1 - i, False), c)[1], 0)
        chunk(0, True)

    rev = lambda j: nsb - 1 - j
    small = pl.BlockSpec((None, hp, A_BLK_CHUNKS, A_CHUNK), lambda b, h, j: (b, h, rev(j), 0))
    wide = pl.BlockSpec((None, A_SEQ_BLK, hp * A_HEAD_COLS), lambda b, h, j: (b, rev(j), h))
    val = pl.BlockSpec((None, A_SEQ_BLK, hp * A_DV), lambda b, h, j: (b, rev(j), h))
    return pl.pallas_call(
        body, name="gdn_bwd", grid=(bn, A_HEADS // hp, nsb),
        in_specs=[wide,
                  pl.BlockSpec((None, None, HALO, hp * A_HEAD_COLS), lambda b, h, j: (b, rev(j), 0, h)),
                  pl.BlockSpec((A_CONV, hp * A_CONV_COLS), lambda b, h, j: (0, h)),
                  small, small,
                  pl.BlockSpec((1, A_DV), lambda b, h, j: (0, 0)),
                  val,
                  pl.BlockSpec((None, hp, A_BLK_CHUNKS, A_DK, A_DV), lambda b, h, j: (b, h, rev(j), 0, 0)),
                  pl.BlockSpec((None, hp, A_BLK_CHUNKS, A_CHUNK, A_CHUNK), lambda b, h, j: (b, h, rev(j), 0, 0)),
                  val],
        out_specs=[wide, small, small,
                   pl.BlockSpec((None, A_CONV, hp * A_CONV_COLS), lambda b, h, j: (b, 0, h)),
                   pl.BlockSpec((None, hp, 8, A_DV), lambda b, h, j: (b, h, 0, 0))],
        out_shape=[jax.ShapeDtypeStruct((bn, s, A_HEADS * A_HEAD_COLS), BF16),
                   jax.ShapeDtypeStruct((bn, A_HEADS, n, A_CHUNK), F32),
                   jax.ShapeDtypeStruct((bn, A_HEADS, n, A_CHUNK), F32),
                   jax.ShapeDtypeStruct((bn, A_CONV, A_HEADS * A_CONV_COLS), F32),
                   jax.ShapeDtypeStruct((bn, A_HEADS, 8, A_DV), F32)],
        scratch_shapes=[pltpu.VMEM((hp, A_DK, A_DV), F32), pltpu.VMEM((hp, HALO, A_CONV_COLS), F32)],
        compiler_params=_params(("parallel", "parallel", "arbitrary"), VMEM_BIG),
    )(proj_hm, halo, cw_hm, beta, gc, norm_g, oraw, states, t_mats, dog)


def _rope_tables(posf, inv_freq_row):
    t = posf.shape[0]
    tm = 512

    def body(p_ref, f_ref, c_ref, sa_ref, sb_ref):
        ang = p_ref[...] * f_ref[...]
        lane = lax.broadcasted_iota(jnp.int32, ang.shape, 1)
        half = ROPE_DIMS // 2
        c_ref[...] = jnp.where(lane < ROPE_DIMS, jnp.cos(ang), 1.0)
        sn = jnp.sin(ang)
        sa_ref[...] = jnp.where(lane < half, -sn, 0.0)
        sb_ref[...] = jnp.where((lane >= half) & (lane < ROPE_DIMS), sn, 0.0)

    row = pl.BlockSpec((tm, 128), lambda i: (i, 0))
    return pl.pallas_call(
        body, name="rope_tables", grid=(t // tm,),
        in_specs=[row, pl.BlockSpec((1, 128), lambda i: (0, 0))], out_specs=[row] * 3,
        out_shape=[jax.ShapeDtypeStruct((t, 128), F32)] * 3,
        compiler_params=_params(("parallel",)),
    )(posf, inv_freq_row)


def _rope(x, c, sa, sb):
    half = ROPE_DIMS // 2
    return x * c + pltpu.roll(x, 128 - half, 1) * sa + pltpu.roll(x, half, 1) * sb


def _rope_t(d, c, sa, sb):
    half = ROPE_DIMS // 2
    return d * c + pltpu.roll(d * sa, half, 1) + pltpu.roll(d * sb, 128 - half, 1)


def _qk_prep(proj, c, sa, sb, qg, kg, name, tm=256):
    t = proj.shape[0]
    wide = proj.shape[1]

    def body(x_ref, c_ref, sa_ref, sb_ref, qg_ref, kg_ref, o_ref):
        cc, s1, s2 = c_ref[...], sa_ref[...], sb_ref[...]
        for which, g_ref in ((0, qg_ref), (1, kg_ref)):
            g = g_ref[...]
            for h in range(B_HEADS):
                lo = which * B_W + h * B_DH
                xv = x_ref[:, lo:lo + B_DH]
                r = lax.rsqrt(jnp.mean(xv * xv, axis=1, keepdims=True) + EPS)
                o_ref[:, lo:lo + B_DH] = _rope(xv * r * g, cc, s1, s2).astype(BF16)
        o_ref[:, 2 * B_W:3 * B_W] = x_ref[:, 2 * B_W:3 * B_W].astype(BF16)

    tab = pl.BlockSpec((tm, 128), lambda i: (i, 0))
    gain = pl.BlockSpec((1, B_DH), lambda i: (0, 0))
    return pl.pallas_call(
        body, name=name, grid=(t // tm,),
        in_specs=[pl.BlockSpec((tm, wide), lambda i: (i, 0)), tab, tab, tab, gain, gain],
        out_specs=pl.BlockSpec((tm, 3 * B_W), lambda i: (i, 0)),
        out_shape=jax.ShapeDtypeStruct((t, 3 * B_W), BF16),
        compiler_params=_params(("parallel",), 40 * 1024 * 1024),
    )(proj, c, sa, sb, qg, kg)


def _qk_prep_bwd(proj, c, sa, sb, qg, kg, dq, dk, dv, dz, name, tm=256):
    t = proj.shape[0]
    wide = proj.shape[1]
    out_w = 3 * B_W + (B_W if dz is not None else 0)

    def body(*refs):
        x_ref, c_ref, sa_ref, sb_ref, qg_ref, kg_ref, dq_ref, dk_ref, dv_ref = refs[:9]
        if dz is not None:
            dz_ref, o_ref, dgain_ref = refs[9:]
        else:
            o_ref, dgain_ref = refs[9:]
        i = pl.program_id(0)

        @pl.when(i == 0)
        def _():
            dgain_ref[...] = jnp.zeros_like(dgain_ref)

        cc, s1, s2 = c_ref[...], sa_ref[...], sb_ref[...]
        for which, g_ref, d_ref in ((0, qg_ref, dq_ref), (1, kg_ref, dk_ref)):
            g = g_ref[...]
            acc = jnp.zeros((1, B_DH), F32)
            for h in range(B_HEADS):
                lo = which * B_W + h * B_DH
                xv = x_ref[:, lo:lo + B_DH]
                r = lax.rsqrt(jnp.mean(xv * xv, axis=1, keepdims=True) + EPS)
                xh = xv * r
                d_xn = _rope_t(d_ref[:, h * B_DH:(h + 1) * B_DH], cc, s1, s2)
                acc = acc + jnp.sum(d_xn * xh, axis=0, keepdims=True)
                d_xh = d_xn * g
                d_x = r * (d_xh - xh * jnp.mean(d_xh * xh, axis=1, keepdims=True))
                o_ref[:, lo:lo + B_DH] = d_x.astype(BF16)
            dgain_ref[which:which + 1, :] += acc
        o_ref[:, 2 * B_W:3 * B_W] = dv_ref[...].astype(BF16)
        if dz is not None:
            o_ref[:, 3 * B_W:4 * B_W] = dz_ref[...]

    tab = pl.BlockSpec((tm, 128), lambda i: (i, 0))
    gain = pl.BlockSpec((1, B_DH), lambda i: (0, 0))
    grad = pl.BlockSpec((tm, B_W), lambda i: (i, 0))
    in_specs = [pl.BlockSpec((tm, wide), lambda i: (i, 0)), tab, tab, tab, gain, gain, grad, grad, grad]
    args = [proj, c, sa, sb, qg, kg, dq, dk, dv]
    if dz is not None:
        in_specs.append(grad)
        args.append(dz)
    return pl.pallas_call(
        body, name=name, grid=(t // tm,), in_specs=in_specs,
        out_specs=[pl.BlockSpec((tm, out_w), lambda i: (i, 0)), pl.BlockSpec((8, B_DH), lambda i: (0, 0))],
        out_shape=[jax.ShapeDtypeStruct((t, out_w), BF16), jax.ShapeDtypeStruct((8, B_DH), F32)],
        compiler_params=_params(("arbitrary",), 40 * 1024 * 1024),
    )(*args)


def _attn_masks():
    qi = lax.broadcasted_iota(jnp.int32, (B_BLK, 2 * B_BLK), 0)
    kj = lax.broadcasted_iota(jnp.int32, (B_BLK, 2 * B_BLK), 1)
    two = (kj >= qi) & (kj <= qi + B_BLK)
    q1 = lax.broadcasted_iota(jnp.int32, (B_BLK, B_BLK), 0)
    k1 = lax.broadcasted_iota(jnp.int32, (B_BLK, B_BLK), 1)
    return k1 <= q1, two


def _lane_pick(ref_rows, h):
    lane = lax.broadcasted_iota(jnp.int32, ref_rows.shape, 1)
    return jnp.sum(jnp.where(lane == h, ref_rows, 0.0), axis=1, keepdims=True)


def _attn_fwd(qkv, name):
    ns, ln, _ = qkv.shape
    nb = ln // B_BLK
    scale = B_DH ** -0.5

    def body(q_ref, k_ref, v_ref, o_ref, lse_ref):
        h = pl.program_id(1)
        mask1, mask2 = _attn_masks()

        @pl.when(h == 0)
        def _():
            lse_ref[...] = jnp.zeros_like(lse_ref)

        def block(i, first):
            rows = pl.ds(pl.multiple_of(i * B_BLK, B_BLK), B_BLK)
            if first:
                win, mask = pl.ds(0, B_BLK), mask1
            else:
                win, mask = pl.ds(pl.multiple_of((i - 1) * B_BLK, B_BLK), 2 * B_BLK), mask2
            sc = jnp.where(mask, _dot(q_ref[rows, :], k_ref[win, :], 1, 1) * scale, -1e30)
            m = jnp.max(sc, axis=1, keepdims=True)
            p = jnp.exp(sc - m)
            l = jnp.sum(p, axis=1, keepdims=True)
            o_ref[rows, :] = _dot(p, v_ref[win, :], 1, 0) / l
            lane = lax.broadcasted_iota(jnp.int32, (B_BLK, B_HEADS), 1)
            lse_ref[rows, :] = jnp.where(lane == h, m + jnp.log(l), lse_ref[rows, :])

        block(0, True)
        if nb > 1:
            lax.fori_loop(1, nb, lambda i, c: (block(i, False), c)[1], 0)

    return pl.pallas_call(
        body, name=name, grid=(ns, B_HEADS),
        in_specs=[pl.BlockSpec((None, ln, B_DH), lambda s, h: (s, 0, h)),
                  pl.BlockSpec((None, ln, B_DH), lambda s, h: (s, 0, B_HEADS + h)),
                  pl.BlockSpec((None, ln, B_DH), lambda s, h: (s, 0, 2 * B_HEADS + h))],
        out_specs=[pl.BlockSpec((None, ln, B_DH), lambda s, h: (s, 0, h)),
                   pl.BlockSpec((None, ln, B_HEADS), lambda s, h: (s, 0, 0))],
        out_shape=[jax.ShapeDtypeStruct((ns, ln, B_W), F32), jax.ShapeDtypeStruct((ns, ln, B_HEADS), F32)],
        compiler_params=_params(("parallel", "arbitrary")),
    )(qkv, qkv, qkv)


def _attn_bwd(qkv, d_o, lse_joint, delta, name):
    ns, ln, _ = qkv.shape
    nb = ln // B_BLK
    scale = B_DH ** -0.5

    def body(q_ref, k_ref, v_ref, do_ref, lj_ref, dl_ref, dq_ref, dk_ref, dv_ref):
        h = pl.program_id(1)
        mask1, mask2 = _attn_masks()
        dk_ref[...] = jnp.zeros_like(dk_ref)
        dv_ref[...] = jnp.zeros_like(dv_ref)

        def block(i, first):
            rows = pl.ds(pl.multiple_of(i * B_BLK, B_BLK), B_BLK)
            if first:
                win, mask = pl.ds(0, B_BLK), mask1
            else:
                win, mask = pl.ds(pl.multiple_of((i - 1) * B_BLK, B_BLK), 2 * B_BLK), mask2
            q = q_ref[rows, :]
            d_out = do_ref[rows, :]
            l_col = _lane_pick(lj_ref[rows, :], h)
            d_col = _lane_pick(dl_ref[rows, :], h)
            sc = _dot(q, k_ref[win, :], 1, 1) * scale
            p = jnp.exp(jnp.where(mask, sc - l_col, -1e30))
            d_p = _dot(d_out, v_ref[win, :], 1, 1)
            d_s = p * (d_p - d_col) * scale
            dq_ref[rows, :] = _dot(d_s, k_ref[win, :], 1, 0)
            dk_ref[win, :] += _dot(d_s, q, 0, 0)
            dv_ref[win, :] += _dot(p, d_out, 0, 0)

        block(0, True)
        if nb > 1:
            lax.fori_loop(1, nb, lambda i, c: (block(i, False), c)[1], 0)

    head = lambda off: pl.BlockSpec((None, ln, B_DH), lambda s, h: (s, 0, off + h))
    small = pl.BlockSpec((None, ln, B_HEADS), lambda s, h: (s, 0, 0))
    return pl.pallas_call(
        body, name=name, grid=(ns, B_HEADS),
        in_specs=[head(0), head(B_HEADS), head(2 * B_HEADS), head(0), small, small],
        out_specs=[head(0)] * 3,
        out_shape=[jax.ShapeDtypeStruct((ns, ln, B_W), F32)] * 3,
        compiler_params=_params(("parallel", "parallel")),
    )(qkv, qkv, qkv, d_o, lse_joint, delta)


B_ROWS = 2048


def _attn_schedule(nb, sb, block):
    way = 4

    def run(items):
        for at in range(0, len(items), way):
            _round_robin([block(*it) for it in items[at:at + way]])

    run([(si, 0, True) for si in range(sb)])
    if nb == 1:
        return
    per = max(1, way // sb)
    lead = 1 + (nb - 1) % per
    if lead > 1:
        run([(si, i, False) for i in range(1, lead) for si in range(sb)])

    def step(it, carry):
        run([(si, lead + it * per + u, False) for u in range(per) for si in range(sb)])
        return carry

    lax.fori_loop(0, (nb - lead) // per, step, 0)


def _attn_rows(i, first):
    if first:
        return pl.ds(0, B_BLK), pl.ds(0, B_BLK)
    rows = pl.ds(pl.multiple_of(i * B_BLK, B_BLK), B_BLK)
    return rows, pl.ds(pl.multiple_of((i - 1) * B_BLK, B_BLK), 2 * B_BLK)


def _attn_fwd(qkv, name):
    ns, ln, _ = qkv.shape
    nb = ln // B_BLK
    sb = B_ROWS // ln
    scale = B_DH ** -0.5

    def body(q_ref, k_ref, v_ref, o_ref, lse_ref):
        h = pl.program_id(1)
        mask1, mask2 = _attn_masks()
        lane = lax.broadcasted_iota(jnp.int32, (B_BLK, B_HEADS), 1)

        @pl.when(h == 0)
        def _():
            lse_ref[...] = jnp.zeros_like(lse_ref)

        def block(si, i, first):
            rows, win = _attn_rows(i, first)
            mask = mask1 if first else mask2
            sc = jnp.where(mask, _dot(q_ref[si, rows, :], k_ref[si, win, :], 1, 1) * scale, -1e30)
            yield
            m = jnp.max(sc, axis=1, keepdims=True)
            p = jnp.exp(sc - m)
            l = jnp.sum(p, axis=1, keepdims=True)
            pv = _dot(p, v_ref[si, win, :], 1, 0)
            yield
            o_ref[si, rows, :] = pv / l
            lse_ref[si, rows, :] = jnp.where(lane == h, m + jnp.log(l), lse_ref[si, rows, :])

        _attn_schedule(nb, sb, block)

    head = lambda off: pl.BlockSpec((sb, ln, B_DH), lambda s, h: (s, 0, off + h))
    return pl.pallas_call(
        body, name=name, grid=(ns // sb, B_HEADS),
        in_specs=[head(0), head(B_HEADS), head(2 * B_HEADS)],
        out_specs=[head(0), pl.BlockSpec((sb, ln, B_HEADS), lambda s, h: (s, 0, 0))],
        out_shape=[jax.ShapeDtypeStruct((ns, ln, B_W), F32), jax.ShapeDtypeStruct((ns, ln, B_HEADS), F32)],
        compiler_params=_params(("parallel", "arbitrary")),
    )(qkv, qkv, qkv)


def _attn_bwd(qkv, d_o, lse_joint, delta, name):
    ns, ln, _ = qkv.shape
    nb = ln // B_BLK
    sb = B_ROWS // ln
    scale = B_DH ** -0.5

    def body(q_ref, k_ref, v_ref, do_ref, lj_ref, dl_ref, dq_ref, dk_ref, dv_ref):
        h = pl.program_id(1)
        mask1, mask2 = _attn_masks()
        dk_ref[...] = jnp.zeros_like(dk_ref)
        dv_ref[...] = jnp.zeros_like(dv_ref)

        def block(si, i, first):
            rows, win = _attn_rows(i, first)
            mask = mask1 if first else mask2
            q = q_ref[si, rows, :]
            d_out = do_ref[si, rows, :]
            l_col = _lane_pick(lj_ref[si, rows, :], h)
            d_col = _lane_pick(dl_ref[si, rows, :], h)
            sc = _dot(q, k_ref[si, win, :], 1, 1) * scale
            d_p = _dot(d_out, v_ref[si, win, :], 1, 1)
            yield
            p = jnp.exp(jnp.where(mask, sc - l_col, -1e30))
            d_s = p * (d_p - d_col) * scale
            d_q = _dot(d_s, k_ref[si, win, :], 1, 0)
            d_k = _dot(d_s, q, 0, 0)
            d_v = _dot(p, d_out, 0, 0)
            yield
            dq_ref[si, rows, :] = d_q
            dk_ref[si, win, :] += d_k
            dv_ref[si, win, :] += d_v

        _attn_schedule(nb, sb, block)

    head = lambda off: pl.BlockSpec((sb, ln, B_DH), lambda s, h: (s, 0, off + h))
    small = pl.BlockSpec((sb, ln, B_HEADS), lambda s, h: (s, 0, 0))
    return pl.pallas_call(
        body, name=name, grid=(ns // sb, B_HEADS),
        in_specs=[head(0), head(B_HEADS), head(2 * B_HEADS), head(0), small, small],
        out_specs=[head(0)] * 3,
        out_shape=[jax.ShapeDtypeStruct((ns, ln, B_W), F32)] * 3,
        compiler_params=_params(("parallel", "parallel")),
    )(qkv, qkv, qkv, d_o, lse_joint, delta)


def _merge_weights(lse_refs):
    ls = [r[...] for r in lse_refs]
    m = jnp.maximum(jnp.maximum(ls[0], ls[1]), ls[2])
    es = [jnp.exp(l - m) for l in ls]
    tot = es[0] + es[1] + es[2]
    return [e / tot for e in es], m + jnp.log(tot)


def _merge_fwd(outs, lses, proj0, tm=256):
    t = outs[0].shape[0]

    def body(o0, o1, o2, l0, l1, l2, z_ref, og_ref):
        wts, _ = _merge_weights((l0, l1, l2))
        for h in range(B_HEADS):
            cols = slice(h * B_DH, (h + 1) * B_DH)
            o = (wts[0][:, h:h + 1] * o0[:, cols] + wts[1][:, h:h + 1] * o1[:, cols]
                 + wts[2][:, h:h + 1] * o2[:, cols])
            og_ref[:, cols] = (o * _silu(z_ref[:, cols])).astype(BF16)

    wide = pl.BlockSpec((tm, B_W), lambda i: (i, 0))
    small = pl.BlockSpec((tm, B_HEADS), lambda i: (i, 0))
    return pl.pallas_call(
        body, name="merge_fwd", grid=(t // tm,),
        in_specs=[wide] * 3 + [small] * 3 + [pl.BlockSpec((tm, B_W), lambda i: (i, 3))],
        out_specs=wide, out_shape=jax.ShapeDtypeStruct((t, B_W), BF16),
        compiler_params=_params(("parallel",)),
    )(*outs, *lses, proj0)


def _merge_bwd(outs, lses, proj0, d_og, tm=256):
    t = outs[0].shape[0]

    def body(o0, o1, o2, l0, l1, l2, z_ref, dog_ref, do_ref, lj_ref, dl_ref, dz_ref):
        wts, lj = _merge_weights((l0, l1, l2))
        lj_ref[...] = lj
        lane = lax.broadcasted_iota(jnp.int32, (tm, B_HEADS), 1)
        delta = jnp.zeros((tm, B_HEADS), F32)
        for h in range(B_HEADS):
            cols = slice(h * B_DH, (h + 1) * B_DH)
            o = (wts[0][:, h:h + 1] * o0[:, cols] + wts[1][:, h:h + 1] * o1[:, cols]
                 + wts[2][:, h:h + 1] * o2[:, cols])
            z = z_ref[:, cols]
            d_g = dog_ref[:, cols]
            d_out = d_g * _silu(z)
            dz_ref[:, cols] = (d_g * o * _dsilu(z)).astype(BF16)
            do_ref[:, cols] = d_out.astype(BF16)
            delta = jnp.where(lane == h, jnp.sum(d_out * o, axis=1, keepdims=True), delta)
        dl_ref[...] = delta

    wide = pl.BlockSpec((tm, B_W), lambda i: (i, 0))
    small = pl.BlockSpec((tm, B_HEADS), lambda i: (i, 0))
    return pl.pallas_call(
        body, name="merge_bwd", grid=(t // tm,),
        in_specs=[wide] * 3 + [small] * 3 + [pl.BlockSpec((tm, B_W), lambda i: (i, 3)), wide],
        out_specs=[wide, small, small, wide],
        out_shape=[jax.ShapeDtypeStruct((t, B_W), BF16), jax.ShapeDtypeStruct((t, B_HEADS), F32),
                   jax.ShapeDtypeStruct((t, B_HEADS), F32), jax.ShapeDtypeStruct((t, B_W), BF16)],
        compiler_params=_params(("parallel",)),
    )(*outs, *lses, proj0, d_og)


def _adamw(w, g, m, v, name):
    r, c = w.shape
    tr = r
    for cand in (256, 128, 64, 32, 16, 8):
        if r % cand == 0:
            tr = cand
            break

    def body(w_ref, g_ref, m_ref, v_ref, d_ref, nm_ref, nv_ref):
        gv = g_ref[...]
        nm = ADAM_B1 * m_ref[...] + (1.0 - ADAM_B1) * gv
        nv = ADAM_B2 * v_ref[...] + (1.0 - ADAM_B2) * (gv * gv)
        m_hat = nm / (1.0 - ADAM_B1 ** ADAM_STEP)
        v_hat = nv / (1.0 - ADAM_B2 ** ADAM_STEP)
        d_ref[...] = -ADAM_LR * (m_hat / (jnp.sqrt(v_hat) + ADAM_EPS) + ADAM_WD * w_ref[...])
        nm_ref[...] = nm
        nv_ref[...] = nv

    blk = pl.BlockSpec((tr, c), lambda i: (i, 0))
    return pl.pallas_call(
        body, name=name, grid=(r // tr,), in_specs=[blk] * 4, out_specs=[blk] * 3,
        out_shape=[jax.ShapeDtypeStruct((r, c), F32)] * 3,
        compiler_params=_params(("parallel",)),
    )(w, g, m, v)


def _adam_update(w, gv, m, v):
    nm = ADAM_B1 * m + (1.0 - ADAM_B1) * gv
    nv = ADAM_B2 * v + (1.0 - ADAM_B2) * (gv * gv)
    m_hat = nm / (1.0 - ADAM_B1 ** ADAM_STEP)
    v_hat = nv / (1.0 - ADAM_B2 ** ADAM_STEP)
    return -ADAM_LR * (m_hat / (jnp.sqrt(v_hat) + ADAM_EPS) + ADAM_WD * w), nm, nv


def _adamw_shard(w, mine, theirs, m, v, half_index, name, tr=128):
    _, r, c = w.shape
    nhb = (r // 2) // tr

    def body(c_ref, w_ref, mine_ref, theirs_ref, m_ref, v_ref, g_ref, d_ref, nm_ref, nv_ref):
        is_mine = (pl.program_id(0) // nhb) == c_ref[0]
        gv = jnp.where(is_mine, mine_ref[...], theirs_ref[...])
        d, nm, nv = _adam_update(w_ref[...], gv, m_ref[...], v_ref[...])
        g_ref[...] = gv
        d_ref[...] = d
        nm_ref[...] = nm
        nv_ref[...] = nv

    full = pl.BlockSpec((None, tr, c), lambda i, cc: (0, i, 0))
    half = pl.BlockSpec((tr, c), lambda i, cc: (i % nhb, 0))
    return pl.pallas_call(
        body, name=name,
        grid_spec=pltpu.PrefetchScalarGridSpec(
            num_scalar_prefetch=1, grid=(2 * nhb,),
            in_specs=[full, half, half, full, full], out_specs=[full] * 4),
        out_shape=[jax.ShapeDtypeStruct(w.shape, F32)] * 4,
        compiler_params=_params(("parallel",), 40 * 1024 * 1024),
    )(half_index, w, mine, theirs, m, v)


def _pair_sum(own, other, half_index, name, tr=256):
    _, r, c = own.shape
    rh = r // 2
    tr = min(tr, rh)
    nrb = rh // tr

    def body(c_ref, own_ref, oth_ref, out_ref):
        out_ref[...] = (own_ref[...] + oth_ref[...].astype(F32)).astype(BF16)

    return pl.pallas_call(
        body, name=name,
        grid_spec=pltpu.PrefetchScalarGridSpec(
            num_scalar_prefetch=1, grid=(N_CHIPS, nrb),
            in_specs=[pl.BlockSpec((None, tr, c), lambda k, i, cc: (k, cc[0] * nrb + i, 0)),
                      pl.BlockSpec((None, tr, c), lambda k, i, cc: (k, i, 0))],
            out_specs=pl.BlockSpec((None, tr, c), lambda k, i, cc: (k, i, 0))),
        out_shape=jax.ShapeDtypeStruct((N_CHIPS, rh, c), BF16),
        compiler_params=_params(("parallel", "parallel")),
    )(half_index, own, other)


def _chip_sum(sums, others, chip_index, name, tr=256):
    _, r, c = sums.shape
    tr = min(tr, r)

    def body(k_ref, own_ref, oth_ref, out_ref):
        acc = own_ref[...].astype(F32)
        for j in range(N_CHIPS - 1):
            acc = acc + oth_ref[j].astype(F32)
        out_ref[...] = acc

    return pl.pallas_call(
        body, name=name,
        grid_spec=pltpu.PrefetchScalarGridSpec(
            num_scalar_prefetch=1, grid=(r // tr,),
            in_specs=[pl.BlockSpec((None, tr, c), lambda i, kk: (kk[0], i, 0)),
                      pl.BlockSpec((N_CHIPS - 1, tr, c), lambda i, kk: (0, i, 0))],
            out_specs=pl.BlockSpec((tr, c), lambda i, kk: (i, 0))),
        out_shape=jax.ShapeDtypeStruct((r, c), F32),
        compiler_params=_params(("parallel",)),
    )(chip_index, sums, others)


HBM = pl.BlockSpec(memory_space=pltpu.HBM)


def _place():
    x, y, c = lax.axis_index("x"), lax.axis_index("y"), lax.axis_index("c")
    chips = [(1 - x, y), (x, 1 - y), (1 - x, 1 - y)]
    return x, y, c, chips


def _weight_allgather(shards, conv_shard):
    na = len(shards)

    def body(*refs):
        ins = refs[:na]
        conv_in = refs[na]
        outs = refs[na + 1:2 * na + 1]
        conv_out = refs[2 * na + 1]
        send, recv, fsend, frecv, csend, crecv = refs[2 * na + 2:]
        x, y, c, chips = _place()
        me = 2 * x + y
        sib = (x, y, 1 - c)
        first, conv_cp = [], []
        for i in range(na):
            rh = ins[i].shape[0] // 2
            mine = pl.ds(c * rh, rh)
            for j, (px, py) in enumerate(chips):
                cp = pltpu.make_async_remote_copy(
                    src_ref=ins[i].at[mine], dst_ref=outs[i].at[me, mine],
                    send_sem=send.at[3 * i + j], recv_sem=recv.at[3 * i + j],
                    device_id=(px, py, c), device_id_type=MESH)
                cp.start()
                first.append(cp)
        for j, (px, py) in enumerate(chips):
            cp = pltpu.make_async_remote_copy(
                src_ref=conv_in, dst_ref=conv_out.at[me], send_sem=csend.at[j], recv_sem=crecv.at[j],
                device_id=(px, py, c), device_id_type=MESH)
            cp.start()
            conv_cp.append(cp)
        passed = []
        for i in range(na):
            rh = ins[i].shape[0] // 2
            mine = pl.ds(c * rh, rh)
            for j, (px, py) in enumerate(chips):
                slot = outs[i].at[2 * px + py, mine]
                pltpu.make_async_remote_copy(
                    src_ref=slot, dst_ref=slot, send_sem=send.at[3 * i + j], recv_sem=recv.at[3 * i + j],
                    device_id=(px, py, c), device_id_type=MESH).wait_recv()
                cp = pltpu.make_async_remote_copy(
                    src_ref=slot, dst_ref=slot, send_sem=fsend.at[3 * i + j], recv_sem=frecv.at[3 * i + j],
                    device_id=sib, device_id_type=MESH)
                cp.start()
                passed.append(cp)
        for i in range(na):
            rh = ins[i].shape[0] // 2
            theirs = pl.ds((1 - c) * rh, rh)
            for j, (px, py) in enumerate(chips):
                slot = outs[i].at[2 * px + py, theirs]
                pltpu.make_async_remote_copy(
                    src_ref=slot, dst_ref=slot, send_sem=fsend.at[3 * i + j], recv_sem=frecv.at[3 * i + j],
                    device_id=sib, device_id_type=MESH).wait_recv()
        for j, (px, py) in enumerate(chips):
            slot = conv_out.at[2 * px + py]
            pltpu.make_async_remote_copy(
                src_ref=slot, dst_ref=slot, send_sem=csend.at[j], recv_sem=crecv.at[j],
                device_id=(px, py, c), device_id_type=MESH).wait_recv()
        for cp in first + passed + conv_cp:
            cp.wait_send()

    out_shape = [jax.ShapeDtypeStruct((N_CHIPS,) + s.shape, s.dtype) for s in shards]
    out_shape.append(jax.ShapeDtypeStruct((N_CHIPS,) + conv_shard.shape, conv_shard.dtype))
    res = pl.pallas_call(
        body, name="weight_allgather",
        in_specs=[HBM] * (na + 1), out_specs=[HBM] * (na + 1), out_shape=out_shape,
        scratch_shapes=[pltpu.SemaphoreType.DMA((3 * na,)), pltpu.SemaphoreType.DMA((3 * na,)),
                        pltpu.SemaphoreType.DMA((3 * na,)), pltpu.SemaphoreType.DMA((3 * na,)),
                        pltpu.SemaphoreType.DMA((3,)), pltpu.SemaphoreType.DMA((3,))],
    )(*shards, conv_shard)
    my_chip = 2 * lax.axis_index("x") + lax.axis_index("y")
    pick = lambda got, own: [jnp.where(my_chip == k, own, got[k]) for k in range(N_CHIPS)]
    return [pick(g, s) for g, s in zip(res[:na], shards)], pick(res[na], conv_shard)


def _sibling_swap_halves(grads, name):
    na = len(grads)

    def body(*refs):
        ins, outs = refs[:na], refs[na:2 * na]
        send, recv = refs[2 * na:]
        x, y, c, _ = _place()
        sib = (x, y, 1 - c)
        cps = []
        for i in range(na):
            rh = ins[i].shape[1] // 2
            cp = pltpu.make_async_remote_copy(
                src_ref=ins[i].at[:, pl.ds((1 - c) * rh, rh), :], dst_ref=outs[i],
                send_sem=send.at[i], recv_sem=recv.at[i], device_id=sib, device_id_type=MESH)
            cp.start()
            cps.append(cp)
        for cp in cps:
            cp.wait()

    out_shape = [jax.ShapeDtypeStruct((g.shape[0], g.shape[1] // 2, g.shape[2]), g.dtype) for g in grads]
    return pl.pallas_call(
        body, name=name, in_specs=[HBM] * na, out_specs=[HBM] * na, out_shape=out_shape,
        scratch_shapes=[pltpu.SemaphoreType.DMA((na,)), pltpu.SemaphoreType.DMA((na,))],
    )(*grads)


def _chip_exchange(sums):
    na = len(sums)

    def body(*refs):
        ins, outs = refs[:na], refs[na:2 * na]
        send, recv = refs[2 * na:]
        x, y, c, chips = _place()
        cps = []
        for i in range(na):
            for j, (px, py) in enumerate(chips):
                cp = pltpu.make_async_remote_copy(
                    src_ref=ins[i].at[2 * px + py], dst_ref=outs[i].at[j],
                    send_sem=send.at[3 * i + j], recv_sem=recv.at[3 * i + j],
                    device_id=(px, py, c), device_id_type=MESH)
                cp.start()
                cps.append(cp)
        for i in range(na):
            for j, (px, py) in enumerate(chips):
                slot = outs[i].at[j]
                pltpu.make_async_remote_copy(
                    src_ref=slot, dst_ref=slot, send_sem=send.at[3 * i + j], recv_sem=recv.at[3 * i + j],
                    device_id=(px, py, c), device_id_type=MESH).wait_recv()
        for cp in cps:
            cp.wait_send()

    out_shape = [jax.ShapeDtypeStruct((3,) + s.shape[1:], s.dtype) for s in sums]
    return pl.pallas_call(
        body, name="grad_chip_exchange", in_specs=[HBM] * na, out_specs=[HBM] * na, out_shape=out_shape,
        scratch_shapes=[pltpu.SemaphoreType.DMA((3 * na,)), pltpu.SemaphoreType.DMA((3 * na,))],
    )(*sums)


def _sibling_swap_whole(halves):
    na = len(halves)

    def body(*refs):
        ins, outs = refs[:na], refs[na:2 * na]
        send, recv = refs[2 * na:]
        x, y, c, _ = _place()
        cps = []
        for i in range(na):
            cp = pltpu.make_async_remote_copy(
                src_ref=ins[i], dst_ref=outs[i], send_sem=send.at[i], recv_sem=recv.at[i],
                device_id=(x, y, 1 - c), device_id_type=MESH)
            cp.start()
            cps.append(cp)
        for cp in cps:
            cp.wait()

    out_shape = [jax.ShapeDtypeStruct(h.shape, h.dtype) for h in halves]
    return pl.pallas_call(
        body, name="grad_sibling_join", in_specs=[HBM] * na, out_specs=[HBM] * na, out_shape=out_shape,
        scratch_shapes=[pltpu.SemaphoreType.DMA((na,)), pltpu.SemaphoreType.DMA((na,))],
    )(*halves)


SEM = pl.BlockSpec(memory_space=pltpu.SEMAPHORE)
ANY = pl.BlockSpec(memory_space=pl.ANY)
EFFECT = pltpu.SideEffectType.DATAFLOW_SIDE_EFFECTING


def _split_copy_start(name, plan, srcs, lands, after):
    ns, nl = len(srcs), len(lands)

    def body(*refs):
        src_refs, land_refs = refs[:ns], refs[ns:ns + nl]
        send, recv = refs[ns + nl + 1], refs[ns + nl + 2]
        token = refs[-1]
        outgoing, _ = plan(src_refs, land_refs)
        for src, dst, dev, si, ri in outgoing:
            pltpu.make_async_remote_copy(src_ref=src, dst_ref=dst, send_sem=send.at[si], recv_sem=recv.at[ri],
                                         device_id=dev, device_id_type=MESH).start()
        token[...] = jnp.zeros_like(token)

    n_out, n_in = plan.counts
    thru = [pltpu.HBM(a.shape, a.dtype) for a in list(srcs) + list(lands)]
    res = pl.pallas_call(
        body, name=name,
        out_shape=[pltpu.SemaphoreType.DMA((n_out,)), pltpu.SemaphoreType.DMA((n_in,))] + thru
        + [jax.ShapeDtypeStruct((8, 128), F32)],
        in_specs=[HBM] * (ns + nl) + [ANY],
        out_specs=[SEM, SEM] + [HBM] * (ns + nl) + [pl.BlockSpec(memory_space=pltpu.VMEM)],
        input_output_aliases={i: 2 + i for i in range(ns + nl)},
        compiler_params=pltpu.CompilerParams(has_side_effects=EFFECT),
    )(*[pltpu.with_memory_space_constraint(a, pltpu.HBM) for a in list(srcs) + list(lands)], after)
    return res[0], res[1], res[2:2 + ns], res[2 + ns:2 + ns + nl], res[-1]


def _split_copy_wait(name, plan, send, recv, srcs, lands, after):
    ns, nl = len(srcs), len(lands)

    def body(*refs):
        src_refs, land_refs = refs[:ns], refs[ns:ns + nl]
        send_ref, recv_ref = refs[ns + nl], refs[ns + nl + 1]
        outgoing, arrivals = plan(src_refs, land_refs)
        for src, dst, dev, si, ri in outgoing:
            pltpu.make_async_remote_copy(src_ref=src, dst_ref=dst, send_sem=send_ref.at[si], recv_sem=recv_ref.at[ri],
                                         device_id=dev, device_id_type=MESH).wait_send()
        for view, ri in arrivals:
            pltpu.make_async_remote_copy(src_ref=view, dst_ref=view, send_sem=send_ref.at[0], recv_sem=recv_ref.at[ri],
                                         device_id=_place()[:3], device_id_type=MESH).wait_recv()

    thru = [pltpu.HBM(a.shape, a.dtype) for a in list(srcs) + list(lands)]
    res = pl.pallas_call(
        body, name=name, out_shape=thru,
        in_specs=[HBM] * (ns + nl) + [SEM, SEM, ANY], out_specs=[HBM] * (ns + nl),
        input_output_aliases={i: i for i in range(ns + nl)},
        compiler_params=pltpu.CompilerParams(has_side_effects=EFFECT),
    )(*srcs, *lands, send, recv, after)
    return res[:ns], res[ns:]


def _gather_plan(n_arrays):
    def plan(src_refs, land_refs):
        x, y, c, chips = _place()
        me = 2 * x + y
        outgoing, arrivals = [], []
        for i in range(n_arrays):
            rh = src_refs[i].shape[0] // 2
            mine = pl.ds(c * rh, rh)
            for j, (px, py) in enumerate(chips):
                for delta in range(2):
                    tc = c ^ delta
                    outgoing.append((src_refs[i].at[mine], land_refs[i].at[me, mine], (px, py, tc),
                                     6 * i + 2 * j + delta, 6 * i + 2 * j + delta))
                    theirs = pl.ds(tc * rh, rh)
                    arrivals.append((land_refs[i].at[2 * px + py, theirs], 6 * i + 2 * j + delta))
        return outgoing, arrivals

    plan.counts = (6 * n_arrays, 6 * n_arrays)
    return plan


def _exchange_plan(n_arrays):
    def plan(src_refs, land_refs):
        x, y, c, chips = _place()
        outgoing, arrivals = [], []
        for i in range(n_arrays):
            for j, (px, py) in enumerate(chips):
                outgoing.append((src_refs[i].at[2 * px + py], land_refs[i].at[j], (px, py, c), 3 * i + j, 3 * i + j))
                arrivals.append((land_refs[i].at[j], 3 * i + j))
        return outgoing, arrivals

    plan.counts = (3 * n_arrays, 3 * n_arrays)
    return plan


def _small_allreduce(vec):
    r, cdim = vec.shape
    n_dev = 8

    def body(v_ref, out_ref, buf, send, recv):
        x, y, c, _ = _place()
        me = 4 * x + 2 * y + c
        buf[me] = v_ref[...]
        cps = []
        for k in range(1, n_dev):
            dx, dy, dc = (k >> 2) & 1, (k >> 1) & 1, k & 1
            peer = (x ^ dx, y ^ dy, c ^ dc)
            cp = pltpu.make_async_remote_copy(
                src_ref=v_ref, dst_ref=buf.at[me], send_sem=send.at[k - 1], recv_sem=recv.at[k - 1],
                device_id=peer, device_id_type=MESH)
            cp.start()
            cps.append(cp)
        for k in range(1, n_dev):
            dx, dy, dc = (k >> 2) & 1, (k >> 1) & 1, k & 1
            src = 4 * (x ^ dx) + 2 * (y ^ dy) + (c ^ dc)
            slot = buf.at[src]
            pltpu.make_async_remote_copy(
                src_ref=slot, dst_ref=slot, send_sem=send.at[k - 1], recv_sem=recv.at[k - 1],
                device_id=(x ^ dx, y ^ dy, c ^ dc), device_id_type=MESH).wait_recv()
        for cp in cps:
            cp.wait_send()
        acc = buf[0]
        for k in range(1, n_dev):
            acc = acc + buf[k]
        out_ref[...] = acc

    vm = pl.BlockSpec(memory_space=pltpu.VMEM)
    return pl.pallas_call(
        body, name="small_allreduce", in_specs=[vm], out_specs=vm,
        out_shape=jax.ShapeDtypeStruct((r, cdim), F32),
        scratch_shapes=[pltpu.VMEM((n_dev, r, cdim), F32), pltpu.SemaphoreType.DMA((n_dev - 1,)),
                        pltpu.SemaphoreType.DMA((n_dev - 1,))],
    )(vec)


def _a_cols_to_head_major(w):
    lead = w.shape[:-1]
    q = w[..., :A_QK].reshape(lead + (A_HEADS, A_DK))
    k = w[..., A_QK:2 * A_QK].reshape(lead + (A_HEADS, A_DK))
    v = w[..., 2 * A_QK:2 * A_QK + A_VW].reshape(lead + (A_HEADS, A_DV))
    z = w[..., 2 * A_QK + A_VW:].reshape(lead + (A_HEADS, A_DV))
    return jnp.concatenate([q, k, v, z], axis=-1).reshape(lead + (A_HEADS * A_HEAD_COLS,))


def _a_cols_from_head_major(w):
    lead = w.shape[:-1]
    w = w.reshape(lead + (A_HEADS, A_HEAD_COLS))
    parts = [w[..., :A_DK], w[..., A_DK:2 * A_DK], w[..., 2 * A_DK:2 * A_DK + A_DV], w[..., 2 * A_DK + A_DV:]]
    return jnp.concatenate([p.reshape(lead + (-1,)) for p in parts], axis=-1)


def _conv_cols_to_head_major(w):
    lead = w.shape[:-1]
    q = w[..., :A_QK].reshape(lead + (A_HEADS, A_DK))
    k = w[..., A_QK:2 * A_QK].reshape(lead + (A_HEADS, A_DK))
    v = w[..., 2 * A_QK:].reshape(lead + (A_HEADS, A_DV))
    return jnp.concatenate([q, k, v], axis=-1).reshape(lead + (A_HEADS * A_CONV_COLS,))


def _conv_cols_from_head_major(w):
    lead = w.shape[:-1]
    w = w.reshape(lead + (A_HEADS, A_CONV_COLS))
    parts = [w[..., :A_DK], w[..., A_DK:2 * A_DK], w[..., 2 * A_DK:]]
    return jnp.concatenate([p.reshape(lead + (-1,)) for p in parts], axis=-1)


def _to_stream(a, bn, d):
    rest = a.shape[1:]
    s = a.shape[0] // bn
    a = a.reshape((bn, s // d, d) + rest)
    a = jnp.swapaxes(a, 1, 2)
    return a.reshape((bn * d, s // d) + rest)


def _from_stream(a, bn, d):
    rest = a.shape[2:]
    ln = a.shape[1]
    a = a.reshape((bn, d, ln) + rest)
    a = jnp.swapaxes(a, 1, 2)
    return a.reshape((bn * ln * d,) + rest)


def _b_group_cols(w, gi):
    n_qkv = 3 * B_GROUPS * B_W
    qkv = w[..., :n_qkv].reshape(w.shape[:-1] + (3, B_GROUPS, B_W))
    return qkv[..., :, gi, :].reshape(w.shape[:-1] + (3 * B_W,))


def _shard_major(g, ncols):
    r = g.shape[0]
    return jnp.swapaxes(g.reshape(r, N_CHIPS, ncols), 0, 1)


def _pack_rows(items):
    rows, offs = [], []
    at = 0
    for a in items:
        flat = a.reshape(-1).astype(F32)
        nr = -(-flat.shape[0] // 1024) * 8
        flat = jnp.pad(flat, (0, nr * 128 - flat.shape[0]))
        rows.append(flat.reshape(nr, 128))
        offs.append((at, nr, a.shape))
        at += nr
    return jnp.concatenate(rows, axis=0), offs


def _unpack_rows(packed, offs):
    out = []
    for at, nr, shape in offs:
        size = int(np.prod(shape)) if len(shape) else 1
        out.append(packed[at:at + nr].reshape(-1)[:size].reshape(shape))
    return out


def _local_step(x, positions, loss_target, norm_g, wa_in, conv_w, a_log, a_dt_bias, a_norm_g,
                b_q_norm_g, b_k_norm_g, start_token, late_weights, b_grads_ready):
    bn, s, d = x.shape
    t = bn * s
    n_chunks = s // A_CHUNK
    wa_main = _a_cols_to_head_major(wa_in[:, :A_MAIN])
    wa_tail = jnp.pad(wa_in[:, A_MAIN:], ((0, 0), (0, 128 - 2 * A_HEADS)))
    cw_hm = _conv_cols_to_head_major(conv_w)

    x0 = x.reshape(t, d)
    h0 = _rms_fwd(x0, norm_g[0:1] + start_token, "rms0_fwd")
    proj_a = _matmul(h0, wa_main, "nn", F32, "a_in_main")
    tail_a = _matmul(h0, wa_tail, "nn", F32, "a_in_tail")
    tail_t = jnp.swapaxes(tail_a[:, :2 * A_HEADS].reshape(bn, s, 2 * A_HEADS), 1, 2)
    tail_t = tail_t.reshape(bn, 2 * A_HEADS, n_chunks, A_CHUNK)
    beta, gc = _gdn_prep(tail_t, a_log[0], a_dt_bias[0])
    proj_a3 = proj_a.reshape(bn, s, A_MAIN)
    og_a, oraw_a, states, t_mats = _gdn_fwd(proj_a3, cw_hm, beta, gc, a_norm_g)
    wa_out, wb_in, wb_out = late_weights(og_a)
    wb_groups = [_b_group_cols(wb_in, gi) for gi in range(B_GROUPS)]
    wb_groups[0] = jnp.concatenate([wb_groups[0], wb_in[:, 3 * B_GROUPS * B_W:]], axis=1)
    x1 = _matmul(og_a.reshape(t, A_VW), wa_out, "nn", F32, "a_out", res=x0, tk=2048)

    h1 = _rms_fwd(x1, norm_g[1:2], "rms1_fwd")
    inv_freq = ROPE_THETA ** (-jnp.arange(0, ROPE_DIMS, 2, dtype=F32) / ROPE_DIMS)
    freq_row = jnp.concatenate([inv_freq, inv_freq, jnp.zeros((128 - ROPE_DIMS,), F32)]).reshape(1, 128)
    posf = jnp.broadcast_to(positions.astype(F32).reshape(t, 1), (t, 128))
    tabs = _rope_tables(posf, freq_row)
    h1_s, tabs_s, proj_b, qkv_b, o_b, lse_b = [], [], [], [], [], []
    for gi, dil in enumerate(B_DIL):
        hs = h1 if dil == 1 else _to_stream(h1, bn, dil).reshape(t, d)
        ts = tabs if dil == 1 else [_to_stream(tb, bn, dil).reshape(t, 128) for tb in tabs]
        pj = _matmul(hs, wb_groups[gi], "nn", F32, f"b_in_g{gi}")
        qkv = _qk_prep(pj, *ts, b_q_norm_g[0, gi:gi + 1], b_k_norm_g[0, gi:gi + 1], f"qk_prep_g{gi}")
        o_s, lse_s = _attn_fwd(qkv.reshape(bn * dil, s // dil, 3 * B_W), f"attn_fwd_g{gi}")
        h1_s.append(hs), tabs_s.append(ts), proj_b.append(pj), qkv_b.append(qkv)
        o_b.append(o_s.reshape(t, B_W) if dil == 1 else _from_stream(o_s, bn, dil))
        lse_b.append(lse_s.reshape(t, B_HEADS) if dil == 1 else _from_stream(lse_s, bn, dil))
    og_b = _merge_fwd(o_b, lse_b, proj_b[0])
    x2 = _matmul(og_b, wb_out, "nn", F32, "b_out", res=x1)

    d_x2, loss_parts = _loss_grad(x2, loss_target.reshape(t, d))
    loss_local = jnp.sum(loss_parts)

    d_x2b = d_x2.astype(BF16)
    g_wb_out = _matmul(og_b.T, d_x2b, "nn", F32, "b_out_dw")
    d_og_b = _matmul(d_x2b, wb_out, "nt", F32, "b_out_dx")
    d_o, lse_joint, delta, d_z = _merge_bwd(o_b, lse_b, proj_b[0], d_og_b)
    d_h1, g_wb_cols, g_qn, g_kn = [], [], [], []
    for gi, dil in enumerate(B_DIL):
        if dil == 1:
            do_s, lj_s, dl_s = d_o, lse_joint, delta
        else:
            do_s, lj_s, dl_s = (_to_stream(a, bn, dil).reshape(t, -1) for a in (d_o, lse_joint, delta))
        ns, ln = bn * dil, s // dil
        dq, dk, dv = _attn_bwd(qkv_b[gi].reshape(ns, ln, 3 * B_W), do_s.reshape(ns, ln, B_W),
                               lj_s.reshape(ns, ln, B_HEADS), dl_s.reshape(ns, ln, B_HEADS), f"attn_bwd_g{gi}")
        d_pj, d_gain = _qk_prep_bwd(proj_b[gi], *tabs_s[gi], b_q_norm_g[0, gi:gi + 1], b_k_norm_g[0, gi:gi + 1],
                                    dq.reshape(t, B_W), dk.reshape(t, B_W), dv.reshape(t, B_W),
                                    d_z if gi == 0 else None, f"qk_prep_bwd_g{gi}")
        g_wb_cols.append(_matmul(h1_s[gi].T, d_pj, "nn", F32, f"b_in_dw_g{gi}"))
        dh = _matmul(d_pj, wb_groups[gi], "nt", F32, f"b_in_dx_g{gi}")
        d_h1.append(dh if dil == 1 else _from_stream(dh.reshape(ns, ln, d), bn, dil))
        g_qn.append(d_gain[0]), g_kn.append(d_gain[1])
    d_x1, g_norm1 = _rms_bwd(x1, norm_g[1:2], d_h1, d_x2, "rms1_bwd")
    pieces = [g_wb_cols[gi][:, w * B_W:(w + 1) * B_W] for w in range(3) for gi in range(B_GROUPS)]
    g_wb_in = jnp.concatenate(pieces + [g_wb_cols[0][:, 3 * B_W:]], axis=1)

    d_x1b = (d_x1 + b_grads_ready(g_wb_in, g_wb_out)).astype(BF16)
    g_wa_out = _matmul(og_a.reshape(t, A_VW).T, d_x1b, "nn", F32, "a_out_dw")
    d_og_a = _matmul(d_x1b, wa_out, "nt", F32, "a_out_dx")
    d_pa, d_gc, d_beta, d_cw, d_ng = _gdn_bwd(proj_a3, cw_hm, beta, gc, a_norm_g, oraw_a, states, t_mats,
                                              d_og_a.reshape(bn, s, A_VW))
    d_tail_t, d_alog, d_dtb = _gdn_prep_bwd(tail_t, a_log[0], a_dt_bias[0], d_gc, d_beta)
    d_tail = jnp.swapaxes(d_tail_t.reshape(bn, 2 * A_HEADS, s), 1, 2).reshape(t, 2 * A_HEADS)
    d_tail = jnp.pad(d_tail, ((0, 0), (0, 128 - 2 * A_HEADS))).astype(BF16)
    d_pa = d_pa.reshape(t, A_MAIN)
    h0_t = h0.T
    g_wa_main = _matmul(h0_t, d_pa, "nn", F32, "a_in_dw_main")
    g_wa_tail = _matmul(h0_t, d_tail, "nn", F32, "a_in_dw_tail")
    d_h0 = _matmul(d_pa, wa_main, "nt", F32, "a_in_dx_main")
    d_h0t = _matmul(d_tail, wa_tail, "nt", F32, "a_in_dx_tail")
    d_x0, g_norm0 = _rms_bwd(x0, norm_g[0:1], [d_h0, d_h0t], d_x1, "rms0_bwd")
    g_wa_in = jnp.concatenate([_a_cols_from_head_major(g_wa_main), g_wa_tail[:, :2 * A_HEADS]], axis=1)

    gfull = {
        "norm_g": jnp.concatenate([g_norm0, g_norm1], axis=0), "a_w_in": g_wa_in,
        "a_conv_w": _conv_cols_from_head_major(jnp.sum(d_cw, axis=0)),
        "a_log": jnp.sum(d_alog[:, :, 0], axis=0), "a_dt_bias": jnp.sum(d_dtb[:, :, 0], axis=0),
        "a_norm_g": jnp.sum(d_ng[:, :, 0, :], axis=(0, 1)), "a_w_out": g_wa_out, "b_w_in": g_wb_in,
        "b_q_norm_g": jnp.stack(g_qn), "b_k_norm_g": jnp.stack(g_kn), "b_w_out": g_wb_out}
    return loss_local, d_x0.reshape(bn, s, d), gfull


def kernel(x, positions, norm_g, a_w_in, a_conv_w, a_log, a_dt_bias, a_norm_g, a_w_out, b_w_in, b_q_norm_g, b_k_norm_g, b_w_out, loss_target, m_norm_g, m_a_w_in, m_a_conv_w, m_a_log, m_a_dt_bias, m_a_norm_g, m_a_w_out, m_b_w_in, m_b_q_norm_g, m_b_k_norm_g, m_b_w_out, v_norm_g, v_a_w_in, v_a_conv_w, v_a_log, v_a_dt_bias, v_a_norm_g, v_a_w_out, v_b_w_in, v_b_q_norm_g, v_b_k_norm_g, v_b_w_out):
    d = x.shape[2]
    my_c = lax.axis_index("c")
    my_chip = 2 * lax.axis_index("x") + lax.axis_index("y")

    half_index = jnp.reshape(my_c, (1,)).astype(jnp.int32)
    chip_index = jnp.reshape(my_chip, (1,)).astype(jnp.int32)
    (ga_in,), g_conv = _weight_allgather([a_w_in[0].astype(BF16)], a_conv_w[0])
    wa_in = jnp.concatenate(ga_in, axis=1)
    conv_w = jnp.concatenate(g_conv, axis=1)

    late_shards = [a_w_out[0].astype(BF16), b_w_in[0].astype(BF16), b_w_out[0].astype(BF16)]
    late_lands = [lax.empty((N_CHIPS,) + s.shape, BF16) for s in late_shards]
    gather = _gather_plan(len(late_shards))
    ag_send, ag_recv, ag_srcs, ag_lands, ag_token = _split_copy_start(
        "late_weights_start", gather, late_shards, late_lands, conv_w)

    def late_weights(after):
        own, got = _split_copy_wait("late_weights_wait", gather, ag_send, ag_recv, ag_srcs, ag_lands, after)
        pick = lambda g, mine: [jnp.where(my_chip == k, mine, g[k]) for k in range(N_CHIPS)]
        ga_out, gb_in, gb_out = [pick(g, mine) for g, mine in zip(got, own)]
        return (jnp.concatenate(ga_out, axis=0), jnp.concatenate(gb_in, axis=1), jnp.concatenate(gb_out, axis=0))

    def reduce_to_chip_sums(mats, tag):
        recv_sib = _sibling_swap_halves([g.astype(BF16) for g in mats], f"grad_{tag}_sibling_swap")
        return [_pair_sum(g, r, half_index, f"grad_{tag}_pair_sum_{i}") for i, (g, r) in enumerate(zip(mats, recv_sib))]

    exchange = _exchange_plan(2)
    pending = {}

    def b_grads_ready(g_wb_in, g_wb_out):
        sums = reduce_to_chip_sums([_shard_major(g_wb_in, b_w_in.shape[2]), g_wb_out.reshape(N_CHIPS, -1, d)], "b")
        lands = [lax.empty((N_CHIPS - 1,) + s.shape[1:], BF16) for s in sums]
        pending["b"] = _split_copy_start("grad_b_exchange_start", exchange, sums, lands, chip_index)
        return pending["b"][4][0, 0]

    loss_local, d_x0, gfull = _local_step(x, positions, loss_target, norm_g, wa_in, conv_w, a_log, a_dt_bias,
                                          a_norm_g, b_q_norm_g, b_k_norm_g, ag_token[0, 0], late_weights, b_grads_ready)

    b_send, b_recv, b_srcs, b_lands, _ = pending["b"]
    b_sums, b_received = _split_copy_wait("grad_b_exchange_wait", exchange, b_send, b_recv, b_srcs, b_lands, d_x0)
    a_sums = reduce_to_chip_sums([_shard_major(gfull["a_w_in"], a_w_in.shape[2]),
                                  gfull["a_w_out"].reshape(N_CHIPS, -1, d)], "a")
    a_received = _chip_exchange(a_sums)
    chip_sums = [a_sums[0], a_sums[1], b_sums[0], b_sums[1]]
    received = [a_received[0], a_received[1], b_received[0], b_received[1]]
    halves = [_chip_sum(s, r, chip_index, f"grad_chip_sum_{i}") for i, (s, r) in enumerate(zip(chip_sums, received))]
    theirs = _sibling_swap_whole(halves)
    big = ("a_w_in", "a_w_out", "b_w_in", "b_w_out")
    big_halves = dict(zip(big, zip(halves, theirs)))

    small = [gfull["norm_g"], gfull["a_conv_w"], gfull["a_log"], gfull["a_dt_bias"], gfull["a_norm_g"],
             gfull["b_q_norm_g"], gfull["b_k_norm_g"], loss_local]
    packed, offs = _pack_rows(small)
    red = _unpack_rows(_small_allreduce(packed), offs)
    g_norm, g_conv_all, g_alog, g_dtb, g_ang, g_q, g_k, loss = red
    g_conv_mine = lax.dynamic_slice_in_dim(g_conv_all, my_chip * a_conv_w.shape[2], a_conv_w.shape[2], axis=1)

    grads = {
        "norm_g": g_norm, "a_conv_w": g_conv_mine[None], "a_log": g_alog[None], "a_dt_bias": g_dtb[None],
        "a_norm_g": g_ang[None], "b_q_norm_g": g_q[None], "b_k_norm_g": g_k[None]}
    weights = {"norm_g": norm_g, "a_w_in": a_w_in, "a_conv_w": a_conv_w, "a_log": a_log, "a_dt_bias": a_dt_bias,
               "a_norm_g": a_norm_g, "a_w_out": a_w_out, "b_w_in": b_w_in, "b_q_norm_g": b_q_norm_g,
               "b_k_norm_g": b_k_norm_g, "b_w_out": b_w_out}
    m_in = {"norm_g": m_norm_g, "a_w_in": m_a_w_in, "a_conv_w": m_a_conv_w, "a_log": m_a_log,
            "a_dt_bias": m_a_dt_bias, "a_norm_g": m_a_norm_g, "a_w_out": m_a_w_out, "b_w_in": m_b_w_in,
            "b_q_norm_g": m_b_q_norm_g, "b_k_norm_g": m_b_k_norm_g, "b_w_out": m_b_w_out}
    v_in = {"norm_g": v_norm_g, "a_w_in": v_a_w_in, "a_conv_w": v_a_conv_w, "a_log": v_a_log,
            "a_dt_bias": v_a_dt_bias, "a_norm_g": v_a_norm_g, "a_w_out": v_a_w_out, "b_w_in": v_b_w_in,
            "b_q_norm_g": v_b_q_norm_g, "b_k_norm_g": v_b_k_norm_g, "b_w_out": v_b_w_out}
    names = list(weights)

    delta_w, new_m, new_v = {}, {}, {}
    for nm in big:
        mine, other = big_halves[nm]
        grads[nm], delta_w[nm], new_m[nm], new_v[nm] = _adamw_shard(
            weights[nm], mine, other, m_in[nm], v_in[nm], half_index, f"adamw_{nm}")
    small_names = [nm for nm in names if nm not in big]
    packs = [_pack_rows([src[nm] for nm in small_names]) for src in (weights, grads, m_in, v_in)]
    offs = packs[0][1]
    dl, m2, v2 = _adamw(packs[0][0], packs[1][0], packs[2][0], packs[3][0], "adamw_small")
    for nm, a, b, c2 in zip(small_names, _unpack_rows(dl, offs), _unpack_rows(m2, offs), _unpack_rows(v2, offs)):
        delta_w[nm], new_m[nm], new_v[nm] = a, b, c2

    return (loss, d_x0, *[grads[nm] for nm in names], *[delta_w[nm] for nm in names],
            *[new_m[nm] for nm in names], *[new_v[nm] for nm in names])
```

```python
import functools
import math

import jax
import jax.numpy as jnp
import numpy as np
from jax import lax
from jax.experimental import pallas as pl
from jax.experimental.pallas import tpu as pltpu

F32 = jnp.float32
BF16 = jnp.bfloat16
MESH = pl.DeviceIdType.MESH

EPS = 1e-6
D_MODEL = 1024
A_HEADS = 8
A_DK = 128
A_DV = 256
A_QK = A_HEADS * A_DK
A_VW = A_HEADS * A_DV
A_MAIN = 2 * A_QK + 2 * A_VW
A_HEAD_COLS = 2 * A_DK + 2 * A_DV
A_CONV_COLS = 2 * A_DK + A_DV
A_CHUNK = 64
A_CONV = 4
B_GROUPS = 3
B_HEADS = 8
B_DH = 128
B_W = B_HEADS * B_DH
B_DIL = (1, 4, 16)
B_BLK = 128
ROPE_THETA = 500000.0
ROPE_DIMS = B_DH // 4
ADAM_LR, ADAM_B1, ADAM_B2, ADAM_EPS, ADAM_WD, ADAM_STEP = 0.001, 0.9, 0.999, 1e-08, 0.01, 10
N_CHIPS = 4
VMEM_BIG = 56 * 1024 * 1024


def _params(sem=None, vmem=None):
    return pltpu.CompilerParams(dimension_semantics=sem, vmem_limit_bytes=vmem)


def _dot(a, b, ca, cb):
    return lax.dot_general(a.astype(BF16), b.astype(BF16), (((ca,), (cb,)), ((), ())),
                           preferred_element_type=F32)


def _split3(a):
    hi = a.astype(BF16)
    r = a - hi.astype(F32)
    mid = r.astype(BF16)
    lo = (r - mid.astype(F32)).astype(BF16)
    return hi, mid, lo


def _dot_hi(a, b, ca, cb):
    a_hi, a_lo, _ = _split3(a)
    b_hi, b_lo, _ = _split3(b)
    dn = (((ca,), (cb,)), ((), ()))
    out = lax.dot_general(a_hi, b_hi, dn, preferred_element_type=F32)
    out = out + lax.dot_general(a_hi, b_lo, dn, preferred_element_type=F32)
    return out + lax.dot_general(a_lo, b_hi, dn, preferred_element_type=F32)


def _sigmoid(y):
    return 1.0 / (1.0 + jnp.exp(-y))


def _silu(y):
    return y * _sigmoid(y)


def _dsilu(y):
    s = _sigmoid(y)
    return s * (1.0 + y * (1.0 - s))


def _matmul(a, b, mode, out_dtype, name, res=None, tm=1024, tn=1024, tk=1024, n=None, b_spec=None, into=None):
    m, k = a.shape
    if n is None:
        n = b.shape[1] if mode == "nn" else b.shape[0]
    tm, tn, tk = min(tm, m), min(tn, n), min(tk, k)
    assert m % tm == 0 and n % tn == 0 and k % tk == 0, (name, a.shape, b.shape)
    nk = k // tk
    dims = {"nn": ((1,), (0,)), "nt": ((1,), (1,))}[mode]

    def body(*refs):
        a_ref, b_ref = refs[0], refs[1]
        r_ref = refs[2] if res is not None else None
        o_ref = refs[2 + (res is not None) + (into is not None)]
        prod = lax.dot_general(a_ref[...], b_ref[...], (dims, ((), ())), preferred_element_type=F32)

        def finish(r):
            if res is not None:
                r = r + r_ref[...]
            o_ref[...] = r.astype(out_dtype)

        if nk == 1:
            finish(prod)
            return
        acc = refs[-1]
        kk = pl.program_id(2)

        @pl.when(kk == 0)
        def _():
            acc[...] = prod

        @pl.when((kk > 0) & (kk < nk - 1))
        def _():
            acc[...] += prod

        @pl.when(kk == nk - 1)
        def _():
            finish(acc[...] + prod)

    a_spec = pl.BlockSpec((tm, tk), lambda i, j, kk: (i, kk))
    if b_spec is None and mode == "nt":
        b_spec = pl.BlockSpec((tn, tk), lambda i, j, kk: (j, kk))
    elif b_spec is None:
        b_spec = pl.BlockSpec((tk, tn), lambda i, j, kk: (kk, j))
    in_specs = [a_spec, b_spec]
    args = [a, b]
    if res is not None:
        in_specs.append(pl.BlockSpec((tm, tn), lambda i, j, kk: (i, j)))
        args.append(res)
    out_spec = pl.BlockSpec((tm, tn), lambda i, j, kk: (i, j))
    out_shape = jax.ShapeDtypeStruct((m, n), out_dtype)
    aliases = {}
    if into is not None:
        assert res is None
        buf, out_spec = into
        out_shape = jax.ShapeDtypeStruct(buf.shape, buf.dtype)
        in_specs.append(ANY)
        args.append(buf)
        aliases = {2: 0}
    return pl.pallas_call(
        body, name=name, grid=(m // tm, n // tn, nk),
        in_specs=in_specs, out_specs=out_spec, out_shape=out_shape, input_output_aliases=aliases,
        scratch_shapes=[pltpu.VMEM((tm, tn), F32)] if nk > 1 else [],
        compiler_params=_params(("parallel", "parallel", "arbitrary"), 48 * 1024 * 1024),
    )(*args)


def _rms_fwd(x, g, name, tm=256):
    t, d = x.shape

    def body(x_ref, g_ref, h_ref):
        xv = x_ref[...]
        r = lax.rsqrt(jnp.mean(xv * xv, axis=-1, keepdims=True) + EPS)
        h_ref[...] = (xv * r * g_ref[...]).astype(BF16)

    return pl.pallas_call(
        body, name=name, grid=(t // tm,),
        in_specs=[pl.BlockSpec((tm, d), lambda i: (i, 0)), pl.BlockSpec((1, d), lambda i: (0, 0))],
        out_specs=pl.BlockSpec((tm, d), lambda i: (i, 0)),
        out_shape=jax.ShapeDtypeStruct((t, d), BF16),
        compiler_params=_params(("parallel",)),
    )(x, g)


def _rms_bwd(x, g, dhs, dres, name, tm=256):
    t, d = x.shape
    n_dh = len(dhs)

    def body(*refs):
        x_ref, g_ref = refs[0], refs[1]
        dh_refs = refs[2:2 + n_dh]
        dres_ref, dx_ref, dg_ref = refs[2 + n_dh:]
        i = pl.program_id(0)

        @pl.when(i == 0)
        def _():
            dg_ref[...] = jnp.zeros_like(dg_ref)

        xv = x_ref[...]
        r = lax.rsqrt(jnp.mean(xv * xv, axis=-1, keepdims=True) + EPS)
        xh = xv * r
        dh = dh_refs[0][...]
        for ref in dh_refs[1:]:
            dh = dh + ref[...]
        dg_ref[0:1, :] += jnp.sum(dh * xh, axis=0, keepdims=True)
        dxh = dh * g_ref[...]
        dx = r * (dxh - xh * jnp.mean(dxh * xh, axis=-1, keepdims=True))
        dx_ref[...] = dx + dres_ref[...]

    row = pl.BlockSpec((tm, d), lambda i: (i, 0))
    dx, dg = pl.pallas_call(
        body, name=name, grid=(t // tm,),
        in_specs=[row, pl.BlockSpec((1, d), lambda i: (0, 0))] + [row] * n_dh + [row],
        out_specs=[row, pl.BlockSpec((8, d), lambda i: (0, 0))],
        out_shape=[jax.ShapeDtypeStruct((t, d), F32), jax.ShapeDtypeStruct((8, d), F32)],
        compiler_params=_params(("arbitrary",)),
    )(x, g, *dhs, dres)
    return dx, dg[0:1]


def _loss_grad(y, target, name="loss_grad", tm=256):
    t, d = y.shape
    nb = t // tm

    def body(y_ref, t_ref, dy_ref, part_ref):
        e = y_ref[...] - t_ref[...]
        dy_ref[...] = e * (1.0 / d)
        s = jnp.sum(jnp.sum(e * e, axis=1, keepdims=True), axis=0, keepdims=True) * (0.5 / d)
        part_ref[...] = jnp.broadcast_to(s, (8, 128))

    row = pl.BlockSpec((tm, d), lambda i: (i, 0))
    dy, part = pl.pallas_call(
        body, name=name, grid=(nb,), in_specs=[row, row],
        out_specs=[row, pl.BlockSpec((None, 8, 128), lambda i: (i, 0, 0))],
        out_shape=[jax.ShapeDtypeStruct((t, d), F32), jax.ShapeDtypeStruct((nb, 8, 128), F32)],
        compiler_params=_params(("parallel",)),
    )(y, target)
    return dy, part[:, 0, 0]


def _softplus(x):
    t = jnp.exp(-jnp.abs(x))
    return jnp.maximum(x, 0.0) + jnp.where(t < 1e-3, t * (1.0 - 0.5 * t), jnp.log(1.0 + t))


def _tri(rows_le_cols):
    r = lax.broadcasted_iota(jnp.int32, (A_CHUNK, A_CHUNK), 0)
    c = lax.broadcasted_iota(jnp.int32, (A_CHUNK, A_CHUNK), 1)
    return jnp.where((r <= c) if rows_le_cols else (r >= c), 1.0, 0.0).astype(BF16)


def _dot_exact_rhs(a, ones_bf16):
    dn = (((1,), (0,)), ((), ()))
    hi, mid, lo = _split3(a)
    out = lax.dot_general(hi, ones_bf16, dn, preferred_element_type=F32)
    out = out + lax.dot_general(mid, ones_bf16, dn, preferred_element_type=F32)
    return out + lax.dot_general(lo, ones_bf16, dn, preferred_element_type=F32)


def _gdn_prep(tail_t, a_log, dt_bias):
    bn, _, n, c = tail_t.shape

    def body(t_ref, alog_ref, dtb_ref, beta_ref, gc_ref):
        upper = _tri(True)
        for h in range(A_HEADS):
            beta_ref[h] = _sigmoid(t_ref[h])
            ea = jnp.exp(jnp.full((n, c), alog_ref[h], F32))
            g = -ea * _softplus(t_ref[A_HEADS + h] + dtb_ref[h])
            gc_ref[h] = _dot_exact_rhs(g, upper)

    smem = pl.BlockSpec(memory_space=pltpu.SMEM)
    blk = pl.BlockSpec((None, A_HEADS, n, c), lambda b: (b, 0, 0, 0))
    return pl.pallas_call(
        body, name="gdn_prep", grid=(bn,),
        in_specs=[pl.BlockSpec((None, 2 * A_HEADS, n, c), lambda b: (b, 0, 0, 0)), smem, smem],
        out_specs=[blk, blk],
        out_shape=[jax.ShapeDtypeStruct((bn, A_HEADS, n, c), F32)] * 2,
        compiler_params=_params(("parallel",)),
    )(tail_t, a_log, dt_bias)


def _gdn_prep_bwd(tail_t, a_log, dt_bias, d_gc, d_beta):
    bn, _, n, c = tail_t.shape

    def body(t_ref, alog_ref, dtb_ref, dgc_ref, dbeta_ref, dt_ref, dal_ref, ddt_ref):
        lower = _tri(False)
        for h in range(A_HEADS):
            beta = _sigmoid(t_ref[h])
            dt_ref[h] = dbeta_ref[h] * beta * (1.0 - beta)
            dg = _dot_exact_rhs(dgc_ref[h], lower)
            ea = jnp.exp(jnp.full((n, c), alog_ref[h], F32))
            xa = t_ref[A_HEADS + h] + dtb_ref[h]
            g = -ea * _softplus(xa)
            dxa = -ea * dg * _sigmoid(xa)
            dt_ref[A_HEADS + h] = dxa
            s1 = jnp.sum(jnp.sum(g * dg, axis=1, keepdims=True), axis=0, keepdims=True)
            s2 = jnp.sum(jnp.sum(dxa, axis=1, keepdims=True), axis=0, keepdims=True)
            dal_ref[h:h + 1, :] = jnp.broadcast_to(s1, (1, 128))
            ddt_ref[h:h + 1, :] = jnp.broadcast_to(s2, (1, 128))

    smem = pl.BlockSpec(memory_space=pltpu.SMEM)
    blk8 = pl.BlockSpec((None, A_HEADS, n, c), lambda b: (b, 0, 0, 0))
    blk16 = pl.BlockSpec((None, 2 * A_HEADS, n, c), lambda b: (b, 0, 0, 0))
    sm = pl.BlockSpec((None, A_HEADS, 128), lambda b: (b, 0, 0))
    return pl.pallas_call(
        body, name="gdn_prep_bwd", grid=(bn,),
        in_specs=[blk16, smem, smem, blk8, blk8],
        out_specs=[blk16, sm, sm],
        out_shape=[jax.ShapeDtypeStruct((bn, 2 * A_HEADS, n, c), F32),
                   jax.ShapeDtypeStruct((bn, A_HEADS, 128), F32),
                   jax.ShapeDtypeStruct((bn, A_HEADS, 128), F32)],
        compiler_params=_params(("parallel",)),
    )(tail_t, a_log, dt_bias, d_gc, d_beta)


HALO = 8


def _conv_window(x_ref, n, first, lo, width):
    if first:
        return jnp.concatenate([jnp.zeros((HALO, width), F32), x_ref[0:A_CHUNK, lo:lo + width]], axis=0)
    start = pl.multiple_of(n * A_CHUNK - HALO, HALO)
    return x_ref[pl.ds(start, A_CHUNK + HALO), lo:lo + width]


def _conv_taps(xw, w):
    y = w[A_CONV - 1:A_CONV, :] * xw
    for j in range(1, A_CONV):
        y = y + w[A_CONV - 1 - j:A_CONV - j, :] * pltpu.roll(xw, j, 0)
    return y[HALO:, :]


def _row_to_col(row, eye):
    c = eye.shape[0]
    return jnp.sum(jnp.where(eye, jnp.broadcast_to(row, (c, c)), 0.0), axis=1, keepdims=True)


def _col_to_row(col, eye):
    c = eye.shape[0]
    return jnp.sum(jnp.where(eye, jnp.broadcast_to(col, (c, c)), 0.0), axis=0, keepdims=True)


def _unit_lower_inverse(a, ri, ci):
    eye = jnp.where(ri == ci, 1.0, 0.0)
    a8 = jnp.where((ri >> 3) == (ci >> 3), a, 0.0)
    a2 = _dot(a8, a8, 1, 0)
    yield
    a4 = _dot(a2, a2, 1, 0)
    t = eye - a8
    t = t + _dot(t, a2, 1, 0)
    yield
    t = t + _dot(t, a4, 1, 0)
    yield
    for sh in (3, 4, 5):
        off = jnp.where(((ri >> (sh + 1)) == (ci >> (sh + 1))) & ((ri >> sh) != (ci >> sh)), a, 0.0)
        left = _dot(t, off, 1, 0)
        yield
        t = t - _dot(left, t, 1, 0)
        yield
    return t


def _round_robin(gens):
    live = list(gens)
    while live:
        nxt = []
        for g in live:
            try:
                next(g)
                nxt.append(g)
            except StopIteration:
                pass
        live = nxt


def _gdn_chunk_inputs(x_ref, cw, n, first):
    xw = _conv_window(x_ref, n, first, 0, A_CONV_COLS)
    y = _conv_taps(xw, cw)
    a = _silu(y)
    aq, ak, v = a[:, 0:A_DK], a[:, A_DK:2 * A_DK], a[:, 2 * A_DK:]
    rq = lax.rsqrt(jnp.sum(aq * aq, axis=1, keepdims=True) + EPS)
    rk = lax.rsqrt(jnp.sum(ak * ak, axis=1, keepdims=True) + EPS)
    return dict(xw=xw, y=y, aq=aq, ak=ak, rq=rq, rk=rk,
                q=aq * rq * (A_DK ** -0.5), k=ak * rk, v=v)


def _gdn_chunk_core(q, k, v, g_row, b_row, t_mat, ri, ci):
    eye = ri == ci
    g_col = _row_to_col(g_row, eye)
    b_col = _row_to_col(b_row, eye)
    causal = ri >= ci
    strict = ri > ci
    dec = jnp.where(causal, jnp.exp(jnp.where(causal, g_col - g_row, 0.0)), 0.0)
    gam = jnp.exp(g_col)
    g_last = g_row[:, A_CHUNK - 1:A_CHUNK]
    gam_last = jnp.exp(g_last)
    e = jnp.exp(g_last - g_col)
    kb = k * b_col
    bv = v * b_col
    kbg = kb * gam
    kk = _dot(kb, k, 1, 1)
    p = _dot(q, k, 1, 1) * dec
    yield
    a_mat = jnp.where(strict, kk * dec, 0.0)
    if t_mat is None:
        t_mat = yield from _unit_lower_inverse(a_mat, ri, ci)
    u = _dot(t_mat, bv, 1, 0)
    w = _dot(t_mat, kbg, 1, 0)
    yield
    return dict(eye=eye, g_col=g_col, b_col=b_col, dec=dec, strict=strict, causal=causal, gam=gam,
                gam_last=gam_last, e=e, kb=kb, bv=bv, kbg=kbg, a_mat=a_mat, t_mat=t_mat, u=u, w=w, p=p,
                qg=q * gam, kd=k * e)


def _gdn_fwd(proj_hm, cw_hm, beta, gc, norm_g):
    bn, s, _ = proj_hm.shape
    n = s // A_CHUNK

    def body(x_ref, cw_ref, beta_ref, gc_ref, ng_ref, og_ref, oraw_ref, st_ref, t_ref, state):
        ri = lax.broadcasted_iota(jnp.int32, (A_CHUNK, A_CHUNK), 0)
        ci = lax.broadcasted_iota(jnp.int32, (A_CHUNK, A_CHUNK), 1)
        cw = cw_ref[...]
        ng = ng_ref[...]
        state[...] = jnp.zeros_like(state)

        def chunk(i, first):
            rows = pl.ds(0 if first else pl.multiple_of(i * A_CHUNK, A_CHUNK), A_CHUNK)
            cin = _gdn_chunk_inputs(x_ref, cw, i, first)
            core = _gdn_chunk_core(cin["q"], cin["k"], cin["v"], gc_ref[pl.ds(i, 1), :],
                                   beta_ref[pl.ds(i, 1), :], None, ri, ci)
            st = state[...]
            st_ref[i] = st
            t_ref[i] = core["t_mat"]
            vn = core["u"] - _dot(core["w"], st, 1, 0)
            o = _dot(core["qg"], st, 1, 0) + _dot(core["p"], vn, 1, 0)
            state[...] = st * core["gam_last"] + _dot(core["kd"], vn, 0, 0)
            oraw_ref[rows, :] = o
            r = lax.rsqrt(jnp.mean(o * o, axis=1, keepdims=True) + EPS)
            z = x_ref[rows, A_CONV_COLS:A_HEAD_COLS]
            og_ref[rows, :] = (o * r * ng * _silu(z)).astype(BF16)

        chunk(0, True)
        lax.fori_loop(1, n, lambda i, c: (chunk(i, False), c)[1], 0)

    return pl.pallas_call(
        body, name="gdn_fwd", grid=(bn, A_HEADS),
        in_specs=[pl.BlockSpec((None, s, A_HEAD_COLS), lambda b, h: (b, 0, h)),
                  pl.BlockSpec((A_CONV, A_CONV_COLS), lambda b, h: (0, h)),
                  pl.BlockSpec((None, None, n, A_CHUNK), lambda b, h: (b, h, 0, 0)),
                  pl.BlockSpec((None, None, n, A_CHUNK), lambda b, h: (b, h, 0, 0)),
                  pl.BlockSpec((1, A_DV), lambda b, h: (0, 0))],
        out_specs=[pl.BlockSpec((None, s, A_DV), lambda b, h: (b, 0, h)),
                   pl.BlockSpec((None, s, A_DV), lambda b, h: (b, 0, h)),
                   pl.BlockSpec((None, None, n, A_DK, A_DV), lambda b, h: (b, h, 0, 0, 0)),
                   pl.BlockSpec((None, None, n, A_CHUNK, A_CHUNK), lambda b, h: (b, h, 0, 0, 0))],
        out_shape=[jax.ShapeDtypeStruct((bn, s, A_VW), BF16),
                   jax.ShapeDtypeStruct((bn, s, A_VW), F32),
                   jax.ShapeDtypeStruct((bn, A_HEADS, n, A_DK, A_DV), F32),
                   jax.ShapeDtypeStruct((bn, A_HEADS, n, A_CHUNK, A_CHUNK), F32)],
        scratch_shapes=[pltpu.VMEM((A_DK, A_DV), F32)],
        compiler_params=_params(("parallel", "parallel"), VMEM_BIG),
    )(proj_hm, cw_hm, beta, gc, norm_g)


def _gdn_bwd(proj_hm, cw_hm, beta, gc, norm_g, oraw, states, t_mats, dog):
    bn, s, _ = proj_hm.shape
    n = s // A_CHUNK

    def body(x_ref, cw_ref, beta_ref, gc_ref, ng_ref, oraw_ref, st_ref, t_ref, dog_ref,
             dx_ref, dgc_ref, dbeta_ref, dcw_ref, dng_ref, dstate, dy_next):
        ri = lax.broadcasted_iota(jnp.int32, (A_CHUNK, A_CHUNK), 0)
        ci = lax.broadcasted_iota(jnp.int32, (A_CHUNK, A_CHUNK), 1)
        cw = cw_ref[...]
        ng = ng_ref[...]
        dstate[...] = jnp.zeros_like(dstate)
        dy_next[...] = jnp.zeros_like(dy_next)
        dcw_ref[...] = jnp.zeros_like(dcw_ref)
        dng_ref[...] = jnp.zeros_like(dng_ref)

        def chunk(i, first):
            rows = pl.ds(0 if first else pl.multiple_of(i * A_CHUNK, A_CHUNK), A_CHUNK)
            cin = _gdn_chunk_inputs(x_ref, cw, i, first)
            q, k, v = cin["q"], cin["k"], cin["v"]
            g_row = gc_ref[pl.ds(i, 1), :]
            b_row = beta_ref[pl.ds(i, 1), :]
            cr = _gdn_chunk_core(q, k, v, g_row, b_row, t_ref[i], ri, ci)
            eye, dec, gam, e = cr["eye"], cr["dec"], cr["gam"], cr["e"]
            b_col, t_mat, u, w, p = cr["b_col"], cr["t_mat"], cr["u"], cr["w"], cr["p"]
            st = st_ref[i]
            ds_out = dstate[...]

            o = oraw_ref[rows, :]
            z = x_ref[rows, A_CONV_COLS:A_HEAD_COLS]
            d_og = dog_ref[rows, :]
            r = lax.rsqrt(jnp.mean(o * o, axis=1, keepdims=True) + EPS)
            oh = o * r
            d_on = d_og * _silu(z)
            dz = d_og * oh * ng * _dsilu(z)
            dng_ref[0:1, :] += jnp.sum(d_on * oh, axis=0, keepdims=True)
            d_oh = d_on * ng
            d_o = r * (d_oh - oh * jnp.mean(d_oh * oh, axis=1, keepdims=True))

            vn = u - _dot(w, st, 1, 0)
            d_vn = _dot(p, d_o, 0, 0) + _dot(cr["kd"], ds_out, 1, 0)
            d_p = jnp.where(cr["causal"], _dot(d_o, vn, 1, 1), 0.0)
            d_qg = _dot(d_o, st, 1, 1)
            d_kd = _dot(vn, ds_out, 1, 1)
            d_gam_last = jnp.sum(jnp.sum(st * ds_out, axis=1, keepdims=True), axis=0, keepdims=True)
            d_w = -_dot(d_vn, st, 1, 1)
            dstate[...] = _dot(cr["qg"], d_o, 0, 0) + ds_out * cr["gam_last"] - _dot(w, d_vn, 0, 0)
            d_bv = _dot(t_mat, d_vn, 0, 0)
            d_kbg = _dot(t_mat, d_w, 0, 0)
            d_a = jnp.where(cr["strict"], -(_dot(d_bv, u, 1, 1) + _dot(d_kbg, w, 1, 1)), 0.0)
            m_a = d_a * dec
            n_p = d_p * dec
            d_kb = _dot(m_a, k, 1, 0) + d_kbg * gam
            d_q = _dot(n_p, k, 1, 0) + d_qg * gam
            d_k = (_dot(m_a, cr["kb"], 0, 0) + _dot(n_p, q, 0, 0) + d_kd * e + d_kb * b_col)
            d_v = d_bv * b_col
            d_beta_col = (jnp.sum(d_bv * v, axis=1, keepdims=True)
                          + jnp.sum(d_kb * k, axis=1, keepdims=True))
            gterm = d_a * cr["a_mat"] + d_p * p
            d_e = jnp.sum(d_kd * k, axis=1, keepdims=True) * e
            d_g_col = (jnp.sum(gterm, axis=1, keepdims=True)
                       + (jnp.sum(d_qg * q, axis=1, keepdims=True)
                          + jnp.sum(d_kbg * cr["kb"], axis=1, keepdims=True)) * gam
                       - d_e)
            d_g_last = jnp.sum(d_e, axis=0, keepdims=True) + d_gam_last * cr["gam_last"]
            lane = lax.broadcasted_iota(jnp.int32, (1, A_CHUNK), 1)
            d_g_row = (_col_to_row(d_g_col, eye) - jnp.sum(gterm, axis=0, keepdims=True)
                       + jnp.where(lane == A_CHUNK - 1, d_g_last, 0.0))
            dgc_ref[pl.ds(i, 1), :] = d_g_row
            dbeta_ref[pl.ds(i, 1), :] = _col_to_row(d_beta_col, eye)

            qh = cin["aq"] * cin["rq"]
            kh = cin["ak"] * cin["rk"]
            d_qh = d_q * (A_DK ** -0.5)
            d_aq = cin["rq"] * (d_qh - qh * jnp.sum(d_qh * qh, axis=1, keepdims=True))
            d_ak = cin["rk"] * (d_k - kh * jnp.sum(d_k * kh, axis=1, keepdims=True))
            d_y = jnp.concatenate([d_aq, d_ak, d_v], axis=1) * _dsilu(cin["y"])
            xw = cin["xw"]
            dyw = jnp.concatenate([d_y, dy_next[...]], axis=0)
            d_x = cw[A_CONV - 1:A_CONV, :] * dyw
            for j in range(1, A_CONV):
                d_x = d_x + cw[A_CONV - 1 - j:A_CONV - j, :] * pltpu.roll(dyw, A_CHUNK + HALO - j, 0)
            d_x = d_x[0:A_CHUNK, :]
            dy_pad = jnp.concatenate([jnp.zeros((HALO, A_CONV_COLS), F32), d_y], axis=0)
            for j in range(A_CONV):
                xs = xw if j == 0 else pltpu.roll(xw, j, 0)
                dcw_ref[A_CONV - 1 - j:A_CONV - j, :] += jnp.sum(dy_pad * xs, axis=0, keepdims=True)
            dy_next[...] = d_y[0:HALO, :]
            dx_ref[rows, 0:A_CONV_COLS] = d_x.astype(BF16)
            dx_ref[rows, A_CONV_COLS:A_HEAD_COLS] = dz.astype(BF16)

        lax.fori_loop(0, n - 1, lambda i, c: (chunk(n - 1 - i, False), c)[1], 0)
        chunk(0, True)

    hn = lambda b, h: (b, h, 0, 0)
    return pl.pallas_call(
        body, name="gdn_bwd", grid=(bn, A_HEADS),
        in_specs=[pl.BlockSpec((None, s, A_HEAD_COLS), lambda b, h: (b, 0, h)),
                  pl.BlockSpec((A_CONV, A_CONV_COLS), lambda b, h: (0, h)),
                  pl.BlockSpec((None, None, n, A_CHUNK), hn),
                  pl.BlockSpec((None, None, n, A_CHUNK), hn),
                  pl.BlockSpec((1, A_DV), lambda b, h: (0, 0)),
                  pl.BlockSpec((None, s, A_DV), lambda b, h: (b, 0, h)),
                  pl.BlockSpec((None, None, n, A_DK, A_DV), lambda b, h: (b, h, 0, 0, 0)),
                  pl.BlockSpec((None, None, n, A_CHUNK, A_CHUNK), lambda b, h: (b, h, 0, 0, 0)),
                  pl.BlockSpec((None, s, A_DV), lambda b, h: (b, 0, h))],
        out_specs=[pl.BlockSpec((None, s, A_HEAD_COLS), lambda b, h: (b, 0, h)),
                   pl.BlockSpec((None, None, n, A_CHUNK), hn),
                   pl.BlockSpec((None, None, n, A_CHUNK), hn),
                   pl.BlockSpec((None, A_CONV, A_CONV_COLS), lambda b, h: (b, 0, h)),
                   pl.BlockSpec((None, None, 8, A_DV), hn)],
        out_shape=[jax.ShapeDtypeStruct((bn, s, A_HEADS * A_HEAD_COLS), BF16),
                   jax.ShapeDtypeStruct((bn, A_HEADS, n, A_CHUNK), F32),
                   jax.ShapeDtypeStruct((bn, A_HEADS, n, A_CHUNK), F32),
                   jax.ShapeDtypeStruct((bn, A_CONV, A_HEADS * A_CONV_COLS), F32),
                   jax.ShapeDtypeStruct((bn, A_HEADS, 8, A_DV), F32)],
        scratch_shapes=[pltpu.VMEM((A_DK, A_DV), F32), pltpu.VMEM((HALO, A_CONV_COLS), F32)],
        compiler_params=_params(("parallel", "parallel"), VMEM_BIG),
    )(proj_hm, cw_hm, beta, gc, norm_g, oraw, states, t_mats, dog)


A_SEQ_BLK = 512
A_BLK_CHUNKS = A_SEQ_BLK // A_CHUNK


def _gdn_halo(proj_hm):
    bn, s, w = proj_hm.shape
    last = proj_hm.reshape(bn, s // A_SEQ_BLK, A_SEQ_BLK, w)[:, :, A_SEQ_BLK - HALO:, :]
    return jnp.concatenate([jnp.zeros((bn, 1, HALO, w), proj_hm.dtype), last[:, :-1]], axis=1)


def _gdn_window(x_ref, halo_ref, ci, first, lo):
    if first:
        return jnp.concatenate([halo_ref[:, lo:lo + A_CONV_COLS], x_ref[0:A_CHUNK, lo:lo + A_CONV_COLS]], axis=0)
    start = pl.multiple_of(ci * A_CHUNK - HALO, HALO)
    return x_ref[pl.ds(start, A_CHUNK + HALO), lo:lo + A_CONV_COLS]


def _gdn_chunk_prep(xw, cw):
    y = _conv_taps(xw, cw)
    a = _silu(y)
    aq, ak, v = a[:, 0:A_DK], a[:, A_DK:2 * A_DK], a[:, 2 * A_DK:]
    rq = lax.rsqrt(jnp.sum(aq * aq, axis=1, keepdims=True) + EPS)
    rk = lax.rsqrt(jnp.sum(ak * ak, axis=1, keepdims=True) + EPS)
    return dict(xw=xw, y=y, aq=aq, ak=ak, rq=rq, rk=rk, q=aq * rq * (A_DK ** -0.5), k=ak * rk, v=v)


def _gdn_fwd(proj_hm, cw_hm, beta, gc, norm_g, hp=4):
    bn, s, _ = proj_hm.shape
    n = s // A_CHUNK
    nsb = s // A_SEQ_BLK
    halo = _gdn_halo(proj_hm)

    def body(x_ref, halo_ref, cw_ref, beta_ref, gc_ref, ng_ref, og_ref, oraw_ref, st_ref, t_ref, state):
        ri = lax.broadcasted_iota(jnp.int32, (A_CHUNK, A_CHUNK), 0)
        ci_ = lax.broadcasted_iota(jnp.int32, (A_CHUNK, A_CHUNK), 1)
        ng = ng_ref[...]

        @pl.when(pl.program_id(2) == 0)
        def _():
            state[...] = jnp.zeros_like(state)

        def one_head(hh, ci, first, rows):
            lo = hh * A_HEAD_COLS
            cw = cw_ref[:, hh * A_CONV_COLS:(hh + 1) * A_CONV_COLS]
            cin = _gdn_chunk_prep(_gdn_window(x_ref, halo_ref, ci, first, lo), cw)
            core = yield from _gdn_chunk_core(cin["q"], cin["k"], cin["v"], gc_ref[hh, pl.ds(ci, 1), :],
                                              beta_ref[hh, pl.ds(ci, 1), :], None, ri, ci_)
            st = state[hh]
            st_ref[hh, ci] = st
            t_ref[hh, ci] = core["t_mat"]
            vn = core["u"] - _dot(core["w"], st, 1, 0)
            qs = _dot(core["qg"], st, 1, 0)
            yield
            o = qs + _dot(core["p"], vn, 1, 0)
            state[hh] = st * core["gam_last"] + _dot(core["kd"], vn, 0, 0)
            yield
            ocols = slice(hh * A_DV, (hh + 1) * A_DV)
            oraw_ref[rows, ocols] = o
            r = lax.rsqrt(jnp.mean(o * o, axis=1, keepdims=True) + EPS)
            z = x_ref[rows, lo + A_CONV_COLS:lo + A_HEAD_COLS]
            og_ref[rows, ocols] = (o * r * ng * _silu(z)).astype(BF16)

        def chunk(ci, first):
            rows = pl.ds(0 if first else pl.multiple_of(ci * A_CHUNK, A_CHUNK), A_CHUNK)
            _round_robin([one_head(hh, ci, first, rows) for hh in range(hp)])

        chunk(0, True)
        lax.fori_loop(1, A_BLK_CHUNKS, lambda i, c: (chunk(i, False), c)[1], 0)

    small = pl.BlockSpec((None, hp, A_BLK_CHUNKS, A_CHUNK), lambda b, h, j: (b, h, j, 0))
    return pl.pallas_call(
        body, name="gdn_fwd", grid=(bn, A_HEADS // hp, nsb),
        in_specs=[pl.BlockSpec((None, A_SEQ_BLK, hp * A_HEAD_COLS), lambda b, h, j: (b, j, h)),
                  pl.BlockSpec((None, None, HALO, hp * A_HEAD_COLS), lambda b, h, j: (b, j, 0, h)),
                  pl.BlockSpec((A_CONV, hp * A_CONV_COLS), lambda b, h, j: (0, h)),
                  small, small,
                  pl.BlockSpec((1, A_DV), lambda b, h, j: (0, 0))],
        out_specs=[pl.BlockSpec((None, A_SEQ_BLK, hp * A_DV), lambda b, h, j: (b, j, h)),
                   pl.BlockSpec((None, A_SEQ_BLK, hp * A_DV), lambda b, h, j: (b, j, h)),
                   pl.BlockSpec((None, hp, A_BLK_CHUNKS, A_DK, A_DV), lambda b, h, j: (b, h, j, 0, 0)),
                   pl.BlockSpec((None, hp, A_BLK_CHUNKS, A_CHUNK, A_CHUNK), lambda b, h, j: (b, h, j, 0, 0))],
        out_shape=[jax.ShapeDtypeStruct((bn, s, A_VW), BF16),
                   jax.ShapeDtypeStruct((bn, s, A_VW), F32),
                   jax.ShapeDtypeStruct((bn, A_HEADS, n, A_DK, A_DV), F32),
                   jax.ShapeDtypeStruct((bn, A_HEADS, n, A_CHUNK, A_CHUNK), F32)],
        scratch_shapes=[pltpu.VMEM((hp, A_DK, A_DV), F32)],
        compiler_params=_params(("parallel", "parallel", "arbitrary"), VMEM_BIG),
    )(proj_hm, halo, cw_hm, beta, gc, norm_g)


def _gdn_bwd(proj_hm, cw_hm, beta, gc, norm_g, oraw, states, t_mats, dog, hp=4):
    bn, s, _ = proj_hm.shape
    n = s // A_CHUNK
    nsb = s // A_SEQ_BLK
    halo = _gdn_halo(proj_hm)

    def body(x_ref, halo_ref, cw_ref, beta_ref, gc_ref, ng_ref, oraw_ref, st_ref, t_ref, dog_ref,
             dx_ref, dgc_ref, dbeta_ref, dcw_ref, dng_ref, dstate, dy_next):
        ri = lax.broadcasted_iota(jnp.int32, (A_CHUNK, A_CHUNK), 0)
        ci_ = lax.broadcasted_iota(jnp.int32, (A_CHUNK, A_CHUNK), 1)
        lane = lax.broadcasted_iota(jnp.int32, (1, A_CHUNK), 1)
        ng = ng_ref[...]

        @pl.when(pl.program_id(2) == 0)
        def _():
            dstate[...] = jnp.zeros_like(dstate)
            dy_next[...] = jnp.zeros_like(dy_next)
            dcw_ref[...] = jnp.zeros_like(dcw_ref)
            dng_ref[...] = jnp.zeros_like(dng_ref)

        def one_head(hh, ci, first, rows):
            lo = hh * A_HEAD_COLS
            ccols = slice(hh * A_CONV_COLS, (hh + 1) * A_CONV_COLS)
            ocols = slice(hh * A_DV, (hh + 1) * A_DV)
            cw = cw_ref[:, ccols]
            cin = _gdn_chunk_prep(_gdn_window(x_ref, halo_ref, ci, first, lo), cw)
            q, k, v = cin["q"], cin["k"], cin["v"]
            cr = yield from _gdn_chunk_core(q, k, v, gc_ref[hh, pl.ds(ci, 1), :], beta_ref[hh, pl.ds(ci, 1), :],
                                            t_ref[hh, ci], ri, ci_)
            eye, dec, gam, e = cr["eye"], cr["dec"], cr["gam"], cr["e"]
            b_col, t_mat, u, w, p = cr["b_col"], cr["t_mat"], cr["u"], cr["w"], cr["p"]
            st = st_ref[hh, ci]
            ds_out = dstate[hh]

            o = oraw_ref[rows, ocols]
            z = x_ref[rows, lo + A_CONV_COLS:lo + A_HEAD_COLS]
            d_og = dog_ref[rows, ocols]
            r = lax.rsqrt(jnp.mean(o * o, axis=1, keepdims=True) + EPS)
            oh = o * r
            d_on = d_og * _silu(z)
            dz = d_og * oh * ng * _dsilu(z)
            dng_ref[hh, 0:1, :] += jnp.sum(d_on * oh, axis=0, keepdims=True)
            d_oh = d_on * ng
            d_o = r * (d_oh - oh * jnp.mean(d_oh * oh, axis=1, keepdims=True))

            vn = u - _dot(w, st, 1, 0)
            d_vn = _dot(p, d_o, 0, 0) + _dot(cr["kd"], ds_out, 1, 0)
            d_qg = _dot(d_o, st, 1, 1)
            qgdo = _dot(cr["qg"], d_o, 0, 0)
            yield
            d_p = jnp.where(cr["causal"], _dot(d_o, vn, 1, 1), 0.0)
            d_kd = _dot(vn, ds_out, 1, 1)
            d_gam_last = jnp.sum(jnp.sum(st * ds_out, axis=1, keepdims=True), axis=0, keepdims=True)
            d_w = -_dot(d_vn, st, 1, 1)
            dstate[hh] = qgdo + ds_out * cr["gam_last"] - _dot(w, d_vn, 0, 0)
            d_bv = _dot(t_mat, d_vn, 0, 0)
            yield
            d_kbg = _dot(t_mat, d_w, 0, 0)
            n_p = d_p * dec
            d_q = _dot(n_p, k, 1, 0) + d_qg * gam
            npq = _dot(n_p, q, 0, 0)
            yield
            d_a = jnp.where(cr["strict"], -(_dot(d_bv, u, 1, 1) + _dot(d_kbg, w, 1, 1)), 0.0)
            yield
            m_a = d_a * dec
            d_kb = _dot(m_a, k, 1, 0) + d_kbg * gam
            d_k = (_dot(m_a, cr["kb"], 0, 0) + npq + d_kd * e + d_kb * b_col)
            yield
            d_v = d_bv * b_col
            d_beta_col = (jnp.sum(d_bv * v, axis=1, keepdims=True)
                          + jnp.sum(d_kb * k, axis=1, keepdims=True))
            gterm = d_a * cr["a_mat"] + d_p * p
            d_e = jnp.sum(d_kd * k, axis=1, keepdims=True) * e
            d_g_col = (jnp.sum(gterm, axis=1, keepdims=True)
                       + (jnp.sum(d_qg * q, axis=1, keepdims=True)
                          + jnp.sum(d_kbg * cr["kb"], axis=1, keepdims=True)) * gam
                       - d_e)
            d_g_last = jnp.sum(d_e, axis=0, keepdims=True) + d_gam_last * cr["gam_last"]
            d_g_row = (_col_to_row(d_g_col, eye) - jnp.sum(gterm, axis=0, keepdims=True)
                       + jnp.where(lane == A_CHUNK - 1, d_g_last, 0.0))
            dgc_ref[hh, pl.ds(ci, 1), :] = d_g_row
            dbeta_ref[hh, pl.ds(ci, 1), :] = _col_to_row(d_beta_col, eye)

            qh = cin["aq"] * cin["rq"]
            kh = cin["ak"] * cin["rk"]
            d_qh = d_q * (A_DK ** -0.5)
            d_aq = cin["rq"] * (d_qh - qh * jnp.sum(d_qh * qh, axis=1, keepdims=True))
            d_ak = cin["rk"] * (d_k - kh * jnp.sum(d_k * kh, axis=1, keepdims=True))
            d_y = jnp.concatenate([d_aq, d_ak, d_v], axis=1) * _dsilu(cin["y"])
            xw = cin["xw"]
            dyw = jnp.concatenate([d_y, dy_next[hh]], axis=0)
            d_x = cw[A_CONV - 1:A_CONV, :] * dyw
            for j in range(1, A_CONV):
                d_x = d_x + cw[A_CONV - 1 - j:A_CONV - j, :] * pltpu.roll(dyw, A_CHUNK + HALO - j, 0)
            dy_pad = jnp.concatenate([jnp.zeros((HALO, A_CONV_COLS), F32), d_y], axis=0)
            for j in range(A_CONV):
                xs = xw if j == 0 else pltpu.roll(xw, j, 0)
                dcw_ref[A_CONV - 1 - j:A_CONV - j, ccols] += jnp.sum(dy_pad * xs, axis=0, keepdims=True)
            dy_next[hh] = d_y[0:HALO, :]
            dx_ref[rows, lo:lo + A_CONV_COLS] = d_x[0:A_CHUNK, :].astype(BF16)
            dx_ref[rows, lo + A_CONV_COLS:lo + A_HEAD_COLS] = dz.astype(BF16)

        def chunk(ci, first):
            rows = pl.ds(0 if first else pl.multiple_of(ci * A_CHUNK, A_CHUNK), A_CHUNK)
            _round_robin([one_head(hh, ci, first, rows) for hh in range(hp)])

        lax.fori_loop(0, A_BLK_CHUNKS - 1, lambda i, c: (chunk(A_BLK_CHUNKS - 1 - i, False), c)[1], 0)
        chunk(0, True)

    rev = lambda j: nsb - 1 - j
    small = pl.BlockSpec((None, hp, A_BLK_CHUNKS, A_CHUNK), lambda b, h, j: (b, h, rev(j), 0))
    wide = pl.BlockSpec((None, A_SEQ_BLK, hp * A_HEAD_COLS), lambda b, h, j: (b, rev(j), h))
    val = pl.BlockSpec((None, A_SEQ_BLK, hp * A_DV), lambda b, h, j: (b, rev(j), h))
    return pl.pallas_call(
        body, name="gdn_bwd", grid=(bn, A_HEADS // hp, nsb),
        in_specs=[wide,
                  pl.BlockSpec((None, None, HALO, hp * A_HEAD_COLS), lambda b, h, j: (b, rev(j), 0, h)),
                  pl.BlockSpec((A_CONV, hp * A_CONV_COLS), lambda b, h, j: (0, h)),
                  small, small,
                  pl.BlockSpec((1, A_DV), lambda b, h, j: (0, 0)),
                  val,
                  pl.BlockSpec((None, hp, A_BLK_CHUNKS, A_DK, A_DV), lambda b, h, j: (b, h, rev(j), 0, 0)),
                  pl.BlockSpec((None, hp, A_BLK_CHUNKS, A_CHUNK, A_CHUNK), lambda b, h, j: (b, h, rev(j), 0, 0)),
                  val],
        out_specs=[wide, small, small,
                   pl.BlockSpec((None, A_CONV, hp * A_CONV_COLS), lambda b, h, j: (b, 0, h)),
                   pl.BlockSpec((None, hp, 8, A_DV), lambda b, h, j: (b, h, 0, 0))],
        out_shape=[jax.ShapeDtypeStruct((bn, s, A_HEADS * A_HEAD_COLS), BF16),
                   jax.ShapeDtypeStruct((bn, A_HEADS, n, A_CHUNK), F32),
                   jax.ShapeDtypeStruct((bn, A_HEADS, n, A_CHUNK), F32),
                   jax.ShapeDtypeStruct((bn, A_CONV, A_HEADS * A_CONV_COLS), F32),
                   jax.ShapeDtypeStruct((bn, A_HEADS, 8, A_DV), F32)],
        scratch_shapes=[pltpu.VMEM((hp, A_DK, A_DV), F32), pltpu.VMEM((hp, HALO, A_CONV_COLS), F32)],
        compiler_params=_params(("parallel", "parallel", "arbitrary"), VMEM_BIG),
    )(proj_hm, halo, cw_hm, beta, gc, norm_g, oraw, states, t_mats, dog)


def _rope_tables(posf, inv_freq_row):
    t = posf.shape[0]
    tm = 512

    def body(p_ref, f_ref, c_ref, sa_ref, sb_ref):
        ang = p_ref[...] * f_ref[...]
        lane = lax.broadcasted_iota(jnp.int32, ang.shape, 1)
        half = ROPE_DIMS // 2
        c_ref[...] = jnp.where(lane < ROPE_DIMS, jnp.cos(ang), 1.0)
        sn = jnp.sin(ang)
        sa_ref[...] = jnp.where(lane < half, -sn, 0.0)
        sb_ref[...] = jnp.where((lane >= half) & (lane < ROPE_DIMS), sn, 0.0)

    row = pl.BlockSpec((tm, 128), lambda i: (i, 0))
    return pl.pallas_call(
        body, name="rope_tables", grid=(t // tm,),
        in_specs=[row, pl.BlockSpec((1, 128), lambda i: (0, 0))], out_specs=[row] * 3,
        out_shape=[jax.ShapeDtypeStruct((t, 128), F32)] * 3,
        compiler_params=_params(("parallel",)),
    )(posf, inv_freq_row)


def _rope(x, c, sa, sb):
    half = ROPE_DIMS // 2
    return x * c + pltpu.roll(x, 128 - half, 1) * sa + pltpu.roll(x, half, 1) * sb


def _rope_t(d, c, sa, sb):
    half = ROPE_DIMS // 2
    return d * c + pltpu.roll(d * sa, half, 1) + pltpu.roll(d * sb, 128 - half, 1)


def _qk_prep(proj, c, sa, sb, qg, kg, name, tm=256):
    t = proj.shape[0]
    wide = proj.shape[1]

    def body(x_ref, c_ref, sa_ref, sb_ref, qg_ref, kg_ref, o_ref):
        cc, s1, s2 = c_ref[...], sa_ref[...], sb_ref[...]
        for which, g_ref in ((0, qg_ref), (1, kg_ref)):
            g = g_ref[...]
            for h in range(B_HEADS):
                lo = which * B_W + h * B_DH
                xv = x_ref[:, lo:lo + B_DH]
                r = lax.rsqrt(jnp.mean(xv * xv, axis=1, keepdims=True) + EPS)
                o_ref[:, lo:lo + B_DH] = _rope(xv * r * g, cc, s1, s2).astype(BF16)
        o_ref[:, 2 * B_W:3 * B_W] = x_ref[:, 2 * B_W:3 * B_W].astype(BF16)

    tab = pl.BlockSpec((tm, 128), lambda i: (i, 0))
    gain = pl.BlockSpec((1, B_DH), lambda i: (0, 0))
    return pl.pallas_call(
        body, name=name, grid=(t // tm,),
        in_specs=[pl.BlockSpec((tm, wide), lambda i: (i, 0)), tab, tab, tab, gain, gain],
        out_specs=pl.BlockSpec((tm, 3 * B_W), lambda i: (i, 0)),
        out_shape=jax.ShapeDtypeStruct((t, 3 * B_W), BF16),
        compiler_params=_params(("parallel",), 40 * 1024 * 1024),
    )(proj, c, sa, sb, qg, kg)


def _qk_prep_bwd(proj, c, sa, sb, qg, kg, dq, dk, dv, dz, name, tm=256):
    t = proj.shape[0]
    wide = proj.shape[1]
    out_w = 3 * B_W + (B_W if dz is not None else 0)

    def body(*refs):
        x_ref, c_ref, sa_ref, sb_ref, qg_ref, kg_ref, dq_ref, dk_ref, dv_ref = refs[:9]
        if dz is not None:
            dz_ref, o_ref, dgain_ref = refs[9:]
        else:
            o_ref, dgain_ref = refs[9:]
        i = pl.program_id(0)

        @pl.when(i == 0)
        def _():
            dgain_ref[...] = jnp.zeros_like(dgain_ref)

        cc, s1, s2 = c_ref[...], sa_ref[...], sb_ref[...]
        for which, g_ref, d_ref in ((0, qg_ref, dq_ref), (1, kg_ref, dk_ref)):
            g = g_ref[...]
            acc = jnp.zeros((1, B_DH), F32)
            for h in range(B_HEADS):
                lo = which * B_W + h * B_DH
                xv = x_ref[:, lo:lo + B_DH]
                r = lax.rsqrt(jnp.mean(xv * xv, axis=1, keepdims=True) + EPS)
                xh = xv * r
                d_xn = _rope_t(d_ref[:, h * B_DH:(h + 1) * B_DH], cc, s1, s2)
                acc = acc + jnp.sum(d_xn * xh, axis=0, keepdims=True)
                d_xh = d_xn * g
                d_x = r * (d_xh - xh * jnp.mean(d_xh * xh, axis=1, keepdims=True))
                o_ref[:, lo:lo + B_DH] = d_x.astype(BF16)
            dgain_ref[which:which + 1, :] += acc
        o_ref[:, 2 * B_W:3 * B_W] = dv_ref[...].astype(BF16)
        if dz is not None:
            o_ref[:, 3 * B_W:4 * B_W] = dz_ref[...]

    tab = pl.BlockSpec((tm, 128), lambda i: (i, 0))
    gain = pl.BlockSpec((1, B_DH), lambda i: (0, 0))
    grad = pl.BlockSpec((tm, B_W), lambda i: (i, 0))
    in_specs = [pl.BlockSpec((tm, wide), lambda i: (i, 0)), tab, tab, tab, gain, gain, grad, grad, grad]
    args = [proj, c, sa, sb, qg, kg, dq, dk, dv]
    if dz is not None:
        in_specs.append(grad)
        args.append(dz)
    return pl.pallas_call(
        body, name=name, grid=(t // tm,), in_specs=in_specs,
        out_specs=[pl.BlockSpec((tm, out_w), lambda i: (i, 0)), pl.BlockSpec((8, B_DH), lambda i: (0, 0))],
        out_shape=[jax.ShapeDtypeStruct((t, out_w), BF16), jax.ShapeDtypeStruct((8, B_DH), F32)],
        compiler_params=_params(("arbitrary",), 40 * 1024 * 1024),
    )(*args)


def _attn_masks():
    qi = lax.broadcasted_iota(jnp.int32, (B_BLK, 2 * B_BLK), 0)
    kj = lax.broadcasted_iota(jnp.int32, (B_BLK, 2 * B_BLK), 1)
    two = (kj >= qi) & (kj <= qi + B_BLK)
    q1 = lax.broadcasted_iota(jnp.int32, (B_BLK, B_BLK), 0)
    k1 = lax.broadcasted_iota(jnp.int32, (B_BLK, B_BLK), 1)
    return k1 <= q1, two


def _lane_pick(ref_rows, h):
    lane = lax.broadcasted_iota(jnp.int32, ref_rows.shape, 1)
    return jnp.sum(jnp.where(lane == h, ref_rows, 0.0), axis=1, keepdims=True)


def _attn_fwd(qkv, name):
    ns, ln, _ = qkv.shape
    nb = ln // B_BLK
    scale = B_DH ** -0.5

    def body(q_ref, k_ref, v_ref, o_ref, lse_ref):
        h = pl.program_id(1)
        mask1, mask2 = _attn_masks()

        @pl.when(h == 0)
        def _():
            lse_ref[...] = jnp.zeros_like(lse_ref)

        def block(i, first):
            rows = pl.ds(pl.multiple_of(i * B_BLK, B_BLK), B_BLK)
            if first:
                win, mask = pl.ds(0, B_BLK), mask1
            else:
                win, mask = pl.ds(pl.multiple_of((i - 1) * B_BLK, B_BLK), 2 * B_BLK), mask2
            sc = jnp.where(mask, _dot(q_ref[rows, :], k_ref[win, :], 1, 1) * scale, -1e30)
            m = jnp.max(sc, axis=1, keepdims=True)
            p = jnp.exp(sc - m)
            l = jnp.sum(p, axis=1, keepdims=True)
            o_ref[rows, :] = _dot(p, v_ref[win, :], 1, 0) / l
            lane = lax.broadcasted_iota(jnp.int32, (B_BLK, B_HEADS), 1)
            lse_ref[rows, :] = jnp.where(lane == h, m + jnp.log(l), lse_ref[rows, :])

        block(0, True)
        if nb > 1:
            lax.fori_loop(1, nb, lambda i, c: (block(i, False), c)[1], 0)

    return pl.pallas_call(
        body, name=name, grid=(ns, B_HEADS),
        in_specs=[pl.BlockSpec((None, ln, B_DH), lambda s, h: (s, 0, h)),
                  pl.BlockSpec((None, ln, B_DH), lambda s, h: (s, 0, B_HEADS + h)),
                  pl.BlockSpec((None, ln, B_DH), lambda s, h: (s, 0, 2 * B_HEADS + h))],
        out_specs=[pl.BlockSpec((None, ln, B_DH), lambda s, h: (s, 0, h)),
                   pl.BlockSpec((None, ln, B_HEADS), lambda s, h: (s, 0, 0))],
        out_shape=[jax.ShapeDtypeStruct((ns, ln, B_W), F32), jax.ShapeDtypeStruct((ns, ln, B_HEADS), F32)],
        compiler_params=_params(("parallel", "arbitrary")),
    )(qkv, qkv, qkv)


def _attn_bwd(qkv, d_o, lse_joint, delta, name):
    ns, ln, _ = qkv.shape
    nb = ln // B_BLK
    scale = B_DH ** -0.5

    def body(q_ref, k_ref, v_ref, do_ref, lj_ref, dl_ref, dq_ref, dk_ref, dv_ref):
        h = pl.program_id(1)
        mask1, mask2 = _attn_masks()
        dk_ref[...] = jnp.zeros_like(dk_ref)
        dv_ref[...] = jnp.zeros_like(dv_ref)

        def block(i, first):
            rows = pl.ds(pl.multiple_of(i * B_BLK, B_BLK), B_BLK)
            if first:
                win, mask = pl.ds(0, B_BLK), mask1
            else:
                win, mask = pl.ds(pl.multiple_of((i - 1) * B_BLK, B_BLK), 2 * B_BLK), mask2
            q = q_ref[rows, :]
            d_out = do_ref[rows, :]
            l_col = _lane_pick(lj_ref[rows, :], h)
            d_col = _lane_pick(dl_ref[rows, :], h)
            sc = _dot(q, k_ref[win, :], 1, 1) * scale
            p = jnp.exp(jnp.where(mask, sc - l_col, -1e30))
            d_p = _dot(d_out, v_ref[win, :], 1, 1)
            d_s = p * (d_p - d_col) * scale
            dq_ref[rows, :] = _dot(d_s, k_ref[win, :], 1, 0)
            dk_ref[win, :] += _dot(d_s, q, 0, 0)
            dv_ref[win, :] += _dot(p, d_out, 0, 0)

        block(0, True)
        if nb > 1:
            lax.fori_loop(1, nb, lambda i, c: (block(i, False), c)[1], 0)

    head = lambda off: pl.BlockSpec((None, ln, B_DH), lambda s, h: (s, 0, off + h))
    small = pl.BlockSpec((None, ln, B_HEADS), lambda s, h: (s, 0, 0))
    return pl.pallas_call(
        body, name=name, grid=(ns, B_HEADS),
        in_specs=[head(0), head(B_HEADS), head(2 * B_HEADS), head(0), small, small],
        out_specs=[head(0)] * 3,
        out_shape=[jax.ShapeDtypeStruct((ns, ln, B_W), F32)] * 3,
        compiler_params=_params(("parallel", "parallel")),
    )(qkv, qkv, qkv, d_o, lse_joint, delta)


B_ROWS = 2048


def _attn_schedule(nb, sb, block):
    way = 4

    def run(items):
        for at in range(0, len(items), way):
            _round_robin([block(*it) for it in items[at:at + way]])

    run([(si, 0, True) for si in range(sb)])
    if nb == 1:
        return
    per = max(1, way // sb)
    lead = 1 + (nb - 1) % per
    if lead > 1:
        run([(si, i, False) for i in range(1, lead) for si in range(sb)])

    def step(it, carry):
        run([(si, lead + it * per + u, False) for u in range(per) for si in range(sb)])
        return carry

    lax.fori_loop(0, (nb - lead) // per, step, 0)


def _attn_rows(i, first):
    if first:
        return pl.ds(0, B_BLK), pl.ds(0, B_BLK)
    rows = pl.ds(pl.multiple_of(i * B_BLK, B_BLK), B_BLK)
    return rows, pl.ds(pl.multiple_of((i - 1) * B_BLK, B_BLK), 2 * B_BLK)


def _attn_fwd(qkv, name):
    ns, ln, _ = qkv.shape
    nb = ln // B_BLK
    sb = B_ROWS // ln
    scale = B_DH ** -0.5

    def body(q_ref, k_ref, v_ref, o_ref, lse_ref):
        h = pl.program_id(1)
        mask1, mask2 = _attn_masks()
        lane = lax.broadcasted_iota(jnp.int32, (B_BLK, B_HEADS), 1)

        @pl.when(h == 0)
        def _():
            lse_ref[...] = jnp.zeros_like(lse_ref)

        def block(si, i, first):
            rows, win = _attn_rows(i, first)
            mask = mask1 if first else mask2
            sc = jnp.where(mask, _dot(q_ref[si, rows, :], k_ref[si, win, :], 1, 1) * scale, -1e30)
            yield
            m = jnp.max(sc, axis=1, keepdims=True)
            p = jnp.exp(sc - m)
            l = jnp.sum(p, axis=1, keepdims=True)
            pv = _dot(p, v_ref[si, win, :], 1, 0)
            yield
            o_ref[si, rows, :] = pv / l
            lse_ref[si, rows, :] = jnp.where(lane == h, m + jnp.log(l), lse_ref[si, rows, :])

        _attn_schedule(nb, sb, block)

    head = lambda off: pl.BlockSpec((sb, ln, B_DH), lambda s, h: (s, 0, off + h))
    return pl.pallas_call(
        body, name=name, grid=(ns // sb, B_HEADS),
        in_specs=[head(0), head(B_HEADS), head(2 * B_HEADS)],
        out_specs=[head(0), pl.BlockSpec((sb, ln, B_HEADS), lambda s, h: (s, 0, 0))],
        out_shape=[jax.ShapeDtypeStruct((ns, ln, B_W), F32), jax.ShapeDtypeStruct((ns, ln, B_HEADS), F32)],
        compiler_params=_params(("parallel", "arbitrary")),
    )(qkv, qkv, qkv)


def _attn_bwd(qkv, d_o, lse_joint, delta, name):
    ns, ln, _ = qkv.shape
    nb = ln // B_BLK
    sb = B_ROWS // ln
    scale = B_DH ** -0.5

    def body(q_ref, k_ref, v_ref, do_ref, lj_ref, dl_ref, dq_ref, dk_ref, dv_ref):
        h = pl.program_id(1)
        mask1, mask2 = _attn_masks()
        dk_ref[...] = jnp.zeros_like(dk_ref)
        dv_ref[...] = jnp.zeros_like(dv_ref)

        def block(si, i, first):
            rows, win = _attn_rows(i, first)
            mask = mask1 if first else mask2
            q = q_ref[si, rows, :]
            d_out = do_ref[si, rows, :]
            l_col = _lane_pick(lj_ref[si, rows, :], h)
            d_col = _lane_pick(dl_ref[si, rows, :], h)
            sc = _dot(q, k_ref[si, win, :], 1, 1) * scale
            d_p = _dot(d_out, v_ref[si, win, :], 1, 1)
            yield
            p = jnp.exp(jnp.where(mask, sc - l_col, -1e30))
            d_s = p * (d_p - d_col) * scale
            d_q = _dot(d_s, k_ref[si, win, :], 1, 0)
            d_k = _dot(d_s, q, 0, 0)
            d_v = _dot(p, d_out, 0, 0)
            yield
            dq_ref[si, rows, :] = d_q
            dk_ref[si, win, :] += d_k
            dv_ref[si, win, :] += d_v

        _attn_schedule(nb, sb, block)

    head = lambda off: pl.BlockSpec((sb, ln, B_DH), lambda s, h: (s, 0, off + h))
    small = pl.BlockSpec((sb, ln, B_HEADS), lambda s, h: (s, 0, 0))
    return pl.pallas_call(
        body, name=name, grid=(ns // sb, B_HEADS),
        in_specs=[head(0), head(B_HEADS), head(2 * B_HEADS), head(0), small, small],
        out_specs=[head(0)] * 3,
        out_shape=[jax.ShapeDtypeStruct((ns, ln, B_W), F32)] * 3,
        compiler_params=_params(("parallel", "parallel")),
    )(qkv, qkv, qkv, d_o, lse_joint, delta)


def _merge_weights(lse_refs):
    ls = [r[...] for r in lse_refs]
    m = jnp.maximum(jnp.maximum(ls[0], ls[1]), ls[2])
    es = [jnp.exp(l - m) for l in ls]
    tot = es[0] + es[1] + es[2]
    return [e / tot for e in es], m + jnp.log(tot)


def _merge_fwd(outs, lses, proj0, tm=256):
    t = outs[0].shape[0]

    def body(o0, o1, o2, l0, l1, l2, z_ref, og_ref):
        wts, _ = _merge_weights((l0, l1, l2))
        for h in range(B_HEADS):
            cols = slice(h * B_DH, (h + 1) * B_DH)
            o = (wts[0][:, h:h + 1] * o0[:, cols] + wts[1][:, h:h + 1] * o1[:, cols]
                 + wts[2][:, h:h + 1] * o2[:, cols])
            og_ref[:, cols] = (o * _silu(z_ref[:, cols])).astype(BF16)

    wide = pl.BlockSpec((tm, B_W), lambda i: (i, 0))
    small = pl.BlockSpec((tm, B_HEADS), lambda i: (i, 0))
    return pl.pallas_call(
        body, name="merge_fwd", grid=(t // tm,),
        in_specs=[wide] * 3 + [small] * 3 + [pl.BlockSpec((tm, B_W), lambda i: (i, 3))],
        out_specs=wide, out_shape=jax.ShapeDtypeStruct((t, B_W), BF16),
        compiler_params=_params(("parallel",)),
    )(*outs, *lses, proj0)


def _merge_bwd(outs, lses, proj0, d_og, tm=256):
    t = outs[0].shape[0]

    def body(o0, o1, o2, l0, l1, l2, z_ref, dog_ref, do_ref, lj_ref, dl_ref, dz_ref):
        wts, lj = _merge_weights((l0, l1, l2))
        lj_ref[...] = lj
        lane = lax.broadcasted_iota(jnp.int32, (tm, B_HEADS), 1)
        delta = jnp.zeros((tm, B_HEADS), F32)
        for h in range(B_HEADS):
            cols = slice(h * B_DH, (h + 1) * B_DH)
            o = (wts[0][:, h:h + 1] * o0[:, cols] + wts[1][:, h:h + 1] * o1[:, cols]
                 + wts[2][:, h:h + 1] * o2[:, cols])
            z = z_ref[:, cols]
            d_g = dog_ref[:, cols]
            d_out = d_g * _silu(z)
            dz_ref[:, cols] = (d_g * o * _dsilu(z)).astype(BF16)
            do_ref[:, cols] = d_out.astype(BF16)
            delta = jnp.where(lane == h, jnp.sum(d_out * o, axis=1, keepdims=True), delta)
        dl_ref[...] = delta

    wide = pl.BlockSpec((tm, B_W), lambda i: (i, 0))
    small = pl.BlockSpec((tm, B_HEADS), lambda i: (i, 0))
    return pl.pallas_call(
        body, name="merge_bwd", grid=(t // tm,),
        in_specs=[wide] * 3 + [small] * 3 + [pl.BlockSpec((tm, B_W), lambda i: (i, 3)), wide],
        out_specs=[wide, small, small, wide],
        out_shape=[jax.ShapeDtypeStruct((t, B_W), BF16), jax.ShapeDtypeStruct((t, B_HEADS), F32),
                   jax.ShapeDtypeStruct((t, B_HEADS), F32), jax.ShapeDtypeStruct((t, B_W), BF16)],
        compiler_params=_params(("parallel",)),
    )(*outs, *lses, proj0, d_og)


def _adamw(w, g, m, v, name):
    r, c = w.shape
    tr = r
    for cand in (256, 128, 64, 32, 16, 8):
        if r % cand == 0:
            tr = cand
            break

    def body(w_ref, g_ref, m_ref, v_ref, d_ref, nm_ref, nv_ref):
        gv = g_ref[...]
        nm = ADAM_B1 * m_ref[...] + (1.0 - ADAM_B1) * gv
        nv = ADAM_B2 * v_ref[...] + (1.0 - ADAM_B2) * (gv * gv)
        m_hat = nm / (1.0 - ADAM_B1 ** ADAM_STEP)
        v_hat = nv / (1.0 - ADAM_B2 ** ADAM_STEP)
        d_ref[...] = -ADAM_LR * (m_hat / (jnp.sqrt(v_hat) + ADAM_EPS) + ADAM_WD * w_ref[...])
        nm_ref[...] = nm
        nv_ref[...] = nv

    blk = pl.BlockSpec((tr, c), lambda i: (i, 0))
    return pl.pallas_call(
        body, name=name, grid=(r // tr,), in_specs=[blk] * 4, out_specs=[blk] * 3,
        out_shape=[jax.ShapeDtypeStruct((r, c), F32)] * 3,
        compiler_params=_params(("parallel",)),
    )(w, g, m, v)


def _adam_update(w, gv, m, v):
    nm = ADAM_B1 * m + (1.0 - ADAM_B1) * gv
    nv = ADAM_B2 * v + (1.0 - ADAM_B2) * (gv * gv)
    m_hat = nm / (1.0 - ADAM_B1 ** ADAM_STEP)
    v_hat = nv / (1.0 - ADAM_B2 ** ADAM_STEP)
    return -ADAM_LR * (m_hat / (jnp.sqrt(v_hat) + ADAM_EPS) + ADAM_WD * w), nm, nv


def _adamw_shard(w, mine, theirs, m, v, half_index, name, tr=128):
    _, r, c = w.shape
    nhb = (r // 2) // tr

    def body(c_ref, w_ref, mine_ref, theirs_ref, m_ref, v_ref, g_ref, d_ref, nm_ref, nv_ref):
        is_mine = (pl.program_id(0) // nhb) == c_ref[0]
        gv = jnp.where(is_mine, mine_ref[...], theirs_ref[...])
        d, nm, nv = _adam_update(w_ref[...], gv, m_ref[...], v_ref[...])
        g_ref[...] = gv
        d_ref[...] = d
        nm_ref[...] = nm
        nv_ref[...] = nv

    full = pl.BlockSpec((None, tr, c), lambda i, cc: (0, i, 0))
    half = pl.BlockSpec((tr, c), lambda i, cc: (i % nhb, 0))
    return pl.pallas_call(
        body, name=name,
        grid_spec=pltpu.PrefetchScalarGridSpec(
            num_scalar_prefetch=1, grid=(2 * nhb,),
            in_specs=[full, half, half, full, full], out_specs=[full] * 4),
        out_shape=[jax.ShapeDtypeStruct(w.shape, F32)] * 4,
        compiler_params=_params(("parallel",), 40 * 1024 * 1024),
    )(half_index, w, mine, theirs, m, v)


def _pair_sum(own, other, half_index, name, tr=256):
    _, r, c = own.shape
    rh = r // 2
    tr = min(tr, rh)
    nrb = rh // tr

    def body(c_ref, own_ref, oth_ref, out_ref):
        out_ref[...] = (own_ref[...] + oth_ref[...].astype(F32)).astype(BF16)

    return pl.pallas_call(
        body, name=name,
        grid_spec=pltpu.PrefetchScalarGridSpec(
            num_scalar_prefetch=1, grid=(N_CHIPS, nrb),
            in_specs=[pl.BlockSpec((None, tr, c), lambda k, i, cc: (k, cc[0] * nrb + i, 0)),
                      pl.BlockSpec((None, tr, c), lambda k, i, cc: (k, i, 0))],
            out_specs=pl.BlockSpec((None, tr, c), lambda k, i, cc: (k, i, 0))),
        out_shape=jax.ShapeDtypeStruct((N_CHIPS, rh, c), BF16),
        compiler_params=_params(("parallel", "parallel")),
    )(half_index, own, other)


def _chip_sum(sums, others, chip_index, name, tr=256):
    _, r, c = sums.shape
    tr = min(tr, r)

    def body(k_ref, own_ref, oth_ref, out_ref):
        acc = own_ref[...].astype(F32)
        for j in range(N_CHIPS - 1):
            acc = acc + oth_ref[j].astype(F32)
        out_ref[...] = acc

    return pl.pallas_call(
        body, name=name,
        grid_spec=pltpu.PrefetchScalarGridSpec(
            num_scalar_prefetch=1, grid=(r // tr,),
            in_specs=[pl.BlockSpec((None, tr, c), lambda i, kk: (kk[0], i, 0)),
                      pl.BlockSpec((N_CHIPS - 1, tr, c), lambda i, kk: (0, i, 0))],
            out_specs=pl.BlockSpec((tr, c), lambda i, kk: (i, 0))),
        out_shape=jax.ShapeDtypeStruct((r, c), F32),
        compiler_params=_params(("parallel",)),
    )(chip_index, sums, others)


HBM = pl.BlockSpec(memory_space=pltpu.HBM)


def _place():
    x, y, c = lax.axis_index("x"), lax.axis_index("y"), lax.axis_index("c")
    chips = [(1 - x, y), (x, 1 - y), (1 - x, 1 - y)]
    return x, y, c, chips


def _weight_allgather(shards, conv_shard):
    na = len(shards)

    def body(*refs):
        ins = refs[:na]
        conv_in = refs[na]
        outs = refs[na + 1:2 * na + 1]
        conv_out = refs[2 * na + 1]
        send, recv, fsend, frecv, csend, crecv = refs[2 * na + 2:]
        x, y, c, chips = _place()
        me = 2 * x + y
        sib = (x, y, 1 - c)
        first, conv_cp = [], []
        for i in range(na):
            rh = ins[i].shape[0] // 2
            mine = pl.ds(c * rh, rh)
            for j, (px, py) in enumerate(chips):
                cp = pltpu.make_async_remote_copy(
                    src_ref=ins[i].at[mine], dst_ref=outs[i].at[me, mine],
                    send_sem=send.at[3 * i + j], recv_sem=recv.at[3 * i + j],
                    device_id=(px, py, c), device_id_type=MESH)
                cp.start()
                first.append(cp)
        for j, (px, py) in enumerate(chips):
            cp = pltpu.make_async_remote_copy(
                src_ref=conv_in, dst_ref=conv_out.at[me], send_sem=csend.at[j], recv_sem=crecv.at[j],
                device_id=(px, py, c), device_id_type=MESH)
            cp.start()
            conv_cp.append(cp)
        passed = []
        for i in range(na):
            rh = ins[i].shape[0] // 2
            mine = pl.ds(c * rh, rh)
            for j, (px, py) in enumerate(chips):
                slot = outs[i].at[2 * px + py, mine]
                pltpu.make_async_remote_copy(
                    src_ref=slot, dst_ref=slot, send_sem=send.at[3 * i + j], recv_sem=recv.at[3 * i + j],
                    device_id=(px, py, c), device_id_type=MESH).wait_recv()
                cp = pltpu.make_async_remote_copy(
                    src_ref=slot, dst_ref=slot, send_sem=fsend.at[3 * i + j], recv_sem=frecv.at[3 * i + j],
                    device_id=sib, device_id_type=MESH)
                cp.start()
                passed.append(cp)
        for i in range(na):
            rh = ins[i].shape[0] // 2
            theirs = pl.ds((1 - c) * rh, rh)
            for j, (px, py) in enumerate(chips):
                slot = outs[i].at[2 * px + py, theirs]
                pltpu.make_async_remote_copy(
                    src_ref=slot, dst_ref=slot, send_sem=fsend.at[3 * i + j], recv_sem=frecv.at[3 * i + j],
                    device_id=sib, device_id_type=MESH).wait_recv()
        for j, (px, py) in enumerate(chips):
            slot = conv_out.at[2 * px + py]
            pltpu.make_async_remote_copy(
                src_ref=slot, dst_ref=slot, send_sem=csend.at[j], recv_sem=crecv.at[j],
                device_id=(px, py, c), device_id_type=MESH).wait_recv()
        for cp in first + passed + conv_cp:
            cp.wait_send()

    out_shape = [jax.ShapeDtypeStruct((N_CHIPS,) + s.shape, s.dtype) for s in shards]
    out_shape.append(jax.ShapeDtypeStruct((N_CHIPS,) + conv_shard.shape, conv_shard.dtype))
    res = pl.pallas_call(
        body, name="weight_allgather",
        in_specs=[HBM] * (na + 1), out_specs=[HBM] * (na + 1), out_shape=out_shape,
        scratch_shapes=[pltpu.SemaphoreType.DMA((3 * na,)), pltpu.SemaphoreType.DMA((3 * na,)),
                        pltpu.SemaphoreType.DMA((3 * na,)), pltpu.SemaphoreType.DMA((3 * na,)),
                        pltpu.SemaphoreType.DMA((3,)), pltpu.SemaphoreType.DMA((3,))],
    )(*shards, conv_shard)
    my_chip = 2 * lax.axis_index("x") + lax.axis_index("y")
    pick = lambda got, own: [jnp.where(my_chip == k, own, got[k]) for k in range(N_CHIPS)]
    return [pick(g, s) for g, s in zip(res[:na], shards)], pick(res[na], conv_shard)


def _sibling_swap_halves(grads, name):
    na = len(grads)

    def body(*refs):
        ins, outs = refs[:na], refs[na:2 * na]
        send, recv = refs[2 * na:]
        x, y, c, _ = _place()
        sib = (x, y, 1 - c)
        cps = []
        for i in range(na):
            rh = ins[i].shape[1] // 2
            cp = pltpu.make_async_remote_copy(
                src_ref=ins[i].at[:, pl.ds((1 - c) * rh, rh), :], dst_ref=outs[i],
                send_sem=send.at[i], recv_sem=recv.at[i], device_id=sib, device_id_type=MESH)
            cp.start()
            cps.append(cp)
        for cp in cps:
            cp.wait()

    out_shape = [jax.ShapeDtypeStruct((g.shape[0], g.shape[1] // 2, g.shape[2]), g.dtype) for g in grads]
    return pl.pallas_call(
        body, name=name, in_specs=[HBM] * na, out_specs=[HBM] * na, out_shape=out_shape,
        scratch_shapes=[pltpu.SemaphoreType.DMA((na,)), pltpu.SemaphoreType.DMA((na,))],
    )(*grads)


def _chip_exchange(sums):
    na = len(sums)

    def body(*refs):
        ins, outs = refs[:na], refs[na:2 * na]
        send, recv = refs[2 * na:]
        x, y, c, chips = _place()
        cps = []
        for i in range(na):
            for j, (px, py) in enumerate(chips):
                cp = pltpu.make_async_remote_copy(
                    src_ref=ins[i].at[2 * px + py], dst_ref=outs[i].at[j],
                    send_sem=send.at[3 * i + j], recv_sem=recv.at[3 * i + j],
                    device_id=(px, py, c), device_id_type=MESH)
                cp.start()
                cps.append(cp)
        for i in range(na):
            for j, (px, py) in enumerate(chips):
                slot = outs[i].at[j]
                pltpu.make_async_remote_copy(
                    src_ref=slot, dst_ref=slot, send_sem=send.at[3 * i + j], recv_sem=recv.at[3 * i + j],
                    device_id=(px, py, c), device_id_type=MESH).wait_recv()
        for cp in cps:
            cp.wait_send()

    out_shape = [jax.ShapeDtypeStruct((3,) + s.shape[1:], s.dtype) for s in sums]
    return pl.pallas_call(
        body, name="grad_chip_exchange", in_specs=[HBM] * na, out_specs=[HBM] * na, out_shape=out_shape,
        scratch_shapes=[pltpu.SemaphoreType.DMA((3 * na,)), pltpu.SemaphoreType.DMA((3 * na,))],
    )(*sums)


def _sibling_swap_whole(halves):
    na = len(halves)

    def body(*refs):
        ins, outs = refs[:na], refs[na:2 * na]
        send, recv = refs[2 * na:]
        x, y, c, _ = _place()
        cps = []
        for i in range(na):
            cp = pltpu.make_async_remote_copy(
                src_ref=ins[i], dst_ref=outs[i], send_sem=send.at[i], recv_sem=recv.at[i],
                device_id=(x, y, 1 - c), device_id_type=MESH)
            cp.start()
            cps.append(cp)
        for cp in cps:
            cp.wait()

    out_shape = [jax.ShapeDtypeStruct(h.shape, h.dtype) for h in halves]
    return pl.pallas_call(
        body, name="grad_sibling_join", in_specs=[HBM] * na, out_specs=[HBM] * na, out_shape=out_shape,
        scratch_shapes=[pltpu.SemaphoreType.DMA((na,)), pltpu.SemaphoreType.DMA((na,))],
    )(*halves)


SEM = pl.BlockSpec(memory_space=pltpu.SEMAPHORE)
ANY = pl.BlockSpec(memory_space=pl.ANY)
EFFECT = pltpu.SideEffectType.DATAFLOW_SIDE_EFFECTING


def _split_copy_start(name, plan, srcs, lands, after):
    ns, nl = len(srcs), len(lands)

    def body(*refs):
        src_refs, land_refs = refs[:ns], refs[ns:ns + nl]
        send, recv = refs[ns + nl + 1], refs[ns + nl + 2]
        token = refs[-1]
        outgoing, _ = plan(src_refs, land_refs)
        for src, dst, dev, si, ri in outgoing:
            pltpu.make_async_remote_copy(src_ref=src, dst_ref=dst, send_sem=send.at[si], recv_sem=recv.at[ri],
                                         device_id=dev, device_id_type=MESH).start()
        token[...] = jnp.zeros_like(token)

    n_out, n_in = plan.counts
    thru = [pltpu.HBM(a.shape, a.dtype) for a in list(srcs) + list(lands)]
    res = pl.pallas_call(
        body, name=name,
        out_shape=[pltpu.SemaphoreType.DMA((n_out,)), pltpu.SemaphoreType.DMA((n_in,))] + thru
        + [jax.ShapeDtypeStruct((8, 128), F32)],
        in_specs=[HBM] * (ns + nl) + [ANY],
        out_specs=[SEM, SEM] + [HBM] * (ns + nl) + [pl.BlockSpec(memory_space=pltpu.VMEM)],
        input_output_aliases={i: 2 + i for i in range(ns + nl)},
        compiler_params=pltpu.CompilerParams(has_side_effects=EFFECT),
    )(*[pltpu.with_memory_space_constraint(a, pltpu.HBM) for a in list(srcs) + list(lands)], after)
    return res[0], res[1], res[2:2 + ns], res[2 + ns:2 + ns + nl], res[-1]


def _split_copy_wait(name, plan, send, recv, srcs, lands, after):
    ns, nl = len(srcs), len(lands)

    def body(*refs):
        src_refs, land_refs = refs[:ns], refs[ns:ns + nl]
        send_ref, recv_ref = refs[ns + nl], refs[ns + nl + 1]
        outgoing, arrivals = plan(src_refs, land_refs)
        for src, dst, dev, si, ri in outgoing:
            pltpu.make_async_remote_copy(src_ref=src, dst_ref=dst, send_sem=send_ref.at[si], recv_sem=recv_ref.at[ri],
                                         device_id=dev, device_id_type=MESH).wait_send()
        for view, ri in arrivals:
            pltpu.make_async_remote_copy(src_ref=view, dst_ref=view, send_sem=send_ref.at[0], recv_sem=recv_ref.at[ri],
                                         device_id=_place()[:3], device_id_type=MESH).wait_recv()

    thru = [pltpu.HBM(a.shape, a.dtype) for a in list(srcs) + list(lands)]
    res = pl.pallas_call(
        body, name=name, out_shape=thru,
        in_specs=[HBM] * (ns + nl) + [SEM, SEM, ANY], out_specs=[HBM] * (ns + nl),
        input_output_aliases={i: i for i in range(ns + nl)},
        compiler_params=pltpu.CompilerParams(has_side_effects=EFFECT),
    )(*srcs, *lands, send, recv, after)
    return res[:ns], res[ns:]


def _gather_plan(n_arrays):
    def plan(src_refs, land_refs):
        x, y, c, chips = _place()
        me = 2 * x + y
        outgoing, arrivals = [], []
        for i in range(n_arrays):
            rh = src_refs[i].shape[0] // 2
            mine = pl.ds(c * rh, rh)
            for j, (px, py) in enumerate(chips):
                for delta in range(2):
                    tc = c ^ delta
                    outgoing.append((src_refs[i].at[mine], land_refs[i].at[me, mine], (px, py, tc),
                                     6 * i + 2 * j + delta, 6 * i + 2 * j + delta))
                    theirs = pl.ds(tc * rh, rh)
                    arrivals.append((land_refs[i].at[2 * px + py, theirs], 6 * i + 2 * j + delta))
        return outgoing, arrivals

    plan.counts = (6 * n_arrays, 6 * n_arrays)
    return plan


def _exchange_plan(n_arrays):
    def plan(src_refs, land_refs):
        x, y, c, chips = _place()
        outgoing, arrivals = [], []
        for i in range(n_arrays):
            for j, (px, py) in enumerate(chips):
                outgoing.append((src_refs[i].at[2 * px + py], land_refs[i].at[j], (px, py, c), 3 * i + j, 3 * i + j))
                arrivals.append((land_refs[i].at[j], 3 * i + j))
        return outgoing, arrivals

    plan.counts = (3 * n_arrays, 3 * n_arrays)
    return plan


def _small_allreduce(vec):
    r, cdim = vec.shape
    n_dev = 8

    def body(v_ref, out_ref, buf, send, recv):
        x, y, c, _ = _place()
        me = 4 * x + 2 * y + c
        buf[me] = v_ref[...]
        cps = []
        for k in range(1, n_dev):
            dx, dy, dc = (k >> 2) & 1, (k >> 1) & 1, k & 1
            peer = (x ^ dx, y ^ dy, c ^ dc)
            cp = pltpu.make_async_remote_copy(
                src_ref=v_ref, dst_ref=buf.at[me], send_sem=send.at[k - 1], recv_sem=recv.at[k - 1],
                device_id=peer, device_id_type=MESH)
            cp.start()
            cps.append(cp)
        for k in range(1, n_dev):
            dx, dy, dc = (k >> 2) & 1, (k >> 1) & 1, k & 1
            src = 4 * (x ^ dx) + 2 * (y ^ dy) + (c ^ dc)
            slot = buf.at[src]
            pltpu.make_async_remote_copy(
                src_ref=slot, dst_ref=slot, send_sem=send.at[k - 1], recv_sem=recv.at[k - 1],
                device_id=(x ^ dx, y ^ dy, c ^ dc), device_id_type=MESH).wait_recv()
        for cp in cps:
            cp.wait_send()
        acc = buf[0]
        for k in range(1, n_dev):
            acc = acc + buf[k]
        out_ref[...] = acc

    vm = pl.BlockSpec(memory_space=pltpu.VMEM)
    return pl.pallas_call(
        body, name="small_allreduce", in_specs=[vm], out_specs=vm,
        out_shape=jax.ShapeDtypeStruct((r, cdim), F32),
        scratch_shapes=[pltpu.VMEM((n_dev, r, cdim), F32), pltpu.SemaphoreType.DMA((n_dev - 1,)),
                        pltpu.SemaphoreType.DMA((n_dev - 1,))],
    )(vec)


def _a_cols_to_head_major(w):
    lead = w.shape[:-1]
    q = w[..., :A_QK].reshape(lead + (A_HEADS, A_DK))
    k = w[..., A_QK:2 * A_QK].reshape(lead + (A_HEADS, A_DK))
    v = w[..., 2 * A_QK:2 * A_QK + A_VW].reshape(lead + (A_HEADS, A_DV))
    z = w[..., 2 * A_QK + A_VW:].reshape(lead + (A_HEADS, A_DV))
    return jnp.concatenate([q, k, v, z], axis=-1).reshape(lead + (A_HEADS * A_HEAD_COLS,))


def _a_cols_from_head_major(w):
    lead = w.shape[:-1]
    w = w.reshape(lead + (A_HEADS, A_HEAD_COLS))
    parts = [w[..., :A_DK], w[..., A_DK:2 * A_DK], w[..., 2 * A_DK:2 * A_DK + A_DV], w[..., 2 * A_DK + A_DV:]]
    return jnp.concatenate([p.reshape(lead + (-1,)) for p in parts], axis=-1)


def _conv_cols_to_head_major(w):
    lead = w.shape[:-1]
    q = w[..., :A_QK].reshape(lead + (A_HEADS, A_DK))
    k = w[..., A_QK:2 * A_QK].reshape(lead + (A_HEADS, A_DK))
    v = w[..., 2 * A_QK:].reshape(lead + (A_HEADS, A_DV))
    return jnp.concatenate([q, k, v], axis=-1).reshape(lead + (A_HEADS * A_CONV_COLS,))


def _conv_cols_from_head_major(w):
    lead = w.shape[:-1]
    w = w.reshape(lead + (A_HEADS, A_CONV_COLS))
    parts = [w[..., :A_DK], w[..., A_DK:2 * A_DK], w[..., 2 * A_DK:]]
    return jnp.concatenate([p.reshape(lead + (-1,)) for p in parts], axis=-1)


def _to_stream(a, bn, d):
    rest = a.shape[1:]
    s = a.shape[0] // bn
    a = a.reshape((bn, s // d, d) + rest)
    a = jnp.swapaxes(a, 1, 2)
    return a.reshape((bn * d, s // d) + rest)


def _from_stream(a, bn, d):
    rest = a.shape[2:]
    ln = a.shape[1]
    a = a.reshape((bn, d, ln) + rest)
    a = jnp.swapaxes(a, 1, 2)
    return a.reshape((bn * ln * d,) + rest)


B_SUB = 512
B_SHARD_BLOCKS = (3 * B_GROUPS * B_W + B_W) // N_CHIPS // B_SUB


def _b_block(gi, jj):
    nb = (B_GROUPS * (jj // 2) + gi) * 2 + jj % 2
    return nb // B_SHARD_BLOCKS, nb % B_SHARD_BLOCKS


def _shard_major(g, ncols):
    r = g.shape[0]
    return jnp.swapaxes(g.reshape(r, N_CHIPS, ncols), 0, 1)


def _pack_rows(items):
    rows, offs = [], []
    at = 0
    for a in items:
        flat = a.reshape(-1).astype(F32)
        nr = -(-flat.shape[0] // 1024) * 8
        flat = jnp.pad(flat, (0, nr * 128 - flat.shape[0]))
        rows.append(flat.reshape(nr, 128))
        offs.append((at, nr, a.shape))
        at += nr
    return jnp.concatenate(rows, axis=0), offs


def _unpack_rows(packed, offs):
    out = []
    for at, nr, shape in offs:
        size = int(np.prod(shape)) if len(shape) else 1
        out.append(packed[at:at + nr].reshape(-1)[:size].reshape(shape))
    return out


def _local_step(x, positions, loss_target, norm_g, wa_in, conv_w, a_log, a_dt_bias, a_norm_g,
                b_q_norm_g, b_k_norm_g, start_token, late_weights, b_grads_ready):
    bn, s, d = x.shape
    t = bn * s
    n_chunks = s // A_CHUNK
    wa_main = _a_cols_to_head_major(wa_in[:, :A_MAIN])
    wa_tail = jnp.pad(wa_in[:, A_MAIN:], ((0, 0), (0, 128 - 2 * A_HEADS)))
    cw_hm = _conv_cols_to_head_major(conv_w)

    x0 = x.reshape(t, d)
    h0 = _rms_fwd(x0, norm_g[0:1] + start_token, "rms0_fwd")
    proj_a = _matmul(h0, wa_main, "nn", F32, "a_in_main")
    tail_a = _matmul(h0, wa_tail, "nn", F32, "a_in_tail")
    tail_t = jnp.swapaxes(tail_a[:, :2 * A_HEADS].reshape(bn, s, 2 * A_HEADS), 1, 2)
    tail_t = tail_t.reshape(bn, 2 * A_HEADS, n_chunks, A_CHUNK)
    beta, gc = _gdn_prep(tail_t, a_log[0], a_dt_bias[0])
    proj_a3 = proj_a.reshape(bn, s, A_MAIN)
    og_a, oraw_a, states, t_mats = _gdn_fwd(proj_a3, cw_hm, beta, gc, a_norm_g)
    wa_out, wb_in, wb_out = late_weights(og_a)
    b_cols = [4 * B_W] + [3 * B_W] * (B_GROUPS - 1)
    x1 = _matmul(og_a.reshape(t, A_VW), wa_out, "nn", F32, "a_out", res=x0, tk=2048)

    h1 = _rms_fwd(x1, norm_g[1:2], "rms1_fwd")
    inv_freq = ROPE_THETA ** (-jnp.arange(0, ROPE_DIMS, 2, dtype=F32) / ROPE_DIMS)
    freq_row = jnp.concatenate([inv_freq, inv_freq, jnp.zeros((128 - ROPE_DIMS,), F32)]).reshape(1, 128)
    posf = jnp.broadcast_to(positions.astype(F32).reshape(t, 1), (t, 128))
    tabs = _rope_tables(posf, freq_row)
    h1_s, tabs_s, proj_b, qkv_b, o_b, lse_b = [], [], [], [], [], []
    for gi, dil in enumerate(B_DIL):
        hs = h1 if dil == 1 else _to_stream(h1, bn, dil).reshape(t, d)
        ts = tabs if dil == 1 else [_to_stream(tb, bn, dil).reshape(t, 128) for tb in tabs]
        pj = _matmul(hs, wb_in, "nn", F32, f"b_in_g{gi}", tn=B_SUB, n=b_cols[gi], b_spec=pl.BlockSpec(
            (None, d, B_SUB), lambda i, j, kk, gi=gi: (_b_block(gi, j)[0], kk, _b_block(gi, j)[1])))
        qkv = _qk_prep(pj, *ts, b_q_norm_g[0, gi:gi + 1], b_k_norm_g[0, gi:gi + 1], f"qk_prep_g{gi}")
        o_s, lse_s = _attn_fwd(qkv.reshape(bn * dil, s // dil, 3 * B_W), f"attn_fwd_g{gi}")
        h1_s.append(hs), tabs_s.append(ts), proj_b.append(pj), qkv_b.append(qkv)
        o_b.append(o_s.reshape(t, B_W) if dil == 1 else _from_stream(o_s, bn, dil))
        lse_b.append(lse_s.reshape(t, B_HEADS) if dil == 1 else _from_stream(lse_s, bn, dil))
    og_b = _merge_fwd(o_b, lse_b, proj_b[0])
    x2 = _matmul(og_b, wb_out, "nn", F32, "b_out", res=x1)

    d_x2, loss_parts = _loss_grad(x2, loss_target.reshape(t, d))
    loss_local = jnp.sum(loss_parts)

    d_x2b = d_x2.astype(BF16)
    g_wb_out = _matmul(og_b.T, d_x2b, "nn", F32, "b_out_dw")
    d_og_b = _matmul(d_x2b, wb_out, "nt", F32, "b_out_dx")
    d_o, lse_joint, delta, d_z = _merge_bwd(o_b, lse_b, proj_b[0], d_og_b)
    d_h1, g_qn, g_kn = [], [], []
    g_wb_in = lax.empty(wb_in.shape, F32)
    for gi, dil in enumerate(B_DIL):
        if dil == 1:
            do_s, lj_s, dl_s = d_o, lse_joint, delta
        else:
            do_s, lj_s, dl_s = (_to_stream(a, bn, dil).reshape(t, -1) for a in (d_o, lse_joint, delta))
        ns, ln = bn * dil, s // dil
        dq, dk, dv = _attn_bwd(qkv_b[gi].reshape(ns, ln, 3 * B_W), do_s.reshape(ns, ln, B_W),
                               lj_s.reshape(ns, ln, B_HEADS), dl_s.reshape(ns, ln, B_HEADS), f"attn_bwd_g{gi}")
        d_pj, d_gain = _qk_prep_bwd(proj_b[gi], *tabs_s[gi], b_q_norm_g[0, gi:gi + 1], b_k_norm_g[0, gi:gi + 1],
                                    dq.reshape(t, B_W), dk.reshape(t, B_W), dv.reshape(t, B_W),
                                    d_z if gi == 0 else None, f"qk_prep_bwd_g{gi}")
        g_wb_in = _matmul(h1_s[gi].T, d_pj, "nn", F32, f"b_in_dw_g{gi}", tn=B_SUB, into=(g_wb_in, pl.BlockSpec(
            (None, d, B_SUB), lambda i, j, kk, gi=gi: (_b_block(gi, j)[0], i, _b_block(gi, j)[1]))))
        dh = _matmul(d_pj, wb_in, "nt", F32, f"b_in_dx_g{gi}", tk=B_SUB, n=d, b_spec=pl.BlockSpec(
            (None, d, B_SUB), lambda i, j, kk, gi=gi: (_b_block(gi, kk)[0], j, _b_block(gi, kk)[1])))
        d_h1.append(dh if dil == 1 else _from_stream(dh.reshape(ns, ln, d), bn, dil))
        g_qn.append(d_gain[0]), g_kn.append(d_gain[1])
    d_x1, g_norm1 = _rms_bwd(x1, norm_g[1:2], d_h1, d_x2, "rms1_bwd")

    d_x1b = (d_x1 + b_grads_ready(g_wb_in, g_wb_out)).astype(BF16)
    g_wa_out = _matmul(og_a.reshape(t, A_VW).T, d_x1b, "nn", F32, "a_out_dw")
    d_og_a = _matmul(d_x1b, wa_out, "nt", F32, "a_out_dx")
    d_pa, d_gc, d_beta, d_cw, d_ng = _gdn_bwd(proj_a3, cw_hm, beta, gc, a_norm_g, oraw_a, states, t_mats,
                                              d_og_a.reshape(bn, s, A_VW))
    d_tail_t, d_alog, d_dtb = _gdn_prep_bwd(tail_t, a_log[0], a_dt_bias[0], d_gc, d_beta)
    d_tail = jnp.swapaxes(d_tail_t.reshape(bn, 2 * A_HEADS, s), 1, 2).reshape(t, 2 * A_HEADS)
    d_tail = jnp.pad(d_tail, ((0, 0), (0, 128 - 2 * A_HEADS))).astype(BF16)
    d_pa = d_pa.reshape(t, A_MAIN)
    h0_t = h0.T
    g_wa_main = _matmul(h0_t, d_pa, "nn", F32, "a_in_dw_main")
    g_wa_tail = _matmul(h0_t, d_tail, "nn", F32, "a_in_dw_tail")
    d_h0 = _matmul(d_pa, wa_main, "nt", F32, "a_in_dx_main")
    d_h0t = _matmul(d_tail, wa_tail, "nt", F32, "a_in_dx_tail")
    d_x0, g_norm0 = _rms_bwd(x0, norm_g[0:1], [d_h0, d_h0t], d_x1, "rms0_bwd")
    g_wa_in = jnp.concatenate([_a_cols_from_head_major(g_wa_main), g_wa_tail[:, :2 * A_HEADS]], axis=1)

    gfull = {
        "norm_g": jnp.concatenate([g_norm0, g_norm1], axis=0), "a_w_in": g_wa_in,
        "a_conv_w": _conv_cols_from_head_major(jnp.sum(d_cw, axis=0)),
        "a_log": jnp.sum(d_alog[:, :, 0], axis=0), "a_dt_bias": jnp.sum(d_dtb[:, :, 0], axis=0),
        "a_norm_g": jnp.sum(d_ng[:, :, 0, :], axis=(0, 1)), "a_w_out": g_wa_out, "b_w_in": g_wb_in,
        "b_q_norm_g": jnp.stack(g_qn), "b_k_norm_g": jnp.stack(g_kn), "b_w_out": g_wb_out}
    return loss_local, d_x0.reshape(bn, s, d), gfull


def kernel(x, positions, norm_g, a_w_in, a_conv_w, a_log, a_dt_bias, a_norm_g, a_w_out, b_w_in, b_q_norm_g, b_k_norm_g, b_w_out, loss_target, m_norm_g, m_a_w_in, m_a_conv_w, m_a_log, m_a_dt_bias, m_a_norm_g, m_a_w_out, m_b_w_in, m_b_q_norm_g, m_b_k_norm_g, m_b_w_out, v_norm_g, v_a_w_in, v_a_conv_w, v_a_log, v_a_dt_bias, v_a_norm_g, v_a_w_out, v_b_w_in, v_b_q_norm_g, v_b_k_norm_g, v_b_w_out):
    d = x.shape[2]
    my_c = lax.axis_index("c")
    my_chip = 2 * lax.axis_index("x") + lax.axis_index("y")

    half_index = jnp.reshape(my_c, (1,)).astype(jnp.int32)
    chip_index = jnp.reshape(my_chip, (1,)).astype(jnp.int32)
    (ga_in,), g_conv = _weight_allgather([a_w_in[0].astype(BF16)], a_conv_w[0])
    wa_in = jnp.concatenate(ga_in, axis=1)
    conv_w = jnp.concatenate(g_conv, axis=1)

    late_shards = [a_w_out[0].astype(BF16), b_w_in[0].astype(BF16), b_w_out[0].astype(BF16)]
    late_lands = [lax.dynamic_update_slice(lax.empty((N_CHIPS,) + s.shape, BF16), s[None], (my_chip, 0, 0))
                  for s in late_shards]
    gather = _gather_plan(len(late_shards))
    ag_send, ag_recv, ag_srcs, ag_lands, ag_token = _split_copy_start(
        "late_weights_start", gather, late_shards, late_lands, conv_w)

    def late_weights(after):
        _, (ga_out, gb_in, gb_out) = _split_copy_wait(
            "late_weights_wait", gather, ag_send, ag_recv, ag_srcs, ag_lands, after)
        return ga_out.reshape(A_VW, d), gb_in, gb_out.reshape(B_W, d)

    def reduce_to_chip_sums(mats, tag):
        recv_sib = _sibling_swap_halves([g.astype(BF16) for g in mats], f"grad_{tag}_sibling_swap")
        return [_pair_sum(g, r, half_index, f"grad_{tag}_pair_sum_{i}") for i, (g, r) in enumerate(zip(mats, recv_sib))]

    exchange = _exchange_plan(2)
    pending = {}

    def b_grads_ready(g_wb_in, g_wb_out):
        sums = reduce_to_chip_sums([g_wb_in, g_wb_out.reshape(N_CHIPS, -1, d)], "b")
        lands = [lax.empty((N_CHIPS - 1,) + s.shape[1:], BF16) for s in sums]
        pending["b"] = _split_copy_start("grad_b_exchange_start", exchange, sums, lands, chip_index)
        return pending["b"][4][0, 0]

    loss_local, d_x0, gfull = _local_step(x, positions, loss_target, norm_g, wa_in, conv_w, a_log, a_dt_bias,
                                          a_norm_g, b_q_norm_g, b_k_norm_g, ag_token[0, 0], late_weights, b_grads_ready)

    b_send, b_recv, b_srcs, b_lands, _ = pending["b"]
    b_sums, b_received = _split_copy_wait("grad_b_exchange_wait", exchange, b_send, b_recv, b_srcs, b_lands, d_x0)
    a_sums = reduce_to_chip_sums([_shard_major(gfull["a_w_in"], a_w_in.shape[2]),
                                  gfull["a_w_out"].reshape(N_CHIPS, -1, d)], "a")
    a_received = _chip_exchange(a_sums)
    chip_sums = [a_sums[0], a_sums[1], b_sums[0], b_sums[1]]
    received = [a_received[0], a_received[1], b_received[0], b_received[1]]
    halves = [_chip_sum(s, r, chip_index, f"grad_chip_sum_{i}") for i, (s, r) in enumerate(zip(chip_sums, received))]
    theirs = _sibling_swap_whole(halves)
    big = ("a_w_in", "a_w_out", "b_w_in", "b_w_out")
    big_halves = dict(zip(big, zip(halves, theirs)))

    small = [gfull["norm_g"], gfull["a_conv_w"], gfull["a_log"], gfull["a_dt_bias"], gfull["a_norm_g"],
             gfull["b_q_norm_g"], gfull["b_k_norm_g"], loss_local]
    packed, offs = _pack_rows(small)
    red = _unpack_rows(_small_allreduce(packed), offs)
    g_norm, g_conv_all, g_alog, g_dtb, g_ang, g_q, g_k, loss = red
    g_conv_mine = lax.dynamic_slice_in_dim(g_conv_all, my_chip * a_conv_w.shape[2], a_conv_w.shape[2], axis=1)

    grads = {
        "norm_g": g_norm, "a_conv_w": g_conv_mine[None], "a_log": g_alog[None], "a_dt_bias": g_dtb[None],
        "a_norm_g": g_ang[None], "b_q_norm_g": g_q[None], "b_k_norm_g": g_k[None]}
    weights = {"norm_g": norm_g, "a_w_in": a_w_in, "a_conv_w": a_conv_w, "a_log": a_log, "a_dt_bias": a_dt_bias,
               "a_norm_g": a_norm_g, "a_w_out": a_w_out, "b_w_in": b_w_in, "b_q_norm_g": b_q_norm_g,
               "b_k_norm_g": b_k_norm_g, "b_w_out": b_w_out}
    m_in = {"norm_g": m_norm_g, "a_w_in": m_a_w_in, "a_conv_w": m_a_conv_w, "a_log": m_a_log,
            "a_dt_bias": m_a_dt_bias, "a_norm_g": m_a_norm_g, "a_w_out": m_a_w_out, "b_w_in": m_b_w_in,
            "b_q_norm_g": m_b_q_norm_g, "b_k_norm_g": m_b_k_norm_g, "b_w_out": m_b_w_out}
    v_in = {"norm_g": v_norm_g, "a_w_in": v_a_w_in, "a_conv_w": v_a_conv_w, "a_log": v_a_log,
            "a_dt_bias": v_a_dt_bias, "a_norm_g": v_a_norm_g, "a_w_out": v_a_w_out, "b_w_in": v_b_w_in,
            "b_q_norm_g": v_b_q_norm_g, "b_k_norm_g": v_b_k_norm_g, "b_w_out": v_b_w_out}
    names = list(weights)

    delta_w, new_m, new_v = {}, {}, {}
    for nm in big:
        mine, other = big_halves[nm]
        grads[nm], delta_w[nm], new_m[nm], new_v[nm] = _adamw_shard(
            weights[nm], mine, other, m_in[nm], v_in[nm], half_index, f"adamw_{nm}")
    small_names = [nm for nm in names if nm not in big]
    packs = [_pack_rows([src[nm] for nm in small_names]) for src in (weights, grads, m_in, v_in)]
    offs = packs[0][1]
    dl, m2, v2 = _adamw(packs[0][0], packs[1][0], packs[2][0], packs[3][0], "adamw_small")
    for nm, a, b, c2 in zip(small_names, _unpack_rows(dl, offs), _unpack_rows(m2, offs), _unpack_rows(v2, offs)):
        delta_w[nm], new_m[nm], new_v[nm] = a, b, c2

    return (loss, d_x0, *[grads[nm] for nm in names], *[delta_w[nm] for nm in names],
            *[new_m[nm] for nm in names], *[new_v[nm] for nm in names])
```

```python
import functools
import math

import jax
import jax.numpy as jnp
import numpy as np
from jax import lax
from jax.experimental import pallas as pl
from jax.experimental.pallas import tpu as pltpu

F32 = jnp.float32
BF16 = jnp.bfloat16
MESH = pl.DeviceIdType.MESH

EPS = 1e-6
D_MODEL = 1024
A_HEADS = 8
A_DK = 128
A_DV = 256
A_QK = A_HEADS * A_DK
A_VW = A_HEADS * A_DV
A_MAIN = 2 * A_QK + 2 * A_VW
A_HEAD_COLS = 2 * A_DK + 2 * A_DV
A_CONV_COLS = 2 * A_DK + A_DV
A_CHUNK = 64
A_CONV = 4
B_GROUPS = 3
B_HEADS = 8
B_DH = 128
B_W = B_HEADS * B_DH
B_DIL = (1, 4, 16)
B_BLK = 128
ROPE_THETA = 500000.0
ROPE_DIMS = B_DH // 4
ADAM_LR, ADAM_B1, ADAM_B2, ADAM_EPS, ADAM_WD, ADAM_STEP = 0.001, 0.9, 0.999, 1e-08, 0.01, 10
N_CHIPS = 4
VMEM_BIG = 56 * 1024 * 1024


def _params(sem=None, vmem=None):
    return pltpu.CompilerParams(dimension_semantics=sem, vmem_limit_bytes=vmem)


def _dot(a, b, ca, cb):
    return lax.dot_general(a.astype(BF16), b.astype(BF16), (((ca,), (cb,)), ((), ())),
                           preferred_element_type=F32)


def _split3(a):
    hi = a.astype(BF16)
    r = a - hi.astype(F32)
    mid = r.astype(BF16)
    lo = (r - mid.astype(F32)).astype(BF16)
    return hi, mid, lo


def _dot_hi(a, b, ca, cb):
    a_hi, a_lo, _ = _split3(a)
    b_hi, b_lo, _ = _split3(b)
    dn = (((ca,), (cb,)), ((), ()))
    out = lax.dot_general(a_hi, b_hi, dn, preferred_element_type=F32)
    out = out + lax.dot_general(a_hi, b_lo, dn, preferred_element_type=F32)
    return out + lax.dot_general(a_lo, b_hi, dn, preferred_element_type=F32)


def _sigmoid(y):
    return 1.0 / (1.0 + jnp.exp(-y))


def _silu(y):
    return y * _sigmoid(y)


def _dsilu(y):
    s = _sigmoid(y)
    return s * (1.0 + y * (1.0 - s))


def _matmul(a, b, mode, out_dtype, name, res=None, tm=1024, tn=1024, tk=1024, n=None, b_spec=None, into=None):
    m, k = a.shape
    if n is None:
        n = b.shape[1] if mode == "nn" else b.shape[0]
    tm, tn, tk = min(tm, m), min(tn, n), min(tk, k)
    assert m % tm == 0 and n % tn == 0 and k % tk == 0, (name, a.shape, b.shape)
    nk = k // tk
    dims = {"nn": ((1,), (0,)), "nt": ((1,), (1,))}[mode]

    def body(*refs):
        a_ref, b_ref = refs[0], refs[1]
        r_ref = refs[2] if res is not None else None
        o_ref = refs[2 + (res is not None) + (into is not None)]
        prod = lax.dot_general(a_ref[...], b_ref[...], (dims, ((), ())), preferred_element_type=F32)

        def finish(r):
            if res is not None:
                r = r + r_ref[...]
            o_ref[...] = r.astype(out_dtype)

        if nk == 1:
            finish(prod)
            return
        acc = refs[-1]
        kk = pl.program_id(2)

        @pl.when(kk == 0)
        def _():
            acc[...] = prod

        @pl.when((kk > 0) & (kk < nk - 1))
        def _():
            acc[...] += prod

        @pl.when(kk == nk - 1)
        def _():
            finish(acc[...] + prod)

    a_spec = pl.BlockSpec((tm, tk), lambda i, j, kk: (i, kk))
    if b_spec is None and mode == "nt":
        b_spec = pl.BlockSpec((tn, tk), lambda i, j, kk: (j, kk))
    elif b_spec is None:
        b_spec = pl.BlockSpec((tk, tn), lambda i, j, kk: (kk, j))
    in_specs = [a_spec, b_spec]
    args = [a, b]
    if res is not None:
        in_specs.append(pl.BlockSpec((tm, tn), lambda i, j, kk: (i, j)))
        args.append(res)
    out_spec = pl.BlockSpec((tm, tn), lambda i, j, kk: (i, j))
    out_shape = jax.ShapeDtypeStruct((m, n), out_dtype)
    aliases = {}
    if into is not None:
        assert res is None
        buf, out_spec = into
        out_shape = jax.ShapeDtypeStruct(buf.shape, buf.dtype)
        in_specs.append(ANY)
        args.append(buf)
        aliases = {2: 0}
    return pl.pallas_call(
        body, name=name, grid=(m // tm, n // tn, nk),
        in_specs=in_specs, out_specs=out_spec, out_shape=out_shape, input_output_aliases=aliases,
        scratch_shapes=[pltpu.VMEM((tm, tn), F32)] if nk > 1 else [],
        compiler_params=_params(("parallel", "parallel", "arbitrary"), 48 * 1024 * 1024),
    )(*args)


def _rms_fwd(x, g, name, tm=256):
    t, d = x.shape

    def body(x_ref, g_ref, h_ref):
        xv = x_ref[...]
        r = lax.rsqrt(jnp.mean(xv * xv, axis=-1, keepdims=True) + EPS)
        h_ref[...] = (xv * r * g_ref[...]).astype(BF16)

    return pl.pallas_call(
        body, name=name, grid=(t // tm,),
        in_specs=[pl.BlockSpec((tm, d), lambda i: (i, 0)), pl.BlockSpec((1, d), lambda i: (0, 0))],
        out_specs=pl.BlockSpec((tm, d), lambda i: (i, 0)),
        out_shape=jax.ShapeDtypeStruct((t, d), BF16),
        compiler_params=_params(("parallel",)),
    )(x, g)


def _rms_bwd(x, g, dhs, dres, name, tm=256):
    t, d = x.shape
    n_dh = len(dhs)

    def body(*refs):
        x_ref, g_ref = refs[0], refs[1]
        dh_refs = refs[2:2 + n_dh]
        dres_ref, dx_ref, dg_ref = refs[2 + n_dh:]
        i = pl.program_id(0)

        @pl.when(i == 0)
        def _():
            dg_ref[...] = jnp.zeros_like(dg_ref)

        xv = x_ref[...]
        r = lax.rsqrt(jnp.mean(xv * xv, axis=-1, keepdims=True) + EPS)
        xh = xv * r
        dh = dh_refs[0][...]
        for ref in dh_refs[1:]:
            dh = dh + ref[...]
        dg_ref[0:1, :] += jnp.sum(dh * xh, axis=0, keepdims=True)
        dxh = dh * g_ref[...]
        dx = r * (dxh - xh * jnp.mean(dxh * xh, axis=-1, keepdims=True))
        dx_ref[...] = dx + dres_ref[...]

    row = pl.BlockSpec((tm, d), lambda i: (i, 0))
    dx, dg = pl.pallas_call(
        body, name=name, grid=(t // tm,),
        in_specs=[row, pl.BlockSpec((1, d), lambda i: (0, 0))] + [row] * n_dh + [row],
        out_specs=[row, pl.BlockSpec((8, d), lambda i: (0, 0))],
        out_shape=[jax.ShapeDtypeStruct((t, d), F32), jax.ShapeDtypeStruct((8, d), F32)],
        compiler_params=_params(("arbitrary",)),
    )(x, g, *dhs, dres)
    return dx, dg[0:1]


def _loss_grad(y, target, name="loss_grad", tm=256):
    t, d = y.shape
    nb = t // tm

    def body(y_ref, t_ref, dy_ref, part_ref):
        e = y_ref[...] - t_ref[...]
        dy_ref[...] = e * (1.0 / d)
        s = jnp.sum(jnp.sum(e * e, axis=1, keepdims=True), axis=0, keepdims=True) * (0.5 / d)
        part_ref[...] = jnp.broadcast_to(s, (8, 128))

    row = pl.BlockSpec((tm, d), lambda i: (i, 0))
    dy, part = pl.pallas_call(
        body, name=name, grid=(nb,), in_specs=[row, row],
        out_specs=[row, pl.BlockSpec((None, 8, 128), lambda i: (i, 0, 0))],
        out_shape=[jax.ShapeDtypeStruct((t, d), F32), jax.ShapeDtypeStruct((nb, 8, 128), F32)],
        compiler_params=_params(("parallel",)),
    )(y, target)
    return dy, part[:, 0, 0]


def _softplus(x):
    t = jnp.exp(-jnp.abs(x))
    return jnp.maximum(x, 0.0) + jnp.where(t < 1e-3, t * (1.0 - 0.5 * t), jnp.log(1.0 + t))


def _tri(rows_le_cols):
    r = lax.broadcasted_iota(jnp.int32, (A_CHUNK, A_CHUNK), 0)
    c = lax.broadcasted_iota(jnp.int32, (A_CHUNK, A_CHUNK), 1)
    return jnp.where((r <= c) if rows_le_cols else (r >= c), 1.0, 0.0).astype(BF16)


def _dot_exact_rhs(a, ones_bf16):
    dn = (((1,), (0,)), ((), ()))
    hi, mid, lo = _split3(a)
    out = lax.dot_general(hi, ones_bf16, dn, preferred_element_type=F32)
    out = out + lax.dot_general(mid, ones_bf16, dn, preferred_element_type=F32)
    return out + lax.dot_general(lo, ones_bf16, dn, preferred_element_type=F32)


def _gdn_prep(tail_t, a_log, dt_bias):
    bn, _, n, c = tail_t.shape

    def body(t_ref, alog_ref, dtb_ref, beta_ref, gc_ref):
        upper = _tri(True)
        for h in range(A_HEADS):
            beta_ref[h] = _sigmoid(t_ref[h])
            ea = jnp.exp(jnp.full((n, c), alog_ref[h], F32))
            g = -ea * _softplus(t_ref[A_HEADS + h] + dtb_ref[h])
            gc_ref[h] = _dot_exact_rhs(g, upper)

    smem = pl.BlockSpec(memory_space=pltpu.SMEM)
    blk = pl.BlockSpec((None, A_HEADS, n, c), lambda b: (b, 0, 0, 0))
    return pl.pallas_call(
        body, name="gdn_prep", grid=(bn,),
        in_specs=[pl.BlockSpec((None, 2 * A_HEADS, n, c), lambda b: (b, 0, 0, 0)), smem, smem],
        out_specs=[blk, blk],
        out_shape=[jax.ShapeDtypeStruct((bn, A_HEADS, n, c), F32)] * 2,
        compiler_params=_params(("parallel",)),
    )(tail_t, a_log, dt_bias)


def _gdn_prep_bwd(tail_t, a_log, dt_bias, d_gc, d_beta):
    bn, _, n, c = tail_t.shape

    def body(t_ref, alog_ref, dtb_ref, dgc_ref, dbeta_ref, dt_ref, dal_ref, ddt_ref):
        lower = _tri(False)
        for h in range(A_HEADS):
            beta = _sigmoid(t_ref[h])
            dt_ref[h] = dbeta_ref[h] * beta * (1.0 - beta)
            dg = _dot_exact_rhs(dgc_ref[h], lower)
            ea = jnp.exp(jnp.full((n, c), alog_ref[h], F32))
            xa = t_ref[A_HEADS + h] + dtb_ref[h]
            g = -ea * _softplus(xa)
            dxa = -ea * dg * _sigmoid(xa)
            dt_ref[A_HEADS + h] = dxa
            s1 = jnp.sum(jnp.sum(g * dg, axis=1, keepdims=True), axis=0, keepdims=True)
            s2 = jnp.sum(jnp.sum(dxa, axis=1, keepdims=True), axis=0, keepdims=True)
            dal_ref[h:h + 1, :] = jnp.broadcast_to(s1, (1, 128))
            ddt_ref[h:h + 1, :] = jnp.broadcast_to(s2, (1, 128))

    smem = pl.BlockSpec(memory_space=pltpu.SMEM)
    blk8 = pl.BlockSpec((None, A_HEADS, n, c), lambda b: (b, 0, 0, 0))
    blk16 = pl.BlockSpec((None, 2 * A_HEADS, n, c), lambda b: (b, 0, 0, 0))
    sm = pl.BlockSpec((None, A_HEADS, 128), lambda b: (b, 0, 0))
    return pl.pallas_call(
        body, name="gdn_prep_bwd", grid=(bn,),
        in_specs=[blk16, smem, smem, blk8, blk8],
        out_specs=[blk16, sm, sm],
        out_shape=[jax.ShapeDtypeStruct((bn, 2 * A_HEADS, n, c), F32),
                   jax.ShapeDtypeStruct((bn, A_HEADS, 128), F32),
                   jax.ShapeDtypeStruct((bn, A_HEADS, 128), F32)],
        compiler_params=_params(("parallel",)),
    )(tail_t, a_log, dt_bias, d_gc, d_beta)


HALO = 8


def _conv_window(x_ref, n, first, lo, width):
    if first:
        return jnp.concatenate([jnp.zeros((HALO, width), F32), x_ref[0:A_CHUNK, lo:lo + width]], axis=0)
    start = pl.multiple_of(n * A_CHUNK - HALO, HALO)
    return x_ref[pl.ds(start, A_CHUNK + HALO), lo:lo + width]


def _conv_taps(xw, w):
    y = w[A_CONV - 1:A_CONV, :] * xw
    for j in range(1, A_CONV):
        y = y + w[A_CONV - 1 - j:A_CONV - j, :] * pltpu.roll(xw, j, 0)
    return y[HALO:, :]


def _row_to_col(row, eye):
    c = eye.shape[0]
    return jnp.sum(jnp.where(eye, jnp.broadcast_to(row, (c, c)), 0.0), axis=1, keepdims=True)


def _col_to_row(col, eye):
    c = eye.shape[0]
    return jnp.sum(jnp.where(eye, jnp.broadcast_to(col, (c, c)), 0.0), axis=0, keepdims=True)


def _unit_lower_inverse(a, ri, ci):
    eye = jnp.where(ri == ci, 1.0, 0.0)
    a8 = jnp.where((ri >> 3) == (ci >> 3), a, 0.0)
    a2 = _dot(a8, a8, 1, 0)
    yield
    a4 = _dot(a2, a2, 1, 0)
    t = eye - a8
    t = t + _dot(t, a2, 1, 0)
    yield
    t = t + _dot(t, a4, 1, 0)
    yield
    for sh in (3, 4, 5):
        off = jnp.where(((ri >> (sh + 1)) == (ci >> (sh + 1))) & ((ri >> sh) != (ci >> sh)), a, 0.0)
        left = _dot(t, off, 1, 0)
        yield
        t = t - _dot(left, t, 1, 0)
        yield
    return t


def _round_robin(gens):
    live = list(gens)
    while live:
        nxt = []
        for g in live:
            try:
                next(g)
                nxt.append(g)
            except StopIteration:
                pass
        live = nxt


def _gdn_chunk_inputs(x_ref, cw, n, first):
    xw = _conv_window(x_ref, n, first, 0, A_CONV_COLS)
    y = _conv_taps(xw, cw)
    a = _silu(y)
    aq, ak, v = a[:, 0:A_DK], a[:, A_DK:2 * A_DK], a[:, 2 * A_DK:]
    rq = lax.rsqrt(jnp.sum(aq * aq, axis=1, keepdims=True) + EPS)
    rk = lax.rsqrt(jnp.sum(ak * ak, axis=1, keepdims=True) + EPS)
    return dict(xw=xw, y=y, aq=aq, ak=ak, rq=rq, rk=rk,
                q=aq * rq * (A_DK ** -0.5), k=ak * rk, v=v)


def _gdn_chunk_core(q, k, v, g_row, b_row, t_mat, ri, ci):
    eye = ri == ci
    g_col = _row_to_col(g_row, eye)
    b_col = _row_to_col(b_row, eye)
    causal = ri >= ci
    strict = ri > ci
    dec = jnp.where(causal, jnp.exp(jnp.where(causal, g_col - g_row, 0.0)), 0.0)
    gam = jnp.exp(g_col)
    g_last = g_row[:, A_CHUNK - 1:A_CHUNK]
    gam_last = jnp.exp(g_last)
    e = jnp.exp(g_last - g_col)
    kb = k * b_col
    bv = v * b_col
    kbg = kb * gam
    kk = _dot(kb, k, 1, 1)
    p = _dot(q, k, 1, 1) * dec
    yield
    a_mat = jnp.where(strict, kk * dec, 0.0)
    if t_mat is None:
        t_mat = yield from _unit_lower_inverse(a_mat, ri, ci)
    u = _dot(t_mat, bv, 1, 0)
    w = _dot(t_mat, kbg, 1, 0)
    yield
    return dict(eye=eye, g_col=g_col, b_col=b_col, dec=dec, strict=strict, causal=causal, gam=gam,
                gam_last=gam_last, e=e, kb=kb, bv=bv, kbg=kbg, a_mat=a_mat, t_mat=t_mat, u=u, w=w, p=p,
                qg=q * gam, kd=k * e)


def _gdn_fwd(proj_hm, cw_hm, beta, gc, norm_g):
    bn, s, _ = proj_hm.shape
    n = s // A_CHUNK

    def body(x_ref, cw_ref, beta_ref, gc_ref, ng_ref, og_ref, oraw_ref, st_ref, t_ref, state):
        ri = lax.broadcasted_iota(jnp.int32, (A_CHUNK, A_CHUNK), 0)
        ci = lax.broadcasted_iota(jnp.int32, (A_CHUNK, A_CHUNK), 1)
        cw = cw_ref[...]
        ng = ng_ref[...]
        state[...] = jnp.zeros_like(state)

        def chunk(i, first):
            rows = pl.ds(0 if first else pl.multiple_of(i * A_CHUNK, A_CHUNK), A_CHUNK)
            cin = _gdn_chunk_inputs(x_ref, cw, i, first)
            core = _gdn_chunk_core(cin["q"], cin["k"], cin["v"], gc_ref[pl.ds(i, 1), :],
                                   beta_ref[pl.ds(i, 1), :], None, ri, ci)
            st = state[...]
            st_ref[i] = st
            t_ref[i] = core["t_mat"]
            vn = core["u"] - _dot(core["w"], st, 1, 0)
            o = _dot(core["qg"], st, 1, 0) + _dot(core["p"], vn, 1, 0)
            state[...] = st * core["gam_last"] + _dot(core["kd"], vn, 0, 0)
            oraw_ref[rows, :] = o
            r = lax.rsqrt(jnp.mean(o * o, axis=1, keepdims=True) + EPS)
            z = x_ref[rows, A_CONV_COLS:A_HEAD_COLS]
            og_ref[rows, :] = (o * r * ng * _silu(z)).astype(BF16)

        chunk(0, True)
        lax.fori_loop(1, n, lambda i, c: (chunk(i, False), c)[1], 0)

    return pl.pallas_call(
        body, name="gdn_fwd", grid=(bn, A_HEADS),
        in_specs=[pl.BlockSpec((None, s, A_HEAD_COLS), lambda b, h: (b, 0, h)),
                  pl.BlockSpec((A_CONV, A_CONV_COLS), lambda b, h: (0, h)),
                  pl.BlockSpec((None, None, n, A_CHUNK), lambda b, h: (b, h, 0, 0)),
                  pl.BlockSpec((None, None, n, A_CHUNK), lambda b, h: (b, h, 0, 0)),
                  pl.BlockSpec((1, A_DV), lambda b, h: (0, 0))],
        out_specs=[pl.BlockSpec((None, s, A_DV), lambda b, h: (b, 0, h)),
                   pl.BlockSpec((None, s, A_DV), lambda b, h: (b, 0, h)),
                   pl.BlockSpec((None, None, n, A_DK, A_DV), lambda b, h: (b, h, 0, 0, 0)),
                   pl.BlockSpec((None, None, n, A_CHUNK, A_CHUNK), lambda b, h: (b, h, 0, 0, 0))],
        out_shape=[jax.ShapeDtypeStruct((bn, s, A_VW), BF16),
                   jax.ShapeDtypeStruct((bn, s, A_VW), F32),
                   jax.ShapeDtypeStruct((bn, A_HEADS, n, A_DK, A_DV), F32),
                   jax.ShapeDtypeStruct((bn, A_HEADS, n, A_CHUNK, A_CHUNK), F32)],
        scratch_shapes=[pltpu.VMEM((A_DK, A_DV), F32)],
        compiler_params=_params(("parallel", "parallel"), VMEM_BIG),
    )(proj_hm, cw_hm, beta, gc, norm_g)


def _gdn_bwd(proj_hm, cw_hm, beta, gc, norm_g, oraw, states, t_mats, dog):
    bn, s, _ = proj_hm.shape
    n = s // A_CHUNK

    def body(x_ref, cw_ref, beta_ref, gc_ref, ng_ref, oraw_ref, st_ref, t_ref, dog_ref,
             dx_ref, dgc_ref, dbeta_ref, dcw_ref, dng_ref, dstate, dy_next):
        ri = lax.broadcasted_iota(jnp.int32, (A_CHUNK, A_CHUNK), 0)
        ci = lax.broadcasted_iota(jnp.int32, (A_CHUNK, A_CHUNK), 1)
        cw = cw_ref[...]
        ng = ng_ref[...]
        dstate[...] = jnp.zeros_like(dstate)
        dy_next[...] = jnp.zeros_like(dy_next)
        dcw_ref[...] = jnp.zeros_like(dcw_ref)
        dng_ref[...] = jnp.zeros_like(dng_ref)

        def chunk(i, first):
            rows = pl.ds(0 if first else pl.multiple_of(i * A_CHUNK, A_CHUNK), A_CHUNK)
            cin = _gdn_chunk_inputs(x_ref, cw, i, first)
            q, k, v = cin["q"], cin["k"], cin["v"]
            g_row = gc_ref[pl.ds(i, 1), :]
            b_row = beta_ref[pl.ds(i, 1), :]
            cr = _gdn_chunk_core(q, k, v, g_row, b_row, t_ref[i], ri, ci)
            eye, dec, gam, e = cr["eye"], cr["dec"], cr["gam"], cr["e"]
            b_col, t_mat, u, w, p = cr["b_col"], cr["t_mat"], cr["u"], cr["w"], cr["p"]
            st = st_ref[i]
            ds_out = dstate[...]

            o = oraw_ref[rows, :]
            z = x_ref[rows, A_CONV_COLS:A_HEAD_COLS]
            d_og = dog_ref[rows, :]
            r = lax.rsqrt(jnp.mean(o * o, axis=1, keepdims=True) + EPS)
            oh = o * r
            d_on = d_og * _silu(z)
            dz = d_og * oh * ng * _dsilu(z)
            dng_ref[0:1, :] += jnp.sum(d_on * oh, axis=0, keepdims=True)
            d_oh = d_on * ng
            d_o = r * (d_oh - oh * jnp.mean(d_oh * oh, axis=1, keepdims=True))

            vn = u - _dot(w, st, 1, 0)
            d_vn = _dot(p, d_o, 0, 0) + _dot(cr["kd"], ds_out, 1, 0)
            d_p = jnp.where(cr["causal"], _dot(d_o, vn, 1, 1), 0.0)
            d_qg = _dot(d_o, st, 1, 1)
            d_kd = _dot(vn, ds_out, 1, 1)
            d_gam_last = jnp.sum(jnp.sum(st * ds_out, axis=1, keepdims=True), axis=0, keepdims=True)
            d_w = -_dot(d_vn, st, 1, 1)
            dstate[...] = _dot(cr["qg"], d_o, 0, 0) + ds_out * cr["gam_last"] - _dot(w, d_vn, 0, 0)
            d_bv = _dot(t_mat, d_vn, 0, 0)
            d_kbg = _dot(t_mat, d_w, 0, 0)
            d_a = jnp.where(cr["strict"], -(_dot(d_bv, u, 1, 1) + _dot(d_kbg, w, 1, 1)), 0.0)
            m_a = d_a * dec
            n_p = d_p * dec
            d_kb = _dot(m_a, k, 1, 0) + d_kbg * gam
            d_q = _dot(n_p, k, 1, 0) + d_qg * gam
            d_k = (_dot(m_a, cr["kb"], 0, 0) + _dot(n_p, q, 0, 0) + d_kd * e + d_kb * b_col)
            d_v = d_bv * b_col
            d_beta_col = (jnp.sum(d_bv * v, axis=1, keepdims=True)
                          + jnp.sum(d_kb * k, axis=1, keepdims=True))
            gterm = d_a * cr["a_mat"] + d_p * p
            d_e = jnp.sum(d_kd * k, axis=1, keepdims=True) * e
            d_g_col = (jnp.sum(gterm, axis=1, keepdims=True)
                       + (jnp.sum(d_qg * q, axis=1, keepdims=True)
                          + jnp.sum(d_kbg * cr["kb"], axis=1, keepdims=True)) * gam
                       - d_e)
            d_g_last = jnp.sum(d_e, axis=0, keepdims=True) + d_gam_last * cr["gam_last"]
            lane = lax.broadcasted_iota(jnp.int32, (1, A_CHUNK), 1)
            d_g_row = (_col_to_row(d_g_col, eye) - jnp.sum(gterm, axis=0, keepdims=True)
                       + jnp.where(lane == A_CHUNK - 1, d_g_last, 0.0))
            dgc_ref[pl.ds(i, 1), :] = d_g_row
            dbeta_ref[pl.ds(i, 1), :] = _col_to_row(d_beta_col, eye)

            qh = cin["aq"] * cin["rq"]
            kh = cin["ak"] * cin["rk"]
            d_qh = d_q * (A_DK ** -0.5)
            d_aq = cin["rq"] * (d_qh - qh * jnp.sum(d_qh * qh, axis=1, keepdims=True))
            d_ak = cin["rk"] * (d_k - kh * jnp.sum(d_k * kh, axis=1, keepdims=True))
            d_y = jnp.concatenate([d_aq, d_ak, d_v], axis=1) * _dsilu(cin["y"])
            xw = cin["xw"]
            dyw = jnp.concatenate([d_y, dy_next[...]], axis=0)
            d_x = cw[A_CONV - 1:A_CONV, :] * dyw
            for j in range(1, A_CONV):
                d_x = d_x + cw[A_CONV - 1 - j:A_CONV - j, :] * pltpu.roll(dyw, A_CHUNK + HALO - j, 0)
            d_x = d_x[0:A_CHUNK, :]
            dy_pad = jnp.concatenate([jnp.zeros((HALO, A_CONV_COLS), F32), d_y], axis=0)
            for j in range(A_CONV):
                xs = xw if j == 0 else pltpu.roll(xw, j, 0)
                dcw_ref[A_CONV - 1 - j:A_CONV - j, :] += jnp.sum(dy_pad * xs, axis=0, keepdims=True)
            dy_next[...] = d_y[0:HALO, :]
            dx_ref[rows, 0:A_CONV_COLS] = d_x.astype(BF16)
            dx_ref[rows, A_CONV_COLS:A_HEAD_COLS] = dz.astype(BF16)

        lax.fori_loop(0, n - 1, lambda i, c: (chunk(n - 1 - i, False), c)[1], 0)
        chunk(0, True)

    hn = lambda b, h: (b, h, 0, 0)
    return pl.pallas_call(
        body, name="gdn_bwd", grid=(bn, A_HEADS),
        in_specs=[pl.BlockSpec((None, s, A_HEAD_COLS), lambda b, h: (b, 0, h)),
                  pl.BlockSpec((A_CONV, A_CONV_COLS), lambda b, h: (0, h)),
                  pl.BlockSpec((None, None, n, A_CHUNK), hn),
                  pl.BlockSpec((None, None, n, A_CHUNK), hn),
                  pl.BlockSpec((1, A_DV), lambda b, h: (0, 0)),
                  pl.BlockSpec((None, s, A_DV), lambda b, h: (b, 0, h)),
                  pl.BlockSpec((None, None, n, A_DK, A_DV), lambda b, h: (b, h, 0, 0, 0)),
                  pl.BlockSpec((None, None, n, A_CHUNK, A_CHUNK), lambda b, h: (b, h, 0, 0, 0)),
                  pl.BlockSpec((None, s, A_DV), lambda b, h: (b, 0, h))],
        out_specs=[pl.BlockSpec((None, s, A_HEAD_COLS), lambda b, h: (b, 0, h)),
                   pl.BlockSpec((None, None, n, A_CHUNK), hn),
                   pl.BlockSpec((None, None, n, A_CHUNK), hn),
                   pl.BlockSpec((None, A_CONV, A_CONV_COLS), lambda b, h: (b, 0, h)),
                   pl.BlockSpec((None, None, 8, A_DV), hn)],
        out_shape=[jax.ShapeDtypeStruct((bn, s, A_HEADS * A_HEAD_COLS), BF16),
                   jax.ShapeDtypeStruct((bn, A_HEADS, n, A_CHUNK), F32),
                   jax.ShapeDtypeStruct((bn, A_HEADS, n, A_CHUNK), F32),
                   jax.ShapeDtypeStruct((bn, A_CONV, A_HEADS * A_CONV_COLS), F32),
                   jax.ShapeDtypeStruct((bn, A_HEADS, 8, A_DV), F32)],
        scratch_shapes=[pltpu.VMEM((A_DK, A_DV), F32), pltpu.VMEM((HALO, A_CONV_COLS), F32)],
        compiler_params=_params(("parallel", "parallel"), VMEM_BIG),
    )(proj_hm, cw_hm, beta, gc, norm_g, oraw, states, t_mats, dog)


A_SEQ_BLK = 512
A_BLK_CHUNKS = A_SEQ_BLK // A_CHUNK


def _gdn_halo(proj_hm):
    bn, s, w = proj_hm.shape
    last = proj_hm.reshape(bn, s // A_SEQ_BLK, A_SEQ_BLK, w)[:, :, A_SEQ_BLK - HALO:, :]
    return jnp.concatenate([jnp.zeros((bn, 1, HALO, w), proj_hm.dtype), last[:, :-1]], axis=1)


def _gdn_window(x_ref, halo_ref, ci, first, lo):
    if first:
        return jnp.concatenate([halo_ref[:, lo:lo + A_CONV_COLS], x_ref[0:A_CHUNK, lo:lo + A_CONV_COLS]], axis=0)
    start = pl.multiple_of(ci * A_CHUNK - HALO, HALO)
    return x_ref[pl.ds(start, A_CHUNK + HALO), lo:lo + A_CONV_COLS]


def _gdn_chunk_prep(xw, cw):
    y = _conv_taps(xw, cw)
    a = _silu(y)
    aq, ak, v = a[:, 0:A_DK], a[:, A_DK:2 * A_DK], a[:, 2 * A_DK:]
    rq = lax.rsqrt(jnp.sum(aq * aq, axis=1, keepdims=True) + EPS)
    rk = lax.rsqrt(jnp.sum(ak * ak, axis=1, keepdims=True) + EPS)
    return dict(xw=xw, y=y, aq=aq, ak=ak, rq=rq, rk=rk, q=aq * rq * (A_DK ** -0.5), k=ak * rk, v=v)


def _gdn_fwd(proj_hm, cw_hm, beta, gc, norm_g, hp=4):
    bn, s, _ = proj_hm.shape
    n = s // A_CHUNK
    nsb = s // A_SEQ_BLK
    halo = _gdn_halo(proj_hm)

    def body(x_ref, halo_ref, cw_ref, beta_ref, gc_ref, ng_ref, og_ref, oraw_ref, st_ref, t_ref, state):
        ri = lax.broadcasted_iota(jnp.int32, (A_CHUNK, A_CHUNK), 0)
        ci_ = lax.broadcasted_iota(jnp.int32, (A_CHUNK, A_CHUNK), 1)
        ng = ng_ref[...]

        @pl.when(pl.program_id(2) == 0)
        def _():
            state[...] = jnp.zeros_like(state)

        def one_head(hh, ci, first, rows):
            lo = hh * A_HEAD_COLS
            cw = cw_ref[:, hh * A_CONV_COLS:(hh + 1) * A_CONV_COLS]
            cin = _gdn_chunk_prep(_gdn_window(x_ref, halo_ref, ci, first, lo), cw)
            core = yield from _gdn_chunk_core(cin["q"], cin["k"], cin["v"], gc_ref[hh, pl.ds(ci, 1), :],
                                              beta_ref[hh, pl.ds(ci, 1), :], None, ri, ci_)
            st = state[hh]
            st_ref[hh, ci] = st
            t_ref[hh, ci] = core["t_mat"]
            vn = core["u"] - _dot(core["w"], st, 1, 0)
            qs = _dot(core["qg"], st, 1, 0)
            yield
            o = qs + _dot(core["p"], vn, 1, 0)
            state[hh] = st * core["gam_last"] + _dot(core["kd"], vn, 0, 0)
            yield
            ocols = slice(hh * A_DV, (hh + 1) * A_DV)
            oraw_ref[rows, ocols] = o
            r = lax.rsqrt(jnp.mean(o * o, axis=1, keepdims=True) + EPS)
            z = x_ref[rows, lo + A_CONV_COLS:lo + A_HEAD_COLS]
            og_ref[rows, ocols] = (o * r * ng * _silu(z)).astype(BF16)

        def chunk(ci, first):
            rows = pl.ds(0 if first else pl.multiple_of(ci * A_CHUNK, A_CHUNK), A_CHUNK)
            _round_robin([one_head(hh, ci, first, rows) for hh in range(hp)])

        chunk(0, True)
        lax.fori_loop(1, A_BLK_CHUNKS, lambda i, c: (chunk(i, False), c)[1], 0)

    small = pl.BlockSpec((None, hp, A_BLK_CHUNKS, A_CHUNK), lambda b, h, j: (b, h, j, 0))
    return pl.pallas_call(
        body, name="gdn_fwd", grid=(bn, A_HEADS // hp, nsb),
        in_specs=[pl.BlockSpec((None, A_SEQ_BLK, hp * A_HEAD_COLS), lambda b, h, j: (b, j, h)),
                  pl.BlockSpec((None, None, HALO, hp * A_HEAD_COLS), lambda b, h, j: (b, j, 0, h)),
                  pl.BlockSpec((A_CONV, hp * A_CONV_COLS), lambda b, h, j: (0, h)),
                  small, small,
                  pl.BlockSpec((1, A_DV), lambda b, h, j: (0, 0))],
        out_specs=[pl.BlockSpec((None, A_SEQ_BLK, hp * A_DV), lambda b, h, j: (b, j, h)),
                   pl.BlockSpec((None, A_SEQ_BLK, hp * A_DV), lambda b, h, j: (b, j, h)),
                   pl.BlockSpec((None, hp, A_BLK_CHUNKS, A_DK, A_DV), lambda b, h, j: (b, h, j, 0, 0)),
                   pl.BlockSpec((None, hp, A_BLK_CHUNKS, A_CHUNK, A_CHUNK), lambda b, h, j: (b, h, j, 0, 0))],
        out_shape=[jax.ShapeDtypeStruct((bn, s, A_VW), BF16),
                   jax.ShapeDtypeStruct((bn, s, A_VW), F32),
                   jax.ShapeDtypeStruct((bn, A_HEADS, n, A_DK, A_DV), F32),
                   jax.ShapeDtypeStruct((bn, A_HEADS, n, A_CHUNK, A_CHUNK), F32)],
        scratch_shapes=[pltpu.VMEM((hp, A_DK, A_DV), F32)],
        compiler_params=_params(("parallel", "parallel", "arbitrary"), VMEM_BIG),
    )(proj_hm, halo, cw_hm, beta, gc, norm_g)


def _gdn_bwd(proj_hm, cw_hm, beta, gc, norm_g, oraw, states, t_mats, dog, hp=4):
    bn, s, _ = proj_hm.shape
    n = s // A_CHUNK
    nsb = s // A_SEQ_BLK
    halo = _gdn_halo(proj_hm)

    def body(x_ref, halo_ref, cw_ref, beta_ref, gc_ref, ng_ref, oraw_ref, st_ref, t_ref, dog_ref,
             dx_ref, dgc_ref, dbeta_ref, dcw_ref, dng_ref, dstate, dy_next):
        ri = lax.broadcasted_iota(jnp.int32, (A_CHUNK, A_CHUNK), 0)
        ci_ = lax.broadcasted_iota(jnp.int32, (A_CHUNK, A_CHUNK), 1)
        lane = lax.broadcasted_iota(jnp.int32, (1, A_CHUNK), 1)
        ng = ng_ref[...]

        @pl.when(pl.program_id(2) == 0)
        def _():
            dstate[...] = jnp.zeros_like(dstate)
            dy_next[...] = jnp.zeros_like(dy_next)
            dcw_ref[...] = jnp.zeros_like(dcw_ref)
            dng_ref[...] = jnp.zeros_like(dng_ref)

        def one_head(hh, ci, first, rows):
            lo = hh * A_HEAD_COLS
            ccols = slice(hh * A_CONV_COLS, (hh + 1) * A_CONV_COLS)
            ocols = slice(hh * A_DV, (hh + 1) * A_DV)
            cw = cw_ref[:, ccols]
            cin = _gdn_chunk_prep(_gdn_window(x_ref, halo_ref, ci, first, lo), cw)
            q, k, v = cin["q"], cin["k"], cin["v"]
            cr = yield from _gdn_chunk_core(q, k, v, gc_ref[hh, pl.ds(ci, 1), :], beta_ref[hh, pl.ds(ci, 1), :],
                                            t_ref[hh, ci], ri, ci_)
            eye, dec, gam, e = cr["eye"], cr["dec"], cr["gam"], cr["e"]
            b_col, t_mat, u, w, p = cr["b_col"], cr["t_mat"], cr["u"], cr["w"], cr["p"]
            st = st_ref[hh, ci]
            ds_out = dstate[hh]

            o = oraw_ref[rows, ocols]
            z = x_ref[rows, lo + A_CONV_COLS:lo + A_HEAD_COLS]
            d_og = dog_ref[rows, ocols]
            r = lax.rsqrt(jnp.mean(o * o, axis=1, keepdims=True) + EPS)
            oh = o * r
            d_on = d_og * _silu(z)
            dz = d_og * oh * ng * _dsilu(z)
            dng_ref[hh, 0:1, :] += jnp.sum(d_on * oh, axis=0, keepdims=True)
            d_oh = d_on * ng
            d_o = r * (d_oh - oh * jnp.mean(d_oh * oh, axis=1, keepdims=True))

            vn = u - _dot(w, st, 1, 0)
            d_vn = _dot(p, d_o, 0, 0) + _dot(cr["kd"], ds_out, 1, 0)
            d_qg = _dot(d_o, st, 1, 1)
            qgdo = _dot(cr["qg"], d_o, 0, 0)
            yield
            d_p = jnp.where(cr["causal"], _dot(d_o, vn, 1, 1), 0.0)
            d_kd = _dot(vn, ds_out, 1, 1)
            d_gam_last = jnp.sum(jnp.sum(st * ds_out, axis=1, keepdims=True), axis=0, keepdims=True)
            d_w = -_dot(d_vn, st, 1, 1)
            dstate[hh] = qgdo + ds_out * cr["gam_last"] - _dot(w, d_vn, 0, 0)
            d_bv = _dot(t_mat, d_vn, 0, 0)
            yield
            d_kbg = _dot(t_mat, d_w, 0, 0)
            n_p = d_p * dec
            d_q = _dot(n_p, k, 1, 0) + d_qg * gam
            npq = _dot(n_p, q, 0, 0)
            yield
            d_a = jnp.where(cr["strict"], -(_dot(d_bv, u, 1, 1) + _dot(d_kbg, w, 1, 1)), 0.0)
            yield
            m_a = d_a * dec
            d_kb = _dot(m_a, k, 1, 0) + d_kbg * gam
            d_k = (_dot(m_a, cr["kb"], 0, 0) + npq + d_kd * e + d_kb * b_col)
            yield
            d_v = d_bv * b_col
            d_beta_col = (jnp.sum(d_bv * v, axis=1, keepdims=True)
                          + jnp.sum(d_kb * k, axis=1, keepdims=True))
            gterm = d_a * cr["a_mat"] + d_p * p
            d_e = jnp.sum(d_kd * k, axis=1, keepdims=True) * e
            d_g_col = (jnp.sum(gterm, axis=1, keepdims=True)
                       + (jnp.sum(d_qg * q, axis=1, keepdims=True)
                          + jnp.sum(d_kbg * cr["kb"], axis=1, keepdims=True)) * gam
                       - d_e)
            d_g_last = jnp.sum(d_e, axis=0, keepdims=True) + d_gam_last * cr["gam_last"]
            d_g_row = (_col_to_row(d_g_col, eye) - jnp.sum(gterm, axis=0, keepdims=True)
                       + jnp.where(lane == A_CHUNK - 1, d_g_last, 0.0))
            dgc_ref[hh, pl.ds(ci, 1), :] = d_g_row
            dbeta_ref[hh, pl.ds(ci, 1), :] = _col_to_row(d_beta_col, eye)

            qh = cin["aq"] * cin["rq"]
            kh = cin["ak"] * cin["rk"]
            d_qh = d_q * (A_DK ** -0.5)
            d_aq = cin["rq"] * (d_qh - qh * jnp.sum(d_qh * qh, axis=1, keepdims=True))
            d_ak = cin["rk"] * (d_k - kh * jnp.sum(d_k * kh, axis=1, keepdims=True))
            d_y = jnp.concatenate([d_aq, d_ak, d_v], axis=1) * _dsilu(cin["y"])
            xw = cin["xw"]
            dyw = jnp.concatenate([d_y, dy_next[hh]], axis=0)
            d_x = cw[A_CONV - 1:A_CONV, :] * dyw
            for j in range(1, A_CONV):
                d_x = d_x + cw[A_CONV - 1 - j:A_CONV - j, :] * pltpu.roll(dyw, A_CHUNK + HALO - j, 0)
            dy_pad = jnp.concatenate([jnp.zeros((HALO, A_CONV_COLS), F32), d_y], axis=0)
            for j in range(A_CONV):
                xs = xw if j == 0 else pltpu.roll(xw, j, 0)
                dcw_ref[A_CONV - 1 - j:A_CONV - j, ccols] += jnp.sum(dy_pad * xs, axis=0, keepdims=True)
            dy_next[hh] = d_y[0:HALO, :]
            dx_ref[rows, lo:lo + A_CONV_COLS] = d_x[0:A_CHUNK, :].astype(BF16)
            dx_ref[rows, lo + A_CONV_COLS:lo + A_HEAD_COLS] = dz.astype(BF16)

        def chunk(ci, first):
            rows = pl.ds(0 if first else pl.multiple_of(ci * A_CHUNK, A_CHUNK), A_CHUNK)
            _round_robin([one_head(hh, ci, first, rows) for hh in range(hp)])

        lax.fori_loop(0, A_BLK_CHUNKS - 1, lambda i, c: (chunk(A_BLK_CHUNKS - 1 - i, False), c)[1], 0)
        chunk(0, True)

    rev = lambda j: nsb - 1 - j
    small = pl.BlockSpec((None, hp, A_BLK_CHUNKS, A_CHUNK), lambda b, h, j: (b, h, rev(j), 0))
    wide = pl.BlockSpec((None, A_SEQ_BLK, hp * A_HEAD_COLS), lambda b, h, j: (b, rev(j), h))
    val = pl.BlockSpec((None, A_SEQ_BLK, hp * A_DV), lambda b, h, j: (b, rev(j), h))
    return pl.pallas_call(
        body, name="gdn_bwd", grid=(bn, A_HEADS // hp, nsb),
        in_specs=[wide,
                  pl.BlockSpec((None, None, HALO, hp * A_HEAD_COLS), lambda b, h, j: (b, rev(j), 0, h)),
                  pl.BlockSpec((A_CONV, hp * A_CONV_COLS), lambda b, h, j: (0, h)),
                  small, small,
                  pl.BlockSpec((1, A_DV), lambda b, h, j: (0, 0)),
                  val,
                  pl.BlockSpec((None, hp, A_BLK_CHUNKS, A_DK, A_DV), lambda b, h, j: (b, h, rev(j), 0, 0)),
                  pl.BlockSpec((None, hp, A_BLK_CHUNKS, A_CHUNK, A_CHUNK), lambda b, h, j: (b, h, rev(j), 0, 0)),
                  val],
        out_specs=[wide, small, small,
                   pl.BlockSpec((None, A_CONV, hp * A_CONV_COLS), lambda b, h, j: (b, 0, h)),
                   pl.BlockSpec((None, hp, 8, A_DV), lambda b, h, j: (b, h, 0, 0))],
        out_shape=[jax.ShapeDtypeStruct((bn, s, A_HEADS * A_HEAD_COLS), BF16),
                   jax.ShapeDtypeStruct((bn, A_HEADS, n, A_CHUNK), F32),
                   jax.ShapeDtypeStruct((bn, A_HEADS, n, A_CHUNK), F32),
                   jax.ShapeDtypeStruct((bn, A_CONV, A_HEADS * A_CONV_COLS), F32),
                   jax.ShapeDtypeStruct((bn, A_HEADS, 8, A_DV), F32)],
        scratch_shapes=[pltpu.VMEM((hp, A_DK, A_DV), F32), pltpu.VMEM((hp, HALO, A_CONV_COLS), F32)],
        compiler_params=_params(("parallel", "parallel", "arbitrary"), VMEM_BIG),
    )(proj_hm, halo, cw_hm, beta, gc, norm_g, oraw, states, t_mats, dog)


def _rope_tables(posf, inv_freq_row):
    t = posf.shape[0]
    tm = 512

    def body(p_ref, f_ref, c_ref, sa_ref, sb_ref):
        ang = p_ref[...] * f_ref[...]
        lane = lax.broadcasted_iota(jnp.int32, ang.shape, 1)
        half = ROPE_DIMS // 2
        c_ref[...] = jnp.where(lane < ROPE_DIMS, jnp.cos(ang), 1.0)
        sn = jnp.sin(ang)
        sa_ref[...] = jnp.where(lane < half, -sn, 0.0)
        sb_ref[...] = jnp.where((lane >= half) & (lane < ROPE_DIMS), sn, 0.0)

    row = pl.BlockSpec((tm, 128), lambda i: (i, 0))
    return pl.pallas_call(
        body, name="rope_tables", grid=(t // tm,),
        in_specs=[row, pl.BlockSpec((1, 128), lambda i: (0, 0))], out_specs=[row] * 3,
        out_shape=[jax.ShapeDtypeStruct((t, 128), F32)] * 3,
        compiler_params=_params(("parallel",)),
    )(posf, inv_freq_row)


def _rope(x, c, sa, sb):
    half = ROPE_DIMS // 2
    return x * c + pltpu.roll(x, 128 - half, 1) * sa + pltpu.roll(x, half, 1) * sb


def _rope_t(d, c, sa, sb):
    half = ROPE_DIMS // 2
    return d * c + pltpu.roll(d * sa, half, 1) + pltpu.roll(d * sb, 128 - half, 1)


def _qk_prep(proj, c, sa, sb, qg, kg, name, tm=256):
    t = proj.shape[0]
    wide = proj.shape[1]

    def body(x_ref, c_ref, sa_ref, sb_ref, qg_ref, kg_ref, o_ref):
        cc, s1, s2 = c_ref[...], sa_ref[...], sb_ref[...]
        for which, g_ref in ((0, qg_ref), (1, kg_ref)):
            g = g_ref[...]
            for h in range(B_HEADS):
                lo = which * B_W + h * B_DH
                xv = x_ref[:, lo:lo + B_DH]
                r = lax.rsqrt(jnp.mean(xv * xv, axis=1, keepdims=True) + EPS)
                o_ref[:, lo:lo + B_DH] = _rope(xv * r * g, cc, s1, s2).astype(BF16)
        o_ref[:, 2 * B_W:3 * B_W] = x_ref[:, 2 * B_W:3 * B_W].astype(BF16)

    tab = pl.BlockSpec((tm, 128), lambda i: (i, 0))
    gain = pl.BlockSpec((1, B_DH), lambda i: (0, 0))
    return pl.pallas_call(
        body, name=name, grid=(t // tm,),
        in_specs=[pl.BlockSpec((tm, wide), lambda i: (i, 0)), tab, tab, tab, gain, gain],
        out_specs=pl.BlockSpec((tm, 3 * B_W), lambda i: (i, 0)),
        out_shape=jax.ShapeDtypeStruct((t, 3 * B_W), BF16),
        compiler_params=_params(("parallel",), 40 * 1024 * 1024),
    )(proj, c, sa, sb, qg, kg)


def _qk_prep_bwd(proj, c, sa, sb, qg, kg, dq, dk, dv, dz, name, tm=256):
    t = proj.shape[0]
    wide = proj.shape[1]
    out_w = 3 * B_W + (B_W if dz is not None else 0)

    def body(*refs):
        x_ref, c_ref, sa_ref, sb_ref, qg_ref, kg_ref, dq_ref, dk_ref, dv_ref = refs[:9]
        if dz is not None:
            dz_ref, o_ref, dgain_ref = refs[9:]
        else:
            o_ref, dgain_ref = refs[9:]
        i = pl.program_id(0)

        @pl.when(i == 0)
        def _():
            dgain_ref[...] = jnp.zeros_like(dgain_ref)

        cc, s1, s2 = c_ref[...], sa_ref[...], sb_ref[...]
        for which, g_ref, d_ref in ((0, qg_ref, dq_ref), (1, kg_ref, dk_ref)):
            g = g_ref[...]
            acc = jnp.zeros((1, B_DH), F32)
            for h in range(B_HEADS):
                lo = which * B_W + h * B_DH
                xv = x_ref[:, lo:lo + B_DH]
                r = lax.rsqrt(jnp.mean(xv * xv, axis=1, keepdims=True) + EPS)
                xh = xv * r
                d_xn = _rope_t(d_ref[:, h * B_DH:(h + 1) * B_DH], cc, s1, s2)
                acc = acc + jnp.sum(d_xn * xh, axis=0, keepdims=True)
                d_xh = d_xn * g
                d_x = r * (d_xh - xh * jnp.mean(d_xh * xh, axis=1, keepdims=True))
                o_ref[:, lo:lo + B_DH] = d_x.astype(BF16)
            dgain_ref[which:which + 1, :] += acc
        o_ref[:, 2 * B_W:3 * B_W] = dv_ref[...].astype(BF16)
        if dz is not None:
            o_ref[:, 3 * B_W:4 * B_W] = dz_ref[...]

    tab = pl.BlockSpec((tm, 128), lambda i: (i, 0))
    gain = pl.BlockSpec((1, B_DH), lambda i: (0, 0))
    grad = pl.BlockSpec((tm, B_W), lambda i: (i, 0))
    in_specs = [pl.BlockSpec((tm, wide), lambda i: (i, 0)), tab, tab, tab, gain, gain, grad, grad, grad]
    args = [proj, c, sa, sb, qg, kg, dq, dk, dv]
    if dz is not None:
        in_specs.append(grad)
        args.append(dz)
    return pl.pallas_call(
        body, name=name, grid=(t // tm,), in_specs=in_specs,
        out_specs=[pl.BlockSpec((tm, out_w), lambda i: (i, 0)), pl.BlockSpec((8, B_DH), lambda i: (0, 0))],
        out_shape=[jax.ShapeDtypeStruct((t, out_w), BF16), jax.ShapeDtypeStruct((8, B_DH), F32)],
        compiler_params=_params(("arbitrary",), 40 * 1024 * 1024),
    )(*args)


def _attn_masks():
    qi = lax.broadcasted_iota(jnp.int32, (B_BLK, 2 * B_BLK), 0)
    kj = lax.broadcasted_iota(jnp.int32, (B_BLK, 2 * B_BLK), 1)
    two = (kj >= qi) & (kj <= qi + B_BLK)
    q1 = lax.broadcasted_iota(jnp.int32, (B_BLK, B_BLK), 0)
    k1 = lax.broadcasted_iota(jnp.int32, (B_BLK, B_BLK), 1)
    return k1 <= q1, two


def _lane_pick(ref_rows, h):
    lane = lax.broadcasted_iota(jnp.int32, ref_rows.shape, 1)
    return jnp.sum(jnp.where(lane == h, ref_rows, 0.0), axis=1, keepdims=True)


def _attn_fwd(qkv, name):
    ns, ln, _ = qkv.shape
    nb = ln // B_BLK
    scale = B_DH ** -0.5

    def body(q_ref, k_ref, v_ref, o_ref, lse_ref):
        h = pl.program_id(1)
        mask1, mask2 = _attn_masks()

        @pl.when(h == 0)
        def _():
            lse_ref[...] = jnp.zeros_like(lse_ref)

        def block(i, first):
            rows = pl.ds(pl.multiple_of(i * B_BLK, B_BLK), B_BLK)
            if first:
                win, mask = pl.ds(0, B_BLK), mask1
            else:
                win, mask = pl.ds(pl.multiple_of((i - 1) * B_BLK, B_BLK), 2 * B_BLK), mask2
            sc = jnp.where(mask, _dot(q_ref[rows, :], k_ref[win, :], 1, 1) * scale, -1e30)
            m = jnp.max(sc, axis=1, keepdims=True)
            p = jnp.exp(sc - m)
            l = jnp.sum(p, axis=1, keepdims=True)
            o_ref[rows, :] = _dot(p, v_ref[win, :], 1, 0) / l
            lane = lax.broadcasted_iota(jnp.int32, (B_BLK, B_HEADS), 1)
            lse_ref[rows, :] = jnp.where(lane == h, m + jnp.log(l), lse_ref[rows, :])

        block(0, True)
        if nb > 1:
            lax.fori_loop(1, nb, lambda i, c: (block(i, False), c)[1], 0)

    return pl.pallas_call(
        body, name=name, grid=(ns, B_HEADS),
        in_specs=[pl.BlockSpec((None, ln, B_DH), lambda s, h: (s, 0, h)),
                  pl.BlockSpec((None, ln, B_DH), lambda s, h: (s, 0, B_HEADS + h)),
                  pl.BlockSpec((None, ln, B_DH), lambda s, h: (s, 0, 2 * B_HEADS + h))],
        out_specs=[pl.BlockSpec((None, ln, B_DH), lambda s, h: (s, 0, h)),
                   pl.BlockSpec((None, ln, B_HEADS), lambda s, h: (s, 0, 0))],
        out_shape=[jax.ShapeDtypeStruct((ns, ln, B_W), F32), jax.ShapeDtypeStruct((ns, ln, B_HEADS), F32)],
        compiler_params=_params(("parallel", "arbitrary")),
    )(qkv, qkv, qkv)


def _attn_bwd(qkv, d_o, lse_joint, delta, name):
    ns, ln, _ = qkv.shape
    nb = ln // B_BLK
    scale = B_DH ** -0.5

    def body(q_ref, k_ref, v_ref, do_ref, lj_ref, dl_ref, dq_ref, dk_ref, dv_ref):
        h = pl.program_id(1)
        mask1, mask2 = _attn_masks()
        dk_ref[...] = jnp.zeros_like(dk_ref)
        dv_ref[...] = jnp.zeros_like(dv_ref)

        def block(i, first):
            rows = pl.ds(pl.multiple_of(i * B_BLK, B_BLK), B_BLK)
            if first:
                win, mask = pl.ds(0, B_BLK), mask1
            else:
                win, mask = pl.ds(pl.multiple_of((i - 1) * B_BLK, B_BLK), 2 * B_BLK), mask2
            q = q_ref[rows, :]
            d_out = do_ref[rows, :]
            l_col = _lane_pick(lj_ref[rows, :], h)
            d_col = _lane_pick(dl_ref[rows, :], h)
            sc = _dot(q, k_ref[win, :], 1, 1) * scale
            p = jnp.exp(jnp.where(mask, sc - l_col, -1e30))
            d_p = _dot(d_out, v_ref[win, :], 1, 1)
            d_s = p * (d_p - d_col) * scale
            dq_ref[rows, :] = _dot(d_s, k_ref[win, :], 1, 0)
            dk_ref[win, :] += _dot(d_s, q, 0, 0)
            dv_ref[win, :] += _dot(p, d_out, 0, 0)

        block(0, True)
        if nb > 1:
            lax.fori_loop(1, nb, lambda i, c: (block(i, False), c)[1], 0)

    head = lambda off: pl.BlockSpec((None, ln, B_DH), lambda s, h: (s, 0, off + h))
    small = pl.BlockSpec((None, ln, B_HEADS), lambda s, h: (s, 0, 0))
    return pl.pallas_call(
        body, name=name, grid=(ns, B_HEADS),
        in_specs=[head(0), head(B_HEADS), head(2 * B_HEADS), head(0), small, small],
        out_specs=[head(0)] * 3,
        out_shape=[jax.ShapeDtypeStruct((ns, ln, B_W), F32)] * 3,
        compiler_params=_params(("parallel", "parallel")),
    )(qkv, qkv, qkv, d_o, lse_joint, delta)


B_ROWS = 2048


def _attn_schedule(nb, sb, block):
    way = 4

    def run(items):
        for at in range(0, len(items), way):
            _round_robin([block(*it) for it in items[at:at + way]])

    run([(si, 0, True) for si in range(sb)])
    if nb == 1:
        return
    per = max(1, way // sb)
    lead = 1 + (nb - 1) % per
    if lead > 1:
        run([(si, i, False) for i in range(1, lead) for si in range(sb)])

    def step(it, carry):
        run([(si, lead + it * per + u, False) for u in range(per) for si in range(sb)])
        return carry

    lax.fori_loop(0, (nb - lead) // per, step, 0)


def _attn_rows(i, first):
    if first:
        return pl.ds(0, B_BLK), pl.ds(0, B_BLK)
    rows = pl.ds(pl.multiple_of(i * B_BLK, B_BLK), B_BLK)
    return rows, pl.ds(pl.multiple_of((i - 1) * B_BLK, B_BLK), 2 * B_BLK)


def _attn_fwd(qkv, name):
    ns, ln, _ = qkv.shape
    nb = ln // B_BLK
    sb = B_ROWS // ln
    scale = B_DH ** -0.5

    def body(q_ref, k_ref, v_ref, o_ref, lse_ref):
        h = pl.program_id(1)
        mask1, mask2 = _attn_masks()
        lane = lax.broadcasted_iota(jnp.int32, (B_BLK, B_HEADS), 1)

        @pl.when(h == 0)
        def _():
            lse_ref[...] = jnp.zeros_like(lse_ref)

        def block(si, i, first):
            rows, win = _attn_rows(i, first)
            mask = mask1 if first else mask2
            sc = jnp.where(mask, _dot(q_ref[si, rows, :], k_ref[si, win, :], 1, 1) * scale, -1e30)
            yield
            m = jnp.max(sc, axis=1, keepdims=True)
            p = jnp.exp(sc - m)
            l = jnp.sum(p, axis=1, keepdims=True)
            pv = _dot(p, v_ref[si, win, :], 1, 0)
            yield
            o_ref[si, rows, :] = pv / l
            lse_ref[si, rows, :] = jnp.where(lane == h, m + jnp.log(l), lse_ref[si, rows, :])

        _attn_schedule(nb, sb, block)

    head = lambda off: pl.BlockSpec((sb, ln, B_DH), lambda s, h: (s, 0, off + h))
    return pl.pallas_call(
        body, name=name, grid=(ns // sb, B_HEADS),
        in_specs=[head(0), head(B_HEADS), head(2 * B_HEADS)],
        out_specs=[head(0), pl.BlockSpec((sb, ln, B_HEADS), lambda s, h: (s, 0, 0))],
        out_shape=[jax.ShapeDtypeStruct((ns, ln, B_W), F32), jax.ShapeDtypeStruct((ns, ln, B_HEADS), F32)],
        compiler_params=_params(("parallel", "arbitrary")),
    )(qkv, qkv, qkv)


def _attn_bwd(qkv, d_o, lse_joint, delta, name):
    ns, ln, _ = qkv.shape
    nb = ln // B_BLK
    sb = B_ROWS // ln
    scale = B_DH ** -0.5

    def body(q_ref, k_ref, v_ref, do_ref, lj_ref, dl_ref, dq_ref, dk_ref, dv_ref):
        h = pl.program_id(1)
        mask1, mask2 = _attn_masks()
        dk_ref[...] = jnp.zeros_like(dk_ref)
        dv_ref[...] = jnp.zeros_like(dv_ref)

        def block(si, i, first):
            rows, win = _attn_rows(i, first)
            mask = mask1 if first else mask2
            q = q_ref[si, rows, :]
            d_out = do_ref[si, rows, :]
            l_col = _lane_pick(lj_ref[si, rows, :], h)
            d_col = _lane_pick(dl_ref[si, rows, :], h)
            sc = _dot(q, k_ref[si, win, :], 1, 1) * scale
            d_p = _dot(d_out, v_ref[si, win, :], 1, 1)
            yield
            p = jnp.exp(jnp.where(mask, sc - l_col, -1e30))
            d_s = p * (d_p - d_col) * scale
            d_q = _dot(d_s, k_ref[si, win, :], 1, 0)
            d_k = _dot(d_s, q, 0, 0)
            d_v = _dot(p, d_out, 0, 0)
            yield
            dq_ref[si, rows, :] = d_q
            dk_ref[si, win, :] += d_k
            dv_ref[si, win, :] += d_v

        _attn_schedule(nb, sb, block)

    head = lambda off: pl.BlockSpec((sb, ln, B_DH), lambda s, h: (s, 0, off + h))
    small = pl.BlockSpec((sb, ln, B_HEADS), lambda s, h: (s, 0, 0))
    return pl.pallas_call(
        body, name=name, grid=(ns // sb, B_HEADS),
        in_specs=[head(0), head(B_HEADS), head(2 * B_HEADS), head(0), small, small],
        out_specs=[head(0)] * 3,
        out_shape=[jax.ShapeDtypeStruct((ns, ln, B_W), F32)] * 3,
        compiler_params=_params(("parallel", "parallel")),
    )(qkv, qkv, qkv, d_o, lse_joint, delta)


def _merge_weights(lse_refs):
    ls = [r[...] for r in lse_refs]
    m = jnp.maximum(jnp.maximum(ls[0], ls[1]), ls[2])
    es = [jnp.exp(l - m) for l in ls]
    tot = es[0] + es[1] + es[2]
    return [e / tot for e in es], m + jnp.log(tot)


def _merge_fwd(outs, lses, proj0, tm=256):
    t = outs[0].shape[0]

    def body(o0, o1, o2, l0, l1, l2, z_ref, og_ref):
        wts, _ = _merge_weights((l0, l1, l2))
        for h in range(B_HEADS):
            cols = slice(h * B_DH, (h + 1) * B_DH)
            o = (wts[0][:, h:h + 1] * o0[:, cols] + wts[1][:, h:h + 1] * o1[:, cols]
                 + wts[2][:, h:h + 1] * o2[:, cols])
            og_ref[:, cols] = (o * _silu(z_ref[:, cols])).astype(BF16)

    wide = pl.BlockSpec((tm, B_W), lambda i: (i, 0))
    small = pl.BlockSpec((tm, B_HEADS), lambda i: (i, 0))
    return pl.pallas_call(
        body, name="merge_fwd", grid=(t // tm,),
        in_specs=[wide] * 3 + [small] * 3 + [pl.BlockSpec((tm, B_W), lambda i: (i, 3))],
        out_specs=wide, out_shape=jax.ShapeDtypeStruct((t, B_W), BF16),
        compiler_params=_params(("parallel",)),
    )(*outs, *lses, proj0)


def _merge_bwd(outs, lses, proj0, d_og, tm=256):
    t = outs[0].shape[0]

    def body(o0, o1, o2, l0, l1, l2, z_ref, dog_ref, do_ref, lj_ref, dl_ref, dz_ref):
        wts, lj = _merge_weights((l0, l1, l2))
        lj_ref[...] = lj
        lane = lax.broadcasted_iota(jnp.int32, (tm, B_HEADS), 1)
        delta = jnp.zeros((tm, B_HEADS), F32)
        for h in range(B_HEADS):
            cols = slice(h * B_DH, (h + 1) * B_DH)
            o = (wts[0][:, h:h + 1] * o0[:, cols] + wts[1][:, h:h + 1] * o1[:, cols]
                 + wts[2][:, h:h + 1] * o2[:, cols])
            z = z_ref[:, cols]
            d_g = dog_ref[:, cols]
            d_out = d_g * _silu(z)
            dz_ref[:, cols] = (d_g * o * _dsilu(z)).astype(BF16)
            do_ref[:, cols] = d_out.astype(BF16)
            delta = jnp.where(lane == h, jnp.sum(d_out * o, axis=1, keepdims=True), delta)
        dl_ref[...] = delta

    wide = pl.BlockSpec((tm, B_W), lambda i: (i, 0))
    small = pl.BlockSpec((tm, B_HEADS), lambda i: (i, 0))
    return pl.pallas_call(
        body, name="merge_bwd", grid=(t // tm,),
        in_specs=[wide] * 3 + [small] * 3 + [pl.BlockSpec((tm, B_W), lambda i: (i, 3)), wide],
        out_specs=[wide, small, small, wide],
        out_shape=[jax.ShapeDtypeStruct((t, B_W), BF16), jax.ShapeDtypeStruct((t, B_HEADS), F32),
                   jax.ShapeDtypeStruct((t, B_HEADS), F32), jax.ShapeDtypeStruct((t, B_W), BF16)],
        compiler_params=_params(("parallel",)),
    )(*outs, *lses, proj0, d_og)


def _adamw(w, g, m, v, name):
    r, c = w.shape
    tr = r
    for cand in (256, 128, 64, 32, 16, 8):
        if r % cand == 0:
            tr = cand
            break

    def body(w_ref, g_ref, m_ref, v_ref, d_ref, nm_ref, nv_ref):
        gv = g_ref[...]
        nm = ADAM_B1 * m_ref[...] + (1.0 - ADAM_B1) * gv
        nv = ADAM_B2 * v_ref[...] + (1.0 - ADAM_B2) * (gv * gv)
        m_hat = nm / (1.0 - ADAM_B1 ** ADAM_STEP)
        v_hat = nv / (1.0 - ADAM_B2 ** ADAM_STEP)
        d_ref[...] = -ADAM_LR * (m_hat / (jnp.sqrt(v_hat) + ADAM_EPS) + ADAM_WD * w_ref[...])
        nm_ref[...] = nm
        nv_ref[...] = nv

    blk = pl.BlockSpec((tr, c), lambda i: (i, 0))
    return pl.pallas_call(
        body, name=name, grid=(r // tr,), in_specs=[blk] * 4, out_specs=[blk] * 3,
        out_shape=[jax.ShapeDtypeStruct((r, c), F32)] * 3,
        compiler_params=_params(("parallel",)),
    )(w, g, m, v)


def _adam_update(w, gv, m, v):
    nm = ADAM_B1 * m + (1.0 - ADAM_B1) * gv
    nv = ADAM_B2 * v + (1.0 - ADAM_B2) * (gv * gv)
    m_hat = nm / (1.0 - ADAM_B1 ** ADAM_STEP)
    v_hat = nv / (1.0 - ADAM_B2 ** ADAM_STEP)
    return -ADAM_LR * (m_hat / (jnp.sqrt(v_hat) + ADAM_EPS) + ADAM_WD * w), nm, nv


def _adamw_shard(w, mine, theirs, m, v, half_index, name, tr=128):
    _, r, c = w.shape
    nhb = (r // 2) // tr

    def body(c_ref, w_ref, mine_ref, theirs_ref, m_ref, v_ref, g_ref, d_ref, nm_ref, nv_ref):
        is_mine = (pl.program_id(0) // nhb) == c_ref[0]
        gv = jnp.where(is_mine, mine_ref[...], theirs_ref[...])
        d, nm, nv = _adam_update(w_ref[...], gv, m_ref[...], v_ref[...])
        g_ref[...] = gv
        d_ref[...] = d
        nm_ref[...] = nm
        nv_ref[...] = nv

    full = pl.BlockSpec((None, tr, c), lambda i, cc: (0, i, 0))
    half = pl.BlockSpec((tr, c), lambda i, cc: (i % nhb, 0))
    return pl.pallas_call(
        body, name=name,
        grid_spec=pltpu.PrefetchScalarGridSpec(
            num_scalar_prefetch=1, grid=(2 * nhb,),
            in_specs=[full, half, half, full, full], out_specs=[full] * 4),
        out_shape=[jax.ShapeDtypeStruct(w.shape, F32)] * 4,
        compiler_params=_params(("parallel",), 40 * 1024 * 1024),
    )(half_index, w, mine, theirs, m, v)


def _pair_sum(own, other, half_index, name, tr=256):
    _, r, c = own.shape
    rh = r // 2
    tr = min(tr, rh)
    nrb = rh // tr

    def body(c_ref, own_ref, oth_ref, out_ref):
        out_ref[...] = (own_ref[...] + oth_ref[...].astype(F32)).astype(BF16)

    return pl.pallas_call(
        body, name=name,
        grid_spec=pltpu.PrefetchScalarGridSpec(
            num_scalar_prefetch=1, grid=(N_CHIPS, nrb),
            in_specs=[pl.BlockSpec((None, tr, c), lambda k, i, cc: (k, cc[0] * nrb + i, 0)),
                      pl.BlockSpec((None, tr, c), lambda k, i, cc: (k, i, 0))],
            out_specs=pl.BlockSpec((None, tr, c), lambda k, i, cc: (k, i, 0))),
        out_shape=jax.ShapeDtypeStruct((N_CHIPS, rh, c), BF16),
        compiler_params=_params(("parallel", "parallel")),
    )(half_index, own, other)


def _chip_sum(sums, others, chip_index, name, tr=256):
    _, r, c = sums.shape
    tr = min(tr, r)

    def body(k_ref, own_ref, oth_ref, out_ref):
        acc = own_ref[...].astype(F32)
        for j in range(N_CHIPS - 1):
            acc = acc + oth_ref[j].astype(F32)
        out_ref[...] = acc

    return pl.pallas_call(
        body, name=name,
        grid_spec=pltpu.PrefetchScalarGridSpec(
            num_scalar_prefetch=1, grid=(r // tr,),
            in_specs=[pl.BlockSpec((None, tr, c), lambda i, kk: (kk[0], i, 0)),
                      pl.BlockSpec((N_CHIPS - 1, tr, c), lambda i, kk: (0, i, 0))],
            out_specs=pl.BlockSpec((tr, c), lambda i, kk: (i, 0))),
        out_shape=jax.ShapeDtypeStruct((r, c), F32),
        compiler_params=_params(("parallel",)),
    )(chip_index, sums, others)


HBM = pl.BlockSpec(memory_space=pltpu.HBM)


def _place():
    x, y, c = lax.axis_index("x"), lax.axis_index("y"), lax.axis_index("c")
    chips = [(1 - x, y), (x, 1 - y), (1 - x, 1 - y)]
    return x, y, c, chips


def _weight_allgather(shards, conv_shard):
    na = len(shards)

    def body(*refs):
        ins = refs[:na]
        conv_in = refs[na]
        outs = refs[na + 1:2 * na + 1]
        conv_out = refs[2 * na + 1]
        send, recv, fsend, frecv, csend, crecv = refs[2 * na + 2:]
        x, y, c, chips = _place()
        me = 2 * x + y
        sib = (x, y, 1 - c)
        first, conv_cp = [], []
        for i in range(na):
            rh = ins[i].shape[0] // 2
            mine = pl.ds(c * rh, rh)
            for j, (px, py) in enumerate(chips):
                cp = pltpu.make_async_remote_copy(
                    src_ref=ins[i].at[mine], dst_ref=outs[i].at[me, mine],
                    send_sem=send.at[3 * i + j], recv_sem=recv.at[3 * i + j],
                    device_id=(px, py, c), device_id_type=MESH)
                cp.start()
                first.append(cp)
        for j, (px, py) in enumerate(chips):
            cp = pltpu.make_async_remote_copy(
                src_ref=conv_in, dst_ref=conv_out.at[me], send_sem=csend.at[j], recv_sem=crecv.at[j],
                device_id=(px, py, c), device_id_type=MESH)
            cp.start()
            conv_cp.append(cp)
        passed = []
        for i in range(na):
            rh = ins[i].shape[0] // 2
            mine = pl.ds(c * rh, rh)
            for j, (px, py) in enumerate(chips):
                slot = outs[i].at[2 * px + py, mine]
                pltpu.make_async_remote_copy(
                    src_ref=slot, dst_ref=slot, send_sem=send.at[3 * i + j], recv_sem=recv.at[3 * i + j],
                    device_id=(px, py, c), device_id_type=MESH).wait_recv()
                cp = pltpu.make_async_remote_copy(
                    src_ref=slot, dst_ref=slot, send_sem=fsend.at[3 * i + j], recv_sem=frecv.at[3 * i + j],
                    device_id=sib, device_id_type=MESH)
                cp.start()
                passed.append(cp)
        for i in range(na):
            rh = ins[i].shape[0] // 2
            theirs = pl.ds((1 - c) * rh, rh)
            for j, (px, py) in enumerate(chips):
                slot = outs[i].at[2 * px + py, theirs]
                pltpu.make_async_remote_copy(
                    src_ref=slot, dst_ref=slot, send_sem=fsend.at[3 * i + j], recv_sem=frecv.at[3 * i + j],
                    device_id=sib, device_id_type=MESH).wait_recv()
        for j, (px, py) in enumerate(chips):
            slot = conv_out.at[2 * px + py]
            pltpu.make_async_remote_copy(
                src_ref=slot, dst_ref=slot, send_sem=csend.at[j], recv_sem=crecv.at[j],
                device_id=(px, py, c), device_id_type=MESH).wait_recv()
        for cp in first + passed + conv_cp:
            cp.wait_send()

    out_shape = [jax.ShapeDtypeStruct((N_CHIPS,) + s.shape, s.dtype) for s in shards]
    out_shape.append(jax.ShapeDtypeStruct((N_CHIPS,) + conv_shard.shape, conv_shard.dtype))
    res = pl.pallas_call(
        body, name="weight_allgather",
        in_specs=[HBM] * (na + 1), out_specs=[HBM] * (na + 1), out_shape=out_shape,
        scratch_shapes=[pltpu.SemaphoreType.DMA((3 * na,)), pltpu.SemaphoreType.DMA((3 * na,)),
                        pltpu.SemaphoreType.DMA((3 * na,)), pltpu.SemaphoreType.DMA((3 * na,)),
                        pltpu.SemaphoreType.DMA((3,)), pltpu.SemaphoreType.DMA((3,))],
    )(*shards, conv_shard)
    my_chip = 2 * lax.axis_index("x") + lax.axis_index("y")
    pick = lambda got, own: [jnp.where(my_chip == k, own, got[k]) for k in range(N_CHIPS)]
    return [pick(g, s) for g, s in zip(res[:na], shards)], pick(res[na], conv_shard)


def _sibling_swap_halves(grads, name):
    na = len(grads)

    def body(*refs):
        ins, outs = refs[:na], refs[na:2 * na]
        send, recv = refs[2 * na:]
        x, y, c, _ = _place()
        sib = (x, y, 1 - c)
        cps = []
        for i in range(na):
            rh = ins[i].shape[1] // 2
            cp = pltpu.make_async_remote_copy(
                src_ref=ins[i].at[:, pl.ds((1 - c) * rh, rh), :], dst_ref=outs[i],
                send_sem=send.at[i], recv_sem=recv.at[i], device_id=sib, device_id_type=MESH)
            cp.start()
            cps.append(cp)
        for cp in cps:
            cp.wait()

    out_shape = [jax.ShapeDtypeStruct((g.shape[0], g.shape[1] // 2, g.shape[2]), g.dtype) for g in grads]
    return pl.pallas_call(
        body, name=name, in_specs=[HBM] * na, out_specs=[HBM] * na, out_shape=out_shape,
        scratch_shapes=[pltpu.SemaphoreType.DMA((na,)), pltpu.SemaphoreType.DMA((na,))],
    )(*grads)


def _chip_exchange(sums):
    na = len(sums)

    def body(*refs):
        ins, outs = refs[:na], refs[na:2 * na]
        send, recv = refs[2 * na:]
        x, y, c, chips = _place()
        cps = []
        for i in range(na):
            for j, (px, py) in enumerate(chips):
                cp = pltpu.make_async_remote_copy(
                    src_ref=ins[i].at[2 * px + py], dst_ref=outs[i].at[j],
                    send_sem=send.at[3 * i + j], recv_sem=recv.at[3 * i + j],
                    device_id=(px, py, c), device_id_type=MESH)
                cp.start()
                cps.append(cp)
        for i in range(na):
            for j, (px, py) in enumerate(chips):
                slot = outs[i].at[j]
                pltpu.make_async_remote_copy(
                    src_ref=slot, dst_ref=slot, send_sem=send.at[3 * i + j], recv_sem=recv.at[3 * i + j],
                    device_id=(px, py, c), device_id_type=MESH).wait_recv()
        for cp in cps:
            cp.wait_send()

    out_shape = [jax.ShapeDtypeStruct((3,) + s.shape[1:], s.dtype) for s in sums]
    return pl.pallas_call(
        body, name="grad_chip_exchange", in_specs=[HBM] * na, out_specs=[HBM] * na, out_shape=out_shape,
        scratch_shapes=[pltpu.SemaphoreType.DMA((3 * na,)), pltpu.SemaphoreType.DMA((3 * na,))],
    )(*sums)


def _sibling_swap_whole(halves):
    na = len(halves)

    def body(*refs):
        ins, outs = refs[:na], refs[na:2 * na]
        send, recv = refs[2 * na:]
        x, y, c, _ = _place()
        cps = []
        for i in range(na):
            cp = pltpu.make_async_remote_copy(
                src_ref=ins[i], dst_ref=outs[i], send_sem=send.at[i], recv_sem=recv.at[i],
                device_id=(x, y, 1 - c), device_id_type=MESH)
            cp.start()
            cps.append(cp)
        for cp in cps:
            cp.wait()

    out_shape = [jax.ShapeDtypeStruct(h.shape, h.dtype) for h in halves]
    return pl.pallas_call(
        body, name="grad_sibling_join", in_specs=[HBM] * na, out_specs=[HBM] * na, out_shape=out_shape,
        scratch_shapes=[pltpu.SemaphoreType.DMA((na,)), pltpu.SemaphoreType.DMA((na,))],
    )(*halves)


SEM = pl.BlockSpec(memory_space=pltpu.SEMAPHORE)
ANY = pl.BlockSpec(memory_space=pl.ANY)
EFFECT = pltpu.SideEffectType.DATAFLOW_SIDE_EFFECTING


def _split_copy_start(name, plan, srcs, lands, after):
    ns, nl = len(srcs), len(lands)

    def body(*refs):
        src_refs, land_refs = refs[:ns], refs[ns:ns + nl]
        send, recv = refs[ns + nl + 1], refs[ns + nl + 2]
        token = refs[-1]
        outgoing, _ = plan(src_refs, land_refs)
        for src, dst, dev, si, ri in outgoing:
            pltpu.make_async_remote_copy(src_ref=src, dst_ref=dst, send_sem=send.at[si], recv_sem=recv.at[ri],
                                         device_id=dev, device_id_type=MESH).start()
        token[...] = jnp.zeros_like(token)

    n_out, n_in = plan.counts
    thru = [pltpu.HBM(a.shape, a.dtype) for a in list(srcs) + list(lands)]
    res = pl.pallas_call(
        body, name=name,
        out_shape=[pltpu.SemaphoreType.DMA((n_out,)), pltpu.SemaphoreType.DMA((n_in,))] + thru
        + [jax.ShapeDtypeStruct((8, 128), F32)],
        in_specs=[HBM] * (ns + nl) + [ANY],
        out_specs=[SEM, SEM] + [HBM] * (ns + nl) + [pl.BlockSpec(memory_space=pltpu.VMEM)],
        input_output_aliases={i: 2 + i for i in range(ns + nl)},
        compiler_params=pltpu.CompilerParams(has_side_effects=EFFECT),
    )(*[pltpu.with_memory_space_constraint(a, pltpu.HBM) for a in list(srcs) + list(lands)], after)
    return res[0], res[1], res[2:2 + ns], res[2 + ns:2 + ns + nl], res[-1]


def _split_copy_wait(name, plan, send, recv, srcs, lands, after):
    ns, nl = len(srcs), len(lands)

    def body(*refs):
        src_refs, land_refs = refs[:ns], refs[ns:ns + nl]
        send_ref, recv_ref = refs[ns + nl], refs[ns + nl + 1]
        outgoing, arrivals = plan(src_refs, land_refs)
        for src, dst, dev, si, ri in outgoing:
            pltpu.make_async_remote_copy(src_ref=src, dst_ref=dst, send_sem=send_ref.at[si], recv_sem=recv_ref.at[ri],
                                         device_id=dev, device_id_type=MESH).wait_send()
        for view, ri in arrivals:
            pltpu.make_async_remote_copy(src_ref=view, dst_ref=view, send_sem=send_ref.at[0], recv_sem=recv_ref.at[ri],
                                         device_id=_place()[:3], device_id_type=MESH).wait_recv()

    thru = [pltpu.HBM(a.shape, a.dtype) for a in list(srcs) + list(lands)]
    res = pl.pallas_call(
        body, name=name, out_shape=thru,
        in_specs=[HBM] * (ns + nl) + [SEM, SEM, ANY], out_specs=[HBM] * (ns + nl),
        input_output_aliases={i: i for i in range(ns + nl)},
        compiler_params=pltpu.CompilerParams(has_side_effects=EFFECT),
    )(*srcs, *lands, send, recv, after)
    return res[:ns], res[ns:]


def _gather_plan(n_arrays):
    def plan(src_refs, land_refs):
        x, y, c, chips = _place()
        me = 2 * x + y
        outgoing, arrivals = [], []
        for i in range(n_arrays):
            rh = src_refs[i].shape[0] // 2
            mine = pl.ds(c * rh, rh)
            for j, (px, py) in enumerate(chips):
                for delta in range(2):
                    tc = c ^ delta
                    outgoing.append((src_refs[i].at[mine], land_refs[i].at[me, mine], (px, py, tc),
                                     6 * i + 2 * j + delta, 6 * i + 2 * j + delta))
                    theirs = pl.ds(tc * rh, rh)
                    arrivals.append((land_refs[i].at[2 * px + py, theirs], 6 * i + 2 * j + delta))
        return outgoing, arrivals

    plan.counts = (6 * n_arrays, 6 * n_arrays)
    return plan


def _exchange_plan(n_arrays):
    def plan(src_refs, land_refs):
        x, y, c, chips = _place()
        outgoing, arrivals = [], []
        for i in range(n_arrays):
            for j, (px, py) in enumerate(chips):
                outgoing.append((src_refs[i].at[2 * px + py], land_refs[i].at[j], (px, py, c), 3 * i + j, 3 * i + j))
                arrivals.append((land_refs[i].at[j], 3 * i + j))
        return outgoing, arrivals

    plan.counts = (3 * n_arrays, 3 * n_arrays)
    return plan


def _small_allreduce(vec):
    r, cdim = vec.shape
    n_dev = 8

    def body(v_ref, out_ref, buf, send, recv):
        x, y, c, _ = _place()
        me = 4 * x + 2 * y + c
        buf[me] = v_ref[...]
        cps = []
        for k in range(1, n_dev):
            dx, dy, dc = (k >> 2) & 1, (k >> 1) & 1, k & 1
            peer = (x ^ dx, y ^ dy, c ^ dc)
            cp = pltpu.make_async_remote_copy(
                src_ref=v_ref, dst_ref=buf.at[me], send_sem=send.at[k - 1], recv_sem=recv.at[k - 1],
                device_id=peer, device_id_type=MESH)
            cp.start()
            cps.append(cp)
        for k in range(1, n_dev):
            dx, dy, dc = (k >> 2) & 1, (k >> 1) & 1, k & 1
            src = 4 * (x ^ dx) + 2 * (y ^ dy) + (c ^ dc)
            slot = buf.at[src]
            pltpu.make_async_remote_copy(
                src_ref=slot, dst_ref=slot, send_sem=send.at[k - 1], recv_sem=recv.at[k - 1],
                device_id=(x ^ dx, y ^ dy, c ^ dc), device_id_type=MESH).wait_recv()
        for cp in cps:
            cp.wait_send()
        acc = buf[0]
        for k in range(1, n_dev):
            acc = acc + buf[k]
        out_ref[...] = acc

    vm = pl.BlockSpec(memory_space=pltpu.VMEM)
    return pl.pallas_call(
        body, name="small_allreduce", in_specs=[vm], out_specs=vm,
        out_shape=jax.ShapeDtypeStruct((r, cdim), F32),
        scratch_shapes=[pltpu.VMEM((n_dev, r, cdim), F32), pltpu.SemaphoreType.DMA((n_dev - 1,)),
                        pltpu.SemaphoreType.DMA((n_dev - 1,))],
    )(vec)


def _a_cols_to_head_major(w):
    lead = w.shape[:-1]
    q = w[..., :A_QK].reshape(lead + (A_HEADS, A_DK))
    k = w[..., A_QK:2 * A_QK].reshape(lead + (A_HEADS, A_DK))
    v = w[..., 2 * A_QK:2 * A_QK + A_VW].reshape(lead + (A_HEADS, A_DV))
    z = w[..., 2 * A_QK + A_VW:].reshape(lead + (A_HEADS, A_DV))
    return jnp.concatenate([q, k, v, z], axis=-1).reshape(lead + (A_HEADS * A_HEAD_COLS,))


def _a_cols_from_head_major(w):
    lead = w.shape[:-1]
    w = w.reshape(lead + (A_HEADS, A_HEAD_COLS))
    parts = [w[..., :A_DK], w[..., A_DK:2 * A_DK], w[..., 2 * A_DK:2 * A_DK + A_DV], w[..., 2 * A_DK + A_DV:]]
    return jnp.concatenate([p.reshape(lead + (-1,)) for p in parts], axis=-1)


def _conv_cols_to_head_major(w):
    lead = w.shape[:-1]
    q = w[..., :A_QK].reshape(lead + (A_HEADS, A_DK))
    k = w[..., A_QK:2 * A_QK].reshape(lead + (A_HEADS, A_DK))
    v = w[..., 2 * A_QK:].reshape(lead + (A_HEADS, A_DV))
    return jnp.concatenate([q, k, v], axis=-1).reshape(lead + (A_HEADS * A_CONV_COLS,))


def _conv_cols_from_head_major(w):
    lead = w.shape[:-1]
    w = w.reshape(lead + (A_HEADS, A_CONV_COLS))
    parts = [w[..., :A_DK], w[..., A_DK:2 * A_DK], w[..., 2 * A_DK:]]
    return jnp.concatenate([p.reshape(lead + (-1,)) for p in parts], axis=-1)


def _to_stream(a, bn, d):
    rest = a.shape[1:]
    s = a.shape[0] // bn
    a = a.reshape((bn, s // d, d) + rest)
    a = jnp.swapaxes(a, 1, 2)
    return a.reshape((bn * d, s // d) + rest)


def _from_stream(a, bn, d):
    rest = a.shape[2:]
    ln = a.shape[1]
    a = a.reshape((bn, d, ln) + rest)
    a = jnp.swapaxes(a, 1, 2)
    return a.reshape((bn * ln * d,) + rest)


B_SUB = 512
B_SHARD_BLOCKS = (3 * B_GROUPS * B_W + B_W) // N_CHIPS // B_SUB


def _b_block(gi, jj):
    nb = (B_GROUPS * (jj // 2) + gi) * 2 + jj % 2
    return nb // B_SHARD_BLOCKS, nb % B_SHARD_BLOCKS


def _shard_major(g, ncols):
    r = g.shape[0]
    return jnp.swapaxes(g.reshape(r, N_CHIPS, ncols), 0, 1)


def _pack_rows(items):
    rows, offs = [], []
    at = 0
    for a in items:
        flat = a.reshape(-1).astype(F32)
        nr = -(-flat.shape[0] // 1024) * 8
        flat = jnp.pad(flat, (0, nr * 128 - flat.shape[0]))
        rows.append(flat.reshape(nr, 128))
        offs.append((at, nr, a.shape))
        at += nr
    return jnp.concatenate(rows, axis=0), offs


def _unpack_rows(packed, offs):
    out = []
    for at, nr, shape in offs:
        size = int(np.prod(shape)) if len(shape) else 1
        out.append(packed[at:at + nr].reshape(-1)[:size].reshape(shape))
    return out


def _local_step(x, positions, loss_target, norm_g, wa_in, conv_w, a_log, a_dt_bias, a_norm_g,
                b_q_norm_g, b_k_norm_g, start_token, late_weights, b_grads_ready, a_grads_ready):
    bn, s, d = x.shape
    t = bn * s
    n_chunks = s // A_CHUNK
    wa_main = _a_cols_to_head_major(wa_in[:, :A_MAIN])
    wa_tail = jnp.pad(wa_in[:, A_MAIN:], ((0, 0), (0, 128 - 2 * A_HEADS)))
    cw_hm = _conv_cols_to_head_major(conv_w)

    x0 = x.reshape(t, d)
    h0 = _rms_fwd(x0, norm_g[0:1] + start_token, "rms0_fwd")
    proj_a = _matmul(h0, wa_main, "nn", F32, "a_in_main")
    tail_a = _matmul(h0, wa_tail, "nn", F32, "a_in_tail")
    tail_t = jnp.swapaxes(tail_a[:, :2 * A_HEADS].reshape(bn, s, 2 * A_HEADS), 1, 2)
    tail_t = tail_t.reshape(bn, 2 * A_HEADS, n_chunks, A_CHUNK)
    beta, gc = _gdn_prep(tail_t, a_log[0], a_dt_bias[0])
    proj_a3 = proj_a.reshape(bn, s, A_MAIN)
    og_a, oraw_a, states, t_mats = _gdn_fwd(proj_a3, cw_hm, beta, gc, a_norm_g)
    wa_out, wb_in, wb_out = late_weights(og_a)
    b_cols = [4 * B_W] + [3 * B_W] * (B_GROUPS - 1)
    x1 = _matmul(og_a.reshape(t, A_VW), wa_out, "nn", F32, "a_out", res=x0, tk=2048)

    h1 = _rms_fwd(x1, norm_g[1:2], "rms1_fwd")
    inv_freq = ROPE_THETA ** (-jnp.arange(0, ROPE_DIMS, 2, dtype=F32) / ROPE_DIMS)
    freq_row = jnp.concatenate([inv_freq, inv_freq, jnp.zeros((128 - ROPE_DIMS,), F32)]).reshape(1, 128)
    posf = jnp.broadcast_to(positions.astype(F32).reshape(t, 1), (t, 128))
    tabs = _rope_tables(posf, freq_row)
    h1_s, tabs_s, proj_b, qkv_b, o_b, lse_b = [], [], [], [], [], []
    for gi, dil in enumerate(B_DIL):
        hs = h1 if dil == 1 else _to_stream(h1, bn, dil).reshape(t, d)
        ts = tabs if dil == 1 else [_to_stream(tb, bn, dil).reshape(t, 128) for tb in tabs]
        pj = _matmul(hs, wb_in, "nn", F32, f"b_in_g{gi}", tm=2048, tn=B_SUB, n=b_cols[gi], b_spec=pl.BlockSpec(
            (None, d, B_SUB), lambda i, j, kk, gi=gi: (_b_block(gi, j)[0], kk, _b_block(gi, j)[1])))
        qkv = _qk_prep(pj, *ts, b_q_norm_g[0, gi:gi + 1], b_k_norm_g[0, gi:gi + 1], f"qk_prep_g{gi}")
        o_s, lse_s = _attn_fwd(qkv.reshape(bn * dil, s // dil, 3 * B_W), f"attn_fwd_g{gi}")
        h1_s.append(hs), tabs_s.append(ts), proj_b.append(pj), qkv_b.append(qkv)
        o_b.append(o_s.reshape(t, B_W) if dil == 1 else _from_stream(o_s, bn, dil))
        lse_b.append(lse_s.reshape(t, B_HEADS) if dil == 1 else _from_stream(lse_s, bn, dil))
    og_b = _merge_fwd(o_b, lse_b, proj_b[0])
    x2 = _matmul(og_b, wb_out, "nn", F32, "b_out", res=x1)

    d_x2, loss_parts = _loss_grad(x2, loss_target.reshape(t, d))
    loss_local = jnp.sum(loss_parts)

    d_x2b = d_x2.astype(BF16)
    g_wb_out = _matmul(og_b.T, d_x2b, "nn", F32, "b_out_dw")
    d_og_b = _matmul(d_x2b, wb_out, "nt", F32, "b_out_dx")
    d_o, lse_joint, delta, d_z = _merge_bwd(o_b, lse_b, proj_b[0], d_og_b)
    d_h1, g_qn, g_kn = [], [], []
    g_wb_in = lax.empty(wb_in.shape, F32)
    for gi, dil in enumerate(B_DIL):
        if dil == 1:
            do_s, lj_s, dl_s = d_o, lse_joint, delta
        else:
            do_s, lj_s, dl_s = (_to_stream(a, bn, dil).reshape(t, -1) for a in (d_o, lse_joint, delta))
        ns, ln = bn * dil, s // dil
        dq, dk, dv = _attn_bwd(qkv_b[gi].reshape(ns, ln, 3 * B_W), do_s.reshape(ns, ln, B_W),
                               lj_s.reshape(ns, ln, B_HEADS), dl_s.reshape(ns, ln, B_HEADS), f"attn_bwd_g{gi}")
        d_pj, d_gain = _qk_prep_bwd(proj_b[gi], *tabs_s[gi], b_q_norm_g[0, gi:gi + 1], b_k_norm_g[0, gi:gi + 1],
                                    dq.reshape(t, B_W), dk.reshape(t, B_W), dv.reshape(t, B_W),
                                    d_z if gi == 0 else None, f"qk_prep_bwd_g{gi}")
        g_wb_in = _matmul(h1_s[gi].T, d_pj, "nn", F32, f"b_in_dw_g{gi}", tn=B_SUB, tk=2048, into=(g_wb_in, pl.BlockSpec(
            (None, d, B_SUB), lambda i, j, kk, gi=gi: (_b_block(gi, j)[0], i, _b_block(gi, j)[1]))))
        dh = _matmul(d_pj, wb_in, "nt", F32, f"b_in_dx_g{gi}", tm=2048, tk=B_SUB, n=d, b_spec=pl.BlockSpec(
            (None, d, B_SUB), lambda i, j, kk, gi=gi: (_b_block(gi, kk)[0], j, _b_block(gi, kk)[1])))
        d_h1.append(dh if dil == 1 else _from_stream(dh.reshape(ns, ln, d), bn, dil))
        g_qn.append(d_gain[0]), g_kn.append(d_gain[1])
    d_x1, g_norm1 = _rms_bwd(x1, norm_g[1:2], d_h1, d_x2, "rms1_bwd")

    d_x1b = (d_x1 + b_grads_ready(g_wb_in, g_wb_out)).astype(BF16)
    g_wa_out = _matmul(og_a.reshape(t, A_VW).T, d_x1b, "nn", F32, "a_out_dw")
    d_og_a = _matmul(d_x1b, wa_out, "nt", F32, "a_out_dx")
    d_pa, d_gc, d_beta, d_cw, d_ng = _gdn_bwd(proj_a3, cw_hm, beta, gc, a_norm_g, oraw_a, states, t_mats,
                                              d_og_a.reshape(bn, s, A_VW))
    d_tail_t, d_alog, d_dtb = _gdn_prep_bwd(tail_t, a_log[0], a_dt_bias[0], d_gc, d_beta)
    d_tail = jnp.swapaxes(d_tail_t.reshape(bn, 2 * A_HEADS, s), 1, 2).reshape(t, 2 * A_HEADS)
    d_tail = jnp.pad(d_tail, ((0, 0), (0, 128 - 2 * A_HEADS))).astype(BF16)
    d_pa = d_pa.reshape(t, A_MAIN)
    h0_t = h0.T
    g_wa_main = _matmul(h0_t, d_pa, "nn", F32, "a_in_dw_main")
    g_wa_tail = _matmul(h0_t, d_tail, "nn", F32, "a_in_dw_tail")
    g_wa_in = jnp.concatenate([_a_cols_from_head_major(g_wa_main), g_wa_tail[:, :2 * A_HEADS]], axis=1)
    a_token = a_grads_ready(g_wa_in, g_wa_out)
    d_h0 = _matmul(d_pa, wa_main, "nt", F32, "a_in_dx_main")
    d_h0t = _matmul(d_tail + a_token.astype(BF16), wa_tail, "nt", F32, "a_in_dx_tail")
    d_x0, g_norm0 = _rms_bwd(x0, norm_g[0:1], [d_h0, d_h0t], d_x1, "rms0_bwd")

    gfull = {
        "norm_g": jnp.concatenate([g_norm0, g_norm1], axis=0), "a_w_in": g_wa_in,
        "a_conv_w": _conv_cols_from_head_major(jnp.sum(d_cw, axis=0)),
        "a_log": jnp.sum(d_alog[:, :, 0], axis=0), "a_dt_bias": jnp.sum(d_dtb[:, :, 0], axis=0),
        "a_norm_g": jnp.sum(d_ng[:, :, 0, :], axis=(0, 1)), "a_w_out": g_wa_out, "b_w_in": g_wb_in,
        "b_q_norm_g": jnp.stack(g_qn), "b_k_norm_g": jnp.stack(g_kn), "b_w_out": g_wb_out}
    return loss_local, d_x0.reshape(bn, s, d), gfull


def kernel(x, positions, norm_g, a_w_in, a_conv_w, a_log, a_dt_bias, a_norm_g, a_w_out, b_w_in, b_q_norm_g, b_k_norm_g, b_w_out, loss_target, m_norm_g, m_a_w_in, m_a_conv_w, m_a_log, m_a_dt_bias, m_a_norm_g, m_a_w_out, m_b_w_in, m_b_q_norm_g, m_b_k_norm_g, m_b_w_out, v_norm_g, v_a_w_in, v_a_conv_w, v_a_log, v_a_dt_bias, v_a_norm_g, v_a_w_out, v_b_w_in, v_b_q_norm_g, v_b_k_norm_g, v_b_w_out):
    d = x.shape[2]
    my_c = lax.axis_index("c")
    my_chip = 2 * lax.axis_index("x") + lax.axis_index("y")

    half_index = jnp.reshape(my_c, (1,)).astype(jnp.int32)
    chip_index = jnp.reshape(my_chip, (1,)).astype(jnp.int32)
    (ga_in,), g_conv = _weight_allgather([a_w_in[0].astype(BF16)], a_conv_w[0])
    wa_in = jnp.concatenate(ga_in, axis=1)
    conv_w = jnp.concatenate(g_conv, axis=1)

    late_shards = [a_w_out[0].astype(BF16), b_w_in[0].astype(BF16), b_w_out[0].astype(BF16)]
    late_lands = [lax.dynamic_update_slice(lax.empty((N_CHIPS,) + s.shape, BF16), s[None], (my_chip, 0, 0))
                  for s in late_shards]
    gather = _gather_plan(len(late_shards))
    ag_send, ag_recv, ag_srcs, ag_lands, ag_token = _split_copy_start(
        "late_weights_start", gather, late_shards, late_lands, conv_w)

    def late_weights(after):
        _, (ga_out, gb_in, gb_out) = _split_copy_wait(
            "late_weights_wait", gather, ag_send, ag_recv, ag_srcs, ag_lands, after)
        return ga_out.reshape(A_VW, d), gb_in, gb_out.reshape(B_W, d)

    def reduce_to_chip_sums(mats, tag):
        recv_sib = _sibling_swap_halves([g.astype(BF16) for g in mats], f"grad_{tag}_sibling_swap")
        return [_pair_sum(g, r, half_index, f"grad_{tag}_pair_sum_{i}") for i, (g, r) in enumerate(zip(mats, recv_sib))]

    exchange = _exchange_plan(2)
    pending = {}

    def start_exchange(tag, mats):
        sums = reduce_to_chip_sums(mats, tag)
        lands = [lax.empty((N_CHIPS - 1,) + s.shape[1:], BF16) for s in sums]
        pending[tag] = _split_copy_start(f"grad_{tag}_exchange_start", exchange, sums, lands, chip_index)
        return pending[tag][4][0, 0]

    def finish_exchange(tag, after):
        send, recv, srcs, lands, _ = pending[tag]
        return _split_copy_wait(f"grad_{tag}_exchange_wait", exchange, send, recv, srcs, lands, after)

    def b_grads_ready(g_wb_in, g_wb_out):
        return start_exchange("b", [g_wb_in, g_wb_out.reshape(N_CHIPS, -1, d)])

    def a_grads_ready(g_wa_in, g_wa_out):
        return start_exchange("a", [_shard_major(g_wa_in, a_w_in.shape[2]), g_wa_out.reshape(N_CHIPS, -1, d)])

    loss_local, d_x0, gfull = _local_step(x, positions, loss_target, norm_g, wa_in, conv_w, a_log, a_dt_bias,
                                          a_norm_g, b_q_norm_g, b_k_norm_g, ag_token[0, 0], late_weights,
                                          b_grads_ready, a_grads_ready)

    small = [gfull["norm_g"], gfull["a_conv_w"], gfull["a_log"], gfull["a_dt_bias"], gfull["a_norm_g"],
             gfull["b_q_norm_g"], gfull["b_k_norm_g"], loss_local]
    packed, offs = _pack_rows(small)
    reduced = _small_allreduce(packed)
    g_norm, g_conv_all, g_alog, g_dtb, g_ang, g_q, g_k, loss = _unpack_rows(reduced, offs)
    g_conv_mine = lax.dynamic_slice_in_dim(g_conv_all, my_chip * a_conv_w.shape[2], a_conv_w.shape[2], axis=1)

    b_sums, b_received = finish_exchange("b", d_x0)
    a_sums, a_received = finish_exchange("a", reduced)
    chip_sums = [a_sums[0], a_sums[1], b_sums[0], b_sums[1]]
    received = [a_received[0], a_received[1], b_received[0], b_received[1]]
    halves = [_chip_sum(s, r, chip_index, f"grad_chip_sum_{i}") for i, (s, r) in enumerate(zip(chip_sums, received))]
    theirs = _sibling_swap_whole(halves)
    big = ("a_w_in", "a_w_out", "b_w_in", "b_w_out")
    big_halves = dict(zip(big, zip(halves, theirs)))

    grads = {
        "norm_g": g_norm, "a_conv_w": g_conv_mine[None], "a_log": g_alog[None], "a_dt_bias": g_dtb[None],
        "a_norm_g": g_ang[None], "b_q_norm_g": g_q[None], "b_k_norm_g": g_k[None]}
    weights = {"norm_g": norm_g, "a_w_in": a_w_in, "a_conv_w": a_conv_w, "a_log": a_log, "a_dt_bias": a_dt_bias,
               "a_norm_g": a_norm_g, "a_w_out": a_w_out, "b_w_in": b_w_in, "b_q_norm_g": b_q_norm_g,
               "b_k_norm_g": b_k_norm_g, "b_w_out": b_w_out}
    m_in = {"norm_g": m_norm_g, "a_w_in": m_a_w_in, "a_conv_w": m_a_conv_w, "a_log": m_a_log,
            "a_dt_bias": m_a_dt_bias, "a_norm_g": m_a_norm_g, "a_w_out": m_a_w_out, "b_w_in": m_b_w_in,
            "b_q_norm_g": m_b_q_norm_g, "b_k_norm_g": m_b_k_norm_g, "b_w_out": m_b_w_out}
    v_in = {"norm_g": v_norm_g, "a_w_in": v_a_w_in, "a_conv_w": v_a_conv_w, "a_log": v_a_log,
            "a_dt_bias": v_a_dt_bias, "a_norm_g": v_a_norm_g, "a_w_out": v_a_w_out, "b_w_in": v_b_w_in,
            "b_q_norm_g": v_b_q_norm_g, "b_k_norm_g": v_b_k_norm_g, "b_w_out": v_b_w_out}
    names = list(weights)

    delta_w, new_m, new_v = {}, {}, {}
    for nm in big:
        mine, other = big_halves[nm]
        grads[nm], delta_w[nm], new_m[nm], new_v[nm] = _adamw_shard(
            weights[nm], mine, other, m_in[nm], v_in[nm], half_index, f"adamw_{nm}")
    small_names = [nm for nm in names if nm not in big]
    packs = [_pack_rows([src[nm] for nm in small_names]) for src in (weights, grads, m_in, v_in)]
    offs = packs[0][1]
    dl, m2, v2 = _adamw(packs[0][0], packs[1][0], packs[2][0], packs[3][0], "adamw_small")
    for nm, a, b, c2 in zip(small_names, _unpack_rows(dl, offs), _unpack_rows(m2, offs), _unpack_rows(v2, offs)):
        delta_w[nm], new_m[nm], new_v[nm] = a, b, c2

    return (loss, d_x0, *[grads[nm] for nm in names], *[delta_w[nm] for nm in names],
            *[new_m[nm] for nm in names], *[new_v[nm] for nm in names])
```

```python
import jax
import jax.numpy as jnp
import numpy as np
from jax import lax
from jax.experimental import pallas as pl
from jax.experimental.pallas import tpu as pltpu

F32 = jnp.float32
BF16 = jnp.bfloat16
MESH = pl.DeviceIdType.MESH

EPS = 1e-6
D_MODEL = 1024
A_HEADS = 8
A_DK = 128
A_DV = 256
A_QK = A_HEADS * A_DK
A_VW = A_HEADS * A_DV
A_MAIN = 2 * A_QK + 2 * A_VW
A_HEAD_COLS = 2 * A_DK + 2 * A_DV
A_CONV_COLS = 2 * A_DK + A_DV
A_CHUNK = 64
A_CONV = 4
B_GROUPS = 3
B_HEADS = 8
B_DH = 128
B_W = B_HEADS * B_DH
B_DIL = (1, 4, 16)
B_BLK = 128
ROPE_THETA = 500000.0
ROPE_DIMS = B_DH // 4
ADAM_LR, ADAM_B1, ADAM_B2, ADAM_EPS, ADAM_WD, ADAM_STEP = 0.001, 0.9, 0.999, 1e-08, 0.01, 10
N_CHIPS = 4
VMEM_BIG = 56 * 1024 * 1024


def _params(sem=None, vmem=None):
    return pltpu.CompilerParams(dimension_semantics=sem, vmem_limit_bytes=vmem)


def _dot(a, b, ca, cb):
    return lax.dot_general(a.astype(BF16), b.astype(BF16), (((ca,), (cb,)), ((), ())),
                           preferred_element_type=F32)


def _split3(a):
    hi = a.astype(BF16)
    r = a - hi.astype(F32)
    mid = r.astype(BF16)
    lo = (r - mid.astype(F32)).astype(BF16)
    return hi, mid, lo


def _sigmoid(y):
    return 1.0 / (1.0 + jnp.exp(-y))


def _silu(y):
    return y * _sigmoid(y)


def _dsilu(y):
    s = _sigmoid(y)
    return s * (1.0 + y * (1.0 - s))


def _matmul(a, b, mode, out_dtype, name, res=None, tm=1024, tn=1024, tk=1024, n=None, b_spec=None, into=None):
    m, k = a.shape
    if n is None:
        n = b.shape[1] if mode == "nn" else b.shape[0]
    tm, tn, tk = min(tm, m), min(tn, n), min(tk, k)
    assert m % tm == 0 and n % tn == 0 and k % tk == 0, (name, a.shape, b.shape)
    nk = k // tk
    dims = {"nn": ((1,), (0,)), "nt": ((1,), (1,))}[mode]

    def body(*refs):
        a_ref, b_ref = refs[0], refs[1]
        r_ref = refs[2] if res is not None else None
        o_ref = refs[2 + (res is not None) + (into is not None)]
        prod = lax.dot_general(a_ref[...], b_ref[...], (dims, ((), ())), preferred_element_type=F32)

        def finish(r):
            if res is not None:
                r = r + r_ref[...]
            o_ref[...] = r.astype(out_dtype)

        if nk == 1:
            finish(prod)
            return
        acc = refs[-1]
        kk = pl.program_id(2)

        @pl.when(kk == 0)
        def _():
            acc[...] = prod

        @pl.when((kk > 0) & (kk < nk - 1))
        def _():
            acc[...] += prod

        @pl.when(kk == nk - 1)
        def _():
            finish(acc[...] + prod)

    a_spec = pl.BlockSpec((tm, tk), lambda i, j, kk: (i, kk))
    if b_spec is None and mode == "nt":
        b_spec = pl.BlockSpec((tn, tk), lambda i, j, kk: (j, kk))
    elif b_spec is None:
        b_spec = pl.BlockSpec((tk, tn), lambda i, j, kk: (kk, j))
    in_specs = [a_spec, b_spec]
    args = [a, b]
    if res is not None:
        in_specs.append(pl.BlockSpec((tm, tn), lambda i, j, kk: (i, j)))
        args.append(res)
    out_spec = pl.BlockSpec((tm, tn), lambda i, j, kk: (i, j))
    out_shape = jax.ShapeDtypeStruct((m, n), out_dtype)
    aliases = {}
    if into is not None:
        assert res is None
        buf, out_spec = into
        out_shape = jax.ShapeDtypeStruct(buf.shape, buf.dtype)
        in_specs.append(ANY)
        args.append(buf)
        aliases = {2: 0}
    return pl.pallas_call(
        body, name=name, grid=(m // tm, n // tn, nk),
        in_specs=in_specs, out_specs=out_spec, out_shape=out_shape, input_output_aliases=aliases,
        scratch_shapes=[pltpu.VMEM((tm, tn), F32)] if nk > 1 else [],
        compiler_params=_params(("parallel", "parallel", "arbitrary"), 48 * 1024 * 1024),
    )(*args)


def _rms_fwd(x, g, name, tm=256):
    t, d = x.shape

    def body(x_ref, g_ref, h_ref):
        xv = x_ref[...]
        r = lax.rsqrt(jnp.mean(xv * xv, axis=-1, keepdims=True) + EPS)
        h_ref[...] = (xv * r * g_ref[...]).astype(BF16)

    return pl.pallas_call(
        body, name=name, grid=(t // tm,),
        in_specs=[pl.BlockSpec((tm, d), lambda i: (i, 0)), pl.BlockSpec((1, d), lambda i: (0, 0))],
        out_specs=pl.BlockSpec((tm, d), lambda i: (i, 0)),
        out_shape=jax.ShapeDtypeStruct((t, d), BF16),
        compiler_params=_params(("parallel",)),
    )(x, g)


def _rms_bwd(x, g, dhs, dres, name, tm=256):
    t, d = x.shape
    n_dh = len(dhs)

    def body(*refs):
        x_ref, g_ref = refs[0], refs[1]
        dh_refs = refs[2:2 + n_dh]
        dres_ref, dx_ref, dg_ref = refs[2 + n_dh:]
        i = pl.program_id(0)

        @pl.when(i == 0)
        def _():
            dg_ref[...] = jnp.zeros_like(dg_ref)

        xv = x_ref[...]
        r = lax.rsqrt(jnp.mean(xv * xv, axis=-1, keepdims=True) + EPS)
        xh = xv * r
        dh = dh_refs[0][...]
        for ref in dh_refs[1:]:
            dh = dh + ref[...]
        dg_ref[0:1, :] += jnp.sum(dh * xh, axis=0, keepdims=True)
        dxh = dh * g_ref[...]
        dx = r * (dxh - xh * jnp.mean(dxh * xh, axis=-1, keepdims=True))
        dx_ref[...] = dx + dres_ref[...]

    row = pl.BlockSpec((tm, d), lambda i: (i, 0))
    dx, dg = pl.pallas_call(
        body, name=name, grid=(t // tm,),
        in_specs=[row, pl.BlockSpec((1, d), lambda i: (0, 0))] + [row] * n_dh + [row],
        out_specs=[row, pl.BlockSpec((8, d), lambda i: (0, 0))],
        out_shape=[jax.ShapeDtypeStruct((t, d), F32), jax.ShapeDtypeStruct((8, d), F32)],
        compiler_params=_params(("arbitrary",)),
    )(x, g, *dhs, dres)
    return dx, dg[0:1]


def _loss_grad(y, target, name="loss_grad", tm=256):
    t, d = y.shape
    nb = t // tm

    def body(y_ref, t_ref, dy_ref, part_ref):
        e = y_ref[...] - t_ref[...]
        dy_ref[...] = e * (1.0 / d)
        s = jnp.sum(jnp.sum(e * e, axis=1, keepdims=True), axis=0, keepdims=True) * (0.5 / d)
        part_ref[...] = jnp.broadcast_to(s, (8, 128))

    row = pl.BlockSpec((tm, d), lambda i: (i, 0))
    dy, part = pl.pallas_call(
        body, name=name, grid=(nb,), in_specs=[row, row],
        out_specs=[row, pl.BlockSpec((None, 8, 128), lambda i: (i, 0, 0))],
        out_shape=[jax.ShapeDtypeStruct((t, d), F32), jax.ShapeDtypeStruct((nb, 8, 128), F32)],
        compiler_params=_params(("parallel",)),
    )(y, target)
    return dy, part[:, 0, 0]


def _softplus(x):
    t = jnp.exp(-jnp.abs(x))
    return jnp.maximum(x, 0.0) + jnp.where(t < 1e-3, t * (1.0 - 0.5 * t), jnp.log(1.0 + t))


def _tri(rows_le_cols):
    r = lax.broadcasted_iota(jnp.int32, (A_CHUNK, A_CHUNK), 0)
    c = lax.broadcasted_iota(jnp.int32, (A_CHUNK, A_CHUNK), 1)
    return jnp.where((r <= c) if rows_le_cols else (r >= c), 1.0, 0.0).astype(BF16)


def _dot_exact_rhs(a, ones_bf16):
    dn = (((1,), (0,)), ((), ()))
    hi, mid, lo = _split3(a)
    out = lax.dot_general(hi, ones_bf16, dn, preferred_element_type=F32)
    out = out + lax.dot_general(mid, ones_bf16, dn, preferred_element_type=F32)
    return out + lax.dot_general(lo, ones_bf16, dn, preferred_element_type=F32)


def _gdn_prep(tail_t, a_log, dt_bias):
    bn, _, n, c = tail_t.shape

    def body(t_ref, alog_ref, dtb_ref, beta_ref, gc_ref):
        upper = _tri(True)
        for h in range(A_HEADS):
            beta_ref[h] = _sigmoid(t_ref[h])
            ea = jnp.exp(jnp.full((n, c), alog_ref[h], F32))
            g = -ea * _softplus(t_ref[A_HEADS + h] + dtb_ref[h])
            gc_ref[h] = _dot_exact_rhs(g, upper)

    smem = pl.BlockSpec(memory_space=pltpu.SMEM)
    blk = pl.BlockSpec((None, A_HEADS, n, c), lambda b: (b, 0, 0, 0))
    return pl.pallas_call(
        body, name="gdn_prep", grid=(bn,),
        in_specs=[pl.BlockSpec((None, 2 * A_HEADS, n, c), lambda b: (b, 0, 0, 0)), smem, smem],
        out_specs=[blk, blk],
        out_shape=[jax.ShapeDtypeStruct((bn, A_HEADS, n, c), F32)] * 2,
        compiler_params=_params(("parallel",)),
    )(tail_t, a_log, dt_bias)


def _gdn_prep_bwd(tail_t, a_log, dt_bias, d_gc, d_beta):
    bn, _, n, c = tail_t.shape

    def body(t_ref, alog_ref, dtb_ref, dgc_ref, dbeta_ref, dt_ref, dal_ref, ddt_ref):
        lower = _tri(False)
        for h in range(A_HEADS):
            beta = _sigmoid(t_ref[h])
            dt_ref[h] = dbeta_ref[h] * beta * (1.0 - beta)
            dg = _dot_exact_rhs(dgc_ref[h], lower)
            ea = jnp.exp(jnp.full((n, c), alog_ref[h], F32))
            xa = t_ref[A_HEADS + h] + dtb_ref[h]
            g = -ea * _softplus(xa)
            dxa = -ea * dg * _sigmoid(xa)
            dt_ref[A_HEADS + h] = dxa
            s1 = jnp.sum(jnp.sum(g * dg, axis=1, keepdims=True), axis=0, keepdims=True)
            s2 = jnp.sum(jnp.sum(dxa, axis=1, keepdims=True), axis=0, keepdims=True)
            dal_ref[h:h + 1, :] = jnp.broadcast_to(s1, (1, 128))
            ddt_ref[h:h + 1, :] = jnp.broadcast_to(s2, (1, 128))

    smem = pl.BlockSpec(memory_space=pltpu.SMEM)
    blk8 = pl.BlockSpec((None, A_HEADS, n, c), lambda b: (b, 0, 0, 0))
    blk16 = pl.BlockSpec((None, 2 * A_HEADS, n, c), lambda b: (b, 0, 0, 0))
    sm = pl.BlockSpec((None, A_HEADS, 128), lambda b: (b, 0, 0))
    return pl.pallas_call(
        body, name="gdn_prep_bwd", grid=(bn,),
        in_specs=[blk16, smem, smem, blk8, blk8],
        out_specs=[blk16, sm, sm],
        out_shape=[jax.ShapeDtypeStruct((bn, 2 * A_HEADS, n, c), F32),
                   jax.ShapeDtypeStruct((bn, A_HEADS, 128), F32),
                   jax.ShapeDtypeStruct((bn, A_HEADS, 128), F32)],
        compiler_params=_params(("parallel",)),
    )(tail_t, a_log, dt_bias, d_gc, d_beta)


HALO = 8


def _conv_taps(xw, w):
    y = w[A_CONV - 1:A_CONV, :] * xw
    for j in range(1, A_CONV):
        y = y + w[A_CONV - 1 - j:A_CONV - j, :] * pltpu.roll(xw, j, 0)
    return y[HALO:, :]


def _row_to_col(row, eye):
    c = eye.shape[0]
    return jnp.sum(jnp.where(eye, jnp.broadcast_to(row, (c, c)), 0.0), axis=1, keepdims=True)


def _col_to_row(col, eye):
    c = eye.shape[0]
    return jnp.sum(jnp.where(eye, jnp.broadcast_to(col, (c, c)), 0.0), axis=0, keepdims=True)


def _unit_lower_inverse(a, ri, ci):
    eye = jnp.where(ri == ci, 1.0, 0.0)
    a8 = jnp.where((ri >> 3) == (ci >> 3), a, 0.0)
    a2 = _dot(a8, a8, 1, 0)
    yield
    a4 = _dot(a2, a2, 1, 0)
    t = eye - a8
    t = t + _dot(t, a2, 1, 0)
    yield
    t = t + _dot(t, a4, 1, 0)
    yield
    for sh in (3, 4, 5):
        off = jnp.where(((ri >> (sh + 1)) == (ci >> (sh + 1))) & ((ri >> sh) != (ci >> sh)), a, 0.0)
        left = _dot(t, off, 1, 0)
        yield
        t = t - _dot(left, t, 1, 0)
        yield
    return t


def _round_robin(gens):
    live = list(gens)
    while live:
        nxt = []
        for g in live:
            try:
                next(g)
                nxt.append(g)
            except StopIteration:
                pass
        live = nxt


def _gdn_chunk_core(q, k, v, g_row, b_row, t_mat, ri, ci):
    eye = ri == ci
    g_col = _row_to_col(g_row, eye)
    b_col = _row_to_col(b_row, eye)
    causal = ri >= ci
    strict = ri > ci
    dec = jnp.where(causal, jnp.exp(jnp.where(causal, g_col - g_row, 0.0)), 0.0)
    gam = jnp.exp(g_col)
    g_last = g_row[:, A_CHUNK - 1:A_CHUNK]
    gam_last = jnp.exp(g_last)
    e = jnp.exp(g_last - g_col)
    kb = k * b_col
    bv = v * b_col
    kbg = kb * gam
    q16, k16, kb16 = q.astype(BF16), k.astype(BF16), kb.astype(BF16)
    kk = _dot(kb16, k16, 1, 1)
    p = _dot(q16, k16, 1, 1) * dec
    yield
    a_mat = jnp.where(strict, kk * dec, 0.0)
    if t_mat is None:
        t_mat = yield from _unit_lower_inverse(a_mat, ri, ci)
    t16 = t_mat.astype(BF16)
    u = _dot(t16, bv, 1, 0)
    w = _dot(t16, kbg, 1, 0)
    yield
    return dict(eye=eye, g_col=g_col, b_col=b_col, dec=dec, strict=strict, causal=causal, gam=gam,
                gam_last=gam_last, e=e, kb=kb, bv=bv, kbg=kbg, a_mat=a_mat, t_mat=t_mat, u=u, w=w, p=p,
                qg=q * gam, kd=k * e, q16=q16, k16=k16, kb16=kb16, t16=t16)


A_SEQ_BLK = 512
A_BLK_CHUNKS = A_SEQ_BLK // A_CHUNK


def _gdn_halo(proj_hm):
    bn, s, w = proj_hm.shape
    last = proj_hm.reshape(bn, s // A_SEQ_BLK, A_SEQ_BLK, w)[:, :, A_SEQ_BLK - HALO:, :]
    return jnp.concatenate([jnp.zeros((bn, 1, HALO, w), proj_hm.dtype), last[:, :-1]], axis=1)


def _gdn_window(x_ref, halo_ref, ci, first, lo):
    if first:
        return jnp.concatenate([halo_ref[:, lo:lo + A_CONV_COLS], x_ref[0:A_CHUNK, lo:lo + A_CONV_COLS]], axis=0)
    start = pl.multiple_of(ci * A_CHUNK - HALO, HALO)
    return x_ref[pl.ds(start, A_CHUNK + HALO), lo:lo + A_CONV_COLS]


def _gdn_chunk_prep(xw, cw):
    y = _conv_taps(xw, cw)
    a = _silu(y)
    aq, ak, v = a[:, 0:A_DK], a[:, A_DK:2 * A_DK], a[:, 2 * A_DK:]
    rq = lax.rsqrt(jnp.sum(aq * aq, axis=1, keepdims=True) + EPS)
    rk = lax.rsqrt(jnp.sum(ak * ak, axis=1, keepdims=True) + EPS)
    return dict(xw=xw, y=y, aq=aq, ak=ak, rq=rq, rk=rk, q=aq * rq * (A_DK ** -0.5), k=ak * rk, v=v)


def _gdn_fwd(proj_hm, cw_hm, beta, gc, norm_g, hp=4):
    bn, s, _ = proj_hm.shape
    n = s // A_CHUNK
    nsb = s // A_SEQ_BLK
    halo = _gdn_halo(proj_hm)

    def body(x_ref, halo_ref, cw_ref, beta_ref, gc_ref, ng_ref, og_ref, oraw_ref, st_ref, t_ref, state):
        ri = lax.broadcasted_iota(jnp.int32, (A_CHUNK, A_CHUNK), 0)
        ci_ = lax.broadcasted_iota(jnp.int32, (A_CHUNK, A_CHUNK), 1)
        ng = ng_ref[...]

        @pl.when(pl.program_id(2) == 0)
        def _():
            state[...] = jnp.zeros_like(state)

        def one_head(hh, ci, first, rows):
            lo = hh * A_HEAD_COLS
            cw = cw_ref[:, hh * A_CONV_COLS:(hh + 1) * A_CONV_COLS]
            cin = _gdn_chunk_prep(_gdn_window(x_ref, halo_ref, ci, first, lo), cw)
            core = yield from _gdn_chunk_core(cin["q"], cin["k"], cin["v"], gc_ref[hh, pl.ds(ci, 1), :],
                                              beta_ref[hh, pl.ds(ci, 1), :], None, ri, ci_)
            st = state[hh]
            st_ref[hh, ci] = st
            t_ref[hh, ci] = core["t_mat"]
            st16 = st.astype(BF16)
            vn = core["u"] - _dot(core["w"], st16, 1, 0)
            qs = _dot(core["qg"], st16, 1, 0)
            yield
            vn16 = vn.astype(BF16)
            o = qs + _dot(core["p"], vn16, 1, 0)
            state[hh] = st * core["gam_last"] + _dot(core["kd"], vn16, 0, 0)
            yield
            ocols = slice(hh * A_DV, (hh + 1) * A_DV)
            oraw_ref[rows, ocols] = o
            r = lax.rsqrt(jnp.mean(o * o, axis=1, keepdims=True) + EPS)
            z = x_ref[rows, lo + A_CONV_COLS:lo + A_HEAD_COLS]
            og_ref[rows, ocols] = (o * r * ng * _silu(z)).astype(BF16)

        def chunk(ci, first):
            rows = pl.ds(0 if first else pl.multiple_of(ci * A_CHUNK, A_CHUNK), A_CHUNK)
            _round_robin([one_head(hh, ci, first, rows) for hh in range(hp)])

        chunk(0, True)
        lax.fori_loop(1, A_BLK_CHUNKS, lambda i, c: (chunk(i, False), c)[1], 0)

    small = pl.BlockSpec((None, hp, A_BLK_CHUNKS, A_CHUNK), lambda b, h, j: (b, h, j, 0))
    return pl.pallas_call(
        body, name="gdn_fwd", grid=(bn, A_HEADS // hp, nsb),
        in_specs=[pl.BlockSpec((None, A_SEQ_BLK, hp * A_HEAD_COLS), lambda b, h, j: (b, j, h)),
                  pl.BlockSpec((None, None, HALO, hp * A_HEAD_COLS), lambda b, h, j: (b, j, 0, h)),
                  pl.BlockSpec((A_CONV, hp * A_CONV_COLS), lambda b, h, j: (0, h)),
                  small, small,
                  pl.BlockSpec((1, A_DV), lambda b, h, j: (0, 0))],
        out_specs=[pl.BlockSpec((None, A_SEQ_BLK, hp * A_DV), lambda b, h, j: (b, j, h)),
                   pl.BlockSpec((None, A_SEQ_BLK, hp * A_DV), lambda b, h, j: (b, j, h)),
                   pl.BlockSpec((None, hp, A_BLK_CHUNKS, A_DK, A_DV), lambda b, h, j: (b, h, j, 0, 0)),
                   pl.BlockSpec((None, hp, A_BLK_CHUNKS, A_CHUNK, A_CHUNK), lambda b, h, j: (b, h, j, 0, 0))],
        out_shape=[jax.ShapeDtypeStruct((bn, s, A_VW), BF16),
                   jax.ShapeDtypeStruct((bn, s, A_VW), F32),
                   jax.ShapeDtypeStruct((bn, A_HEADS, n, A_DK, A_DV), F32),
                   jax.ShapeDtypeStruct((bn, A_HEADS, n, A_CHUNK, A_CHUNK), F32)],
        scratch_shapes=[pltpu.VMEM((hp, A_DK, A_DV), F32)],
        compiler_params=_params(("parallel", "parallel", "arbitrary"), VMEM_BIG),
    )(proj_hm, halo, cw_hm, beta, gc, norm_g)


def _gdn_bwd(proj_hm, cw_hm, beta, gc, norm_g, oraw, states, t_mats, dog, hp=4):
    bn, s, _ = proj_hm.shape
    n = s // A_CHUNK
    nsb = s // A_SEQ_BLK
    halo = _gdn_halo(proj_hm)

    def body(x_ref, halo_ref, cw_ref, beta_ref, gc_ref, ng_ref, oraw_ref, st_ref, t_ref, dog_ref,
             dx_ref, dgc_ref, dbeta_ref, dcw_ref, dng_ref, dstate, dy_next):
        ri = lax.broadcasted_iota(jnp.int32, (A_CHUNK, A_CHUNK), 0)
        ci_ = lax.broadcasted_iota(jnp.int32, (A_CHUNK, A_CHUNK), 1)
        lane = lax.broadcasted_iota(jnp.int32, (1, A_CHUNK), 1)
        ng = ng_ref[...]

        @pl.when(pl.program_id(2) == 0)
        def _():
            dstate[...] = jnp.zeros_like(dstate)
            dy_next[...] = jnp.zeros_like(dy_next)
            dcw_ref[...] = jnp.zeros_like(dcw_ref)
            dng_ref[...] = jnp.zeros_like(dng_ref)

        def one_head(hh, ci, first, rows):
            lo = hh * A_HEAD_COLS
            ccols = slice(hh * A_CONV_COLS, (hh + 1) * A_CONV_COLS)
            ocols = slice(hh * A_DV, (hh + 1) * A_DV)
            cw = cw_ref[:, ccols]
            cin = _gdn_chunk_prep(_gdn_window(x_ref, halo_ref, ci, first, lo), cw)
            q, k, v = cin["q"], cin["k"], cin["v"]
            cr = yield from _gdn_chunk_core(q, k, v, gc_ref[hh, pl.ds(ci, 1), :], beta_ref[hh, pl.ds(ci, 1), :],
                                            t_ref[hh, ci], ri, ci_)
            eye, dec, gam, e = cr["eye"], cr["dec"], cr["gam"], cr["e"]
            b_col, t_mat, u, w, p = cr["b_col"], cr["t_mat"], cr["u"], cr["w"], cr["p"]
            st = st_ref[hh, ci]
            ds_out = dstate[hh]

            o = oraw_ref[rows, ocols]
            z = x_ref[rows, lo + A_CONV_COLS:lo + A_HEAD_COLS]
            d_og = dog_ref[rows, ocols]
            r = lax.rsqrt(jnp.mean(o * o, axis=1, keepdims=True) + EPS)
            oh = o * r
            d_on = d_og * _silu(z)
            dz = d_og * oh * ng * _dsilu(z)
            dng_ref[hh, 0:1, :] += jnp.sum(d_on * oh, axis=0, keepdims=True)
            d_oh = d_on * ng
            d_o = r * (d_oh - oh * jnp.mean(d_oh * oh, axis=1, keepdims=True))

            st16, ds16, do16, w16 = st.astype(BF16), ds_out.astype(BF16), d_o.astype(BF16), w.astype(BF16)
            q16, k16, t16 = cr["q16"], cr["k16"], cr["t16"]
            vn = u - _dot(w16, st16, 1, 0)
            d_vn = _dot(p, do16, 0, 0) + _dot(cr["kd"], ds16, 1, 0)
            d_qg = _dot(do16, st16, 1, 1)
            qgdo = _dot(cr["qg"], do16, 0, 0)
            yield
            vn16, dvn16 = vn.astype(BF16), d_vn.astype(BF16)
            d_p = jnp.where(cr["causal"], _dot(do16, vn16, 1, 1), 0.0)
            d_kd = _dot(vn16, ds16, 1, 1)
            d_gam_last = jnp.sum(jnp.sum(st * ds_out, axis=1, keepdims=True), axis=0, keepdims=True)
            d_w = -_dot(dvn16, st16, 1, 1)
            dstate[hh] = qgdo + ds_out * cr["gam_last"] - _dot(w16, dvn16, 0, 0)
            d_bv = _dot(t16, dvn16, 0, 0)
            yield
            d_kbg = _dot(t16, d_w, 0, 0)
            n_p = (d_p * dec).astype(BF16)
            d_q = _dot(n_p, k16, 1, 0) + d_qg * gam
            npq = _dot(n_p, q16, 0, 0)
            yield
            d_a = jnp.where(cr["strict"], -(_dot(d_bv, u, 1, 1) + _dot(d_kbg, w16, 1, 1)), 0.0)
            yield
            m_a = (d_a * dec).astype(BF16)
            d_kb = _dot(m_a, k16, 1, 0) + d_kbg * gam
            d_k = (_dot(m_a, cr["kb16"], 0, 0) + npq + d_kd * e + d_kb * b_col)
            yield
            d_v = d_bv * b_col
            d_beta_col = (jnp.sum(d_bv * v, axis=1, keepdims=True)
                          + jnp.sum(d_kb * k, axis=1, keepdims=True))
            gterm = d_a * cr["a_mat"] + d_p * p
            d_e = jnp.sum(d_kd * k, axis=1, keepdims=True) * e
            d_g_col = (jnp.sum(gterm, axis=1, keepdims=True)
                       + (jnp.sum(d_qg * q, axis=1, keepdims=True)
                          + jnp.sum(d_kbg * cr["kb"], axis=1, keepdims=True)) * gam
                       - d_e)
            d_g_last = jnp.sum(d_e, axis=0, keepdims=True) + d_gam_last * cr["gam_last"]
            d_g_row = (_col_to_row(d_g_col, eye) - jnp.sum(gterm, axis=0, keepdims=True)
                       + jnp.where(lane == A_CHUNK - 1, d_g_last, 0.0))
            dgc_ref[hh, pl.ds(ci, 1), :] = d_g_row
            dbeta_ref[hh, pl.ds(ci, 1), :] = _col_to_row(d_beta_col, eye)

            qh = cin["aq"] * cin["rq"]
            kh = cin["ak"] * cin["rk"]
            d_qh = d_q * (A_DK ** -0.5)
            d_aq = cin["rq"] * (d_qh - qh * jnp.sum(d_qh * qh, axis=1, keepdims=True))
            d_ak = cin["rk"] * (d_k - kh * jnp.sum(d_k * kh, axis=1, keepdims=True))
            d_y = jnp.concatenate([d_aq, d_ak, d_v], axis=1) * _dsilu(cin["y"])
            xw = cin["xw"]
            dyw = jnp.concatenate([d_y, dy_next[hh]], axis=0)
            d_x = cw[A_CONV - 1:A_CONV, :] * dyw
            for j in range(1, A_CONV):
                d_x = d_x + cw[A_CONV - 1 - j:A_CONV - j, :] * pltpu.roll(dyw, A_CHUNK + HALO - j, 0)
            dy_pad = jnp.concatenate([jnp.zeros((HALO, A_CONV_COLS), F32), d_y], axis=0)
            for j in range(A_CONV):
                xs = xw if j == 0 else pltpu.roll(xw, j, 0)
                dcw_ref[A_CONV - 1 - j:A_CONV - j, ccols] += jnp.sum(dy_pad * xs, axis=0, keepdims=True)
            dy_next[hh] = d_y[0:HALO, :]
            dx_ref[rows, lo:lo + A_CONV_COLS] = d_x[0:A_CHUNK, :].astype(BF16)
            dx_ref[rows, lo + A_CONV_COLS:lo + A_HEAD_COLS] = dz.astype(BF16)

        def chunk(ci, first):
            rows = pl.ds(0 if first else pl.multiple_of(ci * A_CHUNK, A_CHUNK), A_CHUNK)
            _round_robin([one_head(hh, ci, first, rows) for hh in range(hp)])

        lax.fori_loop(0, A_BLK_CHUNKS - 1, lambda i, c: (chunk(A_BLK_CHUNKS - 1 - i, False), c)[1], 0)
        chunk(0, True)

    rev = lambda j: nsb - 1 - j
    small = pl.BlockSpec((None, hp, A_BLK_CHUNKS, A_CHUNK), lambda b, h, j: (b, h, rev(j), 0))
    wide = pl.BlockSpec((None, A_SEQ_BLK, hp * A_HEAD_COLS), lambda b, h, j: (b, rev(j), h))
    val = pl.BlockSpec((None, A_SEQ_BLK, hp * A_DV), lambda b, h, j: (b, rev(j), h))
    return pl.pallas_call(
        body, name="gdn_bwd", grid=(bn, A_HEADS // hp, nsb),
        in_specs=[wide,
                  pl.BlockSpec((None, None, HALO, hp * A_HEAD_COLS), lambda b, h, j: (b, rev(j), 0, h)),
                  pl.BlockSpec((A_CONV, hp * A_CONV_COLS), lambda b, h, j: (0, h)),
                  small, small,
                  pl.BlockSpec((1, A_DV), lambda b, h, j: (0, 0)),
                  val,
                  pl.BlockSpec((None, hp, A_BLK_CHUNKS, A_DK, A_DV), lambda b, h, j: (b, h, rev(j), 0, 0)),
                  pl.BlockSpec((None, hp, A_BLK_CHUNKS, A_CHUNK, A_CHUNK), lambda b, h, j: (b, h, rev(j), 0, 0)),
                  val],
        out_specs=[wide, small, small,
                   pl.BlockSpec((None, A_CONV, hp * A_CONV_COLS), lambda b, h, j: (b, 0, h)),
                   pl.BlockSpec((None, hp, 8, A_DV), lambda b, h, j: (b, h, 0, 0))],
        out_shape=[jax.ShapeDtypeStruct((bn, s, A_HEADS * A_HEAD_COLS), BF16),
                   jax.ShapeDtypeStruct((bn, A_HEADS, n, A_CHUNK), F32),
                   jax.ShapeDtypeStruct((bn, A_HEADS, n, A_CHUNK), F32),
                   jax.ShapeDtypeStruct((bn, A_CONV, A_HEADS * A_CONV_COLS), F32),
                   jax.ShapeDtypeStruct((bn, A_HEADS, 8, A_DV), F32)],
        scratch_shapes=[pltpu.VMEM((hp, A_DK, A_DV), F32), pltpu.VMEM((hp, HALO, A_CONV_COLS), F32)],
        compiler_params=_params(("parallel", "parallel", "arbitrary"), VMEM_BIG),
    )(proj_hm, halo, cw_hm, beta, gc, norm_g, oraw, states, t_mats, dog)


def _rope_tables(posf, inv_freq_row):
    t = posf.shape[0]
    tm = 512

    def body(p_ref, f_ref, c_ref, sa_ref, sb_ref):
        ang = p_ref[...] * f_ref[...]
        lane = lax.broadcasted_iota(jnp.int32, ang.shape, 1)
        half = ROPE_DIMS // 2
        c_ref[...] = jnp.where(lane < ROPE_DIMS, jnp.cos(ang), 1.0)
        sn = jnp.sin(ang)
        sa_ref[...] = jnp.where(lane < half, -sn, 0.0)
        sb_ref[...] = jnp.where((lane >= half) & (lane < ROPE_DIMS), sn, 0.0)

    row = pl.BlockSpec((tm, 128), lambda i: (i, 0))
    return pl.pallas_call(
        body, name="rope_tables", grid=(t // tm,),
        in_specs=[row, pl.BlockSpec((1, 128), lambda i: (0, 0))], out_specs=[row] * 3,
        out_shape=[jax.ShapeDtypeStruct((t, 128), F32)] * 3,
        compiler_params=_params(("parallel",)),
    )(posf, inv_freq_row)


def _rope(x, c, sa, sb):
    half = ROPE_DIMS // 2
    return x * c + pltpu.roll(x, 128 - half, 1) * sa + pltpu.roll(x, half, 1) * sb


def _rope_t(d, c, sa, sb):
    half = ROPE_DIMS // 2
    return d * c + pltpu.roll(d * sa, half, 1) + pltpu.roll(d * sb, 128 - half, 1)


def _qk_prep(proj, c, sa, sb, qg, kg, name, tm=256):
    t = proj.shape[0]
    wide = proj.shape[1]

    def body(x_ref, c_ref, sa_ref, sb_ref, qg_ref, kg_ref, o_ref):
        cc, s1, s2 = c_ref[...], sa_ref[...], sb_ref[...]
        for which, g_ref in ((0, qg_ref), (1, kg_ref)):
            g = g_ref[...]
            for h in range(B_HEADS):
                lo = which * B_W + h * B_DH
                xv = x_ref[:, lo:lo + B_DH]
                r = lax.rsqrt(jnp.mean(xv * xv, axis=1, keepdims=True) + EPS)
                o_ref[:, lo:lo + B_DH] = _rope(xv * r * g, cc, s1, s2).astype(BF16)
        o_ref[:, 2 * B_W:3 * B_W] = x_ref[:, 2 * B_W:3 * B_W].astype(BF16)

    tab = pl.BlockSpec((tm, 128), lambda i: (i, 0))
    gain = pl.BlockSpec((1, B_DH), lambda i: (0, 0))
    return pl.pallas_call(
        body, name=name, grid=(t // tm,),
        in_specs=[pl.BlockSpec((tm, wide), lambda i: (i, 0)), tab, tab, tab, gain, gain],
        out_specs=pl.BlockSpec((tm, 3 * B_W), lambda i: (i, 0)),
        out_shape=jax.ShapeDtypeStruct((t, 3 * B_W), BF16),
        compiler_params=_params(("parallel",), 40 * 1024 * 1024),
    )(proj, c, sa, sb, qg, kg)


def _qk_prep_bwd(proj, c, sa, sb, qg, kg, dq, dk, dv, dz, name, tm=256):
    t = proj.shape[0]
    wide = proj.shape[1]
    out_w = 3 * B_W + (B_W if dz is not None else 0)

    def body(*refs):
        x_ref, c_ref, sa_ref, sb_ref, qg_ref, kg_ref, dq_ref, dk_ref, dv_ref = refs[:9]
        if dz is not None:
            dz_ref, o_ref, dgain_ref = refs[9:]
        else:
            o_ref, dgain_ref = refs[9:]
        i = pl.program_id(0)

        @pl.when(i == 0)
        def _():
            dgain_ref[...] = jnp.zeros_like(dgain_ref)

        cc, s1, s2 = c_ref[...], sa_ref[...], sb_ref[...]
        for which, g_ref, d_ref in ((0, qg_ref, dq_ref), (1, kg_ref, dk_ref)):
            g = g_ref[...]
            acc = jnp.zeros((1, B_DH), F32)
            for h in range(B_HEADS):
                lo = which * B_W + h * B_DH
                xv = x_ref[:, lo:lo + B_DH]
                r = lax.rsqrt(jnp.mean(xv * xv, axis=1, keepdims=True) + EPS)
                xh = xv * r
                d_xn = _rope_t(d_ref[:, h * B_DH:(h + 1) * B_DH], cc, s1, s2)
                acc = acc + jnp.sum(d_xn * xh, axis=0, keepdims=True)
                d_xh = d_xn * g
                d_x = r * (d_xh - xh * jnp.mean(d_xh * xh, axis=1, keepdims=True))
                o_ref[:, lo:lo + B_DH] = d_x.astype(BF16)
            dgain_ref[which:which + 1, :] += acc
        o_ref[:, 2 * B_W:3 * B_W] = dv_ref[...].astype(BF16)
        if dz is not None:
            o_ref[:, 3 * B_W:4 * B_W] = dz_ref[...]

    tab = pl.BlockSpec((tm, 128), lambda i: (i, 0))
    gain = pl.BlockSpec((1, B_DH), lambda i: (0, 0))
    grad = pl.BlockSpec((tm, B_W), lambda i: (i, 0))
    in_specs = [pl.BlockSpec((tm, wide), lambda i: (i, 0)), tab, tab, tab, gain, gain, grad, grad, grad]
    args = [proj, c, sa, sb, qg, kg, dq, dk, dv]
    if dz is not None:
        in_specs.append(grad)
        args.append(dz)
    return pl.pallas_call(
        body, name=name, grid=(t // tm,), in_specs=in_specs,
        out_specs=[pl.BlockSpec((tm, out_w), lambda i: (i, 0)), pl.BlockSpec((8, B_DH), lambda i: (0, 0))],
        out_shape=[jax.ShapeDtypeStruct((t, out_w), BF16), jax.ShapeDtypeStruct((8, B_DH), F32)],
        compiler_params=_params(("arbitrary",), 40 * 1024 * 1024),
    )(*args)


def _attn_masks():
    qi = lax.broadcasted_iota(jnp.int32, (B_BLK, 2 * B_BLK), 0)
    kj = lax.broadcasted_iota(jnp.int32, (B_BLK, 2 * B_BLK), 1)
    two = (kj >= qi) & (kj <= qi + B_BLK)
    q1 = lax.broadcasted_iota(jnp.int32, (B_BLK, B_BLK), 0)
    k1 = lax.broadcasted_iota(jnp.int32, (B_BLK, B_BLK), 1)
    return k1 <= q1, two


def _lane_pick(ref_rows, h):
    lane = lax.broadcasted_iota(jnp.int32, ref_rows.shape, 1)
    return jnp.sum(jnp.where(lane == h, ref_rows, 0.0), axis=1, keepdims=True)


B_ROWS = 2048


def _attn_schedule(nb, sb, block):
    way = 4

    def run(items):
        for at in range(0, len(items), way):
            _round_robin([block(*it) for it in items[at:at + way]])

    run([(si, 0, True) for si in range(sb)])
    if nb == 1:
        return
    per = max(1, way // sb)
    lead = 1 + (nb - 1) % per
    if lead > 1:
        run([(si, i, False) for i in range(1, lead) for si in range(sb)])

    def step(it, carry):
        run([(si, lead + it * per + u, False) for u in range(per) for si in range(sb)])
        return carry

    lax.fori_loop(0, (nb - lead) // per, step, 0)


def _attn_rows(i, first):
    if first:
        return pl.ds(0, B_BLK), pl.ds(0, B_BLK)
    rows = pl.ds(pl.multiple_of(i * B_BLK, B_BLK), B_BLK)
    return rows, pl.ds(pl.multiple_of((i - 1) * B_BLK, B_BLK), 2 * B_BLK)


def _attn_fwd(qkv, name):
    ns, ln, _ = qkv.shape
    nb = ln // B_BLK
    sb = B_ROWS // ln
    scale = B_DH ** -0.5

    def body(q_ref, k_ref, v_ref, o_ref, lse_ref):
        h = pl.program_id(1)
        mask1, mask2 = _attn_masks()
        lane = lax.broadcasted_iota(jnp.int32, (B_BLK, B_HEADS), 1)

        @pl.when(h == 0)
        def _():
            lse_ref[...] = jnp.zeros_like(lse_ref)

        def block(si, i, first):
            rows, win = _attn_rows(i, first)
            mask = mask1 if first else mask2
            sc = jnp.where(mask, _dot(q_ref[si, rows, :], k_ref[si, win, :], 1, 1) * scale, -1e30)
            yield
            m = jnp.max(sc, axis=1, keepdims=True)
            p = jnp.exp(sc - m)
            l = jnp.sum(p, axis=1, keepdims=True)
            pv = _dot(p, v_ref[si, win, :], 1, 0)
            yield
            o_ref[si, rows, :] = pv / l
            lse_ref[si, rows, :] = jnp.where(lane == h, m + jnp.log(l), lse_ref[si, rows, :])

        _attn_schedule(nb, sb, block)

    head = lambda off: pl.BlockSpec((sb, ln, B_DH), lambda s, h: (s, 0, off + h))
    return pl.pallas_call(
        body, name=name, grid=(ns // sb, B_HEADS),
        in_specs=[head(0), head(B_HEADS), head(2 * B_HEADS)],
        out_specs=[head(0), pl.BlockSpec((sb, ln, B_HEADS), lambda s, h: (s, 0, 0))],
        out_shape=[jax.ShapeDtypeStruct((ns, ln, B_W), F32), jax.ShapeDtypeStruct((ns, ln, B_HEADS), F32)],
        compiler_params=_params(("parallel", "arbitrary")),
    )(qkv, qkv, qkv)


def _attn_bwd(qkv, d_o, lse_joint, delta, name):
    ns, ln, _ = qkv.shape
    nb = ln // B_BLK
    sb = B_ROWS // ln
    scale = B_DH ** -0.5

    def body(q_ref, k_ref, v_ref, do_ref, lj_ref, dl_ref, dq_ref, dk_ref, dv_ref):
        h = pl.program_id(1)
        mask1, mask2 = _attn_masks()
        dk_ref[...] = jnp.zeros_like(dk_ref)
        dv_ref[...] = jnp.zeros_like(dv_ref)

        def block(si, i, first):
            rows, win = _attn_rows(i, first)
            mask = mask1 if first else mask2
            q = q_ref[si, rows, :]
            d_out = do_ref[si, rows, :]
            l_col = _lane_pick(lj_ref[si, rows, :], h)
            d_col = _lane_pick(dl_ref[si, rows, :], h)
            sc = _dot(q, k_ref[si, win, :], 1, 1) * scale
            d_p = _dot(d_out, v_ref[si, win, :], 1, 1)
            yield
            p = jnp.exp(jnp.where(mask, sc - l_col, -1e30))
            d_s = p * (d_p - d_col) * scale
            d_q = _dot(d_s, k_ref[si, win, :], 1, 0)
            d_k = _dot(d_s, q, 0, 0)
            d_v = _dot(p, d_out, 0, 0)
            yield
            dq_ref[si, rows, :] = d_q
            dk_ref[si, win, :] += d_k
            dv_ref[si, win, :] += d_v

        _attn_schedule(nb, sb, block)

    head = lambda off: pl.BlockSpec((sb, ln, B_DH), lambda s, h: (s, 0, off + h))
    small = pl.BlockSpec((sb, ln, B_HEADS), lambda s, h: (s, 0, 0))
    return pl.pallas_call(
        body, name=name, grid=(ns // sb, B_HEADS),
        in_specs=[head(0), head(B_HEADS), head(2 * B_HEADS), head(0), small, small],
        out_specs=[head(0)] * 3,
        out_shape=[jax.ShapeDtypeStruct((ns, ln, B_W), F32)] * 3,
        compiler_params=_params(("parallel", "parallel")),
    )(qkv, qkv, qkv, d_o, lse_joint, delta)


def _merge_weights(lse_refs):
    ls = [r[...] for r in lse_refs]
    m = jnp.maximum(jnp.maximum(ls[0], ls[1]), ls[2])
    es = [jnp.exp(l - m) for l in ls]
    tot = es[0] + es[1] + es[2]
    return [e / tot for e in es], m + jnp.log(tot)


def _merge_fwd(outs, lses, proj0, tm=256):
    t = outs[0].shape[0]

    def body(o0, o1, o2, l0, l1, l2, z_ref, og_ref):
        wts, _ = _merge_weights((l0, l1, l2))
        for h in range(B_HEADS):
            cols = slice(h * B_DH, (h + 1) * B_DH)
            o = (wts[0][:, h:h + 1] * o0[:, cols] + wts[1][:, h:h + 1] * o1[:, cols]
                 + wts[2][:, h:h + 1] * o2[:, cols])
            og_ref[:, cols] = (o * _silu(z_ref[:, cols])).astype(BF16)

    wide = pl.BlockSpec((tm, B_W), lambda i: (i, 0))
    small = pl.BlockSpec((tm, B_HEADS), lambda i: (i, 0))
    return pl.pallas_call(
        body, name="merge_fwd", grid=(t // tm,),
        in_specs=[wide] * 3 + [small] * 3 + [pl.BlockSpec((tm, B_W), lambda i: (i, 3))],
        out_specs=wide, out_shape=jax.ShapeDtypeStruct((t, B_W), BF16),
        compiler_params=_params(("parallel",)),
    )(*outs, *lses, proj0)


def _merge_bwd(outs, lses, proj0, d_og, tm=256):
    t = outs[0].shape[0]

    def body(o0, o1, o2, l0, l1, l2, z_ref, dog_ref, do_ref, lj_ref, dl_ref, dz_ref):
        wts, lj = _merge_weights((l0, l1, l2))
        lj_ref[...] = lj
        lane = lax.broadcasted_iota(jnp.int32, (tm, B_HEADS), 1)
        delta = jnp.zeros((tm, B_HEADS), F32)
        for h in range(B_HEADS):
            cols = slice(h * B_DH, (h + 1) * B_DH)
            o = (wts[0][:, h:h + 1] * o0[:, cols] + wts[1][:, h:h + 1] * o1[:, cols]
                 + wts[2][:, h:h + 1] * o2[:, cols])
            z = z_ref[:, cols]
            d_g = dog_ref[:, cols]
            d_out = d_g * _silu(z)
            dz_ref[:, cols] = (d_g * o * _dsilu(z)).astype(BF16)
            do_ref[:, cols] = d_out.astype(BF16)
            delta = jnp.where(lane == h, jnp.sum(d_out * o, axis=1, keepdims=True), delta)
        dl_ref[...] = delta

    wide = pl.BlockSpec((tm, B_W), lambda i: (i, 0))
    small = pl.BlockSpec((tm, B_HEADS), lambda i: (i, 0))
    return pl.pallas_call(
        body, name="merge_bwd", grid=(t // tm,),
        in_specs=[wide] * 3 + [small] * 3 + [pl.BlockSpec((tm, B_W), lambda i: (i, 3)), wide],
        out_specs=[wide, small, small, wide],
        out_shape=[jax.ShapeDtypeStruct((t, B_W), BF16), jax.ShapeDtypeStruct((t, B_HEADS), F32),
                   jax.ShapeDtypeStruct((t, B_HEADS), F32), jax.ShapeDtypeStruct((t, B_W), BF16)],
        compiler_params=_params(("parallel",)),
    )(*outs, *lses, proj0, d_og)


def _adamw(w, g, m, v, name):
    r, c = w.shape
    tr = r
    for cand in (256, 128, 64, 32, 16, 8):
        if r % cand == 0:
            tr = cand
            break

    def body(w_ref, g_ref, m_ref, v_ref, d_ref, nm_ref, nv_ref):
        gv = g_ref[...]
        nm = ADAM_B1 * m_ref[...] + (1.0 - ADAM_B1) * gv
        nv = ADAM_B2 * v_ref[...] + (1.0 - ADAM_B2) * (gv * gv)
        m_hat = nm / (1.0 - ADAM_B1 ** ADAM_STEP)
        v_hat = nv / (1.0 - ADAM_B2 ** ADAM_STEP)
        d_ref[...] = -ADAM_LR * (m_hat / (jnp.sqrt(v_hat) + ADAM_EPS) + ADAM_WD * w_ref[...])
        nm_ref[...] = nm
        nv_ref[...] = nv

    blk = pl.BlockSpec((tr, c), lambda i: (i, 0))
    return pl.pallas_call(
        body, name=name, grid=(r // tr,), in_specs=[blk] * 4, out_specs=[blk] * 3,
        out_shape=[jax.ShapeDtypeStruct((r, c), F32)] * 3,
        compiler_params=_params(("parallel",)),
    )(w, g, m, v)


def _adam_update(w, gv, m, v):
    nm = ADAM_B1 * m + (1.0 - ADAM_B1) * gv
    nv = ADAM_B2 * v + (1.0 - ADAM_B2) * (gv * gv)
    m_hat = nm / (1.0 - ADAM_B1 ** ADAM_STEP)
    v_hat = nv / (1.0 - ADAM_B2 ** ADAM_STEP)
    return -ADAM_LR * (m_hat / (jnp.sqrt(v_hat) + ADAM_EPS) + ADAM_WD * w), nm, nv


def _adamw_shard(w, mine, theirs, m, v, half_index, name, tr=128):
    _, r, c = w.shape
    nhb = (r // 2) // tr

    def body(c_ref, w_ref, mine_ref, theirs_ref, m_ref, v_ref, g_ref, d_ref, nm_ref, nv_ref):
        is_mine = (pl.program_id(0) // nhb) == c_ref[0]
        gv = jnp.where(is_mine, mine_ref[...], theirs_ref[...])
        d, nm, nv = _adam_update(w_ref[...], gv, m_ref[...], v_ref[...])
        g_ref[...] = gv
        d_ref[...] = d
        nm_ref[...] = nm
        nv_ref[...] = nv

    full = pl.BlockSpec((None, tr, c), lambda i, cc: (0, i, 0))
    half = pl.BlockSpec((tr, c), lambda i, cc: (i % nhb, 0))
    return pl.pallas_call(
        body, name=name,
        grid_spec=pltpu.PrefetchScalarGridSpec(
            num_scalar_prefetch=1, grid=(2 * nhb,),
            in_specs=[full, half, half, full, full], out_specs=[full] * 4),
        out_shape=[jax.ShapeDtypeStruct(w.shape, F32)] * 4,
        compiler_params=_params(("parallel",), 40 * 1024 * 1024),
    )(half_index, w, mine, theirs, m, v)


def _pair_sum(own, other, half_index, name, tr=256):
    _, r, c = own.shape
    rh = r // 2
    tr = min(tr, rh)
    nrb = rh // tr

    def body(c_ref, own_ref, oth_ref, out_ref):
        out_ref[...] = (own_ref[...] + oth_ref[...].astype(F32)).astype(BF16)

    return pl.pallas_call(
        body, name=name,
        grid_spec=pltpu.PrefetchScalarGridSpec(
            num_scalar_prefetch=1, grid=(N_CHIPS, nrb),
            in_specs=[pl.BlockSpec((None, tr, c), lambda k, i, cc: (k, cc[0] * nrb + i, 0)),
                      pl.BlockSpec((None, tr, c), lambda k, i, cc: (k, i, 0))],
            out_specs=pl.BlockSpec((None, tr, c), lambda k, i, cc: (k, i, 0))),
        out_shape=jax.ShapeDtypeStruct((N_CHIPS, rh, c), BF16),
        compiler_params=_params(("parallel", "parallel")),
    )(half_index, own, other)


def _chip_sum(sums, others, chip_index, name, tr=256):
    _, r, c = sums.shape
    tr = min(tr, r)

    def body(k_ref, own_ref, oth_ref, out_ref):
        acc = own_ref[...].astype(F32)
        for j in range(N_CHIPS - 1):
            acc = acc + oth_ref[j].astype(F32)
        out_ref[...] = acc

    return pl.pallas_call(
        body, name=name,
        grid_spec=pltpu.PrefetchScalarGridSpec(
            num_scalar_prefetch=1, grid=(r // tr,),
            in_specs=[pl.BlockSpec((None, tr, c), lambda i, kk: (kk[0], i, 0)),
                      pl.BlockSpec((N_CHIPS - 1, tr, c), lambda i, kk: (0, i, 0))],
            out_specs=pl.BlockSpec((tr, c), lambda i, kk: (i, 0))),
        out_shape=jax.ShapeDtypeStruct((r, c), F32),
        compiler_params=_params(("parallel",)),
    )(chip_index, sums, others)


HBM = pl.BlockSpec(memory_space=pltpu.HBM)


def _place():
    x, y, c = lax.axis_index("x"), lax.axis_index("y"), lax.axis_index("c")
    chips = [(1 - x, y), (x, 1 - y), (1 - x, 1 - y)]
    return x, y, c, chips


def _weight_allgather(shards, conv_shard):
    na = len(shards)

    def body(*refs):
        ins = refs[:na]
        conv_in = refs[na]
        outs = refs[na + 1:2 * na + 1]
        conv_out = refs[2 * na + 1]
        send, recv, fsend, frecv, csend, crecv = refs[2 * na + 2:]
        x, y, c, chips = _place()
        me = 2 * x + y
        sib = (x, y, 1 - c)
        first, conv_cp = [], []
        for i in range(na):
            rh = ins[i].shape[0] // 2
            mine = pl.ds(c * rh, rh)
            for j, (px, py) in enumerate(chips):
                cp = pltpu.make_async_remote_copy(
                    src_ref=ins[i].at[mine], dst_ref=outs[i].at[me, mine],
                    send_sem=send.at[3 * i + j], recv_sem=recv.at[3 * i + j],
                    device_id=(px, py, c), device_id_type=MESH)
                cp.start()
                first.append(cp)
        for j, (px, py) in enumerate(chips):
            cp = pltpu.make_async_remote_copy(
                src_ref=conv_in, dst_ref=conv_out.at[me], send_sem=csend.at[j], recv_sem=crecv.at[j],
                device_id=(px, py, c), device_id_type=MESH)
            cp.start()
            conv_cp.append(cp)
        passed = []
        for i in range(na):
            rh = ins[i].shape[0] // 2
            mine = pl.ds(c * rh, rh)
            for j, (px, py) in enumerate(chips):
                slot = outs[i].at[2 * px + py, mine]
                pltpu.make_async_remote_copy(
                    src_ref=slot, dst_ref=slot, send_sem=send.at[3 * i + j], recv_sem=recv.at[3 * i + j],
                    device_id=(px, py, c), device_id_type=MESH).wait_recv()
                cp = pltpu.make_async_remote_copy(
                    src_ref=slot, dst_ref=slot, send_sem=fsend.at[3 * i + j], recv_sem=frecv.at[3 * i + j],
                    device_id=sib, device_id_type=MESH)
                cp.start()
                passed.append(cp)
        for i in range(na):
            rh = ins[i].shape[0] // 2
            theirs = pl.ds((1 - c) * rh, rh)
            for j, (px, py) in enumerate(chips):
                slot = outs[i].at[2 * px + py, theirs]
                pltpu.make_async_remote_copy(
                    src_ref=slot, dst_ref=slot, send_sem=fsend.at[3 * i + j], recv_sem=frecv.at[3 * i + j],
                    device_id=sib, device_id_type=MESH).wait_recv()
        for j, (px, py) in enumerate(chips):
            slot = conv_out.at[2 * px + py]
            pltpu.make_async_remote_copy(
                src_ref=slot, dst_ref=slot, send_sem=csend.at[j], recv_sem=crecv.at[j],
                device_id=(px, py, c), device_id_type=MESH).wait_recv()
        for cp in first + passed + conv_cp:
            cp.wait_send()

    out_shape = [jax.ShapeDtypeStruct((N_CHIPS,) + s.shape, s.dtype) for s in shards]
    out_shape.append(jax.ShapeDtypeStruct((N_CHIPS,) + conv_shard.shape, conv_shard.dtype))
    res = pl.pallas_call(
        body, name="weight_allgather",
        in_specs=[HBM] * (na + 1), out_specs=[HBM] * (na + 1), out_shape=out_shape,
        scratch_shapes=[pltpu.SemaphoreType.DMA((3 * na,)), pltpu.SemaphoreType.DMA((3 * na,)),
                        pltpu.SemaphoreType.DMA((3 * na,)), pltpu.SemaphoreType.DMA((3 * na,)),
                        pltpu.SemaphoreType.DMA((3,)), pltpu.SemaphoreType.DMA((3,))],
    )(*shards, conv_shard)
    my_chip = 2 * lax.axis_index("x") + lax.axis_index("y")
    pick = lambda got, own: [jnp.where(my_chip == k, own, got[k]) for k in range(N_CHIPS)]
    return [pick(g, s) for g, s in zip(res[:na], shards)], pick(res[na], conv_shard)


def _sibling_swap_halves(grads, name):
    na = len(grads)

    def body(*refs):
        ins, outs = refs[:na], refs[na:2 * na]
        send, recv = refs[2 * na:]
        x, y, c, _ = _place()
        sib = (x, y, 1 - c)
        cps = []
        for i in range(na):
            rh = ins[i].shape[1] // 2
            cp = pltpu.make_async_remote_copy(
                src_ref=ins[i].at[:, pl.ds((1 - c) * rh, rh), :], dst_ref=outs[i],
                send_sem=send.at[i], recv_sem=recv.at[i], device_id=sib, device_id_type=MESH)
            cp.start()
            cps.append(cp)
        for cp in cps:
            cp.wait()

    out_shape = [jax.ShapeDtypeStruct((g.shape[0], g.shape[1] // 2, g.shape[2]), g.dtype) for g in grads]
    return pl.pallas_call(
        body, name=name, in_specs=[HBM] * na, out_specs=[HBM] * na, out_shape=out_shape,
        scratch_shapes=[pltpu.SemaphoreType.DMA((na,)), pltpu.SemaphoreType.DMA((na,))],
    )(*grads)


def _sibling_swap_whole(halves):
    na = len(halves)

    def body(*refs):
        ins, outs = refs[:na], refs[na:2 * na]
        send, recv = refs[2 * na:]
        x, y, c, _ = _place()
        cps = []
        for i in range(na):
            cp = pltpu.make_async_remote_copy(
                src_ref=ins[i], dst_ref=outs[i], send_sem=send.at[i], recv_sem=recv.at[i],
                device_id=(x, y, 1 - c), device_id_type=MESH)
            cp.start()
            cps.append(cp)
        for cp in cps:
            cp.wait()

    out_shape = [jax.ShapeDtypeStruct(h.shape, h.dtype) for h in halves]
    return pl.pallas_call(
        body, name="grad_sibling_join", in_specs=[HBM] * na, out_specs=[HBM] * na, out_shape=out_shape,
        scratch_shapes=[pltpu.SemaphoreType.DMA((na,)), pltpu.SemaphoreType.DMA((na,))],
    )(*halves)


SEM = pl.BlockSpec(memory_space=pltpu.SEMAPHORE)
ANY = pl.BlockSpec(memory_space=pl.ANY)
EFFECT = pltpu.SideEffectType.DATAFLOW_SIDE_EFFECTING


def _split_copy_start(name, plan, srcs, lands, after):
    ns, nl = len(srcs), len(lands)

    def body(*refs):
        src_refs, land_refs = refs[:ns], refs[ns:ns + nl]
        send, recv = refs[ns + nl + 1], refs[ns + nl + 2]
        token = refs[-1]
        outgoing, _ = plan(src_refs, land_refs)
        for src, dst, dev, si, ri in outgoing:
            pltpu.make_async_remote_copy(src_ref=src, dst_ref=dst, send_sem=send.at[si], recv_sem=recv.at[ri],
                                         device_id=dev, device_id_type=MESH).start()
        token[...] = jnp.zeros_like(token)

    n_out, n_in = plan.counts
    thru = [pltpu.HBM(a.shape, a.dtype) for a in list(srcs) + list(lands)]
    res = pl.pallas_call(
        body, name=name,
        out_shape=[pltpu.SemaphoreType.DMA((n_out,)), pltpu.SemaphoreType.DMA((n_in,))] + thru
        + [jax.ShapeDtypeStruct((8, 128), F32)],
        in_specs=[HBM] * (ns + nl) + [ANY],
        out_specs=[SEM, SEM] + [HBM] * (ns + nl) + [pl.BlockSpec(memory_space=pltpu.VMEM)],
        input_output_aliases={i: 2 + i for i in range(ns + nl)},
        compiler_params=pltpu.CompilerParams(has_side_effects=EFFECT),
    )(*[pltpu.with_memory_space_constraint(a, pltpu.HBM) for a in list(srcs) + list(lands)], after)
    return res[0], res[1], res[2:2 + ns], res[2 + ns:2 + ns + nl], res[-1]


def _split_copy_wait(name, plan, send, recv, srcs, lands, after):
    ns, nl = len(srcs), len(lands)

    def body(*refs):
        src_refs, land_refs = refs[:ns], refs[ns:ns + nl]
        send_ref, recv_ref = refs[ns + nl], refs[ns + nl + 1]
        outgoing, arrivals = plan(src_refs, land_refs)
        for src, dst, dev, si, ri in outgoing:
            pltpu.make_async_remote_copy(src_ref=src, dst_ref=dst, send_sem=send_ref.at[si], recv_sem=recv_ref.at[ri],
                                         device_id=dev, device_id_type=MESH).wait_send()
        for view, ri in arrivals:
            pltpu.make_async_remote_copy(src_ref=view, dst_ref=view, send_sem=send_ref.at[0], recv_sem=recv_ref.at[ri],
                                         device_id=_place()[:3], device_id_type=MESH).wait_recv()

    thru = [pltpu.HBM(a.shape, a.dtype) for a in list(srcs) + list(lands)]
    res = pl.pallas_call(
        body, name=name, out_shape=thru,
        in_specs=[HBM] * (ns + nl) + [SEM, SEM, ANY], out_specs=[HBM] * (ns + nl),
        input_output_aliases={i: i for i in range(ns + nl)},
        compiler_params=pltpu.CompilerParams(has_side_effects=EFFECT),
    )(*srcs, *lands, send, recv, after)
    return res[:ns], res[ns:]


def _gather_plan(n_arrays):
    def plan(src_refs, land_refs):
        x, y, c, chips = _place()
        me = 2 * x + y
        outgoing, arrivals = [], []
        for i in range(n_arrays):
            rh = src_refs[i].shape[0] // 2
            mine = pl.ds(c * rh, rh)
            for j, (px, py) in enumerate(chips):
                for delta in range(2):
                    tc = c ^ delta
                    outgoing.append((src_refs[i].at[mine], land_refs[i].at[me, mine], (px, py, tc),
                                     6 * i + 2 * j + delta, 6 * i + 2 * j + delta))
                    theirs = pl.ds(tc * rh, rh)
                    arrivals.append((land_refs[i].at[2 * px + py, theirs], 6 * i + 2 * j + delta))
        return outgoing, arrivals

    plan.counts = (6 * n_arrays, 6 * n_arrays)
    return plan


def _exchange_plan(n_arrays):
    def plan(src_refs, land_refs):
        x, y, c, chips = _place()
        outgoing, arrivals = [], []
        for i in range(n_arrays):
            for j, (px, py) in enumerate(chips):
                outgoing.append((src_refs[i].at[2 * px + py], land_refs[i].at[j], (px, py, c), 3 * i + j, 3 * i + j))
                arrivals.append((land_refs[i].at[j], 3 * i + j))
        return outgoing, arrivals

    plan.counts = (3 * n_arrays, 3 * n_arrays)
    return plan


def _small_allreduce(vec):
    r, cdim = vec.shape
    n_dev = 8

    def body(v_ref, out_ref, buf, send, recv):
        x, y, c, _ = _place()
        me = 4 * x + 2 * y + c
        buf[me] = v_ref[...]
        cps = []
        for k in range(1, n_dev):
            dx, dy, dc = (k >> 2) & 1, (k >> 1) & 1, k & 1
            peer = (x ^ dx, y ^ dy, c ^ dc)
            cp = pltpu.make_async_remote_copy(
                src_ref=v_ref, dst_ref=buf.at[me], send_sem=send.at[k - 1], recv_sem=recv.at[k - 1],
                device_id=peer, device_id_type=MESH)
            cp.start()
            cps.append(cp)
        for k in range(1, n_dev):
            dx, dy, dc = (k >> 2) & 1, (k >> 1) & 1, k & 1
            src = 4 * (x ^ dx) + 2 * (y ^ dy) + (c ^ dc)
            slot = buf.at[src]
            pltpu.make_async_remote_copy(
                src_ref=slot, dst_ref=slot, send_sem=send.at[k - 1], recv_sem=recv.at[k - 1],
                device_id=(x ^ dx, y ^ dy, c ^ dc), device_id_type=MESH).wait_recv()
        for cp in cps:
            cp.wait_send()
        acc = buf[0]
        for k in range(1, n_dev):
            acc = acc + buf[k]
        out_ref[...] = acc

    vm = pl.BlockSpec(memory_space=pltpu.VMEM)
    return pl.pallas_call(
        body, name="small_allreduce", in_specs=[vm], out_specs=vm,
        out_shape=jax.ShapeDtypeStruct((r, cdim), F32),
        scratch_shapes=[pltpu.VMEM((n_dev, r, cdim), F32), pltpu.SemaphoreType.DMA((n_dev - 1,)),
                        pltpu.SemaphoreType.DMA((n_dev - 1,))],
    )(vec)


def _a_cols_to_head_major(w):
    lead = w.shape[:-1]
    q = w[..., :A_QK].reshape(lead + (A_HEADS, A_DK))
    k = w[..., A_QK:2 * A_QK].reshape(lead + (A_HEADS, A_DK))
    v = w[..., 2 * A_QK:2 * A_QK + A_VW].reshape(lead + (A_HEADS, A_DV))
    z = w[..., 2 * A_QK + A_VW:].reshape(lead + (A_HEADS, A_DV))
    return jnp.concatenate([q, k, v, z], axis=-1).reshape(lead + (A_HEADS * A_HEAD_COLS,))


def _a_cols_from_head_major(w):
    lead = w.shape[:-1]
    w = w.reshape(lead + (A_HEADS, A_HEAD_COLS))
    parts = [w[..., :A_DK], w[..., A_DK:2 * A_DK], w[..., 2 * A_DK:2 * A_DK + A_DV], w[..., 2 * A_DK + A_DV:]]
    return jnp.concatenate([p.reshape(lead + (-1,)) for p in parts], axis=-1)


def _conv_cols_to_head_major(w):
    lead = w.shape[:-1]
    q = w[..., :A_QK].reshape(lead + (A_HEADS, A_DK))
    k = w[..., A_QK:2 * A_QK].reshape(lead + (A_HEADS, A_DK))
    v = w[..., 2 * A_QK:].reshape(lead + (A_HEADS, A_DV))
    return jnp.concatenate([q, k, v], axis=-1).reshape(lead + (A_HEADS * A_CONV_COLS,))


def _conv_cols_from_head_major(w):
    lead = w.shape[:-1]
    w = w.reshape(lead + (A_HEADS, A_CONV_COLS))
    parts = [w[..., :A_DK], w[..., A_DK:2 * A_DK], w[..., 2 * A_DK:]]
    return jnp.concatenate([p.reshape(lead + (-1,)) for p in parts], axis=-1)


def _to_stream(a, bn, d):
    rest = a.shape[1:]
    s = a.shape[0] // bn
    a = a.reshape((bn, s // d, d) + rest)
    a = jnp.swapaxes(a, 1, 2)
    return a.reshape((bn * d, s // d) + rest)


def _from_stream(a, bn, d):
    rest = a.shape[2:]
    ln = a.shape[1]
    a = a.reshape((bn, d, ln) + rest)
    a = jnp.swapaxes(a, 1, 2)
    return a.reshape((bn * ln * d,) + rest)


B_SUB = 512
B_SHARD_BLOCKS = (3 * B_GROUPS * B_W + B_W) // N_CHIPS // B_SUB


def _b_block(gi, jj):
    nb = (B_GROUPS * (jj // 2) + gi) * 2 + jj % 2
    return nb // B_SHARD_BLOCKS, nb % B_SHARD_BLOCKS


def _shard_major(g, ncols):
    r = g.shape[0]
    return jnp.swapaxes(g.reshape(r, N_CHIPS, ncols), 0, 1)


def _pack_rows(items):
    rows, offs = [], []
    at = 0
    for a in items:
        flat = a.reshape(-1).astype(F32)
        nr = -(-flat.shape[0] // 1024) * 8
        flat = jnp.pad(flat, (0, nr * 128 - flat.shape[0]))
        rows.append(flat.reshape(nr, 128))
        offs.append((at, nr, a.shape))
        at += nr
    return jnp.concatenate(rows, axis=0), offs


def _unpack_rows(packed, offs):
    out = []
    for at, nr, shape in offs:
        size = int(np.prod(shape)) if len(shape) else 1
        out.append(packed[at:at + nr].reshape(-1)[:size].reshape(shape))
    return out


def _local_step(x, positions, loss_target, norm_g, wa_in, conv_w, a_log, a_dt_bias, a_norm_g,
                b_q_norm_g, b_k_norm_g, start_token, late_weights, b_grads_ready, a_grads_ready):
    bn, s, d = x.shape
    t = bn * s
    n_chunks = s // A_CHUNK
    wa_main = _a_cols_to_head_major(wa_in[:, :A_MAIN])
    wa_tail = jnp.pad(wa_in[:, A_MAIN:], ((0, 0), (0, 128 - 2 * A_HEADS)))
    cw_hm = _conv_cols_to_head_major(conv_w)

    x0 = x.reshape(t, d)
    h0 = _rms_fwd(x0, norm_g[0:1] + start_token, "rms0_fwd")
    proj_a = _matmul(h0, wa_main, "nn", F32, "a_in_main")
    tail_a = _matmul(h0, wa_tail, "nn", F32, "a_in_tail")
    tail_t = jnp.swapaxes(tail_a[:, :2 * A_HEADS].reshape(bn, s, 2 * A_HEADS), 1, 2)
    tail_t = tail_t.reshape(bn, 2 * A_HEADS, n_chunks, A_CHUNK)
    beta, gc = _gdn_prep(tail_t, a_log[0], a_dt_bias[0])
    proj_a3 = proj_a.reshape(bn, s, A_MAIN)
    og_a, oraw_a, states, t_mats = _gdn_fwd(proj_a3, cw_hm, beta, gc, a_norm_g)
    wa_out, wb_in, wb_out = late_weights(og_a)
    b_cols = [4 * B_W] + [3 * B_W] * (B_GROUPS - 1)
    x1 = _matmul(og_a.reshape(t, A_VW), wa_out, "nn", F32, "a_out", res=x0, tk=2048)

    h1 = _rms_fwd(x1, norm_g[1:2], "rms1_fwd")
    inv_freq = ROPE_THETA ** (-jnp.arange(0, ROPE_DIMS, 2, dtype=F32) / ROPE_DIMS)
    freq_row = jnp.concatenate([inv_freq, inv_freq, jnp.zeros((128 - ROPE_DIMS,), F32)]).reshape(1, 128)
    posf = jnp.broadcast_to(positions.astype(F32).reshape(t, 1), (t, 128))
    tabs = _rope_tables(posf, freq_row)
    h1_s, tabs_s, proj_b, qkv_b, o_b, lse_b = [], [], [], [], [], []
    for gi, dil in enumerate(B_DIL):
        hs = h1 if dil == 1 else _to_stream(h1, bn, dil).reshape(t, d)
        ts = tabs if dil == 1 else [_to_stream(tb, bn, dil).reshape(t, 128) for tb in tabs]
        pj = _matmul(hs, wb_in, "nn", F32, f"b_in_g{gi}", tm=2048, tn=B_SUB, n=b_cols[gi], b_spec=pl.BlockSpec(
            (None, d, B_SUB), lambda i, j, kk, gi=gi: (_b_block(gi, j)[0], kk, _b_block(gi, j)[1])))
        qkv = _qk_prep(pj, *ts, b_q_norm_g[0, gi:gi + 1], b_k_norm_g[0, gi:gi + 1], f"qk_prep_g{gi}")
        o_s, lse_s = _attn_fwd(qkv.reshape(bn * dil, s // dil, 3 * B_W), f"attn_fwd_g{gi}")
        h1_s.append(hs), tabs_s.append(ts), proj_b.append(pj), qkv_b.append(qkv)
        o_b.append(o_s.reshape(t, B_W) if dil == 1 else _from_stream(o_s, bn, dil))
        lse_b.append(lse_s.reshape(t, B_HEADS) if dil == 1 else _from_stream(lse_s, bn, dil))
    og_b = _merge_fwd(o_b, lse_b, proj_b[0])
    x2 = _matmul(og_b, wb_out, "nn", F32, "b_out", res=x1)

    d_x2, loss_parts = _loss_grad(x2, loss_target.reshape(t, d))
    loss_local = jnp.sum(loss_parts)

    d_x2b = d_x2.astype(BF16)
    g_wb_out = _matmul(og_b.T, d_x2b, "nn", F32, "b_out_dw")
    d_og_b = _matmul(d_x2b, wb_out, "nt", F32, "b_out_dx")
    d_o, lse_joint, delta, d_z = _merge_bwd(o_b, lse_b, proj_b[0], d_og_b)
    d_h1, g_qn, g_kn = [], [], []
    g_wb_in = lax.empty(wb_in.shape, F32)
    for gi, dil in enumerate(B_DIL):
        if dil == 1:
            do_s, lj_s, dl_s = d_o, lse_joint, delta
        else:
            do_s, lj_s, dl_s = (_to_stream(a, bn, dil).reshape(t, -1) for a in (d_o, lse_joint, delta))
        ns, ln = bn * dil, s // dil
        dq, dk, dv = _attn_bwd(qkv_b[gi].reshape(ns, ln, 3 * B_W), do_s.reshape(ns, ln, B_W),
                               lj_s.reshape(ns, ln, B_HEADS), dl_s.reshape(ns, ln, B_HEADS), f"attn_bwd_g{gi}")
        d_pj, d_gain = _qk_prep_bwd(proj_b[gi], *tabs_s[gi], b_q_norm_g[0, gi:gi + 1], b_k_norm_g[0, gi:gi + 1],
                                    dq.reshape(t, B_W), dk.reshape(t, B_W), dv.reshape(t, B_W),
                                    d_z if gi == 0 else None, f"qk_prep_bwd_g{gi}")
        g_wb_in = _matmul(h1_s[gi].T, d_pj, "nn", F32, f"b_in_dw_g{gi}", tn=B_SUB, tk=2048, into=(g_wb_in, pl.BlockSpec(
            (None, d, B_SUB), lambda i, j, kk, gi=gi: (_b_block(gi, j)[0], i, _b_block(gi, j)[1]))))
        dh = _matmul(d_pj, wb_in, "nt", F32, f"b_in_dx_g{gi}", tm=2048, tk=B_SUB, n=d, b_spec=pl.BlockSpec(
            (None, d, B_SUB), lambda i, j, kk, gi=gi: (_b_block(gi, kk)[0], j, _b_block(gi, kk)[1])))
        d_h1.append(dh if dil == 1 else _from_stream(dh.reshape(ns, ln, d), bn, dil))
        g_qn.append(d_gain[0]), g_kn.append(d_gain[1])
    d_x1, g_norm1 = _rms_bwd(x1, norm_g[1:2], d_h1, d_x2, "rms1_bwd")

    d_x1b = (d_x1 + b_grads_ready(g_wb_in, g_wb_out)).astype(BF16)
    g_wa_out = _matmul(og_a.reshape(t, A_VW).T, d_x1b, "nn", F32, "a_out_dw")
    d_og_a = _matmul(d_x1b, wa_out, "nt", F32, "a_out_dx")
    d_pa, d_gc, d_beta, d_cw, d_ng = _gdn_bwd(proj_a3, cw_hm, beta, gc, a_norm_g, oraw_a, states, t_mats,
                                              d_og_a.reshape(bn, s, A_VW))
    d_tail_t, d_alog, d_dtb = _gdn_prep_bwd(tail_t, a_log[0], a_dt_bias[0], d_gc, d_beta)
    d_tail = jnp.swapaxes(d_tail_t.reshape(bn, 2 * A_HEADS, s), 1, 2).reshape(t, 2 * A_HEADS)
    d_tail = jnp.pad(d_tail, ((0, 0), (0, 128 - 2 * A_HEADS))).astype(BF16)
    d_pa = d_pa.reshape(t, A_MAIN)
    h0_t = h0.T
    g_wa_main = _matmul(h0_t, d_pa, "nn", F32, "a_in_dw_main")
    g_wa_tail = _matmul(h0_t, d_tail, "nn", F32, "a_in_dw_tail")
    g_wa_in = jnp.concatenate([_a_cols_from_head_major(g_wa_main), g_wa_tail[:, :2 * A_HEADS]], axis=1)
    a_token = a_grads_ready(g_wa_in, g_wa_out)
    d_h0 = _matmul(d_pa, wa_main, "nt", F32, "a_in_dx_main")
    d_h0t = _matmul(d_tail + a_token.astype(BF16), wa_tail, "nt", F32, "a_in_dx_tail")
    d_x0, g_norm0 = _rms_bwd(x0, norm_g[0:1], [d_h0, d_h0t], d_x1, "rms0_bwd")

    gfull = {
        "norm_g": jnp.concatenate([g_norm0, g_norm1], axis=0), "a_w_in": g_wa_in,
        "a_conv_w": _conv_cols_from_head_major(jnp.sum(d_cw, axis=0)),
        "a_log": jnp.sum(d_alog[:, :, 0], axis=0), "a_dt_bias": jnp.sum(d_dtb[:, :, 0], axis=0),
        "a_norm_g": jnp.sum(d_ng[:, :, 0, :], axis=(0, 1)), "a_w_out": g_wa_out, "b_w_in": g_wb_in,
        "b_q_norm_g": jnp.stack(g_qn), "b_k_norm_g": jnp.stack(g_kn), "b_w_out": g_wb_out}
    return loss_local, d_x0.reshape(bn, s, d), gfull


def kernel(x, positions, norm_g, a_w_in, a_conv_w, a_log, a_dt_bias, a_norm_g, a_w_out, b_w_in, b_q_norm_g, b_k_norm_g, b_w_out, loss_target, m_norm_g, m_a_w_in, m_a_conv_w, m_a_log, m_a_dt_bias, m_a_norm_g, m_a_w_out, m_b_w_in, m_b_q_norm_g, m_b_k_norm_g, m_b_w_out, v_norm_g, v_a_w_in, v_a_conv_w, v_a_log, v_a_dt_bias, v_a_norm_g, v_a_w_out, v_b_w_in, v_b_q_norm_g, v_b_k_norm_g, v_b_w_out):
    d = x.shape[2]
    my_c = lax.axis_index("c")
    my_chip = 2 * lax.axis_index("x") + lax.axis_index("y")

    half_index = jnp.reshape(my_c, (1,)).astype(jnp.int32)
    chip_index = jnp.reshape(my_chip, (1,)).astype(jnp.int32)
    (ga_in,), g_conv = _weight_allgather([a_w_in[0].astype(BF16)], a_conv_w[0])
    wa_in = jnp.concatenate(ga_in, axis=1)
    conv_w = jnp.concatenate(g_conv, axis=1)

    late_shards = [a_w_out[0].astype(BF16), b_w_in[0].astype(BF16), b_w_out[0].astype(BF16)]
    late_lands = [lax.dynamic_update_slice(lax.empty((N_CHIPS,) + s.shape, BF16), s[None], (my_chip, 0, 0))
                  for s in late_shards]
    gather = _gather_plan(len(late_shards))
    ag_send, ag_recv, ag_srcs, ag_lands, ag_token = _split_copy_start(
        "late_weights_start", gather, late_shards, late_lands, conv_w)

    def late_weights(after):
        _, (ga_out, gb_in, gb_out) = _split_copy_wait(
            "late_weights_wait", gather, ag_send, ag_recv, ag_srcs, ag_lands, after)
        return ga_out.reshape(A_VW, d), gb_in, gb_out.reshape(B_W, d)

    def reduce_to_chip_sums(mats, tag):
        recv_sib = _sibling_swap_halves([g.astype(BF16) for g in mats], f"grad_{tag}_sibling_swap")
        return [_pair_sum(g, r, half_index, f"grad_{tag}_pair_sum_{i}") for i, (g, r) in enumerate(zip(mats, recv_sib))]

    exchange = _exchange_plan(2)
    pending = {}

    def start_exchange(tag, mats):
        sums = reduce_to_chip_sums(mats, tag)
        lands = [lax.empty((N_CHIPS - 1,) + s.shape[1:], BF16) for s in sums]
        pending[tag] = _split_copy_start(f"grad_{tag}_exchange_start", exchange, sums, lands, chip_index)
        return pending[tag][4][0, 0]

    def finish_exchange(tag, after):
        send, recv, srcs, lands, _ = pending[tag]
        return _split_copy_wait(f"grad_{tag}_exchange_wait", exchange, send, recv, srcs, lands, after)

    def b_grads_ready(g_wb_in, g_wb_out):
        return start_exchange("b", [g_wb_in, g_wb_out.reshape(N_CHIPS, -1, d)])

    def a_grads_ready(g_wa_in, g_wa_out):
        return start_exchange("a", [_shard_major(g_wa_in, a_w_in.shape[2]), g_wa_out.reshape(N_CHIPS, -1, d)])

    loss_local, d_x0, gfull = _local_step(x, positions, loss_target, norm_g, wa_in, conv_w, a_log, a_dt_bias,
                                          a_norm_g, b_q_norm_g, b_k_norm_g, ag_token[0, 0], late_weights,
                                          b_grads_ready, a_grads_ready)

    small = [gfull["norm_g"], gfull["a_conv_w"], gfull["a_log"], gfull["a_dt_bias"], gfull["a_norm_g"],
             gfull["b_q_norm_g"], gfull["b_k_norm_g"], loss_local]
    packed, offs = _pack_rows(small)
    reduced = _small_allreduce(packed)
    g_norm, g_conv_all, g_alog, g_dtb, g_ang, g_q, g_k, loss = _unpack_rows(reduced, offs)
    g_conv_mine = lax.dynamic_slice_in_dim(g_conv_all, my_chip * a_conv_w.shape[2], a_conv_w.shape[2], axis=1)

    b_sums, b_received = finish_exchange("b", d_x0)
    a_sums, a_received = finish_exchange("a", reduced)
    chip_sums = [a_sums[0], a_sums[1], b_sums[0], b_sums[1]]
    received = [a_received[0], a_received[1], b_received[0], b_received[1]]
    halves = [_chip_sum(s, r, chip_index, f"grad_chip_sum_{i}") for i, (s, r) in enumerate(zip(chip_sums, received))]
    theirs = _sibling_swap_whole(halves)
    big = ("a_w_in", "a_w_out", "b_w_in", "b_w_out")
    big_halves = dict(zip(big, zip(halves, theirs)))

    grads = {
        "norm_g": g_norm, "a_conv_w": g_conv_mine[None], "a_log": g_alog[None], "a_dt_bias": g_dtb[None],
        "a_norm_g": g_ang[None], "b_q_norm_g": g_q[None], "b_k_norm_g": g_k[None]}
    weights = {"norm_g": norm_g, "a_w_in": a_w_in, "a_conv_w": a_conv_w, "a_log": a_log, "a_dt_bias": a_dt_bias,
               "a_norm_g": a_norm_g, "a_w_out": a_w_out, "b_w_in": b_w_in, "b_q_norm_g": b_q_norm_g,
               "b_k_norm_g": b_k_norm_g, "b_w_out": b_w_out}
    m_in = {"norm_g": m_norm_g, "a_w_in": m_a_w_in, "a_conv_w": m_a_conv_w, "a_log": m_a_log,
            "a_dt_bias": m_a_dt_bias, "a_norm_g": m_a_norm_g, "a_w_out": m_a_w_out, "b_w_in": m_b_w_in,
            "b_q_norm_g": m_b_q_norm_g, "b_k_norm_g": m_b_k_norm_g, "b_w_out": m_b_w_out}
    v_in = {"norm_g": v_norm_g, "a_w_in": v_a_w_in, "a_conv_w": v_a_conv_w, "a_log": v_a_log,
            "a_dt_bias": v_a_dt_bias, "a_norm_g": v_a_norm_g, "a_w_out": v_a_w_out, "b_w_in": v_b_w_in,
            "b_q_norm_g": v_b_q_norm_g, "b_k_norm_g": v_b_k_norm_g, "b_w_out": v_b_w_out}
    names = list(weights)

    delta_w, new_m, new_v = {}, {}, {}
    for nm in big:
        mine, other = big_halves[nm]
        grads[nm], delta_w[nm], new_m[nm], new_v[nm] = _adamw_shard(
            weights[nm], mine, other, m_in[nm], v_in[nm], half_index, f"adamw_{nm}")
    small_names = [nm for nm in names if nm not in big]
    packs = [_pack_rows([src[nm] for nm in small_names]) for src in (weights, grads, m_in, v_in)]
    offs = packs[0][1]
    dl, m2, v2 = _adamw(packs[0][0], packs[1][0], packs[2][0], packs[3][0], "adamw_small")
    for nm, a, b, c2 in zip(small_names, _unpack_rows(dl, offs), _unpack_rows(m2, offs), _unpack_rows(v2, offs)):
        delta_w[nm], new_m[nm], new_v[nm] = a, b, c2

    return (loss, d_x0, *[grads[nm] for nm in names], *[delta_w[nm] for nm in names],
            *[new_m[nm] for nm in names], *[new_v[nm] for nm in names])
```

```python
import jax
import jax.numpy as jnp
import numpy as np
from jax import lax
from jax.experimental import pallas as pl
from jax.experimental.pallas import tpu as pltpu

F32 = jnp.float32
BF16 = jnp.bfloat16
MESH = pl.DeviceIdType.MESH

EPS = 1e-6
D_MODEL = 1024
A_HEADS = 8
A_DK = 128
A_DV = 256
A_QK = A_HEADS * A_DK
A_VW = A_HEADS * A_DV
A_MAIN = 2 * A_QK + 2 * A_VW
A_HEAD_COLS = 2 * A_DK + 2 * A_DV
A_CONV_COLS = 2 * A_DK + A_DV
A_CHUNK = 64
A_CONV = 4
B_GROUPS = 3
B_HEADS = 8
B_DH = 128
B_W = B_HEADS * B_DH
B_DIL = (1, 4, 16)
B_BLK = 128
ROPE_THETA = 500000.0
ROPE_DIMS = B_DH // 4
ADAM_LR, ADAM_B1, ADAM_B2, ADAM_EPS, ADAM_WD, ADAM_STEP = 0.001, 0.9, 0.999, 1e-08, 0.01, 10
N_CHIPS = 4
VMEM_BIG = 56 * 1024 * 1024


def _params(sem=None, vmem=None):
    return pltpu.CompilerParams(dimension_semantics=sem, vmem_limit_bytes=vmem)


def _dot(a, b, ca, cb):
    return lax.dot_general(a.astype(BF16), b.astype(BF16), (((ca,), (cb,)), ((), ())),
                           preferred_element_type=F32)


def _split3(a):
    hi = a.astype(BF16)
    r = a - hi.astype(F32)
    mid = r.astype(BF16)
    lo = (r - mid.astype(F32)).astype(BF16)
    return hi, mid, lo


def _sigmoid(y):
    return 1.0 / (1.0 + jnp.exp(-y))


def _silu(y):
    return y * _sigmoid(y)


def _dsilu(y):
    s = _sigmoid(y)
    return s * (1.0 + y * (1.0 - s))


def _matmul(a, b, mode, out_dtype, name, res=None, tm=1024, tn=1024, tk=1024, n=None, b_spec=None, into=None):
    m, k = a.shape
    if n is None:
        n = b.shape[1] if mode == "nn" else b.shape[0]
    tm, tn, tk = min(tm, m), min(tn, n), min(tk, k)
    assert m % tm == 0 and n % tn == 0 and k % tk == 0, (name, a.shape, b.shape)
    nk = k // tk
    dims = {"nn": ((1,), (0,)), "nt": ((1,), (1,))}[mode]

    def body(*refs):
        a_ref, b_ref = refs[0], refs[1]
        r_ref = refs[2] if res is not None else None
        o_ref = refs[2 + (res is not None) + (into is not None)]
        prod = lax.dot_general(a_ref[...], b_ref[...], (dims, ((), ())), preferred_element_type=F32)

        def finish(r):
            if res is not None:
                r = r + r_ref[...]
            o_ref[...] = r.astype(out_dtype)

        if nk == 1:
            finish(prod)
            return
        acc = refs[-1]
        kk = pl.program_id(2)

        @pl.when(kk == 0)
        def _():
            acc[...] = prod

        @pl.when((kk > 0) & (kk < nk - 1))
        def _():
            acc[...] += prod

        @pl.when(kk == nk - 1)
        def _():
            finish(acc[...] + prod)

    a_spec = pl.BlockSpec((tm, tk), lambda i, j, kk: (i, kk))
    if b_spec is None and mode == "nt":
        b_spec = pl.BlockSpec((tn, tk), lambda i, j, kk: (j, kk))
    elif b_spec is None:
        b_spec = pl.BlockSpec((tk, tn), lambda i, j, kk: (kk, j))
    in_specs = [a_spec, b_spec]
    args = [a, b]
    if res is not None:
        in_specs.append(pl.BlockSpec((tm, tn), lambda i, j, kk: (i, j)))
        args.append(res)
    out_spec = pl.BlockSpec((tm, tn), lambda i, j, kk: (i, j))
    out_shape = jax.ShapeDtypeStruct((m, n), out_dtype)
    aliases = {}
    if into is not None:
        assert res is None
        buf, out_spec = into
        out_shape = jax.ShapeDtypeStruct(buf.shape, buf.dtype)
        in_specs.append(ANY)
        args.append(buf)
        aliases = {2: 0}
    return pl.pallas_call(
        body, name=name, grid=(m // tm, n // tn, nk),
        in_specs=in_specs, out_specs=out_spec, out_shape=out_shape, input_output_aliases=aliases,
        scratch_shapes=[pltpu.VMEM((tm, tn), F32)] if nk > 1 else [],
        compiler_params=_params(("parallel", "parallel", "arbitrary"), 48 * 1024 * 1024),
    )(*args)


def _rms_fwd(x, g, name, tm=256):
    t, d = x.shape

    def body(x_ref, g_ref, h_ref):
        xv = x_ref[...]
        r = lax.rsqrt(jnp.mean(xv * xv, axis=-1, keepdims=True) + EPS)
        h_ref[...] = (xv * r * g_ref[...]).astype(BF16)

    return pl.pallas_call(
        body, name=name, grid=(t // tm,),
        in_specs=[pl.BlockSpec((tm, d), lambda i: (i, 0)), pl.BlockSpec((1, d), lambda i: (0, 0))],
        out_specs=pl.BlockSpec((tm, d), lambda i: (i, 0)),
        out_shape=jax.ShapeDtypeStruct((t, d), BF16),
        compiler_params=_params(("parallel",)),
    )(x, g)


def _rms_bwd(x, g, dhs, dres, name, tm=256):
    t, d = x.shape
    n_dh = len(dhs)

    def body(*refs):
        x_ref, g_ref = refs[0], refs[1]
        dh_refs = refs[2:2 + n_dh]
        dres_ref, dx_ref, dg_ref = refs[2 + n_dh:]
        i = pl.program_id(0)

        @pl.when(i == 0)
        def _():
            dg_ref[...] = jnp.zeros_like(dg_ref)

        xv = x_ref[...]
        r = lax.rsqrt(jnp.mean(xv * xv, axis=-1, keepdims=True) + EPS)
        xh = xv * r
        dh = dh_refs[0][...]
        for ref in dh_refs[1:]:
            dh = dh + ref[...]
        dg_ref[0:1, :] += jnp.sum(dh * xh, axis=0, keepdims=True)
        dxh = dh * g_ref[...]
        dx = r * (dxh - xh * jnp.mean(dxh * xh, axis=-1, keepdims=True))
        dx_ref[...] = dx + dres_ref[...]

    row = pl.BlockSpec((tm, d), lambda i: (i, 0))
    dx, dg = pl.pallas_call(
        body, name=name, grid=(t // tm,),
        in_specs=[row, pl.BlockSpec((1, d), lambda i: (0, 0))] + [row] * n_dh + [row],
        out_specs=[row, pl.BlockSpec((8, d), lambda i: (0, 0))],
        out_shape=[jax.ShapeDtypeStruct((t, d), F32), jax.ShapeDtypeStruct((8, d), F32)],
        compiler_params=_params(("arbitrary",)),
    )(x, g, *dhs, dres)
    return dx, dg[0:1]


def _loss_grad(y, target, name="loss_grad", tm=256):
    t, d = y.shape
    nb = t // tm

    def body(y_ref, t_ref, dy_ref, part_ref):
        e = y_ref[...] - t_ref[...]
        dy_ref[...] = e * (1.0 / d)
        s = jnp.sum(jnp.sum(e * e, axis=1, keepdims=True), axis=0, keepdims=True) * (0.5 / d)
        part_ref[...] = jnp.broadcast_to(s, (8, 128))

    row = pl.BlockSpec((tm, d), lambda i: (i, 0))
    dy, part = pl.pallas_call(
        body, name=name, grid=(nb,), in_specs=[row, row],
        out_specs=[row, pl.BlockSpec((None, 8, 128), lambda i: (i, 0, 0))],
        out_shape=[jax.ShapeDtypeStruct((t, d), F32), jax.ShapeDtypeStruct((nb, 8, 128), F32)],
        compiler_params=_params(("parallel",)),
    )(y, target)
    return dy, part[:, 0, 0]


def _softplus(x):
    t = jnp.exp(-jnp.abs(x))
    return jnp.maximum(x, 0.0) + jnp.where(t < 1e-3, t * (1.0 - 0.5 * t), jnp.log(1.0 + t))


def _tri(rows_le_cols):
    r = lax.broadcasted_iota(jnp.int32, (A_CHUNK, A_CHUNK), 0)
    c = lax.broadcasted_iota(jnp.int32, (A_CHUNK, A_CHUNK), 1)
    return jnp.where((r <= c) if rows_le_cols else (r >= c), 1.0, 0.0).astype(BF16)


def _dot_exact_rhs(a, ones_bf16):
    dn = (((1,), (0,)), ((), ()))
    hi, mid, lo = _split3(a)
    out = lax.dot_general(hi, ones_bf16, dn, preferred_element_type=F32)
    out = out + lax.dot_general(mid, ones_bf16, dn, preferred_element_type=F32)
    return out + lax.dot_general(lo, ones_bf16, dn, preferred_element_type=F32)


def _gdn_prep(tail_t, a_log, dt_bias):
    bn, _, n, c = tail_t.shape

    def body(t_ref, alog_ref, dtb_ref, beta_ref, gc_ref):
        upper = _tri(True)
        for h in range(A_HEADS):
            beta_ref[h] = _sigmoid(t_ref[h])
            ea = jnp.exp(jnp.full((n, c), alog_ref[h], F32))
            g = -ea * _softplus(t_ref[A_HEADS + h] + dtb_ref[h])
            gc_ref[h] = _dot_exact_rhs(g, upper)

    smem = pl.BlockSpec(memory_space=pltpu.SMEM)
    blk = pl.BlockSpec((None, A_HEADS, n, c), lambda b: (b, 0, 0, 0))
    return pl.pallas_call(
        body, name="gdn_prep", grid=(bn,),
        in_specs=[pl.BlockSpec((None, 2 * A_HEADS, n, c), lambda b: (b, 0, 0, 0)), smem, smem],
        out_specs=[blk, blk],
        out_shape=[jax.ShapeDtypeStruct((bn, A_HEADS, n, c), F32)] * 2,
        compiler_params=_params(("parallel",)),
    )(tail_t, a_log, dt_bias)


def _gdn_prep_bwd(tail_t, a_log, dt_bias, d_gc, d_beta):
    bn, _, n, c = tail_t.shape

    def body(t_ref, alog_ref, dtb_ref, dgc_ref, dbeta_ref, dt_ref, dal_ref, ddt_ref):
        lower = _tri(False)
        for h in range(A_HEADS):
            beta = _sigmoid(t_ref[h])
            dt_ref[h] = dbeta_ref[h] * beta * (1.0 - beta)
            dg = _dot_exact_rhs(dgc_ref[h], lower)
            ea = jnp.exp(jnp.full((n, c), alog_ref[h], F32))
            xa = t_ref[A_HEADS + h] + dtb_ref[h]
            g = -ea * _softplus(xa)
            dxa = -ea * dg * _sigmoid(xa)
            dt_ref[A_HEADS + h] = dxa
            s1 = jnp.sum(jnp.sum(g * dg, axis=1, keepdims=True), axis=0, keepdims=True)
            s2 = jnp.sum(jnp.sum(dxa, axis=1, keepdims=True), axis=0, keepdims=True)
            dal_ref[h:h + 1, :] = jnp.broadcast_to(s1, (1, 128))
            ddt_ref[h:h + 1, :] = jnp.broadcast_to(s2, (1, 128))

    smem = pl.BlockSpec(memory_space=pltpu.SMEM)
    blk8 = pl.BlockSpec((None, A_HEADS, n, c), lambda b: (b, 0, 0, 0))
    blk16 = pl.BlockSpec((None, 2 * A_HEADS, n, c), lambda b: (b, 0, 0, 0))
    sm = pl.BlockSpec((None, A_HEADS, 128), lambda b: (b, 0, 0))
    return pl.pallas_call(
        body, name="gdn_prep_bwd", grid=(bn,),
        in_specs=[blk16, smem, smem, blk8, blk8],
        out_specs=[blk16, sm, sm],
        out_shape=[jax.ShapeDtypeStruct((bn, 2 * A_HEADS, n, c), F32),
                   jax.ShapeDtypeStruct((bn, A_HEADS, 128), F32),
                   jax.ShapeDtypeStruct((bn, A_HEADS, 128), F32)],
        compiler_params=_params(("parallel",)),
    )(tail_t, a_log, dt_bias, d_gc, d_beta)


HALO = 8


def _conv_taps(xw, w):
    y = w[A_CONV - 1:A_CONV, :] * xw
    for j in range(1, A_CONV):
        y = y + w[A_CONV - 1 - j:A_CONV - j, :] * pltpu.roll(xw, j, 0)
    return y[HALO:, :]


def _row_to_col(row, eye):
    c = eye.shape[0]
    return jnp.sum(jnp.where(eye, jnp.broadcast_to(row, (c, c)), 0.0), axis=1, keepdims=True)


def _col_to_row(col, eye):
    c = eye.shape[0]
    return jnp.sum(jnp.where(eye, jnp.broadcast_to(col, (c, c)), 0.0), axis=0, keepdims=True)


def _unit_lower_inverse(a, ri, ci):
    eye = jnp.where(ri == ci, 1.0, 0.0)
    a8 = jnp.where((ri >> 3) == (ci >> 3), a, 0.0)
    a2 = _dot(a8, a8, 1, 0)
    yield
    a4 = _dot(a2, a2, 1, 0)
    t = eye - a8
    t = t + _dot(t, a2, 1, 0)
    yield
    t = t + _dot(t, a4, 1, 0)
    yield
    for sh in (3, 4, 5):
        off = jnp.where(((ri >> (sh + 1)) == (ci >> (sh + 1))) & ((ri >> sh) != (ci >> sh)), a, 0.0)
        left = _dot(t, off, 1, 0)
        yield
        t = t - _dot(left, t, 1, 0)
        yield
    return t


def _round_robin(gens):
    live = list(gens)
    while live:
        nxt = []
        for g in live:
            try:
                next(g)
                nxt.append(g)
            except StopIteration:
                pass
        live = nxt


def _gdn_chunk_core(q, k, v, g_row, b_row, t_mat, ri, ci):
    eye = ri == ci
    g_col = _row_to_col(g_row, eye)
    b_col = _row_to_col(b_row, eye)
    causal = ri >= ci
    strict = ri > ci
    dec = jnp.where(causal, jnp.exp(jnp.where(causal, g_col - g_row, 0.0)), 0.0)
    gam = jnp.exp(g_col)
    g_last = g_row[:, A_CHUNK - 1:A_CHUNK]
    gam_last = jnp.exp(g_last)
    e = jnp.exp(g_last - g_col)
    kb = k * b_col
    bv = v * b_col
    kbg = kb * gam
    q16, k16, kb16 = q.astype(BF16), k.astype(BF16), kb.astype(BF16)
    kk = _dot(kb16, k16, 1, 1)
    p = _dot(q16, k16, 1, 1) * dec
    yield
    a_mat = jnp.where(strict, kk * dec, 0.0)
    if t_mat is None:
        t_mat = yield from _unit_lower_inverse(a_mat, ri, ci)
    t16 = t_mat.astype(BF16)
    u = _dot(t16, bv, 1, 0)
    w = _dot(t16, kbg, 1, 0)
    yield
    return dict(eye=eye, g_col=g_col, b_col=b_col, dec=dec, strict=strict, causal=causal, gam=gam,
                gam_last=gam_last, e=e, kb=kb, bv=bv, kbg=kbg, a_mat=a_mat, t_mat=t_mat, u=u, w=w, p=p,
                qg=q * gam, kd=k * e, q16=q16, k16=k16, kb16=kb16, t16=t16)


A_SEQ_BLK = 256
A_BLK_CHUNKS = A_SEQ_BLK // A_CHUNK


def _gdn_halo(proj_hm):
    bn, s, w = proj_hm.shape
    last = proj_hm.reshape(bn, s // A_SEQ_BLK, A_SEQ_BLK, w)[:, :, A_SEQ_BLK - HALO:, :]
    return jnp.concatenate([jnp.zeros((bn, 1, HALO, w), proj_hm.dtype), last[:, :-1]], axis=1)


def _gdn_window(x_ref, halo_ref, ci, first, lo):
    if first:
        return jnp.concatenate([halo_ref[:, lo:lo + A_CONV_COLS], x_ref[0:A_CHUNK, lo:lo + A_CONV_COLS]], axis=0)
    start = pl.multiple_of(ci * A_CHUNK - HALO, HALO)
    return x_ref[pl.ds(start, A_CHUNK + HALO), lo:lo + A_CONV_COLS]


def _gdn_chunk_prep(xw, cw, y=None):
    if y is None:
        y = _conv_taps(xw, cw)
    a = _silu(y)
    aq, ak, v = a[:, 0:A_DK], a[:, A_DK:2 * A_DK], a[:, 2 * A_DK:]
    rq = lax.rsqrt(jnp.sum(aq * aq, axis=1, keepdims=True) + EPS)
    rk = lax.rsqrt(jnp.sum(ak * ak, axis=1, keepdims=True) + EPS)
    return dict(xw=xw, y=y, aq=aq, ak=ak, rq=rq, rk=rk, q=aq * rq * (A_DK ** -0.5), k=ak * rk, v=v)


def _gdn_fwd(proj_hm, cw_hm, beta, gc, norm_g, hp=8):
    bn, s, _ = proj_hm.shape
    n = s // A_CHUNK
    nsb = s // A_SEQ_BLK
    halo = _gdn_halo(proj_hm)

    def body(x_ref, halo_ref, cw_ref, beta_ref, gc_ref, ng_ref, og_ref, oraw_ref, st_ref, t_ref, y_ref, state):
        first_chunk = pl.program_id(2) * A_BLK_CHUNKS
        ri = lax.broadcasted_iota(jnp.int32, (A_CHUNK, A_CHUNK), 0)
        ci_ = lax.broadcasted_iota(jnp.int32, (A_CHUNK, A_CHUNK), 1)
        ng = ng_ref[...]

        @pl.when(pl.program_id(2) == 0)
        def _():
            state[...] = jnp.zeros_like(state)

        def one_head(hh, ci, first, rows):
            lo = hh * A_HEAD_COLS
            cw = cw_ref[:, hh * A_CONV_COLS:(hh + 1) * A_CONV_COLS]
            cin = _gdn_chunk_prep(_gdn_window(x_ref, halo_ref, ci, first, lo), cw)
            y_ref[rows, hh * A_CONV_COLS:(hh + 1) * A_CONV_COLS] = cin["y"]
            seq_chunk = pl.ds(first_chunk + ci, 1)
            core = yield from _gdn_chunk_core(cin["q"], cin["k"], cin["v"], gc_ref[hh, seq_chunk, :],
                                              beta_ref[hh, seq_chunk, :], None, ri, ci_)
            st = state[hh]
            st_ref[hh, ci] = st
            t_ref[hh, ci] = core["t_mat"]
            st16 = st.astype(BF16)
            vn = core["u"] - _dot(core["w"], st16, 1, 0)
            qs = _dot(core["qg"], st16, 1, 0)
            yield
            vn16 = vn.astype(BF16)
            o = qs + _dot(core["p"], vn16, 1, 0)
            state[hh] = st * core["gam_last"] + _dot(core["kd"], vn16, 0, 0)
            yield
            ocols = slice(hh * A_DV, (hh + 1) * A_DV)
            oraw_ref[rows, ocols] = o
            r = lax.rsqrt(jnp.mean(o * o, axis=1, keepdims=True) + EPS)
            z = x_ref[rows, lo + A_CONV_COLS:lo + A_HEAD_COLS]
            og_ref[rows, ocols] = (o * r * ng * _silu(z)).astype(BF16)

        def chunk(ci, first):
            rows = pl.ds(0 if first else pl.multiple_of(ci * A_CHUNK, A_CHUNK), A_CHUNK)
            _round_robin([one_head(hh, ci, first, rows) for hh in range(hp)])

        chunk(0, True)
        lax.fori_loop(1, A_BLK_CHUNKS, lambda i, c: (chunk(i, False), c)[1], 0)

    small = pl.BlockSpec((None, hp, n, A_CHUNK), lambda b, h, j: (b, h, 0, 0))
    return pl.pallas_call(
        body, name="gdn_fwd", grid=(bn, A_HEADS // hp, nsb),
        in_specs=[pl.BlockSpec((None, A_SEQ_BLK, hp * A_HEAD_COLS), lambda b, h, j: (b, j, h)),
                  pl.BlockSpec((None, None, HALO, hp * A_HEAD_COLS), lambda b, h, j: (b, j, 0, h)),
                  pl.BlockSpec((A_CONV, hp * A_CONV_COLS), lambda b, h, j: (0, h)),
                  small, small,
                  pl.BlockSpec((1, A_DV), lambda b, h, j: (0, 0))],
        out_specs=[pl.BlockSpec((None, A_SEQ_BLK, hp * A_DV), lambda b, h, j: (b, j, h)),
                   pl.BlockSpec((None, A_SEQ_BLK, hp * A_DV), lambda b, h, j: (b, j, h)),
                   pl.BlockSpec((None, hp, A_BLK_CHUNKS, A_DK, A_DV), lambda b, h, j: (b, h, j, 0, 0)),
                   pl.BlockSpec((None, hp, A_BLK_CHUNKS, A_CHUNK, A_CHUNK), lambda b, h, j: (b, h, j, 0, 0)),
                   pl.BlockSpec((None, A_SEQ_BLK, hp * A_CONV_COLS), lambda b, h, j: (b, j, h))],
        out_shape=[jax.ShapeDtypeStruct((bn, s, A_VW), BF16),
                   jax.ShapeDtypeStruct((bn, s, A_VW), F32),
                   jax.ShapeDtypeStruct((bn, A_HEADS, n, A_DK, A_DV), F32),
                   jax.ShapeDtypeStruct((bn, A_HEADS, n, A_CHUNK, A_CHUNK), F32),
                   jax.ShapeDtypeStruct((bn, s, A_HEADS * A_CONV_COLS), F32)],
        scratch_shapes=[pltpu.VMEM((hp, A_DK, A_DV), F32)],
        compiler_params=_params(("parallel", "parallel", "arbitrary"), VMEM_BIG),
    )(proj_hm, halo, cw_hm, beta, gc, norm_g)


def _gdn_bwd(proj_hm, cw_hm, beta, gc, norm_g, oraw, states, t_mats, conv_y, dog, hp=4):
    bn, s, _ = proj_hm.shape
    n = s // A_CHUNK
    nsb = s // A_SEQ_BLK
    halo = _gdn_halo(proj_hm)

    def body(x_ref, halo_ref, cw_ref, beta_ref, gc_ref, ng_ref, oraw_ref, st_ref, t_ref, y_ref, dog_ref,
             dx_ref, dgc_ref, dbeta_ref, dcw_ref, dng_ref, dstate, dy_next, shifted):
        first_chunk = (nsb - 1 - pl.program_id(2)) * A_BLK_CHUNKS
        ri = lax.broadcasted_iota(jnp.int32, (A_CHUNK, A_CHUNK), 0)
        ci_ = lax.broadcasted_iota(jnp.int32, (A_CHUNK, A_CHUNK), 1)
        lane = lax.broadcasted_iota(jnp.int32, (1, A_CHUNK), 1)
        ng = ng_ref[...]

        @pl.when(pl.program_id(2) == 0)
        def _():
            dstate[...] = jnp.zeros_like(dstate)
            dy_next[...] = jnp.zeros_like(dy_next)
            dcw_ref[...] = jnp.zeros_like(dcw_ref)
            dng_ref[...] = jnp.zeros_like(dng_ref)

        def one_head(hh, ci, first, rows):
            lo = hh * A_HEAD_COLS
            ccols = slice(hh * A_CONV_COLS, (hh + 1) * A_CONV_COLS)
            ocols = slice(hh * A_DV, (hh + 1) * A_DV)
            cw = cw_ref[:, ccols]
            cin = _gdn_chunk_prep(_gdn_window(x_ref, halo_ref, ci, first, lo), cw, y_ref[rows, ccols])
            q, k, v = cin["q"], cin["k"], cin["v"]
            seq_chunk = pl.ds(first_chunk + ci, 1)
            cr = yield from _gdn_chunk_core(q, k, v, gc_ref[hh, seq_chunk, :], beta_ref[hh, seq_chunk, :],
                                            t_ref[hh, ci], ri, ci_)
            eye, dec, gam, e = cr["eye"], cr["dec"], cr["gam"], cr["e"]
            b_col, t_mat, u, w, p = cr["b_col"], cr["t_mat"], cr["u"], cr["w"], cr["p"]
            st = st_ref[hh, ci]
            ds_out = dstate[hh]

            o = oraw_ref[rows, ocols]
            z = x_ref[rows, lo + A_CONV_COLS:lo + A_HEAD_COLS]
            d_og = dog_ref[rows, ocols]
            r = lax.rsqrt(jnp.mean(o * o, axis=1, keepdims=True) + EPS)
            oh = o * r
            d_on = d_og * _silu(z)
            dz = d_og * oh * ng * _dsilu(z)
            dng_ref[hh, 0:1, :] += jnp.sum(d_on * oh, axis=0, keepdims=True)
            d_oh = d_on * ng
            d_o = r * (d_oh - oh * jnp.mean(d_oh * oh, axis=1, keepdims=True))

            st16, ds16, do16, w16 = st.astype(BF16), ds_out.astype(BF16), d_o.astype(BF16), w.astype(BF16)
            q16, k16, t16 = cr["q16"], cr["k16"], cr["t16"]
            vn = u - _dot(w16, st16, 1, 0)
            d_vn = _dot(p, do16, 0, 0) + _dot(cr["kd"], ds16, 1, 0)
            d_qg = _dot(do16, st16, 1, 1)
            qgdo = _dot(cr["qg"], do16, 0, 0)
            yield
            vn16, dvn16 = vn.astype(BF16), d_vn.astype(BF16)
            d_p = jnp.where(cr["causal"], _dot(do16, vn16, 1, 1), 0.0)
            d_kd = _dot(vn16, ds16, 1, 1)
            d_gam_last = jnp.sum(jnp.sum(st * ds_out, axis=1, keepdims=True), axis=0, keepdims=True)
            d_w = -_dot(dvn16, st16, 1, 1)
            dstate[hh] = qgdo + ds_out * cr["gam_last"] - _dot(w16, dvn16, 0, 0)
            d_bv = _dot(t16, dvn16, 0, 0)
            yield
            d_kbg = _dot(t16, d_w, 0, 0)
            n_p = (d_p * dec).astype(BF16)
            d_q = _dot(n_p, k16, 1, 0) + d_qg * gam
            npq = _dot(n_p, q16, 0, 0)
            yield
            d_a = jnp.where(cr["strict"], -(_dot(d_bv, u, 1, 1) + _dot(d_kbg, w16, 1, 1)), 0.0)
            yield
            m_a = (d_a * dec).astype(BF16)
            d_kb = _dot(m_a, k16, 1, 0) + d_kbg * gam
            d_k = (_dot(m_a, cr["kb16"], 0, 0) + npq + d_kd * e + d_kb * b_col)
            yield
            d_v = d_bv * b_col
            d_beta_col = (jnp.sum(d_bv * v, axis=1, keepdims=True)
                          + jnp.sum(d_kb * k, axis=1, keepdims=True))
            gterm = d_a * cr["a_mat"] + d_p * p
            d_e = jnp.sum(d_kd * k, axis=1, keepdims=True) * e
            d_g_col = (jnp.sum(gterm, axis=1, keepdims=True)
                       + (jnp.sum(d_qg * q, axis=1, keepdims=True)
                          + jnp.sum(d_kbg * cr["kb"], axis=1, keepdims=True)) * gam
                       - d_e)
            d_g_last = jnp.sum(d_e, axis=0, keepdims=True) + d_gam_last * cr["gam_last"]
            d_g_row = (_col_to_row(d_g_col, eye) - jnp.sum(gterm, axis=0, keepdims=True)
                       + jnp.where(lane == A_CHUNK - 1, d_g_last, 0.0))
            dgc_ref[hh, seq_chunk, :] = d_g_row
            dbeta_ref[hh, seq_chunk, :] = _col_to_row(d_beta_col, eye)

            qh = cin["aq"] * cin["rq"]
            kh = cin["ak"] * cin["rk"]
            d_qh = d_q * (A_DK ** -0.5)
            d_aq = cin["rq"] * (d_qh - qh * jnp.sum(d_qh * qh, axis=1, keepdims=True))
            d_ak = cin["rk"] * (d_k - kh * jnp.sum(d_k * kh, axis=1, keepdims=True))
            d_y = jnp.concatenate([d_aq, d_ak, d_v], axis=1) * _dsilu(cin["y"])
            shifted[hh, 0, 0:A_CHUNK, :] = d_y
            shifted[hh, 0, A_CHUNK:A_CHUNK + HALO, :] = dy_next[hh]
            shifted[hh, 1, 0:A_CHUNK + HALO, :] = cin["xw"]
            d_x = cw[A_CONV - 1:A_CONV, :] * d_y
            for j in range(1, A_CONV):
                d_x = d_x + cw[A_CONV - 1 - j:A_CONV - j, :] * shifted[hh, 0, j:j + A_CHUNK, :]
            for j in range(A_CONV):
                xs = shifted[hh, 1, HALO - j:HALO - j + A_CHUNK, :]
                dcw_ref[A_CONV - 1 - j:A_CONV - j, ccols] += jnp.sum(d_y * xs, axis=0, keepdims=True)
            dy_next[hh] = d_y[0:HALO, :]
            dx_ref[rows, lo:lo + A_CONV_COLS] = d_x.astype(BF16)
            dx_ref[rows, lo + A_CONV_COLS:lo + A_HEAD_COLS] = dz.astype(BF16)

        def chunk(ci, first):
            rows = pl.ds(0 if first else pl.multiple_of(ci * A_CHUNK, A_CHUNK), A_CHUNK)
            _round_robin([one_head(hh, ci, first, rows) for hh in range(hp)])

        lax.fori_loop(0, A_BLK_CHUNKS - 1, lambda i, c: (chunk(A_BLK_CHUNKS - 1 - i, False), c)[1], 0)
        chunk(0, True)

    rev = lambda j: nsb - 1 - j
    small = pl.BlockSpec((None, hp, n, A_CHUNK), lambda b, h, j: (b, h, 0, 0))
    wide = pl.BlockSpec((None, A_SEQ_BLK, hp * A_HEAD_COLS), lambda b, h, j: (b, rev(j), h))
    val = pl.BlockSpec((None, A_SEQ_BLK, hp * A_DV), lambda b, h, j: (b, rev(j), h))
    return pl.pallas_call(
        body, name="gdn_bwd", grid=(bn, A_HEADS // hp, nsb),
        in_specs=[wide,
                  pl.BlockSpec((None, None, HALO, hp * A_HEAD_COLS), lambda b, h, j: (b, rev(j), 0, h)),
                  pl.BlockSpec((A_CONV, hp * A_CONV_COLS), lambda b, h, j: (0, h)),
                  small, small,
                  pl.BlockSpec((1, A_DV), lambda b, h, j: (0, 0)),
                  val,
                  pl.BlockSpec((None, hp, A_BLK_CHUNKS, A_DK, A_DV), lambda b, h, j: (b, h, rev(j), 0, 0)),
                  pl.BlockSpec((None, hp, A_BLK_CHUNKS, A_CHUNK, A_CHUNK), lambda b, h, j: (b, h, rev(j), 0, 0)),
                  pl.BlockSpec((None, A_SEQ_BLK, hp * A_CONV_COLS), lambda b, h, j: (b, rev(j), h)),
                  val],
        out_specs=[wide, small, small,
                   pl.BlockSpec((None, A_CONV, hp * A_CONV_COLS), lambda b, h, j: (b, 0, h)),
                   pl.BlockSpec((None, hp, 8, A_DV), lambda b, h, j: (b, h, 0, 0))],
        out_shape=[jax.ShapeDtypeStruct((bn, s, A_HEADS * A_HEAD_COLS), BF16),
                   jax.ShapeDtypeStruct((bn, A_HEADS, n, A_CHUNK), F32),
                   jax.ShapeDtypeStruct((bn, A_HEADS, n, A_CHUNK), F32),
                   jax.ShapeDtypeStruct((bn, A_CONV, A_HEADS * A_CONV_COLS), F32),
                   jax.ShapeDtypeStruct((bn, A_HEADS, 8, A_DV), F32)],
        scratch_shapes=[pltpu.VMEM((hp, A_DK, A_DV), F32), pltpu.VMEM((hp, HALO, A_CONV_COLS), F32),
                        pltpu.VMEM((hp, 2, A_CHUNK + 2 * HALO, A_CONV_COLS), F32)],
        compiler_params=_params(("parallel", "parallel", "arbitrary"), VMEM_BIG),
    )(proj_hm, halo, cw_hm, beta, gc, norm_g, oraw, states, t_mats, conv_y, dog)


def _rope_tables(posf, inv_freq_row):
    t = posf.shape[0]
    tm = 512

    def body(p_ref, f_ref, c_ref, sa_ref, sb_ref):
        ang = p_ref[...] * f_ref[...]
        lane = lax.broadcasted_iota(jnp.int32, ang.shape, 1)
        half = ROPE_DIMS // 2
        c_ref[...] = jnp.where(lane < ROPE_DIMS, jnp.cos(ang), 1.0)
        sn = jnp.sin(ang)
        sa_ref[...] = jnp.where(lane < half, -sn, 0.0)
        sb_ref[...] = jnp.where((lane >= half) & (lane < ROPE_DIMS), sn, 0.0)

    row = pl.BlockSpec((tm, 128), lambda i: (i, 0))
    return pl.pallas_call(
        body, name="rope_tables", grid=(t // tm,),
        in_specs=[row, pl.BlockSpec((1, 128), lambda i: (0, 0))], out_specs=[row] * 3,
        out_shape=[jax.ShapeDtypeStruct((t, 128), F32)] * 3,
        compiler_params=_params(("parallel",)),
    )(posf, inv_freq_row)


def _rope(x, c, sa, sb):
    half = ROPE_DIMS // 2
    return x * c + pltpu.roll(x, 128 - half, 1) * sa + pltpu.roll(x, half, 1) * sb


def _rope_t(d, c, sa, sb):
    half = ROPE_DIMS // 2
    return d * c + pltpu.roll(d * sa, half, 1) + pltpu.roll(d * sb, 128 - half, 1)


def _qk_prep(proj, c, sa, sb, qg, kg, name, tm=256):
    t = proj.shape[0]
    wide = proj.shape[1]

    def body(x_ref, c_ref, sa_ref, sb_ref, qg_ref, kg_ref, o_ref):
        cc, s1, s2 = c_ref[...], sa_ref[...], sb_ref[...]
        for which, g_ref in ((0, qg_ref), (1, kg_ref)):
            g = g_ref[...]
            for h in range(B_HEADS):
                lo = which * B_W + h * B_DH
                xv = x_ref[:, lo:lo + B_DH]
                r = lax.rsqrt(jnp.mean(xv * xv, axis=1, keepdims=True) + EPS)
                o_ref[:, lo:lo + B_DH] = _rope(xv * r * g, cc, s1, s2).astype(BF16)
        o_ref[:, 2 * B_W:3 * B_W] = x_ref[:, 2 * B_W:3 * B_W].astype(BF16)

    tab = pl.BlockSpec((tm, 128), lambda i: (i, 0))
    gain = pl.BlockSpec((1, B_DH), lambda i: (0, 0))
    return pl.pallas_call(
        body, name=name, grid=(t // tm,),
        in_specs=[pl.BlockSpec((tm, wide), lambda i: (i, 0)), tab, tab, tab, gain, gain],
        out_specs=pl.BlockSpec((tm, 3 * B_W), lambda i: (i, 0)),
        out_shape=jax.ShapeDtypeStruct((t, 3 * B_W), BF16),
        compiler_params=_params(("parallel",), 40 * 1024 * 1024),
    )(proj, c, sa, sb, qg, kg)


def _qk_prep_bwd(proj, c, sa, sb, qg, kg, dq, dk, dv, dz, name, tm=256):
    t = proj.shape[0]
    wide = proj.shape[1]
    out_w = 3 * B_W + (B_W if dz is not None else 0)

    def body(*refs):
        x_ref, c_ref, sa_ref, sb_ref, qg_ref, kg_ref, dq_ref, dk_ref, dv_ref = refs[:9]
        if dz is not None:
            dz_ref, o_ref, dgain_ref = refs[9:]
        else:
            o_ref, dgain_ref = refs[9:]
        i = pl.program_id(0)

        @pl.when(i == 0)
        def _():
            dgain_ref[...] = jnp.zeros_like(dgain_ref)

        cc, s1, s2 = c_ref[...], sa_ref[...], sb_ref[...]
        for which, g_ref, d_ref in ((0, qg_ref, dq_ref), (1, kg_ref, dk_ref)):
            g = g_ref[...]
            acc = jnp.zeros((1, B_DH), F32)
            for h in range(B_HEADS):
                lo = which * B_W + h * B_DH
                xv = x_ref[:, lo:lo + B_DH]
                r = lax.rsqrt(jnp.mean(xv * xv, axis=1, keepdims=True) + EPS)
                xh = xv * r
                d_xn = _rope_t(d_ref[:, h * B_DH:(h + 1) * B_DH], cc, s1, s2)
                acc = acc + jnp.sum(d_xn * xh, axis=0, keepdims=True)
                d_xh = d_xn * g
                d_x = r * (d_xh - xh * jnp.mean(d_xh * xh, axis=1, keepdims=True))
                o_ref[:, lo:lo + B_DH] = d_x.astype(BF16)
            dgain_ref[which:which + 1, :] += acc
        o_ref[:, 2 * B_W:3 * B_W] = dv_ref[...].astype(BF16)
        if dz is not None:
            o_ref[:, 3 * B_W:4 * B_W] = dz_ref[...]

    tab = pl.BlockSpec((tm, 128), lambda i: (i, 0))
    gain = pl.BlockSpec((1, B_DH), lambda i: (0, 0))
    grad = pl.BlockSpec((tm, B_W), lambda i: (i, 0))
    in_specs = [pl.BlockSpec((tm, wide), lambda i: (i, 0)), tab, tab, tab, gain, gain, grad, grad, grad]
    args = [proj, c, sa, sb, qg, kg, dq, dk, dv]
    if dz is not None:
        in_specs.append(grad)
        args.append(dz)
    return pl.pallas_call(
        body, name=name, grid=(t // tm,), in_specs=in_specs,
        out_specs=[pl.BlockSpec((tm, out_w), lambda i: (i, 0)), pl.BlockSpec((8, B_DH), lambda i: (0, 0))],
        out_shape=[jax.ShapeDtypeStruct((t, out_w), BF16), jax.ShapeDtypeStruct((8, B_DH), F32)],
        compiler_params=_params(("arbitrary",), 40 * 1024 * 1024),
    )(*args)


def _attn_masks():
    qi = lax.broadcasted_iota(jnp.int32, (B_BLK, 2 * B_BLK), 0)
    kj = lax.broadcasted_iota(jnp.int32, (B_BLK, 2 * B_BLK), 1)
    two = (kj >= qi) & (kj <= qi + B_BLK)
    q1 = lax.broadcasted_iota(jnp.int32, (B_BLK, B_BLK), 0)
    k1 = lax.broadcasted_iota(jnp.int32, (B_BLK, B_BLK), 1)
    return k1 <= q1, two


def _lane_pick(ref_rows, h):
    lane = lax.broadcasted_iota(jnp.int32, ref_rows.shape, 1)
    return jnp.sum(jnp.where(lane == h, ref_rows, 0.0), axis=1, keepdims=True)


B_ROWS = 2048


def _attn_schedule(nb, sb, block):
    way = 4

    def run(items):
        for at in range(0, len(items), way):
            _round_robin([block(*it) for it in items[at:at + way]])

    run([(si, 0, True) for si in range(sb)])
    if nb == 1:
        return
    per = max(1, way // sb)
    lead = 1 + (nb - 1) % per
    if lead > 1:
        run([(si, i, False) for i in range(1, lead) for si in range(sb)])

    def step(it, carry):
        run([(si, lead + it * per + u, False) for u in range(per) for si in range(sb)])
        return carry

    lax.fori_loop(0, (nb - lead) // per, step, 0)


def _attn_rows(i, first):
    if first:
        return pl.ds(0, B_BLK), pl.ds(0, B_BLK)
    rows = pl.ds(pl.multiple_of(i * B_BLK, B_BLK), B_BLK)
    return rows, pl.ds(pl.multiple_of((i - 1) * B_BLK, B_BLK), 2 * B_BLK)


def _attn_fwd(qkv, name):
    ns, ln, _ = qkv.shape
    nb = ln // B_BLK
    sb = B_ROWS // ln
    scale = B_DH ** -0.5

    def body(q_ref, k_ref, v_ref, o_ref, lse_ref):
        h = pl.program_id(1)
        mask1, mask2 = _attn_masks()
        lane = lax.broadcasted_iota(jnp.int32, (B_BLK, B_HEADS), 1)

        @pl.when(h == 0)
        def _():
            lse_ref[...] = jnp.zeros_like(lse_ref)

        def block(si, i, first):
            rows, win = _attn_rows(i, first)
            mask = mask1 if first else mask2
            sc = jnp.where(mask, _dot(q_ref[si, rows, :], k_ref[si, win, :], 1, 1) * scale, -1e30)
            yield
            m = jnp.max(sc, axis=1, keepdims=True)
            p = jnp.exp(sc - m)
            l = jnp.sum(p, axis=1, keepdims=True)
            pv = _dot(p, v_ref[si, win, :], 1, 0)
            yield
            o_ref[si, rows, :] = pv / l
            lse_ref[si, rows, :] = jnp.where(lane == h, m + jnp.log(l), lse_ref[si, rows, :])

        _attn_schedule(nb, sb, block)

    head = lambda off: pl.BlockSpec((sb, ln, B_DH), lambda s, h: (s, 0, off + h))
    return pl.pallas_call(
        body, name=name, grid=(ns // sb, B_HEADS),
        in_specs=[head(0), head(B_HEADS), head(2 * B_HEADS)],
        out_specs=[head(0), pl.BlockSpec((sb, ln, B_HEADS), lambda s, h: (s, 0, 0))],
        out_shape=[jax.ShapeDtypeStruct((ns, ln, B_W), F32), jax.ShapeDtypeStruct((ns, ln, B_HEADS), F32)],
        compiler_params=_params(("parallel", "arbitrary")),
    )(qkv, qkv, qkv)


def _attn_bwd(qkv, d_o, lse_joint, delta, name):
    ns, ln, _ = qkv.shape
    nb = ln // B_BLK
    sb = B_ROWS // ln
    scale = B_DH ** -0.5

    def body(q_ref, k_ref, v_ref, do_ref, lj_ref, dl_ref, dq_ref, dk_ref, dv_ref):
        h = pl.program_id(1)
        mask1, mask2 = _attn_masks()
        dk_ref[...] = jnp.zeros_like(dk_ref)
        dv_ref[...] = jnp.zeros_like(dv_ref)

        def block(si, i, first):
            rows, win = _attn_rows(i, first)
            mask = mask1 if first else mask2
            q = q_ref[si, rows, :]
            d_out = do_ref[si, rows, :]
            l_col = _lane_pick(lj_ref[si, rows, :], h)
            d_col = _lane_pick(dl_ref[si, rows, :], h)
            sc = _dot(q, k_ref[si, win, :], 1, 1) * scale
            d_p = _dot(d_out, v_ref[si, win, :], 1, 1)
            yield
            p = jnp.exp(jnp.where(mask, sc - l_col, -1e30))
            d_s = p * (d_p - d_col) * scale
            d_q = _dot(d_s, k_ref[si, win, :], 1, 0)
            d_k = _dot(d_s, q, 0, 0)
            d_v = _dot(p, d_out, 0, 0)
            yield
            dq_ref[si, rows, :] = d_q
            dk_ref[si, win, :] += d_k
            dv_ref[si, win, :] += d_v

        _attn_schedule(nb, sb, block)

    head = lambda off: pl.BlockSpec((sb, ln, B_DH), lambda s, h: (s, 0, off + h))
    small = pl.BlockSpec((sb, ln, B_HEADS), lambda s, h: (s, 0, 0))
    return pl.pallas_call(
        body, name=name, grid=(ns // sb, B_HEADS),
        in_specs=[head(0), head(B_HEADS), head(2 * B_HEADS), head(0), small, small],
        out_specs=[head(0)] * 3,
        out_shape=[jax.ShapeDtypeStruct((ns, ln, B_W), F32)] * 3,
        compiler_params=_params(("parallel", "parallel")),
    )(qkv, qkv, qkv, d_o, lse_joint, delta)


def _merge_weights(lse_refs):
    ls = [r[...] for r in lse_refs]
    m = jnp.maximum(jnp.maximum(ls[0], ls[1]), ls[2])
    es = [jnp.exp(l - m) for l in ls]
    tot = es[0] + es[1] + es[2]
    return [e / tot for e in es], m + jnp.log(tot)


def _merge_fwd(outs, lses, proj0, tm=256):
    t = outs[0].shape[0]

    def body(o0, o1, o2, l0, l1, l2, z_ref, og_ref):
        wts, _ = _merge_weights((l0, l1, l2))
        for h in range(B_HEADS):
            cols = slice(h * B_DH, (h + 1) * B_DH)
            o = (wts[0][:, h:h + 1] * o0[:, cols] + wts[1][:, h:h + 1] * o1[:, cols]
                 + wts[2][:, h:h + 1] * o2[:, cols])
            og_ref[:, cols] = (o * _silu(z_ref[:, cols])).astype(BF16)

    wide = pl.BlockSpec((tm, B_W), lambda i: (i, 0))
    small = pl.BlockSpec((tm, B_HEADS), lambda i: (i, 0))
    return pl.pallas_call(
        body, name="merge_fwd", grid=(t // tm,),
        in_specs=[wide] * 3 + [small] * 3 + [pl.BlockSpec((tm, B_W), lambda i: (i, 3))],
        out_specs=wide, out_shape=jax.ShapeDtypeStruct((t, B_W), BF16),
        compiler_params=_params(("parallel",)),
    )(*outs, *lses, proj0)


def _merge_bwd(outs, lses, proj0, d_og, tm=256):
    t = outs[0].shape[0]

    def body(o0, o1, o2, l0, l1, l2, z_ref, dog_ref, do_ref, lj_ref, dl_ref, dz_ref):
        wts, lj = _merge_weights((l0, l1, l2))
        lj_ref[...] = lj
        lane = lax.broadcasted_iota(jnp.int32, (tm, B_HEADS), 1)
        delta = jnp.zeros((tm, B_HEADS), F32)
        for h in range(B_HEADS):
            cols = slice(h * B_DH, (h + 1) * B_DH)
            o = (wts[0][:, h:h + 1] * o0[:, cols] + wts[1][:, h:h + 1] * o1[:, cols]
                 + wts[2][:, h:h + 1] * o2[:, cols])
            z = z_ref[:, cols]
            d_g = dog_ref[:, cols]
            d_out = d_g * _silu(z)
            dz_ref[:, cols] = (d_g * o * _dsilu(z)).astype(BF16)
            do_ref[:, cols] = d_out.astype(BF16)
            delta = jnp.where(lane == h, jnp.sum(d_out * o, axis=1, keepdims=True), delta)
        dl_ref[...] = delta

    wide = pl.BlockSpec((tm, B_W), lambda i: (i, 0))
    small = pl.BlockSpec((tm, B_HEADS), lambda i: (i, 0))
    return pl.pallas_call(
        body, name="merge_bwd", grid=(t // tm,),
        in_specs=[wide] * 3 + [small] * 3 + [pl.BlockSpec((tm, B_W), lambda i: (i, 3)), wide],
        out_specs=[wide, small, small, wide],
        out_shape=[jax.ShapeDtypeStruct((t, B_W), BF16), jax.ShapeDtypeStruct((t, B_HEADS), F32),
                   jax.ShapeDtypeStruct((t, B_HEADS), F32), jax.ShapeDtypeStruct((t, B_W), BF16)],
        compiler_params=_params(("parallel",)),
    )(*outs, *lses, proj0, d_og)


def _adamw(w, g, m, v, name):
    r, c = w.shape
    tr = r
    for cand in (256, 128, 64, 32, 16, 8):
        if r % cand == 0:
            tr = cand
            break

    def body(w_ref, g_ref, m_ref, v_ref, d_ref, nm_ref, nv_ref):
        gv = g_ref[...]
        nm = ADAM_B1 * m_ref[...] + (1.0 - ADAM_B1) * gv
        nv = ADAM_B2 * v_ref[...] + (1.0 - ADAM_B2) * (gv * gv)
        m_hat = nm / (1.0 - ADAM_B1 ** ADAM_STEP)
        v_hat = nv / (1.0 - ADAM_B2 ** ADAM_STEP)
        d_ref[...] = -ADAM_LR * (m_hat / (jnp.sqrt(v_hat) + ADAM_EPS) + ADAM_WD * w_ref[...])
        nm_ref[...] = nm
        nv_ref[...] = nv

    blk = pl.BlockSpec((tr, c), lambda i: (i, 0))
    return pl.pallas_call(
        body, name=name, grid=(r // tr,), in_specs=[blk] * 4, out_specs=[blk] * 3,
        out_shape=[jax.ShapeDtypeStruct((r, c), F32)] * 3,
        compiler_params=_params(("parallel",)),
    )(w, g, m, v)


def _adam_update(w, gv, m, v):
    nm = ADAM_B1 * m + (1.0 - ADAM_B1) * gv
    nv = ADAM_B2 * v + (1.0 - ADAM_B2) * (gv * gv)
    m_hat = nm / (1.0 - ADAM_B1 ** ADAM_STEP)
    v_hat = nv / (1.0 - ADAM_B2 ** ADAM_STEP)
    return -ADAM_LR * (m_hat / (jnp.sqrt(v_hat) + ADAM_EPS) + ADAM_WD * w), nm, nv


def _adamw_shard(w, mine, theirs, m, v, half_index, name, tr=128):
    _, r, c = w.shape
    nhb = (r // 2) // tr

    def body(c_ref, w_ref, mine_ref, theirs_ref, m_ref, v_ref, g_ref, d_ref, nm_ref, nv_ref):
        is_mine = (pl.program_id(0) // nhb) == c_ref[0]
        gv = jnp.where(is_mine, mine_ref[...], theirs_ref[...])
        d, nm, nv = _adam_update(w_ref[...], gv, m_ref[...], v_ref[...])
        g_ref[...] = gv
        d_ref[...] = d
        nm_ref[...] = nm
        nv_ref[...] = nv

    full = pl.BlockSpec((None, tr, c), lambda i, cc: (0, i, 0))
    half = pl.BlockSpec((tr, c), lambda i, cc: (i % nhb, 0))
    return pl.pallas_call(
        body, name=name,
        grid_spec=pltpu.PrefetchScalarGridSpec(
            num_scalar_prefetch=1, grid=(2 * nhb,),
            in_specs=[full, half, half, full, full], out_specs=[full] * 4),
        out_shape=[jax.ShapeDtypeStruct(w.shape, F32)] * 4,
        compiler_params=_params(("parallel",), 40 * 1024 * 1024),
    )(half_index, w, mine, theirs, m, v)


def _pair_sum(own, other, half_index, name, tr=256):
    _, r, c = own.shape
    rh = r // 2
    tr = min(tr, rh)
    nrb = rh // tr

    def body(c_ref, own_ref, oth_ref, out_ref):
        out_ref[...] = (own_ref[...] + oth_ref[...].astype(F32)).astype(BF16)

    return pl.pallas_call(
        body, name=name,
        grid_spec=pltpu.PrefetchScalarGridSpec(
            num_scalar_prefetch=1, grid=(N_CHIPS, nrb),
            in_specs=[pl.BlockSpec((None, tr, c), lambda k, i, cc: (k, cc[0] * nrb + i, 0)),
                      pl.BlockSpec((None, tr, c), lambda k, i, cc: (k, i, 0))],
            out_specs=pl.BlockSpec((None, tr, c), lambda k, i, cc: (k, i, 0))),
        out_shape=jax.ShapeDtypeStruct((N_CHIPS, rh, c), BF16),
        compiler_params=_params(("parallel", "parallel")),
    )(half_index, own, other)


def _chip_sum(sums, others, chip_index, name, tr=256):
    _, r, c = sums.shape
    tr = min(tr, r)

    def body(k_ref, own_ref, oth_ref, out_ref):
        acc = own_ref[...].astype(F32)
        for j in range(N_CHIPS - 1):
            acc = acc + oth_ref[j].astype(F32)
        out_ref[...] = acc

    return pl.pallas_call(
        body, name=name,
        grid_spec=pltpu.PrefetchScalarGridSpec(
            num_scalar_prefetch=1, grid=(r // tr,),
            in_specs=[pl.BlockSpec((None, tr, c), lambda i, kk: (kk[0], i, 0)),
                      pl.BlockSpec((N_CHIPS - 1, tr, c), lambda i, kk: (0, i, 0))],
            out_specs=pl.BlockSpec((tr, c), lambda i, kk: (i, 0))),
        out_shape=jax.ShapeDtypeStruct((r, c), F32),
        compiler_params=_params(("parallel",)),
    )(chip_index, sums, others)


HBM = pl.BlockSpec(memory_space=pltpu.HBM)


def _place():
    x, y, c = lax.axis_index("x"), lax.axis_index("y"), lax.axis_index("c")
    chips = [(1 - x, y), (x, 1 - y), (1 - x, 1 - y)]
    return x, y, c, chips


def _weight_allgather(shards, conv_shard):
    na = len(shards)

    def body(*refs):
        ins = refs[:na]
        conv_in = refs[na]
        outs = refs[na + 1:2 * na + 1]
        conv_out = refs[2 * na + 1]
        send, recv, fsend, frecv, csend, crecv = refs[2 * na + 2:]
        x, y, c, chips = _place()
        me = 2 * x + y
        sib = (x, y, 1 - c)
        first, conv_cp = [], []
        for i in range(na):
            rh = ins[i].shape[0] // 2
            mine = pl.ds(c * rh, rh)
            for j, (px, py) in enumerate(chips):
                cp = pltpu.make_async_remote_copy(
                    src_ref=ins[i].at[mine], dst_ref=outs[i].at[me, mine],
                    send_sem=send.at[3 * i + j], recv_sem=recv.at[3 * i + j],
                    device_id=(px, py, c), device_id_type=MESH)
                cp.start()
                first.append(cp)
        for j, (px, py) in enumerate(chips):
            cp = pltpu.make_async_remote_copy(
                src_ref=conv_in, dst_ref=conv_out.at[me], send_sem=csend.at[j], recv_sem=crecv.at[j],
                device_id=(px, py, c), device_id_type=MESH)
            cp.start()
            conv_cp.append(cp)
        passed = []
        for i in range(na):
            rh = ins[i].shape[0] // 2
            mine = pl.ds(c * rh, rh)
            for j, (px, py) in enumerate(chips):
                slot = outs[i].at[2 * px + py, mine]
                pltpu.make_async_remote_copy(
                    src_ref=slot, dst_ref=slot, send_sem=send.at[3 * i + j], recv_sem=recv.at[3 * i + j],
                    device_id=(px, py, c), device_id_type=MESH).wait_recv()
                cp = pltpu.make_async_remote_copy(
                    src_ref=slot, dst_ref=slot, send_sem=fsend.at[3 * i + j], recv_sem=frecv.at[3 * i + j],
                    device_id=sib, device_id_type=MESH)
                cp.start()
                passed.append(cp)
        for i in range(na):
            rh = ins[i].shape[0] // 2
            theirs = pl.ds((1 - c) * rh, rh)
            for j, (px, py) in enumerate(chips):
                slot = outs[i].at[2 * px + py, theirs]
                pltpu.make_async_remote_copy(
                    src_ref=slot, dst_ref=slot, send_sem=fsend.at[3 * i + j], recv_sem=frecv.at[3 * i + j],
                    device_id=sib, device_id_type=MESH).wait_recv()
        for j, (px, py) in enumerate(chips):
            slot = conv_out.at[2 * px + py]
            pltpu.make_async_remote_copy(
                src_ref=slot, dst_ref=slot, send_sem=csend.at[j], recv_sem=crecv.at[j],
                device_id=(px, py, c), device_id_type=MESH).wait_recv()
        for cp in first + passed + conv_cp:
            cp.wait_send()

    out_shape = [jax.ShapeDtypeStruct((N_CHIPS,) + s.shape, s.dtype) for s in shards]
    out_shape.append(jax.ShapeDtypeStruct((N_CHIPS,) + conv_shard.shape, conv_shard.dtype))
    res = pl.pallas_call(
        body, name="weight_allgather",
        in_specs=[HBM] * (na + 1), out_specs=[HBM] * (na + 1), out_shape=out_shape,
        scratch_shapes=[pltpu.SemaphoreType.DMA((3 * na,)), pltpu.SemaphoreType.DMA((3 * na,)),
                        pltpu.SemaphoreType.DMA((3 * na,)), pltpu.SemaphoreType.DMA((3 * na,)),
                        pltpu.SemaphoreType.DMA((3,)), pltpu.SemaphoreType.DMA((3,))],
    )(*shards, conv_shard)
    my_chip = 2 * lax.axis_index("x") + lax.axis_index("y")
    pick = lambda got, own: [jnp.where(my_chip == k, own, got[k]) for k in range(N_CHIPS)]
    return [pick(g, s) for g, s in zip(res[:na], shards)], pick(res[na], conv_shard)


def _sibling_swap_halves(grads, name):
    na = len(grads)

    def body(*refs):
        ins, outs = refs[:na], refs[na:2 * na]
        send, recv = refs[2 * na:]
        x, y, c, _ = _place()
        sib = (x, y, 1 - c)
        cps = []
        for i in range(na):
            rh = ins[i].shape[1] // 2
            cp = pltpu.make_async_remote_copy(
                src_ref=ins[i].at[:, pl.ds((1 - c) * rh, rh), :], dst_ref=outs[i],
                send_sem=send.at[i], recv_sem=recv.at[i], device_id=sib, device_id_type=MESH)
            cp.start()
            cps.append(cp)
        for cp in cps:
            cp.wait()

    out_shape = [jax.ShapeDtypeStruct((g.shape[0], g.shape[1] // 2, g.shape[2]), g.dtype) for g in grads]
    return pl.pallas_call(
        body, name=name, in_specs=[HBM] * na, out_specs=[HBM] * na, out_shape=out_shape,
        scratch_shapes=[pltpu.SemaphoreType.DMA((na,)), pltpu.SemaphoreType.DMA((na,))],
    )(*grads)


def _sibling_swap_whole(halves):
    na = len(halves)

    def body(*refs):
        ins, outs = refs[:na], refs[na:2 * na]
        send, recv = refs[2 * na:]
        x, y, c, _ = _place()
        cps = []
        for i in range(na):
            cp = pltpu.make_async_remote_copy(
                src_ref=ins[i], dst_ref=outs[i], send_sem=send.at[i], recv_sem=recv.at[i],
                device_id=(x, y, 1 - c), device_id_type=MESH)
            cp.start()
            cps.append(cp)
        for cp in cps:
            cp.wait()

    out_shape = [jax.ShapeDtypeStruct(h.shape, h.dtype) for h in halves]
    return pl.pallas_call(
        body, name="grad_sibling_join", in_specs=[HBM] * na, out_specs=[HBM] * na, out_shape=out_shape,
        scratch_shapes=[pltpu.SemaphoreType.DMA((na,)), pltpu.SemaphoreType.DMA((na,))],
    )(*halves)


SEM = pl.BlockSpec(memory_space=pltpu.SEMAPHORE)
ANY = pl.BlockSpec(memory_space=pl.ANY)
EFFECT = pltpu.SideEffectType.DATAFLOW_SIDE_EFFECTING


def _split_copy_start(name, plan, srcs, lands, after):
    ns, nl = len(srcs), len(lands)

    def body(*refs):
        src_refs, land_refs = refs[:ns], refs[ns:ns + nl]
        send, recv = refs[ns + nl + 1], refs[ns + nl + 2]
        token = refs[-1]
        outgoing, _ = plan(src_refs, land_refs)
        for src, dst, dev, si, ri in outgoing:
            pltpu.make_async_remote_copy(src_ref=src, dst_ref=dst, send_sem=send.at[si], recv_sem=recv.at[ri],
                                         device_id=dev, device_id_type=MESH).start()
        token[...] = jnp.zeros_like(token)

    n_out, n_in = plan.counts
    thru = [pltpu.HBM(a.shape, a.dtype) for a in list(srcs) + list(lands)]
    res = pl.pallas_call(
        body, name=name,
        out_shape=[pltpu.SemaphoreType.DMA((n_out,)), pltpu.SemaphoreType.DMA((n_in,))] + thru
        + [jax.ShapeDtypeStruct((8, 128), F32)],
        in_specs=[HBM] * (ns + nl) + [ANY],
        out_specs=[SEM, SEM] + [HBM] * (ns + nl) + [pl.BlockSpec(memory_space=pltpu.VMEM)],
        input_output_aliases={i: 2 + i for i in range(ns + nl)},
        compiler_params=pltpu.CompilerParams(has_side_effects=EFFECT),
    )(*[pltpu.with_memory_space_constraint(a, pltpu.HBM) for a in list(srcs) + list(lands)], after)
    return res[0], res[1], res[2:2 + ns], res[2 + ns:2 + ns + nl], res[-1]


def _split_copy_wait(name, plan, send, recv, srcs, lands, after):
    ns, nl = len(srcs), len(lands)

    def body(*refs):
        src_refs, land_refs = refs[:ns], refs[ns:ns + nl]
        send_ref, recv_ref = refs[ns + nl], refs[ns + nl + 1]
        outgoing, arrivals = plan(src_refs, land_refs)
        for src, dst, dev, si, ri in outgoing:
            pltpu.make_async_remote_copy(src_ref=src, dst_ref=dst, send_sem=send_ref.at[si], recv_sem=recv_ref.at[ri],
                                         device_id=dev, device_id_type=MESH).wait_send()
        for view, ri in arrivals:
            pltpu.make_async_remote_copy(src_ref=view, dst_ref=view, send_sem=send_ref.at[0], recv_sem=recv_ref.at[ri],
                                         device_id=_place()[:3], device_id_type=MESH).wait_recv()

    thru = [pltpu.HBM(a.shape, a.dtype) for a in list(srcs) + list(lands)]
    res = pl.pallas_call(
        body, name=name, out_shape=thru,
        in_specs=[HBM] * (ns + nl) + [SEM, SEM, ANY], out_specs=[HBM] * (ns + nl),
        input_output_aliases={i: i for i in range(ns + nl)},
        compiler_params=pltpu.CompilerParams(has_side_effects=EFFECT),
    )(*srcs, *lands, send, recv, after)
    return res[:ns], res[ns:]


def _gather_plan(n_arrays):
    def plan(src_refs, land_refs):
        x, y, c, chips = _place()
        me = 2 * x + y
        outgoing, arrivals = [], []
        for i in range(n_arrays):
            rh = src_refs[i].shape[0] // 2
            mine = pl.ds(c * rh, rh)
            for j, (px, py) in enumerate(chips):
                for delta in range(2):
                    tc = c ^ delta
                    outgoing.append((src_refs[i].at[mine], land_refs[i].at[me, mine], (px, py, tc),
                                     6 * i + 2 * j + delta, 6 * i + 2 * j + delta))
                    theirs = pl.ds(tc * rh, rh)
                    arrivals.append((land_refs[i].at[2 * px + py, theirs], 6 * i + 2 * j + delta))
        return outgoing, arrivals

    plan.counts = (6 * n_arrays, 6 * n_arrays)
    return plan


def _exchange_plan(n_arrays):
    def plan(src_refs, land_refs):
        x, y, c, chips = _place()
        outgoing, arrivals = [], []
        for i in range(n_arrays):
            for j, (px, py) in enumerate(chips):
                outgoing.append((src_refs[i].at[2 * px + py], land_refs[i].at[j], (px, py, c), 3 * i + j, 3 * i + j))
                arrivals.append((land_refs[i].at[j], 3 * i + j))
        return outgoing, arrivals

    plan.counts = (3 * n_arrays, 3 * n_arrays)
    return plan


def _small_allreduce(vec):
    r, cdim = vec.shape
    n_dev = 8

    def body(v_ref, out_ref, buf, send, recv):
        x, y, c, _ = _place()
        me = 4 * x + 2 * y + c
        buf[me] = v_ref[...]
        cps = []
        for k in range(1, n_dev):
            dx, dy, dc = (k >> 2) & 1, (k >> 1) & 1, k & 1
            peer = (x ^ dx, y ^ dy, c ^ dc)
            cp = pltpu.make_async_remote_copy(
                src_ref=v_ref, dst_ref=buf.at[me], send_sem=send.at[k - 1], recv_sem=recv.at[k - 1],
                device_id=peer, device_id_type=MESH)
            cp.start()
            cps.append(cp)
        for k in range(1, n_dev):
            dx, dy, dc = (k >> 2) & 1, (k >> 1) & 1, k & 1
            src = 4 * (x ^ dx) + 2 * (y ^ dy) + (c ^ dc)
            slot = buf.at[src]
            pltpu.make_async_remote_copy(
                src_ref=slot, dst_ref=slot, send_sem=send.at[k - 1], recv_sem=recv.at[k - 1],
                device_id=(x ^ dx, y ^ dy, c ^ dc), device_id_type=MESH).wait_recv()
        for cp in cps:
            cp.wait_send()
        acc = buf[0]
        for k in range(1, n_dev):
            acc = acc + buf[k]
        out_ref[...] = acc

    vm = pl.BlockSpec(memory_space=pltpu.VMEM)
    return pl.pallas_call(
        body, name="small_allreduce", in_specs=[vm], out_specs=vm,
        out_shape=jax.ShapeDtypeStruct((r, cdim), F32),
        scratch_shapes=[pltpu.VMEM((n_dev, r, cdim), F32), pltpu.SemaphoreType.DMA((n_dev - 1,)),
                        pltpu.SemaphoreType.DMA((n_dev - 1,))],
    )(vec)


def _a_cols_to_head_major(w):
    lead = w.shape[:-1]
    q = w[..., :A_QK].reshape(lead + (A_HEADS, A_DK))
    k = w[..., A_QK:2 * A_QK].reshape(lead + (A_HEADS, A_DK))
    v = w[..., 2 * A_QK:2 * A_QK + A_VW].reshape(lead + (A_HEADS, A_DV))
    z = w[..., 2 * A_QK + A_VW:].reshape(lead + (A_HEADS, A_DV))
    return jnp.concatenate([q, k, v, z], axis=-1).reshape(lead + (A_HEADS * A_HEAD_COLS,))


def _a_cols_from_head_major(w):
    lead = w.shape[:-1]
    w = w.reshape(lead + (A_HEADS, A_HEAD_COLS))
    parts = [w[..., :A_DK], w[..., A_DK:2 * A_DK], w[..., 2 * A_DK:2 * A_DK + A_DV], w[..., 2 * A_DK + A_DV:]]
    return jnp.concatenate([p.reshape(lead + (-1,)) for p in parts], axis=-1)


def _conv_cols_to_head_major(w):
    lead = w.shape[:-1]
    q = w[..., :A_QK].reshape(lead + (A_HEADS, A_DK))
    k = w[..., A_QK:2 * A_QK].reshape(lead + (A_HEADS, A_DK))
    v = w[..., 2 * A_QK:].reshape(lead + (A_HEADS, A_DV))
    return jnp.concatenate([q, k, v], axis=-1).reshape(lead + (A_HEADS * A_CONV_COLS,))


def _conv_cols_from_head_major(w):
    lead = w.shape[:-1]
    w = w.reshape(lead + (A_HEADS, A_CONV_COLS))
    parts = [w[..., :A_DK], w[..., A_DK:2 * A_DK], w[..., 2 * A_DK:]]
    return jnp.concatenate([p.reshape(lead + (-1,)) for p in parts], axis=-1)


def _to_stream(a, bn, d):
    rest = a.shape[1:]
    s = a.shape[0] // bn
    a = a.reshape((bn, s // d, d) + rest)
    a = jnp.swapaxes(a, 1, 2)
    return a.reshape((bn * d, s // d) + rest)


def _from_stream(a, bn, d):
    rest = a.shape[2:]
    ln = a.shape[1]
    a = a.reshape((bn, d, ln) + rest)
    a = jnp.swapaxes(a, 1, 2)
    return a.reshape((bn * ln * d,) + rest)


B_SUB = 512
B_SHARD_BLOCKS = (3 * B_GROUPS * B_W + B_W) // N_CHIPS // B_SUB


def _b_block(gi, jj):
    nb = (B_GROUPS * (jj // 2) + gi) * 2 + jj % 2
    return nb // B_SHARD_BLOCKS, nb % B_SHARD_BLOCKS


def _shard_major(g, ncols):
    r = g.shape[0]
    return jnp.swapaxes(g.reshape(r, N_CHIPS, ncols), 0, 1)


def _pack_rows(items):
    rows, offs = [], []
    at = 0
    for a in items:
        flat = a.reshape(-1).astype(F32)
        nr = -(-flat.shape[0] // 1024) * 8
        flat = jnp.pad(flat, (0, nr * 128 - flat.shape[0]))
        rows.append(flat.reshape(nr, 128))
        offs.append((at, nr, a.shape))
        at += nr
    return jnp.concatenate(rows, axis=0), offs


def _unpack_rows(packed, offs):
    out = []
    for at, nr, shape in offs:
        size = int(np.prod(shape)) if len(shape) else 1
        out.append(packed[at:at + nr].reshape(-1)[:size].reshape(shape))
    return out


def _local_step(x, positions, loss_target, norm_g, wa_in, conv_w, a_log, a_dt_bias, a_norm_g,
                b_q_norm_g, b_k_norm_g, start_token, late_weights, b_grads_ready, a_grads_ready):
    bn, s, d = x.shape
    t = bn * s
    n_chunks = s // A_CHUNK
    wa_main = _a_cols_to_head_major(wa_in[:, :A_MAIN])
    wa_tail = jnp.pad(wa_in[:, A_MAIN:], ((0, 0), (0, 128 - 2 * A_HEADS)))
    cw_hm = _conv_cols_to_head_major(conv_w)

    x0 = x.reshape(t, d)
    h0 = _rms_fwd(x0, norm_g[0:1] + start_token, "rms0_fwd")
    proj_a = _matmul(h0, wa_main, "nn", F32, "a_in_main")
    tail_a = _matmul(h0, wa_tail, "nn", F32, "a_in_tail")
    tail_t = jnp.swapaxes(tail_a[:, :2 * A_HEADS].reshape(bn, s, 2 * A_HEADS), 1, 2)
    tail_t = tail_t.reshape(bn, 2 * A_HEADS, n_chunks, A_CHUNK)
    beta, gc = _gdn_prep(tail_t, a_log[0], a_dt_bias[0])
    proj_a3 = proj_a.reshape(bn, s, A_MAIN)
    og_a, oraw_a, states, t_mats, conv_y = _gdn_fwd(proj_a3, cw_hm, beta, gc, a_norm_g)
    wa_out, wb_in, wb_out = late_weights(og_a)
    b_cols = [4 * B_W] + [3 * B_W] * (B_GROUPS - 1)
    x1 = _matmul(og_a.reshape(t, A_VW), wa_out, "nn", F32, "a_out", res=x0, tk=2048)

    h1 = _rms_fwd(x1, norm_g[1:2], "rms1_fwd")
    inv_freq = ROPE_THETA ** (-jnp.arange(0, ROPE_DIMS, 2, dtype=F32) / ROPE_DIMS)
    freq_row = jnp.concatenate([inv_freq, inv_freq, jnp.zeros((128 - ROPE_DIMS,), F32)]).reshape(1, 128)
    posf = jnp.broadcast_to(positions.astype(F32).reshape(t, 1), (t, 128))
    tabs = _rope_tables(posf, freq_row)
    h1_s, tabs_s, proj_b, qkv_b, o_b, lse_b = [], [], [], [], [], []
    for gi, dil in enumerate(B_DIL):
        hs = h1 if dil == 1 else _to_stream(h1, bn, dil).reshape(t, d)
        ts = tabs if dil == 1 else [_to_stream(tb, bn, dil).reshape(t, 128) for tb in tabs]
        pj = _matmul(hs, wb_in, "nn", F32, f"b_in_g{gi}", tm=2048, tn=B_SUB, n=b_cols[gi], b_spec=pl.BlockSpec(
            (None, d, B_SUB), lambda i, j, kk, gi=gi: (_b_block(gi, j)[0], kk, _b_block(gi, j)[1])))
        qkv = _qk_prep(pj, *ts, b_q_norm_g[0, gi:gi + 1], b_k_norm_g[0, gi:gi + 1], f"qk_prep_g{gi}")
        o_s, lse_s = _attn_fwd(qkv.reshape(bn * dil, s // dil, 3 * B_W), f"attn_fwd_g{gi}")
        h1_s.append(hs), tabs_s.append(ts), proj_b.append(pj), qkv_b.append(qkv)
        o_b.append(o_s.reshape(t, B_W) if dil == 1 else _from_stream(o_s, bn, dil))
        lse_b.append(lse_s.reshape(t, B_HEADS) if dil == 1 else _from_stream(lse_s, bn, dil))
    og_b = _merge_fwd(o_b, lse_b, proj_b[0])
    x2 = _matmul(og_b, wb_out, "nn", F32, "b_out", res=x1)

    d_x2, loss_parts = _loss_grad(x2, loss_target.reshape(t, d))
    loss_local = jnp.sum(loss_parts)

    d_x2b = d_x2.astype(BF16)
    g_wb_out = _matmul(og_b.T, d_x2b, "nn", F32, "b_out_dw")
    d_og_b = _matmul(d_x2b, wb_out, "nt", F32, "b_out_dx")
    d_o, lse_joint, delta, d_z = _merge_bwd(o_b, lse_b, proj_b[0], d_og_b)
    d_h1, g_qn, g_kn = [], [], []
    g_wb_in = lax.empty(wb_in.shape, F32)
    for gi, dil in enumerate(B_DIL):
        if dil == 1:
            do_s, lj_s, dl_s = d_o, lse_joint, delta
        else:
            do_s, lj_s, dl_s = (_to_stream(a, bn, dil).reshape(t, -1) for a in (d_o, lse_joint, delta))
        ns, ln = bn * dil, s // dil
        dq, dk, dv = _attn_bwd(qkv_b[gi].reshape(ns, ln, 3 * B_W), do_s.reshape(ns, ln, B_W),
                               lj_s.reshape(ns, ln, B_HEADS), dl_s.reshape(ns, ln, B_HEADS), f"attn_bwd_g{gi}")
        d_pj, d_gain = _qk_prep_bwd(proj_b[gi], *tabs_s[gi], b_q_norm_g[0, gi:gi + 1], b_k_norm_g[0, gi:gi + 1],
                                    dq.reshape(t, B_W), dk.reshape(t, B_W), dv.reshape(t, B_W),
                                    d_z if gi == 0 else None, f"qk_prep_bwd_g{gi}")
        g_wb_in = _matmul(h1_s[gi].T, d_pj, "nn", F32, f"b_in_dw_g{gi}", tn=B_SUB, tk=2048, into=(g_wb_in, pl.BlockSpec(
            (None, d, B_SUB), lambda i, j, kk, gi=gi: (_b_block(gi, j)[0], i, _b_block(gi, j)[1]))))
        dh = _matmul(d_pj, wb_in, "nt", F32, f"b_in_dx_g{gi}", tm=2048, tk=B_SUB, n=d, b_spec=pl.BlockSpec(
            (None, d, B_SUB), lambda i, j, kk, gi=gi: (_b_block(gi, kk)[0], j, _b_block(gi, kk)[1])))
        d_h1.append(dh if dil == 1 else _from_stream(dh.reshape(ns, ln, d), bn, dil))
        g_qn.append(d_gain[0]), g_kn.append(d_gain[1])
    d_x1, g_norm1 = _rms_bwd(x1, norm_g[1:2], d_h1, d_x2, "rms1_bwd")

    d_x1b = (d_x1 + b_grads_ready(g_wb_in, g_wb_out)).astype(BF16)
    g_wa_out = _matmul(og_a.reshape(t, A_VW).T, d_x1b, "nn", F32, "a_out_dw")
    d_og_a = _matmul(d_x1b, wa_out, "nt", F32, "a_out_dx")
    d_pa, d_gc, d_beta, d_cw, d_ng = _gdn_bwd(proj_a3, cw_hm, beta, gc, a_norm_g, oraw_a, states, t_mats,
                                              conv_y, d_og_a.reshape(bn, s, A_VW))
    d_tail_t, d_alog, d_dtb = _gdn_prep_bwd(tail_t, a_log[0], a_dt_bias[0], d_gc, d_beta)
    d_tail = jnp.swapaxes(d_tail_t.reshape(bn, 2 * A_HEADS, s), 1, 2).reshape(t, 2 * A_HEADS)
    d_tail = jnp.pad(d_tail, ((0, 0), (0, 128 - 2 * A_HEADS))).astype(BF16)
    d_pa = d_pa.reshape(t, A_MAIN)
    h0_t = h0.T
    g_wa_main = _matmul(h0_t, d_pa, "nn", F32, "a_in_dw_main")
    g_wa_tail = _matmul(h0_t, d_tail, "nn", F32, "a_in_dw_tail")
    g_wa_in = jnp.concatenate([_a_cols_from_head_major(g_wa_main), g_wa_tail[:, :2 * A_HEADS]], axis=1)
    a_token = a_grads_ready(g_wa_in, g_wa_out)
    d_h0 = _matmul(d_pa, wa_main, "nt", F32, "a_in_dx_main")
    d_h0t = _matmul(d_tail + a_token.astype(BF16), wa_tail, "nt", F32, "a_in_dx_tail")
    d_x0, g_norm0 = _rms_bwd(x0, norm_g[0:1], [d_h0, d_h0t], d_x1, "rms0_bwd")

    gfull = {
        "norm_g": jnp.concatenate([g_norm0, g_norm1], axis=0), "a_w_in": g_wa_in,
        "a_conv_w": _conv_cols_from_head_major(jnp.sum(d_cw, axis=0)),
        "a_log": jnp.sum(d_alog[:, :, 0], axis=0), "a_dt_bias": jnp.sum(d_dtb[:, :, 0], axis=0),
        "a_norm_g": jnp.sum(d_ng[:, :, 0, :], axis=(0, 1)), "a_w_out": g_wa_out, "b_w_in": g_wb_in,
        "b_q_norm_g": jnp.stack(g_qn), "b_k_norm_g": jnp.stack(g_kn), "b_w_out": g_wb_out}
    return loss_local, d_x0.reshape(bn, s, d), gfull


def kernel(x, positions, norm_g, a_w_in, a_conv_w, a_log, a_dt_bias, a_norm_g, a_w_out, b_w_in, b_q_norm_g, b_k_norm_g, b_w_out, loss_target, m_norm_g, m_a_w_in, m_a_conv_w, m_a_log, m_a_dt_bias, m_a_norm_g, m_a_w_out, m_b_w_in, m_b_q_norm_g, m_b_k_norm_g, m_b_w_out, v_norm_g, v_a_w_in, v_a_conv_w, v_a_log, v_a_dt_bias, v_a_norm_g, v_a_w_out, v_b_w_in, v_b_q_norm_g, v_b_k_norm_g, v_b_w_out):
    d = x.shape[2]
    my_c = lax.axis_index("c")
    my_chip = 2 * lax.axis_index("x") + lax.axis_index("y")

    half_index = jnp.reshape(my_c, (1,)).astype(jnp.int32)
    chip_index = jnp.reshape(my_chip, (1,)).astype(jnp.int32)
    (ga_in,), g_conv = _weight_allgather([a_w_in[0].astype(BF16)], a_conv_w[0])
    wa_in = jnp.concatenate(ga_in, axis=1)
    conv_w = jnp.concatenate(g_conv, axis=1)

    late_shards = [a_w_out[0].astype(BF16), b_w_in[0].astype(BF16), b_w_out[0].astype(BF16)]
    late_lands = [lax.dynamic_update_slice(lax.empty((N_CHIPS,) + s.shape, BF16), s[None], (my_chip, 0, 0))
                  for s in late_shards]
    gather = _gather_plan(len(late_shards))
    ag_send, ag_recv, ag_srcs, ag_lands, ag_token = _split_copy_start(
        "late_weights_start", gather, late_shards, late_lands, conv_w)

    def late_weights(after):
        _, (ga_out, gb_in, gb_out) = _split_copy_wait(
            "late_weights_wait", gather, ag_send, ag_recv, ag_srcs, ag_lands, after)
        return ga_out.reshape(A_VW, d), gb_in, gb_out.reshape(B_W, d)

    def reduce_to_chip_sums(mats, tag):
        recv_sib = _sibling_swap_halves([g.astype(BF16) for g in mats], f"grad_{tag}_sibling_swap")
        return [_pair_sum(g, r, half_index, f"grad_{tag}_pair_sum_{i}") for i, (g, r) in enumerate(zip(mats, recv_sib))]

    exchange = _exchange_plan(2)
    pending = {}

    def start_exchange(tag, mats):
        sums = reduce_to_chip_sums(mats, tag)
        lands = [lax.empty((N_CHIPS - 1,) + s.shape[1:], BF16) for s in sums]
        pending[tag] = _split_copy_start(f"grad_{tag}_exchange_start", exchange, sums, lands, chip_index)
        return pending[tag][4][0, 0]

    def finish_exchange(tag, after):
        send, recv, srcs, lands, _ = pending[tag]
        return _split_copy_wait(f"grad_{tag}_exchange_wait", exchange, send, recv, srcs, lands, after)

    def b_grads_ready(g_wb_in, g_wb_out):
        return start_exchange("b", [g_wb_in, g_wb_out.reshape(N_CHIPS, -1, d)])

    def a_grads_ready(g_wa_in, g_wa_out):
        return start_exchange("a", [_shard_major(g_wa_in, a_w_in.shape[2]), g_wa_out.reshape(N_CHIPS, -1, d)])

    loss_local, d_x0, gfull = _local_step(x, positions, loss_target, norm_g, wa_in, conv_w, a_log, a_dt_bias,
                                          a_norm_g, b_q_norm_g, b_k_norm_g, ag_token[0, 0], late_weights,
                                          b_grads_ready, a_grads_ready)

    small = [gfull["norm_g"], gfull["a_conv_w"], gfull["a_log"], gfull["a_dt_bias"], gfull["a_norm_g"],
             gfull["b_q_norm_g"], gfull["b_k_norm_g"], loss_local]
    packed, offs = _pack_rows(small)
    reduced = _small_allreduce(packed)
    g_norm, g_conv_all, g_alog, g_dtb, g_ang, g_q, g_k, loss = _unpack_rows(reduced, offs)
    g_conv_mine = lax.dynamic_slice_in_dim(g_conv_all, my_chip * a_conv_w.shape[2], a_conv_w.shape[2], axis=1)

    b_sums, b_received = finish_exchange("b", d_x0)
    a_sums, a_received = finish_exchange("a", reduced)
    chip_sums = [a_sums[0], a_sums[1], b_sums[0], b_sums[1]]
    received = [a_received[0], a_received[1], b_received[0], b_received[1]]
    halves = [_chip_sum(s, r, chip_index, f"grad_chip_sum_{i}") for i, (s, r) in enumerate(zip(chip_sums, received))]
    theirs = _sibling_swap_whole(halves)
    big = ("a_w_in", "a_w_out", "b_w_in", "b_w_out")
    big_halves = dict(zip(big, zip(halves, theirs)))

    grads = {
        "norm_g": g_norm, "a_conv_w": g_conv_mine[None], "a_log": g_alog[None], "a_dt_bias": g_dtb[None],
        "a_norm_g": g_ang[None], "b_q_norm_g": g_q[None], "b_k_norm_g": g_k[None]}
    weights = {"norm_g": norm_g, "a_w_in": a_w_in, "a_conv_w": a_conv_w, "a_log": a_log, "a_dt_bias": a_dt_bias,
               "a_norm_g": a_norm_g, "a_w_out": a_w_out, "b_w_in": b_w_in, "b_q_norm_g": b_q_norm_g,
               "b_k_norm_g": b_k_norm_g, "b_w_out": b_w_out}
    m_in = {"norm_g": m_norm_g, "a_w_in": m_a_w_in, "a_conv_w": m_a_conv_w, "a_log": m_a_log,
            "a_dt_bias": m_a_dt_bias, "a_norm_g": m_a_norm_g, "a_w_out": m_a_w_out, "b_w_in": m_b_w_in,
            "b_q_norm_g": m_b_q_norm_g, "b_k_norm_g": m_b_k_norm_g, "b_w_out": m_b_w_out}
    v_in = {"norm_g": v_norm_g, "a_w_in": v_a_w_in, "a_conv_w": v_a_conv_w, "a_log": v_a_log,
            "a_dt_bias": v_a_dt_bias, "a_norm_g": v_a_norm_g, "a_w_out": v_a_w_out, "b_w_in": v_b_w_in,
            "b_q_norm_g": v_b_q_norm_g, "b_k_norm_g": v_b_k_norm_g, "b_w_out": v_b_w_out}
    names = list(weights)

    delta_w, new_m, new_v = {}, {}, {}
    for nm in big:
        mine, other = big_halves[nm]
        grads[nm], delta_w[nm], new_m[nm], new_v[nm] = _adamw_shard(
            weights[nm], mine, other, m_in[nm], v_in[nm], half_index, f"adamw_{nm}")
    small_names = [nm for nm in names if nm not in big]
    packs = [_pack_rows([src[nm] for nm in small_names]) for src in (weights, grads, m_in, v_in)]
    offs = packs[0][1]
    dl, m2, v2 = _adamw(packs[0][0], packs[1][0], packs[2][0], packs[3][0], "adamw_small")
    for nm, a, b, c2 in zip(small_names, _unpack_rows(dl, offs), _unpack_rows(m2, offs), _unpack_rows(v2, offs)):
        delta_w[nm], new_m[nm], new_v[nm] = a, b, c2

    return (loss, d_x0, *[grads[nm] for nm in names], *[delta_w[nm] for nm in names],
            *[new_m[nm] for nm in names], *[new_v[nm] for nm in names])
```

```python
import jax
import jax.numpy as jnp
import numpy as np
from jax import lax
from jax.experimental import pallas as pl
from jax.experimental.pallas import tpu as pltpu

F32 = jnp.float32
BF16 = jnp.bfloat16
MESH = pl.DeviceIdType.MESH

EPS = 1e-6
D_MODEL = 1024
A_HEADS = 8
A_DK = 128
A_DV = 256
A_QK = A_HEADS * A_DK
A_VW = A_HEADS * A_DV
A_MAIN = 2 * A_QK + 2 * A_VW
A_HEAD_COLS = 2 * A_DK + 2 * A_DV
A_CONV_COLS = 2 * A_DK + A_DV
A_CHUNK = 64
A_CONV = 4
B_GROUPS = 3
B_HEADS = 8
B_DH = 128
B_W = B_HEADS * B_DH
B_DIL = (1, 4, 16)
B_BLK = 128
ROPE_THETA = 500000.0
ROPE_DIMS = B_DH // 4
ADAM_LR, ADAM_B1, ADAM_B2, ADAM_EPS, ADAM_WD, ADAM_STEP = 0.001, 0.9, 0.999, 1e-08, 0.01, 10
N_CHIPS = 4
VMEM_BIG = 56 * 1024 * 1024


def _params(sem=None, vmem=None):
    return pltpu.CompilerParams(dimension_semantics=sem, vmem_limit_bytes=vmem)


def _dot(a, b, ca, cb):
    return lax.dot_general(a.astype(BF16), b.astype(BF16), (((ca,), (cb,)), ((), ())),
                           preferred_element_type=F32)


def _split3(a):
    hi = a.astype(BF16)
    r = a - hi.astype(F32)
    mid = r.astype(BF16)
    lo = (r - mid.astype(F32)).astype(BF16)
    return hi, mid, lo


def _sigmoid(y):
    return 1.0 / (1.0 + jnp.exp(-y))


def _silu(y):
    return y * _sigmoid(y)


def _dsilu(y):
    s = _sigmoid(y)
    return s * (1.0 + y * (1.0 - s))


def _matmul(a, b, mode, out_dtype, name, res=None, tm=1024, tn=1024, tk=1024, n=None, b_spec=None, into=None):
    m, k = a.shape
    if n is None:
        n = b.shape[1] if mode == "nn" else b.shape[0]
    tm, tn, tk = min(tm, m), min(tn, n), min(tk, k)
    assert m % tm == 0 and n % tn == 0 and k % tk == 0, (name, a.shape, b.shape)
    nk = k // tk
    dims = {"nn": ((1,), (0,)), "nt": ((1,), (1,))}[mode]

    def body(*refs):
        a_ref, b_ref = refs[0], refs[1]
        r_ref = refs[2] if res is not None else None
        o_ref = refs[2 + (res is not None) + (into is not None)]
        prod = lax.dot_general(a_ref[...], b_ref[...], (dims, ((), ())), preferred_element_type=F32)

        def finish(r):
            if res is not None:
                r = r + r_ref[...]
            o_ref[...] = r.astype(out_dtype)

        if nk == 1:
            finish(prod)
            return
        acc = refs[-1]
        kk = pl.program_id(2)

        @pl.when(kk == 0)
        def _():
            acc[...] = prod

        @pl.when((kk > 0) & (kk < nk - 1))
        def _():
            acc[...] += prod

        @pl.when(kk == nk - 1)
        def _():
            finish(acc[...] + prod)

    a_spec = pl.BlockSpec((tm, tk), lambda i, j, kk: (i, kk))
    if b_spec is None and mode == "nt":
        b_spec = pl.BlockSpec((tn, tk), lambda i, j, kk: (j, kk))
    elif b_spec is None:
        b_spec = pl.BlockSpec((tk, tn), lambda i, j, kk: (kk, j))
    in_specs = [a_spec, b_spec]
    args = [a, b]
    if res is not None:
        in_specs.append(pl.BlockSpec((tm, tn), lambda i, j, kk: (i, j)))
        args.append(res)
    out_spec = pl.BlockSpec((tm, tn), lambda i, j, kk: (i, j))
    out_shape = jax.ShapeDtypeStruct((m, n), out_dtype)
    aliases = {}
    if into is not None:
        assert res is None
        buf, out_spec = into
        out_shape = jax.ShapeDtypeStruct(buf.shape, buf.dtype)
        in_specs.append(ANY)
        args.append(buf)
        aliases = {2: 0}
    return pl.pallas_call(
        body, name=name, grid=(m // tm, n // tn, nk),
        in_specs=in_specs, out_specs=out_spec, out_shape=out_shape, input_output_aliases=aliases,
        scratch_shapes=[pltpu.VMEM((tm, tn), F32)] if nk > 1 else [],
        compiler_params=_params(("parallel", "parallel", "arbitrary"), 48 * 1024 * 1024),
    )(*args)


def _rms_fwd(x, g, name, tm=256):
    t, d = x.shape

    def body(x_ref, g_ref, h_ref):
        xv = x_ref[...]
        r = lax.rsqrt(jnp.mean(xv * xv, axis=-1, keepdims=True) + EPS)
        h_ref[...] = (xv * r * g_ref[...]).astype(BF16)

    return pl.pallas_call(
        body, name=name, grid=(t // tm,),
        in_specs=[pl.BlockSpec((tm, d), lambda i: (i, 0)), pl.BlockSpec((1, d), lambda i: (0, 0))],
        out_specs=pl.BlockSpec((tm, d), lambda i: (i, 0)),
        out_shape=jax.ShapeDtypeStruct((t, d), BF16),
        compiler_params=_params(("parallel",)),
    )(x, g)


def _rms_bwd(x, g, dhs, dres, name, tm=256):
    t, d = x.shape
    n_dh = len(dhs)

    def body(*refs):
        x_ref, g_ref = refs[0], refs[1]
        dh_refs = refs[2:2 + n_dh]
        dres_ref, dx_ref, dg_ref = refs[2 + n_dh:]
        i = pl.program_id(0)

        @pl.when(i == 0)
        def _():
            dg_ref[...] = jnp.zeros_like(dg_ref)

        xv = x_ref[...]
        r = lax.rsqrt(jnp.mean(xv * xv, axis=-1, keepdims=True) + EPS)
        xh = xv * r
        dh = dh_refs[0][...]
        for ref in dh_refs[1:]:
            dh = dh + ref[...]
        dg_ref[0:1, :] += jnp.sum(dh * xh, axis=0, keepdims=True)
        dxh = dh * g_ref[...]
        dx = r * (dxh - xh * jnp.mean(dxh * xh, axis=-1, keepdims=True))
        dx_ref[...] = dx + dres_ref[...]

    row = pl.BlockSpec((tm, d), lambda i: (i, 0))
    dx, dg = pl.pallas_call(
        body, name=name, grid=(t // tm,),
        in_specs=[row, pl.BlockSpec((1, d), lambda i: (0, 0))] + [row] * n_dh + [row],
        out_specs=[row, pl.BlockSpec((8, d), lambda i: (0, 0))],
        out_shape=[jax.ShapeDtypeStruct((t, d), F32), jax.ShapeDtypeStruct((8, d), F32)],
        compiler_params=_params(("arbitrary",)),
    )(x, g, *dhs, dres)
    return dx, dg[0:1]


def _loss_grad(y, target, name="loss_grad", tm=256):
    t, d = y.shape
    nb = t // tm

    def body(y_ref, t_ref, dy_ref, part_ref):
        e = y_ref[...] - t_ref[...]
        dy_ref[...] = e * (1.0 / d)
        s = jnp.sum(jnp.sum(e * e, axis=1, keepdims=True), axis=0, keepdims=True) * (0.5 / d)
        part_ref[...] = jnp.broadcast_to(s, (8, 128))

    row = pl.BlockSpec((tm, d), lambda i: (i, 0))
    dy, part = pl.pallas_call(
        body, name=name, grid=(nb,), in_specs=[row, row],
        out_specs=[row, pl.BlockSpec((None, 8, 128), lambda i: (i, 0, 0))],
        out_shape=[jax.ShapeDtypeStruct((t, d), F32), jax.ShapeDtypeStruct((nb, 8, 128), F32)],
        compiler_params=_params(("parallel",)),
    )(y, target)
    return dy, part[:, 0, 0]


def _softplus(x):
    t = jnp.exp(-jnp.abs(x))
    return jnp.maximum(x, 0.0) + jnp.where(t < 1e-3, t * (1.0 - 0.5 * t), jnp.log(1.0 + t))


def _tri(rows_le_cols):
    r = lax.broadcasted_iota(jnp.int32, (A_CHUNK, A_CHUNK), 0)
    c = lax.broadcasted_iota(jnp.int32, (A_CHUNK, A_CHUNK), 1)
    return jnp.where((r <= c) if rows_le_cols else (r >= c), 1.0, 0.0).astype(BF16)


def _dot_exact_rhs(a, ones_bf16):
    dn = (((1,), (0,)), ((), ()))
    hi, mid, lo = _split3(a)
    out = lax.dot_general(hi, ones_bf16, dn, preferred_element_type=F32)
    out = out + lax.dot_general(mid, ones_bf16, dn, preferred_element_type=F32)
    return out + lax.dot_general(lo, ones_bf16, dn, preferred_element_type=F32)


def _gdn_prep(tail_t, a_log, dt_bias):
    bn, _, n, c = tail_t.shape

    def body(t_ref, alog_ref, dtb_ref, beta_ref, gc_ref):
        upper = _tri(True)
        for h in range(A_HEADS):
            beta_ref[h] = _sigmoid(t_ref[h])
            ea = jnp.exp(jnp.full((n, c), alog_ref[h], F32))
            g = -ea * _softplus(t_ref[A_HEADS + h] + dtb_ref[h])
            gc_ref[h] = _dot_exact_rhs(g, upper)

    smem = pl.BlockSpec(memory_space=pltpu.SMEM)
    blk = pl.BlockSpec((None, A_HEADS, n, c), lambda b: (b, 0, 0, 0))
    return pl.pallas_call(
        body, name="gdn_prep", grid=(bn,),
        in_specs=[pl.BlockSpec((None, 2 * A_HEADS, n, c), lambda b: (b, 0, 0, 0)), smem, smem],
        out_specs=[blk, blk],
        out_shape=[jax.ShapeDtypeStruct((bn, A_HEADS, n, c), F32)] * 2,
        compiler_params=_params(("parallel",)),
    )(tail_t, a_log, dt_bias)


def _gdn_prep_bwd(tail_t, a_log, dt_bias, d_gc, d_beta):
    bn, _, n, c = tail_t.shape

    def body(t_ref, alog_ref, dtb_ref, dgc_ref, dbeta_ref, dt_ref, dal_ref, ddt_ref):
        lower = _tri(False)
        for h in range(A_HEADS):
            beta = _sigmoid(t_ref[h])
            dt_ref[h] = dbeta_ref[h] * beta * (1.0 - beta)
            dg = _dot_exact_rhs(dgc_ref[h], lower)
            ea = jnp.exp(jnp.full((n, c), alog_ref[h], F32))
            xa = t_ref[A_HEADS + h] + dtb_ref[h]
            g = -ea * _softplus(xa)
            dxa = -ea * dg * _sigmoid(xa)
            dt_ref[A_HEADS + h] = dxa
            s1 = jnp.sum(jnp.sum(g * dg, axis=1, keepdims=True), axis=0, keepdims=True)
            s2 = jnp.sum(jnp.sum(dxa, axis=1, keepdims=True), axis=0, keepdims=True)
            dal_ref[h:h + 1, :] = jnp.broadcast_to(s1, (1, 128))
            ddt_ref[h:h + 1, :] = jnp.broadcast_to(s2, (1, 128))

    smem = pl.BlockSpec(memory_space=pltpu.SMEM)
    blk8 = pl.BlockSpec((None, A_HEADS, n, c), lambda b: (b, 0, 0, 0))
    blk16 = pl.BlockSpec((None, 2 * A_HEADS, n, c), lambda b: (b, 0, 0, 0))
    sm = pl.BlockSpec((None, A_HEADS, 128), lambda b: (b, 0, 0))
    return pl.pallas_call(
        body, name="gdn_prep_bwd", grid=(bn,),
        in_specs=[blk16, smem, smem, blk8, blk8],
        out_specs=[blk16, sm, sm],
        out_shape=[jax.ShapeDtypeStruct((bn, 2 * A_HEADS, n, c), F32),
                   jax.ShapeDtypeStruct((bn, A_HEADS, 128), F32),
                   jax.ShapeDtypeStruct((bn, A_HEADS, 128), F32)],
        compiler_params=_params(("parallel",)),
    )(tail_t, a_log, dt_bias, d_gc, d_beta)


HALO = 8


def _conv_taps(xw, w):
    y = w[A_CONV - 1:A_CONV, :] * xw
    for j in range(1, A_CONV):
        y = y + w[A_CONV - 1 - j:A_CONV - j, :] * pltpu.roll(xw, j, 0)
    return y[HALO:, :]


def _row_to_col(row, eye):
    c = eye.shape[0]
    return jnp.sum(jnp.where(eye, jnp.broadcast_to(row, (c, c)), 0.0), axis=1, keepdims=True)


def _col_to_row(col, eye):
    c = eye.shape[0]
    return jnp.sum(jnp.where(eye, jnp.broadcast_to(col, (c, c)), 0.0), axis=0, keepdims=True)


def _unit_lower_inverse(a, ri, ci):
    eye = jnp.where(ri == ci, 1.0, 0.0)
    a8 = jnp.where((ri >> 3) == (ci >> 3), a, 0.0)
    a2 = _dot(a8, a8, 1, 0)
    yield
    a4 = _dot(a2, a2, 1, 0)
    t = eye - a8
    t = t + _dot(t, a2, 1, 0)
    yield
    t = t + _dot(t, a4, 1, 0)
    yield
    for sh in (3, 4, 5):
        off = jnp.where(((ri >> (sh + 1)) == (ci >> (sh + 1))) & ((ri >> sh) != (ci >> sh)), a, 0.0)
        left = _dot(t, off, 1, 0)
        yield
        t = t - _dot(left, t, 1, 0)
        yield
    return t


def _round_robin(gens):
    live = list(gens)
    while live:
        nxt = []
        for g in live:
            try:
                next(g)
                nxt.append(g)
            except StopIteration:
                pass
        live = nxt


def _gdn_chunk_core(q, k, v, g_row, b_row, t_mat, ri, ci):
    eye = ri == ci
    g_col = _row_to_col(g_row, eye)
    b_col = _row_to_col(b_row, eye)
    causal = ri >= ci
    strict = ri > ci
    dec = jnp.where(causal, jnp.exp(jnp.where(causal, g_col - g_row, 0.0)), 0.0)
    gam = jnp.exp(g_col)
    g_last = g_row[:, A_CHUNK - 1:A_CHUNK]
    gam_last = jnp.exp(g_last)
    e = jnp.exp(g_last - g_col)
    kb = k * b_col
    bv = v * b_col
    kbg = kb * gam
    q16, k16, kb16 = q.astype(BF16), k.astype(BF16), kb.astype(BF16)
    kk = _dot(kb16, k16, 1, 1)
    p = _dot(q16, k16, 1, 1) * dec
    yield
    a_mat = jnp.where(strict, kk * dec, 0.0)
    if t_mat is None:
        t_mat = yield from _unit_lower_inverse(a_mat, ri, ci)
    t16 = t_mat.astype(BF16)
    u = _dot(t16, bv, 1, 0)
    w = _dot(t16, kbg, 1, 0)
    yield
    return dict(eye=eye, g_col=g_col, b_col=b_col, dec=dec, strict=strict, causal=causal, gam=gam,
                gam_last=gam_last, e=e, kb=kb, bv=bv, kbg=kbg, a_mat=a_mat, t_mat=t_mat, u=u, w=w, p=p,
                qg=q * gam, kd=k * e, q16=q16, k16=k16, kb16=kb16, t16=t16)


A_SEQ_BLK = 256
A_BLK_CHUNKS = A_SEQ_BLK // A_CHUNK


def _gdn_halo(proj_hm):
    bn, s, w = proj_hm.shape
    last = proj_hm.reshape(bn, s // A_SEQ_BLK, A_SEQ_BLK, w)[:, :, A_SEQ_BLK - HALO:, :]
    return jnp.concatenate([jnp.zeros((bn, 1, HALO, w), proj_hm.dtype), last[:, :-1]], axis=1)


def _gdn_window(x_ref, halo_ref, ci, first, lo):
    if first:
        return jnp.concatenate([halo_ref[:, lo:lo + A_CONV_COLS], x_ref[0:A_CHUNK, lo:lo + A_CONV_COLS]], axis=0)
    start = pl.multiple_of(ci * A_CHUNK - HALO, HALO)
    return x_ref[pl.ds(start, A_CHUNK + HALO), lo:lo + A_CONV_COLS]


def _gdn_chunk_prep(xw, cw, y=None):
    if y is None:
        y = _conv_taps(xw, cw)
    a = _silu(y)
    aq, ak, v = a[:, 0:A_DK], a[:, A_DK:2 * A_DK], a[:, 2 * A_DK:]
    rq = lax.rsqrt(jnp.sum(aq * aq, axis=1, keepdims=True) + EPS)
    rk = lax.rsqrt(jnp.sum(ak * ak, axis=1, keepdims=True) + EPS)
    return dict(xw=xw, y=y, aq=aq, ak=ak, rq=rq, rk=rk, q=aq * rq * (A_DK ** -0.5), k=ak * rk, v=v)


def _gdn_fwd(proj_hm, cw_hm, beta, gc, norm_g, hp=8):
    bn, s, _ = proj_hm.shape
    n = s // A_CHUNK
    nsb = s // A_SEQ_BLK
    halo = _gdn_halo(proj_hm)

    def body(x_ref, halo_ref, cw_ref, beta_ref, gc_ref, ng_ref, og_ref, oraw_ref, st_ref, t_ref, y_ref, state):
        first_chunk = pl.program_id(2) * A_BLK_CHUNKS
        ri = lax.broadcasted_iota(jnp.int32, (A_CHUNK, A_CHUNK), 0)
        ci_ = lax.broadcasted_iota(jnp.int32, (A_CHUNK, A_CHUNK), 1)
        ng = ng_ref[...]

        @pl.when(pl.program_id(2) == 0)
        def _():
            state[...] = jnp.zeros_like(state)

        def one_head(hh, ci, first, rows):
            lo = hh * A_HEAD_COLS
            cw = cw_ref[:, hh * A_CONV_COLS:(hh + 1) * A_CONV_COLS]
            cin = _gdn_chunk_prep(_gdn_window(x_ref, halo_ref, ci, first, lo), cw)
            y_ref[rows, hh * A_CONV_COLS:(hh + 1) * A_CONV_COLS] = cin["y"]
            seq_chunk = pl.ds(first_chunk + ci, 1)
            core = yield from _gdn_chunk_core(cin["q"], cin["k"], cin["v"], gc_ref[hh, seq_chunk, :],
                                              beta_ref[hh, seq_chunk, :], None, ri, ci_)
            st = state[hh]
            st_ref[hh, ci] = st
            t_ref[hh, ci] = core["t_mat"]
            st16 = st.astype(BF16)
            vn = core["u"] - _dot(core["w"], st16, 1, 0)
            qs = _dot(core["qg"], st16, 1, 0)
            yield
            vn16 = vn.astype(BF16)
            o = qs + _dot(core["p"], vn16, 1, 0)
            state[hh] = st * core["gam_last"] + _dot(core["kd"], vn16, 0, 0)
            yield
            ocols = slice(hh * A_DV, (hh + 1) * A_DV)
            oraw_ref[rows, ocols] = o
            r = lax.rsqrt(jnp.mean(o * o, axis=1, keepdims=True) + EPS)
            z = x_ref[rows, lo + A_CONV_COLS:lo + A_HEAD_COLS]
            og_ref[rows, ocols] = (o * r * ng * _silu(z)).astype(BF16)

        def chunk(ci, first):
            rows = pl.ds(0 if first else pl.multiple_of(ci * A_CHUNK, A_CHUNK), A_CHUNK)
            _round_robin([one_head(hh, ci, first, rows) for hh in range(hp)])

        chunk(0, True)
        lax.fori_loop(1, A_BLK_CHUNKS, lambda i, c: (chunk(i, False), c)[1], 0)

    small = pl.BlockSpec((None, hp, n, A_CHUNK), lambda b, h, j: (b, h, 0, 0))
    return pl.pallas_call(
        body, name="gdn_fwd", grid=(bn, A_HEADS // hp, nsb),
        in_specs=[pl.BlockSpec((None, A_SEQ_BLK, hp * A_HEAD_COLS), lambda b, h, j: (b, j, h)),
                  pl.BlockSpec((None, None, HALO, hp * A_HEAD_COLS), lambda b, h, j: (b, j, 0, h)),
                  pl.BlockSpec((A_CONV, hp * A_CONV_COLS), lambda b, h, j: (0, h)),
                  small, small,
                  pl.BlockSpec((1, A_DV), lambda b, h, j: (0, 0))],
        out_specs=[pl.BlockSpec((None, A_SEQ_BLK, hp * A_DV), lambda b, h, j: (b, j, h)),
                   pl.BlockSpec((None, A_SEQ_BLK, hp * A_DV), lambda b, h, j: (b, j, h)),
                   pl.BlockSpec((None, hp, A_BLK_CHUNKS, A_DK, A_DV), lambda b, h, j: (b, h, j, 0, 0)),
                   pl.BlockSpec((None, hp, A_BLK_CHUNKS, A_CHUNK, A_CHUNK), lambda b, h, j: (b, h, j, 0, 0)),
                   pl.BlockSpec((None, A_SEQ_BLK, hp * A_CONV_COLS), lambda b, h, j: (b, j, h))],
        out_shape=[jax.ShapeDtypeStruct((bn, s, A_VW), BF16),
                   jax.ShapeDtypeStruct((bn, s, A_VW), F32),
                   jax.ShapeDtypeStruct((bn, A_HEADS, n, A_DK, A_DV), F32),
                   jax.ShapeDtypeStruct((bn, A_HEADS, n, A_CHUNK, A_CHUNK), F32),
                   jax.ShapeDtypeStruct((bn, s, A_HEADS * A_CONV_COLS), F32)],
        scratch_shapes=[pltpu.VMEM((hp, A_DK, A_DV), F32)],
        compiler_params=_params(("parallel", "parallel", "arbitrary"), VMEM_BIG),
    )(proj_hm, halo, cw_hm, beta, gc, norm_g)


def _gdn_bwd(proj_hm, cw_hm, beta, gc, norm_g, oraw, states, t_mats, conv_y, dog, hp=4):
    bn, s, _ = proj_hm.shape
    n = s // A_CHUNK
    nsb = s // A_SEQ_BLK
    halo = _gdn_halo(proj_hm)

    def body(x_ref, halo_ref, cw_ref, beta_ref, gc_ref, ng_ref, oraw_ref, st_ref, t_ref, y_ref, dog_ref,
             dx_ref, dgc_ref, dbeta_ref, dcw_ref, dng_ref, dstate, dy_next, shifted):
        first_chunk = (nsb - 1 - pl.program_id(2)) * A_BLK_CHUNKS
        ri = lax.broadcasted_iota(jnp.int32, (A_CHUNK, A_CHUNK), 0)
        ci_ = lax.broadcasted_iota(jnp.int32, (A_CHUNK, A_CHUNK), 1)
        lane = lax.broadcasted_iota(jnp.int32, (1, A_CHUNK), 1)
        ng = ng_ref[...]

        @pl.when(pl.program_id(2) == 0)
        def _():
            dstate[...] = jnp.zeros_like(dstate)
            dy_next[...] = jnp.zeros_like(dy_next)
            dcw_ref[...] = jnp.zeros_like(dcw_ref)
            dng_ref[...] = jnp.zeros_like(dng_ref)

        def one_head(hh, ci, first, rows):
            lo = hh * A_HEAD_COLS
            ccols = slice(hh * A_CONV_COLS, (hh + 1) * A_CONV_COLS)
            ocols = slice(hh * A_DV, (hh + 1) * A_DV)
            cw = cw_ref[:, ccols]
            cin = _gdn_chunk_prep(_gdn_window(x_ref, halo_ref, ci, first, lo), cw, y_ref[rows, ccols])
            q, k, v = cin["q"], cin["k"], cin["v"]
            seq_chunk = pl.ds(first_chunk + ci, 1)
            cr = yield from _gdn_chunk_core(q, k, v, gc_ref[hh, seq_chunk, :], beta_ref[hh, seq_chunk, :],
                                            t_ref[hh, ci], ri, ci_)
            eye, dec, gam, e = cr["eye"], cr["dec"], cr["gam"], cr["e"]
            b_col, t_mat, u, w, p = cr["b_col"], cr["t_mat"], cr["u"], cr["w"], cr["p"]
            st = st_ref[hh, ci]
            ds_out = dstate[hh]

            o = oraw_ref[rows, ocols]
            z = x_ref[rows, lo + A_CONV_COLS:lo + A_HEAD_COLS]
            d_og = dog_ref[rows, ocols]
            r = lax.rsqrt(jnp.mean(o * o, axis=1, keepdims=True) + EPS)
            oh = o * r
            d_on = d_og * _silu(z)
            dz = d_og * oh * ng * _dsilu(z)
            dng_ref[hh, 0:1, :] += jnp.sum(d_on * oh, axis=0, keepdims=True)
            d_oh = d_on * ng
            d_o = r * (d_oh - oh * jnp.mean(d_oh * oh, axis=1, keepdims=True))

            st16, ds16, do16, w16 = st.astype(BF16), ds_out.astype(BF16), d_o.astype(BF16), w.astype(BF16)
            q16, k16, t16 = cr["q16"], cr["k16"], cr["t16"]
            vn = u - _dot(w16, st16, 1, 0)
            d_vn = _dot(p, do16, 0, 0) + _dot(cr["kd"], ds16, 1, 0)
            d_qg = _dot(do16, st16, 1, 1)
            qgdo = _dot(cr["qg"], do16, 0, 0)
            yield
            vn16, dvn16 = vn.astype(BF16), d_vn.astype(BF16)
            d_p = jnp.where(cr["causal"], _dot(do16, vn16, 1, 1), 0.0)
            d_kd = _dot(vn16, ds16, 1, 1)
            d_gam_last = jnp.sum(jnp.sum(st * ds_out, axis=1, keepdims=True), axis=0, keepdims=True)
            d_w = -_dot(dvn16, st16, 1, 1)
            dstate[hh] = qgdo + ds_out * cr["gam_last"] - _dot(w16, dvn16, 0, 0)
            d_bv = _dot(t16, dvn16, 0, 0)
            yield
            d_kbg = _dot(t16, d_w, 0, 0)
            n_p = (d_p * dec).astype(BF16)
            d_q = _dot(n_p, k16, 1, 0) + d_qg * gam
            npq = _dot(n_p, q16, 0, 0)
            yield
            d_a = jnp.where(cr["strict"], -(_dot(d_bv, u, 1, 1) + _dot(d_kbg, w16, 1, 1)), 0.0)
            yield
            m_a = (d_a * dec).astype(BF16)
            d_kb = _dot(m_a, k16, 1, 0) + d_kbg * gam
            d_k = (_dot(m_a, cr["kb16"], 0, 0) + npq + d_kd * e + d_kb * b_col)
            yield
            d_v = d_bv * b_col
            d_beta_col = (jnp.sum(d_bv * v, axis=1, keepdims=True)
                          + jnp.sum(d_kb * k, axis=1, keepdims=True))
            gterm = d_a * cr["a_mat"] + d_p * p
            d_e = jnp.sum(d_kd * k, axis=1, keepdims=True) * e
            d_g_col = (jnp.sum(gterm, axis=1, keepdims=True)
                       + (jnp.sum(d_qg * q, axis=1, keepdims=True)
                          + jnp.sum(d_kbg * cr["kb"], axis=1, keepdims=True)) * gam
                       - d_e)
            d_g_last = jnp.sum(d_e, axis=0, keepdims=True) + d_gam_last * cr["gam_last"]
            d_g_row = (_col_to_row(d_g_col, eye) - jnp.sum(gterm, axis=0, keepdims=True)
                       + jnp.where(lane == A_CHUNK - 1, d_g_last, 0.0))
            dgc_ref[hh, seq_chunk, :] = d_g_row
            dbeta_ref[hh, seq_chunk, :] = _col_to_row(d_beta_col, eye)

            qh = cin["aq"] * cin["rq"]
            kh = cin["ak"] * cin["rk"]
            d_qh = d_q * (A_DK ** -0.5)
            d_aq = cin["rq"] * (d_qh - qh * jnp.sum(d_qh * qh, axis=1, keepdims=True))
            d_ak = cin["rk"] * (d_k - kh * jnp.sum(d_k * kh, axis=1, keepdims=True))
            d_y = jnp.concatenate([d_aq, d_ak, d_v], axis=1) * _dsilu(cin["y"])
            shifted[hh, 0, 0:A_CHUNK, :] = d_y
            shifted[hh, 0, A_CHUNK:A_CHUNK + HALO, :] = dy_next[hh]
            shifted[hh, 1, 0:A_CHUNK + HALO, :] = cin["xw"]
            d_x = cw[A_CONV - 1:A_CONV, :] * d_y
            for j in range(1, A_CONV):
                d_x = d_x + cw[A_CONV - 1 - j:A_CONV - j, :] * shifted[hh, 0, j:j + A_CHUNK, :]
            for j in range(A_CONV):
                xs = shifted[hh, 1, HALO - j:HALO - j + A_CHUNK, :]
                dcw_ref[A_CONV - 1 - j:A_CONV - j, ccols] += jnp.sum(d_y * xs, axis=0, keepdims=True)
            dy_next[hh] = d_y[0:HALO, :]
            dx_ref[rows, lo:lo + A_CONV_COLS] = d_x.astype(BF16)
            dx_ref[rows, lo + A_CONV_COLS:lo + A_HEAD_COLS] = dz.astype(BF16)

        def chunk(ci, first):
            rows = pl.ds(0 if first else pl.multiple_of(ci * A_CHUNK, A_CHUNK), A_CHUNK)
            _round_robin([one_head(hh, ci, first, rows) for hh in range(hp)])

        lax.fori_loop(0, A_BLK_CHUNKS - 1, lambda i, c: (chunk(A_BLK_CHUNKS - 1 - i, False), c)[1], 0)
        chunk(0, True)

    rev = lambda j: nsb - 1 - j
    small = pl.BlockSpec((None, hp, n, A_CHUNK), lambda b, h, j: (b, h, 0, 0))
    wide = pl.BlockSpec((None, A_SEQ_BLK, hp * A_HEAD_COLS), lambda b, h, j: (b, rev(j), h))
    val = pl.BlockSpec((None, A_SEQ_BLK, hp * A_DV), lambda b, h, j: (b, rev(j), h))
    return pl.pallas_call(
        body, name="gdn_bwd", grid=(bn, A_HEADS // hp, nsb),
        in_specs=[wide,
                  pl.BlockSpec((None, None, HALO, hp * A_HEAD_COLS), lambda b, h, j: (b, rev(j), 0, h)),
                  pl.BlockSpec((A_CONV, hp * A_CONV_COLS), lambda b, h, j: (0, h)),
                  small, small,
                  pl.BlockSpec((1, A_DV), lambda b, h, j: (0, 0)),
                  val,
                  pl.BlockSpec((None, hp, A_BLK_CHUNKS, A_DK, A_DV), lambda b, h, j: (b, h, rev(j), 0, 0)),
                  pl.BlockSpec((None, hp, A_BLK_CHUNKS, A_CHUNK, A_CHUNK), lambda b, h, j: (b, h, rev(j), 0, 0)),
                  pl.BlockSpec((None, A_SEQ_BLK, hp * A_CONV_COLS), lambda b, h, j: (b, rev(j), h)),
                  val],
        out_specs=[wide, small, small,
                   pl.BlockSpec((None, A_CONV, hp * A_CONV_COLS), lambda b, h, j: (b, 0, h)),
                   pl.BlockSpec((None, hp, 8, A_DV), lambda b, h, j: (b, h, 0, 0))],
        out_shape=[jax.ShapeDtypeStruct((bn, s, A_HEADS * A_HEAD_COLS), BF16),
                   jax.ShapeDtypeStruct((bn, A_HEADS, n, A_CHUNK), F32),
                   jax.ShapeDtypeStruct((bn, A_HEADS, n, A_CHUNK), F32),
                   jax.ShapeDtypeStruct((bn, A_CONV, A_HEADS * A_CONV_COLS), F32),
                   jax.ShapeDtypeStruct((bn, A_HEADS, 8, A_DV), F32)],
        scratch_shapes=[pltpu.VMEM((hp, A_DK, A_DV), F32), pltpu.VMEM((hp, HALO, A_CONV_COLS), F32),
                        pltpu.VMEM((hp, 2, A_CHUNK + 2 * HALO, A_CONV_COLS), F32)],
        compiler_params=_params(("parallel", "parallel", "arbitrary"), VMEM_BIG),
    )(proj_hm, halo, cw_hm, beta, gc, norm_g, oraw, states, t_mats, conv_y, dog)


def _rope_tables(posf, inv_freq_row):
    t = posf.shape[0]
    tm = 512

    def body(p_ref, f_ref, c_ref, sa_ref, sb_ref):
        ang = p_ref[...] * f_ref[...]
        lane = lax.broadcasted_iota(jnp.int32, ang.shape, 1)
        half = ROPE_DIMS // 2
        c_ref[...] = jnp.where(lane < ROPE_DIMS, jnp.cos(ang), 1.0)
        sn = jnp.sin(ang)
        sa_ref[...] = jnp.where(lane < half, -sn, 0.0)
        sb_ref[...] = jnp.where((lane >= half) & (lane < ROPE_DIMS), sn, 0.0)

    row = pl.BlockSpec((tm, 128), lambda i: (i, 0))
    return pl.pallas_call(
        body, name="rope_tables", grid=(t // tm,),
        in_specs=[row, pl.BlockSpec((1, 128), lambda i: (0, 0))], out_specs=[row] * 3,
        out_shape=[jax.ShapeDtypeStruct((t, 128), F32)] * 3,
        compiler_params=_params(("parallel",)),
    )(posf, inv_freq_row)


def _rope(x, c, sa, sb):
    half = ROPE_DIMS // 2
    return x * c + pltpu.roll(x, 128 - half, 1) * sa + pltpu.roll(x, half, 1) * sb


def _rope_t(d, c, sa, sb):
    half = ROPE_DIMS // 2
    return d * c + pltpu.roll(d * sa, half, 1) + pltpu.roll(d * sb, 128 - half, 1)


def _qk_prep(proj, c, sa, sb, qg, kg, name, tm=256):
    t = proj.shape[0]

    def body(x_ref, c_ref, sa_ref, sb_ref, qg_ref, kg_ref, o_ref):
        cc, s1, s2 = c_ref[...], sa_ref[...], sb_ref[...]
        for which, g_ref in ((0, qg_ref), (1, kg_ref)):
            g = g_ref[...]
            for h in range(B_HEADS):
                lo = which * B_W + h * B_DH
                xv = x_ref[:, lo:lo + B_DH]
                r = lax.rsqrt(jnp.mean(xv * xv, axis=1, keepdims=True) + EPS)
                o_ref[:, lo:lo + B_DH] = _rope(xv * r * g, cc, s1, s2).astype(BF16)
        o_ref[:, 2 * B_W:3 * B_W] = x_ref[:, 2 * B_W:3 * B_W].astype(BF16)

    tab = pl.BlockSpec((tm, 128), lambda i: (i, 0))
    gain = pl.BlockSpec((1, B_DH), lambda i: (0, 0))
    return pl.pallas_call(
        body, name=name, grid=(t // tm,),
        in_specs=[pl.BlockSpec((tm, 3 * B_W), lambda i: (i, 0)), tab, tab, tab, gain, gain],
        out_specs=pl.BlockSpec((tm, 3 * B_W), lambda i: (i, 0)),
        out_shape=jax.ShapeDtypeStruct((t, 3 * B_W), BF16),
        compiler_params=_params(("parallel",), 40 * 1024 * 1024),
    )(proj, c, sa, sb, qg, kg)


def _qk_prep_bwd(proj, c, sa, sb, qg, kg, dq, dk, dv, dz, name, tm=256):
    t = proj.shape[0]
    out_w = 3 * B_W + (B_W if dz is not None else 0)

    def body(*refs):
        x_ref, c_ref, sa_ref, sb_ref, qg_ref, kg_ref, dq_ref, dk_ref, dv_ref = refs[:9]
        if dz is not None:
            dz_ref, o_ref, dgain_ref = refs[9:]
        else:
            o_ref, dgain_ref = refs[9:]
        i = pl.program_id(0)

        @pl.when(i == 0)
        def _():
            dgain_ref[...] = jnp.zeros_like(dgain_ref)

        cc, s1, s2 = c_ref[...], sa_ref[...], sb_ref[...]
        for which, g_ref, d_ref in ((0, qg_ref, dq_ref), (1, kg_ref, dk_ref)):
            g = g_ref[...]
            acc = jnp.zeros((1, B_DH), F32)
            for h in range(B_HEADS):
                lo = which * B_W + h * B_DH
                xv = x_ref[:, lo:lo + B_DH]
                r = lax.rsqrt(jnp.mean(xv * xv, axis=1, keepdims=True) + EPS)
                xh = xv * r
                d_xn = _rope_t(d_ref[:, h * B_DH:(h + 1) * B_DH].astype(F32), cc, s1, s2)
                acc = acc + jnp.sum(d_xn * xh, axis=0, keepdims=True)
                d_xh = d_xn * g
                d_x = r * (d_xh - xh * jnp.mean(d_xh * xh, axis=1, keepdims=True))
                o_ref[:, lo:lo + B_DH] = d_x.astype(BF16)
            dgain_ref[which:which + 1, :] += acc
        o_ref[:, 2 * B_W:3 * B_W] = dv_ref[...]
        if dz is not None:
            o_ref[:, 3 * B_W:4 * B_W] = dz_ref[...]

    tab = pl.BlockSpec((tm, 128), lambda i: (i, 0))
    gain = pl.BlockSpec((1, B_DH), lambda i: (0, 0))
    grad = pl.BlockSpec((tm, B_W), lambda i: (i, 0))
    in_specs = [pl.BlockSpec((tm, 2 * B_W), lambda i: (i, 0)), tab, tab, tab, gain, gain, grad, grad, grad]
    args = [proj, c, sa, sb, qg, kg, dq, dk, dv]
    if dz is not None:
        in_specs.append(grad)
        args.append(dz)
    return pl.pallas_call(
        body, name=name, grid=(t // tm,), in_specs=in_specs,
        out_specs=[pl.BlockSpec((tm, out_w), lambda i: (i, 0)), pl.BlockSpec((8, B_DH), lambda i: (0, 0))],
        out_shape=[jax.ShapeDtypeStruct((t, out_w), BF16), jax.ShapeDtypeStruct((8, B_DH), F32)],
        compiler_params=_params(("arbitrary",), 40 * 1024 * 1024),
    )(*args)


def _attn_masks():
    qi = lax.broadcasted_iota(jnp.int32, (B_BLK, 2 * B_BLK), 0)
    kj = lax.broadcasted_iota(jnp.int32, (B_BLK, 2 * B_BLK), 1)
    two = (kj >= qi) & (kj <= qi + B_BLK)
    q1 = lax.broadcasted_iota(jnp.int32, (B_BLK, B_BLK), 0)
    k1 = lax.broadcasted_iota(jnp.int32, (B_BLK, B_BLK), 1)
    return k1 <= q1, two


def _lane_pick(ref_rows, h):
    lane = lax.broadcasted_iota(jnp.int32, ref_rows.shape, 1)
    return jnp.sum(jnp.where(lane == h, ref_rows, 0.0), axis=1, keepdims=True)


B_ROWS = 2048


def _attn_schedule(nb, sb, block):
    way = 4

    def run(items):
        for at in range(0, len(items), way):
            _round_robin([block(*it) for it in items[at:at + way]])

    run([(si, 0, True) for si in range(sb)])
    if nb == 1:
        return
    per = max(1, way // sb)
    lead = 1 + (nb - 1) % per
    if lead > 1:
        run([(si, i, False) for i in range(1, lead) for si in range(sb)])

    def step(it, carry):
        run([(si, lead + it * per + u, False) for u in range(per) for si in range(sb)])
        return carry

    lax.fori_loop(0, (nb - lead) // per, step, 0)


def _attn_rows(i, first):
    if first:
        return pl.ds(0, B_BLK), pl.ds(0, B_BLK)
    rows = pl.ds(pl.multiple_of(i * B_BLK, B_BLK), B_BLK)
    return rows, pl.ds(pl.multiple_of((i - 1) * B_BLK, B_BLK), 2 * B_BLK)


def _attn_fwd(qkv, name):
    ns, ln, _ = qkv.shape
    nb = ln // B_BLK
    sb = B_ROWS // ln
    scale = B_DH ** -0.5

    def body(q_ref, k_ref, v_ref, o_ref, lse_ref):
        h = pl.program_id(1)
        mask1, mask2 = _attn_masks()
        lane = lax.broadcasted_iota(jnp.int32, (B_BLK, B_HEADS), 1)

        @pl.when(h == 0)
        def _():
            lse_ref[...] = jnp.zeros_like(lse_ref)

        def block(si, i, first):
            rows, win = _attn_rows(i, first)
            mask = mask1 if first else mask2
            sc = jnp.where(mask, _dot(q_ref[si, rows, :], k_ref[si, win, :], 1, 1) * scale, -1e30)
            yield
            m = jnp.max(sc, axis=1, keepdims=True)
            p = jnp.exp(sc - m)
            l = jnp.sum(p, axis=1, keepdims=True)
            pv = _dot(p, v_ref[si, win, :], 1, 0)
            yield
            o_ref[si, rows, :] = pv / l
            lse_ref[si, rows, :] = jnp.where(lane == h, m + jnp.log(l), lse_ref[si, rows, :])

        _attn_schedule(nb, sb, block)

    head = lambda off: pl.BlockSpec((sb, ln, B_DH), lambda s, h: (s, 0, off + h))
    return pl.pallas_call(
        body, name=name, grid=(ns // sb, B_HEADS),
        in_specs=[head(0), head(B_HEADS), head(2 * B_HEADS)],
        out_specs=[head(0), pl.BlockSpec((sb, ln, B_HEADS), lambda s, h: (s, 0, 0))],
        out_shape=[jax.ShapeDtypeStruct((ns, ln, B_W), F32), jax.ShapeDtypeStruct((ns, ln, B_HEADS), F32)],
        compiler_params=_params(("parallel", "arbitrary")),
    )(qkv, qkv, qkv)


def _attn_bwd(qkv, d_o, lse_joint, delta, name):
    ns, ln, _ = qkv.shape
    nb = ln // B_BLK
    sb = B_ROWS // ln
    scale = B_DH ** -0.5

    def body(q_ref, k_ref, v_ref, do_ref, lj_ref, dl_ref, dq_ref, dk_out, dv_out, dk_ref, dv_ref):
        h = pl.program_id(1)
        mask1, mask2 = _attn_masks()
        dk_ref[...] = jnp.zeros_like(dk_ref)
        dv_ref[...] = jnp.zeros_like(dv_ref)

        def block(si, i, first):
            rows, win = _attn_rows(i, first)
            mask = mask1 if first else mask2
            q = q_ref[si, rows, :]
            d_out = do_ref[si, rows, :]
            l_col = _lane_pick(lj_ref[si, rows, :], h)
            d_col = _lane_pick(dl_ref[si, rows, :], h)
            sc = _dot(q, k_ref[si, win, :], 1, 1) * scale
            d_p = _dot(d_out, v_ref[si, win, :], 1, 1)
            yield
            p = jnp.exp(jnp.where(mask, sc - l_col, -1e30))
            d_s = p * (d_p - d_col) * scale
            d_q = _dot(d_s, k_ref[si, win, :], 1, 0)
            d_k = _dot(d_s, q, 0, 0)
            d_v = _dot(p, d_out, 0, 0)
            yield
            dq_ref[si, rows, :] = d_q.astype(BF16)
            dk_ref[si, win, :] += d_k
            dv_ref[si, win, :] += d_v

        _attn_schedule(nb, sb, block)
        dk_out[...] = dk_ref[...].astype(BF16)
        dv_out[...] = dv_ref[...].astype(BF16)

    head = lambda off: pl.BlockSpec((sb, ln, B_DH), lambda s, h: (s, 0, off + h))
    small = pl.BlockSpec((sb, ln, B_HEADS), lambda s, h: (s, 0, 0))
    return pl.pallas_call(
        body, name=name, grid=(ns // sb, B_HEADS),
        in_specs=[head(0), head(B_HEADS), head(2 * B_HEADS), head(0), small, small],
        out_specs=[head(0)] * 3,
        out_shape=[jax.ShapeDtypeStruct((ns, ln, B_W), BF16)] * 3,
        scratch_shapes=[pltpu.VMEM((sb, ln, B_DH), F32)] * 2,
        compiler_params=_params(("parallel", "parallel")),
    )(qkv, qkv, qkv, d_o, lse_joint, delta)


def _merge_weights(lse_refs):
    ls = [r[...] for r in lse_refs]
    m = jnp.maximum(jnp.maximum(ls[0], ls[1]), ls[2])
    es = [jnp.exp(l - m) for l in ls]
    tot = es[0] + es[1] + es[2]
    return [e / tot for e in es], m + jnp.log(tot)


def _merge_fwd(outs, lses, proj0, tm=256):
    t = outs[0].shape[0]

    def body(o0, o1, o2, l0, l1, l2, z_ref, og_ref):
        wts, _ = _merge_weights((l0, l1, l2))
        for h in range(B_HEADS):
            cols = slice(h * B_DH, (h + 1) * B_DH)
            o = (wts[0][:, h:h + 1] * o0[:, cols] + wts[1][:, h:h + 1] * o1[:, cols]
                 + wts[2][:, h:h + 1] * o2[:, cols])
            og_ref[:, cols] = (o * _silu(z_ref[:, cols])).astype(BF16)

    wide = pl.BlockSpec((tm, B_W), lambda i: (i, 0))
    small = pl.BlockSpec((tm, B_HEADS), lambda i: (i, 0))
    return pl.pallas_call(
        body, name="merge_fwd", grid=(t // tm,),
        in_specs=[wide] * 3 + [small] * 3 + [pl.BlockSpec((tm, B_W), lambda i: (i, 3))],
        out_specs=wide, out_shape=jax.ShapeDtypeStruct((t, B_W), BF16),
        compiler_params=_params(("parallel",)),
    )(*outs, *lses, proj0)


def _merge_bwd(outs, lses, proj0, d_og, tm=256):
    t = outs[0].shape[0]

    def body(o0, o1, o2, l0, l1, l2, z_ref, dog_ref, do_ref, lj_ref, dl_ref, dz_ref):
        wts, lj = _merge_weights((l0, l1, l2))
        lj_ref[...] = lj
        lane = lax.broadcasted_iota(jnp.int32, (tm, B_HEADS), 1)
        delta = jnp.zeros((tm, B_HEADS), F32)
        for h in range(B_HEADS):
            cols = slice(h * B_DH, (h + 1) * B_DH)
            o = (wts[0][:, h:h + 1] * o0[:, cols] + wts[1][:, h:h + 1] * o1[:, cols]
                 + wts[2][:, h:h + 1] * o2[:, cols])
            z = z_ref[:, cols]
            d_g = dog_ref[:, cols]
            d_out = d_g * _silu(z)
            dz_ref[:, cols] = (d_g * o * _dsilu(z)).astype(BF16)
            do_ref[:, cols] = d_out.astype(BF16)
            delta = jnp.where(lane == h, jnp.sum(d_out * o, axis=1, keepdims=True), delta)
        dl_ref[...] = delta

    wide = pl.BlockSpec((tm, B_W), lambda i: (i, 0))
    small = pl.BlockSpec((tm, B_HEADS), lambda i: (i, 0))
    return pl.pallas_call(
        body, name="merge_bwd", grid=(t // tm,),
        in_specs=[wide] * 3 + [small] * 3 + [pl.BlockSpec((tm, B_W), lambda i: (i, 3)), wide],
        out_specs=[wide, small, small, wide],
        out_shape=[jax.ShapeDtypeStruct((t, B_W), BF16), jax.ShapeDtypeStruct((t, B_HEADS), F32),
                   jax.ShapeDtypeStruct((t, B_HEADS), F32), jax.ShapeDtypeStruct((t, B_W), BF16)],
        compiler_params=_params(("parallel",)),
    )(*outs, *lses, proj0, d_og)


def _adamw(w, g, m, v, name):
    r, c = w.shape
    tr = r
    for cand in (256, 128, 64, 32, 16, 8):
        if r % cand == 0:
            tr = cand
            break

    def body(w_ref, g_ref, m_ref, v_ref, d_ref, nm_ref, nv_ref):
        gv = g_ref[...]
        nm = ADAM_B1 * m_ref[...] + (1.0 - ADAM_B1) * gv
        nv = ADAM_B2 * v_ref[...] + (1.0 - ADAM_B2) * (gv * gv)
        m_hat = nm / (1.0 - ADAM_B1 ** ADAM_STEP)
        v_hat = nv / (1.0 - ADAM_B2 ** ADAM_STEP)
        d_ref[...] = -ADAM_LR * (m_hat / (jnp.sqrt(v_hat) + ADAM_EPS) + ADAM_WD * w_ref[...])
        nm_ref[...] = nm
        nv_ref[...] = nv

    blk = pl.BlockSpec((tr, c), lambda i: (i, 0))
    return pl.pallas_call(
        body, name=name, grid=(r // tr,), in_specs=[blk] * 4, out_specs=[blk] * 3,
        out_shape=[jax.ShapeDtypeStruct((r, c), F32)] * 3,
        compiler_params=_params(("parallel",)),
    )(w, g, m, v)


def _adam_update(w, gv, m, v):
    nm = ADAM_B1 * m + (1.0 - ADAM_B1) * gv
    nv = ADAM_B2 * v + (1.0 - ADAM_B2) * (gv * gv)
    m_hat = nm / (1.0 - ADAM_B1 ** ADAM_STEP)
    v_hat = nv / (1.0 - ADAM_B2 ** ADAM_STEP)
    return -ADAM_LR * (m_hat / (jnp.sqrt(v_hat) + ADAM_EPS) + ADAM_WD * w), nm, nv


def _adamw_shard(w, mine, theirs, m, v, half_index, name, tr=128):
    _, r, c = w.shape
    nhb = (r // 2) // tr

    def body(c_ref, w_ref, mine_ref, theirs_ref, m_ref, v_ref, g_ref, d_ref, nm_ref, nv_ref):
        is_mine = (pl.program_id(0) // nhb) == c_ref[0]
        gv = jnp.where(is_mine, mine_ref[...], theirs_ref[...])
        d, nm, nv = _adam_update(w_ref[...], gv, m_ref[...], v_ref[...])
        g_ref[...] = gv
        d_ref[...] = d
        nm_ref[...] = nm
        nv_ref[...] = nv

    full = pl.BlockSpec((None, tr, c), lambda i, cc: (0, i, 0))
    half = pl.BlockSpec((tr, c), lambda i, cc: (i % nhb, 0))
    return pl.pallas_call(
        body, name=name,
        grid_spec=pltpu.PrefetchScalarGridSpec(
            num_scalar_prefetch=1, grid=(2 * nhb,),
            in_specs=[full, half, half, full, full], out_specs=[full] * 4),
        out_shape=[jax.ShapeDtypeStruct(w.shape, F32)] * 4,
        compiler_params=_params(("parallel",), 40 * 1024 * 1024),
    )(half_index, w, mine, theirs, m, v)


def _pair_sum(own, other, half_index, name, tr=256):
    _, r, c = own.shape
    rh = r // 2
    tr = min(tr, rh)
    nrb = rh // tr

    def body(c_ref, own_ref, oth_ref, out_ref):
        out_ref[...] = (own_ref[...] + oth_ref[...].astype(F32)).astype(BF16)

    return pl.pallas_call(
        body, name=name,
        grid_spec=pltpu.PrefetchScalarGridSpec(
            num_scalar_prefetch=1, grid=(N_CHIPS, nrb),
            in_specs=[pl.BlockSpec((None, tr, c), lambda k, i, cc: (k, cc[0] * nrb + i, 0)),
                      pl.BlockSpec((None, tr, c), lambda k, i, cc: (k, i, 0))],
            out_specs=pl.BlockSpec((None, tr, c), lambda k, i, cc: (k, i, 0))),
        out_shape=jax.ShapeDtypeStruct((N_CHIPS, rh, c), BF16),
        compiler_params=_params(("parallel", "parallel")),
    )(half_index, own, other)


def _chip_sum(sums, others, chip_index, name, tr=256):
    _, r, c = sums.shape
    tr = min(tr, r)

    def body(k_ref, own_ref, oth_ref, out_ref):
        acc = own_ref[...].astype(F32)
        for j in range(N_CHIPS - 1):
            acc = acc + oth_ref[j].astype(F32)
        out_ref[...] = acc

    return pl.pallas_call(
        body, name=name,
        grid_spec=pltpu.PrefetchScalarGridSpec(
            num_scalar_prefetch=1, grid=(r // tr,),
            in_specs=[pl.BlockSpec((None, tr, c), lambda i, kk: (kk[0], i, 0)),
                      pl.BlockSpec((N_CHIPS - 1, tr, c), lambda i, kk: (0, i, 0))],
            out_specs=pl.BlockSpec((tr, c), lambda i, kk: (i, 0))),
        out_shape=jax.ShapeDtypeStruct((r, c), F32),
        compiler_params=_params(("parallel",)),
    )(chip_index, sums, others)


HBM = pl.BlockSpec(memory_space=pltpu.HBM)


def _place():
    x, y, c = lax.axis_index("x"), lax.axis_index("y"), lax.axis_index("c")
    chips = [(1 - x, y), (x, 1 - y), (1 - x, 1 - y)]
    return x, y, c, chips


def _weight_allgather(shards, conv_shard):
    na = len(shards)

    def body(*refs):
        ins = refs[:na]
        conv_in = refs[na]
        outs = refs[na + 1:2 * na + 1]
        conv_out = refs[2 * na + 1]
        send, recv, fsend, frecv, csend, crecv = refs[2 * na + 2:]
        x, y, c, chips = _place()
        me = 2 * x + y
        sib = (x, y, 1 - c)
        first, conv_cp = [], []
        for i in range(na):
            rh = ins[i].shape[0] // 2
            mine = pl.ds(c * rh, rh)
            for j, (px, py) in enumerate(chips):
                cp = pltpu.make_async_remote_copy(
                    src_ref=ins[i].at[mine], dst_ref=outs[i].at[me, mine],
                    send_sem=send.at[3 * i + j], recv_sem=recv.at[3 * i + j],
                    device_id=(px, py, c), device_id_type=MESH)
                cp.start()
                first.append(cp)
        for j, (px, py) in enumerate(chips):
            cp = pltpu.make_async_remote_copy(
                src_ref=conv_in, dst_ref=conv_out.at[me], send_sem=csend.at[j], recv_sem=crecv.at[j],
                device_id=(px, py, c), device_id_type=MESH)
            cp.start()
            conv_cp.append(cp)
        passed = []
        for i in range(na):
            rh = ins[i].shape[0] // 2
            mine = pl.ds(c * rh, rh)
            for j, (px, py) in enumerate(chips):
                slot = outs[i].at[2 * px + py, mine]
                pltpu.make_async_remote_copy(
                    src_ref=slot, dst_ref=slot, send_sem=send.at[3 * i + j], recv_sem=recv.at[3 * i + j],
                    device_id=(px, py, c), device_id_type=MESH).wait_recv()
                cp = pltpu.make_async_remote_copy(
                    src_ref=slot, dst_ref=slot, send_sem=fsend.at[3 * i + j], recv_sem=frecv.at[3 * i + j],
                    device_id=sib, device_id_type=MESH)
                cp.start()
                passed.append(cp)
        for i in range(na):
            rh = ins[i].shape[0] // 2
            theirs = pl.ds((1 - c) * rh, rh)
            for j, (px, py) in enumerate(chips):
                slot = outs[i].at[2 * px + py, theirs]
                pltpu.make_async_remote_copy(
                    src_ref=slot, dst_ref=slot, send_sem=fsend.at[3 * i + j], recv_sem=frecv.at[3 * i + j],
                    device_id=sib, device_id_type=MESH).wait_recv()
        for j, (px, py) in enumerate(chips):
            slot = conv_out.at[2 * px + py]
            pltpu.make_async_remote_copy(
                src_ref=slot, dst_ref=slot, send_sem=csend.at[j], recv_sem=crecv.at[j],
                device_id=(px, py, c), device_id_type=MESH).wait_recv()
        for cp in first + passed + conv_cp:
            cp.wait_send()

    out_shape = [jax.ShapeDtypeStruct((N_CHIPS,) + s.shape, s.dtype) for s in shards]
    out_shape.append(jax.ShapeDtypeStruct((N_CHIPS,) + conv_shard.shape, conv_shard.dtype))
    res = pl.pallas_call(
        body, name="weight_allgather",
        in_specs=[HBM] * (na + 1), out_specs=[HBM] * (na + 1), out_shape=out_shape,
        scratch_shapes=[pltpu.SemaphoreType.DMA((3 * na,)), pltpu.SemaphoreType.DMA((3 * na,)),
                        pltpu.SemaphoreType.DMA((3 * na,)), pltpu.SemaphoreType.DMA((3 * na,)),
                        pltpu.SemaphoreType.DMA((3,)), pltpu.SemaphoreType.DMA((3,))],
    )(*shards, conv_shard)
    my_chip = 2 * lax.axis_index("x") + lax.axis_index("y")
    pick = lambda got, own: [jnp.where(my_chip == k, own, got[k]) for k in range(N_CHIPS)]
    return [pick(g, s) for g, s in zip(res[:na], shards)], pick(res[na], conv_shard)


def _sibling_swap_halves(grads, name):
    na = len(grads)

    def body(*refs):
        ins, outs = refs[:na], refs[na:2 * na]
        send, recv = refs[2 * na:]
        x, y, c, _ = _place()
        sib = (x, y, 1 - c)
        cps = []
        for i in range(na):
            rh = ins[i].shape[1] // 2
            cp = pltpu.make_async_remote_copy(
                src_ref=ins[i].at[:, pl.ds((1 - c) * rh, rh), :], dst_ref=outs[i],
                send_sem=send.at[i], recv_sem=recv.at[i], device_id=sib, device_id_type=MESH)
            cp.start()
            cps.append(cp)
        for cp in cps:
            cp.wait()

    out_shape = [jax.ShapeDtypeStruct((g.shape[0], g.shape[1] // 2, g.shape[2]), g.dtype) for g in grads]
    return pl.pallas_call(
        body, name=name, in_specs=[HBM] * na, out_specs=[HBM] * na, out_shape=out_shape,
        scratch_shapes=[pltpu.SemaphoreType.DMA((na,)), pltpu.SemaphoreType.DMA((na,))],
    )(*grads)


def _sibling_swap_whole(halves):
    na = len(halves)

    def body(*refs):
        ins, outs = refs[:na], refs[na:2 * na]
        send, recv = refs[2 * na:]
        x, y, c, _ = _place()
        cps = []
        for i in range(na):
            cp = pltpu.make_async_remote_copy(
                src_ref=ins[i], dst_ref=outs[i], send_sem=send.at[i], recv_sem=recv.at[i],
                device_id=(x, y, 1 - c), device_id_type=MESH)
            cp.start()
            cps.append(cp)
        for cp in cps:
            cp.wait()

    out_shape = [jax.ShapeDtypeStruct(h.shape, h.dtype) for h in halves]
    return pl.pallas_call(
        body, name="grad_sibling_join", in_specs=[HBM] * na, out_specs=[HBM] * na, out_shape=out_shape,
        scratch_shapes=[pltpu.SemaphoreType.DMA((na,)), pltpu.SemaphoreType.DMA((na,))],
    )(*halves)


SEM = pl.BlockSpec(memory_space=pltpu.SEMAPHORE)
ANY = pl.BlockSpec(memory_space=pl.ANY)
EFFECT = pltpu.SideEffectType.DATAFLOW_SIDE_EFFECTING


def _split_copy_start(name, plan, srcs, lands, after):
    ns, nl = len(srcs), len(lands)

    def body(*refs):
        src_refs, land_refs = refs[:ns], refs[ns:ns + nl]
        send, recv = refs[ns + nl + 1], refs[ns + nl + 2]
        token = refs[-1]
        outgoing, _ = plan(src_refs, land_refs)
        for src, dst, dev, si, ri in outgoing:
            pltpu.make_async_remote_copy(src_ref=src, dst_ref=dst, send_sem=send.at[si], recv_sem=recv.at[ri],
                                         device_id=dev, device_id_type=MESH).start()
        token[...] = jnp.zeros_like(token)

    n_out, n_in = plan.counts
    thru = [pltpu.HBM(a.shape, a.dtype) for a in list(srcs) + list(lands)]
    res = pl.pallas_call(
        body, name=name,
        out_shape=[pltpu.SemaphoreType.DMA((n_out,)), pltpu.SemaphoreType.DMA((n_in,))] + thru
        + [jax.ShapeDtypeStruct((8, 128), F32)],
        in_specs=[HBM] * (ns + nl) + [ANY],
        out_specs=[SEM, SEM] + [HBM] * (ns + nl) + [pl.BlockSpec(memory_space=pltpu.VMEM)],
        input_output_aliases={i: 2 + i for i in range(ns + nl)},
        compiler_params=pltpu.CompilerParams(has_side_effects=EFFECT),
    )(*[pltpu.with_memory_space_constraint(a, pltpu.HBM) for a in list(srcs) + list(lands)], after)
    return res[0], res[1], res[2:2 + ns], res[2 + ns:2 + ns + nl], res[-1]


def _split_copy_wait(name, plan, send, recv, srcs, lands, after):
    ns, nl = len(srcs), len(lands)

    def body(*refs):
        src_refs, land_refs = refs[:ns], refs[ns:ns + nl]
        send_ref, recv_ref = refs[ns + nl], refs[ns + nl + 1]
        outgoing, arrivals = plan(src_refs, land_refs)
        for src, dst, dev, si, ri in outgoing:
            pltpu.make_async_remote_copy(src_ref=src, dst_ref=dst, send_sem=send_ref.at[si], recv_sem=recv_ref.at[ri],
                                         device_id=dev, device_id_type=MESH).wait_send()
        for view, ri in arrivals:
            pltpu.make_async_remote_copy(src_ref=view, dst_ref=view, send_sem=send_ref.at[0], recv_sem=recv_ref.at[ri],
                                         device_id=_place()[:3], device_id_type=MESH).wait_recv()

    thru = [pltpu.HBM(a.shape, a.dtype) for a in list(srcs) + list(lands)]
    res = pl.pallas_call(
        body, name=name, out_shape=thru,
        in_specs=[HBM] * (ns + nl) + [SEM, SEM, ANY], out_specs=[HBM] * (ns + nl),
        input_output_aliases={i: i for i in range(ns + nl)},
        compiler_params=pltpu.CompilerParams(has_side_effects=EFFECT),
    )(*srcs, *lands, send, recv, after)
    return res[:ns], res[ns:]


def _gather_plan(n_arrays):
    def plan(src_refs, land_refs):
        x, y, c, chips = _place()
        me = 2 * x + y
        outgoing, arrivals = [], []
        for i in range(n_arrays):
            rh = src_refs[i].shape[0] // 2
            mine = pl.ds(c * rh, rh)
            for j, (px, py) in enumerate(chips):
                for delta in range(2):
                    tc = c ^ delta
                    outgoing.append((src_refs[i].at[mine], land_refs[i].at[me, mine], (px, py, tc),
                                     6 * i + 2 * j + delta, 6 * i + 2 * j + delta))
                    theirs = pl.ds(tc * rh, rh)
                    arrivals.append((land_refs[i].at[2 * px + py, theirs], 6 * i + 2 * j + delta))
        return outgoing, arrivals

    plan.counts = (6 * n_arrays, 6 * n_arrays)
    return plan


def _exchange_plan(n_arrays):
    def plan(src_refs, land_refs):
        x, y, c, chips = _place()
        outgoing, arrivals = [], []
        for i in range(n_arrays):
            for j, (px, py) in enumerate(chips):
                outgoing.append((src_refs[i].at[2 * px + py], land_refs[i].at[j], (px, py, c), 3 * i + j, 3 * i + j))
                arrivals.append((land_refs[i].at[j], 3 * i + j))
        return outgoing, arrivals

    plan.counts = (3 * n_arrays, 3 * n_arrays)
    return plan


def _small_allreduce(vec):
    r, cdim = vec.shape
    n_dev = 8

    def body(v_ref, out_ref, buf, send, recv):
        x, y, c, _ = _place()
        me = 4 * x + 2 * y + c
        buf[me] = v_ref[...]
        cps = []
        for k in range(1, n_dev):
            dx, dy, dc = (k >> 2) & 1, (k >> 1) & 1, k & 1
            peer = (x ^ dx, y ^ dy, c ^ dc)
            cp = pltpu.make_async_remote_copy(
                src_ref=v_ref, dst_ref=buf.at[me], send_sem=send.at[k - 1], recv_sem=recv.at[k - 1],
                device_id=peer, device_id_type=MESH)
            cp.start()
            cps.append(cp)
        for k in range(1, n_dev):
            dx, dy, dc = (k >> 2) & 1, (k >> 1) & 1, k & 1
            src = 4 * (x ^ dx) + 2 * (y ^ dy) + (c ^ dc)
            slot = buf.at[src]
            pltpu.make_async_remote_copy(
                src_ref=slot, dst_ref=slot, send_sem=send.at[k - 1], recv_sem=recv.at[k - 1],
                device_id=(x ^ dx, y ^ dy, c ^ dc), device_id_type=MESH).wait_recv()
        for cp in cps:
            cp.wait_send()
        acc = buf[0]
        for k in range(1, n_dev):
            acc = acc + buf[k]
        out_ref[...] = acc

    vm = pl.BlockSpec(memory_space=pltpu.VMEM)
    return pl.pallas_call(
        body, name="small_allreduce", in_specs=[vm], out_specs=vm,
        out_shape=jax.ShapeDtypeStruct((r, cdim), F32),
        scratch_shapes=[pltpu.VMEM((n_dev, r, cdim), F32), pltpu.SemaphoreType.DMA((n_dev - 1,)),
                        pltpu.SemaphoreType.DMA((n_dev - 1,))],
    )(vec)


def _a_cols_to_head_major(w):
    lead = w.shape[:-1]
    q = w[..., :A_QK].reshape(lead + (A_HEADS, A_DK))
    k = w[..., A_QK:2 * A_QK].reshape(lead + (A_HEADS, A_DK))
    v = w[..., 2 * A_QK:2 * A_QK + A_VW].reshape(lead + (A_HEADS, A_DV))
    z = w[..., 2 * A_QK + A_VW:].reshape(lead + (A_HEADS, A_DV))
    return jnp.concatenate([q, k, v, z], axis=-1).reshape(lead + (A_HEADS * A_HEAD_COLS,))


def _a_cols_from_head_major(w):
    lead = w.shape[:-1]
    w = w.reshape(lead + (A_HEADS, A_HEAD_COLS))
    parts = [w[..., :A_DK], w[..., A_DK:2 * A_DK], w[..., 2 * A_DK:2 * A_DK + A_DV], w[..., 2 * A_DK + A_DV:]]
    return jnp.concatenate([p.reshape(lead + (-1,)) for p in parts], axis=-1)


def _conv_cols_to_head_major(w):
    lead = w.shape[:-1]
    q = w[..., :A_QK].reshape(lead + (A_HEADS, A_DK))
    k = w[..., A_QK:2 * A_QK].reshape(lead + (A_HEADS, A_DK))
    v = w[..., 2 * A_QK:].reshape(lead + (A_HEADS, A_DV))
    return jnp.concatenate([q, k, v], axis=-1).reshape(lead + (A_HEADS * A_CONV_COLS,))


def _conv_cols_from_head_major(w):
    lead = w.shape[:-1]
    w = w.reshape(lead + (A_HEADS, A_CONV_COLS))
    parts = [w[..., :A_DK], w[..., A_DK:2 * A_DK], w[..., 2 * A_DK:]]
    return jnp.concatenate([p.reshape(lead + (-1,)) for p in parts], axis=-1)


def _to_stream(a, bn, d):
    rest = a.shape[1:]
    s = a.shape[0] // bn
    a = a.reshape((bn, s // d, d) + rest)
    a = jnp.swapaxes(a, 1, 2)
    return a.reshape((bn * d, s // d) + rest)


def _from_stream(a, bn, d):
    rest = a.shape[2:]
    ln = a.shape[1]
    a = a.reshape((bn, d, ln) + rest)
    a = jnp.swapaxes(a, 1, 2)
    return a.reshape((bn * ln * d,) + rest)


B_SUB = 512
B_SHARD_BLOCKS = (3 * B_GROUPS * B_W + B_W) // N_CHIPS // B_SUB


def _b_block(gi, jj):
    nb = (B_GROUPS * (jj // 2) + gi) * 2 + jj % 2
    return nb // B_SHARD_BLOCKS, nb % B_SHARD_BLOCKS


def _shard_major(g, ncols):
    r = g.shape[0]
    return jnp.swapaxes(g.reshape(r, N_CHIPS, ncols), 0, 1)


def _pack_rows(items):
    rows, offs = [], []
    at = 0
    for a in items:
        flat = a.reshape(-1).astype(F32)
        nr = -(-flat.shape[0] // 1024) * 8
        flat = jnp.pad(flat, (0, nr * 128 - flat.shape[0]))
        rows.append(flat.reshape(nr, 128))
        offs.append((at, nr, a.shape))
        at += nr
    return jnp.concatenate(rows, axis=0), offs


def _unpack_rows(packed, offs):
    out = []
    for at, nr, shape in offs:
        size = int(np.prod(shape)) if len(shape) else 1
        out.append(packed[at:at + nr].reshape(-1)[:size].reshape(shape))
    return out


def _local_step(x, positions, loss_target, norm_g, wa_in, conv_w, a_log, a_dt_bias, a_norm_g,
                b_q_norm_g, b_k_norm_g, start_token, late_weights, b_grads_ready, a_grads_ready):
    bn, s, d = x.shape
    t = bn * s
    n_chunks = s // A_CHUNK
    wa_main = _a_cols_to_head_major(wa_in[:, :A_MAIN])
    wa_tail = jnp.pad(wa_in[:, A_MAIN:], ((0, 0), (0, 128 - 2 * A_HEADS)))
    cw_hm = _conv_cols_to_head_major(conv_w)

    x0 = x.reshape(t, d)
    h0 = _rms_fwd(x0, norm_g[0:1] + start_token, "rms0_fwd")
    proj_a = _matmul(h0, wa_main, "nn", F32, "a_in_main")
    tail_a = _matmul(h0, wa_tail, "nn", F32, "a_in_tail")
    tail_t = jnp.swapaxes(tail_a[:, :2 * A_HEADS].reshape(bn, s, 2 * A_HEADS), 1, 2)
    tail_t = tail_t.reshape(bn, 2 * A_HEADS, n_chunks, A_CHUNK)
    beta, gc = _gdn_prep(tail_t, a_log[0], a_dt_bias[0])
    proj_a3 = proj_a.reshape(bn, s, A_MAIN)
    og_a, oraw_a, states, t_mats, conv_y = _gdn_fwd(proj_a3, cw_hm, beta, gc, a_norm_g)
    wa_out, wb_in, wb_out = late_weights(og_a)
    b_cols = [4 * B_W] + [3 * B_W] * (B_GROUPS - 1)
    x1 = _matmul(og_a.reshape(t, A_VW), wa_out, "nn", F32, "a_out", res=x0, tk=2048)

    h1 = _rms_fwd(x1, norm_g[1:2], "rms1_fwd")
    inv_freq = ROPE_THETA ** (-jnp.arange(0, ROPE_DIMS, 2, dtype=F32) / ROPE_DIMS)
    freq_row = jnp.concatenate([inv_freq, inv_freq, jnp.zeros((128 - ROPE_DIMS,), F32)]).reshape(1, 128)
    posf = jnp.broadcast_to(positions.astype(F32).reshape(t, 1), (t, 128))
    tabs = _rope_tables(posf, freq_row)
    h1_s, tabs_s, proj_b, qkv_b, o_b, lse_b = [], [], [], [], [], []
    for gi, dil in enumerate(B_DIL):
        hs = h1 if dil == 1 else _to_stream(h1, bn, dil).reshape(t, d)
        ts = tabs if dil == 1 else [_to_stream(tb, bn, dil).reshape(t, 128) for tb in tabs]
        pj = _matmul(hs, wb_in, "nn", F32, f"b_in_g{gi}", tm=2048, tn=B_SUB, n=b_cols[gi], b_spec=pl.BlockSpec(
            (None, d, B_SUB), lambda i, j, kk, gi=gi: (_b_block(gi, j)[0], kk, _b_block(gi, j)[1])))
        qkv = _qk_prep(pj, *ts, b_q_norm_g[0, gi:gi + 1], b_k_norm_g[0, gi:gi + 1], f"qk_prep_g{gi}")
        o_s, lse_s = _attn_fwd(qkv.reshape(bn * dil, s // dil, 3 * B_W), f"attn_fwd_g{gi}")
        h1_s.append(hs), tabs_s.append(ts), proj_b.append(pj), qkv_b.append(qkv)
        o_b.append(o_s.reshape(t, B_W) if dil == 1 else _from_stream(o_s, bn, dil))
        lse_b.append(lse_s.reshape(t, B_HEADS) if dil == 1 else _from_stream(lse_s, bn, dil))
    og_b = _merge_fwd(o_b, lse_b, proj_b[0])
    x2 = _matmul(og_b, wb_out, "nn", F32, "b_out", res=x1)

    d_x2, loss_parts = _loss_grad(x2, loss_target.reshape(t, d))
    loss_local = jnp.sum(loss_parts)

    d_x2b = d_x2.astype(BF16)
    g_wb_out = _matmul(og_b.T, d_x2b, "nn", F32, "b_out_dw")
    d_og_b = _matmul(d_x2b, wb_out, "nt", F32, "b_out_dx")
    d_o, lse_joint, delta, d_z = _merge_bwd(o_b, lse_b, proj_b[0], d_og_b)
    d_h1, g_qn, g_kn = [], [], []
    g_wb_in = lax.empty(wb_in.shape, F32)
    for gi, dil in enumerate(B_DIL):
        if dil == 1:
            do_s, lj_s, dl_s = d_o, lse_joint, delta
        else:
            do_s, lj_s, dl_s = (_to_stream(a, bn, dil).reshape(t, -1) for a in (d_o, lse_joint, delta))
        ns, ln = bn * dil, s // dil
        dq, dk, dv = _attn_bwd(qkv_b[gi].reshape(ns, ln, 3 * B_W), do_s.reshape(ns, ln, B_W),
                               lj_s.reshape(ns, ln, B_HEADS), dl_s.reshape(ns, ln, B_HEADS), f"attn_bwd_g{gi}")
        d_pj, d_gain = _qk_prep_bwd(proj_b[gi], *tabs_s[gi], b_q_norm_g[0, gi:gi + 1], b_k_norm_g[0, gi:gi + 1],
                                    dq.reshape(t, B_W), dk.reshape(t, B_W), dv.reshape(t, B_W),
                                    d_z if gi == 0 else None, f"qk_prep_bwd_g{gi}")
        g_wb_in = _matmul(h1_s[gi].T, d_pj, "nn", F32, f"b_in_dw_g{gi}", tn=B_SUB, tk=2048, into=(g_wb_in, pl.BlockSpec(
            (None, d, B_SUB), lambda i, j, kk, gi=gi: (_b_block(gi, j)[0], i, _b_block(gi, j)[1]))))
        dh = _matmul(d_pj, wb_in, "nt", F32, f"b_in_dx_g{gi}", tm=2048, tk=B_SUB, n=d, b_spec=pl.BlockSpec(
            (None, d, B_SUB), lambda i, j, kk, gi=gi: (_b_block(gi, kk)[0], j, _b_block(gi, kk)[1])))
        d_h1.append(dh if dil == 1 else _from_stream(dh.reshape(ns, ln, d), bn, dil))
        g_qn.append(d_gain[0]), g_kn.append(d_gain[1])
    d_x1, g_norm1 = _rms_bwd(x1, norm_g[1:2], d_h1, d_x2, "rms1_bwd")

    d_x1b = d_x1.astype(BF16)
    g_wa_out = _matmul(og_a.reshape(t, A_VW).T, d_x1b, "nn", F32, "a_out_dw")
    b_token = b_grads_ready(g_wb_in, g_wb_out, g_wa_out)
    d_og_a = _matmul(d_x1b, wa_out, "nt", F32, "a_out_dx")
    d_pa, d_gc, d_beta, d_cw, d_ng = _gdn_bwd(proj_a3, cw_hm, beta, gc, a_norm_g + b_token, oraw_a, states,
                                              t_mats, conv_y, d_og_a.reshape(bn, s, A_VW))
    d_tail_t, d_alog, d_dtb = _gdn_prep_bwd(tail_t, a_log[0], a_dt_bias[0], d_gc, d_beta)
    d_tail = jnp.swapaxes(d_tail_t.reshape(bn, 2 * A_HEADS, s), 1, 2).reshape(t, 2 * A_HEADS)
    d_tail = jnp.pad(d_tail, ((0, 0), (0, 128 - 2 * A_HEADS))).astype(BF16)
    d_pa = d_pa.reshape(t, A_MAIN)
    h0_t = h0.T
    g_wa_main = _matmul(h0_t, d_pa, "nn", F32, "a_in_dw_main")
    g_wa_tail = _matmul(h0_t, d_tail, "nn", F32, "a_in_dw_tail")
    g_wa_in = jnp.concatenate([_a_cols_from_head_major(g_wa_main), g_wa_tail[:, :2 * A_HEADS]], axis=1)
    a_token = a_grads_ready(g_wa_in)
    d_h0 = _matmul(d_pa, wa_main, "nt", F32, "a_in_dx_main")
    d_h0t = _matmul(d_tail + a_token.astype(BF16), wa_tail, "nt", F32, "a_in_dx_tail")
    d_x0, g_norm0 = _rms_bwd(x0, norm_g[0:1], [d_h0, d_h0t], d_x1, "rms0_bwd")

    gfull = {
        "norm_g": jnp.concatenate([g_norm0, g_norm1], axis=0), "a_w_in": g_wa_in,
        "a_conv_w": _conv_cols_from_head_major(jnp.sum(d_cw, axis=0)),
        "a_log": jnp.sum(d_alog[:, :, 0], axis=0), "a_dt_bias": jnp.sum(d_dtb[:, :, 0], axis=0),
        "a_norm_g": jnp.sum(d_ng[:, :, 0, :], axis=(0, 1)), "a_w_out": g_wa_out, "b_w_in": g_wb_in,
        "b_q_norm_g": jnp.stack(g_qn), "b_k_norm_g": jnp.stack(g_kn), "b_w_out": g_wb_out}
    return loss_local, d_x0.reshape(bn, s, d), gfull


def kernel(x, positions, norm_g, a_w_in, a_conv_w, a_log, a_dt_bias, a_norm_g, a_w_out, b_w_in, b_q_norm_g, b_k_norm_g, b_w_out, loss_target, m_norm_g, m_a_w_in, m_a_conv_w, m_a_log, m_a_dt_bias, m_a_norm_g, m_a_w_out, m_b_w_in, m_b_q_norm_g, m_b_k_norm_g, m_b_w_out, v_norm_g, v_a_w_in, v_a_conv_w, v_a_log, v_a_dt_bias, v_a_norm_g, v_a_w_out, v_b_w_in, v_b_q_norm_g, v_b_k_norm_g, v_b_w_out):
    d = x.shape[2]
    my_c = lax.axis_index("c")
    my_chip = 2 * lax.axis_index("x") + lax.axis_index("y")

    half_index = jnp.reshape(my_c, (1,)).astype(jnp.int32)
    chip_index = jnp.reshape(my_chip, (1,)).astype(jnp.int32)
    (ga_in,), g_conv = _weight_allgather([a_w_in[0].astype(BF16)], a_conv_w[0])
    wa_in = jnp.concatenate(ga_in, axis=1)
    conv_w = jnp.concatenate(g_conv, axis=1)

    late_shards = [a_w_out[0].astype(BF16), b_w_in[0].astype(BF16), b_w_out[0].astype(BF16)]
    late_lands = [lax.dynamic_update_slice(lax.empty((N_CHIPS,) + s.shape, BF16), s[None], (my_chip, 0, 0))
                  for s in late_shards]
    gather = _gather_plan(len(late_shards))
    ag_send, ag_recv, ag_srcs, ag_lands, ag_token = _split_copy_start(
        "late_weights_start", gather, late_shards, late_lands, conv_w)

    def late_weights(after):
        _, (ga_out, gb_in, gb_out) = _split_copy_wait(
            "late_weights_wait", gather, ag_send, ag_recv, ag_srcs, ag_lands, after)
        return ga_out.reshape(A_VW, d), gb_in, gb_out.reshape(B_W, d)

    def reduce_to_chip_sums(mats, tag):
        recv_sib = _sibling_swap_halves([g.astype(BF16) for g in mats], f"grad_{tag}_sibling_swap")
        return [_pair_sum(g, r, half_index, f"grad_{tag}_pair_sum_{i}") for i, (g, r) in enumerate(zip(mats, recv_sib))]

    pending = {}

    def start_exchange(tag, mats):
        sums = reduce_to_chip_sums(mats, tag)
        lands = [lax.empty((N_CHIPS - 1,) + s.shape[1:], BF16) for s in sums]
        plan = _exchange_plan(len(mats))
        pending[tag] = (plan,) + tuple(_split_copy_start(f"grad_{tag}_exchange_start", plan, sums, lands, chip_index))
        return pending[tag][5][0, 0]

    def finish_exchange(tag, after):
        plan, send, recv, srcs, lands, _ = pending[tag]
        return _split_copy_wait(f"grad_{tag}_exchange_wait", plan, send, recv, srcs, lands, after)

    def b_grads_ready(g_wb_in, g_wb_out, g_wa_out):
        return start_exchange("b", [g_wb_in, g_wb_out.reshape(N_CHIPS, -1, d), g_wa_out.reshape(N_CHIPS, -1, d)])

    def a_grads_ready(g_wa_in):
        return start_exchange("a", [_shard_major(g_wa_in, a_w_in.shape[2])])

    loss_local, d_x0, gfull = _local_step(x, positions, loss_target, norm_g, wa_in, conv_w, a_log, a_dt_bias,
                                          a_norm_g, b_q_norm_g, b_k_norm_g, ag_token[0, 0], late_weights,
                                          b_grads_ready, a_grads_ready)

    small = [gfull["norm_g"], gfull["a_conv_w"], gfull["a_log"], gfull["a_dt_bias"], gfull["a_norm_g"],
             gfull["b_q_norm_g"], gfull["b_k_norm_g"], loss_local]
    packed, offs = _pack_rows(small)
    reduced = _small_allreduce(packed)
    g_norm, g_conv_all, g_alog, g_dtb, g_ang, g_q, g_k, loss = _unpack_rows(reduced, offs)
    g_conv_mine = lax.dynamic_slice_in_dim(g_conv_all, my_chip * a_conv_w.shape[2], a_conv_w.shape[2], axis=1)

    b_sums, b_received = finish_exchange("b", d_x0)
    a_sums, a_received = finish_exchange("a", reduced)
    chip_sums = [a_sums[0], b_sums[2], b_sums[0], b_sums[1]]
    received = [a_received[0], b_received[2], b_received[0], b_received[1]]
    halves = [_chip_sum(s, r, chip_index, f"grad_chip_sum_{i}") for i, (s, r) in enumerate(zip(chip_sums, received))]
    theirs = _sibling_swap_whole(halves)
    big = ("a_w_in", "a_w_out", "b_w_in", "b_w_out")
    big_halves = dict(zip(big, zip(halves, theirs)))

    grads = {
        "norm_g": g_norm, "a_conv_w": g_conv_mine[None], "a_log": g_alog[None], "a_dt_bias": g_dtb[None],
        "a_norm_g": g_ang[None], "b_q_norm_g": g_q[None], "b_k_norm_g": g_k[None]}
    weights = {"norm_g": norm_g, "a_w_in": a_w_in, "a_conv_w": a_conv_w, "a_log": a_log, "a_dt_bias": a_dt_bias,
               "a_norm_g": a_norm_g, "a_w_out": a_w_out, "b_w_in": b_w_in, "b_q_norm_g": b_q_norm_g,
               "b_k_norm_g": b_k_norm_g, "b_w_out": b_w_out}
    m_in = {"norm_g": m_norm_g, "a_w_in": m_a_w_in, "a_conv_w": m_a_conv_w, "a_log": m_a_log,
            "a_dt_bias": m_a_dt_bias, "a_norm_g": m_a_norm_g, "a_w_out": m_a_w_out, "b_w_in": m_b_w_in,
            "b_q_norm_g": m_b_q_norm_g, "b_k_norm_g": m_b_k_norm_g, "b_w_out": m_b_w_out}
    v_in = {"norm_g": v_norm_g, "a_w_in": v_a_w_in, "a_conv_w": v_a_conv_w, "a_log": v_a_log,
            "a_dt_bias": v_a_dt_bias, "a_norm_g": v_a_norm_g, "a_w_out": v_a_w_out, "b_w_in": v_b_w_in,
            "b_q_norm_g": v_b_q_norm_g, "b_k_norm_g": v_b_k_norm_g, "b_w_out": v_b_w_out}
    names = list(weights)

    delta_w, new_m, new_v = {}, {}, {}
    for nm in big:
        mine, other = big_halves[nm]
        grads[nm], delta_w[nm], new_m[nm], new_v[nm] = _adamw_shard(
            weights[nm], mine, other, m_in[nm], v_in[nm], half_index, f"adamw_{nm}")
    small_names = [nm for nm in names if nm not in big]
    packs = [_pack_rows([src[nm] for nm in small_names]) for src in (weights, grads, m_in, v_in)]
    offs = packs[0][1]
    dl, m2, v2 = _adamw(packs[0][0], packs[1][0], packs[2][0], packs[3][0], "adamw_small")
    for nm, a, b, c2 in zip(small_names, _unpack_rows(dl, offs), _unpack_rows(m2, offs), _unpack_rows(v2, offs)):
        delta_w[nm], new_m[nm], new_v[nm] = a, b, c2

    return (loss, d_x0, *[grads[nm] for nm in names], *[delta_w[nm] for nm in names],
            *[new_m[nm] for nm in names], *[new_v[nm] for nm in names])
```

```python
import jax
import jax.numpy as jnp
import numpy as np
from jax import lax
from jax.experimental import pallas as pl
from jax.experimental.pallas import tpu as pltpu

F32 = jnp.float32
BF16 = jnp.bfloat16
MESH = pl.DeviceIdType.MESH

EPS = 1e-6
D_MODEL = 1024
A_HEADS = 8
A_DK = 128
A_DV = 256
A_QK = A_HEADS * A_DK
A_VW = A_HEADS * A_DV
A_MAIN = 2 * A_QK + 2 * A_VW
A_HEAD_COLS = 2 * A_DK + 2 * A_DV
A_CONV_COLS = 2 * A_DK + A_DV
A_CHUNK = 64
A_CONV = 4
B_GROUPS = 3
B_HEADS = 8
B_DH = 128
B_W = B_HEADS * B_DH
B_DIL = (1, 4, 16)
B_BLK = 128
ROPE_THETA = 500000.0
ROPE_DIMS = B_DH // 4
ADAM_LR, ADAM_B1, ADAM_B2, ADAM_EPS, ADAM_WD, ADAM_STEP = 0.001, 0.9, 0.999, 1e-08, 0.01, 10
N_CHIPS = 4
VMEM_BIG = 56 * 1024 * 1024


def _params(sem=None, vmem=None):
    return pltpu.CompilerParams(dimension_semantics=sem, vmem_limit_bytes=vmem)


def _dot(a, b, ca, cb):
    return lax.dot_general(a.astype(BF16), b.astype(BF16), (((ca,), (cb,)), ((), ())),
                           preferred_element_type=F32)


def _split3(a):
    hi = a.astype(BF16)
    r = a - hi.astype(F32)
    mid = r.astype(BF16)
    lo = (r - mid.astype(F32)).astype(BF16)
    return hi, mid, lo


def _sigmoid(y):
    return 1.0 / (1.0 + jnp.exp(-y))


def _silu(y):
    return y * _sigmoid(y)


def _dsilu(y):
    s = _sigmoid(y)
    return s * (1.0 + y * (1.0 - s))


def _matmul(a, b, mode, out_dtype, name, res=None, tm=1024, tn=1024, tk=1024, n=None, b_spec=None, into=None):
    m, k = a.shape
    if n is None:
        n = b.shape[1] if mode == "nn" else b.shape[0]
    tm, tn, tk = min(tm, m), min(tn, n), min(tk, k)
    assert m % tm == 0 and n % tn == 0 and k % tk == 0, (name, a.shape, b.shape)
    nk = k // tk
    dims = {"nn": ((1,), (0,)), "nt": ((1,), (1,))}[mode]

    def body(*refs):
        a_ref, b_ref = refs[0], refs[1]
        r_ref = refs[2] if res is not None else None
        o_ref = refs[2 + (res is not None) + (into is not None)]
        prod = lax.dot_general(a_ref[...], b_ref[...], (dims, ((), ())), preferred_element_type=F32)

        def finish(r):
            if res is not None:
                r = r + r_ref[...]
            o_ref[...] = r.astype(out_dtype)

        if nk == 1:
            finish(prod)
            return
        acc = refs[-1]
        kk = pl.program_id(2)

        @pl.when(kk == 0)
        def _():
            acc[...] = prod

        @pl.when((kk > 0) & (kk < nk - 1))
        def _():
            acc[...] += prod

        @pl.when(kk == nk - 1)
        def _():
            finish(acc[...] + prod)

    a_spec = pl.BlockSpec((tm, tk), lambda i, j, kk: (i, kk))
    if b_spec is None and mode == "nt":
        b_spec = pl.BlockSpec((tn, tk), lambda i, j, kk: (j, kk))
    elif b_spec is None:
        b_spec = pl.BlockSpec((tk, tn), lambda i, j, kk: (kk, j))
    in_specs = [a_spec, b_spec]
    args = [a, b]
    if res is not None:
        in_specs.append(pl.BlockSpec((tm, tn), lambda i, j, kk: (i, j)))
        args.append(res)
    out_spec = pl.BlockSpec((tm, tn), lambda i, j, kk: (i, j))
    out_shape = jax.ShapeDtypeStruct((m, n), out_dtype)
    aliases = {}
    if into is not None:
        assert res is None
        buf, out_spec = into
        out_shape = jax.ShapeDtypeStruct(buf.shape, buf.dtype)
        in_specs.append(ANY)
        args.append(buf)
        aliases = {2: 0}
    return pl.pallas_call(
        body, name=name, grid=(m // tm, n // tn, nk),
        in_specs=in_specs, out_specs=out_spec, out_shape=out_shape, input_output_aliases=aliases,
        scratch_shapes=[pltpu.VMEM((tm, tn), F32)] if nk > 1 else [],
        compiler_params=_params(("parallel", "parallel", "arbitrary"), 48 * 1024 * 1024),
    )(*args)


def _rms_fwd(x, g, name, tm=256):
    t, d = x.shape

    def body(x_ref, g_ref, h_ref):
        xv = x_ref[...]
        r = lax.rsqrt(jnp.mean(xv * xv, axis=-1, keepdims=True) + EPS)
        h_ref[...] = (xv * r * g_ref[...]).astype(BF16)

    return pl.pallas_call(
        body, name=name, grid=(t // tm,),
        in_specs=[pl.BlockSpec((tm, d), lambda i: (i, 0)), pl.BlockSpec((1, d), lambda i: (0, 0))],
        out_specs=pl.BlockSpec((tm, d), lambda i: (i, 0)),
        out_shape=jax.ShapeDtypeStruct((t, d), BF16),
        compiler_params=_params(("parallel",)),
    )(x, g)


def _rms_bwd(x, g, dhs, dres, name, tm=256):
    t, d = x.shape
    n_dh = len(dhs)

    def body(*refs):
        x_ref, g_ref = refs[0], refs[1]
        dh_refs = refs[2:2 + n_dh]
        dres_ref, dx_ref, dg_ref = refs[2 + n_dh:]
        i = pl.program_id(0)

        @pl.when(i == 0)
        def _():
            dg_ref[...] = jnp.zeros_like(dg_ref)

        xv = x_ref[...]
        r = lax.rsqrt(jnp.mean(xv * xv, axis=-1, keepdims=True) + EPS)
        xh = xv * r
        dh = dh_refs[0][...]
        for ref in dh_refs[1:]:
            dh = dh + ref[...]
        dg_ref[0:1, :] += jnp.sum(dh * xh, axis=0, keepdims=True)
        dxh = dh * g_ref[...]
        dx = r * (dxh - xh * jnp.mean(dxh * xh, axis=-1, keepdims=True))
        dx_ref[...] = dx + dres_ref[...]

    row = pl.BlockSpec((tm, d), lambda i: (i, 0))
    dx, dg = pl.pallas_call(
        body, name=name, grid=(t // tm,),
        in_specs=[row, pl.BlockSpec((1, d), lambda i: (0, 0))] + [row] * n_dh + [row],
        out_specs=[row, pl.BlockSpec((8, d), lambda i: (0, 0))],
        out_shape=[jax.ShapeDtypeStruct((t, d), F32), jax.ShapeDtypeStruct((8, d), F32)],
        compiler_params=_params(("arbitrary",)),
    )(x, g, *dhs, dres)
    return dx, dg[0:1]


def _loss_grad(y, target, name="loss_grad", tm=256):
    t, d = y.shape
    nb = t // tm

    def body(y_ref, t_ref, dy_ref, part_ref):
        e = y_ref[...] - t_ref[...]
        dy_ref[...] = e * (1.0 / d)
        s = jnp.sum(jnp.sum(e * e, axis=1, keepdims=True), axis=0, keepdims=True) * (0.5 / d)
        part_ref[...] = jnp.broadcast_to(s, (8, 128))

    row = pl.BlockSpec((tm, d), lambda i: (i, 0))
    dy, part = pl.pallas_call(
        body, name=name, grid=(nb,), in_specs=[row, row],
        out_specs=[row, pl.BlockSpec((None, 8, 128), lambda i: (i, 0, 0))],
        out_shape=[jax.ShapeDtypeStruct((t, d), F32), jax.ShapeDtypeStruct((nb, 8, 128), F32)],
        compiler_params=_params(("parallel",)),
    )(y, target)
    return dy, part[:, 0, 0]


def _softplus(x):
    t = jnp.exp(-jnp.abs(x))
    return jnp.maximum(x, 0.0) + jnp.where(t < 1e-3, t * (1.0 - 0.5 * t), jnp.log(1.0 + t))


def _tri(rows_le_cols):
    r = lax.broadcasted_iota(jnp.int32, (A_CHUNK, A_CHUNK), 0)
    c = lax.broadcasted_iota(jnp.int32, (A_CHUNK, A_CHUNK), 1)
    return jnp.where((r <= c) if rows_le_cols else (r >= c), 1.0, 0.0).astype(BF16)


def _dot_exact_rhs(a, ones_bf16):
    dn = (((1,), (0,)), ((), ()))
    hi, mid, lo = _split3(a)
    out = lax.dot_general(hi, ones_bf16, dn, preferred_element_type=F32)
    out = out + lax.dot_general(mid, ones_bf16, dn, preferred_element_type=F32)
    return out + lax.dot_general(lo, ones_bf16, dn, preferred_element_type=F32)


def _gdn_prep(tail_t, a_log, dt_bias):
    bn, _, n, c = tail_t.shape

    def body(t_ref, alog_ref, dtb_ref, beta_ref, gc_ref):
        upper = _tri(True)
        for h in range(A_HEADS):
            beta_ref[h] = _sigmoid(t_ref[h])
            ea = jnp.exp(jnp.full((n, c), alog_ref[h], F32))
            g = -ea * _softplus(t_ref[A_HEADS + h] + dtb_ref[h])
            gc_ref[h] = _dot_exact_rhs(g, upper)

    smem = pl.BlockSpec(memory_space=pltpu.SMEM)
    blk = pl.BlockSpec((None, A_HEADS, n, c), lambda b: (b, 0, 0, 0))
    return pl.pallas_call(
        body, name="gdn_prep", grid=(bn,),
        in_specs=[pl.BlockSpec((None, 2 * A_HEADS, n, c), lambda b: (b, 0, 0, 0)), smem, smem],
        out_specs=[blk, blk],
        out_shape=[jax.ShapeDtypeStruct((bn, A_HEADS, n, c), F32)] * 2,
        compiler_params=_params(("parallel",)),
    )(tail_t, a_log, dt_bias)


def _gdn_prep_bwd(tail_t, a_log, dt_bias, d_gc, d_beta):
    bn, _, n, c = tail_t.shape

    def body(t_ref, alog_ref, dtb_ref, dgc_ref, dbeta_ref, dt_ref, dal_ref, ddt_ref):
        lower = _tri(False)
        for h in range(A_HEADS):
            beta = _sigmoid(t_ref[h])
            dt_ref[h] = dbeta_ref[h] * beta * (1.0 - beta)
            dg = _dot_exact_rhs(dgc_ref[h], lower)
            ea = jnp.exp(jnp.full((n, c), alog_ref[h], F32))
            xa = t_ref[A_HEADS + h] + dtb_ref[h]
            g = -ea * _softplus(xa)
            dxa = -ea * dg * _sigmoid(xa)
            dt_ref[A_HEADS + h] = dxa
            s1 = jnp.sum(jnp.sum(g * dg, axis=1, keepdims=True), axis=0, keepdims=True)
            s2 = jnp.sum(jnp.sum(dxa, axis=1, keepdims=True), axis=0, keepdims=True)
            dal_ref[h:h + 1, :] = jnp.broadcast_to(s1, (1, 128))
            ddt_ref[h:h + 1, :] = jnp.broadcast_to(s2, (1, 128))

    smem = pl.BlockSpec(memory_space=pltpu.SMEM)
    blk8 = pl.BlockSpec((None, A_HEADS, n, c), lambda b: (b, 0, 0, 0))
    blk16 = pl.BlockSpec((None, 2 * A_HEADS, n, c), lambda b: (b, 0, 0, 0))
    sm = pl.BlockSpec((None, A_HEADS, 128), lambda b: (b, 0, 0))
    return pl.pallas_call(
        body, name="gdn_prep_bwd", grid=(bn,),
        in_specs=[blk16, smem, smem, blk8, blk8],
        out_specs=[blk16, sm, sm],
        out_shape=[jax.ShapeDtypeStruct((bn, 2 * A_HEADS, n, c), F32),
                   jax.ShapeDtypeStruct((bn, A_HEADS, 128), F32),
                   jax.ShapeDtypeStruct((bn, A_HEADS, 128), F32)],
        compiler_params=_params(("parallel",)),
    )(tail_t, a_log, dt_bias, d_gc, d_beta)


HALO = 8


def _conv_taps(xw, w):
    y = w[A_CONV - 1:A_CONV, :] * xw
    for j in range(1, A_CONV):
        y = y + w[A_CONV - 1 - j:A_CONV - j, :] * pltpu.roll(xw, j, 0)
    return y[HALO:, :]


def _row_to_col(row, eye):
    c = eye.shape[0]
    return jnp.sum(jnp.where(eye, jnp.broadcast_to(row, (c, c)), 0.0), axis=1, keepdims=True)


def _col_to_row(col, eye):
    c = eye.shape[0]
    return jnp.sum(jnp.where(eye, jnp.broadcast_to(col, (c, c)), 0.0), axis=0, keepdims=True)


def _unit_lower_inverse(a, ri, ci):
    eye = jnp.where(ri == ci, 1.0, 0.0)
    a8 = jnp.where((ri >> 3) == (ci >> 3), a, 0.0)
    a2 = _dot(a8, a8, 1, 0)
    yield
    a4 = _dot(a2, a2, 1, 0)
    t = eye - a8
    t = t + _dot(t, a2, 1, 0)
    yield
    t = t + _dot(t, a4, 1, 0)
    yield
    for sh in (3, 4, 5):
        off = jnp.where(((ri >> (sh + 1)) == (ci >> (sh + 1))) & ((ri >> sh) != (ci >> sh)), a, 0.0)
        left = _dot(t, off, 1, 0)
        yield
        t = t - _dot(left, t, 1, 0)
        yield
    return t


def _round_robin(gens):
    live = list(gens)
    while live:
        nxt = []
        for g in live:
            try:
                next(g)
                nxt.append(g)
            except StopIteration:
                pass
        live = nxt


def _gdn_chunk_core(q, k, v, g_row, b_row, t_mat, ri, ci):
    eye = ri == ci
    g_col = _row_to_col(g_row, eye)
    b_col = _row_to_col(b_row, eye)
    causal = ri >= ci
    strict = ri > ci
    dec = jnp.where(causal, jnp.exp(jnp.where(causal, g_col - g_row, 0.0)), 0.0)
    gam = jnp.exp(g_col)
    g_last = g_row[:, A_CHUNK - 1:A_CHUNK]
    gam_last = jnp.exp(g_last)
    e = jnp.exp(g_last - g_col)
    kb = k * b_col
    bv = v * b_col
    kbg = kb * gam
    q16, k16, kb16 = q.astype(BF16), k.astype(BF16), kb.astype(BF16)
    kk = _dot(kb16, k16, 1, 1)
    p = _dot(q16, k16, 1, 1) * dec
    yield
    a_mat = jnp.where(strict, kk * dec, 0.0)
    if t_mat is None:
        t_mat = yield from _unit_lower_inverse(a_mat, ri, ci)
    t16 = t_mat.astype(BF16)
    u = _dot(t16, bv, 1, 0)
    w = _dot(t16, kbg, 1, 0)
    yield
    return dict(eye=eye, g_col=g_col, b_col=b_col, dec=dec, strict=strict, causal=causal, gam=gam,
                gam_last=gam_last, e=e, kb=kb, bv=bv, kbg=kbg, a_mat=a_mat, t_mat=t_mat, u=u, w=w, p=p,
                qg=q * gam, kd=k * e, q16=q16, k16=k16, kb16=kb16, t16=t16)


A_SEQ_BLK = 256
A_BLK_CHUNKS = A_SEQ_BLK // A_CHUNK


def _gdn_halo(proj_hm):
    bn, s, w = proj_hm.shape
    last = proj_hm.reshape(bn, s // A_SEQ_BLK, A_SEQ_BLK, w)[:, :, A_SEQ_BLK - HALO:, :]
    return jnp.concatenate([jnp.zeros((bn, 1, HALO, w), proj_hm.dtype), last[:, :-1]], axis=1)


def _gdn_window(x_ref, halo_ref, ci, first, lo):
    if first:
        return jnp.concatenate([halo_ref[:, lo:lo + A_CONV_COLS], x_ref[0:A_CHUNK, lo:lo + A_CONV_COLS]], axis=0)
    start = pl.multiple_of(ci * A_CHUNK - HALO, HALO)
    return x_ref[pl.ds(start, A_CHUNK + HALO), lo:lo + A_CONV_COLS]


def _gdn_chunk_prep(xw, cw, y=None):
    if y is None:
        y = _conv_taps(xw, cw)
    a = _silu(y)
    aq, ak, v = a[:, 0:A_DK], a[:, A_DK:2 * A_DK], a[:, 2 * A_DK:]
    rq = lax.rsqrt(jnp.sum(aq * aq, axis=1, keepdims=True) + EPS)
    rk = lax.rsqrt(jnp.sum(ak * ak, axis=1, keepdims=True) + EPS)
    return dict(xw=xw, y=y, aq=aq, ak=ak, rq=rq, rk=rk, q=aq * rq * (A_DK ** -0.5), k=ak * rk, v=v)


def _gdn_fwd(proj_hm, cw_hm, beta, gc, norm_g, hp=8):
    bn, s, _ = proj_hm.shape
    n = s // A_CHUNK
    nsb = s // A_SEQ_BLK
    halo = _gdn_halo(proj_hm)

    def body(x_ref, halo_ref, cw_ref, beta_ref, gc_ref, ng_ref, og_ref, oraw_ref, st_ref, t_ref, y_ref, state):
        first_chunk = pl.program_id(2) * A_BLK_CHUNKS
        ri = lax.broadcasted_iota(jnp.int32, (A_CHUNK, A_CHUNK), 0)
        ci_ = lax.broadcasted_iota(jnp.int32, (A_CHUNK, A_CHUNK), 1)
        ng = ng_ref[...]

        @pl.when(pl.program_id(2) == 0)
        def _():
            state[...] = jnp.zeros_like(state)

        def one_head(hh, ci, first, rows):
            lo = hh * A_HEAD_COLS
            cw = cw_ref[:, hh * A_CONV_COLS:(hh + 1) * A_CONV_COLS]
            cin = _gdn_chunk_prep(_gdn_window(x_ref, halo_ref, ci, first, lo), cw)
            y_ref[rows, hh * A_CONV_COLS:(hh + 1) * A_CONV_COLS] = cin["y"]
            seq_chunk = pl.ds(first_chunk + ci, 1)
            core = yield from _gdn_chunk_core(cin["q"], cin["k"], cin["v"], gc_ref[hh, seq_chunk, :],
                                              beta_ref[hh, seq_chunk, :], None, ri, ci_)
            st = state[hh]
            st_ref[hh, ci] = st
            t_ref[hh, ci] = core["t_mat"]
            st16 = st.astype(BF16)
            vn = core["u"] - _dot(core["w"], st16, 1, 0)
            qs = _dot(core["qg"], st16, 1, 0)
            yield
            vn16 = vn.astype(BF16)
            o = qs + _dot(core["p"], vn16, 1, 0)
            state[hh] = st * core["gam_last"] + _dot(core["kd"], vn16, 0, 0)
            yield
            ocols = slice(hh * A_DV, (hh + 1) * A_DV)
            oraw_ref[rows, ocols] = o
            r = lax.rsqrt(jnp.mean(o * o, axis=1, keepdims=True) + EPS)
            z = x_ref[rows, lo + A_CONV_COLS:lo + A_HEAD_COLS]
            og_ref[rows, ocols] = (o * r * ng * _silu(z)).astype(BF16)

        def chunk(ci, first):
            rows = pl.ds(0 if first else pl.multiple_of(ci * A_CHUNK, A_CHUNK), A_CHUNK)
            _round_robin([one_head(hh, ci, first, rows) for hh in range(hp)])

        chunk(0, True)
        lax.fori_loop(1, A_BLK_CHUNKS, lambda i, c: (chunk(i, False), c)[1], 0)

    small = pl.BlockSpec((None, hp, n, A_CHUNK), lambda b, h, j: (b, h, 0, 0))
    return pl.pallas_call(
        body, name="gdn_fwd", grid=(bn, A_HEADS // hp, nsb),
        in_specs=[pl.BlockSpec((None, A_SEQ_BLK, hp * A_HEAD_COLS), lambda b, h, j: (b, j, h)),
                  pl.BlockSpec((None, None, HALO, hp * A_HEAD_COLS), lambda b, h, j: (b, j, 0, h)),
                  pl.BlockSpec((A_CONV, hp * A_CONV_COLS), lambda b, h, j: (0, h)),
                  small, small,
                  pl.BlockSpec((1, A_DV), lambda b, h, j: (0, 0))],
        out_specs=[pl.BlockSpec((None, A_SEQ_BLK, hp * A_DV), lambda b, h, j: (b, j, h)),
                   pl.BlockSpec((None, A_SEQ_BLK, hp * A_DV), lambda b, h, j: (b, j, h)),
                   pl.BlockSpec((None, hp, A_BLK_CHUNKS, A_DK, A_DV), lambda b, h, j: (b, h, j, 0, 0)),
                   pl.BlockSpec((None, hp, A_BLK_CHUNKS, A_CHUNK, A_CHUNK), lambda b, h, j: (b, h, j, 0, 0)),
                   pl.BlockSpec((None, A_SEQ_BLK, hp * A_CONV_COLS), lambda b, h, j: (b, j, h))],
        out_shape=[jax.ShapeDtypeStruct((bn, s, A_VW), BF16),
                   jax.ShapeDtypeStruct((bn, s, A_VW), F32),
                   jax.ShapeDtypeStruct((bn, A_HEADS, n, A_DK, A_DV), F32),
                   jax.ShapeDtypeStruct((bn, A_HEADS, n, A_CHUNK, A_CHUNK), F32),
                   jax.ShapeDtypeStruct((bn, s, A_HEADS * A_CONV_COLS), F32)],
        scratch_shapes=[pltpu.VMEM((hp, A_DK, A_DV), F32)],
        compiler_params=_params(("parallel", "parallel", "arbitrary"), VMEM_BIG),
    )(proj_hm, halo, cw_hm, beta, gc, norm_g)


def _gdn_bwd(proj_hm, cw_hm, beta, gc, norm_g, oraw, states, t_mats, conv_y, dog, hp=4):
    bn, s, _ = proj_hm.shape
    n = s // A_CHUNK
    nsb = s // A_SEQ_BLK
    halo = _gdn_halo(proj_hm)

    def body(x_ref, halo_ref, cw_ref, beta_ref, gc_ref, ng_ref, oraw_ref, st_ref, t_ref, y_ref, dog_ref,
             dx_ref, dgc_ref, dbeta_ref, dcw_ref, dng_ref, dstate, dy_next, shifted):
        first_chunk = (nsb - 1 - pl.program_id(2)) * A_BLK_CHUNKS
        ri = lax.broadcasted_iota(jnp.int32, (A_CHUNK, A_CHUNK), 0)
        ci_ = lax.broadcasted_iota(jnp.int32, (A_CHUNK, A_CHUNK), 1)
        lane = lax.broadcasted_iota(jnp.int32, (1, A_CHUNK), 1)
        ng = ng_ref[...]

        @pl.when(pl.program_id(2) == 0)
        def _():
            dstate[...] = jnp.zeros_like(dstate)
            dy_next[...] = jnp.zeros_like(dy_next)
            dcw_ref[...] = jnp.zeros_like(dcw_ref)
            dng_ref[...] = jnp.zeros_like(dng_ref)

        def one_head(hh, ci, first, rows):
            lo = hh * A_HEAD_COLS
            ccols = slice(hh * A_CONV_COLS, (hh + 1) * A_CONV_COLS)
            ocols = slice(hh * A_DV, (hh + 1) * A_DV)
            cw = cw_ref[:, ccols]
            cin = _gdn_chunk_prep(_gdn_window(x_ref, halo_ref, ci, first, lo), cw, y_ref[rows, ccols])
            q, k, v = cin["q"], cin["k"], cin["v"]
            seq_chunk = pl.ds(first_chunk + ci, 1)
            cr = yield from _gdn_chunk_core(q, k, v, gc_ref[hh, seq_chunk, :], beta_ref[hh, seq_chunk, :],
                                            t_ref[hh, ci], ri, ci_)
            eye, dec, gam, e = cr["eye"], cr["dec"], cr["gam"], cr["e"]
            b_col, t_mat, u, w, p = cr["b_col"], cr["t_mat"], cr["u"], cr["w"], cr["p"]
            st = st_ref[hh, ci]
            ds_out = dstate[hh]

            o = oraw_ref[rows, ocols]
            z = x_ref[rows, lo + A_CONV_COLS:lo + A_HEAD_COLS]
            d_og = dog_ref[rows, ocols]
            r = lax.rsqrt(jnp.mean(o * o, axis=1, keepdims=True) + EPS)
            oh = o * r
            d_on = d_og * _silu(z)
            dz = d_og * oh * ng * _dsilu(z)
            dng_ref[hh, 0:1, :] += jnp.sum(d_on * oh, axis=0, keepdims=True)
            d_oh = d_on * ng
            d_o = r * (d_oh - oh * jnp.mean(d_oh * oh, axis=1, keepdims=True))

            st16, ds16, do16, w16 = st.astype(BF16), ds_out.astype(BF16), d_o.astype(BF16), w.astype(BF16)
            q16, k16, t16 = cr["q16"], cr["k16"], cr["t16"]
            vn = u - _dot(w16, st16, 1, 0)
            d_vn = _dot(p, do16, 0, 0) + _dot(cr["kd"], ds16, 1, 0)
            d_qg = _dot(do16, st16, 1, 1)
            qgdo = _dot(cr["qg"], do16, 0, 0)
            yield
            vn16, dvn16 = vn.astype(BF16), d_vn.astype(BF16)
            d_p = jnp.where(cr["causal"], _dot(do16, vn16, 1, 1), 0.0)
            d_kd = _dot(vn16, ds16, 1, 1)
            d_gam_last = jnp.sum(jnp.sum(st * ds_out, axis=1, keepdims=True), axis=0, keepdims=True)
            d_w = -_dot(dvn16, st16, 1, 1)
            dstate[hh] = qgdo + ds_out * cr["gam_last"] - _dot(w16, dvn16, 0, 0)
            d_bv = _dot(t16, dvn16, 0, 0)
            yield
            d_kbg = _dot(t16, d_w, 0, 0)
            n_p = (d_p * dec).astype(BF16)
            d_q = _dot(n_p, k16, 1, 0) + d_qg * gam
            npq = _dot(n_p, q16, 0, 0)
            yield
            d_a = jnp.where(cr["strict"], -(_dot(d_bv, u, 1, 1) + _dot(d_kbg, w16, 1, 1)), 0.0)
            yield
            m_a = (d_a * dec).astype(BF16)
            d_kb = _dot(m_a, k16, 1, 0) + d_kbg * gam
            d_k = (_dot(m_a, cr["kb16"], 0, 0) + npq + d_kd * e + d_kb * b_col)
            yield
            d_v = d_bv * b_col
            d_beta_col = (jnp.sum(d_bv * v, axis=1, keepdims=True)
                          + jnp.sum(d_kb * k, axis=1, keepdims=True))
            gterm = d_a * cr["a_mat"] + d_p * p
            d_e = jnp.sum(d_kd * k, axis=1, keepdims=True) * e
            d_g_col = (jnp.sum(gterm, axis=1, keepdims=True)
                       + (jnp.sum(d_qg * q, axis=1, keepdims=True)
                          + jnp.sum(d_kbg * cr["kb"], axis=1, keepdims=True)) * gam
                       - d_e)
            d_g_last = jnp.sum(d_e, axis=0, keepdims=True) + d_gam_last * cr["gam_last"]
            d_g_row = (_col_to_row(d_g_col, eye) - jnp.sum(gterm, axis=0, keepdims=True)
                       + jnp.where(lane == A_CHUNK - 1, d_g_last, 0.0))
            dgc_ref[hh, seq_chunk, :] = d_g_row
            dbeta_ref[hh, seq_chunk, :] = _col_to_row(d_beta_col, eye)

            qh = cin["aq"] * cin["rq"]
            kh = cin["ak"] * cin["rk"]
            d_qh = d_q * (A_DK ** -0.5)
            d_aq = cin["rq"] * (d_qh - qh * jnp.sum(d_qh * qh, axis=1, keepdims=True))
            d_ak = cin["rk"] * (d_k - kh * jnp.sum(d_k * kh, axis=1, keepdims=True))
            d_y = jnp.concatenate([d_aq, d_ak, d_v], axis=1) * _dsilu(cin["y"])
            shifted[hh, 0, 0:A_CHUNK, :] = d_y
            shifted[hh, 0, A_CHUNK:A_CHUNK + HALO, :] = dy_next[hh]
            shifted[hh, 1, 0:A_CHUNK + HALO, :] = cin["xw"]
            d_x = cw[A_CONV - 1:A_CONV, :] * d_y
            for j in range(1, A_CONV):
                d_x = d_x + cw[A_CONV - 1 - j:A_CONV - j, :] * shifted[hh, 0, j:j + A_CHUNK, :]
            for j in range(A_CONV):
                xs = shifted[hh, 1, HALO - j:HALO - j + A_CHUNK, :]
                dcw_ref[A_CONV - 1 - j:A_CONV - j, ccols] += jnp.sum(d_y * xs, axis=0, keepdims=True)
            dy_next[hh] = d_y[0:HALO, :]
            dx_ref[rows, lo:lo + A_CONV_COLS] = d_x.astype(BF16)
            dx_ref[rows, lo + A_CONV_COLS:lo + A_HEAD_COLS] = dz.astype(BF16)

        def chunk(ci, first):
            rows = pl.ds(0 if first else pl.multiple_of(ci * A_CHUNK, A_CHUNK), A_CHUNK)
            _round_robin([one_head(hh, ci, first, rows) for hh in range(hp)])

        lax.fori_loop(0, A_BLK_CHUNKS - 1, lambda i, c: (chunk(A_BLK_CHUNKS - 1 - i, False), c)[1], 0)
        chunk(0, True)

    rev = lambda j: nsb - 1 - j
    small = pl.BlockSpec((None, hp, n, A_CHUNK), lambda b, h, j: (b, h, 0, 0))
    wide = pl.BlockSpec((None, A_SEQ_BLK, hp * A_HEAD_COLS), lambda b, h, j: (b, rev(j), h))
    val = pl.BlockSpec((None, A_SEQ_BLK, hp * A_DV), lambda b, h, j: (b, rev(j), h))
    return pl.pallas_call(
        body, name="gdn_bwd", grid=(bn, A_HEADS // hp, nsb),
        in_specs=[wide,
                  pl.BlockSpec((None, None, HALO, hp * A_HEAD_COLS), lambda b, h, j: (b, rev(j), 0, h)),
                  pl.BlockSpec((A_CONV, hp * A_CONV_COLS), lambda b, h, j: (0, h)),
                  small, small,
                  pl.BlockSpec((1, A_DV), lambda b, h, j: (0, 0)),
                  val,
                  pl.BlockSpec((None, hp, A_BLK_CHUNKS, A_DK, A_DV), lambda b, h, j: (b, h, rev(j), 0, 0)),
                  pl.BlockSpec((None, hp, A_BLK_CHUNKS, A_CHUNK, A_CHUNK), lambda b, h, j: (b, h, rev(j), 0, 0)),
                  pl.BlockSpec((None, A_SEQ_BLK, hp * A_CONV_COLS), lambda b, h, j: (b, rev(j), h)),
                  val],
        out_specs=[wide, small, small,
                   pl.BlockSpec((None, A_CONV, hp * A_CONV_COLS), lambda b, h, j: (b, 0, h)),
                   pl.BlockSpec((None, hp, 8, A_DV), lambda b, h, j: (b, h, 0, 0))],
        out_shape=[jax.ShapeDtypeStruct((bn, s, A_HEADS * A_HEAD_COLS), BF16),
                   jax.ShapeDtypeStruct((bn, A_HEADS, n, A_CHUNK), F32),
                   jax.ShapeDtypeStruct((bn, A_HEADS, n, A_CHUNK), F32),
                   jax.ShapeDtypeStruct((bn, A_CONV, A_HEADS * A_CONV_COLS), F32),
                   jax.ShapeDtypeStruct((bn, A_HEADS, 8, A_DV), F32)],
        scratch_shapes=[pltpu.VMEM((hp, A_DK, A_DV), F32), pltpu.VMEM((hp, HALO, A_CONV_COLS), F32),
                        pltpu.VMEM((hp, 2, A_CHUNK + 2 * HALO, A_CONV_COLS), F32)],
        compiler_params=_params(("parallel", "parallel", "arbitrary"), VMEM_BIG),
    )(proj_hm, halo, cw_hm, beta, gc, norm_g, oraw, states, t_mats, conv_y, dog)


def _rope_tables(posf, inv_freq_row):
    t = posf.shape[0]
    tm = 512

    def body(p_ref, f_ref, c_ref, sa_ref, sb_ref):
        ang = p_ref[...] * f_ref[...]
        lane = lax.broadcasted_iota(jnp.int32, ang.shape, 1)
        half = ROPE_DIMS // 2
        c_ref[...] = jnp.where(lane < ROPE_DIMS, jnp.cos(ang), 1.0)
        sn = jnp.sin(ang)
        sa_ref[...] = jnp.where(lane < half, -sn, 0.0)
        sb_ref[...] = jnp.where((lane >= half) & (lane < ROPE_DIMS), sn, 0.0)

    row = pl.BlockSpec((tm, 128), lambda i: (i, 0))
    return pl.pallas_call(
        body, name="rope_tables", grid=(t // tm,),
        in_specs=[row, pl.BlockSpec((1, 128), lambda i: (0, 0))], out_specs=[row] * 3,
        out_shape=[jax.ShapeDtypeStruct((t, 128), F32)] * 3,
        compiler_params=_params(("parallel",)),
    )(posf, inv_freq_row)


def _rope(x, c, sa, sb):
    half = ROPE_DIMS // 2
    return x * c + pltpu.roll(x, 128 - half, 1) * sa + pltpu.roll(x, half, 1) * sb


def _rope_t(d, c, sa, sb):
    half = ROPE_DIMS // 2
    return d * c + pltpu.roll(d * sa, half, 1) + pltpu.roll(d * sb, 128 - half, 1)


def _qk_prep(proj, c, sa, sb, qg, kg, name, tm=256):
    t = proj.shape[0]

    def body(x_ref, c_ref, sa_ref, sb_ref, qg_ref, kg_ref, o_ref):
        cc, s1, s2 = c_ref[...], sa_ref[...], sb_ref[...]
        for which, g_ref in ((0, qg_ref), (1, kg_ref)):
            g = g_ref[...]
            for h in range(B_HEADS):
                lo = which * B_W + h * B_DH
                xv = x_ref[:, lo:lo + B_DH]
                r = lax.rsqrt(jnp.mean(xv * xv, axis=1, keepdims=True) + EPS)
                o_ref[:, lo:lo + B_DH] = _rope(xv * r * g, cc, s1, s2).astype(BF16)
        o_ref[:, 2 * B_W:3 * B_W] = x_ref[:, 2 * B_W:3 * B_W].astype(BF16)

    tab = pl.BlockSpec((tm, 128), lambda i: (i, 0))
    gain = pl.BlockSpec((1, B_DH), lambda i: (0, 0))
    return pl.pallas_call(
        body, name=name, grid=(t // tm,),
        in_specs=[pl.BlockSpec((tm, 3 * B_W), lambda i: (i, 0)), tab, tab, tab, gain, gain],
        out_specs=pl.BlockSpec((tm, 3 * B_W), lambda i: (i, 0)),
        out_shape=jax.ShapeDtypeStruct((t, 3 * B_W), BF16),
        compiler_params=_params(("parallel",), 40 * 1024 * 1024),
    )(proj, c, sa, sb, qg, kg)


def _qk_prep_bwd(proj, c, sa, sb, qg, kg, dq, dk, dv, dz, name, tm=256):
    t = proj.shape[0]
    out_w = 3 * B_W + (B_W if dz is not None else 0)

    def body(*refs):
        x_ref, c_ref, sa_ref, sb_ref, qg_ref, kg_ref, dq_ref, dk_ref, dv_ref = refs[:9]
        if dz is not None:
            dz_ref, o_ref, dgain_ref = refs[9:]
        else:
            o_ref, dgain_ref = refs[9:]
        i = pl.program_id(0)

        @pl.when(i == 0)
        def _():
            dgain_ref[...] = jnp.zeros_like(dgain_ref)

        cc, s1, s2 = c_ref[...], sa_ref[...], sb_ref[...]
        for which, g_ref, d_ref in ((0, qg_ref, dq_ref), (1, kg_ref, dk_ref)):
            g = g_ref[...]
            acc = jnp.zeros((1, B_DH), F32)
            for h in range(B_HEADS):
                lo = which * B_W + h * B_DH
                xv = x_ref[:, lo:lo + B_DH]
                r = lax.rsqrt(jnp.mean(xv * xv, axis=1, keepdims=True) + EPS)
                xh = xv * r
                d_xn = _rope_t(d_ref[:, h * B_DH:(h + 1) * B_DH].astype(F32), cc, s1, s2)
                acc = acc + jnp.sum(d_xn * xh, axis=0, keepdims=True)
                d_xh = d_xn * g
                d_x = r * (d_xh - xh * jnp.mean(d_xh * xh, axis=1, keepdims=True))
                o_ref[:, lo:lo + B_DH] = d_x.astype(BF16)
            dgain_ref[which:which + 1, :] += acc
        o_ref[:, 2 * B_W:3 * B_W] = dv_ref[...]
        if dz is not None:
            o_ref[:, 3 * B_W:4 * B_W] = dz_ref[...]

    tab = pl.BlockSpec((tm, 128), lambda i: (i, 0))
    gain = pl.BlockSpec((1, B_DH), lambda i: (0, 0))
    grad = pl.BlockSpec((tm, B_W), lambda i: (i, 0))
    in_specs = [pl.BlockSpec((tm, 2 * B_W), lambda i: (i, 0)), tab, tab, tab, gain, gain, grad, grad, grad]
    args = [proj, c, sa, sb, qg, kg, dq, dk, dv]
    if dz is not None:
        in_specs.append(grad)
        args.append(dz)
    return pl.pallas_call(
        body, name=name, grid=(t // tm,), in_specs=in_specs,
        out_specs=[pl.BlockSpec((tm, out_w), lambda i: (i, 0)), pl.BlockSpec((8, B_DH), lambda i: (0, 0))],
        out_shape=[jax.ShapeDtypeStruct((t, out_w), BF16), jax.ShapeDtypeStruct((8, B_DH), F32)],
        compiler_params=_params(("arbitrary",), 40 * 1024 * 1024),
    )(*args)


def _attn_masks():
    qi = lax.broadcasted_iota(jnp.int32, (B_BLK, 2 * B_BLK), 0)
    kj = lax.broadcasted_iota(jnp.int32, (B_BLK, 2 * B_BLK), 1)
    two = (kj >= qi) & (kj <= qi + B_BLK)
    q1 = lax.broadcasted_iota(jnp.int32, (B_BLK, B_BLK), 0)
    k1 = lax.broadcasted_iota(jnp.int32, (B_BLK, B_BLK), 1)
    return k1 <= q1, two


def _lane_pick(ref_rows, h):
    lane = lax.broadcasted_iota(jnp.int32, ref_rows.shape, 1)
    return jnp.sum(jnp.where(lane == h, ref_rows, 0.0), axis=1, keepdims=True)


B_ROWS = 2048


def _attn_schedule(nb, sb, block):
    way = 4

    def run(items):
        for at in range(0, len(items), way):
            _round_robin([block(*it) for it in items[at:at + way]])

    run([(si, 0, True) for si in range(sb)])
    if nb == 1:
        return
    per = max(1, way // sb)
    lead = 1 + (nb - 1) % per
    if lead > 1:
        run([(si, i, False) for i in range(1, lead) for si in range(sb)])

    def step(it, carry):
        run([(si, lead + it * per + u, False) for u in range(per) for si in range(sb)])
        return carry

    lax.fori_loop(0, (nb - lead) // per, step, 0)


def _attn_rows(i, first):
    if first:
        return pl.ds(0, B_BLK), pl.ds(0, B_BLK)
    rows = pl.ds(pl.multiple_of(i * B_BLK, B_BLK), B_BLK)
    return rows, pl.ds(pl.multiple_of((i - 1) * B_BLK, B_BLK), 2 * B_BLK)


def _attn_fwd(qkv, name):
    ns, ln, _ = qkv.shape
    nb = ln // B_BLK
    sb = B_ROWS // ln
    scale = B_DH ** -0.5

    def body(q_ref, k_ref, v_ref, o_ref, lse_ref):
        h = pl.program_id(1)
        mask1, mask2 = _attn_masks()
        lane = lax.broadcasted_iota(jnp.int32, (B_BLK, B_HEADS), 1)

        @pl.when(h == 0)
        def _():
            lse_ref[...] = jnp.zeros_like(lse_ref)

        def block(si, i, first):
            rows, win = _attn_rows(i, first)
            mask = mask1 if first else mask2
            sc = jnp.where(mask, _dot(q_ref[si, rows, :], k_ref[si, win, :], 1, 1) * scale, -1e30)
            yield
            m = jnp.max(sc, axis=1, keepdims=True)
            p = jnp.exp(sc - m)
            l = jnp.sum(p, axis=1, keepdims=True)
            pv = _dot(p, v_ref[si, win, :], 1, 0)
            yield
            o_ref[si, rows, :] = pv / l
            lse_ref[si, rows, :] = jnp.where(lane == h, m + jnp.log(l), lse_ref[si, rows, :])

        _attn_schedule(nb, sb, block)

    head = lambda off: pl.BlockSpec((sb, ln, B_DH), lambda s, h: (s, 0, off + h))
    return pl.pallas_call(
        body, name=name, grid=(ns // sb, B_HEADS),
        in_specs=[head(0), head(B_HEADS), head(2 * B_HEADS)],
        out_specs=[head(0), pl.BlockSpec((sb, ln, B_HEADS), lambda s, h: (s, 0, 0))],
        out_shape=[jax.ShapeDtypeStruct((ns, ln, B_W), F32), jax.ShapeDtypeStruct((ns, ln, B_HEADS), F32)],
        compiler_params=_params(("parallel", "arbitrary")),
    )(qkv, qkv, qkv)


def _attn_bwd(qkv, d_o, lse_joint, delta, name):
    ns, ln, _ = qkv.shape
    nb = ln // B_BLK
    sb = B_ROWS // ln
    scale = B_DH ** -0.5

    def body(q_ref, k_ref, v_ref, do_ref, lj_ref, dl_ref, dq_ref, dk_out, dv_out, dk_ref, dv_ref):
        h = pl.program_id(1)
        mask1, mask2 = _attn_masks()
        dk_ref[...] = jnp.zeros_like(dk_ref)
        dv_ref[...] = jnp.zeros_like(dv_ref)

        def block(si, i, first):
            rows, win = _attn_rows(i, first)
            mask = mask1 if first else mask2
            q = q_ref[si, rows, :]
            d_out = do_ref[si, rows, :]
            l_col = _lane_pick(lj_ref[si, rows, :], h)
            d_col = _lane_pick(dl_ref[si, rows, :], h)
            sc = _dot(q, k_ref[si, win, :], 1, 1) * scale
            d_p = _dot(d_out, v_ref[si, win, :], 1, 1)
            yield
            p = jnp.exp(jnp.where(mask, sc - l_col, -1e30))
            d_s = p * (d_p - d_col) * scale
            d_q = _dot(d_s, k_ref[si, win, :], 1, 0)
            d_k = _dot(d_s, q, 0, 0)
            d_v = _dot(p, d_out, 0, 0)
            yield
            dq_ref[si, rows, :] = d_q.astype(BF16)
            dk_ref[si, win, :] += d_k
            dv_ref[si, win, :] += d_v

        _attn_schedule(nb, sb, block)
        dk_out[...] = dk_ref[...].astype(BF16)
        dv_out[...] = dv_ref[...].astype(BF16)

    head = lambda off: pl.BlockSpec((sb, ln, B_DH), lambda s, h: (s, 0, off + h))
    small = pl.BlockSpec((sb, ln, B_HEADS), lambda s, h: (s, 0, 0))
    return pl.pallas_call(
        body, name=name, grid=(ns // sb, B_HEADS),
        in_specs=[head(0), head(B_HEADS), head(2 * B_HEADS), head(0), small, small],
        out_specs=[head(0)] * 3,
        out_shape=[jax.ShapeDtypeStruct((ns, ln, B_W), BF16)] * 3,
        scratch_shapes=[pltpu.VMEM((sb, ln, B_DH), F32)] * 2,
        compiler_params=_params(("parallel", "parallel")),
    )(qkv, qkv, qkv, d_o, lse_joint, delta)


def _merge_weights(lse_refs):
    ls = [r[...] for r in lse_refs]
    m = jnp.maximum(jnp.maximum(ls[0], ls[1]), ls[2])
    es = [jnp.exp(l - m) for l in ls]
    tot = es[0] + es[1] + es[2]
    return [e / tot for e in es], m + jnp.log(tot)


def _merge_fwd(outs, lses, proj0, tm=256):
    t = outs[0].shape[0]

    def body(o0, o1, o2, l0, l1, l2, z_ref, og_ref):
        wts, _ = _merge_weights((l0, l1, l2))
        for h in range(B_HEADS):
            cols = slice(h * B_DH, (h + 1) * B_DH)
            o = (wts[0][:, h:h + 1] * o0[:, cols] + wts[1][:, h:h + 1] * o1[:, cols]
                 + wts[2][:, h:h + 1] * o2[:, cols])
            og_ref[:, cols] = (o * _silu(z_ref[:, cols])).astype(BF16)

    wide = pl.BlockSpec((tm, B_W), lambda i: (i, 0))
    small = pl.BlockSpec((tm, B_HEADS), lambda i: (i, 0))
    return pl.pallas_call(
        body, name="merge_fwd", grid=(t // tm,),
        in_specs=[wide] * 3 + [small] * 3 + [pl.BlockSpec((tm, B_W), lambda i: (i, 3))],
        out_specs=wide, out_shape=jax.ShapeDtypeStruct((t, B_W), BF16),
        compiler_params=_params(("parallel",)),
    )(*outs, *lses, proj0)


def _merge_bwd(outs, lses, proj0, d_og, tm=256):
    t = outs[0].shape[0]

    def body(o0, o1, o2, l0, l1, l2, z_ref, dog_ref, do_ref, lj_ref, dl_ref, dz_ref):
        wts, lj = _merge_weights((l0, l1, l2))
        lj_ref[...] = lj
        lane = lax.broadcasted_iota(jnp.int32, (tm, B_HEADS), 1)
        delta = jnp.zeros((tm, B_HEADS), F32)
        for h in range(B_HEADS):
            cols = slice(h * B_DH, (h + 1) * B_DH)
            o = (wts[0][:, h:h + 1] * o0[:, cols] + wts[1][:, h:h + 1] * o1[:, cols]
                 + wts[2][:, h:h + 1] * o2[:, cols])
            z = z_ref[:, cols]
            d_g = dog_ref[:, cols]
            d_out = d_g * _silu(z)
            dz_ref[:, cols] = (d_g * o * _dsilu(z)).astype(BF16)
            do_ref[:, cols] = d_out.astype(BF16)
            delta = jnp.where(lane == h, jnp.sum(d_out * o, axis=1, keepdims=True), delta)
        dl_ref[...] = delta

    wide = pl.BlockSpec((tm, B_W), lambda i: (i, 0))
    small = pl.BlockSpec((tm, B_HEADS), lambda i: (i, 0))
    return pl.pallas_call(
        body, name="merge_bwd", grid=(t // tm,),
        in_specs=[wide] * 3 + [small] * 3 + [pl.BlockSpec((tm, B_W), lambda i: (i, 3)), wide],
        out_specs=[wide, small, small, wide],
        out_shape=[jax.ShapeDtypeStruct((t, B_W), BF16), jax.ShapeDtypeStruct((t, B_HEADS), F32),
                   jax.ShapeDtypeStruct((t, B_HEADS), F32), jax.ShapeDtypeStruct((t, B_W), BF16)],
        compiler_params=_params(("parallel",)),
    )(*outs, *lses, proj0, d_og)


def _adamw(w, g, m, v, name):
    r, c = w.shape
    tr = r
    for cand in (256, 128, 64, 32, 16, 8):
        if r % cand == 0:
            tr = cand
            break

    def body(w_ref, g_ref, m_ref, v_ref, d_ref, nm_ref, nv_ref):
        gv = g_ref[...]
        nm = ADAM_B1 * m_ref[...] + (1.0 - ADAM_B1) * gv
        nv = ADAM_B2 * v_ref[...] + (1.0 - ADAM_B2) * (gv * gv)
        m_hat = nm / (1.0 - ADAM_B1 ** ADAM_STEP)
        v_hat = nv / (1.0 - ADAM_B2 ** ADAM_STEP)
        d_ref[...] = -ADAM_LR * (m_hat / (jnp.sqrt(v_hat) + ADAM_EPS) + ADAM_WD * w_ref[...])
        nm_ref[...] = nm
        nv_ref[...] = nv

    blk = pl.BlockSpec((tr, c), lambda i: (i, 0))
    return pl.pallas_call(
        body, name=name, grid=(r // tr,), in_specs=[blk] * 4, out_specs=[blk] * 3,
        out_shape=[jax.ShapeDtypeStruct((r, c), F32)] * 3,
        compiler_params=_params(("parallel",)),
    )(w, g, m, v)


def _adam_update(w, gv, m, v):
    nm = ADAM_B1 * m + (1.0 - ADAM_B1) * gv
    nv = ADAM_B2 * v + (1.0 - ADAM_B2) * (gv * gv)
    m_hat = nm / (1.0 - ADAM_B1 ** ADAM_STEP)
    v_hat = nv / (1.0 - ADAM_B2 ** ADAM_STEP)
    return -ADAM_LR * (m_hat / (jnp.sqrt(v_hat) + ADAM_EPS) + ADAM_WD * w), nm, nv


def _adamw_shard(w, mine, theirs, m, v, half_index, name, tr=128):
    _, r, c = w.shape
    nhb = (r // 2) // tr

    def body(c_ref, w_ref, mine_ref, theirs_ref, m_ref, v_ref, g_ref, d_ref, nm_ref, nv_ref):
        is_mine = (pl.program_id(0) // nhb) == c_ref[0]
        gv = jnp.where(is_mine, mine_ref[...], theirs_ref[...])
        d, nm, nv = _adam_update(w_ref[...], gv, m_ref[...], v_ref[...])
        g_ref[...] = gv
        d_ref[...] = d
        nm_ref[...] = nm
        nv_ref[...] = nv

    full = pl.BlockSpec((None, tr, c), lambda i, cc: (0, i, 0))
    half = pl.BlockSpec((tr, c), lambda i, cc: (i % nhb, 0))
    return pl.pallas_call(
        body, name=name,
        grid_spec=pltpu.PrefetchScalarGridSpec(
            num_scalar_prefetch=1, grid=(2 * nhb,),
            in_specs=[full, half, half, full, full], out_specs=[full] * 4),
        out_shape=[jax.ShapeDtypeStruct(w.shape, F32)] * 4,
        compiler_params=_params(("parallel",), 40 * 1024 * 1024),
    )(half_index, w, mine, theirs, m, v)


def _adamw_shard_cols(w, mine, theirs, m, v, half_index, name, steps=20):
    c, _, r = w.shape
    tc = c // steps
    assert tc * steps == c

    def body(c_ref, w_ref, mine_ref, theirs_ref, m_ref, v_ref, g_ref, d_ref, nm_ref, nv_ref):
        first = jnp.where(c_ref[0] == 0, mine_ref[...], theirs_ref[...])
        second = jnp.where(c_ref[0] == 0, theirs_ref[...], mine_ref[...])
        for lo, gv in ((0, first), (r // 2, second)):
            cols = slice(lo, lo + r // 2)
            d, nm, nv = _adam_update(w_ref[:, :, cols], gv, m_ref[:, :, cols], v_ref[:, :, cols])
            g_ref[:, :, cols] = gv
            d_ref[:, :, cols] = d
            nm_ref[:, :, cols] = nm
            nv_ref[:, :, cols] = nv

    full = pl.BlockSpec((tc, 1, r), lambda i, cc: (i, 0, 0))
    half = pl.BlockSpec((tc, 1, r // 2), lambda i, cc: (i, 0, 0))
    return pl.pallas_call(
        body, name=name,
        grid_spec=pltpu.PrefetchScalarGridSpec(
            num_scalar_prefetch=1, grid=(steps,),
            in_specs=[full, half, half, full, full], out_specs=[full] * 4),
        out_shape=[jax.ShapeDtypeStruct(w.shape, F32)] * 4,
        compiler_params=_params(("parallel",), 40 * 1024 * 1024),
    )(half_index, w, mine, theirs, m, v)


def _pair_sum(own, other, half_index, name, tr=256):
    _, r, c = own.shape
    rh = r // 2
    tr = min(tr, rh)
    nrb = rh // tr

    def body(c_ref, own_ref, oth_ref, out_ref):
        out_ref[...] = (own_ref[...] + oth_ref[...].astype(F32)).astype(BF16)

    return pl.pallas_call(
        body, name=name,
        grid_spec=pltpu.PrefetchScalarGridSpec(
            num_scalar_prefetch=1, grid=(N_CHIPS, nrb),
            in_specs=[pl.BlockSpec((None, tr, c), lambda k, i, cc: (k, cc[0] * nrb + i, 0)),
                      pl.BlockSpec((None, tr, c), lambda k, i, cc: (k, i, 0))],
            out_specs=pl.BlockSpec((None, tr, c), lambda k, i, cc: (k, i, 0))),
        out_shape=jax.ShapeDtypeStruct((N_CHIPS, rh, c), BF16),
        compiler_params=_params(("parallel", "parallel")),
    )(half_index, own, other)


def _chip_sum(sums, others, chip_index, name, tr=256):
    _, r, c = sums.shape
    tr = min(tr, r)

    def body(k_ref, own_ref, oth_ref, out_ref):
        acc = own_ref[...].astype(F32)
        for j in range(N_CHIPS - 1):
            acc = acc + oth_ref[j].astype(F32)
        out_ref[...] = acc

    return pl.pallas_call(
        body, name=name,
        grid_spec=pltpu.PrefetchScalarGridSpec(
            num_scalar_prefetch=1, grid=(r // tr,),
            in_specs=[pl.BlockSpec((None, tr, c), lambda i, kk: (kk[0], i, 0)),
                      pl.BlockSpec((N_CHIPS - 1, tr, c), lambda i, kk: (0, i, 0))],
            out_specs=pl.BlockSpec((tr, c), lambda i, kk: (i, 0))),
        out_shape=jax.ShapeDtypeStruct((r, c), F32),
        compiler_params=_params(("parallel",)),
    )(chip_index, sums, others)


HBM = pl.BlockSpec(memory_space=pltpu.HBM)


def _place():
    x, y, c = lax.axis_index("x"), lax.axis_index("y"), lax.axis_index("c")
    chips = [(1 - x, y), (x, 1 - y), (1 - x, 1 - y)]
    return x, y, c, chips


def _weight_allgather(shards, conv_shard):
    na = len(shards)

    def body(*refs):
        ins = refs[:na]
        conv_in = refs[na]
        outs = refs[na + 1:2 * na + 1]
        conv_out = refs[2 * na + 1]
        send, recv, fsend, frecv, csend, crecv = refs[2 * na + 2:]
        x, y, c, chips = _place()
        me = 2 * x + y
        sib = (x, y, 1 - c)
        first, conv_cp = [], []
        for i in range(na):
            rh = ins[i].shape[0] // 2
            mine = pl.ds(c * rh, rh)
            for j, (px, py) in enumerate(chips):
                cp = pltpu.make_async_remote_copy(
                    src_ref=ins[i].at[mine], dst_ref=outs[i].at[me, mine],
                    send_sem=send.at[3 * i + j], recv_sem=recv.at[3 * i + j],
                    device_id=(px, py, c), device_id_type=MESH)
                cp.start()
                first.append(cp)
        for j, (px, py) in enumerate(chips):
            cp = pltpu.make_async_remote_copy(
                src_ref=conv_in, dst_ref=conv_out.at[me], send_sem=csend.at[j], recv_sem=crecv.at[j],
                device_id=(px, py, c), device_id_type=MESH)
            cp.start()
            conv_cp.append(cp)
        passed = []
        for i in range(na):
            rh = ins[i].shape[0] // 2
            mine = pl.ds(c * rh, rh)
            for j, (px, py) in enumerate(chips):
                slot = outs[i].at[2 * px + py, mine]
                pltpu.make_async_remote_copy(
                    src_ref=slot, dst_ref=slot, send_sem=send.at[3 * i + j], recv_sem=recv.at[3 * i + j],
                    device_id=(px, py, c), device_id_type=MESH).wait_recv()
                cp = pltpu.make_async_remote_copy(
                    src_ref=slot, dst_ref=slot, send_sem=fsend.at[3 * i + j], recv_sem=frecv.at[3 * i + j],
                    device_id=sib, device_id_type=MESH)
                cp.start()
                passed.append(cp)
        for i in range(na):
            rh = ins[i].shape[0] // 2
            theirs = pl.ds((1 - c) * rh, rh)
            for j, (px, py) in enumerate(chips):
                slot = outs[i].at[2 * px + py, theirs]
                pltpu.make_async_remote_copy(
                    src_ref=slot, dst_ref=slot, send_sem=fsend.at[3 * i + j], recv_sem=frecv.at[3 * i + j],
                    device_id=sib, device_id_type=MESH).wait_recv()
        for j, (px, py) in enumerate(chips):
            slot = conv_out.at[2 * px + py]
            pltpu.make_async_remote_copy(
                src_ref=slot, dst_ref=slot, send_sem=csend.at[j], recv_sem=crecv.at[j],
                device_id=(px, py, c), device_id_type=MESH).wait_recv()
        for cp in first + passed + conv_cp:
            cp.wait_send()

    out_shape = [jax.ShapeDtypeStruct((N_CHIPS,) + s.shape, s.dtype) for s in shards]
    out_shape.append(jax.ShapeDtypeStruct((N_CHIPS,) + conv_shard.shape, conv_shard.dtype))
    res = pl.pallas_call(
        body, name="weight_allgather",
        in_specs=[HBM] * (na + 1), out_specs=[HBM] * (na + 1), out_shape=out_shape,
        scratch_shapes=[pltpu.SemaphoreType.DMA((3 * na,)), pltpu.SemaphoreType.DMA((3 * na,)),
                        pltpu.SemaphoreType.DMA((3 * na,)), pltpu.SemaphoreType.DMA((3 * na,)),
                        pltpu.SemaphoreType.DMA((3,)), pltpu.SemaphoreType.DMA((3,))],
    )(*shards, conv_shard)
    my_chip = 2 * lax.axis_index("x") + lax.axis_index("y")
    pick = lambda got, own: [jnp.where(my_chip == k, own, got[k]) for k in range(N_CHIPS)]
    return [pick(g, s) for g, s in zip(res[:na], shards)], pick(res[na], conv_shard)


def _sibling_swap_halves(grads, name):
    na = len(grads)

    def body(*refs):
        ins, outs = refs[:na], refs[na:2 * na]
        send, recv = refs[2 * na:]
        x, y, c, _ = _place()
        sib = (x, y, 1 - c)
        cps = []
        for i in range(na):
            rh = ins[i].shape[1] // 2
            cp = pltpu.make_async_remote_copy(
                src_ref=ins[i].at[:, pl.ds((1 - c) * rh, rh), :], dst_ref=outs[i],
                send_sem=send.at[i], recv_sem=recv.at[i], device_id=sib, device_id_type=MESH)
            cp.start()
            cps.append(cp)
        for cp in cps:
            cp.wait()

    out_shape = [jax.ShapeDtypeStruct((g.shape[0], g.shape[1] // 2, g.shape[2]), g.dtype) for g in grads]
    return pl.pallas_call(
        body, name=name, in_specs=[HBM] * na, out_specs=[HBM] * na, out_shape=out_shape,
        scratch_shapes=[pltpu.SemaphoreType.DMA((na,)), pltpu.SemaphoreType.DMA((na,))],
    )(*grads)


def _sibling_swap_whole(halves):
    na = len(halves)

    def body(*refs):
        ins, outs = refs[:na], refs[na:2 * na]
        send, recv = refs[2 * na:]
        x, y, c, _ = _place()
        cps = []
        for i in range(na):
            cp = pltpu.make_async_remote_copy(
                src_ref=ins[i], dst_ref=outs[i], send_sem=send.at[i], recv_sem=recv.at[i],
                device_id=(x, y, 1 - c), device_id_type=MESH)
            cp.start()
            cps.append(cp)
        for cp in cps:
            cp.wait()

    out_shape = [jax.ShapeDtypeStruct(h.shape, h.dtype) for h in halves]
    return pl.pallas_call(
        body, name="grad_sibling_join", in_specs=[HBM] * na, out_specs=[HBM] * na, out_shape=out_shape,
        scratch_shapes=[pltpu.SemaphoreType.DMA((na,)), pltpu.SemaphoreType.DMA((na,))],
    )(*halves)


SEM = pl.BlockSpec(memory_space=pltpu.SEMAPHORE)
ANY = pl.BlockSpec(memory_space=pl.ANY)
EFFECT = pltpu.SideEffectType.DATAFLOW_SIDE_EFFECTING


def _split_copy_start(name, plan, srcs, lands, after):
    ns, nl = len(srcs), len(lands)

    def body(*refs):
        src_refs, land_refs = refs[:ns], refs[ns:ns + nl]
        send, recv = refs[ns + nl + 1], refs[ns + nl + 2]
        token = refs[-1]
        outgoing, _ = plan(src_refs, land_refs)
        for src, dst, dev, si, ri in outgoing:
            pltpu.make_async_remote_copy(src_ref=src, dst_ref=dst, send_sem=send.at[si], recv_sem=recv.at[ri],
                                         device_id=dev, device_id_type=MESH).start()
        token[...] = jnp.zeros_like(token)

    n_out, n_in = plan.counts
    thru = [pltpu.HBM(a.shape, a.dtype) for a in list(srcs) + list(lands)]
    res = pl.pallas_call(
        body, name=name,
        out_shape=[pltpu.SemaphoreType.DMA((n_out,)), pltpu.SemaphoreType.DMA((n_in,))] + thru
        + [jax.ShapeDtypeStruct((8, 128), F32)],
        in_specs=[HBM] * (ns + nl) + [ANY],
        out_specs=[SEM, SEM] + [HBM] * (ns + nl) + [pl.BlockSpec(memory_space=pltpu.VMEM)],
        input_output_aliases={i: 2 + i for i in range(ns + nl)},
        compiler_params=pltpu.CompilerParams(has_side_effects=EFFECT),
    )(*[pltpu.with_memory_space_constraint(a, pltpu.HBM) for a in list(srcs) + list(lands)], after)
    return res[0], res[1], res[2:2 + ns], res[2 + ns:2 + ns + nl], res[-1]


def _split_copy_wait(name, plan, send, recv, srcs, lands, after):
    ns, nl = len(srcs), len(lands)

    def body(*refs):
        src_refs, land_refs = refs[:ns], refs[ns:ns + nl]
        send_ref, recv_ref = refs[ns + nl], refs[ns + nl + 1]
        outgoing, arrivals = plan(src_refs, land_refs)
        for src, dst, dev, si, ri in outgoing:
            pltpu.make_async_remote_copy(src_ref=src, dst_ref=dst, send_sem=send_ref.at[si], recv_sem=recv_ref.at[ri],
                                         device_id=dev, device_id_type=MESH).wait_send()
        for view, ri in arrivals:
            pltpu.make_async_remote_copy(src_ref=view, dst_ref=view, send_sem=send_ref.at[0], recv_sem=recv_ref.at[ri],
                                         device_id=_place()[:3], device_id_type=MESH).wait_recv()

    thru = [pltpu.HBM(a.shape, a.dtype) for a in list(srcs) + list(lands)]
    res = pl.pallas_call(
        body, name=name, out_shape=thru,
        in_specs=[HBM] * (ns + nl) + [SEM, SEM, ANY], out_specs=[HBM] * (ns + nl),
        input_output_aliases={i: i for i in range(ns + nl)},
        compiler_params=pltpu.CompilerParams(has_side_effects=EFFECT),
    )(*srcs, *lands, send, recv, after)
    return res[:ns], res[ns:]


def _gather_plan(n_arrays):
    def plan(src_refs, land_refs):
        x, y, c, chips = _place()
        me = 2 * x + y
        outgoing, arrivals = [], []
        for i in range(n_arrays):
            rh = src_refs[i].shape[0] // 2
            mine = pl.ds(c * rh, rh)
            for j, (px, py) in enumerate(chips):
                for delta in range(2):
                    tc = c ^ delta
                    outgoing.append((src_refs[i].at[mine], land_refs[i].at[me, mine], (px, py, tc),
                                     6 * i + 2 * j + delta, 6 * i + 2 * j + delta))
                    theirs = pl.ds(tc * rh, rh)
                    arrivals.append((land_refs[i].at[2 * px + py, theirs], 6 * i + 2 * j + delta))
        return outgoing, arrivals

    plan.counts = (6 * n_arrays, 6 * n_arrays)
    return plan


def _exchange_plan(n_arrays):
    def plan(src_refs, land_refs):
        x, y, c, chips = _place()
        outgoing, arrivals = [], []
        for i in range(n_arrays):
            for j, (px, py) in enumerate(chips):
                outgoing.append((src_refs[i].at[2 * px + py], land_refs[i].at[j], (px, py, c), 3 * i + j, 3 * i + j))
                arrivals.append((land_refs[i].at[j], 3 * i + j))
        return outgoing, arrivals

    plan.counts = (3 * n_arrays, 3 * n_arrays)
    return plan


def _small_allreduce(vec):
    r, cdim = vec.shape
    n_dev = 8

    def body(v_ref, out_ref, buf, send, recv):
        x, y, c, _ = _place()
        me = 4 * x + 2 * y + c
        buf[me] = v_ref[...]
        cps = []
        for k in range(1, n_dev):
            dx, dy, dc = (k >> 2) & 1, (k >> 1) & 1, k & 1
            peer = (x ^ dx, y ^ dy, c ^ dc)
            cp = pltpu.make_async_remote_copy(
                src_ref=v_ref, dst_ref=buf.at[me], send_sem=send.at[k - 1], recv_sem=recv.at[k - 1],
                device_id=peer, device_id_type=MESH)
            cp.start()
            cps.append(cp)
        for k in range(1, n_dev):
            dx, dy, dc = (k >> 2) & 1, (k >> 1) & 1, k & 1
            src = 4 * (x ^ dx) + 2 * (y ^ dy) + (c ^ dc)
            slot = buf.at[src]
            pltpu.make_async_remote_copy(
                src_ref=slot, dst_ref=slot, send_sem=send.at[k - 1], recv_sem=recv.at[k - 1],
                device_id=(x ^ dx, y ^ dy, c ^ dc), device_id_type=MESH).wait_recv()
        for cp in cps:
            cp.wait_send()
        acc = buf[0]
        for k in range(1, n_dev):
            acc = acc + buf[k]
        out_ref[...] = acc

    vm = pl.BlockSpec(memory_space=pltpu.VMEM)
    return pl.pallas_call(
        body, name="small_allreduce", in_specs=[vm], out_specs=vm,
        out_shape=jax.ShapeDtypeStruct((r, cdim), F32),
        scratch_shapes=[pltpu.VMEM((n_dev, r, cdim), F32), pltpu.SemaphoreType.DMA((n_dev - 1,)),
                        pltpu.SemaphoreType.DMA((n_dev - 1,))],
    )(vec)


def _a_cols_to_head_major(w):
    lead = w.shape[:-1]
    q = w[..., :A_QK].reshape(lead + (A_HEADS, A_DK))
    k = w[..., A_QK:2 * A_QK].reshape(lead + (A_HEADS, A_DK))
    v = w[..., 2 * A_QK:2 * A_QK + A_VW].reshape(lead + (A_HEADS, A_DV))
    z = w[..., 2 * A_QK + A_VW:].reshape(lead + (A_HEADS, A_DV))
    return jnp.concatenate([q, k, v, z], axis=-1).reshape(lead + (A_HEADS * A_HEAD_COLS,))


def _a_cols_from_head_major(w):
    lead = w.shape[:-1]
    w = w.reshape(lead + (A_HEADS, A_HEAD_COLS))
    parts = [w[..., :A_DK], w[..., A_DK:2 * A_DK], w[..., 2 * A_DK:2 * A_DK + A_DV], w[..., 2 * A_DK + A_DV:]]
    return jnp.concatenate([p.reshape(lead + (-1,)) for p in parts], axis=-1)


def _conv_cols_to_head_major(w):
    lead = w.shape[:-1]
    q = w[..., :A_QK].reshape(lead + (A_HEADS, A_DK))
    k = w[..., A_QK:2 * A_QK].reshape(lead + (A_HEADS, A_DK))
    v = w[..., 2 * A_QK:].reshape(lead + (A_HEADS, A_DV))
    return jnp.concatenate([q, k, v], axis=-1).reshape(lead + (A_HEADS * A_CONV_COLS,))


def _conv_cols_from_head_major(w):
    lead = w.shape[:-1]
    w = w.reshape(lead + (A_HEADS, A_CONV_COLS))
    parts = [w[..., :A_DK], w[..., A_DK:2 * A_DK], w[..., 2 * A_DK:]]
    return jnp.concatenate([p.reshape(lead + (-1,)) for p in parts], axis=-1)


def _to_stream(a, bn, d):
    rest = a.shape[1:]
    s = a.shape[0] // bn
    a = a.reshape((bn, s // d, d) + rest)
    a = jnp.swapaxes(a, 1, 2)
    return a.reshape((bn * d, s // d) + rest)


def _from_stream(a, bn, d):
    rest = a.shape[2:]
    ln = a.shape[1]
    a = a.reshape((bn, d, ln) + rest)
    a = jnp.swapaxes(a, 1, 2)
    return a.reshape((bn * ln * d,) + rest)


B_SUB = 512
B_SHARD_BLOCKS = (3 * B_GROUPS * B_W + B_W) // N_CHIPS // B_SUB


def _b_block(gi, jj):
    nb = (B_GROUPS * (jj // 2) + gi) * 2 + jj % 2
    return nb // B_SHARD_BLOCKS, nb % B_SHARD_BLOCKS


def _shard_major(g, ncols):
    r = g.shape[0]
    return jnp.swapaxes(g.reshape(r, N_CHIPS, ncols), 0, 1)


def _pack_rows(items):
    rows, offs = [], []
    at = 0
    for a in items:
        flat = a.reshape(-1).astype(F32)
        nr = -(-flat.shape[0] // 1024) * 8
        flat = jnp.pad(flat, (0, nr * 128 - flat.shape[0]))
        rows.append(flat.reshape(nr, 128))
        offs.append((at, nr, a.shape))
        at += nr
    return jnp.concatenate(rows, axis=0), offs


def _unpack_rows(packed, offs):
    out = []
    for at, nr, shape in offs:
        size = int(np.prod(shape)) if len(shape) else 1
        out.append(packed[at:at + nr].reshape(-1)[:size].reshape(shape))
    return out


def _local_step(x, positions, loss_target, norm_g, wa_in, conv_w, a_log, a_dt_bias, a_norm_g,
                b_q_norm_g, b_k_norm_g, start_token, late_weights, b_grads_ready, a_grads_ready):
    bn, s, d = x.shape
    t = bn * s
    n_chunks = s // A_CHUNK
    wa_main = _a_cols_to_head_major(wa_in[:, :A_MAIN])
    wa_tail = jnp.pad(wa_in[:, A_MAIN:], ((0, 0), (0, 128 - 2 * A_HEADS)))
    cw_hm = _conv_cols_to_head_major(conv_w)

    x0 = x.reshape(t, d)
    h0 = _rms_fwd(x0, norm_g[0:1] + start_token, "rms0_fwd")
    proj_a = _matmul(h0, wa_main, "nn", F32, "a_in_main")
    tail_a = _matmul(h0, wa_tail, "nn", F32, "a_in_tail")
    tail_t = jnp.swapaxes(tail_a[:, :2 * A_HEADS].reshape(bn, s, 2 * A_HEADS), 1, 2)
    tail_t = tail_t.reshape(bn, 2 * A_HEADS, n_chunks, A_CHUNK)
    beta, gc = _gdn_prep(tail_t, a_log[0], a_dt_bias[0])
    proj_a3 = proj_a.reshape(bn, s, A_MAIN)
    og_a, oraw_a, states, t_mats, conv_y = _gdn_fwd(proj_a3, cw_hm, beta, gc, a_norm_g)
    wa_out, wb_in, wb_out = late_weights(og_a)
    b_cols = [4 * B_W] + [3 * B_W] * (B_GROUPS - 1)
    x1 = _matmul(og_a.reshape(t, A_VW), wa_out, "nn", F32, "a_out", res=x0, tk=2048)

    h1 = _rms_fwd(x1, norm_g[1:2], "rms1_fwd")
    inv_freq = ROPE_THETA ** (-jnp.arange(0, ROPE_DIMS, 2, dtype=F32) / ROPE_DIMS)
    freq_row = jnp.concatenate([inv_freq, inv_freq, jnp.zeros((128 - ROPE_DIMS,), F32)]).reshape(1, 128)
    posf = jnp.broadcast_to(positions.astype(F32).reshape(t, 1), (t, 128))
    tabs = _rope_tables(posf, freq_row)
    h1_s, tabs_s, proj_b, qkv_b, o_b, lse_b = [], [], [], [], [], []
    for gi, dil in enumerate(B_DIL):
        hs = h1 if dil == 1 else _to_stream(h1, bn, dil).reshape(t, d)
        ts = tabs if dil == 1 else [_to_stream(tb, bn, dil).reshape(t, 128) for tb in tabs]
        pj = _matmul(hs, wb_in, "nn", F32, f"b_in_g{gi}", tm=2048, tn=B_SUB, n=b_cols[gi], b_spec=pl.BlockSpec(
            (None, d, B_SUB), lambda i, j, kk, gi=gi: (_b_block(gi, j)[0], kk, _b_block(gi, j)[1])))
        qkv = _qk_prep(pj, *ts, b_q_norm_g[0, gi:gi + 1], b_k_norm_g[0, gi:gi + 1], f"qk_prep_g{gi}")
        o_s, lse_s = _attn_fwd(qkv.reshape(bn * dil, s // dil, 3 * B_W), f"attn_fwd_g{gi}")
        h1_s.append(hs), tabs_s.append(ts), proj_b.append(pj), qkv_b.append(qkv)
        o_b.append(o_s.reshape(t, B_W) if dil == 1 else _from_stream(o_s, bn, dil))
        lse_b.append(lse_s.reshape(t, B_HEADS) if dil == 1 else _from_stream(lse_s, bn, dil))
    og_b = _merge_fwd(o_b, lse_b, proj_b[0])
    x2 = _matmul(og_b, wb_out, "nn", F32, "b_out", res=x1)

    d_x2, loss_parts = _loss_grad(x2, loss_target.reshape(t, d))
    loss_local = jnp.sum(loss_parts)

    d_x2b = d_x2.astype(BF16)
    g_wb_out = _matmul(og_b.T, d_x2b, "nn", F32, "b_out_dw")
    d_og_b = _matmul(d_x2b, wb_out, "nt", F32, "b_out_dx")
    d_o, lse_joint, delta, d_z = _merge_bwd(o_b, lse_b, proj_b[0], d_og_b)
    d_h1, g_qn, g_kn = [], [], []
    g_wb_in = lax.empty(wb_in.shape, F32)
    for gi, dil in enumerate(B_DIL):
        if dil == 1:
            do_s, lj_s, dl_s = d_o, lse_joint, delta
        else:
            do_s, lj_s, dl_s = (_to_stream(a, bn, dil).reshape(t, -1) for a in (d_o, lse_joint, delta))
        ns, ln = bn * dil, s // dil
        dq, dk, dv = _attn_bwd(qkv_b[gi].reshape(ns, ln, 3 * B_W), do_s.reshape(ns, ln, B_W),
                               lj_s.reshape(ns, ln, B_HEADS), dl_s.reshape(ns, ln, B_HEADS), f"attn_bwd_g{gi}")
        d_pj, d_gain = _qk_prep_bwd(proj_b[gi], *tabs_s[gi], b_q_norm_g[0, gi:gi + 1], b_k_norm_g[0, gi:gi + 1],
                                    dq.reshape(t, B_W), dk.reshape(t, B_W), dv.reshape(t, B_W),
                                    d_z if gi == 0 else None, f"qk_prep_bwd_g{gi}")
        g_wb_in = _matmul(h1_s[gi].T, d_pj, "nn", F32, f"b_in_dw_g{gi}", tn=B_SUB, tk=2048, into=(g_wb_in, pl.BlockSpec(
            (None, d, B_SUB), lambda i, j, kk, gi=gi: (_b_block(gi, j)[0], i, _b_block(gi, j)[1]))))
        dh = _matmul(d_pj, wb_in, "nt", F32, f"b_in_dx_g{gi}", tm=2048, tk=B_SUB, n=d, b_spec=pl.BlockSpec(
            (None, d, B_SUB), lambda i, j, kk, gi=gi: (_b_block(gi, kk)[0], j, _b_block(gi, kk)[1])))
        d_h1.append(dh if dil == 1 else _from_stream(dh.reshape(ns, ln, d), bn, dil))
        g_qn.append(d_gain[0]), g_kn.append(d_gain[1])
    d_x1, g_norm1 = _rms_bwd(x1, norm_g[1:2], d_h1, d_x2, "rms1_bwd")

    d_x1b = d_x1.astype(BF16)
    g_wa_out = _matmul(og_a.reshape(t, A_VW).T, d_x1b, "nn", F32, "a_out_dw")
    b_token = b_grads_ready(g_wb_in, g_wb_out, g_wa_out)
    d_og_a = _matmul(d_x1b, wa_out, "nt", F32, "a_out_dx")
    d_pa, d_gc, d_beta, d_cw, d_ng = _gdn_bwd(proj_a3, cw_hm, beta, gc, a_norm_g + b_token, oraw_a, states,
                                              t_mats, conv_y, d_og_a.reshape(bn, s, A_VW))
    d_tail_t, d_alog, d_dtb = _gdn_prep_bwd(tail_t, a_log[0], a_dt_bias[0], d_gc, d_beta)
    d_tail = jnp.swapaxes(d_tail_t.reshape(bn, 2 * A_HEADS, s), 1, 2).reshape(t, 2 * A_HEADS)
    d_tail = jnp.pad(d_tail, ((0, 0), (0, 128 - 2 * A_HEADS))).astype(BF16)
    d_pa = d_pa.reshape(t, A_MAIN)
    h0_t = h0.T
    g_wa_main = _matmul(h0_t, d_pa, "nn", F32, "a_in_dw_main")
    g_wa_tail = _matmul(h0_t, d_tail, "nn", F32, "a_in_dw_tail")
    g_wa_in = jnp.concatenate([_a_cols_from_head_major(g_wa_main), g_wa_tail[:, :2 * A_HEADS]], axis=1)
    a_token = a_grads_ready(g_wa_in)
    d_h0 = _matmul(d_pa, wa_main, "nt", F32, "a_in_dx_main")
    d_h0t = _matmul(d_tail + a_token.astype(BF16), wa_tail, "nt", F32, "a_in_dx_tail")
    d_x0, g_norm0 = _rms_bwd(x0, norm_g[0:1], [d_h0, d_h0t], d_x1, "rms0_bwd")

    gfull = {
        "norm_g": jnp.concatenate([g_norm0, g_norm1], axis=0), "a_w_in": g_wa_in,
        "a_conv_w": _conv_cols_from_head_major(jnp.sum(d_cw, axis=0)),
        "a_log": jnp.sum(d_alog[:, :, 0], axis=0), "a_dt_bias": jnp.sum(d_dtb[:, :, 0], axis=0),
        "a_norm_g": jnp.sum(d_ng[:, :, 0, :], axis=(0, 1)), "a_w_out": g_wa_out, "b_w_in": g_wb_in,
        "b_q_norm_g": jnp.stack(g_qn), "b_k_norm_g": jnp.stack(g_kn), "b_w_out": g_wb_out}
    return loss_local, d_x0.reshape(bn, s, d), gfull


def kernel(x, positions, norm_g, a_w_in, a_conv_w, a_log, a_dt_bias, a_norm_g, a_w_out, b_w_in, b_q_norm_g, b_k_norm_g, b_w_out, loss_target, m_norm_g, m_a_w_in, m_a_conv_w, m_a_log, m_a_dt_bias, m_a_norm_g, m_a_w_out, m_b_w_in, m_b_q_norm_g, m_b_k_norm_g, m_b_w_out, v_norm_g, v_a_w_in, v_a_conv_w, v_a_log, v_a_dt_bias, v_a_norm_g, v_a_w_out, v_b_w_in, v_b_q_norm_g, v_b_k_norm_g, v_b_w_out):
    d = x.shape[2]
    my_c = lax.axis_index("c")
    my_chip = 2 * lax.axis_index("x") + lax.axis_index("y")

    half_index = jnp.reshape(my_c, (1,)).astype(jnp.int32)
    chip_index = jnp.reshape(my_chip, (1,)).astype(jnp.int32)
    (ga_in,), g_conv = _weight_allgather([a_w_in[0].astype(BF16)], a_conv_w[0])
    wa_in = jnp.concatenate(ga_in, axis=1)
    conv_w = jnp.concatenate(g_conv, axis=1)

    late_shards = [a_w_out[0].astype(BF16), b_w_in[0].astype(BF16), b_w_out[0].astype(BF16)]
    late_lands = [lax.dynamic_update_slice(lax.empty((N_CHIPS,) + s.shape, BF16), s[None], (my_chip, 0, 0))
                  for s in late_shards]
    gather = _gather_plan(len(late_shards))
    ag_send, ag_recv, ag_srcs, ag_lands, ag_token = _split_copy_start(
        "late_weights_start", gather, late_shards, late_lands, conv_w)

    def late_weights(after):
        _, (ga_out, gb_in, gb_out) = _split_copy_wait(
            "late_weights_wait", gather, ag_send, ag_recv, ag_srcs, ag_lands, after)
        return ga_out.reshape(A_VW, d), gb_in, gb_out.reshape(B_W, d)

    def reduce_to_chip_sums(mats, tag):
        recv_sib = _sibling_swap_halves([g.astype(BF16) for g in mats], f"grad_{tag}_sibling_swap")
        return [_pair_sum(g, r, half_index, f"grad_{tag}_pair_sum_{i}") for i, (g, r) in enumerate(zip(mats, recv_sib))]

    pending = {}

    def start_exchange(tag, mats):
        sums = reduce_to_chip_sums(mats, tag)
        lands = [lax.empty((N_CHIPS - 1,) + s.shape[1:], BF16) for s in sums]
        plan = _exchange_plan(len(mats))
        pending[tag] = (plan,) + tuple(_split_copy_start(f"grad_{tag}_exchange_start", plan, sums, lands, chip_index))
        return pending[tag][5][0, 0]

    def finish_exchange(tag, after):
        plan, send, recv, srcs, lands, _ = pending[tag]
        return _split_copy_wait(f"grad_{tag}_exchange_wait", plan, send, recv, srcs, lands, after)

    def b_grads_ready(g_wb_in, g_wb_out, g_wa_out):
        return start_exchange("b", [g_wb_in, g_wb_out.reshape(N_CHIPS, -1, d), g_wa_out.reshape(N_CHIPS, -1, d)])

    def a_grads_ready(g_wa_in):
        return start_exchange("a", [_shard_major(g_wa_in, a_w_in.shape[2])])

    loss_local, d_x0, gfull = _local_step(x, positions, loss_target, norm_g, wa_in, conv_w, a_log, a_dt_bias,
                                          a_norm_g, b_q_norm_g, b_k_norm_g, ag_token[0, 0], late_weights,
                                          b_grads_ready, a_grads_ready)

    small = [gfull["norm_g"], gfull["a_conv_w"], gfull["a_log"], gfull["a_dt_bias"], gfull["a_norm_g"],
             gfull["b_q_norm_g"], gfull["b_k_norm_g"], loss_local]
    packed, offs = _pack_rows(small)
    reduced = _small_allreduce(packed)
    g_norm, g_conv_all, g_alog, g_dtb, g_ang, g_q, g_k, loss = _unpack_rows(reduced, offs)
    g_conv_mine = lax.dynamic_slice_in_dim(g_conv_all, my_chip * a_conv_w.shape[2], a_conv_w.shape[2], axis=1)

    b_sums, b_received = finish_exchange("b", d_x0)
    a_sums, a_received = finish_exchange("a", reduced)
    chip_sums = [a_sums[0], b_sums[2], b_sums[0], b_sums[1]]
    received = [a_received[0], b_received[2], b_received[0], b_received[1]]
    halves = [_chip_sum(s, r, chip_index, f"grad_chip_sum_{i}") for i, (s, r) in enumerate(zip(chip_sums, received))]
    theirs = _sibling_swap_whole(halves)
    big = ("a_w_in", "a_w_out", "b_w_in", "b_w_out")
    big_halves = dict(zip(big, zip(halves, theirs)))

    grads = {
        "norm_g": g_norm, "a_conv_w": g_conv_mine[None], "a_log": g_alog[None], "a_dt_bias": g_dtb[None],
        "a_norm_g": g_ang[None], "b_q_norm_g": g_q[None], "b_k_norm_g": g_k[None]}
    weights = {"norm_g": norm_g, "a_w_in": a_w_in, "a_conv_w": a_conv_w, "a_log": a_log, "a_dt_bias": a_dt_bias,
               "a_norm_g": a_norm_g, "a_w_out": a_w_out, "b_w_in": b_w_in, "b_q_norm_g": b_q_norm_g,
               "b_k_norm_g": b_k_norm_g, "b_w_out": b_w_out}
    m_in = {"norm_g": m_norm_g, "a_w_in": m_a_w_in, "a_conv_w": m_a_conv_w, "a_log": m_a_log,
            "a_dt_bias": m_a_dt_bias, "a_norm_g": m_a_norm_g, "a_w_out": m_a_w_out, "b_w_in": m_b_w_in,
            "b_q_norm_g": m_b_q_norm_g, "b_k_norm_g": m_b_k_norm_g, "b_w_out": m_b_w_out}
    v_in = {"norm_g": v_norm_g, "a_w_in": v_a_w_in, "a_conv_w": v_a_conv_w, "a_log": v_a_log,
            "a_dt_bias": v_a_dt_bias, "a_norm_g": v_a_norm_g, "a_w_out": v_a_w_out, "b_w_in": v_b_w_in,
            "b_q_norm_g": v_b_q_norm_g, "b_k_norm_g": v_b_k_norm_g, "b_w_out": v_b_w_out}
    names = list(weights)

    delta_w, new_m, new_v = {}, {}, {}
    for nm in big:
        mine, other = big_halves[nm]
        if weights[nm].shape[2] % 128:
            cols = lambda a: jnp.transpose(a, (2, 0, 1))
            half_cols = lambda a: jnp.transpose(a)[:, None, :]
            outs = _adamw_shard_cols(cols(weights[nm]), half_cols(mine), half_cols(other), cols(m_in[nm]),
                                     cols(v_in[nm]), half_index, f"adamw_{nm}")
            outs = [jnp.transpose(o, (1, 2, 0)) for o in outs]
        else:
            outs = _adamw_shard(weights[nm], mine, other, m_in[nm], v_in[nm], half_index, f"adamw_{nm}")
        grads[nm], delta_w[nm], new_m[nm], new_v[nm] = outs
    small_names = [nm for nm in names if nm not in big]
    packs = [_pack_rows([src[nm] for nm in small_names]) for src in (weights, grads, m_in, v_in)]
    offs = packs[0][1]
    dl, m2, v2 = _adamw(packs[0][0], packs[1][0], packs[2][0], packs[3][0], "adamw_small")
    for nm, a, b, c2 in zip(small_names, _unpack_rows(dl, offs), _unpack_rows(m2, offs), _unpack_rows(v2, offs)):
        delta_w[nm], new_m[nm], new_v[nm] = a, b, c2

    return (loss, d_x0, *[grads[nm] for nm in names], *[delta_w[nm] for nm in names],
            *[new_m[nm] for nm in names], *[new_v[nm] for nm in names])
```

```python
import jax
import jax.numpy as jnp
import numpy as np
from jax import lax
from jax.experimental import pallas as pl
from jax.experimental.pallas import tpu as pltpu

F32 = jnp.float32
BF16 = jnp.bfloat16
MESH = pl.DeviceIdType.MESH

EPS = 1e-6
D_MODEL = 1024
A_HEADS = 8
A_DK = 128
A_DV = 256
A_QK = A_HEADS * A_DK
A_VW = A_HEADS * A_DV
A_MAIN = 2 * A_QK + 2 * A_VW
A_HEAD_COLS = 2 * A_DK + 2 * A_DV
A_CONV_COLS = 2 * A_DK + A_DV
A_CHUNK = 64
A_CONV = 4
B_GROUPS = 3
B_HEADS = 8
B_DH = 128
B_W = B_HEADS * B_DH
B_DIL = (1, 4, 16)
B_BLK = 128
ROPE_THETA = 500000.0
ROPE_DIMS = B_DH // 4
ADAM_LR, ADAM_B1, ADAM_B2, ADAM_EPS, ADAM_WD, ADAM_STEP = 0.001, 0.9, 0.999, 1e-08, 0.01, 10
N_CHIPS = 4
VMEM_BIG = 56 * 1024 * 1024


def _params(sem=None, vmem=None):
    return pltpu.CompilerParams(dimension_semantics=sem, vmem_limit_bytes=vmem)


def _dot(a, b, ca, cb):
    return lax.dot_general(a.astype(BF16), b.astype(BF16), (((ca,), (cb,)), ((), ())),
                           preferred_element_type=F32)


def _split3(a):
    hi = a.astype(BF16)
    r = a - hi.astype(F32)
    mid = r.astype(BF16)
    lo = (r - mid.astype(F32)).astype(BF16)
    return hi, mid, lo


def _sigmoid(y):
    return 1.0 / (1.0 + jnp.exp(-y))


def _silu(y):
    return y * _sigmoid(y)


def _dsilu(y):
    s = _sigmoid(y)
    return s * (1.0 + y * (1.0 - s))


def _matmul(a, b, mode, out_dtype, name, res=None, tm=1024, tn=1024, tk=1024, n=None, b_spec=None, into=None):
    m, k = a.shape[::-1] if mode == "tn" else a.shape
    if n is None:
        n = b.shape[0] if mode == "nt" else b.shape[1]
    tm, tn, tk = min(tm, m), min(tn, n), min(tk, k)
    assert m % tm == 0 and n % tn == 0 and k % tk == 0, (name, a.shape, b.shape)
    nk = k // tk
    dims = {"nn": ((1,), (0,)), "nt": ((1,), (1,)), "tn": ((0,), (0,))}[mode]

    def body(*refs):
        a_ref, b_ref = refs[0], refs[1]
        r_ref = refs[2] if res is not None else None
        o_ref = refs[2 + (res is not None) + (into is not None)]
        prod = lax.dot_general(a_ref[...], b_ref[...], (dims, ((), ())), preferred_element_type=F32)

        def finish(r):
            if res is not None:
                r = r + r_ref[...]
            o_ref[...] = r.astype(out_dtype)

        if nk == 1:
            finish(prod)
            return
        acc = refs[-1]
        kk = pl.program_id(2)

        @pl.when(kk == 0)
        def _():
            acc[...] = prod

        @pl.when((kk > 0) & (kk < nk - 1))
        def _():
            acc[...] += prod

        @pl.when(kk == nk - 1)
        def _():
            finish(acc[...] + prod)

    a_spec = pl.BlockSpec((tm, tk), lambda i, j, kk: (i, kk))
    if mode == "tn":
        a_spec = pl.BlockSpec((tk, tm), lambda i, j, kk: (kk, i))
    if b_spec is None and mode == "nt":
        b_spec = pl.BlockSpec((tn, tk), lambda i, j, kk: (j, kk))
    elif b_spec is None:
        b_spec = pl.BlockSpec((tk, tn), lambda i, j, kk: (kk, j))
    in_specs = [a_spec, b_spec]
    args = [a, b]
    if res is not None:
        in_specs.append(pl.BlockSpec((tm, tn), lambda i, j, kk: (i, j)))
        args.append(res)
    out_spec = pl.BlockSpec((tm, tn), lambda i, j, kk: (i, j))
    out_shape = jax.ShapeDtypeStruct((m, n), out_dtype)
    aliases = {}
    if into is not None:
        assert res is None
        buf, out_spec = into
        out_shape = jax.ShapeDtypeStruct(buf.shape, buf.dtype)
        in_specs.append(ANY)
        args.append(buf)
        aliases = {2: 0}
    return pl.pallas_call(
        body, name=name, grid=(m // tm, n // tn, nk),
        in_specs=in_specs, out_specs=out_spec, out_shape=out_shape, input_output_aliases=aliases,
        scratch_shapes=[pltpu.VMEM((tm, tn), F32)] if nk > 1 else [],
        compiler_params=_params(("parallel", "parallel", "arbitrary"), 48 * 1024 * 1024),
    )(*args)


def _rms_fwd(x, g, name, tm=256):
    t, d = x.shape

    def body(x_ref, g_ref, h_ref):
        xv = x_ref[...]
        r = lax.rsqrt(jnp.mean(xv * xv, axis=-1, keepdims=True) + EPS)
        h_ref[...] = (xv * r * g_ref[...]).astype(BF16)

    return pl.pallas_call(
        body, name=name, grid=(t // tm,),
        in_specs=[pl.BlockSpec((tm, d), lambda i: (i, 0)), pl.BlockSpec((1, d), lambda i: (0, 0))],
        out_specs=pl.BlockSpec((tm, d), lambda i: (i, 0)),
        out_shape=jax.ShapeDtypeStruct((t, d), BF16),
        compiler_params=_params(("parallel",)),
    )(x, g)


def _rms_bwd(x, g, dhs, dres, name, tm=256):
    t, d = x.shape
    n_dh = len(dhs)

    def body(*refs):
        x_ref, g_ref = refs[0], refs[1]
        dh_refs = refs[2:2 + n_dh]
        dres_ref, dx_ref, dg_ref = refs[2 + n_dh:]
        i = pl.program_id(0)

        @pl.when(i == 0)
        def _():
            dg_ref[...] = jnp.zeros_like(dg_ref)

        xv = x_ref[...]
        r = lax.rsqrt(jnp.mean(xv * xv, axis=-1, keepdims=True) + EPS)
        xh = xv * r
        dh = dh_refs[0][...]
        for ref in dh_refs[1:]:
            dh = dh + ref[...]
        dg_ref[0:1, :] += jnp.sum(dh * xh, axis=0, keepdims=True)
        dxh = dh * g_ref[...]
        dx = r * (dxh - xh * jnp.mean(dxh * xh, axis=-1, keepdims=True))
        dx_ref[...] = dx + dres_ref[...]

    row = pl.BlockSpec((tm, d), lambda i: (i, 0))
    dx, dg = pl.pallas_call(
        body, name=name, grid=(t // tm,),
        in_specs=[row, pl.BlockSpec((1, d), lambda i: (0, 0))] + [row] * n_dh + [row],
        out_specs=[row, pl.BlockSpec((8, d), lambda i: (0, 0))],
        out_shape=[jax.ShapeDtypeStruct((t, d), F32), jax.ShapeDtypeStruct((8, d), F32)],
        compiler_params=_params(("arbitrary",)),
    )(x, g, *dhs, dres)
    return dx, dg[0:1]


def _loss_grad(y, target, name="loss_grad", tm=256):
    t, d = y.shape
    nb = t // tm

    def body(y_ref, t_ref, dy_ref, part_ref):
        e = y_ref[...] - t_ref[...]
        dy_ref[...] = e * (1.0 / d)
        s = jnp.sum(jnp.sum(e * e, axis=1, keepdims=True), axis=0, keepdims=True) * (0.5 / d)
        part_ref[...] = jnp.broadcast_to(s, (8, 128))

    row = pl.BlockSpec((tm, d), lambda i: (i, 0))
    dy, part = pl.pallas_call(
        body, name=name, grid=(nb,), in_specs=[row, row],
        out_specs=[row, pl.BlockSpec((None, 8, 128), lambda i: (i, 0, 0))],
        out_shape=[jax.ShapeDtypeStruct((t, d), F32), jax.ShapeDtypeStruct((nb, 8, 128), F32)],
        compiler_params=_params(("parallel",)),
    )(y, target)
    return dy, part[:, 0, 0]


def _softplus(x):
    t = jnp.exp(-jnp.abs(x))
    return jnp.maximum(x, 0.0) + jnp.where(t < 1e-3, t * (1.0 - 0.5 * t), jnp.log(1.0 + t))


def _tri(rows_le_cols):
    r = lax.broadcasted_iota(jnp.int32, (A_CHUNK, A_CHUNK), 0)
    c = lax.broadcasted_iota(jnp.int32, (A_CHUNK, A_CHUNK), 1)
    return jnp.where((r <= c) if rows_le_cols else (r >= c), 1.0, 0.0).astype(BF16)


def _dot_exact_rhs(a, ones_bf16):
    dn = (((1,), (0,)), ((), ()))
    hi, mid, lo = _split3(a)
    out = lax.dot_general(hi, ones_bf16, dn, preferred_element_type=F32)
    out = out + lax.dot_general(mid, ones_bf16, dn, preferred_element_type=F32)
    return out + lax.dot_general(lo, ones_bf16, dn, preferred_element_type=F32)


def _gdn_prep(tail_t, a_log, dt_bias):
    bn, _, n, c = tail_t.shape

    def body(t_ref, alog_ref, dtb_ref, beta_ref, gc_ref):
        upper = _tri(True)
        for h in range(A_HEADS):
            beta_ref[h] = _sigmoid(t_ref[h])
            ea = jnp.exp(jnp.full((n, c), alog_ref[h], F32))
            g = -ea * _softplus(t_ref[A_HEADS + h] + dtb_ref[h])
            gc_ref[h] = _dot_exact_rhs(g, upper)

    smem = pl.BlockSpec(memory_space=pltpu.SMEM)
    blk = pl.BlockSpec((None, A_HEADS, n, c), lambda b: (b, 0, 0, 0))
    return pl.pallas_call(
        body, name="gdn_prep", grid=(bn,),
        in_specs=[pl.BlockSpec((None, 2 * A_HEADS, n, c), lambda b: (b, 0, 0, 0)), smem, smem],
        out_specs=[blk, blk],
        out_shape=[jax.ShapeDtypeStruct((bn, A_HEADS, n, c), F32)] * 2,
        compiler_params=_params(("parallel",)),
    )(tail_t, a_log, dt_bias)


def _gdn_prep_bwd(tail_t, a_log, dt_bias, d_gc, d_beta):
    bn, _, n, c = tail_t.shape

    def body(t_ref, alog_ref, dtb_ref, dgc_ref, dbeta_ref, dt_ref, dal_ref, ddt_ref):
        lower = _tri(False)
        for h in range(A_HEADS):
            beta = _sigmoid(t_ref[h])
            dt_ref[h] = dbeta_ref[h] * beta * (1.0 - beta)
            dg = _dot_exact_rhs(dgc_ref[h], lower)
            ea = jnp.exp(jnp.full((n, c), alog_ref[h], F32))
            xa = t_ref[A_HEADS + h] + dtb_ref[h]
            g = -ea * _softplus(xa)
            dxa = -ea * dg * _sigmoid(xa)
            dt_ref[A_HEADS + h] = dxa
            s1 = jnp.sum(jnp.sum(g * dg, axis=1, keepdims=True), axis=0, keepdims=True)
            s2 = jnp.sum(jnp.sum(dxa, axis=1, keepdims=True), axis=0, keepdims=True)
            dal_ref[h:h + 1, :] = jnp.broadcast_to(s1, (1, 128))
            ddt_ref[h:h + 1, :] = jnp.broadcast_to(s2, (1, 128))

    smem = pl.BlockSpec(memory_space=pltpu.SMEM)
    blk8 = pl.BlockSpec((None, A_HEADS, n, c), lambda b: (b, 0, 0, 0))
    blk16 = pl.BlockSpec((None, 2 * A_HEADS, n, c), lambda b: (b, 0, 0, 0))
    sm = pl.BlockSpec((None, A_HEADS, 128), lambda b: (b, 0, 0))
    return pl.pallas_call(
        body, name="gdn_prep_bwd", grid=(bn,),
        in_specs=[blk16, smem, smem, blk8, blk8],
        out_specs=[blk16, sm, sm],
        out_shape=[jax.ShapeDtypeStruct((bn, 2 * A_HEADS, n, c), F32),
                   jax.ShapeDtypeStruct((bn, A_HEADS, 128), F32),
                   jax.ShapeDtypeStruct((bn, A_HEADS, 128), F32)],
        compiler_params=_params(("parallel",)),
    )(tail_t, a_log, dt_bias, d_gc, d_beta)


HALO = 8


def _conv_taps(xw, w):
    y = w[A_CONV - 1:A_CONV, :] * xw
    for j in range(1, A_CONV):
        y = y + w[A_CONV - 1 - j:A_CONV - j, :] * pltpu.roll(xw, j, 0)
    return y[HALO:, :]


def _row_to_col(row, eye):
    c = eye.shape[0]
    return jnp.sum(jnp.where(eye, jnp.broadcast_to(row, (c, c)), 0.0), axis=1, keepdims=True)


def _col_to_row(col, eye):
    c = eye.shape[0]
    return jnp.sum(jnp.where(eye, jnp.broadcast_to(col, (c, c)), 0.0), axis=0, keepdims=True)


def _unit_lower_inverse(a, ri, ci):
    eye = jnp.where(ri == ci, 1.0, 0.0)
    a8 = jnp.where((ri >> 3) == (ci >> 3), a, 0.0)
    a2 = _dot(a8, a8, 1, 0)
    yield
    a4 = _dot(a2, a2, 1, 0)
    t = eye - a8
    t = t + _dot(t, a2, 1, 0)
    yield
    t = t + _dot(t, a4, 1, 0)
    yield
    for sh in (3, 4, 5):
        off = jnp.where(((ri >> (sh + 1)) == (ci >> (sh + 1))) & ((ri >> sh) != (ci >> sh)), a, 0.0)
        left = _dot(t, off, 1, 0)
        yield
        t = t - _dot(left, t, 1, 0)
        yield
    return t


def _round_robin(gens):
    live = list(gens)
    while live:
        nxt = []
        for g in live:
            try:
                next(g)
                nxt.append(g)
            except StopIteration:
                pass
        live = nxt


def _gdn_chunk_core(q, k, v, g_row, b_row, t_mat, ri, ci):
    eye = ri == ci
    g_col = _row_to_col(g_row, eye)
    b_col = _row_to_col(b_row, eye)
    causal = ri >= ci
    strict = ri > ci
    dec = jnp.where(causal, jnp.exp(jnp.where(causal, g_col - g_row, 0.0)), 0.0)
    gam = jnp.exp(g_col)
    g_last = g_row[:, A_CHUNK - 1:A_CHUNK]
    gam_last = jnp.exp(g_last)
    e = jnp.exp(g_last - g_col)
    kb = k * b_col
    bv = v * b_col
    kbg = kb * gam
    q16, k16, kb16 = q.astype(BF16), k.astype(BF16), kb.astype(BF16)
    kk = _dot(kb16, k16, 1, 1)
    p = _dot(q16, k16, 1, 1) * dec
    yield
    a_mat = jnp.where(strict, kk * dec, 0.0)
    if t_mat is None:
        t_mat = yield from _unit_lower_inverse(a_mat, ri, ci)
    t16 = t_mat.astype(BF16)
    u = _dot(t16, bv, 1, 0)
    w = _dot(t16, kbg, 1, 0)
    yield
    return dict(eye=eye, g_col=g_col, b_col=b_col, dec=dec, strict=strict, causal=causal, gam=gam,
                gam_last=gam_last, e=e, kb=kb, bv=bv, kbg=kbg, a_mat=a_mat, t_mat=t_mat, u=u, w=w, p=p,
                qg=q * gam, kd=k * e, q16=q16, k16=k16, kb16=kb16, t16=t16)


A_SEQ_BLK = 256
A_BLK_CHUNKS = A_SEQ_BLK // A_CHUNK


def _gdn_halo(proj_hm):
    bn, s, w = proj_hm.shape
    last = proj_hm.reshape(bn, s // A_SEQ_BLK, A_SEQ_BLK, w)[:, :, A_SEQ_BLK - HALO:, :]
    return jnp.concatenate([jnp.zeros((bn, 1, HALO, w), proj_hm.dtype), last[:, :-1]], axis=1)


def _gdn_window(x_ref, halo_ref, ci, first, lo):
    if first:
        return jnp.concatenate([halo_ref[:, lo:lo + A_CONV_COLS], x_ref[0:A_CHUNK, lo:lo + A_CONV_COLS]], axis=0)
    start = pl.multiple_of(ci * A_CHUNK - HALO, HALO)
    return x_ref[pl.ds(start, A_CHUNK + HALO), lo:lo + A_CONV_COLS]


def _gdn_chunk_prep(xw, cw, y=None):
    if y is None:
        y = _conv_taps(xw, cw)
    a = _silu(y)
    aq, ak, v = a[:, 0:A_DK], a[:, A_DK:2 * A_DK], a[:, 2 * A_DK:]
    rq = lax.rsqrt(jnp.sum(aq * aq, axis=1, keepdims=True) + EPS)
    rk = lax.rsqrt(jnp.sum(ak * ak, axis=1, keepdims=True) + EPS)
    return dict(xw=xw, y=y, aq=aq, ak=ak, rq=rq, rk=rk, q=aq * rq * (A_DK ** -0.5), k=ak * rk, v=v)


def _gdn_fwd(proj_hm, cw_hm, beta, gc, norm_g, hp=8):
    bn, s, _ = proj_hm.shape
    n = s // A_CHUNK
    nsb = s // A_SEQ_BLK
    halo = _gdn_halo(proj_hm)

    def body(x_ref, halo_ref, cw_ref, beta_ref, gc_ref, ng_ref, og_ref, oraw_ref, st_ref, t_ref, y_ref, state):
        first_chunk = pl.program_id(2) * A_BLK_CHUNKS
        ri = lax.broadcasted_iota(jnp.int32, (A_CHUNK, A_CHUNK), 0)
        ci_ = lax.broadcasted_iota(jnp.int32, (A_CHUNK, A_CHUNK), 1)
        ng = ng_ref[...]

        @pl.when(pl.program_id(2) == 0)
        def _():
            state[...] = jnp.zeros_like(state)

        def one_head(hh, ci, first, rows):
            lo = hh * A_HEAD_COLS
            cw = cw_ref[:, hh * A_CONV_COLS:(hh + 1) * A_CONV_COLS]
            cin = _gdn_chunk_prep(_gdn_window(x_ref, halo_ref, ci, first, lo), cw)
            y_ref[rows, hh * A_CONV_COLS:(hh + 1) * A_CONV_COLS] = cin["y"]
            seq_chunk = pl.ds(first_chunk + ci, 1)
            core = yield from _gdn_chunk_core(cin["q"], cin["k"], cin["v"], gc_ref[hh, seq_chunk, :],
                                              beta_ref[hh, seq_chunk, :], None, ri, ci_)
            st = state[hh]
            st_ref[hh, ci] = st
            t_ref[hh, ci] = core["t_mat"]
            st16 = st.astype(BF16)
            vn = core["u"] - _dot(core["w"], st16, 1, 0)
            qs = _dot(core["qg"], st16, 1, 0)
            yield
            vn16 = vn.astype(BF16)
            o = qs + _dot(core["p"], vn16, 1, 0)
            state[hh] = st * core["gam_last"] + _dot(core["kd"], vn16, 0, 0)
            yield
            ocols = slice(hh * A_DV, (hh + 1) * A_DV)
            oraw_ref[rows, ocols] = o
            r = lax.rsqrt(jnp.mean(o * o, axis=1, keepdims=True) + EPS)
            z = x_ref[rows, lo + A_CONV_COLS:lo + A_HEAD_COLS]
            og_ref[rows, ocols] = (o * r * ng * _silu(z)).astype(BF16)

        def chunk(ci, first):
            rows = pl.ds(0 if first else pl.multiple_of(ci * A_CHUNK, A_CHUNK), A_CHUNK)
            _round_robin([one_head(hh, ci, first, rows) for hh in range(hp)])

        chunk(0, True)
        lax.fori_loop(1, A_BLK_CHUNKS, lambda i, c: (chunk(i, False), c)[1], 0)

    small = pl.BlockSpec((None, hp, n, A_CHUNK), lambda b, h, j: (b, h, 0, 0))
    return pl.pallas_call(
        body, name="gdn_fwd", grid=(bn, A_HEADS // hp, nsb),
        in_specs=[pl.BlockSpec((None, A_SEQ_BLK, hp * A_HEAD_COLS), lambda b, h, j: (b, j, h)),
                  pl.BlockSpec((None, None, HALO, hp * A_HEAD_COLS), lambda b, h, j: (b, j, 0, h)),
                  pl.BlockSpec((A_CONV, hp * A_CONV_COLS), lambda b, h, j: (0, h)),
                  small, small,
                  pl.BlockSpec((1, A_DV), lambda b, h, j: (0, 0))],
        out_specs=[pl.BlockSpec((None, A_SEQ_BLK, hp * A_DV), lambda b, h, j: (b, j, h)),
                   pl.BlockSpec((None, A_SEQ_BLK, hp * A_DV), lambda b, h, j: (b, j, h)),
                   pl.BlockSpec((None, hp, A_BLK_CHUNKS, A_DK, A_DV), lambda b, h, j: (b, h, j, 0, 0)),
                   pl.BlockSpec((None, hp, A_BLK_CHUNKS, A_CHUNK, A_CHUNK), lambda b, h, j: (b, h, j, 0, 0)),
                   pl.BlockSpec((None, A_SEQ_BLK, hp * A_CONV_COLS), lambda b, h, j: (b, j, h))],
        out_shape=[jax.ShapeDtypeStruct((bn, s, A_VW), BF16),
                   jax.ShapeDtypeStruct((bn, s, A_VW), F32),
                   jax.ShapeDtypeStruct((bn, A_HEADS, n, A_DK, A_DV), F32),
                   jax.ShapeDtypeStruct((bn, A_HEADS, n, A_CHUNK, A_CHUNK), F32),
                   jax.ShapeDtypeStruct((bn, s, A_HEADS * A_CONV_COLS), F32)],
        scratch_shapes=[pltpu.VMEM((hp, A_DK, A_DV), F32)],
        compiler_params=_params(("parallel", "parallel", "arbitrary"), VMEM_BIG),
    )(proj_hm, halo, cw_hm, beta, gc, norm_g)


def _gdn_bwd(proj_hm, cw_hm, beta, gc, norm_g, oraw, states, t_mats, conv_y, dog, hp=4):
    bn, s, _ = proj_hm.shape
    n = s // A_CHUNK
    nsb = s // A_SEQ_BLK
    halo = _gdn_halo(proj_hm)

    def body(x_ref, halo_ref, cw_ref, beta_ref, gc_ref, ng_ref, oraw_ref, st_ref, t_ref, y_ref, dog_ref,
             dx_ref, dgc_ref, dbeta_ref, dcw_ref, dng_ref, dstate, dy_next, shifted):
        first_chunk = (nsb - 1 - pl.program_id(2)) * A_BLK_CHUNKS
        ri = lax.broadcasted_iota(jnp.int32, (A_CHUNK, A_CHUNK), 0)
        ci_ = lax.broadcasted_iota(jnp.int32, (A_CHUNK, A_CHUNK), 1)
        lane = lax.broadcasted_iota(jnp.int32, (1, A_CHUNK), 1)
        ng = ng_ref[...]

        @pl.when(pl.program_id(2) == 0)
        def _():
            dstate[...] = jnp.zeros_like(dstate)
            dy_next[...] = jnp.zeros_like(dy_next)
            dcw_ref[...] = jnp.zeros_like(dcw_ref)
            dng_ref[...] = jnp.zeros_like(dng_ref)

        def one_head(hh, ci, first, rows):
            lo = hh * A_HEAD_COLS
            ccols = slice(hh * A_CONV_COLS, (hh + 1) * A_CONV_COLS)
            ocols = slice(hh * A_DV, (hh + 1) * A_DV)
            cw = cw_ref[:, ccols]
            cin = _gdn_chunk_prep(_gdn_window(x_ref, halo_ref, ci, first, lo), cw, y_ref[rows, ccols])
            q, k, v = cin["q"], cin["k"], cin["v"]
            seq_chunk = pl.ds(first_chunk + ci, 1)
            cr = yield from _gdn_chunk_core(q, k, v, gc_ref[hh, seq_chunk, :], beta_ref[hh, seq_chunk, :],
                                            t_ref[hh, ci], ri, ci_)
            eye, dec, gam, e = cr["eye"], cr["dec"], cr["gam"], cr["e"]
            b_col, t_mat, u, w, p = cr["b_col"], cr["t_mat"], cr["u"], cr["w"], cr["p"]
            st = st_ref[hh, ci]
            ds_out = dstate[hh]

            o = oraw_ref[rows, ocols]
            z = x_ref[rows, lo + A_CONV_COLS:lo + A_HEAD_COLS]
            d_og = dog_ref[rows, ocols]
            r = lax.rsqrt(jnp.mean(o * o, axis=1, keepdims=True) + EPS)
            oh = o * r
            d_on = d_og * _silu(z)
            dz = d_og * oh * ng * _dsilu(z)
            dng_ref[hh, 0:1, :] += jnp.sum(d_on * oh, axis=0, keepdims=True)
            d_oh = d_on * ng
            d_o = r * (d_oh - oh * jnp.mean(d_oh * oh, axis=1, keepdims=True))

            st16, ds16, do16, w16 = st.astype(BF16), ds_out.astype(BF16), d_o.astype(BF16), w.astype(BF16)
            q16, k16, t16 = cr["q16"], cr["k16"], cr["t16"]
            vn = u - _dot(w16, st16, 1, 0)
            d_vn = _dot(p, do16, 0, 0) + _dot(cr["kd"], ds16, 1, 0)
            d_qg = _dot(do16, st16, 1, 1)
            qgdo = _dot(cr["qg"], do16, 0, 0)
            yield
            vn16, dvn16 = vn.astype(BF16), d_vn.astype(BF16)
            d_p = jnp.where(cr["causal"], _dot(do16, vn16, 1, 1), 0.0)
            d_kd = _dot(vn16, ds16, 1, 1)
            d_gam_last = jnp.sum(jnp.sum(st * ds_out, axis=1, keepdims=True), axis=0, keepdims=True)
            d_w = -_dot(dvn16, st16, 1, 1)
            dstate[hh] = qgdo + ds_out * cr["gam_last"] - _dot(w16, dvn16, 0, 0)
            d_bv = _dot(t16, dvn16, 0, 0)
            yield
            d_kbg = _dot(t16, d_w, 0, 0)
            n_p = (d_p * dec).astype(BF16)
            d_q = _dot(n_p, k16, 1, 0) + d_qg * gam
            npq = _dot(n_p, q16, 0, 0)
            yield
            d_a = jnp.where(cr["strict"], -(_dot(d_bv, u, 1, 1) + _dot(d_kbg, w16, 1, 1)), 0.0)
            yield
            m_a = (d_a * dec).astype(BF16)
            d_kb = _dot(m_a, k16, 1, 0) + d_kbg * gam
            d_k = (_dot(m_a, cr["kb16"], 0, 0) + npq + d_kd * e + d_kb * b_col)
            yield
            d_v = d_bv * b_col
            d_beta_col = (jnp.sum(d_bv * v, axis=1, keepdims=True)
                          + jnp.sum(d_kb * k, axis=1, keepdims=True))
            gterm = d_a * cr["a_mat"] + d_p * p
            d_e = jnp.sum(d_kd * k, axis=1, keepdims=True) * e
            d_g_col = (jnp.sum(gterm, axis=1, keepdims=True)
                       + (jnp.sum(d_qg * q, axis=1, keepdims=True)
                          + jnp.sum(d_kbg * cr["kb"], axis=1, keepdims=True)) * gam
                       - d_e)
            d_g_last = jnp.sum(d_e, axis=0, keepdims=True) + d_gam_last * cr["gam_last"]
            d_g_row = (_col_to_row(d_g_col, eye) - jnp.sum(gterm, axis=0, keepdims=True)
                       + jnp.where(lane == A_CHUNK - 1, d_g_last, 0.0))
            dgc_ref[hh, seq_chunk, :] = d_g_row
            dbeta_ref[hh, seq_chunk, :] = _col_to_row(d_beta_col, eye)

            qh = cin["aq"] * cin["rq"]
            kh = cin["ak"] * cin["rk"]
            d_qh = d_q * (A_DK ** -0.5)
            d_aq = cin["rq"] * (d_qh - qh * jnp.sum(d_qh * qh, axis=1, keepdims=True))
            d_ak = cin["rk"] * (d_k - kh * jnp.sum(d_k * kh, axis=1, keepdims=True))
            d_y = jnp.concatenate([d_aq, d_ak, d_v], axis=1) * _dsilu(cin["y"])
            shifted[hh, 0, 0:A_CHUNK, :] = d_y
            shifted[hh, 0, A_CHUNK:A_CHUNK + HALO, :] = dy_next[hh]
            shifted[hh, 1, 0:A_CHUNK + HALO, :] = cin["xw"]
            d_x = cw[A_CONV - 1:A_CONV, :] * d_y
            for j in range(1, A_CONV):
                d_x = d_x + cw[A_CONV - 1 - j:A_CONV - j, :] * shifted[hh, 0, j:j + A_CHUNK, :]
            for j in range(A_CONV):
                xs = shifted[hh, 1, HALO - j:HALO - j + A_CHUNK, :]
                dcw_ref[A_CONV - 1 - j:A_CONV - j, ccols] += jnp.sum(d_y * xs, axis=0, keepdims=True)
            dy_next[hh] = d_y[0:HALO, :]
            dx_ref[rows, lo:lo + A_CONV_COLS] = d_x.astype(BF16)
            dx_ref[rows, lo + A_CONV_COLS:lo + A_HEAD_COLS] = dz.astype(BF16)

        def chunk(ci, first):
            rows = pl.ds(0 if first else pl.multiple_of(ci * A_CHUNK, A_CHUNK), A_CHUNK)
            _round_robin([one_head(hh, ci, first, rows) for hh in range(hp)])

        lax.fori_loop(0, A_BLK_CHUNKS - 1, lambda i, c: (chunk(A_BLK_CHUNKS - 1 - i, False), c)[1], 0)
        chunk(0, True)

    rev = lambda j: nsb - 1 - j
    small = pl.BlockSpec((None, hp, n, A_CHUNK), lambda b, h, j: (b, h, 0, 0))
    wide = pl.BlockSpec((None, A_SEQ_BLK, hp * A_HEAD_COLS), lambda b, h, j: (b, rev(j), h))
    val = pl.BlockSpec((None, A_SEQ_BLK, hp * A_DV), lambda b, h, j: (b, rev(j), h))
    return pl.pallas_call(
        body, name="gdn_bwd", grid=(bn, A_HEADS // hp, nsb),
        in_specs=[wide,
                  pl.BlockSpec((None, None, HALO, hp * A_HEAD_COLS), lambda b, h, j: (b, rev(j), 0, h)),
                  pl.BlockSpec((A_CONV, hp * A_CONV_COLS), lambda b, h, j: (0, h)),
                  small, small,
                  pl.BlockSpec((1, A_DV), lambda b, h, j: (0, 0)),
                  val,
                  pl.BlockSpec((None, hp, A_BLK_CHUNKS, A_DK, A_DV), lambda b, h, j: (b, h, rev(j), 0, 0)),
                  pl.BlockSpec((None, hp, A_BLK_CHUNKS, A_CHUNK, A_CHUNK), lambda b, h, j: (b, h, rev(j), 0, 0)),
                  pl.BlockSpec((None, A_SEQ_BLK, hp * A_CONV_COLS), lambda b, h, j: (b, rev(j), h)),
                  val],
        out_specs=[wide, small, small,
                   pl.BlockSpec((None, A_CONV, hp * A_CONV_COLS), lambda b, h, j: (b, 0, h)),
                   pl.BlockSpec((None, hp, 8, A_DV), lambda b, h, j: (b, h, 0, 0))],
        out_shape=[jax.ShapeDtypeStruct((bn, s, A_HEADS * A_HEAD_COLS), BF16),
                   jax.ShapeDtypeStruct((bn, A_HEADS, n, A_CHUNK), F32),
                   jax.ShapeDtypeStruct((bn, A_HEADS, n, A_CHUNK), F32),
                   jax.ShapeDtypeStruct((bn, A_CONV, A_HEADS * A_CONV_COLS), F32),
                   jax.ShapeDtypeStruct((bn, A_HEADS, 8, A_DV), F32)],
        scratch_shapes=[pltpu.VMEM((hp, A_DK, A_DV), F32), pltpu.VMEM((hp, HALO, A_CONV_COLS), F32),
                        pltpu.VMEM((hp, 2, A_CHUNK + 2 * HALO, A_CONV_COLS), F32)],
        compiler_params=_params(("parallel", "parallel", "arbitrary"), VMEM_BIG),
    )(proj_hm, halo, cw_hm, beta, gc, norm_g, oraw, states, t_mats, conv_y, dog)


def _rope_tables(posf, inv_freq_row):
    t = posf.shape[0]
    tm = 512

    def body(p_ref, f_ref, c_ref, sa_ref, sb_ref):
        ang = p_ref[...] * f_ref[...]
        lane = lax.broadcasted_iota(jnp.int32, ang.shape, 1)
        half = ROPE_DIMS // 2
        c_ref[...] = jnp.where(lane < ROPE_DIMS, jnp.cos(ang), 1.0)
        sn = jnp.sin(ang)
        sa_ref[...] = jnp.where(lane < half, -sn, 0.0)
        sb_ref[...] = jnp.where((lane >= half) & (lane < ROPE_DIMS), sn, 0.0)

    row = pl.BlockSpec((tm, 128), lambda i: (i, 0))
    return pl.pallas_call(
        body, name="rope_tables", grid=(t // tm,),
        in_specs=[row, pl.BlockSpec((1, 128), lambda i: (0, 0))], out_specs=[row] * 3,
        out_shape=[jax.ShapeDtypeStruct((t, 128), F32)] * 3,
        compiler_params=_params(("parallel",)),
    )(posf, inv_freq_row)


def _rope(x, c, sa, sb):
    half = ROPE_DIMS // 2
    return x * c + pltpu.roll(x, 128 - half, 1) * sa + pltpu.roll(x, half, 1) * sb


def _rope_t(d, c, sa, sb):
    half = ROPE_DIMS // 2
    return d * c + pltpu.roll(d * sa, half, 1) + pltpu.roll(d * sb, 128 - half, 1)


def _qk_prep(proj, c, sa, sb, qg, kg, name, tm=256):
    t = proj.shape[0]

    def body(x_ref, c_ref, sa_ref, sb_ref, qg_ref, kg_ref, o_ref):
        cc, s1, s2 = c_ref[...], sa_ref[...], sb_ref[...]
        for which, g_ref in ((0, qg_ref), (1, kg_ref)):
            g = g_ref[...]
            for h in range(B_HEADS):
                lo = which * B_W + h * B_DH
                xv = x_ref[:, lo:lo + B_DH]
                r = lax.rsqrt(jnp.mean(xv * xv, axis=1, keepdims=True) + EPS)
                o_ref[:, lo:lo + B_DH] = _rope(xv * r * g, cc, s1, s2).astype(BF16)
        o_ref[:, 2 * B_W:3 * B_W] = x_ref[:, 2 * B_W:3 * B_W].astype(BF16)

    tab = pl.BlockSpec((tm, 128), lambda i: (i, 0))
    gain = pl.BlockSpec((1, B_DH), lambda i: (0, 0))
    return pl.pallas_call(
        body, name=name, grid=(t // tm,),
        in_specs=[pl.BlockSpec((tm, 3 * B_W), lambda i: (i, 0)), tab, tab, tab, gain, gain],
        out_specs=pl.BlockSpec((tm, 3 * B_W), lambda i: (i, 0)),
        out_shape=jax.ShapeDtypeStruct((t, 3 * B_W), BF16),
        compiler_params=_params(("parallel",), 40 * 1024 * 1024),
    )(proj, c, sa, sb, qg, kg)


def _qk_prep_bwd(proj, c, sa, sb, qg, kg, dq, dk, dv, dz, name, tm=256):
    t = proj.shape[0]
    out_w = 3 * B_W + (B_W if dz is not None else 0)

    def body(*refs):
        x_ref, c_ref, sa_ref, sb_ref, qg_ref, kg_ref, dq_ref, dk_ref, dv_ref = refs[:9]
        if dz is not None:
            dz_ref, o_ref, dgain_ref = refs[9:]
        else:
            o_ref, dgain_ref = refs[9:]
        i = pl.program_id(0)

        @pl.when(i == 0)
        def _():
            dgain_ref[...] = jnp.zeros_like(dgain_ref)

        cc, s1, s2 = c_ref[...], sa_ref[...], sb_ref[...]
        for which, g_ref, d_ref in ((0, qg_ref, dq_ref), (1, kg_ref, dk_ref)):
            g = g_ref[...]
            acc = jnp.zeros((1, B_DH), F32)
            for h in range(B_HEADS):
                lo = which * B_W + h * B_DH
                xv = x_ref[:, lo:lo + B_DH]
                r = lax.rsqrt(jnp.mean(xv * xv, axis=1, keepdims=True) + EPS)
                xh = xv * r
                d_xn = _rope_t(d_ref[:, h * B_DH:(h + 1) * B_DH].astype(F32), cc, s1, s2)
                acc = acc + jnp.sum(d_xn * xh, axis=0, keepdims=True)
                d_xh = d_xn * g
                d_x = r * (d_xh - xh * jnp.mean(d_xh * xh, axis=1, keepdims=True))
                o_ref[:, lo:lo + B_DH] = d_x.astype(BF16)
            dgain_ref[which:which + 1, :] += acc
        o_ref[:, 2 * B_W:3 * B_W] = dv_ref[...]
        if dz is not None:
            o_ref[:, 3 * B_W:4 * B_W] = dz_ref[...]

    tab = pl.BlockSpec((tm, 128), lambda i: (i, 0))
    gain = pl.BlockSpec((1, B_DH), lambda i: (0, 0))
    grad = pl.BlockSpec((tm, B_W), lambda i: (i, 0))
    in_specs = [pl.BlockSpec((tm, 2 * B_W), lambda i: (i, 0)), tab, tab, tab, gain, gain, grad, grad, grad]
    args = [proj, c, sa, sb, qg, kg, dq, dk, dv]
    if dz is not None:
        in_specs.append(grad)
        args.append(dz)
    return pl.pallas_call(
        body, name=name, grid=(t // tm,), in_specs=in_specs,
        out_specs=[pl.BlockSpec((tm, out_w), lambda i: (i, 0)), pl.BlockSpec((8, B_DH), lambda i: (0, 0))],
        out_shape=[jax.ShapeDtypeStruct((t, out_w), BF16), jax.ShapeDtypeStruct((8, B_DH), F32)],
        compiler_params=_params(("arbitrary",), 40 * 1024 * 1024),
    )(*args)


def _attn_masks():
    qi = lax.broadcasted_iota(jnp.int32, (B_BLK, 2 * B_BLK), 0)
    kj = lax.broadcasted_iota(jnp.int32, (B_BLK, 2 * B_BLK), 1)
    two = (kj >= qi) & (kj <= qi + B_BLK)
    q1 = lax.broadcasted_iota(jnp.int32, (B_BLK, B_BLK), 0)
    k1 = lax.broadcasted_iota(jnp.int32, (B_BLK, B_BLK), 1)
    return k1 <= q1, two


def _lane_pick(ref_rows, h):
    lane = lax.broadcasted_iota(jnp.int32, ref_rows.shape, 1)
    return jnp.sum(jnp.where(lane == h, ref_rows, 0.0), axis=1, keepdims=True)


B_ROWS = 2048


def _attn_schedule(nb, sb, block):
    way = 4

    def run(items):
        for at in range(0, len(items), way):
            _round_robin([block(*it) for it in items[at:at + way]])

    run([(si, 0, True) for si in range(sb)])
    if nb == 1:
        return
    per = max(1, way // sb)
    lead = 1 + (nb - 1) % per
    if lead > 1:
        run([(si, i, False) for i in range(1, lead) for si in range(sb)])

    def step(it, carry):
        run([(si, lead + it * per + u, False) for u in range(per) for si in range(sb)])
        return carry

    lax.fori_loop(0, (nb - lead) // per, step, 0)


def _attn_rows(i, first):
    if first:
        return pl.ds(0, B_BLK), pl.ds(0, B_BLK)
    rows = pl.ds(pl.multiple_of(i * B_BLK, B_BLK), B_BLK)
    return rows, pl.ds(pl.multiple_of((i - 1) * B_BLK, B_BLK), 2 * B_BLK)


def _attn_fwd(qkv, name):
    ns, ln, _ = qkv.shape
    nb = ln // B_BLK
    sb = B_ROWS // ln
    scale = B_DH ** -0.5

    def body(q_ref, k_ref, v_ref, o_ref, lse_ref):
        h = pl.program_id(1)
        mask1, mask2 = _attn_masks()
        lane = lax.broadcasted_iota(jnp.int32, (B_BLK, B_HEADS), 1)

        @pl.when(h == 0)
        def _():
            lse_ref[...] = jnp.zeros_like(lse_ref)

        def block(si, i, first):
            rows, win = _attn_rows(i, first)
            mask = mask1 if first else mask2
            sc = jnp.where(mask, _dot(q_ref[si, rows, :], k_ref[si, win, :], 1, 1) * scale, -1e30)
            yield
            m = jnp.max(sc, axis=1, keepdims=True)
            p = jnp.exp(sc - m)
            l = jnp.sum(p, axis=1, keepdims=True)
            pv = _dot(p, v_ref[si, win, :], 1, 0)
            yield
            o_ref[si, rows, :] = pv / l
            lse_ref[si, rows, :] = jnp.where(lane == h, m + jnp.log(l), lse_ref[si, rows, :])

        _attn_schedule(nb, sb, block)

    head = lambda off: pl.BlockSpec((sb, ln, B_DH), lambda s, h: (s, 0, off + h))
    return pl.pallas_call(
        body, name=name, grid=(ns // sb, B_HEADS),
        in_specs=[head(0), head(B_HEADS), head(2 * B_HEADS)],
        out_specs=[head(0), pl.BlockSpec((sb, ln, B_HEADS), lambda s, h: (s, 0, 0))],
        out_shape=[jax.ShapeDtypeStruct((ns, ln, B_W), F32), jax.ShapeDtypeStruct((ns, ln, B_HEADS), F32)],
        compiler_params=_params(("parallel", "arbitrary")),
    )(qkv, qkv, qkv)


def _attn_bwd(qkv, d_o, lse_joint, delta, name):
    ns, ln, _ = qkv.shape
    nb = ln // B_BLK
    sb = B_ROWS // ln
    scale = B_DH ** -0.5

    def body(q_ref, k_ref, v_ref, do_ref, lj_ref, dl_ref, dq_ref, dk_out, dv_out, dk_ref, dv_ref):
        h = pl.program_id(1)
        mask1, mask2 = _attn_masks()
        dk_ref[...] = jnp.zeros_like(dk_ref)
        dv_ref[...] = jnp.zeros_like(dv_ref)

        def block(si, i, first):
            rows, win = _attn_rows(i, first)
            mask = mask1 if first else mask2
            q = q_ref[si, rows, :]
            d_out = do_ref[si, rows, :]
            l_col = _lane_pick(lj_ref[si, rows, :], h)
            d_col = _lane_pick(dl_ref[si, rows, :], h)
            sc = _dot(q, k_ref[si, win, :], 1, 1) * scale
            d_p = _dot(d_out, v_ref[si, win, :], 1, 1)
            yield
            p = jnp.exp(jnp.where(mask, sc - l_col, -1e30))
            d_s = p * (d_p - d_col) * scale
            d_q = _dot(d_s, k_ref[si, win, :], 1, 0)
            d_k = _dot(d_s, q, 0, 0)
            d_v = _dot(p, d_out, 0, 0)
            yield
            dq_ref[si, rows, :] = d_q.astype(BF16)
            dk_ref[si, win, :] += d_k
            dv_ref[si, win, :] += d_v

        _attn_schedule(nb, sb, block)
        dk_out[...] = dk_ref[...].astype(BF16)
        dv_out[...] = dv_ref[...].astype(BF16)

    head = lambda off: pl.BlockSpec((sb, ln, B_DH), lambda s, h: (s, 0, off + h))
    small = pl.BlockSpec((sb, ln, B_HEADS), lambda s, h: (s, 0, 0))
    return pl.pallas_call(
        body, name=name, grid=(ns // sb, B_HEADS),
        in_specs=[head(0), head(B_HEADS), head(2 * B_HEADS), head(0), small, small],
        out_specs=[head(0)] * 3,
        out_shape=[jax.ShapeDtypeStruct((ns, ln, B_W), BF16)] * 3,
        scratch_shapes=[pltpu.VMEM((sb, ln, B_DH), F32)] * 2,
        compiler_params=_params(("parallel", "parallel")),
    )(qkv, qkv, qkv, d_o, lse_joint, delta)


def _merge_weights(lse_refs):
    ls = [r[...] for r in lse_refs]
    m = jnp.maximum(jnp.maximum(ls[0], ls[1]), ls[2])
    es = [jnp.exp(l - m) for l in ls]
    tot = es[0] + es[1] + es[2]
    return [e / tot for e in es], m + jnp.log(tot)


def _merge_fwd(outs, lses, proj0, tm=256):
    t = outs[0].shape[0]

    def body(o0, o1, o2, l0, l1, l2, z_ref, og_ref):
        wts, _ = _merge_weights((l0, l1, l2))
        for h in range(B_HEADS):
            cols = slice(h * B_DH, (h + 1) * B_DH)
            o = (wts[0][:, h:h + 1] * o0[:, cols] + wts[1][:, h:h + 1] * o1[:, cols]
                 + wts[2][:, h:h + 1] * o2[:, cols])
            og_ref[:, cols] = (o * _silu(z_ref[:, cols])).astype(BF16)

    wide = pl.BlockSpec((tm, B_W), lambda i: (i, 0))
    small = pl.BlockSpec((tm, B_HEADS), lambda i: (i, 0))
    return pl.pallas_call(
        body, name="merge_fwd", grid=(t // tm,),
        in_specs=[wide] * 3 + [small] * 3 + [pl.BlockSpec((tm, B_W), lambda i: (i, 3))],
        out_specs=wide, out_shape=jax.ShapeDtypeStruct((t, B_W), BF16),
        compiler_params=_params(("parallel",)),
    )(*outs, *lses, proj0)


def _merge_bwd(outs, lses, proj0, d_og, tm=256):
    t = outs[0].shape[0]

    def body(o0, o1, o2, l0, l1, l2, z_ref, dog_ref, do_ref, lj_ref, dl_ref, dz_ref):
        wts, lj = _merge_weights((l0, l1, l2))
        lj_ref[...] = lj
        lane = lax.broadcasted_iota(jnp.int32, (tm, B_HEADS), 1)
        delta = jnp.zeros((tm, B_HEADS), F32)
        for h in range(B_HEADS):
            cols = slice(h * B_DH, (h + 1) * B_DH)
            o = (wts[0][:, h:h + 1] * o0[:, cols] + wts[1][:, h:h + 1] * o1[:, cols]
                 + wts[2][:, h:h + 1] * o2[:, cols])
            z = z_ref[:, cols]
            d_g = dog_ref[:, cols]
            d_out = d_g * _silu(z)
            dz_ref[:, cols] = (d_g * o * _dsilu(z)).astype(BF16)
            do_ref[:, cols] = d_out.astype(BF16)
            delta = jnp.where(lane == h, jnp.sum(d_out * o, axis=1, keepdims=True), delta)
        dl_ref[...] = delta

    wide = pl.BlockSpec((tm, B_W), lambda i: (i, 0))
    small = pl.BlockSpec((tm, B_HEADS), lambda i: (i, 0))
    return pl.pallas_call(
        body, name="merge_bwd", grid=(t // tm,),
        in_specs=[wide] * 3 + [small] * 3 + [pl.BlockSpec((tm, B_W), lambda i: (i, 3)), wide],
        out_specs=[wide, small, small, wide],
        out_shape=[jax.ShapeDtypeStruct((t, B_W), BF16), jax.ShapeDtypeStruct((t, B_HEADS), F32),
                   jax.ShapeDtypeStruct((t, B_HEADS), F32), jax.ShapeDtypeStruct((t, B_W), BF16)],
        compiler_params=_params(("parallel",)),
    )(*outs, *lses, proj0, d_og)


def _adamw(w, g, m, v, name):
    r, c = w.shape
    tr = r
    for cand in (256, 128, 64, 32, 16, 8):
        if r % cand == 0:
            tr = cand
            break

    def body(w_ref, g_ref, m_ref, v_ref, d_ref, nm_ref, nv_ref):
        gv = g_ref[...]
        nm = ADAM_B1 * m_ref[...] + (1.0 - ADAM_B1) * gv
        nv = ADAM_B2 * v_ref[...] + (1.0 - ADAM_B2) * (gv * gv)
        m_hat = nm / (1.0 - ADAM_B1 ** ADAM_STEP)
        v_hat = nv / (1.0 - ADAM_B2 ** ADAM_STEP)
        d_ref[...] = -ADAM_LR * (m_hat / (jnp.sqrt(v_hat) + ADAM_EPS) + ADAM_WD * w_ref[...])
        nm_ref[...] = nm
        nv_ref[...] = nv

    blk = pl.BlockSpec((tr, c), lambda i: (i, 0))
    return pl.pallas_call(
        body, name=name, grid=(r // tr,), in_specs=[blk] * 4, out_specs=[blk] * 3,
        out_shape=[jax.ShapeDtypeStruct((r, c), F32)] * 3,
        compiler_params=_params(("parallel",)),
    )(w, g, m, v)


def _adam_update(w, gv, m, v):
    nm = ADAM_B1 * m + (1.0 - ADAM_B1) * gv
    nv = ADAM_B2 * v + (1.0 - ADAM_B2) * (gv * gv)
    m_hat = nm / (1.0 - ADAM_B1 ** ADAM_STEP)
    v_hat = nv / (1.0 - ADAM_B2 ** ADAM_STEP)
    return -ADAM_LR * (m_hat / (jnp.sqrt(v_hat) + ADAM_EPS) + ADAM_WD * w), nm, nv


def _adamw_shard(w, mine, theirs, m, v, half_index, name, tr=128):
    _, r, c = w.shape
    nhb = (r // 2) // tr

    def body(c_ref, w_ref, mine_ref, theirs_ref, m_ref, v_ref, g_ref, d_ref, nm_ref, nv_ref):
        is_mine = (pl.program_id(0) // nhb) == c_ref[0]
        gv = jnp.where(is_mine, mine_ref[...], theirs_ref[...])
        d, nm, nv = _adam_update(w_ref[...], gv, m_ref[...], v_ref[...])
        g_ref[...] = gv
        d_ref[...] = d
        nm_ref[...] = nm
        nv_ref[...] = nv

    full = pl.BlockSpec((None, tr, c), lambda i, cc: (0, i, 0))
    half = pl.BlockSpec((tr, c), lambda i, cc: (i % nhb, 0))
    return pl.pallas_call(
        body, name=name,
        grid_spec=pltpu.PrefetchScalarGridSpec(
            num_scalar_prefetch=1, grid=(2 * nhb,),
            in_specs=[full, half, half, full, full], out_specs=[full] * 4),
        out_shape=[jax.ShapeDtypeStruct(w.shape, F32)] * 4,
        compiler_params=_params(("parallel",), 40 * 1024 * 1024),
    )(half_index, w, mine, theirs, m, v)


def _adamw_shard_cols(w, mine, theirs, m, v, half_index, name, steps=20):
    c, _, r = w.shape
    tc = c // steps
    assert tc * steps == c

    def body(c_ref, w_ref, mine_ref, theirs_ref, m_ref, v_ref, g_ref, d_ref, nm_ref, nv_ref):
        first = jnp.where(c_ref[0] == 0, mine_ref[...], theirs_ref[...])
        second = jnp.where(c_ref[0] == 0, theirs_ref[...], mine_ref[...])
        for lo, gv in ((0, first), (r // 2, second)):
            cols = slice(lo, lo + r // 2)
            d, nm, nv = _adam_update(w_ref[:, :, cols], gv, m_ref[:, :, cols], v_ref[:, :, cols])
            g_ref[:, :, cols] = gv
            d_ref[:, :, cols] = d
            nm_ref[:, :, cols] = nm
            nv_ref[:, :, cols] = nv

    full = pl.BlockSpec((tc, 1, r), lambda i, cc: (i, 0, 0))
    half = pl.BlockSpec((tc, 1, r // 2), lambda i, cc: (i, 0, 0))
    return pl.pallas_call(
        body, name=name,
        grid_spec=pltpu.PrefetchScalarGridSpec(
            num_scalar_prefetch=1, grid=(steps,),
            in_specs=[full, half, half, full, full], out_specs=[full] * 4),
        out_shape=[jax.ShapeDtypeStruct(w.shape, F32)] * 4,
        compiler_params=_params(("parallel",), 40 * 1024 * 1024),
    )(half_index, w, mine, theirs, m, v)


def _pair_sum(own, other, half_index, name, tr=256):
    _, r, c = own.shape
    rh = r // 2
    tr = min(tr, rh)
    nrb = rh // tr

    def body(c_ref, own_ref, oth_ref, out_ref):
        out_ref[...] = (own_ref[...] + oth_ref[...].astype(F32)).astype(BF16)

    return pl.pallas_call(
        body, name=name,
        grid_spec=pltpu.PrefetchScalarGridSpec(
            num_scalar_prefetch=1, grid=(N_CHIPS, nrb),
            in_specs=[pl.BlockSpec((None, tr, c), lambda k, i, cc: (k, cc[0] * nrb + i, 0)),
                      pl.BlockSpec((None, tr, c), lambda k, i, cc: (k, i, 0))],
            out_specs=pl.BlockSpec((None, tr, c), lambda k, i, cc: (k, i, 0))),
        out_shape=jax.ShapeDtypeStruct((N_CHIPS, rh, c), BF16),
        compiler_params=_params(("parallel", "parallel")),
    )(half_index, own, other)


def _chip_sum(sums, others, chip_index, name, tr=256):
    _, r, c = sums.shape
    tr = min(tr, r)

    def body(k_ref, own_ref, oth_ref, out_ref):
        acc = own_ref[...].astype(F32)
        for j in range(N_CHIPS - 1):
            acc = acc + oth_ref[j].astype(F32)
        out_ref[...] = acc

    return pl.pallas_call(
        body, name=name,
        grid_spec=pltpu.PrefetchScalarGridSpec(
            num_scalar_prefetch=1, grid=(r // tr,),
            in_specs=[pl.BlockSpec((None, tr, c), lambda i, kk: (kk[0], i, 0)),
                      pl.BlockSpec((N_CHIPS - 1, tr, c), lambda i, kk: (0, i, 0))],
            out_specs=pl.BlockSpec((tr, c), lambda i, kk: (i, 0))),
        out_shape=jax.ShapeDtypeStruct((r, c), F32),
        compiler_params=_params(("parallel",)),
    )(chip_index, sums, others)


HBM = pl.BlockSpec(memory_space=pltpu.HBM)


def _place():
    x, y, c = lax.axis_index("x"), lax.axis_index("y"), lax.axis_index("c")
    chips = [(1 - x, y), (x, 1 - y), (1 - x, 1 - y)]
    return x, y, c, chips


def _weight_allgather(shards, conv_shard):
    na = len(shards)

    def body(*refs):
        ins = refs[:na]
        conv_in = refs[na]
        outs = refs[na + 1:2 * na + 1]
        conv_out = refs[2 * na + 1]
        send, recv, fsend, frecv, csend, crecv = refs[2 * na + 2:]
        x, y, c, chips = _place()
        me = 2 * x + y
        sib = (x, y, 1 - c)
        first, conv_cp = [], []
        for i in range(na):
            rh = ins[i].shape[0] // 2
            mine = pl.ds(c * rh, rh)
            for j, (px, py) in enumerate(chips):
                cp = pltpu.make_async_remote_copy(
                    src_ref=ins[i].at[mine], dst_ref=outs[i].at[me, mine],
                    send_sem=send.at[3 * i + j], recv_sem=recv.at[3 * i + j],
                    device_id=(px, py, c), device_id_type=MESH)
                cp.start()
                first.append(cp)
        for j, (px, py) in enumerate(chips):
            cp = pltpu.make_async_remote_copy(
                src_ref=conv_in, dst_ref=conv_out.at[me], send_sem=csend.at[j], recv_sem=crecv.at[j],
                device_id=(px, py, c), device_id_type=MESH)
            cp.start()
            conv_cp.append(cp)
        passed = []
        for i in range(na):
            rh = ins[i].shape[0] // 2
            mine = pl.ds(c * rh, rh)
            for j, (px, py) in enumerate(chips):
                slot = outs[i].at[2 * px + py, mine]
                pltpu.make_async_remote_copy(
                    src_ref=slot, dst_ref=slot, send_sem=send.at[3 * i + j], recv_sem=recv.at[3 * i + j],
                    device_id=(px, py, c), device_id_type=MESH).wait_recv()
                cp = pltpu.make_async_remote_copy(
                    src_ref=slot, dst_ref=slot, send_sem=fsend.at[3 * i + j], recv_sem=frecv.at[3 * i + j],
                    device_id=sib, device_id_type=MESH)
                cp.start()
                passed.append(cp)
        for i in range(na):
            rh = ins[i].shape[0] // 2
            theirs = pl.ds((1 - c) * rh, rh)
            for j, (px, py) in enumerate(chips):
                slot = outs[i].at[2 * px + py, theirs]
                pltpu.make_async_remote_copy(
                    src_ref=slot, dst_ref=slot, send_sem=fsend.at[3 * i + j], recv_sem=frecv.at[3 * i + j],
                    device_id=sib, device_id_type=MESH).wait_recv()
        for j, (px, py) in enumerate(chips):
            slot = conv_out.at[2 * px + py]
            pltpu.make_async_remote_copy(
                src_ref=slot, dst_ref=slot, send_sem=csend.at[j], recv_sem=crecv.at[j],
                device_id=(px, py, c), device_id_type=MESH).wait_recv()
        for cp in first + passed + conv_cp:
            cp.wait_send()

    out_shape = [jax.ShapeDtypeStruct((N_CHIPS,) + s.shape, s.dtype) for s in shards]
    out_shape.append(jax.ShapeDtypeStruct((N_CHIPS,) + conv_shard.shape, conv_shard.dtype))
    res = pl.pallas_call(
        body, name="weight_allgather",
        in_specs=[HBM] * (na + 1), out_specs=[HBM] * (na + 1), out_shape=out_shape,
        scratch_shapes=[pltpu.SemaphoreType.DMA((3 * na,)), pltpu.SemaphoreType.DMA((3 * na,)),
                        pltpu.SemaphoreType.DMA((3 * na,)), pltpu.SemaphoreType.DMA((3 * na,)),
                        pltpu.SemaphoreType.DMA((3,)), pltpu.SemaphoreType.DMA((3,))],
    )(*shards, conv_shard)
    my_chip = 2 * lax.axis_index("x") + lax.axis_index("y")
    pick = lambda got, own: [jnp.where(my_chip == k, own, got[k]) for k in range(N_CHIPS)]
    return [pick(g, s) for g, s in zip(res[:na], shards)], pick(res[na], conv_shard)


def _sibling_swap_halves(grads, name):
    na = len(grads)

    def body(*refs):
        ins, outs = refs[:na], refs[na:2 * na]
        send, recv = refs[2 * na:]
        x, y, c, _ = _place()
        sib = (x, y, 1 - c)
        cps = []
        for i in range(na):
            rh = ins[i].shape[1] // 2
            cp = pltpu.make_async_remote_copy(
                src_ref=ins[i].at[:, pl.ds((1 - c) * rh, rh), :], dst_ref=outs[i],
                send_sem=send.at[i], recv_sem=recv.at[i], device_id=sib, device_id_type=MESH)
            cp.start()
            cps.append(cp)
        for cp in cps:
            cp.wait()

    out_shape = [jax.ShapeDtypeStruct((g.shape[0], g.shape[1] // 2, g.shape[2]), g.dtype) for g in grads]
    return pl.pallas_call(
        body, name=name, in_specs=[HBM] * na, out_specs=[HBM] * na, out_shape=out_shape,
        scratch_shapes=[pltpu.SemaphoreType.DMA((na,)), pltpu.SemaphoreType.DMA((na,))],
    )(*grads)


def _sibling_swap_whole(halves):
    na = len(halves)

    def body(*refs):
        ins, outs = refs[:na], refs[na:2 * na]
        send, recv = refs[2 * na:]
        x, y, c, _ = _place()
        cps = []
        for i in range(na):
            cp = pltpu.make_async_remote_copy(
                src_ref=ins[i], dst_ref=outs[i], send_sem=send.at[i], recv_sem=recv.at[i],
                device_id=(x, y, 1 - c), device_id_type=MESH)
            cp.start()
            cps.append(cp)
        for cp in cps:
            cp.wait()

    out_shape = [jax.ShapeDtypeStruct(h.shape, h.dtype) for h in halves]
    return pl.pallas_call(
        body, name="grad_sibling_join", in_specs=[HBM] * na, out_specs=[HBM] * na, out_shape=out_shape,
        scratch_shapes=[pltpu.SemaphoreType.DMA((na,)), pltpu.SemaphoreType.DMA((na,))],
    )(*halves)


SEM = pl.BlockSpec(memory_space=pltpu.SEMAPHORE)
ANY = pl.BlockSpec(memory_space=pl.ANY)
EFFECT = pltpu.SideEffectType.DATAFLOW_SIDE_EFFECTING


def _split_copy_start(name, plan, srcs, lands, after):
    ns, nl = len(srcs), len(lands)

    def body(*refs):
        src_refs, land_refs = refs[:ns], refs[ns:ns + nl]
        send, recv = refs[ns + nl + 1], refs[ns + nl + 2]
        token = refs[-1]
        outgoing, _ = plan(src_refs, land_refs)
        for src, dst, dev, si, ri in outgoing:
            pltpu.make_async_remote_copy(src_ref=src, dst_ref=dst, send_sem=send.at[si], recv_sem=recv.at[ri],
                                         device_id=dev, device_id_type=MESH).start()
        token[...] = jnp.zeros_like(token)

    n_out, n_in = plan.counts
    thru = [pltpu.HBM(a.shape, a.dtype) for a in list(srcs) + list(lands)]
    res = pl.pallas_call(
        body, name=name,
        out_shape=[pltpu.SemaphoreType.DMA((n_out,)), pltpu.SemaphoreType.DMA((n_in,))] + thru
        + [jax.ShapeDtypeStruct((8, 128), F32)],
        in_specs=[HBM] * (ns + nl) + [ANY],
        out_specs=[SEM, SEM] + [HBM] * (ns + nl) + [pl.BlockSpec(memory_space=pltpu.VMEM)],
        input_output_aliases={i: 2 + i for i in range(ns + nl)},
        compiler_params=pltpu.CompilerParams(has_side_effects=EFFECT),
    )(*[pltpu.with_memory_space_constraint(a, pltpu.HBM) for a in list(srcs) + list(lands)], after)
    return res[0], res[1], res[2:2 + ns], res[2 + ns:2 + ns + nl], res[-1]


def _split_copy_wait(name, plan, send, recv, srcs, lands, after):
    ns, nl = len(srcs), len(lands)

    def body(*refs):
        src_refs, land_refs = refs[:ns], refs[ns:ns + nl]
        send_ref, recv_ref = refs[ns + nl], refs[ns + nl + 1]
        outgoing, arrivals = plan(src_refs, land_refs)
        for src, dst, dev, si, ri in outgoing:
            pltpu.make_async_remote_copy(src_ref=src, dst_ref=dst, send_sem=send_ref.at[si], recv_sem=recv_ref.at[ri],
                                         device_id=dev, device_id_type=MESH).wait_send()
        for view, ri in arrivals:
            pltpu.make_async_remote_copy(src_ref=view, dst_ref=view, send_sem=send_ref.at[0], recv_sem=recv_ref.at[ri],
                                         device_id=_place()[:3], device_id_type=MESH).wait_recv()

    thru = [pltpu.HBM(a.shape, a.dtype) for a in list(srcs) + list(lands)]
    res = pl.pallas_call(
        body, name=name, out_shape=thru,
        in_specs=[HBM] * (ns + nl) + [SEM, SEM, ANY], out_specs=[HBM] * (ns + nl),
        input_output_aliases={i: i for i in range(ns + nl)},
        compiler_params=pltpu.CompilerParams(has_side_effects=EFFECT),
    )(*srcs, *lands, send, recv, after)
    return res[:ns], res[ns:]


def _gather_plan(n_arrays):
    def plan(src_refs, land_refs):
        x, y, c, chips = _place()
        me = 2 * x + y
        outgoing, arrivals = [], []
        for i in range(n_arrays):
            rh = src_refs[i].shape[0] // 2
            mine = pl.ds(c * rh, rh)
            for j, (px, py) in enumerate(chips):
                for delta in range(2):
                    tc = c ^ delta
                    outgoing.append((src_refs[i].at[mine], land_refs[i].at[me, mine], (px, py, tc),
                                     6 * i + 2 * j + delta, 6 * i + 2 * j + delta))
                    theirs = pl.ds(tc * rh, rh)
                    arrivals.append((land_refs[i].at[2 * px + py, theirs], 6 * i + 2 * j + delta))
        return outgoing, arrivals

    plan.counts = (6 * n_arrays, 6 * n_arrays)
    return plan


def _exchange_plan(n_arrays):
    def plan(src_refs, land_refs):
        x, y, c, chips = _place()
        outgoing, arrivals = [], []
        for i in range(n_arrays):
            for j, (px, py) in enumerate(chips):
                outgoing.append((src_refs[i].at[2 * px + py], land_refs[i].at[j], (px, py, c), 3 * i + j, 3 * i + j))
                arrivals.append((land_refs[i].at[j], 3 * i + j))
        return outgoing, arrivals

    plan.counts = (3 * n_arrays, 3 * n_arrays)
    return plan


def _small_allreduce(vec):
    r, cdim = vec.shape
    n_dev = 8

    def body(v_ref, out_ref, buf, send, recv):
        x, y, c, _ = _place()
        me = 4 * x + 2 * y + c
        buf[me] = v_ref[...]
        cps = []
        for k in range(1, n_dev):
            dx, dy, dc = (k >> 2) & 1, (k >> 1) & 1, k & 1
            peer = (x ^ dx, y ^ dy, c ^ dc)
            cp = pltpu.make_async_remote_copy(
                src_ref=v_ref, dst_ref=buf.at[me], send_sem=send.at[k - 1], recv_sem=recv.at[k - 1],
                device_id=peer, device_id_type=MESH)
            cp.start()
            cps.append(cp)
        for k in range(1, n_dev):
            dx, dy, dc = (k >> 2) & 1, (k >> 1) & 1, k & 1
            src = 4 * (x ^ dx) + 2 * (y ^ dy) + (c ^ dc)
            slot = buf.at[src]
            pltpu.make_async_remote_copy(
                src_ref=slot, dst_ref=slot, send_sem=send.at[k - 1], recv_sem=recv.at[k - 1],
                device_id=(x ^ dx, y ^ dy, c ^ dc), device_id_type=MESH).wait_recv()
        for cp in cps:
            cp.wait_send()
        acc = buf[0]
        for k in range(1, n_dev):
            acc = acc + buf[k]
        out_ref[...] = acc

    vm = pl.BlockSpec(memory_space=pltpu.VMEM)
    return pl.pallas_call(
        body, name="small_allreduce", in_specs=[vm], out_specs=vm,
        out_shape=jax.ShapeDtypeStruct((r, cdim), F32),
        scratch_shapes=[pltpu.VMEM((n_dev, r, cdim), F32), pltpu.SemaphoreType.DMA((n_dev - 1,)),
                        pltpu.SemaphoreType.DMA((n_dev - 1,))],
    )(vec)


def _a_cols_to_head_major(w):
    lead = w.shape[:-1]
    q = w[..., :A_QK].reshape(lead + (A_HEADS, A_DK))
    k = w[..., A_QK:2 * A_QK].reshape(lead + (A_HEADS, A_DK))
    v = w[..., 2 * A_QK:2 * A_QK + A_VW].reshape(lead + (A_HEADS, A_DV))
    z = w[..., 2 * A_QK + A_VW:].reshape(lead + (A_HEADS, A_DV))
    return jnp.concatenate([q, k, v, z], axis=-1).reshape(lead + (A_HEADS * A_HEAD_COLS,))


def _a_cols_from_head_major(w):
    lead = w.shape[:-1]
    w = w.reshape(lead + (A_HEADS, A_HEAD_COLS))
    parts = [w[..., :A_DK], w[..., A_DK:2 * A_DK], w[..., 2 * A_DK:2 * A_DK + A_DV], w[..., 2 * A_DK + A_DV:]]
    return jnp.concatenate([p.reshape(lead + (-1,)) for p in parts], axis=-1)


def _conv_cols_to_head_major(w):
    lead = w.shape[:-1]
    q = w[..., :A_QK].reshape(lead + (A_HEADS, A_DK))
    k = w[..., A_QK:2 * A_QK].reshape(lead + (A_HEADS, A_DK))
    v = w[..., 2 * A_QK:].reshape(lead + (A_HEADS, A_DV))
    return jnp.concatenate([q, k, v], axis=-1).reshape(lead + (A_HEADS * A_CONV_COLS,))


def _conv_cols_from_head_major(w):
    lead = w.shape[:-1]
    w = w.reshape(lead + (A_HEADS, A_CONV_COLS))
    parts = [w[..., :A_DK], w[..., A_DK:2 * A_DK], w[..., 2 * A_DK:]]
    return jnp.concatenate([p.reshape(lead + (-1,)) for p in parts], axis=-1)


def _to_stream(a, bn, d):
    rest = a.shape[1:]
    s = a.shape[0] // bn
    a = a.reshape((bn, s // d, d) + rest)
    a = jnp.swapaxes(a, 1, 2)
    return a.reshape((bn * d, s // d) + rest)


def _from_stream(a, bn, d):
    rest = a.shape[2:]
    ln = a.shape[1]
    a = a.reshape((bn, d, ln) + rest)
    a = jnp.swapaxes(a, 1, 2)
    return a.reshape((bn * ln * d,) + rest)


B_SUB = 512
B_SHARD_BLOCKS = (3 * B_GROUPS * B_W + B_W) // N_CHIPS // B_SUB


def _b_block(gi, jj):
    nb = (B_GROUPS * (jj // 2) + gi) * 2 + jj % 2
    return nb // B_SHARD_BLOCKS, nb % B_SHARD_BLOCKS


def _shard_major(g, ncols):
    r = g.shape[0]
    return jnp.swapaxes(g.reshape(r, N_CHIPS, ncols), 0, 1)


def _pack_rows(items):
    rows, offs = [], []
    at = 0
    for a in items:
        flat = a.reshape(-1).astype(F32)
        nr = -(-flat.shape[0] // 1024) * 8
        flat = jnp.pad(flat, (0, nr * 128 - flat.shape[0]))
        rows.append(flat.reshape(nr, 128))
        offs.append((at, nr, a.shape))
        at += nr
    return jnp.concatenate(rows, axis=0), offs


def _unpack_rows(packed, offs):
    out = []
    for at, nr, shape in offs:
        size = int(np.prod(shape)) if len(shape) else 1
        out.append(packed[at:at + nr].reshape(-1)[:size].reshape(shape))
    return out


def _local_step(x, positions, loss_target, norm_g, wa_in, conv_w, a_log, a_dt_bias, a_norm_g,
                b_q_norm_g, b_k_norm_g, start_token, late_weights, b_grads_ready, a_grads_ready):
    bn, s, d = x.shape
    t = bn * s
    n_chunks = s // A_CHUNK
    wa_main = _a_cols_to_head_major(wa_in[:, :A_MAIN])
    wa_tail = jnp.pad(wa_in[:, A_MAIN:], ((0, 0), (0, 128 - 2 * A_HEADS)))
    cw_hm = _conv_cols_to_head_major(conv_w)

    x0 = x.reshape(t, d)
    h0 = _rms_fwd(x0, norm_g[0:1] + start_token, "rms0_fwd")
    proj_a = _matmul(h0, wa_main, "nn", F32, "a_in_main")
    tail_a = _matmul(h0, wa_tail, "nn", F32, "a_in_tail")
    tail_t = jnp.swapaxes(tail_a[:, :2 * A_HEADS].reshape(bn, s, 2 * A_HEADS), 1, 2)
    tail_t = tail_t.reshape(bn, 2 * A_HEADS, n_chunks, A_CHUNK)
    beta, gc = _gdn_prep(tail_t, a_log[0], a_dt_bias[0])
    proj_a3 = proj_a.reshape(bn, s, A_MAIN)
    og_a, oraw_a, states, t_mats, conv_y = _gdn_fwd(proj_a3, cw_hm, beta, gc, a_norm_g)
    wa_out, wb_in, wb_out = late_weights(og_a)
    b_cols = [4 * B_W] + [3 * B_W] * (B_GROUPS - 1)
    x1 = _matmul(og_a.reshape(t, A_VW), wa_out, "nn", F32, "a_out", res=x0, tk=2048)

    h1 = _rms_fwd(x1, norm_g[1:2], "rms1_fwd")
    inv_freq = ROPE_THETA ** (-jnp.arange(0, ROPE_DIMS, 2, dtype=F32) / ROPE_DIMS)
    freq_row = jnp.concatenate([inv_freq, inv_freq, jnp.zeros((128 - ROPE_DIMS,), F32)]).reshape(1, 128)
    posf = jnp.broadcast_to(positions.astype(F32).reshape(t, 1), (t, 128))
    tabs = _rope_tables(posf, freq_row)
    h1_s, tabs_s, proj_b, qkv_b, o_b, lse_b = [], [], [], [], [], []
    for gi, dil in enumerate(B_DIL):
        hs = h1 if dil == 1 else _to_stream(h1, bn, dil).reshape(t, d)
        ts = tabs if dil == 1 else [_to_stream(tb, bn, dil).reshape(t, 128) for tb in tabs]
        pj = _matmul(hs, wb_in, "nn", F32, f"b_in_g{gi}", tm=2048, tn=B_SUB, n=b_cols[gi], b_spec=pl.BlockSpec(
            (None, d, B_SUB), lambda i, j, kk, gi=gi: (_b_block(gi, j)[0], kk, _b_block(gi, j)[1])))
        qkv = _qk_prep(pj, *ts, b_q_norm_g[0, gi:gi + 1], b_k_norm_g[0, gi:gi + 1], f"qk_prep_g{gi}")
        o_s, lse_s = _attn_fwd(qkv.reshape(bn * dil, s // dil, 3 * B_W), f"attn_fwd_g{gi}")
        h1_s.append(hs), tabs_s.append(ts), proj_b.append(pj), qkv_b.append(qkv)
        o_b.append(o_s.reshape(t, B_W) if dil == 1 else _from_stream(o_s, bn, dil))
        lse_b.append(lse_s.reshape(t, B_HEADS) if dil == 1 else _from_stream(lse_s, bn, dil))
    og_b = _merge_fwd(o_b, lse_b, proj_b[0])
    x2 = _matmul(og_b, wb_out, "nn", F32, "b_out", res=x1)

    d_x2, loss_parts = _loss_grad(x2, loss_target.reshape(t, d))
    loss_local = jnp.sum(loss_parts)

    d_x2b = d_x2.astype(BF16)
    g_wb_out = _matmul(og_b, d_x2b, "tn", F32, "b_out_dw")
    d_og_b = _matmul(d_x2b, wb_out, "nt", F32, "b_out_dx")
    d_o, lse_joint, delta, d_z = _merge_bwd(o_b, lse_b, proj_b[0], d_og_b)
    d_h1, g_qn, g_kn = [], [], []
    g_wb_in = lax.empty(wb_in.shape, F32)
    for gi, dil in enumerate(B_DIL):
        if dil == 1:
            do_s, lj_s, dl_s = d_o, lse_joint, delta
        else:
            do_s, lj_s, dl_s = (_to_stream(a, bn, dil).reshape(t, -1) for a in (d_o, lse_joint, delta))
        ns, ln = bn * dil, s // dil
        dq, dk, dv = _attn_bwd(qkv_b[gi].reshape(ns, ln, 3 * B_W), do_s.reshape(ns, ln, B_W),
                               lj_s.reshape(ns, ln, B_HEADS), dl_s.reshape(ns, ln, B_HEADS), f"attn_bwd_g{gi}")
        d_pj, d_gain = _qk_prep_bwd(proj_b[gi], *tabs_s[gi], b_q_norm_g[0, gi:gi + 1], b_k_norm_g[0, gi:gi + 1],
                                    dq.reshape(t, B_W), dk.reshape(t, B_W), dv.reshape(t, B_W),
                                    d_z if gi == 0 else None, f"qk_prep_bwd_g{gi}")
        g_wb_in = _matmul(h1_s[gi], d_pj, "tn", F32, f"b_in_dw_g{gi}", tn=B_SUB, tk=2048, into=(g_wb_in, pl.BlockSpec(
            (None, d, B_SUB), lambda i, j, kk, gi=gi: (_b_block(gi, j)[0], i, _b_block(gi, j)[1]))))
        dh = _matmul(d_pj, wb_in, "nt", F32, f"b_in_dx_g{gi}", tm=2048, tk=B_SUB, n=d, b_spec=pl.BlockSpec(
            (None, d, B_SUB), lambda i, j, kk, gi=gi: (_b_block(gi, kk)[0], j, _b_block(gi, kk)[1])))
        d_h1.append(dh if dil == 1 else _from_stream(dh.reshape(ns, ln, d), bn, dil))
        g_qn.append(d_gain[0]), g_kn.append(d_gain[1])
    d_x1, g_norm1 = _rms_bwd(x1, norm_g[1:2], d_h1, d_x2, "rms1_bwd")

    d_x1b = d_x1.astype(BF16)
    g_wa_out = _matmul(og_a.reshape(t, A_VW), d_x1b, "tn", F32, "a_out_dw")
    b_token = b_grads_ready(g_wb_in, g_wb_out, g_wa_out)
    d_og_a = _matmul(d_x1b, wa_out, "nt", F32, "a_out_dx")
    d_pa, d_gc, d_beta, d_cw, d_ng = _gdn_bwd(proj_a3, cw_hm, beta, gc, a_norm_g + b_token, oraw_a, states,
                                              t_mats, conv_y, d_og_a.reshape(bn, s, A_VW))
    d_tail_t, d_alog, d_dtb = _gdn_prep_bwd(tail_t, a_log[0], a_dt_bias[0], d_gc, d_beta)
    d_tail = jnp.swapaxes(d_tail_t.reshape(bn, 2 * A_HEADS, s), 1, 2).reshape(t, 2 * A_HEADS)
    d_tail = jnp.pad(d_tail, ((0, 0), (0, 128 - 2 * A_HEADS))).astype(BF16)
    d_pa = d_pa.reshape(t, A_MAIN)
    g_wa_main = _matmul(h0, d_pa, "tn", F32, "a_in_dw_main")
    g_wa_tail = _matmul(h0, d_tail, "tn", F32, "a_in_dw_tail")
    g_wa_in = jnp.concatenate([_a_cols_from_head_major(g_wa_main), g_wa_tail[:, :2 * A_HEADS]], axis=1)
    a_token = a_grads_ready(g_wa_in)
    d_h0 = _matmul(d_pa, wa_main, "nt", F32, "a_in_dx_main")
    d_h0t = _matmul(d_tail + a_token.astype(BF16), wa_tail, "nt", F32, "a_in_dx_tail")
    d_x0, g_norm0 = _rms_bwd(x0, norm_g[0:1], [d_h0, d_h0t], d_x1, "rms0_bwd")

    gfull = {
        "norm_g": jnp.concatenate([g_norm0, g_norm1], axis=0), "a_w_in": g_wa_in,
        "a_conv_w": _conv_cols_from_head_major(jnp.sum(d_cw, axis=0)),
        "a_log": jnp.sum(d_alog[:, :, 0], axis=0), "a_dt_bias": jnp.sum(d_dtb[:, :, 0], axis=0),
        "a_norm_g": jnp.sum(d_ng[:, :, 0, :], axis=(0, 1)), "a_w_out": g_wa_out, "b_w_in": g_wb_in,
        "b_q_norm_g": jnp.stack(g_qn), "b_k_norm_g": jnp.stack(g_kn), "b_w_out": g_wb_out}
    return loss_local, d_x0.reshape(bn, s, d), gfull


def kernel(x, positions, norm_g, a_w_in, a_conv_w, a_log, a_dt_bias, a_norm_g, a_w_out, b_w_in, b_q_norm_g, b_k_norm_g, b_w_out, loss_target, m_norm_g, m_a_w_in, m_a_conv_w, m_a_log, m_a_dt_bias, m_a_norm_g, m_a_w_out, m_b_w_in, m_b_q_norm_g, m_b_k_norm_g, m_b_w_out, v_norm_g, v_a_w_in, v_a_conv_w, v_a_log, v_a_dt_bias, v_a_norm_g, v_a_w_out, v_b_w_in, v_b_q_norm_g, v_b_k_norm_g, v_b_w_out):
    d = x.shape[2]
    my_c = lax.axis_index("c")
    my_chip = 2 * lax.axis_index("x") + lax.axis_index("y")

    half_index = jnp.reshape(my_c, (1,)).astype(jnp.int32)
    chip_index = jnp.reshape(my_chip, (1,)).astype(jnp.int32)
    (ga_in,), g_conv = _weight_allgather([a_w_in[0].astype(BF16)], a_conv_w[0])
    wa_in = jnp.concatenate(ga_in, axis=1)
    conv_w = jnp.concatenate(g_conv, axis=1)

    late_shards = [a_w_out[0].astype(BF16), b_w_in[0].astype(BF16), b_w_out[0].astype(BF16)]
    late_lands = [lax.dynamic_update_slice(lax.empty((N_CHIPS,) + s.shape, BF16), s[None], (my_chip, 0, 0))
                  for s in late_shards]
    gather = _gather_plan(len(late_shards))
    ag_send, ag_recv, ag_srcs, ag_lands, ag_token = _split_copy_start(
        "late_weights_start", gather, late_shards, late_lands, conv_w)

    def late_weights(after):
        _, (ga_out, gb_in, gb_out) = _split_copy_wait(
            "late_weights_wait", gather, ag_send, ag_recv, ag_srcs, ag_lands, after)
        return ga_out.reshape(A_VW, d), gb_in, gb_out.reshape(B_W, d)

    def reduce_to_chip_sums(mats, tag):
        recv_sib = _sibling_swap_halves([g.astype(BF16) for g in mats], f"grad_{tag}_sibling_swap")
        return [_pair_sum(g, r, half_index, f"grad_{tag}_pair_sum_{i}") for i, (g, r) in enumerate(zip(mats, recv_sib))]

    pending = {}

    def start_exchange(tag, mats):
        sums = reduce_to_chip_sums(mats, tag)
        lands = [lax.empty((N_CHIPS - 1,) + s.shape[1:], BF16) for s in sums]
        plan = _exchange_plan(len(mats))
        pending[tag] = (plan,) + tuple(_split_copy_start(f"grad_{tag}_exchange_start", plan, sums, lands, chip_index))
        return pending[tag][5][0, 0]

    def finish_exchange(tag, after):
        plan, send, recv, srcs, lands, _ = pending[tag]
        return _split_copy_wait(f"grad_{tag}_exchange_wait", plan, send, recv, srcs, lands, after)

    def b_grads_ready(g_wb_in, g_wb_out, g_wa_out):
        return start_exchange("b", [g_wb_in, g_wb_out.reshape(N_CHIPS, -1, d), g_wa_out.reshape(N_CHIPS, -1, d)])

    def a_grads_ready(g_wa_in):
        return start_exchange("a", [_shard_major(g_wa_in, a_w_in.shape[2])])

    loss_local, d_x0, gfull = _local_step(x, positions, loss_target, norm_g, wa_in, conv_w, a_log, a_dt_bias,
                                          a_norm_g, b_q_norm_g, b_k_norm_g, ag_token[0, 0], late_weights,
                                          b_grads_ready, a_grads_ready)

    small = [gfull["norm_g"], gfull["a_conv_w"], gfull["a_log"], gfull["a_dt_bias"], gfull["a_norm_g"],
             gfull["b_q_norm_g"], gfull["b_k_norm_g"], loss_local]
    packed, offs = _pack_rows(small)
    reduced = _small_allreduce(packed)
    g_norm, g_conv_all, g_alog, g_dtb, g_ang, g_q, g_k, loss = _unpack_rows(reduced, offs)
    g_conv_mine = lax.dynamic_slice_in_dim(g_conv_all, my_chip * a_conv_w.shape[2], a_conv_w.shape[2], axis=1)

    b_sums, b_received = finish_exchange("b", d_x0)
    a_sums, a_received = finish_exchange("a", reduced)
    chip_sums = [a_sums[0], b_sums[2], b_sums[0], b_sums[1]]
    received = [a_received[0], b_received[2], b_received[0], b_received[1]]
    halves = [_chip_sum(s, r, chip_index, f"grad_chip_sum_{i}") for i, (s, r) in enumerate(zip(chip_sums, received))]
    theirs = _sibling_swap_whole(halves)
    big = ("a_w_in", "a_w_out", "b_w_in", "b_w_out")
    big_halves = dict(zip(big, zip(halves, theirs)))

    grads = {
        "norm_g": g_norm, "a_conv_w": g_conv_mine[None], "a_log": g_alog[None], "a_dt_bias": g_dtb[None],
        "a_norm_g": g_ang[None], "b_q_norm_g": g_q[None], "b_k_norm_g": g_k[None]}
    weights = {"norm_g": norm_g, "a_w_in": a_w_in, "a_conv_w": a_conv_w, "a_log": a_log, "a_dt_bias": a_dt_bias,
               "a_norm_g": a_norm_g, "a_w_out": a_w_out, "b_w_in": b_w_in, "b_q_norm_g": b_q_norm_g,
               "b_k_norm_g": b_k_norm_g, "b_w_out": b_w_out}
    m_in = {"norm_g": m_norm_g, "a_w_in": m_a_w_in, "a_conv_w": m_a_conv_w, "a_log": m_a_log,
            "a_dt_bias": m_a_dt_bias, "a_norm_g": m_a_norm_g, "a_w_out": m_a_w_out, "b_w_in": m_b_w_in,
            "b_q_norm_g": m_b_q_norm_g, "b_k_norm_g": m_b_k_norm_g, "b_w_out": m_b_w_out}
    v_in = {"norm_g": v_norm_g, "a_w_in": v_a_w_in, "a_conv_w": v_a_conv_w, "a_log": v_a_log,
            "a_dt_bias": v_a_dt_bias, "a_norm_g": v_a_norm_g, "a_w_out": v_a_w_out, "b_w_in": v_b_w_in,
            "b_q_norm_g": v_b_q_norm_g, "b_k_norm_g": v_b_k_norm_g, "b_w_out": v_b_w_out}
    names = list(weights)

    delta_w, new_m, new_v = {}, {}, {}
    for nm in big:
        mine, other = big_halves[nm]
        if weights[nm].shape[2] % 128:
            cols = lambda a: jnp.transpose(a, (2, 0, 1))
            half_cols = lambda a: jnp.transpose(a)[:, None, :]
            outs = _adamw_shard_cols(cols(weights[nm]), half_cols(mine), half_cols(other), cols(m_in[nm]),
                                     cols(v_in[nm]), half_index, f"adamw_{nm}")
            outs = [jnp.transpose(o, (1, 2, 0)) for o in outs]
        else:
            outs = _adamw_shard(weights[nm], mine, other, m_in[nm], v_in[nm], half_index, f"adamw_{nm}")
        grads[nm], delta_w[nm], new_m[nm], new_v[nm] = outs
    small_names = [nm for nm in names if nm not in big]
    packs = [_pack_rows([src[nm] for nm in small_names]) for src in (weights, grads, m_in, v_in)]
    offs = packs[0][1]
    dl, m2, v2 = _adamw(packs[0][0], packs[1][0], packs[2][0], packs[3][0], "adamw_small")
    for nm, a, b, c2 in zip(small_names, _unpack_rows(dl, offs), _unpack_rows(m2, offs), _unpack_rows(v2, offs)):
        delta_w[nm], new_m[nm], new_v[nm] = a, b, c2

    return (loss, d_x0, *[grads[nm] for nm in names], *[delta_w[nm] for nm in names],
            *[new_m[nm] for nm in names], *[new_v[nm] for nm in names])
```

```python
import jax
import jax.numpy as jnp
import numpy as np
from jax import lax
from jax.experimental import pallas as pl
from jax.experimental.pallas import tpu as pltpu

F32 = jnp.float32
BF16 = jnp.bfloat16
MESH = pl.DeviceIdType.MESH

EPS = 1e-6
D_MODEL = 1024
A_HEADS = 8
A_DK = 128
A_DV = 256
A_QK = A_HEADS * A_DK
A_VW = A_HEADS * A_DV
A_MAIN = 2 * A_QK + 2 * A_VW
A_HEAD_COLS = 2 * A_DK + 2 * A_DV
A_CONV_COLS = 2 * A_DK + A_DV
A_CHUNK = 64
A_CONV = 4
B_GROUPS = 3
B_HEADS = 8
B_DH = 128
B_W = B_HEADS * B_DH
B_DIL = (1, 4, 16)
B_BLK = 128
ROPE_THETA = 500000.0
ROPE_DIMS = B_DH // 4
ADAM_LR, ADAM_B1, ADAM_B2, ADAM_EPS, ADAM_WD, ADAM_STEP = 0.001, 0.9, 0.999, 1e-08, 0.01, 10
N_CHIPS = 4
VMEM_BIG = 56 * 1024 * 1024


def _params(sem=None, vmem=None):
    return pltpu.CompilerParams(dimension_semantics=sem, vmem_limit_bytes=vmem)


def _dot(a, b, ca, cb):
    return lax.dot_general(a.astype(BF16), b.astype(BF16), (((ca,), (cb,)), ((), ())),
                           preferred_element_type=F32)


def _split3(a):
    hi = a.astype(BF16)
    r = a - hi.astype(F32)
    mid = r.astype(BF16)
    lo = (r - mid.astype(F32)).astype(BF16)
    return hi, mid, lo


def _sigmoid(y):
    return 1.0 / (1.0 + jnp.exp(-y))


def _silu(y):
    return y * _sigmoid(y)


def _dsilu(y):
    s = _sigmoid(y)
    return s * (1.0 + y * (1.0 - s))


def _matmul(a, b, mode, out_dtype, name, res=None, tm=1024, tn=1024, tk=1024, n=None, b_spec=None, into=None):
    m, k = a.shape[::-1] if mode == "tn" else a.shape
    if n is None:
        n = b.shape[0] if mode == "nt" else b.shape[1]
    tm, tn, tk = min(tm, m), min(tn, n), min(tk, k)
    assert m % tm == 0 and n % tn == 0 and k % tk == 0, (name, a.shape, b.shape)
    nk = k // tk
    dims = {"nn": ((1,), (0,)), "nt": ((1,), (1,)), "tn": ((0,), (0,))}[mode]

    def body(*refs):
        a_ref, b_ref = refs[0], refs[1]
        r_ref = refs[2] if res is not None else None
        o_ref = refs[2 + (res is not None) + (into is not None)]
        prod = lax.dot_general(a_ref[...], b_ref[...], (dims, ((), ())), preferred_element_type=F32)

        def finish(r):
            if res is not None:
                r = r + r_ref[...]
            o_ref[...] = r.astype(out_dtype)

        if nk == 1:
            finish(prod)
            return
        acc = refs[-1]
        kk = pl.program_id(2)

        @pl.when(kk == 0)
        def _():
            acc[...] = prod

        @pl.when((kk > 0) & (kk < nk - 1))
        def _():
            acc[...] += prod

        @pl.when(kk == nk - 1)
        def _():
            finish(acc[...] + prod)

    a_spec = pl.BlockSpec((tm, tk), lambda i, j, kk: (i, kk))
    if mode == "tn":
        a_spec = pl.BlockSpec((tk, tm), lambda i, j, kk: (kk, i))
    if b_spec is None and mode == "nt":
        b_spec = pl.BlockSpec((tn, tk), lambda i, j, kk: (j, kk))
    elif b_spec is None:
        b_spec = pl.BlockSpec((tk, tn), lambda i, j, kk: (kk, j))
    in_specs = [a_spec, b_spec]
    args = [a, b]
    if res is not None:
        in_specs.append(pl.BlockSpec((tm, tn), lambda i, j, kk: (i, j)))
        args.append(res)
    out_spec = pl.BlockSpec((tm, tn), lambda i, j, kk: (i, j))
    out_shape = jax.ShapeDtypeStruct((m, n), out_dtype)
    aliases = {}
    if into is not None:
        assert res is None
        buf, out_spec = into
        out_shape = jax.ShapeDtypeStruct(buf.shape, buf.dtype)
        in_specs.append(ANY)
        args.append(buf)
        aliases = {2: 0}
    return pl.pallas_call(
        body, name=name, grid=(m // tm, n // tn, nk),
        in_specs=in_specs, out_specs=out_spec, out_shape=out_shape, input_output_aliases=aliases,
        scratch_shapes=[pltpu.VMEM((tm, tn), F32)] if nk > 1 else [],
        compiler_params=_params(("parallel", "parallel", "arbitrary"), 48 * 1024 * 1024),
    )(*args)


def _rms_fwd(x, g, name, tm=256):
    t, d = x.shape

    def body(x_ref, g_ref, h_ref):
        xv = x_ref[...]
        r = lax.rsqrt(jnp.mean(xv * xv, axis=-1, keepdims=True) + EPS)
        h_ref[...] = (xv * r * g_ref[...]).astype(BF16)

    return pl.pallas_call(
        body, name=name, grid=(t // tm,),
        in_specs=[pl.BlockSpec((tm, d), lambda i: (i, 0)), pl.BlockSpec((1, d), lambda i: (0, 0))],
        out_specs=pl.BlockSpec((tm, d), lambda i: (i, 0)),
        out_shape=jax.ShapeDtypeStruct((t, d), BF16),
        compiler_params=_params(("parallel",)),
    )(x, g)


def _rms_bwd(x, g, dhs, dres, name, tm=256):
    t, d = x.shape
    n_dh = len(dhs)

    def body(*refs):
        x_ref, g_ref = refs[0], refs[1]
        dh_refs = refs[2:2 + n_dh]
        dres_ref, dx_ref, dg_ref = refs[2 + n_dh:]
        i = pl.program_id(0)

        @pl.when(i == 0)
        def _():
            dg_ref[...] = jnp.zeros_like(dg_ref)

        xv = x_ref[...]
        r = lax.rsqrt(jnp.mean(xv * xv, axis=-1, keepdims=True) + EPS)
        xh = xv * r
        dh = dh_refs[0][...]
        for ref in dh_refs[1:]:
            dh = dh + ref[...]
        dg_ref[0:1, :] += jnp.sum(dh * xh, axis=0, keepdims=True)
        dxh = dh * g_ref[...]
        dx = r * (dxh - xh * jnp.mean(dxh * xh, axis=-1, keepdims=True))
        dx_ref[...] = dx + dres_ref[...]

    row = pl.BlockSpec((tm, d), lambda i: (i, 0))
    dx, dg = pl.pallas_call(
        body, name=name, grid=(t // tm,),
        in_specs=[row, pl.BlockSpec((1, d), lambda i: (0, 0))] + [row] * n_dh + [row],
        out_specs=[row, pl.BlockSpec((8, d), lambda i: (0, 0))],
        out_shape=[jax.ShapeDtypeStruct((t, d), F32), jax.ShapeDtypeStruct((8, d), F32)],
        compiler_params=_params(("arbitrary",)),
    )(x, g, *dhs, dres)
    return dx, dg[0:1]


def _loss_grad(y, target, name="loss_grad", tm=256):
    t, d = y.shape
    nb = t // tm

    def body(y_ref, t_ref, dy_ref, part_ref):
        e = y_ref[...] - t_ref[...]
        dy_ref[...] = e * (1.0 / d)
        s = jnp.sum(jnp.sum(e * e, axis=1, keepdims=True), axis=0, keepdims=True) * (0.5 / d)
        part_ref[...] = jnp.broadcast_to(s, (8, 128))

    row = pl.BlockSpec((tm, d), lambda i: (i, 0))
    dy, part = pl.pallas_call(
        body, name=name, grid=(nb,), in_specs=[row, row],
        out_specs=[row, pl.BlockSpec((None, 8, 128), lambda i: (i, 0, 0))],
        out_shape=[jax.ShapeDtypeStruct((t, d), F32), jax.ShapeDtypeStruct((nb, 8, 128), F32)],
        compiler_params=_params(("parallel",)),
    )(y, target)
    return dy, part[:, 0, 0]


def _softplus(x):
    t = jnp.exp(-jnp.abs(x))
    return jnp.maximum(x, 0.0) + jnp.where(t < 1e-3, t * (1.0 - 0.5 * t), jnp.log(1.0 + t))


def _tri(rows_le_cols):
    r = lax.broadcasted_iota(jnp.int32, (A_CHUNK, A_CHUNK), 0)
    c = lax.broadcasted_iota(jnp.int32, (A_CHUNK, A_CHUNK), 1)
    return jnp.where((r <= c) if rows_le_cols else (r >= c), 1.0, 0.0).astype(BF16)


def _dot_exact_rhs(a, ones_bf16):
    dn = (((1,), (0,)), ((), ()))
    hi, mid, lo = _split3(a)
    out = lax.dot_general(hi, ones_bf16, dn, preferred_element_type=F32)
    out = out + lax.dot_general(mid, ones_bf16, dn, preferred_element_type=F32)
    return out + lax.dot_general(lo, ones_bf16, dn, preferred_element_type=F32)


def _gdn_prep(tail_t, a_log, dt_bias):
    bn, _, n, c = tail_t.shape

    def body(t_ref, alog_ref, dtb_ref, beta_ref, gc_ref):
        upper = _tri(True)
        for h in range(A_HEADS):
            beta_ref[h] = _sigmoid(t_ref[h])
            ea = jnp.exp(jnp.full((n, c), alog_ref[h], F32))
            g = -ea * _softplus(t_ref[A_HEADS + h] + dtb_ref[h])
            gc_ref[h] = _dot_exact_rhs(g, upper)

    smem = pl.BlockSpec(memory_space=pltpu.SMEM)
    blk = pl.BlockSpec((None, A_HEADS, n, c), lambda b: (b, 0, 0, 0))
    return pl.pallas_call(
        body, name="gdn_prep", grid=(bn,),
        in_specs=[pl.BlockSpec((None, 2 * A_HEADS, n, c), lambda b: (b, 0, 0, 0)), smem, smem],
        out_specs=[blk, blk],
        out_shape=[jax.ShapeDtypeStruct((bn, A_HEADS, n, c), F32)] * 2,
        compiler_params=_params(("parallel",)),
    )(tail_t, a_log, dt_bias)


def _gdn_prep_bwd(tail_t, a_log, dt_bias, d_gc, d_beta):
    bn, _, n, c = tail_t.shape

    def body(t_ref, alog_ref, dtb_ref, dgc_ref, dbeta_ref, dt_ref, dal_ref, ddt_ref):
        lower = _tri(False)
        for h in range(A_HEADS):
            beta = _sigmoid(t_ref[h])
            dt_ref[h] = dbeta_ref[h] * beta * (1.0 - beta)
            dg = _dot_exact_rhs(dgc_ref[h], lower)
            ea = jnp.exp(jnp.full((n, c), alog_ref[h], F32))
            xa = t_ref[A_HEADS + h] + dtb_ref[h]
            g = -ea * _softplus(xa)
            dxa = -ea * dg * _sigmoid(xa)
            dt_ref[A_HEADS + h] = dxa
            s1 = jnp.sum(jnp.sum(g * dg, axis=1, keepdims=True), axis=0, keepdims=True)
            s2 = jnp.sum(jnp.sum(dxa, axis=1, keepdims=True), axis=0, keepdims=True)
            dal_ref[h:h + 1, :] = jnp.broadcast_to(s1, (1, 128))
            ddt_ref[h:h + 1, :] = jnp.broadcast_to(s2, (1, 128))

    smem = pl.BlockSpec(memory_space=pltpu.SMEM)
    blk8 = pl.BlockSpec((None, A_HEADS, n, c), lambda b: (b, 0, 0, 0))
    blk16 = pl.BlockSpec((None, 2 * A_HEADS, n, c), lambda b: (b, 0, 0, 0))
    sm = pl.BlockSpec((None, A_HEADS, 128), lambda b: (b, 0, 0))
    return pl.pallas_call(
        body, name="gdn_prep_bwd", grid=(bn,),
        in_specs=[blk16, smem, smem, blk8, blk8],
        out_specs=[blk16, sm, sm],
        out_shape=[jax.ShapeDtypeStruct((bn, 2 * A_HEADS, n, c), F32),
                   jax.ShapeDtypeStruct((bn, A_HEADS, 128), F32),
                   jax.ShapeDtypeStruct((bn, A_HEADS, 128), F32)],
        compiler_params=_params(("parallel",)),
    )(tail_t, a_log, dt_bias, d_gc, d_beta)


HALO = 8


def _conv_taps(xw, w):
    y = w[A_CONV - 1:A_CONV, :] * xw
    for j in range(1, A_CONV):
        y = y + w[A_CONV - 1 - j:A_CONV - j, :] * pltpu.roll(xw, j, 0)
    return y[HALO:, :]


def _row_to_col(row, eye):
    c = eye.shape[0]
    return jnp.sum(jnp.where(eye, jnp.broadcast_to(row, (c, c)), 0.0), axis=1, keepdims=True)


def _col_to_row(col, eye):
    c = eye.shape[0]
    return jnp.sum(jnp.where(eye, jnp.broadcast_to(col, (c, c)), 0.0), axis=0, keepdims=True)


def _unit_lower_inverse(a, ri, ci):
    eye = jnp.where(ri == ci, 1.0, 0.0)
    a8 = jnp.where((ri >> 3) == (ci >> 3), a, 0.0)
    a2 = _dot(a8, a8, 1, 0)
    yield
    a4 = _dot(a2, a2, 1, 0)
    t = eye - a8
    t = t + _dot(t, a2, 1, 0)
    yield
    t = t + _dot(t, a4, 1, 0)
    yield
    for sh in (3, 4, 5):
        off = jnp.where(((ri >> (sh + 1)) == (ci >> (sh + 1))) & ((ri >> sh) != (ci >> sh)), a, 0.0)
        left = _dot(t, off, 1, 0)
        yield
        t = t - _dot(left, t, 1, 0)
        yield
    return t


def _round_robin(gens):
    live = list(gens)
    while live:
        nxt = []
        for g in live:
            try:
                next(g)
                nxt.append(g)
            except StopIteration:
                pass
        live = nxt


def _gdn_chunk_core(q, k, v, g_row, b_row, t_mat, ri, ci):
    eye = ri == ci
    g_col = _row_to_col(g_row, eye)
    b_col = _row_to_col(b_row, eye)
    causal = ri >= ci
    strict = ri > ci
    dec = jnp.where(causal, jnp.exp(jnp.where(causal, g_col - g_row, 0.0)), 0.0)
    gam = jnp.exp(g_col)
    g_last = g_row[:, A_CHUNK - 1:A_CHUNK]
    gam_last = jnp.exp(g_last)
    e = jnp.exp(g_last - g_col)
    kb = k * b_col
    bv = v * b_col
    kbg = kb * gam
    q16, k16, kb16 = q.astype(BF16), k.astype(BF16), kb.astype(BF16)
    kk = _dot(kb16, k16, 1, 1)
    p = _dot(q16, k16, 1, 1) * dec
    yield
    a_mat = jnp.where(strict, kk * dec, 0.0)
    if t_mat is None:
        t_mat = yield from _unit_lower_inverse(a_mat, ri, ci)
    t16 = t_mat.astype(BF16)
    u = _dot(t16, bv, 1, 0)
    w = _dot(t16, kbg, 1, 0)
    yield
    return dict(eye=eye, g_col=g_col, b_col=b_col, dec=dec, strict=strict, causal=causal, gam=gam,
                gam_last=gam_last, e=e, kb=kb, bv=bv, kbg=kbg, a_mat=a_mat, t_mat=t_mat, u=u, w=w, p=p,
                qg=q * gam, kd=k * e, q16=q16, k16=k16, kb16=kb16, t16=t16)


A_SEQ_BLK = 256
A_BLK_CHUNKS = A_SEQ_BLK // A_CHUNK


def _gdn_halo(proj_hm):
    bn, s, w = proj_hm.shape
    last = proj_hm.reshape(bn, s // A_SEQ_BLK, A_SEQ_BLK, w)[:, :, A_SEQ_BLK - HALO:, :]
    return jnp.concatenate([jnp.zeros((bn, 1, HALO, w), proj_hm.dtype), last[:, :-1]], axis=1)


def _gdn_window(x_ref, halo_ref, ci, first, lo):
    if first:
        return jnp.concatenate([halo_ref[:, lo:lo + A_CONV_COLS], x_ref[0:A_CHUNK, lo:lo + A_CONV_COLS]], axis=0)
    start = pl.multiple_of(ci * A_CHUNK - HALO, HALO)
    return x_ref[pl.ds(start, A_CHUNK + HALO), lo:lo + A_CONV_COLS]


def _gdn_chunk_prep(xw, cw, y=None):
    if y is None:
        y = _conv_taps(xw, cw)
    a = _silu(y)
    aq, ak, v = a[:, 0:A_DK], a[:, A_DK:2 * A_DK], a[:, 2 * A_DK:]
    rq = lax.rsqrt(jnp.sum(aq * aq, axis=1, keepdims=True) + EPS)
    rk = lax.rsqrt(jnp.sum(ak * ak, axis=1, keepdims=True) + EPS)
    return dict(xw=xw, y=y, aq=aq, ak=ak, rq=rq, rk=rk, q=aq * rq * (A_DK ** -0.5), k=ak * rk, v=v)


def _gdn_fwd(proj_hm, cw_hm, beta, gc, norm_g, hp=8):
    bn, s, _ = proj_hm.shape
    n = s // A_CHUNK
    nsb = s // A_SEQ_BLK
    halo = _gdn_halo(proj_hm)

    def body(x_ref, halo_ref, cw_ref, beta_ref, gc_ref, ng_ref, og_ref, oraw_ref, st_ref, t_ref, y_ref, state):
        first_chunk = pl.program_id(2) * A_BLK_CHUNKS
        ri = lax.broadcasted_iota(jnp.int32, (A_CHUNK, A_CHUNK), 0)
        ci_ = lax.broadcasted_iota(jnp.int32, (A_CHUNK, A_CHUNK), 1)
        ng = ng_ref[...]

        @pl.when(pl.program_id(2) == 0)
        def _():
            state[...] = jnp.zeros_like(state)

        def one_head(hh, ci, first, rows):
            lo = hh * A_HEAD_COLS
            cw = cw_ref[:, hh * A_CONV_COLS:(hh + 1) * A_CONV_COLS]
            cin = _gdn_chunk_prep(_gdn_window(x_ref, halo_ref, ci, first, lo), cw)
            y_ref[rows, hh * A_CONV_COLS:(hh + 1) * A_CONV_COLS] = cin["y"]
            seq_chunk = pl.ds(first_chunk + ci, 1)
            core = yield from _gdn_chunk_core(cin["q"], cin["k"], cin["v"], gc_ref[hh, seq_chunk, :],
                                              beta_ref[hh, seq_chunk, :], None, ri, ci_)
            st = state[hh]
            st_ref[hh, ci] = st
            t_ref[hh, ci] = core["t_mat"]
            st16 = st.astype(BF16)
            vn = core["u"] - _dot(core["w"], st16, 1, 0)
            qs = _dot(core["qg"], st16, 1, 0)
            yield
            vn16 = vn.astype(BF16)
            o = qs + _dot(core["p"], vn16, 1, 0)
            state[hh] = st * core["gam_last"] + _dot(core["kd"], vn16, 0, 0)
            yield
            ocols = slice(hh * A_DV, (hh + 1) * A_DV)
            oraw_ref[rows, ocols] = o
            r = lax.rsqrt(jnp.mean(o * o, axis=1, keepdims=True) + EPS)
            z = x_ref[rows, lo + A_CONV_COLS:lo + A_HEAD_COLS]
            og_ref[rows, ocols] = (o * r * ng * _silu(z)).astype(BF16)

        def chunk(ci, first):
            rows = pl.ds(0 if first else pl.multiple_of(ci * A_CHUNK, A_CHUNK), A_CHUNK)
            _round_robin([one_head(hh, ci, first, rows) for hh in range(hp)])

        chunk(0, True)
        lax.fori_loop(1, A_BLK_CHUNKS, lambda i, c: (chunk(i, False), c)[1], 0)

    small = pl.BlockSpec((None, hp, n, A_CHUNK), lambda b, h, j: (b, h, 0, 0))
    return pl.pallas_call(
        body, name="gdn_fwd", grid=(bn, A_HEADS // hp, nsb),
        in_specs=[pl.BlockSpec((None, A_SEQ_BLK, hp * A_HEAD_COLS), lambda b, h, j: (b, j, h)),
                  pl.BlockSpec((None, None, HALO, hp * A_HEAD_COLS), lambda b, h, j: (b, j, 0, h)),
                  pl.BlockSpec((A_CONV, hp * A_CONV_COLS), lambda b, h, j: (0, h)),
                  small, small,
                  pl.BlockSpec((1, A_DV), lambda b, h, j: (0, 0))],
        out_specs=[pl.BlockSpec((None, A_SEQ_BLK, hp * A_DV), lambda b, h, j: (b, j, h)),
                   pl.BlockSpec((None, A_SEQ_BLK, hp * A_DV), lambda b, h, j: (b, j, h)),
                   pl.BlockSpec((None, hp, A_BLK_CHUNKS, A_DK, A_DV), lambda b, h, j: (b, h, j, 0, 0)),
                   pl.BlockSpec((None, hp, A_BLK_CHUNKS, A_CHUNK, A_CHUNK), lambda b, h, j: (b, h, j, 0, 0)),
                   pl.BlockSpec((None, A_SEQ_BLK, hp * A_CONV_COLS), lambda b, h, j: (b, j, h))],
        out_shape=[jax.ShapeDtypeStruct((bn, s, A_VW), BF16),
                   jax.ShapeDtypeStruct((bn, s, A_VW), F32),
                   jax.ShapeDtypeStruct((bn, A_HEADS, n, A_DK, A_DV), F32),
                   jax.ShapeDtypeStruct((bn, A_HEADS, n, A_CHUNK, A_CHUNK), F32),
                   jax.ShapeDtypeStruct((bn, s, A_HEADS * A_CONV_COLS), F32)],
        scratch_shapes=[pltpu.VMEM((hp, A_DK, A_DV), F32)],
        compiler_params=_params(("parallel", "parallel", "arbitrary"), VMEM_BIG),
    )(proj_hm, halo, cw_hm, beta, gc, norm_g)


def _gdn_bwd(proj_hm, cw_hm, beta, gc, norm_g, oraw, states, t_mats, conv_y, dog, hp=4):
    bn, s, _ = proj_hm.shape
    n = s // A_CHUNK
    nsb = s // A_SEQ_BLK
    halo = _gdn_halo(proj_hm)

    def body(x_ref, halo_ref, cw_ref, beta_ref, gc_ref, ng_ref, oraw_ref, st_ref, t_ref, y_ref, dog_ref,
             dx_ref, dgc_ref, dbeta_ref, dcw_ref, dng_ref, dstate, dy_next, shifted):
        first_chunk = (nsb - 1 - pl.program_id(2)) * A_BLK_CHUNKS
        ri = lax.broadcasted_iota(jnp.int32, (A_CHUNK, A_CHUNK), 0)
        ci_ = lax.broadcasted_iota(jnp.int32, (A_CHUNK, A_CHUNK), 1)
        lane = lax.broadcasted_iota(jnp.int32, (1, A_CHUNK), 1)
        ng = ng_ref[...]

        @pl.when(pl.program_id(2) == 0)
        def _():
            dstate[...] = jnp.zeros_like(dstate)
            dy_next[...] = jnp.zeros_like(dy_next)
            dcw_ref[...] = jnp.zeros_like(dcw_ref)
            dng_ref[...] = jnp.zeros_like(dng_ref)

        def one_head(hh, ci, first, rows):
            lo = hh * A_HEAD_COLS
            ccols = slice(hh * A_CONV_COLS, (hh + 1) * A_CONV_COLS)
            ocols = slice(hh * A_DV, (hh + 1) * A_DV)
            cw = cw_ref[:, ccols]
            cin = _gdn_chunk_prep(_gdn_window(x_ref, halo_ref, ci, first, lo), cw, y_ref[rows, ccols])
            q, k, v = cin["q"], cin["k"], cin["v"]
            seq_chunk = pl.ds(first_chunk + ci, 1)
            cr = yield from _gdn_chunk_core(q, k, v, gc_ref[hh, seq_chunk, :], beta_ref[hh, seq_chunk, :],
                                            t_ref[hh, ci], ri, ci_)
            eye, dec, gam, e = cr["eye"], cr["dec"], cr["gam"], cr["e"]
            b_col, t_mat, u, w, p = cr["b_col"], cr["t_mat"], cr["u"], cr["w"], cr["p"]
            st = st_ref[hh, ci]
            ds_out = dstate[hh]

            o = oraw_ref[rows, ocols]
            z = x_ref[rows, lo + A_CONV_COLS:lo + A_HEAD_COLS]
            d_og = dog_ref[rows, ocols]
            r = lax.rsqrt(jnp.mean(o * o, axis=1, keepdims=True) + EPS)
            oh = o * r
            d_on = d_og * _silu(z)
            dz = d_og * oh * ng * _dsilu(z)
            dng_ref[hh, 0:1, :] += jnp.sum(d_on * oh, axis=0, keepdims=True)
            d_oh = d_on * ng
            d_o = r * (d_oh - oh * jnp.mean(d_oh * oh, axis=1, keepdims=True))

            st16, ds16, do16, w16 = st.astype(BF16), ds_out.astype(BF16), d_o.astype(BF16), w.astype(BF16)
            q16, k16, t16 = cr["q16"], cr["k16"], cr["t16"]
            vn = u - _dot(w16, st16, 1, 0)
            d_vn = _dot(p, do16, 0, 0) + _dot(cr["kd"], ds16, 1, 0)
            d_qg = _dot(do16, st16, 1, 1)
            qgdo = _dot(cr["qg"], do16, 0, 0)
            yield
            vn16, dvn16 = vn.astype(BF16), d_vn.astype(BF16)
            d_p = jnp.where(cr["causal"], _dot(do16, vn16, 1, 1), 0.0)
            d_kd = _dot(vn16, ds16, 1, 1)
            d_gam_last = jnp.sum(jnp.sum(st * ds_out, axis=1, keepdims=True), axis=0, keepdims=True)
            d_w = -_dot(dvn16, st16, 1, 1)
            dstate[hh] = qgdo + ds_out * cr["gam_last"] - _dot(w16, dvn16, 0, 0)
            d_bv = _dot(t16, dvn16, 0, 0)
            yield
            d_kbg = _dot(t16, d_w, 0, 0)
            n_p = (d_p * dec).astype(BF16)
            d_q = _dot(n_p, k16, 1, 0) + d_qg * gam
            npq = _dot(n_p, q16, 0, 0)
            yield
            d_a = jnp.where(cr["strict"], -(_dot(d_bv, u, 1, 1) + _dot(d_kbg, w16, 1, 1)), 0.0)
            yield
            m_a = (d_a * dec).astype(BF16)
            d_kb = _dot(m_a, k16, 1, 0) + d_kbg * gam
            d_k = (_dot(m_a, cr["kb16"], 0, 0) + npq + d_kd * e + d_kb * b_col)
            yield
            d_v = d_bv * b_col
            d_beta_col = (jnp.sum(d_bv * v, axis=1, keepdims=True)
                          + jnp.sum(d_kb * k, axis=1, keepdims=True))
            gterm = d_a * cr["a_mat"] + d_p * p
            d_e = jnp.sum(d_kd * k, axis=1, keepdims=True) * e
            d_g_col = (jnp.sum(gterm, axis=1, keepdims=True)
                       + (jnp.sum(d_qg * q, axis=1, keepdims=True)
                          + jnp.sum(d_kbg * cr["kb"], axis=1, keepdims=True)) * gam
                       - d_e)
            d_g_last = jnp.sum(d_e, axis=0, keepdims=True) + d_gam_last * cr["gam_last"]
            d_g_row = (_col_to_row(d_g_col, eye) - jnp.sum(gterm, axis=0, keepdims=True)
                       + jnp.where(lane == A_CHUNK - 1, d_g_last, 0.0))
            dgc_ref[hh, seq_chunk, :] = d_g_row
            dbeta_ref[hh, seq_chunk, :] = _col_to_row(d_beta_col, eye)

            qh = cin["aq"] * cin["rq"]
            kh = cin["ak"] * cin["rk"]
            d_qh = d_q * (A_DK ** -0.5)
            d_aq = cin["rq"] * (d_qh - qh * jnp.sum(d_qh * qh, axis=1, keepdims=True))
            d_ak = cin["rk"] * (d_k - kh * jnp.sum(d_k * kh, axis=1, keepdims=True))
            d_y = jnp.concatenate([d_aq, d_ak, d_v], axis=1) * _dsilu(cin["y"])
            shifted[hh, 0, 0:A_CHUNK, :] = d_y
            shifted[hh, 0, A_CHUNK:A_CHUNK + HALO, :] = dy_next[hh]
            shifted[hh, 1, 0:A_CHUNK + HALO, :] = cin["xw"]
            d_x = cw[A_CONV - 1:A_CONV, :] * d_y
            for j in range(1, A_CONV):
                d_x = d_x + cw[A_CONV - 1 - j:A_CONV - j, :] * shifted[hh, 0, j:j + A_CHUNK, :]
            for j in range(A_CONV):
                xs = shifted[hh, 1, HALO - j:HALO - j + A_CHUNK, :]
                dcw_ref[A_CONV - 1 - j:A_CONV - j, ccols] += jnp.sum(d_y * xs, axis=0, keepdims=True)
            dy_next[hh] = d_y[0:HALO, :]
            dx_ref[rows, lo:lo + A_CONV_COLS] = d_x.astype(BF16)
            dx_ref[rows, lo + A_CONV_COLS:lo + A_HEAD_COLS] = dz.astype(BF16)

        def chunk(ci, first):
            rows = pl.ds(0 if first else pl.multiple_of(ci * A_CHUNK, A_CHUNK), A_CHUNK)
            _round_robin([one_head(hh, ci, first, rows) for hh in range(hp)])

        lax.fori_loop(0, A_BLK_CHUNKS - 1, lambda i, c: (chunk(A_BLK_CHUNKS - 1 - i, False), c)[1], 0)
        chunk(0, True)

    rev = lambda j: nsb - 1 - j
    small = pl.BlockSpec((None, hp, n, A_CHUNK), lambda b, h, j: (b, h, 0, 0))
    wide = pl.BlockSpec((None, A_SEQ_BLK, hp * A_HEAD_COLS), lambda b, h, j: (b, rev(j), h))
    val = pl.BlockSpec((None, A_SEQ_BLK, hp * A_DV), lambda b, h, j: (b, rev(j), h))
    return pl.pallas_call(
        body, name="gdn_bwd", grid=(bn, A_HEADS // hp, nsb),
        in_specs=[wide,
                  pl.BlockSpec((None, None, HALO, hp * A_HEAD_COLS), lambda b, h, j: (b, rev(j), 0, h)),
                  pl.BlockSpec((A_CONV, hp * A_CONV_COLS), lambda b, h, j: (0, h)),
                  small, small,
                  pl.BlockSpec((1, A_DV), lambda b, h, j: (0, 0)),
                  val,
                  pl.BlockSpec((None, hp, A_BLK_CHUNKS, A_DK, A_DV), lambda b, h, j: (b, h, rev(j), 0, 0)),
                  pl.BlockSpec((None, hp, A_BLK_CHUNKS, A_CHUNK, A_CHUNK), lambda b, h, j: (b, h, rev(j), 0, 0)),
                  pl.BlockSpec((None, A_SEQ_BLK, hp * A_CONV_COLS), lambda b, h, j: (b, rev(j), h)),
                  val],
        out_specs=[wide, small, small,
                   pl.BlockSpec((None, A_CONV, hp * A_CONV_COLS), lambda b, h, j: (b, 0, h)),
                   pl.BlockSpec((None, hp, 8, A_DV), lambda b, h, j: (b, h, 0, 0))],
        out_shape=[jax.ShapeDtypeStruct((bn, s, A_HEADS * A_HEAD_COLS), BF16),
                   jax.ShapeDtypeStruct((bn, A_HEADS, n, A_CHUNK), F32),
                   jax.ShapeDtypeStruct((bn, A_HEADS, n, A_CHUNK), F32),
                   jax.ShapeDtypeStruct((bn, A_CONV, A_HEADS * A_CONV_COLS), F32),
                   jax.ShapeDtypeStruct((bn, A_HEADS, 8, A_DV), F32)],
        scratch_shapes=[pltpu.VMEM((hp, A_DK, A_DV), F32), pltpu.VMEM((hp, HALO, A_CONV_COLS), F32),
                        pltpu.VMEM((hp, 2, A_CHUNK + 2 * HALO, A_CONV_COLS), F32)],
        compiler_params=_params(("parallel", "parallel", "arbitrary"), VMEM_BIG),
    )(proj_hm, halo, cw_hm, beta, gc, norm_g, oraw, states, t_mats, conv_y, dog)


def _rope_tables(posf, inv_freq_row):
    t = posf.shape[0]
    tm = 512

    def body(p_ref, f_ref, c_ref, sa_ref, sb_ref):
        ang = p_ref[...] * f_ref[...]
        lane = lax.broadcasted_iota(jnp.int32, ang.shape, 1)
        half = ROPE_DIMS // 2
        c_ref[...] = jnp.where(lane < ROPE_DIMS, jnp.cos(ang), 1.0)
        sn = jnp.sin(ang)
        sa_ref[...] = jnp.where(lane < half, -sn, 0.0)
        sb_ref[...] = jnp.where((lane >= half) & (lane < ROPE_DIMS), sn, 0.0)

    row = pl.BlockSpec((tm, 128), lambda i: (i, 0))
    return pl.pallas_call(
        body, name="rope_tables", grid=(t // tm,),
        in_specs=[row, pl.BlockSpec((1, 128), lambda i: (0, 0))], out_specs=[row] * 3,
        out_shape=[jax.ShapeDtypeStruct((t, 128), F32)] * 3,
        compiler_params=_params(("parallel",)),
    )(posf, inv_freq_row)


def _rope(x, c, sa, sb):
    half = ROPE_DIMS // 2
    return x * c + pltpu.roll(x, 128 - half, 1) * sa + pltpu.roll(x, half, 1) * sb


def _rope_t(d, c, sa, sb):
    half = ROPE_DIMS // 2
    return d * c + pltpu.roll(d * sa, half, 1) + pltpu.roll(d * sb, 128 - half, 1)


def _qk_prep(proj, c, sa, sb, qg, kg, name, tm=256):
    t = proj.shape[0]

    def body(x_ref, c_ref, sa_ref, sb_ref, qg_ref, kg_ref, o_ref):
        cc, s1, s2 = c_ref[...], sa_ref[...], sb_ref[...]
        half = ROPE_DIMS // 2

        def one_head(lo, g):
            xv = x_ref[:, lo:lo + B_DH]
            ms = jnp.mean(xv * xv, axis=1, keepdims=True)
            yield
            xn = xv * lax.rsqrt(ms + EPS) * g
            r1, r2 = pltpu.roll(xn, 128 - half, 1), pltpu.roll(xn, half, 1)
            yield
            o_ref[:, lo:lo + B_DH] = (xn * cc + r1 * s1 + r2 * s2).astype(BF16)

        for which, g_ref in ((0, qg_ref), (1, kg_ref)):
            g = g_ref[...]
            _round_robin([one_head(which * B_W + h * B_DH, g) for h in range(B_HEADS)])
        o_ref[:, 2 * B_W:3 * B_W] = x_ref[:, 2 * B_W:3 * B_W].astype(BF16)

    tab = pl.BlockSpec((tm, 128), lambda i: (i, 0))
    gain = pl.BlockSpec((1, B_DH), lambda i: (0, 0))
    return pl.pallas_call(
        body, name=name, grid=(t // tm,),
        in_specs=[pl.BlockSpec((tm, 3 * B_W), lambda i: (i, 0)), tab, tab, tab, gain, gain],
        out_specs=pl.BlockSpec((tm, 3 * B_W), lambda i: (i, 0)),
        out_shape=jax.ShapeDtypeStruct((t, 3 * B_W), BF16),
        compiler_params=_params(("parallel",), 40 * 1024 * 1024),
    )(proj, c, sa, sb, qg, kg)


def _qk_prep_bwd(proj, c, sa, sb, qg, kg, dq, dk, dv, dz, name, tm=256):
    t = proj.shape[0]
    out_w = 3 * B_W + (B_W if dz is not None else 0)

    def body(*refs):
        x_ref, c_ref, sa_ref, sb_ref, qg_ref, kg_ref, dq_ref, dk_ref, dv_ref = refs[:9]
        if dz is not None:
            dz_ref, o_ref, dgain_ref = refs[9:]
        else:
            o_ref, dgain_ref = refs[9:]
        i = pl.program_id(0)

        @pl.when(i == 0)
        def _():
            dgain_ref[...] = jnp.zeros_like(dgain_ref)

        cc, s1, s2 = c_ref[...], sa_ref[...], sb_ref[...]
        half = ROPE_DIMS // 2

        def one_head(which, h, g, d_ref, parts):
            lo = which * B_W + h * B_DH
            xv = x_ref[:, lo:lo + B_DH]
            d_out = d_ref[:, h * B_DH:(h + 1) * B_DH].astype(F32)
            ms = jnp.mean(xv * xv, axis=1, keepdims=True)
            r1, r2 = pltpu.roll(d_out * s1, half, 1), pltpu.roll(d_out * s2, 128 - half, 1)
            yield
            r = lax.rsqrt(ms + EPS)
            xh = xv * r
            d_xn = d_out * cc + r1 + r2
            parts.append(jnp.sum(d_xn * xh, axis=0, keepdims=True))
            d_xh = d_xn * g
            dot = jnp.mean(d_xh * xh, axis=1, keepdims=True)
            yield
            o_ref[:, lo:lo + B_DH] = (r * (d_xh - xh * dot)).astype(BF16)

        for which, g_ref, d_ref in ((0, qg_ref, dq_ref), (1, kg_ref, dk_ref)):
            parts = []
            _round_robin([one_head(which, h, g_ref[...], d_ref, parts) for h in range(B_HEADS)])
            acc = parts[0]
            for part in parts[1:]:
                acc = acc + part
            dgain_ref[which:which + 1, :] += acc
        o_ref[:, 2 * B_W:3 * B_W] = dv_ref[...]
        if dz is not None:
            o_ref[:, 3 * B_W:4 * B_W] = dz_ref[...]

    tab = pl.BlockSpec((tm, 128), lambda i: (i, 0))
    gain = pl.BlockSpec((1, B_DH), lambda i: (0, 0))
    grad = pl.BlockSpec((tm, B_W), lambda i: (i, 0))
    in_specs = [pl.BlockSpec((tm, 2 * B_W), lambda i: (i, 0)), tab, tab, tab, gain, gain, grad, grad, grad]
    args = [proj, c, sa, sb, qg, kg, dq, dk, dv]
    if dz is not None:
        in_specs.append(grad)
        args.append(dz)
    return pl.pallas_call(
        body, name=name, grid=(t // tm,), in_specs=in_specs,
        out_specs=[pl.BlockSpec((tm, out_w), lambda i: (i, 0)), pl.BlockSpec((8, B_DH), lambda i: (0, 0))],
        out_shape=[jax.ShapeDtypeStruct((t, out_w), BF16), jax.ShapeDtypeStruct((8, B_DH), F32)],
        compiler_params=_params(("arbitrary",), 40 * 1024 * 1024),
    )(*args)


def _attn_masks():
    qi = lax.broadcasted_iota(jnp.int32, (B_BLK, 2 * B_BLK), 0)
    kj = lax.broadcasted_iota(jnp.int32, (B_BLK, 2 * B_BLK), 1)
    two = (kj >= qi) & (kj <= qi + B_BLK)
    q1 = lax.broadcasted_iota(jnp.int32, (B_BLK, B_BLK), 0)
    k1 = lax.broadcasted_iota(jnp.int32, (B_BLK, B_BLK), 1)
    return k1 <= q1, two


def _lane_pick(ref_rows, h):
    lane = lax.broadcasted_iota(jnp.int32, ref_rows.shape, 1)
    return jnp.sum(jnp.where(lane == h, ref_rows, 0.0), axis=1, keepdims=True)


B_ROWS = 2048


def _attn_schedule(nb, sb, block):
    way = 4

    def run(items):
        for at in range(0, len(items), way):
            _round_robin([block(*it) for it in items[at:at + way]])

    run([(si, 0, True) for si in range(sb)])
    if nb == 1:
        return
    per = max(1, way // sb)
    lead = 1 + (nb - 1) % per
    if lead > 1:
        run([(si, i, False) for i in range(1, lead) for si in range(sb)])

    def step(it, carry):
        run([(si, lead + it * per + u, False) for u in range(per) for si in range(sb)])
        return carry

    lax.fori_loop(0, (nb - lead) // per, step, 0)


def _attn_rows(i, first):
    if first:
        return pl.ds(0, B_BLK), pl.ds(0, B_BLK)
    rows = pl.ds(pl.multiple_of(i * B_BLK, B_BLK), B_BLK)
    return rows, pl.ds(pl.multiple_of((i - 1) * B_BLK, B_BLK), 2 * B_BLK)


def _attn_fwd(qkv, name):
    ns, ln, _ = qkv.shape
    nb = ln // B_BLK
    sb = B_ROWS // ln
    scale = B_DH ** -0.5

    def body(q_ref, k_ref, v_ref, o_ref, lse_ref):
        h = pl.program_id(1)
        mask1, mask2 = _attn_masks()
        lane = lax.broadcasted_iota(jnp.int32, (B_BLK, B_HEADS), 1)

        @pl.when(h == 0)
        def _():
            lse_ref[...] = jnp.zeros_like(lse_ref)

        def block(si, i, first):
            rows, win = _attn_rows(i, first)
            mask = mask1 if first else mask2
            sc = jnp.where(mask, _dot(q_ref[si, rows, :], k_ref[si, win, :], 1, 1) * scale, -1e30)
            yield
            m = jnp.max(sc, axis=1, keepdims=True)
            p = jnp.exp(sc - m)
            l = jnp.sum(p, axis=1, keepdims=True)
            pv = _dot(p, v_ref[si, win, :], 1, 0)
            yield
            o_ref[si, rows, :] = pv / l
            lse_ref[si, rows, :] = jnp.where(lane == h, m + jnp.log(l), lse_ref[si, rows, :])

        _attn_schedule(nb, sb, block)

    head = lambda off: pl.BlockSpec((sb, ln, B_DH), lambda s, h: (s, 0, off + h))
    return pl.pallas_call(
        body, name=name, grid=(ns // sb, B_HEADS),
        in_specs=[head(0), head(B_HEADS), head(2 * B_HEADS)],
        out_specs=[head(0), pl.BlockSpec((sb, ln, B_HEADS), lambda s, h: (s, 0, 0))],
        out_shape=[jax.ShapeDtypeStruct((ns, ln, B_W), F32), jax.ShapeDtypeStruct((ns, ln, B_HEADS), F32)],
        compiler_params=_params(("parallel", "arbitrary")),
    )(qkv, qkv, qkv)


def _attn_bwd(qkv, d_o, lse_joint, delta, name):
    ns, ln, _ = qkv.shape
    nb = ln // B_BLK
    sb = B_ROWS // ln
    scale = B_DH ** -0.5

    def body(q_ref, k_ref, v_ref, do_ref, lj_ref, dl_ref, dq_ref, dk_out, dv_out, dk_ref, dv_ref):
        h = pl.program_id(1)
        mask1, mask2 = _attn_masks()
        dk_ref[...] = jnp.zeros_like(dk_ref)
        dv_ref[...] = jnp.zeros_like(dv_ref)

        def block(si, i, first):
            rows, win = _attn_rows(i, first)
            mask = mask1 if first else mask2
            q = q_ref[si, rows, :]
            d_out = do_ref[si, rows, :]
            l_col = _lane_pick(lj_ref[si, rows, :], h)
            d_col = _lane_pick(dl_ref[si, rows, :], h)
            sc = _dot(q, k_ref[si, win, :], 1, 1) * scale
            d_p = _dot(d_out, v_ref[si, win, :], 1, 1)
            yield
            p = jnp.exp(jnp.where(mask, sc - l_col, -1e30))
            d_s = p * (d_p - d_col) * scale
            d_q = _dot(d_s, k_ref[si, win, :], 1, 0)
            d_k = _dot(d_s, q, 0, 0)
            d_v = _dot(p, d_out, 0, 0)
            yield
            dq_ref[si, rows, :] = d_q.astype(BF16)
            dk_ref[si, win, :] += d_k
            dv_ref[si, win, :] += d_v

        _attn_schedule(nb, sb, block)
        dk_out[...] = dk_ref[...].astype(BF16)
        dv_out[...] = dv_ref[...].astype(BF16)

    head = lambda off: pl.BlockSpec((sb, ln, B_DH), lambda s, h: (s, 0, off + h))
    small = pl.BlockSpec((sb, ln, B_HEADS), lambda s, h: (s, 0, 0))
    return pl.pallas_call(
        body, name=name, grid=(ns // sb, B_HEADS),
        in_specs=[head(0), head(B_HEADS), head(2 * B_HEADS), head(0), small, small],
        out_specs=[head(0)] * 3,
        out_shape=[jax.ShapeDtypeStruct((ns, ln, B_W), BF16)] * 3,
        scratch_shapes=[pltpu.VMEM((sb, ln, B_DH), F32)] * 2,
        compiler_params=_params(("parallel", "parallel")),
    )(qkv, qkv, qkv, d_o, lse_joint, delta)


def _merge_weights(lse_refs):
    ls = [r[...] for r in lse_refs]
    m = jnp.maximum(jnp.maximum(ls[0], ls[1]), ls[2])
    es = [jnp.exp(l - m) for l in ls]
    tot = es[0] + es[1] + es[2]
    return [e / tot for e in es], m + jnp.log(tot)


def _merge_fwd(outs, lses, proj0, tm=256):
    t = outs[0].shape[0]

    def body(o0, o1, o2, l0, l1, l2, z_ref, og_ref):
        wts, _ = _merge_weights((l0, l1, l2))
        for h in range(B_HEADS):
            cols = slice(h * B_DH, (h + 1) * B_DH)
            o = (wts[0][:, h:h + 1] * o0[:, cols] + wts[1][:, h:h + 1] * o1[:, cols]
                 + wts[2][:, h:h + 1] * o2[:, cols])
            og_ref[:, cols] = (o * _silu(z_ref[:, cols])).astype(BF16)

    wide = pl.BlockSpec((tm, B_W), lambda i: (i, 0))
    small = pl.BlockSpec((tm, B_HEADS), lambda i: (i, 0))
    return pl.pallas_call(
        body, name="merge_fwd", grid=(t // tm,),
        in_specs=[wide] * 3 + [small] * 3 + [pl.BlockSpec((tm, B_W), lambda i: (i, 3))],
        out_specs=wide, out_shape=jax.ShapeDtypeStruct((t, B_W), BF16),
        compiler_params=_params(("parallel",)),
    )(*outs, *lses, proj0)


def _merge_bwd(outs, lses, proj0, d_og, tm=256):
    t = outs[0].shape[0]

    def body(o0, o1, o2, l0, l1, l2, z_ref, dog_ref, do_ref, lj_ref, dl_ref, dz_ref):
        wts, lj = _merge_weights((l0, l1, l2))
        lj_ref[...] = lj
        lane = lax.broadcasted_iota(jnp.int32, (tm, B_HEADS), 1)
        delta = jnp.zeros((tm, B_HEADS), F32)
        for h in range(B_HEADS):
            cols = slice(h * B_DH, (h + 1) * B_DH)
            o = (wts[0][:, h:h + 1] * o0[:, cols] + wts[1][:, h:h + 1] * o1[:, cols]
                 + wts[2][:, h:h + 1] * o2[:, cols])
            z = z_ref[:, cols]
            d_g = dog_ref[:, cols]
            d_out = d_g * _silu(z)
            dz_ref[:, cols] = (d_g * o * _dsilu(z)).astype(BF16)
            do_ref[:, cols] = d_out.astype(BF16)
            delta = jnp.where(lane == h, jnp.sum(d_out * o, axis=1, keepdims=True), delta)
        dl_ref[...] = delta

    wide = pl.BlockSpec((tm, B_W), lambda i: (i, 0))
    small = pl.BlockSpec((tm, B_HEADS), lambda i: (i, 0))
    return pl.pallas_call(
        body, name="merge_bwd", grid=(t // tm,),
        in_specs=[wide] * 3 + [small] * 3 + [pl.BlockSpec((tm, B_W), lambda i: (i, 3)), wide],
        out_specs=[wide, small, small, wide],
        out_shape=[jax.ShapeDtypeStruct((t, B_W), BF16), jax.ShapeDtypeStruct((t, B_HEADS), F32),
                   jax.ShapeDtypeStruct((t, B_HEADS), F32), jax.ShapeDtypeStruct((t, B_W), BF16)],
        compiler_params=_params(("parallel",)),
    )(*outs, *lses, proj0, d_og)


def _adamw(w, g, m, v, name):
    r, c = w.shape
    tr = r
    for cand in (256, 128, 64, 32, 16, 8):
        if r % cand == 0:
            tr = cand
            break

    def body(w_ref, g_ref, m_ref, v_ref, d_ref, nm_ref, nv_ref):
        gv = g_ref[...]
        nm = ADAM_B1 * m_ref[...] + (1.0 - ADAM_B1) * gv
        nv = ADAM_B2 * v_ref[...] + (1.0 - ADAM_B2) * (gv * gv)
        m_hat = nm / (1.0 - ADAM_B1 ** ADAM_STEP)
        v_hat = nv / (1.0 - ADAM_B2 ** ADAM_STEP)
        d_ref[...] = -ADAM_LR * (m_hat / (jnp.sqrt(v_hat) + ADAM_EPS) + ADAM_WD * w_ref[...])
        nm_ref[...] = nm
        nv_ref[...] = nv

    blk = pl.BlockSpec((tr, c), lambda i: (i, 0))
    return pl.pallas_call(
        body, name=name, grid=(r // tr,), in_specs=[blk] * 4, out_specs=[blk] * 3,
        out_shape=[jax.ShapeDtypeStruct((r, c), F32)] * 3,
        compiler_params=_params(("parallel",)),
    )(w, g, m, v)


def _adam_update(w, gv, m, v):
    nm = ADAM_B1 * m + (1.0 - ADAM_B1) * gv
    nv = ADAM_B2 * v + (1.0 - ADAM_B2) * (gv * gv)
    m_hat = nm / (1.0 - ADAM_B1 ** ADAM_STEP)
    v_hat = nv / (1.0 - ADAM_B2 ** ADAM_STEP)
    return -ADAM_LR * (m_hat / (jnp.sqrt(v_hat) + ADAM_EPS) + ADAM_WD * w), nm, nv


def _adamw_shard(w, mine, theirs, m, v, half_index, name, tr=128):
    _, r, c = w.shape
    nhb = (r // 2) // tr

    def body(c_ref, w_ref, mine_ref, theirs_ref, m_ref, v_ref, g_ref, d_ref, nm_ref, nv_ref):
        is_mine = (pl.program_id(0) // nhb) == c_ref[0]
        gv = jnp.where(is_mine, mine_ref[...], theirs_ref[...])
        d, nm, nv = _adam_update(w_ref[...], gv, m_ref[...], v_ref[...])
        g_ref[...] = gv
        d_ref[...] = d
        nm_ref[...] = nm
        nv_ref[...] = nv

    full = pl.BlockSpec((None, tr, c), lambda i, cc: (0, i, 0))
    half = pl.BlockSpec((tr, c), lambda i, cc: (i % nhb, 0))
    return pl.pallas_call(
        body, name=name,
        grid_spec=pltpu.PrefetchScalarGridSpec(
            num_scalar_prefetch=1, grid=(2 * nhb,),
            in_specs=[full, half, half, full, full], out_specs=[full] * 4),
        out_shape=[jax.ShapeDtypeStruct(w.shape, F32)] * 4,
        compiler_params=_params(("parallel",), 40 * 1024 * 1024),
    )(half_index, w, mine, theirs, m, v)


def _adamw_shard_cols(w, mine, theirs, m, v, half_index, name, steps=20):
    c, _, r = w.shape
    tc = c // steps
    assert tc * steps == c

    def body(c_ref, w_ref, mine_ref, theirs_ref, m_ref, v_ref, g_ref, d_ref, nm_ref, nv_ref):
        first = jnp.where(c_ref[0] == 0, mine_ref[...], theirs_ref[...])
        second = jnp.where(c_ref[0] == 0, theirs_ref[...], mine_ref[...])
        for lo, gv in ((0, first), (r // 2, second)):
            cols = slice(lo, lo + r // 2)
            d, nm, nv = _adam_update(w_ref[:, :, cols], gv, m_ref[:, :, cols], v_ref[:, :, cols])
            g_ref[:, :, cols] = gv
            d_ref[:, :, cols] = d
            nm_ref[:, :, cols] = nm
            nv_ref[:, :, cols] = nv

    full = pl.BlockSpec((tc, 1, r), lambda i, cc: (i, 0, 0))
    half = pl.BlockSpec((tc, 1, r // 2), lambda i, cc: (i, 0, 0))
    return pl.pallas_call(
        body, name=name,
        grid_spec=pltpu.PrefetchScalarGridSpec(
            num_scalar_prefetch=1, grid=(steps,),
            in_specs=[full, half, half, full, full], out_specs=[full] * 4),
        out_shape=[jax.ShapeDtypeStruct(w.shape, F32)] * 4,
        compiler_params=_params(("parallel",), 40 * 1024 * 1024),
    )(half_index, w, mine, theirs, m, v)


def _pair_sum(own, other, half_index, name, tr=256):
    _, r, c = own.shape
    rh = r // 2
    tr = min(tr, rh)
    nrb = rh // tr

    def body(c_ref, own_ref, oth_ref, out_ref):
        out_ref[...] = (own_ref[...] + oth_ref[...].astype(F32)).astype(BF16)

    return pl.pallas_call(
        body, name=name,
        grid_spec=pltpu.PrefetchScalarGridSpec(
            num_scalar_prefetch=1, grid=(N_CHIPS, nrb),
            in_specs=[pl.BlockSpec((None, tr, c), lambda k, i, cc: (k, cc[0] * nrb + i, 0)),
                      pl.BlockSpec((None, tr, c), lambda k, i, cc: (k, i, 0))],
            out_specs=pl.BlockSpec((None, tr, c), lambda k, i, cc: (k, i, 0))),
        out_shape=jax.ShapeDtypeStruct((N_CHIPS, rh, c), BF16),
        compiler_params=_params(("parallel", "parallel")),
    )(half_index, own, other)


def _chip_sum(sums, others, chip_index, name, tr=256):
    _, r, c = sums.shape
    tr = min(tr, r)

    def body(k_ref, own_ref, oth_ref, out_ref):
        acc = own_ref[...].astype(F32)
        for j in range(N_CHIPS - 1):
            acc = acc + oth_ref[j].astype(F32)
        out_ref[...] = acc

    return pl.pallas_call(
        body, name=name,
        grid_spec=pltpu.PrefetchScalarGridSpec(
            num_scalar_prefetch=1, grid=(r // tr,),
            in_specs=[pl.BlockSpec((None, tr, c), lambda i, kk: (kk[0], i, 0)),
                      pl.BlockSpec((N_CHIPS - 1, tr, c), lambda i, kk: (0, i, 0))],
            out_specs=pl.BlockSpec((tr, c), lambda i, kk: (i, 0))),
        out_shape=jax.ShapeDtypeStruct((r, c), F32),
        compiler_params=_params(("parallel",)),
    )(chip_index, sums, others)


HBM = pl.BlockSpec(memory_space=pltpu.HBM)


def _place():
    x, y, c = lax.axis_index("x"), lax.axis_index("y"), lax.axis_index("c")
    chips = [(1 - x, y), (x, 1 - y), (1 - x, 1 - y)]
    return x, y, c, chips


def _weight_allgather(shards, conv_shard):
    na = len(shards)

    def body(*refs):
        ins = refs[:na]
        conv_in = refs[na]
        outs = refs[na + 1:2 * na + 1]
        conv_out = refs[2 * na + 1]
        send, recv, fsend, frecv, csend, crecv = refs[2 * na + 2:]
        x, y, c, chips = _place()
        me = 2 * x + y
        sib = (x, y, 1 - c)
        first, conv_cp = [], []
        for i in range(na):
            rh = ins[i].shape[0] // 2
            mine = pl.ds(c * rh, rh)
            for j, (px, py) in enumerate(chips):
                cp = pltpu.make_async_remote_copy(
                    src_ref=ins[i].at[mine], dst_ref=outs[i].at[me, mine],
                    send_sem=send.at[3 * i + j], recv_sem=recv.at[3 * i + j],
                    device_id=(px, py, c), device_id_type=MESH)
                cp.start()
                first.append(cp)
        for j, (px, py) in enumerate(chips):
            cp = pltpu.make_async_remote_copy(
                src_ref=conv_in, dst_ref=conv_out.at[me], send_sem=csend.at[j], recv_sem=crecv.at[j],
                device_id=(px, py, c), device_id_type=MESH)
            cp.start()
            conv_cp.append(cp)
        passed = []
        for i in range(na):
            rh = ins[i].shape[0] // 2
            mine = pl.ds(c * rh, rh)
            for j, (px, py) in enumerate(chips):
                slot = outs[i].at[2 * px + py, mine]
                pltpu.make_async_remote_copy(
                    src_ref=slot, dst_ref=slot, send_sem=send.at[3 * i + j], recv_sem=recv.at[3 * i + j],
                    device_id=(px, py, c), device_id_type=MESH).wait_recv()
                cp = pltpu.make_async_remote_copy(
                    src_ref=slot, dst_ref=slot, send_sem=fsend.at[3 * i + j], recv_sem=frecv.at[3 * i + j],
                    device_id=sib, device_id_type=MESH)
                cp.start()
                passed.append(cp)
        for i in range(na):
            rh = ins[i].shape[0] // 2
            theirs = pl.ds((1 - c) * rh, rh)
            for j, (px, py) in enumerate(chips):
                slot = outs[i].at[2 * px + py, theirs]
                pltpu.make_async_remote_copy(
                    src_ref=slot, dst_ref=slot, send_sem=fsend.at[3 * i + j], recv_sem=frecv.at[3 * i + j],
                    device_id=sib, device_id_type=MESH).wait_recv()
        for j, (px, py) in enumerate(chips):
            slot = conv_out.at[2 * px + py]
            pltpu.make_async_remote_copy(
                src_ref=slot, dst_ref=slot, send_sem=csend.at[j], recv_sem=crecv.at[j],
                device_id=(px, py, c), device_id_type=MESH).wait_recv()
        for cp in first + passed + conv_cp:
            cp.wait_send()

    out_shape = [jax.ShapeDtypeStruct((N_CHIPS,) + s.shape, s.dtype) for s in shards]
    out_shape.append(jax.ShapeDtypeStruct((N_CHIPS,) + conv_shard.shape, conv_shard.dtype))
    res = pl.pallas_call(
        body, name="weight_allgather",
        in_specs=[HBM] * (na + 1), out_specs=[HBM] * (na + 1), out_shape=out_shape,
        scratch_shapes=[pltpu.SemaphoreType.DMA((3 * na,)), pltpu.SemaphoreType.DMA((3 * na,)),
                        pltpu.SemaphoreType.DMA((3 * na,)), pltpu.SemaphoreType.DMA((3 * na,)),
                        pltpu.SemaphoreType.DMA((3,)), pltpu.SemaphoreType.DMA((3,))],
    )(*shards, conv_shard)
    my_chip = 2 * lax.axis_index("x") + lax.axis_index("y")
    pick = lambda got, own: [jnp.where(my_chip == k, own, got[k]) for k in range(N_CHIPS)]
    return [pick(g, s) for g, s in zip(res[:na], shards)], pick(res[na], conv_shard)


def _sibling_swap_halves(grads, name):
    na = len(grads)

    def body(*refs):
        ins, outs = refs[:na], refs[na:2 * na]
        send, recv = refs[2 * na:]
        x, y, c, _ = _place()
        sib = (x, y, 1 - c)
        cps = []
        for i in range(na):
            rh = ins[i].shape[1] // 2
            cp = pltpu.make_async_remote_copy(
                src_ref=ins[i].at[:, pl.ds((1 - c) * rh, rh), :], dst_ref=outs[i],
                send_sem=send.at[i], recv_sem=recv.at[i], device_id=sib, device_id_type=MESH)
            cp.start()
            cps.append(cp)
        for cp in cps:
            cp.wait()

    out_shape = [jax.ShapeDtypeStruct((g.shape[0], g.shape[1] // 2, g.shape[2]), g.dtype) for g in grads]
    return pl.pallas_call(
        body, name=name, in_specs=[HBM] * na, out_specs=[HBM] * na, out_shape=out_shape,
        scratch_shapes=[pltpu.SemaphoreType.DMA((na,)), pltpu.SemaphoreType.DMA((na,))],
    )(*grads)


def _sibling_swap_whole(halves):
    na = len(halves)

    def body(*refs):
        ins, outs = refs[:na], refs[na:2 * na]
        send, recv = refs[2 * na:]
        x, y, c, _ = _place()
        cps = []
        for i in range(na):
            cp = pltpu.make_async_remote_copy(
                src_ref=ins[i], dst_ref=outs[i], send_sem=send.at[i], recv_sem=recv.at[i],
                device_id=(x, y, 1 - c), device_id_type=MESH)
            cp.start()
            cps.append(cp)
        for cp in cps:
            cp.wait()

    out_shape = [jax.ShapeDtypeStruct(h.shape, h.dtype) for h in halves]
    return pl.pallas_call(
        body, name="grad_sibling_join", in_specs=[HBM] * na, out_specs=[HBM] * na, out_shape=out_shape,
        scratch_shapes=[pltpu.SemaphoreType.DMA((na,)), pltpu.SemaphoreType.DMA((na,))],
    )(*halves)


SEM = pl.BlockSpec(memory_space=pltpu.SEMAPHORE)
ANY = pl.BlockSpec(memory_space=pl.ANY)
EFFECT = pltpu.SideEffectType.DATAFLOW_SIDE_EFFECTING


def _split_copy_start(name, plan, srcs, lands, after):
    ns, nl = len(srcs), len(lands)

    def body(*refs):
        src_refs, land_refs = refs[:ns], refs[ns:ns + nl]
        send, recv = refs[ns + nl + 1], refs[ns + nl + 2]
        token = refs[-1]
        outgoing, _ = plan(src_refs, land_refs)
        for src, dst, dev, si, ri in outgoing:
            pltpu.make_async_remote_copy(src_ref=src, dst_ref=dst, send_sem=send.at[si], recv_sem=recv.at[ri],
                                         device_id=dev, device_id_type=MESH).start()
        token[...] = jnp.zeros_like(token)

    n_out, n_in = plan.counts
    thru = [pltpu.HBM(a.shape, a.dtype) for a in list(srcs) + list(lands)]
    res = pl.pallas_call(
        body, name=name,
        out_shape=[pltpu.SemaphoreType.DMA((n_out,)), pltpu.SemaphoreType.DMA((n_in,))] + thru
        + [jax.ShapeDtypeStruct((8, 128), F32)],
        in_specs=[HBM] * (ns + nl) + [ANY],
        out_specs=[SEM, SEM] + [HBM] * (ns + nl) + [pl.BlockSpec(memory_space=pltpu.VMEM)],
        input_output_aliases={i: 2 + i for i in range(ns + nl)},
        compiler_params=pltpu.CompilerParams(has_side_effects=EFFECT),
    )(*[pltpu.with_memory_space_constraint(a, pltpu.HBM) for a in list(srcs) + list(lands)], after)
    return res[0], res[1], res[2:2 + ns], res[2 + ns:2 + ns + nl], res[-1]


def _split_copy_wait(name, plan, send, recv, srcs, lands, after):
    ns, nl = len(srcs), len(lands)

    def body(*refs):
        src_refs, land_refs = refs[:ns], refs[ns:ns + nl]
        send_ref, recv_ref = refs[ns + nl], refs[ns + nl + 1]
        outgoing, arrivals = plan(src_refs, land_refs)
        for src, dst, dev, si, ri in outgoing:
            pltpu.make_async_remote_copy(src_ref=src, dst_ref=dst, send_sem=send_ref.at[si], recv_sem=recv_ref.at[ri],
                                         device_id=dev, device_id_type=MESH).wait_send()
        for view, ri in arrivals:
            pltpu.make_async_remote_copy(src_ref=view, dst_ref=view, send_sem=send_ref.at[0], recv_sem=recv_ref.at[ri],
                                         device_id=_place()[:3], device_id_type=MESH).wait_recv()

    thru = [pltpu.HBM(a.shape, a.dtype) for a in list(srcs) + list(lands)]
    res = pl.pallas_call(
        body, name=name, out_shape=thru,
        in_specs=[HBM] * (ns + nl) + [SEM, SEM, ANY], out_specs=[HBM] * (ns + nl),
        input_output_aliases={i: i for i in range(ns + nl)},
        compiler_params=pltpu.CompilerParams(has_side_effects=EFFECT),
    )(*srcs, *lands, send, recv, after)
    return res[:ns], res[ns:]


def _gather_plan(n_arrays):
    def plan(src_refs, land_refs):
        x, y, c, chips = _place()
        me = 2 * x + y
        outgoing, arrivals = [], []
        for i in range(n_arrays):
            rh = src_refs[i].shape[0] // 2
            mine = pl.ds(c * rh, rh)
            for j, (px, py) in enumerate(chips):
                for delta in range(2):
                    tc = c ^ delta
                    outgoing.append((src_refs[i].at[mine], land_refs[i].at[me, mine], (px, py, tc),
                                     6 * i + 2 * j + delta, 6 * i + 2 * j + delta))
                    theirs = pl.ds(tc * rh, rh)
                    arrivals.append((land_refs[i].at[2 * px + py, theirs], 6 * i + 2 * j + delta))
        return outgoing, arrivals

    plan.counts = (6 * n_arrays, 6 * n_arrays)
    return plan


def _exchange_plan(n_arrays):
    def plan(src_refs, land_refs):
        x, y, c, chips = _place()
        outgoing, arrivals = [], []
        for i in range(n_arrays):
            for j, (px, py) in enumerate(chips):
                outgoing.append((src_refs[i].at[2 * px + py], land_refs[i].at[j], (px, py, c), 3 * i + j, 3 * i + j))
                arrivals.append((land_refs[i].at[j], 3 * i + j))
        return outgoing, arrivals

    plan.counts = (3 * n_arrays, 3 * n_arrays)
    return plan


def _small_allreduce(vec):
    r, cdim = vec.shape
    n_dev = 8

    def body(v_ref, out_ref, buf, send, recv):
        x, y, c, _ = _place()
        me = 4 * x + 2 * y + c
        buf[me] = v_ref[...]
        cps = []
        for k in range(1, n_dev):
            dx, dy, dc = (k >> 2) & 1, (k >> 1) & 1, k & 1
            peer = (x ^ dx, y ^ dy, c ^ dc)
            cp = pltpu.make_async_remote_copy(
                src_ref=v_ref, dst_ref=buf.at[me], send_sem=send.at[k - 1], recv_sem=recv.at[k - 1],
                device_id=peer, device_id_type=MESH)
            cp.start()
            cps.append(cp)
        for k in range(1, n_dev):
            dx, dy, dc = (k >> 2) & 1, (k >> 1) & 1, k & 1
            src = 4 * (x ^ dx) + 2 * (y ^ dy) + (c ^ dc)
            slot = buf.at[src]
            pltpu.make_async_remote_copy(
                src_ref=slot, dst_ref=slot, send_sem=send.at[k - 1], recv_sem=recv.at[k - 1],
                device_id=(x ^ dx, y ^ dy, c ^ dc), device_id_type=MESH).wait_recv()
        for cp in cps:
            cp.wait_send()
        acc = buf[0]
        for k in range(1, n_dev):
            acc = acc + buf[k]
        out_ref[...] = acc

    vm = pl.BlockSpec(memory_space=pltpu.VMEM)
    return pl.pallas_call(
        body, name="small_allreduce", in_specs=[vm], out_specs=vm,
        out_shape=jax.ShapeDtypeStruct((r, cdim), F32),
        scratch_shapes=[pltpu.VMEM((n_dev, r, cdim), F32), pltpu.SemaphoreType.DMA((n_dev - 1,)),
                        pltpu.SemaphoreType.DMA((n_dev - 1,))],
    )(vec)


def _a_cols_to_head_major(w):
    lead = w.shape[:-1]
    q = w[..., :A_QK].reshape(lead + (A_HEADS, A_DK))
    k = w[..., A_QK:2 * A_QK].reshape(lead + (A_HEADS, A_DK))
    v = w[..., 2 * A_QK:2 * A_QK + A_VW].reshape(lead + (A_HEADS, A_DV))
    z = w[..., 2 * A_QK + A_VW:].reshape(lead + (A_HEADS, A_DV))
    return jnp.concatenate([q, k, v, z], axis=-1).reshape(lead + (A_HEADS * A_HEAD_COLS,))


def _a_cols_from_head_major(w):
    lead = w.shape[:-1]
    w = w.reshape(lead + (A_HEADS, A_HEAD_COLS))
    parts = [w[..., :A_DK], w[..., A_DK:2 * A_DK], w[..., 2 * A_DK:2 * A_DK + A_DV], w[..., 2 * A_DK + A_DV:]]
    return jnp.concatenate([p.reshape(lead + (-1,)) for p in parts], axis=-1)


def _conv_cols_to_head_major(w):
    lead = w.shape[:-1]
    q = w[..., :A_QK].reshape(lead + (A_HEADS, A_DK))
    k = w[..., A_QK:2 * A_QK].reshape(lead + (A_HEADS, A_DK))
    v = w[..., 2 * A_QK:].reshape(lead + (A_HEADS, A_DV))
    return jnp.concatenate([q, k, v], axis=-1).reshape(lead + (A_HEADS * A_CONV_COLS,))


def _conv_cols_from_head_major(w):
    lead = w.shape[:-1]
    w = w.reshape(lead + (A_HEADS, A_CONV_COLS))
    parts = [w[..., :A_DK], w[..., A_DK:2 * A_DK], w[..., 2 * A_DK:]]
    return jnp.concatenate([p.reshape(lead + (-1,)) for p in parts], axis=-1)


def _to_stream(a, bn, d):
    rest = a.shape[1:]
    s = a.shape[0] // bn
    a = a.reshape((bn, s // d, d) + rest)
    a = jnp.swapaxes(a, 1, 2)
    return a.reshape((bn * d, s // d) + rest)


def _from_stream(a, bn, d):
    rest = a.shape[2:]
    ln = a.shape[1]
    a = a.reshape((bn, d, ln) + rest)
    a = jnp.swapaxes(a, 1, 2)
    return a.reshape((bn * ln * d,) + rest)


B_SUB = 512
B_SHARD_BLOCKS = (3 * B_GROUPS * B_W + B_W) // N_CHIPS // B_SUB


def _b_block(gi, jj):
    nb = (B_GROUPS * (jj // 2) + gi) * 2 + jj % 2
    return nb // B_SHARD_BLOCKS, nb % B_SHARD_BLOCKS


def _shard_major(g, ncols):
    r = g.shape[0]
    return jnp.swapaxes(g.reshape(r, N_CHIPS, ncols), 0, 1)


def _pack_rows(items):
    rows, offs = [], []
    at = 0
    for a in items:
        flat = a.reshape(-1).astype(F32)
        nr = -(-flat.shape[0] // 1024) * 8
        flat = jnp.pad(flat, (0, nr * 128 - flat.shape[0]))
        rows.append(flat.reshape(nr, 128))
        offs.append((at, nr, a.shape))
        at += nr
    return jnp.concatenate(rows, axis=0), offs


def _unpack_rows(packed, offs):
    out = []
    for at, nr, shape in offs:
        size = int(np.prod(shape)) if len(shape) else 1
        out.append(packed[at:at + nr].reshape(-1)[:size].reshape(shape))
    return out


def _local_step(x, positions, loss_target, norm_g, wa_in, conv_w, a_log, a_dt_bias, a_norm_g,
                b_q_norm_g, b_k_norm_g, start_token, late_weights, b_grads_ready, a_grads_ready):
    bn, s, d = x.shape
    t = bn * s
    n_chunks = s // A_CHUNK
    wa_main = _a_cols_to_head_major(wa_in[:, :A_MAIN])
    wa_tail = jnp.pad(wa_in[:, A_MAIN:], ((0, 0), (0, 128 - 2 * A_HEADS)))
    cw_hm = _conv_cols_to_head_major(conv_w)

    x0 = x.reshape(t, d)
    h0 = _rms_fwd(x0, norm_g[0:1] + start_token, "rms0_fwd")
    proj_a = _matmul(h0, wa_main, "nn", F32, "a_in_main")
    tail_a = _matmul(h0, wa_tail, "nn", F32, "a_in_tail")
    tail_t = jnp.swapaxes(tail_a[:, :2 * A_HEADS].reshape(bn, s, 2 * A_HEADS), 1, 2)
    tail_t = tail_t.reshape(bn, 2 * A_HEADS, n_chunks, A_CHUNK)
    beta, gc = _gdn_prep(tail_t, a_log[0], a_dt_bias[0])
    proj_a3 = proj_a.reshape(bn, s, A_MAIN)
    og_a, oraw_a, states, t_mats, conv_y = _gdn_fwd(proj_a3, cw_hm, beta, gc, a_norm_g)
    wa_out, wb_in, wb_out = late_weights(og_a)
    b_cols = [4 * B_W] + [3 * B_W] * (B_GROUPS - 1)
    x1 = _matmul(og_a.reshape(t, A_VW), wa_out, "nn", F32, "a_out", res=x0, tk=2048)

    h1 = _rms_fwd(x1, norm_g[1:2], "rms1_fwd")
    inv_freq = ROPE_THETA ** (-jnp.arange(0, ROPE_DIMS, 2, dtype=F32) / ROPE_DIMS)
    freq_row = jnp.concatenate([inv_freq, inv_freq, jnp.zeros((128 - ROPE_DIMS,), F32)]).reshape(1, 128)
    posf = jnp.broadcast_to(positions.astype(F32).reshape(t, 1), (t, 128))
    tabs = _rope_tables(posf, freq_row)
    h1_s, tabs_s, proj_b, qkv_b, o_b, lse_b = [], [], [], [], [], []
    for gi, dil in enumerate(B_DIL):
        hs = h1 if dil == 1 else _to_stream(h1, bn, dil).reshape(t, d)
        ts = tabs if dil == 1 else [_to_stream(tb, bn, dil).reshape(t, 128) for tb in tabs]
        pj = _matmul(hs, wb_in, "nn", F32, f"b_in_g{gi}", tm=2048, tn=B_SUB, n=b_cols[gi], b_spec=pl.BlockSpec(
            (None, d, B_SUB), lambda i, j, kk, gi=gi: (_b_block(gi, j)[0], kk, _b_block(gi, j)[1])))
        qkv = _qk_prep(pj, *ts, b_q_norm_g[0, gi:gi + 1], b_k_norm_g[0, gi:gi + 1], f"qk_prep_g{gi}")
        o_s, lse_s = _attn_fwd(qkv.reshape(bn * dil, s // dil, 3 * B_W), f"attn_fwd_g{gi}")
        h1_s.append(hs), tabs_s.append(ts), proj_b.append(pj), qkv_b.append(qkv)
        o_b.append(o_s.reshape(t, B_W) if dil == 1 else _from_stream(o_s, bn, dil))
        lse_b.append(lse_s.reshape(t, B_HEADS) if dil == 1 else _from_stream(lse_s, bn, dil))
    og_b = _merge_fwd(o_b, lse_b, proj_b[0])
    x2 = _matmul(og_b, wb_out, "nn", F32, "b_out", res=x1)

    d_x2, loss_parts = _loss_grad(x2, loss_target.reshape(t, d))
    loss_local = jnp.sum(loss_parts)

    d_x2b = d_x2.astype(BF16)
    g_wb_out = _matmul(og_b, d_x2b, "tn", F32, "b_out_dw")
    d_og_b = _matmul(d_x2b, wb_out, "nt", F32, "b_out_dx")
    d_o, lse_joint, delta, d_z = _merge_bwd(o_b, lse_b, proj_b[0], d_og_b)
    d_h1, g_qn, g_kn = [], [], []
    g_wb_in = lax.empty(wb_in.shape, F32)
    for gi, dil in enumerate(B_DIL):
        if dil == 1:
            do_s, lj_s, dl_s = d_o, lse_joint, delta
        else:
            do_s, lj_s, dl_s = (_to_stream(a, bn, dil).reshape(t, -1) for a in (d_o, lse_joint, delta))
        ns, ln = bn * dil, s // dil
        dq, dk, dv = _attn_bwd(qkv_b[gi].reshape(ns, ln, 3 * B_W), do_s.reshape(ns, ln, B_W),
                               lj_s.reshape(ns, ln, B_HEADS), dl_s.reshape(ns, ln, B_HEADS), f"attn_bwd_g{gi}")
        d_pj, d_gain = _qk_prep_bwd(proj_b[gi], *tabs_s[gi], b_q_norm_g[0, gi:gi + 1], b_k_norm_g[0, gi:gi + 1],
                                    dq.reshape(t, B_W), dk.reshape(t, B_W), dv.reshape(t, B_W),
                                    d_z if gi == 0 else None, f"qk_prep_bwd_g{gi}")
        g_wb_in = _matmul(h1_s[gi], d_pj, "tn", F32, f"b_in_dw_g{gi}", tn=B_SUB, tk=2048, into=(g_wb_in, pl.BlockSpec(
            (None, d, B_SUB), lambda i, j, kk, gi=gi: (_b_block(gi, j)[0], i, _b_block(gi, j)[1]))))
        dh = _matmul(d_pj, wb_in, "nt", F32, f"b_in_dx_g{gi}", tm=2048, tk=B_SUB, n=d, b_spec=pl.BlockSpec(
            (None, d, B_SUB), lambda i, j, kk, gi=gi: (_b_block(gi, kk)[0], j, _b_block(gi, kk)[1])))
        d_h1.append(dh if dil == 1 else _from_stream(dh.reshape(ns, ln, d), bn, dil))
        g_qn.append(d_gain[0]), g_kn.append(d_gain[1])
    d_x1, g_norm1 = _rms_bwd(x1, norm_g[1:2], d_h1, d_x2, "rms1_bwd")

    d_x1b = d_x1.astype(BF16)
    g_wa_out = _matmul(og_a.reshape(t, A_VW), d_x1b, "tn", F32, "a_out_dw")
    b_token = b_grads_ready(g_wb_in, g_wb_out, g_wa_out)
    d_og_a = _matmul(d_x1b, wa_out, "nt", F32, "a_out_dx")
    d_pa, d_gc, d_beta, d_cw, d_ng = _gdn_bwd(proj_a3, cw_hm, beta, gc, a_norm_g + b_token, oraw_a, states,
                                              t_mats, conv_y, d_og_a.reshape(bn, s, A_VW))
    d_tail_t, d_alog, d_dtb = _gdn_prep_bwd(tail_t, a_log[0], a_dt_bias[0], d_gc, d_beta)
    d_tail = jnp.swapaxes(d_tail_t.reshape(bn, 2 * A_HEADS, s), 1, 2).reshape(t, 2 * A_HEADS)
    d_tail = jnp.pad(d_tail, ((0, 0), (0, 128 - 2 * A_HEADS))).astype(BF16)
    d_pa = d_pa.reshape(t, A_MAIN)
    g_wa_main = _matmul(h0, d_pa, "tn", F32, "a_in_dw_main")
    g_wa_tail = _matmul(h0, d_tail, "tn", F32, "a_in_dw_tail")
    g_wa_in = jnp.concatenate([_a_cols_from_head_major(g_wa_main), g_wa_tail[:, :2 * A_HEADS]], axis=1)
    a_token = a_grads_ready(g_wa_in)
    d_h0 = _matmul(d_pa, wa_main, "nt", F32, "a_in_dx_main")
    d_h0t = _matmul(d_tail + a_token.astype(BF16), wa_tail, "nt", F32, "a_in_dx_tail")
    d_x0, g_norm0 = _rms_bwd(x0, norm_g[0:1], [d_h0, d_h0t], d_x1, "rms0_bwd")

    gfull = {
        "norm_g": jnp.concatenate([g_norm0, g_norm1], axis=0), "a_w_in": g_wa_in,
        "a_conv_w": _conv_cols_from_head_major(jnp.sum(d_cw, axis=0)),
        "a_log": jnp.sum(d_alog[:, :, 0], axis=0), "a_dt_bias": jnp.sum(d_dtb[:, :, 0], axis=0),
        "a_norm_g": jnp.sum(d_ng[:, :, 0, :], axis=(0, 1)), "a_w_out": g_wa_out, "b_w_in": g_wb_in,
        "b_q_norm_g": jnp.stack(g_qn), "b_k_norm_g": jnp.stack(g_kn), "b_w_out": g_wb_out}
    return loss_local, d_x0.reshape(bn, s, d), gfull


def kernel(x, positions, norm_g, a_w_in, a_conv_w, a_log, a_dt_bias, a_norm_g, a_w_out, b_w_in, b_q_norm_g, b_k_norm_g, b_w_out, loss_target, m_norm_g, m_a_w_in, m_a_conv_w, m_a_log, m_a_dt_bias, m_a_norm_g, m_a_w_out, m_b_w_in, m_b_q_norm_g, m_b_k_norm_g, m_b_w_out, v_norm_g, v_a_w_in, v_a_conv_w, v_a_log, v_a_dt_bias, v_a_norm_g, v_a_w_out, v_b_w_in, v_b_q_norm_g, v_b_k_norm_g, v_b_w_out):
    d = x.shape[2]
    my_c = lax.axis_index("c")
    my_chip = 2 * lax.axis_index("x") + lax.axis_index("y")

    half_index = jnp.reshape(my_c, (1,)).astype(jnp.int32)
    chip_index = jnp.reshape(my_chip, (1,)).astype(jnp.int32)
    (ga_in,), g_conv = _weight_allgather([a_w_in[0].astype(BF16)], a_conv_w[0])
    wa_in = jnp.concatenate(ga_in, axis=1)
    conv_w = jnp.concatenate(g_conv, axis=1)

    late_shards = [a_w_out[0].astype(BF16), b_w_in[0].astype(BF16), b_w_out[0].astype(BF16)]
    late_lands = [lax.dynamic_update_slice(lax.empty((N_CHIPS,) + s.shape, BF16), s[None], (my_chip, 0, 0))
                  for s in late_shards]
    gather = _gather_plan(len(late_shards))
    ag_send, ag_recv, ag_srcs, ag_lands, ag_token = _split_copy_start(
        "late_weights_start", gather, late_shards, late_lands, conv_w)

    def late_weights(after):
        _, (ga_out, gb_in, gb_out) = _split_copy_wait(
            "late_weights_wait", gather, ag_send, ag_recv, ag_srcs, ag_lands, after)
        return ga_out.reshape(A_VW, d), gb_in, gb_out.reshape(B_W, d)

    def reduce_to_chip_sums(mats, tag):
        recv_sib = _sibling_swap_halves([g.astype(BF16) for g in mats], f"grad_{tag}_sibling_swap")
        return [_pair_sum(g, r, half_index, f"grad_{tag}_pair_sum_{i}") for i, (g, r) in enumerate(zip(mats, recv_sib))]

    pending = {}

    def start_exchange(tag, mats):
        sums = reduce_to_chip_sums(mats, tag)
        lands = [lax.empty((N_CHIPS - 1,) + s.shape[1:], BF16) for s in sums]
        plan = _exchange_plan(len(mats))
        pending[tag] = (plan,) + tuple(_split_copy_start(f"grad_{tag}_exchange_start", plan, sums, lands, chip_index))
        return pending[tag][5][0, 0]

    def finish_exchange(tag, after):
        plan, send, recv, srcs, lands, _ = pending[tag]
        return _split_copy_wait(f"grad_{tag}_exchange_wait", plan, send, recv, srcs, lands, after)

    def b_grads_ready(g_wb_in, g_wb_out, g_wa_out):
        return start_exchange("b", [g_wb_in, g_wb_out.reshape(N_CHIPS, -1, d), g_wa_out.reshape(N_CHIPS, -1, d)])

    def a_grads_ready(g_wa_in):
        return start_exchange("a", [_shard_major(g_wa_in, a_w_in.shape[2])])

    loss_local, d_x0, gfull = _local_step(x, positions, loss_target, norm_g, wa_in, conv_w, a_log, a_dt_bias,
                                          a_norm_g, b_q_norm_g, b_k_norm_g, ag_token[0, 0], late_weights,
                                          b_grads_ready, a_grads_ready)

    small = [gfull["norm_g"], gfull["a_conv_w"], gfull["a_log"], gfull["a_dt_bias"], gfull["a_norm_g"],
             gfull["b_q_norm_g"], gfull["b_k_norm_g"], loss_local]
    packed, offs = _pack_rows(small)
    reduced = _small_allreduce(packed)
    g_norm, g_conv_all, g_alog, g_dtb, g_ang, g_q, g_k, loss = _unpack_rows(reduced, offs)
    g_conv_mine = lax.dynamic_slice_in_dim(g_conv_all, my_chip * a_conv_w.shape[2], a_conv_w.shape[2], axis=1)

    b_sums, b_received = finish_exchange("b", d_x0)
    a_sums, a_received = finish_exchange("a", reduced)
    chip_sums = [a_sums[0], b_sums[2], b_sums[0], b_sums[1]]
    received = [a_received[0], b_received[2], b_received[0], b_received[1]]
    halves = [_chip_sum(s, r, chip_index, f"grad_chip_sum_{i}") for i, (s, r) in enumerate(zip(chip_sums, received))]
    theirs = _sibling_swap_whole(halves)
    big = ("a_w_in", "a_w_out", "b_w_in", "b_w_out")
    big_halves = dict(zip(big, zip(halves, theirs)))

    grads = {
        "norm_g": g_norm, "a_conv_w": g_conv_mine[None], "a_log": g_alog[None], "a_dt_bias": g_dtb[None],
        "a_norm_g": g_ang[None], "b_q_norm_g": g_q[None], "b_k_norm_g": g_k[None]}
    weights = {"norm_g": norm_g, "a_w_in": a_w_in, "a_conv_w": a_conv_w, "a_log": a_log, "a_dt_bias": a_dt_bias,
               "a_norm_g": a_norm_g, "a_w_out": a_w_out, "b_w_in": b_w_in, "b_q_norm_g": b_q_norm_g,
               "b_k_norm_g": b_k_norm_g, "b_w_out": b_w_out}
    m_in = {"norm_g": m_norm_g, "a_w_in": m_a_w_in, "a_conv_w": m_a_conv_w, "a_log": m_a_log,
            "a_dt_bias": m_a_dt_bias, "a_norm_g": m_a_norm_g, "a_w_out": m_a_w_out, "b_w_in": m_b_w_in,
            "b_q_norm_g": m_b_q_norm_g, "b_k_norm_g": m_b_k_norm_g, "b_w_out": m_b_w_out}
    v_in = {"norm_g": v_norm_g, "a_w_in": v_a_w_in, "a_conv_w": v_a_conv_w, "a_log": v_a_log,
            "a_dt_bias": v_a_dt_bias, "a_norm_g": v_a_norm_g, "a_w_out": v_a_w_out, "b_w_in": v_b_w_in,
            "b_q_norm_g": v_b_q_norm_g, "b_k_norm_g": v_b_k_norm_g, "b_w_out": v_b_w_out}
    names = list(weights)

    delta_w, new_m, new_v = {}, {}, {}
    for nm in big:
        mine, other = big_halves[nm]
        if weights[nm].shape[2] % 128:
            cols = lambda a: jnp.transpose(a, (2, 0, 1))
            half_cols = lambda a: jnp.transpose(a)[:, None, :]
            outs = _adamw_shard_cols(cols(weights[nm]), half_cols(mine), half_cols(other), cols(m_in[nm]),
                                     cols(v_in[nm]), half_index, f"adamw_{nm}")
            outs = [jnp.transpose(o, (1, 2, 0)) for o in outs]
        else:
            outs = _adamw_shard(weights[nm], mine, other, m_in[nm], v_in[nm], half_index, f"adamw_{nm}")
        grads[nm], delta_w[nm], new_m[nm], new_v[nm] = outs
    small_names = [nm for nm in names if nm not in big]
    packs = [_pack_rows([src[nm] for nm in small_names]) for src in (weights, grads, m_in, v_in)]
    offs = packs[0][1]
    dl, m2, v2 = _adamw(packs[0][0], packs[1][0], packs[2][0], packs[3][0], "adamw_small")
    for nm, a, b, c2 in zip(small_names, _unpack_rows(dl, offs), _unpack_rows(m2, offs), _unpack_rows(v2, offs)):
        delta_w[nm], new_m[nm], new_v[nm] = a, b, c2

    return (loss, d_x0, *[grads[nm] for nm in names], *[delta_w[nm] for nm in names],
            *[new_m[nm] for nm in names], *[new_v[nm] for nm in names])
```

```python
import jax
import jax.numpy as jnp
import numpy as np
from jax import lax
from jax.experimental import pallas as pl
from jax.experimental.pallas import tpu as pltpu

F32 = jnp.float32
BF16 = jnp.bfloat16
MESH = pl.DeviceIdType.MESH

EPS = 1e-6
D_MODEL = 1024
A_HEADS = 8
A_DK = 128
A_DV = 256
A_QK = A_HEADS * A_DK
A_VW = A_HEADS * A_DV
A_MAIN = 2 * A_QK + 2 * A_VW
A_HEAD_COLS = 2 * A_DK + 2 * A_DV
A_CONV_COLS = 2 * A_DK + A_DV
A_CHUNK = 64
A_CONV = 4
B_GROUPS = 3
B_HEADS = 8
B_DH = 128
B_W = B_HEADS * B_DH
B_DIL = (1, 4, 16)
B_BLK = 128
ROPE_THETA = 500000.0
ROPE_DIMS = B_DH // 4
ADAM_LR, ADAM_B1, ADAM_B2, ADAM_EPS, ADAM_WD, ADAM_STEP = 0.001, 0.9, 0.999, 1e-08, 0.01, 10
N_CHIPS = 4
VMEM_BIG = 56 * 1024 * 1024


def _params(sem=None, vmem=None):
    return pltpu.CompilerParams(dimension_semantics=sem, vmem_limit_bytes=vmem)


def _dot(a, b, ca, cb):
    return lax.dot_general(a.astype(BF16), b.astype(BF16), (((ca,), (cb,)), ((), ())),
                           preferred_element_type=F32)


def _split3(a):
    hi = a.astype(BF16)
    r = a - hi.astype(F32)
    mid = r.astype(BF16)
    lo = (r - mid.astype(F32)).astype(BF16)
    return hi, mid, lo


def _sigmoid(y):
    return 1.0 / (1.0 + jnp.exp(-y))


def _silu(y):
    return y * _sigmoid(y)


def _dsilu(y):
    s = _sigmoid(y)
    return s * (1.0 + y * (1.0 - s))


def _matmul(a, b, mode, out_dtype, name, res=None, tm=1024, tn=1024, tk=1024, n=None, b_spec=None, into=None):
    m, k = a.shape[::-1] if mode == "tn" else a.shape
    if n is None:
        n = b.shape[0] if mode == "nt" else b.shape[1]
    tm, tn, tk = min(tm, m), min(tn, n), min(tk, k)
    assert m % tm == 0 and n % tn == 0 and k % tk == 0, (name, a.shape, b.shape)
    nk = k // tk
    dims = {"nn": ((1,), (0,)), "nt": ((1,), (1,)), "tn": ((0,), (0,))}[mode]

    def body(*refs):
        a_ref, b_ref = refs[0], refs[1]
        r_ref = refs[2] if res is not None else None
        o_ref = refs[2 + (res is not None) + (into is not None)]
        prod = lax.dot_general(a_ref[...], b_ref[...], (dims, ((), ())), preferred_element_type=F32)

        def finish(r):
            if res is not None:
                r = r + r_ref[...]
            o_ref[...] = r.astype(out_dtype)

        if nk == 1:
            finish(prod)
            return
        acc = refs[-1]
        kk = pl.program_id(2)

        @pl.when(kk == 0)
        def _():
            acc[...] = prod

        @pl.when((kk > 0) & (kk < nk - 1))
        def _():
            acc[...] += prod

        @pl.when(kk == nk - 1)
        def _():
            finish(acc[...] + prod)

    a_spec = pl.BlockSpec((tm, tk), lambda i, j, kk: (i, kk))
    if mode == "tn":
        a_spec = pl.BlockSpec((tk, tm), lambda i, j, kk: (kk, i))
    if b_spec is None and mode == "nt":
        b_spec = pl.BlockSpec((tn, tk), lambda i, j, kk: (j, kk))
    elif b_spec is None:
        b_spec = pl.BlockSpec((tk, tn), lambda i, j, kk: (kk, j))
    in_specs = [a_spec, b_spec]
    args = [a, b]
    if res is not None:
        in_specs.append(pl.BlockSpec((tm, tn), lambda i, j, kk: (i, j)))
        args.append(res)
    out_spec = pl.BlockSpec((tm, tn), lambda i, j, kk: (i, j))
    out_shape = jax.ShapeDtypeStruct((m, n), out_dtype)
    aliases = {}
    if into is not None:
        assert res is None
        buf, out_spec = into
        out_shape = jax.ShapeDtypeStruct(buf.shape, buf.dtype)
        in_specs.append(ANY)
        args.append(buf)
        aliases = {2: 0}
    return pl.pallas_call(
        body, name=name, grid=(m // tm, n // tn, nk),
        in_specs=in_specs, out_specs=out_spec, out_shape=out_shape, input_output_aliases=aliases,
        scratch_shapes=[pltpu.VMEM((tm, tn), F32)] if nk > 1 else [],
        compiler_params=_params(("parallel", "parallel", "arbitrary"), 48 * 1024 * 1024),
    )(*args)


def _rms_fwd(x, g, name, tm=256):
    t, d = x.shape

    def body(x_ref, g_ref, h_ref):
        xv = x_ref[...]
        r = lax.rsqrt(jnp.mean(xv * xv, axis=-1, keepdims=True) + EPS)
        h_ref[...] = (xv * r * g_ref[...]).astype(BF16)

    return pl.pallas_call(
        body, name=name, grid=(t // tm,),
        in_specs=[pl.BlockSpec((tm, d), lambda i: (i, 0)), pl.BlockSpec((1, d), lambda i: (0, 0))],
        out_specs=pl.BlockSpec((tm, d), lambda i: (i, 0)),
        out_shape=jax.ShapeDtypeStruct((t, d), BF16),
        compiler_params=_params(("parallel",)),
    )(x, g)


def _rms_bwd(x, g, dhs, dres, name, tm=256):
    t, d = x.shape
    n_dh = len(dhs)

    def body(*refs):
        x_ref, g_ref = refs[0], refs[1]
        dh_refs = refs[2:2 + n_dh]
        dres_ref, dx_ref, dg_ref = refs[2 + n_dh:]
        i = pl.program_id(0)

        @pl.when(i == 0)
        def _():
            dg_ref[...] = jnp.zeros_like(dg_ref)

        xv = x_ref[...]
        r = lax.rsqrt(jnp.mean(xv * xv, axis=-1, keepdims=True) + EPS)
        xh = xv * r
        dh = dh_refs[0][...]
        for ref in dh_refs[1:]:
            dh = dh + ref[...]
        dg_ref[0:1, :] += jnp.sum(dh * xh, axis=0, keepdims=True)
        dxh = dh * g_ref[...]
        dx = r * (dxh - xh * jnp.mean(dxh * xh, axis=-1, keepdims=True))
        dx_ref[...] = dx + dres_ref[...]

    row = pl.BlockSpec((tm, d), lambda i: (i, 0))
    dx, dg = pl.pallas_call(
        body, name=name, grid=(t // tm,),
        in_specs=[row, pl.BlockSpec((1, d), lambda i: (0, 0))] + [row] * n_dh + [row],
        out_specs=[row, pl.BlockSpec((8, d), lambda i: (0, 0))],
        out_shape=[jax.ShapeDtypeStruct((t, d), F32), jax.ShapeDtypeStruct((8, d), F32)],
        compiler_params=_params(("arbitrary",)),
    )(x, g, *dhs, dres)
    return dx, dg[0:1]


def _loss_grad(y, target, name="loss_grad", tm=256):
    t, d = y.shape
    nb = t // tm

    def body(y_ref, t_ref, dy_ref, part_ref):
        e = y_ref[...] - t_ref[...]
        dy_ref[...] = e * (1.0 / d)
        s = jnp.sum(jnp.sum(e * e, axis=1, keepdims=True), axis=0, keepdims=True) * (0.5 / d)
        part_ref[...] = jnp.broadcast_to(s, (8, 128))

    row = pl.BlockSpec((tm, d), lambda i: (i, 0))
    dy, part = pl.pallas_call(
        body, name=name, grid=(nb,), in_specs=[row, row],
        out_specs=[row, pl.BlockSpec((None, 8, 128), lambda i: (i, 0, 0))],
        out_shape=[jax.ShapeDtypeStruct((t, d), F32), jax.ShapeDtypeStruct((nb, 8, 128), F32)],
        compiler_params=_params(("parallel",)),
    )(y, target)
    return dy, part[:, 0, 0]


def _softplus(x):
    t = jnp.exp(-jnp.abs(x))
    return jnp.maximum(x, 0.0) + jnp.where(t < 1e-3, t * (1.0 - 0.5 * t), jnp.log(1.0 + t))


def _tri(rows_le_cols):
    r = lax.broadcasted_iota(jnp.int32, (A_CHUNK, A_CHUNK), 0)
    c = lax.broadcasted_iota(jnp.int32, (A_CHUNK, A_CHUNK), 1)
    return jnp.where((r <= c) if rows_le_cols else (r >= c), 1.0, 0.0).astype(BF16)


def _dot_exact_rhs(a, ones_bf16):
    dn = (((1,), (0,)), ((), ()))
    hi, mid, lo = _split3(a)
    out = lax.dot_general(hi, ones_bf16, dn, preferred_element_type=F32)
    out = out + lax.dot_general(mid, ones_bf16, dn, preferred_element_type=F32)
    return out + lax.dot_general(lo, ones_bf16, dn, preferred_element_type=F32)


def _gdn_prep(tail_t, a_log, dt_bias):
    bn, _, n, c = tail_t.shape

    def body(t_ref, alog_ref, dtb_ref, beta_ref, gc_ref):
        upper = _tri(True)
        for h in range(A_HEADS):
            beta_ref[h] = _sigmoid(t_ref[h])
            ea = jnp.exp(jnp.full((n, c), alog_ref[h], F32))
            g = -ea * _softplus(t_ref[A_HEADS + h] + dtb_ref[h])
            gc_ref[h] = _dot_exact_rhs(g, upper)

    smem = pl.BlockSpec(memory_space=pltpu.SMEM)
    blk = pl.BlockSpec((None, A_HEADS, n, c), lambda b: (b, 0, 0, 0))
    return pl.pallas_call(
        body, name="gdn_prep", grid=(bn,),
        in_specs=[pl.BlockSpec((None, 2 * A_HEADS, n, c), lambda b: (b, 0, 0, 0)), smem, smem],
        out_specs=[blk, blk],
        out_shape=[jax.ShapeDtypeStruct((bn, A_HEADS, n, c), F32)] * 2,
        compiler_params=_params(("parallel",)),
    )(tail_t, a_log, dt_bias)


def _gdn_prep_bwd(tail_t, a_log, dt_bias, d_gc, d_beta):
    bn, _, n, c = tail_t.shape

    def body(t_ref, alog_ref, dtb_ref, dgc_ref, dbeta_ref, dt_ref, dal_ref, ddt_ref):
        lower = _tri(False)
        for h in range(A_HEADS):
            beta = _sigmoid(t_ref[h])
            dt_ref[h] = dbeta_ref[h] * beta * (1.0 - beta)
            dg = _dot_exact_rhs(dgc_ref[h], lower)
            ea = jnp.exp(jnp.full((n, c), alog_ref[h], F32))
            xa = t_ref[A_HEADS + h] + dtb_ref[h]
            g = -ea * _softplus(xa)
            dxa = -ea * dg * _sigmoid(xa)
            dt_ref[A_HEADS + h] = dxa
            s1 = jnp.sum(jnp.sum(g * dg, axis=1, keepdims=True), axis=0, keepdims=True)
            s2 = jnp.sum(jnp.sum(dxa, axis=1, keepdims=True), axis=0, keepdims=True)
            dal_ref[h:h + 1, :] = jnp.broadcast_to(s1, (1, 128))
            ddt_ref[h:h + 1, :] = jnp.broadcast_to(s2, (1, 128))

    smem = pl.BlockSpec(memory_space=pltpu.SMEM)
    blk8 = pl.BlockSpec((None, A_HEADS, n, c), lambda b: (b, 0, 0, 0))
    blk16 = pl.BlockSpec((None, 2 * A_HEADS, n, c), lambda b: (b, 0, 0, 0))
    sm = pl.BlockSpec((None, A_HEADS, 128), lambda b: (b, 0, 0))
    return pl.pallas_call(
        body, name="gdn_prep_bwd", grid=(bn,),
        in_specs=[blk16, smem, smem, blk8, blk8],
        out_specs=[blk16, sm, sm],
        out_shape=[jax.ShapeDtypeStruct((bn, 2 * A_HEADS, n, c), F32),
                   jax.ShapeDtypeStruct((bn, A_HEADS, 128), F32),
                   jax.ShapeDtypeStruct((bn, A_HEADS, 128), F32)],
        compiler_params=_params(("parallel",)),
    )(tail_t, a_log, dt_bias, d_gc, d_beta)


HALO = 8


def _conv_taps(xw, w):
    y = w[A_CONV - 1:A_CONV, :] * xw
    for j in range(1, A_CONV):
        y = y + w[A_CONV - 1 - j:A_CONV - j, :] * pltpu.roll(xw, j, 0)
    return y[HALO:, :]


def _row_to_col(row, eye):
    c = eye.shape[0]
    return jnp.sum(jnp.where(eye, jnp.broadcast_to(row, (c, c)), 0.0), axis=1, keepdims=True)


def _col_to_row(col, eye):
    c = eye.shape[0]
    return jnp.sum(jnp.where(eye, jnp.broadcast_to(col, (c, c)), 0.0), axis=0, keepdims=True)


def _unit_lower_inverse(a, ri, ci):
    eye = jnp.where(ri == ci, 1.0, 0.0)
    a8 = jnp.where((ri >> 3) == (ci >> 3), a, 0.0)
    a2 = _dot(a8, a8, 1, 0)
    yield
    a4 = _dot(a2, a2, 1, 0)
    t = eye - a8
    t = t + _dot(t, a2, 1, 0)
    yield
    t = t + _dot(t, a4, 1, 0)
    yield
    for sh in (3, 4, 5):
        off = jnp.where(((ri >> (sh + 1)) == (ci >> (sh + 1))) & ((ri >> sh) != (ci >> sh)), a, 0.0)
        left = _dot(t, off, 1, 0)
        yield
        t = t - _dot(left, t, 1, 0)
        yield
    return t


def _round_robin(gens):
    live = list(gens)
    while live:
        nxt = []
        for g in live:
            try:
                next(g)
                nxt.append(g)
            except StopIteration:
                pass
        live = nxt


def _gdn_chunk_core(q, k, v, g_row, b_row, t_mat, ri, ci):
    eye = ri == ci
    g_col = _row_to_col(g_row, eye)
    b_col = _row_to_col(b_row, eye)
    causal = ri >= ci
    strict = ri > ci
    dec = jnp.where(causal, jnp.exp(jnp.where(causal, g_col - g_row, 0.0)), 0.0)
    gam = jnp.exp(g_col)
    g_last = g_row[:, A_CHUNK - 1:A_CHUNK]
    gam_last = jnp.exp(g_last)
    e = jnp.exp(g_last - g_col)
    kb = k * b_col
    bv = v * b_col
    kbg = kb * gam
    q16, k16, kb16 = q.astype(BF16), k.astype(BF16), kb.astype(BF16)
    kk = _dot(kb16, k16, 1, 1)
    p = _dot(q16, k16, 1, 1) * dec
    yield
    a_mat = jnp.where(strict, kk * dec, 0.0)
    if t_mat is None:
        t_mat = yield from _unit_lower_inverse(a_mat, ri, ci)
    t16 = t_mat.astype(BF16)
    u = _dot(t16, bv, 1, 0)
    w = _dot(t16, kbg, 1, 0)
    yield
    return dict(eye=eye, g_col=g_col, b_col=b_col, dec=dec, strict=strict, causal=causal, gam=gam,
                gam_last=gam_last, e=e, kb=kb, bv=bv, kbg=kbg, a_mat=a_mat, t_mat=t_mat, u=u, w=w, p=p,
                qg=q * gam, kd=k * e, q16=q16, k16=k16, kb16=kb16, t16=t16)


A_SEQ_BLK = 256
A_BLK_CHUNKS = A_SEQ_BLK // A_CHUNK


def _gdn_halo(proj_hm):
    bn, s, w = proj_hm.shape
    last = proj_hm.reshape(bn, s // A_SEQ_BLK, A_SEQ_BLK, w)[:, :, A_SEQ_BLK - HALO:, :]
    return jnp.concatenate([jnp.zeros((bn, 1, HALO, w), proj_hm.dtype), last[:, :-1]], axis=1)


def _gdn_window(x_ref, halo_ref, ci, first, lo):
    if first:
        return jnp.concatenate([halo_ref[:, lo:lo + A_CONV_COLS], x_ref[0:A_CHUNK, lo:lo + A_CONV_COLS]], axis=0)
    start = pl.multiple_of(ci * A_CHUNK - HALO, HALO)
    return x_ref[pl.ds(start, A_CHUNK + HALO), lo:lo + A_CONV_COLS]


def _gdn_chunk_prep(xw, cw, y=None):
    if y is None:
        y = _conv_taps(xw, cw)
    a = _silu(y)
    aq, ak, v = a[:, 0:A_DK], a[:, A_DK:2 * A_DK], a[:, 2 * A_DK:]
    rq = lax.rsqrt(jnp.sum(aq * aq, axis=1, keepdims=True) + EPS)
    rk = lax.rsqrt(jnp.sum(ak * ak, axis=1, keepdims=True) + EPS)
    return dict(xw=xw, y=y, aq=aq, ak=ak, rq=rq, rk=rk, q=aq * rq * (A_DK ** -0.5), k=ak * rk, v=v)


def _gdn_fwd(proj_hm, cw_hm, beta, gc, norm_g, hp=8):
    bn, s, _ = proj_hm.shape
    n = s // A_CHUNK
    nsb = s // A_SEQ_BLK
    halo = _gdn_halo(proj_hm)

    def body(x_ref, halo_ref, cw_ref, beta_ref, gc_ref, ng_ref, og_ref, oraw_ref, st_ref, t_ref, y_ref, state):
        first_chunk = pl.program_id(2) * A_BLK_CHUNKS
        ri = lax.broadcasted_iota(jnp.int32, (A_CHUNK, A_CHUNK), 0)
        ci_ = lax.broadcasted_iota(jnp.int32, (A_CHUNK, A_CHUNK), 1)
        ng = ng_ref[...]

        @pl.when(pl.program_id(2) == 0)
        def _():
            state[...] = jnp.zeros_like(state)

        def one_head(hh, ci, first, rows):
            lo = hh * A_HEAD_COLS
            cw = cw_ref[:, hh * A_CONV_COLS:(hh + 1) * A_CONV_COLS]
            cin = _gdn_chunk_prep(_gdn_window(x_ref, halo_ref, ci, first, lo), cw)
            y_ref[rows, hh * A_CONV_COLS:(hh + 1) * A_CONV_COLS] = cin["y"]
            seq_chunk = pl.ds(first_chunk + ci, 1)
            core = yield from _gdn_chunk_core(cin["q"], cin["k"], cin["v"], gc_ref[hh, seq_chunk, :],
                                              beta_ref[hh, seq_chunk, :], None, ri, ci_)
            st = state[hh]
            st_ref[hh, ci] = st
            t_ref[hh, ci] = core["t_mat"]
            st16 = st.astype(BF16)
            vn = core["u"] - _dot(core["w"], st16, 1, 0)
            qs = _dot(core["qg"], st16, 1, 0)
            yield
            vn16 = vn.astype(BF16)
            o = qs + _dot(core["p"], vn16, 1, 0)
            state[hh] = st * core["gam_last"] + _dot(core["kd"], vn16, 0, 0)
            yield
            ocols = slice(hh * A_DV, (hh + 1) * A_DV)
            oraw_ref[rows, ocols] = o
            r = lax.rsqrt(jnp.mean(o * o, axis=1, keepdims=True) + EPS)
            z = x_ref[rows, lo + A_CONV_COLS:lo + A_HEAD_COLS]
            og_ref[rows, ocols] = (o * r * ng * _silu(z)).astype(BF16)

        def chunk(ci, first):
            rows = pl.ds(0 if first else pl.multiple_of(ci * A_CHUNK, A_CHUNK), A_CHUNK)
            _round_robin([one_head(hh, ci, first, rows) for hh in range(hp)])

        chunk(0, True)
        lax.fori_loop(1, A_BLK_CHUNKS, lambda i, c: (chunk(i, False), c)[1], 0)

    small = pl.BlockSpec((None, hp, n, A_CHUNK), lambda b, h, j: (b, h, 0, 0))
    return pl.pallas_call(
        body, name="gdn_fwd", grid=(bn, A_HEADS // hp, nsb),
        in_specs=[pl.BlockSpec((None, A_SEQ_BLK, hp * A_HEAD_COLS), lambda b, h, j: (b, j, h)),
                  pl.BlockSpec((None, None, HALO, hp * A_HEAD_COLS), lambda b, h, j: (b, j, 0, h)),
                  pl.BlockSpec((A_CONV, hp * A_CONV_COLS), lambda b, h, j: (0, h)),
                  small, small,
                  pl.BlockSpec((1, A_DV), lambda b, h, j: (0, 0))],
        out_specs=[pl.BlockSpec((None, A_SEQ_BLK, hp * A_DV), lambda b, h, j: (b, j, h)),
                   pl.BlockSpec((None, A_SEQ_BLK, hp * A_DV), lambda b, h, j: (b, j, h)),
                   pl.BlockSpec((None, hp, A_BLK_CHUNKS, A_DK, A_DV), lambda b, h, j: (b, h, j, 0, 0)),
                   pl.BlockSpec((None, hp, A_BLK_CHUNKS, A_CHUNK, A_CHUNK), lambda b, h, j: (b, h, j, 0, 0)),
                   pl.BlockSpec((None, A_SEQ_BLK, hp * A_CONV_COLS), lambda b, h, j: (b, j, h))],
        out_shape=[jax.ShapeDtypeStruct((bn, s, A_VW), BF16),
                   jax.ShapeDtypeStruct((bn, s, A_VW), F32),
                   jax.ShapeDtypeStruct((bn, A_HEADS, n, A_DK, A_DV), F32),
                   jax.ShapeDtypeStruct((bn, A_HEADS, n, A_CHUNK, A_CHUNK), F32),
                   jax.ShapeDtypeStruct((bn, s, A_HEADS * A_CONV_COLS), F32)],
        scratch_shapes=[pltpu.VMEM((hp, A_DK, A_DV), F32)],
        compiler_params=_params(("parallel", "parallel", "arbitrary"), VMEM_BIG),
    )(proj_hm, halo, cw_hm, beta, gc, norm_g)


def _gdn_bwd(proj_hm, cw_hm, beta, gc, norm_g, oraw, states, t_mats, conv_y, dog, hp=4):
    bn, s, _ = proj_hm.shape
    n = s // A_CHUNK
    nsb = s // A_SEQ_BLK
    halo = _gdn_halo(proj_hm)

    def body(x_ref, halo_ref, cw_ref, beta_ref, gc_ref, ng_ref, oraw_ref, st_ref, t_ref, y_ref, dog_ref,
             dx_ref, dgc_ref, dbeta_ref, dcw_ref, dng_ref, dstate, dy_next, shifted):
        first_chunk = (nsb - 1 - pl.program_id(2)) * A_BLK_CHUNKS
        ri = lax.broadcasted_iota(jnp.int32, (A_CHUNK, A_CHUNK), 0)
        ci_ = lax.broadcasted_iota(jnp.int32, (A_CHUNK, A_CHUNK), 1)
        lane = lax.broadcasted_iota(jnp.int32, (1, A_CHUNK), 1)
        ng = ng_ref[...]

        @pl.when(pl.program_id(2) == 0)
        def _():
            dstate[...] = jnp.zeros_like(dstate)
            dy_next[...] = jnp.zeros_like(dy_next)
            dcw_ref[...] = jnp.zeros_like(dcw_ref)
            dng_ref[...] = jnp.zeros_like(dng_ref)

        def one_head(hh, ci, first, rows):
            lo = hh * A_HEAD_COLS
            ccols = slice(hh * A_CONV_COLS, (hh + 1) * A_CONV_COLS)
            ocols = slice(hh * A_DV, (hh + 1) * A_DV)
            cw = cw_ref[:, ccols]
            cin = _gdn_chunk_prep(_gdn_window(x_ref, halo_ref, ci, first, lo), cw, y_ref[rows, ccols])
            q, k, v = cin["q"], cin["k"], cin["v"]
            seq_chunk = pl.ds(first_chunk + ci, 1)
            cr = yield from _gdn_chunk_core(q, k, v, gc_ref[hh, seq_chunk, :], beta_ref[hh, seq_chunk, :],
                                            t_ref[hh, ci], ri, ci_)
            eye, dec, gam, e = cr["eye"], cr["dec"], cr["gam"], cr["e"]
            b_col, t_mat, u, w, p = cr["b_col"], cr["t_mat"], cr["u"], cr["w"], cr["p"]
            st = st_ref[hh, ci]
            ds_out = dstate[hh]

            o = oraw_ref[rows, ocols]
            z = x_ref[rows, lo + A_CONV_COLS:lo + A_HEAD_COLS]
            d_og = dog_ref[rows, ocols]
            r = lax.rsqrt(jnp.mean(o * o, axis=1, keepdims=True) + EPS)
            oh = o * r
            d_on = d_og * _silu(z)
            dz = d_og * oh * ng * _dsilu(z)
            dng_ref[hh, 0:1, :] += jnp.sum(d_on * oh, axis=0, keepdims=True)
            d_oh = d_on * ng
            d_o = r * (d_oh - oh * jnp.mean(d_oh * oh, axis=1, keepdims=True))

            st16, ds16, do16, w16 = st.astype(BF16), ds_out.astype(BF16), d_o.astype(BF16), w.astype(BF16)
            q16, k16, t16 = cr["q16"], cr["k16"], cr["t16"]
            vn = u - _dot(w16, st16, 1, 0)
            d_vn = _dot(p, do16, 0, 0) + _dot(cr["kd"], ds16, 1, 0)
            d_qg = _dot(do16, st16, 1, 1)
            qgdo = _dot(cr["qg"], do16, 0, 0)
            yield
            vn16, dvn16 = vn.astype(BF16), d_vn.astype(BF16)
            d_p = jnp.where(cr["causal"], _dot(do16, vn16, 1, 1), 0.0)
            d_kd = _dot(vn16, ds16, 1, 1)
            d_gam_last = jnp.sum(jnp.sum(st * ds_out, axis=1, keepdims=True), axis=0, keepdims=True)
            d_w = -_dot(dvn16, st16, 1, 1)
            dstate[hh] = qgdo + ds_out * cr["gam_last"] - _dot(w16, dvn16, 0, 0)
            d_bv = _dot(t16, dvn16, 0, 0)
            yield
            d_kbg = _dot(t16, d_w, 0, 0)
            n_p = (d_p * dec).astype(BF16)
            d_q = _dot(n_p, k16, 1, 0) + d_qg * gam
            npq = _dot(n_p, q16, 0, 0)
            yield
            d_a = jnp.where(cr["strict"], -(_dot(d_bv, u, 1, 1) + _dot(d_kbg, w16, 1, 1)), 0.0)
            yield
            m_a = (d_a * dec).astype(BF16)
            d_kb = _dot(m_a, k16, 1, 0) + d_kbg * gam
            d_k = (_dot(m_a, cr["kb16"], 0, 0) + npq + d_kd * e + d_kb * b_col)
            yield
            d_v = d_bv * b_col
            d_beta_col = (jnp.sum(d_bv * v, axis=1, keepdims=True)
                          + jnp.sum(d_kb * k, axis=1, keepdims=True))
            gterm = d_a * cr["a_mat"] + d_p * p
            d_e = jnp.sum(d_kd * k, axis=1, keepdims=True) * e
            d_g_col = (jnp.sum(gterm, axis=1, keepdims=True)
                       + (jnp.sum(d_qg * q, axis=1, keepdims=True)
                          + jnp.sum(d_kbg * cr["kb"], axis=1, keepdims=True)) * gam
                       - d_e)
            d_g_last = jnp.sum(d_e, axis=0, keepdims=True) + d_gam_last * cr["gam_last"]
            d_g_row = (_col_to_row(d_g_col, eye) - jnp.sum(gterm, axis=0, keepdims=True)
                       + jnp.where(lane == A_CHUNK - 1, d_g_last, 0.0))
            dgc_ref[hh, seq_chunk, :] = d_g_row
            dbeta_ref[hh, seq_chunk, :] = _col_to_row(d_beta_col, eye)

            qh = cin["aq"] * cin["rq"]
            kh = cin["ak"] * cin["rk"]
            d_qh = d_q * (A_DK ** -0.5)
            d_aq = cin["rq"] * (d_qh - qh * jnp.sum(d_qh * qh, axis=1, keepdims=True))
            d_ak = cin["rk"] * (d_k - kh * jnp.sum(d_k * kh, axis=1, keepdims=True))
            d_y = jnp.concatenate([d_aq, d_ak, d_v], axis=1) * _dsilu(cin["y"])
            shifted[hh, 0, 0:A_CHUNK, :] = d_y
            shifted[hh, 0, A_CHUNK:A_CHUNK + HALO, :] = dy_next[hh]
            shifted[hh, 1, 0:A_CHUNK + HALO, :] = cin["xw"]
            d_x = cw[A_CONV - 1:A_CONV, :] * d_y
            for j in range(1, A_CONV):
                d_x = d_x + cw[A_CONV - 1 - j:A_CONV - j, :] * shifted[hh, 0, j:j + A_CHUNK, :]
            for j in range(A_CONV):
                xs = shifted[hh, 1, HALO - j:HALO - j + A_CHUNK, :]
                dcw_ref[A_CONV - 1 - j:A_CONV - j, ccols] += jnp.sum(d_y * xs, axis=0, keepdims=True)
            dy_next[hh] = d_y[0:HALO, :]
            dx_ref[rows, lo:lo + A_CONV_COLS] = d_x.astype(BF16)
            dx_ref[rows, lo + A_CONV_COLS:lo + A_HEAD_COLS] = dz.astype(BF16)

        def chunk(ci, first):
            rows = pl.ds(0 if first else pl.multiple_of(ci * A_CHUNK, A_CHUNK), A_CHUNK)
            _round_robin([one_head(hh, ci, first, rows) for hh in range(hp)])

        lax.fori_loop(0, A_BLK_CHUNKS - 1, lambda i, c: (chunk(A_BLK_CHUNKS - 1 - i, False), c)[1], 0)
        chunk(0, True)

    rev = lambda j: nsb - 1 - j
    small = pl.BlockSpec((None, hp, n, A_CHUNK), lambda b, h, j: (b, h, 0, 0))
    wide = pl.BlockSpec((None, A_SEQ_BLK, hp * A_HEAD_COLS), lambda b, h, j: (b, rev(j), h))
    val = pl.BlockSpec((None, A_SEQ_BLK, hp * A_DV), lambda b, h, j: (b, rev(j), h))
    return pl.pallas_call(
        body, name="gdn_bwd", grid=(bn, A_HEADS // hp, nsb),
        in_specs=[wide,
                  pl.BlockSpec((None, None, HALO, hp * A_HEAD_COLS), lambda b, h, j: (b, rev(j), 0, h)),
                  pl.BlockSpec((A_CONV, hp * A_CONV_COLS), lambda b, h, j: (0, h)),
                  small, small,
                  pl.BlockSpec((1, A_DV), lambda b, h, j: (0, 0)),
                  val,
                  pl.BlockSpec((None, hp, A_BLK_CHUNKS, A_DK, A_DV), lambda b, h, j: (b, h, rev(j), 0, 0)),
                  pl.BlockSpec((None, hp, A_BLK_CHUNKS, A_CHUNK, A_CHUNK), lambda b, h, j: (b, h, rev(j), 0, 0)),
                  pl.BlockSpec((None, A_SEQ_BLK, hp * A_CONV_COLS), lambda b, h, j: (b, rev(j), h)),
                  val],
        out_specs=[wide, small, small,
                   pl.BlockSpec((None, A_CONV, hp * A_CONV_COLS), lambda b, h, j: (b, 0, h)),
                   pl.BlockSpec((None, hp, 8, A_DV), lambda b, h, j: (b, h, 0, 0))],
        out_shape=[jax.ShapeDtypeStruct((bn, s, A_HEADS * A_HEAD_COLS), BF16),
                   jax.ShapeDtypeStruct((bn, A_HEADS, n, A_CHUNK), F32),
                   jax.ShapeDtypeStruct((bn, A_HEADS, n, A_CHUNK), F32),
                   jax.ShapeDtypeStruct((bn, A_CONV, A_HEADS * A_CONV_COLS), F32),
                   jax.ShapeDtypeStruct((bn, A_HEADS, 8, A_DV), F32)],
        scratch_shapes=[pltpu.VMEM((hp, A_DK, A_DV), F32), pltpu.VMEM((hp, HALO, A_CONV_COLS), F32),
                        pltpu.VMEM((hp, 2, A_CHUNK + 2 * HALO, A_CONV_COLS), F32)],
        compiler_params=_params(("parallel", "parallel", "arbitrary"), VMEM_BIG),
    )(proj_hm, halo, cw_hm, beta, gc, norm_g, oraw, states, t_mats, conv_y, dog)


def _rope_tables(posf, inv_freq_row):
    t = posf.shape[0]
    tm = 512

    def body(p_ref, f_ref, c_ref, sa_ref, sb_ref):
        ang = p_ref[...] * f_ref[...]
        lane = lax.broadcasted_iota(jnp.int32, ang.shape, 1)
        half = ROPE_DIMS // 2
        c_ref[...] = jnp.where(lane < ROPE_DIMS, jnp.cos(ang), 1.0)
        sn = jnp.sin(ang)
        sa_ref[...] = jnp.where(lane < half, -sn, 0.0)
        sb_ref[...] = jnp.where((lane >= half) & (lane < ROPE_DIMS), sn, 0.0)

    row = pl.BlockSpec((tm, 128), lambda i: (i, 0))
    return pl.pallas_call(
        body, name="rope_tables", grid=(t // tm,),
        in_specs=[row, pl.BlockSpec((1, 128), lambda i: (0, 0))], out_specs=[row] * 3,
        out_shape=[jax.ShapeDtypeStruct((t, 128), F32)] * 3,
        compiler_params=_params(("parallel",)),
    )(posf, inv_freq_row)


def _rope(x, c, sa, sb):
    half = ROPE_DIMS // 2
    return x * c + pltpu.roll(x, 128 - half, 1) * sa + pltpu.roll(x, half, 1) * sb


def _rope_t(d, c, sa, sb):
    half = ROPE_DIMS // 2
    return d * c + pltpu.roll(d * sa, half, 1) + pltpu.roll(d * sb, 128 - half, 1)


def _qk_prep(proj, c, sa, sb, qg, kg, name, tm=256):
    t = proj.shape[0]

    def body(x_ref, c_ref, sa_ref, sb_ref, qg_ref, kg_ref, o_ref):
        cc, s1, s2 = c_ref[...], sa_ref[...], sb_ref[...]
        half = ROPE_DIMS // 2

        def one_head(lo, g):
            xv = x_ref[:, lo:lo + B_DH]
            ms = jnp.mean(xv * xv, axis=1, keepdims=True)
            yield
            xn = xv * lax.rsqrt(ms + EPS) * g
            r1, r2 = pltpu.roll(xn, 128 - half, 1), pltpu.roll(xn, half, 1)
            yield
            o_ref[:, lo:lo + B_DH] = (xn * cc + r1 * s1 + r2 * s2).astype(BF16)

        for which, g_ref in ((0, qg_ref), (1, kg_ref)):
            g = g_ref[...]
            _round_robin([one_head(which * B_W + h * B_DH, g) for h in range(B_HEADS)])
        o_ref[:, 2 * B_W:3 * B_W] = x_ref[:, 2 * B_W:3 * B_W].astype(BF16)

    tab = pl.BlockSpec((tm, 128), lambda i: (i, 0))
    gain = pl.BlockSpec((1, B_DH), lambda i: (0, 0))
    return pl.pallas_call(
        body, name=name, grid=(t // tm,),
        in_specs=[pl.BlockSpec((tm, 3 * B_W), lambda i: (i, 0)), tab, tab, tab, gain, gain],
        out_specs=pl.BlockSpec((tm, 3 * B_W), lambda i: (i, 0)),
        out_shape=jax.ShapeDtypeStruct((t, 3 * B_W), BF16),
        compiler_params=_params(("parallel",), 40 * 1024 * 1024),
    )(proj, c, sa, sb, qg, kg)


def _qk_prep_bwd(proj, c, sa, sb, qg, kg, dq, dk, dv, dz, name, tm=256):
    t = proj.shape[0]
    out_w = 3 * B_W + (B_W if dz is not None else 0)

    def body(*refs):
        x_ref, c_ref, sa_ref, sb_ref, qg_ref, kg_ref, dq_ref, dk_ref, dv_ref = refs[:9]
        if dz is not None:
            dz_ref, o_ref, dgain_ref = refs[9:]
        else:
            o_ref, dgain_ref = refs[9:]
        i = pl.program_id(0)

        @pl.when(i == 0)
        def _():
            dgain_ref[...] = jnp.zeros_like(dgain_ref)

        cc, s1, s2 = c_ref[...], sa_ref[...], sb_ref[...]
        half = ROPE_DIMS // 2

        def one_head(which, h, g, d_ref, parts):
            lo = which * B_W + h * B_DH
            xv = x_ref[:, lo:lo + B_DH]
            d_out = d_ref[:, h * B_DH:(h + 1) * B_DH].astype(F32)
            ms = jnp.mean(xv * xv, axis=1, keepdims=True)
            r1, r2 = pltpu.roll(d_out * s1, half, 1), pltpu.roll(d_out * s2, 128 - half, 1)
            yield
            r = lax.rsqrt(ms + EPS)
            xh = xv * r
            d_xn = d_out * cc + r1 + r2
            parts.append(jnp.sum(d_xn * xh, axis=0, keepdims=True))
            d_xh = d_xn * g
            dot = jnp.mean(d_xh * xh, axis=1, keepdims=True)
            yield
            o_ref[:, lo:lo + B_DH] = (r * (d_xh - xh * dot)).astype(BF16)

        for which, g_ref, d_ref in ((0, qg_ref, dq_ref), (1, kg_ref, dk_ref)):
            parts = []
            _round_robin([one_head(which, h, g_ref[...], d_ref, parts) for h in range(B_HEADS)])
            acc = parts[0]
            for part in parts[1:]:
                acc = acc + part
            dgain_ref[which:which + 1, :] += acc
        o_ref[:, 2 * B_W:3 * B_W] = dv_ref[...]
        if dz is not None:
            o_ref[:, 3 * B_W:4 * B_W] = dz_ref[...]

    tab = pl.BlockSpec((tm, 128), lambda i: (i, 0))
    gain = pl.BlockSpec((1, B_DH), lambda i: (0, 0))
    grad = pl.BlockSpec((tm, B_W), lambda i: (i, 0))
    in_specs = [pl.BlockSpec((tm, 2 * B_W), lambda i: (i, 0)), tab, tab, tab, gain, gain, grad, grad, grad]
    args = [proj, c, sa, sb, qg, kg, dq, dk, dv]
    if dz is not None:
        in_specs.append(grad)
        args.append(dz)
    return pl.pallas_call(
        body, name=name, grid=(t // tm,), in_specs=in_specs,
        out_specs=[pl.BlockSpec((tm, out_w), lambda i: (i, 0)), pl.BlockSpec((8, B_DH), lambda i: (0, 0))],
        out_shape=[jax.ShapeDtypeStruct((t, out_w), BF16), jax.ShapeDtypeStruct((8, B_DH), F32)],
        compiler_params=_params(("arbitrary",), 40 * 1024 * 1024),
    )(*args)


def _attn_masks():
    qi = lax.broadcasted_iota(jnp.int32, (B_BLK, 2 * B_BLK), 0)
    kj = lax.broadcasted_iota(jnp.int32, (B_BLK, 2 * B_BLK), 1)
    two = (kj >= qi) & (kj <= qi + B_BLK)
    q1 = lax.broadcasted_iota(jnp.int32, (B_BLK, B_BLK), 0)
    k1 = lax.broadcasted_iota(jnp.int32, (B_BLK, B_BLK), 1)
    return k1 <= q1, two


def _lane_pick(ref_rows, h):
    lane = lax.broadcasted_iota(jnp.int32, ref_rows.shape, 1)
    return jnp.sum(jnp.where(lane == h, ref_rows, 0.0), axis=1, keepdims=True)


B_ROWS = 2048


def _attn_schedule(nb, sb, block):
    way = 4

    def run(items):
        for at in range(0, len(items), way):
            _round_robin([block(*it) for it in items[at:at + way]])

    run([(si, 0, True) for si in range(sb)])
    if nb == 1:
        return
    per = max(1, way // sb)
    lead = 1 + (nb - 1) % per
    if lead > 1:
        run([(si, i, False) for i in range(1, lead) for si in range(sb)])

    def step(it, carry):
        run([(si, lead + it * per + u, False) for u in range(per) for si in range(sb)])
        return carry

    lax.fori_loop(0, (nb - lead) // per, step, 0)


def _attn_rows(i, first):
    if first:
        return pl.ds(0, B_BLK), pl.ds(0, B_BLK)
    rows = pl.ds(pl.multiple_of(i * B_BLK, B_BLK), B_BLK)
    return rows, pl.ds(pl.multiple_of((i - 1) * B_BLK, B_BLK), 2 * B_BLK)


def _attn_fwd(qkv, name):
    ns, ln, _ = qkv.shape
    nb = ln // B_BLK
    sb = B_ROWS // ln
    scale = B_DH ** -0.5

    def body(q_ref, k_ref, v_ref, o_ref, lse_ref):
        h = pl.program_id(1)
        mask1, mask2 = _attn_masks()
        lane = lax.broadcasted_iota(jnp.int32, (B_BLK, B_HEADS), 1)

        @pl.when(h == 0)
        def _():
            lse_ref[...] = jnp.zeros_like(lse_ref)

        def block(si, i, first):
            rows, win = _attn_rows(i, first)
            mask = mask1 if first else mask2
            sc = jnp.where(mask, _dot(q_ref[si, rows, :], k_ref[si, win, :], 1, 1) * scale, -1e30)
            yield
            m = jnp.max(sc, axis=1, keepdims=True)
            p = jnp.exp(sc - m)
            l = jnp.sum(p, axis=1, keepdims=True)
            pv = _dot(p, v_ref[si, win, :], 1, 0)
            yield
            o_ref[si, rows, :] = pv / l
            lse_ref[si, rows, :] = jnp.where(lane == h, m + jnp.log(l), lse_ref[si, rows, :])

        _attn_schedule(nb, sb, block)

    head = lambda off: pl.BlockSpec((sb, ln, B_DH), lambda s, h: (s, 0, off + h))
    return pl.pallas_call(
        body, name=name, grid=(ns // sb, B_HEADS),
        in_specs=[head(0), head(B_HEADS), head(2 * B_HEADS)],
        out_specs=[head(0), pl.BlockSpec((sb, ln, B_HEADS), lambda s, h: (s, 0, 0))],
        out_shape=[jax.ShapeDtypeStruct((ns, ln, B_W), F32), jax.ShapeDtypeStruct((ns, ln, B_HEADS), F32)],
        compiler_params=_params(("parallel", "arbitrary")),
    )(qkv, qkv, qkv)


def _attn_bwd(qkv, d_o, lse_joint, delta, name):
    ns, ln, _ = qkv.shape
    nb = ln // B_BLK
    sb = B_ROWS // ln
    scale = B_DH ** -0.5

    def body(q_ref, k_ref, v_ref, do_ref, lj_ref, dl_ref, dq_ref, dk_out, dv_out, dk_ref, dv_ref):
        h = pl.program_id(1)
        mask1, mask2 = _attn_masks()
        dk_ref[...] = jnp.zeros_like(dk_ref)
        dv_ref[...] = jnp.zeros_like(dv_ref)

        def block(si, i, first):
            rows, win = _attn_rows(i, first)
            mask = mask1 if first else mask2
            q = q_ref[si, rows, :]
            d_out = do_ref[si, rows, :]
            l_col = _lane_pick(lj_ref[si, rows, :], h)
            d_col = _lane_pick(dl_ref[si, rows, :], h)
            sc = _dot(q, k_ref[si, win, :], 1, 1) * scale
            d_p = _dot(d_out, v_ref[si, win, :], 1, 1)
            yield
            p = jnp.exp(jnp.where(mask, sc - l_col, -1e30))
            d_s = p * (d_p - d_col) * scale
            d_q = _dot(d_s, k_ref[si, win, :], 1, 0)
            d_k = _dot(d_s, q, 0, 0)
            d_v = _dot(p, d_out, 0, 0)
            yield
            dq_ref[si, rows, :] = d_q.astype(BF16)
            dk_ref[si, win, :] += d_k
            dv_ref[si, win, :] += d_v

        _attn_schedule(nb, sb, block)
        dk_out[...] = dk_ref[...].astype(BF16)
        dv_out[...] = dv_ref[...].astype(BF16)

    head = lambda off: pl.BlockSpec((sb, ln, B_DH), lambda s, h: (s, 0, off + h))
    small = pl.BlockSpec((sb, ln, B_HEADS), lambda s, h: (s, 0, 0))
    return pl.pallas_call(
        body, name=name, grid=(ns // sb, B_HEADS),
        in_specs=[head(0), head(B_HEADS), head(2 * B_HEADS), head(0), small, small],
        out_specs=[head(0)] * 3,
        out_shape=[jax.ShapeDtypeStruct((ns, ln, B_W), BF16)] * 3,
        scratch_shapes=[pltpu.VMEM((sb, ln, B_DH), F32)] * 2,
        compiler_params=_params(("parallel", "parallel")),
    )(qkv, qkv, qkv, d_o, lse_joint, delta)


def _merge_weights(lse_refs):
    ls = [r[...] for r in lse_refs]
    m = jnp.maximum(jnp.maximum(ls[0], ls[1]), ls[2])
    es = [jnp.exp(l - m) for l in ls]
    tot = es[0] + es[1] + es[2]
    return [e / tot for e in es], m + jnp.log(tot)


def _merge_fwd(outs, lses, proj0, tm=256):
    t = outs[0].shape[0]

    def body(o0, o1, o2, l0, l1, l2, z_ref, og_ref):
        wts, _ = _merge_weights((l0, l1, l2))
        for h in range(B_HEADS):
            cols = slice(h * B_DH, (h + 1) * B_DH)
            o = (wts[0][:, h:h + 1] * o0[:, cols] + wts[1][:, h:h + 1] * o1[:, cols]
                 + wts[2][:, h:h + 1] * o2[:, cols])
            og_ref[:, cols] = (o * _silu(z_ref[:, cols])).astype(BF16)

    wide = pl.BlockSpec((tm, B_W), lambda i: (i, 0))
    small = pl.BlockSpec((tm, B_HEADS), lambda i: (i, 0))
    return pl.pallas_call(
        body, name="merge_fwd", grid=(t // tm,),
        in_specs=[wide] * 3 + [small] * 3 + [pl.BlockSpec((tm, B_W), lambda i: (i, 3))],
        out_specs=wide, out_shape=jax.ShapeDtypeStruct((t, B_W), BF16),
        compiler_params=_params(("parallel",)),
    )(*outs, *lses, proj0)


def _merge_bwd(outs, lses, proj0, d_og, tm=256):
    t = outs[0].shape[0]

    def body(o0, o1, o2, l0, l1, l2, z_ref, dog_ref, do_ref, lj_ref, dl_ref, dz_ref):
        wts, lj = _merge_weights((l0, l1, l2))
        lj_ref[...] = lj
        lane = lax.broadcasted_iota(jnp.int32, (tm, B_HEADS), 1)
        delta = jnp.zeros((tm, B_HEADS), F32)
        for h in range(B_HEADS):
            cols = slice(h * B_DH, (h + 1) * B_DH)
            o = (wts[0][:, h:h + 1] * o0[:, cols] + wts[1][:, h:h + 1] * o1[:, cols]
                 + wts[2][:, h:h + 1] * o2[:, cols])
            z = z_ref[:, cols]
            d_g = dog_ref[:, cols]
            d_out = d_g * _silu(z)
            dz_ref[:, cols] = (d_g * o * _dsilu(z)).astype(BF16)
            do_ref[:, cols] = d_out.astype(BF16)
            delta = jnp.where(lane == h, jnp.sum(d_out * o, axis=1, keepdims=True), delta)
        dl_ref[...] = delta

    wide = pl.BlockSpec((tm, B_W), lambda i: (i, 0))
    small = pl.BlockSpec((tm, B_HEADS), lambda i: (i, 0))
    return pl.pallas_call(
        body, name="merge_bwd", grid=(t // tm,),
        in_specs=[wide] * 3 + [small] * 3 + [pl.BlockSpec((tm, B_W), lambda i: (i, 3)), wide],
        out_specs=[wide, small, small, wide],
        out_shape=[jax.ShapeDtypeStruct((t, B_W), BF16), jax.ShapeDtypeStruct((t, B_HEADS), F32),
                   jax.ShapeDtypeStruct((t, B_HEADS), F32), jax.ShapeDtypeStruct((t, B_W), BF16)],
        compiler_params=_params(("parallel",)),
    )(*outs, *lses, proj0, d_og)


def _adamw(w, g, m, v, name):
    r, c = w.shape
    tr = r
    for cand in (256, 128, 64, 32, 16, 8):
        if r % cand == 0:
            tr = cand
            break

    def body(w_ref, g_ref, m_ref, v_ref, d_ref, nm_ref, nv_ref):
        gv = g_ref[...]
        nm = ADAM_B1 * m_ref[...] + (1.0 - ADAM_B1) * gv
        nv = ADAM_B2 * v_ref[...] + (1.0 - ADAM_B2) * (gv * gv)
        m_hat = nm / (1.0 - ADAM_B1 ** ADAM_STEP)
        v_hat = nv / (1.0 - ADAM_B2 ** ADAM_STEP)
        d_ref[...] = -ADAM_LR * (m_hat / (jnp.sqrt(v_hat) + ADAM_EPS) + ADAM_WD * w_ref[...])
        nm_ref[...] = nm
        nv_ref[...] = nv

    blk = pl.BlockSpec((tr, c), lambda i: (i, 0))
    return pl.pallas_call(
        body, name=name, grid=(r // tr,), in_specs=[blk] * 4, out_specs=[blk] * 3,
        out_shape=[jax.ShapeDtypeStruct((r, c), F32)] * 3,
        compiler_params=_params(("parallel",)),
    )(w, g, m, v)


def _adam_update(w, gv, m, v):
    nm = ADAM_B1 * m + (1.0 - ADAM_B1) * gv
    nv = ADAM_B2 * v + (1.0 - ADAM_B2) * (gv * gv)
    m_hat = nm / (1.0 - ADAM_B1 ** ADAM_STEP)
    v_hat = nv / (1.0 - ADAM_B2 ** ADAM_STEP)
    return -ADAM_LR * (m_hat / (jnp.sqrt(v_hat) + ADAM_EPS) + ADAM_WD * w), nm, nv


def _adamw_shard(w, mine, theirs, m, v, half_index, name, tr=128):
    _, r, c = w.shape
    nhb = (r // 2) // tr

    def body(c_ref, w_ref, mine_ref, theirs_ref, m_ref, v_ref, g_ref, d_ref, nm_ref, nv_ref):
        is_mine = (pl.program_id(0) // nhb) == c_ref[0]
        gv = jnp.where(is_mine, mine_ref[...], theirs_ref[...])
        d, nm, nv = _adam_update(w_ref[...], gv, m_ref[...], v_ref[...])
        g_ref[...] = gv
        d_ref[...] = d
        nm_ref[...] = nm
        nv_ref[...] = nv

    full = pl.BlockSpec((None, tr, c), lambda i, cc: (0, i, 0))
    half = pl.BlockSpec((tr, c), lambda i, cc: (i % nhb, 0))
    return pl.pallas_call(
        body, name=name,
        grid_spec=pltpu.PrefetchScalarGridSpec(
            num_scalar_prefetch=1, grid=(2 * nhb,),
            in_specs=[full, half, half, full, full], out_specs=[full] * 4),
        out_shape=[jax.ShapeDtypeStruct(w.shape, F32)] * 4,
        compiler_params=_params(("parallel",), 40 * 1024 * 1024),
    )(half_index, w, mine, theirs, m, v)


def _adamw_shard_cols(w, mine, theirs, m, v, half_index, name, steps=20):
    c, _, r = w.shape
    tc = c // steps
    assert tc * steps == c

    def body(c_ref, w_ref, mine_ref, theirs_ref, m_ref, v_ref, g_ref, d_ref, nm_ref, nv_ref):
        first = jnp.where(c_ref[0] == 0, mine_ref[...], theirs_ref[...])
        second = jnp.where(c_ref[0] == 0, theirs_ref[...], mine_ref[...])
        for lo, gv in ((0, first), (r // 2, second)):
            cols = slice(lo, lo + r // 2)
            d, nm, nv = _adam_update(w_ref[:, :, cols], gv, m_ref[:, :, cols], v_ref[:, :, cols])
            g_ref[:, :, cols] = gv
            d_ref[:, :, cols] = d
            nm_ref[:, :, cols] = nm
            nv_ref[:, :, cols] = nv

    full = pl.BlockSpec((tc, 1, r), lambda i, cc: (i, 0, 0))
    half = pl.BlockSpec((tc, 1, r // 2), lambda i, cc: (i, 0, 0))
    return pl.pallas_call(
        body, name=name,
        grid_spec=pltpu.PrefetchScalarGridSpec(
            num_scalar_prefetch=1, grid=(steps,),
            in_specs=[full, half, half, full, full], out_specs=[full] * 4),
        out_shape=[jax.ShapeDtypeStruct(w.shape, F32)] * 4,
        compiler_params=_params(("parallel",), 40 * 1024 * 1024),
    )(half_index, w, mine, theirs, m, v)


def _pair_sum(own, other, half_index, name, tr=256):
    _, r, c = own.shape
    rh = r // 2
    tr = min(tr, rh)
    nrb = rh // tr

    def body(c_ref, own_ref, oth_ref, out_ref):
        out_ref[...] = (own_ref[...] + oth_ref[...].astype(F32)).astype(BF16)

    return pl.pallas_call(
        body, name=name,
        grid_spec=pltpu.PrefetchScalarGridSpec(
            num_scalar_prefetch=1, grid=(N_CHIPS, nrb),
            in_specs=[pl.BlockSpec((None, tr, c), lambda k, i, cc: (k, cc[0] * nrb + i, 0)),
                      pl.BlockSpec((None, tr, c), lambda k, i, cc: (k, i, 0))],
            out_specs=pl.BlockSpec((None, tr, c), lambda k, i, cc: (k, i, 0))),
        out_shape=jax.ShapeDtypeStruct((N_CHIPS, rh, c), BF16),
        compiler_params=_params(("parallel", "parallel")),
    )(half_index, own, other)


def _chip_sum(sums, others, chip_index, name, tr=256):
    _, r, c = sums.shape
    tr = min(tr, r)

    def body(k_ref, own_ref, oth_ref, out_ref):
        acc = own_ref[...].astype(F32)
        for j in range(N_CHIPS - 1):
            acc = acc + oth_ref[j].astype(F32)
        out_ref[...] = acc

    return pl.pallas_call(
        body, name=name,
        grid_spec=pltpu.PrefetchScalarGridSpec(
            num_scalar_prefetch=1, grid=(r // tr,),
            in_specs=[pl.BlockSpec((None, tr, c), lambda i, kk: (kk[0], i, 0)),
                      pl.BlockSpec((N_CHIPS - 1, tr, c), lambda i, kk: (0, i, 0))],
            out_specs=pl.BlockSpec((tr, c), lambda i, kk: (i, 0))),
        out_shape=jax.ShapeDtypeStruct((r, c), F32),
        compiler_params=_params(("parallel",)),
    )(chip_index, sums, others)


HBM = pl.BlockSpec(memory_space=pltpu.HBM)


def _place():
    x, y, c = lax.axis_index("x"), lax.axis_index("y"), lax.axis_index("c")
    chips = [(1 - x, y), (x, 1 - y), (1 - x, 1 - y)]
    return x, y, c, chips


def _sibling_forward(land):
    def body(in_ref, out_ref, send, recv):
        x, y, c, chips = _place()
        rh = out_ref.shape[1] // 2
        cps = []
        for j, (px, py) in enumerate(chips):
            slot = out_ref.at[2 * px + py, pl.ds(c * rh, rh)]
            cp = pltpu.make_async_remote_copy(
                src_ref=slot, dst_ref=slot, send_sem=send.at[j], recv_sem=recv.at[j],
                device_id=(x, y, 1 - c), device_id_type=MESH)
            cp.start()
            cps.append(cp)
        for j, (px, py) in enumerate(chips):
            slot = out_ref.at[2 * px + py, pl.ds((1 - c) * rh, rh)]
            pltpu.make_async_remote_copy(
                src_ref=slot, dst_ref=slot, send_sem=send.at[j], recv_sem=recv.at[j],
                device_id=(x, y, 1 - c), device_id_type=MESH).wait_recv()
        for cp in cps:
            cp.wait_send()

    return pl.pallas_call(
        body, name="first_weights_sibling_forward", in_specs=[HBM], out_specs=HBM,
        out_shape=jax.ShapeDtypeStruct(land.shape, land.dtype), input_output_aliases={0: 0},
        scratch_shapes=[pltpu.SemaphoreType.DMA((3,)), pltpu.SemaphoreType.DMA((3,))],
    )(land)


def _sibling_swap_halves(grads, name):
    na = len(grads)

    def body(*refs):
        ins, outs = refs[:na], refs[na:2 * na]
        send, recv = refs[2 * na:]
        x, y, c, _ = _place()
        sib = (x, y, 1 - c)
        cps = []
        for i in range(na):
            rh = ins[i].shape[1] // 2
            cp = pltpu.make_async_remote_copy(
                src_ref=ins[i].at[:, pl.ds((1 - c) * rh, rh), :], dst_ref=outs[i],
                send_sem=send.at[i], recv_sem=recv.at[i], device_id=sib, device_id_type=MESH)
            cp.start()
            cps.append(cp)
        for cp in cps:
            cp.wait()

    out_shape = [jax.ShapeDtypeStruct((g.shape[0], g.shape[1] // 2, g.shape[2]), g.dtype) for g in grads]
    return pl.pallas_call(
        body, name=name, in_specs=[HBM] * na, out_specs=[HBM] * na, out_shape=out_shape,
        scratch_shapes=[pltpu.SemaphoreType.DMA((na,)), pltpu.SemaphoreType.DMA((na,))],
    )(*grads)


def _sibling_swap_whole(halves):
    na = len(halves)

    def body(*refs):
        ins, outs = refs[:na], refs[na:2 * na]
        send, recv = refs[2 * na:]
        x, y, c, _ = _place()
        cps = []
        for i in range(na):
            cp = pltpu.make_async_remote_copy(
                src_ref=ins[i], dst_ref=outs[i], send_sem=send.at[i], recv_sem=recv.at[i],
                device_id=(x, y, 1 - c), device_id_type=MESH)
            cp.start()
            cps.append(cp)
        for cp in cps:
            cp.wait()

    out_shape = [jax.ShapeDtypeStruct(h.shape, h.dtype) for h in halves]
    return pl.pallas_call(
        body, name="grad_sibling_join", in_specs=[HBM] * na, out_specs=[HBM] * na, out_shape=out_shape,
        scratch_shapes=[pltpu.SemaphoreType.DMA((na,)), pltpu.SemaphoreType.DMA((na,))],
    )(*halves)


SEM = pl.BlockSpec(memory_space=pltpu.SEMAPHORE)
ANY = pl.BlockSpec(memory_space=pl.ANY)
EFFECT = pltpu.SideEffectType.DATAFLOW_SIDE_EFFECTING


def _split_copy_start(name, plan, srcs, lands, after):
    ns, nl = len(srcs), len(lands)

    def body(*refs):
        src_refs, land_refs = refs[:ns], refs[ns:ns + nl]
        send, recv = refs[ns + nl + 1], refs[ns + nl + 2]
        token = refs[-1]
        outgoing, _ = plan(src_refs, land_refs)
        for src, dst, dev, si, ri in outgoing:
            pltpu.make_async_remote_copy(src_ref=src, dst_ref=dst, send_sem=send.at[si], recv_sem=recv.at[ri],
                                         device_id=dev, device_id_type=MESH).start()
        token[...] = jnp.zeros_like(token)

    n_out, n_in = plan.counts
    thru = [pltpu.HBM(a.shape, a.dtype) for a in list(srcs) + list(lands)]
    res = pl.pallas_call(
        body, name=name,
        out_shape=[pltpu.SemaphoreType.DMA((n_out,)), pltpu.SemaphoreType.DMA((n_in,))] + thru
        + [jax.ShapeDtypeStruct((8, 128), F32)],
        in_specs=[HBM] * (ns + nl) + [ANY],
        out_specs=[SEM, SEM] + [HBM] * (ns + nl) + [pl.BlockSpec(memory_space=pltpu.VMEM)],
        input_output_aliases={i: 2 + i for i in range(ns + nl)},
        compiler_params=pltpu.CompilerParams(has_side_effects=EFFECT),
    )(*[pltpu.with_memory_space_constraint(a, pltpu.HBM) for a in list(srcs) + list(lands)], after)
    return res[0], res[1], res[2:2 + ns], res[2 + ns:2 + ns + nl], res[-1]


def _split_copy_wait(name, plan, send, recv, srcs, lands, after):
    ns, nl = len(srcs), len(lands)

    def body(*refs):
        src_refs, land_refs = refs[:ns], refs[ns:ns + nl]
        send_ref, recv_ref = refs[ns + nl], refs[ns + nl + 1]
        outgoing, arrivals = plan(src_refs, land_refs)
        for src, dst, dev, si, ri in outgoing:
            pltpu.make_async_remote_copy(src_ref=src, dst_ref=dst, send_sem=send_ref.at[si], recv_sem=recv_ref.at[ri],
                                         device_id=dev, device_id_type=MESH).wait_send()
        for view, ri in arrivals:
            pltpu.make_async_remote_copy(src_ref=view, dst_ref=view, send_sem=send_ref.at[0], recv_sem=recv_ref.at[ri],
                                         device_id=_place()[:3], device_id_type=MESH).wait_recv()

    thru = [pltpu.HBM(a.shape, a.dtype) for a in list(srcs) + list(lands)]
    res = pl.pallas_call(
        body, name=name, out_shape=thru,
        in_specs=[HBM] * (ns + nl) + [SEM, SEM, ANY], out_specs=[HBM] * (ns + nl),
        input_output_aliases={i: i for i in range(ns + nl)},
        compiler_params=pltpu.CompilerParams(has_side_effects=EFFECT),
    )(*srcs, *lands, send, recv, after)
    return res[:ns], res[ns:]


def _gather_plan(n_arrays):
    def plan(src_refs, land_refs):
        x, y, c, chips = _place()
        me = 2 * x + y
        outgoing, arrivals = [], []
        for i in range(n_arrays):
            rh = src_refs[i].shape[0] // 2
            mine = pl.ds(c * rh, rh)
            for j, (px, py) in enumerate(chips):
                for delta in range(2):
                    tc = c ^ delta
                    outgoing.append((src_refs[i].at[mine], land_refs[i].at[me, mine], (px, py, tc),
                                     6 * i + 2 * j + delta, 6 * i + 2 * j + delta))
                    theirs = pl.ds(tc * rh, rh)
                    arrivals.append((land_refs[i].at[2 * px + py, theirs], 6 * i + 2 * j + delta))
        return outgoing, arrivals

    plan.counts = (6 * n_arrays, 6 * n_arrays)
    return plan


def _first_gather_plan():
    def plan(src_refs, land_refs):
        x, y, c, chips = _place()
        me = 2 * x + y
        rh = src_refs[0].shape[0] // 2
        mine = pl.ds(c * rh, rh)
        outgoing, arrivals = [], []
        for j, (px, py) in enumerate(chips):
            outgoing.append((src_refs[0].at[mine], land_refs[0].at[me, mine], (px, py, c), j, j))
            arrivals.append((land_refs[0].at[2 * px + py, mine], j))
            outgoing.append((src_refs[1], land_refs[1].at[me], (px, py, c), 3 + j, 3 + j))
            arrivals.append((land_refs[1].at[2 * px + py], 3 + j))
        return outgoing, arrivals

    plan.counts = (6, 6)
    return plan


def _exchange_plan(n_arrays):
    def plan(src_refs, land_refs):
        x, y, c, chips = _place()
        outgoing, arrivals = [], []
        for i in range(n_arrays):
            for j, (px, py) in enumerate(chips):
                outgoing.append((src_refs[i].at[2 * px + py], land_refs[i].at[j], (px, py, c), 3 * i + j, 3 * i + j))
                arrivals.append((land_refs[i].at[j], 3 * i + j))
        return outgoing, arrivals

    plan.counts = (3 * n_arrays, 3 * n_arrays)
    return plan


def _small_allreduce(vec):
    r, cdim = vec.shape
    n_dev = 8

    def body(v_ref, out_ref, buf, send, recv):
        x, y, c, _ = _place()
        me = 4 * x + 2 * y + c
        buf[me] = v_ref[...]
        cps = []
        for k in range(1, n_dev):
            dx, dy, dc = (k >> 2) & 1, (k >> 1) & 1, k & 1
            peer = (x ^ dx, y ^ dy, c ^ dc)
            cp = pltpu.make_async_remote_copy(
                src_ref=v_ref, dst_ref=buf.at[me], send_sem=send.at[k - 1], recv_sem=recv.at[k - 1],
                device_id=peer, device_id_type=MESH)
            cp.start()
            cps.append(cp)
        for k in range(1, n_dev):
            dx, dy, dc = (k >> 2) & 1, (k >> 1) & 1, k & 1
            src = 4 * (x ^ dx) + 2 * (y ^ dy) + (c ^ dc)
            slot = buf.at[src]
            pltpu.make_async_remote_copy(
                src_ref=slot, dst_ref=slot, send_sem=send.at[k - 1], recv_sem=recv.at[k - 1],
                device_id=(x ^ dx, y ^ dy, c ^ dc), device_id_type=MESH).wait_recv()
        for cp in cps:
            cp.wait_send()
        acc = buf[0]
        for k in range(1, n_dev):
            acc = acc + buf[k]
        out_ref[...] = acc

    vm = pl.BlockSpec(memory_space=pltpu.VMEM)
    return pl.pallas_call(
        body, name="small_allreduce", in_specs=[vm], out_specs=vm,
        out_shape=jax.ShapeDtypeStruct((r, cdim), F32),
        scratch_shapes=[pltpu.VMEM((n_dev, r, cdim), F32), pltpu.SemaphoreType.DMA((n_dev - 1,)),
                        pltpu.SemaphoreType.DMA((n_dev - 1,))],
    )(vec)


def _a_cols_to_head_major(w):
    lead = w.shape[:-1]
    q = w[..., :A_QK].reshape(lead + (A_HEADS, A_DK))
    k = w[..., A_QK:2 * A_QK].reshape(lead + (A_HEADS, A_DK))
    v = w[..., 2 * A_QK:2 * A_QK + A_VW].reshape(lead + (A_HEADS, A_DV))
    z = w[..., 2 * A_QK + A_VW:].reshape(lead + (A_HEADS, A_DV))
    return jnp.concatenate([q, k, v, z], axis=-1).reshape(lead + (A_HEADS * A_HEAD_COLS,))


def _a_cols_from_head_major(w):
    lead = w.shape[:-1]
    w = w.reshape(lead + (A_HEADS, A_HEAD_COLS))
    parts = [w[..., :A_DK], w[..., A_DK:2 * A_DK], w[..., 2 * A_DK:2 * A_DK + A_DV], w[..., 2 * A_DK + A_DV:]]
    return jnp.concatenate([p.reshape(lead + (-1,)) for p in parts], axis=-1)


def _conv_cols_to_head_major(w):
    lead = w.shape[:-1]
    q = w[..., :A_QK].reshape(lead + (A_HEADS, A_DK))
    k = w[..., A_QK:2 * A_QK].reshape(lead + (A_HEADS, A_DK))
    v = w[..., 2 * A_QK:].reshape(lead + (A_HEADS, A_DV))
    return jnp.concatenate([q, k, v], axis=-1).reshape(lead + (A_HEADS * A_CONV_COLS,))


def _conv_cols_from_head_major(w):
    lead = w.shape[:-1]
    w = w.reshape(lead + (A_HEADS, A_CONV_COLS))
    parts = [w[..., :A_DK], w[..., A_DK:2 * A_DK], w[..., 2 * A_DK:]]
    return jnp.concatenate([p.reshape(lead + (-1,)) for p in parts], axis=-1)


def _to_stream(a, bn, d):
    rest = a.shape[1:]
    s = a.shape[0] // bn
    a = a.reshape((bn, s // d, d) + rest)
    a = jnp.swapaxes(a, 1, 2)
    return a.reshape((bn * d, s // d) + rest)


def _from_stream(a, bn, d):
    rest = a.shape[2:]
    ln = a.shape[1]
    a = a.reshape((bn, d, ln) + rest)
    a = jnp.swapaxes(a, 1, 2)
    return a.reshape((bn * ln * d,) + rest)


B_SUB = 512
B_SHARD_BLOCKS = (3 * B_GROUPS * B_W + B_W) // N_CHIPS // B_SUB


def _b_block(gi, jj):
    nb = (B_GROUPS * (jj // 2) + gi) * 2 + jj % 2
    return nb // B_SHARD_BLOCKS, nb % B_SHARD_BLOCKS


def _shard_major(g, ncols):
    r = g.shape[0]
    return jnp.swapaxes(g.reshape(r, N_CHIPS, ncols), 0, 1)


def _pack_rows(items):
    rows, offs = [], []
    at = 0
    for a in items:
        flat = a.reshape(-1).astype(F32)
        nr = -(-flat.shape[0] // 1024) * 8
        flat = jnp.pad(flat, (0, nr * 128 - flat.shape[0]))
        rows.append(flat.reshape(nr, 128))
        offs.append((at, nr, a.shape))
        at += nr
    return jnp.concatenate(rows, axis=0), offs


def _unpack_rows(packed, offs):
    out = []
    for at, nr, shape in offs:
        size = int(np.prod(shape)) if len(shape) else 1
        out.append(packed[at:at + nr].reshape(-1)[:size].reshape(shape))
    return out


def _local_step(x, positions, loss_target, norm_g, a_log, a_dt_bias, a_norm_g, b_q_norm_g, b_k_norm_g,
                start_token, first_weights, late_weights, b_grads_ready, a_grads_ready):
    bn, s, d = x.shape
    t = bn * s
    n_chunks = s // A_CHUNK
    x0 = x.reshape(t, d)
    h0 = _rms_fwd(x0, norm_g[0:1] + start_token, "rms0_fwd")
    inv_freq = ROPE_THETA ** (-jnp.arange(0, ROPE_DIMS, 2, dtype=F32) / ROPE_DIMS)
    freq_row = jnp.concatenate([inv_freq, inv_freq, jnp.zeros((128 - ROPE_DIMS,), F32)]).reshape(1, 128)
    posf = jnp.broadcast_to(positions.astype(F32).reshape(t, 1), (t, 128))
    tabs = _rope_tables(posf, freq_row)
    tabs_s = [tabs if dil == 1 else [_to_stream(tb, bn, dil).reshape(t, 128) for tb in tabs] for dil in B_DIL]
    wa_in, conv_w, late_token = first_weights(tabs_s[-1][0])
    wa_main = _a_cols_to_head_major(wa_in[:, :A_MAIN])
    wa_tail = jnp.pad(wa_in[:, A_MAIN:], ((0, 0), (0, 128 - 2 * A_HEADS))) + late_token.astype(BF16)
    cw_hm = _conv_cols_to_head_major(conv_w)

    proj_a = _matmul(h0, wa_main, "nn", F32, "a_in_main")
    tail_a = _matmul(h0, wa_tail, "nn", F32, "a_in_tail")
    tail_t = jnp.swapaxes(tail_a[:, :2 * A_HEADS].reshape(bn, s, 2 * A_HEADS), 1, 2)
    tail_t = tail_t.reshape(bn, 2 * A_HEADS, n_chunks, A_CHUNK)
    beta, gc = _gdn_prep(tail_t, a_log[0], a_dt_bias[0])
    proj_a3 = proj_a.reshape(bn, s, A_MAIN)
    og_a, oraw_a, states, t_mats, conv_y = _gdn_fwd(proj_a3, cw_hm, beta, gc, a_norm_g)
    wa_out, wb_in, wb_out = late_weights(og_a)
    b_cols = [4 * B_W] + [3 * B_W] * (B_GROUPS - 1)
    x1 = _matmul(og_a.reshape(t, A_VW), wa_out, "nn", F32, "a_out", res=x0, tk=2048)

    h1 = _rms_fwd(x1, norm_g[1:2], "rms1_fwd")
    h1_s, proj_b, qkv_b, o_b, lse_b = [], [], [], [], []
    for gi, dil in enumerate(B_DIL):
        hs = h1 if dil == 1 else _to_stream(h1, bn, dil).reshape(t, d)
        ts = tabs_s[gi]
        pj = _matmul(hs, wb_in, "nn", F32, f"b_in_g{gi}", tm=2048, tn=B_SUB, n=b_cols[gi], b_spec=pl.BlockSpec(
            (None, d, B_SUB), lambda i, j, kk, gi=gi: (_b_block(gi, j)[0], kk, _b_block(gi, j)[1])))
        qkv = _qk_prep(pj, *ts, b_q_norm_g[0, gi:gi + 1], b_k_norm_g[0, gi:gi + 1], f"qk_prep_g{gi}")
        o_s, lse_s = _attn_fwd(qkv.reshape(bn * dil, s // dil, 3 * B_W), f"attn_fwd_g{gi}")
        h1_s.append(hs), proj_b.append(pj), qkv_b.append(qkv)
        o_b.append(o_s.reshape(t, B_W) if dil == 1 else _from_stream(o_s, bn, dil))
        lse_b.append(lse_s.reshape(t, B_HEADS) if dil == 1 else _from_stream(lse_s, bn, dil))
    og_b = _merge_fwd(o_b, lse_b, proj_b[0])
    x2 = _matmul(og_b, wb_out, "nn", F32, "b_out", res=x1)

    d_x2, loss_parts = _loss_grad(x2, loss_target.reshape(t, d))
    loss_local = jnp.sum(loss_parts)

    d_x2b = d_x2.astype(BF16)
    g_wb_out = _matmul(og_b, d_x2b, "tn", F32, "b_out_dw")
    d_og_b = _matmul(d_x2b, wb_out, "nt", F32, "b_out_dx")
    d_o, lse_joint, delta, d_z = _merge_bwd(o_b, lse_b, proj_b[0], d_og_b)
    d_h1, g_qn, g_kn = [], [], []
    g_wb_in = lax.empty(wb_in.shape, F32)
    for gi, dil in enumerate(B_DIL):
        if dil == 1:
            do_s, lj_s, dl_s = d_o, lse_joint, delta
        else:
            do_s, lj_s, dl_s = (_to_stream(a, bn, dil).reshape(t, -1) for a in (d_o, lse_joint, delta))
        ns, ln = bn * dil, s // dil
        dq, dk, dv = _attn_bwd(qkv_b[gi].reshape(ns, ln, 3 * B_W), do_s.reshape(ns, ln, B_W),
                               lj_s.reshape(ns, ln, B_HEADS), dl_s.reshape(ns, ln, B_HEADS), f"attn_bwd_g{gi}")
        d_pj, d_gain = _qk_prep_bwd(proj_b[gi], *tabs_s[gi], b_q_norm_g[0, gi:gi + 1], b_k_norm_g[0, gi:gi + 1],
                                    dq.reshape(t, B_W), dk.reshape(t, B_W), dv.reshape(t, B_W),
                                    d_z if gi == 0 else None, f"qk_prep_bwd_g{gi}")
        g_wb_in = _matmul(h1_s[gi], d_pj, "tn", F32, f"b_in_dw_g{gi}", tn=B_SUB, tk=2048, into=(g_wb_in, pl.BlockSpec(
            (None, d, B_SUB), lambda i, j, kk, gi=gi: (_b_block(gi, j)[0], i, _b_block(gi, j)[1]))))
        dh = _matmul(d_pj, wb_in, "nt", F32, f"b_in_dx_g{gi}", tm=2048, tk=B_SUB, n=d, b_spec=pl.BlockSpec(
            (None, d, B_SUB), lambda i, j, kk, gi=gi: (_b_block(gi, kk)[0], j, _b_block(gi, kk)[1])))
        d_h1.append(dh if dil == 1 else _from_stream(dh.reshape(ns, ln, d), bn, dil))
        g_qn.append(d_gain[0]), g_kn.append(d_gain[1])
    d_x1, g_norm1 = _rms_bwd(x1, norm_g[1:2], d_h1, d_x2, "rms1_bwd")

    d_x1b = d_x1.astype(BF16)
    g_wa_out = _matmul(og_a.reshape(t, A_VW), d_x1b, "tn", F32, "a_out_dw")
    b_token = b_grads_ready(g_wb_in, g_wb_out, g_wa_out)
    d_og_a = _matmul(d_x1b, wa_out, "nt", F32, "a_out_dx")
    d_pa, d_gc, d_beta, d_cw, d_ng = _gdn_bwd(proj_a3, cw_hm, beta, gc, a_norm_g + b_token, oraw_a, states,
                                              t_mats, conv_y, d_og_a.reshape(bn, s, A_VW))
    d_tail_t, d_alog, d_dtb = _gdn_prep_bwd(tail_t, a_log[0], a_dt_bias[0], d_gc, d_beta)
    d_tail = jnp.swapaxes(d_tail_t.reshape(bn, 2 * A_HEADS, s), 1, 2).reshape(t, 2 * A_HEADS)
    d_tail = jnp.pad(d_tail, ((0, 0), (0, 128 - 2 * A_HEADS))).astype(BF16)
    d_pa = d_pa.reshape(t, A_MAIN)
    g_wa_main = _matmul(h0, d_pa, "tn", F32, "a_in_dw_main")
    g_wa_tail = _matmul(h0, d_tail, "tn", F32, "a_in_dw_tail")
    g_wa_in = jnp.concatenate([_a_cols_from_head_major(g_wa_main), g_wa_tail[:, :2 * A_HEADS]], axis=1)
    a_token = a_grads_ready(g_wa_in)
    d_h0 = _matmul(d_pa, wa_main, "nt", F32, "a_in_dx_main")
    d_h0t = _matmul(d_tail + a_token.astype(BF16), wa_tail, "nt", F32, "a_in_dx_tail")
    d_x0, g_norm0 = _rms_bwd(x0, norm_g[0:1], [d_h0, d_h0t], d_x1, "rms0_bwd")

    gfull = {
        "norm_g": jnp.concatenate([g_norm0, g_norm1], axis=0), "a_w_in": g_wa_in,
        "a_conv_w": _conv_cols_from_head_major(jnp.sum(d_cw, axis=0)),
        "a_log": jnp.sum(d_alog[:, :, 0], axis=0), "a_dt_bias": jnp.sum(d_dtb[:, :, 0], axis=0),
        "a_norm_g": jnp.sum(d_ng[:, :, 0, :], axis=(0, 1)), "a_w_out": g_wa_out, "b_w_in": g_wb_in,
        "b_q_norm_g": jnp.stack(g_qn), "b_k_norm_g": jnp.stack(g_kn), "b_w_out": g_wb_out}
    return loss_local, d_x0.reshape(bn, s, d), gfull


def kernel(x, positions, norm_g, a_w_in, a_conv_w, a_log, a_dt_bias, a_norm_g, a_w_out, b_w_in, b_q_norm_g, b_k_norm_g, b_w_out, loss_target, m_norm_g, m_a_w_in, m_a_conv_w, m_a_log, m_a_dt_bias, m_a_norm_g, m_a_w_out, m_b_w_in, m_b_q_norm_g, m_b_k_norm_g, m_b_w_out, v_norm_g, v_a_w_in, v_a_conv_w, v_a_log, v_a_dt_bias, v_a_norm_g, v_a_w_out, v_b_w_in, v_b_q_norm_g, v_b_k_norm_g, v_b_w_out):
    d = x.shape[2]
    my_c = lax.axis_index("c")
    my_chip = 2 * lax.axis_index("x") + lax.axis_index("y")

    half_index = jnp.reshape(my_c, (1,)).astype(jnp.int32)
    chip_index = jnp.reshape(my_chip, (1,)).astype(jnp.int32)
    def landing(shard):
        return lax.dynamic_update_slice(lax.empty((N_CHIPS,) + shard.shape, shard.dtype), shard[None],
                                        (my_chip,) + (0,) * shard.ndim)

    first_shards = [a_w_in[0].astype(BF16), a_conv_w[0]]
    first_plan = _first_gather_plan()
    first = _split_copy_start("first_weights_start", first_plan, first_shards,
                              [landing(s) for s in first_shards], half_index)
    pending = {}

    def first_weights(after):
        _, (ga_in, g_conv) = _split_copy_wait("first_weights_wait", first_plan, *first[:4], after)
        ga_in = _sibling_forward(ga_in)
        wa_in = jnp.concatenate([ga_in[k] for k in range(N_CHIPS)], axis=1)
        conv_w = jnp.concatenate([g_conv[k] for k in range(N_CHIPS)], axis=1)
        late_shards = [a_w_out[0].astype(BF16), b_w_in[0].astype(BF16), b_w_out[0].astype(BF16)]
        plan = _gather_plan(len(late_shards))
        pending["late"] = (plan,) + tuple(_split_copy_start(
            "late_weights_start", plan, late_shards, [landing(s) for s in late_shards], conv_w))
        return wa_in, conv_w, pending["late"][5][0, 0]

    def late_weights(after):
        plan, send, recv, srcs, lands, _ = pending["late"]
        _, (ga_out, gb_in, gb_out) = _split_copy_wait("late_weights_wait", plan, send, recv, srcs, lands, after)
        return ga_out.reshape(A_VW, d), gb_in, gb_out.reshape(B_W, d)

    def reduce_to_chip_sums(mats, tag):
        recv_sib = _sibling_swap_halves([g.astype(BF16) for g in mats], f"grad_{tag}_sibling_swap")
        return [_pair_sum(g, r, half_index, f"grad_{tag}_pair_sum_{i}") for i, (g, r) in enumerate(zip(mats, recv_sib))]

    def start_exchange(tag, mats):
        sums = reduce_to_chip_sums(mats, tag)
        lands = [lax.empty((N_CHIPS - 1,) + s.shape[1:], BF16) for s in sums]
        plan = _exchange_plan(len(mats))
        pending[tag] = (plan,) + tuple(_split_copy_start(f"grad_{tag}_exchange_start", plan, sums, lands, chip_index))
        return pending[tag][5][0, 0]

    def finish_exchange(tag, after):
        plan, send, recv, srcs, lands, _ = pending[tag]
        return _split_copy_wait(f"grad_{tag}_exchange_wait", plan, send, recv, srcs, lands, after)

    def b_grads_ready(g_wb_in, g_wb_out, g_wa_out):
        return start_exchange("b", [g_wb_in, g_wb_out.reshape(N_CHIPS, -1, d), g_wa_out.reshape(N_CHIPS, -1, d)])

    def a_grads_ready(g_wa_in):
        return start_exchange("a", [_shard_major(g_wa_in, a_w_in.shape[2])])

    loss_local, d_x0, gfull = _local_step(x, positions, loss_target, norm_g, a_log, a_dt_bias, a_norm_g,
                                          b_q_norm_g, b_k_norm_g, first[4][0, 0], first_weights, late_weights,
                                          b_grads_ready, a_grads_ready)

    small = [gfull["norm_g"], gfull["a_conv_w"], gfull["a_log"], gfull["a_dt_bias"], gfull["a_norm_g"],
             gfull["b_q_norm_g"], gfull["b_k_norm_g"], loss_local]
    packed, offs = _pack_rows(small)
    reduced = _small_allreduce(packed)
    g_norm, g_conv_all, g_alog, g_dtb, g_ang, g_q, g_k, loss = _unpack_rows(reduced, offs)
    g_conv_mine = lax.dynamic_slice_in_dim(g_conv_all, my_chip * a_conv_w.shape[2], a_conv_w.shape[2], axis=1)

    b_sums, b_received = finish_exchange("b", d_x0)
    a_sums, a_received = finish_exchange("a", reduced)
    chip_sums = [a_sums[0], b_sums[2], b_sums[0], b_sums[1]]
    received = [a_received[0], b_received[2], b_received[0], b_received[1]]
    halves = [_chip_sum(s, r, chip_index, f"grad_chip_sum_{i}") for i, (s, r) in enumerate(zip(chip_sums, received))]
    theirs = _sibling_swap_whole(halves)
    big = ("a_w_in", "a_w_out", "b_w_in", "b_w_out")
    big_halves = dict(zip(big, zip(halves, theirs)))

    grads = {
        "norm_g": g_norm, "a_conv_w": g_conv_mine[None], "a_log": g_alog[None], "a_dt_bias": g_dtb[None],
        "a_norm_g": g_ang[None], "b_q_norm_g": g_q[None], "b_k_norm_g": g_k[None]}
    weights = {"norm_g": norm_g, "a_w_in": a_w_in, "a_conv_w": a_conv_w, "a_log": a_log, "a_dt_bias": a_dt_bias,
               "a_norm_g": a_norm_g, "a_w_out": a_w_out, "b_w_in": b_w_in, "b_q_norm_g": b_q_norm_g,
               "b_k_norm_g": b_k_norm_g, "b_w_out": b_w_out}
    m_in = {"norm_g": m_norm_g, "a_w_in": m_a_w_in, "a_conv_w": m_a_conv_w, "a_log": m_a_log,
            "a_dt_bias": m_a_dt_bias, "a_norm_g": m_a_norm_g, "a_w_out": m_a_w_out, "b_w_in": m_b_w_in,
            "b_q_norm_g": m_b_q_norm_g, "b_k_norm_g": m_b_k_norm_g, "b_w_out": m_b_w_out}
    v_in = {"norm_g": v_norm_g, "a_w_in": v_a_w_in, "a_conv_w": v_a_conv_w, "a_log": v_a_log,
            "a_dt_bias": v_a_dt_bias, "a_norm_g": v_a_norm_g, "a_w_out": v_a_w_out, "b_w_in": v_b_w_in,
            "b_q_norm_g": v_b_q_norm_g, "b_k_norm_g": v_b_k_norm_g, "b_w_out": v_b_w_out}
    names = list(weights)

    delta_w, new_m, new_v = {}, {}, {}
    for nm in big:
        mine, other = big_halves[nm]
        if weights[nm].shape[2] % 128:
            cols = lambda a: jnp.transpose(a, (2, 0, 1))
            half_cols = lambda a: jnp.transpose(a)[:, None, :]
            outs = _adamw_shard_cols(cols(weights[nm]), half_cols(mine), half_cols(other), cols(m_in[nm]),
                                     cols(v_in[nm]), half_index, f"adamw_{nm}")
            outs = [jnp.transpose(o, (1, 2, 0)) for o in outs]
        else:
            outs = _adamw_shard(weights[nm], mine, other, m_in[nm], v_in[nm], half_index, f"adamw_{nm}")
        grads[nm], delta_w[nm], new_m[nm], new_v[nm] = outs
    small_names = [nm for nm in names if nm not in big]
    packs = [_pack_rows([src[nm] for nm in small_names]) for src in (weights, grads, m_in, v_in)]
    offs = packs[0][1]
    dl, m2, v2 = _adamw(packs[0][0], packs[1][0], packs[2][0], packs[3][0], "adamw_small")
    for nm, a, b, c2 in zip(small_names, _unpack_rows(dl, offs), _unpack_rows(m2, offs), _unpack_rows(v2, offs)):
        delta_w[nm], new_m[nm], new_v[nm] = a, b, c2

    return (loss, d_x0, *[grads[nm] for nm in names], *[delta_w[nm] for nm in names],
            *[new_m[nm] for nm in names], *[new_v[nm] for nm in names])
```

```python
import jax
import jax.numpy as jnp
import numpy as np
from jax import lax
from jax.experimental import pallas as pl
from jax.experimental.pallas import tpu as pltpu

F32 = jnp.float32
BF16 = jnp.bfloat16
MESH = pl.DeviceIdType.MESH

EPS = 1e-6
D_MODEL = 1024
A_HEADS = 8
A_DK = 128
A_DV = 256
A_QK = A_HEADS * A_DK
A_VW = A_HEADS * A_DV
A_MAIN = 2 * A_QK + 2 * A_VW
A_HEAD_COLS = 2 * A_DK + 2 * A_DV
A_CONV_COLS = 2 * A_DK + A_DV
A_CHUNK = 64
A_CONV = 4
B_GROUPS = 3
B_HEADS = 8
B_DH = 128
B_W = B_HEADS * B_DH
B_DIL = (1, 4, 16)
B_BLK = 128
ROPE_THETA = 500000.0
ROPE_DIMS = B_DH // 4
ADAM_LR, ADAM_B1, ADAM_B2, ADAM_EPS, ADAM_WD, ADAM_STEP = 0.001, 0.9, 0.999, 1e-08, 0.01, 10
N_CHIPS = 4
VMEM_BIG = 56 * 1024 * 1024


def _params(sem=None, vmem=None):
    return pltpu.CompilerParams(dimension_semantics=sem, vmem_limit_bytes=vmem)


def _dot(a, b, ca, cb):
    return lax.dot_general(a.astype(BF16), b.astype(BF16), (((ca,), (cb,)), ((), ())),
                           preferred_element_type=F32)


def _split3(a):
    hi = a.astype(BF16)
    r = a - hi.astype(F32)
    mid = r.astype(BF16)
    lo = (r - mid.astype(F32)).astype(BF16)
    return hi, mid, lo


def _sigmoid(y):
    return 1.0 / (1.0 + jnp.exp(-y))


def _silu(y):
    return y * _sigmoid(y)


def _dsilu(y):
    s = _sigmoid(y)
    return s * (1.0 + y * (1.0 - s))


def _matmul(a, b, mode, out_dtype, name, res=None, tm=1024, tn=1024, tk=1024, n=None, b_spec=None, into=None):
    m, k = a.shape[::-1] if mode == "tn" else a.shape
    if n is None:
        n = b.shape[0] if mode == "nt" else b.shape[1]
    tm, tn, tk = min(tm, m), min(tn, n), min(tk, k)
    assert m % tm == 0 and n % tn == 0 and k % tk == 0, (name, a.shape, b.shape)
    nk = k // tk
    dims = {"nn": ((1,), (0,)), "nt": ((1,), (1,)), "tn": ((0,), (0,))}[mode]

    def body(*refs):
        a_ref, b_ref = refs[0], refs[1]
        r_ref = refs[2] if res is not None else None
        o_ref = refs[2 + (res is not None) + (into is not None)]
        prod = lax.dot_general(a_ref[...], b_ref[...], (dims, ((), ())), preferred_element_type=F32)

        def finish(r):
            if res is not None:
                r = r + r_ref[...]
            o_ref[...] = r.astype(out_dtype)

        if nk == 1:
            finish(prod)
            return
        acc = refs[-1]
        kk = pl.program_id(2)

        @pl.when(kk == 0)
        def _():
            acc[...] = prod

        @pl.when((kk > 0) & (kk < nk - 1))
        def _():
            acc[...] += prod

        @pl.when(kk == nk - 1)
        def _():
            finish(acc[...] + prod)

    a_spec = pl.BlockSpec((tm, tk), lambda i, j, kk: (i, kk))
    if mode == "tn":
        a_spec = pl.BlockSpec((tk, tm), lambda i, j, kk: (kk, i))
    if b_spec is None and mode == "nt":
        b_spec = pl.BlockSpec((tn, tk), lambda i, j, kk: (j, kk))
    elif b_spec is None:
        b_spec = pl.BlockSpec((tk, tn), lambda i, j, kk: (kk, j))
    in_specs = [a_spec, b_spec]
    args = [a, b]
    if res is not None:
        in_specs.append(pl.BlockSpec((tm, tn), lambda i, j, kk: (i, j)))
        args.append(res)
    out_spec = pl.BlockSpec((tm, tn), lambda i, j, kk: (i, j))
    out_shape = jax.ShapeDtypeStruct((m, n), out_dtype)
    aliases = {}
    if into is not None:
        assert res is None
        buf, out_spec = into
        out_shape = jax.ShapeDtypeStruct(buf.shape, buf.dtype)
        in_specs.append(ANY)
        args.append(buf)
        aliases = {2: 0}
    return pl.pallas_call(
        body, name=name, grid=(m // tm, n // tn, nk),
        in_specs=in_specs, out_specs=out_spec, out_shape=out_shape, input_output_aliases=aliases,
        scratch_shapes=[pltpu.VMEM((tm, tn), F32)] if nk > 1 else [],
        compiler_params=_params(("parallel", "parallel", "arbitrary"), 48 * 1024 * 1024),
    )(*args)


def _rms_fwd(x, g, name, tm=256):
    t, d = x.shape

    def body(x_ref, g_ref, h_ref):
        xv = x_ref[...]
        r = lax.rsqrt(jnp.mean(xv * xv, axis=-1, keepdims=True) + EPS)
        h_ref[...] = (xv * r * g_ref[...]).astype(BF16)

    return pl.pallas_call(
        body, name=name, grid=(t // tm,),
        in_specs=[pl.BlockSpec((tm, d), lambda i: (i, 0)), pl.BlockSpec((1, d), lambda i: (0, 0))],
        out_specs=pl.BlockSpec((tm, d), lambda i: (i, 0)),
        out_shape=jax.ShapeDtypeStruct((t, d), BF16),
        compiler_params=_params(("parallel",)),
    )(x, g)


def _rms_bwd(x, g, dhs, dres, name, tm=256):
    t, d = x.shape
    n_dh = len(dhs)

    def body(*refs):
        x_ref, g_ref = refs[0], refs[1]
        dh_refs = refs[2:2 + n_dh]
        dres_ref, dx_ref, dg_ref = refs[2 + n_dh:]
        i = pl.program_id(0)

        @pl.when(i == 0)
        def _():
            dg_ref[...] = jnp.zeros_like(dg_ref)

        xv = x_ref[...]
        r = lax.rsqrt(jnp.mean(xv * xv, axis=-1, keepdims=True) + EPS)
        xh = xv * r
        dh = dh_refs[0][...]
        for ref in dh_refs[1:]:
            dh = dh + ref[...]
        dg_ref[0:1, :] += jnp.sum(dh * xh, axis=0, keepdims=True)
        dxh = dh * g_ref[...]
        dx = r * (dxh - xh * jnp.mean(dxh * xh, axis=-1, keepdims=True))
        dx_ref[...] = dx + dres_ref[...]

    row = pl.BlockSpec((tm, d), lambda i: (i, 0))
    dx, dg = pl.pallas_call(
        body, name=name, grid=(t // tm,),
        in_specs=[row, pl.BlockSpec((1, d), lambda i: (0, 0))] + [row] * n_dh + [row],
        out_specs=[row, pl.BlockSpec((8, d), lambda i: (0, 0))],
        out_shape=[jax.ShapeDtypeStruct((t, d), F32), jax.ShapeDtypeStruct((8, d), F32)],
        compiler_params=_params(("arbitrary",)),
    )(x, g, *dhs, dres)
    return dx, dg[0:1]


def _loss_grad(y, target, name="loss_grad", tm=256):
    t, d = y.shape
    nb = t // tm

    def body(y_ref, t_ref, dy_ref, part_ref):
        e = y_ref[...] - t_ref[...]
        dy_ref[...] = e * (1.0 / d)
        s = jnp.sum(jnp.sum(e * e, axis=1, keepdims=True), axis=0, keepdims=True) * (0.5 / d)
        part_ref[...] = jnp.broadcast_to(s, (8, 128))

    row = pl.BlockSpec((tm, d), lambda i: (i, 0))
    dy, part = pl.pallas_call(
        body, name=name, grid=(nb,), in_specs=[row, row],
        out_specs=[row, pl.BlockSpec((None, 8, 128), lambda i: (i, 0, 0))],
        out_shape=[jax.ShapeDtypeStruct((t, d), F32), jax.ShapeDtypeStruct((nb, 8, 128), F32)],
        compiler_params=_params(("parallel",)),
    )(y, target)
    return dy, part[:, 0, 0]


def _softplus(x):
    t = jnp.exp(-jnp.abs(x))
    return jnp.maximum(x, 0.0) + jnp.where(t < 1e-3, t * (1.0 - 0.5 * t), jnp.log(1.0 + t))


def _tri(rows_le_cols):
    r = lax.broadcasted_iota(jnp.int32, (A_CHUNK, A_CHUNK), 0)
    c = lax.broadcasted_iota(jnp.int32, (A_CHUNK, A_CHUNK), 1)
    return jnp.where((r <= c) if rows_le_cols else (r >= c), 1.0, 0.0).astype(BF16)


def _dot_exact_rhs(a, ones_bf16):
    dn = (((1,), (0,)), ((), ()))
    hi, mid, lo = _split3(a)
    out = lax.dot_general(hi, ones_bf16, dn, preferred_element_type=F32)
    out = out + lax.dot_general(mid, ones_bf16, dn, preferred_element_type=F32)
    return out + lax.dot_general(lo, ones_bf16, dn, preferred_element_type=F32)


def _gdn_prep(tail_t, a_log, dt_bias):
    bn, _, n, c = tail_t.shape

    def body(t_ref, alog_ref, dtb_ref, beta_ref, gc_ref):
        upper = _tri(True)
        for h in range(A_HEADS):
            beta_ref[h] = _sigmoid(t_ref[h])
            ea = jnp.exp(jnp.full((n, c), alog_ref[h], F32))
            g = -ea * _softplus(t_ref[A_HEADS + h] + dtb_ref[h])
            gc_ref[h] = _dot_exact_rhs(g, upper)

    smem = pl.BlockSpec(memory_space=pltpu.SMEM)
    blk = pl.BlockSpec((None, A_HEADS, n, c), lambda b: (b, 0, 0, 0))
    return pl.pallas_call(
        body, name="gdn_prep", grid=(bn,),
        in_specs=[pl.BlockSpec((None, 2 * A_HEADS, n, c), lambda b: (b, 0, 0, 0)), smem, smem],
        out_specs=[blk, blk],
        out_shape=[jax.ShapeDtypeStruct((bn, A_HEADS, n, c), F32)] * 2,
        compiler_params=_params(("parallel",)),
    )(tail_t, a_log, dt_bias)


def _gdn_prep_bwd(tail_t, a_log, dt_bias, d_gc, d_beta):
    bn, _, n, c = tail_t.shape

    def body(t_ref, alog_ref, dtb_ref, dgc_ref, dbeta_ref, dt_ref, dal_ref, ddt_ref):
        lower = _tri(False)
        for h in range(A_HEADS):
            beta = _sigmoid(t_ref[h])
            dt_ref[h] = dbeta_ref[h] * beta * (1.0 - beta)
            dg = _dot_exact_rhs(dgc_ref[h], lower)
            ea = jnp.exp(jnp.full((n, c), alog_ref[h], F32))
            xa = t_ref[A_HEADS + h] + dtb_ref[h]
            g = -ea * _softplus(xa)
            dxa = -ea * dg * _sigmoid(xa)
            dt_ref[A_HEADS + h] = dxa
            s1 = jnp.sum(jnp.sum(g * dg, axis=1, keepdims=True), axis=0, keepdims=True)
            s2 = jnp.sum(jnp.sum(dxa, axis=1, keepdims=True), axis=0, keepdims=True)
            dal_ref[h:h + 1, :] = jnp.broadcast_to(s1, (1, 128))
            ddt_ref[h:h + 1, :] = jnp.broadcast_to(s2, (1, 128))

    smem = pl.BlockSpec(memory_space=pltpu.SMEM)
    blk8 = pl.BlockSpec((None, A_HEADS, n, c), lambda b: (b, 0, 0, 0))
    blk16 = pl.BlockSpec((None, 2 * A_HEADS, n, c), lambda b: (b, 0, 0, 0))
    sm = pl.BlockSpec((None, A_HEADS, 128), lambda b: (b, 0, 0))
    return pl.pallas_call(
        body, name="gdn_prep_bwd", grid=(bn,),
        in_specs=[blk16, smem, smem, blk8, blk8],
        out_specs=[blk16, sm, sm],
        out_shape=[jax.ShapeDtypeStruct((bn, 2 * A_HEADS, n, c), F32),
                   jax.ShapeDtypeStruct((bn, A_HEADS, 128), F32),
                   jax.ShapeDtypeStruct((bn, A_HEADS, 128), F32)],
        compiler_params=_params(("parallel",)),
    )(tail_t, a_log, dt_bias, d_gc, d_beta)


HALO = 8


def _conv_taps(xw, w):
    y = w[A_CONV - 1:A_CONV, :] * xw
    for j in range(1, A_CONV):
        y = y + w[A_CONV - 1 - j:A_CONV - j, :] * pltpu.roll(xw, j, 0)
    return y[HALO:, :]


def _row_to_col(row, eye):
    c = eye.shape[0]
    return jnp.sum(jnp.where(eye, jnp.broadcast_to(row, (c, c)), 0.0), axis=1, keepdims=True)


def _col_to_row(col, eye):
    c = eye.shape[0]
    return jnp.sum(jnp.where(eye, jnp.broadcast_to(col, (c, c)), 0.0), axis=0, keepdims=True)


def _unit_lower_inverse(a, ri, ci):
    eye = jnp.where(ri == ci, 1.0, 0.0)
    a8 = jnp.where((ri >> 3) == (ci >> 3), a, 0.0)
    a2 = _dot(a8, a8, 1, 0)
    yield
    a4 = _dot(a2, a2, 1, 0)
    t = eye - a8
    t = t + _dot(t, a2, 1, 0)
    yield
    t = t + _dot(t, a4, 1, 0)
    yield
    for sh in (3, 4, 5):
        off = jnp.where(((ri >> (sh + 1)) == (ci >> (sh + 1))) & ((ri >> sh) != (ci >> sh)), a, 0.0)
        left = _dot(t, off, 1, 0)
        yield
        t = t - _dot(left, t, 1, 0)
        yield
    return t


def _round_robin(gens):
    live = list(gens)
    while live:
        nxt = []
        for g in live:
            try:
                next(g)
                nxt.append(g)
            except StopIteration:
                pass
        live = nxt


def _gdn_chunk_core(q, k, v, g_row, b_row, t_mat, ri, ci):
    eye = ri == ci
    g_col = _row_to_col(g_row, eye)
    b_col = _row_to_col(b_row, eye)
    causal = ri >= ci
    strict = ri > ci
    dec = jnp.where(causal, jnp.exp(jnp.where(causal, g_col - g_row, 0.0)), 0.0)
    gam = jnp.exp(g_col)
    g_last = g_row[:, A_CHUNK - 1:A_CHUNK]
    gam_last = jnp.exp(g_last)
    e = jnp.exp(g_last - g_col)
    kb = k * b_col
    bv = v * b_col
    kbg = kb * gam
    q16, k16, kb16 = q.astype(BF16), k.astype(BF16), kb.astype(BF16)
    kk = _dot(kb16, k16, 1, 1)
    p = _dot(q16, k16, 1, 1) * dec
    yield
    a_mat = jnp.where(strict, kk * dec, 0.0)
    if t_mat is None:
        t_mat = yield from _unit_lower_inverse(a_mat, ri, ci)
    t16 = t_mat.astype(BF16)
    u = _dot(t16, bv, 1, 0)
    w = _dot(t16, kbg, 1, 0)
    yield
    return dict(eye=eye, g_col=g_col, b_col=b_col, dec=dec, strict=strict, causal=causal, gam=gam,
                gam_last=gam_last, e=e, kb=kb, bv=bv, kbg=kbg, a_mat=a_mat, t_mat=t_mat, u=u, w=w, p=p,
                qg=q * gam, kd=k * e, q16=q16, k16=k16, kb16=kb16, t16=t16)


A_SEQ_BLK = 256
A_BLK_CHUNKS = A_SEQ_BLK // A_CHUNK


def _gdn_halo(proj_hm):
    bn, s, w = proj_hm.shape
    last = proj_hm.reshape(bn, s // A_SEQ_BLK, A_SEQ_BLK, w)[:, :, A_SEQ_BLK - HALO:, :]
    return jnp.concatenate([jnp.zeros((bn, 1, HALO, w), proj_hm.dtype), last[:, :-1]], axis=1)


def _gdn_window(x_ref, halo_ref, ci, first, lo):
    if first:
        return jnp.concatenate([halo_ref[:, lo:lo + A_CONV_COLS], x_ref[0:A_CHUNK, lo:lo + A_CONV_COLS]], axis=0)
    start = pl.multiple_of(ci * A_CHUNK - HALO, HALO)
    return x_ref[pl.ds(start, A_CHUNK + HALO), lo:lo + A_CONV_COLS]


def _gdn_chunk_prep(xw, cw, y=None):
    if y is None:
        y = _conv_taps(xw, cw)
    a = _silu(y)
    aq, ak, v = a[:, 0:A_DK], a[:, A_DK:2 * A_DK], a[:, 2 * A_DK:]
    rq = lax.rsqrt(jnp.sum(aq * aq, axis=1, keepdims=True) + EPS)
    rk = lax.rsqrt(jnp.sum(ak * ak, axis=1, keepdims=True) + EPS)
    return dict(xw=xw, y=y, aq=aq, ak=ak, rq=rq, rk=rk, q=aq * rq * (A_DK ** -0.5), k=ak * rk, v=v)


def _gdn_fwd(proj_hm, cw_hm, beta, gc, norm_g, hp=8):
    bn, s, _ = proj_hm.shape
    n = s // A_CHUNK
    nsb = s // A_SEQ_BLK
    halo = _gdn_halo(proj_hm)

    def body(x_ref, halo_ref, cw_ref, beta_ref, gc_ref, ng_ref, og_ref, oraw_ref, st_ref, t_ref, y_ref, state):
        first_chunk = pl.program_id(2) * A_BLK_CHUNKS
        ri = lax.broadcasted_iota(jnp.int32, (A_CHUNK, A_CHUNK), 0)
        ci_ = lax.broadcasted_iota(jnp.int32, (A_CHUNK, A_CHUNK), 1)
        ng = ng_ref[...]

        @pl.when(pl.program_id(2) == 0)
        def _():
            state[...] = jnp.zeros_like(state)

        def one_head(hh, ci, first, rows):
            lo = hh * A_HEAD_COLS
            cw = cw_ref[:, hh * A_CONV_COLS:(hh + 1) * A_CONV_COLS]
            cin = _gdn_chunk_prep(_gdn_window(x_ref, halo_ref, ci, first, lo), cw)
            y_ref[rows, hh * A_CONV_COLS:(hh + 1) * A_CONV_COLS] = cin["y"]
            seq_chunk = pl.ds(first_chunk + ci, 1)
            core = yield from _gdn_chunk_core(cin["q"], cin["k"], cin["v"], gc_ref[hh, seq_chunk, :],
                                              beta_ref[hh, seq_chunk, :], None, ri, ci_)
            st = state[hh]
            st_ref[hh, ci] = st
            t_ref[hh, ci] = core["t_mat"]
            st16 = st.astype(BF16)
            vn = core["u"] - _dot(core["w"], st16, 1, 0)
            qs = _dot(core["qg"], st16, 1, 0)
            yield
            vn16 = vn.astype(BF16)
            o = qs + _dot(core["p"], vn16, 1, 0)
            state[hh] = st * core["gam_last"] + _dot(core["kd"], vn16, 0, 0)
            yield
            ocols = slice(hh * A_DV, (hh + 1) * A_DV)
            oraw_ref[rows, ocols] = o
            r = lax.rsqrt(jnp.mean(o * o, axis=1, keepdims=True) + EPS)
            z = x_ref[rows, lo + A_CONV_COLS:lo + A_HEAD_COLS]
            og_ref[rows, ocols] = (o * r * ng * _silu(z)).astype(BF16)

        def chunk(ci, first):
            rows = pl.ds(0 if first else pl.multiple_of(ci * A_CHUNK, A_CHUNK), A_CHUNK)
            _round_robin([one_head(hh, ci, first, rows) for hh in range(hp)])

        chunk(0, True)
        lax.fori_loop(1, A_BLK_CHUNKS, lambda i, c: (chunk(i, False), c)[1], 0)

    small = pl.BlockSpec((None, hp, n, A_CHUNK), lambda b, h, j: (b, h, 0, 0))
    return pl.pallas_call(
        body, name="gdn_fwd", grid=(bn, A_HEADS // hp, nsb),
        in_specs=[pl.BlockSpec((None, A_SEQ_BLK, hp * A_HEAD_COLS), lambda b, h, j: (b, j, h)),
                  pl.BlockSpec((None, None, HALO, hp * A_HEAD_COLS), lambda b, h, j: (b, j, 0, h)),
                  pl.BlockSpec((A_CONV, hp * A_CONV_COLS), lambda b, h, j: (0, h)),
                  small, small,
                  pl.BlockSpec((1, A_DV), lambda b, h, j: (0, 0))],
        out_specs=[pl.BlockSpec((None, A_SEQ_BLK, hp * A_DV), lambda b, h, j: (b, j, h)),
                   pl.BlockSpec((None, A_SEQ_BLK, hp * A_DV), lambda b, h, j: (b, j, h)),
                   pl.BlockSpec((None, hp, A_BLK_CHUNKS, A_DK, A_DV), lambda b, h, j: (b, h, j, 0, 0)),
                   pl.BlockSpec((None, hp, A_BLK_CHUNKS, A_CHUNK, A_CHUNK), lambda b, h, j: (b, h, j, 0, 0)),
                   pl.BlockSpec((None, A_SEQ_BLK, hp * A_CONV_COLS), lambda b, h, j: (b, j, h))],
        out_shape=[jax.ShapeDtypeStruct((bn, s, A_VW), BF16),
                   jax.ShapeDtypeStruct((bn, s, A_VW), F32),
                   jax.ShapeDtypeStruct((bn, A_HEADS, n, A_DK, A_DV), F32),
                   jax.ShapeDtypeStruct((bn, A_HEADS, n, A_CHUNK, A_CHUNK), F32),
                   jax.ShapeDtypeStruct((bn, s, A_HEADS * A_CONV_COLS), F32)],
        scratch_shapes=[pltpu.VMEM((hp, A_DK, A_DV), F32)],
        compiler_params=_params(("parallel", "parallel", "arbitrary"), VMEM_BIG),
    )(proj_hm, halo, cw_hm, beta, gc, norm_g)


def _gdn_bwd(proj_hm, cw_hm, beta, gc, norm_g, oraw, states, t_mats, conv_y, dog, hp=4):
    bn, s, _ = proj_hm.shape
    n = s // A_CHUNK
    nsb = s // A_SEQ_BLK
    halo = _gdn_halo(proj_hm)

    def body(x_ref, halo_ref, cw_ref, beta_ref, gc_ref, ng_ref, oraw_ref, st_ref, t_ref, y_ref, dog_ref,
             dx_ref, dgc_ref, dbeta_ref, dcw_ref, dng_ref, dstate, dy_next, shifted):
        first_chunk = (nsb - 1 - pl.program_id(2)) * A_BLK_CHUNKS
        ri = lax.broadcasted_iota(jnp.int32, (A_CHUNK, A_CHUNK), 0)
        ci_ = lax.broadcasted_iota(jnp.int32, (A_CHUNK, A_CHUNK), 1)
        lane = lax.broadcasted_iota(jnp.int32, (1, A_CHUNK), 1)
        ng = ng_ref[...]

        @pl.when(pl.program_id(2) == 0)
        def _():
            dstate[...] = jnp.zeros_like(dstate)
            dy_next[...] = jnp.zeros_like(dy_next)
            dcw_ref[...] = jnp.zeros_like(dcw_ref)
            dng_ref[...] = jnp.zeros_like(dng_ref)

        def one_head(hh, ci, first, rows):
            lo = hh * A_HEAD_COLS
            ccols = slice(hh * A_CONV_COLS, (hh + 1) * A_CONV_COLS)
            ocols = slice(hh * A_DV, (hh + 1) * A_DV)
            cw = cw_ref[:, ccols]
            cin = _gdn_chunk_prep(_gdn_window(x_ref, halo_ref, ci, first, lo), cw, y_ref[rows, ccols])
            q, k, v = cin["q"], cin["k"], cin["v"]
            seq_chunk = pl.ds(first_chunk + ci, 1)
            cr = yield from _gdn_chunk_core(q, k, v, gc_ref[hh, seq_chunk, :], beta_ref[hh, seq_chunk, :],
                                            t_ref[hh, ci], ri, ci_)
            eye, dec, gam, e = cr["eye"], cr["dec"], cr["gam"], cr["e"]
            b_col, t_mat, u, w, p = cr["b_col"], cr["t_mat"], cr["u"], cr["w"], cr["p"]
            st = st_ref[hh, ci]
            ds_out = dstate[hh]

            o = oraw_ref[rows, ocols]
            z = x_ref[rows, lo + A_CONV_COLS:lo + A_HEAD_COLS]
            d_og = dog_ref[rows, ocols]
            r = lax.rsqrt(jnp.mean(o * o, axis=1, keepdims=True) + EPS)
            oh = o * r
            d_on = d_og * _silu(z)
            dz = d_og * oh * ng * _dsilu(z)
            dng_ref[hh, 0:1, :] += jnp.sum(d_on * oh, axis=0, keepdims=True)
            d_oh = d_on * ng
            d_o = r * (d_oh - oh * jnp.mean(d_oh * oh, axis=1, keepdims=True))

            st16, ds16, do16, w16 = st.astype(BF16), ds_out.astype(BF16), d_o.astype(BF16), w.astype(BF16)
            q16, k16, t16 = cr["q16"], cr["k16"], cr["t16"]
            vn = u - _dot(w16, st16, 1, 0)
            d_vn = _dot(p, do16, 0, 0) + _dot(cr["kd"], ds16, 1, 0)
            d_qg = _dot(do16, st16, 1, 1)
            qgdo = _dot(cr["qg"], do16, 0, 0)
            yield
            vn16, dvn16 = vn.astype(BF16), d_vn.astype(BF16)
            d_p = jnp.where(cr["causal"], _dot(do16, vn16, 1, 1), 0.0)
            d_kd = _dot(vn16, ds16, 1, 1)
            d_gam_last = jnp.sum(jnp.sum(st * ds_out, axis=1, keepdims=True), axis=0, keepdims=True)
            d_w = -_dot(dvn16, st16, 1, 1)
            dstate[hh] = qgdo + ds_out * cr["gam_last"] - _dot(w16, dvn16, 0, 0)
            d_bv = _dot(t16, dvn16, 0, 0)
            yield
            d_kbg = _dot(t16, d_w, 0, 0)
            n_p = (d_p * dec).astype(BF16)
            d_q = _dot(n_p, k16, 1, 0) + d_qg * gam
            npq = _dot(n_p, q16, 0, 0)
            yield
            d_a = jnp.where(cr["strict"], -(_dot(d_bv, u, 1, 1) + _dot(d_kbg, w16, 1, 1)), 0.0)
            yield
            m_a = (d_a * dec).astype(BF16)
            d_kb = _dot(m_a, k16, 1, 0) + d_kbg * gam
            d_k = (_dot(m_a, cr["kb16"], 0, 0) + npq + d_kd * e + d_kb * b_col)
            yield
            d_v = d_bv * b_col
            d_beta_col = (jnp.sum(d_bv * v, axis=1, keepdims=True)
                          + jnp.sum(d_kb * k, axis=1, keepdims=True))
            gterm = d_a * cr["a_mat"] + d_p * p
            d_e = jnp.sum(d_kd * k, axis=1, keepdims=True) * e
            d_g_col = (jnp.sum(gterm, axis=1, keepdims=True)
                       + (jnp.sum(d_qg * q, axis=1, keepdims=True)
                          + jnp.sum(d_kbg * cr["kb"], axis=1, keepdims=True)) * gam
                       - d_e)
            d_g_last = jnp.sum(d_e, axis=0, keepdims=True) + d_gam_last * cr["gam_last"]
            d_g_row = (_col_to_row(d_g_col, eye) - jnp.sum(gterm, axis=0, keepdims=True)
                       + jnp.where(lane == A_CHUNK - 1, d_g_last, 0.0))
            dgc_ref[hh, seq_chunk, :] = d_g_row
            dbeta_ref[hh, seq_chunk, :] = _col_to_row(d_beta_col, eye)

            qh = cin["aq"] * cin["rq"]
            kh = cin["ak"] * cin["rk"]
            d_qh = d_q * (A_DK ** -0.5)
            d_aq = cin["rq"] * (d_qh - qh * jnp.sum(d_qh * qh, axis=1, keepdims=True))
            d_ak = cin["rk"] * (d_k - kh * jnp.sum(d_k * kh, axis=1, keepdims=True))
            d_y = jnp.concatenate([d_aq, d_ak, d_v], axis=1) * _dsilu(cin["y"])
            shifted[hh, 0, 0:A_CHUNK, :] = d_y
            shifted[hh, 0, A_CHUNK:A_CHUNK + HALO, :] = dy_next[hh]
            shifted[hh, 1, 0:A_CHUNK + HALO, :] = cin["xw"]
            d_x = cw[A_CONV - 1:A_CONV, :] * d_y
            for j in range(1, A_CONV):
                d_x = d_x + cw[A_CONV - 1 - j:A_CONV - j, :] * shifted[hh, 0, j:j + A_CHUNK, :]
            for j in range(A_CONV):
                xs = shifted[hh, 1, HALO - j:HALO - j + A_CHUNK, :]
                dcw_ref[A_CONV - 1 - j:A_CONV - j, ccols] += jnp.sum(d_y * xs, axis=0, keepdims=True)
            dy_next[hh] = d_y[0:HALO, :]
            dx_ref[rows, lo:lo + A_CONV_COLS] = d_x.astype(BF16)
            dx_ref[rows, lo + A_CONV_COLS:lo + A_HEAD_COLS] = dz.astype(BF16)

        def chunk(ci, first):
            rows = pl.ds(0 if first else pl.multiple_of(ci * A_CHUNK, A_CHUNK), A_CHUNK)
            _round_robin([one_head(hh, ci, first, rows) for hh in range(hp)])

        lax.fori_loop(0, A_BLK_CHUNKS - 1, lambda i, c: (chunk(A_BLK_CHUNKS - 1 - i, False), c)[1], 0)
        chunk(0, True)

    rev = lambda j: nsb - 1 - j
    small = pl.BlockSpec((None, hp, n, A_CHUNK), lambda b, h, j: (b, h, 0, 0))
    wide = pl.BlockSpec((None, A_SEQ_BLK, hp * A_HEAD_COLS), lambda b, h, j: (b, rev(j), h))
    val = pl.BlockSpec((None, A_SEQ_BLK, hp * A_DV), lambda b, h, j: (b, rev(j), h))
    return pl.pallas_call(
        body, name="gdn_bwd", grid=(bn, A_HEADS // hp, nsb),
        in_specs=[wide,
                  pl.BlockSpec((None, None, HALO, hp * A_HEAD_COLS), lambda b, h, j: (b, rev(j), 0, h)),
                  pl.BlockSpec((A_CONV, hp * A_CONV_COLS), lambda b, h, j: (0, h)),
                  small, small,
                  pl.BlockSpec((1, A_DV), lambda b, h, j: (0, 0)),
                  val,
                  pl.BlockSpec((None, hp, A_BLK_CHUNKS, A_DK, A_DV), lambda b, h, j: (b, h, rev(j), 0, 0)),
                  pl.BlockSpec((None, hp, A_BLK_CHUNKS, A_CHUNK, A_CHUNK), lambda b, h, j: (b, h, rev(j), 0, 0)),
                  pl.BlockSpec((None, A_SEQ_BLK, hp * A_CONV_COLS), lambda b, h, j: (b, rev(j), h)),
                  val],
        out_specs=[wide, small, small,
                   pl.BlockSpec((None, A_CONV, hp * A_CONV_COLS), lambda b, h, j: (b, 0, h)),
                   pl.BlockSpec((None, hp, 8, A_DV), lambda b, h, j: (b, h, 0, 0))],
        out_shape=[jax.ShapeDtypeStruct((bn, s, A_HEADS * A_HEAD_COLS), BF16),
                   jax.ShapeDtypeStruct((bn, A_HEADS, n, A_CHUNK), F32),
                   jax.ShapeDtypeStruct((bn, A_HEADS, n, A_CHUNK), F32),
                   jax.ShapeDtypeStruct((bn, A_CONV, A_HEADS * A_CONV_COLS), F32),
                   jax.ShapeDtypeStruct((bn, A_HEADS, 8, A_DV), F32)],
        scratch_shapes=[pltpu.VMEM((hp, A_DK, A_DV), F32), pltpu.VMEM((hp, HALO, A_CONV_COLS), F32),
                        pltpu.VMEM((hp, 2, A_CHUNK + 2 * HALO, A_CONV_COLS), F32)],
        compiler_params=_params(("parallel", "parallel", "arbitrary"), VMEM_BIG),
    )(proj_hm, halo, cw_hm, beta, gc, norm_g, oraw, states, t_mats, conv_y, dog)


def _rope_tables(posf, inv_freq_row):
    t = posf.shape[0]
    tm = 512

    def body(p_ref, f_ref, c_ref, sa_ref, sb_ref):
        ang = p_ref[...] * f_ref[...]
        lane = lax.broadcasted_iota(jnp.int32, ang.shape, 1)
        half = ROPE_DIMS // 2
        c_ref[...] = jnp.where(lane < ROPE_DIMS, jnp.cos(ang), 1.0)
        sn = jnp.sin(ang)
        sa_ref[...] = jnp.where(lane < half, -sn, 0.0)
        sb_ref[...] = jnp.where((lane >= half) & (lane < ROPE_DIMS), sn, 0.0)

    row = pl.BlockSpec((tm, 128), lambda i: (i, 0))
    return pl.pallas_call(
        body, name="rope_tables", grid=(t // tm,),
        in_specs=[row, pl.BlockSpec((1, 128), lambda i: (0, 0))], out_specs=[row] * 3,
        out_shape=[jax.ShapeDtypeStruct((t, 128), F32)] * 3,
        compiler_params=_params(("parallel",)),
    )(posf, inv_freq_row)


def _rope(x, c, sa, sb):
    half = ROPE_DIMS // 2
    return x * c + pltpu.roll(x, 128 - half, 1) * sa + pltpu.roll(x, half, 1) * sb


def _rope_t(d, c, sa, sb):
    half = ROPE_DIMS // 2
    return d * c + pltpu.roll(d * sa, half, 1) + pltpu.roll(d * sb, 128 - half, 1)


def _qk_prep(proj, c, sa, sb, qg, kg, name, tm=256):
    t = proj.shape[0]

    def body(x_ref, c_ref, sa_ref, sb_ref, qg_ref, kg_ref, o_ref):
        cc, s1, s2 = c_ref[...], sa_ref[...], sb_ref[...]
        half = ROPE_DIMS // 2

        def one_head(lo, g):
            xv = x_ref[:, lo:lo + B_DH]
            ms = jnp.mean(xv * xv, axis=1, keepdims=True)
            yield
            xn = xv * lax.rsqrt(ms + EPS) * g
            r1, r2 = pltpu.roll(xn, 128 - half, 1), pltpu.roll(xn, half, 1)
            yield
            o_ref[:, lo:lo + B_DH] = (xn * cc + r1 * s1 + r2 * s2).astype(BF16)

        for which, g_ref in ((0, qg_ref), (1, kg_ref)):
            g = g_ref[...]
            _round_robin([one_head(which * B_W + h * B_DH, g) for h in range(B_HEADS)])
        o_ref[:, 2 * B_W:3 * B_W] = x_ref[:, 2 * B_W:3 * B_W].astype(BF16)

    tab = pl.BlockSpec((tm, 128), lambda i: (i, 0))
    gain = pl.BlockSpec((1, B_DH), lambda i: (0, 0))
    return pl.pallas_call(
        body, name=name, grid=(t // tm,),
        in_specs=[pl.BlockSpec((tm, 3 * B_W), lambda i: (i, 0)), tab, tab, tab, gain, gain],
        out_specs=pl.BlockSpec((tm, 3 * B_W), lambda i: (i, 0)),
        out_shape=jax.ShapeDtypeStruct((t, 3 * B_W), BF16),
        compiler_params=_params(("parallel",), 40 * 1024 * 1024),
    )(proj, c, sa, sb, qg, kg)


def _qk_prep_bwd(proj, c, sa, sb, qg, kg, dq, dk, dv, dz, name, tm=256):
    t = proj.shape[0]
    out_w = 3 * B_W + (B_W if dz is not None else 0)

    def body(*refs):
        x_ref, c_ref, sa_ref, sb_ref, qg_ref, kg_ref, dq_ref, dk_ref, dv_ref = refs[:9]
        if dz is not None:
            dz_ref, o_ref, dgain_ref = refs[9:]
        else:
            o_ref, dgain_ref = refs[9:]
        i = pl.program_id(0)

        @pl.when(i == 0)
        def _():
            dgain_ref[...] = jnp.zeros_like(dgain_ref)

        cc, s1, s2 = c_ref[...], sa_ref[...], sb_ref[...]
        half = ROPE_DIMS // 2

        def one_head(which, h, g, d_ref, parts):
            lo = which * B_W + h * B_DH
            xv = x_ref[:, lo:lo + B_DH]
            d_out = d_ref[:, h * B_DH:(h + 1) * B_DH].astype(F32)
            ms = jnp.mean(xv * xv, axis=1, keepdims=True)
            r1, r2 = pltpu.roll(d_out * s1, half, 1), pltpu.roll(d_out * s2, 128 - half, 1)
            yield
            r = lax.rsqrt(ms + EPS)
            xh = xv * r
            d_xn = d_out * cc + r1 + r2
            parts.append(jnp.sum(d_xn * xh, axis=0, keepdims=True))
            d_xh = d_xn * g
            dot = jnp.mean(d_xh * xh, axis=1, keepdims=True)
            yield
            o_ref[:, lo:lo + B_DH] = (r * (d_xh - xh * dot)).astype(BF16)

        for which, g_ref, d_ref in ((0, qg_ref, dq_ref), (1, kg_ref, dk_ref)):
            parts = []
            _round_robin([one_head(which, h, g_ref[...], d_ref, parts) for h in range(B_HEADS)])
            acc = parts[0]
            for part in parts[1:]:
                acc = acc + part
            dgain_ref[which:which + 1, :] += acc
        o_ref[:, 2 * B_W:3 * B_W] = dv_ref[...]
        if dz is not None:
            o_ref[:, 3 * B_W:4 * B_W] = dz_ref[...]

    tab = pl.BlockSpec((tm, 128), lambda i: (i, 0))
    gain = pl.BlockSpec((1, B_DH), lambda i: (0, 0))
    grad = pl.BlockSpec((tm, B_W), lambda i: (i, 0))
    in_specs = [pl.BlockSpec((tm, 2 * B_W), lambda i: (i, 0)), tab, tab, tab, gain, gain, grad, grad, grad]
    args = [proj, c, sa, sb, qg, kg, dq, dk, dv]
    if dz is not None:
        in_specs.append(grad)
        args.append(dz)
    return pl.pallas_call(
        body, name=name, grid=(t // tm,), in_specs=in_specs,
        out_specs=[pl.BlockSpec((tm, out_w), lambda i: (i, 0)), pl.BlockSpec((8, B_DH), lambda i: (0, 0))],
        out_shape=[jax.ShapeDtypeStruct((t, out_w), BF16), jax.ShapeDtypeStruct((8, B_DH), F32)],
        compiler_params=_params(("arbitrary",), 40 * 1024 * 1024),
    )(*args)


def _attn_masks():
    qi = lax.broadcasted_iota(jnp.int32, (B_BLK, 2 * B_BLK), 0)
    kj = lax.broadcasted_iota(jnp.int32, (B_BLK, 2 * B_BLK), 1)
    two = (kj >= qi) & (kj <= qi + B_BLK)
    q1 = lax.broadcasted_iota(jnp.int32, (B_BLK, B_BLK), 0)
    k1 = lax.broadcasted_iota(jnp.int32, (B_BLK, B_BLK), 1)
    return k1 <= q1, two


def _lane_pick(ref_rows, h):
    lane = lax.broadcasted_iota(jnp.int32, ref_rows.shape, 1)
    return jnp.sum(jnp.where(lane == h, ref_rows, 0.0), axis=1, keepdims=True)


B_ROWS = 2048


def _attn_schedule(nb, sb, block):
    way = 4

    def run(items):
        for at in range(0, len(items), way):
            _round_robin([block(*it) for it in items[at:at + way]])

    run([(si, 0, True) for si in range(sb)])
    if nb == 1:
        return
    per = max(1, way // sb)
    lead = 1 + (nb - 1) % per
    if lead > 1:
        run([(si, i, False) for i in range(1, lead) for si in range(sb)])

    def step(it, carry):
        run([(si, lead + it * per + u, False) for u in range(per) for si in range(sb)])
        return carry

    lax.fori_loop(0, (nb - lead) // per, step, 0)


def _attn_rows(i, first):
    if first:
        return pl.ds(0, B_BLK), pl.ds(0, B_BLK)
    rows = pl.ds(pl.multiple_of(i * B_BLK, B_BLK), B_BLK)
    return rows, pl.ds(pl.multiple_of((i - 1) * B_BLK, B_BLK), 2 * B_BLK)


def _attn_fwd(qkv, name):
    ns, ln, _ = qkv.shape
    nb = ln // B_BLK
    sb = B_ROWS // ln
    scale = B_DH ** -0.5

    def body(q_ref, k_ref, v_ref, o_ref, lse_ref):
        h = pl.program_id(1)
        mask1, mask2 = _attn_masks()
        lane = lax.broadcasted_iota(jnp.int32, (B_BLK, B_HEADS), 1)

        @pl.when(h == 0)
        def _():
            lse_ref[...] = jnp.zeros_like(lse_ref)

        def block(si, i, first):
            rows, win = _attn_rows(i, first)
            mask = mask1 if first else mask2
            sc = jnp.where(mask, _dot(q_ref[si, rows, :], k_ref[si, win, :], 1, 1) * scale, -1e30)
            yield
            m = jnp.max(sc, axis=1, keepdims=True)
            p = jnp.exp(sc - m)
            l = jnp.sum(p, axis=1, keepdims=True)
            pv = _dot(p, v_ref[si, win, :], 1, 0)
            yield
            o_ref[si, rows, :] = pv / l
            lse_ref[si, rows, :] = jnp.where(lane == h, m + jnp.log(l), lse_ref[si, rows, :])

        _attn_schedule(nb, sb, block)

    head = lambda off: pl.BlockSpec((sb, ln, B_DH), lambda s, h: (s, 0, off + h))
    return pl.pallas_call(
        body, name=name, grid=(ns // sb, B_HEADS),
        in_specs=[head(0), head(B_HEADS), head(2 * B_HEADS)],
        out_specs=[head(0), pl.BlockSpec((sb, ln, B_HEADS), lambda s, h: (s, 0, 0))],
        out_shape=[jax.ShapeDtypeStruct((ns, ln, B_W), F32), jax.ShapeDtypeStruct((ns, ln, B_HEADS), F32)],
        compiler_params=_params(("parallel", "arbitrary")),
    )(qkv, qkv, qkv)


def _attn_bwd(qkv, d_o, lse_joint, delta, name):
    ns, ln, _ = qkv.shape
    nb = ln // B_BLK
    sb = B_ROWS // ln
    scale = B_DH ** -0.5

    def body(q_ref, k_ref, v_ref, do_ref, lj_ref, dl_ref, dq_ref, dk_out, dv_out, dk_ref, dv_ref):
        h = pl.program_id(1)
        mask1, mask2 = _attn_masks()
        dk_ref[...] = jnp.zeros_like(dk_ref)
        dv_ref[...] = jnp.zeros_like(dv_ref)

        def block(si, i, first):
            rows, win = _attn_rows(i, first)
            mask = mask1 if first else mask2
            q = q_ref[si, rows, :]
            d_out = do_ref[si, rows, :]
            l_col = _lane_pick(lj_ref[si, rows, :], h)
            d_col = _lane_pick(dl_ref[si, rows, :], h)
            sc = _dot(q, k_ref[si, win, :], 1, 1) * scale
            d_p = _dot(d_out, v_ref[si, win, :], 1, 1)
            yield
            p = jnp.exp(jnp.where(mask, sc - l_col, -1e30))
            d_s = p * (d_p - d_col) * scale
            d_q = _dot(d_s, k_ref[si, win, :], 1, 0)
            d_k = _dot(d_s, q, 0, 0)
            d_v = _dot(p, d_out, 0, 0)
            yield
            dq_ref[si, rows, :] = d_q.astype(BF16)
            dk_ref[si, win, :] += d_k
            dv_ref[si, win, :] += d_v

        _attn_schedule(nb, sb, block)
        dk_out[...] = dk_ref[...].astype(BF16)
        dv_out[...] = dv_ref[...].astype(BF16)

    head = lambda off: pl.BlockSpec((sb, ln, B_DH), lambda s, h: (s, 0, off + h))
    small = pl.BlockSpec((sb, ln, B_HEADS), lambda s, h: (s, 0, 0))
    return pl.pallas_call(
        body, name=name, grid=(ns // sb, B_HEADS),
        in_specs=[head(0), head(B_HEADS), head(2 * B_HEADS), head(0), small, small],
        out_specs=[head(0)] * 3,
        out_shape=[jax.ShapeDtypeStruct((ns, ln, B_W), BF16)] * 3,
        scratch_shapes=[pltpu.VMEM((sb, ln, B_DH), F32)] * 2,
        compiler_params=_params(("parallel", "parallel")),
    )(qkv, qkv, qkv, d_o, lse_joint, delta)


def _merge_weights(lse_refs):
    ls = [r[...] for r in lse_refs]
    m = jnp.maximum(jnp.maximum(ls[0], ls[1]), ls[2])
    es = [jnp.exp(l - m) for l in ls]
    tot = es[0] + es[1] + es[2]
    return [e / tot for e in es], m + jnp.log(tot)


def _merge_fwd(outs, lses, proj0, tm=256):
    t = outs[0].shape[0]

    def body(o0, o1, o2, l0, l1, l2, z_ref, og_ref):
        wts, _ = _merge_weights((l0, l1, l2))
        for h in range(B_HEADS):
            cols = slice(h * B_DH, (h + 1) * B_DH)
            o = (wts[0][:, h:h + 1] * o0[:, cols] + wts[1][:, h:h + 1] * o1[:, cols]
                 + wts[2][:, h:h + 1] * o2[:, cols])
            og_ref[:, cols] = (o * _silu(z_ref[:, cols])).astype(BF16)

    wide = pl.BlockSpec((tm, B_W), lambda i: (i, 0))
    small = pl.BlockSpec((tm, B_HEADS), lambda i: (i, 0))
    return pl.pallas_call(
        body, name="merge_fwd", grid=(t // tm,),
        in_specs=[wide] * 3 + [small] * 3 + [pl.BlockSpec((tm, B_W), lambda i: (i, 3))],
        out_specs=wide, out_shape=jax.ShapeDtypeStruct((t, B_W), BF16),
        compiler_params=_params(("parallel",)),
    )(*outs, *lses, proj0)


def _merge_bwd(outs, lses, proj0, d_og, tm=256):
    t = outs[0].shape[0]

    def body(o0, o1, o2, l0, l1, l2, z_ref, dog_ref, do_ref, lj_ref, dl_ref, dz_ref):
        wts, lj = _merge_weights((l0, l1, l2))
        lj_ref[...] = lj
        lane = lax.broadcasted_iota(jnp.int32, (tm, B_HEADS), 1)
        delta = jnp.zeros((tm, B_HEADS), F32)
        for h in range(B_HEADS):
            cols = slice(h * B_DH, (h + 1) * B_DH)
            o = (wts[0][:, h:h + 1] * o0[:, cols] + wts[1][:, h:h + 1] * o1[:, cols]
                 + wts[2][:, h:h + 1] * o2[:, cols])
            z = z_ref[:, cols]
            d_g = dog_ref[:, cols]
            d_out = d_g * _silu(z)
            dz_ref[:, cols] = (d_g * o * _dsilu(z)).astype(BF16)
            do_ref[:, cols] = d_out.astype(BF16)
            delta = jnp.where(lane == h, jnp.sum(d_out * o, axis=1, keepdims=True), delta)
        dl_ref[...] = delta

    wide = pl.BlockSpec((tm, B_W), lambda i: (i, 0))
    small = pl.BlockSpec((tm, B_HEADS), lambda i: (i, 0))
    return pl.pallas_call(
        body, name="merge_bwd", grid=(t // tm,),
        in_specs=[wide] * 3 + [small] * 3 + [pl.BlockSpec((tm, B_W), lambda i: (i, 3)), wide],
        out_specs=[wide, small, small, wide],
        out_shape=[jax.ShapeDtypeStruct((t, B_W), BF16), jax.ShapeDtypeStruct((t, B_HEADS), F32),
                   jax.ShapeDtypeStruct((t, B_HEADS), F32), jax.ShapeDtypeStruct((t, B_W), BF16)],
        compiler_params=_params(("parallel",)),
    )(*outs, *lses, proj0, d_og)


def _adamw(w, g, m, v, name):
    r, c = w.shape
    tr = r
    for cand in (256, 128, 64, 32, 16, 8):
        if r % cand == 0:
            tr = cand
            break

    def body(w_ref, g_ref, m_ref, v_ref, d_ref, nm_ref, nv_ref):
        gv = g_ref[...]
        nm = ADAM_B1 * m_ref[...] + (1.0 - ADAM_B1) * gv
        nv = ADAM_B2 * v_ref[...] + (1.0 - ADAM_B2) * (gv * gv)
        m_hat = nm / (1.0 - ADAM_B1 ** ADAM_STEP)
        v_hat = nv / (1.0 - ADAM_B2 ** ADAM_STEP)
        d_ref[...] = -ADAM_LR * (m_hat / (jnp.sqrt(v_hat) + ADAM_EPS) + ADAM_WD * w_ref[...])
        nm_ref[...] = nm
        nv_ref[...] = nv

    blk = pl.BlockSpec((tr, c), lambda i: (i, 0))
    return pl.pallas_call(
        body, name=name, grid=(r // tr,), in_specs=[blk] * 4, out_specs=[blk] * 3,
        out_shape=[jax.ShapeDtypeStruct((r, c), F32)] * 3,
        compiler_params=_params(("parallel",)),
    )(w, g, m, v)


def _adam_update(w, gv, m, v):
    nm = ADAM_B1 * m + (1.0 - ADAM_B1) * gv
    nv = ADAM_B2 * v + (1.0 - ADAM_B2) * (gv * gv)
    m_hat = nm / (1.0 - ADAM_B1 ** ADAM_STEP)
    v_hat = nv / (1.0 - ADAM_B2 ** ADAM_STEP)
    return -ADAM_LR * (m_hat / (jnp.sqrt(v_hat) + ADAM_EPS) + ADAM_WD * w), nm, nv


def _adamw_shard(w, mine, theirs, m, v, half_index, name, tr=128):
    _, r, c = w.shape
    nhb = (r // 2) // tr

    def body(c_ref, w_ref, mine_ref, theirs_ref, m_ref, v_ref, g_ref, d_ref, nm_ref, nv_ref):
        is_mine = (pl.program_id(0) // nhb) == c_ref[0]
        gv = jnp.where(is_mine, mine_ref[...], theirs_ref[...])
        d, nm, nv = _adam_update(w_ref[...], gv, m_ref[...], v_ref[...])
        g_ref[...] = gv
        d_ref[...] = d
        nm_ref[...] = nm
        nv_ref[...] = nv

    full = pl.BlockSpec((None, tr, c), lambda i, cc: (0, i, 0))
    half = pl.BlockSpec((tr, c), lambda i, cc: (i % nhb, 0))
    return pl.pallas_call(
        body, name=name,
        grid_spec=pltpu.PrefetchScalarGridSpec(
            num_scalar_prefetch=1, grid=(2 * nhb,),
            in_specs=[full, half, half, full, full], out_specs=[full] * 4),
        out_shape=[jax.ShapeDtypeStruct(w.shape, F32)] * 4,
        compiler_params=_params(("parallel",), 40 * 1024 * 1024),
    )(half_index, w, mine, theirs, m, v)


def _adamw_shard_cols(w, mine, theirs, m, v, half_index, name, steps=20):
    c, _, r = w.shape
    tc = c // steps
    assert tc * steps == c

    def body(c_ref, w_ref, mine_ref, theirs_ref, m_ref, v_ref, g_ref, d_ref, nm_ref, nv_ref):
        first = jnp.where(c_ref[0] == 0, mine_ref[...], theirs_ref[...])
        second = jnp.where(c_ref[0] == 0, theirs_ref[...], mine_ref[...])
        for lo, gv in ((0, first), (r // 2, second)):
            cols = slice(lo, lo + r // 2)
            d, nm, nv = _adam_update(w_ref[:, :, cols], gv, m_ref[:, :, cols], v_ref[:, :, cols])
            g_ref[:, :, cols] = gv
            d_ref[:, :, cols] = d
            nm_ref[:, :, cols] = nm
            nv_ref[:, :, cols] = nv

    full = pl.BlockSpec((tc, 1, r), lambda i, cc: (i, 0, 0))
    half = pl.BlockSpec((tc, 1, r // 2), lambda i, cc: (i, 0, 0))
    return pl.pallas_call(
        body, name=name,
        grid_spec=pltpu.PrefetchScalarGridSpec(
            num_scalar_prefetch=1, grid=(steps,),
            in_specs=[full, half, half, full, full], out_specs=[full] * 4),
        out_shape=[jax.ShapeDtypeStruct(w.shape, F32)] * 4,
        compiler_params=_params(("parallel",), 40 * 1024 * 1024),
    )(half_index, w, mine, theirs, m, v)


def _pair_sum(own, other, half_index, name, tr=256):
    _, r, c = own.shape
    rh = r // 2
    tr = min(tr, rh)
    nrb = rh // tr

    def body(c_ref, own_ref, oth_ref, out_ref):
        out_ref[...] = (own_ref[...] + oth_ref[...].astype(F32)).astype(BF16)

    return pl.pallas_call(
        body, name=name,
        grid_spec=pltpu.PrefetchScalarGridSpec(
            num_scalar_prefetch=1, grid=(N_CHIPS, nrb),
            in_specs=[pl.BlockSpec((None, tr, c), lambda k, i, cc: (k, cc[0] * nrb + i, 0)),
                      pl.BlockSpec((None, tr, c), lambda k, i, cc: (k, i, 0))],
            out_specs=pl.BlockSpec((None, tr, c), lambda k, i, cc: (k, i, 0))),
        out_shape=jax.ShapeDtypeStruct((N_CHIPS, rh, c), BF16),
        compiler_params=_params(("parallel", "parallel")),
    )(half_index, own, other)


def _chip_sum(sums, others, chip_index, name, tr=256):
    _, r, c = sums.shape
    tr = min(tr, r)

    def body(k_ref, own_ref, oth_ref, out_ref):
        acc = own_ref[...].astype(F32)
        for j in range(N_CHIPS - 1):
            acc = acc + oth_ref[j].astype(F32)
        out_ref[...] = acc

    return pl.pallas_call(
        body, name=name,
        grid_spec=pltpu.PrefetchScalarGridSpec(
            num_scalar_prefetch=1, grid=(r // tr,),
            in_specs=[pl.BlockSpec((None, tr, c), lambda i, kk: (kk[0], i, 0)),
                      pl.BlockSpec((N_CHIPS - 1, tr, c), lambda i, kk: (0, i, 0))],
            out_specs=pl.BlockSpec((tr, c), lambda i, kk: (i, 0))),
        out_shape=jax.ShapeDtypeStruct((r, c), F32),
        compiler_params=_params(("parallel",)),
    )(chip_index, sums, others)


HBM = pl.BlockSpec(memory_space=pltpu.HBM)


def _place():
    x, y, c = lax.axis_index("x"), lax.axis_index("y"), lax.axis_index("c")
    chips = [(1 - x, y), (x, 1 - y), (1 - x, 1 - y)]
    return x, y, c, chips


def _sibling_forward(land):
    def body(in_ref, out_ref, send, recv):
        x, y, c, chips = _place()
        rh = out_ref.shape[1] // 2
        cps = []
        for j, (px, py) in enumerate(chips):
            slot = out_ref.at[2 * px + py, pl.ds(c * rh, rh)]
            cp = pltpu.make_async_remote_copy(
                src_ref=slot, dst_ref=slot, send_sem=send.at[j], recv_sem=recv.at[j],
                device_id=(x, y, 1 - c), device_id_type=MESH)
            cp.start()
            cps.append(cp)
        for j, (px, py) in enumerate(chips):
            slot = out_ref.at[2 * px + py, pl.ds((1 - c) * rh, rh)]
            pltpu.make_async_remote_copy(
                src_ref=slot, dst_ref=slot, send_sem=send.at[j], recv_sem=recv.at[j],
                device_id=(x, y, 1 - c), device_id_type=MESH).wait_recv()
        for cp in cps:
            cp.wait_send()

    return pl.pallas_call(
        body, name="first_weights_sibling_forward", in_specs=[HBM], out_specs=HBM,
        out_shape=jax.ShapeDtypeStruct(land.shape, land.dtype), input_output_aliases={0: 0},
        scratch_shapes=[pltpu.SemaphoreType.DMA((3,)), pltpu.SemaphoreType.DMA((3,))],
    )(land)


def _sibling_swap_halves(grads, name):
    na = len(grads)

    def body(*refs):
        ins, outs = refs[:na], refs[na:2 * na]
        send, recv = refs[2 * na:]
        x, y, c, _ = _place()
        sib = (x, y, 1 - c)
        cps = []
        for i in range(na):
            rh = ins[i].shape[1] // 2
            cp = pltpu.make_async_remote_copy(
                src_ref=ins[i].at[:, pl.ds((1 - c) * rh, rh), :], dst_ref=outs[i],
                send_sem=send.at[i], recv_sem=recv.at[i], device_id=sib, device_id_type=MESH)
            cp.start()
            cps.append(cp)
        for cp in cps:
            cp.wait()

    out_shape = [jax.ShapeDtypeStruct((g.shape[0], g.shape[1] // 2, g.shape[2]), g.dtype) for g in grads]
    return pl.pallas_call(
        body, name=name, in_specs=[HBM] * na, out_specs=[HBM] * na, out_shape=out_shape,
        scratch_shapes=[pltpu.SemaphoreType.DMA((na,)), pltpu.SemaphoreType.DMA((na,))],
    )(*grads)


def _sibling_swap_whole(halves):
    na = len(halves)

    def body(*refs):
        ins, outs = refs[:na], refs[na:2 * na]
        send, recv = refs[2 * na:]
        x, y, c, _ = _place()
        cps = []
        for i in range(na):
            cp = pltpu.make_async_remote_copy(
                src_ref=ins[i], dst_ref=outs[i], send_sem=send.at[i], recv_sem=recv.at[i],
                device_id=(x, y, 1 - c), device_id_type=MESH)
            cp.start()
            cps.append(cp)
        for cp in cps:
            cp.wait()

    out_shape = [jax.ShapeDtypeStruct(h.shape, h.dtype) for h in halves]
    return pl.pallas_call(
        body, name="grad_sibling_join", in_specs=[HBM] * na, out_specs=[HBM] * na, out_shape=out_shape,
        scratch_shapes=[pltpu.SemaphoreType.DMA((na,)), pltpu.SemaphoreType.DMA((na,))],
    )(*halves)


SEM = pl.BlockSpec(memory_space=pltpu.SEMAPHORE)
ANY = pl.BlockSpec(memory_space=pl.ANY)
EFFECT = pltpu.SideEffectType.DATAFLOW_SIDE_EFFECTING


def _split_copy_start(name, plan, srcs, lands, after):
    ns, nl = len(srcs), len(lands)

    def body(*refs):
        src_refs, land_refs = refs[:ns], refs[ns:ns + nl]
        send, recv = refs[ns + nl + 1], refs[ns + nl + 2]
        token = refs[-1]
        outgoing, _ = plan(src_refs, land_refs)
        for src, dst, dev, si, ri in outgoing:
            pltpu.make_async_remote_copy(src_ref=src, dst_ref=dst, send_sem=send.at[si], recv_sem=recv.at[ri],
                                         device_id=dev, device_id_type=MESH).start()
        token[...] = jnp.zeros_like(token)

    n_out, n_in = plan.counts
    thru = [pltpu.HBM(a.shape, a.dtype) for a in list(srcs) + list(lands)]
    res = pl.pallas_call(
        body, name=name,
        out_shape=[pltpu.SemaphoreType.DMA((n_out,)), pltpu.SemaphoreType.DMA((n_in,))] + thru
        + [jax.ShapeDtypeStruct((8, 128), F32)],
        in_specs=[HBM] * (ns + nl) + [ANY],
        out_specs=[SEM, SEM] + [HBM] * (ns + nl) + [pl.BlockSpec(memory_space=pltpu.VMEM)],
        input_output_aliases={i: 2 + i for i in range(ns + nl)},
        compiler_params=pltpu.CompilerParams(has_side_effects=EFFECT),
    )(*[pltpu.with_memory_space_constraint(a, pltpu.HBM) for a in list(srcs) + list(lands)], after)
    return res[0], res[1], res[2:2 + ns], res[2 + ns:2 + ns + nl], res[-1]


def _split_copy_wait(name, plan, send, recv, srcs, lands, after):
    ns, nl = len(srcs), len(lands)
    after = list(after) if isinstance(after, (list, tuple)) else [after]

    def body(*refs):
        src_refs, land_refs = refs[:ns], refs[ns:ns + nl]
        send_ref, recv_ref = refs[ns + nl], refs[ns + nl + 1]
        outgoing, arrivals = plan(src_refs, land_refs)
        for src, dst, dev, si, ri in outgoing:
            pltpu.make_async_remote_copy(src_ref=src, dst_ref=dst, send_sem=send_ref.at[si], recv_sem=recv_ref.at[ri],
                                         device_id=dev, device_id_type=MESH).wait_send()
        for view, ri in arrivals:
            pltpu.make_async_remote_copy(src_ref=view, dst_ref=view, send_sem=send_ref.at[0], recv_sem=recv_ref.at[ri],
                                         device_id=_place()[:3], device_id_type=MESH).wait_recv()

    thru = [pltpu.HBM(a.shape, a.dtype) for a in list(srcs) + list(lands)]
    res = pl.pallas_call(
        body, name=name, out_shape=thru,
        in_specs=[HBM] * (ns + nl) + [SEM, SEM] + [ANY] * len(after), out_specs=[HBM] * (ns + nl),
        input_output_aliases={i: i for i in range(ns + nl)},
        compiler_params=pltpu.CompilerParams(has_side_effects=EFFECT),
    )(*srcs, *lands, send, recv, *after)
    return res[:ns], res[ns:]


def _gather_plan(n_arrays):
    def plan(src_refs, land_refs):
        x, y, c, chips = _place()
        me = 2 * x + y
        outgoing, arrivals = [], []
        for i in range(n_arrays):
            rh = src_refs[i].shape[0] // 2
            mine = pl.ds(c * rh, rh)
            for j, (px, py) in enumerate(chips):
                for delta in range(2):
                    tc = c ^ delta
                    outgoing.append((src_refs[i].at[mine], land_refs[i].at[me, mine], (px, py, tc),
                                     6 * i + 2 * j + delta, 6 * i + 2 * j + delta))
                    theirs = pl.ds(tc * rh, rh)
                    arrivals.append((land_refs[i].at[2 * px + py, theirs], 6 * i + 2 * j + delta))
        return outgoing, arrivals

    plan.counts = (6 * n_arrays, 6 * n_arrays)
    return plan


def _first_gather_plan():
    def plan(src_refs, land_refs):
        x, y, c, chips = _place()
        me = 2 * x + y
        rh = src_refs[0].shape[0] // 2
        mine = pl.ds(c * rh, rh)
        outgoing, arrivals = [], []
        for j, (px, py) in enumerate(chips):
            outgoing.append((src_refs[0].at[mine], land_refs[0].at[me, mine], (px, py, c), j, j))
            arrivals.append((land_refs[0].at[2 * px + py, mine], j))
            outgoing.append((src_refs[1], land_refs[1].at[me], (px, py, c), 3 + j, 3 + j))
            arrivals.append((land_refs[1].at[2 * px + py], 3 + j))
        return outgoing, arrivals

    plan.counts = (6, 6)
    return plan


def _exchange_plan(n_arrays):
    def plan(src_refs, land_refs):
        x, y, c, chips = _place()
        outgoing, arrivals = [], []
        for i in range(n_arrays):
            for j, (px, py) in enumerate(chips):
                outgoing.append((src_refs[i].at[2 * px + py], land_refs[i].at[j], (px, py, c), 3 * i + j, 3 * i + j))
                arrivals.append((land_refs[i].at[j], 3 * i + j))
        return outgoing, arrivals

    plan.counts = (3 * n_arrays, 3 * n_arrays)
    return plan


def _small_allreduce(vec):
    r, cdim = vec.shape
    n_dev = 8

    def body(v_ref, out_ref, buf, send, recv):
        x, y, c, _ = _place()
        me = 4 * x + 2 * y + c
        buf[me] = v_ref[...]
        cps = []
        for k in range(1, n_dev):
            dx, dy, dc = (k >> 2) & 1, (k >> 1) & 1, k & 1
            peer = (x ^ dx, y ^ dy, c ^ dc)
            cp = pltpu.make_async_remote_copy(
                src_ref=v_ref, dst_ref=buf.at[me], send_sem=send.at[k - 1], recv_sem=recv.at[k - 1],
                device_id=peer, device_id_type=MESH)
            cp.start()
            cps.append(cp)
        for k in range(1, n_dev):
            dx, dy, dc = (k >> 2) & 1, (k >> 1) & 1, k & 1
            src = 4 * (x ^ dx) + 2 * (y ^ dy) + (c ^ dc)
            slot = buf.at[src]
            pltpu.make_async_remote_copy(
                src_ref=slot, dst_ref=slot, send_sem=send.at[k - 1], recv_sem=recv.at[k - 1],
                device_id=(x ^ dx, y ^ dy, c ^ dc), device_id_type=MESH).wait_recv()
        for cp in cps:
            cp.wait_send()
        acc = buf[0]
        for k in range(1, n_dev):
            acc = acc + buf[k]
        out_ref[...] = acc

    vm = pl.BlockSpec(memory_space=pltpu.VMEM)
    return pl.pallas_call(
        body, name="small_allreduce", in_specs=[vm], out_specs=vm,
        out_shape=jax.ShapeDtypeStruct((r, cdim), F32),
        scratch_shapes=[pltpu.VMEM((n_dev, r, cdim), F32), pltpu.SemaphoreType.DMA((n_dev - 1,)),
                        pltpu.SemaphoreType.DMA((n_dev - 1,))],
    )(vec)


def _a_cols_to_head_major(w):
    lead = w.shape[:-1]
    q = w[..., :A_QK].reshape(lead + (A_HEADS, A_DK))
    k = w[..., A_QK:2 * A_QK].reshape(lead + (A_HEADS, A_DK))
    v = w[..., 2 * A_QK:2 * A_QK + A_VW].reshape(lead + (A_HEADS, A_DV))
    z = w[..., 2 * A_QK + A_VW:].reshape(lead + (A_HEADS, A_DV))
    return jnp.concatenate([q, k, v, z], axis=-1).reshape(lead + (A_HEADS * A_HEAD_COLS,))


def _a_cols_from_head_major(w):
    lead = w.shape[:-1]
    w = w.reshape(lead + (A_HEADS, A_HEAD_COLS))
    parts = [w[..., :A_DK], w[..., A_DK:2 * A_DK], w[..., 2 * A_DK:2 * A_DK + A_DV], w[..., 2 * A_DK + A_DV:]]
    return jnp.concatenate([p.reshape(lead + (-1,)) for p in parts], axis=-1)


def _conv_cols_to_head_major(w):
    lead = w.shape[:-1]
    q = w[..., :A_QK].reshape(lead + (A_HEADS, A_DK))
    k = w[..., A_QK:2 * A_QK].reshape(lead + (A_HEADS, A_DK))
    v = w[..., 2 * A_QK:].reshape(lead + (A_HEADS, A_DV))
    return jnp.concatenate([q, k, v], axis=-1).reshape(lead + (A_HEADS * A_CONV_COLS,))


def _conv_cols_from_head_major(w):
    lead = w.shape[:-1]
    w = w.reshape(lead + (A_HEADS, A_CONV_COLS))
    parts = [w[..., :A_DK], w[..., A_DK:2 * A_DK], w[..., 2 * A_DK:]]
    return jnp.concatenate([p.reshape(lead + (-1,)) for p in parts], axis=-1)


def _to_stream(a, bn, d):
    rest = a.shape[1:]
    s = a.shape[0] // bn
    a = a.reshape((bn, s // d, d) + rest)
    a = jnp.swapaxes(a, 1, 2)
    return a.reshape((bn * d, s // d) + rest)


def _from_stream(a, bn, d):
    rest = a.shape[2:]
    ln = a.shape[1]
    a = a.reshape((bn, d, ln) + rest)
    a = jnp.swapaxes(a, 1, 2)
    return a.reshape((bn * ln * d,) + rest)


B_SUB = 512
B_SHARD_BLOCKS = (3 * B_GROUPS * B_W + B_W) // N_CHIPS // B_SUB


def _b_block(gi, jj):
    nb = (B_GROUPS * (jj // 2) + gi) * 2 + jj % 2
    return nb // B_SHARD_BLOCKS, nb % B_SHARD_BLOCKS


def _shard_major(g, ncols):
    r = g.shape[0]
    return jnp.swapaxes(g.reshape(r, N_CHIPS, ncols), 0, 1)


def _pack_rows(items):
    rows, offs = [], []
    at = 0
    for a in items:
        flat = a.reshape(-1).astype(F32)
        nr = -(-flat.shape[0] // 1024) * 8
        flat = jnp.pad(flat, (0, nr * 128 - flat.shape[0]))
        rows.append(flat.reshape(nr, 128))
        offs.append((at, nr, a.shape))
        at += nr
    return jnp.concatenate(rows, axis=0), offs


def _unpack_rows(packed, offs):
    out = []
    for at, nr, shape in offs:
        size = int(np.prod(shape)) if len(shape) else 1
        out.append(packed[at:at + nr].reshape(-1)[:size].reshape(shape))
    return out


def _local_step(x, positions, loss_target, norm_g, a_log, a_dt_bias, a_norm_g, b_q_norm_g, b_k_norm_g,
                start_token, first_weights, late_weights, b_grads_ready, a_grads_ready):
    bn, s, d = x.shape
    t = bn * s
    n_chunks = s // A_CHUNK
    x0 = x.reshape(t, d)
    h0 = _rms_fwd(x0, norm_g[0:1] + start_token, "rms0_fwd")
    inv_freq = ROPE_THETA ** (-jnp.arange(0, ROPE_DIMS, 2, dtype=F32) / ROPE_DIMS)
    freq_row = jnp.concatenate([inv_freq, inv_freq, jnp.zeros((128 - ROPE_DIMS,), F32)]).reshape(1, 128)
    posf = jnp.broadcast_to(positions.astype(F32).reshape(t, 1), (t, 128)) + start_token
    tabs = _rope_tables(posf, freq_row)
    tabs_s = [tabs if dil == 1 else [_to_stream(tb, bn, dil).reshape(t, 128) for tb in tabs] for dil in B_DIL]
    wa_in, conv_w, late_token = first_weights([h0] + [tb for ts in tabs_s for tb in ts])
    wa_main = _a_cols_to_head_major(wa_in[:, :A_MAIN])
    wa_tail = jnp.pad(wa_in[:, A_MAIN:], ((0, 0), (0, 128 - 2 * A_HEADS))) + late_token.astype(BF16)
    cw_hm = _conv_cols_to_head_major(conv_w)

    proj_a = _matmul(h0, wa_main, "nn", F32, "a_in_main")
    tail_a = _matmul(h0, wa_tail, "nn", F32, "a_in_tail")
    tail_t = jnp.swapaxes(tail_a[:, :2 * A_HEADS].reshape(bn, s, 2 * A_HEADS), 1, 2)
    tail_t = tail_t.reshape(bn, 2 * A_HEADS, n_chunks, A_CHUNK)
    beta, gc = _gdn_prep(tail_t, a_log[0], a_dt_bias[0])
    proj_a3 = proj_a.reshape(bn, s, A_MAIN)
    og_a, oraw_a, states, t_mats, conv_y = _gdn_fwd(proj_a3, cw_hm, beta, gc, a_norm_g)
    wa_out, wb_in, wb_out = late_weights(og_a)
    b_cols = [4 * B_W] + [3 * B_W] * (B_GROUPS - 1)
    x1 = _matmul(og_a.reshape(t, A_VW), wa_out, "nn", F32, "a_out", res=x0, tk=2048)

    h1 = _rms_fwd(x1, norm_g[1:2], "rms1_fwd")
    h1_s, proj_b, qkv_b, o_b, lse_b = [], [], [], [], []
    for gi, dil in enumerate(B_DIL):
        hs = h1 if dil == 1 else _to_stream(h1, bn, dil).reshape(t, d)
        ts = tabs_s[gi]
        pj = _matmul(hs, wb_in, "nn", F32, f"b_in_g{gi}", tm=2048, tn=B_SUB, n=b_cols[gi], b_spec=pl.BlockSpec(
            (None, d, B_SUB), lambda i, j, kk, gi=gi: (_b_block(gi, j)[0], kk, _b_block(gi, j)[1])))
        qkv = _qk_prep(pj, *ts, b_q_norm_g[0, gi:gi + 1], b_k_norm_g[0, gi:gi + 1], f"qk_prep_g{gi}")
        o_s, lse_s = _attn_fwd(qkv.reshape(bn * dil, s // dil, 3 * B_W), f"attn_fwd_g{gi}")
        h1_s.append(hs), proj_b.append(pj), qkv_b.append(qkv)
        o_b.append(o_s.reshape(t, B_W) if dil == 1 else _from_stream(o_s, bn, dil))
        lse_b.append(lse_s.reshape(t, B_HEADS) if dil == 1 else _from_stream(lse_s, bn, dil))
    og_b = _merge_fwd(o_b, lse_b, proj_b[0])
    x2 = _matmul(og_b, wb_out, "nn", F32, "b_out", res=x1)

    d_x2, loss_parts = _loss_grad(x2, loss_target.reshape(t, d))
    loss_local = jnp.sum(loss_parts)

    d_x2b = d_x2.astype(BF16)
    g_wb_out = _matmul(og_b, d_x2b, "tn", F32, "b_out_dw")
    d_og_b = _matmul(d_x2b, wb_out, "nt", F32, "b_out_dx")
    d_o, lse_joint, delta, d_z = _merge_bwd(o_b, lse_b, proj_b[0], d_og_b)
    d_h1, g_qn, g_kn = [], [], []
    g_wb_in = lax.empty(wb_in.shape, F32)
    for gi, dil in enumerate(B_DIL):
        if dil == 1:
            do_s, lj_s, dl_s = d_o, lse_joint, delta
        else:
            do_s, lj_s, dl_s = (_to_stream(a, bn, dil).reshape(t, -1) for a in (d_o, lse_joint, delta))
        ns, ln = bn * dil, s // dil
        dq, dk, dv = _attn_bwd(qkv_b[gi].reshape(ns, ln, 3 * B_W), do_s.reshape(ns, ln, B_W),
                               lj_s.reshape(ns, ln, B_HEADS), dl_s.reshape(ns, ln, B_HEADS), f"attn_bwd_g{gi}")
        d_pj, d_gain = _qk_prep_bwd(proj_b[gi], *tabs_s[gi], b_q_norm_g[0, gi:gi + 1], b_k_norm_g[0, gi:gi + 1],
                                    dq.reshape(t, B_W), dk.reshape(t, B_W), dv.reshape(t, B_W),
                                    d_z if gi == 0 else None, f"qk_prep_bwd_g{gi}")
        g_wb_in = _matmul(h1_s[gi], d_pj, "tn", F32, f"b_in_dw_g{gi}", tn=B_SUB, tk=2048, into=(g_wb_in, pl.BlockSpec(
            (None, d, B_SUB), lambda i, j, kk, gi=gi: (_b_block(gi, j)[0], i, _b_block(gi, j)[1]))))
        dh = _matmul(d_pj, wb_in, "nt", F32, f"b_in_dx_g{gi}", tm=2048, tk=B_SUB, n=d, b_spec=pl.BlockSpec(
            (None, d, B_SUB), lambda i, j, kk, gi=gi: (_b_block(gi, kk)[0], j, _b_block(gi, kk)[1])))
        d_h1.append(dh if dil == 1 else _from_stream(dh.reshape(ns, ln, d), bn, dil))
        g_qn.append(d_gain[0]), g_kn.append(d_gain[1])
    d_x1, g_norm1 = _rms_bwd(x1, norm_g[1:2], d_h1, d_x2, "rms1_bwd")

    d_x1b = d_x1.astype(BF16)
    g_wa_out = _matmul(og_a.reshape(t, A_VW), d_x1b, "tn", F32, "a_out_dw")
    b_token = b_grads_ready(g_wb_in, g_wb_out, g_wa_out)
    d_og_a = _matmul(d_x1b, wa_out, "nt", F32, "a_out_dx")
    d_pa, d_gc, d_beta, d_cw, d_ng = _gdn_bwd(proj_a3, cw_hm, beta, gc, a_norm_g + b_token, oraw_a, states,
                                              t_mats, conv_y, d_og_a.reshape(bn, s, A_VW))
    d_tail_t, d_alog, d_dtb = _gdn_prep_bwd(tail_t, a_log[0], a_dt_bias[0], d_gc, d_beta)
    d_tail = jnp.swapaxes(d_tail_t.reshape(bn, 2 * A_HEADS, s), 1, 2).reshape(t, 2 * A_HEADS)
    d_tail = jnp.pad(d_tail, ((0, 0), (0, 128 - 2 * A_HEADS))).astype(BF16)
    d_pa = d_pa.reshape(t, A_MAIN)
    g_wa_main = _matmul(h0, d_pa, "tn", F32, "a_in_dw_main")
    g_wa_tail = _matmul(h0, d_tail, "tn", F32, "a_in_dw_tail")
    g_wa_in = jnp.concatenate([_a_cols_from_head_major(g_wa_main), g_wa_tail[:, :2 * A_HEADS]], axis=1)
    a_token = a_grads_ready(g_wa_in)
    d_h0 = _matmul(d_pa, wa_main, "nt", F32, "a_in_dx_main")
    d_h0t = _matmul(d_tail + a_token.astype(BF16), wa_tail, "nt", F32, "a_in_dx_tail")
    d_x0, g_norm0 = _rms_bwd(x0, norm_g[0:1], [d_h0, d_h0t], d_x1, "rms0_bwd")

    gfull = {
        "norm_g": jnp.concatenate([g_norm0, g_norm1], axis=0), "a_w_in": g_wa_in,
        "a_conv_w": _conv_cols_from_head_major(jnp.sum(d_cw, axis=0)),
        "a_log": jnp.sum(d_alog[:, :, 0], axis=0), "a_dt_bias": jnp.sum(d_dtb[:, :, 0], axis=0),
        "a_norm_g": jnp.sum(d_ng[:, :, 0, :], axis=(0, 1)), "a_w_out": g_wa_out, "b_w_in": g_wb_in,
        "b_q_norm_g": jnp.stack(g_qn), "b_k_norm_g": jnp.stack(g_kn), "b_w_out": g_wb_out}
    return loss_local, d_x0.reshape(bn, s, d), gfull


def kernel(x, positions, norm_g, a_w_in, a_conv_w, a_log, a_dt_bias, a_norm_g, a_w_out, b_w_in, b_q_norm_g, b_k_norm_g, b_w_out, loss_target, m_norm_g, m_a_w_in, m_a_conv_w, m_a_log, m_a_dt_bias, m_a_norm_g, m_a_w_out, m_b_w_in, m_b_q_norm_g, m_b_k_norm_g, m_b_w_out, v_norm_g, v_a_w_in, v_a_conv_w, v_a_log, v_a_dt_bias, v_a_norm_g, v_a_w_out, v_b_w_in, v_b_q_norm_g, v_b_k_norm_g, v_b_w_out):
    d = x.shape[2]
    my_c = lax.axis_index("c")
    my_chip = 2 * lax.axis_index("x") + lax.axis_index("y")

    half_index = jnp.reshape(my_c, (1,)).astype(jnp.int32)
    chip_index = jnp.reshape(my_chip, (1,)).astype(jnp.int32)
    def landing(shard):
        return lax.dynamic_update_slice(lax.empty((N_CHIPS,) + shard.shape, shard.dtype), shard[None],
                                        (my_chip,) + (0,) * shard.ndim)

    first_shards = [a_w_in[0].astype(BF16), a_conv_w[0]]
    first_plan = _first_gather_plan()
    first = _split_copy_start("first_weights_start", first_plan, first_shards,
                              [landing(s) for s in first_shards], half_index)
    pending = {}
    late_shards = [(w[0] + first[4][0, 0]).astype(BF16) for w in (a_w_out, b_w_in, b_w_out)]
    late_lands = [landing(s) for s in late_shards]

    def first_weights(after):
        _, (ga_in, g_conv) = _split_copy_wait("first_weights_wait", first_plan, *first[:4],
                                              list(after) + late_lands)
        ga_in = _sibling_forward(ga_in)
        wa_in = jnp.concatenate([ga_in[k] for k in range(N_CHIPS)], axis=1)
        conv_w = jnp.concatenate([g_conv[k] for k in range(N_CHIPS)], axis=1)
        plan = _gather_plan(len(late_shards))
        pending["late"] = (plan,) + tuple(_split_copy_start(
            "late_weights_start", plan, late_shards, late_lands, conv_w))
        return wa_in, conv_w, pending["late"][5][0, 0]

    def late_weights(after):
        plan, send, recv, srcs, lands, _ = pending["late"]
        _, (ga_out, gb_in, gb_out) = _split_copy_wait("late_weights_wait", plan, send, recv, srcs, lands, after)
        return ga_out.reshape(A_VW, d), gb_in, gb_out.reshape(B_W, d)

    def reduce_to_chip_sums(mats, tag):
        recv_sib = _sibling_swap_halves([g.astype(BF16) for g in mats], f"grad_{tag}_sibling_swap")
        return [_pair_sum(g, r, half_index, f"grad_{tag}_pair_sum_{i}") for i, (g, r) in enumerate(zip(mats, recv_sib))]

    def start_exchange(tag, mats):
        sums = reduce_to_chip_sums(mats, tag)
        lands = [lax.empty((N_CHIPS - 1,) + s.shape[1:], BF16) for s in sums]
        plan = _exchange_plan(len(mats))
        pending[tag] = (plan,) + tuple(_split_copy_start(f"grad_{tag}_exchange_start", plan, sums, lands, chip_index))
        return pending[tag][5][0, 0]

    def finish_exchange(tag, after):
        plan, send, recv, srcs, lands, _ = pending[tag]
        return _split_copy_wait(f"grad_{tag}_exchange_wait", plan, send, recv, srcs, lands, after)

    def b_grads_ready(g_wb_in, g_wb_out, g_wa_out):
        return start_exchange("b", [g_wb_in, g_wb_out.reshape(N_CHIPS, -1, d), g_wa_out.reshape(N_CHIPS, -1, d)])

    def a_grads_ready(g_wa_in):
        return start_exchange("a", [_shard_major(g_wa_in, a_w_in.shape[2])])

    loss_local, d_x0, gfull = _local_step(x, positions, loss_target, norm_g, a_log, a_dt_bias, a_norm_g,
                                          b_q_norm_g, b_k_norm_g, first[4][0, 0], first_weights, late_weights,
                                          b_grads_ready, a_grads_ready)

    small = [gfull["norm_g"], gfull["a_conv_w"], gfull["a_log"], gfull["a_dt_bias"], gfull["a_norm_g"],
             gfull["b_q_norm_g"], gfull["b_k_norm_g"], loss_local]
    packed, offs = _pack_rows(small)
    reduced = _small_allreduce(packed)
    g_norm, g_conv_all, g_alog, g_dtb, g_ang, g_q, g_k, loss = _unpack_rows(reduced, offs)
    g_conv_mine = lax.dynamic_slice_in_dim(g_conv_all, my_chip * a_conv_w.shape[2], a_conv_w.shape[2], axis=1)

    b_sums, b_received = finish_exchange("b", d_x0)
    a_sums, a_received = finish_exchange("a", reduced)
    chip_sums = [a_sums[0], b_sums[2], b_sums[0], b_sums[1]]
    received = [a_received[0], b_received[2], b_received[0], b_received[1]]
    halves = [_chip_sum(s, r, chip_index, f"grad_chip_sum_{i}") for i, (s, r) in enumerate(zip(chip_sums, received))]
    theirs = _sibling_swap_whole(halves)
    big = ("a_w_in", "a_w_out", "b_w_in", "b_w_out")
    big_halves = dict(zip(big, zip(halves, theirs)))

    grads = {
        "norm_g": g_norm, "a_conv_w": g_conv_mine[None], "a_log": g_alog[None], "a_dt_bias": g_dtb[None],
        "a_norm_g": g_ang[None], "b_q_norm_g": g_q[None], "b_k_norm_g": g_k[None]}
    weights = {"norm_g": norm_g, "a_w_in": a_w_in, "a_conv_w": a_conv_w, "a_log": a_log, "a_dt_bias": a_dt_bias,
               "a_norm_g": a_norm_g, "a_w_out": a_w_out, "b_w_in": b_w_in, "b_q_norm_g": b_q_norm_g,
               "b_k_norm_g": b_k_norm_g, "b_w_out": b_w_out}
    m_in = {"norm_g": m_norm_g, "a_w_in": m_a_w_in, "a_conv_w": m_a_conv_w, "a_log": m_a_log,
            "a_dt_bias": m_a_dt_bias, "a_norm_g": m_a_norm_g, "a_w_out": m_a_w_out, "b_w_in": m_b_w_in,
            "b_q_norm_g": m_b_q_norm_g, "b_k_norm_g": m_b_k_norm_g, "b_w_out": m_b_w_out}
    v_in = {"norm_g": v_norm_g, "a_w_in": v_a_w_in, "a_conv_w": v_a_conv_w, "a_log": v_a_log,
            "a_dt_bias": v_a_dt_bias, "a_norm_g": v_a_norm_g, "a_w_out": v_a_w_out, "b_w_in": v_b_w_in,
            "b_q_norm_g": v_b_q_norm_g, "b_k_norm_g": v_b_k_norm_g, "b_w_out": v_b_w_out}
    names = list(weights)

    delta_w, new_m, new_v = {}, {}, {}
    for nm in big:
        mine, other = big_halves[nm]
        if weights[nm].shape[2] % 128:
            cols = lambda a: jnp.transpose(a, (2, 0, 1))
            half_cols = lambda a: jnp.transpose(a)[:, None, :]
            outs = _adamw_shard_cols(cols(weights[nm]), half_cols(mine), half_cols(other), cols(m_in[nm]),
                                     cols(v_in[nm]), half_index, f"adamw_{nm}")
            outs = [jnp.transpose(o, (1, 2, 0)) for o in outs]
        else:
            outs = _adamw_shard(weights[nm], mine, other, m_in[nm], v_in[nm], half_index, f"adamw_{nm}")
        grads[nm], delta_w[nm], new_m[nm], new_v[nm] = outs
    small_names = [nm for nm in names if nm not in big]
    packs = [_pack_rows([src[nm] for nm in small_names]) for src in (weights, grads, m_in, v_in)]
    offs = packs[0][1]
    dl, m2, v2 = _adamw(packs[0][0], packs[1][0], packs[2][0], packs[3][0], "adamw_small")
    for nm, a, b, c2 in zip(small_names, _unpack_rows(dl, offs), _unpack_rows(m2, offs), _unpack_rows(v2, offs)):
        delta_w[nm], new_m[nm], new_v[nm] = a, b, c2

    return (loss, d_x0, *[grads[nm] for nm in names], *[delta_w[nm] for nm in names],
            *[new_m[nm] for nm in names], *[new_v[nm] for nm in names])
```

```python
import jax
import jax.numpy as jnp
import numpy as np
from jax import lax
from jax.experimental import pallas as pl
from jax.experimental.pallas import tpu as pltpu

F32 = jnp.float32
BF16 = jnp.bfloat16
MESH = pl.DeviceIdType.MESH

EPS = 1e-6
A_HEADS = 8
A_DK = 128
A_DV = 256
A_QK = A_HEADS * A_DK
A_VW = A_HEADS * A_DV
A_MAIN = 2 * A_QK + 2 * A_VW
A_HEAD_COLS = 2 * A_DK + 2 * A_DV
A_CONV_COLS = 2 * A_DK + A_DV
A_CHUNK = 64
A_CONV = 4
B_GROUPS = 3
B_HEADS = 8
B_DH = 128
B_W = B_HEADS * B_DH
B_DIL = (1, 4, 16)
B_BLK = 128
ROPE_THETA = 500000.0
ROPE_DIMS = B_DH // 4
ADAM_LR, ADAM_B1, ADAM_B2, ADAM_EPS, ADAM_WD, ADAM_STEP = 0.001, 0.9, 0.999, 1e-08, 0.01, 10
N_CHIPS = 4
VMEM_BIG = 56 * 1024 * 1024


def _params(sem=None, vmem=None):
    return pltpu.CompilerParams(dimension_semantics=sem, vmem_limit_bytes=vmem)


def _dot(a, b, ca, cb):
    return lax.dot_general(a.astype(BF16), b.astype(BF16), (((ca,), (cb,)), ((), ())),
                           preferred_element_type=F32)


def _split3(a):
    hi = a.astype(BF16)
    r = a - hi.astype(F32)
    mid = r.astype(BF16)
    lo = (r - mid.astype(F32)).astype(BF16)
    return hi, mid, lo


def _sigmoid(y):
    return 1.0 / (1.0 + jnp.exp(-y))


def _silu(y):
    return y * _sigmoid(y)


def _dsilu(y):
    s = _sigmoid(y)
    return s * (1.0 + y * (1.0 - s))


def _matmul(a, b, mode, out_dtype, name, res=None, tm=1024, tn=1024, tk=1024, n=None, b_spec=None, into=None):
    m, k = a.shape[::-1] if mode == "tn" else a.shape
    if n is None:
        n = b.shape[0] if mode == "nt" else b.shape[1]
    tm, tn, tk = min(tm, m), min(tn, n), min(tk, k)
    assert m % tm == 0 and n % tn == 0 and k % tk == 0, (name, a.shape, b.shape)
    nk = k // tk
    dims = {"nn": ((1,), (0,)), "nt": ((1,), (1,)), "tn": ((0,), (0,))}[mode]

    def body(*refs):
        a_ref, b_ref = refs[0], refs[1]
        r_ref = refs[2] if res is not None else None
        o_ref = refs[2 + (res is not None) + (into is not None)]
        prod = lax.dot_general(a_ref[...], b_ref[...], (dims, ((), ())), preferred_element_type=F32)

        def finish(r):
            if res is not None:
                r = r + r_ref[...]
            o_ref[...] = r.astype(out_dtype)

        if nk == 1:
            finish(prod)
            return
        acc = refs[-1]
        kk = pl.program_id(2)

        @pl.when(kk == 0)
        def _():
            acc[...] = prod

        @pl.when((kk > 0) & (kk < nk - 1))
        def _():
            acc[...] += prod

        @pl.when(kk == nk - 1)
        def _():
            finish(acc[...] + prod)

    a_spec = pl.BlockSpec((tm, tk), lambda i, j, kk: (i, kk))
    if mode == "tn":
        a_spec = pl.BlockSpec((tk, tm), lambda i, j, kk: (kk, i))
    if b_spec is None and mode == "nt":
        b_spec = pl.BlockSpec((tn, tk), lambda i, j, kk: (j, kk))
    elif b_spec is None:
        b_spec = pl.BlockSpec((tk, tn), lambda i, j, kk: (kk, j))
    in_specs = [a_spec, b_spec]
    args = [a, b]
    if res is not None:
        in_specs.append(pl.BlockSpec((tm, tn), lambda i, j, kk: (i, j)))
        args.append(res)
    out_spec = pl.BlockSpec((tm, tn), lambda i, j, kk: (i, j))
    out_shape = jax.ShapeDtypeStruct((m, n), out_dtype)
    aliases = {}
    if into is not None:
        assert res is None
        buf, out_spec = into
        out_shape = jax.ShapeDtypeStruct(buf.shape, buf.dtype)
        in_specs.append(ANY)
        args.append(buf)
        aliases = {2: 0}
    return pl.pallas_call(
        body, name=name, grid=(m // tm, n // tn, nk),
        in_specs=in_specs, out_specs=out_spec, out_shape=out_shape, input_output_aliases=aliases,
        scratch_shapes=[pltpu.VMEM((tm, tn), F32)] if nk > 1 else [],
        compiler_params=_params(("parallel", "parallel", "arbitrary"), 48 * 1024 * 1024),
    )(*args)


def _rms_fwd(x, g, name, tm=256):
    t, d = x.shape

    def body(x_ref, g_ref, h_ref):
        xv = x_ref[...]
        r = lax.rsqrt(jnp.mean(xv * xv, axis=-1, keepdims=True) + EPS)
        h_ref[...] = (xv * r * g_ref[...]).astype(BF16)

    return pl.pallas_call(
        body, name=name, grid=(t // tm,),
        in_specs=[pl.BlockSpec((tm, d), lambda i: (i, 0)), pl.BlockSpec((1, d), lambda i: (0, 0))],
        out_specs=pl.BlockSpec((tm, d), lambda i: (i, 0)),
        out_shape=jax.ShapeDtypeStruct((t, d), BF16),
        compiler_params=_params(("parallel",)),
    )(x, g)


def _rms_bwd(x, g, dhs, dres, name, tm=256):
    t, d = x.shape
    n_dh = len(dhs)

    def body(*refs):
        x_ref, g_ref = refs[0], refs[1]
        dh_refs = refs[2:2 + n_dh]
        dres_ref, dx_ref, dg_ref = refs[2 + n_dh:]
        i = pl.program_id(0)

        @pl.when(i == 0)
        def _():
            dg_ref[...] = jnp.zeros_like(dg_ref)

        xv = x_ref[...]
        r = lax.rsqrt(jnp.mean(xv * xv, axis=-1, keepdims=True) + EPS)
        xh = xv * r
        dh = dh_refs[0][...]
        for ref in dh_refs[1:]:
            dh = dh + ref[...]
        dg_ref[0:1, :] += jnp.sum(dh * xh, axis=0, keepdims=True)
        dxh = dh * g_ref[...]
        dx = r * (dxh - xh * jnp.mean(dxh * xh, axis=-1, keepdims=True))
        dx_ref[...] = dx + dres_ref[...]

    row = pl.BlockSpec((tm, d), lambda i: (i, 0))
    dx, dg = pl.pallas_call(
        body, name=name, grid=(t // tm,),
        in_specs=[row, pl.BlockSpec((1, d), lambda i: (0, 0))] + [row] * n_dh + [row],
        out_specs=[row, pl.BlockSpec((8, d), lambda i: (0, 0))],
        out_shape=[jax.ShapeDtypeStruct((t, d), F32), jax.ShapeDtypeStruct((8, d), F32)],
        compiler_params=_params(("arbitrary",)),
    )(x, g, *dhs, dres)
    return dx, dg[0:1]


def _loss_grad(y, target, name="loss_grad", tm=256):
    t, d = y.shape
    nb = t // tm

    def body(y_ref, t_ref, dy_ref, part_ref):
        e = y_ref[...] - t_ref[...]
        dy_ref[...] = e * (1.0 / d)
        s = jnp.sum(jnp.sum(e * e, axis=1, keepdims=True), axis=0, keepdims=True) * (0.5 / d)
        part_ref[...] = jnp.broadcast_to(s, (8, 128))

    row = pl.BlockSpec((tm, d), lambda i: (i, 0))
    dy, part = pl.pallas_call(
        body, name=name, grid=(nb,), in_specs=[row, row],
        out_specs=[row, pl.BlockSpec((None, 8, 128), lambda i: (i, 0, 0))],
        out_shape=[jax.ShapeDtypeStruct((t, d), F32), jax.ShapeDtypeStruct((nb, 8, 128), F32)],
        compiler_params=_params(("parallel",)),
    )(y, target)
    return dy, part[:, 0, 0]


def _softplus(x):
    t = jnp.exp(-jnp.abs(x))
    return jnp.maximum(x, 0.0) + jnp.where(t < 1e-3, t * (1.0 - 0.5 * t), jnp.log(1.0 + t))


def _tri(rows_le_cols):
    r = lax.broadcasted_iota(jnp.int32, (A_CHUNK, A_CHUNK), 0)
    c = lax.broadcasted_iota(jnp.int32, (A_CHUNK, A_CHUNK), 1)
    return jnp.where((r <= c) if rows_le_cols else (r >= c), 1.0, 0.0).astype(BF16)


def _dot_exact_rhs(a, ones_bf16):
    dn = (((1,), (0,)), ((), ()))
    hi, mid, lo = _split3(a)
    out = lax.dot_general(hi, ones_bf16, dn, preferred_element_type=F32)
    out = out + lax.dot_general(mid, ones_bf16, dn, preferred_element_type=F32)
    return out + lax.dot_general(lo, ones_bf16, dn, preferred_element_type=F32)


def _gdn_prep(tail_t, a_log, dt_bias):
    bn, _, n, c = tail_t.shape

    def body(t_ref, alog_ref, dtb_ref, beta_ref, gc_ref):
        upper = _tri(True)
        for h in range(A_HEADS):
            beta_ref[h] = _sigmoid(t_ref[h])
            ea = jnp.exp(jnp.full((n, c), alog_ref[h], F32))
            g = -ea * _softplus(t_ref[A_HEADS + h] + dtb_ref[h])
            gc_ref[h] = _dot_exact_rhs(g, upper)

    smem = pl.BlockSpec(memory_space=pltpu.SMEM)
    blk = pl.BlockSpec((None, A_HEADS, n, c), lambda b: (b, 0, 0, 0))
    return pl.pallas_call(
        body, name="gdn_prep", grid=(bn,),
        in_specs=[pl.BlockSpec((None, 2 * A_HEADS, n, c), lambda b: (b, 0, 0, 0)), smem, smem],
        out_specs=[blk, blk],
        out_shape=[jax.ShapeDtypeStruct((bn, A_HEADS, n, c), F32)] * 2,
        compiler_params=_params(("parallel",)),
    )(tail_t, a_log, dt_bias)


def _gdn_prep_bwd(tail_t, a_log, dt_bias, d_gc, d_beta):
    bn, _, n, c = tail_t.shape

    def body(t_ref, alog_ref, dtb_ref, dgc_ref, dbeta_ref, dt_ref, dal_ref, ddt_ref):
        lower = _tri(False)
        for h in range(A_HEADS):
            beta = _sigmoid(t_ref[h])
            dt_ref[h] = dbeta_ref[h] * beta * (1.0 - beta)
            dg = _dot_exact_rhs(dgc_ref[h], lower)
            ea = jnp.exp(jnp.full((n, c), alog_ref[h], F32))
            xa = t_ref[A_HEADS + h] + dtb_ref[h]
            g = -ea * _softplus(xa)
            dxa = -ea * dg * _sigmoid(xa)
            dt_ref[A_HEADS + h] = dxa
            s1 = jnp.sum(jnp.sum(g * dg, axis=1, keepdims=True), axis=0, keepdims=True)
            s2 = jnp.sum(jnp.sum(dxa, axis=1, keepdims=True), axis=0, keepdims=True)
            dal_ref[h:h + 1, :] = jnp.broadcast_to(s1, (1, 128))
            ddt_ref[h:h + 1, :] = jnp.broadcast_to(s2, (1, 128))

    smem = pl.BlockSpec(memory_space=pltpu.SMEM)
    blk8 = pl.BlockSpec((None, A_HEADS, n, c), lambda b: (b, 0, 0, 0))
    blk16 = pl.BlockSpec((None, 2 * A_HEADS, n, c), lambda b: (b, 0, 0, 0))
    sm = pl.BlockSpec((None, A_HEADS, 128), lambda b: (b, 0, 0))
    return pl.pallas_call(
        body, name="gdn_prep_bwd", grid=(bn,),
        in_specs=[blk16, smem, smem, blk8, blk8],
        out_specs=[blk16, sm, sm],
        out_shape=[jax.ShapeDtypeStruct((bn, 2 * A_HEADS, n, c), F32),
                   jax.ShapeDtypeStruct((bn, A_HEADS, 128), F32),
                   jax.ShapeDtypeStruct((bn, A_HEADS, 128), F32)],
        compiler_params=_params(("parallel",)),
    )(tail_t, a_log, dt_bias, d_gc, d_beta)


HALO = 8


def _conv_taps(xw, w):
    y = w[A_CONV - 1:A_CONV, :] * xw
    for j in range(1, A_CONV):
        y = y + w[A_CONV - 1 - j:A_CONV - j, :] * pltpu.roll(xw, j, 0)
    return y[HALO:, :]


def _row_to_col(row, eye):
    c = eye.shape[0]
    return jnp.sum(jnp.where(eye, jnp.broadcast_to(row, (c, c)), 0.0), axis=1, keepdims=True)


def _col_to_row(col, eye):
    c = eye.shape[0]
    return jnp.sum(jnp.where(eye, jnp.broadcast_to(col, (c, c)), 0.0), axis=0, keepdims=True)


def _unit_lower_inverse(a, ri, ci):
    eye = jnp.where(ri == ci, 1.0, 0.0)
    a8 = jnp.where((ri >> 3) == (ci >> 3), a, 0.0)
    a2 = _dot(a8, a8, 1, 0)
    yield
    a4 = _dot(a2, a2, 1, 0)
    t = eye - a8
    t = t + _dot(t, a2, 1, 0)
    yield
    t = t + _dot(t, a4, 1, 0)
    yield
    for sh in (3, 4, 5):
        off = jnp.where(((ri >> (sh + 1)) == (ci >> (sh + 1))) & ((ri >> sh) != (ci >> sh)), a, 0.0)
        left = _dot(t, off, 1, 0)
        yield
        t = t - _dot(left, t, 1, 0)
        yield
    return t


def _round_robin(gens):
    live = list(gens)
    while live:
        nxt = []
        for g in live:
            try:
                next(g)
                nxt.append(g)
            except StopIteration:
                pass
        live = nxt


def _gdn_chunk_core(q, k, v, g_row, b_row, t_mat, ri, ci):
    eye = ri == ci
    g_col = _row_to_col(g_row, eye)
    b_col = _row_to_col(b_row, eye)
    causal = ri >= ci
    strict = ri > ci
    dec = jnp.where(causal, jnp.exp(jnp.where(causal, g_col - g_row, 0.0)), 0.0)
    gam = jnp.exp(g_col)
    g_last = g_row[:, A_CHUNK - 1:A_CHUNK]
    gam_last = jnp.exp(g_last)
    e = jnp.exp(g_last - g_col)
    kb = k * b_col
    bv = v * b_col
    kbg = kb * gam
    q16, k16, kb16 = q.astype(BF16), k.astype(BF16), kb.astype(BF16)
    kk = _dot(kb16, k16, 1, 1)
    p = _dot(q16, k16, 1, 1) * dec
    yield
    a_mat = jnp.where(strict, kk * dec, 0.0)
    if t_mat is None:
        t_mat = yield from _unit_lower_inverse(a_mat, ri, ci)
    t16 = t_mat.astype(BF16)
    u = _dot(t16, bv, 1, 0)
    w = _dot(t16, kbg, 1, 0)
    yield
    return dict(eye=eye, g_col=g_col, b_col=b_col, dec=dec, strict=strict, causal=causal, gam=gam,
                gam_last=gam_last, e=e, kb=kb, bv=bv, kbg=kbg, a_mat=a_mat, t_mat=t_mat, u=u, w=w, p=p,
                qg=q * gam, kd=k * e, q16=q16, k16=k16, kb16=kb16, t16=t16)


A_SEQ_BLK = 256
A_BLK_CHUNKS = A_SEQ_BLK // A_CHUNK


def _gdn_halo(proj_hm):
    bn, s, w = proj_hm.shape
    last = proj_hm.reshape(bn, s // A_SEQ_BLK, A_SEQ_BLK, w)[:, :, A_SEQ_BLK - HALO:, :]
    return jnp.concatenate([jnp.zeros((bn, 1, HALO, w), proj_hm.dtype), last[:, :-1]], axis=1)


def _gdn_window(x_ref, halo_ref, ci, first, lo):
    if first:
        return jnp.concatenate([halo_ref[:, lo:lo + A_CONV_COLS], x_ref[0:A_CHUNK, lo:lo + A_CONV_COLS]], axis=0)
    start = pl.multiple_of(ci * A_CHUNK - HALO, HALO)
    return x_ref[pl.ds(start, A_CHUNK + HALO), lo:lo + A_CONV_COLS]


def _gdn_chunk_prep(xw, cw, y=None):
    if y is None:
        y = _conv_taps(xw, cw)
    a = _silu(y)
    aq, ak, v = a[:, 0:A_DK], a[:, A_DK:2 * A_DK], a[:, 2 * A_DK:]
    rq = lax.rsqrt(jnp.sum(aq * aq, axis=1, keepdims=True) + EPS)
    rk = lax.rsqrt(jnp.sum(ak * ak, axis=1, keepdims=True) + EPS)
    return dict(xw=xw, y=y, aq=aq, ak=ak, rq=rq, rk=rk, q=aq * rq * (A_DK ** -0.5), k=ak * rk, v=v)


def _gdn_fwd(proj_hm, cw_hm, beta, gc, norm_g, hp=8):
    bn, s, _ = proj_hm.shape
    n = s // A_CHUNK
    nsb = s // A_SEQ_BLK
    halo = _gdn_halo(proj_hm)

    def body(x_ref, halo_ref, cw_ref, beta_ref, gc_ref, ng_ref, og_ref, oraw_ref, st_ref, t_ref, y_ref, state):
        first_chunk = pl.program_id(2) * A_BLK_CHUNKS
        ri = lax.broadcasted_iota(jnp.int32, (A_CHUNK, A_CHUNK), 0)
        ci_ = lax.broadcasted_iota(jnp.int32, (A_CHUNK, A_CHUNK), 1)
        ng = ng_ref[...]

        @pl.when(pl.program_id(2) == 0)
        def _():
            state[...] = jnp.zeros_like(state)

        def one_head(hh, ci, first, rows):
            lo = hh * A_HEAD_COLS
            cw = cw_ref[:, hh * A_CONV_COLS:(hh + 1) * A_CONV_COLS]
            cin = _gdn_chunk_prep(_gdn_window(x_ref, halo_ref, ci, first, lo), cw)
            y_ref[rows, hh * A_CONV_COLS:(hh + 1) * A_CONV_COLS] = cin["y"]
            seq_chunk = pl.ds(first_chunk + ci, 1)
            core = yield from _gdn_chunk_core(cin["q"], cin["k"], cin["v"], gc_ref[hh, seq_chunk, :],
                                              beta_ref[hh, seq_chunk, :], None, ri, ci_)
            st = state[hh]
            st_ref[hh, ci] = st
            t_ref[hh, ci] = core["t_mat"]
            st16 = st.astype(BF16)
            vn = core["u"] - _dot(core["w"], st16, 1, 0)
            qs = _dot(core["qg"], st16, 1, 0)
            yield
            vn16 = vn.astype(BF16)
            o = qs + _dot(core["p"], vn16, 1, 0)
            state[hh] = st * core["gam_last"] + _dot(core["kd"], vn16, 0, 0)
            yield
            ocols = slice(hh * A_DV, (hh + 1) * A_DV)
            oraw_ref[rows, ocols] = o
            r = lax.rsqrt(jnp.mean(o * o, axis=1, keepdims=True) + EPS)
            z = x_ref[rows, lo + A_CONV_COLS:lo + A_HEAD_COLS]
            og_ref[rows, ocols] = (o * r * ng * _silu(z)).astype(BF16)

        def chunk(ci, first):
            rows = pl.ds(0 if first else pl.multiple_of(ci * A_CHUNK, A_CHUNK), A_CHUNK)
            _round_robin([one_head(hh, ci, first, rows) for hh in range(hp)])

        chunk(0, True)
        lax.fori_loop(1, A_BLK_CHUNKS, lambda i, c: (chunk(i, False), c)[1], 0)

    small = pl.BlockSpec((None, hp, n, A_CHUNK), lambda b, h, j: (b, h, 0, 0))
    return pl.pallas_call(
        body, name="gdn_fwd", grid=(bn, A_HEADS // hp, nsb),
        in_specs=[pl.BlockSpec((None, A_SEQ_BLK, hp * A_HEAD_COLS), lambda b, h, j: (b, j, h)),
                  pl.BlockSpec((None, None, HALO, hp * A_HEAD_COLS), lambda b, h, j: (b, j, 0, h)),
                  pl.BlockSpec((A_CONV, hp * A_CONV_COLS), lambda b, h, j: (0, h)),
                  small, small,
                  pl.BlockSpec((1, A_DV), lambda b, h, j: (0, 0))],
        out_specs=[pl.BlockSpec((None, A_SEQ_BLK, hp * A_DV), lambda b, h, j: (b, j, h)),
                   pl.BlockSpec((None, A_SEQ_BLK, hp * A_DV), lambda b, h, j: (b, j, h)),
                   pl.BlockSpec((None, hp, A_BLK_CHUNKS, A_DK, A_DV), lambda b, h, j: (b, h, j, 0, 0)),
                   pl.BlockSpec((None, hp, A_BLK_CHUNKS, A_CHUNK, A_CHUNK), lambda b, h, j: (b, h, j, 0, 0)),
                   pl.BlockSpec((None, A_SEQ_BLK, hp * A_CONV_COLS), lambda b, h, j: (b, j, h))],
        out_shape=[jax.ShapeDtypeStruct((bn, s, A_VW), BF16),
                   jax.ShapeDtypeStruct((bn, s, A_VW), F32),
                   jax.ShapeDtypeStruct((bn, A_HEADS, n, A_DK, A_DV), F32),
                   jax.ShapeDtypeStruct((bn, A_HEADS, n, A_CHUNK, A_CHUNK), F32),
                   jax.ShapeDtypeStruct((bn, s, A_HEADS * A_CONV_COLS), F32)],
        scratch_shapes=[pltpu.VMEM((hp, A_DK, A_DV), F32)],
        compiler_params=_params(("parallel", "parallel", "arbitrary"), VMEM_BIG),
    )(proj_hm, halo, cw_hm, beta, gc, norm_g)


def _gdn_bwd(proj_hm, cw_hm, beta, gc, norm_g, oraw, states, t_mats, conv_y, dog, hp=4):
    bn, s, _ = proj_hm.shape
    n = s // A_CHUNK
    nsb = s // A_SEQ_BLK
    halo = _gdn_halo(proj_hm)

    def body(x_ref, halo_ref, cw_ref, beta_ref, gc_ref, ng_ref, oraw_ref, st_ref, t_ref, y_ref, dog_ref,
             dx_ref, dgc_ref, dbeta_ref, dcw_ref, dng_ref, dstate, dy_next, shifted):
        first_chunk = (nsb - 1 - pl.program_id(2)) * A_BLK_CHUNKS
        ri = lax.broadcasted_iota(jnp.int32, (A_CHUNK, A_CHUNK), 0)
        ci_ = lax.broadcasted_iota(jnp.int32, (A_CHUNK, A_CHUNK), 1)
        lane = lax.broadcasted_iota(jnp.int32, (1, A_CHUNK), 1)
        ng = ng_ref[...]

        @pl.when(pl.program_id(2) == 0)
        def _():
            dstate[...] = jnp.zeros_like(dstate)
            dy_next[...] = jnp.zeros_like(dy_next)
            dcw_ref[...] = jnp.zeros_like(dcw_ref)
            dng_ref[...] = jnp.zeros_like(dng_ref)

        def one_head(hh, ci, first, rows):
            lo = hh * A_HEAD_COLS
            ccols = slice(hh * A_CONV_COLS, (hh + 1) * A_CONV_COLS)
            ocols = slice(hh * A_DV, (hh + 1) * A_DV)
            cw = cw_ref[:, ccols]
            cin = _gdn_chunk_prep(_gdn_window(x_ref, halo_ref, ci, first, lo), cw, y_ref[rows, ccols])
            q, k, v = cin["q"], cin["k"], cin["v"]
            seq_chunk = pl.ds(first_chunk + ci, 1)
            cr = yield from _gdn_chunk_core(q, k, v, gc_ref[hh, seq_chunk, :], beta_ref[hh, seq_chunk, :],
                                            t_ref[hh, ci], ri, ci_)
            eye, dec, gam, e = cr["eye"], cr["dec"], cr["gam"], cr["e"]
            b_col, t_mat, u, w, p = cr["b_col"], cr["t_mat"], cr["u"], cr["w"], cr["p"]
            st = st_ref[hh, ci]
            ds_out = dstate[hh]

            o = oraw_ref[rows, ocols]
            z = x_ref[rows, lo + A_CONV_COLS:lo + A_HEAD_COLS]
            d_og = dog_ref[rows, ocols]
            r = lax.rsqrt(jnp.mean(o * o, axis=1, keepdims=True) + EPS)
            oh = o * r
            d_on = d_og * _silu(z)
            dz = d_og * oh * ng * _dsilu(z)
            dng_ref[hh, 0:1, :] += jnp.sum(d_on * oh, axis=0, keepdims=True)
            d_oh = d_on * ng
            d_o = r * (d_oh - oh * jnp.mean(d_oh * oh, axis=1, keepdims=True))

            st16, ds16, do16, w16 = st.astype(BF16), ds_out.astype(BF16), d_o.astype(BF16), w.astype(BF16)
            q16, k16, t16 = cr["q16"], cr["k16"], cr["t16"]
            vn = u - _dot(w16, st16, 1, 0)
            d_vn = _dot(p, do16, 0, 0) + _dot(cr["kd"], ds16, 1, 0)
            d_qg = _dot(do16, st16, 1, 1)
            qgdo = _dot(cr["qg"], do16, 0, 0)
            yield
            vn16, dvn16 = vn.astype(BF16), d_vn.astype(BF16)
            d_p = jnp.where(cr["causal"], _dot(do16, vn16, 1, 1), 0.0)
            d_kd = _dot(vn16, ds16, 1, 1)
            d_gam_last = jnp.sum(jnp.sum(st * ds_out, axis=1, keepdims=True), axis=0, keepdims=True)
            d_w = -_dot(dvn16, st16, 1, 1)
            dstate[hh] = qgdo + ds_out * cr["gam_last"] - _dot(w16, dvn16, 0, 0)
            d_bv = _dot(t16, dvn16, 0, 0)
            yield
            d_kbg = _dot(t16, d_w, 0, 0)
            n_p = (d_p * dec).astype(BF16)
            d_q = _dot(n_p, k16, 1, 0) + d_qg * gam
            npq = _dot(n_p, q16, 0, 0)
            yield
            d_a = jnp.where(cr["strict"], -(_dot(d_bv, u, 1, 1) + _dot(d_kbg, w16, 1, 1)), 0.0)
            yield
            m_a = (d_a * dec).astype(BF16)
            d_kb = _dot(m_a, k16, 1, 0) + d_kbg * gam
            d_k = (_dot(m_a, cr["kb16"], 0, 0) + npq + d_kd * e + d_kb * b_col)
            yield
            d_v = d_bv * b_col
            d_beta_col = (jnp.sum(d_bv * v, axis=1, keepdims=True)
                          + jnp.sum(d_kb * k, axis=1, keepdims=True))
            gterm = d_a * cr["a_mat"] + d_p * p
            d_e = jnp.sum(d_kd * k, axis=1, keepdims=True) * e
            d_g_col = (jnp.sum(gterm, axis=1, keepdims=True)
                       + (jnp.sum(d_qg * q, axis=1, keepdims=True)
                          + jnp.sum(d_kbg * cr["kb"], axis=1, keepdims=True)) * gam
                       - d_e)
            d_g_last = jnp.sum(d_e, axis=0, keepdims=True) + d_gam_last * cr["gam_last"]
            d_g_row = (_col_to_row(d_g_col, eye) - jnp.sum(gterm, axis=0, keepdims=True)
                       + jnp.where(lane == A_CHUNK - 1, d_g_last, 0.0))
            dgc_ref[hh, seq_chunk, :] = d_g_row
            dbeta_ref[hh, seq_chunk, :] = _col_to_row(d_beta_col, eye)

            qh = cin["aq"] * cin["rq"]
            kh = cin["ak"] * cin["rk"]
            d_qh = d_q * (A_DK ** -0.5)
            d_aq = cin["rq"] * (d_qh - qh * jnp.sum(d_qh * qh, axis=1, keepdims=True))
            d_ak = cin["rk"] * (d_k - kh * jnp.sum(d_k * kh, axis=1, keepdims=True))
            d_y = jnp.concatenate([d_aq, d_ak, d_v], axis=1) * _dsilu(cin["y"])
            shifted[hh, 0, 0:A_CHUNK, :] = d_y
            shifted[hh, 0, A_CHUNK:A_CHUNK + HALO, :] = dy_next[hh]
            shifted[hh, 1, 0:A_CHUNK + HALO, :] = cin["xw"]
            d_x = cw[A_CONV - 1:A_CONV, :] * d_y
            for j in range(1, A_CONV):
                d_x = d_x + cw[A_CONV - 1 - j:A_CONV - j, :] * shifted[hh, 0, j:j + A_CHUNK, :]
            for j in range(A_CONV):
                xs = shifted[hh, 1, HALO - j:HALO - j + A_CHUNK, :]
                dcw_ref[A_CONV - 1 - j:A_CONV - j, ccols] += jnp.sum(d_y * xs, axis=0, keepdims=True)
            dy_next[hh] = d_y[0:HALO, :]
            dx_ref[rows, lo:lo + A_CONV_COLS] = d_x.astype(BF16)
            dx_ref[rows, lo + A_CONV_COLS:lo + A_HEAD_COLS] = dz.astype(BF16)

        def chunk(ci, first):
            rows = pl.ds(0 if first else pl.multiple_of(ci * A_CHUNK, A_CHUNK), A_CHUNK)
            _round_robin([one_head(hh, ci, first, rows) for hh in range(hp)])

        lax.fori_loop(0, A_BLK_CHUNKS - 1, lambda i, c: (chunk(A_BLK_CHUNKS - 1 - i, False), c)[1], 0)
        chunk(0, True)

    rev = lambda j: nsb - 1 - j
    small = pl.BlockSpec((None, hp, n, A_CHUNK), lambda b, h, j: (b, h, 0, 0))
    wide = pl.BlockSpec((None, A_SEQ_BLK, hp * A_HEAD_COLS), lambda b, h, j: (b, rev(j), h))
    val = pl.BlockSpec((None, A_SEQ_BLK, hp * A_DV), lambda b, h, j: (b, rev(j), h))
    return pl.pallas_call(
        body, name="gdn_bwd", grid=(bn, A_HEADS // hp, nsb),
        in_specs=[wide,
                  pl.BlockSpec((None, None, HALO, hp * A_HEAD_COLS), lambda b, h, j: (b, rev(j), 0, h)),
                  pl.BlockSpec((A_CONV, hp * A_CONV_COLS), lambda b, h, j: (0, h)),
                  small, small,
                  pl.BlockSpec((1, A_DV), lambda b, h, j: (0, 0)),
                  val,
                  pl.BlockSpec((None, hp, A_BLK_CHUNKS, A_DK, A_DV), lambda b, h, j: (b, h, rev(j), 0, 0)),
                  pl.BlockSpec((None, hp, A_BLK_CHUNKS, A_CHUNK, A_CHUNK), lambda b, h, j: (b, h, rev(j), 0, 0)),
                  pl.BlockSpec((None, A_SEQ_BLK, hp * A_CONV_COLS), lambda b, h, j: (b, rev(j), h)),
                  val],
        out_specs=[wide, small, small,
                   pl.BlockSpec((None, A_CONV, hp * A_CONV_COLS), lambda b, h, j: (b, 0, h)),
                   pl.BlockSpec((None, hp, 8, A_DV), lambda b, h, j: (b, h, 0, 0))],
        out_shape=[jax.ShapeDtypeStruct((bn, s, A_HEADS * A_HEAD_COLS), BF16),
                   jax.ShapeDtypeStruct((bn, A_HEADS, n, A_CHUNK), F32),
                   jax.ShapeDtypeStruct((bn, A_HEADS, n, A_CHUNK), F32),
                   jax.ShapeDtypeStruct((bn, A_CONV, A_HEADS * A_CONV_COLS), F32),
                   jax.ShapeDtypeStruct((bn, A_HEADS, 8, A_DV), F32)],
        scratch_shapes=[pltpu.VMEM((hp, A_DK, A_DV), F32), pltpu.VMEM((hp, HALO, A_CONV_COLS), F32),
                        pltpu.VMEM((hp, 2, A_CHUNK + 2 * HALO, A_CONV_COLS), F32)],
        compiler_params=_params(("parallel", "parallel", "arbitrary"), VMEM_BIG),
    )(proj_hm, halo, cw_hm, beta, gc, norm_g, oraw, states, t_mats, conv_y, dog)


def _rope_tables(posf, inv_freq_row):
    t = posf.shape[0]
    tm = 512

    def body(p_ref, f_ref, c_ref, sa_ref, sb_ref):
        ang = p_ref[...] * f_ref[...]
        lane = lax.broadcasted_iota(jnp.int32, ang.shape, 1)
        half = ROPE_DIMS // 2
        c_ref[...] = jnp.where(lane < ROPE_DIMS, jnp.cos(ang), 1.0)
        sn = jnp.sin(ang)
        sa_ref[...] = jnp.where(lane < half, -sn, 0.0)
        sb_ref[...] = jnp.where((lane >= half) & (lane < ROPE_DIMS), sn, 0.0)

    row = pl.BlockSpec((tm, 128), lambda i: (i, 0))
    return pl.pallas_call(
        body, name="rope_tables", grid=(t // tm,),
        in_specs=[row, pl.BlockSpec((1, 128), lambda i: (0, 0))], out_specs=[row] * 3,
        out_shape=[jax.ShapeDtypeStruct((t, 128), F32)] * 3,
        compiler_params=_params(("parallel",)),
    )(posf, inv_freq_row)


def _qk_prep(proj, c, sa, sb, qg, kg, name, tm=256):
    t = proj.shape[0]

    def body(x_ref, c_ref, sa_ref, sb_ref, qg_ref, kg_ref, o_ref):
        cc, s1, s2 = c_ref[...], sa_ref[...], sb_ref[...]
        half = ROPE_DIMS // 2

        def one_head(lo, g):
            xv = x_ref[:, lo:lo + B_DH]
            ms = jnp.mean(xv * xv, axis=1, keepdims=True)
            yield
            xn = xv * lax.rsqrt(ms + EPS) * g
            r1, r2 = pltpu.roll(xn, 128 - half, 1), pltpu.roll(xn, half, 1)
            yield
            o_ref[:, lo:lo + B_DH] = (xn * cc + r1 * s1 + r2 * s2).astype(BF16)

        for which, g_ref in ((0, qg_ref), (1, kg_ref)):
            g = g_ref[...]
            _round_robin([one_head(which * B_W + h * B_DH, g) for h in range(B_HEADS)])
        o_ref[:, 2 * B_W:3 * B_W] = x_ref[:, 2 * B_W:3 * B_W].astype(BF16)

    tab = pl.BlockSpec((tm, 128), lambda i: (i, 0))
    gain = pl.BlockSpec((1, B_DH), lambda i: (0, 0))
    return pl.pallas_call(
        body, name=name, grid=(t // tm,),
        in_specs=[pl.BlockSpec((tm, 3 * B_W), lambda i: (i, 0)), tab, tab, tab, gain, gain],
        out_specs=pl.BlockSpec((tm, 3 * B_W), lambda i: (i, 0)),
        out_shape=jax.ShapeDtypeStruct((t, 3 * B_W), BF16),
        compiler_params=_params(("parallel",), 40 * 1024 * 1024),
    )(proj, c, sa, sb, qg, kg)


def _qk_prep_bwd(proj, c, sa, sb, qg, kg, dq, dk, dv, dz, name, tm=256):
    t = proj.shape[0]
    out_w = 3 * B_W + (B_W if dz is not None else 0)

    def body(*refs):
        x_ref, c_ref, sa_ref, sb_ref, qg_ref, kg_ref, dq_ref, dk_ref, dv_ref = refs[:9]
        if dz is not None:
            dz_ref, o_ref, dgain_ref = refs[9:]
        else:
            o_ref, dgain_ref = refs[9:]
        i = pl.program_id(0)

        @pl.when(i == 0)
        def _():
            dgain_ref[...] = jnp.zeros_like(dgain_ref)

        cc, s1, s2 = c_ref[...], sa_ref[...], sb_ref[...]
        half = ROPE_DIMS // 2

        def one_head(which, h, g, d_ref, parts):
            lo = which * B_W + h * B_DH
            xv = x_ref[:, lo:lo + B_DH]
            d_out = d_ref[:, h * B_DH:(h + 1) * B_DH].astype(F32)
            ms = jnp.mean(xv * xv, axis=1, keepdims=True)
            r1, r2 = pltpu.roll(d_out * s1, half, 1), pltpu.roll(d_out * s2, 128 - half, 1)
            yield
            r = lax.rsqrt(ms + EPS)
            xh = xv * r
            d_xn = d_out * cc + r1 + r2
            parts.append(jnp.sum(d_xn * xh, axis=0, keepdims=True))
            d_xh = d_xn * g
            dot = jnp.mean(d_xh * xh, axis=1, keepdims=True)
            yield
            o_ref[:, lo:lo + B_DH] = (r * (d_xh - xh * dot)).astype(BF16)

        for which, g_ref, d_ref in ((0, qg_ref, dq_ref), (1, kg_ref, dk_ref)):
            parts = []
            _round_robin([one_head(which, h, g_ref[...], d_ref, parts) for h in range(B_HEADS)])
            acc = parts[0]
            for part in parts[1:]:
                acc = acc + part
            dgain_ref[which:which + 1, :] += acc
        o_ref[:, 2 * B_W:3 * B_W] = dv_ref[...]
        if dz is not None:
            o_ref[:, 3 * B_W:4 * B_W] = dz_ref[...]

    tab = pl.BlockSpec((tm, 128), lambda i: (i, 0))
    gain = pl.BlockSpec((1, B_DH), lambda i: (0, 0))
    grad = pl.BlockSpec((tm, B_W), lambda i: (i, 0))
    in_specs = [pl.BlockSpec((tm, 2 * B_W), lambda i: (i, 0)), tab, tab, tab, gain, gain, grad, grad, grad]
    args = [proj, c, sa, sb, qg, kg, dq, dk, dv]
    if dz is not None:
        in_specs.append(grad)
        args.append(dz)
    return pl.pallas_call(
        body, name=name, grid=(t // tm,), in_specs=in_specs,
        out_specs=[pl.BlockSpec((tm, out_w), lambda i: (i, 0)), pl.BlockSpec((8, B_DH), lambda i: (0, 0))],
        out_shape=[jax.ShapeDtypeStruct((t, out_w), BF16), jax.ShapeDtypeStruct((8, B_DH), F32)],
        compiler_params=_params(("arbitrary",), 40 * 1024 * 1024),
    )(*args)


def _attn_masks():
    qi = lax.broadcasted_iota(jnp.int32, (B_BLK, 2 * B_BLK), 0)
    kj = lax.broadcasted_iota(jnp.int32, (B_BLK, 2 * B_BLK), 1)
    two = (kj >= qi) & (kj <= qi + B_BLK)
    q1 = lax.broadcasted_iota(jnp.int32, (B_BLK, B_BLK), 0)
    k1 = lax.broadcasted_iota(jnp.int32, (B_BLK, B_BLK), 1)
    return k1 <= q1, two


def _lane_pick(ref_rows, h):
    lane = lax.broadcasted_iota(jnp.int32, ref_rows.shape, 1)
    return jnp.sum(jnp.where(lane == h, ref_rows, 0.0), axis=1, keepdims=True)


B_ROWS = 2048


def _attn_schedule(nb, sb, block):
    way = 4

    def run(items):
        for at in range(0, len(items), way):
            _round_robin([block(*it) for it in items[at:at + way]])

    run([(si, 0, True) for si in range(sb)])
    if nb == 1:
        return
    per = max(1, way // sb)
    lead = 1 + (nb - 1) % per
    if lead > 1:
        run([(si, i, False) for i in range(1, lead) for si in range(sb)])

    def step(it, carry):
        run([(si, lead + it * per + u, False) for u in range(per) for si in range(sb)])
        return carry

    lax.fori_loop(0, (nb - lead) // per, step, 0)


def _attn_rows(i, first):
    if first:
        return pl.ds(0, B_BLK), pl.ds(0, B_BLK)
    rows = pl.ds(pl.multiple_of(i * B_BLK, B_BLK), B_BLK)
    return rows, pl.ds(pl.multiple_of((i - 1) * B_BLK, B_BLK), 2 * B_BLK)


def _attn_fwd(qkv, name):
    ns, ln, _ = qkv.shape
    nb = ln // B_BLK
    sb = B_ROWS // ln
    scale = B_DH ** -0.5

    def body(q_ref, k_ref, v_ref, o_ref, lse_ref):
        h = pl.program_id(1)
        mask1, mask2 = _attn_masks()
        lane = lax.broadcasted_iota(jnp.int32, (B_BLK, B_HEADS), 1)

        @pl.when(h == 0)
        def _():
            lse_ref[...] = jnp.zeros_like(lse_ref)

        def block(si, i, first):
            rows, win = _attn_rows(i, first)
            mask = mask1 if first else mask2
            sc = jnp.where(mask, _dot(q_ref[si, rows, :], k_ref[si, win, :], 1, 1) * scale, -1e30)
            yield
            m = jnp.max(sc, axis=1, keepdims=True)
            p = jnp.exp(sc - m)
            l = jnp.sum(p, axis=1, keepdims=True)
            pv = _dot(p, v_ref[si, win, :], 1, 0)
            yield
            o_ref[si, rows, :] = pv / l
            lse_ref[si, rows, :] = jnp.where(lane == h, m + jnp.log(l), lse_ref[si, rows, :])

        _attn_schedule(nb, sb, block)

    head = lambda off: pl.BlockSpec((sb, ln, B_DH), lambda s, h: (s, 0, off + h))
    return pl.pallas_call(
        body, name=name, grid=(ns // sb, B_HEADS),
        in_specs=[head(0), head(B_HEADS), head(2 * B_HEADS)],
        out_specs=[head(0), pl.BlockSpec((sb, ln, B_HEADS), lambda s, h: (s, 0, 0))],
        out_shape=[jax.ShapeDtypeStruct((ns, ln, B_W), F32), jax.ShapeDtypeStruct((ns, ln, B_HEADS), F32)],
        compiler_params=_params(("parallel", "arbitrary")),
    )(qkv, qkv, qkv)


def _attn_bwd(qkv, d_o, lse_joint, delta, name):
    ns, ln, _ = qkv.shape
    nb = ln // B_BLK
    sb = B_ROWS // ln
    scale = B_DH ** -0.5

    def body(q_ref, k_ref, v_ref, do_ref, lj_ref, dl_ref, dq_ref, dk_out, dv_out, dk_ref, dv_ref):
        h = pl.program_id(1)
        mask1, mask2 = _attn_masks()
        dk_ref[...] = jnp.zeros_like(dk_ref)
        dv_ref[...] = jnp.zeros_like(dv_ref)

        def block(si, i, first):
            rows, win = _attn_rows(i, first)
            mask = mask1 if first else mask2
            q = q_ref[si, rows, :]
            d_out = do_ref[si, rows, :]
            l_col = _lane_pick(lj_ref[si, rows, :], h)
            d_col = _lane_pick(dl_ref[si, rows, :], h)
            sc = _dot(q, k_ref[si, win, :], 1, 1) * scale
            d_p = _dot(d_out, v_ref[si, win, :], 1, 1)
            yield
            p = jnp.exp(jnp.where(mask, sc - l_col, -1e30))
            d_s = p * (d_p - d_col) * scale
            d_q = _dot(d_s, k_ref[si, win, :], 1, 0)
            d_k = _dot(d_s, q, 0, 0)
            d_v = _dot(p, d_out, 0, 0)
            yield
            dq_ref[si, rows, :] = d_q.astype(BF16)
            dk_ref[si, win, :] += d_k
            dv_ref[si, win, :] += d_v

        _attn_schedule(nb, sb, block)
        dk_out[...] = dk_ref[...].astype(BF16)
        dv_out[...] = dv_ref[...].astype(BF16)

    head = lambda off: pl.BlockSpec((sb, ln, B_DH), lambda s, h: (s, 0, off + h))
    small = pl.BlockSpec((sb, ln, B_HEADS), lambda s, h: (s, 0, 0))
    return pl.pallas_call(
        body, name=name, grid=(ns // sb, B_HEADS),
        in_specs=[head(0), head(B_HEADS), head(2 * B_HEADS), head(0), small, small],
        out_specs=[head(0)] * 3,
        out_shape=[jax.ShapeDtypeStruct((ns, ln, B_W), BF16)] * 3,
        scratch_shapes=[pltpu.VMEM((sb, ln, B_DH), F32)] * 2,
        compiler_params=_params(("parallel", "parallel")),
    )(qkv, qkv, qkv, d_o, lse_joint, delta)


def _merge_weights(lse_refs):
    ls = [r[...] for r in lse_refs]
    m = jnp.maximum(jnp.maximum(ls[0], ls[1]), ls[2])
    es = [jnp.exp(l - m) for l in ls]
    tot = es[0] + es[1] + es[2]
    return [e / tot for e in es], m + jnp.log(tot)


def _merge_fwd(outs, lses, proj0, tm=256):
    t = outs[0].shape[0]

    def body(o0, o1, o2, l0, l1, l2, z_ref, og_ref):
        wts, _ = _merge_weights((l0, l1, l2))

        def one_head(h):
            cols = slice(h * B_DH, (h + 1) * B_DH)
            w0, w1, w2 = (jnp.broadcast_to(w[:, h:h + 1], (tm, B_DH)) for w in wts)
            yield
            o = w0 * o0[:, cols] + w1 * o1[:, cols] + w2 * o2[:, cols]
            og_ref[:, cols] = (o * _silu(z_ref[:, cols])).astype(BF16)

        _round_robin([one_head(h) for h in range(B_HEADS)])

    wide = pl.BlockSpec((tm, B_W), lambda i: (i, 0))
    small = pl.BlockSpec((tm, B_HEADS), lambda i: (i, 0))
    return pl.pallas_call(
        body, name="merge_fwd", grid=(t // tm,),
        in_specs=[wide] * 3 + [small] * 3 + [pl.BlockSpec((tm, B_W), lambda i: (i, 3))],
        out_specs=wide, out_shape=jax.ShapeDtypeStruct((t, B_W), BF16),
        compiler_params=_params(("parallel",)),
    )(*outs, *lses, proj0)


def _merge_bwd(outs, lses, proj0, d_og, tm=256):
    t = outs[0].shape[0]

    def body(o0, o1, o2, l0, l1, l2, z_ref, dog_ref, do_ref, lj_ref, dl_ref, dz_ref):
        wts, lj = _merge_weights((l0, l1, l2))
        lj_ref[...] = lj
        lane = lax.broadcasted_iota(jnp.int32, (tm, B_HEADS), 1)
        sums = [None] * B_HEADS

        def one_head(h):
            cols = slice(h * B_DH, (h + 1) * B_DH)
            w0, w1, w2 = (jnp.broadcast_to(w[:, h:h + 1], (tm, B_DH)) for w in wts)
            yield
            o = w0 * o0[:, cols] + w1 * o1[:, cols] + w2 * o2[:, cols]
            z = z_ref[:, cols]
            d_g = dog_ref[:, cols]
            d_out = d_g * _silu(z)
            dz_ref[:, cols] = (d_g * o * _dsilu(z)).astype(BF16)
            do_ref[:, cols] = d_out.astype(BF16)
            sums[h] = jnp.sum(d_out * o, axis=1, keepdims=True)
            yield

        _round_robin([one_head(h) for h in range(B_HEADS)])
        delta = jnp.zeros((tm, B_HEADS), F32)
        for h in range(B_HEADS):
            delta = jnp.where(lane == h, sums[h], delta)
        dl_ref[...] = delta

    wide = pl.BlockSpec((tm, B_W), lambda i: (i, 0))
    small = pl.BlockSpec((tm, B_HEADS), lambda i: (i, 0))
    return pl.pallas_call(
        body, name="merge_bwd", grid=(t // tm,),
        in_specs=[wide] * 3 + [small] * 3 + [pl.BlockSpec((tm, B_W), lambda i: (i, 3)), wide],
        out_specs=[wide, small, small, wide],
        out_shape=[jax.ShapeDtypeStruct((t, B_W), BF16), jax.ShapeDtypeStruct((t, B_HEADS), F32),
                   jax.ShapeDtypeStruct((t, B_HEADS), F32), jax.ShapeDtypeStruct((t, B_W), BF16)],
        compiler_params=_params(("parallel",)),
    )(*outs, *lses, proj0, d_og)


def _adamw(w, g, m, v, name):
    r, c = w.shape
    tr = r
    for cand in (256, 128, 64, 32, 16, 8):
        if r % cand == 0:
            tr = cand
            break

    def body(w_ref, g_ref, m_ref, v_ref, d_ref, nm_ref, nv_ref):
        gv = g_ref[...]
        nm = ADAM_B1 * m_ref[...] + (1.0 - ADAM_B1) * gv
        nv = ADAM_B2 * v_ref[...] + (1.0 - ADAM_B2) * (gv * gv)
        m_hat = nm / (1.0 - ADAM_B1 ** ADAM_STEP)
        v_hat = nv / (1.0 - ADAM_B2 ** ADAM_STEP)
        d_ref[...] = -ADAM_LR * (m_hat / (jnp.sqrt(v_hat) + ADAM_EPS) + ADAM_WD * w_ref[...])
        nm_ref[...] = nm
        nv_ref[...] = nv

    blk = pl.BlockSpec((tr, c), lambda i: (i, 0))
    return pl.pallas_call(
        body, name=name, grid=(r // tr,), in_specs=[blk] * 4, out_specs=[blk] * 3,
        out_shape=[jax.ShapeDtypeStruct((r, c), F32)] * 3,
        compiler_params=_params(("parallel",)),
    )(w, g, m, v)


def _adam_update(w, gv, m, v):
    nm = ADAM_B1 * m + (1.0 - ADAM_B1) * gv
    nv = ADAM_B2 * v + (1.0 - ADAM_B2) * (gv * gv)
    m_hat = nm / (1.0 - ADAM_B1 ** ADAM_STEP)
    v_hat = nv / (1.0 - ADAM_B2 ** ADAM_STEP)
    return -ADAM_LR * (m_hat / (jnp.sqrt(v_hat) + ADAM_EPS) + ADAM_WD * w), nm, nv


def _adamw_shard(w, mine, theirs, m, v, half_index, name, tr=128):
    _, r, c = w.shape
    nhb = (r // 2) // tr

    def body(c_ref, w_ref, mine_ref, theirs_ref, m_ref, v_ref, g_ref, d_ref, nm_ref, nv_ref):
        is_mine = (pl.program_id(0) // nhb) == c_ref[0]
        gv = jnp.where(is_mine, mine_ref[...], theirs_ref[...])
        d, nm, nv = _adam_update(w_ref[...], gv, m_ref[...], v_ref[...])
        g_ref[...] = gv
        d_ref[...] = d
        nm_ref[...] = nm
        nv_ref[...] = nv

    full = pl.BlockSpec((None, tr, c), lambda i, cc: (0, i, 0))
    half = pl.BlockSpec((tr, c), lambda i, cc: (i % nhb, 0))
    return pl.pallas_call(
        body, name=name,
        grid_spec=pltpu.PrefetchScalarGridSpec(
            num_scalar_prefetch=1, grid=(2 * nhb,),
            in_specs=[full, half, half, full, full], out_specs=[full] * 4),
        out_shape=[jax.ShapeDtypeStruct(w.shape, F32)] * 4,
        compiler_params=_params(("parallel",), 40 * 1024 * 1024),
    )(half_index, w, mine, theirs, m, v)


def _adamw_shard_cols(w, mine, theirs, m, v, half_index, name, steps=20):
    c, _, r = w.shape
    tc = c // steps
    assert tc * steps == c

    def body(c_ref, w_ref, mine_ref, theirs_ref, m_ref, v_ref, g_ref, d_ref, nm_ref, nv_ref):
        first = jnp.where(c_ref[0] == 0, mine_ref[...], theirs_ref[...])
        second = jnp.where(c_ref[0] == 0, theirs_ref[...], mine_ref[...])
        for lo, gv in ((0, first), (r // 2, second)):
            cols = slice(lo, lo + r // 2)
            d, nm, nv = _adam_update(w_ref[:, :, cols], gv, m_ref[:, :, cols], v_ref[:, :, cols])
            g_ref[:, :, cols] = gv
            d_ref[:, :, cols] = d
            nm_ref[:, :, cols] = nm
            nv_ref[:, :, cols] = nv

    full = pl.BlockSpec((tc, 1, r), lambda i, cc: (i, 0, 0))
    half = pl.BlockSpec((tc, 1, r // 2), lambda i, cc: (i, 0, 0))
    return pl.pallas_call(
        body, name=name,
        grid_spec=pltpu.PrefetchScalarGridSpec(
            num_scalar_prefetch=1, grid=(steps,),
            in_specs=[full, half, half, full, full], out_specs=[full] * 4),
        out_shape=[jax.ShapeDtypeStruct(w.shape, F32)] * 4,
        compiler_params=_params(("parallel",), 40 * 1024 * 1024),
    )(half_index, w, mine, theirs, m, v)


def _pair_sum(own, other, half_index, name, tr=256):
    _, r, c = own.shape
    rh = r // 2
    tr = min(tr, rh)
    nrb = rh // tr

    def body(c_ref, own_ref, oth_ref, out_ref):
        out_ref[...] = (own_ref[...] + oth_ref[...].astype(F32)).astype(BF16)

    return pl.pallas_call(
        body, name=name,
        grid_spec=pltpu.PrefetchScalarGridSpec(
            num_scalar_prefetch=1, grid=(N_CHIPS, nrb),
            in_specs=[pl.BlockSpec((None, tr, c), lambda k, i, cc: (k, cc[0] * nrb + i, 0)),
                      pl.BlockSpec((None, tr, c), lambda k, i, cc: (k, i, 0))],
            out_specs=pl.BlockSpec((None, tr, c), lambda k, i, cc: (k, i, 0))),
        out_shape=jax.ShapeDtypeStruct((N_CHIPS, rh, c), BF16),
        compiler_params=_params(("parallel", "parallel")),
    )(half_index, own, other)


def _chip_sum(sums, others, chip_index, name, tr=256):
    _, r, c = sums.shape
    tr = min(tr, r)

    def body(k_ref, own_ref, oth_ref, out_ref):
        acc = own_ref[...].astype(F32)
        for j in range(N_CHIPS - 1):
            acc = acc + oth_ref[j].astype(F32)
        out_ref[...] = acc

    return pl.pallas_call(
        body, name=name,
        grid_spec=pltpu.PrefetchScalarGridSpec(
            num_scalar_prefetch=1, grid=(r // tr,),
            in_specs=[pl.BlockSpec((None, tr, c), lambda i, kk: (kk[0], i, 0)),
                      pl.BlockSpec((N_CHIPS - 1, tr, c), lambda i, kk: (0, i, 0))],
            out_specs=pl.BlockSpec((tr, c), lambda i, kk: (i, 0))),
        out_shape=jax.ShapeDtypeStruct((r, c), F32),
        compiler_params=_params(("parallel",)),
    )(chip_index, sums, others)


HBM = pl.BlockSpec(memory_space=pltpu.HBM)


def _place():
    x, y, c = lax.axis_index("x"), lax.axis_index("y"), lax.axis_index("c")
    chips = [(1 - x, y), (x, 1 - y), (1 - x, 1 - y)]
    return x, y, c, chips


def _sibling_forward(land):
    def body(in_ref, out_ref, send, recv):
        x, y, c, chips = _place()
        rh = out_ref.shape[1] // 2
        cps = []
        for j, (px, py) in enumerate(chips):
            slot = out_ref.at[2 * px + py, pl.ds(c * rh, rh)]
            cp = pltpu.make_async_remote_copy(
                src_ref=slot, dst_ref=slot, send_sem=send.at[j], recv_sem=recv.at[j],
                device_id=(x, y, 1 - c), device_id_type=MESH)
            cp.start()
            cps.append(cp)
        for j, (px, py) in enumerate(chips):
            slot = out_ref.at[2 * px + py, pl.ds((1 - c) * rh, rh)]
            pltpu.make_async_remote_copy(
                src_ref=slot, dst_ref=slot, send_sem=send.at[j], recv_sem=recv.at[j],
                device_id=(x, y, 1 - c), device_id_type=MESH).wait_recv()
        for cp in cps:
            cp.wait_send()

    return pl.pallas_call(
        body, name="first_weights_sibling_forward", in_specs=[HBM], out_specs=HBM,
        out_shape=jax.ShapeDtypeStruct(land.shape, land.dtype), input_output_aliases={0: 0},
        scratch_shapes=[pltpu.SemaphoreType.DMA((3,)), pltpu.SemaphoreType.DMA((3,))],
    )(land)


def _sibling_swap_halves(grads, name):
    na = len(grads)

    def body(*refs):
        ins, outs = refs[:na], refs[na:2 * na]
        send, recv = refs[2 * na:]
        x, y, c, _ = _place()
        sib = (x, y, 1 - c)
        cps = []
        for i in range(na):
            rh = ins[i].shape[1] // 2
            cp = pltpu.make_async_remote_copy(
                src_ref=ins[i].at[:, pl.ds((1 - c) * rh, rh), :], dst_ref=outs[i],
                send_sem=send.at[i], recv_sem=recv.at[i], device_id=sib, device_id_type=MESH)
            cp.start()
            cps.append(cp)
        for cp in cps:
            cp.wait()

    out_shape = [jax.ShapeDtypeStruct((g.shape[0], g.shape[1] // 2, g.shape[2]), g.dtype) for g in grads]
    return pl.pallas_call(
        body, name=name, in_specs=[HBM] * na, out_specs=[HBM] * na, out_shape=out_shape,
        scratch_shapes=[pltpu.SemaphoreType.DMA((na,)), pltpu.SemaphoreType.DMA((na,))],
    )(*grads)


def _sibling_swap_whole(halves):
    na = len(halves)

    def body(*refs):
        ins, outs = refs[:na], refs[na:2 * na]
        send, recv = refs[2 * na:]
        x, y, c, _ = _place()
        cps = []
        for i in range(na):
            cp = pltpu.make_async_remote_copy(
                src_ref=ins[i], dst_ref=outs[i], send_sem=send.at[i], recv_sem=recv.at[i],
                device_id=(x, y, 1 - c), device_id_type=MESH)
            cp.start()
            cps.append(cp)
        for cp in cps:
            cp.wait()

    out_shape = [jax.ShapeDtypeStruct(h.shape, h.dtype) for h in halves]
    return pl.pallas_call(
        body, name="grad_sibling_join", in_specs=[HBM] * na, out_specs=[HBM] * na, out_shape=out_shape,
        scratch_shapes=[pltpu.SemaphoreType.DMA((na,)), pltpu.SemaphoreType.DMA((na,))],
    )(*halves)


SEM = pl.BlockSpec(memory_space=pltpu.SEMAPHORE)
ANY = pl.BlockSpec(memory_space=pl.ANY)
EFFECT = pltpu.SideEffectType.DATAFLOW_SIDE_EFFECTING


def _split_copy_start(name, plan, srcs, lands, after):
    ns, nl = len(srcs), len(lands)

    def body(*refs):
        src_refs, land_refs = refs[:ns], refs[ns:ns + nl]
        send, recv = refs[ns + nl + 1], refs[ns + nl + 2]
        token = refs[-1]
        outgoing, _ = plan(src_refs, land_refs)
        for src, dst, dev, si, ri in outgoing:
            pltpu.make_async_remote_copy(src_ref=src, dst_ref=dst, send_sem=send.at[si], recv_sem=recv.at[ri],
                                         device_id=dev, device_id_type=MESH).start()
        token[...] = jnp.zeros_like(token)

    n_out, n_in = plan.counts
    thru = [pltpu.HBM(a.shape, a.dtype) for a in list(srcs) + list(lands)]
    res = pl.pallas_call(
        body, name=name,
        out_shape=[pltpu.SemaphoreType.DMA((n_out,)), pltpu.SemaphoreType.DMA((n_in,))] + thru
        + [jax.ShapeDtypeStruct((8, 128), F32)],
        in_specs=[HBM] * (ns + nl) + [ANY],
        out_specs=[SEM, SEM] + [HBM] * (ns + nl) + [pl.BlockSpec(memory_space=pltpu.VMEM)],
        input_output_aliases={i: 2 + i for i in range(ns + nl)},
        compiler_params=pltpu.CompilerParams(has_side_effects=EFFECT),
    )(*[pltpu.with_memory_space_constraint(a, pltpu.HBM) for a in list(srcs) + list(lands)], after)
    return res[0], res[1], res[2:2 + ns], res[2 + ns:2 + ns + nl], res[-1]


def _split_copy_wait(name, plan, send, recv, srcs, lands, after):
    ns, nl = len(srcs), len(lands)
    after = list(after) if isinstance(after, (list, tuple)) else [after]

    def body(*refs):
        src_refs, land_refs = refs[:ns], refs[ns:ns + nl]
        send_ref, recv_ref = refs[ns + nl], refs[ns + nl + 1]
        outgoing, arrivals = plan(src_refs, land_refs)
        for src, dst, dev, si, ri in outgoing:
            pltpu.make_async_remote_copy(src_ref=src, dst_ref=dst, send_sem=send_ref.at[si], recv_sem=recv_ref.at[ri],
                                         device_id=dev, device_id_type=MESH).wait_send()
        for view, ri in arrivals:
            pltpu.make_async_remote_copy(src_ref=view, dst_ref=view, send_sem=send_ref.at[0], recv_sem=recv_ref.at[ri],
                                         device_id=_place()[:3], device_id_type=MESH).wait_recv()

    thru = [pltpu.HBM(a.shape, a.dtype) for a in list(srcs) + list(lands)]
    res = pl.pallas_call(
        body, name=name, out_shape=thru,
        in_specs=[HBM] * (ns + nl) + [SEM, SEM] + [ANY] * len(after), out_specs=[HBM] * (ns + nl),
        input_output_aliases={i: i for i in range(ns + nl)},
        compiler_params=pltpu.CompilerParams(has_side_effects=EFFECT),
    )(*srcs, *lands, send, recv, *after)
    return res[:ns], res[ns:]


def _gather_plan(n_arrays):
    def plan(src_refs, land_refs):
        x, y, c, chips = _place()
        me = 2 * x + y
        outgoing, arrivals = [], []
        for i in range(n_arrays):
            rh = src_refs[i].shape[0] // 2
            mine = pl.ds(c * rh, rh)
            for j, (px, py) in enumerate(chips):
                for delta in range(2):
                    tc = c ^ delta
                    outgoing.append((src_refs[i].at[mine], land_refs[i].at[me, mine], (px, py, tc),
                                     6 * i + 2 * j + delta, 6 * i + 2 * j + delta))
                    theirs = pl.ds(tc * rh, rh)
                    arrivals.append((land_refs[i].at[2 * px + py, theirs], 6 * i + 2 * j + delta))
        return outgoing, arrivals

    plan.counts = (6 * n_arrays, 6 * n_arrays)
    return plan


def _first_gather_plan():
    def plan(src_refs, land_refs):
        x, y, c, chips = _place()
        me = 2 * x + y
        rh = src_refs[0].shape[0] // 2
        mine = pl.ds(c * rh, rh)
        outgoing, arrivals = [], []
        for j, (px, py) in enumerate(chips):
            outgoing.append((src_refs[0].at[mine], land_refs[0].at[me, mine], (px, py, c), j, j))
            arrivals.append((land_refs[0].at[2 * px + py, mine], j))
            outgoing.append((src_refs[1], land_refs[1].at[me], (px, py, c), 3 + j, 3 + j))
            arrivals.append((land_refs[1].at[2 * px + py], 3 + j))
        return outgoing, arrivals

    plan.counts = (6, 6)
    return plan


def _exchange_plan(n_arrays):
    def plan(src_refs, land_refs):
        x, y, c, chips = _place()
        outgoing, arrivals = [], []
        for i in range(n_arrays):
            for j, (px, py) in enumerate(chips):
                outgoing.append((src_refs[i].at[2 * px + py], land_refs[i].at[j], (px, py, c), 3 * i + j, 3 * i + j))
                arrivals.append((land_refs[i].at[j], 3 * i + j))
        return outgoing, arrivals

    plan.counts = (3 * n_arrays, 3 * n_arrays)
    return plan


def _small_allreduce(vec):
    r, cdim = vec.shape
    n_dev = 8

    def body(v_ref, out_ref, buf, send, recv):
        x, y, c, _ = _place()
        me = 4 * x + 2 * y + c
        buf[me] = v_ref[...]
        cps = []
        for k in range(1, n_dev):
            dx, dy, dc = (k >> 2) & 1, (k >> 1) & 1, k & 1
            peer = (x ^ dx, y ^ dy, c ^ dc)
            cp = pltpu.make_async_remote_copy(
                src_ref=v_ref, dst_ref=buf.at[me], send_sem=send.at[k - 1], recv_sem=recv.at[k - 1],
                device_id=peer, device_id_type=MESH)
            cp.start()
            cps.append(cp)
        for k in range(1, n_dev):
            dx, dy, dc = (k >> 2) & 1, (k >> 1) & 1, k & 1
            src = 4 * (x ^ dx) + 2 * (y ^ dy) + (c ^ dc)
            slot = buf.at[src]
            pltpu.make_async_remote_copy(
                src_ref=slot, dst_ref=slot, send_sem=send.at[k - 1], recv_sem=recv.at[k - 1],
                device_id=(x ^ dx, y ^ dy, c ^ dc), device_id_type=MESH).wait_recv()
        for cp in cps:
            cp.wait_send()
        acc = buf[0]
        for k in range(1, n_dev):
            acc = acc + buf[k]
        out_ref[...] = acc

    vm = pl.BlockSpec(memory_space=pltpu.VMEM)
    return pl.pallas_call(
        body, name="small_allreduce", in_specs=[vm], out_specs=vm,
        out_shape=jax.ShapeDtypeStruct((r, cdim), F32),
        scratch_shapes=[pltpu.VMEM((n_dev, r, cdim), F32), pltpu.SemaphoreType.DMA((n_dev - 1,)),
                        pltpu.SemaphoreType.DMA((n_dev - 1,))],
    )(vec)


def _a_cols_to_head_major(w):
    lead = w.shape[:-1]
    q = w[..., :A_QK].reshape(lead + (A_HEADS, A_DK))
    k = w[..., A_QK:2 * A_QK].reshape(lead + (A_HEADS, A_DK))
    v = w[..., 2 * A_QK:2 * A_QK + A_VW].reshape(lead + (A_HEADS, A_DV))
    z = w[..., 2 * A_QK + A_VW:].reshape(lead + (A_HEADS, A_DV))
    return jnp.concatenate([q, k, v, z], axis=-1).reshape(lead + (A_HEADS * A_HEAD_COLS,))


def _a_cols_from_head_major(w):
    lead = w.shape[:-1]
    w = w.reshape(lead + (A_HEADS, A_HEAD_COLS))
    parts = [w[..., :A_DK], w[..., A_DK:2 * A_DK], w[..., 2 * A_DK:2 * A_DK + A_DV], w[..., 2 * A_DK + A_DV:]]
    return jnp.concatenate([p.reshape(lead + (-1,)) for p in parts], axis=-1)


def _conv_cols_to_head_major(w):
    lead = w.shape[:-1]
    q = w[..., :A_QK].reshape(lead + (A_HEADS, A_DK))
    k = w[..., A_QK:2 * A_QK].reshape(lead + (A_HEADS, A_DK))
    v = w[..., 2 * A_QK:].reshape(lead + (A_HEADS, A_DV))
    return jnp.concatenate([q, k, v], axis=-1).reshape(lead + (A_HEADS * A_CONV_COLS,))


def _conv_cols_from_head_major(w):
    lead = w.shape[:-1]
    w = w.reshape(lead + (A_HEADS, A_CONV_COLS))
    parts = [w[..., :A_DK], w[..., A_DK:2 * A_DK], w[..., 2 * A_DK:]]
    return jnp.concatenate([p.reshape(lead + (-1,)) for p in parts], axis=-1)


def _to_stream(a, bn, d):
    rest = a.shape[1:]
    s = a.shape[0] // bn
    a = a.reshape((bn, s // d, d) + rest)
    a = jnp.swapaxes(a, 1, 2)
    return a.reshape((bn * d, s // d) + rest)


def _from_stream(a, bn, d):
    rest = a.shape[2:]
    ln = a.shape[1]
    a = a.reshape((bn, d, ln) + rest)
    a = jnp.swapaxes(a, 1, 2)
    return a.reshape((bn * ln * d,) + rest)


B_SUB = 512
B_SHARD_BLOCKS = (3 * B_GROUPS * B_W + B_W) // N_CHIPS // B_SUB


def _b_block(gi, jj):
    nb = (B_GROUPS * (jj // 2) + gi) * 2 + jj % 2
    return nb // B_SHARD_BLOCKS, nb % B_SHARD_BLOCKS


def _shard_major(g, ncols):
    r = g.shape[0]
    return jnp.swapaxes(g.reshape(r, N_CHIPS, ncols), 0, 1)


def _pack_rows(items):
    rows, offs = [], []
    at = 0
    for a in items:
        flat = a.reshape(-1).astype(F32)
        nr = -(-flat.shape[0] // 1024) * 8
        flat = jnp.pad(flat, (0, nr * 128 - flat.shape[0]))
        rows.append(flat.reshape(nr, 128))
        offs.append((at, nr, a.shape))
        at += nr
    return jnp.concatenate(rows, axis=0), offs


def _unpack_rows(packed, offs):
    out = []
    for at, nr, shape in offs:
        size = int(np.prod(shape)) if len(shape) else 1
        out.append(packed[at:at + nr].reshape(-1)[:size].reshape(shape))
    return out


def _local_step(x, positions, loss_target, norm_g, a_log, a_dt_bias, a_norm_g, b_q_norm_g, b_k_norm_g,
                start_token, first_weights, late_weights, b_grads_ready, a_grads_ready):
    bn, s, d = x.shape
    t = bn * s
    n_chunks = s // A_CHUNK
    x0 = x.reshape(t, d)
    h0 = _rms_fwd(x0, norm_g[0:1] + start_token, "rms0_fwd")
    inv_freq = ROPE_THETA ** (-jnp.arange(0, ROPE_DIMS, 2, dtype=F32) / ROPE_DIMS)
    freq_row = jnp.concatenate([inv_freq, inv_freq, jnp.zeros((128 - ROPE_DIMS,), F32)]).reshape(1, 128)
    posf = jnp.broadcast_to(positions.astype(F32).reshape(t, 1), (t, 128)) + start_token
    tabs = _rope_tables(posf, freq_row)
    tabs_s = [tabs if dil == 1 else [_to_stream(tb, bn, dil).reshape(t, 128) for tb in tabs] for dil in B_DIL]
    wa_in, conv_w, late_token = first_weights([h0] + [tb for ts in tabs_s for tb in ts])
    wa_main = _a_cols_to_head_major(wa_in[:, :A_MAIN])
    wa_tail = jnp.pad(wa_in[:, A_MAIN:], ((0, 0), (0, 128 - 2 * A_HEADS))) + late_token.astype(BF16)
    cw_hm = _conv_cols_to_head_major(conv_w)

    proj_a = _matmul(h0, wa_main, "nn", F32, "a_in_main")
    tail_a = _matmul(h0, wa_tail, "nn", F32, "a_in_tail")
    tail_t = jnp.swapaxes(tail_a[:, :2 * A_HEADS].reshape(bn, s, 2 * A_HEADS), 1, 2)
    tail_t = tail_t.reshape(bn, 2 * A_HEADS, n_chunks, A_CHUNK)
    beta, gc = _gdn_prep(tail_t, a_log[0], a_dt_bias[0])
    proj_a3 = proj_a.reshape(bn, s, A_MAIN)
    og_a, oraw_a, states, t_mats, conv_y = _gdn_fwd(proj_a3, cw_hm, beta, gc, a_norm_g)
    wa_out, wb_in, wb_out = late_weights(og_a)
    b_cols = [4 * B_W] + [3 * B_W] * (B_GROUPS - 1)
    x1 = _matmul(og_a.reshape(t, A_VW), wa_out, "nn", F32, "a_out", res=x0, tk=2048)

    h1 = _rms_fwd(x1, norm_g[1:2], "rms1_fwd")
    h1_s, proj_b, qkv_b, o_b, lse_b = [], [], [], [], []
    for gi, dil in enumerate(B_DIL):
        hs = h1 if dil == 1 else _to_stream(h1, bn, dil).reshape(t, d)
        ts = tabs_s[gi]
        pj = _matmul(hs, wb_in, "nn", F32, f"b_in_g{gi}", tm=2048, tn=B_SUB, n=b_cols[gi], b_spec=pl.BlockSpec(
            (None, d, B_SUB), lambda i, j, kk, gi=gi: (_b_block(gi, j)[0], kk, _b_block(gi, j)[1])))
        qkv = _qk_prep(pj, *ts, b_q_norm_g[0, gi:gi + 1], b_k_norm_g[0, gi:gi + 1], f"qk_prep_g{gi}")
        o_s, lse_s = _attn_fwd(qkv.reshape(bn * dil, s // dil, 3 * B_W), f"attn_fwd_g{gi}")
        h1_s.append(hs), proj_b.append(pj), qkv_b.append(qkv)
        o_b.append(o_s.reshape(t, B_W) if dil == 1 else _from_stream(o_s, bn, dil))
        lse_b.append(lse_s.reshape(t, B_HEADS) if dil == 1 else _from_stream(lse_s, bn, dil))
    og_b = _merge_fwd(o_b, lse_b, proj_b[0])
    x2 = _matmul(og_b, wb_out, "nn", F32, "b_out", res=x1)

    d_x2, loss_parts = _loss_grad(x2, loss_target.reshape(t, d))
    loss_local = jnp.sum(loss_parts)

    d_x2b = d_x2.astype(BF16)
    g_wb_out = _matmul(og_b, d_x2b, "tn", F32, "b_out_dw")
    d_og_b = _matmul(d_x2b, wb_out, "nt", F32, "b_out_dx")
    d_o, lse_joint, delta, d_z = _merge_bwd(o_b, lse_b, proj_b[0], d_og_b)
    d_h1, g_qn, g_kn = [], [], []
    g_wb_in = lax.empty(wb_in.shape, F32)
    for gi, dil in enumerate(B_DIL):
        if dil == 1:
            do_s, lj_s, dl_s = d_o, lse_joint, delta
        else:
            do_s, lj_s, dl_s = (_to_stream(a, bn, dil).reshape(t, -1) for a in (d_o, lse_joint, delta))
        ns, ln = bn * dil, s // dil
        dq, dk, dv = _attn_bwd(qkv_b[gi].reshape(ns, ln, 3 * B_W), do_s.reshape(ns, ln, B_W),
                               lj_s.reshape(ns, ln, B_HEADS), dl_s.reshape(ns, ln, B_HEADS), f"attn_bwd_g{gi}")
        d_pj, d_gain = _qk_prep_bwd(proj_b[gi], *tabs_s[gi], b_q_norm_g[0, gi:gi + 1], b_k_norm_g[0, gi:gi + 1],
                                    dq.reshape(t, B_W), dk.reshape(t, B_W), dv.reshape(t, B_W),
                                    d_z if gi == 0 else None, f"qk_prep_bwd_g{gi}")
        g_wb_in = _matmul(h1_s[gi], d_pj, "tn", F32, f"b_in_dw_g{gi}", tn=B_SUB, tk=2048, into=(g_wb_in, pl.BlockSpec(
            (None, d, B_SUB), lambda i, j, kk, gi=gi: (_b_block(gi, j)[0], i, _b_block(gi, j)[1]))))
        dh = _matmul(d_pj, wb_in, "nt", F32, f"b_in_dx_g{gi}", tm=2048, tk=B_SUB, n=d, b_spec=pl.BlockSpec(
            (None, d, B_SUB), lambda i, j, kk, gi=gi: (_b_block(gi, kk)[0], j, _b_block(gi, kk)[1])))
        d_h1.append(dh if dil == 1 else _from_stream(dh.reshape(ns, ln, d), bn, dil))
        g_qn.append(d_gain[0]), g_kn.append(d_gain[1])
    d_x1, g_norm1 = _rms_bwd(x1, norm_g[1:2], d_h1, d_x2, "rms1_bwd")

    d_x1b = d_x1.astype(BF16)
    g_wa_out = _matmul(og_a.reshape(t, A_VW), d_x1b, "tn", F32, "a_out_dw")
    b_token = b_grads_ready(g_wb_in, g_wb_out, g_wa_out)
    d_og_a = _matmul(d_x1b, wa_out, "nt", F32, "a_out_dx")
    d_pa, d_gc, d_beta, d_cw, d_ng = _gdn_bwd(proj_a3, cw_hm, beta, gc, a_norm_g + b_token, oraw_a, states,
                                              t_mats, conv_y, d_og_a.reshape(bn, s, A_VW))
    d_tail_t, d_alog, d_dtb = _gdn_prep_bwd(tail_t, a_log[0], a_dt_bias[0], d_gc, d_beta)
    d_tail = jnp.swapaxes(d_tail_t.reshape(bn, 2 * A_HEADS, s), 1, 2).reshape(t, 2 * A_HEADS)
    d_tail = jnp.pad(d_tail, ((0, 0), (0, 128 - 2 * A_HEADS))).astype(BF16)
    d_pa = d_pa.reshape(t, A_MAIN)
    g_wa_main = _matmul(h0, d_pa, "tn", F32, "a_in_dw_main")
    g_wa_tail = _matmul(h0, d_tail, "tn", F32, "a_in_dw_tail")
    g_wa_in = jnp.concatenate([_a_cols_from_head_major(g_wa_main), g_wa_tail[:, :2 * A_HEADS]], axis=1)
    a_token = a_grads_ready(g_wa_in)
    d_h0 = _matmul(d_pa, wa_main, "nt", F32, "a_in_dx_main")
    d_h0t = _matmul(d_tail + a_token.astype(BF16), wa_tail, "nt", F32, "a_in_dx_tail")
    d_x0, g_norm0 = _rms_bwd(x0, norm_g[0:1], [d_h0, d_h0t], d_x1, "rms0_bwd")

    gfull = {
        "norm_g": jnp.concatenate([g_norm0, g_norm1], axis=0), "a_w_in": g_wa_in,
        "a_conv_w": _conv_cols_from_head_major(jnp.sum(d_cw, axis=0)),
        "a_log": jnp.sum(d_alog[:, :, 0], axis=0), "a_dt_bias": jnp.sum(d_dtb[:, :, 0], axis=0),
        "a_norm_g": jnp.sum(d_ng[:, :, 0, :], axis=(0, 1)), "a_w_out": g_wa_out, "b_w_in": g_wb_in,
        "b_q_norm_g": jnp.stack(g_qn), "b_k_norm_g": jnp.stack(g_kn), "b_w_out": g_wb_out}
    return loss_local, d_x0.reshape(bn, s, d), gfull


def kernel(x, positions, norm_g, a_w_in, a_conv_w, a_log, a_dt_bias, a_norm_g, a_w_out, b_w_in, b_q_norm_g, b_k_norm_g, b_w_out, loss_target, m_norm_g, m_a_w_in, m_a_conv_w, m_a_log, m_a_dt_bias, m_a_norm_g, m_a_w_out, m_b_w_in, m_b_q_norm_g, m_b_k_norm_g, m_b_w_out, v_norm_g, v_a_w_in, v_a_conv_w, v_a_log, v_a_dt_bias, v_a_norm_g, v_a_w_out, v_b_w_in, v_b_q_norm_g, v_b_k_norm_g, v_b_w_out):
    d = x.shape[2]
    my_c = lax.axis_index("c")
    my_chip = 2 * lax.axis_index("x") + lax.axis_index("y")

    half_index = jnp.reshape(my_c, (1,)).astype(jnp.int32)
    chip_index = jnp.reshape(my_chip, (1,)).astype(jnp.int32)
    def landing(shard):
        return lax.dynamic_update_slice(lax.empty((N_CHIPS,) + shard.shape, shard.dtype), shard[None],
                                        (my_chip,) + (0,) * shard.ndim)

    first_shards = [a_w_in[0].astype(BF16), a_conv_w[0]]
    first_plan = _first_gather_plan()
    first = _split_copy_start("first_weights_start", first_plan, first_shards,
                              [landing(s) for s in first_shards], half_index)
    pending = {}
    late_shards = [(w[0] + first[4][0, 0]).astype(BF16) for w in (a_w_out, b_w_in, b_w_out)]
    late_lands = [landing(s) for s in late_shards]

    def first_weights(after):
        _, (ga_in, g_conv) = _split_copy_wait("first_weights_wait", first_plan, *first[:4],
                                              list(after) + late_lands)
        ga_in = _sibling_forward(ga_in)
        wa_in = jnp.concatenate([ga_in[k] for k in range(N_CHIPS)], axis=1)
        conv_w = jnp.concatenate([g_conv[k] for k in range(N_CHIPS)], axis=1)
        plan = _gather_plan(len(late_shards))
        pending["late"] = (plan,) + tuple(_split_copy_start(
            "late_weights_start", plan, late_shards, late_lands, conv_w))
        return wa_in, conv_w, pending["late"][5][0, 0]

    def late_weights(after):
        plan, send, recv, srcs, lands, _ = pending["late"]
        _, (ga_out, gb_in, gb_out) = _split_copy_wait("late_weights_wait", plan, send, recv, srcs, lands, after)
        return ga_out.reshape(A_VW, d), gb_in, gb_out.reshape(B_W, d)

    def reduce_to_chip_sums(mats, tag):
        recv_sib = _sibling_swap_halves([g.astype(BF16) for g in mats], f"grad_{tag}_sibling_swap")
        return [_pair_sum(g, r, half_index, f"grad_{tag}_pair_sum_{i}") for i, (g, r) in enumerate(zip(mats, recv_sib))]

    def start_exchange(tag, mats):
        sums = reduce_to_chip_sums(mats, tag)
        lands = [lax.empty((N_CHIPS - 1,) + s.shape[1:], BF16) for s in sums]
        plan = _exchange_plan(len(mats))
        pending[tag] = (plan,) + tuple(_split_copy_start(f"grad_{tag}_exchange_start", plan, sums, lands, chip_index))
        return pending[tag][5][0, 0]

    def finish_exchange(tag, after):
        plan, send, recv, srcs, lands, _ = pending[tag]
        return _split_copy_wait(f"grad_{tag}_exchange_wait", plan, send, recv, srcs, lands, after)

    def b_grads_ready(g_wb_in, g_wb_out, g_wa_out):
        return start_exchange("b", [g_wb_in, g_wb_out.reshape(N_CHIPS, -1, d), g_wa_out.reshape(N_CHIPS, -1, d)])

    def a_grads_ready(g_wa_in):
        return start_exchange("a", [_shard_major(g_wa_in, a_w_in.shape[2])])

    loss_local, d_x0, gfull = _local_step(x, positions, loss_target, norm_g, a_log, a_dt_bias, a_norm_g,
                                          b_q_norm_g, b_k_norm_g, first[4][0, 0], first_weights, late_weights,
                                          b_grads_ready, a_grads_ready)

    small = [gfull["norm_g"], gfull["a_conv_w"], gfull["a_log"], gfull["a_dt_bias"], gfull["a_norm_g"],
             gfull["b_q_norm_g"], gfull["b_k_norm_g"], loss_local]
    packed, offs = _pack_rows(small)
    reduced = _small_allreduce(packed)
    g_norm, g_conv_all, g_alog, g_dtb, g_ang, g_q, g_k, loss = _unpack_rows(reduced, offs)
    g_conv_mine = lax.dynamic_slice_in_dim(g_conv_all, my_chip * a_conv_w.shape[2], a_conv_w.shape[2], axis=1)

    b_sums, b_received = finish_exchange("b", d_x0)
    a_sums, a_received = finish_exchange("a", reduced)
    chip_sums = [a_sums[0], b_sums[2], b_sums[0], b_sums[1]]
    received = [a_received[0], b_received[2], b_received[0], b_received[1]]
    halves = [_chip_sum(s, r, chip_index, f"grad_chip_sum_{i}") for i, (s, r) in enumerate(zip(chip_sums, received))]
    theirs = _sibling_swap_whole(halves)
    big = ("a_w_in", "a_w_out", "b_w_in", "b_w_out")
    big_halves = dict(zip(big, zip(halves, theirs)))

    grads = {
        "norm_g": g_norm, "a_conv_w": g_conv_mine[None], "a_log": g_alog[None], "a_dt_bias": g_dtb[None],
        "a_norm_g": g_ang[None], "b_q_norm_g": g_q[None], "b_k_norm_g": g_k[None]}
    weights = {"norm_g": norm_g, "a_w_in": a_w_in, "a_conv_w": a_conv_w, "a_log": a_log, "a_dt_bias": a_dt_bias,
               "a_norm_g": a_norm_g, "a_w_out": a_w_out, "b_w_in": b_w_in, "b_q_norm_g": b_q_norm_g,
               "b_k_norm_g": b_k_norm_g, "b_w_out": b_w_out}
    m_in = {"norm_g": m_norm_g, "a_w_in": m_a_w_in, "a_conv_w": m_a_conv_w, "a_log": m_a_log,
            "a_dt_bias": m_a_dt_bias, "a_norm_g": m_a_norm_g, "a_w_out": m_a_w_out, "b_w_in": m_b_w_in,
            "b_q_norm_g": m_b_q_norm_g, "b_k_norm_g": m_b_k_norm_g, "b_w_out": m_b_w_out}
    v_in = {"norm_g": v_norm_g, "a_w_in": v_a_w_in, "a_conv_w": v_a_conv_w, "a_log": v_a_log,
            "a_dt_bias": v_a_dt_bias, "a_norm_g": v_a_norm_g, "a_w_out": v_a_w_out, "b_w_in": v_b_w_in,
            "b_q_norm_g": v_b_q_norm_g, "b_k_norm_g": v_b_k_norm_g, "b_w_out": v_b_w_out}
    names = list(weights)

    delta_w, new_m, new_v = {}, {}, {}
    for nm in big:
        mine, other = big_halves[nm]
        if weights[nm].shape[2] % 128:
            cols = lambda a: jnp.transpose(a, (2, 0, 1))
            half_cols = lambda a: jnp.transpose(a)[:, None, :]
            outs = _adamw_shard_cols(cols(weights[nm]), half_cols(mine), half_cols(other), cols(m_in[nm]),
                                     cols(v_in[nm]), half_index, f"adamw_{nm}")
            outs = [jnp.transpose(o, (1, 2, 0)) for o in outs]
        else:
            outs = _adamw_shard(weights[nm], mine, other, m_in[nm], v_in[nm], half_index, f"adamw_{nm}")
        grads[nm], delta_w[nm], new_m[nm], new_v[nm] = outs
    small_names = [nm for nm in names if nm not in big]
    packs = [_pack_rows([src[nm] for nm in small_names]) for src in (weights, grads, m_in, v_in)]
    offs = packs[0][1]
    dl, m2, v2 = _adamw(packs[0][0], packs[1][0], packs[2][0], packs[3][0], "adamw_small")
    for nm, a, b, c2 in zip(small_names, _unpack_rows(dl, offs), _unpack_rows(m2, offs), _unpack_rows(v2, offs)):
        delta_w[nm], new_m[nm], new_v[nm] = a, b, c2

    return (loss, d_x0, *[grads[nm] for nm in names], *[delta_w[nm] for nm in names],
            *[new_m[nm] for nm in names], *[new_v[nm] for nm in names])
```

```python
import jax
import jax.numpy as jnp
import numpy as np
from jax import lax
from jax.experimental import pallas as pl
from jax.experimental.pallas import tpu as pltpu

F32 = jnp.float32
BF16 = jnp.bfloat16
MESH = pl.DeviceIdType.MESH

EPS = 1e-6
A_HEADS = 8
A_DK = 128
A_DV = 256
A_QK = A_HEADS * A_DK
A_VW = A_HEADS * A_DV
A_MAIN = 2 * A_QK + 2 * A_VW
A_HEAD_COLS = 2 * A_DK + 2 * A_DV
A_CONV_COLS = 2 * A_DK + A_DV
A_CHUNK = 64
A_CONV = 4
B_GROUPS = 3
B_HEADS = 8
B_DH = 128
B_W = B_HEADS * B_DH
B_DIL = (1, 4, 16)
B_BLK = 128
ROPE_THETA = 500000.0
ROPE_DIMS = B_DH // 4
ADAM_LR, ADAM_B1, ADAM_B2, ADAM_EPS, ADAM_WD, ADAM_STEP = 0.001, 0.9, 0.999, 1e-08, 0.01, 10
N_CHIPS = 4
VMEM_BIG = 56 * 1024 * 1024


def _params(sem=None, vmem=None):
    return pltpu.CompilerParams(dimension_semantics=sem, vmem_limit_bytes=vmem)


def _dot(a, b, ca, cb):
    return lax.dot_general(a.astype(BF16), b.astype(BF16), (((ca,), (cb,)), ((), ())),
                           preferred_element_type=F32)


def _split3(a):
    hi = a.astype(BF16)
    r = a - hi.astype(F32)
    mid = r.astype(BF16)
    lo = (r - mid.astype(F32)).astype(BF16)
    return hi, mid, lo


def _sigmoid(y):
    return 1.0 / (1.0 + jnp.exp(-y))


def _silu(y):
    return y * _sigmoid(y)


def _dsilu(y):
    s = _sigmoid(y)
    return s * (1.0 + y * (1.0 - s))


def _matmul(a, b, mode, out_dtype, name, res=None, tm=1024, tn=1024, tk=1024, n=None, b_spec=None, into=None):
    m, k = a.shape[::-1] if mode == "tn" else a.shape
    if n is None:
        n = b.shape[0] if mode == "nt" else b.shape[1]
    tm, tn, tk = min(tm, m), min(tn, n), min(tk, k)
    assert m % tm == 0 and n % tn == 0 and k % tk == 0, (name, a.shape, b.shape)
    nk = k // tk
    dims = {"nn": ((1,), (0,)), "nt": ((1,), (1,)), "tn": ((0,), (0,))}[mode]

    def body(*refs):
        a_ref, b_ref = refs[0], refs[1]
        r_ref = refs[2] if res is not None else None
        o_ref = refs[2 + (res is not None) + (into is not None)]
        prod = lax.dot_general(a_ref[...], b_ref[...], (dims, ((), ())), preferred_element_type=F32)

        def finish(r):
            if res is not None:
                r = r + r_ref[...]
            o_ref[...] = r.astype(out_dtype)

        if nk == 1:
            finish(prod)
            return
        acc = refs[-1]
        kk = pl.program_id(2)

        @pl.when(kk == 0)
        def _():
            acc[...] = prod

        @pl.when((kk > 0) & (kk < nk - 1))
        def _():
            acc[...] += prod

        @pl.when(kk == nk - 1)
        def _():
            finish(acc[...] + prod)

    a_spec = pl.BlockSpec((tm, tk), lambda i, j, kk: (i, kk))
    if mode == "tn":
        a_spec = pl.BlockSpec((tk, tm), lambda i, j, kk: (kk, i))
    if b_spec is None and mode == "nt":
        b_spec = pl.BlockSpec((tn, tk), lambda i, j, kk: (j, kk))
    elif b_spec is None:
        b_spec = pl.BlockSpec((tk, tn), lambda i, j, kk: (kk, j))
    in_specs = [a_spec, b_spec]
    args = [a, b]
    if res is not None:
        in_specs.append(pl.BlockSpec((tm, tn), lambda i, j, kk: (i, j)))
        args.append(res)
    out_spec = pl.BlockSpec((tm, tn), lambda i, j, kk: (i, j))
    out_shape = jax.ShapeDtypeStruct((m, n), out_dtype)
    aliases = {}
    if into is not None:
        assert res is None
        buf, out_spec = into
        out_shape = jax.ShapeDtypeStruct(buf.shape, buf.dtype)
        in_specs.append(ANY)
        args.append(buf)
        aliases = {2: 0}
    return pl.pallas_call(
        body, name=name, grid=(m // tm, n // tn, nk),
        in_specs=in_specs, out_specs=out_spec, out_shape=out_shape, input_output_aliases=aliases,
        scratch_shapes=[pltpu.VMEM((tm, tn), F32)] if nk > 1 else [],
        compiler_params=_params(("parallel", "parallel", "arbitrary"), 48 * 1024 * 1024),
    )(*args)


def _rms_fwd(x, g, name, tm=256):
    t, d = x.shape

    def body(x_ref, g_ref, h_ref):
        xv = x_ref[...]
        r = lax.rsqrt(jnp.mean(xv * xv, axis=-1, keepdims=True) + EPS)
        h_ref[...] = (xv * r * g_ref[...]).astype(BF16)

    return pl.pallas_call(
        body, name=name, grid=(t // tm,),
        in_specs=[pl.BlockSpec((tm, d), lambda i: (i, 0)), pl.BlockSpec((1, d), lambda i: (0, 0))],
        out_specs=pl.BlockSpec((tm, d), lambda i: (i, 0)),
        out_shape=jax.ShapeDtypeStruct((t, d), BF16),
        compiler_params=_params(("parallel",)),
    )(x, g)


def _rms_bwd(x, g, dhs, dres, name, tm=256):
    t, d = x.shape
    n_dh = len(dhs)

    def body(*refs):
        x_ref, g_ref = refs[0], refs[1]
        dh_refs = refs[2:2 + n_dh]
        dres_ref, dx_ref, dg_ref = refs[2 + n_dh:]
        i = pl.program_id(0)

        @pl.when(i == 0)
        def _():
            dg_ref[...] = jnp.zeros_like(dg_ref)

        xv = x_ref[...]
        r = lax.rsqrt(jnp.mean(xv * xv, axis=-1, keepdims=True) + EPS)
        xh = xv * r
        dh = dh_refs[0][...]
        for ref in dh_refs[1:]:
            dh = dh + ref[...]
        dg_ref[0:1, :] += jnp.sum(dh * xh, axis=0, keepdims=True)
        dxh = dh * g_ref[...]
        dx = r * (dxh - xh * jnp.mean(dxh * xh, axis=-1, keepdims=True))
        dx_ref[...] = dx + dres_ref[...]

    row = pl.BlockSpec((tm, d), lambda i: (i, 0))
    dx, dg = pl.pallas_call(
        body, name=name, grid=(t // tm,),
        in_specs=[row, pl.BlockSpec((1, d), lambda i: (0, 0))] + [row] * n_dh + [row],
        out_specs=[row, pl.BlockSpec((8, d), lambda i: (0, 0))],
        out_shape=[jax.ShapeDtypeStruct((t, d), F32), jax.ShapeDtypeStruct((8, d), F32)],
        compiler_params=_params(("arbitrary",)),
    )(x, g, *dhs, dres)
    return dx, dg[0:1]


def _loss_grad(y, target, name="loss_grad", tm=256):
    t, d = y.shape
    nb = t // tm

    def body(y_ref, t_ref, dy_ref, part_ref):
        e = y_ref[...] - t_ref[...]
        dy_ref[...] = e * (1.0 / d)
        s = jnp.sum(jnp.sum(e * e, axis=1, keepdims=True), axis=0, keepdims=True) * (0.5 / d)
        part_ref[...] = jnp.broadcast_to(s, (8, 128))

    row = pl.BlockSpec((tm, d), lambda i: (i, 0))
    dy, part = pl.pallas_call(
        body, name=name, grid=(nb,), in_specs=[row, row],
        out_specs=[row, pl.BlockSpec((None, 8, 128), lambda i: (i, 0, 0))],
        out_shape=[jax.ShapeDtypeStruct((t, d), F32), jax.ShapeDtypeStruct((nb, 8, 128), F32)],
        compiler_params=_params(("parallel",)),
    )(y, target)
    return dy, part[:, 0, 0]


def _softplus(x):
    t = jnp.exp(-jnp.abs(x))
    return jnp.maximum(x, 0.0) + jnp.where(t < 1e-3, t * (1.0 - 0.5 * t), jnp.log(1.0 + t))


def _tri(rows_le_cols):
    r = lax.broadcasted_iota(jnp.int32, (A_CHUNK, A_CHUNK), 0)
    c = lax.broadcasted_iota(jnp.int32, (A_CHUNK, A_CHUNK), 1)
    return jnp.where((r <= c) if rows_le_cols else (r >= c), 1.0, 0.0).astype(BF16)


def _dot_exact_rhs(a, ones_bf16):
    dn = (((1,), (0,)), ((), ()))
    hi, mid, lo = _split3(a)
    out = lax.dot_general(hi, ones_bf16, dn, preferred_element_type=F32)
    out = out + lax.dot_general(mid, ones_bf16, dn, preferred_element_type=F32)
    return out + lax.dot_general(lo, ones_bf16, dn, preferred_element_type=F32)


def _gdn_prep(tail_t, a_log, dt_bias):
    bn, _, n, c = tail_t.shape

    def body(t_ref, alog_ref, dtb_ref, beta_ref, gc_ref):
        upper = _tri(True)
        for h in range(A_HEADS):
            beta_ref[h] = _sigmoid(t_ref[h])
            ea = jnp.exp(jnp.full((n, c), alog_ref[h], F32))
            g = -ea * _softplus(t_ref[A_HEADS + h] + dtb_ref[h])
            gc_ref[h] = _dot_exact_rhs(g, upper)

    smem = pl.BlockSpec(memory_space=pltpu.SMEM)
    blk = pl.BlockSpec((None, A_HEADS, n, c), lambda b: (b, 0, 0, 0))
    return pl.pallas_call(
        body, name="gdn_prep", grid=(bn,),
        in_specs=[pl.BlockSpec((None, 2 * A_HEADS, n, c), lambda b: (b, 0, 0, 0)), smem, smem],
        out_specs=[blk, blk],
        out_shape=[jax.ShapeDtypeStruct((bn, A_HEADS, n, c), F32)] * 2,
        compiler_params=_params(("parallel",)),
    )(tail_t, a_log, dt_bias)


def _gdn_prep_bwd(tail_t, a_log, dt_bias, d_gc, d_beta):
    bn, _, n, c = tail_t.shape

    def body(t_ref, alog_ref, dtb_ref, dgc_ref, dbeta_ref, dt_ref, dal_ref, ddt_ref):
        lower = _tri(False)
        for h in range(A_HEADS):
            beta = _sigmoid(t_ref[h])
            dt_ref[h] = dbeta_ref[h] * beta * (1.0 - beta)
            dg = _dot_exact_rhs(dgc_ref[h], lower)
            ea = jnp.exp(jnp.full((n, c), alog_ref[h], F32))
            xa = t_ref[A_HEADS + h] + dtb_ref[h]
            g = -ea * _softplus(xa)
            dxa = -ea * dg * _sigmoid(xa)
            dt_ref[A_HEADS + h] = dxa
            s1 = jnp.sum(jnp.sum(g * dg, axis=1, keepdims=True), axis=0, keepdims=True)
            s2 = jnp.sum(jnp.sum(dxa, axis=1, keepdims=True), axis=0, keepdims=True)
            dal_ref[h:h + 1, :] = jnp.broadcast_to(s1, (1, 128))
            ddt_ref[h:h + 1, :] = jnp.broadcast_to(s2, (1, 128))

    smem = pl.BlockSpec(memory_space=pltpu.SMEM)
    blk8 = pl.BlockSpec((None, A_HEADS, n, c), lambda b: (b, 0, 0, 0))
    blk16 = pl.BlockSpec((None, 2 * A_HEADS, n, c), lambda b: (b, 0, 0, 0))
    sm = pl.BlockSpec((None, A_HEADS, 128), lambda b: (b, 0, 0))
    return pl.pallas_call(
        body, name="gdn_prep_bwd", grid=(bn,),
        in_specs=[blk16, smem, smem, blk8, blk8],
        out_specs=[blk16, sm, sm],
        out_shape=[jax.ShapeDtypeStruct((bn, 2 * A_HEADS, n, c), F32),
                   jax.ShapeDtypeStruct((bn, A_HEADS, 128), F32),
                   jax.ShapeDtypeStruct((bn, A_HEADS, 128), F32)],
        compiler_params=_params(("parallel",)),
    )(tail_t, a_log, dt_bias, d_gc, d_beta)


HALO = 8


def _conv_taps(xw, w):
    y = w[A_CONV - 1:A_CONV, :] * xw
    for j in range(1, A_CONV):
        y = y + w[A_CONV - 1 - j:A_CONV - j, :] * pltpu.roll(xw, j, 0)
    return y[HALO:, :]


def _row_to_col(row, eye):
    c = eye.shape[0]
    return jnp.sum(jnp.where(eye, jnp.broadcast_to(row, (c, c)), 0.0), axis=1, keepdims=True)


def _col_to_row(col, eye):
    c = eye.shape[0]
    return jnp.sum(jnp.where(eye, jnp.broadcast_to(col, (c, c)), 0.0), axis=0, keepdims=True)


def _unit_lower_inverse(a, ri, ci):
    eye = jnp.where(ri == ci, 1.0, 0.0)
    a8 = jnp.where((ri >> 3) == (ci >> 3), a, 0.0)
    a2 = _dot(a8, a8, 1, 0)
    yield
    a4 = _dot(a2, a2, 1, 0)
    t = eye - a8
    t = t + _dot(t, a2, 1, 0)
    yield
    t = t + _dot(t, a4, 1, 0)
    yield
    for sh in (3, 4, 5):
        off = jnp.where(((ri >> (sh + 1)) == (ci >> (sh + 1))) & ((ri >> sh) != (ci >> sh)), a, 0.0)
        left = _dot(t, off, 1, 0)
        yield
        t = t - _dot(left, t, 1, 0)
        yield
    return t


def _round_robin(gens):
    live = list(gens)
    while live:
        nxt = []
        for g in live:
            try:
                next(g)
                nxt.append(g)
            except StopIteration:
                pass
        live = nxt


def _gdn_chunk_core(q, k, v, g_row, b_row, t_mat, ri, ci):
    eye = ri == ci
    g_col = _row_to_col(g_row, eye)
    b_col = _row_to_col(b_row, eye)
    causal = ri >= ci
    strict = ri > ci
    dec = jnp.where(causal, jnp.exp(jnp.where(causal, g_col - g_row, 0.0)), 0.0)
    gam = jnp.exp(g_col)
    g_last = g_row[:, A_CHUNK - 1:A_CHUNK]
    gam_last = jnp.exp(g_last)
    e = jnp.exp(g_last - g_col)
    kb = k * b_col
    bv = v * b_col
    kbg = kb * gam
    q16, k16, kb16 = q.astype(BF16), k.astype(BF16), kb.astype(BF16)
    kk = _dot(kb16, k16, 1, 1)
    p = _dot(q16, k16, 1, 1) * dec
    yield
    a_mat = jnp.where(strict, kk * dec, 0.0)
    if t_mat is None:
        t_mat = yield from _unit_lower_inverse(a_mat, ri, ci)
    t16 = t_mat.astype(BF16)
    u = _dot(t16, bv, 1, 0)
    w = _dot(t16, kbg, 1, 0)
    yield
    return dict(eye=eye, g_col=g_col, b_col=b_col, dec=dec, strict=strict, causal=causal, gam=gam,
                gam_last=gam_last, e=e, kb=kb, bv=bv, kbg=kbg, a_mat=a_mat, t_mat=t_mat, u=u, w=w, p=p,
                qg=q * gam, kd=k * e, q16=q16, k16=k16, kb16=kb16, t16=t16)


A_SEQ_BLK = 256
A_BLK_CHUNKS = A_SEQ_BLK // A_CHUNK


def _gdn_halo(proj_hm):
    bn, s, w = proj_hm.shape
    last = proj_hm.reshape(bn, s // A_SEQ_BLK, A_SEQ_BLK, w)[:, :, A_SEQ_BLK - HALO:, :]
    return jnp.concatenate([jnp.zeros((bn, 1, HALO, w), proj_hm.dtype), last[:, :-1]], axis=1)


def _gdn_window(x_ref, halo_ref, ci, first, lo):
    if first:
        return jnp.concatenate([halo_ref[:, lo:lo + A_CONV_COLS], x_ref[0:A_CHUNK, lo:lo + A_CONV_COLS]], axis=0)
    start = pl.multiple_of(ci * A_CHUNK - HALO, HALO)
    return x_ref[pl.ds(start, A_CHUNK + HALO), lo:lo + A_CONV_COLS]


def _gdn_chunk_prep(xw, cw, y=None):
    if y is None:
        y = _conv_taps(xw, cw)
    a = _silu(y)
    aq, ak, v = a[:, 0:A_DK], a[:, A_DK:2 * A_DK], a[:, 2 * A_DK:]
    rq = lax.rsqrt(jnp.sum(aq * aq, axis=1, keepdims=True) + EPS)
    rk = lax.rsqrt(jnp.sum(ak * ak, axis=1, keepdims=True) + EPS)
    return dict(xw=xw, y=y, aq=aq, ak=ak, rq=rq, rk=rk, q=aq * rq * (A_DK ** -0.5), k=ak * rk, v=v)


def _gdn_fwd(proj_hm, cw_hm, beta, gc, norm_g, hp=8):
    bn, s, _ = proj_hm.shape
    n = s // A_CHUNK
    nsb = s // A_SEQ_BLK
    halo = _gdn_halo(proj_hm)

    def body(x_ref, halo_ref, cw_ref, beta_ref, gc_ref, ng_ref, og_ref, oraw_ref, st_ref, t_ref, y_ref, state):
        first_chunk = pl.program_id(2) * A_BLK_CHUNKS
        ri = lax.broadcasted_iota(jnp.int32, (A_CHUNK, A_CHUNK), 0)
        ci_ = lax.broadcasted_iota(jnp.int32, (A_CHUNK, A_CHUNK), 1)
        ng = ng_ref[...]

        @pl.when(pl.program_id(2) == 0)
        def _():
            state[...] = jnp.zeros_like(state)

        def one_head(hh, ci, first, rows):
            lo = hh * A_HEAD_COLS
            cw = cw_ref[:, hh * A_CONV_COLS:(hh + 1) * A_CONV_COLS]
            cin = _gdn_chunk_prep(_gdn_window(x_ref, halo_ref, ci, first, lo), cw)
            y_ref[rows, hh * A_CONV_COLS:(hh + 1) * A_CONV_COLS] = cin["y"]
            seq_chunk = pl.ds(first_chunk + ci, 1)
            core = yield from _gdn_chunk_core(cin["q"], cin["k"], cin["v"], gc_ref[hh, seq_chunk, :],
                                              beta_ref[hh, seq_chunk, :], None, ri, ci_)
            st = state[hh]
            st_ref[hh, ci] = st
            t_ref[hh, ci] = core["t_mat"]
            st16 = st.astype(BF16)
            vn = core["u"] - _dot(core["w"], st16, 1, 0)
            qs = _dot(core["qg"], st16, 1, 0)
            yield
            vn16 = vn.astype(BF16)
            o = qs + _dot(core["p"], vn16, 1, 0)
            state[hh] = st * core["gam_last"] + _dot(core["kd"], vn16, 0, 0)
            yield
            ocols = slice(hh * A_DV, (hh + 1) * A_DV)
            oraw_ref[rows, ocols] = o
            r = lax.rsqrt(jnp.mean(o * o, axis=1, keepdims=True) + EPS)
            z = x_ref[rows, lo + A_CONV_COLS:lo + A_HEAD_COLS]
            og_ref[rows, ocols] = (o * r * ng * _silu(z)).astype(BF16)

        def chunk(ci, first):
            rows = pl.ds(0 if first else pl.multiple_of(ci * A_CHUNK, A_CHUNK), A_CHUNK)
            _round_robin([one_head(hh, ci, first, rows) for hh in range(hp)])

        chunk(0, True)
        lax.fori_loop(1, A_BLK_CHUNKS, lambda i, c: (chunk(i, False), c)[1], 0)

    small = pl.BlockSpec((None, hp, n, A_CHUNK), lambda b, h, j: (b, h, 0, 0))
    return pl.pallas_call(
        body, name="gdn_fwd", grid=(bn, A_HEADS // hp, nsb),
        in_specs=[pl.BlockSpec((None, A_SEQ_BLK, hp * A_HEAD_COLS), lambda b, h, j: (b, j, h)),
                  pl.BlockSpec((None, None, HALO, hp * A_HEAD_COLS), lambda b, h, j: (b, j, 0, h)),
                  pl.BlockSpec((A_CONV, hp * A_CONV_COLS), lambda b, h, j: (0, h)),
                  small, small,
                  pl.BlockSpec((1, A_DV), lambda b, h, j: (0, 0))],
        out_specs=[pl.BlockSpec((None, A_SEQ_BLK, hp * A_DV), lambda b, h, j: (b, j, h)),
                   pl.BlockSpec((None, A_SEQ_BLK, hp * A_DV), lambda b, h, j: (b, j, h)),
                   pl.BlockSpec((None, hp, A_BLK_CHUNKS, A_DK, A_DV), lambda b, h, j: (b, h, j, 0, 0)),
                   pl.BlockSpec((None, hp, A_BLK_CHUNKS, A_CHUNK, A_CHUNK), lambda b, h, j: (b, h, j, 0, 0)),
                   pl.BlockSpec((None, A_SEQ_BLK, hp * A_CONV_COLS), lambda b, h, j: (b, j, h))],
        out_shape=[jax.ShapeDtypeStruct((bn, s, A_VW), BF16),
                   jax.ShapeDtypeStruct((bn, s, A_VW), F32),
                   jax.ShapeDtypeStruct((bn, A_HEADS, n, A_DK, A_DV), F32),
                   jax.ShapeDtypeStruct((bn, A_HEADS, n, A_CHUNK, A_CHUNK), F32),
                   jax.ShapeDtypeStruct((bn, s, A_HEADS * A_CONV_COLS), F32)],
        scratch_shapes=[pltpu.VMEM((hp, A_DK, A_DV), F32)],
        compiler_params=_params(("parallel", "parallel", "arbitrary"), VMEM_BIG),
    )(proj_hm, halo, cw_hm, beta, gc, norm_g)


def _gdn_bwd(proj_hm, cw_hm, beta, gc, norm_g, oraw, states, t_mats, conv_y, dog, hp=4):
    bn, s, _ = proj_hm.shape
    n = s // A_CHUNK
    nsb = s // A_SEQ_BLK
    halo = _gdn_halo(proj_hm)

    def body(x_ref, halo_ref, cw_ref, beta_ref, gc_ref, ng_ref, oraw_ref, st_ref, t_ref, y_ref, dog_ref,
             dx_ref, dgc_ref, dbeta_ref, dcw_ref, dng_ref, dstate, dy_next, shifted):
        first_chunk = (nsb - 1 - pl.program_id(2)) * A_BLK_CHUNKS
        ri = lax.broadcasted_iota(jnp.int32, (A_CHUNK, A_CHUNK), 0)
        ci_ = lax.broadcasted_iota(jnp.int32, (A_CHUNK, A_CHUNK), 1)
        lane = lax.broadcasted_iota(jnp.int32, (1, A_CHUNK), 1)
        ng = ng_ref[...]

        @pl.when(pl.program_id(2) == 0)
        def _():
            dstate[...] = jnp.zeros_like(dstate)
            dy_next[...] = jnp.zeros_like(dy_next)
            dcw_ref[...] = jnp.zeros_like(dcw_ref)
            dng_ref[...] = jnp.zeros_like(dng_ref)

        def one_head(hh, ci, first, rows):
            lo = hh * A_HEAD_COLS
            ccols = slice(hh * A_CONV_COLS, (hh + 1) * A_CONV_COLS)
            ocols = slice(hh * A_DV, (hh + 1) * A_DV)
            cw = cw_ref[:, ccols]
            cin = _gdn_chunk_prep(_gdn_window(x_ref, halo_ref, ci, first, lo), cw, y_ref[rows, ccols])
            q, k, v = cin["q"], cin["k"], cin["v"]
            seq_chunk = pl.ds(first_chunk + ci, 1)
            cr = yield from _gdn_chunk_core(q, k, v, gc_ref[hh, seq_chunk, :], beta_ref[hh, seq_chunk, :],
                                            t_ref[hh, ci], ri, ci_)
            eye, dec, gam, e = cr["eye"], cr["dec"], cr["gam"], cr["e"]
            b_col, t_mat, u, w, p = cr["b_col"], cr["t_mat"], cr["u"], cr["w"], cr["p"]
            st = st_ref[hh, ci]
            ds_out = dstate[hh]

            o = oraw_ref[rows, ocols]
            z = x_ref[rows, lo + A_CONV_COLS:lo + A_HEAD_COLS]
            d_og = dog_ref[rows, ocols]
            r = lax.rsqrt(jnp.mean(o * o, axis=1, keepdims=True) + EPS)
            oh = o * r
            d_on = d_og * _silu(z)
            dz = d_og * oh * ng * _dsilu(z)
            dng_ref[hh, 0:1, :] += jnp.sum(d_on * oh, axis=0, keepdims=True)
            d_oh = d_on * ng
            d_o = r * (d_oh - oh * jnp.mean(d_oh * oh, axis=1, keepdims=True))

            st16, ds16, do16, w16 = st.astype(BF16), ds_out.astype(BF16), d_o.astype(BF16), w.astype(BF16)
            q16, k16, t16 = cr["q16"], cr["k16"], cr["t16"]
            vn = u - _dot(w16, st16, 1, 0)
            d_vn = _dot(p, do16, 0, 0) + _dot(cr["kd"], ds16, 1, 0)
            d_qg = _dot(do16, st16, 1, 1)
            qgdo = _dot(cr["qg"], do16, 0, 0)
            yield
            vn16, dvn16 = vn.astype(BF16), d_vn.astype(BF16)
            d_p = jnp.where(cr["causal"], _dot(do16, vn16, 1, 1), 0.0)
            d_kd = _dot(vn16, ds16, 1, 1)
            d_gam_last = jnp.sum(jnp.sum(st * ds_out, axis=1, keepdims=True), axis=0, keepdims=True)
            d_w = -_dot(dvn16, st16, 1, 1)
            dstate[hh] = qgdo + ds_out * cr["gam_last"] - _dot(w16, dvn16, 0, 0)
            d_bv = _dot(t16, dvn16, 0, 0)
            yield
            d_kbg = _dot(t16, d_w, 0, 0)
            n_p = (d_p * dec).astype(BF16)
            d_q = _dot(n_p, k16, 1, 0) + d_qg * gam
            npq = _dot(n_p, q16, 0, 0)
            yield
            d_a = jnp.where(cr["strict"], -(_dot(d_bv, u, 1, 1) + _dot(d_kbg, w16, 1, 1)), 0.0)
            yield
            m_a = (d_a * dec).astype(BF16)
            d_kb = _dot(m_a, k16, 1, 0) + d_kbg * gam
            d_k = (_dot(m_a, cr["kb16"], 0, 0) + npq + d_kd * e + d_kb * b_col)
            yield
            d_v = d_bv * b_col
            d_beta_col = (jnp.sum(d_bv * v, axis=1, keepdims=True)
                          + jnp.sum(d_kb * k, axis=1, keepdims=True))
            gterm = d_a * cr["a_mat"] + d_p * p
            d_e = jnp.sum(d_kd * k, axis=1, keepdims=True) * e
            d_g_col = (jnp.sum(gterm, axis=1, keepdims=True)
                       + (jnp.sum(d_qg * q, axis=1, keepdims=True)
                          + jnp.sum(d_kbg * cr["kb"], axis=1, keepdims=True)) * gam
                       - d_e)
            d_g_last = jnp.sum(d_e, axis=0, keepdims=True) + d_gam_last * cr["gam_last"]
            d_g_row = (_col_to_row(d_g_col, eye) - jnp.sum(gterm, axis=0, keepdims=True)
                       + jnp.where(lane == A_CHUNK - 1, d_g_last, 0.0))
            dgc_ref[hh, seq_chunk, :] = d_g_row
            dbeta_ref[hh, seq_chunk, :] = _col_to_row(d_beta_col, eye)

            qh = cin["aq"] * cin["rq"]
            kh = cin["ak"] * cin["rk"]
            d_qh = d_q * (A_DK ** -0.5)
            d_aq = cin["rq"] * (d_qh - qh * jnp.sum(d_qh * qh, axis=1, keepdims=True))
            d_ak = cin["rk"] * (d_k - kh * jnp.sum(d_k * kh, axis=1, keepdims=True))
            d_y = jnp.concatenate([d_aq, d_ak, d_v], axis=1) * _dsilu(cin["y"])
            shifted[hh, 0, 0:A_CHUNK, :] = d_y
            shifted[hh, 0, A_CHUNK:A_CHUNK + HALO, :] = dy_next[hh]
            shifted[hh, 1, 0:A_CHUNK + HALO, :] = cin["xw"]
            d_x = cw[A_CONV - 1:A_CONV, :] * d_y
            for j in range(1, A_CONV):
                d_x = d_x + cw[A_CONV - 1 - j:A_CONV - j, :] * shifted[hh, 0, j:j + A_CHUNK, :]
            for j in range(A_CONV):
                xs = shifted[hh, 1, HALO - j:HALO - j + A_CHUNK, :]
                dcw_ref[A_CONV - 1 - j:A_CONV - j, ccols] += jnp.sum(d_y * xs, axis=0, keepdims=True)
            dy_next[hh] = d_y[0:HALO, :]
            dx_ref[rows, lo:lo + A_CONV_COLS] = d_x.astype(BF16)
            dx_ref[rows, lo + A_CONV_COLS:lo + A_HEAD_COLS] = dz.astype(BF16)

        def chunk(ci, first):
            rows = pl.ds(0 if first else pl.multiple_of(ci * A_CHUNK, A_CHUNK), A_CHUNK)
            _round_robin([one_head(hh, ci, first, rows) for hh in range(hp)])

        lax.fori_loop(0, A_BLK_CHUNKS - 1, lambda i, c: (chunk(A_BLK_CHUNKS - 1 - i, False), c)[1], 0)
        chunk(0, True)

    rev = lambda j: nsb - 1 - j
    small = pl.BlockSpec((None, hp, n, A_CHUNK), lambda b, h, j: (b, h, 0, 0))
    wide = pl.BlockSpec((None, A_SEQ_BLK, hp * A_HEAD_COLS), lambda b, h, j: (b, rev(j), h))
    val = pl.BlockSpec((None, A_SEQ_BLK, hp * A_DV), lambda b, h, j: (b, rev(j), h))
    return pl.pallas_call(
        body, name="gdn_bwd", grid=(bn, A_HEADS // hp, nsb),
        in_specs=[wide,
                  pl.BlockSpec((None, None, HALO, hp * A_HEAD_COLS), lambda b, h, j: (b, rev(j), 0, h)),
                  pl.BlockSpec((A_CONV, hp * A_CONV_COLS), lambda b, h, j: (0, h)),
                  small, small,
                  pl.BlockSpec((1, A_DV), lambda b, h, j: (0, 0)),
                  val,
                  pl.BlockSpec((None, hp, A_BLK_CHUNKS, A_DK, A_DV), lambda b, h, j: (b, h, rev(j), 0, 0)),
                  pl.BlockSpec((None, hp, A_BLK_CHUNKS, A_CHUNK, A_CHUNK), lambda b, h, j: (b, h, rev(j), 0, 0)),
                  pl.BlockSpec((None, A_SEQ_BLK, hp * A_CONV_COLS), lambda b, h, j: (b, rev(j), h)),
                  val],
        out_specs=[wide, small, small,
                   pl.BlockSpec((None, A_CONV, hp * A_CONV_COLS), lambda b, h, j: (b, 0, h)),
                   pl.BlockSpec((None, hp, 8, A_DV), lambda b, h, j: (b, h, 0, 0))],
        out_shape=[jax.ShapeDtypeStruct((bn, s, A_HEADS * A_HEAD_COLS), BF16),
                   jax.ShapeDtypeStruct((bn, A_HEADS, n, A_CHUNK), F32),
                   jax.ShapeDtypeStruct((bn, A_HEADS, n, A_CHUNK), F32),
                   jax.ShapeDtypeStruct((bn, A_CONV, A_HEADS * A_CONV_COLS), F32),
                   jax.ShapeDtypeStruct((bn, A_HEADS, 8, A_DV), F32)],
        scratch_shapes=[pltpu.VMEM((hp, A_DK, A_DV), F32), pltpu.VMEM((hp, HALO, A_CONV_COLS), F32),
                        pltpu.VMEM((hp, 2, A_CHUNK + 2 * HALO, A_CONV_COLS), F32)],
        compiler_params=_params(("parallel", "parallel", "arbitrary"), VMEM_BIG),
    )(proj_hm, halo, cw_hm, beta, gc, norm_g, oraw, states, t_mats, conv_y, dog)


def _rope_tables(posf, inv_freq_row):
    t = posf.shape[0]
    tm = 512

    def body(p_ref, f_ref, c_ref, sa_ref, sb_ref):
        ang = p_ref[...] * f_ref[...]
        lane = lax.broadcasted_iota(jnp.int32, ang.shape, 1)
        half = ROPE_DIMS // 2
        c_ref[...] = jnp.where(lane < ROPE_DIMS, jnp.cos(ang), 1.0)
        sn = jnp.sin(ang)
        sa_ref[...] = jnp.where(lane < half, -sn, 0.0)
        sb_ref[...] = jnp.where((lane >= half) & (lane < ROPE_DIMS), sn, 0.0)

    row = pl.BlockSpec((tm, 128), lambda i: (i, 0))
    return pl.pallas_call(
        body, name="rope_tables", grid=(t // tm,),
        in_specs=[row, pl.BlockSpec((1, 128), lambda i: (0, 0))], out_specs=[row] * 3,
        out_shape=[jax.ShapeDtypeStruct((t, 128), F32)] * 3,
        compiler_params=_params(("parallel",)),
    )(posf, inv_freq_row)


def _qk_prep(proj, c, sa, sb, qg, kg, name, tm=256):
    t = proj.shape[0]

    def body(x_ref, c_ref, sa_ref, sb_ref, qg_ref, kg_ref, o_ref):
        cc, s1, s2 = c_ref[...], sa_ref[...], sb_ref[...]
        half = ROPE_DIMS // 2

        def one_head(lo, g):
            xv = x_ref[:, lo:lo + B_DH]
            ms = jnp.mean(xv * xv, axis=1, keepdims=True)
            yield
            xn = xv * lax.rsqrt(ms + EPS) * g
            r1, r2 = pltpu.roll(xn, 128 - half, 1), pltpu.roll(xn, half, 1)
            yield
            o_ref[:, lo:lo + B_DH] = (xn * cc + r1 * s1 + r2 * s2).astype(BF16)

        for which, g_ref in ((0, qg_ref), (1, kg_ref)):
            g = g_ref[...]
            _round_robin([one_head(which * B_W + h * B_DH, g) for h in range(B_HEADS)])
        o_ref[:, 2 * B_W:3 * B_W] = x_ref[:, 2 * B_W:3 * B_W].astype(BF16)

    tab = pl.BlockSpec((tm, 128), lambda i: (i, 0))
    gain = pl.BlockSpec((1, B_DH), lambda i: (0, 0))
    return pl.pallas_call(
        body, name=name, grid=(t // tm,),
        in_specs=[pl.BlockSpec((tm, 3 * B_W), lambda i: (i, 0)), tab, tab, tab, gain, gain],
        out_specs=pl.BlockSpec((tm, 3 * B_W), lambda i: (i, 0)),
        out_shape=jax.ShapeDtypeStruct((t, 3 * B_W), BF16),
        compiler_params=_params(("parallel",), 40 * 1024 * 1024),
    )(proj, c, sa, sb, qg, kg)


def _qk_prep_bwd(proj, c, sa, sb, qg, kg, dq, dk, dv, dz, name, tm=256):
    t = proj.shape[0]
    out_w = 3 * B_W + (B_W if dz is not None else 0)

    def body(*refs):
        x_ref, c_ref, sa_ref, sb_ref, qg_ref, kg_ref, dq_ref, dk_ref, dv_ref = refs[:9]
        if dz is not None:
            dz_ref, o_ref, dgain_ref = refs[9:]
        else:
            o_ref, dgain_ref = refs[9:]
        i = pl.program_id(0)

        @pl.when(i == 0)
        def _():
            dgain_ref[...] = jnp.zeros_like(dgain_ref)

        cc, s1, s2 = c_ref[...], sa_ref[...], sb_ref[...]
        half = ROPE_DIMS // 2

        def one_head(which, h, g, d_ref, parts):
            lo = which * B_W + h * B_DH
            xv = x_ref[:, lo:lo + B_DH]
            d_out = d_ref[:, h * B_DH:(h + 1) * B_DH].astype(F32)
            ms = jnp.mean(xv * xv, axis=1, keepdims=True)
            r1, r2 = pltpu.roll(d_out * s1, half, 1), pltpu.roll(d_out * s2, 128 - half, 1)
            yield
            r = lax.rsqrt(ms + EPS)
            xh = xv * r
            d_xn = d_out * cc + r1 + r2
            parts.append(jnp.sum(d_xn * xh, axis=0, keepdims=True))
            d_xh = d_xn * g
            dot = jnp.mean(d_xh * xh, axis=1, keepdims=True)
            yield
            o_ref[:, lo:lo + B_DH] = (r * (d_xh - xh * dot)).astype(BF16)

        for which, g_ref, d_ref in ((0, qg_ref, dq_ref), (1, kg_ref, dk_ref)):
            parts = []
            _round_robin([one_head(which, h, g_ref[...], d_ref, parts) for h in range(B_HEADS)])
            acc = parts[0]
            for part in parts[1:]:
                acc = acc + part
            dgain_ref[which:which + 1, :] += acc
        o_ref[:, 2 * B_W:3 * B_W] = dv_ref[...]
        if dz is not None:
            o_ref[:, 3 * B_W:4 * B_W] = dz_ref[...]

    tab = pl.BlockSpec((tm, 128), lambda i: (i, 0))
    gain = pl.BlockSpec((1, B_DH), lambda i: (0, 0))
    grad = pl.BlockSpec((tm, B_W), lambda i: (i, 0))
    in_specs = [pl.BlockSpec((tm, 2 * B_W), lambda i: (i, 0)), tab, tab, tab, gain, gain, grad, grad, grad]
    args = [proj, c, sa, sb, qg, kg, dq, dk, dv]
    if dz is not None:
        in_specs.append(grad)
        args.append(dz)
    return pl.pallas_call(
        body, name=name, grid=(t // tm,), in_specs=in_specs,
        out_specs=[pl.BlockSpec((tm, out_w), lambda i: (i, 0)), pl.BlockSpec((8, B_DH), lambda i: (0, 0))],
        out_shape=[jax.ShapeDtypeStruct((t, out_w), BF16), jax.ShapeDtypeStruct((8, B_DH), F32)],
        compiler_params=_params(("arbitrary",), 40 * 1024 * 1024),
    )(*args)


def _attn_masks():
    qi = lax.broadcasted_iota(jnp.int32, (B_BLK, 2 * B_BLK), 0)
    kj = lax.broadcasted_iota(jnp.int32, (B_BLK, 2 * B_BLK), 1)
    two = (kj >= qi) & (kj <= qi + B_BLK)
    q1 = lax.broadcasted_iota(jnp.int32, (B_BLK, B_BLK), 0)
    k1 = lax.broadcasted_iota(jnp.int32, (B_BLK, B_BLK), 1)
    return k1 <= q1, two


def _lane_pick(ref_rows, h):
    lane = lax.broadcasted_iota(jnp.int32, ref_rows.shape, 1)
    return jnp.sum(jnp.where(lane == h, ref_rows, 0.0), axis=1, keepdims=True)


B_ROWS = 2048


def _attn_schedule(nb, sb, block):
    way = 8

    def run(items):
        for at in range(0, len(items), way):
            _round_robin([block(*it) for it in items[at:at + way]])

    run([(si, 0, True) for si in range(sb)])
    if nb == 1:
        return
    per = max(1, way // sb)
    lead = 1 + (nb - 1) % per
    if lead > 1:
        run([(si, i, False) for i in range(1, lead) for si in range(sb)])

    def step(it, carry):
        run([(si, lead + it * per + u, False) for u in range(per) for si in range(sb)])
        return carry

    lax.fori_loop(0, (nb - lead) // per, step, 0)


def _attn_rows(i, first):
    if first:
        return pl.ds(0, B_BLK), pl.ds(0, B_BLK)
    rows = pl.ds(pl.multiple_of(i * B_BLK, B_BLK), B_BLK)
    return rows, pl.ds(pl.multiple_of((i - 1) * B_BLK, B_BLK), 2 * B_BLK)


def _attn_fwd(qkv, name):
    ns, ln, _ = qkv.shape
    nb = ln // B_BLK
    sb = B_ROWS // ln
    scale = B_DH ** -0.5

    def body(q_ref, k_ref, v_ref, o_ref, lse_ref):
        h = pl.program_id(1)
        mask1, mask2 = _attn_masks()
        lane = lax.broadcasted_iota(jnp.int32, (B_BLK, B_HEADS), 1)

        @pl.when(h == 0)
        def _():
            lse_ref[...] = jnp.zeros_like(lse_ref)

        def block(si, i, first):
            rows, win = _attn_rows(i, first)
            mask = mask1 if first else mask2
            sc = jnp.where(mask, _dot(q_ref[si, rows, :], k_ref[si, win, :], 1, 1) * scale, -1e30)
            yield
            m = jnp.max(sc, axis=1, keepdims=True)
            p = jnp.exp(sc - m)
            l = jnp.sum(p, axis=1, keepdims=True)
            pv = _dot(p, v_ref[si, win, :], 1, 0)
            yield
            o_ref[si, rows, :] = pv / l
            lse_ref[si, rows, :] = jnp.where(lane == h, m + jnp.log(l), lse_ref[si, rows, :])

        _attn_schedule(nb, sb, block)

    head = lambda off: pl.BlockSpec((sb, ln, B_DH), lambda s, h: (s, 0, off + h))
    return pl.pallas_call(
        body, name=name, grid=(ns // sb, B_HEADS),
        in_specs=[head(0), head(B_HEADS), head(2 * B_HEADS)],
        out_specs=[head(0), pl.BlockSpec((sb, ln, B_HEADS), lambda s, h: (s, 0, 0))],
        out_shape=[jax.ShapeDtypeStruct((ns, ln, B_W), F32), jax.ShapeDtypeStruct((ns, ln, B_HEADS), F32)],
        compiler_params=_params(("parallel", "arbitrary")),
    )(qkv, qkv, qkv)


def _attn_bwd(qkv, d_o, lse_joint, delta, name):
    ns, ln, _ = qkv.shape
    nb = ln // B_BLK
    sb = B_ROWS // ln
    scale = B_DH ** -0.5

    def body(q_ref, k_ref, v_ref, do_ref, lj_ref, dl_ref, dq_ref, dk_out, dv_out, dk_ref, dv_ref):
        h = pl.program_id(1)
        mask1, mask2 = _attn_masks()
        dk_ref[...] = jnp.zeros_like(dk_ref)
        dv_ref[...] = jnp.zeros_like(dv_ref)

        def block(si, i, first):
            rows, win = _attn_rows(i, first)
            mask = mask1 if first else mask2
            q = q_ref[si, rows, :]
            d_out = do_ref[si, rows, :]
            l_col = _lane_pick(lj_ref[si, rows, :], h)
            d_col = _lane_pick(dl_ref[si, rows, :], h)
            sc = _dot(q, k_ref[si, win, :], 1, 1) * scale
            d_p = _dot(d_out, v_ref[si, win, :], 1, 1)
            yield
            p = jnp.exp(jnp.where(mask, sc - l_col, -1e30))
            d_s = p * (d_p - d_col) * scale
            d_q = _dot(d_s, k_ref[si, win, :], 1, 0)
            d_k = _dot(d_s, q, 0, 0)
            d_v = _dot(p, d_out, 0, 0)
            yield
            dq_ref[si, rows, :] = d_q.astype(BF16)
            dk_ref[si, win, :] += d_k
            dv_ref[si, win, :] += d_v

        _attn_schedule(nb, sb, block)
        dk_out[...] = dk_ref[...].astype(BF16)
        dv_out[...] = dv_ref[...].astype(BF16)

    head = lambda off: pl.BlockSpec((sb, ln, B_DH), lambda s, h: (s, 0, off + h))
    small = pl.BlockSpec((sb, ln, B_HEADS), lambda s, h: (s, 0, 0))
    return pl.pallas_call(
        body, name=name, grid=(ns // sb, B_HEADS),
        in_specs=[head(0), head(B_HEADS), head(2 * B_HEADS), head(0), small, small],
        out_specs=[head(0)] * 3,
        out_shape=[jax.ShapeDtypeStruct((ns, ln, B_W), BF16)] * 3,
        scratch_shapes=[pltpu.VMEM((sb, ln, B_DH), F32)] * 2,
        compiler_params=_params(("parallel", "parallel")),
    )(qkv, qkv, qkv, d_o, lse_joint, delta)


def _merge_weights(lse_refs):
    ls = [r[...] for r in lse_refs]
    m = jnp.maximum(jnp.maximum(ls[0], ls[1]), ls[2])
    es = [jnp.exp(l - m) for l in ls]
    tot = es[0] + es[1] + es[2]
    return [e / tot for e in es], m + jnp.log(tot)


def _merge_fwd(outs, lses, proj0, tm=256):
    t = outs[0].shape[0]

    def body(o0, o1, o2, l0, l1, l2, z_ref, og_ref):
        wts, _ = _merge_weights((l0, l1, l2))

        def one_head(h):
            cols = slice(h * B_DH, (h + 1) * B_DH)
            w0, w1, w2 = (jnp.broadcast_to(w[:, h:h + 1], (tm, B_DH)) for w in wts)
            yield
            o = w0 * o0[:, cols] + w1 * o1[:, cols] + w2 * o2[:, cols]
            og_ref[:, cols] = (o * _silu(z_ref[:, cols])).astype(BF16)

        _round_robin([one_head(h) for h in range(B_HEADS)])

    wide = pl.BlockSpec((tm, B_W), lambda i: (i, 0))
    small = pl.BlockSpec((tm, B_HEADS), lambda i: (i, 0))
    return pl.pallas_call(
        body, name="merge_fwd", grid=(t // tm,),
        in_specs=[wide] * 3 + [small] * 3 + [pl.BlockSpec((tm, B_W), lambda i: (i, 3))],
        out_specs=wide, out_shape=jax.ShapeDtypeStruct((t, B_W), BF16),
        compiler_params=_params(("parallel",)),
    )(*outs, *lses, proj0)


def _merge_bwd(outs, lses, proj0, d_og, tm=256):
    t = outs[0].shape[0]

    def body(o0, o1, o2, l0, l1, l2, z_ref, dog_ref, do_ref, lj_ref, dl_ref, dz_ref):
        wts, lj = _merge_weights((l0, l1, l2))
        lj_ref[...] = lj
        lane = lax.broadcasted_iota(jnp.int32, (tm, B_HEADS), 1)
        sums = [None] * B_HEADS

        def one_head(h):
            cols = slice(h * B_DH, (h + 1) * B_DH)
            w0, w1, w2 = (jnp.broadcast_to(w[:, h:h + 1], (tm, B_DH)) for w in wts)
            yield
            o = w0 * o0[:, cols] + w1 * o1[:, cols] + w2 * o2[:, cols]
            z = z_ref[:, cols]
            d_g = dog_ref[:, cols]
            d_out = d_g * _silu(z)
            dz_ref[:, cols] = (d_g * o * _dsilu(z)).astype(BF16)
            do_ref[:, cols] = d_out.astype(BF16)
            sums[h] = jnp.sum(d_out * o, axis=1, keepdims=True)
            yield

        _round_robin([one_head(h) for h in range(B_HEADS)])
        delta = jnp.zeros((tm, B_HEADS), F32)
        for h in range(B_HEADS):
            delta = jnp.where(lane == h, sums[h], delta)
        dl_ref[...] = delta

    wide = pl.BlockSpec((tm, B_W), lambda i: (i, 0))
    small = pl.BlockSpec((tm, B_HEADS), lambda i: (i, 0))
    return pl.pallas_call(
        body, name="merge_bwd", grid=(t // tm,),
        in_specs=[wide] * 3 + [small] * 3 + [pl.BlockSpec((tm, B_W), lambda i: (i, 3)), wide],
        out_specs=[wide, small, small, wide],
        out_shape=[jax.ShapeDtypeStruct((t, B_W), BF16), jax.ShapeDtypeStruct((t, B_HEADS), F32),
                   jax.ShapeDtypeStruct((t, B_HEADS), F32), jax.ShapeDtypeStruct((t, B_W), BF16)],
        compiler_params=_params(("parallel",)),
    )(*outs, *lses, proj0, d_og)


def _adamw(w, g, m, v, name):
    r, c = w.shape
    tr = r
    for cand in (256, 128, 64, 32, 16, 8):
        if r % cand == 0:
            tr = cand
            break

    def body(w_ref, g_ref, m_ref, v_ref, d_ref, nm_ref, nv_ref):
        gv = g_ref[...]
        nm = ADAM_B1 * m_ref[...] + (1.0 - ADAM_B1) * gv
        nv = ADAM_B2 * v_ref[...] + (1.0 - ADAM_B2) * (gv * gv)
        m_hat = nm / (1.0 - ADAM_B1 ** ADAM_STEP)
        v_hat = nv / (1.0 - ADAM_B2 ** ADAM_STEP)
        d_ref[...] = -ADAM_LR * (m_hat / (jnp.sqrt(v_hat) + ADAM_EPS) + ADAM_WD * w_ref[...])
        nm_ref[...] = nm
        nv_ref[...] = nv

    blk = pl.BlockSpec((tr, c), lambda i: (i, 0))
    return pl.pallas_call(
        body, name=name, grid=(r // tr,), in_specs=[blk] * 4, out_specs=[blk] * 3,
        out_shape=[jax.ShapeDtypeStruct((r, c), F32)] * 3,
        compiler_params=_params(("parallel",)),
    )(w, g, m, v)


def _adam_update(w, gv, m, v):
    nm = ADAM_B1 * m + (1.0 - ADAM_B1) * gv
    nv = ADAM_B2 * v + (1.0 - ADAM_B2) * (gv * gv)
    m_hat = nm / (1.0 - ADAM_B1 ** ADAM_STEP)
    v_hat = nv / (1.0 - ADAM_B2 ** ADAM_STEP)
    return -ADAM_LR * (m_hat / (jnp.sqrt(v_hat) + ADAM_EPS) + ADAM_WD * w), nm, nv


def _adamw_shard(w, mine, theirs, m, v, half_index, name, tr=128):
    _, r, c = w.shape
    nhb = (r // 2) // tr

    def body(c_ref, w_ref, mine_ref, theirs_ref, m_ref, v_ref, g_ref, d_ref, nm_ref, nv_ref):
        is_mine = (pl.program_id(0) // nhb) == c_ref[0]
        gv = jnp.where(is_mine, mine_ref[...], theirs_ref[...])
        d, nm, nv = _adam_update(w_ref[...], gv, m_ref[...], v_ref[...])
        g_ref[...] = gv
        d_ref[...] = d
        nm_ref[...] = nm
        nv_ref[...] = nv

    full = pl.BlockSpec((None, tr, c), lambda i, cc: (0, i, 0))
    half = pl.BlockSpec((tr, c), lambda i, cc: (i % nhb, 0))
    return pl.pallas_call(
        body, name=name,
        grid_spec=pltpu.PrefetchScalarGridSpec(
            num_scalar_prefetch=1, grid=(2 * nhb,),
            in_specs=[full, half, half, full, full], out_specs=[full] * 4),
        out_shape=[jax.ShapeDtypeStruct(w.shape, F32)] * 4,
        compiler_params=_params(("parallel",), 40 * 1024 * 1024),
    )(half_index, w, mine, theirs, m, v)


def _adamw_shard_cols(w, mine, theirs, m, v, half_index, name, steps=20):
    c, _, r = w.shape
    tc = c // steps
    assert tc * steps == c

    def body(c_ref, w_ref, mine_ref, theirs_ref, m_ref, v_ref, g_ref, d_ref, nm_ref, nv_ref):
        first = jnp.where(c_ref[0] == 0, mine_ref[...], theirs_ref[...])
        second = jnp.where(c_ref[0] == 0, theirs_ref[...], mine_ref[...])
        for lo, gv in ((0, first), (r // 2, second)):
            cols = slice(lo, lo + r // 2)
            d, nm, nv = _adam_update(w_ref[:, :, cols], gv, m_ref[:, :, cols], v_ref[:, :, cols])
            g_ref[:, :, cols] = gv
            d_ref[:, :, cols] = d
            nm_ref[:, :, cols] = nm
            nv_ref[:, :, cols] = nv

    full = pl.BlockSpec((tc, 1, r), lambda i, cc: (i, 0, 0))
    half = pl.BlockSpec((tc, 1, r // 2), lambda i, cc: (i, 0, 0))
    return pl.pallas_call(
        body, name=name,
        grid_spec=pltpu.PrefetchScalarGridSpec(
            num_scalar_prefetch=1, grid=(steps,),
            in_specs=[full, half, half, full, full], out_specs=[full] * 4),
        out_shape=[jax.ShapeDtypeStruct(w.shape, F32)] * 4,
        compiler_params=_params(("parallel",), 40 * 1024 * 1024),
    )(half_index, w, mine, theirs, m, v)


def _pair_sum(own, other, half_index, name, tr=256):
    _, r, c = own.shape
    rh = r // 2
    tr = min(tr, rh)
    nrb = rh // tr

    def body(c_ref, own_ref, oth_ref, out_ref):
        out_ref[...] = (own_ref[...] + oth_ref[...].astype(F32)).astype(BF16)

    return pl.pallas_call(
        body, name=name,
        grid_spec=pltpu.PrefetchScalarGridSpec(
            num_scalar_prefetch=1, grid=(N_CHIPS, nrb),
            in_specs=[pl.BlockSpec((None, tr, c), lambda k, i, cc: (k, cc[0] * nrb + i, 0)),
                      pl.BlockSpec((None, tr, c), lambda k, i, cc: (k, i, 0))],
            out_specs=pl.BlockSpec((None, tr, c), lambda k, i, cc: (k, i, 0))),
        out_shape=jax.ShapeDtypeStruct((N_CHIPS, rh, c), BF16),
        compiler_params=_params(("parallel", "parallel")),
    )(half_index, own, other)


def _chip_sum(sums, others, chip_index, name, tr=256):
    _, r, c = sums.shape
    tr = min(tr, r)

    def body(k_ref, own_ref, oth_ref, out_ref):
        acc = own_ref[...].astype(F32)
        for j in range(N_CHIPS - 1):
            acc = acc + oth_ref[j].astype(F32)
        out_ref[...] = acc

    return pl.pallas_call(
        body, name=name,
        grid_spec=pltpu.PrefetchScalarGridSpec(
            num_scalar_prefetch=1, grid=(r // tr,),
            in_specs=[pl.BlockSpec((None, tr, c), lambda i, kk: (kk[0], i, 0)),
                      pl.BlockSpec((N_CHIPS - 1, tr, c), lambda i, kk: (0, i, 0))],
            out_specs=pl.BlockSpec((tr, c), lambda i, kk: (i, 0))),
        out_shape=jax.ShapeDtypeStruct((r, c), F32),
        compiler_params=_params(("parallel",)),
    )(chip_index, sums, others)


HBM = pl.BlockSpec(memory_space=pltpu.HBM)


def _place():
    x, y, c = lax.axis_index("x"), lax.axis_index("y"), lax.axis_index("c")
    chips = [(1 - x, y), (x, 1 - y), (1 - x, 1 - y)]
    return x, y, c, chips


def _sibling_forward(land):
    def body(in_ref, out_ref, send, recv):
        x, y, c, chips = _place()
        rh = out_ref.shape[1] // 2
        cps = []
        for j, (px, py) in enumerate(chips):
            slot = out_ref.at[2 * px + py, pl.ds(c * rh, rh)]
            cp = pltpu.make_async_remote_copy(
                src_ref=slot, dst_ref=slot, send_sem=send.at[j], recv_sem=recv.at[j],
                device_id=(x, y, 1 - c), device_id_type=MESH)
            cp.start()
            cps.append(cp)
        for j, (px, py) in enumerate(chips):
            slot = out_ref.at[2 * px + py, pl.ds((1 - c) * rh, rh)]
            pltpu.make_async_remote_copy(
                src_ref=slot, dst_ref=slot, send_sem=send.at[j], recv_sem=recv.at[j],
                device_id=(x, y, 1 - c), device_id_type=MESH).wait_recv()
        for cp in cps:
            cp.wait_send()

    return pl.pallas_call(
        body, name="first_weights_sibling_forward", in_specs=[HBM], out_specs=HBM,
        out_shape=jax.ShapeDtypeStruct(land.shape, land.dtype), input_output_aliases={0: 0},
        scratch_shapes=[pltpu.SemaphoreType.DMA((3,)), pltpu.SemaphoreType.DMA((3,))],
    )(land)


def _sibling_swap_halves(grads, name):
    na = len(grads)

    def body(*refs):
        ins, outs = refs[:na], refs[na:2 * na]
        send, recv = refs[2 * na:]
        x, y, c, _ = _place()
        sib = (x, y, 1 - c)
        cps = []
        for i in range(na):
            rh = ins[i].shape[1] // 2
            cp = pltpu.make_async_remote_copy(
                src_ref=ins[i].at[:, pl.ds((1 - c) * rh, rh), :], dst_ref=outs[i],
                send_sem=send.at[i], recv_sem=recv.at[i], device_id=sib, device_id_type=MESH)
            cp.start()
            cps.append(cp)
        for cp in cps:
            cp.wait()

    out_shape = [jax.ShapeDtypeStruct((g.shape[0], g.shape[1] // 2, g.shape[2]), g.dtype) for g in grads]
    return pl.pallas_call(
        body, name=name, in_specs=[HBM] * na, out_specs=[HBM] * na, out_shape=out_shape,
        scratch_shapes=[pltpu.SemaphoreType.DMA((na,)), pltpu.SemaphoreType.DMA((na,))],
    )(*grads)


def _sibling_swap_whole(halves):
    na = len(halves)

    def body(*refs):
        ins, outs = refs[:na], refs[na:2 * na]
        send, recv = refs[2 * na:]
        x, y, c, _ = _place()
        cps = []
        for i in range(na):
            cp = pltpu.make_async_remote_copy(
                src_ref=ins[i], dst_ref=outs[i], send_sem=send.at[i], recv_sem=recv.at[i],
                device_id=(x, y, 1 - c), device_id_type=MESH)
            cp.start()
            cps.append(cp)
        for cp in cps:
            cp.wait()

    out_shape = [jax.ShapeDtypeStruct(h.shape, h.dtype) for h in halves]
    return pl.pallas_call(
        body, name="grad_sibling_join", in_specs=[HBM] * na, out_specs=[HBM] * na, out_shape=out_shape,
        scratch_shapes=[pltpu.SemaphoreType.DMA((na,)), pltpu.SemaphoreType.DMA((na,))],
    )(*halves)


SEM = pl.BlockSpec(memory_space=pltpu.SEMAPHORE)
ANY = pl.BlockSpec(memory_space=pl.ANY)
EFFECT = pltpu.SideEffectType.DATAFLOW_SIDE_EFFECTING


def _split_copy_start(name, plan, srcs, lands, after):
    ns, nl = len(srcs), len(lands)

    def body(*refs):
        src_refs, land_refs = refs[:ns], refs[ns:ns + nl]
        send, recv = refs[ns + nl + 1], refs[ns + nl + 2]
        token = refs[-1]
        outgoing, _ = plan(src_refs, land_refs)
        for src, dst, dev, si, ri in outgoing:
            pltpu.make_async_remote_copy(src_ref=src, dst_ref=dst, send_sem=send.at[si], recv_sem=recv.at[ri],
                                         device_id=dev, device_id_type=MESH).start()
        token[...] = jnp.zeros_like(token)

    n_out, n_in = plan.counts
    thru = [pltpu.HBM(a.shape, a.dtype) for a in list(srcs) + list(lands)]
    res = pl.pallas_call(
        body, name=name,
        out_shape=[pltpu.SemaphoreType.DMA((n_out,)), pltpu.SemaphoreType.DMA((n_in,))] + thru
        + [jax.ShapeDtypeStruct((8, 128), F32)],
        in_specs=[HBM] * (ns + nl) + [ANY],
        out_specs=[SEM, SEM] + [HBM] * (ns + nl) + [pl.BlockSpec(memory_space=pltpu.VMEM)],
        input_output_aliases={i: 2 + i for i in range(ns + nl)},
        compiler_params=pltpu.CompilerParams(has_side_effects=EFFECT),
    )(*[pltpu.with_memory_space_constraint(a, pltpu.HBM) for a in list(srcs) + list(lands)], after)
    return res[0], res[1], res[2:2 + ns], res[2 + ns:2 + ns + nl], res[-1]


def _split_copy_wait(name, plan, send, recv, srcs, lands, after):
    ns, nl = len(srcs), len(lands)
    after = list(after) if isinstance(after, (list, tuple)) else [after]

    def body(*refs):
        src_refs, land_refs = refs[:ns], refs[ns:ns + nl]
        send_ref, recv_ref = refs[ns + nl], refs[ns + nl + 1]
        outgoing, arrivals = plan(src_refs, land_refs)
        for src, dst, dev, si, ri in outgoing:
            pltpu.make_async_remote_copy(src_ref=src, dst_ref=dst, send_sem=send_ref.at[si], recv_sem=recv_ref.at[ri],
                                         device_id=dev, device_id_type=MESH).wait_send()
        for view, ri in arrivals:
            pltpu.make_async_remote_copy(src_ref=view, dst_ref=view, send_sem=send_ref.at[0], recv_sem=recv_ref.at[ri],
                                         device_id=_place()[:3], device_id_type=MESH).wait_recv()

    thru = [pltpu.HBM(a.shape, a.dtype) for a in list(srcs) + list(lands)]
    res = pl.pallas_call(
        body, name=name, out_shape=thru,
        in_specs=[HBM] * (ns + nl) + [SEM, SEM] + [ANY] * len(after), out_specs=[HBM] * (ns + nl),
        input_output_aliases={i: i for i in range(ns + nl)},
        compiler_params=pltpu.CompilerParams(has_side_effects=EFFECT),
    )(*srcs, *lands, send, recv, *after)
    return res[:ns], res[ns:]


def _gather_plan(n_arrays):
    def plan(src_refs, land_refs):
        x, y, c, chips = _place()
        me = 2 * x + y
        outgoing, arrivals = [], []
        for i in range(n_arrays):
            rh = src_refs[i].shape[0] // 2
            mine = pl.ds(c * rh, rh)
            for j, (px, py) in enumerate(chips):
                for delta in range(2):
                    tc = c ^ delta
                    outgoing.append((src_refs[i].at[mine], land_refs[i].at[me, mine], (px, py, tc),
                                     6 * i + 2 * j + delta, 6 * i + 2 * j + delta))
                    theirs = pl.ds(tc * rh, rh)
                    arrivals.append((land_refs[i].at[2 * px + py, theirs], 6 * i + 2 * j + delta))
        return outgoing, arrivals

    plan.counts = (6 * n_arrays, 6 * n_arrays)
    return plan


def _first_gather_plan():
    def plan(src_refs, land_refs):
        x, y, c, chips = _place()
        me = 2 * x + y
        rh = src_refs[0].shape[0] // 2
        mine = pl.ds(c * rh, rh)
        outgoing, arrivals = [], []
        for j, (px, py) in enumerate(chips):
            outgoing.append((src_refs[0].at[mine], land_refs[0].at[me, mine], (px, py, c), j, j))
            arrivals.append((land_refs[0].at[2 * px + py, mine], j))
            outgoing.append((src_refs[1], land_refs[1].at[me], (px, py, c), 3 + j, 3 + j))
            arrivals.append((land_refs[1].at[2 * px + py], 3 + j))
        return outgoing, arrivals

    plan.counts = (6, 6)
    return plan


def _exchange_plan(n_arrays):
    def plan(src_refs, land_refs):
        x, y, c, chips = _place()
        outgoing, arrivals = [], []
        for i in range(n_arrays):
            for j, (px, py) in enumerate(chips):
                outgoing.append((src_refs[i].at[2 * px + py], land_refs[i].at[j], (px, py, c), 3 * i + j, 3 * i + j))
                arrivals.append((land_refs[i].at[j], 3 * i + j))
        return outgoing, arrivals

    plan.counts = (3 * n_arrays, 3 * n_arrays)
    return plan


def _small_allreduce(vec):
    r, cdim = vec.shape
    n_dev = 8

    def body(v_ref, out_ref, buf, send, recv):
        x, y, c, _ = _place()
        me = 4 * x + 2 * y + c
        buf[me] = v_ref[...]
        cps = []
        for k in range(1, n_dev):
            dx, dy, dc = (k >> 2) & 1, (k >> 1) & 1, k & 1
            peer = (x ^ dx, y ^ dy, c ^ dc)
            cp = pltpu.make_async_remote_copy(
                src_ref=v_ref, dst_ref=buf.at[me], send_sem=send.at[k - 1], recv_sem=recv.at[k - 1],
                device_id=peer, device_id_type=MESH)
            cp.start()
            cps.append(cp)
        for k in range(1, n_dev):
            dx, dy, dc = (k >> 2) & 1, (k >> 1) & 1, k & 1
            src = 4 * (x ^ dx) + 2 * (y ^ dy) + (c ^ dc)
            slot = buf.at[src]
            pltpu.make_async_remote_copy(
                src_ref=slot, dst_ref=slot, send_sem=send.at[k - 1], recv_sem=recv.at[k - 1],
                device_id=(x ^ dx, y ^ dy, c ^ dc), device_id_type=MESH).wait_recv()
        for cp in cps:
            cp.wait_send()
        acc = buf[0]
        for k in range(1, n_dev):
            acc = acc + buf[k]
        out_ref[...] = acc

    vm = pl.BlockSpec(memory_space=pltpu.VMEM)
    return pl.pallas_call(
        body, name="small_allreduce", in_specs=[vm], out_specs=vm,
        out_shape=jax.ShapeDtypeStruct((r, cdim), F32),
        scratch_shapes=[pltpu.VMEM((n_dev, r, cdim), F32), pltpu.SemaphoreType.DMA((n_dev - 1,)),
                        pltpu.SemaphoreType.DMA((n_dev - 1,))],
    )(vec)


def _a_cols_to_head_major(w):
    lead = w.shape[:-1]
    q = w[..., :A_QK].reshape(lead + (A_HEADS, A_DK))
    k = w[..., A_QK:2 * A_QK].reshape(lead + (A_HEADS, A_DK))
    v = w[..., 2 * A_QK:2 * A_QK + A_VW].reshape(lead + (A_HEADS, A_DV))
    z = w[..., 2 * A_QK + A_VW:].reshape(lead + (A_HEADS, A_DV))
    return jnp.concatenate([q, k, v, z], axis=-1).reshape(lead + (A_HEADS * A_HEAD_COLS,))


def _a_cols_from_head_major(w):
    lead = w.shape[:-1]
    w = w.reshape(lead + (A_HEADS, A_HEAD_COLS))
    parts = [w[..., :A_DK], w[..., A_DK:2 * A_DK], w[..., 2 * A_DK:2 * A_DK + A_DV], w[..., 2 * A_DK + A_DV:]]
    return jnp.concatenate([p.reshape(lead + (-1,)) for p in parts], axis=-1)


def _conv_cols_to_head_major(w):
    lead = w.shape[:-1]
    q = w[..., :A_QK].reshape(lead + (A_HEADS, A_DK))
    k = w[..., A_QK:2 * A_QK].reshape(lead + (A_HEADS, A_DK))
    v = w[..., 2 * A_QK:].reshape(lead + (A_HEADS, A_DV))
    return jnp.concatenate([q, k, v], axis=-1).reshape(lead + (A_HEADS * A_CONV_COLS,))


def _conv_cols_from_head_major(w):
    lead = w.shape[:-1]
    w = w.reshape(lead + (A_HEADS, A_CONV_COLS))
    parts = [w[..., :A_DK], w[..., A_DK:2 * A_DK], w[..., 2 * A_DK:]]
    return jnp.concatenate([p.reshape(lead + (-1,)) for p in parts], axis=-1)


def _to_stream(a, bn, d):
    rest = a.shape[1:]
    s = a.shape[0] // bn
    a = a.reshape((bn, s // d, d) + rest)
    a = jnp.swapaxes(a, 1, 2)
    return a.reshape((bn * d, s // d) + rest)


def _from_stream(a, bn, d):
    rest = a.shape[2:]
    ln = a.shape[1]
    a = a.reshape((bn, d, ln) + rest)
    a = jnp.swapaxes(a, 1, 2)
    return a.reshape((bn * ln * d,) + rest)


B_SUB = 512
B_SHARD_BLOCKS = (3 * B_GROUPS * B_W + B_W) // N_CHIPS // B_SUB


def _b_block(gi, jj):
    nb = (B_GROUPS * (jj // 2) + gi) * 2 + jj % 2
    return nb // B_SHARD_BLOCKS, nb % B_SHARD_BLOCKS


def _shard_major(g, ncols):
    r = g.shape[0]
    return jnp.swapaxes(g.reshape(r, N_CHIPS, ncols), 0, 1)


def _pack_rows(items):
    rows, offs = [], []
    at = 0
    for a in items:
        flat = a.reshape(-1).astype(F32)
        nr = -(-flat.shape[0] // 1024) * 8
        flat = jnp.pad(flat, (0, nr * 128 - flat.shape[0]))
        rows.append(flat.reshape(nr, 128))
        offs.append((at, nr, a.shape))
        at += nr
    return jnp.concatenate(rows, axis=0), offs


def _unpack_rows(packed, offs):
    out = []
    for at, nr, shape in offs:
        size = int(np.prod(shape)) if len(shape) else 1
        out.append(packed[at:at + nr].reshape(-1)[:size].reshape(shape))
    return out


def _local_step(x, positions, loss_target, norm_g, a_log, a_dt_bias, a_norm_g, b_q_norm_g, b_k_norm_g,
                start_token, first_weights, late_weights, b_grads_ready, a_grads_ready):
    bn, s, d = x.shape
    t = bn * s
    n_chunks = s // A_CHUNK
    x0 = x.reshape(t, d)
    h0 = _rms_fwd(x0, norm_g[0:1] + start_token, "rms0_fwd")
    inv_freq = ROPE_THETA ** (-jnp.arange(0, ROPE_DIMS, 2, dtype=F32) / ROPE_DIMS)
    freq_row = jnp.concatenate([inv_freq, inv_freq, jnp.zeros((128 - ROPE_DIMS,), F32)]).reshape(1, 128)
    posf = jnp.broadcast_to(positions.astype(F32).reshape(t, 1), (t, 128)) + start_token
    tabs = _rope_tables(posf, freq_row)
    tabs_s = [tabs if dil == 1 else [_to_stream(tb, bn, dil).reshape(t, 128) for tb in tabs] for dil in B_DIL]
    wa_in, conv_w, late_token = first_weights([h0] + [tb for ts in tabs_s for tb in ts])
    wa_main = _a_cols_to_head_major(wa_in[:, :A_MAIN])
    wa_tail = jnp.pad(wa_in[:, A_MAIN:], ((0, 0), (0, 128 - 2 * A_HEADS))) + late_token.astype(BF16)
    cw_hm = _conv_cols_to_head_major(conv_w)

    proj_a = _matmul(h0, wa_main, "nn", F32, "a_in_main")
    tail_a = _matmul(h0, wa_tail, "nn", F32, "a_in_tail")
    tail_t = jnp.swapaxes(tail_a[:, :2 * A_HEADS].reshape(bn, s, 2 * A_HEADS), 1, 2)
    tail_t = tail_t.reshape(bn, 2 * A_HEADS, n_chunks, A_CHUNK)
    beta, gc = _gdn_prep(tail_t, a_log[0], a_dt_bias[0])
    proj_a3 = proj_a.reshape(bn, s, A_MAIN)
    og_a, oraw_a, states, t_mats, conv_y = _gdn_fwd(proj_a3, cw_hm, beta, gc, a_norm_g)
    wa_out, wb_in, wb_out = late_weights(og_a)
    b_cols = [4 * B_W] + [3 * B_W] * (B_GROUPS - 1)
    x1 = _matmul(og_a.reshape(t, A_VW), wa_out, "nn", F32, "a_out", res=x0, tk=2048)

    h1 = _rms_fwd(x1, norm_g[1:2], "rms1_fwd")
    h1_s, proj_b, qkv_b, o_b, lse_b = [], [], [], [], []
    for gi, dil in enumerate(B_DIL):
        hs = h1 if dil == 1 else _to_stream(h1, bn, dil).reshape(t, d)
        ts = tabs_s[gi]
        pj = _matmul(hs, wb_in, "nn", F32, f"b_in_g{gi}", tm=2048, tn=B_SUB, n=b_cols[gi], b_spec=pl.BlockSpec(
            (None, d, B_SUB), lambda i, j, kk, gi=gi: (_b_block(gi, j)[0], kk, _b_block(gi, j)[1])))
        qkv = _qk_prep(pj, *ts, b_q_norm_g[0, gi:gi + 1], b_k_norm_g[0, gi:gi + 1], f"qk_prep_g{gi}")
        o_s, lse_s = _attn_fwd(qkv.reshape(bn * dil, s // dil, 3 * B_W), f"attn_fwd_g{gi}")
        h1_s.append(hs), proj_b.append(pj), qkv_b.append(qkv)
        o_b.append(o_s.reshape(t, B_W) if dil == 1 else _from_stream(o_s, bn, dil))
        lse_b.append(lse_s.reshape(t, B_HEADS) if dil == 1 else _from_stream(lse_s, bn, dil))
    og_b = _merge_fwd(o_b, lse_b, proj_b[0])
    x2 = _matmul(og_b, wb_out, "nn", F32, "b_out", res=x1)

    d_x2, loss_parts = _loss_grad(x2, loss_target.reshape(t, d))
    loss_local = jnp.sum(loss_parts)

    d_x2b = d_x2.astype(BF16)
    g_wb_out = _matmul(og_b, d_x2b, "tn", F32, "b_out_dw")
    d_og_b = _matmul(d_x2b, wb_out, "nt", F32, "b_out_dx")
    d_o, lse_joint, delta, d_z = _merge_bwd(o_b, lse_b, proj_b[0], d_og_b)
    d_h1, g_qn, g_kn = [], [], []
    g_wb_in = lax.empty(wb_in.shape, F32)
    for gi, dil in enumerate(B_DIL):
        if dil == 1:
            do_s, lj_s, dl_s = d_o, lse_joint, delta
        else:
            do_s, lj_s, dl_s = (_to_stream(a, bn, dil).reshape(t, -1) for a in (d_o, lse_joint, delta))
        ns, ln = bn * dil, s // dil
        dq, dk, dv = _attn_bwd(qkv_b[gi].reshape(ns, ln, 3 * B_W), do_s.reshape(ns, ln, B_W),
                               lj_s.reshape(ns, ln, B_HEADS), dl_s.reshape(ns, ln, B_HEADS), f"attn_bwd_g{gi}")
        d_pj, d_gain = _qk_prep_bwd(proj_b[gi], *tabs_s[gi], b_q_norm_g[0, gi:gi + 1], b_k_norm_g[0, gi:gi + 1],
                                    dq.reshape(t, B_W), dk.reshape(t, B_W), dv.reshape(t, B_W),
                                    d_z if gi == 0 else None, f"qk_prep_bwd_g{gi}")
        g_wb_in = _matmul(h1_s[gi], d_pj, "tn", F32, f"b_in_dw_g{gi}", tn=B_SUB, tk=2048, into=(g_wb_in, pl.BlockSpec(
            (None, d, B_SUB), lambda i, j, kk, gi=gi: (_b_block(gi, j)[0], i, _b_block(gi, j)[1]))))
        dh = _matmul(d_pj, wb_in, "nt", F32, f"b_in_dx_g{gi}", tm=2048, tk=B_SUB, n=d, b_spec=pl.BlockSpec(
            (None, d, B_SUB), lambda i, j, kk, gi=gi: (_b_block(gi, kk)[0], j, _b_block(gi, kk)[1])))
        d_h1.append(dh if dil == 1 else _from_stream(dh.reshape(ns, ln, d), bn, dil))
        g_qn.append(d_gain[0]), g_kn.append(d_gain[1])
    d_x1, g_norm1 = _rms_bwd(x1, norm_g[1:2], d_h1, d_x2, "rms1_bwd")

    d_x1b = d_x1.astype(BF16)
    g_wa_out = _matmul(og_a.reshape(t, A_VW), d_x1b, "tn", F32, "a_out_dw")
    b_token = b_grads_ready(g_wb_in, g_wb_out, g_wa_out)
    d_og_a = _matmul(d_x1b, wa_out, "nt", F32, "a_out_dx")
    d_pa, d_gc, d_beta, d_cw, d_ng = _gdn_bwd(proj_a3, cw_hm, beta, gc, a_norm_g + b_token, oraw_a, states,
                                              t_mats, conv_y, d_og_a.reshape(bn, s, A_VW))
    d_tail_t, d_alog, d_dtb = _gdn_prep_bwd(tail_t, a_log[0], a_dt_bias[0], d_gc, d_beta)
    d_tail = jnp.swapaxes(d_tail_t.reshape(bn, 2 * A_HEADS, s), 1, 2).reshape(t, 2 * A_HEADS)
    d_tail = jnp.pad(d_tail, ((0, 0), (0, 128 - 2 * A_HEADS))).astype(BF16)
    d_pa = d_pa.reshape(t, A_MAIN)
    g_wa_main = _matmul(h0, d_pa, "tn", F32, "a_in_dw_main")
    g_wa_tail = _matmul(h0, d_tail, "tn", F32, "a_in_dw_tail")
    g_wa_in = jnp.concatenate([_a_cols_from_head_major(g_wa_main), g_wa_tail[:, :2 * A_HEADS]], axis=1)
    a_token = a_grads_ready(g_wa_in)
    d_h0 = _matmul(d_pa, wa_main, "nt", F32, "a_in_dx_main")
    d_h0t = _matmul(d_tail + a_token.astype(BF16), wa_tail, "nt", F32, "a_in_dx_tail")
    d_x0, g_norm0 = _rms_bwd(x0, norm_g[0:1], [d_h0, d_h0t], d_x1, "rms0_bwd")

    gfull = {
        "norm_g": jnp.concatenate([g_norm0, g_norm1], axis=0), "a_w_in": g_wa_in,
        "a_conv_w": _conv_cols_from_head_major(jnp.sum(d_cw, axis=0)),
        "a_log": jnp.sum(d_alog[:, :, 0], axis=0), "a_dt_bias": jnp.sum(d_dtb[:, :, 0], axis=0),
        "a_norm_g": jnp.sum(d_ng[:, :, 0, :], axis=(0, 1)), "a_w_out": g_wa_out, "b_w_in": g_wb_in,
        "b_q_norm_g": jnp.stack(g_qn), "b_k_norm_g": jnp.stack(g_kn), "b_w_out": g_wb_out}
    return loss_local, d_x0.reshape(bn, s, d), gfull


def kernel(x, positions, norm_g, a_w_in, a_conv_w, a_log, a_dt_bias, a_norm_g, a_w_out, b_w_in, b_q_norm_g, b_k_norm_g, b_w_out, loss_target, m_norm_g, m_a_w_in, m_a_conv_w, m_a_log, m_a_dt_bias, m_a_norm_g, m_a_w_out, m_b_w_in, m_b_q_norm_g, m_b_k_norm_g, m_b_w_out, v_norm_g, v_a_w_in, v_a_conv_w, v_a_log, v_a_dt_bias, v_a_norm_g, v_a_w_out, v_b_w_in, v_b_q_norm_g, v_b_k_norm_g, v_b_w_out):
    d = x.shape[2]
    my_c = lax.axis_index("c")
    my_chip = 2 * lax.axis_index("x") + lax.axis_index("y")

    half_index = jnp.reshape(my_c, (1,)).astype(jnp.int32)
    chip_index = jnp.reshape(my_chip, (1,)).astype(jnp.int32)
    def landing(shard):
        return lax.dynamic_update_slice(lax.empty((N_CHIPS,) + shard.shape, shard.dtype), shard[None],
                                        (my_chip,) + (0,) * shard.ndim)

    first_shards = [a_w_in[0].astype(BF16), a_conv_w[0]]
    first_plan = _first_gather_plan()
    first = _split_copy_start("first_weights_start", first_plan, first_shards,
                              [landing(s) for s in first_shards], half_index)
    pending = {}
    late_shards = [(w[0] + first[4][0, 0]).astype(BF16) for w in (a_w_out, b_w_in, b_w_out)]
    late_lands = [landing(s) for s in late_shards]

    def first_weights(after):
        _, (ga_in, g_conv) = _split_copy_wait("first_weights_wait", first_plan, *first[:4],
                                              list(after) + late_lands)
        ga_in = _sibling_forward(ga_in)
        wa_in = jnp.concatenate([ga_in[k] for k in range(N_CHIPS)], axis=1)
        conv_w = jnp.concatenate([g_conv[k] for k in range(N_CHIPS)], axis=1)
        plan = _gather_plan(len(late_shards))
        pending["late"] = (plan,) + tuple(_split_copy_start(
            "late_weights_start", plan, late_shards, late_lands, conv_w))
        return wa_in, conv_w, pending["late"][5][0, 0]

    def late_weights(after):
        plan, send, recv, srcs, lands, _ = pending["late"]
        _, (ga_out, gb_in, gb_out) = _split_copy_wait("late_weights_wait", plan, send, recv, srcs, lands, after)
        return ga_out.reshape(A_VW, d), gb_in, gb_out.reshape(B_W, d)

    def reduce_to_chip_sums(mats, tag):
        recv_sib = _sibling_swap_halves([g.astype(BF16) for g in mats], f"grad_{tag}_sibling_swap")
        return [_pair_sum(g, r, half_index, f"grad_{tag}_pair_sum_{i}") for i, (g, r) in enumerate(zip(mats, recv_sib))]

    def start_exchange(tag, mats):
        sums = reduce_to_chip_sums(mats, tag)
        lands = [lax.empty((N_CHIPS - 1,) + s.shape[1:], BF16) for s in sums]
        plan = _exchange_plan(len(mats))
        pending[tag] = (plan,) + tuple(_split_copy_start(f"grad_{tag}_exchange_start", plan, sums, lands, chip_index))
        return pending[tag][5][0, 0]

    def finish_exchange(tag, after):
        plan, send, recv, srcs, lands, _ = pending[tag]
        return _split_copy_wait(f"grad_{tag}_exchange_wait", plan, send, recv, srcs, lands, after)

    def b_grads_ready(g_wb_in, g_wb_out, g_wa_out):
        return start_exchange("b", [g_wb_in, g_wb_out.reshape(N_CHIPS, -1, d), g_wa_out.reshape(N_CHIPS, -1, d)])

    def a_grads_ready(g_wa_in):
        return start_exchange("a", [_shard_major(g_wa_in, a_w_in.shape[2])])

    loss_local, d_x0, gfull = _local_step(x, positions, loss_target, norm_g, a_log, a_dt_bias, a_norm_g,
                                          b_q_norm_g, b_k_norm_g, first[4][0, 0], first_weights, late_weights,
                                          b_grads_ready, a_grads_ready)

    small = [gfull["norm_g"], gfull["a_conv_w"], gfull["a_log"], gfull["a_dt_bias"], gfull["a_norm_g"],
             gfull["b_q_norm_g"], gfull["b_k_norm_g"], loss_local]
    packed, offs = _pack_rows(small)
    reduced = _small_allreduce(packed)
    g_norm, g_conv_all, g_alog, g_dtb, g_ang, g_q, g_k, loss = _unpack_rows(reduced, offs)
    g_conv_mine = lax.dynamic_slice_in_dim(g_conv_all, my_chip * a_conv_w.shape[2], a_conv_w.shape[2], axis=1)

    b_sums, b_received = finish_exchange("b", d_x0)
    a_sums, a_received = finish_exchange("a", reduced)
    chip_sums = [a_sums[0], b_sums[2], b_sums[0], b_sums[1]]
    received = [a_received[0], b_received[2], b_received[0], b_received[1]]
    halves = [_chip_sum(s, r, chip_index, f"grad_chip_sum_{i}") for i, (s, r) in enumerate(zip(chip_sums, received))]
    theirs = _sibling_swap_whole(halves)
    big = ("a_w_in", "a_w_out", "b_w_in", "b_w_out")
    big_halves = dict(zip(big, zip(halves, theirs)))

    grads = {
        "norm_g": g_norm, "a_conv_w": g_conv_mine[None], "a_log": g_alog[None], "a_dt_bias": g_dtb[None],
        "a_norm_g": g_ang[None], "b_q_norm_g": g_q[None], "b_k_norm_g": g_k[None]}
    weights = {"norm_g": norm_g, "a_w_in": a_w_in, "a_conv_w": a_conv_w, "a_log": a_log, "a_dt_bias": a_dt_bias,
               "a_norm_g": a_norm_g, "a_w_out": a_w_out, "b_w_in": b_w_in, "b_q_norm_g": b_q_norm_g,
               "b_k_norm_g": b_k_norm_g, "b_w_out": b_w_out}
    m_in = {"norm_g": m_norm_g, "a_w_in": m_a_w_in, "a_conv_w": m_a_conv_w, "a_log": m_a_log,
            "a_dt_bias": m_a_dt_bias, "a_norm_g": m_a_norm_g, "a_w_out": m_a_w_out, "b_w_in": m_b_w_in,
            "b_q_norm_g": m_b_q_norm_g, "b_k_norm_g": m_b_k_norm_g, "b_w_out": m_b_w_out}
    v_in = {"norm_g": v_norm_g, "a_w_in": v_a_w_in, "a_conv_w": v_a_conv_w, "a_log": v_a_log,
            "a_dt_bias": v_a_dt_bias, "a_norm_g": v_a_norm_g, "a_w_out": v_a_w_out, "b_w_in": v_b_w_in,
            "b_q_norm_g": v_b_q_norm_g, "b_k_norm_g": v_b_k_norm_g, "b_w_out": v_b_w_out}
    names = list(weights)

    delta_w, new_m, new_v = {}, {}, {}
    for nm in big:
        mine, other = big_halves[nm]
        if weights[nm].shape[2] % 128:
            cols = lambda a: jnp.transpose(a, (2, 0, 1))
            half_cols = lambda a: jnp.transpose(a)[:, None, :]
            outs = _adamw_shard_cols(cols(weights[nm]), half_cols(mine), half_cols(other), cols(m_in[nm]),
                                     cols(v_in[nm]), half_index, f"adamw_{nm}")
            outs = [jnp.transpose(o, (1, 2, 0)) for o in outs]
        else:
            outs = _adamw_shard(weights[nm], mine, other, m_in[nm], v_in[nm], half_index, f"adamw_{nm}")
        grads[nm], delta_w[nm], new_m[nm], new_v[nm] = outs
    small_names = [nm for nm in names if nm not in big]
    packs = [_pack_rows([src[nm] for nm in small_names]) for src in (weights, grads, m_in, v_in)]
    offs = packs[0][1]
    dl, m2, v2 = _adamw(packs[0][0], packs[1][0], packs[2][0], packs[3][0], "adamw_small")
    for nm, a, b, c2 in zip(small_names, _unpack_rows(dl, offs), _unpack_rows(m2, offs), _unpack_rows(v2, offs)):
        delta_w[nm], new_m[nm], new_v[nm] = a, b, c2

    return (loss, d_x0, *[grads[nm] for nm in names], *[delta_w[nm] for nm in names],
            *[new_m[nm] for nm in names], *[new_v[nm] for nm in names])
```

```python
import jax
import jax.numpy as jnp
import numpy as np
from jax import lax
from jax.experimental import pallas as pl
from jax.experimental.pallas import tpu as pltpu

F32 = jnp.float32
BF16 = jnp.bfloat16
MESH = pl.DeviceIdType.MESH

EPS = 1e-6
A_HEADS = 8
A_DK = 128
A_DV = 256
A_QK = A_HEADS * A_DK
A_VW = A_HEADS * A_DV
A_MAIN = 2 * A_QK + 2 * A_VW
A_HEAD_COLS = 2 * A_DK + 2 * A_DV
A_CONV_COLS = 2 * A_DK + A_DV
A_CHUNK = 64
A_CONV = 4
B_GROUPS = 3
B_HEADS = 8
B_DH = 128
B_W = B_HEADS * B_DH
B_DIL = (1, 4, 16)
B_BLK = 128
ROPE_THETA = 500000.0
ROPE_DIMS = B_DH // 4
ADAM_LR, ADAM_B1, ADAM_B2, ADAM_EPS, ADAM_WD, ADAM_STEP = 0.001, 0.9, 0.999, 1e-08, 0.01, 10
N_CHIPS = 4
VMEM_BIG = 56 * 1024 * 1024


def _params(sem=None, vmem=None):
    return pltpu.CompilerParams(dimension_semantics=sem, vmem_limit_bytes=vmem)


def _dot(a, b, ca, cb):
    return lax.dot_general(a.astype(BF16), b.astype(BF16), (((ca,), (cb,)), ((), ())),
                           preferred_element_type=F32)


def _split3(a):
    hi = a.astype(BF16)
    r = a - hi.astype(F32)
    mid = r.astype(BF16)
    lo = (r - mid.astype(F32)).astype(BF16)
    return hi, mid, lo


def _sigmoid(y):
    return 1.0 / (1.0 + jnp.exp(-y))


def _silu(y):
    return y * _sigmoid(y)


def _dsilu(y):
    s = _sigmoid(y)
    return s * (1.0 + y * (1.0 - s))


def _matmul(a, b, mode, out_dtype, name, res=None, tm=1024, tn=1024, tk=1024, n=None, b_spec=None, into=None):
    m, k = a.shape[::-1] if mode == "tn" else a.shape
    if n is None:
        n = b.shape[0] if mode == "nt" else b.shape[1]
    tm, tn, tk = min(tm, m), min(tn, n), min(tk, k)
    assert m % tm == 0 and n % tn == 0 and k % tk == 0, (name, a.shape, b.shape)
    nk = k // tk
    dims = {"nn": ((1,), (0,)), "nt": ((1,), (1,)), "tn": ((0,), (0,))}[mode]

    def body(*refs):
        a_ref, b_ref = refs[0], refs[1]
        r_ref = refs[2] if res is not None else None
        o_ref = refs[2 + (res is not None) + (into is not None)]
        prod = lax.dot_general(a_ref[...], b_ref[...], (dims, ((), ())), preferred_element_type=F32)

        def finish(r):
            if res is not None:
                r = r + r_ref[...]
            o_ref[...] = r.astype(out_dtype)

        if nk == 1:
            finish(prod)
            return
        acc = refs[-1]
        kk = pl.program_id(2)

        @pl.when(kk == 0)
        def _():
            acc[...] = prod

        @pl.when((kk > 0) & (kk < nk - 1))
        def _():
            acc[...] += prod

        @pl.when(kk == nk - 1)
        def _():
            finish(acc[...] + prod)

    a_spec = pl.BlockSpec((tm, tk), lambda i, j, kk: (i, kk))
    if mode == "tn":
        a_spec = pl.BlockSpec((tk, tm), lambda i, j, kk: (kk, i))
    if b_spec is None and mode == "nt":
        b_spec = pl.BlockSpec((tn, tk), lambda i, j, kk: (j, kk))
    elif b_spec is None:
        b_spec = pl.BlockSpec((tk, tn), lambda i, j, kk: (kk, j))
    in_specs = [a_spec, b_spec]
    args = [a, b]
    if res is not None:
        in_specs.append(pl.BlockSpec((tm, tn), lambda i, j, kk: (i, j)))
        args.append(res)
    out_spec = pl.BlockSpec((tm, tn), lambda i, j, kk: (i, j))
    out_shape = jax.ShapeDtypeStruct((m, n), out_dtype)
    aliases = {}
    if into is not None:
        assert res is None
        buf, out_spec = into
        out_shape = jax.ShapeDtypeStruct(buf.shape, buf.dtype)
        in_specs.append(ANY)
        args.append(buf)
        aliases = {2: 0}
    return pl.pallas_call(
        body, name=name, grid=(m // tm, n // tn, nk),
        in_specs=in_specs, out_specs=out_spec, out_shape=out_shape, input_output_aliases=aliases,
        scratch_shapes=[pltpu.VMEM((tm, tn), F32)] if nk > 1 else [],
        compiler_params=_params(("parallel", "parallel", "arbitrary"), 48 * 1024 * 1024),
    )(*args)


def _rms_fwd(x, g, name, tm=256):
    t, d = x.shape

    def body(x_ref, g_ref, h_ref):
        xv = x_ref[...]
        r = lax.rsqrt(jnp.mean(xv * xv, axis=-1, keepdims=True) + EPS)
        h_ref[...] = (xv * r * g_ref[...]).astype(BF16)

    return pl.pallas_call(
        body, name=name, grid=(t // tm,),
        in_specs=[pl.BlockSpec((tm, d), lambda i: (i, 0)), pl.BlockSpec((1, d), lambda i: (0, 0))],
        out_specs=pl.BlockSpec((tm, d), lambda i: (i, 0)),
        out_shape=jax.ShapeDtypeStruct((t, d), BF16),
        compiler_params=_params(("parallel",)),
    )(x, g)


def _rms_bwd(x, g, dhs, dres, name, tm=256):
    t, d = x.shape
    n_dh = len(dhs)

    def body(*refs):
        x_ref, g_ref = refs[0], refs[1]
        dh_refs = refs[2:2 + n_dh]
        dres_ref, dx_ref, dg_ref = refs[2 + n_dh:]
        i = pl.program_id(0)

        @pl.when(i == 0)
        def _():
            dg_ref[...] = jnp.zeros_like(dg_ref)

        xv = x_ref[...]
        r = lax.rsqrt(jnp.mean(xv * xv, axis=-1, keepdims=True) + EPS)
        xh = xv * r
        dh = dh_refs[0][...]
        for ref in dh_refs[1:]:
            dh = dh + ref[...]
        dg_ref[0:1, :] += jnp.sum(dh * xh, axis=0, keepdims=True)
        dxh = dh * g_ref[...]
        dx = r * (dxh - xh * jnp.mean(dxh * xh, axis=-1, keepdims=True))
        dx_ref[...] = dx + dres_ref[...]

    row = pl.BlockSpec((tm, d), lambda i: (i, 0))
    dx, dg = pl.pallas_call(
        body, name=name, grid=(t // tm,),
        in_specs=[row, pl.BlockSpec((1, d), lambda i: (0, 0))] + [row] * n_dh + [row],
        out_specs=[row, pl.BlockSpec((8, d), lambda i: (0, 0))],
        out_shape=[jax.ShapeDtypeStruct((t, d), F32), jax.ShapeDtypeStruct((8, d), F32)],
        compiler_params=_params(("arbitrary",)),
    )(x, g, *dhs, dres)
    return dx, dg[0:1]


def _in_proj_bwd(dp, w, dh_more, x, g, dres, name, tm=512, tk=1024):
    t, k = dp.shape
    d = w.shape[0]
    nk = k // tk

    def body(dp_ref, w_ref, more_ref, x_ref, g_ref, dres_ref, dx_ref, dg_ref, acc):
        i, kk = pl.program_id(0), pl.program_id(1)

        @pl.when((i == 0) & (kk == 0))
        def _():
            dg_ref[...] = jnp.zeros_like(dg_ref)

        prod = lax.dot_general(dp_ref[...], w_ref[...], (((1,), (1,)), ((), ())), preferred_element_type=F32)

        @pl.when(kk == 0)
        def _():
            acc[...] = prod

        @pl.when((kk > 0) & (kk < nk - 1))
        def _():
            acc[...] += prod

        @pl.when(kk == nk - 1)
        def _():
            dh = acc[...] + prod + more_ref[...]
            xv = x_ref[...]
            r = lax.rsqrt(jnp.mean(xv * xv, axis=-1, keepdims=True) + EPS)
            xh = xv * r
            dg_ref[0:1, :] += jnp.sum(dh * xh, axis=0, keepdims=True)
            dxh = dh * g_ref[...]
            dx_ref[...] = r * (dxh - xh * jnp.mean(dxh * xh, axis=-1, keepdims=True)) + dres_ref[...]

    row = pl.BlockSpec((tm, d), lambda i, kk: (i, 0))
    dx, dg = pl.pallas_call(
        body, name=name, grid=(t // tm, nk),
        in_specs=[pl.BlockSpec((tm, tk), lambda i, kk: (i, kk)), pl.BlockSpec((d, tk), lambda i, kk: (0, kk)),
                  row, row, pl.BlockSpec((1, d), lambda i, kk: (0, 0)), row],
        out_specs=[row, pl.BlockSpec((8, d), lambda i, kk: (0, 0))],
        out_shape=[jax.ShapeDtypeStruct((t, d), F32), jax.ShapeDtypeStruct((8, d), F32)],
        scratch_shapes=[pltpu.VMEM((tm, d), F32)],
        compiler_params=_params(("arbitrary", "arbitrary"), 48 * 1024 * 1024),
    )(dp, w, dh_more, x, g, dres)
    return dx, dg[0:1]


def _out_proj(a, w, res, name, norm_g=None, target=None, tm=512):
    t, k = a.shape
    d = w.shape[1]
    nb = t // tm

    def body(a_ref, w_ref, r_ref, x_ref, o1_ref, o2_ref):
        y = jnp.dot(a_ref[...], w_ref[...], preferred_element_type=F32) + r_ref[...]
        if norm_g is not None:
            o1_ref[...] = y
            r = lax.rsqrt(jnp.mean(y * y, axis=-1, keepdims=True) + EPS)
            o2_ref[...] = (y * r * x_ref[...]).astype(BF16)
        else:
            e = y - x_ref[...]
            o1_ref[...] = e * (1.0 / d)
            s = jnp.sum(jnp.sum(e * e, axis=1, keepdims=True), axis=0, keepdims=True) * (0.5 / d)
            o2_ref[...] = jnp.broadcast_to(s, (8, 128))

    row = pl.BlockSpec((tm, d), lambda i: (i, 0))
    if norm_g is not None:
        extra, extra_spec = norm_g, pl.BlockSpec((1, d), lambda i: (0, 0))
        out2_spec, out2_shape = row, jax.ShapeDtypeStruct((t, d), BF16)
    else:
        extra, extra_spec = target, row
        out2_spec = pl.BlockSpec((None, 8, 128), lambda i: (i, 0, 0))
        out2_shape = jax.ShapeDtypeStruct((nb, 8, 128), F32)
    o1, o2 = pl.pallas_call(
        body, name=name, grid=(nb,),
        in_specs=[pl.BlockSpec((tm, k), lambda i: (i, 0)), pl.BlockSpec((k, d), lambda i: (0, 0)), row, extra_spec],
        out_specs=[row, out2_spec], out_shape=[jax.ShapeDtypeStruct((t, d), F32), out2_shape],
        compiler_params=_params(("parallel",), 48 * 1024 * 1024),
    )(a, w, res, extra)
    return (o1, o2) if norm_g is not None else (o1, o2[:, 0, 0])


def _softplus(x):
    t = jnp.exp(-jnp.abs(x))
    return jnp.maximum(x, 0.0) + jnp.where(t < 1e-3, t * (1.0 - 0.5 * t), jnp.log(1.0 + t))


def _tri(rows_le_cols):
    r = lax.broadcasted_iota(jnp.int32, (A_CHUNK, A_CHUNK), 0)
    c = lax.broadcasted_iota(jnp.int32, (A_CHUNK, A_CHUNK), 1)
    return jnp.where((r <= c) if rows_le_cols else (r >= c), 1.0, 0.0).astype(BF16)


def _dot_exact_rhs(a, ones_bf16):
    dn = (((1,), (0,)), ((), ()))
    hi, mid, lo = _split3(a)
    out = lax.dot_general(hi, ones_bf16, dn, preferred_element_type=F32)
    out = out + lax.dot_general(mid, ones_bf16, dn, preferred_element_type=F32)
    return out + lax.dot_general(lo, ones_bf16, dn, preferred_element_type=F32)


def _gdn_prep(tail_t, a_log, dt_bias):
    bn, _, n, c = tail_t.shape

    def body(t_ref, alog_ref, dtb_ref, beta_ref, gc_ref):
        upper = _tri(True)
        for h in range(A_HEADS):
            beta_ref[h] = _sigmoid(t_ref[h])
            ea = jnp.exp(jnp.full((n, c), alog_ref[h], F32))
            g = -ea * _softplus(t_ref[A_HEADS + h] + dtb_ref[h])
            gc_ref[h] = _dot_exact_rhs(g, upper)

    smem = pl.BlockSpec(memory_space=pltpu.SMEM)
    blk = pl.BlockSpec((None, A_HEADS, n, c), lambda b: (b, 0, 0, 0))
    return pl.pallas_call(
        body, name="gdn_prep", grid=(bn,),
        in_specs=[pl.BlockSpec((None, 2 * A_HEADS, n, c), lambda b: (b, 0, 0, 0)), smem, smem],
        out_specs=[blk, blk],
        out_shape=[jax.ShapeDtypeStruct((bn, A_HEADS, n, c), F32)] * 2,
        compiler_params=_params(("parallel",)),
    )(tail_t, a_log, dt_bias)


def _gdn_prep_bwd(tail_t, a_log, dt_bias, d_gc, d_beta):
    bn, _, n, c = tail_t.shape

    def body(t_ref, alog_ref, dtb_ref, dgc_ref, dbeta_ref, dt_ref, dal_ref, ddt_ref):
        lower = _tri(False)
        for h in range(A_HEADS):
            beta = _sigmoid(t_ref[h])
            dt_ref[h] = dbeta_ref[h] * beta * (1.0 - beta)
            dg = _dot_exact_rhs(dgc_ref[h], lower)
            ea = jnp.exp(jnp.full((n, c), alog_ref[h], F32))
            xa = t_ref[A_HEADS + h] + dtb_ref[h]
            g = -ea * _softplus(xa)
            dxa = -ea * dg * _sigmoid(xa)
            dt_ref[A_HEADS + h] = dxa
            s1 = jnp.sum(jnp.sum(g * dg, axis=1, keepdims=True), axis=0, keepdims=True)
            s2 = jnp.sum(jnp.sum(dxa, axis=1, keepdims=True), axis=0, keepdims=True)
            dal_ref[h:h + 1, :] = jnp.broadcast_to(s1, (1, 128))
            ddt_ref[h:h + 1, :] = jnp.broadcast_to(s2, (1, 128))

    smem = pl.BlockSpec(memory_space=pltpu.SMEM)
    blk8 = pl.BlockSpec((None, A_HEADS, n, c), lambda b: (b, 0, 0, 0))
    blk16 = pl.BlockSpec((None, 2 * A_HEADS, n, c), lambda b: (b, 0, 0, 0))
    sm = pl.BlockSpec((None, A_HEADS, 128), lambda b: (b, 0, 0))
    return pl.pallas_call(
        body, name="gdn_prep_bwd", grid=(bn,),
        in_specs=[blk16, smem, smem, blk8, blk8],
        out_specs=[blk16, sm, sm],
        out_shape=[jax.ShapeDtypeStruct((bn, 2 * A_HEADS, n, c), F32),
                   jax.ShapeDtypeStruct((bn, A_HEADS, 128), F32),
                   jax.ShapeDtypeStruct((bn, A_HEADS, 128), F32)],
        compiler_params=_params(("parallel",)),
    )(tail_t, a_log, dt_bias, d_gc, d_beta)


HALO = 8


def _conv_taps(xw, w):
    y = w[A_CONV - 1:A_CONV, :] * xw
    for j in range(1, A_CONV):
        y = y + w[A_CONV - 1 - j:A_CONV - j, :] * pltpu.roll(xw, j, 0)
    return y[HALO:, :]


def _row_to_col(row, eye):
    c = eye.shape[0]
    return jnp.sum(jnp.where(eye, jnp.broadcast_to(row, (c, c)), 0.0), axis=1, keepdims=True)


def _col_to_row(col, eye):
    c = eye.shape[0]
    return jnp.sum(jnp.where(eye, jnp.broadcast_to(col, (c, c)), 0.0), axis=0, keepdims=True)


def _unit_lower_inverse(a, ri, ci):
    eye = jnp.where(ri == ci, 1.0, 0.0)
    a8 = jnp.where((ri >> 3) == (ci >> 3), a, 0.0)
    a2 = _dot(a8, a8, 1, 0)
    yield
    a4 = _dot(a2, a2, 1, 0)
    t = eye - a8
    t = t + _dot(t, a2, 1, 0)
    yield
    t = t + _dot(t, a4, 1, 0)
    yield
    for sh in (3, 4, 5):
        off = jnp.where(((ri >> (sh + 1)) == (ci >> (sh + 1))) & ((ri >> sh) != (ci >> sh)), a, 0.0)
        left = _dot(t, off, 1, 0)
        yield
        t = t - _dot(left, t, 1, 0)
        yield
    return t


def _round_robin(gens):
    live = list(gens)
    while live:
        nxt = []
        for g in live:
            try:
                next(g)
                nxt.append(g)
            except StopIteration:
                pass
        live = nxt


def _gdn_chunk_core(q, k, v, g_row, b_row, t_mat, ri, ci):
    eye = ri == ci
    g_col = _row_to_col(g_row, eye)
    b_col = _row_to_col(b_row, eye)
    causal = ri >= ci
    strict = ri > ci
    dec = jnp.where(causal, jnp.exp(jnp.where(causal, g_col - g_row, 0.0)), 0.0)
    gam = jnp.exp(g_col)
    g_last = g_row[:, A_CHUNK - 1:A_CHUNK]
    gam_last = jnp.exp(g_last)
    e = jnp.exp(g_last - g_col)
    kb = k * b_col
    bv = v * b_col
    kbg = kb * gam
    q16, k16, kb16 = q.astype(BF16), k.astype(BF16), kb.astype(BF16)
    kk = _dot(kb16, k16, 1, 1)
    p = _dot(q16, k16, 1, 1) * dec
    yield
    a_mat = jnp.where(strict, kk * dec, 0.0)
    if t_mat is None:
        t_mat = yield from _unit_lower_inverse(a_mat, ri, ci)
    t16 = t_mat.astype(BF16)
    u = _dot(t16, bv, 1, 0)
    w = _dot(t16, kbg, 1, 0)
    yield
    return dict(eye=eye, g_col=g_col, b_col=b_col, dec=dec, strict=strict, causal=causal, gam=gam,
                gam_last=gam_last, e=e, kb=kb, bv=bv, kbg=kbg, a_mat=a_mat, t_mat=t_mat, u=u, w=w, p=p,
                qg=q * gam, kd=k * e, q16=q16, k16=k16, kb16=kb16, t16=t16)


A_SEQ_BLK = 256
A_BLK_CHUNKS = A_SEQ_BLK // A_CHUNK


def _gdn_halo(proj_hm):
    bn, s, w = proj_hm.shape
    last = proj_hm.reshape(bn, s // A_SEQ_BLK, A_SEQ_BLK, w)[:, :, A_SEQ_BLK - HALO:, :]
    return jnp.concatenate([jnp.zeros((bn, 1, HALO, w), proj_hm.dtype), last[:, :-1]], axis=1)


def _gdn_window(x_ref, halo_ref, ci, first, lo):
    if first:
        return jnp.concatenate([halo_ref[:, lo:lo + A_CONV_COLS], x_ref[0:A_CHUNK, lo:lo + A_CONV_COLS]], axis=0)
    start = pl.multiple_of(ci * A_CHUNK - HALO, HALO)
    return x_ref[pl.ds(start, A_CHUNK + HALO), lo:lo + A_CONV_COLS]


def _gdn_chunk_prep(xw, cw, y=None):
    if y is None:
        y = _conv_taps(xw, cw)
    a = _silu(y)
    aq, ak, v = a[:, 0:A_DK], a[:, A_DK:2 * A_DK], a[:, 2 * A_DK:]
    rq = lax.rsqrt(jnp.sum(aq * aq, axis=1, keepdims=True) + EPS)
    rk = lax.rsqrt(jnp.sum(ak * ak, axis=1, keepdims=True) + EPS)
    return dict(xw=xw, y=y, aq=aq, ak=ak, rq=rq, rk=rk, q=aq * rq * (A_DK ** -0.5), k=ak * rk, v=v)


def _gdn_fwd(proj_hm, cw_hm, beta, gc, norm_g, hp=8):
    bn, s, _ = proj_hm.shape
    n = s // A_CHUNK
    nsb = s // A_SEQ_BLK
    halo = _gdn_halo(proj_hm)

    def body(x_ref, halo_ref, cw_ref, beta_ref, gc_ref, ng_ref, og_ref, oraw_ref, st_ref, t_ref, y_ref, state):
        first_chunk = pl.program_id(2) * A_BLK_CHUNKS
        ri = lax.broadcasted_iota(jnp.int32, (A_CHUNK, A_CHUNK), 0)
        ci_ = lax.broadcasted_iota(jnp.int32, (A_CHUNK, A_CHUNK), 1)
        ng = ng_ref[...]

        @pl.when(pl.program_id(2) == 0)
        def _():
            state[...] = jnp.zeros_like(state)

        def one_head(hh, ci, first, rows):
            lo = hh * A_HEAD_COLS
            cw = cw_ref[:, hh * A_CONV_COLS:(hh + 1) * A_CONV_COLS]
            cin = _gdn_chunk_prep(_gdn_window(x_ref, halo_ref, ci, first, lo), cw)
            y_ref[rows, hh * A_CONV_COLS:(hh + 1) * A_CONV_COLS] = cin["y"]
            seq_chunk = pl.ds(first_chunk + ci, 1)
            core = yield from _gdn_chunk_core(cin["q"], cin["k"], cin["v"], gc_ref[hh, seq_chunk, :],
                                              beta_ref[hh, seq_chunk, :], None, ri, ci_)
            st = state[hh]
            st_ref[hh, ci] = st
            t_ref[hh, ci] = core["t_mat"]
            st16 = st.astype(BF16)
            vn = core["u"] - _dot(core["w"], st16, 1, 0)
            qs = _dot(core["qg"], st16, 1, 0)
            yield
            vn16 = vn.astype(BF16)
            o = qs + _dot(core["p"], vn16, 1, 0)
            state[hh] = st * core["gam_last"] + _dot(core["kd"], vn16, 0, 0)
            yield
            ocols = slice(hh * A_DV, (hh + 1) * A_DV)
            oraw_ref[rows, ocols] = o
            r = lax.rsqrt(jnp.mean(o * o, axis=1, keepdims=True) + EPS)
            z = x_ref[rows, lo + A_CONV_COLS:lo + A_HEAD_COLS]
            og_ref[rows, ocols] = (o * r * ng * _silu(z)).astype(BF16)

        def chunk(ci, first):
            rows = pl.ds(0 if first else pl.multiple_of(ci * A_CHUNK, A_CHUNK), A_CHUNK)
            _round_robin([one_head(hh, ci, first, rows) for hh in range(hp)])

        chunk(0, True)
        lax.fori_loop(1, A_BLK_CHUNKS, lambda i, c: (chunk(i, False), c)[1], 0)

    small = pl.BlockSpec((None, hp, n, A_CHUNK), lambda b, h, j: (b, h, 0, 0))
    return pl.pallas_call(
        body, name="gdn_fwd", grid=(bn, A_HEADS // hp, nsb),
        in_specs=[pl.BlockSpec((None, A_SEQ_BLK, hp * A_HEAD_COLS), lambda b, h, j: (b, j, h)),
                  pl.BlockSpec((None, None, HALO, hp * A_HEAD_COLS), lambda b, h, j: (b, j, 0, h)),
                  pl.BlockSpec((A_CONV, hp * A_CONV_COLS), lambda b, h, j: (0, h)),
                  small, small,
                  pl.BlockSpec((1, A_DV), lambda b, h, j: (0, 0))],
        out_specs=[pl.BlockSpec((None, A_SEQ_BLK, hp * A_DV), lambda b, h, j: (b, j, h)),
                   pl.BlockSpec((None, A_SEQ_BLK, hp * A_DV), lambda b, h, j: (b, j, h)),
                   pl.BlockSpec((None, hp, A_BLK_CHUNKS, A_DK, A_DV), lambda b, h, j: (b, h, j, 0, 0)),
                   pl.BlockSpec((None, hp, A_BLK_CHUNKS, A_CHUNK, A_CHUNK), lambda b, h, j: (b, h, j, 0, 0)),
                   pl.BlockSpec((None, A_SEQ_BLK, hp * A_CONV_COLS), lambda b, h, j: (b, j, h))],
        out_shape=[jax.ShapeDtypeStruct((bn, s, A_VW), BF16),
                   jax.ShapeDtypeStruct((bn, s, A_VW), F32),
                   jax.ShapeDtypeStruct((bn, A_HEADS, n, A_DK, A_DV), F32),
                   jax.ShapeDtypeStruct((bn, A_HEADS, n, A_CHUNK, A_CHUNK), F32),
                   jax.ShapeDtypeStruct((bn, s, A_HEADS * A_CONV_COLS), F32)],
        scratch_shapes=[pltpu.VMEM((hp, A_DK, A_DV), F32)],
        compiler_params=_params(("parallel", "parallel", "arbitrary"), VMEM_BIG),
    )(proj_hm, halo, cw_hm, beta, gc, norm_g)


def _gdn_bwd(proj_hm, cw_hm, beta, gc, norm_g, oraw, states, t_mats, conv_y, dog, hp=4):
    bn, s, _ = proj_hm.shape
    n = s // A_CHUNK
    nsb = s // A_SEQ_BLK
    halo = _gdn_halo(proj_hm)

    def body(x_ref, halo_ref, cw_ref, beta_ref, gc_ref, ng_ref, oraw_ref, st_ref, t_ref, y_ref, dog_ref,
             dx_ref, dgc_ref, dbeta_ref, dcw_ref, dng_ref, dstate, dy_next, shifted):
        first_chunk = (nsb - 1 - pl.program_id(2)) * A_BLK_CHUNKS
        ri = lax.broadcasted_iota(jnp.int32, (A_CHUNK, A_CHUNK), 0)
        ci_ = lax.broadcasted_iota(jnp.int32, (A_CHUNK, A_CHUNK), 1)
        lane = lax.broadcasted_iota(jnp.int32, (1, A_CHUNK), 1)
        ng = ng_ref[...]

        @pl.when(pl.program_id(2) == 0)
        def _():
            dstate[...] = jnp.zeros_like(dstate)
            dy_next[...] = jnp.zeros_like(dy_next)
            dcw_ref[...] = jnp.zeros_like(dcw_ref)
            dng_ref[...] = jnp.zeros_like(dng_ref)

        def one_head(hh, ci, first, rows):
            lo = hh * A_HEAD_COLS
            ccols = slice(hh * A_CONV_COLS, (hh + 1) * A_CONV_COLS)
            ocols = slice(hh * A_DV, (hh + 1) * A_DV)
            cw = cw_ref[:, ccols]
            cin = _gdn_chunk_prep(_gdn_window(x_ref, halo_ref, ci, first, lo), cw, y_ref[rows, ccols])
            q, k, v = cin["q"], cin["k"], cin["v"]
            seq_chunk = pl.ds(first_chunk + ci, 1)
            cr = yield from _gdn_chunk_core(q, k, v, gc_ref[hh, seq_chunk, :], beta_ref[hh, seq_chunk, :],
                                            t_ref[hh, ci], ri, ci_)
            eye, dec, gam, e = cr["eye"], cr["dec"], cr["gam"], cr["e"]
            b_col, t_mat, u, w, p = cr["b_col"], cr["t_mat"], cr["u"], cr["w"], cr["p"]
            st = st_ref[hh, ci]
            ds_out = dstate[hh]

            o = oraw_ref[rows, ocols]
            z = x_ref[rows, lo + A_CONV_COLS:lo + A_HEAD_COLS]
            d_og = dog_ref[rows, ocols]
            r = lax.rsqrt(jnp.mean(o * o, axis=1, keepdims=True) + EPS)
            oh = o * r
            d_on = d_og * _silu(z)
            dz = d_og * oh * ng * _dsilu(z)
            dng_ref[hh, 0:1, :] += jnp.sum(d_on * oh, axis=0, keepdims=True)
            d_oh = d_on * ng
            d_o = r * (d_oh - oh * jnp.mean(d_oh * oh, axis=1, keepdims=True))

            st16, ds16, do16, w16 = st.astype(BF16), ds_out.astype(BF16), d_o.astype(BF16), w.astype(BF16)
            q16, k16, t16 = cr["q16"], cr["k16"], cr["t16"]
            vn = u - _dot(w16, st16, 1, 0)
            d_vn = _dot(p, do16, 0, 0) + _dot(cr["kd"], ds16, 1, 0)
            d_qg = _dot(do16, st16, 1, 1)
            qgdo = _dot(cr["qg"], do16, 0, 0)
            yield
            vn16, dvn16 = vn.astype(BF16), d_vn.astype(BF16)
            d_p = jnp.where(cr["causal"], _dot(do16, vn16, 1, 1), 0.0)
            d_kd = _dot(vn16, ds16, 1, 1)
            d_gam_last = jnp.sum(jnp.sum(st * ds_out, axis=1, keepdims=True), axis=0, keepdims=True)
            d_w = -_dot(dvn16, st16, 1, 1)
            dstate[hh] = qgdo + ds_out * cr["gam_last"] - _dot(w16, dvn16, 0, 0)
            d_bv = _dot(t16, dvn16, 0, 0)
            yield
            d_kbg = _dot(t16, d_w, 0, 0)
            n_p = (d_p * dec).astype(BF16)
            d_q = _dot(n_p, k16, 1, 0) + d_qg * gam
            npq = _dot(n_p, q16, 0, 0)
            yield
            d_a = jnp.where(cr["strict"], -(_dot(d_bv, u, 1, 1) + _dot(d_kbg, w16, 1, 1)), 0.0)
            yield
            m_a = (d_a * dec).astype(BF16)
            d_kb = _dot(m_a, k16, 1, 0) + d_kbg * gam
            d_k = (_dot(m_a, cr["kb16"], 0, 0) + npq + d_kd * e + d_kb * b_col)
            yield
            d_v = d_bv * b_col
            d_beta_col = (jnp.sum(d_bv * v, axis=1, keepdims=True)
                          + jnp.sum(d_kb * k, axis=1, keepdims=True))
            gterm = d_a * cr["a_mat"] + d_p * p
            d_e = jnp.sum(d_kd * k, axis=1, keepdims=True) * e
            d_g_col = (jnp.sum(gterm, axis=1, keepdims=True)
                       + (jnp.sum(d_qg * q, axis=1, keepdims=True)
                          + jnp.sum(d_kbg * cr["kb"], axis=1, keepdims=True)) * gam
                       - d_e)
            d_g_last = jnp.sum(d_e, axis=0, keepdims=True) + d_gam_last * cr["gam_last"]
            d_g_row = (_col_to_row(d_g_col, eye) - jnp.sum(gterm, axis=0, keepdims=True)
                       + jnp.where(lane == A_CHUNK - 1, d_g_last, 0.0))
            dgc_ref[hh, seq_chunk, :] = d_g_row
            dbeta_ref[hh, seq_chunk, :] = _col_to_row(d_beta_col, eye)

            qh = cin["aq"] * cin["rq"]
            kh = cin["ak"] * cin["rk"]
            d_qh = d_q * (A_DK ** -0.5)
            d_aq = cin["rq"] * (d_qh - qh * jnp.sum(d_qh * qh, axis=1, keepdims=True))
            d_ak = cin["rk"] * (d_k - kh * jnp.sum(d_k * kh, axis=1, keepdims=True))
            d_y = jnp.concatenate([d_aq, d_ak, d_v], axis=1) * _dsilu(cin["y"])
            shifted[hh, 0, 0:A_CHUNK, :] = d_y
            shifted[hh, 0, A_CHUNK:A_CHUNK + HALO, :] = dy_next[hh]
            shifted[hh, 1, 0:A_CHUNK + HALO, :] = cin["xw"]
            d_x = cw[A_CONV - 1:A_CONV, :] * d_y
            for j in range(1, A_CONV):
                d_x = d_x + cw[A_CONV - 1 - j:A_CONV - j, :] * shifted[hh, 0, j:j + A_CHUNK, :]
            for j in range(A_CONV):
                xs = shifted[hh, 1, HALO - j:HALO - j + A_CHUNK, :]
                dcw_ref[A_CONV - 1 - j:A_CONV - j, ccols] += jnp.sum(d_y * xs, axis=0, keepdims=True)
            dy_next[hh] = d_y[0:HALO, :]
            dx_ref[rows, lo:lo + A_CONV_COLS] = d_x.astype(BF16)
            dx_ref[rows, lo + A_CONV_COLS:lo + A_HEAD_COLS] = dz.astype(BF16)

        def chunk(ci, first):
            rows = pl.ds(0 if first else pl.multiple_of(ci * A_CHUNK, A_CHUNK), A_CHUNK)
            _round_robin([one_head(hh, ci, first, rows) for hh in range(hp)])

        lax.fori_loop(0, A_BLK_CHUNKS - 1, lambda i, c: (chunk(A_BLK_CHUNKS - 1 - i, False), c)[1], 0)
        chunk(0, True)

    rev = lambda j: nsb - 1 - j
    small = pl.BlockSpec((None, hp, n, A_CHUNK), lambda b, h, j: (b, h, 0, 0))
    wide = pl.BlockSpec((None, A_SEQ_BLK, hp * A_HEAD_COLS), lambda b, h, j: (b, rev(j), h))
    val = pl.BlockSpec((None, A_SEQ_BLK, hp * A_DV), lambda b, h, j: (b, rev(j), h))
    return pl.pallas_call(
        body, name="gdn_bwd", grid=(bn, A_HEADS // hp, nsb),
        in_specs=[wide,
                  pl.BlockSpec((None, None, HALO, hp * A_HEAD_COLS), lambda b, h, j: (b, rev(j), 0, h)),
                  pl.BlockSpec((A_CONV, hp * A_CONV_COLS), lambda b, h, j: (0, h)),
                  small, small,
                  pl.BlockSpec((1, A_DV), lambda b, h, j: (0, 0)),
                  val,
                  pl.BlockSpec((None, hp, A_BLK_CHUNKS, A_DK, A_DV), lambda b, h, j: (b, h, rev(j), 0, 0)),
                  pl.BlockSpec((None, hp, A_BLK_CHUNKS, A_CHUNK, A_CHUNK), lambda b, h, j: (b, h, rev(j), 0, 0)),
                  pl.BlockSpec((None, A_SEQ_BLK, hp * A_CONV_COLS), lambda b, h, j: (b, rev(j), h)),
                  val],
        out_specs=[wide, small, small,
                   pl.BlockSpec((None, A_CONV, hp * A_CONV_COLS), lambda b, h, j: (b, 0, h)),
                   pl.BlockSpec((None, hp, 8, A_DV), lambda b, h, j: (b, h, 0, 0))],
        out_shape=[jax.ShapeDtypeStruct((bn, s, A_HEADS * A_HEAD_COLS), BF16),
                   jax.ShapeDtypeStruct((bn, A_HEADS, n, A_CHUNK), F32),
                   jax.ShapeDtypeStruct((bn, A_HEADS, n, A_CHUNK), F32),
                   jax.ShapeDtypeStruct((bn, A_CONV, A_HEADS * A_CONV_COLS), F32),
                   jax.ShapeDtypeStruct((bn, A_HEADS, 8, A_DV), F32)],
        scratch_shapes=[pltpu.VMEM((hp, A_DK, A_DV), F32), pltpu.VMEM((hp, HALO, A_CONV_COLS), F32),
                        pltpu.VMEM((hp, 2, A_CHUNK + 2 * HALO, A_CONV_COLS), F32)],
        compiler_params=_params(("parallel", "parallel", "arbitrary"), VMEM_BIG),
    )(proj_hm, halo, cw_hm, beta, gc, norm_g, oraw, states, t_mats, conv_y, dog)


def _rope_tables(posf, inv_freq_row):
    t = posf.shape[0]
    tm = 512

    def body(p_ref, f_ref, c_ref, sa_ref, sb_ref):
        ang = p_ref[...] * f_ref[...]
        lane = lax.broadcasted_iota(jnp.int32, ang.shape, 1)
        half = ROPE_DIMS // 2
        c_ref[...] = jnp.where(lane < ROPE_DIMS, jnp.cos(ang), 1.0)
        sn = jnp.sin(ang)
        sa_ref[...] = jnp.where(lane < half, -sn, 0.0)
        sb_ref[...] = jnp.where((lane >= half) & (lane < ROPE_DIMS), sn, 0.0)

    row = pl.BlockSpec((tm, 128), lambda i: (i, 0))
    return pl.pallas_call(
        body, name="rope_tables", grid=(t // tm,),
        in_specs=[row, pl.BlockSpec((1, 128), lambda i: (0, 0))], out_specs=[row] * 3,
        out_shape=[jax.ShapeDtypeStruct((t, 128), F32)] * 3,
        compiler_params=_params(("parallel",)),
    )(posf, inv_freq_row)


def _qk_prep(proj, c, sa, sb, qg, kg, name, tm=256):
    t = proj.shape[0]

    def body(x_ref, c_ref, sa_ref, sb_ref, qg_ref, kg_ref, o_ref):
        cc, s1, s2 = c_ref[...], sa_ref[...], sb_ref[...]
        half = ROPE_DIMS // 2

        def one_head(lo, g):
            xv = x_ref[:, lo:lo + B_DH]
            ms = jnp.mean(xv * xv, axis=1, keepdims=True)
            yield
            xn = xv * lax.rsqrt(ms + EPS) * g
            r1, r2 = pltpu.roll(xn, 128 - half, 1), pltpu.roll(xn, half, 1)
            yield
            o_ref[:, lo:lo + B_DH] = (xn * cc + r1 * s1 + r2 * s2).astype(BF16)

        for which, g_ref in ((0, qg_ref), (1, kg_ref)):
            g = g_ref[...]
            _round_robin([one_head(which * B_W + h * B_DH, g) for h in range(B_HEADS)])
        o_ref[:, 2 * B_W:3 * B_W] = x_ref[:, 2 * B_W:3 * B_W].astype(BF16)

    tab = pl.BlockSpec((tm, 128), lambda i: (i, 0))
    gain = pl.BlockSpec((1, B_DH), lambda i: (0, 0))
    return pl.pallas_call(
        body, name=name, grid=(t // tm,),
        in_specs=[pl.BlockSpec((tm, 3 * B_W), lambda i: (i, 0)), tab, tab, tab, gain, gain],
        out_specs=pl.BlockSpec((tm, 3 * B_W), lambda i: (i, 0)),
        out_shape=jax.ShapeDtypeStruct((t, 3 * B_W), BF16),
        compiler_params=_params(("parallel",), 40 * 1024 * 1024),
    )(proj, c, sa, sb, qg, kg)


def _qk_prep_bwd(proj, c, sa, sb, qg, kg, dq, dk, dv, dz, name, tm=256):
    t = proj.shape[0]
    out_w = 3 * B_W + (B_W if dz is not None else 0)

    def body(*refs):
        x_ref, c_ref, sa_ref, sb_ref, qg_ref, kg_ref, dq_ref, dk_ref, dv_ref = refs[:9]
        if dz is not None:
            dz_ref, o_ref, dgain_ref = refs[9:]
        else:
            o_ref, dgain_ref = refs[9:]
        i = pl.program_id(0)

        @pl.when(i == 0)
        def _():
            dgain_ref[...] = jnp.zeros_like(dgain_ref)

        cc, s1, s2 = c_ref[...], sa_ref[...], sb_ref[...]
        half = ROPE_DIMS // 2

        def one_head(which, h, g, d_ref, parts):
            lo = which * B_W + h * B_DH
            xv = x_ref[:, lo:lo + B_DH]
            d_out = d_ref[:, h * B_DH:(h + 1) * B_DH].astype(F32)
            ms = jnp.mean(xv * xv, axis=1, keepdims=True)
            r1, r2 = pltpu.roll(d_out * s1, half, 1), pltpu.roll(d_out * s2, 128 - half, 1)
            yield
            r = lax.rsqrt(ms + EPS)
            xh = xv * r
            d_xn = d_out * cc + r1 + r2
            parts.append(jnp.sum(d_xn * xh, axis=0, keepdims=True))
            d_xh = d_xn * g
            dot = jnp.mean(d_xh * xh, axis=1, keepdims=True)
            yield
            o_ref[:, lo:lo + B_DH] = (r * (d_xh - xh * dot)).astype(BF16)

        for which, g_ref, d_ref in ((0, qg_ref, dq_ref), (1, kg_ref, dk_ref)):
            parts = []
            _round_robin([one_head(which, h, g_ref[...], d_ref, parts) for h in range(B_HEADS)])
            acc = parts[0]
            for part in parts[1:]:
                acc = acc + part
            dgain_ref[which:which + 1, :] += acc
        o_ref[:, 2 * B_W:3 * B_W] = dv_ref[...]
        if dz is not None:
            o_ref[:, 3 * B_W:4 * B_W] = dz_ref[...]

    tab = pl.BlockSpec((tm, 128), lambda i: (i, 0))
    gain = pl.BlockSpec((1, B_DH), lambda i: (0, 0))
    grad = pl.BlockSpec((tm, B_W), lambda i: (i, 0))
    in_specs = [pl.BlockSpec((tm, 2 * B_W), lambda i: (i, 0)), tab, tab, tab, gain, gain, grad, grad, grad]
    args = [proj, c, sa, sb, qg, kg, dq, dk, dv]
    if dz is not None:
        in_specs.append(grad)
        args.append(dz)
    return pl.pallas_call(
        body, name=name, grid=(t // tm,), in_specs=in_specs,
        out_specs=[pl.BlockSpec((tm, out_w), lambda i: (i, 0)), pl.BlockSpec((8, B_DH), lambda i: (0, 0))],
        out_shape=[jax.ShapeDtypeStruct((t, out_w), BF16), jax.ShapeDtypeStruct((8, B_DH), F32)],
        compiler_params=_params(("arbitrary",), 40 * 1024 * 1024),
    )(*args)


def _attn_masks():
    qi = lax.broadcasted_iota(jnp.int32, (B_BLK, 2 * B_BLK), 0)
    kj = lax.broadcasted_iota(jnp.int32, (B_BLK, 2 * B_BLK), 1)
    two = (kj >= qi) & (kj <= qi + B_BLK)
    q1 = lax.broadcasted_iota(jnp.int32, (B_BLK, B_BLK), 0)
    k1 = lax.broadcasted_iota(jnp.int32, (B_BLK, B_BLK), 1)
    return k1 <= q1, two


def _lane_pick(ref_rows, h):
    lane = lax.broadcasted_iota(jnp.int32, ref_rows.shape, 1)
    return jnp.sum(jnp.where(lane == h, ref_rows, 0.0), axis=1, keepdims=True)


B_ROWS = 2048


def _attn_schedule(nb, sb, block):
    way = 16

    def run(items):
        for at in range(0, len(items), way):
            _round_robin([block(*it) for it in items[at:at + way]])

    run([(si, 0, True) for si in range(sb)])
    if nb == 1:
        return
    per = max(1, way // sb)
    lead = 1 + (nb - 1) % per
    if lead > 1:
        run([(si, i, False) for i in range(1, lead) for si in range(sb)])

    def step(it, carry):
        run([(si, lead + it * per + u, False) for u in range(per) for si in range(sb)])
        return carry

    lax.fori_loop(0, (nb - lead) // per, step, 0)


def _attn_rows(i, first):
    if first:
        return pl.ds(0, B_BLK), pl.ds(0, B_BLK)
    rows = pl.ds(pl.multiple_of(i * B_BLK, B_BLK), B_BLK)
    return rows, pl.ds(pl.multiple_of((i - 1) * B_BLK, B_BLK), 2 * B_BLK)


def _attn_fwd(qkv, name):
    ns, ln, _ = qkv.shape
    nb = ln // B_BLK
    sb = B_ROWS // ln
    scale = B_DH ** -0.5

    def body(q_ref, k_ref, v_ref, o_ref, lse_ref):
        h = pl.program_id(1)
        mask1, mask2 = _attn_masks()
        lane = lax.broadcasted_iota(jnp.int32, (B_BLK, B_HEADS), 1)

        @pl.when(h == 0)
        def _():
            lse_ref[...] = jnp.zeros_like(lse_ref)

        def block(si, i, first):
            rows, win = _attn_rows(i, first)
            mask = mask1 if first else mask2
            sc = jnp.where(mask, _dot(q_ref[si, rows, :], k_ref[si, win, :], 1, 1) * scale, -1e30)
            yield
            m = jnp.max(sc, axis=1, keepdims=True)
            p = jnp.exp(sc - m)
            l = jnp.sum(p, axis=1, keepdims=True)
            pv = _dot(p, v_ref[si, win, :], 1, 0)
            yield
            o_ref[si, rows, :] = pv / l
            lse_ref[si, rows, :] = jnp.where(lane == h, m + jnp.log(l), lse_ref[si, rows, :])

        _attn_schedule(nb, sb, block)

    head = lambda off: pl.BlockSpec((sb, ln, B_DH), lambda s, h: (s, 0, off + h))
    return pl.pallas_call(
        body, name=name, grid=(ns // sb, B_HEADS),
        in_specs=[head(0), head(B_HEADS), head(2 * B_HEADS)],
        out_specs=[head(0), pl.BlockSpec((sb, ln, B_HEADS), lambda s, h: (s, 0, 0))],
        out_shape=[jax.ShapeDtypeStruct((ns, ln, B_W), F32), jax.ShapeDtypeStruct((ns, ln, B_HEADS), F32)],
        compiler_params=_params(("parallel", "arbitrary")),
    )(qkv, qkv, qkv)


def _attn_bwd(qkv, d_o, lse_joint, delta, name):
    ns, ln, _ = qkv.shape
    nb = ln // B_BLK
    sb = B_ROWS // ln
    scale = B_DH ** -0.5

    def body(q_ref, k_ref, v_ref, do_ref, lj_ref, dl_ref, dq_ref, dk_out, dv_out, dk_ref, dv_ref):
        h = pl.program_id(1)
        mask1, mask2 = _attn_masks()
        dk_ref[...] = jnp.zeros_like(dk_ref)
        dv_ref[...] = jnp.zeros_like(dv_ref)

        def block(si, i, first):
            rows, win = _attn_rows(i, first)
            mask = mask1 if first else mask2
            q = q_ref[si, rows, :]
            d_out = do_ref[si, rows, :]
            l_col = _lane_pick(lj_ref[si, rows, :], h)
            d_col = _lane_pick(dl_ref[si, rows, :], h)
            sc = _dot(q, k_ref[si, win, :], 1, 1) * scale
            d_p = _dot(d_out, v_ref[si, win, :], 1, 1)
            yield
            p = jnp.exp(jnp.where(mask, sc - l_col, -1e30))
            d_s = p * (d_p - d_col) * scale
            d_q = _dot(d_s, k_ref[si, win, :], 1, 0)
            d_k = _dot(d_s, q, 0, 0)
            d_v = _dot(p, d_out, 0, 0)
            yield
            dq_ref[si, rows, :] = d_q.astype(BF16)
            dk_ref[si, win, :] += d_k
            dv_ref[si, win, :] += d_v

        _attn_schedule(nb, sb, block)
        dk_out[...] = dk_ref[...].astype(BF16)
        dv_out[...] = dv_ref[...].astype(BF16)

    head = lambda off: pl.BlockSpec((sb, ln, B_DH), lambda s, h: (s, 0, off + h))
    small = pl.BlockSpec((sb, ln, B_HEADS), lambda s, h: (s, 0, 0))
    return pl.pallas_call(
        body, name=name, grid=(ns // sb, B_HEADS),
        in_specs=[head(0), head(B_HEADS), head(2 * B_HEADS), head(0), small, small],
        out_specs=[head(0)] * 3,
        out_shape=[jax.ShapeDtypeStruct((ns, ln, B_W), BF16)] * 3,
        scratch_shapes=[pltpu.VMEM((sb, ln, B_DH), F32)] * 2,
        compiler_params=_params(("parallel", "parallel")),
    )(qkv, qkv, qkv, d_o, lse_joint, delta)


def _merge_weights(lse_refs):
    ls = [r[...] for r in lse_refs]
    m = jnp.maximum(jnp.maximum(ls[0], ls[1]), ls[2])
    es = [jnp.exp(l - m) for l in ls]
    tot = es[0] + es[1] + es[2]
    return [e / tot for e in es], m + jnp.log(tot)


def _merge_fwd(outs, lses, proj0, tm=256):
    t = outs[0].shape[0]

    def body(o0, o1, o2, l0, l1, l2, z_ref, og_ref):
        wts, _ = _merge_weights((l0, l1, l2))

        def one_head(h):
            cols = slice(h * B_DH, (h + 1) * B_DH)
            w0, w1, w2 = (jnp.broadcast_to(w[:, h:h + 1], (tm, B_DH)) for w in wts)
            yield
            o = w0 * o0[:, cols] + w1 * o1[:, cols] + w2 * o2[:, cols]
            og_ref[:, cols] = (o * _silu(z_ref[:, cols])).astype(BF16)

        _round_robin([one_head(h) for h in range(B_HEADS)])

    wide = pl.BlockSpec((tm, B_W), lambda i: (i, 0))
    small = pl.BlockSpec((tm, B_HEADS), lambda i: (i, 0))
    return pl.pallas_call(
        body, name="merge_fwd", grid=(t // tm,),
        in_specs=[wide] * 3 + [small] * 3 + [pl.BlockSpec((tm, B_W), lambda i: (i, 3))],
        out_specs=wide, out_shape=jax.ShapeDtypeStruct((t, B_W), BF16),
        compiler_params=_params(("parallel",)),
    )(*outs, *lses, proj0)


def _merge_bwd(outs, lses, proj0, d_og, tm=256):
    t = outs[0].shape[0]

    def body(o0, o1, o2, l0, l1, l2, z_ref, dog_ref, do_ref, lj_ref, dl_ref, dz_ref):
        wts, lj = _merge_weights((l0, l1, l2))
        lj_ref[...] = lj
        lane = lax.broadcasted_iota(jnp.int32, (tm, B_HEADS), 1)
        sums = [None] * B_HEADS

        def one_head(h):
            cols = slice(h * B_DH, (h + 1) * B_DH)
            w0, w1, w2 = (jnp.broadcast_to(w[:, h:h + 1], (tm, B_DH)) for w in wts)
            yield
            o = w0 * o0[:, cols] + w1 * o1[:, cols] + w2 * o2[:, cols]
            z = z_ref[:, cols]
            d_g = dog_ref[:, cols]
            d_out = d_g * _silu(z)
            dz_ref[:, cols] = (d_g * o * _dsilu(z)).astype(BF16)
            do_ref[:, cols] = d_out.astype(BF16)
            sums[h] = jnp.sum(d_out * o, axis=1, keepdims=True)
            yield

        _round_robin([one_head(h) for h in range(B_HEADS)])
        delta = jnp.zeros((tm, B_HEADS), F32)
        for h in range(B_HEADS):
            delta = jnp.where(lane == h, sums[h], delta)
        dl_ref[...] = delta

    wide = pl.BlockSpec((tm, B_W), lambda i: (i, 0))
    small = pl.BlockSpec((tm, B_HEADS), lambda i: (i, 0))
    return pl.pallas_call(
        body, name="merge_bwd", grid=(t // tm,),
        in_specs=[wide] * 3 + [small] * 3 + [pl.BlockSpec((tm, B_W), lambda i: (i, 3)), wide],
        out_specs=[wide, small, small, wide],
        out_shape=[jax.ShapeDtypeStruct((t, B_W), BF16), jax.ShapeDtypeStruct((t, B_HEADS), F32),
                   jax.ShapeDtypeStruct((t, B_HEADS), F32), jax.ShapeDtypeStruct((t, B_W), BF16)],
        compiler_params=_params(("parallel",)),
    )(*outs, *lses, proj0, d_og)


def _adamw(w, g, m, v, name):
    r, c = w.shape
    tr = r
    for cand in (256, 128, 64, 32, 16, 8):
        if r % cand == 0:
            tr = cand
            break

    def body(w_ref, g_ref, m_ref, v_ref, d_ref, nm_ref, nv_ref):
        gv = g_ref[...]
        nm = ADAM_B1 * m_ref[...] + (1.0 - ADAM_B1) * gv
        nv = ADAM_B2 * v_ref[...] + (1.0 - ADAM_B2) * (gv * gv)
        m_hat = nm / (1.0 - ADAM_B1 ** ADAM_STEP)
        v_hat = nv / (1.0 - ADAM_B2 ** ADAM_STEP)
        d_ref[...] = -ADAM_LR * (m_hat / (jnp.sqrt(v_hat) + ADAM_EPS) + ADAM_WD * w_ref[...])
        nm_ref[...] = nm
        nv_ref[...] = nv

    blk = pl.BlockSpec((tr, c), lambda i: (i, 0))
    return pl.pallas_call(
        body, name=name, grid=(r // tr,), in_specs=[blk] * 4, out_specs=[blk] * 3,
        out_shape=[jax.ShapeDtypeStruct((r, c), F32)] * 3,
        compiler_params=_params(("parallel",)),
    )(w, g, m, v)


def _adam_update(w, gv, m, v):
    nm = ADAM_B1 * m + (1.0 - ADAM_B1) * gv
    nv = ADAM_B2 * v + (1.0 - ADAM_B2) * (gv * gv)
    m_hat = nm / (1.0 - ADAM_B1 ** ADAM_STEP)
    v_hat = nv / (1.0 - ADAM_B2 ** ADAM_STEP)
    return -ADAM_LR * (m_hat / (jnp.sqrt(v_hat) + ADAM_EPS) + ADAM_WD * w), nm, nv


def _adamw_shard(w, mine, theirs, m, v, half_index, name, tr=128):
    _, r, c = w.shape
    nhb = (r // 2) // tr

    def body(c_ref, w_ref, mine_ref, theirs_ref, m_ref, v_ref, g_ref, d_ref, nm_ref, nv_ref):
        is_mine = (pl.program_id(0) // nhb) == c_ref[0]
        gv = jnp.where(is_mine, mine_ref[...], theirs_ref[...])
        d, nm, nv = _adam_update(w_ref[...], gv, m_ref[...], v_ref[...])
        g_ref[...] = gv
        d_ref[...] = d
        nm_ref[...] = nm
        nv_ref[...] = nv

    full = pl.BlockSpec((None, tr, c), lambda i, cc: (0, i, 0))
    half = pl.BlockSpec((tr, c), lambda i, cc: (i % nhb, 0))
    return pl.pallas_call(
        body, name=name,
        grid_spec=pltpu.PrefetchScalarGridSpec(
            num_scalar_prefetch=1, grid=(2 * nhb,),
            in_specs=[full, half, half, full, full], out_specs=[full] * 4),
        out_shape=[jax.ShapeDtypeStruct(w.shape, F32)] * 4,
        compiler_params=_params(("parallel",), 40 * 1024 * 1024),
    )(half_index, w, mine, theirs, m, v)


def _adamw_shard_cols(w, mine, theirs, m, v, half_index, name, steps=20):
    c, _, r = w.shape
    tc = c // steps
    assert tc * steps == c

    def body(c_ref, w_ref, mine_ref, theirs_ref, m_ref, v_ref, g_ref, d_ref, nm_ref, nv_ref):
        first = jnp.where(c_ref[0] == 0, mine_ref[...], theirs_ref[...])
        second = jnp.where(c_ref[0] == 0, theirs_ref[...], mine_ref[...])
        for lo, gv in ((0, first), (r // 2, second)):
            cols = slice(lo, lo + r // 2)
            d, nm, nv = _adam_update(w_ref[:, :, cols], gv, m_ref[:, :, cols], v_ref[:, :, cols])
            g_ref[:, :, cols] = gv
            d_ref[:, :, cols] = d
            nm_ref[:, :, cols] = nm
            nv_ref[:, :, cols] = nv

    full = pl.BlockSpec((tc, 1, r), lambda i, cc: (i, 0, 0))
    half = pl.BlockSpec((tc, 1, r // 2), lambda i, cc: (i, 0, 0))
    return pl.pallas_call(
        body, name=name,
        grid_spec=pltpu.PrefetchScalarGridSpec(
            num_scalar_prefetch=1, grid=(steps,),
            in_specs=[full, half, half, full, full], out_specs=[full] * 4),
        out_shape=[jax.ShapeDtypeStruct(w.shape, F32)] * 4,
        compiler_params=_params(("parallel",), 40 * 1024 * 1024),
    )(half_index, w, mine, theirs, m, v)


def _pair_sum(own, other, half_index, name, tr=256):
    _, r, c = own.shape
    rh = r // 2
    tr = min(tr, rh)
    nrb = rh // tr

    def body(c_ref, own_ref, oth_ref, out_ref):
        out_ref[...] = (own_ref[...] + oth_ref[...].astype(F32)).astype(BF16)

    return pl.pallas_call(
        body, name=name,
        grid_spec=pltpu.PrefetchScalarGridSpec(
            num_scalar_prefetch=1, grid=(N_CHIPS, nrb),
            in_specs=[pl.BlockSpec((None, tr, c), lambda k, i, cc: (k, cc[0] * nrb + i, 0)),
                      pl.BlockSpec((None, tr, c), lambda k, i, cc: (k, i, 0))],
            out_specs=pl.BlockSpec((None, tr, c), lambda k, i, cc: (k, i, 0))),
        out_shape=jax.ShapeDtypeStruct((N_CHIPS, rh, c), BF16),
        compiler_params=_params(("parallel", "parallel")),
    )(half_index, own, other)


def _chip_sum(sums, others, chip_index, name, tr=256):
    _, r, c = sums.shape
    tr = min(tr, r)

    def body(k_ref, own_ref, oth_ref, out_ref):
        acc = own_ref[...].astype(F32)
        for j in range(N_CHIPS - 1):
            acc = acc + oth_ref[j].astype(F32)
        out_ref[...] = acc

    return pl.pallas_call(
        body, name=name,
        grid_spec=pltpu.PrefetchScalarGridSpec(
            num_scalar_prefetch=1, grid=(r // tr,),
            in_specs=[pl.BlockSpec((None, tr, c), lambda i, kk: (kk[0], i, 0)),
                      pl.BlockSpec((N_CHIPS - 1, tr, c), lambda i, kk: (0, i, 0))],
            out_specs=pl.BlockSpec((tr, c), lambda i, kk: (i, 0))),
        out_shape=jax.ShapeDtypeStruct((r, c), F32),
        compiler_params=_params(("parallel",)),
    )(chip_index, sums, others)


HBM = pl.BlockSpec(memory_space=pltpu.HBM)


def _place():
    x, y, c = lax.axis_index("x"), lax.axis_index("y"), lax.axis_index("c")
    chips = [(1 - x, y), (x, 1 - y), (1 - x, 1 - y)]
    return x, y, c, chips


def _sibling_forward(land):
    def body(in_ref, out_ref, send, recv):
        x, y, c, chips = _place()
        rh = out_ref.shape[1] // 2
        cps = []
        for j, (px, py) in enumerate(chips):
            slot = out_ref.at[2 * px + py, pl.ds(c * rh, rh)]
            cp = pltpu.make_async_remote_copy(
                src_ref=slot, dst_ref=slot, send_sem=send.at[j], recv_sem=recv.at[j],
                device_id=(x, y, 1 - c), device_id_type=MESH)
            cp.start()
            cps.append(cp)
        for j, (px, py) in enumerate(chips):
            slot = out_ref.at[2 * px + py, pl.ds((1 - c) * rh, rh)]
            pltpu.make_async_remote_copy(
                src_ref=slot, dst_ref=slot, send_sem=send.at[j], recv_sem=recv.at[j],
                device_id=(x, y, 1 - c), device_id_type=MESH).wait_recv()
        for cp in cps:
            cp.wait_send()

    return pl.pallas_call(
        body, name="first_weights_sibling_forward", in_specs=[HBM], out_specs=HBM,
        out_shape=jax.ShapeDtypeStruct(land.shape, land.dtype), input_output_aliases={0: 0},
        scratch_shapes=[pltpu.SemaphoreType.DMA((3,)), pltpu.SemaphoreType.DMA((3,))],
    )(land)


def _sibling_swap_halves(grads, name):
    na = len(grads)

    def body(*refs):
        ins, outs = refs[:na], refs[na:2 * na]
        send, recv = refs[2 * na:]
        x, y, c, _ = _place()
        sib = (x, y, 1 - c)
        cps = []
        for i in range(na):
            rh = ins[i].shape[1] // 2
            cp = pltpu.make_async_remote_copy(
                src_ref=ins[i].at[:, pl.ds((1 - c) * rh, rh), :], dst_ref=outs[i],
                send_sem=send.at[i], recv_sem=recv.at[i], device_id=sib, device_id_type=MESH)
            cp.start()
            cps.append(cp)
        for cp in cps:
            cp.wait()

    out_shape = [jax.ShapeDtypeStruct((g.shape[0], g.shape[1] // 2, g.shape[2]), g.dtype) for g in grads]
    return pl.pallas_call(
        body, name=name, in_specs=[HBM] * na, out_specs=[HBM] * na, out_shape=out_shape,
        scratch_shapes=[pltpu.SemaphoreType.DMA((na,)), pltpu.SemaphoreType.DMA((na,))],
    )(*grads)


def _sibling_swap_whole(halves):
    na = len(halves)

    def body(*refs):
        ins, outs = refs[:na], refs[na:2 * na]
        send, recv = refs[2 * na:]
        x, y, c, _ = _place()
        cps = []
        for i in range(na):
            cp = pltpu.make_async_remote_copy(
                src_ref=ins[i], dst_ref=outs[i], send_sem=send.at[i], recv_sem=recv.at[i],
                device_id=(x, y, 1 - c), device_id_type=MESH)
            cp.start()
            cps.append(cp)
        for cp in cps:
            cp.wait()

    out_shape = [jax.ShapeDtypeStruct(h.shape, h.dtype) for h in halves]
    return pl.pallas_call(
        body, name="grad_sibling_join", in_specs=[HBM] * na, out_specs=[HBM] * na, out_shape=out_shape,
        scratch_shapes=[pltpu.SemaphoreType.DMA((na,)), pltpu.SemaphoreType.DMA((na,))],
    )(*halves)


SEM = pl.BlockSpec(memory_space=pltpu.SEMAPHORE)
ANY = pl.BlockSpec(memory_space=pl.ANY)
EFFECT = pltpu.SideEffectType.DATAFLOW_SIDE_EFFECTING


def _split_copy_start(name, plan, srcs, lands, after):
    ns, nl = len(srcs), len(lands)

    def body(*refs):
        src_refs, land_refs = refs[:ns], refs[ns:ns + nl]
        send, recv = refs[ns + nl + 1], refs[ns + nl + 2]
        token = refs[-1]
        outgoing, _ = plan(src_refs, land_refs)
        for src, dst, dev, si, ri in outgoing:
            pltpu.make_async_remote_copy(src_ref=src, dst_ref=dst, send_sem=send.at[si], recv_sem=recv.at[ri],
                                         device_id=dev, device_id_type=MESH).start()
        token[...] = jnp.zeros_like(token)

    n_out, n_in = plan.counts
    thru = [pltpu.HBM(a.shape, a.dtype) for a in list(srcs) + list(lands)]
    res = pl.pallas_call(
        body, name=name,
        out_shape=[pltpu.SemaphoreType.DMA((n_out,)), pltpu.SemaphoreType.DMA((n_in,))] + thru
        + [jax.ShapeDtypeStruct((8, 128), F32)],
        in_specs=[HBM] * (ns + nl) + [ANY],
        out_specs=[SEM, SEM] + [HBM] * (ns + nl) + [pl.BlockSpec(memory_space=pltpu.VMEM)],
        input_output_aliases={i: 2 + i for i in range(ns + nl)},
        compiler_params=pltpu.CompilerParams(has_side_effects=EFFECT),
    )(*[pltpu.with_memory_space_constraint(a, pltpu.HBM) for a in list(srcs) + list(lands)], after)
    return res[0], res[1], res[2:2 + ns], res[2 + ns:2 + ns + nl], res[-1]


def _split_copy_wait(name, plan, send, recv, srcs, lands, after):
    ns, nl = len(srcs), len(lands)
    after = list(after) if isinstance(after, (list, tuple)) else [after]

    def body(*refs):
        src_refs, land_refs = refs[:ns], refs[ns:ns + nl]
        send_ref, recv_ref = refs[ns + nl], refs[ns + nl + 1]
        outgoing, arrivals = plan(src_refs, land_refs)
        for src, dst, dev, si, ri in outgoing:
            pltpu.make_async_remote_copy(src_ref=src, dst_ref=dst, send_sem=send_ref.at[si], recv_sem=recv_ref.at[ri],
                                         device_id=dev, device_id_type=MESH).wait_send()
        for view, ri in arrivals:
            pltpu.make_async_remote_copy(src_ref=view, dst_ref=view, send_sem=send_ref.at[0], recv_sem=recv_ref.at[ri],
                                         device_id=_place()[:3], device_id_type=MESH).wait_recv()

    thru = [pltpu.HBM(a.shape, a.dtype) for a in list(srcs) + list(lands)]
    res = pl.pallas_call(
        body, name=name, out_shape=thru,
        in_specs=[HBM] * (ns + nl) + [SEM, SEM] + [ANY] * len(after), out_specs=[HBM] * (ns + nl),
        input_output_aliases={i: i for i in range(ns + nl)},
        compiler_params=pltpu.CompilerParams(has_side_effects=EFFECT),
    )(*srcs, *lands, send, recv, *after)
    return res[:ns], res[ns:]


def _gather_plan(n_arrays):
    def plan(src_refs, land_refs):
        x, y, c, chips = _place()
        me = 2 * x + y
        outgoing, arrivals = [], []
        for i in range(n_arrays):
            rh = src_refs[i].shape[0] // 2
            mine = pl.ds(c * rh, rh)
            for j, (px, py) in enumerate(chips):
                for delta in range(2):
                    tc = c ^ delta
                    outgoing.append((src_refs[i].at[mine], land_refs[i].at[me, mine], (px, py, tc),
                                     6 * i + 2 * j + delta, 6 * i + 2 * j + delta))
                    theirs = pl.ds(tc * rh, rh)
                    arrivals.append((land_refs[i].at[2 * px + py, theirs], 6 * i + 2 * j + delta))
        return outgoing, arrivals

    plan.counts = (6 * n_arrays, 6 * n_arrays)
    return plan


def _first_gather_plan():
    def plan(src_refs, land_refs):
        x, y, c, chips = _place()
        me = 2 * x + y
        rh = src_refs[0].shape[0] // 2
        mine = pl.ds(c * rh, rh)
        outgoing, arrivals = [], []
        for j, (px, py) in enumerate(chips):
            outgoing.append((src_refs[0].at[mine], land_refs[0].at[me, mine], (px, py, c), j, j))
            arrivals.append((land_refs[0].at[2 * px + py, mine], j))
            outgoing.append((src_refs[1], land_refs[1].at[me], (px, py, c), 3 + j, 3 + j))
            arrivals.append((land_refs[1].at[2 * px + py], 3 + j))
        return outgoing, arrivals

    plan.counts = (6, 6)
    return plan


def _exchange_plan(n_arrays):
    def plan(src_refs, land_refs):
        x, y, c, chips = _place()
        outgoing, arrivals = [], []
        for i in range(n_arrays):
            for j, (px, py) in enumerate(chips):
                outgoing.append((src_refs[i].at[2 * px + py], land_refs[i].at[j], (px, py, c), 3 * i + j, 3 * i + j))
                arrivals.append((land_refs[i].at[j], 3 * i + j))
        return outgoing, arrivals

    plan.counts = (3 * n_arrays, 3 * n_arrays)
    return plan


def _small_allreduce(vec):
    r, cdim = vec.shape
    n_dev = 8

    def body(v_ref, out_ref, buf, send, recv):
        x, y, c, _ = _place()
        me = 4 * x + 2 * y + c
        buf[me] = v_ref[...]
        cps = []
        for k in range(1, n_dev):
            dx, dy, dc = (k >> 2) & 1, (k >> 1) & 1, k & 1
            peer = (x ^ dx, y ^ dy, c ^ dc)
            cp = pltpu.make_async_remote_copy(
                src_ref=v_ref, dst_ref=buf.at[me], send_sem=send.at[k - 1], recv_sem=recv.at[k - 1],
                device_id=peer, device_id_type=MESH)
            cp.start()
            cps.append(cp)
        for k in range(1, n_dev):
            dx, dy, dc = (k >> 2) & 1, (k >> 1) & 1, k & 1
            src = 4 * (x ^ dx) + 2 * (y ^ dy) + (c ^ dc)
            slot = buf.at[src]
            pltpu.make_async_remote_copy(
                src_ref=slot, dst_ref=slot, send_sem=send.at[k - 1], recv_sem=recv.at[k - 1],
                device_id=(x ^ dx, y ^ dy, c ^ dc), device_id_type=MESH).wait_recv()
        for cp in cps:
            cp.wait_send()
        acc = buf[0]
        for k in range(1, n_dev):
            acc = acc + buf[k]
        out_ref[...] = acc

    vm = pl.BlockSpec(memory_space=pltpu.VMEM)
    return pl.pallas_call(
        body, name="small_allreduce", in_specs=[vm], out_specs=vm,
        out_shape=jax.ShapeDtypeStruct((r, cdim), F32),
        scratch_shapes=[pltpu.VMEM((n_dev, r, cdim), F32), pltpu.SemaphoreType.DMA((n_dev - 1,)),
                        pltpu.SemaphoreType.DMA((n_dev - 1,))],
    )(vec)


def _a_cols_to_head_major(w):
    lead = w.shape[:-1]
    q = w[..., :A_QK].reshape(lead + (A_HEADS, A_DK))
    k = w[..., A_QK:2 * A_QK].reshape(lead + (A_HEADS, A_DK))
    v = w[..., 2 * A_QK:2 * A_QK + A_VW].reshape(lead + (A_HEADS, A_DV))
    z = w[..., 2 * A_QK + A_VW:].reshape(lead + (A_HEADS, A_DV))
    return jnp.concatenate([q, k, v, z], axis=-1).reshape(lead + (A_HEADS * A_HEAD_COLS,))


def _a_cols_from_head_major(w):
    lead = w.shape[:-1]
    w = w.reshape(lead + (A_HEADS, A_HEAD_COLS))
    parts = [w[..., :A_DK], w[..., A_DK:2 * A_DK], w[..., 2 * A_DK:2 * A_DK + A_DV], w[..., 2 * A_DK + A_DV:]]
    return jnp.concatenate([p.reshape(lead + (-1,)) for p in parts], axis=-1)


def _conv_cols_to_head_major(w):
    lead = w.shape[:-1]
    q = w[..., :A_QK].reshape(lead + (A_HEADS, A_DK))
    k = w[..., A_QK:2 * A_QK].reshape(lead + (A_HEADS, A_DK))
    v = w[..., 2 * A_QK:].reshape(lead + (A_HEADS, A_DV))
    return jnp.concatenate([q, k, v], axis=-1).reshape(lead + (A_HEADS * A_CONV_COLS,))


def _conv_cols_from_head_major(w):
    lead = w.shape[:-1]
    w = w.reshape(lead + (A_HEADS, A_CONV_COLS))
    parts = [w[..., :A_DK], w[..., A_DK:2 * A_DK], w[..., 2 * A_DK:]]
    return jnp.concatenate([p.reshape(lead + (-1,)) for p in parts], axis=-1)


def _to_stream(a, bn, d):
    rest = a.shape[1:]
    s = a.shape[0] // bn
    a = a.reshape((bn, s // d, d) + rest)
    a = jnp.swapaxes(a, 1, 2)
    return a.reshape((bn * d, s // d) + rest)


def _from_stream(a, bn, d):
    rest = a.shape[2:]
    ln = a.shape[1]
    a = a.reshape((bn, d, ln) + rest)
    a = jnp.swapaxes(a, 1, 2)
    return a.reshape((bn * ln * d,) + rest)


B_SUB = 512
B_SHARD_BLOCKS = (3 * B_GROUPS * B_W + B_W) // N_CHIPS // B_SUB


def _b_block(gi, jj):
    nb = (B_GROUPS * (jj // 2) + gi) * 2 + jj % 2
    return nb // B_SHARD_BLOCKS, nb % B_SHARD_BLOCKS


def _shard_major(g, ncols):
    r = g.shape[0]
    return jnp.swapaxes(g.reshape(r, N_CHIPS, ncols), 0, 1)


def _pack_rows(items):
    rows, offs = [], []
    at = 0
    for a in items:
        flat = a.reshape(-1).astype(F32)
        nr = -(-flat.shape[0] // 1024) * 8
        flat = jnp.pad(flat, (0, nr * 128 - flat.shape[0]))
        rows.append(flat.reshape(nr, 128))
        offs.append((at, nr, a.shape))
        at += nr
    return jnp.concatenate(rows, axis=0), offs


def _unpack_rows(packed, offs):
    out = []
    for at, nr, shape in offs:
        size = int(np.prod(shape)) if len(shape) else 1
        out.append(packed[at:at + nr].reshape(-1)[:size].reshape(shape))
    return out


def _local_step(x, positions, loss_target, norm_g, a_log, a_dt_bias, a_norm_g, b_q_norm_g, b_k_norm_g,
                start_token, first_weights, late_weights, b_grads_ready, a_grads_ready):
    bn, s, d = x.shape
    t = bn * s
    n_chunks = s // A_CHUNK
    x0 = x.reshape(t, d)
    h0 = _rms_fwd(x0, norm_g[0:1] + start_token, "rms0_fwd")
    inv_freq = ROPE_THETA ** (-jnp.arange(0, ROPE_DIMS, 2, dtype=F32) / ROPE_DIMS)
    freq_row = jnp.concatenate([inv_freq, inv_freq, jnp.zeros((128 - ROPE_DIMS,), F32)]).reshape(1, 128)
    posf = jnp.broadcast_to(positions.astype(F32).reshape(t, 1), (t, 128)) + start_token
    tabs = _rope_tables(posf, freq_row)
    tabs_s = [tabs if dil == 1 else [_to_stream(tb, bn, dil).reshape(t, 128) for tb in tabs] for dil in B_DIL]
    wa_in, conv_w, late_token = first_weights([h0] + [tb for ts in tabs_s for tb in ts])
    wa_main = _a_cols_to_head_major(wa_in[:, :A_MAIN])
    wa_tail = jnp.pad(wa_in[:, A_MAIN:], ((0, 0), (0, 128 - 2 * A_HEADS))) + late_token.astype(BF16)
    cw_hm = _conv_cols_to_head_major(conv_w)

    proj_a = _matmul(h0, wa_main, "nn", F32, "a_in_main")
    tail_a = _matmul(h0, wa_tail, "nn", F32, "a_in_tail")
    tail_t = jnp.swapaxes(tail_a[:, :2 * A_HEADS].reshape(bn, s, 2 * A_HEADS), 1, 2)
    tail_t = tail_t.reshape(bn, 2 * A_HEADS, n_chunks, A_CHUNK)
    beta, gc = _gdn_prep(tail_t, a_log[0], a_dt_bias[0])
    proj_a3 = proj_a.reshape(bn, s, A_MAIN)
    og_a, oraw_a, states, t_mats, conv_y = _gdn_fwd(proj_a3, cw_hm, beta, gc, a_norm_g)
    wa_out, wb_in, wb_out = late_weights(og_a)
    b_cols = [4 * B_W] + [3 * B_W] * (B_GROUPS - 1)
    x1, h1 = _out_proj(og_a.reshape(t, A_VW), wa_out, x0, "a_out", norm_g=norm_g[1:2])

    h1_s, proj_b, qkv_b, o_b, lse_b = [], [], [], [], []
    for gi, dil in enumerate(B_DIL):
        hs = h1 if dil == 1 else _to_stream(h1, bn, dil).reshape(t, d)
        ts = tabs_s[gi]
        pj = _matmul(hs, wb_in, "nn", F32, f"b_in_g{gi}", tm=2048, tn=B_SUB, n=b_cols[gi], b_spec=pl.BlockSpec(
            (None, d, B_SUB), lambda i, j, kk, gi=gi: (_b_block(gi, j)[0], kk, _b_block(gi, j)[1])))
        qkv = _qk_prep(pj, *ts, b_q_norm_g[0, gi:gi + 1], b_k_norm_g[0, gi:gi + 1], f"qk_prep_g{gi}")
        o_s, lse_s = _attn_fwd(qkv.reshape(bn * dil, s // dil, 3 * B_W), f"attn_fwd_g{gi}")
        h1_s.append(hs), proj_b.append(pj), qkv_b.append(qkv)
        o_b.append(o_s.reshape(t, B_W) if dil == 1 else _from_stream(o_s, bn, dil))
        lse_b.append(lse_s.reshape(t, B_HEADS) if dil == 1 else _from_stream(lse_s, bn, dil))
    og_b = _merge_fwd(o_b, lse_b, proj_b[0])
    d_x2, loss_parts = _out_proj(og_b, wb_out, x1, "b_out_loss", target=loss_target.reshape(t, d))
    loss_local = jnp.sum(loss_parts)

    d_x2b = d_x2.astype(BF16)
    g_wb_out = _matmul(og_b, d_x2b, "tn", F32, "b_out_dw")
    d_og_b = _matmul(d_x2b, wb_out, "nt", F32, "b_out_dx")
    d_o, lse_joint, delta, d_z = _merge_bwd(o_b, lse_b, proj_b[0], d_og_b)
    d_h1, g_qn, g_kn = [], [], []
    g_wb_in = lax.empty(wb_in.shape, F32)
    for gi, dil in enumerate(B_DIL):
        if dil == 1:
            do_s, lj_s, dl_s = d_o, lse_joint, delta
        else:
            do_s, lj_s, dl_s = (_to_stream(a, bn, dil).reshape(t, -1) for a in (d_o, lse_joint, delta))
        ns, ln = bn * dil, s // dil
        dq, dk, dv = _attn_bwd(qkv_b[gi].reshape(ns, ln, 3 * B_W), do_s.reshape(ns, ln, B_W),
                               lj_s.reshape(ns, ln, B_HEADS), dl_s.reshape(ns, ln, B_HEADS), f"attn_bwd_g{gi}")
        d_pj, d_gain = _qk_prep_bwd(proj_b[gi], *tabs_s[gi], b_q_norm_g[0, gi:gi + 1], b_k_norm_g[0, gi:gi + 1],
                                    dq.reshape(t, B_W), dk.reshape(t, B_W), dv.reshape(t, B_W),
                                    d_z if gi == 0 else None, f"qk_prep_bwd_g{gi}")
        g_wb_in = _matmul(h1_s[gi], d_pj, "tn", F32, f"b_in_dw_g{gi}", tn=B_SUB, tk=2048, into=(g_wb_in, pl.BlockSpec(
            (None, d, B_SUB), lambda i, j, kk, gi=gi: (_b_block(gi, j)[0], i, _b_block(gi, j)[1]))))
        dh = _matmul(d_pj, wb_in, "nt", F32, f"b_in_dx_g{gi}", tm=2048, tk=B_SUB, n=d, b_spec=pl.BlockSpec(
            (None, d, B_SUB), lambda i, j, kk, gi=gi: (_b_block(gi, kk)[0], j, _b_block(gi, kk)[1])))
        d_h1.append(dh if dil == 1 else _from_stream(dh.reshape(ns, ln, d), bn, dil))
        g_qn.append(d_gain[0]), g_kn.append(d_gain[1])
    d_x1, g_norm1 = _rms_bwd(x1, norm_g[1:2], d_h1, d_x2, "rms1_bwd")

    d_x1b = d_x1.astype(BF16)
    g_wa_out = _matmul(og_a.reshape(t, A_VW), d_x1b, "tn", F32, "a_out_dw")
    b_token = b_grads_ready(g_wb_in, g_wb_out, g_wa_out)
    d_og_a = _matmul(d_x1b, wa_out, "nt", F32, "a_out_dx")
    d_pa, d_gc, d_beta, d_cw, d_ng = _gdn_bwd(proj_a3, cw_hm, beta, gc, a_norm_g + b_token, oraw_a, states,
                                              t_mats, conv_y, d_og_a.reshape(bn, s, A_VW))
    d_tail_t, d_alog, d_dtb = _gdn_prep_bwd(tail_t, a_log[0], a_dt_bias[0], d_gc, d_beta)
    d_tail = jnp.swapaxes(d_tail_t.reshape(bn, 2 * A_HEADS, s), 1, 2).reshape(t, 2 * A_HEADS)
    d_tail = jnp.pad(d_tail, ((0, 0), (0, 128 - 2 * A_HEADS))).astype(BF16)
    d_pa = d_pa.reshape(t, A_MAIN)
    g_wa_main = _matmul(h0, d_pa, "tn", F32, "a_in_dw_main")
    g_wa_tail = _matmul(h0, d_tail, "tn", F32, "a_in_dw_tail")
    g_wa_in = jnp.concatenate([_a_cols_from_head_major(g_wa_main), g_wa_tail[:, :2 * A_HEADS]], axis=1)
    a_token = a_grads_ready(g_wa_in)
    d_h0t = _matmul(d_tail + a_token.astype(BF16), wa_tail, "nt", F32, "a_in_dx_tail")
    d_x0, g_norm0 = _in_proj_bwd(d_pa, wa_main, d_h0t, x0, norm_g[0:1], d_x1, "a_in_dx_rms0_bwd")

    gfull = {
        "norm_g": jnp.concatenate([g_norm0, g_norm1], axis=0), "a_w_in": g_wa_in,
        "a_conv_w": _conv_cols_from_head_major(jnp.sum(d_cw, axis=0)),
        "a_log": jnp.sum(d_alog[:, :, 0], axis=0), "a_dt_bias": jnp.sum(d_dtb[:, :, 0], axis=0),
        "a_norm_g": jnp.sum(d_ng[:, :, 0, :], axis=(0, 1)), "a_w_out": g_wa_out, "b_w_in": g_wb_in,
        "b_q_norm_g": jnp.stack(g_qn), "b_k_norm_g": jnp.stack(g_kn), "b_w_out": g_wb_out}
    return loss_local, d_x0.reshape(bn, s, d), gfull


def kernel(x, positions, norm_g, a_w_in, a_conv_w, a_log, a_dt_bias, a_norm_g, a_w_out, b_w_in, b_q_norm_g, b_k_norm_g, b_w_out, loss_target, m_norm_g, m_a_w_in, m_a_conv_w, m_a_log, m_a_dt_bias, m_a_norm_g, m_a_w_out, m_b_w_in, m_b_q_norm_g, m_b_k_norm_g, m_b_w_out, v_norm_g, v_a_w_in, v_a_conv_w, v_a_log, v_a_dt_bias, v_a_norm_g, v_a_w_out, v_b_w_in, v_b_q_norm_g, v_b_k_norm_g, v_b_w_out):
    d = x.shape[2]
    my_c = lax.axis_index("c")
    my_chip = 2 * lax.axis_index("x") + lax.axis_index("y")

    half_index = jnp.reshape(my_c, (1,)).astype(jnp.int32)
    chip_index = jnp.reshape(my_chip, (1,)).astype(jnp.int32)
    def landing(shard):
        return lax.dynamic_update_slice(lax.empty((N_CHIPS,) + shard.shape, shard.dtype), shard[None],
                                        (my_chip,) + (0,) * shard.ndim)

    first_shards = [a_w_in[0].astype(BF16), a_conv_w[0]]
    first_plan = _first_gather_plan()
    first = _split_copy_start("first_weights_start", first_plan, first_shards,
                              [landing(s) for s in first_shards], half_index)
    pending = {}
    late_shards = [(w[0] + first[4][0, 0]).astype(BF16) for w in (a_w_out, b_w_in, b_w_out)]
    late_lands = [landing(s) for s in late_shards]

    def first_weights(after):
        _, (ga_in, g_conv) = _split_copy_wait("first_weights_wait", first_plan, *first[:4],
                                              list(after) + late_lands)
        ga_in = _sibling_forward(ga_in)
        wa_in = jnp.concatenate([ga_in[k] for k in range(N_CHIPS)], axis=1)
        conv_w = jnp.concatenate([g_conv[k] for k in range(N_CHIPS)], axis=1)
        plan = _gather_plan(len(late_shards))
        pending["late"] = (plan,) + tuple(_split_copy_start(
            "late_weights_start", plan, late_shards, late_lands, conv_w))
        return wa_in, conv_w, pending["late"][5][0, 0]

    def late_weights(after):
        plan, send, recv, srcs, lands, _ = pending["late"]
        _, (ga_out, gb_in, gb_out) = _split_copy_wait("late_weights_wait", plan, send, recv, srcs, lands, after)
        return ga_out.reshape(A_VW, d), gb_in, gb_out.reshape(B_W, d)

    def reduce_to_chip_sums(mats, tag):
        recv_sib = _sibling_swap_halves([g.astype(BF16) for g in mats], f"grad_{tag}_sibling_swap")
        return [_pair_sum(g, r, half_index, f"grad_{tag}_pair_sum_{i}") for i, (g, r) in enumerate(zip(mats, recv_sib))]

    def start_exchange(tag, mats):
        sums = reduce_to_chip_sums(mats, tag)
        lands = [lax.empty((N_CHIPS - 1,) + s.shape[1:], BF16) for s in sums]
        plan = _exchange_plan(len(mats))
        pending[tag] = (plan,) + tuple(_split_copy_start(f"grad_{tag}_exchange_start", plan, sums, lands, chip_index))
        return pending[tag][5][0, 0]

    def finish_exchange(tag, after):
        plan, send, recv, srcs, lands, _ = pending[tag]
        return _split_copy_wait(f"grad_{tag}_exchange_wait", plan, send, recv, srcs, lands, after)

    def b_grads_ready(g_wb_in, g_wb_out, g_wa_out):
        return start_exchange("b", [g_wb_in, g_wb_out.reshape(N_CHIPS, -1, d), g_wa_out.reshape(N_CHIPS, -1, d)])

    def a_grads_ready(g_wa_in):
        return start_exchange("a", [_shard_major(g_wa_in, a_w_in.shape[2])])

    loss_local, d_x0, gfull = _local_step(x, positions, loss_target, norm_g, a_log, a_dt_bias, a_norm_g,
                                          b_q_norm_g, b_k_norm_g, first[4][0, 0], first_weights, late_weights,
                                          b_grads_ready, a_grads_ready)

    small = [gfull["norm_g"], gfull["a_conv_w"], gfull["a_log"], gfull["a_dt_bias"], gfull["a_norm_g"],
             gfull["b_q_norm_g"], gfull["b_k_norm_g"], loss_local]
    packed, offs = _pack_rows(small)
    reduced = _small_allreduce(packed)
    g_norm, g_conv_all, g_alog, g_dtb, g_ang, g_q, g_k, loss = _unpack_rows(reduced, offs)
    g_conv_mine = lax.dynamic_slice_in_dim(g_conv_all, my_chip * a_conv_w.shape[2], a_conv_w.shape[2], axis=1)

    b_sums, b_received = finish_exchange("b", d_x0)
    a_sums, a_received = finish_exchange("a", reduced)
    chip_sums = [a_sums[0], b_sums[2], b_sums[0], b_sums[1]]
    received = [a_received[0], b_received[2], b_received[0], b_received[1]]
    halves = [_chip_sum(s, r, chip_index, f"grad_chip_sum_{i}") for i, (s, r) in enumerate(zip(chip_sums, received))]
    theirs = _sibling_swap_whole(halves)
    big = ("a_w_in", "a_w_out", "b_w_in", "b_w_out")
    big_halves = dict(zip(big, zip(halves, theirs)))

    grads = {
        "norm_g": g_norm, "a_conv_w": g_conv_mine[None], "a_log": g_alog[None], "a_dt_bias": g_dtb[None],
        "a_norm_g": g_ang[None], "b_q_norm_g": g_q[None], "b_k_norm_g": g_k[None]}
    weights = {"norm_g": norm_g, "a_w_in": a_w_in, "a_conv_w": a_conv_w, "a_log": a_log, "a_dt_bias": a_dt_bias,
               "a_norm_g": a_norm_g, "a_w_out": a_w_out, "b_w_in": b_w_in, "b_q_norm_g": b_q_norm_g,
               "b_k_norm_g": b_k_norm_g, "b_w_out": b_w_out}
    m_in = {"norm_g": m_norm_g, "a_w_in": m_a_w_in, "a_conv_w": m_a_conv_w, "a_log": m_a_log,
            "a_dt_bias": m_a_dt_bias, "a_norm_g": m_a_norm_g, "a_w_out": m_a_w_out, "b_w_in": m_b_w_in,
            "b_q_norm_g": m_b_q_norm_g, "b_k_norm_g": m_b_k_norm_g, "b_w_out": m_b_w_out}
    v_in = {"norm_g": v_norm_g, "a_w_in": v_a_w_in, "a_conv_w": v_a_conv_w, "a_log": v_a_log,
            "a_dt_bias": v_a_dt_bias, "a_norm_g": v_a_norm_g, "a_w_out": v_a_w_out, "b_w_in": v_b_w_in,
            "b_q_norm_g": v_b_q_norm_g, "b_k_norm_g": v_b_k_norm_g, "b_w_out": v_b_w_out}
    names = list(weights)

    delta_w, new_m, new_v = {}, {}, {}
    for nm in big:
        mine, other = big_halves[nm]
        if weights[nm].shape[2] % 128:
            cols = lambda a: jnp.transpose(a, (2, 0, 1))
            half_cols = lambda a: jnp.transpose(a)[:, None, :]
            outs = _adamw_shard_cols(cols(weights[nm]), half_cols(mine), half_cols(other), cols(m_in[nm]),
                                     cols(v_in[nm]), half_index, f"adamw_{nm}")
            outs = [jnp.transpose(o, (1, 2, 0)) for o in outs]
        else:
            outs = _adamw_shard(weights[nm], mine, other, m_in[nm], v_in[nm], half_index, f"adamw_{nm}")
        grads[nm], delta_w[nm], new_m[nm], new_v[nm] = outs
    small_names = [nm for nm in names if nm not in big]
    packs = [_pack_rows([src[nm] for nm in small_names]) for src in (weights, grads, m_in, v_in)]
    offs = packs[0][1]
    dl, m2, v2 = _adamw(packs[0][0], packs[1][0], packs[2][0], packs[3][0], "adamw_small")
    for nm, a, b, c2 in zip(small_names, _unpack_rows(dl, offs), _unpack_rows(m2, offs), _unpack_rows(v2, offs)):
        delta_w[nm], new_m[nm], new_v[nm] = a, b, c2

    return (loss, d_x0, *[grads[nm] for nm in names], *[delta_w[nm] for nm in names],
            *[new_m[nm] for nm in names], *[new_v[nm] for nm in names])
```

```python
import jax
import jax.numpy as jnp
import numpy as np
from jax import lax
from jax.experimental import pallas as pl
from jax.experimental.pallas import tpu as pltpu

F32 = jnp.float32
BF16 = jnp.bfloat16
MESH = pl.DeviceIdType.MESH

EPS = 1e-6
A_HEADS = 8
A_DK = 128
A_DV = 256
A_QK = A_HEADS * A_DK
A_VW = A_HEADS * A_DV
A_MAIN = 2 * A_QK + 2 * A_VW
A_HEAD_COLS = 2 * A_DK + 2 * A_DV
A_CONV_COLS = 2 * A_DK + A_DV
A_CHUNK = 64
A_CONV = 4
B_GROUPS = 3
B_HEADS = 8
B_DH = 128
B_W = B_HEADS * B_DH
B_DIL = (1, 4, 16)
B_BLK = 128
ROPE_THETA = 500000.0
ROPE_DIMS = B_DH // 4
ADAM_LR, ADAM_B1, ADAM_B2, ADAM_EPS, ADAM_WD, ADAM_STEP = 0.001, 0.9, 0.999, 1e-08, 0.01, 10
N_CHIPS = 4
VMEM_BIG = 56 * 1024 * 1024


def _params(sem=None, vmem=None):
    return pltpu.CompilerParams(dimension_semantics=sem, vmem_limit_bytes=vmem)


def _dot(a, b, ca, cb):
    return lax.dot_general(a.astype(BF16), b.astype(BF16), (((ca,), (cb,)), ((), ())),
                           preferred_element_type=F32)


def _split3(a):
    hi = a.astype(BF16)
    r = a - hi.astype(F32)
    mid = r.astype(BF16)
    lo = (r - mid.astype(F32)).astype(BF16)
    return hi, mid, lo


def _sigmoid(y):
    return 1.0 / (1.0 + jnp.exp(-y))


def _silu(y):
    return y * _sigmoid(y)


def _silu_and_slope(y):
    s = _sigmoid(y)
    return y * s, s * (1.0 + y * (1.0 - s))


def _matmul(a, b, mode, out_dtype, name, res=None, tm=1024, tn=1024, tk=1024, n=None, b_spec=None, into=None):
    m, k = a.shape[::-1] if mode == "tn" else a.shape
    if n is None:
        n = b.shape[0] if mode == "nt" else b.shape[1]
    tm, tn, tk = min(tm, m), min(tn, n), min(tk, k)
    assert m % tm == 0 and n % tn == 0 and k % tk == 0, (name, a.shape, b.shape)
    nk = k // tk
    dims = {"nn": ((1,), (0,)), "nt": ((1,), (1,)), "tn": ((0,), (0,))}[mode]

    def body(*refs):
        a_ref, b_ref = refs[0], refs[1]
        r_ref = refs[2] if res is not None else None
        o_ref = refs[2 + (res is not None) + (into is not None)]
        prod = lax.dot_general(a_ref[...], b_ref[...], (dims, ((), ())), preferred_element_type=F32)

        def finish(r):
            if res is not None:
                r = r + r_ref[...]
            o_ref[...] = r.astype(out_dtype)

        if nk == 1:
            finish(prod)
            return
        acc = refs[-1]
        kk = pl.program_id(2)

        @pl.when(kk == 0)
        def _():
            acc[...] = prod

        @pl.when((kk > 0) & (kk < nk - 1))
        def _():
            acc[...] += prod

        @pl.when(kk == nk - 1)
        def _():
            finish(acc[...] + prod)

    a_spec = pl.BlockSpec((tm, tk), lambda i, j, kk: (i, kk))
    if mode == "tn":
        a_spec = pl.BlockSpec((tk, tm), lambda i, j, kk: (kk, i))
    if b_spec is None and mode == "nt":
        b_spec = pl.BlockSpec((tn, tk), lambda i, j, kk: (j, kk))
    elif b_spec is None:
        b_spec = pl.BlockSpec((tk, tn), lambda i, j, kk: (kk, j))
    in_specs = [a_spec, b_spec]
    args = [a, b]
    if res is not None:
        in_specs.append(pl.BlockSpec((tm, tn), lambda i, j, kk: (i, j)))
        args.append(res)
    out_spec = pl.BlockSpec((tm, tn), lambda i, j, kk: (i, j))
    out_shape = jax.ShapeDtypeStruct((m, n), out_dtype)
    aliases = {}
    if into is not None:
        assert res is None
        buf, out_spec = into
        out_shape = jax.ShapeDtypeStruct(buf.shape, buf.dtype)
        in_specs.append(ANY)
        args.append(buf)
        aliases = {2: 0}
    return pl.pallas_call(
        body, name=name, grid=(m // tm, n // tn, nk),
        in_specs=in_specs, out_specs=out_spec, out_shape=out_shape, input_output_aliases=aliases,
        scratch_shapes=[pltpu.VMEM((tm, tn), F32)] if nk > 1 else [],
        compiler_params=_params(("parallel", "parallel", "arbitrary"), 48 * 1024 * 1024),
    )(*args)


def _rms_fwd(x, g, name, tm=256):
    t, d = x.shape

    def body(x_ref, g_ref, h_ref):
        xv = x_ref[...]
        r = lax.rsqrt(jnp.mean(xv * xv, axis=-1, keepdims=True) + EPS)
        h_ref[...] = (xv * r * g_ref[...]).astype(BF16)

    return pl.pallas_call(
        body, name=name, grid=(t // tm,),
        in_specs=[pl.BlockSpec((tm, d), lambda i: (i, 0)), pl.BlockSpec((1, d), lambda i: (0, 0))],
        out_specs=pl.BlockSpec((tm, d), lambda i: (i, 0)),
        out_shape=jax.ShapeDtypeStruct((t, d), BF16),
        compiler_params=_params(("parallel",)),
    )(x, g)


def _rms_bwd(x, g, dhs, dres, name, tm=256):
    t, d = x.shape
    n_dh = len(dhs)

    def body(*refs):
        x_ref, g_ref = refs[0], refs[1]
        dh_refs = refs[2:2 + n_dh]
        dres_ref, dx_ref, dg_ref = refs[2 + n_dh:]
        i = pl.program_id(0)

        @pl.when(i == 0)
        def _():
            dg_ref[...] = jnp.zeros_like(dg_ref)

        xv = x_ref[...]
        r = lax.rsqrt(jnp.mean(xv * xv, axis=-1, keepdims=True) + EPS)
        xh = xv * r
        dh = dh_refs[0][...].astype(F32)
        for ref in dh_refs[1:]:
            dh = dh + ref[...].astype(F32)
        dg_ref[0:1, :] += jnp.sum(dh * xh, axis=0, keepdims=True)
        dxh = dh * g_ref[...]
        dx = r * (dxh - xh * jnp.mean(dxh * xh, axis=-1, keepdims=True))
        dx_ref[...] = dx + dres_ref[...]

    row = pl.BlockSpec((tm, d), lambda i: (i, 0))
    dx, dg = pl.pallas_call(
        body, name=name, grid=(t // tm,),
        in_specs=[row, pl.BlockSpec((1, d), lambda i: (0, 0))] + [row] * n_dh + [row],
        out_specs=[row, pl.BlockSpec((8, d), lambda i: (0, 0))],
        out_shape=[jax.ShapeDtypeStruct((t, d), F32), jax.ShapeDtypeStruct((8, d), F32)],
        compiler_params=_params(("arbitrary",)),
    )(x, g, *dhs, dres)
    return dx, dg[0:1]


def _in_proj_bwd(dp, w, dh_more, x, g, dres, name, tm=512, tk=1024):
    t, k = dp.shape
    d = w.shape[0]
    nk = k // tk

    def body(dp_ref, w_ref, more_ref, x_ref, g_ref, dres_ref, dx_ref, dg_ref, acc):
        i, kk = pl.program_id(0), pl.program_id(1)

        @pl.when((i == 0) & (kk == 0))
        def _():
            dg_ref[...] = jnp.zeros_like(dg_ref)

        prod = lax.dot_general(dp_ref[...], w_ref[...], (((1,), (1,)), ((), ())), preferred_element_type=F32)

        @pl.when(kk == 0)
        def _():
            acc[...] = prod

        @pl.when((kk > 0) & (kk < nk - 1))
        def _():
            acc[...] += prod

        @pl.when(kk == nk - 1)
        def _():
            dh = acc[...] + prod + more_ref[...]
            xv = x_ref[...]
            r = lax.rsqrt(jnp.mean(xv * xv, axis=-1, keepdims=True) + EPS)
            xh = xv * r
            dg_ref[0:1, :] += jnp.sum(dh * xh, axis=0, keepdims=True)
            dxh = dh * g_ref[...]
            dx_ref[...] = r * (dxh - xh * jnp.mean(dxh * xh, axis=-1, keepdims=True)) + dres_ref[...]

    row = pl.BlockSpec((tm, d), lambda i, kk: (i, 0))
    dx, dg = pl.pallas_call(
        body, name=name, grid=(t // tm, nk),
        in_specs=[pl.BlockSpec((tm, tk), lambda i, kk: (i, kk)), pl.BlockSpec((d, tk), lambda i, kk: (0, kk)),
                  row, row, pl.BlockSpec((1, d), lambda i, kk: (0, 0)), row],
        out_specs=[row, pl.BlockSpec((8, d), lambda i, kk: (0, 0))],
        out_shape=[jax.ShapeDtypeStruct((t, d), F32), jax.ShapeDtypeStruct((8, d), F32)],
        scratch_shapes=[pltpu.VMEM((tm, d), F32)],
        compiler_params=_params(("arbitrary", "arbitrary"), 48 * 1024 * 1024),
    )(dp, w, dh_more, x, g, dres)
    return dx, dg[0:1]


def _out_proj(a, w, res, name, norm_g=None, target=None, tm=512):
    t, k = a.shape
    d = w.shape[1]
    nb = t // tm

    def body(a_ref, w_ref, r_ref, x_ref, o1_ref, o2_ref):
        y = jnp.dot(a_ref[...], w_ref[...], preferred_element_type=F32) + r_ref[...]
        if norm_g is not None:
            o1_ref[...] = y
            r = lax.rsqrt(jnp.mean(y * y, axis=-1, keepdims=True) + EPS)
            o2_ref[...] = (y * r * x_ref[...]).astype(BF16)
        else:
            e = y - x_ref[...]
            o1_ref[...] = e * (1.0 / d)
            s = jnp.sum(jnp.sum(e * e, axis=1, keepdims=True), axis=0, keepdims=True) * (0.5 / d)
            o2_ref[...] = jnp.broadcast_to(s, (8, 128))

    row = pl.BlockSpec((tm, d), lambda i: (i, 0))
    if norm_g is not None:
        extra, extra_spec = norm_g, pl.BlockSpec((1, d), lambda i: (0, 0))
        out2_spec, out2_shape = row, jax.ShapeDtypeStruct((t, d), BF16)
    else:
        extra, extra_spec = target, row
        out2_spec = pl.BlockSpec((None, 8, 128), lambda i: (i, 0, 0))
        out2_shape = jax.ShapeDtypeStruct((nb, 8, 128), F32)
    o1, o2 = pl.pallas_call(
        body, name=name, grid=(nb,),
        in_specs=[pl.BlockSpec((tm, k), lambda i: (i, 0)), pl.BlockSpec((k, d), lambda i: (0, 0)), row, extra_spec],
        out_specs=[row, out2_spec], out_shape=[jax.ShapeDtypeStruct((t, d), F32), out2_shape],
        compiler_params=_params(("parallel",), 48 * 1024 * 1024),
    )(a, w, res, extra)
    return (o1, o2) if norm_g is not None else (o1, o2[:, 0, 0])


def _softplus(x):
    t = jnp.exp(-jnp.abs(x))
    return jnp.maximum(x, 0.0) + jnp.where(t < 1e-3, t * (1.0 - 0.5 * t), jnp.log(1.0 + t))


def _tri(rows_le_cols):
    r = lax.broadcasted_iota(jnp.int32, (A_CHUNK, A_CHUNK), 0)
    c = lax.broadcasted_iota(jnp.int32, (A_CHUNK, A_CHUNK), 1)
    return jnp.where((r <= c) if rows_le_cols else (r >= c), 1.0, 0.0).astype(BF16)


def _dot_exact_rhs(a, ones_bf16):
    dn = (((1,), (0,)), ((), ()))
    hi, mid, lo = _split3(a)
    out = lax.dot_general(hi, ones_bf16, dn, preferred_element_type=F32)
    out = out + lax.dot_general(mid, ones_bf16, dn, preferred_element_type=F32)
    return out + lax.dot_general(lo, ones_bf16, dn, preferred_element_type=F32)


def _gdn_prep(tail_t, a_log, dt_bias):
    bn, _, n, c = tail_t.shape

    def body(t_ref, alog_ref, dtb_ref, beta_ref, gc_ref):
        upper = _tri(True)
        for h in range(A_HEADS):
            beta_ref[h] = _sigmoid(t_ref[h])
            ea = jnp.exp(jnp.full((n, c), alog_ref[h], F32))
            g = -ea * _softplus(t_ref[A_HEADS + h] + dtb_ref[h])
            gc_ref[h] = _dot_exact_rhs(g, upper)

    smem = pl.BlockSpec(memory_space=pltpu.SMEM)
    blk = pl.BlockSpec((None, A_HEADS, n, c), lambda b: (b, 0, 0, 0))
    return pl.pallas_call(
        body, name="gdn_prep", grid=(bn,),
        in_specs=[pl.BlockSpec((None, 2 * A_HEADS, n, c), lambda b: (b, 0, 0, 0)), smem, smem],
        out_specs=[blk, blk],
        out_shape=[jax.ShapeDtypeStruct((bn, A_HEADS, n, c), F32)] * 2,
        compiler_params=_params(("parallel",)),
    )(tail_t, a_log, dt_bias)


def _gdn_prep_bwd(tail_t, a_log, dt_bias, d_gc, d_beta):
    bn, _, n, c = tail_t.shape

    def body(t_ref, alog_ref, dtb_ref, dgc_ref, dbeta_ref, dt_ref, dal_ref, ddt_ref):
        lower = _tri(False)
        for h in range(A_HEADS):
            beta = _sigmoid(t_ref[h])
            dt_ref[h] = dbeta_ref[h] * beta * (1.0 - beta)
            dg = _dot_exact_rhs(dgc_ref[h], lower)
            ea = jnp.exp(jnp.full((n, c), alog_ref[h], F32))
            xa = t_ref[A_HEADS + h] + dtb_ref[h]
            g = -ea * _softplus(xa)
            dxa = -ea * dg * _sigmoid(xa)
            dt_ref[A_HEADS + h] = dxa
            s1 = jnp.sum(jnp.sum(g * dg, axis=1, keepdims=True), axis=0, keepdims=True)
            s2 = jnp.sum(jnp.sum(dxa, axis=1, keepdims=True), axis=0, keepdims=True)
            dal_ref[h:h + 1, :] = jnp.broadcast_to(s1, (1, 128))
            ddt_ref[h:h + 1, :] = jnp.broadcast_to(s2, (1, 128))

    smem = pl.BlockSpec(memory_space=pltpu.SMEM)
    blk8 = pl.BlockSpec((None, A_HEADS, n, c), lambda b: (b, 0, 0, 0))
    blk16 = pl.BlockSpec((None, 2 * A_HEADS, n, c), lambda b: (b, 0, 0, 0))
    sm = pl.BlockSpec((None, A_HEADS, 128), lambda b: (b, 0, 0))
    return pl.pallas_call(
        body, name="gdn_prep_bwd", grid=(bn,),
        in_specs=[blk16, smem, smem, blk8, blk8],
        out_specs=[blk16, sm, sm],
        out_shape=[jax.ShapeDtypeStruct((bn, 2 * A_HEADS, n, c), F32),
                   jax.ShapeDtypeStruct((bn, A_HEADS, 128), F32),
                   jax.ShapeDtypeStruct((bn, A_HEADS, 128), F32)],
        compiler_params=_params(("parallel",)),
    )(tail_t, a_log, dt_bias, d_gc, d_beta)


HALO = 8


def _conv_taps(xw, w):
    y = w[A_CONV - 1:A_CONV, :] * xw
    for j in range(1, A_CONV):
        y = y + w[A_CONV - 1 - j:A_CONV - j, :] * pltpu.roll(xw, j, 0)
    return y[HALO:, :]


def _row_to_col(row, eye):
    c = eye.shape[0]
    return jnp.sum(jnp.where(eye, jnp.broadcast_to(row, (c, c)), 0.0), axis=1, keepdims=True)


def _col_to_row(col, eye):
    c = eye.shape[0]
    return jnp.sum(jnp.where(eye, jnp.broadcast_to(col, (c, c)), 0.0), axis=0, keepdims=True)


def _unit_lower_inverse(a, ri, ci):
    eye = jnp.where(ri == ci, 1.0, 0.0)
    a8 = jnp.where((ri >> 3) == (ci >> 3), a, 0.0)
    a2 = _dot(a8, a8, 1, 0)
    yield
    a4 = _dot(a2, a2, 1, 0)
    t = eye - a8
    t = t + _dot(t, a2, 1, 0)
    yield
    t = t + _dot(t, a4, 1, 0)
    yield
    for sh in (3, 4, 5):
        off = jnp.where(((ri >> (sh + 1)) == (ci >> (sh + 1))) & ((ri >> sh) != (ci >> sh)), a, 0.0)
        left = _dot(t, off, 1, 0)
        yield
        t = t - _dot(left, t, 1, 0)
        yield
    return t


def _round_robin(gens):
    live = list(gens)
    while live:
        nxt = []
        for g in live:
            try:
                next(g)
                nxt.append(g)
            except StopIteration:
                pass
        live = nxt


def _gdn_chunk_core(q, k, v, g_row, b_row, t_mat, ri, ci):
    eye = ri == ci
    g_col = _row_to_col(g_row, eye)
    b_col = _row_to_col(b_row, eye)
    causal = ri >= ci
    strict = ri > ci
    dec = jnp.where(causal, jnp.exp(jnp.where(causal, g_col - g_row, 0.0)), 0.0)
    gam = jnp.exp(g_col)
    g_last = g_row[:, A_CHUNK - 1:A_CHUNK]
    gam_last = jnp.exp(g_last)
    e = jnp.exp(g_last - g_col)
    kb = k * b_col
    bv = v * b_col
    kbg = kb * gam
    q16, k16, kb16 = q.astype(BF16), k.astype(BF16), kb.astype(BF16)
    kk = _dot(kb16, k16, 1, 1)
    p = _dot(q16, k16, 1, 1) * dec
    yield
    a_mat = jnp.where(strict, kk * dec, 0.0)
    if t_mat is None:
        t_mat = yield from _unit_lower_inverse(a_mat, ri, ci)
    t16 = t_mat.astype(BF16)
    u = _dot(t16, bv, 1, 0)
    w = _dot(t16, kbg, 1, 0)
    yield
    return dict(eye=eye, g_col=g_col, b_col=b_col, dec=dec, strict=strict, causal=causal, gam=gam,
                gam_last=gam_last, e=e, kb=kb, bv=bv, kbg=kbg, a_mat=a_mat, t_mat=t_mat, u=u, w=w, p=p,
                qg=q * gam, kd=k * e, q16=q16, k16=k16, kb16=kb16, t16=t16)


A_SEQ_BLK = 256
A_BLK_CHUNKS = A_SEQ_BLK // A_CHUNK


def _gdn_halo(proj_hm):
    bn, s, w = proj_hm.shape
    last = proj_hm.reshape(bn, s // A_SEQ_BLK, A_SEQ_BLK, w)[:, :, A_SEQ_BLK - HALO:, :]
    return jnp.concatenate([jnp.zeros((bn, 1, HALO, w), proj_hm.dtype), last[:, :-1]], axis=1)


def _gdn_window(x_ref, halo_ref, ci, first, lo):
    if first:
        return jnp.concatenate([halo_ref[:, lo:lo + A_CONV_COLS], x_ref[0:A_CHUNK, lo:lo + A_CONV_COLS]], axis=0)
    start = pl.multiple_of(ci * A_CHUNK - HALO, HALO)
    return x_ref[pl.ds(start, A_CHUNK + HALO), lo:lo + A_CONV_COLS]


def _gdn_chunk_prep(xw, cw, y=None):
    if y is None:
        y = _conv_taps(xw, cw)
    a, slope = _silu_and_slope(y)
    aq, ak, v = a[:, 0:A_DK], a[:, A_DK:2 * A_DK], a[:, 2 * A_DK:]
    rq = lax.rsqrt(jnp.sum(aq * aq, axis=1, keepdims=True) + EPS)
    rk = lax.rsqrt(jnp.sum(ak * ak, axis=1, keepdims=True) + EPS)
    return dict(xw=xw, y=y, slope=slope, aq=aq, ak=ak, rq=rq, rk=rk, q=aq * rq * (A_DK ** -0.5), k=ak * rk, v=v)


def _gdn_fwd(proj_hm, cw_hm, beta, gc, norm_g, hp=8):
    bn, s, _ = proj_hm.shape
    n = s // A_CHUNK
    nsb = s // A_SEQ_BLK
    halo = _gdn_halo(proj_hm)

    def body(x_ref, halo_ref, cw_ref, beta_ref, gc_ref, ng_ref, og_ref, oraw_ref, st_ref, t_ref, y_ref, state):
        first_chunk = pl.program_id(2) * A_BLK_CHUNKS
        ri = lax.broadcasted_iota(jnp.int32, (A_CHUNK, A_CHUNK), 0)
        ci_ = lax.broadcasted_iota(jnp.int32, (A_CHUNK, A_CHUNK), 1)
        ng = ng_ref[...]

        @pl.when(pl.program_id(2) == 0)
        def _():
            state[...] = jnp.zeros_like(state)

        def one_head(hh, ci, first, rows):
            lo = hh * A_HEAD_COLS
            cw = cw_ref[:, hh * A_CONV_COLS:(hh + 1) * A_CONV_COLS]
            cin = _gdn_chunk_prep(_gdn_window(x_ref, halo_ref, ci, first, lo), cw)
            y_ref[rows, hh * A_CONV_COLS:(hh + 1) * A_CONV_COLS] = cin["y"]
            seq_chunk = pl.ds(first_chunk + ci, 1)
            core = yield from _gdn_chunk_core(cin["q"], cin["k"], cin["v"], gc_ref[hh, seq_chunk, :],
                                              beta_ref[hh, seq_chunk, :], None, ri, ci_)
            st = state[hh]
            st_ref[hh, ci] = st
            t_ref[hh, ci] = core["t_mat"]
            st16 = st.astype(BF16)
            vn = core["u"] - _dot(core["w"], st16, 1, 0)
            qs = _dot(core["qg"], st16, 1, 0)
            yield
            vn16 = vn.astype(BF16)
            o = qs + _dot(core["p"], vn16, 1, 0)
            state[hh] = st * core["gam_last"] + _dot(core["kd"], vn16, 0, 0)
            yield
            ocols = slice(hh * A_DV, (hh + 1) * A_DV)
            oraw_ref[rows, ocols] = o
            r = lax.rsqrt(jnp.mean(o * o, axis=1, keepdims=True) + EPS)
            z = x_ref[rows, lo + A_CONV_COLS:lo + A_HEAD_COLS]
            og_ref[rows, ocols] = (o * r * ng * _silu(z)).astype(BF16)

        def chunk(ci, first):
            rows = pl.ds(0 if first else pl.multiple_of(ci * A_CHUNK, A_CHUNK), A_CHUNK)
            _round_robin([one_head(hh, ci, first, rows) for hh in range(hp)])

        chunk(0, True)
        lax.fori_loop(1, A_BLK_CHUNKS, lambda i, c: (chunk(i, False), c)[1], 0)

    small = pl.BlockSpec((None, hp, n, A_CHUNK), lambda b, h, j: (b, h, 0, 0))
    return pl.pallas_call(
        body, name="gdn_fwd", grid=(bn, A_HEADS // hp, nsb),
        in_specs=[pl.BlockSpec((None, A_SEQ_BLK, hp * A_HEAD_COLS), lambda b, h, j: (b, j, h)),
                  pl.BlockSpec((None, None, HALO, hp * A_HEAD_COLS), lambda b, h, j: (b, j, 0, h)),
                  pl.BlockSpec((A_CONV, hp * A_CONV_COLS), lambda b, h, j: (0, h)),
                  small, small,
                  pl.BlockSpec((1, A_DV), lambda b, h, j: (0, 0))],
        out_specs=[pl.BlockSpec((None, A_SEQ_BLK, hp * A_DV), lambda b, h, j: (b, j, h)),
                   pl.BlockSpec((None, A_SEQ_BLK, hp * A_DV), lambda b, h, j: (b, j, h)),
                   pl.BlockSpec((None, hp, A_BLK_CHUNKS, A_DK, A_DV), lambda b, h, j: (b, h, j, 0, 0)),
                   pl.BlockSpec((None, hp, A_BLK_CHUNKS, A_CHUNK, A_CHUNK), lambda b, h, j: (b, h, j, 0, 0)),
                   pl.BlockSpec((None, A_SEQ_BLK, hp * A_CONV_COLS), lambda b, h, j: (b, j, h))],
        out_shape=[jax.ShapeDtypeStruct((bn, s, A_VW), BF16),
                   jax.ShapeDtypeStruct((bn, s, A_VW), F32),
                   jax.ShapeDtypeStruct((bn, A_HEADS, n, A_DK, A_DV), F32),
                   jax.ShapeDtypeStruct((bn, A_HEADS, n, A_CHUNK, A_CHUNK), F32),
                   jax.ShapeDtypeStruct((bn, s, A_HEADS * A_CONV_COLS), F32)],
        scratch_shapes=[pltpu.VMEM((hp, A_DK, A_DV), F32)],
        compiler_params=_params(("parallel", "parallel", "arbitrary"), VMEM_BIG),
    )(proj_hm, halo, cw_hm, beta, gc, norm_g)


def _gdn_bwd(proj_hm, cw_hm, beta, gc, norm_g, oraw, states, t_mats, conv_y, dog, hp=4):
    bn, s, _ = proj_hm.shape
    n = s // A_CHUNK
    nsb = s // A_SEQ_BLK
    halo = _gdn_halo(proj_hm)

    def body(x_ref, halo_ref, cw_ref, beta_ref, gc_ref, ng_ref, oraw_ref, st_ref, t_ref, y_ref, dog_ref,
             dx_ref, dgc_ref, dbeta_ref, dcw_ref, dng_ref, dstate, dy_next, shifted):
        first_chunk = (nsb - 1 - pl.program_id(2)) * A_BLK_CHUNKS
        ri = lax.broadcasted_iota(jnp.int32, (A_CHUNK, A_CHUNK), 0)
        ci_ = lax.broadcasted_iota(jnp.int32, (A_CHUNK, A_CHUNK), 1)
        lane = lax.broadcasted_iota(jnp.int32, (1, A_CHUNK), 1)
        ng = ng_ref[...]

        @pl.when(pl.program_id(2) == 0)
        def _():
            dstate[...] = jnp.zeros_like(dstate)
            dy_next[...] = jnp.zeros_like(dy_next)
            dcw_ref[...] = jnp.zeros_like(dcw_ref)
            dng_ref[...] = jnp.zeros_like(dng_ref)

        def one_head(hh, ci, first, rows):
            lo = hh * A_HEAD_COLS
            ccols = slice(hh * A_CONV_COLS, (hh + 1) * A_CONV_COLS)
            ocols = slice(hh * A_DV, (hh + 1) * A_DV)
            cw = cw_ref[:, ccols]
            cin = _gdn_chunk_prep(_gdn_window(x_ref, halo_ref, ci, first, lo), cw, y_ref[rows, ccols])
            q, k, v = cin["q"], cin["k"], cin["v"]
            seq_chunk = pl.ds(first_chunk + ci, 1)
            cr = yield from _gdn_chunk_core(q, k, v, gc_ref[hh, seq_chunk, :], beta_ref[hh, seq_chunk, :],
                                            t_ref[hh, ci], ri, ci_)
            eye, dec, gam, e = cr["eye"], cr["dec"], cr["gam"], cr["e"]
            b_col, t_mat, u, w, p = cr["b_col"], cr["t_mat"], cr["u"], cr["w"], cr["p"]
            st = st_ref[hh, ci]
            ds_out = dstate[hh]

            o = oraw_ref[rows, ocols]
            z = x_ref[rows, lo + A_CONV_COLS:lo + A_HEAD_COLS]
            d_og = dog_ref[rows, ocols]
            r = lax.rsqrt(jnp.mean(o * o, axis=1, keepdims=True) + EPS)
            oh = o * r
            gate, gate_slope = _silu_and_slope(z)
            d_on = d_og * gate
            dz = d_og * oh * ng * gate_slope
            dng_ref[hh, 0:1, :] += jnp.sum(d_on * oh, axis=0, keepdims=True)
            d_oh = d_on * ng
            d_o = r * (d_oh - oh * jnp.mean(d_oh * oh, axis=1, keepdims=True))

            st16, ds16, do16, w16 = st.astype(BF16), ds_out.astype(BF16), d_o.astype(BF16), w.astype(BF16)
            q16, k16, t16 = cr["q16"], cr["k16"], cr["t16"]
            vn = u - _dot(w16, st16, 1, 0)
            d_vn = _dot(p, do16, 0, 0) + _dot(cr["kd"], ds16, 1, 0)
            d_qg = _dot(do16, st16, 1, 1)
            qgdo = _dot(cr["qg"], do16, 0, 0)
            yield
            vn16, dvn16 = vn.astype(BF16), d_vn.astype(BF16)
            d_p = jnp.where(cr["causal"], _dot(do16, vn16, 1, 1), 0.0)
            d_kd = _dot(vn16, ds16, 1, 1)
            d_gam_last = jnp.sum(jnp.sum(st * ds_out, axis=1, keepdims=True), axis=0, keepdims=True)
            d_w = -_dot(dvn16, st16, 1, 1)
            dstate[hh] = qgdo + ds_out * cr["gam_last"] - _dot(w16, dvn16, 0, 0)
            d_bv = _dot(t16, dvn16, 0, 0)
            yield
            d_kbg = _dot(t16, d_w, 0, 0)
            n_p = (d_p * dec).astype(BF16)
            d_q = _dot(n_p, k16, 1, 0) + d_qg * gam
            npq = _dot(n_p, q16, 0, 0)
            yield
            d_a = jnp.where(cr["strict"], -(_dot(d_bv, u, 1, 1) + _dot(d_kbg, w16, 1, 1)), 0.0)
            yield
            m_a = (d_a * dec).astype(BF16)
            d_kb = _dot(m_a, k16, 1, 0) + d_kbg * gam
            d_k = (_dot(m_a, cr["kb16"], 0, 0) + npq + d_kd * e + d_kb * b_col)
            yield
            d_v = d_bv * b_col
            d_beta_col = (jnp.sum(d_bv * v, axis=1, keepdims=True)
                          + jnp.sum(d_kb * k, axis=1, keepdims=True))
            gterm = d_a * cr["a_mat"] + d_p * p
            d_e = jnp.sum(d_kd * k, axis=1, keepdims=True) * e
            d_g_col = (jnp.sum(gterm, axis=1, keepdims=True)
                       + (jnp.sum(d_qg * q, axis=1, keepdims=True)
                          + jnp.sum(d_kbg * cr["kb"], axis=1, keepdims=True)) * gam
                       - d_e)
            d_g_last = jnp.sum(d_e, axis=0, keepdims=True) + d_gam_last * cr["gam_last"]
            d_g_row = (_col_to_row(d_g_col, eye) - jnp.sum(gterm, axis=0, keepdims=True)
                       + jnp.where(lane == A_CHUNK - 1, d_g_last, 0.0))
            dgc_ref[hh, seq_chunk, :] = d_g_row
            dbeta_ref[hh, seq_chunk, :] = _col_to_row(d_beta_col, eye)

            qh = cin["aq"] * cin["rq"]
            kh = cin["ak"] * cin["rk"]
            d_qh = d_q * (A_DK ** -0.5)
            d_aq = cin["rq"] * (d_qh - qh * jnp.sum(d_qh * qh, axis=1, keepdims=True))
            d_ak = cin["rk"] * (d_k - kh * jnp.sum(d_k * kh, axis=1, keepdims=True))
            d_y = jnp.concatenate([d_aq, d_ak, d_v], axis=1) * cin["slope"]
            shifted[hh, 0, 0:A_CHUNK, :] = d_y
            shifted[hh, 0, A_CHUNK:A_CHUNK + HALO, :] = dy_next[hh]
            shifted[hh, 1, 0:A_CHUNK + HALO, :] = cin["xw"]
            d_x = cw[A_CONV - 1:A_CONV, :] * d_y
            for j in range(1, A_CONV):
                d_x = d_x + cw[A_CONV - 1 - j:A_CONV - j, :] * shifted[hh, 0, j:j + A_CHUNK, :]
            for j in range(A_CONV):
                xs = shifted[hh, 1, HALO - j:HALO - j + A_CHUNK, :]
                dcw_ref[A_CONV - 1 - j:A_CONV - j, ccols] += jnp.sum(d_y * xs, axis=0, keepdims=True)
            dy_next[hh] = d_y[0:HALO, :]
            dx_ref[rows, lo:lo + A_CONV_COLS] = d_x.astype(BF16)
            dx_ref[rows, lo + A_CONV_COLS:lo + A_HEAD_COLS] = dz.astype(BF16)

        def chunk(ci, first):
            rows = pl.ds(0 if first else pl.multiple_of(ci * A_CHUNK, A_CHUNK), A_CHUNK)
            _round_robin([one_head(hh, ci, first, rows) for hh in range(hp)])

        lax.fori_loop(0, A_BLK_CHUNKS - 1, lambda i, c: (chunk(A_BLK_CHUNKS - 1 - i, False), c)[1], 0)
        chunk(0, True)

    rev = lambda j: nsb - 1 - j
    small = pl.BlockSpec((None, hp, n, A_CHUNK), lambda b, h, j: (b, h, 0, 0))
    wide = pl.BlockSpec((None, A_SEQ_BLK, hp * A_HEAD_COLS), lambda b, h, j: (b, rev(j), h))
    val = pl.BlockSpec((None, A_SEQ_BLK, hp * A_DV), lambda b, h, j: (b, rev(j), h))
    return pl.pallas_call(
        body, name="gdn_bwd", grid=(bn, A_HEADS // hp, nsb),
        in_specs=[wide,
                  pl.BlockSpec((None, None, HALO, hp * A_HEAD_COLS), lambda b, h, j: (b, rev(j), 0, h)),
                  pl.BlockSpec((A_CONV, hp * A_CONV_COLS), lambda b, h, j: (0, h)),
                  small, small,
                  pl.BlockSpec((1, A_DV), lambda b, h, j: (0, 0)),
                  val,
                  pl.BlockSpec((None, hp, A_BLK_CHUNKS, A_DK, A_DV), lambda b, h, j: (b, h, rev(j), 0, 0)),
                  pl.BlockSpec((None, hp, A_BLK_CHUNKS, A_CHUNK, A_CHUNK), lambda b, h, j: (b, h, rev(j), 0, 0)),
                  pl.BlockSpec((None, A_SEQ_BLK, hp * A_CONV_COLS), lambda b, h, j: (b, rev(j), h)),
                  val],
        out_specs=[wide, small, small,
                   pl.BlockSpec((None, A_CONV, hp * A_CONV_COLS), lambda b, h, j: (b, 0, h)),
                   pl.BlockSpec((None, hp, 8, A_DV), lambda b, h, j: (b, h, 0, 0))],
        out_shape=[jax.ShapeDtypeStruct((bn, s, A_HEADS * A_HEAD_COLS), BF16),
                   jax.ShapeDtypeStruct((bn, A_HEADS, n, A_CHUNK), F32),
                   jax.ShapeDtypeStruct((bn, A_HEADS, n, A_CHUNK), F32),
                   jax.ShapeDtypeStruct((bn, A_CONV, A_HEADS * A_CONV_COLS), F32),
                   jax.ShapeDtypeStruct((bn, A_HEADS, 8, A_DV), F32)],
        scratch_shapes=[pltpu.VMEM((hp, A_DK, A_DV), F32), pltpu.VMEM((hp, HALO, A_CONV_COLS), F32),
                        pltpu.VMEM((hp, 2, A_CHUNK + 2 * HALO, A_CONV_COLS), F32)],
        compiler_params=_params(("parallel", "parallel", "arbitrary"), VMEM_BIG),
    )(proj_hm, halo, cw_hm, beta, gc, norm_g, oraw, states, t_mats, conv_y, dog)


def _rope_tables(posf, inv_freq_row):
    t = posf.shape[0]
    tm = 512

    def body(p_ref, f_ref, c_ref, sa_ref, sb_ref):
        ang = p_ref[...] * f_ref[...]
        lane = lax.broadcasted_iota(jnp.int32, ang.shape, 1)
        half = ROPE_DIMS // 2
        c_ref[...] = jnp.where(lane < ROPE_DIMS, jnp.cos(ang), 1.0)
        sn = jnp.sin(ang)
        sa_ref[...] = jnp.where(lane < half, -sn, 0.0)
        sb_ref[...] = jnp.where((lane >= half) & (lane < ROPE_DIMS), sn, 0.0)

    row = pl.BlockSpec((tm, 128), lambda i: (i, 0))
    return pl.pallas_call(
        body, name="rope_tables", grid=(t // tm,),
        in_specs=[row, pl.BlockSpec((1, 128), lambda i: (0, 0))], out_specs=[row] * 3,
        out_shape=[jax.ShapeDtypeStruct((t, 128), F32)] * 3,
        compiler_params=_params(("parallel",)),
    )(posf, inv_freq_row)


def _qk_prep(proj, c, sa, sb, qg, kg, name, tm=256):
    t = proj.shape[0]

    def body(x_ref, c_ref, sa_ref, sb_ref, qg_ref, kg_ref, o_ref):
        cc, s1, s2 = c_ref[...], sa_ref[...], sb_ref[...]
        half = ROPE_DIMS // 2

        def one_head(lo, g):
            xv = x_ref[:, lo:lo + B_DH]
            ms = jnp.mean(xv * xv, axis=1, keepdims=True)
            yield
            xn = xv * lax.rsqrt(ms + EPS) * g
            r1, r2 = pltpu.roll(xn, 128 - half, 1), pltpu.roll(xn, half, 1)
            yield
            o_ref[:, lo:lo + B_DH] = (xn * cc + r1 * s1 + r2 * s2).astype(BF16)

        for which, g_ref in ((0, qg_ref), (1, kg_ref)):
            g = g_ref[...]
            _round_robin([one_head(which * B_W + h * B_DH, g) for h in range(B_HEADS)])
        o_ref[:, 2 * B_W:3 * B_W] = x_ref[:, 2 * B_W:3 * B_W].astype(BF16)

    tab = pl.BlockSpec((tm, 128), lambda i: (i, 0))
    gain = pl.BlockSpec((1, B_DH), lambda i: (0, 0))
    return pl.pallas_call(
        body, name=name, grid=(t // tm,),
        in_specs=[pl.BlockSpec((tm, 3 * B_W), lambda i: (i, 0)), tab, tab, tab, gain, gain],
        out_specs=pl.BlockSpec((tm, 3 * B_W), lambda i: (i, 0)),
        out_shape=jax.ShapeDtypeStruct((t, 3 * B_W), BF16),
        compiler_params=_params(("parallel",), 40 * 1024 * 1024),
    )(proj, c, sa, sb, qg, kg)


def _qk_prep_bwd(proj, c, sa, sb, qg, kg, dq, dk, dv, dz, name, tm=256):
    t = proj.shape[0]
    out_w = 3 * B_W + (B_W if dz is not None else 0)

    def body(*refs):
        x_ref, c_ref, sa_ref, sb_ref, qg_ref, kg_ref, dq_ref, dk_ref, dv_ref = refs[:9]
        if dz is not None:
            dz_ref, o_ref, dgain_ref = refs[9:]
        else:
            o_ref, dgain_ref = refs[9:]
        i = pl.program_id(0)

        @pl.when(i == 0)
        def _():
            dgain_ref[...] = jnp.zeros_like(dgain_ref)

        cc, s1, s2 = c_ref[...], sa_ref[...], sb_ref[...]
        half = ROPE_DIMS // 2

        def one_head(which, h, g, d_ref, parts):
            lo = which * B_W + h * B_DH
            xv = x_ref[:, lo:lo + B_DH]
            d_out = d_ref[:, h * B_DH:(h + 1) * B_DH].astype(F32)
            ms = jnp.mean(xv * xv, axis=1, keepdims=True)
            r1, r2 = pltpu.roll(d_out * s1, half, 1), pltpu.roll(d_out * s2, 128 - half, 1)
            yield
            r = lax.rsqrt(ms + EPS)
            xh = xv * r
            d_xn = d_out * cc + r1 + r2
            parts.append(jnp.sum(d_xn * xh, axis=0, keepdims=True))
            d_xh = d_xn * g
            dot = jnp.mean(d_xh * xh, axis=1, keepdims=True)
            yield
            o_ref[:, lo:lo + B_DH] = (r * (d_xh - xh * dot)).astype(BF16)

        for which, g_ref, d_ref in ((0, qg_ref, dq_ref), (1, kg_ref, dk_ref)):
            parts = []
            _round_robin([one_head(which, h, g_ref[...], d_ref, parts) for h in range(B_HEADS)])
            acc = parts[0]
            for part in parts[1:]:
                acc = acc + part
            dgain_ref[which:which + 1, :] += acc
        o_ref[:, 2 * B_W:3 * B_W] = dv_ref[...]
        if dz is not None:
            o_ref[:, 3 * B_W:4 * B_W] = dz_ref[...]

    tab = pl.BlockSpec((tm, 128), lambda i: (i, 0))
    gain = pl.BlockSpec((1, B_DH), lambda i: (0, 0))
    grad = pl.BlockSpec((tm, B_W), lambda i: (i, 0))
    in_specs = [pl.BlockSpec((tm, 2 * B_W), lambda i: (i, 0)), tab, tab, tab, gain, gain, grad, grad, grad]
    args = [proj, c, sa, sb, qg, kg, dq, dk, dv]
    if dz is not None:
        in_specs.append(grad)
        args.append(dz)
    return pl.pallas_call(
        body, name=name, grid=(t // tm,), in_specs=in_specs,
        out_specs=[pl.BlockSpec((tm, out_w), lambda i: (i, 0)), pl.BlockSpec((8, B_DH), lambda i: (0, 0))],
        out_shape=[jax.ShapeDtypeStruct((t, out_w), BF16), jax.ShapeDtypeStruct((8, B_DH), F32)],
        compiler_params=_params(("arbitrary",), 40 * 1024 * 1024),
    )(*args)


def _attn_masks():
    qi = lax.broadcasted_iota(jnp.int32, (B_BLK, 2 * B_BLK), 0)
    kj = lax.broadcasted_iota(jnp.int32, (B_BLK, 2 * B_BLK), 1)
    two = (kj >= qi) & (kj <= qi + B_BLK)
    q1 = lax.broadcasted_iota(jnp.int32, (B_BLK, B_BLK), 0)
    k1 = lax.broadcasted_iota(jnp.int32, (B_BLK, B_BLK), 1)
    return k1 <= q1, two


def _lane_pick(ref_rows, h):
    lane = lax.broadcasted_iota(jnp.int32, ref_rows.shape, 1)
    return jnp.sum(jnp.where(lane == h, ref_rows, 0.0), axis=1, keepdims=True)


B_ROWS = 2048


def _attn_schedule(nb, sb, block):
    way = 16

    def run(items):
        for at in range(0, len(items), way):
            _round_robin([block(*it) for it in items[at:at + way]])

    run([(si, 0, True) for si in range(sb)])
    if nb == 1:
        return
    per = max(1, way // sb)
    lead = 1 + (nb - 1) % per
    if lead > 1:
        run([(si, i, False) for i in range(1, lead) for si in range(sb)])

    def step(it, carry):
        run([(si, lead + it * per + u, False) for u in range(per) for si in range(sb)])
        return carry

    lax.fori_loop(0, (nb - lead) // per, step, 0)


def _attn_rows(i, first):
    if first:
        return pl.ds(0, B_BLK), pl.ds(0, B_BLK)
    rows = pl.ds(pl.multiple_of(i * B_BLK, B_BLK), B_BLK)
    return rows, pl.ds(pl.multiple_of((i - 1) * B_BLK, B_BLK), 2 * B_BLK)


def _attn_fwd(qkv, name):
    ns, ln, _ = qkv.shape
    nb = ln // B_BLK
    sb = B_ROWS // ln
    scale = B_DH ** -0.5

    def body(q_ref, k_ref, v_ref, o_ref, lse_ref):
        h = pl.program_id(1)
        mask1, mask2 = _attn_masks()
        lane = lax.broadcasted_iota(jnp.int32, (B_BLK, B_HEADS), 1)

        @pl.when(h == 0)
        def _():
            lse_ref[...] = jnp.zeros_like(lse_ref)

        def block(si, i, first):
            rows, win = _attn_rows(i, first)
            mask = mask1 if first else mask2
            sc = jnp.where(mask, _dot(q_ref[si, rows, :], k_ref[si, win, :], 1, 1) * scale, -1e30)
            yield
            m = jnp.max(sc, axis=1, keepdims=True)
            p = jnp.exp(sc - m)
            l = jnp.sum(p, axis=1, keepdims=True)
            pv = _dot(p, v_ref[si, win, :], 1, 0)
            yield
            o_ref[si, rows, :] = (pv / l).astype(BF16)
            lse_ref[si, rows, :] = jnp.where(lane == h, m + jnp.log(l), lse_ref[si, rows, :])

        _attn_schedule(nb, sb, block)

    head = lambda off: pl.BlockSpec((sb, ln, B_DH), lambda s, h: (s, 0, off + h))
    return pl.pallas_call(
        body, name=name, grid=(ns // sb, B_HEADS),
        in_specs=[head(0), head(B_HEADS), head(2 * B_HEADS)],
        out_specs=[head(0), pl.BlockSpec((sb, ln, B_HEADS), lambda s, h: (s, 0, 0))],
        out_shape=[jax.ShapeDtypeStruct((ns, ln, B_W), BF16), jax.ShapeDtypeStruct((ns, ln, B_HEADS), F32)],
        compiler_params=_params(("parallel", "arbitrary")),
    )(qkv, qkv, qkv)


def _attn_bwd(qkv, d_o, lse_joint, delta, name):
    ns, ln, _ = qkv.shape
    nb = ln // B_BLK
    sb = B_ROWS // ln
    scale = B_DH ** -0.5

    def body(q_ref, k_ref, v_ref, do_ref, lj_ref, dl_ref, dq_ref, dk_out, dv_out, dk_ref, dv_ref):
        h = pl.program_id(1)
        mask1, mask2 = _attn_masks()
        dk_ref[...] = jnp.zeros_like(dk_ref)
        dv_ref[...] = jnp.zeros_like(dv_ref)

        def block(si, i, first):
            rows, win = _attn_rows(i, first)
            mask = mask1 if first else mask2
            q = q_ref[si, rows, :]
            d_out = do_ref[si, rows, :]
            l_col = _lane_pick(lj_ref[si, rows, :], h)
            d_col = _lane_pick(dl_ref[si, rows, :], h)
            sc = _dot(q, k_ref[si, win, :], 1, 1) * scale
            d_p = _dot(d_out, v_ref[si, win, :], 1, 1)
            yield
            p = jnp.exp(jnp.where(mask, sc - l_col, -1e30))
            d_s = p * (d_p - d_col) * scale
            d_q = _dot(d_s, k_ref[si, win, :], 1, 0)
            d_k = _dot(d_s, q, 0, 0)
            d_v = _dot(p, d_out, 0, 0)
            yield
            dq_ref[si, rows, :] = d_q.astype(BF16)
            dk_ref[si, win, :] += d_k
            dv_ref[si, win, :] += d_v

        _attn_schedule(nb, sb, block)
        dk_out[...] = dk_ref[...].astype(BF16)
        dv_out[...] = dv_ref[...].astype(BF16)

    head = lambda off: pl.BlockSpec((sb, ln, B_DH), lambda s, h: (s, 0, off + h))
    small = pl.BlockSpec((sb, ln, B_HEADS), lambda s, h: (s, 0, 0))
    return pl.pallas_call(
        body, name=name, grid=(ns // sb, B_HEADS),
        in_specs=[head(0), head(B_HEADS), head(2 * B_HEADS), head(0), small, small],
        out_specs=[head(0)] * 3,
        out_shape=[jax.ShapeDtypeStruct((ns, ln, B_W), BF16)] * 3,
        scratch_shapes=[pltpu.VMEM((sb, ln, B_DH), F32)] * 2,
        compiler_params=_params(("parallel", "parallel")),
    )(qkv, qkv, qkv, d_o, lse_joint, delta)


def _merge_weights(lse_refs):
    ls = [r[...] for r in lse_refs]
    m = jnp.maximum(jnp.maximum(ls[0], ls[1]), ls[2])
    es = [jnp.exp(l - m) for l in ls]
    tot = es[0] + es[1] + es[2]
    return [e / tot for e in es], m + jnp.log(tot)


def _merge_fwd(outs, lses, proj0, tm=256):
    t = outs[0].shape[0]

    def body(o0, o1, o2, l0, l1, l2, z_ref, og_ref):
        wts, _ = _merge_weights((l0, l1, l2))

        def one_head(h):
            cols = slice(h * B_DH, (h + 1) * B_DH)
            w0, w1, w2 = (jnp.broadcast_to(w[:, h:h + 1], (tm, B_DH)) for w in wts)
            yield
            o = w0 * o0[:, cols] + w1 * o1[:, cols] + w2 * o2[:, cols]
            og_ref[:, cols] = (o * _silu(z_ref[:, cols])).astype(BF16)

        _round_robin([one_head(h) for h in range(B_HEADS)])

    wide = pl.BlockSpec((tm, B_W), lambda i: (i, 0))
    small = pl.BlockSpec((tm, B_HEADS), lambda i: (i, 0))
    return pl.pallas_call(
        body, name="merge_fwd", grid=(t // tm,),
        in_specs=[wide] * 3 + [small] * 3 + [pl.BlockSpec((tm, B_W), lambda i: (i, 3))],
        out_specs=wide, out_shape=jax.ShapeDtypeStruct((t, B_W), BF16),
        compiler_params=_params(("parallel",)),
    )(*outs, *lses, proj0)


def _merge_bwd(outs, lses, proj0, d_og, tm=256):
    t = outs[0].shape[0]

    def body(o0, o1, o2, l0, l1, l2, z_ref, dog_ref, do_ref, lj_ref, dl_ref, dz_ref):
        wts, lj = _merge_weights((l0, l1, l2))
        lj_ref[...] = lj
        lane = lax.broadcasted_iota(jnp.int32, (tm, B_HEADS), 1)
        sums = [None] * B_HEADS

        def one_head(h):
            cols = slice(h * B_DH, (h + 1) * B_DH)
            w0, w1, w2 = (jnp.broadcast_to(w[:, h:h + 1], (tm, B_DH)) for w in wts)
            yield
            o = w0 * o0[:, cols] + w1 * o1[:, cols] + w2 * o2[:, cols]
            z = z_ref[:, cols]
            d_g = dog_ref[:, cols]
            gate, gate_slope = _silu_and_slope(z)
            d_out = d_g * gate
            dz_ref[:, cols] = (d_g * o * gate_slope).astype(BF16)
            do_ref[:, cols] = d_out.astype(BF16)
            sums[h] = jnp.sum(d_out * o, axis=1, keepdims=True)
            yield

        _round_robin([one_head(h) for h in range(B_HEADS)])
        delta = jnp.zeros((tm, B_HEADS), F32)
        for h in range(B_HEADS):
            delta = jnp.where(lane == h, sums[h], delta)
        dl_ref[...] = delta

    wide = pl.BlockSpec((tm, B_W), lambda i: (i, 0))
    small = pl.BlockSpec((tm, B_HEADS), lambda i: (i, 0))
    return pl.pallas_call(
        body, name="merge_bwd", grid=(t // tm,),
        in_specs=[wide] * 3 + [small] * 3 + [pl.BlockSpec((tm, B_W), lambda i: (i, 3)), wide],
        out_specs=[wide, small, small, wide],
        out_shape=[jax.ShapeDtypeStruct((t, B_W), BF16), jax.ShapeDtypeStruct((t, B_HEADS), F32),
                   jax.ShapeDtypeStruct((t, B_HEADS), F32), jax.ShapeDtypeStruct((t, B_W), BF16)],
        compiler_params=_params(("parallel",)),
    )(*outs, *lses, proj0, d_og)


def _adamw(w, g, m, v, name):
    r, c = w.shape
    tr = r
    for cand in (256, 128, 64, 32, 16, 8):
        if r % cand == 0:
            tr = cand
            break

    def body(w_ref, g_ref, m_ref, v_ref, d_ref, nm_ref, nv_ref):
        gv = g_ref[...]
        nm = ADAM_B1 * m_ref[...] + (1.0 - ADAM_B1) * gv
        nv = ADAM_B2 * v_ref[...] + (1.0 - ADAM_B2) * (gv * gv)
        m_hat = nm / (1.0 - ADAM_B1 ** ADAM_STEP)
        v_hat = nv / (1.0 - ADAM_B2 ** ADAM_STEP)
        d_ref[...] = -ADAM_LR * (m_hat / (jnp.sqrt(v_hat) + ADAM_EPS) + ADAM_WD * w_ref[...])
        nm_ref[...] = nm
        nv_ref[...] = nv

    blk = pl.BlockSpec((tr, c), lambda i: (i, 0))
    return pl.pallas_call(
        body, name=name, grid=(r // tr,), in_specs=[blk] * 4, out_specs=[blk] * 3,
        out_shape=[jax.ShapeDtypeStruct((r, c), F32)] * 3,
        compiler_params=_params(("parallel",)),
    )(w, g, m, v)


def _adam_update(w, gv, m, v):
    nm = ADAM_B1 * m + (1.0 - ADAM_B1) * gv
    nv = ADAM_B2 * v + (1.0 - ADAM_B2) * (gv * gv)
    m_hat = nm / (1.0 - ADAM_B1 ** ADAM_STEP)
    v_hat = nv / (1.0 - ADAM_B2 ** ADAM_STEP)
    return -ADAM_LR * (m_hat / (jnp.sqrt(v_hat) + ADAM_EPS) + ADAM_WD * w), nm, nv


def _adamw_shard(w, mine, theirs, m, v, half_index, name, tr=128):
    _, r, c = w.shape
    nhb = (r // 2) // tr

    def body(c_ref, w_ref, mine_ref, theirs_ref, m_ref, v_ref, g_ref, d_ref, nm_ref, nv_ref):
        is_mine = (pl.program_id(0) // nhb) == c_ref[0]
        gv = jnp.where(is_mine, mine_ref[...], theirs_ref[...])
        d, nm, nv = _adam_update(w_ref[...], gv, m_ref[...], v_ref[...])
        g_ref[...] = gv
        d_ref[...] = d
        nm_ref[...] = nm
        nv_ref[...] = nv

    full = pl.BlockSpec((None, tr, c), lambda i, cc: (0, i, 0))
    half = pl.BlockSpec((tr, c), lambda i, cc: (i % nhb, 0))
    return pl.pallas_call(
        body, name=name,
        grid_spec=pltpu.PrefetchScalarGridSpec(
            num_scalar_prefetch=1, grid=(2 * nhb,),
            in_specs=[full, half, half, full, full], out_specs=[full] * 4),
        out_shape=[jax.ShapeDtypeStruct(w.shape, F32)] * 4,
        compiler_params=_params(("parallel",), 40 * 1024 * 1024),
    )(half_index, w, mine, theirs, m, v)


def _adamw_shard_cols(w, mine, theirs, m, v, half_index, name, steps=20):
    c, _, r = w.shape
    tc = c // steps
    assert tc * steps == c

    def body(c_ref, w_ref, mine_ref, theirs_ref, m_ref, v_ref, g_ref, d_ref, nm_ref, nv_ref):
        first = jnp.where(c_ref[0] == 0, mine_ref[...], theirs_ref[...])
        second = jnp.where(c_ref[0] == 0, theirs_ref[...], mine_ref[...])
        for lo, gv in ((0, first), (r // 2, second)):
            cols = slice(lo, lo + r // 2)
            d, nm, nv = _adam_update(w_ref[:, :, cols], gv, m_ref[:, :, cols], v_ref[:, :, cols])
            g_ref[:, :, cols] = gv
            d_ref[:, :, cols] = d
            nm_ref[:, :, cols] = nm
            nv_ref[:, :, cols] = nv

    full = pl.BlockSpec((tc, 1, r), lambda i, cc: (i, 0, 0))
    half = pl.BlockSpec((tc, 1, r // 2), lambda i, cc: (i, 0, 0))
    return pl.pallas_call(
        body, name=name,
        grid_spec=pltpu.PrefetchScalarGridSpec(
            num_scalar_prefetch=1, grid=(steps,),
            in_specs=[full, half, half, full, full], out_specs=[full] * 4),
        out_shape=[jax.ShapeDtypeStruct(w.shape, F32)] * 4,
        compiler_params=_params(("parallel",), 40 * 1024 * 1024),
    )(half_index, w, mine, theirs, m, v)


def _pair_sum(own, other, half_index, name, tr=256):
    _, r, c = own.shape
    rh = r // 2
    tr = min(tr, rh)
    nrb = rh // tr

    def body(c_ref, own_ref, oth_ref, out_ref):
        out_ref[...] = (own_ref[...] + oth_ref[...].astype(F32)).astype(BF16)

    return pl.pallas_call(
        body, name=name,
        grid_spec=pltpu.PrefetchScalarGridSpec(
            num_scalar_prefetch=1, grid=(N_CHIPS, nrb),
            in_specs=[pl.BlockSpec((None, tr, c), lambda k, i, cc: (k, cc[0] * nrb + i, 0)),
                      pl.BlockSpec((None, tr, c), lambda k, i, cc: (k, i, 0))],
            out_specs=pl.BlockSpec((None, tr, c), lambda k, i, cc: (k, i, 0))),
        out_shape=jax.ShapeDtypeStruct((N_CHIPS, rh, c), BF16),
        compiler_params=_params(("parallel", "parallel")),
    )(half_index, own, other)


def _chip_sum(sums, others, chip_index, name, tr=256):
    _, r, c = sums.shape
    tr = min(tr, r)

    def body(k_ref, own_ref, oth_ref, out_ref):
        acc = own_ref[...].astype(F32)
        for j in range(N_CHIPS - 1):
            acc = acc + oth_ref[j].astype(F32)
        out_ref[...] = acc

    return pl.pallas_call(
        body, name=name,
        grid_spec=pltpu.PrefetchScalarGridSpec(
            num_scalar_prefetch=1, grid=(r // tr,),
            in_specs=[pl.BlockSpec((None, tr, c), lambda i, kk: (kk[0], i, 0)),
                      pl.BlockSpec((N_CHIPS - 1, tr, c), lambda i, kk: (0, i, 0))],
            out_specs=pl.BlockSpec((tr, c), lambda i, kk: (i, 0))),
        out_shape=jax.ShapeDtypeStruct((r, c), F32),
        compiler_params=_params(("parallel",)),
    )(chip_index, sums, others)


HBM = pl.BlockSpec(memory_space=pltpu.HBM)


def _place():
    x, y, c = lax.axis_index("x"), lax.axis_index("y"), lax.axis_index("c")
    chips = [(1 - x, y), (x, 1 - y), (1 - x, 1 - y)]
    return x, y, c, chips


def _sibling_forward(land):
    def body(in_ref, out_ref, send, recv):
        x, y, c, chips = _place()
        rh = out_ref.shape[1] // 2
        cps = []
        for j, (px, py) in enumerate(chips):
            slot = out_ref.at[2 * px + py, pl.ds(c * rh, rh)]
            cp = pltpu.make_async_remote_copy(
                src_ref=slot, dst_ref=slot, send_sem=send.at[j], recv_sem=recv.at[j],
                device_id=(x, y, 1 - c), device_id_type=MESH)
            cp.start()
            cps.append(cp)
        for j, (px, py) in enumerate(chips):
            slot = out_ref.at[2 * px + py, pl.ds((1 - c) * rh, rh)]
            pltpu.make_async_remote_copy(
                src_ref=slot, dst_ref=slot, send_sem=send.at[j], recv_sem=recv.at[j],
                device_id=(x, y, 1 - c), device_id_type=MESH).wait_recv()
        for cp in cps:
            cp.wait_send()

    return pl.pallas_call(
        body, name="first_weights_sibling_forward", in_specs=[HBM], out_specs=HBM,
        out_shape=jax.ShapeDtypeStruct(land.shape, land.dtype), input_output_aliases={0: 0},
        scratch_shapes=[pltpu.SemaphoreType.DMA((3,)), pltpu.SemaphoreType.DMA((3,))],
    )(land)


def _sibling_swap_halves(grads, name):
    na = len(grads)

    def body(*refs):
        ins, outs = refs[:na], refs[na:2 * na]
        send, recv = refs[2 * na:]
        x, y, c, _ = _place()
        sib = (x, y, 1 - c)
        cps = []
        for i in range(na):
            rh = ins[i].shape[1] // 2
            cp = pltpu.make_async_remote_copy(
                src_ref=ins[i].at[:, pl.ds((1 - c) * rh, rh), :], dst_ref=outs[i],
                send_sem=send.at[i], recv_sem=recv.at[i], device_id=sib, device_id_type=MESH)
            cp.start()
            cps.append(cp)
        for cp in cps:
            cp.wait()

    out_shape = [jax.ShapeDtypeStruct((g.shape[0], g.shape[1] // 2, g.shape[2]), g.dtype) for g in grads]
    return pl.pallas_call(
        body, name=name, in_specs=[HBM] * na, out_specs=[HBM] * na, out_shape=out_shape,
        scratch_shapes=[pltpu.SemaphoreType.DMA((na,)), pltpu.SemaphoreType.DMA((na,))],
    )(*grads)


def _sibling_swap_whole(halves):
    na = len(halves)

    def body(*refs):
        ins, outs = refs[:na], refs[na:2 * na]
        send, recv = refs[2 * na:]
        x, y, c, _ = _place()
        cps = []
        for i in range(na):
            cp = pltpu.make_async_remote_copy(
                src_ref=ins[i], dst_ref=outs[i], send_sem=send.at[i], recv_sem=recv.at[i],
                device_id=(x, y, 1 - c), device_id_type=MESH)
            cp.start()
            cps.append(cp)
        for cp in cps:
            cp.wait()

    out_shape = [jax.ShapeDtypeStruct(h.shape, h.dtype) for h in halves]
    return pl.pallas_call(
        body, name="grad_sibling_join", in_specs=[HBM] * na, out_specs=[HBM] * na, out_shape=out_shape,
        scratch_shapes=[pltpu.SemaphoreType.DMA((na,)), pltpu.SemaphoreType.DMA((na,))],
    )(*halves)


SEM = pl.BlockSpec(memory_space=pltpu.SEMAPHORE)
ANY = pl.BlockSpec(memory_space=pl.ANY)
EFFECT = pltpu.SideEffectType.DATAFLOW_SIDE_EFFECTING


def _split_copy_start(name, plan, srcs, lands, after):
    ns, nl = len(srcs), len(lands)

    def body(*refs):
        src_refs, land_refs = refs[:ns], refs[ns:ns + nl]
        send, recv = refs[ns + nl + 1], refs[ns + nl + 2]
        token = refs[-1]
        outgoing, _ = plan(src_refs, land_refs)
        for src, dst, dev, si, ri in outgoing:
            pltpu.make_async_remote_copy(src_ref=src, dst_ref=dst, send_sem=send.at[si], recv_sem=recv.at[ri],
                                         device_id=dev, device_id_type=MESH).start()
        token[...] = jnp.zeros_like(token)

    n_out, n_in = plan.counts
    thru = [pltpu.HBM(a.shape, a.dtype) for a in list(srcs) + list(lands)]
    res = pl.pallas_call(
        body, name=name,
        out_shape=[pltpu.SemaphoreType.DMA((n_out,)), pltpu.SemaphoreType.DMA((n_in,))] + thru
        + [jax.ShapeDtypeStruct((8, 128), F32)],
        in_specs=[HBM] * (ns + nl) + [ANY],
        out_specs=[SEM, SEM] + [HBM] * (ns + nl) + [pl.BlockSpec(memory_space=pltpu.VMEM)],
        input_output_aliases={i: 2 + i for i in range(ns + nl)},
        compiler_params=pltpu.CompilerParams(has_side_effects=EFFECT),
    )(*[pltpu.with_memory_space_constraint(a, pltpu.HBM) for a in list(srcs) + list(lands)], after)
    return res[0], res[1], res[2:2 + ns], res[2 + ns:2 + ns + nl], res[-1]


def _split_copy_wait(name, plan, send, recv, srcs, lands, after):
    ns, nl = len(srcs), len(lands)
    after = list(after) if isinstance(after, (list, tuple)) else [after]

    def body(*refs):
        src_refs, land_refs = refs[:ns], refs[ns:ns + nl]
        send_ref, recv_ref = refs[ns + nl], refs[ns + nl + 1]
        outgoing, arrivals = plan(src_refs, land_refs)
        for src, dst, dev, si, ri in outgoing:
            pltpu.make_async_remote_copy(src_ref=src, dst_ref=dst, send_sem=send_ref.at[si], recv_sem=recv_ref.at[ri],
                                         device_id=dev, device_id_type=MESH).wait_send()
        for view, ri in arrivals:
            pltpu.make_async_remote_copy(src_ref=view, dst_ref=view, send_sem=send_ref.at[0], recv_sem=recv_ref.at[ri],
                                         device_id=_place()[:3], device_id_type=MESH).wait_recv()

    thru = [pltpu.HBM(a.shape, a.dtype) for a in list(srcs) + list(lands)]
    res = pl.pallas_call(
        body, name=name, out_shape=thru,
        in_specs=[HBM] * (ns + nl) + [SEM, SEM] + [ANY] * len(after), out_specs=[HBM] * (ns + nl),
        input_output_aliases={i: i for i in range(ns + nl)},
        compiler_params=pltpu.CompilerParams(has_side_effects=EFFECT),
    )(*srcs, *lands, send, recv, *after)
    return res[:ns], res[ns:]


def _gather_plan(n_arrays):
    def plan(src_refs, land_refs):
        x, y, c, chips = _place()
        me = 2 * x + y
        outgoing, arrivals = [], []
        for i in range(n_arrays):
            rh = src_refs[i].shape[0] // 2
            mine = pl.ds(c * rh, rh)
            for j, (px, py) in enumerate(chips):
                for delta in range(2):
                    tc = c ^ delta
                    outgoing.append((src_refs[i].at[mine], land_refs[i].at[me, mine], (px, py, tc),
                                     6 * i + 2 * j + delta, 6 * i + 2 * j + delta))
                    theirs = pl.ds(tc * rh, rh)
                    arrivals.append((land_refs[i].at[2 * px + py, theirs], 6 * i + 2 * j + delta))
        return outgoing, arrivals

    plan.counts = (6 * n_arrays, 6 * n_arrays)
    return plan


def _first_gather_plan():
    def plan(src_refs, land_refs):
        x, y, c, chips = _place()
        me = 2 * x + y
        rh = src_refs[0].shape[0] // 2
        mine = pl.ds(c * rh, rh)
        outgoing, arrivals = [], []
        for j, (px, py) in enumerate(chips):
            outgoing.append((src_refs[0].at[mine], land_refs[0].at[me, mine], (px, py, c), j, j))
            arrivals.append((land_refs[0].at[2 * px + py, mine], j))
            outgoing.append((src_refs[1], land_refs[1].at[me], (px, py, c), 3 + j, 3 + j))
            arrivals.append((land_refs[1].at[2 * px + py], 3 + j))
        return outgoing, arrivals

    plan.counts = (6, 6)
    return plan


def _exchange_plan(n_arrays):
    def plan(src_refs, land_refs):
        x, y, c, chips = _place()
        outgoing, arrivals = [], []
        for i in range(n_arrays):
            for j, (px, py) in enumerate(chips):
                outgoing.append((src_refs[i].at[2 * px + py], land_refs[i].at[j], (px, py, c), 3 * i + j, 3 * i + j))
                arrivals.append((land_refs[i].at[j], 3 * i + j))
        return outgoing, arrivals

    plan.counts = (3 * n_arrays, 3 * n_arrays)
    return plan


def _small_allreduce(vec):
    r, cdim = vec.shape
    n_dev = 8

    def body(v_ref, out_ref, buf, send, recv):
        x, y, c, _ = _place()
        me = 4 * x + 2 * y + c
        buf[me] = v_ref[...]
        cps = []
        for k in range(1, n_dev):
            dx, dy, dc = (k >> 2) & 1, (k >> 1) & 1, k & 1
            peer = (x ^ dx, y ^ dy, c ^ dc)
            cp = pltpu.make_async_remote_copy(
                src_ref=v_ref, dst_ref=buf.at[me], send_sem=send.at[k - 1], recv_sem=recv.at[k - 1],
                device_id=peer, device_id_type=MESH)
            cp.start()
            cps.append(cp)
        for k in range(1, n_dev):
            dx, dy, dc = (k >> 2) & 1, (k >> 1) & 1, k & 1
            src = 4 * (x ^ dx) + 2 * (y ^ dy) + (c ^ dc)
            slot = buf.at[src]
            pltpu.make_async_remote_copy(
                src_ref=slot, dst_ref=slot, send_sem=send.at[k - 1], recv_sem=recv.at[k - 1],
                device_id=(x ^ dx, y ^ dy, c ^ dc), device_id_type=MESH).wait_recv()
        for cp in cps:
            cp.wait_send()
        acc = buf[0]
        for k in range(1, n_dev):
            acc = acc + buf[k]
        out_ref[...] = acc

    vm = pl.BlockSpec(memory_space=pltpu.VMEM)
    return pl.pallas_call(
        body, name="small_allreduce", in_specs=[vm], out_specs=vm,
        out_shape=jax.ShapeDtypeStruct((r, cdim), F32),
        scratch_shapes=[pltpu.VMEM((n_dev, r, cdim), F32), pltpu.SemaphoreType.DMA((n_dev - 1,)),
                        pltpu.SemaphoreType.DMA((n_dev - 1,))],
    )(vec)


def _a_cols_to_head_major(w):
    lead = w.shape[:-1]
    q = w[..., :A_QK].reshape(lead + (A_HEADS, A_DK))
    k = w[..., A_QK:2 * A_QK].reshape(lead + (A_HEADS, A_DK))
    v = w[..., 2 * A_QK:2 * A_QK + A_VW].reshape(lead + (A_HEADS, A_DV))
    z = w[..., 2 * A_QK + A_VW:].reshape(lead + (A_HEADS, A_DV))
    return jnp.concatenate([q, k, v, z], axis=-1).reshape(lead + (A_HEADS * A_HEAD_COLS,))


def _a_cols_from_head_major(w):
    lead = w.shape[:-1]
    w = w.reshape(lead + (A_HEADS, A_HEAD_COLS))
    parts = [w[..., :A_DK], w[..., A_DK:2 * A_DK], w[..., 2 * A_DK:2 * A_DK + A_DV], w[..., 2 * A_DK + A_DV:]]
    return jnp.concatenate([p.reshape(lead + (-1,)) for p in parts], axis=-1)


def _conv_cols_to_head_major(w):
    lead = w.shape[:-1]
    q = w[..., :A_QK].reshape(lead + (A_HEADS, A_DK))
    k = w[..., A_QK:2 * A_QK].reshape(lead + (A_HEADS, A_DK))
    v = w[..., 2 * A_QK:].reshape(lead + (A_HEADS, A_DV))
    return jnp.concatenate([q, k, v], axis=-1).reshape(lead + (A_HEADS * A_CONV_COLS,))


def _conv_cols_from_head_major(w):
    lead = w.shape[:-1]
    w = w.reshape(lead + (A_HEADS, A_CONV_COLS))
    parts = [w[..., :A_DK], w[..., A_DK:2 * A_DK], w[..., 2 * A_DK:]]
    return jnp.concatenate([p.reshape(lead + (-1,)) for p in parts], axis=-1)


def _to_stream(a, bn, d):
    rest = a.shape[1:]
    s = a.shape[0] // bn
    a = a.reshape((bn, s // d, d) + rest)
    a = jnp.swapaxes(a, 1, 2)
    return a.reshape((bn * d, s // d) + rest)


def _from_stream(a, bn, d):
    rest = a.shape[2:]
    ln = a.shape[1]
    a = a.reshape((bn, d, ln) + rest)
    a = jnp.swapaxes(a, 1, 2)
    return a.reshape((bn * ln * d,) + rest)


B_SUB = 512
B_SHARD_BLOCKS = (3 * B_GROUPS * B_W + B_W) // N_CHIPS // B_SUB


def _b_block(gi, jj):
    nb = (B_GROUPS * (jj // 2) + gi) * 2 + jj % 2
    return nb // B_SHARD_BLOCKS, nb % B_SHARD_BLOCKS


def _shard_major(g, ncols):
    r = g.shape[0]
    return jnp.swapaxes(g.reshape(r, N_CHIPS, ncols), 0, 1)


def _pack_rows(items):
    rows, offs = [], []
    at = 0
    for a in items:
        flat = a.reshape(-1).astype(F32)
        nr = -(-flat.shape[0] // 1024) * 8
        flat = jnp.pad(flat, (0, nr * 128 - flat.shape[0]))
        rows.append(flat.reshape(nr, 128))
        offs.append((at, nr, a.shape))
        at += nr
    return jnp.concatenate(rows, axis=0), offs


def _unpack_rows(packed, offs):
    out = []
    for at, nr, shape in offs:
        size = int(np.prod(shape)) if len(shape) else 1
        out.append(packed[at:at + nr].reshape(-1)[:size].reshape(shape))
    return out


def _local_step(x, positions, loss_target, norm_g, a_log, a_dt_bias, a_norm_g, b_q_norm_g, b_k_norm_g,
                start_token, first_weights, late_weights, b_grads_ready, a_grads_ready):
    bn, s, d = x.shape
    t = bn * s
    n_chunks = s // A_CHUNK
    x0 = x.reshape(t, d)
    h0 = _rms_fwd(x0, norm_g[0:1] + start_token, "rms0_fwd")
    inv_freq = ROPE_THETA ** (-jnp.arange(0, ROPE_DIMS, 2, dtype=F32) / ROPE_DIMS)
    freq_row = jnp.concatenate([inv_freq, inv_freq, jnp.zeros((128 - ROPE_DIMS,), F32)]).reshape(1, 128)
    posf = jnp.broadcast_to(positions.astype(F32).reshape(t, 1), (t, 128)) + start_token
    tabs = _rope_tables(posf, freq_row)
    tabs_s = [tabs if dil == 1 else [_to_stream(tb, bn, dil).reshape(t, 128) for tb in tabs] for dil in B_DIL]
    wa_in, conv_w, late_token = first_weights([h0] + [tb for ts in tabs_s for tb in ts])
    wa_main = _a_cols_to_head_major(wa_in[:, :A_MAIN])
    wa_tail = jnp.pad(wa_in[:, A_MAIN:], ((0, 0), (0, 128 - 2 * A_HEADS))) + late_token.astype(BF16)
    cw_hm = _conv_cols_to_head_major(conv_w)

    proj_a = _matmul(h0, wa_main, "nn", F32, "a_in_main")
    tail_a = _matmul(h0, wa_tail, "nn", F32, "a_in_tail")
    tail_t = jnp.swapaxes(tail_a[:, :2 * A_HEADS].reshape(bn, s, 2 * A_HEADS), 1, 2)
    tail_t = tail_t.reshape(bn, 2 * A_HEADS, n_chunks, A_CHUNK)
    beta, gc = _gdn_prep(tail_t, a_log[0], a_dt_bias[0])
    proj_a3 = proj_a.reshape(bn, s, A_MAIN)
    og_a, oraw_a, states, t_mats, conv_y = _gdn_fwd(proj_a3, cw_hm, beta, gc, a_norm_g)
    wa_out, wb_in, wb_out = late_weights(og_a)
    b_cols = [4 * B_W] + [3 * B_W] * (B_GROUPS - 1)
    x1, h1 = _out_proj(og_a.reshape(t, A_VW), wa_out, x0, "a_out", norm_g=norm_g[1:2])

    h1_s, proj_b, qkv_b, o_b, lse_b = [], [], [], [], []
    for gi, dil in enumerate(B_DIL):
        hs = h1 if dil == 1 else _to_stream(h1, bn, dil).reshape(t, d)
        ts = tabs_s[gi]
        pj = _matmul(hs, wb_in, "nn", F32, f"b_in_g{gi}", tm=2048, tn=B_SUB, n=b_cols[gi], b_spec=pl.BlockSpec(
            (None, d, B_SUB), lambda i, j, kk, gi=gi: (_b_block(gi, j)[0], kk, _b_block(gi, j)[1])))
        qkv = _qk_prep(pj, *ts, b_q_norm_g[0, gi:gi + 1], b_k_norm_g[0, gi:gi + 1], f"qk_prep_g{gi}")
        o_s, lse_s = _attn_fwd(qkv.reshape(bn * dil, s // dil, 3 * B_W), f"attn_fwd_g{gi}")
        h1_s.append(hs), proj_b.append(pj), qkv_b.append(qkv)
        o_b.append(o_s.reshape(t, B_W) if dil == 1 else _from_stream(o_s, bn, dil))
        lse_b.append(lse_s.reshape(t, B_HEADS) if dil == 1 else _from_stream(lse_s, bn, dil))
    og_b = _merge_fwd(o_b, lse_b, proj_b[0])
    d_x2, loss_parts = _out_proj(og_b, wb_out, x1, "b_out_loss", target=loss_target.reshape(t, d))
    loss_local = jnp.sum(loss_parts)

    d_x2b = d_x2.astype(BF16)
    g_wb_out = _matmul(og_b, d_x2b, "tn", F32, "b_out_dw")
    d_og_b = _matmul(d_x2b, wb_out, "nt", F32, "b_out_dx")
    d_o, lse_joint, delta, d_z = _merge_bwd(o_b, lse_b, proj_b[0], d_og_b)
    d_h1, g_qn, g_kn = [], [], []
    g_wb_in = lax.empty(wb_in.shape, F32)
    for gi, dil in enumerate(B_DIL):
        if dil == 1:
            do_s, lj_s, dl_s = d_o, lse_joint, delta
        else:
            do_s, lj_s, dl_s = (_to_stream(a, bn, dil).reshape(t, -1) for a in (d_o, lse_joint, delta))
        ns, ln = bn * dil, s // dil
        dq, dk, dv = _attn_bwd(qkv_b[gi].reshape(ns, ln, 3 * B_W), do_s.reshape(ns, ln, B_W),
                               lj_s.reshape(ns, ln, B_HEADS), dl_s.reshape(ns, ln, B_HEADS), f"attn_bwd_g{gi}")
        d_pj, d_gain = _qk_prep_bwd(proj_b[gi], *tabs_s[gi], b_q_norm_g[0, gi:gi + 1], b_k_norm_g[0, gi:gi + 1],
                                    dq.reshape(t, B_W), dk.reshape(t, B_W), dv.reshape(t, B_W),
                                    d_z if gi == 0 else None, f"qk_prep_bwd_g{gi}")
        g_wb_in = _matmul(h1_s[gi], d_pj, "tn", F32, f"b_in_dw_g{gi}", tn=B_SUB, tk=2048, into=(g_wb_in, pl.BlockSpec(
            (None, d, B_SUB), lambda i, j, kk, gi=gi: (_b_block(gi, j)[0], i, _b_block(gi, j)[1]))))
        dh = _matmul(d_pj, wb_in, "nt", BF16, f"b_in_dx_g{gi}", tm=2048, tk=B_SUB, n=d, b_spec=pl.BlockSpec(
            (None, d, B_SUB), lambda i, j, kk, gi=gi: (_b_block(gi, kk)[0], j, _b_block(gi, kk)[1])))
        d_h1.append(dh if dil == 1 else _from_stream(dh.reshape(ns, ln, d), bn, dil))
        g_qn.append(d_gain[0]), g_kn.append(d_gain[1])
    d_x1, g_norm1 = _rms_bwd(x1, norm_g[1:2], d_h1, d_x2, "rms1_bwd")

    d_x1b = d_x1.astype(BF16)
    g_wa_out = _matmul(og_a.reshape(t, A_VW), d_x1b, "tn", F32, "a_out_dw")
    b_token = b_grads_ready(g_wb_in, g_wb_out, g_wa_out)
    d_og_a = _matmul(d_x1b, wa_out, "nt", F32, "a_out_dx")
    d_pa, d_gc, d_beta, d_cw, d_ng = _gdn_bwd(proj_a3, cw_hm, beta, gc, a_norm_g + b_token, oraw_a, states,
                                              t_mats, conv_y, d_og_a.reshape(bn, s, A_VW))
    d_tail_t, d_alog, d_dtb = _gdn_prep_bwd(tail_t, a_log[0], a_dt_bias[0], d_gc, d_beta)
    d_tail = jnp.swapaxes(d_tail_t.reshape(bn, 2 * A_HEADS, s), 1, 2).reshape(t, 2 * A_HEADS)
    d_tail = jnp.pad(d_tail, ((0, 0), (0, 128 - 2 * A_HEADS))).astype(BF16)
    d_pa = d_pa.reshape(t, A_MAIN)
    g_wa_main = _matmul(h0, d_pa, "tn", F32, "a_in_dw_main")
    g_wa_tail = _matmul(h0, d_tail, "tn", F32, "a_in_dw_tail")
    g_wa_in = jnp.concatenate([_a_cols_from_head_major(g_wa_main), g_wa_tail[:, :2 * A_HEADS]], axis=1)
    a_token = a_grads_ready(g_wa_in)
    d_h0t = _matmul(d_tail + a_token.astype(BF16), wa_tail, "nt", F32, "a_in_dx_tail")
    d_x0, g_norm0 = _in_proj_bwd(d_pa, wa_main, d_h0t, x0, norm_g[0:1], d_x1, "a_in_dx_rms0_bwd")

    gfull = {
        "norm_g": jnp.concatenate([g_norm0, g_norm1], axis=0), "a_w_in": g_wa_in,
        "a_conv_w": _conv_cols_from_head_major(jnp.sum(d_cw, axis=0)),
        "a_log": jnp.sum(d_alog[:, :, 0], axis=0), "a_dt_bias": jnp.sum(d_dtb[:, :, 0], axis=0),
        "a_norm_g": jnp.sum(d_ng[:, :, 0, :], axis=(0, 1)), "a_w_out": g_wa_out, "b_w_in": g_wb_in,
        "b_q_norm_g": jnp.stack(g_qn), "b_k_norm_g": jnp.stack(g_kn), "b_w_out": g_wb_out}
    return loss_local, d_x0.reshape(bn, s, d), gfull


def kernel(x, positions, norm_g, a_w_in, a_conv_w, a_log, a_dt_bias, a_norm_g, a_w_out, b_w_in, b_q_norm_g, b_k_norm_g, b_w_out, loss_target, m_norm_g, m_a_w_in, m_a_conv_w, m_a_log, m_a_dt_bias, m_a_norm_g, m_a_w_out, m_b_w_in, m_b_q_norm_g, m_b_k_norm_g, m_b_w_out, v_norm_g, v_a_w_in, v_a_conv_w, v_a_log, v_a_dt_bias, v_a_norm_g, v_a_w_out, v_b_w_in, v_b_q_norm_g, v_b_k_norm_g, v_b_w_out):
    d = x.shape[2]
    my_c = lax.axis_index("c")
    my_chip = 2 * lax.axis_index("x") + lax.axis_index("y")

    half_index = jnp.reshape(my_c, (1,)).astype(jnp.int32)
    chip_index = jnp.reshape(my_chip, (1,)).astype(jnp.int32)
    def landing(shard):
        return lax.dynamic_update_slice(lax.empty((N_CHIPS,) + shard.shape, shard.dtype), shard[None],
                                        (my_chip,) + (0,) * shard.ndim)

    first_shards = [a_w_in[0].astype(BF16), a_conv_w[0]]
    first_plan = _first_gather_plan()
    first = _split_copy_start("first_weights_start", first_plan, first_shards,
                              [landing(s) for s in first_shards], half_index)
    pending = {}
    late_shards = [(w[0] + first[4][0, 0]).astype(BF16) for w in (a_w_out, b_w_in, b_w_out)]
    late_lands = [landing(s) for s in late_shards]

    def first_weights(after):
        _, (ga_in, g_conv) = _split_copy_wait("first_weights_wait", first_plan, *first[:4],
                                              list(after) + late_lands)
        ga_in = _sibling_forward(ga_in)
        wa_in = jnp.concatenate([ga_in[k] for k in range(N_CHIPS)], axis=1)
        conv_w = jnp.concatenate([g_conv[k] for k in range(N_CHIPS)], axis=1)
        plan = _gather_plan(len(late_shards))
        pending["late"] = (plan,) + tuple(_split_copy_start(
            "late_weights_start", plan, late_shards, late_lands, conv_w))
        return wa_in, conv_w, pending["late"][5][0, 0]

    def late_weights(after):
        plan, send, recv, srcs, lands, _ = pending["late"]
        _, (ga_out, gb_in, gb_out) = _split_copy_wait("late_weights_wait", plan, send, recv, srcs, lands, after)
        return ga_out.reshape(A_VW, d), gb_in, gb_out.reshape(B_W, d)

    def reduce_to_chip_sums(mats, tag):
        recv_sib = _sibling_swap_halves([g.astype(BF16) for g in mats], f"grad_{tag}_sibling_swap")
        return [_pair_sum(g, r, half_index, f"grad_{tag}_pair_sum_{i}") for i, (g, r) in enumerate(zip(mats, recv_sib))]

    def start_exchange(tag, mats):
        sums = reduce_to_chip_sums(mats, tag)
        lands = [lax.empty((N_CHIPS - 1,) + s.shape[1:], BF16) for s in sums]
        plan = _exchange_plan(len(mats))
        pending[tag] = (plan,) + tuple(_split_copy_start(f"grad_{tag}_exchange_start", plan, sums, lands, chip_index))
        return pending[tag][5][0, 0]

    def finish_exchange(tag, after):
        plan, send, recv, srcs, lands, _ = pending[tag]
        return _split_copy_wait(f"grad_{tag}_exchange_wait", plan, send, recv, srcs, lands, after)

    def b_grads_ready(g_wb_in, g_wb_out, g_wa_out):
        return start_exchange("b", [g_wb_in, g_wb_out.reshape(N_CHIPS, -1, d), g_wa_out.reshape(N_CHIPS, -1, d)])

    def a_grads_ready(g_wa_in):
        return start_exchange("a", [_shard_major(g_wa_in, a_w_in.shape[2])])

    loss_local, d_x0, gfull = _local_step(x, positions, loss_target, norm_g, a_log, a_dt_bias, a_norm_g,
                                          b_q_norm_g, b_k_norm_g, first[4][0, 0], first_weights, late_weights,
                                          b_grads_ready, a_grads_ready)

    small = [gfull["norm_g"], gfull["a_conv_w"], gfull["a_log"], gfull["a_dt_bias"], gfull["a_norm_g"],
             gfull["b_q_norm_g"], gfull["b_k_norm_g"], loss_local]
    packed, offs = _pack_rows(small)
    reduced = _small_allreduce(packed)
    g_norm, g_conv_all, g_alog, g_dtb, g_ang, g_q, g_k, loss = _unpack_rows(reduced, offs)
    g_conv_mine = lax.dynamic_slice_in_dim(g_conv_all, my_chip * a_conv_w.shape[2], a_conv_w.shape[2], axis=1)

    b_sums, b_received = finish_exchange("b", d_x0)
    a_sums, a_received = finish_exchange("a", reduced)
    chip_sums = [a_sums[0], b_sums[2], b_sums[0], b_sums[1]]
    received = [a_received[0], b_received[2], b_received[0], b_received[1]]
    halves = [_chip_sum(s, r, chip_index, f"grad_chip_sum_{i}") for i, (s, r) in enumerate(zip(chip_sums, received))]
    theirs = _sibling_swap_whole(halves)
    big = ("a_w_in", "a_w_out", "b_w_in", "b_w_out")
    big_halves = dict(zip(big, zip(halves, theirs)))

    grads = {
        "norm_g": g_norm, "a_conv_w": g_conv_mine[None], "a_log": g_alog[None], "a_dt_bias": g_dtb[None],
        "a_norm_g": g_ang[None], "b_q_norm_g": g_q[None], "b_k_norm_g": g_k[None]}
    weights = {"norm_g": norm_g, "a_w_in": a_w_in, "a_conv_w": a_conv_w, "a_log": a_log, "a_dt_bias": a_dt_bias,
               "a_norm_g": a_norm_g, "a_w_out": a_w_out, "b_w_in": b_w_in, "b_q_norm_g": b_q_norm_g,
               "b_k_norm_g": b_k_norm_g, "b_w_out": b_w_out}
    m_in = {"norm_g": m_norm_g, "a_w_in": m_a_w_in, "a_conv_w": m_a_conv_w, "a_log": m_a_log,
            "a_dt_bias": m_a_dt_bias, "a_norm_g": m_a_norm_g, "a_w_out": m_a_w_out, "b_w_in": m_b_w_in,
            "b_q_norm_g": m_b_q_norm_g, "b_k_norm_g": m_b_k_norm_g, "b_w_out": m_b_w_out}
    v_in = {"norm_g": v_norm_g, "a_w_in": v_a_w_in, "a_conv_w": v_a_conv_w, "a_log": v_a_log,
            "a_dt_bias": v_a_dt_bias, "a_norm_g": v_a_norm_g, "a_w_out": v_a_w_out, "b_w_in": v_b_w_in,
            "b_q_norm_g": v_b_q_norm_g, "b_k_norm_g": v_b_k_norm_g, "b_w_out": v_b_w_out}
    names = list(weights)

    delta_w, new_m, new_v = {}, {}, {}
    for nm in big:
        mine, other = big_halves[nm]
        if weights[nm].shape[2] % 128:
            cols = lambda a: jnp.transpose(a, (2, 0, 1))
            half_cols = lambda a: jnp.transpose(a)[:, None, :]
            outs = _adamw_shard_cols(cols(weights[nm]), half_cols(mine), half_cols(other), cols(m_in[nm]),
                                     cols(v_in[nm]), half_index, f"adamw_{nm}")
            outs = [jnp.transpose(o, (1, 2, 0)) for o in outs]
        else:
            outs = _adamw_shard(weights[nm], mine, other, m_in[nm], v_in[nm], half_index, f"adamw_{nm}")
        grads[nm], delta_w[nm], new_m[nm], new_v[nm] = outs
    small_names = [nm for nm in names if nm not in big]
    packs = [_pack_rows([src[nm] for nm in small_names]) for src in (weights, grads, m_in, v_in)]
    offs = packs[0][1]
    dl, m2, v2 = _adamw(packs[0][0], packs[1][0], packs[2][0], packs[3][0], "adamw_small")
    for nm, a, b, c2 in zip(small_names, _unpack_rows(dl, offs), _unpack_rows(m2, offs), _unpack_rows(v2, offs)):
        delta_w[nm], new_m[nm], new_v[nm] = a, b, c2

    return (loss, d_x0, *[grads[nm] for nm in names], *[delta_w[nm] for nm in names],
            *[new_m[nm] for nm in names], *[new_v[nm] for nm in names])
```

```python
import jax
import jax.numpy as jnp
import numpy as np
from jax import lax
from jax.experimental import pallas as pl
from jax.experimental.pallas import tpu as pltpu

F32 = jnp.float32
BF16 = jnp.bfloat16
MESH = pl.DeviceIdType.MESH

EPS = 1e-6
A_HEADS = 8
A_DK = 128
A_DV = 256
A_QK = A_HEADS * A_DK
A_VW = A_HEADS * A_DV
A_MAIN = 2 * A_QK + 2 * A_VW
A_HEAD_COLS = 2 * A_DK + 2 * A_DV
A_CONV_COLS = 2 * A_DK + A_DV
A_CHUNK = 64
A_CONV = 4
B_GROUPS = 3
B_HEADS = 8
B_DH = 128
B_W = B_HEADS * B_DH
B_DIL = (1, 4, 16)
B_BLK = 128
ROPE_THETA = 500000.0
ROPE_DIMS = B_DH // 4
ADAM_LR, ADAM_B1, ADAM_B2, ADAM_EPS, ADAM_WD, ADAM_STEP = 0.001, 0.9, 0.999, 1e-08, 0.01, 10
N_CHIPS = 4
VMEM_BIG = 56 * 1024 * 1024


def _params(sem=None, vmem=None):
    return pltpu.CompilerParams(dimension_semantics=sem, vmem_limit_bytes=vmem)


def _dot(a, b, ca, cb):
    return lax.dot_general(a.astype(BF16), b.astype(BF16), (((ca,), (cb,)), ((), ())),
                           preferred_element_type=F32)


def _split3(a):
    hi = a.astype(BF16)
    r = a - hi.astype(F32)
    mid = r.astype(BF16)
    lo = (r - mid.astype(F32)).astype(BF16)
    return hi, mid, lo


def _sigmoid(y):
    return 1.0 / (1.0 + jnp.exp(-y))


def _silu(y):
    return y * _sigmoid(y)


def _silu_and_slope(y):
    s = _sigmoid(y)
    return y * s, s * (1.0 + y * (1.0 - s))


def _matmul(a, b, mode, out_dtype, name, res=None, tm=1024, tn=1024, tk=1024, n=None, b_spec=None, into=None):
    m, k = a.shape[::-1] if mode == "tn" else a.shape
    if n is None:
        n = b.shape[0] if mode == "nt" else b.shape[1]
    tm, tn, tk = min(tm, m), min(tn, n), min(tk, k)
    assert m % tm == 0 and n % tn == 0 and k % tk == 0, (name, a.shape, b.shape)
    nk = k // tk
    dims = {"nn": ((1,), (0,)), "nt": ((1,), (1,)), "tn": ((0,), (0,))}[mode]

    def body(*refs):
        a_ref, b_ref = refs[0], refs[1]
        r_ref = refs[2] if res is not None else None
        o_ref = refs[2 + (res is not None) + (into is not None)]
        prod = lax.dot_general(a_ref[...], b_ref[...], (dims, ((), ())), preferred_element_type=F32)

        def finish(r):
            if res is not None:
                r = r + r_ref[...]
            o_ref[...] = r.astype(out_dtype)

        if nk == 1:
            finish(prod)
            return
        acc = refs[-1]
        kk = pl.program_id(2)

        @pl.when(kk == 0)
        def _():
            acc[...] = prod

        @pl.when((kk > 0) & (kk < nk - 1))
        def _():
            acc[...] += prod

        @pl.when(kk == nk - 1)
        def _():
            finish(acc[...] + prod)

    a_spec = pl.BlockSpec((tm, tk), lambda i, j, kk: (i, kk))
    if mode == "tn":
        a_spec = pl.BlockSpec((tk, tm), lambda i, j, kk: (kk, i))
    if b_spec is None and mode == "nt":
        b_spec = pl.BlockSpec((tn, tk), lambda i, j, kk: (j, kk))
    elif b_spec is None:
        b_spec = pl.BlockSpec((tk, tn), lambda i, j, kk: (kk, j))
    in_specs = [a_spec, b_spec]
    args = [a, b]
    if res is not None:
        in_specs.append(pl.BlockSpec((tm, tn), lambda i, j, kk: (i, j)))
        args.append(res)
    out_spec = pl.BlockSpec((tm, tn), lambda i, j, kk: (i, j))
    out_shape = jax.ShapeDtypeStruct((m, n), out_dtype)
    aliases = {}
    if into is not None:
        assert res is None
        buf, out_spec = into
        out_shape = jax.ShapeDtypeStruct(buf.shape, buf.dtype)
        in_specs.append(ANY)
        args.append(buf)
        aliases = {2: 0}
    return pl.pallas_call(
        body, name=name, grid=(m // tm, n // tn, nk),
        in_specs=in_specs, out_specs=out_spec, out_shape=out_shape, input_output_aliases=aliases,
        scratch_shapes=[pltpu.VMEM((tm, tn), F32)] if nk > 1 else [],
        compiler_params=_params(("parallel", "parallel", "arbitrary"), 48 * 1024 * 1024),
    )(*args)


def _rms_fwd(x, g, name, tm=256):
    t, d = x.shape

    def body(x_ref, g_ref, h_ref):
        xv = x_ref[...]
        r = lax.rsqrt(jnp.mean(xv * xv, axis=-1, keepdims=True) + EPS)
        h_ref[...] = (xv * r * g_ref[...]).astype(BF16)

    return pl.pallas_call(
        body, name=name, grid=(t // tm,),
        in_specs=[pl.BlockSpec((tm, d), lambda i: (i, 0)), pl.BlockSpec((1, d), lambda i: (0, 0))],
        out_specs=pl.BlockSpec((tm, d), lambda i: (i, 0)),
        out_shape=jax.ShapeDtypeStruct((t, d), BF16),
        compiler_params=_params(("parallel",)),
    )(x, g)


def _rms_bwd(x, g, dhs, dres, name, tm=256):
    t, d = x.shape
    n_dh = len(dhs)

    def body(*refs):
        x_ref, g_ref = refs[0], refs[1]
        dh_refs = refs[2:2 + n_dh]
        dres_ref, dx_ref, dg_ref = refs[2 + n_dh:]
        i = pl.program_id(0)

        @pl.when(i == 0)
        def _():
            dg_ref[...] = jnp.zeros_like(dg_ref)

        xv = x_ref[...]
        r = lax.rsqrt(jnp.mean(xv * xv, axis=-1, keepdims=True) + EPS)
        xh = xv * r
        dh = dh_refs[0][...].astype(F32)
        for ref in dh_refs[1:]:
            dh = dh + ref[...].astype(F32)
        dg_ref[0:1, :] += jnp.sum(dh * xh, axis=0, keepdims=True)
        dxh = dh * g_ref[...]
        dx = r * (dxh - xh * jnp.mean(dxh * xh, axis=-1, keepdims=True))
        dx_ref[...] = dx + dres_ref[...]

    row = pl.BlockSpec((tm, d), lambda i: (i, 0))
    dx, dg = pl.pallas_call(
        body, name=name, grid=(t // tm,),
        in_specs=[row, pl.BlockSpec((1, d), lambda i: (0, 0))] + [row] * n_dh + [row],
        out_specs=[row, pl.BlockSpec((8, d), lambda i: (0, 0))],
        out_shape=[jax.ShapeDtypeStruct((t, d), F32), jax.ShapeDtypeStruct((8, d), F32)],
        compiler_params=_params(("arbitrary",)),
    )(x, g, *dhs, dres)
    return dx, dg[0:1]


def _in_proj_bwd(dp, w, dh_more, x, g, dres, name, tm=512, tk=1024):
    t, k = dp.shape
    d = w.shape[0]
    nk = k // tk

    def body(dp_ref, w_ref, more_ref, x_ref, g_ref, dres_ref, dx_ref, dg_ref, acc):
        i, kk = pl.program_id(0), pl.program_id(1)

        @pl.when((i == 0) & (kk == 0))
        def _():
            dg_ref[...] = jnp.zeros_like(dg_ref)

        prod = lax.dot_general(dp_ref[...], w_ref[...], (((1,), (1,)), ((), ())), preferred_element_type=F32)

        @pl.when(kk == 0)
        def _():
            acc[...] = prod

        @pl.when((kk > 0) & (kk < nk - 1))
        def _():
            acc[...] += prod

        @pl.when(kk == nk - 1)
        def _():
            dh = acc[...] + prod + more_ref[...]
            xv = x_ref[...]
            r = lax.rsqrt(jnp.mean(xv * xv, axis=-1, keepdims=True) + EPS)
            xh = xv * r
            dg_ref[0:1, :] += jnp.sum(dh * xh, axis=0, keepdims=True)
            dxh = dh * g_ref[...]
            dx_ref[...] = r * (dxh - xh * jnp.mean(dxh * xh, axis=-1, keepdims=True)) + dres_ref[...]

    row = pl.BlockSpec((tm, d), lambda i, kk: (i, 0))
    dx, dg = pl.pallas_call(
        body, name=name, grid=(t // tm, nk),
        in_specs=[pl.BlockSpec((tm, tk), lambda i, kk: (i, kk)), pl.BlockSpec((d, tk), lambda i, kk: (0, kk)),
                  row, row, pl.BlockSpec((1, d), lambda i, kk: (0, 0)), row],
        out_specs=[row, pl.BlockSpec((8, d), lambda i, kk: (0, 0))],
        out_shape=[jax.ShapeDtypeStruct((t, d), F32), jax.ShapeDtypeStruct((8, d), F32)],
        scratch_shapes=[pltpu.VMEM((tm, d), F32)],
        compiler_params=_params(("arbitrary", "arbitrary"), 48 * 1024 * 1024),
    )(dp, w, dh_more, x, g, dres)
    return dx, dg[0:1]


def _out_proj(a, w, res, name, norm_g=None, target=None, tm=512):
    t, k = a.shape
    d = w.shape[1]
    nb = t // tm

    def body(a_ref, w_ref, r_ref, x_ref, o1_ref, o2_ref):
        y = jnp.dot(a_ref[...], w_ref[...], preferred_element_type=F32) + r_ref[...]
        if norm_g is not None:
            o1_ref[...] = y
            r = lax.rsqrt(jnp.mean(y * y, axis=-1, keepdims=True) + EPS)
            o2_ref[...] = (y * r * x_ref[...]).astype(BF16)
        else:
            e = y - x_ref[...]
            o1_ref[...] = e * (1.0 / d)
            s = jnp.sum(jnp.sum(e * e, axis=1, keepdims=True), axis=0, keepdims=True) * (0.5 / d)
            o2_ref[...] = jnp.broadcast_to(s, (8, 128))

    row = pl.BlockSpec((tm, d), lambda i: (i, 0))
    if norm_g is not None:
        extra, extra_spec = norm_g, pl.BlockSpec((1, d), lambda i: (0, 0))
        out2_spec, out2_shape = row, jax.ShapeDtypeStruct((t, d), BF16)
    else:
        extra, extra_spec = target, row
        out2_spec = pl.BlockSpec((None, 8, 128), lambda i: (i, 0, 0))
        out2_shape = jax.ShapeDtypeStruct((nb, 8, 128), F32)
    o1, o2 = pl.pallas_call(
        body, name=name, grid=(nb,),
        in_specs=[pl.BlockSpec((tm, k), lambda i: (i, 0)), pl.BlockSpec((k, d), lambda i: (0, 0)), row, extra_spec],
        out_specs=[row, out2_spec], out_shape=[jax.ShapeDtypeStruct((t, d), F32), out2_shape],
        compiler_params=_params(("parallel",), 48 * 1024 * 1024),
    )(a, w, res, extra)
    return (o1, o2) if norm_g is not None else (o1, o2[:, 0, 0])


def _softplus(x):
    t = jnp.exp(-jnp.abs(x))
    return jnp.maximum(x, 0.0) + jnp.where(t < 1e-3, t * (1.0 - 0.5 * t), jnp.log(1.0 + t))


def _tri(rows_le_cols):
    r = lax.broadcasted_iota(jnp.int32, (A_CHUNK, A_CHUNK), 0)
    c = lax.broadcasted_iota(jnp.int32, (A_CHUNK, A_CHUNK), 1)
    return jnp.where((r <= c) if rows_le_cols else (r >= c), 1.0, 0.0).astype(BF16)


def _dot_exact_rhs(a, ones_bf16):
    dn = (((1,), (0,)), ((), ()))
    hi, mid, lo = _split3(a)
    out = lax.dot_general(hi, ones_bf16, dn, preferred_element_type=F32)
    out = out + lax.dot_general(mid, ones_bf16, dn, preferred_element_type=F32)
    return out + lax.dot_general(lo, ones_bf16, dn, preferred_element_type=F32)


def _gdn_prep(tail_t, a_log, dt_bias):
    bn, _, n, c = tail_t.shape

    def body(t_ref, alog_ref, dtb_ref, beta_ref, gc_ref):
        upper = _tri(True)
        for h in range(A_HEADS):
            beta_ref[h] = _sigmoid(t_ref[h])
            ea = jnp.exp(jnp.full((n, c), alog_ref[h], F32))
            g = -ea * _softplus(t_ref[A_HEADS + h] + dtb_ref[h])
            gc_ref[h] = _dot_exact_rhs(g, upper)

    smem = pl.BlockSpec(memory_space=pltpu.SMEM)
    blk = pl.BlockSpec((None, A_HEADS, n, c), lambda b: (b, 0, 0, 0))
    return pl.pallas_call(
        body, name="gdn_prep", grid=(bn,),
        in_specs=[pl.BlockSpec((None, 2 * A_HEADS, n, c), lambda b: (b, 0, 0, 0)), smem, smem],
        out_specs=[blk, blk],
        out_shape=[jax.ShapeDtypeStruct((bn, A_HEADS, n, c), F32)] * 2,
        compiler_params=_params(("parallel",)),
    )(tail_t, a_log, dt_bias)


def _gdn_prep_bwd(tail_t, a_log, dt_bias, d_gc, d_beta):
    bn, _, n, c = tail_t.shape

    def body(t_ref, alog_ref, dtb_ref, dgc_ref, dbeta_ref, dt_ref, dal_ref, ddt_ref):
        lower = _tri(False)
        for h in range(A_HEADS):
            beta = _sigmoid(t_ref[h])
            dt_ref[h] = dbeta_ref[h] * beta * (1.0 - beta)
            dg = _dot_exact_rhs(dgc_ref[h], lower)
            ea = jnp.exp(jnp.full((n, c), alog_ref[h], F32))
            xa = t_ref[A_HEADS + h] + dtb_ref[h]
            g = -ea * _softplus(xa)
            dxa = -ea * dg * _sigmoid(xa)
            dt_ref[A_HEADS + h] = dxa
            s1 = jnp.sum(jnp.sum(g * dg, axis=1, keepdims=True), axis=0, keepdims=True)
            s2 = jnp.sum(jnp.sum(dxa, axis=1, keepdims=True), axis=0, keepdims=True)
            dal_ref[h:h + 1, :] = jnp.broadcast_to(s1, (1, 128))
            ddt_ref[h:h + 1, :] = jnp.broadcast_to(s2, (1, 128))

    smem = pl.BlockSpec(memory_space=pltpu.SMEM)
    blk8 = pl.BlockSpec((None, A_HEADS, n, c), lambda b: (b, 0, 0, 0))
    blk16 = pl.BlockSpec((None, 2 * A_HEADS, n, c), lambda b: (b, 0, 0, 0))
    sm = pl.BlockSpec((None, A_HEADS, 128), lambda b: (b, 0, 0))
    return pl.pallas_call(
        body, name="gdn_prep_bwd", grid=(bn,),
        in_specs=[blk16, smem, smem, blk8, blk8],
        out_specs=[blk16, sm, sm],
        out_shape=[jax.ShapeDtypeStruct((bn, 2 * A_HEADS, n, c), F32),
                   jax.ShapeDtypeStruct((bn, A_HEADS, 128), F32),
                   jax.ShapeDtypeStruct((bn, A_HEADS, 128), F32)],
        compiler_params=_params(("parallel",)),
    )(tail_t, a_log, dt_bias, d_gc, d_beta)


HALO = 8


def _conv_taps(xw, w):
    y = w[A_CONV - 1:A_CONV, :] * xw
    for j in range(1, A_CONV):
        y = y + w[A_CONV - 1 - j:A_CONV - j, :] * pltpu.roll(xw, j, 0)
    return y[HALO:, :]


def _row_to_col(row, eye):
    c = eye.shape[0]
    return jnp.sum(jnp.where(eye, jnp.broadcast_to(row, (c, c)), 0.0), axis=1, keepdims=True)


def _col_to_row(col, eye):
    c = eye.shape[0]
    return jnp.sum(jnp.where(eye, jnp.broadcast_to(col, (c, c)), 0.0), axis=0, keepdims=True)


def _unit_lower_inverse(a, ri, ci):
    eye = jnp.where(ri == ci, 1.0, 0.0)
    a8 = jnp.where((ri >> 3) == (ci >> 3), a, 0.0)
    a2 = _dot(a8, a8, 1, 0)
    yield
    a4 = _dot(a2, a2, 1, 0)
    t = eye - a8
    t = t + _dot(t, a2, 1, 0)
    yield
    t = t + _dot(t, a4, 1, 0)
    yield
    for sh in (3, 4, 5):
        off = jnp.where(((ri >> (sh + 1)) == (ci >> (sh + 1))) & ((ri >> sh) != (ci >> sh)), a, 0.0)
        left = _dot(t, off, 1, 0)
        yield
        t = t - _dot(left, t, 1, 0)
        yield
    return t


def _round_robin(gens):
    live = list(gens)
    while live:
        nxt = []
        for g in live:
            try:
                next(g)
                nxt.append(g)
            except StopIteration:
                pass
        live = nxt


def _gdn_chunk_core(q, k, v, g_row, b_row, t_mat, ri, ci):
    eye = ri == ci
    g_col = _row_to_col(g_row, eye)
    b_col = _row_to_col(b_row, eye)
    causal = ri >= ci
    strict = ri > ci
    dec = jnp.where(causal, jnp.exp(jnp.where(causal, g_col - g_row, 0.0)), 0.0)
    gam = jnp.exp(g_col)
    g_last = g_row[:, A_CHUNK - 1:A_CHUNK]
    gam_last = jnp.exp(g_last)
    e = jnp.exp(g_last - g_col)
    kb = k * b_col
    bv = v * b_col
    kbg = kb * gam
    q16, k16, kb16 = q.astype(BF16), k.astype(BF16), kb.astype(BF16)
    kk = _dot(kb16, k16, 1, 1)
    p = _dot(q16, k16, 1, 1) * dec
    yield
    a_mat = jnp.where(strict, kk * dec, 0.0)
    if t_mat is None:
        t_mat = yield from _unit_lower_inverse(a_mat, ri, ci)
    t16 = t_mat.astype(BF16)
    u = _dot(t16, bv, 1, 0)
    w = _dot(t16, kbg, 1, 0)
    yield
    return dict(eye=eye, g_col=g_col, b_col=b_col, dec=dec, strict=strict, causal=causal, gam=gam,
                gam_last=gam_last, e=e, kb=kb, bv=bv, kbg=kbg, a_mat=a_mat, t_mat=t_mat, u=u, w=w, p=p,
                qg=q * gam, kd=k * e, q16=q16, k16=k16, kb16=kb16, t16=t16)


A_SEQ_BLK = 256
A_BLK_CHUNKS = A_SEQ_BLK // A_CHUNK


def _gdn_halo(proj_hm):
    bn, s, w = proj_hm.shape
    last = proj_hm.reshape(bn, s // A_SEQ_BLK, A_SEQ_BLK, w)[:, :, A_SEQ_BLK - HALO:, :]
    return jnp.concatenate([jnp.zeros((bn, 1, HALO, w), proj_hm.dtype), last[:, :-1]], axis=1)


def _gdn_window(x_ref, halo_ref, ci, first, lo):
    if first:
        return jnp.concatenate([halo_ref[:, lo:lo + A_CONV_COLS], x_ref[0:A_CHUNK, lo:lo + A_CONV_COLS]], axis=0)
    start = pl.multiple_of(ci * A_CHUNK - HALO, HALO)
    return x_ref[pl.ds(start, A_CHUNK + HALO), lo:lo + A_CONV_COLS]


def _gdn_chunk_prep(xw, cw, y=None):
    if y is None:
        y = _conv_taps(xw, cw)
    a, slope = _silu_and_slope(y)
    aq, ak, v = a[:, 0:A_DK], a[:, A_DK:2 * A_DK], a[:, 2 * A_DK:]
    rq = lax.rsqrt(jnp.sum(aq * aq, axis=1, keepdims=True) + EPS)
    rk = lax.rsqrt(jnp.sum(ak * ak, axis=1, keepdims=True) + EPS)
    return dict(xw=xw, y=y, slope=slope, aq=aq, ak=ak, rq=rq, rk=rk, q=aq * rq * (A_DK ** -0.5), k=ak * rk, v=v)


def _gdn_fwd(proj_hm, cw_hm, beta, gc, norm_g, hp=8):
    bn, s, _ = proj_hm.shape
    n = s // A_CHUNK
    nsb = s // A_SEQ_BLK
    halo = _gdn_halo(proj_hm)

    def body(x_ref, halo_ref, cw_ref, beta_ref, gc_ref, ng_ref, og_ref, oraw_ref, st_ref, t_ref, y_ref, state):
        first_chunk = pl.program_id(2) * A_BLK_CHUNKS
        ri = lax.broadcasted_iota(jnp.int32, (A_CHUNK, A_CHUNK), 0)
        ci_ = lax.broadcasted_iota(jnp.int32, (A_CHUNK, A_CHUNK), 1)
        ng = ng_ref[...]

        @pl.when(pl.program_id(2) == 0)
        def _():
            state[...] = jnp.zeros_like(state)

        def one_head(hh, ci, first, rows):
            lo = hh * A_HEAD_COLS
            cw = cw_ref[:, hh * A_CONV_COLS:(hh + 1) * A_CONV_COLS]
            cin = _gdn_chunk_prep(_gdn_window(x_ref, halo_ref, ci, first, lo), cw)
            y_ref[rows, hh * A_CONV_COLS:(hh + 1) * A_CONV_COLS] = cin["y"]
            seq_chunk = pl.ds(first_chunk + ci, 1)
            core = yield from _gdn_chunk_core(cin["q"], cin["k"], cin["v"], gc_ref[hh, seq_chunk, :],
                                              beta_ref[hh, seq_chunk, :], None, ri, ci_)
            st = state[hh]
            st_ref[hh, ci] = st
            t_ref[hh, ci] = core["t_mat"]
            st16 = st.astype(BF16)
            vn = core["u"] - _dot(core["w"], st16, 1, 0)
            qs = _dot(core["qg"], st16, 1, 0)
            yield
            vn16 = vn.astype(BF16)
            o = qs + _dot(core["p"], vn16, 1, 0)
            state[hh] = st * core["gam_last"] + _dot(core["kd"], vn16, 0, 0)
            yield
            ocols = slice(hh * A_DV, (hh + 1) * A_DV)
            oraw_ref[rows, ocols] = o
            r = lax.rsqrt(jnp.mean(o * o, axis=1, keepdims=True) + EPS)
            z = x_ref[rows, lo + A_CONV_COLS:lo + A_HEAD_COLS]
            og_ref[rows, ocols] = (o * r * ng * _silu(z)).astype(BF16)

        def chunk(ci, first):
            rows = pl.ds(0 if first else pl.multiple_of(ci * A_CHUNK, A_CHUNK), A_CHUNK)
            _round_robin([one_head(hh, ci, first, rows) for hh in range(hp)])

        chunk(0, True)
        lax.fori_loop(1, A_BLK_CHUNKS, lambda i, c: (chunk(i, False), c)[1], 0)

    small = pl.BlockSpec((None, hp, n, A_CHUNK), lambda b, h, j: (b, h, 0, 0))
    return pl.pallas_call(
        body, name="gdn_fwd", grid=(bn, A_HEADS // hp, nsb),
        in_specs=[pl.BlockSpec((None, A_SEQ_BLK, hp * A_HEAD_COLS), lambda b, h, j: (b, j, h)),
                  pl.BlockSpec((None, None, HALO, hp * A_HEAD_COLS), lambda b, h, j: (b, j, 0, h)),
                  pl.BlockSpec((A_CONV, hp * A_CONV_COLS), lambda b, h, j: (0, h)),
                  small, small,
                  pl.BlockSpec((1, A_DV), lambda b, h, j: (0, 0))],
        out_specs=[pl.BlockSpec((None, A_SEQ_BLK, hp * A_DV), lambda b, h, j: (b, j, h)),
                   pl.BlockSpec((None, A_SEQ_BLK, hp * A_DV), lambda b, h, j: (b, j, h)),
                   pl.BlockSpec((None, hp, A_BLK_CHUNKS, A_DK, A_DV), lambda b, h, j: (b, h, j, 0, 0)),
                   pl.BlockSpec((None, hp, A_BLK_CHUNKS, A_CHUNK, A_CHUNK), lambda b, h, j: (b, h, j, 0, 0)),
                   pl.BlockSpec((None, A_SEQ_BLK, hp * A_CONV_COLS), lambda b, h, j: (b, j, h))],
        out_shape=[jax.ShapeDtypeStruct((bn, s, A_VW), BF16),
                   jax.ShapeDtypeStruct((bn, s, A_VW), F32),
                   jax.ShapeDtypeStruct((bn, A_HEADS, n, A_DK, A_DV), F32),
                   jax.ShapeDtypeStruct((bn, A_HEADS, n, A_CHUNK, A_CHUNK), F32),
                   jax.ShapeDtypeStruct((bn, s, A_HEADS * A_CONV_COLS), F32)],
        scratch_shapes=[pltpu.VMEM((hp, A_DK, A_DV), F32)],
        compiler_params=_params(("parallel", "parallel", "arbitrary"), VMEM_BIG),
    )(proj_hm, halo, cw_hm, beta, gc, norm_g)


def _gdn_bwd(proj_hm, cw_hm, beta, gc, norm_g, oraw, states, t_mats, conv_y, dog, hp=4):
    bn, s, _ = proj_hm.shape
    n = s // A_CHUNK
    nsb = s // A_SEQ_BLK
    halo = _gdn_halo(proj_hm)

    def body(x_ref, halo_ref, cw_ref, beta_ref, gc_ref, ng_ref, oraw_ref, st_ref, t_ref, y_ref, dog_ref,
             dx_ref, dgc_ref, dbeta_ref, dcw_ref, dng_ref, dstate, dy_next, shifted):
        first_chunk = (nsb - 1 - pl.program_id(2)) * A_BLK_CHUNKS
        ri = lax.broadcasted_iota(jnp.int32, (A_CHUNK, A_CHUNK), 0)
        ci_ = lax.broadcasted_iota(jnp.int32, (A_CHUNK, A_CHUNK), 1)
        lane = lax.broadcasted_iota(jnp.int32, (1, A_CHUNK), 1)
        ng = ng_ref[...]

        @pl.when(pl.program_id(2) == 0)
        def _():
            dstate[...] = jnp.zeros_like(dstate)
            dy_next[...] = jnp.zeros_like(dy_next)
            dcw_ref[...] = jnp.zeros_like(dcw_ref)
            dng_ref[...] = jnp.zeros_like(dng_ref)

        def one_head(hh, ci, first, rows):
            lo = hh * A_HEAD_COLS
            ccols = slice(hh * A_CONV_COLS, (hh + 1) * A_CONV_COLS)
            ocols = slice(hh * A_DV, (hh + 1) * A_DV)
            cw = cw_ref[:, ccols]
            cin = _gdn_chunk_prep(_gdn_window(x_ref, halo_ref, ci, first, lo), cw, y_ref[rows, ccols])
            q, k, v = cin["q"], cin["k"], cin["v"]
            seq_chunk = pl.ds(first_chunk + ci, 1)
            cr = yield from _gdn_chunk_core(q, k, v, gc_ref[hh, seq_chunk, :], beta_ref[hh, seq_chunk, :],
                                            t_ref[hh, ci], ri, ci_)
            eye, dec, gam, e = cr["eye"], cr["dec"], cr["gam"], cr["e"]
            b_col, t_mat, u, w, p = cr["b_col"], cr["t_mat"], cr["u"], cr["w"], cr["p"]
            st = st_ref[hh, ci]
            ds_out = dstate[hh]

            o = oraw_ref[rows, ocols]
            z = x_ref[rows, lo + A_CONV_COLS:lo + A_HEAD_COLS]
            d_og = dog_ref[rows, ocols].astype(F32)
            r = lax.rsqrt(jnp.mean(o * o, axis=1, keepdims=True) + EPS)
            oh = o * r
            gate, gate_slope = _silu_and_slope(z)
            d_on = d_og * gate
            dz = d_og * oh * ng * gate_slope
            dng_ref[hh, 0:1, :] += jnp.sum(d_on * oh, axis=0, keepdims=True)
            d_oh = d_on * ng
            d_o = r * (d_oh - oh * jnp.mean(d_oh * oh, axis=1, keepdims=True))

            st16, ds16, do16, w16 = st.astype(BF16), ds_out.astype(BF16), d_o.astype(BF16), w.astype(BF16)
            q16, k16, t16 = cr["q16"], cr["k16"], cr["t16"]
            vn = u - _dot(w16, st16, 1, 0)
            d_vn = _dot(p, do16, 0, 0) + _dot(cr["kd"], ds16, 1, 0)
            d_qg = _dot(do16, st16, 1, 1)
            qgdo = _dot(cr["qg"], do16, 0, 0)
            yield
            vn16, dvn16 = vn.astype(BF16), d_vn.astype(BF16)
            d_p = jnp.where(cr["causal"], _dot(do16, vn16, 1, 1), 0.0)
            d_kd = _dot(vn16, ds16, 1, 1)
            d_gam_last = jnp.sum(jnp.sum(st * ds_out, axis=1, keepdims=True), axis=0, keepdims=True)
            d_w = -_dot(dvn16, st16, 1, 1)
            dstate[hh] = qgdo + ds_out * cr["gam_last"] - _dot(w16, dvn16, 0, 0)
            d_bv = _dot(t16, dvn16, 0, 0)
            yield
            d_kbg = _dot(t16, d_w, 0, 0)
            n_p = (d_p * dec).astype(BF16)
            d_q = _dot(n_p, k16, 1, 0) + d_qg * gam
            npq = _dot(n_p, q16, 0, 0)
            yield
            d_a = jnp.where(cr["strict"], -(_dot(d_bv, u, 1, 1) + _dot(d_kbg, w16, 1, 1)), 0.0)
            yield
            m_a = (d_a * dec).astype(BF16)
            d_kb = _dot(m_a, k16, 1, 0) + d_kbg * gam
            d_k = (_dot(m_a, cr["kb16"], 0, 0) + npq + d_kd * e + d_kb * b_col)
            yield
            d_v = d_bv * b_col
            d_beta_col = (jnp.sum(d_bv * v, axis=1, keepdims=True)
                          + jnp.sum(d_kb * k, axis=1, keepdims=True))
            gterm = d_a * cr["a_mat"] + d_p * p
            d_e = jnp.sum(d_kd * k, axis=1, keepdims=True) * e
            d_g_col = (jnp.sum(gterm, axis=1, keepdims=True)
                       + (jnp.sum(d_qg * q, axis=1, keepdims=True)
                          + jnp.sum(d_kbg * cr["kb"], axis=1, keepdims=True)) * gam
                       - d_e)
            d_g_last = jnp.sum(d_e, axis=0, keepdims=True) + d_gam_last * cr["gam_last"]
            d_g_row = (_col_to_row(d_g_col, eye) - jnp.sum(gterm, axis=0, keepdims=True)
                       + jnp.where(lane == A_CHUNK - 1, d_g_last, 0.0))
            dgc_ref[hh, seq_chunk, :] = d_g_row
            dbeta_ref[hh, seq_chunk, :] = _col_to_row(d_beta_col, eye)

            qh = cin["aq"] * cin["rq"]
            kh = cin["ak"] * cin["rk"]
            d_qh = d_q * (A_DK ** -0.5)
            d_aq = cin["rq"] * (d_qh - qh * jnp.sum(d_qh * qh, axis=1, keepdims=True))
            d_ak = cin["rk"] * (d_k - kh * jnp.sum(d_k * kh, axis=1, keepdims=True))
            d_y = jnp.concatenate([d_aq, d_ak, d_v], axis=1) * cin["slope"]
            shifted[hh, 0, 0:A_CHUNK, :] = d_y
            shifted[hh, 0, A_CHUNK:A_CHUNK + HALO, :] = dy_next[hh]
            shifted[hh, 1, 0:A_CHUNK + HALO, :] = cin["xw"]
            d_x = cw[A_CONV - 1:A_CONV, :] * d_y
            for j in range(1, A_CONV):
                d_x = d_x + cw[A_CONV - 1 - j:A_CONV - j, :] * shifted[hh, 0, j:j + A_CHUNK, :]
            for j in range(A_CONV):
                xs = shifted[hh, 1, HALO - j:HALO - j + A_CHUNK, :]
                dcw_ref[A_CONV - 1 - j:A_CONV - j, ccols] += jnp.sum(d_y * xs, axis=0, keepdims=True)
            dy_next[hh] = d_y[0:HALO, :]
            dx_ref[rows, lo:lo + A_CONV_COLS] = d_x.astype(BF16)
            dx_ref[rows, lo + A_CONV_COLS:lo + A_HEAD_COLS] = dz.astype(BF16)

        def chunk(ci, first):
            rows = pl.ds(0 if first else pl.multiple_of(ci * A_CHUNK, A_CHUNK), A_CHUNK)
            _round_robin([one_head(hh, ci, first, rows) for hh in range(hp)])

        lax.fori_loop(0, A_BLK_CHUNKS - 1, lambda i, c: (chunk(A_BLK_CHUNKS - 1 - i, False), c)[1], 0)
        chunk(0, True)

    rev = lambda j: nsb - 1 - j
    small = pl.BlockSpec((None, hp, n, A_CHUNK), lambda b, h, j: (b, h, 0, 0))
    wide = pl.BlockSpec((None, A_SEQ_BLK, hp * A_HEAD_COLS), lambda b, h, j: (b, rev(j), h))
    val = pl.BlockSpec((None, A_SEQ_BLK, hp * A_DV), lambda b, h, j: (b, rev(j), h))
    return pl.pallas_call(
        body, name="gdn_bwd", grid=(bn, A_HEADS // hp, nsb),
        in_specs=[wide,
                  pl.BlockSpec((None, None, HALO, hp * A_HEAD_COLS), lambda b, h, j: (b, rev(j), 0, h)),
                  pl.BlockSpec((A_CONV, hp * A_CONV_COLS), lambda b, h, j: (0, h)),
                  small, small,
                  pl.BlockSpec((1, A_DV), lambda b, h, j: (0, 0)),
                  val,
                  pl.BlockSpec((None, hp, A_BLK_CHUNKS, A_DK, A_DV), lambda b, h, j: (b, h, rev(j), 0, 0)),
                  pl.BlockSpec((None, hp, A_BLK_CHUNKS, A_CHUNK, A_CHUNK), lambda b, h, j: (b, h, rev(j), 0, 0)),
                  pl.BlockSpec((None, A_SEQ_BLK, hp * A_CONV_COLS), lambda b, h, j: (b, rev(j), h)),
                  val],
        out_specs=[wide, small, small,
                   pl.BlockSpec((None, A_CONV, hp * A_CONV_COLS), lambda b, h, j: (b, 0, h)),
                   pl.BlockSpec((None, hp, 8, A_DV), lambda b, h, j: (b, h, 0, 0))],
        out_shape=[jax.ShapeDtypeStruct((bn, s, A_HEADS * A_HEAD_COLS), BF16),
                   jax.ShapeDtypeStruct((bn, A_HEADS, n, A_CHUNK), F32),
                   jax.ShapeDtypeStruct((bn, A_HEADS, n, A_CHUNK), F32),
                   jax.ShapeDtypeStruct((bn, A_CONV, A_HEADS * A_CONV_COLS), F32),
                   jax.ShapeDtypeStruct((bn, A_HEADS, 8, A_DV), F32)],
        scratch_shapes=[pltpu.VMEM((hp, A_DK, A_DV), F32), pltpu.VMEM((hp, HALO, A_CONV_COLS), F32),
                        pltpu.VMEM((hp, 2, A_CHUNK + 2 * HALO, A_CONV_COLS), F32)],
        compiler_params=_params(("parallel", "parallel", "arbitrary"), VMEM_BIG),
    )(proj_hm, halo, cw_hm, beta, gc, norm_g, oraw, states, t_mats, conv_y, dog)


def _rope_tables(posf, inv_freq_row):
    t = posf.shape[0]
    tm = 512

    def body(p_ref, f_ref, c_ref, sa_ref, sb_ref):
        ang = p_ref[...] * f_ref[...]
        lane = lax.broadcasted_iota(jnp.int32, ang.shape, 1)
        half = ROPE_DIMS // 2
        c_ref[...] = jnp.where(lane < ROPE_DIMS, jnp.cos(ang), 1.0)
        sn = jnp.sin(ang)
        sa_ref[...] = jnp.where(lane < half, -sn, 0.0)
        sb_ref[...] = jnp.where((lane >= half) & (lane < ROPE_DIMS), sn, 0.0)

    row = pl.BlockSpec((tm, 128), lambda i: (i, 0))
    return pl.pallas_call(
        body, name="rope_tables", grid=(t // tm,),
        in_specs=[row, pl.BlockSpec((1, 128), lambda i: (0, 0))], out_specs=[row] * 3,
        out_shape=[jax.ShapeDtypeStruct((t, 128), F32)] * 3,
        compiler_params=_params(("parallel",)),
    )(posf, inv_freq_row)


def _qk_prep(proj, c, sa, sb, qg, kg, name, tm=256):
    t = proj.shape[0]

    def body(x_ref, c_ref, sa_ref, sb_ref, qg_ref, kg_ref, o_ref):
        cc, s1, s2 = c_ref[...], sa_ref[...], sb_ref[...]
        half = ROPE_DIMS // 2

        def one_head(lo, g):
            xv = x_ref[:, lo:lo + B_DH].astype(F32)
            ms = jnp.mean(xv * xv, axis=1, keepdims=True)
            yield
            xn = xv * lax.rsqrt(ms + EPS) * g
            r1, r2 = pltpu.roll(xn, 128 - half, 1), pltpu.roll(xn, half, 1)
            yield
            o_ref[:, lo:lo + B_DH] = (xn * cc + r1 * s1 + r2 * s2).astype(BF16)

        for which, g_ref in ((0, qg_ref), (1, kg_ref)):
            g = g_ref[...]
            _round_robin([one_head(which * B_W + h * B_DH, g) for h in range(B_HEADS)])
        o_ref[:, 2 * B_W:3 * B_W] = x_ref[:, 2 * B_W:3 * B_W]

    tab = pl.BlockSpec((tm, 128), lambda i: (i, 0))
    gain = pl.BlockSpec((1, B_DH), lambda i: (0, 0))
    return pl.pallas_call(
        body, name=name, grid=(t // tm,),
        in_specs=[pl.BlockSpec((tm, 3 * B_W), lambda i: (i, 0)), tab, tab, tab, gain, gain],
        out_specs=pl.BlockSpec((tm, 3 * B_W), lambda i: (i, 0)),
        out_shape=jax.ShapeDtypeStruct((t, 3 * B_W), BF16),
        compiler_params=_params(("parallel",), 40 * 1024 * 1024),
    )(proj, c, sa, sb, qg, kg)


def _qk_prep_bwd(proj, c, sa, sb, qg, kg, dq, dk, dv, dz, name, tm=256):
    t = proj.shape[0]
    out_w = 3 * B_W + (B_W if dz is not None else 0)

    def body(*refs):
        x_ref, c_ref, sa_ref, sb_ref, qg_ref, kg_ref, dq_ref, dk_ref, dv_ref = refs[:9]
        if dz is not None:
            dz_ref, o_ref, dgain_ref = refs[9:]
        else:
            o_ref, dgain_ref = refs[9:]
        i = pl.program_id(0)

        @pl.when(i == 0)
        def _():
            dgain_ref[...] = jnp.zeros_like(dgain_ref)

        cc, s1, s2 = c_ref[...], sa_ref[...], sb_ref[...]
        half = ROPE_DIMS // 2

        def one_head(which, h, g, d_ref, parts):
            lo = which * B_W + h * B_DH
            xv = x_ref[:, lo:lo + B_DH].astype(F32)
            d_out = d_ref[:, h * B_DH:(h + 1) * B_DH].astype(F32)
            ms = jnp.mean(xv * xv, axis=1, keepdims=True)
            r1, r2 = pltpu.roll(d_out * s1, half, 1), pltpu.roll(d_out * s2, 128 - half, 1)
            yield
            r = lax.rsqrt(ms + EPS)
            xh = xv * r
            d_xn = d_out * cc + r1 + r2
            parts.append(jnp.sum(d_xn * xh, axis=0, keepdims=True))
            d_xh = d_xn * g
            dot = jnp.mean(d_xh * xh, axis=1, keepdims=True)
            yield
            o_ref[:, lo:lo + B_DH] = (r * (d_xh - xh * dot)).astype(BF16)

        for which, g_ref, d_ref in ((0, qg_ref, dq_ref), (1, kg_ref, dk_ref)):
            parts = []
            _round_robin([one_head(which, h, g_ref[...], d_ref, parts) for h in range(B_HEADS)])
            acc = parts[0]
            for part in parts[1:]:
                acc = acc + part
            dgain_ref[which:which + 1, :] += acc
        o_ref[:, 2 * B_W:3 * B_W] = dv_ref[...]
        if dz is not None:
            o_ref[:, 3 * B_W:4 * B_W] = dz_ref[...]

    tab = pl.BlockSpec((tm, 128), lambda i: (i, 0))
    gain = pl.BlockSpec((1, B_DH), lambda i: (0, 0))
    grad = pl.BlockSpec((tm, B_W), lambda i: (i, 0))
    in_specs = [pl.BlockSpec((tm, 2 * B_W), lambda i: (i, 0)), tab, tab, tab, gain, gain, grad, grad, grad]
    args = [proj, c, sa, sb, qg, kg, dq, dk, dv]
    if dz is not None:
        in_specs.append(grad)
        args.append(dz)
    return pl.pallas_call(
        body, name=name, grid=(t // tm,), in_specs=in_specs,
        out_specs=[pl.BlockSpec((tm, out_w), lambda i: (i, 0)), pl.BlockSpec((8, B_DH), lambda i: (0, 0))],
        out_shape=[jax.ShapeDtypeStruct((t, out_w), BF16), jax.ShapeDtypeStruct((8, B_DH), F32)],
        compiler_params=_params(("arbitrary",), 40 * 1024 * 1024),
    )(*args)


def _attn_masks():
    qi = lax.broadcasted_iota(jnp.int32, (B_BLK, 2 * B_BLK), 0)
    kj = lax.broadcasted_iota(jnp.int32, (B_BLK, 2 * B_BLK), 1)
    two = (kj >= qi) & (kj <= qi + B_BLK)
    q1 = lax.broadcasted_iota(jnp.int32, (B_BLK, B_BLK), 0)
    k1 = lax.broadcasted_iota(jnp.int32, (B_BLK, B_BLK), 1)
    return k1 <= q1, two


def _lane_pick(ref_rows, h):
    lane = lax.broadcasted_iota(jnp.int32, ref_rows.shape, 1)
    return jnp.sum(jnp.where(lane == h, ref_rows, 0.0), axis=1, keepdims=True)


B_ROWS = 2048


def _attn_schedule(nb, sb, block):
    way = 16

    def run(items):
        for at in range(0, len(items), way):
            _round_robin([block(*it) for it in items[at:at + way]])

    run([(si, 0, True) for si in range(sb)])
    if nb == 1:
        return
    per = max(1, way // sb)
    lead = 1 + (nb - 1) % per
    if lead > 1:
        run([(si, i, False) for i in range(1, lead) for si in range(sb)])

    def step(it, carry):
        run([(si, lead + it * per + u, False) for u in range(per) for si in range(sb)])
        return carry

    lax.fori_loop(0, (nb - lead) // per, step, 0)


def _attn_rows(i, first):
    if first:
        return pl.ds(0, B_BLK), pl.ds(0, B_BLK)
    rows = pl.ds(pl.multiple_of(i * B_BLK, B_BLK), B_BLK)
    return rows, pl.ds(pl.multiple_of((i - 1) * B_BLK, B_BLK), 2 * B_BLK)


def _attn_fwd(qkv, name):
    ns, ln, _ = qkv.shape
    nb = ln // B_BLK
    sb = B_ROWS // ln
    scale = B_DH ** -0.5

    def body(q_ref, k_ref, v_ref, o_ref, lse_ref):
        h = pl.program_id(1)
        mask1, mask2 = _attn_masks()
        lane = lax.broadcasted_iota(jnp.int32, (B_BLK, B_HEADS), 1)

        @pl.when(h == 0)
        def _():
            lse_ref[...] = jnp.zeros_like(lse_ref)

        def block(si, i, first):
            rows, win = _attn_rows(i, first)
            mask = mask1 if first else mask2
            sc = jnp.where(mask, _dot(q_ref[si, rows, :], k_ref[si, win, :], 1, 1) * scale, -1e30)
            yield
            m = jnp.max(sc, axis=1, keepdims=True)
            p = jnp.exp(sc - m)
            l = jnp.sum(p, axis=1, keepdims=True)
            pv = _dot(p, v_ref[si, win, :], 1, 0)
            yield
            o_ref[si, rows, :] = (pv / l).astype(BF16)
            lse_ref[si, rows, :] = jnp.where(lane == h, m + jnp.log(l), lse_ref[si, rows, :])

        _attn_schedule(nb, sb, block)

    head = lambda off: pl.BlockSpec((sb, ln, B_DH), lambda s, h: (s, 0, off + h))
    return pl.pallas_call(
        body, name=name, grid=(ns // sb, B_HEADS),
        in_specs=[head(0), head(B_HEADS), head(2 * B_HEADS)],
        out_specs=[head(0), pl.BlockSpec((sb, ln, B_HEADS), lambda s, h: (s, 0, 0))],
        out_shape=[jax.ShapeDtypeStruct((ns, ln, B_W), BF16), jax.ShapeDtypeStruct((ns, ln, B_HEADS), F32)],
        compiler_params=_params(("parallel", "arbitrary")),
    )(qkv, qkv, qkv)


def _attn_bwd(qkv, d_o, lse_joint, delta, name):
    ns, ln, _ = qkv.shape
    nb = ln // B_BLK
    sb = B_ROWS // ln
    scale = B_DH ** -0.5

    def body(q_ref, k_ref, v_ref, do_ref, lj_ref, dl_ref, dq_ref, dk_out, dv_out, dk_ref, dv_ref):
        h = pl.program_id(1)
        mask1, mask2 = _attn_masks()
        dk_ref[...] = jnp.zeros_like(dk_ref)
        dv_ref[...] = jnp.zeros_like(dv_ref)

        def block(si, i, first):
            rows, win = _attn_rows(i, first)
            mask = mask1 if first else mask2
            q = q_ref[si, rows, :]
            d_out = do_ref[si, rows, :]
            l_col = _lane_pick(lj_ref[si, rows, :], h)
            d_col = _lane_pick(dl_ref[si, rows, :], h)
            sc = _dot(q, k_ref[si, win, :], 1, 1) * scale
            d_p = _dot(d_out, v_ref[si, win, :], 1, 1)
            yield
            p = jnp.exp(jnp.where(mask, sc - l_col, -1e30))
            d_s = p * (d_p - d_col) * scale
            d_q = _dot(d_s, k_ref[si, win, :], 1, 0)
            d_k = _dot(d_s, q, 0, 0)
            d_v = _dot(p, d_out, 0, 0)
            yield
            dq_ref[si, rows, :] = d_q.astype(BF16)
            dk_ref[si, win, :] += d_k
            dv_ref[si, win, :] += d_v

        _attn_schedule(nb, sb, block)
        dk_out[...] = dk_ref[...].astype(BF16)
        dv_out[...] = dv_ref[...].astype(BF16)

    head = lambda off: pl.BlockSpec((sb, ln, B_DH), lambda s, h: (s, 0, off + h))
    small = pl.BlockSpec((sb, ln, B_HEADS), lambda s, h: (s, 0, 0))
    return pl.pallas_call(
        body, name=name, grid=(ns // sb, B_HEADS),
        in_specs=[head(0), head(B_HEADS), head(2 * B_HEADS), head(0), small, small],
        out_specs=[head(0)] * 3,
        out_shape=[jax.ShapeDtypeStruct((ns, ln, B_W), BF16)] * 3,
        scratch_shapes=[pltpu.VMEM((sb, ln, B_DH), F32)] * 2,
        compiler_params=_params(("parallel", "parallel")),
    )(qkv, qkv, qkv, d_o, lse_joint, delta)


def _merge_weights(lse_refs):
    ls = [r[...] for r in lse_refs]
    m = jnp.maximum(jnp.maximum(ls[0], ls[1]), ls[2])
    es = [jnp.exp(l - m) for l in ls]
    tot = es[0] + es[1] + es[2]
    return [e / tot for e in es], m + jnp.log(tot)


def _merge_fwd(outs, lses, proj0, tm=256):
    t = outs[0].shape[0]

    def body(o0, o1, o2, l0, l1, l2, z_ref, og_ref):
        wts, _ = _merge_weights((l0, l1, l2))

        def one_head(h):
            cols = slice(h * B_DH, (h + 1) * B_DH)
            w0, w1, w2 = (jnp.broadcast_to(w[:, h:h + 1], (tm, B_DH)) for w in wts)
            yield
            o = w0 * o0[:, cols] + w1 * o1[:, cols] + w2 * o2[:, cols]
            og_ref[:, cols] = (o * _silu(z_ref[:, cols].astype(F32))).astype(BF16)

        _round_robin([one_head(h) for h in range(B_HEADS)])

    wide = pl.BlockSpec((tm, B_W), lambda i: (i, 0))
    small = pl.BlockSpec((tm, B_HEADS), lambda i: (i, 0))
    return pl.pallas_call(
        body, name="merge_fwd", grid=(t // tm,),
        in_specs=[wide] * 3 + [small] * 3 + [pl.BlockSpec((tm, B_W), lambda i: (i, 3))],
        out_specs=wide, out_shape=jax.ShapeDtypeStruct((t, B_W), BF16),
        compiler_params=_params(("parallel",)),
    )(*outs, *lses, proj0)


def _merge_bwd(outs, lses, proj0, d_og, tm=256):
    t = outs[0].shape[0]

    def body(o0, o1, o2, l0, l1, l2, z_ref, dog_ref, do_ref, lj_ref, dl_ref, dz_ref):
        wts, lj = _merge_weights((l0, l1, l2))
        lj_ref[...] = lj
        lane = lax.broadcasted_iota(jnp.int32, (tm, B_HEADS), 1)
        sums = [None] * B_HEADS

        def one_head(h):
            cols = slice(h * B_DH, (h + 1) * B_DH)
            w0, w1, w2 = (jnp.broadcast_to(w[:, h:h + 1], (tm, B_DH)) for w in wts)
            yield
            o = w0 * o0[:, cols] + w1 * o1[:, cols] + w2 * o2[:, cols]
            z = z_ref[:, cols].astype(F32)
            d_g = dog_ref[:, cols].astype(F32)
            gate, gate_slope = _silu_and_slope(z)
            d_out = d_g * gate
            dz_ref[:, cols] = (d_g * o * gate_slope).astype(BF16)
            do_ref[:, cols] = d_out.astype(BF16)
            sums[h] = jnp.sum(d_out * o, axis=1, keepdims=True)
            yield

        _round_robin([one_head(h) for h in range(B_HEADS)])
        delta = jnp.zeros((tm, B_HEADS), F32)
        for h in range(B_HEADS):
            delta = jnp.where(lane == h, sums[h], delta)
        dl_ref[...] = delta

    wide = pl.BlockSpec((tm, B_W), lambda i: (i, 0))
    small = pl.BlockSpec((tm, B_HEADS), lambda i: (i, 0))
    return pl.pallas_call(
        body, name="merge_bwd", grid=(t // tm,),
        in_specs=[wide] * 3 + [small] * 3 + [pl.BlockSpec((tm, B_W), lambda i: (i, 3)), wide],
        out_specs=[wide, small, small, wide],
        out_shape=[jax.ShapeDtypeStruct((t, B_W), BF16), jax.ShapeDtypeStruct((t, B_HEADS), F32),
                   jax.ShapeDtypeStruct((t, B_HEADS), F32), jax.ShapeDtypeStruct((t, B_W), BF16)],
        compiler_params=_params(("parallel",)),
    )(*outs, *lses, proj0, d_og)


def _adamw(w, g, m, v, name):
    r, c = w.shape
    tr = r
    for cand in (256, 128, 64, 32, 16, 8):
        if r % cand == 0:
            tr = cand
            break

    def body(w_ref, g_ref, m_ref, v_ref, d_ref, nm_ref, nv_ref):
        gv = g_ref[...]
        nm = ADAM_B1 * m_ref[...] + (1.0 - ADAM_B1) * gv
        nv = ADAM_B2 * v_ref[...] + (1.0 - ADAM_B2) * (gv * gv)
        m_hat = nm / (1.0 - ADAM_B1 ** ADAM_STEP)
        v_hat = nv / (1.0 - ADAM_B2 ** ADAM_STEP)
        d_ref[...] = -ADAM_LR * (m_hat / (jnp.sqrt(v_hat) + ADAM_EPS) + ADAM_WD * w_ref[...])
        nm_ref[...] = nm
        nv_ref[...] = nv

    blk = pl.BlockSpec((tr, c), lambda i: (i, 0))
    return pl.pallas_call(
        body, name=name, grid=(r // tr,), in_specs=[blk] * 4, out_specs=[blk] * 3,
        out_shape=[jax.ShapeDtypeStruct((r, c), F32)] * 3,
        compiler_params=_params(("parallel",)),
    )(w, g, m, v)


def _adam_update(w, gv, m, v):
    nm = ADAM_B1 * m + (1.0 - ADAM_B1) * gv
    nv = ADAM_B2 * v + (1.0 - ADAM_B2) * (gv * gv)
    m_hat = nm / (1.0 - ADAM_B1 ** ADAM_STEP)
    v_hat = nv / (1.0 - ADAM_B2 ** ADAM_STEP)
    return -ADAM_LR * (m_hat / (jnp.sqrt(v_hat) + ADAM_EPS) + ADAM_WD * w), nm, nv


def _adamw_shard(w, mine, theirs, m, v, half_index, name, tr=128):
    _, r, c = w.shape
    nhb = (r // 2) // tr

    def body(c_ref, w_ref, mine_ref, theirs_ref, m_ref, v_ref, g_ref, d_ref, nm_ref, nv_ref):
        is_mine = (pl.program_id(0) // nhb) == c_ref[0]
        gv = jnp.where(is_mine, mine_ref[...], theirs_ref[...])
        d, nm, nv = _adam_update(w_ref[...], gv, m_ref[...], v_ref[...])
        g_ref[...] = gv
        d_ref[...] = d
        nm_ref[...] = nm
        nv_ref[...] = nv

    full = pl.BlockSpec((None, tr, c), lambda i, cc: (0, i, 0))
    half = pl.BlockSpec((tr, c), lambda i, cc: (i % nhb, 0))
    return pl.pallas_call(
        body, name=name,
        grid_spec=pltpu.PrefetchScalarGridSpec(
            num_scalar_prefetch=1, grid=(2 * nhb,),
            in_specs=[full, half, half, full, full], out_specs=[full] * 4),
        out_shape=[jax.ShapeDtypeStruct(w.shape, F32)] * 4,
        compiler_params=_params(("parallel",), 40 * 1024 * 1024),
    )(half_index, w, mine, theirs, m, v)


def _adamw_shard_cols(w, mine, theirs, m, v, half_index, name, steps=20):
    c, _, r = w.shape
    tc = c // steps
    assert tc * steps == c

    def body(c_ref, w_ref, mine_ref, theirs_ref, m_ref, v_ref, g_ref, d_ref, nm_ref, nv_ref):
        first = jnp.where(c_ref[0] == 0, mine_ref[...], theirs_ref[...])
        second = jnp.where(c_ref[0] == 0, theirs_ref[...], mine_ref[...])
        for lo, gv in ((0, first), (r // 2, second)):
            cols = slice(lo, lo + r // 2)
            d, nm, nv = _adam_update(w_ref[:, :, cols], gv, m_ref[:, :, cols], v_ref[:, :, cols])
            g_ref[:, :, cols] = gv
            d_ref[:, :, cols] = d
            nm_ref[:, :, cols] = nm
            nv_ref[:, :, cols] = nv

    full = pl.BlockSpec((tc, 1, r), lambda i, cc: (i, 0, 0))
    half = pl.BlockSpec((tc, 1, r // 2), lambda i, cc: (i, 0, 0))
    return pl.pallas_call(
        body, name=name,
        grid_spec=pltpu.PrefetchScalarGridSpec(
            num_scalar_prefetch=1, grid=(steps,),
            in_specs=[full, half, half, full, full], out_specs=[full] * 4),
        out_shape=[jax.ShapeDtypeStruct(w.shape, F32)] * 4,
        compiler_params=_params(("parallel",), 40 * 1024 * 1024),
    )(half_index, w, mine, theirs, m, v)


def _pair_sum(own, other, half_index, name, tr=256):
    _, r, c = own.shape
    rh = r // 2
    tr = min(tr, rh)
    nrb = rh // tr

    def body(c_ref, own_ref, oth_ref, out_ref):
        out_ref[...] = (own_ref[...] + oth_ref[...].astype(F32)).astype(BF16)

    return pl.pallas_call(
        body, name=name,
        grid_spec=pltpu.PrefetchScalarGridSpec(
            num_scalar_prefetch=1, grid=(N_CHIPS, nrb),
            in_specs=[pl.BlockSpec((None, tr, c), lambda k, i, cc: (k, cc[0] * nrb + i, 0)),
                      pl.BlockSpec((None, tr, c), lambda k, i, cc: (k, i, 0))],
            out_specs=pl.BlockSpec((None, tr, c), lambda k, i, cc: (k, i, 0))),
        out_shape=jax.ShapeDtypeStruct((N_CHIPS, rh, c), BF16),
        compiler_params=_params(("parallel", "parallel")),
    )(half_index, own, other)


def _chip_sum(sums, others, chip_index, name, tr=256):
    _, r, c = sums.shape
    tr = min(tr, r)

    def body(k_ref, own_ref, oth_ref, out_ref):
        acc = own_ref[...].astype(F32)
        for j in range(N_CHIPS - 1):
            acc = acc + oth_ref[j].astype(F32)
        out_ref[...] = acc

    return pl.pallas_call(
        body, name=name,
        grid_spec=pltpu.PrefetchScalarGridSpec(
            num_scalar_prefetch=1, grid=(r // tr,),
            in_specs=[pl.BlockSpec((None, tr, c), lambda i, kk: (kk[0], i, 0)),
                      pl.BlockSpec((N_CHIPS - 1, tr, c), lambda i, kk: (0, i, 0))],
            out_specs=pl.BlockSpec((tr, c), lambda i, kk: (i, 0))),
        out_shape=jax.ShapeDtypeStruct((r, c), F32),
        compiler_params=_params(("parallel",)),
    )(chip_index, sums, others)


HBM = pl.BlockSpec(memory_space=pltpu.HBM)


def _place():
    x, y, c = lax.axis_index("x"), lax.axis_index("y"), lax.axis_index("c")
    chips = [(1 - x, y), (x, 1 - y), (1 - x, 1 - y)]
    return x, y, c, chips


def _sibling_forward(land):
    def body(in_ref, out_ref, send, recv):
        x, y, c, chips = _place()
        rh = out_ref.shape[1] // 2
        cps = []
        for j, (px, py) in enumerate(chips):
            slot = out_ref.at[2 * px + py, pl.ds(c * rh, rh)]
            cp = pltpu.make_async_remote_copy(
                src_ref=slot, dst_ref=slot, send_sem=send.at[j], recv_sem=recv.at[j],
                device_id=(x, y, 1 - c), device_id_type=MESH)
            cp.start()
            cps.append(cp)
        for j, (px, py) in enumerate(chips):
            slot = out_ref.at[2 * px + py, pl.ds((1 - c) * rh, rh)]
            pltpu.make_async_remote_copy(
                src_ref=slot, dst_ref=slot, send_sem=send.at[j], recv_sem=recv.at[j],
                device_id=(x, y, 1 - c), device_id_type=MESH).wait_recv()
        for cp in cps:
            cp.wait_send()

    return pl.pallas_call(
        body, name="first_weights_sibling_forward", in_specs=[HBM], out_specs=HBM,
        out_shape=jax.ShapeDtypeStruct(land.shape, land.dtype), input_output_aliases={0: 0},
        scratch_shapes=[pltpu.SemaphoreType.DMA((3,)), pltpu.SemaphoreType.DMA((3,))],
    )(land)


def _sibling_swap_halves(grads, name):
    na = len(grads)

    def body(*refs):
        ins, outs = refs[:na], refs[na:2 * na]
        send, recv = refs[2 * na:]
        x, y, c, _ = _place()
        sib = (x, y, 1 - c)
        cps = []
        for i in range(na):
            rh = ins[i].shape[1] // 2
            cp = pltpu.make_async_remote_copy(
                src_ref=ins[i].at[:, pl.ds((1 - c) * rh, rh), :], dst_ref=outs[i],
                send_sem=send.at[i], recv_sem=recv.at[i], device_id=sib, device_id_type=MESH)
            cp.start()
            cps.append(cp)
        for cp in cps:
            cp.wait()

    out_shape = [jax.ShapeDtypeStruct((g.shape[0], g.shape[1] // 2, g.shape[2]), g.dtype) for g in grads]
    return pl.pallas_call(
        body, name=name, in_specs=[HBM] * na, out_specs=[HBM] * na, out_shape=out_shape,
        scratch_shapes=[pltpu.SemaphoreType.DMA((na,)), pltpu.SemaphoreType.DMA((na,))],
    )(*grads)


def _sibling_swap_whole(halves):
    na = len(halves)

    def body(*refs):
        ins, outs = refs[:na], refs[na:2 * na]
        send, recv = refs[2 * na:]
        x, y, c, _ = _place()
        cps = []
        for i in range(na):
            cp = pltpu.make_async_remote_copy(
                src_ref=ins[i], dst_ref=outs[i], send_sem=send.at[i], recv_sem=recv.at[i],
                device_id=(x, y, 1 - c), device_id_type=MESH)
            cp.start()
            cps.append(cp)
        for cp in cps:
            cp.wait()

    out_shape = [jax.ShapeDtypeStruct(h.shape, h.dtype) for h in halves]
    return pl.pallas_call(
        body, name="grad_sibling_join", in_specs=[HBM] * na, out_specs=[HBM] * na, out_shape=out_shape,
        scratch_shapes=[pltpu.SemaphoreType.DMA((na,)), pltpu.SemaphoreType.DMA((na,))],
    )(*halves)


SEM = pl.BlockSpec(memory_space=pltpu.SEMAPHORE)
ANY = pl.BlockSpec(memory_space=pl.ANY)
EFFECT = pltpu.SideEffectType.DATAFLOW_SIDE_EFFECTING


def _split_copy_start(name, plan, srcs, lands, after):
    ns, nl = len(srcs), len(lands)

    def body(*refs):
        src_refs, land_refs = refs[:ns], refs[ns:ns + nl]
        send, recv = refs[ns + nl + 1], refs[ns + nl + 2]
        token = refs[-1]
        outgoing, _ = plan(src_refs, land_refs)
        for src, dst, dev, si, ri in outgoing:
            pltpu.make_async_remote_copy(src_ref=src, dst_ref=dst, send_sem=send.at[si], recv_sem=recv.at[ri],
                                         device_id=dev, device_id_type=MESH).start()
        token[...] = jnp.zeros_like(token)

    n_out, n_in = plan.counts
    thru = [pltpu.HBM(a.shape, a.dtype) for a in list(srcs) + list(lands)]
    res = pl.pallas_call(
        body, name=name,
        out_shape=[pltpu.SemaphoreType.DMA((n_out,)), pltpu.SemaphoreType.DMA((n_in,))] + thru
        + [jax.ShapeDtypeStruct((8, 128), F32)],
        in_specs=[HBM] * (ns + nl) + [ANY],
        out_specs=[SEM, SEM] + [HBM] * (ns + nl) + [pl.BlockSpec(memory_space=pltpu.VMEM)],
        input_output_aliases={i: 2 + i for i in range(ns + nl)},
        compiler_params=pltpu.CompilerParams(has_side_effects=EFFECT),
    )(*[pltpu.with_memory_space_constraint(a, pltpu.HBM) for a in list(srcs) + list(lands)], after)
    return res[0], res[1], res[2:2 + ns], res[2 + ns:2 + ns + nl], res[-1]


def _split_copy_wait(name, plan, send, recv, srcs, lands, after):
    ns, nl = len(srcs), len(lands)
    after = list(after) if isinstance(after, (list, tuple)) else [after]

    def body(*refs):
        src_refs, land_refs = refs[:ns], refs[ns:ns + nl]
        send_ref, recv_ref = refs[ns + nl], refs[ns + nl + 1]
        outgoing, arrivals = plan(src_refs, land_refs)
        for src, dst, dev, si, ri in outgoing:
            pltpu.make_async_remote_copy(src_ref=src, dst_ref=dst, send_sem=send_ref.at[si], recv_sem=recv_ref.at[ri],
                                         device_id=dev, device_id_type=MESH).wait_send()
        for view, ri in arrivals:
            pltpu.make_async_remote_copy(src_ref=view, dst_ref=view, send_sem=send_ref.at[0], recv_sem=recv_ref.at[ri],
                                         device_id=_place()[:3], device_id_type=MESH).wait_recv()

    thru = [pltpu.HBM(a.shape, a.dtype) for a in list(srcs) + list(lands)]
    res = pl.pallas_call(
        body, name=name, out_shape=thru,
        in_specs=[HBM] * (ns + nl) + [SEM, SEM] + [ANY] * len(after), out_specs=[HBM] * (ns + nl),
        input_output_aliases={i: i for i in range(ns + nl)},
        compiler_params=pltpu.CompilerParams(has_side_effects=EFFECT),
    )(*srcs, *lands, send, recv, *after)
    return res[:ns], res[ns:]


def _gather_plan(n_arrays):
    def plan(src_refs, land_refs):
        x, y, c, chips = _place()
        me = 2 * x + y
        outgoing, arrivals = [], []
        for i in range(n_arrays):
            rh = src_refs[i].shape[0] // 2
            mine = pl.ds(c * rh, rh)
            for j, (px, py) in enumerate(chips):
                for delta in range(2):
                    tc = c ^ delta
                    outgoing.append((src_refs[i].at[mine], land_refs[i].at[me, mine], (px, py, tc),
                                     6 * i + 2 * j + delta, 6 * i + 2 * j + delta))
                    theirs = pl.ds(tc * rh, rh)
                    arrivals.append((land_refs[i].at[2 * px + py, theirs], 6 * i + 2 * j + delta))
        return outgoing, arrivals

    plan.counts = (6 * n_arrays, 6 * n_arrays)
    return plan


def _first_gather_plan():
    def plan(src_refs, land_refs):
        x, y, c, chips = _place()
        me = 2 * x + y
        rh = src_refs[0].shape[0] // 2
        mine = pl.ds(c * rh, rh)
        outgoing, arrivals = [], []
        for j, (px, py) in enumerate(chips):
            outgoing.append((src_refs[0].at[mine], land_refs[0].at[me, mine], (px, py, c), j, j))
            arrivals.append((land_refs[0].at[2 * px + py, mine], j))
            outgoing.append((src_refs[1], land_refs[1].at[me], (px, py, c), 3 + j, 3 + j))
            arrivals.append((land_refs[1].at[2 * px + py], 3 + j))
        return outgoing, arrivals

    plan.counts = (6, 6)
    return plan


def _exchange_plan(n_arrays):
    def plan(src_refs, land_refs):
        x, y, c, chips = _place()
        outgoing, arrivals = [], []
        for i in range(n_arrays):
            for j, (px, py) in enumerate(chips):
                outgoing.append((src_refs[i].at[2 * px + py], land_refs[i].at[j], (px, py, c), 3 * i + j, 3 * i + j))
                arrivals.append((land_refs[i].at[j], 3 * i + j))
        return outgoing, arrivals

    plan.counts = (3 * n_arrays, 3 * n_arrays)
    return plan


def _small_allreduce(vec):
    r, cdim = vec.shape
    n_dev = 8

    def body(v_ref, out_ref, buf, send, recv):
        x, y, c, _ = _place()
        me = 4 * x + 2 * y + c
        buf[me] = v_ref[...]
        cps = []
        for k in range(1, n_dev):
            dx, dy, dc = (k >> 2) & 1, (k >> 1) & 1, k & 1
            peer = (x ^ dx, y ^ dy, c ^ dc)
            cp = pltpu.make_async_remote_copy(
                src_ref=v_ref, dst_ref=buf.at[me], send_sem=send.at[k - 1], recv_sem=recv.at[k - 1],
                device_id=peer, device_id_type=MESH)
            cp.start()
            cps.append(cp)
        for k in range(1, n_dev):
            dx, dy, dc = (k >> 2) & 1, (k >> 1) & 1, k & 1
            src = 4 * (x ^ dx) + 2 * (y ^ dy) + (c ^ dc)
            slot = buf.at[src]
            pltpu.make_async_remote_copy(
                src_ref=slot, dst_ref=slot, send_sem=send.at[k - 1], recv_sem=recv.at[k - 1],
                device_id=(x ^ dx, y ^ dy, c ^ dc), device_id_type=MESH).wait_recv()
        for cp in cps:
            cp.wait_send()
        acc = buf[0]
        for k in range(1, n_dev):
            acc = acc + buf[k]
        out_ref[...] = acc

    vm = pl.BlockSpec(memory_space=pltpu.VMEM)
    return pl.pallas_call(
        body, name="small_allreduce", in_specs=[vm], out_specs=vm,
        out_shape=jax.ShapeDtypeStruct((r, cdim), F32),
        scratch_shapes=[pltpu.VMEM((n_dev, r, cdim), F32), pltpu.SemaphoreType.DMA((n_dev - 1,)),
                        pltpu.SemaphoreType.DMA((n_dev - 1,))],
    )(vec)


def _a_cols_to_head_major(w):
    lead = w.shape[:-1]
    q = w[..., :A_QK].reshape(lead + (A_HEADS, A_DK))
    k = w[..., A_QK:2 * A_QK].reshape(lead + (A_HEADS, A_DK))
    v = w[..., 2 * A_QK:2 * A_QK + A_VW].reshape(lead + (A_HEADS, A_DV))
    z = w[..., 2 * A_QK + A_VW:].reshape(lead + (A_HEADS, A_DV))
    return jnp.concatenate([q, k, v, z], axis=-1).reshape(lead + (A_HEADS * A_HEAD_COLS,))


def _a_cols_from_head_major(w):
    lead = w.shape[:-1]
    w = w.reshape(lead + (A_HEADS, A_HEAD_COLS))
    parts = [w[..., :A_DK], w[..., A_DK:2 * A_DK], w[..., 2 * A_DK:2 * A_DK + A_DV], w[..., 2 * A_DK + A_DV:]]
    return jnp.concatenate([p.reshape(lead + (-1,)) for p in parts], axis=-1)


def _conv_cols_to_head_major(w):
    lead = w.shape[:-1]
    q = w[..., :A_QK].reshape(lead + (A_HEADS, A_DK))
    k = w[..., A_QK:2 * A_QK].reshape(lead + (A_HEADS, A_DK))
    v = w[..., 2 * A_QK:].reshape(lead + (A_HEADS, A_DV))
    return jnp.concatenate([q, k, v], axis=-1).reshape(lead + (A_HEADS * A_CONV_COLS,))


def _conv_cols_from_head_major(w):
    lead = w.shape[:-1]
    w = w.reshape(lead + (A_HEADS, A_CONV_COLS))
    parts = [w[..., :A_DK], w[..., A_DK:2 * A_DK], w[..., 2 * A_DK:]]
    return jnp.concatenate([p.reshape(lead + (-1,)) for p in parts], axis=-1)


def _to_stream(a, bn, d):
    rest = a.shape[1:]
    s = a.shape[0] // bn
    a = a.reshape((bn, s // d, d) + rest)
    a = jnp.swapaxes(a, 1, 2)
    return a.reshape((bn * d, s // d) + rest)


def _from_stream(a, bn, d):
    rest = a.shape[2:]
    ln = a.shape[1]
    a = a.reshape((bn, d, ln) + rest)
    a = jnp.swapaxes(a, 1, 2)
    return a.reshape((bn * ln * d,) + rest)


B_SUB = 512
B_SHARD_BLOCKS = (3 * B_GROUPS * B_W + B_W) // N_CHIPS // B_SUB


def _b_block(gi, jj):
    nb = (B_GROUPS * (jj // 2) + gi) * 2 + jj % 2
    return nb // B_SHARD_BLOCKS, nb % B_SHARD_BLOCKS


def _shard_major(g, ncols):
    r = g.shape[0]
    return jnp.swapaxes(g.reshape(r, N_CHIPS, ncols), 0, 1)


def _pack_rows(items):
    rows, offs = [], []
    at = 0
    for a in items:
        flat = a.reshape(-1).astype(F32)
        nr = -(-flat.shape[0] // 1024) * 8
        flat = jnp.pad(flat, (0, nr * 128 - flat.shape[0]))
        rows.append(flat.reshape(nr, 128))
        offs.append((at, nr, a.shape))
        at += nr
    return jnp.concatenate(rows, axis=0), offs


def _unpack_rows(packed, offs):
    out = []
    for at, nr, shape in offs:
        size = int(np.prod(shape)) if len(shape) else 1
        out.append(packed[at:at + nr].reshape(-1)[:size].reshape(shape))
    return out


def _local_step(x, positions, loss_target, norm_g, a_log, a_dt_bias, a_norm_g, b_q_norm_g, b_k_norm_g,
                start_token, first_weights, late_weights, b_grads_ready, a_grads_ready):
    bn, s, d = x.shape
    t = bn * s
    n_chunks = s // A_CHUNK
    x0 = x.reshape(t, d)
    h0 = _rms_fwd(x0, norm_g[0:1] + start_token, "rms0_fwd")
    inv_freq = ROPE_THETA ** (-jnp.arange(0, ROPE_DIMS, 2, dtype=F32) / ROPE_DIMS)
    freq_row = jnp.concatenate([inv_freq, inv_freq, jnp.zeros((128 - ROPE_DIMS,), F32)]).reshape(1, 128)
    posf = jnp.broadcast_to(positions.astype(F32).reshape(t, 1), (t, 128)) + start_token
    tabs = _rope_tables(posf, freq_row)
    tabs_s = [tabs if dil == 1 else [_to_stream(tb, bn, dil).reshape(t, 128) for tb in tabs] for dil in B_DIL]
    wa_in, conv_w, late_token = first_weights([h0] + [tb for ts in tabs_s for tb in ts])
    wa_main = _a_cols_to_head_major(wa_in[:, :A_MAIN])
    wa_tail = jnp.pad(wa_in[:, A_MAIN:], ((0, 0), (0, 128 - 2 * A_HEADS))) + late_token.astype(BF16)
    cw_hm = _conv_cols_to_head_major(conv_w)

    proj_a = _matmul(h0, wa_main, "nn", F32, "a_in_main")
    tail_a = _matmul(h0, wa_tail, "nn", F32, "a_in_tail")
    tail_t = jnp.swapaxes(tail_a[:, :2 * A_HEADS].reshape(bn, s, 2 * A_HEADS), 1, 2)
    tail_t = tail_t.reshape(bn, 2 * A_HEADS, n_chunks, A_CHUNK)
    beta, gc = _gdn_prep(tail_t, a_log[0], a_dt_bias[0])
    proj_a3 = proj_a.reshape(bn, s, A_MAIN)
    og_a, oraw_a, states, t_mats, conv_y = _gdn_fwd(proj_a3, cw_hm, beta, gc, a_norm_g)
    wa_out, wb_in, wb_out = late_weights(og_a)
    b_cols = [4 * B_W] + [3 * B_W] * (B_GROUPS - 1)
    x1, h1 = _out_proj(og_a.reshape(t, A_VW), wa_out, x0, "a_out", norm_g=norm_g[1:2])

    h1_s, proj_b, qkv_b, o_b, lse_b = [], [], [], [], []
    for gi, dil in enumerate(B_DIL):
        hs = h1 if dil == 1 else _to_stream(h1, bn, dil).reshape(t, d)
        ts = tabs_s[gi]
        pj = _matmul(hs, wb_in, "nn", BF16, f"b_in_g{gi}", tm=2048, tn=B_SUB, n=b_cols[gi], b_spec=pl.BlockSpec(
            (None, d, B_SUB), lambda i, j, kk, gi=gi: (_b_block(gi, j)[0], kk, _b_block(gi, j)[1])))
        qkv = _qk_prep(pj, *ts, b_q_norm_g[0, gi:gi + 1], b_k_norm_g[0, gi:gi + 1], f"qk_prep_g{gi}")
        o_s, lse_s = _attn_fwd(qkv.reshape(bn * dil, s // dil, 3 * B_W), f"attn_fwd_g{gi}")
        h1_s.append(hs), proj_b.append(pj), qkv_b.append(qkv)
        o_b.append(o_s.reshape(t, B_W) if dil == 1 else _from_stream(o_s, bn, dil))
        lse_b.append(lse_s.reshape(t, B_HEADS) if dil == 1 else _from_stream(lse_s, bn, dil))
    og_b = _merge_fwd(o_b, lse_b, proj_b[0])
    d_x2, loss_parts = _out_proj(og_b, wb_out, x1, "b_out_loss", target=loss_target.reshape(t, d))
    loss_local = jnp.sum(loss_parts)

    d_x2b = d_x2.astype(BF16)
    g_wb_out = _matmul(og_b, d_x2b, "tn", F32, "b_out_dw")
    d_og_b = _matmul(d_x2b, wb_out, "nt", BF16, "b_out_dx")
    d_o, lse_joint, delta, d_z = _merge_bwd(o_b, lse_b, proj_b[0], d_og_b)
    d_h1, g_qn, g_kn = [], [], []
    g_wb_in = lax.empty(wb_in.shape, F32)
    for gi, dil in enumerate(B_DIL):
        if dil == 1:
            do_s, lj_s, dl_s = d_o, lse_joint, delta
        else:
            do_s, lj_s, dl_s = (_to_stream(a, bn, dil).reshape(t, -1) for a in (d_o, lse_joint, delta))
        ns, ln = bn * dil, s // dil
        dq, dk, dv = _attn_bwd(qkv_b[gi].reshape(ns, ln, 3 * B_W), do_s.reshape(ns, ln, B_W),
                               lj_s.reshape(ns, ln, B_HEADS), dl_s.reshape(ns, ln, B_HEADS), f"attn_bwd_g{gi}")
        d_pj, d_gain = _qk_prep_bwd(proj_b[gi], *tabs_s[gi], b_q_norm_g[0, gi:gi + 1], b_k_norm_g[0, gi:gi + 1],
                                    dq.reshape(t, B_W), dk.reshape(t, B_W), dv.reshape(t, B_W),
                                    d_z if gi == 0 else None, f"qk_prep_bwd_g{gi}")
        g_wb_in = _matmul(h1_s[gi], d_pj, "tn", F32, f"b_in_dw_g{gi}", tn=B_SUB, tk=2048, into=(g_wb_in, pl.BlockSpec(
            (None, d, B_SUB), lambda i, j, kk, gi=gi: (_b_block(gi, j)[0], i, _b_block(gi, j)[1]))))
        dh = _matmul(d_pj, wb_in, "nt", BF16, f"b_in_dx_g{gi}", tm=2048, tk=B_SUB, n=d, b_spec=pl.BlockSpec(
            (None, d, B_SUB), lambda i, j, kk, gi=gi: (_b_block(gi, kk)[0], j, _b_block(gi, kk)[1])))
        d_h1.append(dh if dil == 1 else _from_stream(dh.reshape(ns, ln, d), bn, dil))
        g_qn.append(d_gain[0]), g_kn.append(d_gain[1])
    d_x1, g_norm1 = _rms_bwd(x1, norm_g[1:2], d_h1, d_x2, "rms1_bwd")

    d_x1b = d_x1.astype(BF16)
    g_wa_out = _matmul(og_a.reshape(t, A_VW), d_x1b, "tn", F32, "a_out_dw")
    b_token = b_grads_ready(g_wb_in, g_wb_out, g_wa_out)
    d_og_a = _matmul(d_x1b, wa_out, "nt", BF16, "a_out_dx")
    d_pa, d_gc, d_beta, d_cw, d_ng = _gdn_bwd(proj_a3, cw_hm, beta, gc, a_norm_g + b_token, oraw_a, states,
                                              t_mats, conv_y, d_og_a.reshape(bn, s, A_VW))
    d_tail_t, d_alog, d_dtb = _gdn_prep_bwd(tail_t, a_log[0], a_dt_bias[0], d_gc, d_beta)
    d_tail = jnp.swapaxes(d_tail_t.reshape(bn, 2 * A_HEADS, s), 1, 2).reshape(t, 2 * A_HEADS)
    d_tail = jnp.pad(d_tail, ((0, 0), (0, 128 - 2 * A_HEADS))).astype(BF16)
    d_pa = d_pa.reshape(t, A_MAIN)
    g_wa_main = _matmul(h0, d_pa, "tn", F32, "a_in_dw_main")
    g_wa_tail = _matmul(h0, d_tail, "tn", F32, "a_in_dw_tail")
    g_wa_in = jnp.concatenate([_a_cols_from_head_major(g_wa_main), g_wa_tail[:, :2 * A_HEADS]], axis=1)
    a_token = a_grads_ready(g_wa_in)
    d_h0t = _matmul(d_tail + a_token.astype(BF16), wa_tail, "nt", F32, "a_in_dx_tail")
    d_x0, g_norm0 = _in_proj_bwd(d_pa, wa_main, d_h0t, x0, norm_g[0:1], d_x1, "a_in_dx_rms0_bwd")

    gfull = {
        "norm_g": jnp.concatenate([g_norm0, g_norm1], axis=0), "a_w_in": g_wa_in,
        "a_conv_w": _conv_cols_from_head_major(jnp.sum(d_cw, axis=0)),
        "a_log": jnp.sum(d_alog[:, :, 0], axis=0), "a_dt_bias": jnp.sum(d_dtb[:, :, 0], axis=0),
        "a_norm_g": jnp.sum(d_ng[:, :, 0, :], axis=(0, 1)), "a_w_out": g_wa_out, "b_w_in": g_wb_in,
        "b_q_norm_g": jnp.stack(g_qn), "b_k_norm_g": jnp.stack(g_kn), "b_w_out": g_wb_out}
    return loss_local, d_x0.reshape(bn, s, d), gfull


def kernel(x, positions, norm_g, a_w_in, a_conv_w, a_log, a_dt_bias, a_norm_g, a_w_out, b_w_in, b_q_norm_g, b_k_norm_g, b_w_out, loss_target, m_norm_g, m_a_w_in, m_a_conv_w, m_a_log, m_a_dt_bias, m_a_norm_g, m_a_w_out, m_b_w_in, m_b_q_norm_g, m_b_k_norm_g, m_b_w_out, v_norm_g, v_a_w_in, v_a_conv_w, v_a_log, v_a_dt_bias, v_a_norm_g, v_a_w_out, v_b_w_in, v_b_q_norm_g, v_b_k_norm_g, v_b_w_out):
    d = x.shape[2]
    my_c = lax.axis_index("c")
    my_chip = 2 * lax.axis_index("x") + lax.axis_index("y")

    half_index = jnp.reshape(my_c, (1,)).astype(jnp.int32)
    chip_index = jnp.reshape(my_chip, (1,)).astype(jnp.int32)
    def landing(shard):
        return lax.dynamic_update_slice(lax.empty((N_CHIPS,) + shard.shape, shard.dtype), shard[None],
                                        (my_chip,) + (0,) * shard.ndim)

    first_shards = [a_w_in[0].astype(BF16), a_conv_w[0]]
    first_plan = _first_gather_plan()
    first = _split_copy_start("first_weights_start", first_plan, first_shards,
                              [landing(s) for s in first_shards], half_index)
    pending = {}
    late_shards = [(w[0] + first[4][0, 0]).astype(BF16) for w in (a_w_out, b_w_in, b_w_out)]
    late_lands = [landing(s) for s in late_shards]

    def first_weights(after):
        _, (ga_in, g_conv) = _split_copy_wait("first_weights_wait", first_plan, *first[:4],
                                              list(after) + late_lands)
        ga_in = _sibling_forward(ga_in)
        wa_in = jnp.concatenate([ga_in[k] for k in range(N_CHIPS)], axis=1)
        conv_w = jnp.concatenate([g_conv[k] for k in range(N_CHIPS)], axis=1)
        plan = _gather_plan(len(late_shards))
        pending["late"] = (plan,) + tuple(_split_copy_start(
            "late_weights_start", plan, late_shards, late_lands, conv_w))
        return wa_in, conv_w, pending["late"][5][0, 0]

    def late_weights(after):
        plan, send, recv, srcs, lands, _ = pending["late"]
        _, (ga_out, gb_in, gb_out) = _split_copy_wait("late_weights_wait", plan, send, recv, srcs, lands, after)
        return ga_out.reshape(A_VW, d), gb_in, gb_out.reshape(B_W, d)

    def reduce_to_chip_sums(mats, tag):
        recv_sib = _sibling_swap_halves([g.astype(BF16) for g in mats], f"grad_{tag}_sibling_swap")
        return [_pair_sum(g, r, half_index, f"grad_{tag}_pair_sum_{i}") for i, (g, r) in enumerate(zip(mats, recv_sib))]

    def start_exchange(tag, mats):
        sums = reduce_to_chip_sums(mats, tag)
        lands = [lax.empty((N_CHIPS - 1,) + s.shape[1:], BF16) for s in sums]
        plan = _exchange_plan(len(mats))
        pending[tag] = (plan,) + tuple(_split_copy_start(f"grad_{tag}_exchange_start", plan, sums, lands, chip_index))
        return pending[tag][5][0, 0]

    def finish_exchange(tag, after):
        plan, send, recv, srcs, lands, _ = pending[tag]
        return _split_copy_wait(f"grad_{tag}_exchange_wait", plan, send, recv, srcs, lands, after)

    def b_grads_ready(g_wb_in, g_wb_out, g_wa_out):
        return start_exchange("b", [g_wb_in, g_wb_out.reshape(N_CHIPS, -1, d), g_wa_out.reshape(N_CHIPS, -1, d)])

    def a_grads_ready(g_wa_in):
        return start_exchange("a", [_shard_major(g_wa_in, a_w_in.shape[2])])

    loss_local, d_x0, gfull = _local_step(x, positions, loss_target, norm_g, a_log, a_dt_bias, a_norm_g,
                                          b_q_norm_g, b_k_norm_g, first[4][0, 0], first_weights, late_weights,
                                          b_grads_ready, a_grads_ready)

    small = [gfull["norm_g"], gfull["a_conv_w"], gfull["a_log"], gfull["a_dt_bias"], gfull["a_norm_g"],
             gfull["b_q_norm_g"], gfull["b_k_norm_g"], loss_local]
    packed, offs = _pack_rows(small)
    reduced = _small_allreduce(packed)
    g_norm, g_conv_all, g_alog, g_dtb, g_ang, g_q, g_k, loss = _unpack_rows(reduced, offs)
    g_conv_mine = lax.dynamic_slice_in_dim(g_conv_all, my_chip * a_conv_w.shape[2], a_conv_w.shape[2], axis=1)

    b_sums, b_received = finish_exchange("b", d_x0)
    a_sums, a_received = finish_exchange("a", reduced)
    chip_sums = [a_sums[0], b_sums[2], b_sums[0], b_sums[1]]
    received = [a_received[0], b_received[2], b_received[0], b_received[1]]
    halves = [_chip_sum(s, r, chip_index, f"grad_chip_sum_{i}") for i, (s, r) in enumerate(zip(chip_sums, received))]
    theirs = _sibling_swap_whole(halves)
    big = ("a_w_in", "a_w_out", "b_w_in", "b_w_out")
    big_halves = dict(zip(big, zip(halves, theirs)))

    grads = {
        "norm_g": g_norm, "a_conv_w": g_conv_mine[None], "a_log": g_alog[None], "a_dt_bias": g_dtb[None],
        "a_norm_g": g_ang[None], "b_q_norm_g": g_q[None], "b_k_norm_g": g_k[None]}
    weights = {"norm_g": norm_g, "a_w_in": a_w_in, "a_conv_w": a_conv_w, "a_log": a_log, "a_dt_bias": a_dt_bias,
               "a_norm_g": a_norm_g, "a_w_out": a_w_out, "b_w_in": b_w_in, "b_q_norm_g": b_q_norm_g,
               "b_k_norm_g": b_k_norm_g, "b_w_out": b_w_out}
    m_in = {"norm_g": m_norm_g, "a_w_in": m_a_w_in, "a_conv_w": m_a_conv_w, "a_log": m_a_log,
            "a_dt_bias": m_a_dt_bias, "a_norm_g": m_a_norm_g, "a_w_out": m_a_w_out, "b_w_in": m_b_w_in,
            "b_q_norm_g": m_b_q_norm_g, "b_k_norm_g": m_b_k_norm_g, "b_w_out": m_b_w_out}
    v_in = {"norm_g": v_norm_g, "a_w_in": v_a_w_in, "a_conv_w": v_a_conv_w, "a_log": v_a_log,
            "a_dt_bias": v_a_dt_bias, "a_norm_g": v_a_norm_g, "a_w_out": v_a_w_out, "b_w_in": v_b_w_in,
            "b_q_norm_g": v_b_q_norm_g, "b_k_norm_g": v_b_k_norm_g, "b_w_out": v_b_w_out}
    names = list(weights)

    delta_w, new_m, new_v = {}, {}, {}
    for nm in big:
        mine, other = big_halves[nm]
        if weights[nm].shape[2] % 128:
            cols = lambda a: jnp.transpose(a, (2, 0, 1))
            half_cols = lambda a: jnp.transpose(a)[:, None, :]
            outs = _adamw_shard_cols(cols(weights[nm]), half_cols(mine), half_cols(other), cols(m_in[nm]),
                                     cols(v_in[nm]), half_index, f"adamw_{nm}")
            outs = [jnp.transpose(o, (1, 2, 0)) for o in outs]
        else:
            outs = _adamw_shard(weights[nm], mine, other, m_in[nm], v_in[nm], half_index, f"adamw_{nm}")
        grads[nm], delta_w[nm], new_m[nm], new_v[nm] = outs
    small_names = [nm for nm in names if nm not in big]
    packs = [_pack_rows([src[nm] for nm in small_names]) for src in (weights, grads, m_in, v_in)]
    offs = packs[0][1]
    dl, m2, v2 = _adamw(packs[0][0], packs[1][0], packs[2][0], packs[3][0], "adamw_small")
    for nm, a, b, c2 in zip(small_names, _unpack_rows(dl, offs), _unpack_rows(m2, offs), _unpack_rows(v2, offs)):
        delta_w[nm], new_m[nm], new_v[nm] = a, b, c2

    return (loss, d_x0, *[grads[nm] for nm in names], *[delta_w[nm] for nm in names],
            *[new_m[nm] for nm in names], *[new_v[nm] for nm in names])
```

```python
import jax
import jax.numpy as jnp
import numpy as np
from jax import lax
from jax.experimental import pallas as pl
from jax.experimental.pallas import tpu as pltpu

F32 = jnp.float32
BF16 = jnp.bfloat16
MESH = pl.DeviceIdType.MESH

EPS = 1e-6
A_HEADS = 8
A_DK = 128
A_DV = 256
A_QK = A_HEADS * A_DK
A_VW = A_HEADS * A_DV
A_MAIN = 2 * A_QK + 2 * A_VW
A_HEAD_COLS = 2 * A_DK + 2 * A_DV
A_CONV_COLS = 2 * A_DK + A_DV
A_CHUNK = 64
A_CONV = 4
B_GROUPS = 3
B_HEADS = 8
B_DH = 128
B_W = B_HEADS * B_DH
B_DIL = (1, 4, 16)
B_BLK = 128
ROPE_THETA = 500000.0
ROPE_DIMS = B_DH // 4
ADAM_LR, ADAM_B1, ADAM_B2, ADAM_EPS, ADAM_WD, ADAM_STEP = 0.001, 0.9, 0.999, 1e-08, 0.01, 10
N_CHIPS = 4
VMEM_BIG = 56 * 1024 * 1024


def _params(sem=None, vmem=None):
    return pltpu.CompilerParams(dimension_semantics=sem, vmem_limit_bytes=vmem)


def _dot(a, b, ca, cb):
    return lax.dot_general(a.astype(BF16), b.astype(BF16), (((ca,), (cb,)), ((), ())),
                           preferred_element_type=F32)


def _split3(a):
    hi = a.astype(BF16)
    r = a - hi.astype(F32)
    mid = r.astype(BF16)
    lo = (r - mid.astype(F32)).astype(BF16)
    return hi, mid, lo


def _sigmoid(y):
    return 1.0 / (1.0 + jnp.exp(-y))


def _silu(y):
    return y * _sigmoid(y)


def _silu_and_slope(y):
    s = _sigmoid(y)
    return y * s, s * (1.0 + y * (1.0 - s))


def _matmul(a, b, mode, out_dtype, name, res=None, tm=1024, tn=1024, tk=1024, n=None, b_spec=None, into=None):
    m, k = a.shape[::-1] if mode == "tn" else a.shape
    if n is None:
        n = b.shape[0] if mode == "nt" else b.shape[1]
    tm, tn, tk = min(tm, m), min(tn, n), min(tk, k)
    assert m % tm == 0 and n % tn == 0 and k % tk == 0, (name, a.shape, b.shape)
    nk = k // tk
    dims = {"nn": ((1,), (0,)), "nt": ((1,), (1,)), "tn": ((0,), (0,))}[mode]

    def body(*refs):
        a_ref, b_ref = refs[0], refs[1]
        r_ref = refs[2] if res is not None else None
        o_ref = refs[2 + (res is not None) + (into is not None)]
        prod = lax.dot_general(a_ref[...], b_ref[...], (dims, ((), ())), preferred_element_type=F32)

        def finish(r):
            if res is not None:
                r = r + r_ref[...]
            o_ref[...] = r.astype(out_dtype)

        if nk == 1:
            finish(prod)
            return
        acc = refs[-1]
        kk = pl.program_id(2)

        @pl.when(kk == 0)
        def _():
            acc[...] = prod

        @pl.when((kk > 0) & (kk < nk - 1))
        def _():
            acc[...] += prod

        @pl.when(kk == nk - 1)
        def _():
            finish(acc[...] + prod)

    a_spec = pl.BlockSpec((tm, tk), lambda i, j, kk: (i, kk))
    if mode == "tn":
        a_spec = pl.BlockSpec((tk, tm), lambda i, j, kk: (kk, i))
    if b_spec is None and mode == "nt":
        b_spec = pl.BlockSpec((tn, tk), lambda i, j, kk: (j, kk))
    elif b_spec is None:
        b_spec = pl.BlockSpec((tk, tn), lambda i, j, kk: (kk, j))
    in_specs = [a_spec, b_spec]
    args = [a, b]
    if res is not None:
        in_specs.append(pl.BlockSpec((tm, tn), lambda i, j, kk: (i, j)))
        args.append(res)
    out_spec = pl.BlockSpec((tm, tn), lambda i, j, kk: (i, j))
    out_shape = jax.ShapeDtypeStruct((m, n), out_dtype)
    aliases = {}
    if into is not None:
        assert res is None
        buf, out_spec = into
        out_shape = jax.ShapeDtypeStruct(buf.shape, buf.dtype)
        in_specs.append(ANY)
        args.append(buf)
        aliases = {2: 0}
    return pl.pallas_call(
        body, name=name, grid=(m // tm, n // tn, nk),
        in_specs=in_specs, out_specs=out_spec, out_shape=out_shape, input_output_aliases=aliases,
        scratch_shapes=[pltpu.VMEM((tm, tn), F32)] if nk > 1 else [],
        compiler_params=_params(("parallel", "parallel", "arbitrary"), 48 * 1024 * 1024),
    )(*args)


def _rms_fwd(x, g, name, tm=256):
    t, d = x.shape

    def body(x_ref, g_ref, h_ref):
        xv = x_ref[...]
        r = lax.rsqrt(jnp.mean(xv * xv, axis=-1, keepdims=True) + EPS)
        h_ref[...] = (xv * r * g_ref[...]).astype(BF16)

    return pl.pallas_call(
        body, name=name, grid=(t // tm,),
        in_specs=[pl.BlockSpec((tm, d), lambda i: (i, 0)), pl.BlockSpec((1, d), lambda i: (0, 0))],
        out_specs=pl.BlockSpec((tm, d), lambda i: (i, 0)),
        out_shape=jax.ShapeDtypeStruct((t, d), BF16),
        compiler_params=_params(("parallel",)),
    )(x, g)


def _rms_bwd(x, g, dhs, dres, name, tm=256):
    t, d = x.shape
    n_dh = len(dhs)

    def body(*refs):
        x_ref, g_ref = refs[0], refs[1]
        dh_refs = refs[2:2 + n_dh]
        dres_ref, dx_ref, dg_ref = refs[2 + n_dh:]
        i = pl.program_id(0)

        @pl.when(i == 0)
        def _():
            dg_ref[...] = jnp.zeros_like(dg_ref)

        xv = x_ref[...]
        r = lax.rsqrt(jnp.mean(xv * xv, axis=-1, keepdims=True) + EPS)
        xh = xv * r
        dh = dh_refs[0][...].astype(F32)
        for ref in dh_refs[1:]:
            dh = dh + ref[...].astype(F32)
        dg_ref[0:1, :] += jnp.sum(dh * xh, axis=0, keepdims=True)
        dxh = dh * g_ref[...]
        dx = r * (dxh - xh * jnp.mean(dxh * xh, axis=-1, keepdims=True))
        dx_ref[...] = dx + dres_ref[...]

    row = pl.BlockSpec((tm, d), lambda i: (i, 0))
    dx, dg = pl.pallas_call(
        body, name=name, grid=(t // tm,),
        in_specs=[row, pl.BlockSpec((1, d), lambda i: (0, 0))] + [row] * n_dh + [row],
        out_specs=[row, pl.BlockSpec((8, d), lambda i: (0, 0))],
        out_shape=[jax.ShapeDtypeStruct((t, d), F32), jax.ShapeDtypeStruct((8, d), F32)],
        compiler_params=_params(("arbitrary",)),
    )(x, g, *dhs, dres)
    return dx, dg[0:1]


def _in_proj_bwd(dp, w, dh_more, x, g, dres, name, tm=512, tk=1024):
    t, k = dp.shape
    d = w.shape[0]
    nk = k // tk

    def body(dp_ref, w_ref, more_ref, x_ref, g_ref, dres_ref, dx_ref, dg_ref, acc):
        i, kk = pl.program_id(0), pl.program_id(1)

        @pl.when((i == 0) & (kk == 0))
        def _():
            dg_ref[...] = jnp.zeros_like(dg_ref)

        prod = lax.dot_general(dp_ref[...], w_ref[...], (((1,), (1,)), ((), ())), preferred_element_type=F32)

        @pl.when(kk == 0)
        def _():
            acc[...] = prod

        @pl.when((kk > 0) & (kk < nk - 1))
        def _():
            acc[...] += prod

        @pl.when(kk == nk - 1)
        def _():
            dh = acc[...] + prod + more_ref[...]
            xv = x_ref[...]
            r = lax.rsqrt(jnp.mean(xv * xv, axis=-1, keepdims=True) + EPS)
            xh = xv * r
            dg_ref[0:1, :] += jnp.sum(dh * xh, axis=0, keepdims=True)
            dxh = dh * g_ref[...]
            dx_ref[...] = r * (dxh - xh * jnp.mean(dxh * xh, axis=-1, keepdims=True)) + dres_ref[...]

    row = pl.BlockSpec((tm, d), lambda i, kk: (i, 0))
    dx, dg = pl.pallas_call(
        body, name=name, grid=(t // tm, nk),
        in_specs=[pl.BlockSpec((tm, tk), lambda i, kk: (i, kk)), pl.BlockSpec((d, tk), lambda i, kk: (0, kk)),
                  row, row, pl.BlockSpec((1, d), lambda i, kk: (0, 0)), row],
        out_specs=[row, pl.BlockSpec((8, d), lambda i, kk: (0, 0))],
        out_shape=[jax.ShapeDtypeStruct((t, d), F32), jax.ShapeDtypeStruct((8, d), F32)],
        scratch_shapes=[pltpu.VMEM((tm, d), F32)],
        compiler_params=_params(("arbitrary", "arbitrary"), 48 * 1024 * 1024),
    )(dp, w, dh_more, x, g, dres)
    return dx, dg[0:1]


def _out_proj(a, w, res, name, norm_g=None, target=None, tm=512):
    t, k = a.shape
    d = w.shape[1]
    nb = t // tm

    def body(a_ref, w_ref, r_ref, x_ref, o1_ref, o2_ref):
        y = jnp.dot(a_ref[...], w_ref[...], preferred_element_type=F32) + r_ref[...]
        if norm_g is not None:
            o1_ref[...] = y
            r = lax.rsqrt(jnp.mean(y * y, axis=-1, keepdims=True) + EPS)
            o2_ref[...] = (y * r * x_ref[...]).astype(BF16)
        else:
            e = y - x_ref[...]
            o1_ref[...] = e * (1.0 / d)
            s = jnp.sum(jnp.sum(e * e, axis=1, keepdims=True), axis=0, keepdims=True) * (0.5 / d)
            o2_ref[...] = jnp.broadcast_to(s, (8, 128))

    row = pl.BlockSpec((tm, d), lambda i: (i, 0))
    if norm_g is not None:
        extra, extra_spec = norm_g, pl.BlockSpec((1, d), lambda i: (0, 0))
        out2_spec, out2_shape = row, jax.ShapeDtypeStruct((t, d), BF16)
    else:
        extra, extra_spec = target, row
        out2_spec = pl.BlockSpec((None, 8, 128), lambda i: (i, 0, 0))
        out2_shape = jax.ShapeDtypeStruct((nb, 8, 128), F32)
    o1, o2 = pl.pallas_call(
        body, name=name, grid=(nb,),
        in_specs=[pl.BlockSpec((tm, k), lambda i: (i, 0)), pl.BlockSpec((k, d), lambda i: (0, 0)), row, extra_spec],
        out_specs=[row, out2_spec], out_shape=[jax.ShapeDtypeStruct((t, d), F32), out2_shape],
        compiler_params=_params(("parallel",), 48 * 1024 * 1024),
    )(a, w, res, extra)
    return (o1, o2) if norm_g is not None else (o1, o2[:, 0, 0])


def _softplus(x):
    t = jnp.exp(-jnp.abs(x))
    return jnp.maximum(x, 0.0) + jnp.where(t < 1e-3, t * (1.0 - 0.5 * t), jnp.log(1.0 + t))


def _tri(rows_le_cols):
    r = lax.broadcasted_iota(jnp.int32, (A_CHUNK, A_CHUNK), 0)
    c = lax.broadcasted_iota(jnp.int32, (A_CHUNK, A_CHUNK), 1)
    return jnp.where((r <= c) if rows_le_cols else (r >= c), 1.0, 0.0).astype(BF16)


def _dot_exact_rhs(a, ones_bf16):
    dn = (((1,), (0,)), ((), ()))
    hi, mid, lo = _split3(a)
    out = lax.dot_general(hi, ones_bf16, dn, preferred_element_type=F32)
    out = out + lax.dot_general(mid, ones_bf16, dn, preferred_element_type=F32)
    return out + lax.dot_general(lo, ones_bf16, dn, preferred_element_type=F32)


def _gdn_prep(tail_t, a_log, dt_bias):
    bn, _, n, c = tail_t.shape

    def body(t_ref, alog_ref, dtb_ref, beta_ref, gc_ref):
        upper = _tri(True)
        for h in range(A_HEADS):
            beta_ref[h] = _sigmoid(t_ref[h])
            ea = jnp.exp(jnp.full((n, c), alog_ref[h], F32))
            g = -ea * _softplus(t_ref[A_HEADS + h] + dtb_ref[h])
            gc_ref[h] = _dot_exact_rhs(g, upper)

    smem = pl.BlockSpec(memory_space=pltpu.SMEM)
    blk = pl.BlockSpec((None, A_HEADS, n, c), lambda b: (b, 0, 0, 0))
    return pl.pallas_call(
        body, name="gdn_prep", grid=(bn,),
        in_specs=[pl.BlockSpec((None, 2 * A_HEADS, n, c), lambda b: (b, 0, 0, 0)), smem, smem],
        out_specs=[blk, blk],
        out_shape=[jax.ShapeDtypeStruct((bn, A_HEADS, n, c), F32)] * 2,
        compiler_params=_params(("parallel",)),
    )(tail_t, a_log, dt_bias)


def _gdn_prep_bwd(tail_t, a_log, dt_bias, d_gc, d_beta):
    bn, _, n, c = tail_t.shape

    def body(t_ref, alog_ref, dtb_ref, dgc_ref, dbeta_ref, dt_ref, dal_ref, ddt_ref):
        lower = _tri(False)
        for h in range(A_HEADS):
            beta = _sigmoid(t_ref[h])
            dt_ref[h] = dbeta_ref[h] * beta * (1.0 - beta)
            dg = _dot_exact_rhs(dgc_ref[h], lower)
            ea = jnp.exp(jnp.full((n, c), alog_ref[h], F32))
            xa = t_ref[A_HEADS + h] + dtb_ref[h]
            g = -ea * _softplus(xa)
            dxa = -ea * dg * _sigmoid(xa)
            dt_ref[A_HEADS + h] = dxa
            s1 = jnp.sum(jnp.sum(g * dg, axis=1, keepdims=True), axis=0, keepdims=True)
            s2 = jnp.sum(jnp.sum(dxa, axis=1, keepdims=True), axis=0, keepdims=True)
            dal_ref[h:h + 1, :] = jnp.broadcast_to(s1, (1, 128))
            ddt_ref[h:h + 1, :] = jnp.broadcast_to(s2, (1, 128))

    smem = pl.BlockSpec(memory_space=pltpu.SMEM)
    blk8 = pl.BlockSpec((None, A_HEADS, n, c), lambda b: (b, 0, 0, 0))
    blk16 = pl.BlockSpec((None, 2 * A_HEADS, n, c), lambda b: (b, 0, 0, 0))
    sm = pl.BlockSpec((None, A_HEADS, 128), lambda b: (b, 0, 0))
    return pl.pallas_call(
        body, name="gdn_prep_bwd", grid=(bn,),
        in_specs=[blk16, smem, smem, blk8, blk8],
        out_specs=[blk16, sm, sm],
        out_shape=[jax.ShapeDtypeStruct((bn, 2 * A_HEADS, n, c), F32),
                   jax.ShapeDtypeStruct((bn, A_HEADS, 128), F32),
                   jax.ShapeDtypeStruct((bn, A_HEADS, 128), F32)],
        compiler_params=_params(("parallel",)),
    )(tail_t, a_log, dt_bias, d_gc, d_beta)


HALO = 8


def _conv_taps(xw, w):
    y = w[A_CONV - 1:A_CONV, :] * xw
    for j in range(1, A_CONV):
        y = y + w[A_CONV - 1 - j:A_CONV - j, :] * pltpu.roll(xw, j, 0)
    return y[HALO:, :]


def _row_to_col(row, eye):
    c = eye.shape[0]
    return jnp.sum(jnp.where(eye, jnp.broadcast_to(row, (c, c)), 0.0), axis=1, keepdims=True)


def _col_to_row(col, eye):
    c = eye.shape[0]
    return jnp.sum(jnp.where(eye, jnp.broadcast_to(col, (c, c)), 0.0), axis=0, keepdims=True)


def _unit_lower_inverse(a, ri, ci):
    eye = jnp.where(ri == ci, 1.0, 0.0)
    a8 = jnp.where((ri >> 3) == (ci >> 3), a, 0.0)
    a2 = _dot(a8, a8, 1, 0)
    yield
    a4 = _dot(a2, a2, 1, 0)
    t = eye - a8
    t = t + _dot(t, a2, 1, 0)
    yield
    t = t + _dot(t, a4, 1, 0)
    yield
    for sh in (3, 4, 5):
        off = jnp.where(((ri >> (sh + 1)) == (ci >> (sh + 1))) & ((ri >> sh) != (ci >> sh)), a, 0.0)
        left = _dot(t, off, 1, 0)
        yield
        t = t - _dot(left, t, 1, 0)
        yield
    return t


def _round_robin(gens):
    live = list(gens)
    while live:
        nxt = []
        for g in live:
            try:
                next(g)
                nxt.append(g)
            except StopIteration:
                pass
        live = nxt


def _gdn_chunk_core(q, k, v, g_row, b_row, t_mat, ri, ci):
    eye = ri == ci
    g_col = _row_to_col(g_row, eye)
    b_col = _row_to_col(b_row, eye)
    causal = ri >= ci
    strict = ri > ci
    dec = jnp.where(causal, jnp.exp(jnp.where(causal, g_col - g_row, 0.0)), 0.0)
    gam = jnp.exp(g_col)
    g_last = g_row[:, A_CHUNK - 1:A_CHUNK]
    gam_last = jnp.exp(g_last)
    e = jnp.exp(g_last - g_col)
    kb = k * b_col
    bv = v * b_col
    kbg = kb * gam
    q16, k16, kb16 = q.astype(BF16), k.astype(BF16), kb.astype(BF16)
    kk = _dot(kb16, k16, 1, 1)
    p = _dot(q16, k16, 1, 1) * dec
    yield
    a_mat = jnp.where(strict, kk * dec, 0.0)
    if t_mat is None:
        t_mat = yield from _unit_lower_inverse(a_mat, ri, ci)
    t16 = t_mat.astype(BF16)
    u = _dot(t16, bv, 1, 0)
    w = _dot(t16, kbg, 1, 0)
    yield
    return dict(eye=eye, g_col=g_col, b_col=b_col, dec=dec, strict=strict, causal=causal, gam=gam,
                gam_last=gam_last, e=e, kb=kb, bv=bv, kbg=kbg, a_mat=a_mat, t_mat=t_mat, u=u, w=w, p=p,
                qg=q * gam, kd=k * e, q16=q16, k16=k16, kb16=kb16, t16=t16)


A_SEQ_BLK = 256
A_BLK_CHUNKS = A_SEQ_BLK // A_CHUNK


def _gdn_halo(proj_hm):
    bn, s, w = proj_hm.shape
    last = proj_hm.reshape(bn, s // A_SEQ_BLK, A_SEQ_BLK, w)[:, :, A_SEQ_BLK - HALO:, :]
    return jnp.concatenate([jnp.zeros((bn, 1, HALO, w), proj_hm.dtype), last[:, :-1]], axis=1)


def _gdn_window(x_ref, halo_ref, ci, first, lo):
    if first:
        return jnp.concatenate([halo_ref[:, lo:lo + A_CONV_COLS], x_ref[0:A_CHUNK, lo:lo + A_CONV_COLS]], axis=0)
    start = pl.multiple_of(ci * A_CHUNK - HALO, HALO)
    return x_ref[pl.ds(start, A_CHUNK + HALO), lo:lo + A_CONV_COLS]


def _gdn_chunk_prep(xw, cw, y=None):
    if y is None:
        y = _conv_taps(xw, cw)
    a, slope = _silu_and_slope(y)
    aq, ak, v = a[:, 0:A_DK], a[:, A_DK:2 * A_DK], a[:, 2 * A_DK:]
    rq = lax.rsqrt(jnp.sum(aq * aq, axis=1, keepdims=True) + EPS)
    rk = lax.rsqrt(jnp.sum(ak * ak, axis=1, keepdims=True) + EPS)
    return dict(xw=xw, y=y, slope=slope, aq=aq, ak=ak, rq=rq, rk=rk, q=aq * rq * (A_DK ** -0.5), k=ak * rk, v=v)


def _gdn_fwd(proj_hm, cw_hm, beta, gc, norm_g, hp=8):
    bn, s, _ = proj_hm.shape
    n = s // A_CHUNK
    nsb = s // A_SEQ_BLK
    halo = _gdn_halo(proj_hm)

    def body(x_ref, halo_ref, cw_ref, beta_ref, gc_ref, ng_ref, og_ref, oraw_ref, st_ref, t_ref, y_ref, state):
        first_chunk = pl.program_id(2) * A_BLK_CHUNKS
        ri = lax.broadcasted_iota(jnp.int32, (A_CHUNK, A_CHUNK), 0)
        ci_ = lax.broadcasted_iota(jnp.int32, (A_CHUNK, A_CHUNK), 1)
        ng = ng_ref[...]

        @pl.when(pl.program_id(2) == 0)
        def _():
            state[...] = jnp.zeros_like(state)

        def one_head(hh, ci, first, rows):
            lo = hh * A_HEAD_COLS
            cw = cw_ref[:, hh * A_CONV_COLS:(hh + 1) * A_CONV_COLS]
            cin = _gdn_chunk_prep(_gdn_window(x_ref, halo_ref, ci, first, lo), cw)
            y_ref[rows, hh * A_CONV_COLS:(hh + 1) * A_CONV_COLS] = cin["y"]
            seq_chunk = pl.ds(first_chunk + ci, 1)
            core = yield from _gdn_chunk_core(cin["q"], cin["k"], cin["v"], gc_ref[hh, seq_chunk, :],
                                              beta_ref[hh, seq_chunk, :], None, ri, ci_)
            st = state[hh]
            st_ref[hh, ci] = st
            t_ref[hh, ci] = core["t_mat"]
            st16 = st.astype(BF16)
            vn = core["u"] - _dot(core["w"], st16, 1, 0)
            qs = _dot(core["qg"], st16, 1, 0)
            yield
            vn16 = vn.astype(BF16)
            o = qs + _dot(core["p"], vn16, 1, 0)
            state[hh] = st * core["gam_last"] + _dot(core["kd"], vn16, 0, 0)
            yield
            ocols = slice(hh * A_DV, (hh + 1) * A_DV)
            oraw_ref[rows, ocols] = o
            r = lax.rsqrt(jnp.mean(o * o, axis=1, keepdims=True) + EPS)
            z = x_ref[rows, lo + A_CONV_COLS:lo + A_HEAD_COLS]
            og_ref[rows, ocols] = (o * r * ng * _silu(z)).astype(BF16)

        def chunk(ci, first):
            rows = pl.ds(0 if first else pl.multiple_of(ci * A_CHUNK, A_CHUNK), A_CHUNK)
            _round_robin([one_head(hh, ci, first, rows) for hh in range(hp)])

        chunk(0, True)
        lax.fori_loop(1, A_BLK_CHUNKS, lambda i, c: (chunk(i, False), c)[1], 0)

    small = pl.BlockSpec((None, hp, n, A_CHUNK), lambda b, h, j: (b, h, 0, 0))
    return pl.pallas_call(
        body, name="gdn_fwd", grid=(bn, A_HEADS // hp, nsb),
        in_specs=[pl.BlockSpec((None, A_SEQ_BLK, hp * A_HEAD_COLS), lambda b, h, j: (b, j, h)),
                  pl.BlockSpec((None, None, HALO, hp * A_HEAD_COLS), lambda b, h, j: (b, j, 0, h)),
                  pl.BlockSpec((A_CONV, hp * A_CONV_COLS), lambda b, h, j: (0, h)),
                  small, small,
                  pl.BlockSpec((1, A_DV), lambda b, h, j: (0, 0))],
        out_specs=[pl.BlockSpec((None, A_SEQ_BLK, hp * A_DV), lambda b, h, j: (b, j, h)),
                   pl.BlockSpec((None, A_SEQ_BLK, hp * A_DV), lambda b, h, j: (b, j, h)),
                   pl.BlockSpec((None, hp, A_BLK_CHUNKS, A_DK, A_DV), lambda b, h, j: (b, h, j, 0, 0)),
                   pl.BlockSpec((None, hp, A_BLK_CHUNKS, A_CHUNK, A_CHUNK), lambda b, h, j: (b, h, j, 0, 0)),
                   pl.BlockSpec((None, A_SEQ_BLK, hp * A_CONV_COLS), lambda b, h, j: (b, j, h))],
        out_shape=[jax.ShapeDtypeStruct((bn, s, A_VW), BF16),
                   jax.ShapeDtypeStruct((bn, s, A_VW), F32),
                   jax.ShapeDtypeStruct((bn, A_HEADS, n, A_DK, A_DV), F32),
                   jax.ShapeDtypeStruct((bn, A_HEADS, n, A_CHUNK, A_CHUNK), F32),
                   jax.ShapeDtypeStruct((bn, s, A_HEADS * A_CONV_COLS), F32)],
        scratch_shapes=[pltpu.VMEM((hp, A_DK, A_DV), F32)],
        compiler_params=_params(("parallel", "parallel", "arbitrary"), VMEM_BIG),
    )(proj_hm, halo, cw_hm, beta, gc, norm_g)


def _gdn_bwd(proj_hm, cw_hm, beta, gc, norm_g, oraw, states, t_mats, conv_y, dog, hp=4):
    bn, s, _ = proj_hm.shape
    n = s // A_CHUNK
    nsb = s // A_SEQ_BLK
    halo = _gdn_halo(proj_hm)

    def body(x_ref, halo_ref, cw_ref, beta_ref, gc_ref, ng_ref, oraw_ref, st_ref, t_ref, y_ref, dog_ref,
             dx_ref, dgc_ref, dbeta_ref, dcw_ref, dng_ref, dstate, dy_next, shifted):
        first_chunk = (nsb - 1 - pl.program_id(2)) * A_BLK_CHUNKS
        ri = lax.broadcasted_iota(jnp.int32, (A_CHUNK, A_CHUNK), 0)
        ci_ = lax.broadcasted_iota(jnp.int32, (A_CHUNK, A_CHUNK), 1)
        lane = lax.broadcasted_iota(jnp.int32, (1, A_CHUNK), 1)
        ng = ng_ref[...]

        @pl.when(pl.program_id(2) == 0)
        def _():
            dstate[...] = jnp.zeros_like(dstate)
            dy_next[...] = jnp.zeros_like(dy_next)
            dcw_ref[...] = jnp.zeros_like(dcw_ref)
            dng_ref[...] = jnp.zeros_like(dng_ref)

        def one_head(hh, ci, first, rows):
            lo = hh * A_HEAD_COLS
            ccols = slice(hh * A_CONV_COLS, (hh + 1) * A_CONV_COLS)
            ocols = slice(hh * A_DV, (hh + 1) * A_DV)
            cw = cw_ref[:, ccols]
            cin = _gdn_chunk_prep(_gdn_window(x_ref, halo_ref, ci, first, lo), cw, y_ref[rows, ccols])
            q, k, v = cin["q"], cin["k"], cin["v"]
            seq_chunk = pl.ds(first_chunk + ci, 1)
            cr = yield from _gdn_chunk_core(q, k, v, gc_ref[hh, seq_chunk, :], beta_ref[hh, seq_chunk, :],
                                            t_ref[hh, ci], ri, ci_)
            eye, dec, gam, e = cr["eye"], cr["dec"], cr["gam"], cr["e"]
            b_col, t_mat, u, w, p = cr["b_col"], cr["t_mat"], cr["u"], cr["w"], cr["p"]
            st = st_ref[hh, ci]
            ds_out = dstate[hh]

            o = oraw_ref[rows, ocols]
            z = x_ref[rows, lo + A_CONV_COLS:lo + A_HEAD_COLS]
            d_og = dog_ref[rows, ocols].astype(F32)
            r = lax.rsqrt(jnp.mean(o * o, axis=1, keepdims=True) + EPS)
            oh = o * r
            gate, gate_slope = _silu_and_slope(z)
            d_on = d_og * gate
            dz = d_og * oh * ng * gate_slope
            dng_ref[hh, 0:1, :] += jnp.sum(d_on * oh, axis=0, keepdims=True)
            d_oh = d_on * ng
            d_o = r * (d_oh - oh * jnp.mean(d_oh * oh, axis=1, keepdims=True))

            st16, ds16, do16, w16 = st.astype(BF16), ds_out.astype(BF16), d_o.astype(BF16), w.astype(BF16)
            q16, k16, t16 = cr["q16"], cr["k16"], cr["t16"]
            vn = u - _dot(w16, st16, 1, 0)
            d_vn = _dot(p, do16, 0, 0) + _dot(cr["kd"], ds16, 1, 0)
            d_qg = _dot(do16, st16, 1, 1)
            qgdo = _dot(cr["qg"], do16, 0, 0)
            yield
            vn16, dvn16 = vn.astype(BF16), d_vn.astype(BF16)
            d_p = jnp.where(cr["causal"], _dot(do16, vn16, 1, 1), 0.0)
            d_kd = _dot(vn16, ds16, 1, 1)
            d_gam_last = jnp.sum(jnp.sum(st * ds_out, axis=1, keepdims=True), axis=0, keepdims=True)
            d_w = -_dot(dvn16, st16, 1, 1)
            dstate[hh] = qgdo + ds_out * cr["gam_last"] - _dot(w16, dvn16, 0, 0)
            d_bv = _dot(t16, dvn16, 0, 0)
            yield
            d_kbg = _dot(t16, d_w, 0, 0)
            n_p = (d_p * dec).astype(BF16)
            d_q = _dot(n_p, k16, 1, 0) + d_qg * gam
            npq = _dot(n_p, q16, 0, 0)
            yield
            d_a = jnp.where(cr["strict"], -(_dot(d_bv, u, 1, 1) + _dot(d_kbg, w16, 1, 1)), 0.0)
            yield
            m_a = (d_a * dec).astype(BF16)
            d_kb = _dot(m_a, k16, 1, 0) + d_kbg * gam
            d_k = (_dot(m_a, cr["kb16"], 0, 0) + npq + d_kd * e + d_kb * b_col)
            yield
            d_v = d_bv * b_col
            d_beta_col = (jnp.sum(d_bv * v, axis=1, keepdims=True)
                          + jnp.sum(d_kb * k, axis=1, keepdims=True))
            gterm = d_a * cr["a_mat"] + d_p * p
            d_e = jnp.sum(d_kd * k, axis=1, keepdims=True) * e
            d_g_col = (jnp.sum(gterm, axis=1, keepdims=True)
                       + (jnp.sum(d_qg * q, axis=1, keepdims=True)
                          + jnp.sum(d_kbg * cr["kb"], axis=1, keepdims=True)) * gam
                       - d_e)
            d_g_last = jnp.sum(d_e, axis=0, keepdims=True) + d_gam_last * cr["gam_last"]
            d_g_row = (_col_to_row(d_g_col, eye) - jnp.sum(gterm, axis=0, keepdims=True)
                       + jnp.where(lane == A_CHUNK - 1, d_g_last, 0.0))
            dgc_ref[hh, seq_chunk, :] = d_g_row
            dbeta_ref[hh, seq_chunk, :] = _col_to_row(d_beta_col, eye)

            qh = cin["aq"] * cin["rq"]
            kh = cin["ak"] * cin["rk"]
            d_qh = d_q * (A_DK ** -0.5)
            d_aq = cin["rq"] * (d_qh - qh * jnp.sum(d_qh * qh, axis=1, keepdims=True))
            d_ak = cin["rk"] * (d_k - kh * jnp.sum(d_k * kh, axis=1, keepdims=True))
            d_y = jnp.concatenate([d_aq, d_ak, d_v], axis=1) * cin["slope"]
            shifted[hh, 0, 0:A_CHUNK, :] = d_y
            shifted[hh, 0, A_CHUNK:A_CHUNK + HALO, :] = dy_next[hh]
            shifted[hh, 1, 0:A_CHUNK + HALO, :] = cin["xw"]
            d_x = cw[A_CONV - 1:A_CONV, :] * d_y
            for j in range(1, A_CONV):
                d_x = d_x + cw[A_CONV - 1 - j:A_CONV - j, :] * shifted[hh, 0, j:j + A_CHUNK, :]
            for j in range(A_CONV):
                xs = shifted[hh, 1, HALO - j:HALO - j + A_CHUNK, :]
                dcw_ref[A_CONV - 1 - j:A_CONV - j, ccols] += jnp.sum(d_y * xs, axis=0, keepdims=True)
            dy_next[hh] = d_y[0:HALO, :]
            dx_ref[rows, lo:lo + A_CONV_COLS] = d_x.astype(BF16)
            dx_ref[rows, lo + A_CONV_COLS:lo + A_HEAD_COLS] = dz.astype(BF16)

        def chunk(ci, first):
            rows = pl.ds(0 if first else pl.multiple_of(ci * A_CHUNK, A_CHUNK), A_CHUNK)
            _round_robin([one_head(hh, ci, first, rows) for hh in range(hp)])

        lax.fori_loop(0, A_BLK_CHUNKS - 1, lambda i, c: (chunk(A_BLK_CHUNKS - 1 - i, False), c)[1], 0)
        chunk(0, True)

    rev = lambda j: nsb - 1 - j
    small = pl.BlockSpec((None, hp, n, A_CHUNK), lambda b, h, j: (b, h, 0, 0))
    wide = pl.BlockSpec((None, A_SEQ_BLK, hp * A_HEAD_COLS), lambda b, h, j: (b, rev(j), h))
    val = pl.BlockSpec((None, A_SEQ_BLK, hp * A_DV), lambda b, h, j: (b, rev(j), h))
    return pl.pallas_call(
        body, name="gdn_bwd", grid=(bn, A_HEADS // hp, nsb),
        in_specs=[wide,
                  pl.BlockSpec((None, None, HALO, hp * A_HEAD_COLS), lambda b, h, j: (b, rev(j), 0, h)),
                  pl.BlockSpec((A_CONV, hp * A_CONV_COLS), lambda b, h, j: (0, h)),
                  small, small,
                  pl.BlockSpec((1, A_DV), lambda b, h, j: (0, 0)),
                  val,
                  pl.BlockSpec((None, hp, A_BLK_CHUNKS, A_DK, A_DV), lambda b, h, j: (b, h, rev(j), 0, 0)),
                  pl.BlockSpec((None, hp, A_BLK_CHUNKS, A_CHUNK, A_CHUNK), lambda b, h, j: (b, h, rev(j), 0, 0)),
                  pl.BlockSpec((None, A_SEQ_BLK, hp * A_CONV_COLS), lambda b, h, j: (b, rev(j), h)),
                  val],
        out_specs=[wide, small, small,
                   pl.BlockSpec((None, A_CONV, hp * A_CONV_COLS), lambda b, h, j: (b, 0, h)),
                   pl.BlockSpec((None, hp, 8, A_DV), lambda b, h, j: (b, h, 0, 0))],
        out_shape=[jax.ShapeDtypeStruct((bn, s, A_HEADS * A_HEAD_COLS), BF16),
                   jax.ShapeDtypeStruct((bn, A_HEADS, n, A_CHUNK), F32),
                   jax.ShapeDtypeStruct((bn, A_HEADS, n, A_CHUNK), F32),
                   jax.ShapeDtypeStruct((bn, A_CONV, A_HEADS * A_CONV_COLS), F32),
                   jax.ShapeDtypeStruct((bn, A_HEADS, 8, A_DV), F32)],
        scratch_shapes=[pltpu.VMEM((hp, A_DK, A_DV), F32), pltpu.VMEM((hp, HALO, A_CONV_COLS), F32),
                        pltpu.VMEM((hp, 2, A_CHUNK + 2 * HALO, A_CONV_COLS), F32)],
        compiler_params=_params(("parallel", "parallel", "arbitrary"), VMEM_BIG),
    )(proj_hm, halo, cw_hm, beta, gc, norm_g, oraw, states, t_mats, conv_y, dog)


def _rope_tables(posf, inv_freq_row):
    t = posf.shape[0]
    tm = 512

    def body(p_ref, f_ref, c_ref, sa_ref, sb_ref):
        ang = p_ref[...] * f_ref[...]
        lane = lax.broadcasted_iota(jnp.int32, ang.shape, 1)
        half = ROPE_DIMS // 2
        c_ref[...] = jnp.where(lane < ROPE_DIMS, jnp.cos(ang), 1.0)
        sn = jnp.sin(ang)
        sa_ref[...] = jnp.where(lane < half, -sn, 0.0)
        sb_ref[...] = jnp.where((lane >= half) & (lane < ROPE_DIMS), sn, 0.0)

    row = pl.BlockSpec((tm, 128), lambda i: (i, 0))
    return pl.pallas_call(
        body, name="rope_tables", grid=(t // tm,),
        in_specs=[row, pl.BlockSpec((1, 128), lambda i: (0, 0))], out_specs=[row] * 3,
        out_shape=[jax.ShapeDtypeStruct((t, 128), F32)] * 3,
        compiler_params=_params(("parallel",)),
    )(posf, inv_freq_row)


def _qk_prep(proj, c, sa, sb, qg, kg, name, tm=256):
    t = proj.shape[0]

    def body(x_ref, c_ref, sa_ref, sb_ref, qg_ref, kg_ref, o_ref):
        cc, s1, s2 = c_ref[...], sa_ref[...], sb_ref[...]
        half = ROPE_DIMS // 2

        def one_head(lo, g):
            xv = x_ref[:, lo:lo + B_DH].astype(F32)
            ms = jnp.mean(xv * xv, axis=1, keepdims=True)
            yield
            xn = xv * lax.rsqrt(ms + EPS) * g
            r1, r2 = pltpu.roll(xn, 128 - half, 1), pltpu.roll(xn, half, 1)
            yield
            o_ref[:, lo:lo + B_DH] = (xn * cc + r1 * s1 + r2 * s2).astype(BF16)

        for which, g_ref in ((0, qg_ref), (1, kg_ref)):
            g = g_ref[...]
            _round_robin([one_head(which * B_W + h * B_DH, g) for h in range(B_HEADS)])
        o_ref[:, 2 * B_W:3 * B_W] = x_ref[:, 2 * B_W:3 * B_W]

    tab = pl.BlockSpec((tm, 128), lambda i: (i, 0))
    gain = pl.BlockSpec((1, B_DH), lambda i: (0, 0))
    return pl.pallas_call(
        body, name=name, grid=(t // tm,),
        in_specs=[pl.BlockSpec((tm, 3 * B_W), lambda i: (i, 0)), tab, tab, tab, gain, gain],
        out_specs=pl.BlockSpec((tm, 3 * B_W), lambda i: (i, 0)),
        out_shape=jax.ShapeDtypeStruct((t, 3 * B_W), BF16),
        compiler_params=_params(("parallel",), 40 * 1024 * 1024),
    )(proj, c, sa, sb, qg, kg)


def _qk_prep_bwd(proj, c, sa, sb, qg, kg, dq, dk, dv, dz, name, tm=256):
    t = proj.shape[0]
    out_w = 3 * B_W + (B_W if dz is not None else 0)

    def body(*refs):
        x_ref, c_ref, sa_ref, sb_ref, qg_ref, kg_ref, dq_ref, dk_ref, dv_ref = refs[:9]
        if dz is not None:
            dz_ref, o_ref, dgain_ref = refs[9:]
        else:
            o_ref, dgain_ref = refs[9:]
        i = pl.program_id(0)

        @pl.when(i == 0)
        def _():
            dgain_ref[...] = jnp.zeros_like(dgain_ref)

        cc, s1, s2 = c_ref[...], sa_ref[...], sb_ref[...]
        half = ROPE_DIMS // 2

        def one_head(which, h, g, d_ref, parts):
            lo = which * B_W + h * B_DH
            xv = x_ref[:, lo:lo + B_DH].astype(F32)
            d_out = d_ref[:, h * B_DH:(h + 1) * B_DH].astype(F32)
            ms = jnp.mean(xv * xv, axis=1, keepdims=True)
            r1, r2 = pltpu.roll(d_out * s1, half, 1), pltpu.roll(d_out * s2, 128 - half, 1)
            yield
            r = lax.rsqrt(ms + EPS)
            xh = xv * r
            d_xn = d_out * cc + r1 + r2
            parts.append(jnp.sum(d_xn * xh, axis=0, keepdims=True))
            d_xh = d_xn * g
            dot = jnp.mean(d_xh * xh, axis=1, keepdims=True)
            yield
            o_ref[:, lo:lo + B_DH] = (r * (d_xh - xh * dot)).astype(BF16)

        for which, g_ref, d_ref in ((0, qg_ref, dq_ref), (1, kg_ref, dk_ref)):
            parts = []
            _round_robin([one_head(which, h, g_ref[...], d_ref, parts) for h in range(B_HEADS)])
            acc = parts[0]
            for part in parts[1:]:
                acc = acc + part
            dgain_ref[which:which + 1, :] += acc
        o_ref[:, 2 * B_W:3 * B_W] = dv_ref[...]
        if dz is not None:
            o_ref[:, 3 * B_W:4 * B_W] = dz_ref[...]

    tab = pl.BlockSpec((tm, 128), lambda i: (i, 0))
    gain = pl.BlockSpec((1, B_DH), lambda i: (0, 0))
    grad = pl.BlockSpec((tm, B_W), lambda i: (i, 0))
    in_specs = [pl.BlockSpec((tm, 2 * B_W), lambda i: (i, 0)), tab, tab, tab, gain, gain, grad, grad, grad]
    args = [proj, c, sa, sb, qg, kg, dq, dk, dv]
    if dz is not None:
        in_specs.append(grad)
        args.append(dz)
    return pl.pallas_call(
        body, name=name, grid=(t // tm,), in_specs=in_specs,
        out_specs=[pl.BlockSpec((tm, out_w), lambda i: (i, 0)), pl.BlockSpec((8, B_DH), lambda i: (0, 0))],
        out_shape=[jax.ShapeDtypeStruct((t, out_w), BF16), jax.ShapeDtypeStruct((8, B_DH), F32)],
        compiler_params=_params(("arbitrary",), 40 * 1024 * 1024),
    )(*args)


def _attn_masks():
    qi = lax.broadcasted_iota(jnp.int32, (B_BLK, 2 * B_BLK), 0)
    kj = lax.broadcasted_iota(jnp.int32, (B_BLK, 2 * B_BLK), 1)
    two = (kj >= qi) & (kj <= qi + B_BLK)
    q1 = lax.broadcasted_iota(jnp.int32, (B_BLK, B_BLK), 0)
    k1 = lax.broadcasted_iota(jnp.int32, (B_BLK, B_BLK), 1)
    return k1 <= q1, two


def _lane_pick(ref_rows, h):
    lane = lax.broadcasted_iota(jnp.int32, ref_rows.shape, 1)
    return jnp.sum(jnp.where(lane == h, ref_rows, 0.0), axis=1, keepdims=True)


B_ROWS = 2048


def _attn_schedule(nb, sb, block):
    way = 16

    def run(items):
        for at in range(0, len(items), way):
            _round_robin([block(*it) for it in items[at:at + way]])

    run([(si, 0, True) for si in range(sb)])
    if nb == 1:
        return
    per = max(1, way // sb)
    lead = 1 + (nb - 1) % per
    if lead > 1:
        run([(si, i, False) for i in range(1, lead) for si in range(sb)])

    def step(it, carry):
        run([(si, lead + it * per + u, False) for u in range(per) for si in range(sb)])
        return carry

    lax.fori_loop(0, (nb - lead) // per, step, 0)


def _attn_rows(i, first):
    if first:
        return pl.ds(0, B_BLK), pl.ds(0, B_BLK)
    rows = pl.ds(pl.multiple_of(i * B_BLK, B_BLK), B_BLK)
    return rows, pl.ds(pl.multiple_of((i - 1) * B_BLK, B_BLK), 2 * B_BLK)


def _attn_fwd(qkv, name):
    ns, ln, _ = qkv.shape
    nb = ln // B_BLK
    sb = B_ROWS // ln
    scale = B_DH ** -0.5

    def body(q_ref, k_ref, v_ref, o_ref, lse_ref):
        h = pl.program_id(1)
        mask1, mask2 = _attn_masks()
        lane = lax.broadcasted_iota(jnp.int32, (B_BLK, B_HEADS), 1)

        @pl.when(h == 0)
        def _():
            lse_ref[...] = jnp.zeros_like(lse_ref)

        def block(si, i, first):
            rows, win = _attn_rows(i, first)
            mask = mask1 if first else mask2
            sc = jnp.where(mask, _dot(q_ref[si, rows, :], k_ref[si, win, :], 1, 1) * scale, -1e30)
            yield
            m = jnp.max(sc, axis=1, keepdims=True)
            p = jnp.exp(sc - m)
            l = jnp.sum(p, axis=1, keepdims=True)
            pv = _dot(p, v_ref[si, win, :], 1, 0)
            yield
            o_ref[si, rows, :] = (pv / l).astype(BF16)
            lse_ref[si, rows, :] = jnp.where(lane == h, m + jnp.log(l), lse_ref[si, rows, :])

        _attn_schedule(nb, sb, block)

    head = lambda off: pl.BlockSpec((sb, ln, B_DH), lambda s, h: (s, 0, off + h))
    return pl.pallas_call(
        body, name=name, grid=(ns // sb, B_HEADS),
        in_specs=[head(0), head(B_HEADS), head(2 * B_HEADS)],
        out_specs=[head(0), pl.BlockSpec((sb, ln, B_HEADS), lambda s, h: (s, 0, 0))],
        out_shape=[jax.ShapeDtypeStruct((ns, ln, B_W), BF16), jax.ShapeDtypeStruct((ns, ln, B_HEADS), F32)],
        compiler_params=_params(("parallel", "arbitrary")),
    )(qkv, qkv, qkv)


def _attn_bwd(qkv, d_o, lse_joint, delta, name):
    ns, ln, _ = qkv.shape
    nb = ln // B_BLK
    sb = B_ROWS // ln
    scale = B_DH ** -0.5

    def body(q_ref, k_ref, v_ref, do_ref, lj_ref, dl_ref, dq_ref, dk_out, dv_out, dk_ref, dv_ref):
        h = pl.program_id(1)
        mask1, mask2 = _attn_masks()
        dk_ref[...] = jnp.zeros_like(dk_ref)
        dv_ref[...] = jnp.zeros_like(dv_ref)

        def block(si, i, first):
            rows, win = _attn_rows(i, first)
            mask = mask1 if first else mask2
            q = q_ref[si, rows, :]
            d_out = do_ref[si, rows, :]
            l_col = _lane_pick(lj_ref[si, rows, :], h)
            d_col = _lane_pick(dl_ref[si, rows, :], h)
            sc = _dot(q, k_ref[si, win, :], 1, 1) * scale
            d_p = _dot(d_out, v_ref[si, win, :], 1, 1)
            yield
            p = jnp.exp(jnp.where(mask, sc - l_col, -1e30))
            d_s = p * (d_p - d_col) * scale
            d_q = _dot(d_s, k_ref[si, win, :], 1, 0)
            d_k = _dot(d_s, q, 0, 0)
            d_v = _dot(p, d_out, 0, 0)
            yield
            dq_ref[si, rows, :] = d_q.astype(BF16)
            dk_ref[si, win, :] += d_k
            dv_ref[si, win, :] += d_v

        _attn_schedule(nb, sb, block)
        dk_out[...] = dk_ref[...].astype(BF16)
        dv_out[...] = dv_ref[...].astype(BF16)

    head = lambda off: pl.BlockSpec((sb, ln, B_DH), lambda s, h: (s, 0, off + h))
    small = pl.BlockSpec((sb, ln, B_HEADS), lambda s, h: (s, 0, 0))
    return pl.pallas_call(
        body, name=name, grid=(ns // sb, B_HEADS),
        in_specs=[head(0), head(B_HEADS), head(2 * B_HEADS), head(0), small, small],
        out_specs=[head(0)] * 3,
        out_shape=[jax.ShapeDtypeStruct((ns, ln, B_W), BF16)] * 3,
        scratch_shapes=[pltpu.VMEM((sb, ln, B_DH), F32)] * 2,
        compiler_params=_params(("parallel", "parallel")),
    )(qkv, qkv, qkv, d_o, lse_joint, delta)


def _merge_weights(lse_refs):
    ls = [r[...] for r in lse_refs]
    m = jnp.maximum(jnp.maximum(ls[0], ls[1]), ls[2])
    es = [jnp.exp(l - m) for l in ls]
    tot = es[0] + es[1] + es[2]
    return [e / tot for e in es], m + jnp.log(tot)


def _merge_fwd(outs, lses, proj0, tm=256):
    t = outs[0].shape[0]

    def body(o0, o1, o2, l0, l1, l2, z_ref, og_ref):
        wts, _ = _merge_weights((l0, l1, l2))

        def one_head(h):
            cols = slice(h * B_DH, (h + 1) * B_DH)
            w0, w1, w2 = (jnp.broadcast_to(w[:, h:h + 1], (tm, B_DH)) for w in wts)
            yield
            o = w0 * o0[:, cols] + w1 * o1[:, cols] + w2 * o2[:, cols]
            og_ref[:, cols] = (o * _silu(z_ref[:, cols].astype(F32))).astype(BF16)

        _round_robin([one_head(h) for h in range(B_HEADS)])

    wide = pl.BlockSpec((tm, B_W), lambda i: (i, 0))
    small = pl.BlockSpec((tm, B_HEADS), lambda i: (i, 0))
    return pl.pallas_call(
        body, name="merge_fwd", grid=(t // tm,),
        in_specs=[wide] * 3 + [small] * 3 + [pl.BlockSpec((tm, B_W), lambda i: (i, 3))],
        out_specs=wide, out_shape=jax.ShapeDtypeStruct((t, B_W), BF16),
        compiler_params=_params(("parallel",)),
    )(*outs, *lses, proj0)


def _merge_bwd(outs, lses, proj0, d_og, tm=256):
    t = outs[0].shape[0]

    def body(o0, o1, o2, l0, l1, l2, z_ref, dog_ref, do_ref, lj_ref, dl_ref, dz_ref):
        wts, lj = _merge_weights((l0, l1, l2))
        lj_ref[...] = lj
        lane = lax.broadcasted_iota(jnp.int32, (tm, B_HEADS), 1)
        sums = [None] * B_HEADS

        def one_head(h):
            cols = slice(h * B_DH, (h + 1) * B_DH)
            w0, w1, w2 = (jnp.broadcast_to(w[:, h:h + 1], (tm, B_DH)) for w in wts)
            yield
            o = w0 * o0[:, cols] + w1 * o1[:, cols] + w2 * o2[:, cols]
            z = z_ref[:, cols].astype(F32)
            d_g = dog_ref[:, cols].astype(F32)
            gate, gate_slope = _silu_and_slope(z)
            d_out = d_g * gate
            dz_ref[:, cols] = (d_g * o * gate_slope).astype(BF16)
            do_ref[:, cols] = d_out.astype(BF16)
            sums[h] = jnp.sum(d_out * o, axis=1, keepdims=True)
            yield

        _round_robin([one_head(h) for h in range(B_HEADS)])
        delta = jnp.zeros((tm, B_HEADS), F32)
        for h in range(B_HEADS):
            delta = jnp.where(lane == h, sums[h], delta)
        dl_ref[...] = delta

    wide = pl.BlockSpec((tm, B_W), lambda i: (i, 0))
    small = pl.BlockSpec((tm, B_HEADS), lambda i: (i, 0))
    return pl.pallas_call(
        body, name="merge_bwd", grid=(t // tm,),
        in_specs=[wide] * 3 + [small] * 3 + [pl.BlockSpec((tm, B_W), lambda i: (i, 3)), wide],
        out_specs=[wide, small, small, wide],
        out_shape=[jax.ShapeDtypeStruct((t, B_W), BF16), jax.ShapeDtypeStruct((t, B_HEADS), F32),
                   jax.ShapeDtypeStruct((t, B_HEADS), F32), jax.ShapeDtypeStruct((t, B_W), BF16)],
        compiler_params=_params(("parallel",)),
    )(*outs, *lses, proj0, d_og)


def _adamw(w, g, m, v, name):
    r, c = w.shape
    tr = r
    for cand in (256, 128, 64, 32, 16, 8):
        if r % cand == 0:
            tr = cand
            break

    def body(w_ref, g_ref, m_ref, v_ref, d_ref, nm_ref, nv_ref):
        gv = g_ref[...]
        nm = ADAM_B1 * m_ref[...] + (1.0 - ADAM_B1) * gv
        nv = ADAM_B2 * v_ref[...] + (1.0 - ADAM_B2) * (gv * gv)
        m_hat = nm / (1.0 - ADAM_B1 ** ADAM_STEP)
        v_hat = nv / (1.0 - ADAM_B2 ** ADAM_STEP)
        d_ref[...] = -ADAM_LR * (m_hat / (jnp.sqrt(v_hat) + ADAM_EPS) + ADAM_WD * w_ref[...])
        nm_ref[...] = nm
        nv_ref[...] = nv

    blk = pl.BlockSpec((tr, c), lambda i: (i, 0))
    return pl.pallas_call(
        body, name=name, grid=(r // tr,), in_specs=[blk] * 4, out_specs=[blk] * 3,
        out_shape=[jax.ShapeDtypeStruct((r, c), F32)] * 3,
        compiler_params=_params(("parallel",)),
    )(w, g, m, v)


def _adam_update(w, gv, m, v):
    nm = ADAM_B1 * m + (1.0 - ADAM_B1) * gv
    nv = ADAM_B2 * v + (1.0 - ADAM_B2) * (gv * gv)
    m_hat = nm / (1.0 - ADAM_B1 ** ADAM_STEP)
    v_hat = nv / (1.0 - ADAM_B2 ** ADAM_STEP)
    return -ADAM_LR * (m_hat / (jnp.sqrt(v_hat) + ADAM_EPS) + ADAM_WD * w), nm, nv


def _adamw_shard(w, mine, theirs, m, v, half_index, name, tr=128):
    _, r, c = w.shape
    nhb = (r // 2) // tr

    def body(c_ref, w_ref, mine_ref, theirs_ref, m_ref, v_ref, g_ref, d_ref, nm_ref, nv_ref):
        is_mine = (pl.program_id(0) // nhb) == c_ref[0]
        gv = jnp.where(is_mine, mine_ref[...], theirs_ref[...])
        d, nm, nv = _adam_update(w_ref[...], gv, m_ref[...], v_ref[...])
        g_ref[...] = gv
        d_ref[...] = d
        nm_ref[...] = nm
        nv_ref[...] = nv

    full = pl.BlockSpec((None, tr, c), lambda i, cc: (0, i, 0))
    half = pl.BlockSpec((tr, c), lambda i, cc: (i % nhb, 0))
    return pl.pallas_call(
        body, name=name,
        grid_spec=pltpu.PrefetchScalarGridSpec(
            num_scalar_prefetch=1, grid=(2 * nhb,),
            in_specs=[full, half, half, full, full], out_specs=[full] * 4),
        out_shape=[jax.ShapeDtypeStruct(w.shape, F32)] * 4,
        compiler_params=_params(("parallel",), 40 * 1024 * 1024),
    )(half_index, w, mine, theirs, m, v)


def _adamw_shard_cols(w, mine, theirs, m, v, half_index, name, steps=20):
    c, _, r = w.shape
    tc = c // steps
    assert tc * steps == c

    def body(c_ref, w_ref, mine_ref, theirs_ref, m_ref, v_ref, g_ref, d_ref, nm_ref, nv_ref):
        first = jnp.where(c_ref[0] == 0, mine_ref[...], theirs_ref[...])
        second = jnp.where(c_ref[0] == 0, theirs_ref[...], mine_ref[...])
        for lo, gv in ((0, first), (r // 2, second)):
            cols = slice(lo, lo + r // 2)
            d, nm, nv = _adam_update(w_ref[:, :, cols], gv, m_ref[:, :, cols], v_ref[:, :, cols])
            g_ref[:, :, cols] = gv
            d_ref[:, :, cols] = d
            nm_ref[:, :, cols] = nm
            nv_ref[:, :, cols] = nv

    full = pl.BlockSpec((tc, 1, r), lambda i, cc: (i, 0, 0))
    half = pl.BlockSpec((tc, 1, r // 2), lambda i, cc: (i, 0, 0))
    return pl.pallas_call(
        body, name=name,
        grid_spec=pltpu.PrefetchScalarGridSpec(
            num_scalar_prefetch=1, grid=(steps,),
            in_specs=[full, half, half, full, full], out_specs=[full] * 4),
        out_shape=[jax.ShapeDtypeStruct(w.shape, F32)] * 4,
        compiler_params=_params(("parallel",), 40 * 1024 * 1024),
    )(half_index, w, mine, theirs, m, v)


def _pair_sum(own, other, half_index, name, tr=256):
    _, r, c = own.shape
    rh = r // 2
    tr = min(tr, rh)
    nrb = rh // tr

    def body(c_ref, own_ref, oth_ref, out_ref):
        out_ref[...] = (own_ref[...] + oth_ref[...].astype(F32)).astype(BF16)

    return pl.pallas_call(
        body, name=name,
        grid_spec=pltpu.PrefetchScalarGridSpec(
            num_scalar_prefetch=1, grid=(N_CHIPS, nrb),
            in_specs=[pl.BlockSpec((None, tr, c), lambda k, i, cc: (k, cc[0] * nrb + i, 0)),
                      pl.BlockSpec((None, tr, c), lambda k, i, cc: (k, i, 0))],
            out_specs=pl.BlockSpec((None, tr, c), lambda k, i, cc: (k, i, 0))),
        out_shape=jax.ShapeDtypeStruct((N_CHIPS, rh, c), BF16),
        compiler_params=_params(("parallel", "parallel")),
    )(half_index, own, other)


def _chip_sum(sums, others, chip_index, name, tr=256):
    _, r, c = sums.shape
    tr = min(tr, r)

    def body(k_ref, own_ref, oth_ref, out_ref):
        acc = own_ref[...].astype(F32)
        for j in range(N_CHIPS - 1):
            acc = acc + oth_ref[j].astype(F32)
        out_ref[...] = acc

    return pl.pallas_call(
        body, name=name,
        grid_spec=pltpu.PrefetchScalarGridSpec(
            num_scalar_prefetch=1, grid=(r // tr,),
            in_specs=[pl.BlockSpec((None, tr, c), lambda i, kk: (kk[0], i, 0)),
                      pl.BlockSpec((N_CHIPS - 1, tr, c), lambda i, kk: (0, i, 0))],
            out_specs=pl.BlockSpec((tr, c), lambda i, kk: (i, 0))),
        out_shape=jax.ShapeDtypeStruct((r, c), F32),
        compiler_params=_params(("parallel",)),
    )(chip_index, sums, others)


HBM = pl.BlockSpec(memory_space=pltpu.HBM)


def _place():
    x, y, c = lax.axis_index("x"), lax.axis_index("y"), lax.axis_index("c")
    chips = [(1 - x, y), (x, 1 - y), (1 - x, 1 - y)]
    return x, y, c, chips


def _sibling_forward(land):
    def body(in_ref, out_ref, send, recv):
        x, y, c, chips = _place()
        rh = out_ref.shape[1] // 2
        cps = []
        for j, (px, py) in enumerate(chips):
            slot = out_ref.at[2 * px + py, pl.ds(c * rh, rh)]
            cp = pltpu.make_async_remote_copy(
                src_ref=slot, dst_ref=slot, send_sem=send.at[j], recv_sem=recv.at[j],
                device_id=(x, y, 1 - c), device_id_type=MESH)
            cp.start()
            cps.append(cp)
        for j, (px, py) in enumerate(chips):
            slot = out_ref.at[2 * px + py, pl.ds((1 - c) * rh, rh)]
            pltpu.make_async_remote_copy(
                src_ref=slot, dst_ref=slot, send_sem=send.at[j], recv_sem=recv.at[j],
                device_id=(x, y, 1 - c), device_id_type=MESH).wait_recv()
        for cp in cps:
            cp.wait_send()

    return pl.pallas_call(
        body, name="first_weights_sibling_forward", in_specs=[HBM], out_specs=HBM,
        out_shape=jax.ShapeDtypeStruct(land.shape, land.dtype), input_output_aliases={0: 0},
        scratch_shapes=[pltpu.SemaphoreType.DMA((3,)), pltpu.SemaphoreType.DMA((3,))],
    )(land)


def _sibling_swap(halves, name):
    na = len(halves)

    def body(*refs):
        ins, outs = refs[:na], refs[na:2 * na]
        send, recv = refs[2 * na:]
        x, y, c, _ = _place()
        cps = []
        for i in range(na):
            cp = pltpu.make_async_remote_copy(
                src_ref=ins[i], dst_ref=outs[i], send_sem=send.at[i], recv_sem=recv.at[i],
                device_id=(x, y, 1 - c), device_id_type=MESH)
            cp.start()
            cps.append(cp)
        for cp in cps:
            cp.wait()

    out_shape = [jax.ShapeDtypeStruct(h.shape, h.dtype) for h in halves]
    return pl.pallas_call(
        body, name=name, in_specs=[HBM] * na, out_specs=[HBM] * na, out_shape=out_shape,
        scratch_shapes=[pltpu.SemaphoreType.DMA((na,)), pltpu.SemaphoreType.DMA((na,))],
    )(*halves)


SEM = pl.BlockSpec(memory_space=pltpu.SEMAPHORE)
ANY = pl.BlockSpec(memory_space=pl.ANY)
EFFECT = pltpu.SideEffectType.DATAFLOW_SIDE_EFFECTING


def _split_copy_start(name, plan, srcs, lands, after):
    ns, nl = len(srcs), len(lands)

    def body(*refs):
        src_refs, land_refs = refs[:ns], refs[ns:ns + nl]
        send, recv = refs[ns + nl + 1], refs[ns + nl + 2]
        token = refs[-1]
        outgoing, _ = plan(src_refs, land_refs)
        for src, dst, dev, si, ri in outgoing:
            pltpu.make_async_remote_copy(src_ref=src, dst_ref=dst, send_sem=send.at[si], recv_sem=recv.at[ri],
                                         device_id=dev, device_id_type=MESH).start()
        token[...] = jnp.zeros_like(token)

    n_out, n_in = plan.counts
    thru = [pltpu.HBM(a.shape, a.dtype) for a in list(srcs) + list(lands)]
    res = pl.pallas_call(
        body, name=name,
        out_shape=[pltpu.SemaphoreType.DMA((n_out,)), pltpu.SemaphoreType.DMA((n_in,))] + thru
        + [jax.ShapeDtypeStruct((8, 128), F32)],
        in_specs=[HBM] * (ns + nl) + [ANY],
        out_specs=[SEM, SEM] + [HBM] * (ns + nl) + [pl.BlockSpec(memory_space=pltpu.VMEM)],
        input_output_aliases={i: 2 + i for i in range(ns + nl)},
        compiler_params=pltpu.CompilerParams(has_side_effects=EFFECT),
    )(*[pltpu.with_memory_space_constraint(a, pltpu.HBM) for a in list(srcs) + list(lands)], after)
    return res[0], res[1], res[2:2 + ns], res[2 + ns:2 + ns + nl], res[-1]


def _split_copy_wait(name, plan, send, recv, srcs, lands, after):
    ns, nl = len(srcs), len(lands)
    after = list(after) if isinstance(after, (list, tuple)) else [after]

    def body(*refs):
        src_refs, land_refs = refs[:ns], refs[ns:ns + nl]
        send_ref, recv_ref = refs[ns + nl], refs[ns + nl + 1]
        outgoing, arrivals = plan(src_refs, land_refs)
        for src, dst, dev, si, ri in outgoing:
            pltpu.make_async_remote_copy(src_ref=src, dst_ref=dst, send_sem=send_ref.at[si], recv_sem=recv_ref.at[ri],
                                         device_id=dev, device_id_type=MESH).wait_send()
        for view, ri in arrivals:
            pltpu.make_async_remote_copy(src_ref=view, dst_ref=view, send_sem=send_ref.at[0], recv_sem=recv_ref.at[ri],
                                         device_id=_place()[:3], device_id_type=MESH).wait_recv()

    thru = [pltpu.HBM(a.shape, a.dtype) for a in list(srcs) + list(lands)]
    res = pl.pallas_call(
        body, name=name, out_shape=thru,
        in_specs=[HBM] * (ns + nl) + [SEM, SEM] + [ANY] * len(after), out_specs=[HBM] * (ns + nl),
        input_output_aliases={i: i for i in range(ns + nl)},
        compiler_params=pltpu.CompilerParams(has_side_effects=EFFECT),
    )(*srcs, *lands, send, recv, *after)
    return res[:ns], res[ns:]


def _gather_plan(n_arrays):
    def plan(src_refs, land_refs):
        x, y, c, chips = _place()
        me = 2 * x + y
        outgoing, arrivals = [], []
        for i in range(n_arrays):
            rh = src_refs[i].shape[0] // 2
            mine = pl.ds(c * rh, rh)
            for j, (px, py) in enumerate(chips):
                for delta in range(2):
                    tc = c ^ delta
                    outgoing.append((src_refs[i].at[mine], land_refs[i].at[me, mine], (px, py, tc),
                                     6 * i + 2 * j + delta, 6 * i + 2 * j + delta))
                    theirs = pl.ds(tc * rh, rh)
                    arrivals.append((land_refs[i].at[2 * px + py, theirs], 6 * i + 2 * j + delta))
        return outgoing, arrivals

    plan.counts = (6 * n_arrays, 6 * n_arrays)
    return plan


def _first_gather_plan():
    def plan(src_refs, land_refs):
        x, y, c, chips = _place()
        me = 2 * x + y
        rh = src_refs[0].shape[0] // 2
        mine = pl.ds(c * rh, rh)
        outgoing, arrivals = [], []
        for j, (px, py) in enumerate(chips):
            outgoing.append((src_refs[0].at[mine], land_refs[0].at[me, mine], (px, py, c), j, j))
            arrivals.append((land_refs[0].at[2 * px + py, mine], j))
            outgoing.append((src_refs[1], land_refs[1].at[me], (px, py, c), 3 + j, 3 + j))
            arrivals.append((land_refs[1].at[2 * px + py], 3 + j))
        return outgoing, arrivals

    plan.counts = (6, 6)
    return plan


def _exchange_plan(n_arrays):
    def plan(src_refs, land_refs):
        x, y, c, chips = _place()
        outgoing, arrivals = [], []
        for i in range(n_arrays):
            for j, (px, py) in enumerate(chips):
                outgoing.append((src_refs[i].at[2 * px + py], land_refs[i].at[j], (px, py, c), 3 * i + j, 3 * i + j))
                arrivals.append((land_refs[i].at[j], 3 * i + j))
        return outgoing, arrivals

    plan.counts = (3 * n_arrays, 3 * n_arrays)
    return plan


def _small_allreduce(vec):
    r, cdim = vec.shape
    n_dev = 8

    def body(v_ref, out_ref, buf, send, recv):
        x, y, c, _ = _place()
        me = 4 * x + 2 * y + c
        buf[me] = v_ref[...]
        cps = []
        for k in range(1, n_dev):
            dx, dy, dc = (k >> 2) & 1, (k >> 1) & 1, k & 1
            peer = (x ^ dx, y ^ dy, c ^ dc)
            cp = pltpu.make_async_remote_copy(
                src_ref=v_ref, dst_ref=buf.at[me], send_sem=send.at[k - 1], recv_sem=recv.at[k - 1],
                device_id=peer, device_id_type=MESH)
            cp.start()
            cps.append(cp)
        for k in range(1, n_dev):
            dx, dy, dc = (k >> 2) & 1, (k >> 1) & 1, k & 1
            src = 4 * (x ^ dx) + 2 * (y ^ dy) + (c ^ dc)
            slot = buf.at[src]
            pltpu.make_async_remote_copy(
                src_ref=slot, dst_ref=slot, send_sem=send.at[k - 1], recv_sem=recv.at[k - 1],
                device_id=(x ^ dx, y ^ dy, c ^ dc), device_id_type=MESH).wait_recv()
        for cp in cps:
            cp.wait_send()
        acc = buf[0]
        for k in range(1, n_dev):
            acc = acc + buf[k]
        out_ref[...] = acc

    vm = pl.BlockSpec(memory_space=pltpu.VMEM)
    return pl.pallas_call(
        body, name="small_allreduce", in_specs=[vm], out_specs=vm,
        out_shape=jax.ShapeDtypeStruct((r, cdim), F32),
        scratch_shapes=[pltpu.VMEM((n_dev, r, cdim), F32), pltpu.SemaphoreType.DMA((n_dev - 1,)),
                        pltpu.SemaphoreType.DMA((n_dev - 1,))],
    )(vec)


def _a_cols_to_head_major(w):
    lead = w.shape[:-1]
    q = w[..., :A_QK].reshape(lead + (A_HEADS, A_DK))
    k = w[..., A_QK:2 * A_QK].reshape(lead + (A_HEADS, A_DK))
    v = w[..., 2 * A_QK:2 * A_QK + A_VW].reshape(lead + (A_HEADS, A_DV))
    z = w[..., 2 * A_QK + A_VW:].reshape(lead + (A_HEADS, A_DV))
    return jnp.concatenate([q, k, v, z], axis=-1).reshape(lead + (A_HEADS * A_HEAD_COLS,))


def _a_cols_from_head_major(w):
    lead = w.shape[:-1]
    w = w.reshape(lead + (A_HEADS, A_HEAD_COLS))
    parts = [w[..., :A_DK], w[..., A_DK:2 * A_DK], w[..., 2 * A_DK:2 * A_DK + A_DV], w[..., 2 * A_DK + A_DV:]]
    return jnp.concatenate([p.reshape(lead + (-1,)) for p in parts], axis=-1)


def _conv_cols_to_head_major(w):
    lead = w.shape[:-1]
    q = w[..., :A_QK].reshape(lead + (A_HEADS, A_DK))
    k = w[..., A_QK:2 * A_QK].reshape(lead + (A_HEADS, A_DK))
    v = w[..., 2 * A_QK:].reshape(lead + (A_HEADS, A_DV))
    return jnp.concatenate([q, k, v], axis=-1).reshape(lead + (A_HEADS * A_CONV_COLS,))


def _conv_cols_from_head_major(w):
    lead = w.shape[:-1]
    w = w.reshape(lead + (A_HEADS, A_CONV_COLS))
    parts = [w[..., :A_DK], w[..., A_DK:2 * A_DK], w[..., 2 * A_DK:]]
    return jnp.concatenate([p.reshape(lead + (-1,)) for p in parts], axis=-1)


def _to_stream(a, bn, d):
    rest = a.shape[1:]
    s = a.shape[0] // bn
    a = a.reshape((bn, s // d, d) + rest)
    a = jnp.swapaxes(a, 1, 2)
    return a.reshape((bn * d, s // d) + rest)


def _from_stream(a, bn, d):
    rest = a.shape[2:]
    ln = a.shape[1]
    a = a.reshape((bn, d, ln) + rest)
    a = jnp.swapaxes(a, 1, 2)
    return a.reshape((bn * ln * d,) + rest)


B_SUB = 512
B_SHARD_BLOCKS = (3 * B_GROUPS * B_W + B_W) // N_CHIPS // B_SUB


def _b_block(gi, jj):
    nb = (B_GROUPS * (jj // 2) + gi) * 2 + jj % 2
    return nb // B_SHARD_BLOCKS, nb % B_SHARD_BLOCKS


def _shard_major(g, ncols):
    r = g.shape[0]
    return jnp.swapaxes(g.reshape(r, N_CHIPS, ncols), 0, 1)


def _pack_rows(items):
    rows, offs = [], []
    at = 0
    for a in items:
        flat = a.reshape(-1).astype(F32)
        nr = -(-flat.shape[0] // 1024) * 8
        flat = jnp.pad(flat, (0, nr * 128 - flat.shape[0]))
        rows.append(flat.reshape(nr, 128))
        offs.append((at, nr, a.shape))
        at += nr
    return jnp.concatenate(rows, axis=0), offs


def _unpack_rows(packed, offs):
    out = []
    for at, nr, shape in offs:
        size = int(np.prod(shape)) if len(shape) else 1
        out.append(packed[at:at + nr].reshape(-1)[:size].reshape(shape))
    return out


def _local_step(x, positions, loss_target, norm_g, a_log, a_dt_bias, a_norm_g, b_q_norm_g, b_k_norm_g,
                start_token, first_weights, late_weights, b_grads_ready, a_grads_ready):
    bn, s, d = x.shape
    t = bn * s
    n_chunks = s // A_CHUNK
    x0 = x.reshape(t, d)
    h0 = _rms_fwd(x0, norm_g[0:1] + start_token, "rms0_fwd")
    inv_freq = ROPE_THETA ** (-jnp.arange(0, ROPE_DIMS, 2, dtype=F32) / ROPE_DIMS)
    freq_row = jnp.concatenate([inv_freq, inv_freq, jnp.zeros((128 - ROPE_DIMS,), F32)]).reshape(1, 128)
    posf = jnp.broadcast_to(positions.astype(F32).reshape(t, 1), (t, 128)) + start_token
    tabs = _rope_tables(posf, freq_row)
    tabs_s = [tabs if dil == 1 else [_to_stream(tb, bn, dil).reshape(t, 128) for tb in tabs] for dil in B_DIL]
    wa_in, conv_w, late_token = first_weights([h0] + [tb for ts in tabs_s for tb in ts])
    wa_main = _a_cols_to_head_major(wa_in[:, :A_MAIN])
    wa_tail = jnp.pad(wa_in[:, A_MAIN:], ((0, 0), (0, 128 - 2 * A_HEADS))) + late_token.astype(BF16)
    cw_hm = _conv_cols_to_head_major(conv_w)

    proj_a = _matmul(h0, wa_main, "nn", F32, "a_in_main")
    tail_a = _matmul(h0, wa_tail, "nn", F32, "a_in_tail")
    tail_t = jnp.swapaxes(tail_a[:, :2 * A_HEADS].reshape(bn, s, 2 * A_HEADS), 1, 2)
    tail_t = tail_t.reshape(bn, 2 * A_HEADS, n_chunks, A_CHUNK)
    beta, gc = _gdn_prep(tail_t, a_log[0], a_dt_bias[0])
    proj_a3 = proj_a.reshape(bn, s, A_MAIN)
    og_a, oraw_a, states, t_mats, conv_y = _gdn_fwd(proj_a3, cw_hm, beta, gc, a_norm_g)
    wa_out, wb_in, wb_out = late_weights(og_a)
    b_cols = [4 * B_W] + [3 * B_W] * (B_GROUPS - 1)
    x1, h1 = _out_proj(og_a.reshape(t, A_VW), wa_out, x0, "a_out", norm_g=norm_g[1:2])

    h1_s, proj_b, qkv_b, o_b, lse_b = [], [], [], [], []
    for gi, dil in enumerate(B_DIL):
        hs = h1 if dil == 1 else _to_stream(h1, bn, dil).reshape(t, d)
        ts = tabs_s[gi]
        pj = _matmul(hs, wb_in, "nn", BF16, f"b_in_g{gi}", tm=2048, tn=B_SUB, n=b_cols[gi], b_spec=pl.BlockSpec(
            (None, d, B_SUB), lambda i, j, kk, gi=gi: (_b_block(gi, j)[0], kk, _b_block(gi, j)[1])))
        qkv = _qk_prep(pj, *ts, b_q_norm_g[0, gi:gi + 1], b_k_norm_g[0, gi:gi + 1], f"qk_prep_g{gi}")
        o_s, lse_s = _attn_fwd(qkv.reshape(bn * dil, s // dil, 3 * B_W), f"attn_fwd_g{gi}")
        h1_s.append(hs), proj_b.append(pj), qkv_b.append(qkv)
        o_b.append(o_s.reshape(t, B_W) if dil == 1 else _from_stream(o_s, bn, dil))
        lse_b.append(lse_s.reshape(t, B_HEADS) if dil == 1 else _from_stream(lse_s, bn, dil))
    og_b = _merge_fwd(o_b, lse_b, proj_b[0])
    d_x2, loss_parts = _out_proj(og_b, wb_out, x1, "b_out_loss", target=loss_target.reshape(t, d))
    loss_local = jnp.sum(loss_parts)

    d_x2b = d_x2.astype(BF16)
    g_wb_out = _matmul(og_b, d_x2b, "tn", F32, "b_out_dw")
    d_og_b = _matmul(d_x2b, wb_out, "nt", BF16, "b_out_dx")
    d_o, lse_joint, delta, d_z = _merge_bwd(o_b, lse_b, proj_b[0], d_og_b)
    d_h1, g_qn, g_kn = [], [], []
    g_wb_in = lax.empty(wb_in.shape, F32)
    for gi, dil in enumerate(B_DIL):
        if dil == 1:
            do_s, lj_s, dl_s = d_o, lse_joint, delta
        else:
            do_s, lj_s, dl_s = (_to_stream(a, bn, dil).reshape(t, -1) for a in (d_o, lse_joint, delta))
        ns, ln = bn * dil, s // dil
        dq, dk, dv = _attn_bwd(qkv_b[gi].reshape(ns, ln, 3 * B_W), do_s.reshape(ns, ln, B_W),
                               lj_s.reshape(ns, ln, B_HEADS), dl_s.reshape(ns, ln, B_HEADS), f"attn_bwd_g{gi}")
        d_pj, d_gain = _qk_prep_bwd(proj_b[gi], *tabs_s[gi], b_q_norm_g[0, gi:gi + 1], b_k_norm_g[0, gi:gi + 1],
                                    dq.reshape(t, B_W), dk.reshape(t, B_W), dv.reshape(t, B_W),
                                    d_z if gi == 0 else None, f"qk_prep_bwd_g{gi}")
        g_wb_in = _matmul(h1_s[gi], d_pj, "tn", F32, f"b_in_dw_g{gi}", tn=B_SUB, tk=2048, into=(g_wb_in, pl.BlockSpec(
            (None, d, B_SUB), lambda i, j, kk, gi=gi: (_b_block(gi, j)[0], i, _b_block(gi, j)[1]))))
        dh = _matmul(d_pj, wb_in, "nt", BF16, f"b_in_dx_g{gi}", tm=2048, tk=B_SUB, n=d, b_spec=pl.BlockSpec(
            (None, d, B_SUB), lambda i, j, kk, gi=gi: (_b_block(gi, kk)[0], j, _b_block(gi, kk)[1])))
        d_h1.append(dh if dil == 1 else _from_stream(dh.reshape(ns, ln, d), bn, dil))
        g_qn.append(d_gain[0]), g_kn.append(d_gain[1])
    d_x1, g_norm1 = _rms_bwd(x1, norm_g[1:2], d_h1, d_x2, "rms1_bwd")

    d_x1b = d_x1.astype(BF16)
    g_wa_out = _matmul(og_a.reshape(t, A_VW), d_x1b, "tn", F32, "a_out_dw")
    b_token = b_grads_ready(g_wb_in, g_wb_out, g_wa_out)
    d_og_a = _matmul(d_x1b, wa_out, "nt", BF16, "a_out_dx")
    d_pa, d_gc, d_beta, d_cw, d_ng = _gdn_bwd(proj_a3, cw_hm, beta, gc, a_norm_g + b_token, oraw_a, states,
                                              t_mats, conv_y, d_og_a.reshape(bn, s, A_VW))
    d_tail_t, d_alog, d_dtb = _gdn_prep_bwd(tail_t, a_log[0], a_dt_bias[0], d_gc, d_beta)
    d_tail = jnp.swapaxes(d_tail_t.reshape(bn, 2 * A_HEADS, s), 1, 2).reshape(t, 2 * A_HEADS)
    d_tail = jnp.pad(d_tail, ((0, 0), (0, 128 - 2 * A_HEADS))).astype(BF16)
    d_pa = d_pa.reshape(t, A_MAIN)
    g_wa_main = _matmul(h0, d_pa, "tn", F32, "a_in_dw_main")
    g_wa_tail = _matmul(h0, d_tail, "tn", F32, "a_in_dw_tail")
    g_wa_in = jnp.concatenate([_a_cols_from_head_major(g_wa_main), g_wa_tail[:, :2 * A_HEADS]], axis=1)
    a_token = a_grads_ready(g_wa_in)
    d_h0t = _matmul(d_tail + a_token.astype(BF16), wa_tail, "nt", F32, "a_in_dx_tail")
    d_x0, g_norm0 = _in_proj_bwd(d_pa, wa_main, d_h0t, x0, norm_g[0:1], d_x1, "a_in_dx_rms0_bwd")

    gfull = {
        "norm_g": jnp.concatenate([g_norm0, g_norm1], axis=0), "a_w_in": g_wa_in,
        "a_conv_w": _conv_cols_from_head_major(jnp.sum(d_cw, axis=0)),
        "a_log": jnp.sum(d_alog[:, :, 0], axis=0), "a_dt_bias": jnp.sum(d_dtb[:, :, 0], axis=0),
        "a_norm_g": jnp.sum(d_ng[:, :, 0, :], axis=(0, 1)), "a_w_out": g_wa_out, "b_w_in": g_wb_in,
        "b_q_norm_g": jnp.stack(g_qn), "b_k_norm_g": jnp.stack(g_kn), "b_w_out": g_wb_out}
    return loss_local, d_x0.reshape(bn, s, d), gfull


def kernel(x, positions, norm_g, a_w_in, a_conv_w, a_log, a_dt_bias, a_norm_g, a_w_out, b_w_in, b_q_norm_g, b_k_norm_g, b_w_out, loss_target, m_norm_g, m_a_w_in, m_a_conv_w, m_a_log, m_a_dt_bias, m_a_norm_g, m_a_w_out, m_b_w_in, m_b_q_norm_g, m_b_k_norm_g, m_b_w_out, v_norm_g, v_a_w_in, v_a_conv_w, v_a_log, v_a_dt_bias, v_a_norm_g, v_a_w_out, v_b_w_in, v_b_q_norm_g, v_b_k_norm_g, v_b_w_out):
    d = x.shape[2]
    my_c = lax.axis_index("c")
    my_chip = 2 * lax.axis_index("x") + lax.axis_index("y")

    half_index = jnp.reshape(my_c, (1,)).astype(jnp.int32)
    chip_index = jnp.reshape(my_chip, (1,)).astype(jnp.int32)
    def landing(shard):
        return lax.dynamic_update_slice(lax.empty((N_CHIPS,) + shard.shape, shard.dtype), shard[None],
                                        (my_chip,) + (0,) * shard.ndim)

    first_shards = [a_w_in[0].astype(BF16), a_conv_w[0]]
    first_plan = _first_gather_plan()
    first = _split_copy_start("first_weights_start", first_plan, first_shards,
                              [landing(s) for s in first_shards], half_index)
    pending = {}
    late_shards = [(w[0] + first[4][0, 0]).astype(BF16) for w in (a_w_out, b_w_in, b_w_out)]
    late_lands = [landing(s) for s in late_shards]

    def first_weights(after):
        _, (ga_in, g_conv) = _split_copy_wait("first_weights_wait", first_plan, *first[:4],
                                              list(after) + late_lands)
        ga_in = _sibling_forward(ga_in)
        wa_in = jnp.concatenate([ga_in[k] for k in range(N_CHIPS)], axis=1)
        conv_w = jnp.concatenate([g_conv[k] for k in range(N_CHIPS)], axis=1)
        plan = _gather_plan(len(late_shards))
        pending["late"] = (plan,) + tuple(_split_copy_start(
            "late_weights_start", plan, late_shards, late_lands, conv_w))
        return wa_in, conv_w, pending["late"][5][0, 0]

    def late_weights(after):
        plan, send, recv, srcs, lands, _ = pending["late"]
        _, (ga_out, gb_in, gb_out) = _split_copy_wait("late_weights_wait", plan, send, recv, srcs, lands, after)
        return ga_out.reshape(A_VW, d), gb_in, gb_out.reshape(B_W, d)

    def reduce_to_chip_sums(mats, tag):
        half = lambda g: lax.dynamic_slice_in_dim(g, (1 - my_c) * (g.shape[1] // 2), g.shape[1] // 2, axis=1)
        recv_sib = _sibling_swap([half(g).astype(BF16) for g in mats], f"grad_{tag}_sibling_swap")
        return [_pair_sum(g, r, half_index, f"grad_{tag}_pair_sum_{i}") for i, (g, r) in enumerate(zip(mats, recv_sib))]

    def start_exchange(tag, mats):
        sums = reduce_to_chip_sums(mats, tag)
        lands = [lax.empty((N_CHIPS - 1,) + s.shape[1:], BF16) for s in sums]
        plan = _exchange_plan(len(mats))
        pending[tag] = (plan,) + tuple(_split_copy_start(f"grad_{tag}_exchange_start", plan, sums, lands, chip_index))
        return pending[tag][5][0, 0]

    def finish_exchange(tag, after):
        plan, send, recv, srcs, lands, _ = pending[tag]
        return _split_copy_wait(f"grad_{tag}_exchange_wait", plan, send, recv, srcs, lands, after)

    def b_grads_ready(g_wb_in, g_wb_out, g_wa_out):
        return start_exchange("b", [g_wb_in, g_wb_out.reshape(N_CHIPS, -1, d), g_wa_out.reshape(N_CHIPS, -1, d)])

    def a_grads_ready(g_wa_in):
        return start_exchange("a", [_shard_major(g_wa_in, a_w_in.shape[2])])

    loss_local, d_x0, gfull = _local_step(x, positions, loss_target, norm_g, a_log, a_dt_bias, a_norm_g,
                                          b_q_norm_g, b_k_norm_g, first[4][0, 0], first_weights, late_weights,
                                          b_grads_ready, a_grads_ready)

    small = [gfull["norm_g"], gfull["a_conv_w"], gfull["a_log"], gfull["a_dt_bias"], gfull["a_norm_g"],
             gfull["b_q_norm_g"], gfull["b_k_norm_g"], loss_local]
    packed, offs = _pack_rows(small)
    reduced = _small_allreduce(packed)
    g_norm, g_conv_all, g_alog, g_dtb, g_ang, g_q, g_k, loss = _unpack_rows(reduced, offs)
    g_conv_mine = lax.dynamic_slice_in_dim(g_conv_all, my_chip * a_conv_w.shape[2], a_conv_w.shape[2], axis=1)

    b_sums, b_received = finish_exchange("b", d_x0)
    a_sums, a_received = finish_exchange("a", reduced)
    chip_sums = [a_sums[0], b_sums[2], b_sums[0], b_sums[1]]
    received = [a_received[0], b_received[2], b_received[0], b_received[1]]
    halves = [_chip_sum(s, r, chip_index, f"grad_chip_sum_{i}") for i, (s, r) in enumerate(zip(chip_sums, received))]
    theirs = _sibling_swap(halves, "grad_sibling_join")
    big = ("a_w_in", "a_w_out", "b_w_in", "b_w_out")
    big_halves = dict(zip(big, zip(halves, theirs)))

    grads = {
        "norm_g": g_norm, "a_conv_w": g_conv_mine[None], "a_log": g_alog[None], "a_dt_bias": g_dtb[None],
        "a_norm_g": g_ang[None], "b_q_norm_g": g_q[None], "b_k_norm_g": g_k[None]}
    weights = {"norm_g": norm_g, "a_w_in": a_w_in, "a_conv_w": a_conv_w, "a_log": a_log, "a_dt_bias": a_dt_bias,
               "a_norm_g": a_norm_g, "a_w_out": a_w_out, "b_w_in": b_w_in, "b_q_norm_g": b_q_norm_g,
               "b_k_norm_g": b_k_norm_g, "b_w_out": b_w_out}
    m_in = {"norm_g": m_norm_g, "a_w_in": m_a_w_in, "a_conv_w": m_a_conv_w, "a_log": m_a_log,
            "a_dt_bias": m_a_dt_bias, "a_norm_g": m_a_norm_g, "a_w_out": m_a_w_out, "b_w_in": m_b_w_in,
            "b_q_norm_g": m_b_q_norm_g, "b_k_norm_g": m_b_k_norm_g, "b_w_out": m_b_w_out}
    v_in = {"norm_g": v_norm_g, "a_w_in": v_a_w_in, "a_conv_w": v_a_conv_w, "a_log": v_a_log,
            "a_dt_bias": v_a_dt_bias, "a_norm_g": v_a_norm_g, "a_w_out": v_a_w_out, "b_w_in": v_b_w_in,
            "b_q_norm_g": v_b_q_norm_g, "b_k_norm_g": v_b_k_norm_g, "b_w_out": v_b_w_out}
    names = list(weights)

    delta_w, new_m, new_v = {}, {}, {}
    for nm in big:
        mine, other = big_halves[nm]
        if weights[nm].shape[2] % 128:
            cols = lambda a: jnp.transpose(a, (2, 0, 1))
            half_cols = lambda a: jnp.transpose(a)[:, None, :]
            outs = _adamw_shard_cols(cols(weights[nm]), half_cols(mine), half_cols(other), cols(m_in[nm]),
                                     cols(v_in[nm]), half_index, f"adamw_{nm}")
            outs = [jnp.transpose(o, (1, 2, 0)) for o in outs]
        else:
            outs = _adamw_shard(weights[nm], mine, other, m_in[nm], v_in[nm], half_index, f"adamw_{nm}")
        grads[nm], delta_w[nm], new_m[nm], new_v[nm] = outs
    small_names = [nm for nm in names if nm not in big]
    packs = [_pack_rows([src[nm] for nm in small_names]) for src in (weights, grads, m_in, v_in)]
    offs = packs[0][1]
    dl, m2, v2 = _adamw(packs[0][0], packs[1][0], packs[2][0], packs[3][0], "adamw_small")
    for nm, a, b, c2 in zip(small_names, _unpack_rows(dl, offs), _unpack_rows(m2, offs), _unpack_rows(v2, offs)):
        delta_w[nm], new_m[nm], new_v[nm] = a, b, c2

    return (loss, d_x0, *[grads[nm] for nm in names], *[delta_w[nm] for nm in names],
            *[new_m[nm] for nm in names], *[new_v[nm] for nm in names])
```

```python
import jax
import jax.numpy as jnp
import numpy as np
from jax import lax
from jax.experimental import pallas as pl
from jax.experimental.pallas import tpu as pltpu

F32 = jnp.float32
BF16 = jnp.bfloat16
MESH = pl.DeviceIdType.MESH

EPS = 1e-6
A_HEADS = 8
A_DK = 128
A_DV = 256
A_QK = A_HEADS * A_DK
A_VW = A_HEADS * A_DV
A_MAIN = 2 * A_QK + 2 * A_VW
A_HEAD_COLS = 2 * A_DK + 2 * A_DV
A_CONV_COLS = 2 * A_DK + A_DV
A_CHUNK = 64
A_CONV = 4
B_GROUPS = 3
B_HEADS = 8
B_DH = 128
B_W = B_HEADS * B_DH
B_DIL = (1, 4, 16)
B_BLK = 128
ROPE_THETA = 500000.0
ROPE_DIMS = B_DH // 4
ADAM_LR, ADAM_B1, ADAM_B2, ADAM_EPS, ADAM_WD, ADAM_STEP = 0.001, 0.9, 0.999, 1e-08, 0.01, 10
N_CHIPS = 4
VMEM_BIG = 56 * 1024 * 1024


def _params(sem=None, vmem=None):
    return pltpu.CompilerParams(dimension_semantics=sem, vmem_limit_bytes=vmem)


def _dot(a, b, ca, cb):
    return lax.dot_general(a.astype(BF16), b.astype(BF16), (((ca,), (cb,)), ((), ())),
                           preferred_element_type=F32)


def _split3(a):
    hi = a.astype(BF16)
    r = a - hi.astype(F32)
    mid = r.astype(BF16)
    lo = (r - mid.astype(F32)).astype(BF16)
    return hi, mid, lo


def _sigmoid(y):
    return 1.0 / (1.0 + jnp.exp(-y))


def _silu(y):
    return y * _sigmoid(y)


def _silu_and_slope(y):
    s = _sigmoid(y)
    return y * s, s * (1.0 + y * (1.0 - s))


def _matmul(a, b, mode, out_dtype, name, res=None, tm=1024, tn=1024, tk=1024, n=None, b_spec=None, into=None):
    m, k = a.shape[::-1] if mode == "tn" else a.shape
    if n is None:
        n = b.shape[0] if mode == "nt" else b.shape[1]
    tm, tn, tk = min(tm, m), min(tn, n), min(tk, k)
    assert m % tm == 0 and n % tn == 0 and k % tk == 0, (name, a.shape, b.shape)
    nk = k // tk
    dims = {"nn": ((1,), (0,)), "nt": ((1,), (1,)), "tn": ((0,), (0,))}[mode]

    def body(*refs):
        a_ref, b_ref = refs[0], refs[1]
        r_ref = refs[2] if res is not None else None
        o_ref = refs[2 + (res is not None) + (into is not None)]
        prod = lax.dot_general(a_ref[...], b_ref[...], (dims, ((), ())), preferred_element_type=F32)

        def finish(r):
            if res is not None:
                r = r + r_ref[...]
            o_ref[...] = r.astype(out_dtype)

        if nk == 1:
            finish(prod)
            return
        acc = refs[-1]
        kk = pl.program_id(2)

        @pl.when(kk == 0)
        def _():
            acc[...] = prod

        @pl.when((kk > 0) & (kk < nk - 1))
        def _():
            acc[...] += prod

        @pl.when(kk == nk - 1)
        def _():
            finish(acc[...] + prod)

    a_spec = pl.BlockSpec((tm, tk), lambda i, j, kk: (i, kk))
    if mode == "tn":
        a_spec = pl.BlockSpec((tk, tm), lambda i, j, kk: (kk, i))
    if b_spec is None and mode == "nt":
        b_spec = pl.BlockSpec((tn, tk), lambda i, j, kk: (j, kk))
    elif b_spec is None:
        b_spec = pl.BlockSpec((tk, tn), lambda i, j, kk: (kk, j))
    in_specs = [a_spec, b_spec]
    args = [a, b]
    if res is not None:
        in_specs.append(pl.BlockSpec((tm, tn), lambda i, j, kk: (i, j)))
        args.append(res)
    out_spec = pl.BlockSpec((tm, tn), lambda i, j, kk: (i, j))
    out_shape = jax.ShapeDtypeStruct((m, n), out_dtype)
    aliases = {}
    if into is not None:
        assert res is None
        buf, out_spec = into
        out_shape = jax.ShapeDtypeStruct(buf.shape, buf.dtype)
        in_specs.append(ANY)
        args.append(buf)
        aliases = {2: 0}
    return pl.pallas_call(
        body, name=name, grid=(m // tm, n // tn, nk),
        in_specs=in_specs, out_specs=out_spec, out_shape=out_shape, input_output_aliases=aliases,
        scratch_shapes=[pltpu.VMEM((tm, tn), F32)] if nk > 1 else [],
        compiler_params=_params(("parallel", "parallel", "arbitrary"), 48 * 1024 * 1024),
    )(*args)


def _rms_fwd(x, g, name, tm=512):
    t, d = x.shape

    def body(x_ref, g_ref, h_ref):
        xv = x_ref[...]
        r = lax.rsqrt(jnp.mean(xv * xv, axis=-1, keepdims=True) + EPS)
        h_ref[...] = (xv * r * g_ref[...]).astype(BF16)

    return pl.pallas_call(
        body, name=name, grid=(t // tm,),
        in_specs=[pl.BlockSpec((tm, d), lambda i: (i, 0)), pl.BlockSpec((1, d), lambda i: (0, 0))],
        out_specs=pl.BlockSpec((tm, d), lambda i: (i, 0)),
        out_shape=jax.ShapeDtypeStruct((t, d), BF16),
        compiler_params=_params(("parallel",)),
    )(x, g)


def _rms_bwd(x, g, dhs, dres, name, tm=512):
    t, d = x.shape
    n_dh = len(dhs)

    def body(*refs):
        x_ref, g_ref = refs[0], refs[1]
        dh_refs = refs[2:2 + n_dh]
        dres_ref, dx_ref, dg_ref = refs[2 + n_dh:]
        i = pl.program_id(0)

        @pl.when(i == 0)
        def _():
            dg_ref[...] = jnp.zeros_like(dg_ref)

        xv = x_ref[...]
        r = lax.rsqrt(jnp.mean(xv * xv, axis=-1, keepdims=True) + EPS)
        xh = xv * r
        dh = dh_refs[0][...].astype(F32)
        for ref in dh_refs[1:]:
            dh = dh + ref[...].astype(F32)
        dg_ref[0:1, :] += jnp.sum(dh * xh, axis=0, keepdims=True)
        dxh = dh * g_ref[...]
        dx = r * (dxh - xh * jnp.mean(dxh * xh, axis=-1, keepdims=True))
        dx_ref[...] = dx + dres_ref[...]

    row = pl.BlockSpec((tm, d), lambda i: (i, 0))
    dx, dg = pl.pallas_call(
        body, name=name, grid=(t // tm,),
        in_specs=[row, pl.BlockSpec((1, d), lambda i: (0, 0))] + [row] * n_dh + [row],
        out_specs=[row, pl.BlockSpec((8, d), lambda i: (0, 0))],
        out_shape=[jax.ShapeDtypeStruct((t, d), F32), jax.ShapeDtypeStruct((8, d), F32)],
        compiler_params=_params(("arbitrary",)),
    )(x, g, *dhs, dres)
    return dx, dg[0:1]


def _in_proj_bwd(dp, w, dh_more, x, g, dres, name, tm=512, tk=1024):
    t, k = dp.shape
    d = w.shape[0]
    nk = k // tk

    def body(dp_ref, w_ref, more_ref, x_ref, g_ref, dres_ref, dx_ref, dg_ref, acc):
        i, kk = pl.program_id(0), pl.program_id(1)

        @pl.when((i == 0) & (kk == 0))
        def _():
            dg_ref[...] = jnp.zeros_like(dg_ref)

        prod = lax.dot_general(dp_ref[...], w_ref[...], (((1,), (1,)), ((), ())), preferred_element_type=F32)

        @pl.when(kk == 0)
        def _():
            acc[...] = prod

        @pl.when((kk > 0) & (kk < nk - 1))
        def _():
            acc[...] += prod

        @pl.when(kk == nk - 1)
        def _():
            dh = acc[...] + prod + more_ref[...]
            xv = x_ref[...]
            r = lax.rsqrt(jnp.mean(xv * xv, axis=-1, keepdims=True) + EPS)
            xh = xv * r
            dg_ref[0:1, :] += jnp.sum(dh * xh, axis=0, keepdims=True)
            dxh = dh * g_ref[...]
            dx_ref[...] = r * (dxh - xh * jnp.mean(dxh * xh, axis=-1, keepdims=True)) + dres_ref[...]

    row = pl.BlockSpec((tm, d), lambda i, kk: (i, 0))
    dx, dg = pl.pallas_call(
        body, name=name, grid=(t // tm, nk),
        in_specs=[pl.BlockSpec((tm, tk), lambda i, kk: (i, kk)), pl.BlockSpec((d, tk), lambda i, kk: (0, kk)),
                  row, row, pl.BlockSpec((1, d), lambda i, kk: (0, 0)), row],
        out_specs=[row, pl.BlockSpec((8, d), lambda i, kk: (0, 0))],
        out_shape=[jax.ShapeDtypeStruct((t, d), F32), jax.ShapeDtypeStruct((8, d), F32)],
        scratch_shapes=[pltpu.VMEM((tm, d), F32)],
        compiler_params=_params(("arbitrary", "arbitrary"), 48 * 1024 * 1024),
    )(dp, w, dh_more, x, g, dres)
    return dx, dg[0:1]


def _out_proj(a, w, res, name, norm_g=None, target=None, tm=512):
    t, k = a.shape
    d = w.shape[1]
    nb = t // tm

    def body(a_ref, w_ref, r_ref, x_ref, o1_ref, o2_ref):
        y = jnp.dot(a_ref[...], w_ref[...], preferred_element_type=F32) + r_ref[...]
        if norm_g is not None:
            o1_ref[...] = y
            r = lax.rsqrt(jnp.mean(y * y, axis=-1, keepdims=True) + EPS)
            o2_ref[...] = (y * r * x_ref[...]).astype(BF16)
        else:
            e = y - x_ref[...]
            o1_ref[...] = e * (1.0 / d)
            s = jnp.sum(jnp.sum(e * e, axis=1, keepdims=True), axis=0, keepdims=True) * (0.5 / d)
            o2_ref[...] = jnp.broadcast_to(s, (8, 128))

    row = pl.BlockSpec((tm, d), lambda i: (i, 0))
    if norm_g is not None:
        extra, extra_spec = norm_g, pl.BlockSpec((1, d), lambda i: (0, 0))
        out2_spec, out2_shape = row, jax.ShapeDtypeStruct((t, d), BF16)
    else:
        extra, extra_spec = target, row
        out2_spec = pl.BlockSpec((None, 8, 128), lambda i: (i, 0, 0))
        out2_shape = jax.ShapeDtypeStruct((nb, 8, 128), F32)
    o1, o2 = pl.pallas_call(
        body, name=name, grid=(nb,),
        in_specs=[pl.BlockSpec((tm, k), lambda i: (i, 0)), pl.BlockSpec((k, d), lambda i: (0, 0)), row, extra_spec],
        out_specs=[row, out2_spec], out_shape=[jax.ShapeDtypeStruct((t, d), F32), out2_shape],
        compiler_params=_params(("parallel",), 48 * 1024 * 1024),
    )(a, w, res, extra)
    return (o1, o2) if norm_g is not None else (o1, o2[:, 0, 0])


def _softplus(x):
    t = jnp.exp(-jnp.abs(x))
    return jnp.maximum(x, 0.0) + jnp.where(t < 1e-3, t * (1.0 - 0.5 * t), jnp.log(1.0 + t))


def _tri(rows_le_cols):
    r = lax.broadcasted_iota(jnp.int32, (A_CHUNK, A_CHUNK), 0)
    c = lax.broadcasted_iota(jnp.int32, (A_CHUNK, A_CHUNK), 1)
    return jnp.where((r <= c) if rows_le_cols else (r >= c), 1.0, 0.0).astype(BF16)


def _dot_exact_rhs(a, ones_bf16):
    dn = (((1,), (0,)), ((), ()))
    hi, mid, lo = _split3(a)
    out = lax.dot_general(hi, ones_bf16, dn, preferred_element_type=F32)
    out = out + lax.dot_general(mid, ones_bf16, dn, preferred_element_type=F32)
    return out + lax.dot_general(lo, ones_bf16, dn, preferred_element_type=F32)


def _gdn_prep(tail_t, a_log, dt_bias):
    bn, _, n, c = tail_t.shape

    def body(t_ref, alog_ref, dtb_ref, beta_ref, gc_ref):
        upper = _tri(True)
        for h in range(A_HEADS):
            beta_ref[h] = _sigmoid(t_ref[h])
            ea = jnp.exp(jnp.full((n, c), alog_ref[h], F32))
            g = -ea * _softplus(t_ref[A_HEADS + h] + dtb_ref[h])
            gc_ref[h] = _dot_exact_rhs(g, upper)

    smem = pl.BlockSpec(memory_space=pltpu.SMEM)
    blk = pl.BlockSpec((None, A_HEADS, n, c), lambda b: (b, 0, 0, 0))
    return pl.pallas_call(
        body, name="gdn_prep", grid=(bn,),
        in_specs=[pl.BlockSpec((None, 2 * A_HEADS, n, c), lambda b: (b, 0, 0, 0)), smem, smem],
        out_specs=[blk, blk],
        out_shape=[jax.ShapeDtypeStruct((bn, A_HEADS, n, c), F32)] * 2,
        compiler_params=_params(("parallel",)),
    )(tail_t, a_log, dt_bias)


def _gdn_prep_bwd(tail_t, a_log, dt_bias, d_gc, d_beta):
    bn, _, n, c = tail_t.shape

    def body(t_ref, alog_ref, dtb_ref, dgc_ref, dbeta_ref, dt_ref, dal_ref, ddt_ref):
        lower = _tri(False)
        for h in range(A_HEADS):
            beta = _sigmoid(t_ref[h])
            dt_ref[h] = dbeta_ref[h] * beta * (1.0 - beta)
            dg = _dot_exact_rhs(dgc_ref[h], lower)
            ea = jnp.exp(jnp.full((n, c), alog_ref[h], F32))
            xa = t_ref[A_HEADS + h] + dtb_ref[h]
            g = -ea * _softplus(xa)
            dxa = -ea * dg * _sigmoid(xa)
            dt_ref[A_HEADS + h] = dxa
            s1 = jnp.sum(jnp.sum(g * dg, axis=1, keepdims=True), axis=0, keepdims=True)
            s2 = jnp.sum(jnp.sum(dxa, axis=1, keepdims=True), axis=0, keepdims=True)
            dal_ref[h:h + 1, :] = jnp.broadcast_to(s1, (1, 128))
            ddt_ref[h:h + 1, :] = jnp.broadcast_to(s2, (1, 128))

    smem = pl.BlockSpec(memory_space=pltpu.SMEM)
    blk8 = pl.BlockSpec((None, A_HEADS, n, c), lambda b: (b, 0, 0, 0))
    blk16 = pl.BlockSpec((None, 2 * A_HEADS, n, c), lambda b: (b, 0, 0, 0))
    sm = pl.BlockSpec((None, A_HEADS, 128), lambda b: (b, 0, 0))
    return pl.pallas_call(
        body, name="gdn_prep_bwd", grid=(bn,),
        in_specs=[blk16, smem, smem, blk8, blk8],
        out_specs=[blk16, sm, sm],
        out_shape=[jax.ShapeDtypeStruct((bn, 2 * A_HEADS, n, c), F32),
                   jax.ShapeDtypeStruct((bn, A_HEADS, 128), F32),
                   jax.ShapeDtypeStruct((bn, A_HEADS, 128), F32)],
        compiler_params=_params(("parallel",)),
    )(tail_t, a_log, dt_bias, d_gc, d_beta)


HALO = 8


def _conv_taps(xw, w):
    y = w[A_CONV - 1:A_CONV, :] * xw
    for j in range(1, A_CONV):
        y = y + w[A_CONV - 1 - j:A_CONV - j, :] * pltpu.roll(xw, j, 0)
    return y[HALO:, :]


def _row_to_col(row, eye):
    c = eye.shape[0]
    return jnp.sum(jnp.where(eye, jnp.broadcast_to(row, (c, c)), 0.0), axis=1, keepdims=True)


def _col_to_row(col, eye):
    c = eye.shape[0]
    return jnp.sum(jnp.where(eye, jnp.broadcast_to(col, (c, c)), 0.0), axis=0, keepdims=True)


def _unit_lower_inverse(a, ri, ci):
    eye = jnp.where(ri == ci, 1.0, 0.0)
    a8 = jnp.where((ri >> 3) == (ci >> 3), a, 0.0)
    a2 = _dot(a8, a8, 1, 0)
    yield
    a4 = _dot(a2, a2, 1, 0)
    t = eye - a8
    t = t + _dot(t, a2, 1, 0)
    yield
    t = t + _dot(t, a4, 1, 0)
    yield
    for sh in (3, 4, 5):
        off = jnp.where(((ri >> (sh + 1)) == (ci >> (sh + 1))) & ((ri >> sh) != (ci >> sh)), a, 0.0)
        left = _dot(t, off, 1, 0)
        yield
        t = t - _dot(left, t, 1, 0)
        yield
    return t


def _round_robin(gens):
    live = list(gens)
    while live:
        nxt = []
        for g in live:
            try:
                next(g)
                nxt.append(g)
            except StopIteration:
                pass
        live = nxt


def _gdn_chunk_core(q, k, v, g_row, b_row, t_mat, ri, ci):
    eye = ri == ci
    g_col = _row_to_col(g_row, eye)
    b_col = _row_to_col(b_row, eye)
    causal = ri >= ci
    strict = ri > ci
    dec = jnp.where(causal, jnp.exp(jnp.where(causal, g_col - g_row, 0.0)), 0.0)
    gam = jnp.exp(g_col)
    g_last = g_row[:, A_CHUNK - 1:A_CHUNK]
    gam_last = jnp.exp(g_last)
    e = jnp.exp(g_last - g_col)
    kb = k * b_col
    bv = v * b_col
    kbg = kb * gam
    q16, k16, kb16 = q.astype(BF16), k.astype(BF16), kb.astype(BF16)
    kk = _dot(kb16, k16, 1, 1)
    p = _dot(q16, k16, 1, 1) * dec
    yield
    a_mat = jnp.where(strict, kk * dec, 0.0)
    if t_mat is None:
        t_mat = yield from _unit_lower_inverse(a_mat, ri, ci)
    t16 = t_mat.astype(BF16)
    u = _dot(t16, bv, 1, 0)
    w = _dot(t16, kbg, 1, 0)
    yield
    return dict(eye=eye, g_col=g_col, b_col=b_col, dec=dec, strict=strict, causal=causal, gam=gam,
                gam_last=gam_last, e=e, kb=kb, bv=bv, kbg=kbg, a_mat=a_mat, t_mat=t_mat, u=u, w=w, p=p,
                qg=q * gam, kd=k * e, q16=q16, k16=k16, kb16=kb16, t16=t16)


A_SEQ_BLK = 256
A_BLK_CHUNKS = A_SEQ_BLK // A_CHUNK


def _gdn_halo(proj_hm):
    bn, s, w = proj_hm.shape
    last = proj_hm.reshape(bn, s // A_SEQ_BLK, A_SEQ_BLK, w)[:, :, A_SEQ_BLK - HALO:, :]
    return jnp.concatenate([jnp.zeros((bn, 1, HALO, w), proj_hm.dtype), last[:, :-1]], axis=1)


def _gdn_window(x_ref, halo_ref, ci, first, lo):
    if first:
        return jnp.concatenate([halo_ref[:, lo:lo + A_CONV_COLS], x_ref[0:A_CHUNK, lo:lo + A_CONV_COLS]], axis=0)
    start = pl.multiple_of(ci * A_CHUNK - HALO, HALO)
    return x_ref[pl.ds(start, A_CHUNK + HALO), lo:lo + A_CONV_COLS]


def _gdn_chunk_prep(xw, cw, y=None):
    if y is None:
        y = _conv_taps(xw, cw)
    a, slope = _silu_and_slope(y)
    aq, ak, v = a[:, 0:A_DK], a[:, A_DK:2 * A_DK], a[:, 2 * A_DK:]
    rq = lax.rsqrt(jnp.sum(aq * aq, axis=1, keepdims=True) + EPS)
    rk = lax.rsqrt(jnp.sum(ak * ak, axis=1, keepdims=True) + EPS)
    return dict(xw=xw, y=y, slope=slope, aq=aq, ak=ak, rq=rq, rk=rk, q=aq * rq * (A_DK ** -0.5), k=ak * rk, v=v)


def _gdn_fwd(proj_hm, cw_hm, beta, gc, norm_g, hp=8):
    bn, s, _ = proj_hm.shape
    n = s // A_CHUNK
    nsb = s // A_SEQ_BLK
    halo = _gdn_halo(proj_hm)

    def body(x_ref, halo_ref, cw_ref, beta_ref, gc_ref, ng_ref, og_ref, oraw_ref, st_ref, t_ref, y_ref, state):
        first_chunk = pl.program_id(2) * A_BLK_CHUNKS
        ri = lax.broadcasted_iota(jnp.int32, (A_CHUNK, A_CHUNK), 0)
        ci_ = lax.broadcasted_iota(jnp.int32, (A_CHUNK, A_CHUNK), 1)
        ng = ng_ref[...]

        @pl.when(pl.program_id(2) == 0)
        def _():
            state[...] = jnp.zeros_like(state)

        def one_head(hh, ci, first, rows):
            lo = hh * A_HEAD_COLS
            cw = cw_ref[:, hh * A_CONV_COLS:(hh + 1) * A_CONV_COLS]
            cin = _gdn_chunk_prep(_gdn_window(x_ref, halo_ref, ci, first, lo), cw)
            y_ref[rows, hh * A_CONV_COLS:(hh + 1) * A_CONV_COLS] = cin["y"]
            seq_chunk = pl.ds(first_chunk + ci, 1)
            core = yield from _gdn_chunk_core(cin["q"], cin["k"], cin["v"], gc_ref[hh, seq_chunk, :],
                                              beta_ref[hh, seq_chunk, :], None, ri, ci_)
            st = state[hh]
            st_ref[hh, ci] = st
            t_ref[hh, ci] = core["t_mat"]
            st16 = st.astype(BF16)
            vn = core["u"] - _dot(core["w"], st16, 1, 0)
            qs = _dot(core["qg"], st16, 1, 0)
            yield
            vn16 = vn.astype(BF16)
            o = qs + _dot(core["p"], vn16, 1, 0)
            state[hh] = st * core["gam_last"] + _dot(core["kd"], vn16, 0, 0)
            yield
            ocols = slice(hh * A_DV, (hh + 1) * A_DV)
            oraw_ref[rows, ocols] = o
            r = lax.rsqrt(jnp.mean(o * o, axis=1, keepdims=True) + EPS)
            z = x_ref[rows, lo + A_CONV_COLS:lo + A_HEAD_COLS]
            og_ref[rows, ocols] = (o * r * ng * _silu(z)).astype(BF16)

        def chunk(ci, first):
            rows = pl.ds(0 if first else pl.multiple_of(ci * A_CHUNK, A_CHUNK), A_CHUNK)
            _round_robin([one_head(hh, ci, first, rows) for hh in range(hp)])

        chunk(0, True)
        lax.fori_loop(1, A_BLK_CHUNKS, lambda i, c: (chunk(i, False), c)[1], 0)

    small = pl.BlockSpec((None, hp, n, A_CHUNK), lambda b, h, j: (b, h, 0, 0))
    return pl.pallas_call(
        body, name="gdn_fwd", grid=(bn, A_HEADS // hp, nsb),
        in_specs=[pl.BlockSpec((None, A_SEQ_BLK, hp * A_HEAD_COLS), lambda b, h, j: (b, j, h)),
                  pl.BlockSpec((None, None, HALO, hp * A_HEAD_COLS), lambda b, h, j: (b, j, 0, h)),
                  pl.BlockSpec((A_CONV, hp * A_CONV_COLS), lambda b, h, j: (0, h)),
                  small, small,
                  pl.BlockSpec((1, A_DV), lambda b, h, j: (0, 0))],
        out_specs=[pl.BlockSpec((None, A_SEQ_BLK, hp * A_DV), lambda b, h, j: (b, j, h)),
                   pl.BlockSpec((None, A_SEQ_BLK, hp * A_DV), lambda b, h, j: (b, j, h)),
                   pl.BlockSpec((None, hp, A_BLK_CHUNKS, A_DK, A_DV), lambda b, h, j: (b, h, j, 0, 0)),
                   pl.BlockSpec((None, hp, A_BLK_CHUNKS, A_CHUNK, A_CHUNK), lambda b, h, j: (b, h, j, 0, 0)),
                   pl.BlockSpec((None, A_SEQ_BLK, hp * A_CONV_COLS), lambda b, h, j: (b, j, h))],
        out_shape=[jax.ShapeDtypeStruct((bn, s, A_VW), BF16),
                   jax.ShapeDtypeStruct((bn, s, A_VW), F32),
                   jax.ShapeDtypeStruct((bn, A_HEADS, n, A_DK, A_DV), F32),
                   jax.ShapeDtypeStruct((bn, A_HEADS, n, A_CHUNK, A_CHUNK), F32),
                   jax.ShapeDtypeStruct((bn, s, A_HEADS * A_CONV_COLS), F32)],
        scratch_shapes=[pltpu.VMEM((hp, A_DK, A_DV), F32)],
        compiler_params=_params(("parallel", "parallel", "arbitrary"), VMEM_BIG),
    )(proj_hm, halo, cw_hm, beta, gc, norm_g)


def _gdn_bwd(proj_hm, cw_hm, beta, gc, norm_g, oraw, states, t_mats, conv_y, dog, hp=4):
    bn, s, _ = proj_hm.shape
    n = s // A_CHUNK
    nsb = s // A_SEQ_BLK
    halo = _gdn_halo(proj_hm)

    def body(x_ref, halo_ref, cw_ref, beta_ref, gc_ref, ng_ref, oraw_ref, st_ref, t_ref, y_ref, dog_ref,
             dx_ref, dgc_ref, dbeta_ref, dcw_ref, dng_ref, dstate, dy_next, shifted):
        first_chunk = (nsb - 1 - pl.program_id(2)) * A_BLK_CHUNKS
        ri = lax.broadcasted_iota(jnp.int32, (A_CHUNK, A_CHUNK), 0)
        ci_ = lax.broadcasted_iota(jnp.int32, (A_CHUNK, A_CHUNK), 1)
        lane = lax.broadcasted_iota(jnp.int32, (1, A_CHUNK), 1)
        ng = ng_ref[...]

        @pl.when(pl.program_id(2) == 0)
        def _():
            dstate[...] = jnp.zeros_like(dstate)
            dy_next[...] = jnp.zeros_like(dy_next)
            dcw_ref[...] = jnp.zeros_like(dcw_ref)
            dng_ref[...] = jnp.zeros_like(dng_ref)

        def one_head(hh, ci, first, rows):
            lo = hh * A_HEAD_COLS
            ccols = slice(hh * A_CONV_COLS, (hh + 1) * A_CONV_COLS)
            ocols = slice(hh * A_DV, (hh + 1) * A_DV)
            cw = cw_ref[:, ccols]
            cin = _gdn_chunk_prep(_gdn_window(x_ref, halo_ref, ci, first, lo), cw, y_ref[rows, ccols])
            q, k, v = cin["q"], cin["k"], cin["v"]
            seq_chunk = pl.ds(first_chunk + ci, 1)
            cr = yield from _gdn_chunk_core(q, k, v, gc_ref[hh, seq_chunk, :], beta_ref[hh, seq_chunk, :],
                                            t_ref[hh, ci], ri, ci_)
            eye, dec, gam, e = cr["eye"], cr["dec"], cr["gam"], cr["e"]
            b_col, t_mat, u, w, p = cr["b_col"], cr["t_mat"], cr["u"], cr["w"], cr["p"]
            st = st_ref[hh, ci]
            ds_out = dstate[hh]

            o = oraw_ref[rows, ocols]
            z = x_ref[rows, lo + A_CONV_COLS:lo + A_HEAD_COLS]
            d_og = dog_ref[rows, ocols].astype(F32)
            r = lax.rsqrt(jnp.mean(o * o, axis=1, keepdims=True) + EPS)
            oh = o * r
            gate, gate_slope = _silu_and_slope(z)
            d_on = d_og * gate
            dz = d_og * oh * ng * gate_slope
            dng_ref[hh, 0:1, :] += jnp.sum(d_on * oh, axis=0, keepdims=True)
            d_oh = d_on * ng
            d_o = r * (d_oh - oh * jnp.mean(d_oh * oh, axis=1, keepdims=True))

            st16, ds16, do16, w16 = st.astype(BF16), ds_out.astype(BF16), d_o.astype(BF16), w.astype(BF16)
            q16, k16, t16 = cr["q16"], cr["k16"], cr["t16"]
            vn = u - _dot(w16, st16, 1, 0)
            d_vn = _dot(p, do16, 0, 0) + _dot(cr["kd"], ds16, 1, 0)
            d_qg = _dot(do16, st16, 1, 1)
            qgdo = _dot(cr["qg"], do16, 0, 0)
            yield
            vn16, dvn16 = vn.astype(BF16), d_vn.astype(BF16)
            d_p = jnp.where(cr["causal"], _dot(do16, vn16, 1, 1), 0.0)
            d_kd = _dot(vn16, ds16, 1, 1)
            d_gam_last = jnp.sum(jnp.sum(st * ds_out, axis=1, keepdims=True), axis=0, keepdims=True)
            d_w = -_dot(dvn16, st16, 1, 1)
            dstate[hh] = qgdo + ds_out * cr["gam_last"] - _dot(w16, dvn16, 0, 0)
            d_bv = _dot(t16, dvn16, 0, 0)
            yield
            d_kbg = _dot(t16, d_w, 0, 0)
            n_p = (d_p * dec).astype(BF16)
            d_q = _dot(n_p, k16, 1, 0) + d_qg * gam
            npq = _dot(n_p, q16, 0, 0)
            yield
            d_a = jnp.where(cr["strict"], -(_dot(d_bv, u, 1, 1) + _dot(d_kbg, w16, 1, 1)), 0.0)
            yield
            m_a = (d_a * dec).astype(BF16)
            d_kb = _dot(m_a, k16, 1, 0) + d_kbg * gam
            d_k = (_dot(m_a, cr["kb16"], 0, 0) + npq + d_kd * e + d_kb * b_col)
            yield
            d_v = d_bv * b_col
            d_beta_col = (jnp.sum(d_bv * v, axis=1, keepdims=True)
                          + jnp.sum(d_kb * k, axis=1, keepdims=True))
            gterm = d_a * cr["a_mat"] + d_p * p
            d_e = jnp.sum(d_kd * k, axis=1, keepdims=True) * e
            d_g_col = (jnp.sum(gterm, axis=1, keepdims=True)
                       + (jnp.sum(d_qg * q, axis=1, keepdims=True)
                          + jnp.sum(d_kbg * cr["kb"], axis=1, keepdims=True)) * gam
                       - d_e)
            d_g_last = jnp.sum(d_e, axis=0, keepdims=True) + d_gam_last * cr["gam_last"]
            d_g_row = (_col_to_row(d_g_col, eye) - jnp.sum(gterm, axis=0, keepdims=True)
                       + jnp.where(lane == A_CHUNK - 1, d_g_last, 0.0))
            dgc_ref[hh, seq_chunk, :] = d_g_row
            dbeta_ref[hh, seq_chunk, :] = _col_to_row(d_beta_col, eye)

            qh = cin["aq"] * cin["rq"]
            kh = cin["ak"] * cin["rk"]
            d_qh = d_q * (A_DK ** -0.5)
            d_aq = cin["rq"] * (d_qh - qh * jnp.sum(d_qh * qh, axis=1, keepdims=True))
            d_ak = cin["rk"] * (d_k - kh * jnp.sum(d_k * kh, axis=1, keepdims=True))
            d_y = jnp.concatenate([d_aq, d_ak, d_v], axis=1) * cin["slope"]
            shifted[hh, 0, 0:A_CHUNK, :] = d_y
            shifted[hh, 0, A_CHUNK:A_CHUNK + HALO, :] = dy_next[hh]
            shifted[hh, 1, 0:A_CHUNK + HALO, :] = cin["xw"]
            d_x = cw[A_CONV - 1:A_CONV, :] * d_y
            for j in range(1, A_CONV):
                d_x = d_x + cw[A_CONV - 1 - j:A_CONV - j, :] * shifted[hh, 0, j:j + A_CHUNK, :]
            for j in range(A_CONV):
                xs = shifted[hh, 1, HALO - j:HALO - j + A_CHUNK, :]
                dcw_ref[A_CONV - 1 - j:A_CONV - j, ccols] += jnp.sum(d_y * xs, axis=0, keepdims=True)
            dy_next[hh] = d_y[0:HALO, :]
            dx_ref[rows, lo:lo + A_CONV_COLS] = d_x.astype(BF16)
            dx_ref[rows, lo + A_CONV_COLS:lo + A_HEAD_COLS] = dz.astype(BF16)

        def chunk(ci, first):
            rows = pl.ds(0 if first else pl.multiple_of(ci * A_CHUNK, A_CHUNK), A_CHUNK)
            _round_robin([one_head(hh, ci, first, rows) for hh in range(hp)])

        lax.fori_loop(0, A_BLK_CHUNKS - 1, lambda i, c: (chunk(A_BLK_CHUNKS - 1 - i, False), c)[1], 0)
        chunk(0, True)

    rev = lambda j: nsb - 1 - j
    small = pl.BlockSpec((None, hp, n, A_CHUNK), lambda b, h, j: (b, h, 0, 0))
    wide = pl.BlockSpec((None, A_SEQ_BLK, hp * A_HEAD_COLS), lambda b, h, j: (b, rev(j), h))
    val = pl.BlockSpec((None, A_SEQ_BLK, hp * A_DV), lambda b, h, j: (b, rev(j), h))
    return pl.pallas_call(
        body, name="gdn_bwd", grid=(bn, A_HEADS // hp, nsb),
        in_specs=[wide,
                  pl.BlockSpec((None, None, HALO, hp * A_HEAD_COLS), lambda b, h, j: (b, rev(j), 0, h)),
                  pl.BlockSpec((A_CONV, hp * A_CONV_COLS), lambda b, h, j: (0, h)),
                  small, small,
                  pl.BlockSpec((1, A_DV), lambda b, h, j: (0, 0)),
                  val,
                  pl.BlockSpec((None, hp, A_BLK_CHUNKS, A_DK, A_DV), lambda b, h, j: (b, h, rev(j), 0, 0)),
                  pl.BlockSpec((None, hp, A_BLK_CHUNKS, A_CHUNK, A_CHUNK), lambda b, h, j: (b, h, rev(j), 0, 0)),
                  pl.BlockSpec((None, A_SEQ_BLK, hp * A_CONV_COLS), lambda b, h, j: (b, rev(j), h)),
                  val],
        out_specs=[wide, small, small,
                   pl.BlockSpec((None, A_CONV, hp * A_CONV_COLS), lambda b, h, j: (b, 0, h)),
                   pl.BlockSpec((None, hp, 8, A_DV), lambda b, h, j: (b, h, 0, 0))],
        out_shape=[jax.ShapeDtypeStruct((bn, s, A_HEADS * A_HEAD_COLS), BF16),
                   jax.ShapeDtypeStruct((bn, A_HEADS, n, A_CHUNK), F32),
                   jax.ShapeDtypeStruct((bn, A_HEADS, n, A_CHUNK), F32),
                   jax.ShapeDtypeStruct((bn, A_CONV, A_HEADS * A_CONV_COLS), F32),
                   jax.ShapeDtypeStruct((bn, A_HEADS, 8, A_DV), F32)],
        scratch_shapes=[pltpu.VMEM((hp, A_DK, A_DV), F32), pltpu.VMEM((hp, HALO, A_CONV_COLS), F32),
                        pltpu.VMEM((hp, 2, A_CHUNK + 2 * HALO, A_CONV_COLS), F32)],
        compiler_params=_params(("parallel", "parallel", "arbitrary"), VMEM_BIG),
    )(proj_hm, halo, cw_hm, beta, gc, norm_g, oraw, states, t_mats, conv_y, dog)


def _rope_tables(posf, inv_freq_row):
    t = posf.shape[0]
    tm = 512

    def body(p_ref, f_ref, c_ref, sa_ref, sb_ref):
        ang = p_ref[...] * f_ref[...]
        lane = lax.broadcasted_iota(jnp.int32, ang.shape, 1)
        half = ROPE_DIMS // 2
        c_ref[...] = jnp.where(lane < ROPE_DIMS, jnp.cos(ang), 1.0)
        sn = jnp.sin(ang)
        sa_ref[...] = jnp.where(lane < half, -sn, 0.0)
        sb_ref[...] = jnp.where((lane >= half) & (lane < ROPE_DIMS), sn, 0.0)

    row = pl.BlockSpec((tm, 128), lambda i: (i, 0))
    return pl.pallas_call(
        body, name="rope_tables", grid=(t // tm,),
        in_specs=[row, pl.BlockSpec((1, 128), lambda i: (0, 0))], out_specs=[row] * 3,
        out_shape=[jax.ShapeDtypeStruct((t, 128), F32)] * 3,
        compiler_params=_params(("parallel",)),
    )(posf, inv_freq_row)


def _qk_prep(proj, c, sa, sb, qg, kg, name, tm=512):
    t = proj.shape[0]

    def body(x_ref, c_ref, sa_ref, sb_ref, qg_ref, kg_ref, o_ref):
        cc, s1, s2 = c_ref[...], sa_ref[...], sb_ref[...]
        half = ROPE_DIMS // 2

        def one_head(lo, g):
            xv = x_ref[:, lo:lo + B_DH].astype(F32)
            ms = jnp.mean(xv * xv, axis=1, keepdims=True)
            yield
            xn = xv * lax.rsqrt(ms + EPS) * g
            r1, r2 = pltpu.roll(xn, 128 - half, 1), pltpu.roll(xn, half, 1)
            yield
            o_ref[:, lo:lo + B_DH] = (xn * cc + r1 * s1 + r2 * s2).astype(BF16)

        for which, g_ref in ((0, qg_ref), (1, kg_ref)):
            g = g_ref[...]
            _round_robin([one_head(which * B_W + h * B_DH, g) for h in range(B_HEADS)])
        o_ref[:, 2 * B_W:3 * B_W] = x_ref[:, 2 * B_W:3 * B_W]

    tab = pl.BlockSpec((tm, 128), lambda i: (i, 0))
    gain = pl.BlockSpec((1, B_DH), lambda i: (0, 0))
    return pl.pallas_call(
        body, name=name, grid=(t // tm,),
        in_specs=[pl.BlockSpec((tm, 3 * B_W), lambda i: (i, 0)), tab, tab, tab, gain, gain],
        out_specs=pl.BlockSpec((tm, 3 * B_W), lambda i: (i, 0)),
        out_shape=jax.ShapeDtypeStruct((t, 3 * B_W), BF16),
        compiler_params=_params(("parallel",), 40 * 1024 * 1024),
    )(proj, c, sa, sb, qg, kg)


def _qk_prep_bwd(proj, c, sa, sb, qg, kg, dq, dk, dv, dz, name, tm=512):
    t = proj.shape[0]
    out_w = 3 * B_W + (B_W if dz is not None else 0)

    def body(*refs):
        x_ref, c_ref, sa_ref, sb_ref, qg_ref, kg_ref, dq_ref, dk_ref, dv_ref = refs[:9]
        if dz is not None:
            dz_ref, o_ref, dgain_ref = refs[9:]
        else:
            o_ref, dgain_ref = refs[9:]
        i = pl.program_id(0)

        @pl.when(i == 0)
        def _():
            dgain_ref[...] = jnp.zeros_like(dgain_ref)

        cc, s1, s2 = c_ref[...], sa_ref[...], sb_ref[...]
        half = ROPE_DIMS // 2

        def one_head(which, h, g, d_ref, parts):
            lo = which * B_W + h * B_DH
            xv = x_ref[:, lo:lo + B_DH].astype(F32)
            d_out = d_ref[:, h * B_DH:(h + 1) * B_DH].astype(F32)
            ms = jnp.mean(xv * xv, axis=1, keepdims=True)
            r1, r2 = pltpu.roll(d_out * s1, half, 1), pltpu.roll(d_out * s2, 128 - half, 1)
            yield
            r = lax.rsqrt(ms + EPS)
            xh = xv * r
            d_xn = d_out * cc + r1 + r2
            parts.append(jnp.sum(d_xn * xh, axis=0, keepdims=True))
            d_xh = d_xn * g
            dot = jnp.mean(d_xh * xh, axis=1, keepdims=True)
            yield
            o_ref[:, lo:lo + B_DH] = (r * (d_xh - xh * dot)).astype(BF16)

        for which, g_ref, d_ref in ((0, qg_ref, dq_ref), (1, kg_ref, dk_ref)):
            parts = []
            _round_robin([one_head(which, h, g_ref[...], d_ref, parts) for h in range(B_HEADS)])
            acc = parts[0]
            for part in parts[1:]:
                acc = acc + part
            dgain_ref[which:which + 1, :] += acc
        o_ref[:, 2 * B_W:3 * B_W] = dv_ref[...]
        if dz is not None:
            o_ref[:, 3 * B_W:4 * B_W] = dz_ref[...]

    tab = pl.BlockSpec((tm, 128), lambda i: (i, 0))
    gain = pl.BlockSpec((1, B_DH), lambda i: (0, 0))
    grad = pl.BlockSpec((tm, B_W), lambda i: (i, 0))
    in_specs = [pl.BlockSpec((tm, 2 * B_W), lambda i: (i, 0)), tab, tab, tab, gain, gain, grad, grad, grad]
    args = [proj, c, sa, sb, qg, kg, dq, dk, dv]
    if dz is not None:
        in_specs.append(grad)
        args.append(dz)
    return pl.pallas_call(
        body, name=name, grid=(t // tm,), in_specs=in_specs,
        out_specs=[pl.BlockSpec((tm, out_w), lambda i: (i, 0)), pl.BlockSpec((8, B_DH), lambda i: (0, 0))],
        out_shape=[jax.ShapeDtypeStruct((t, out_w), BF16), jax.ShapeDtypeStruct((8, B_DH), F32)],
        compiler_params=_params(("arbitrary",), 40 * 1024 * 1024),
    )(*args)


def _attn_masks():
    qi = lax.broadcasted_iota(jnp.int32, (B_BLK, 2 * B_BLK), 0)
    kj = lax.broadcasted_iota(jnp.int32, (B_BLK, 2 * B_BLK), 1)
    two = (kj >= qi) & (kj <= qi + B_BLK)
    q1 = lax.broadcasted_iota(jnp.int32, (B_BLK, B_BLK), 0)
    k1 = lax.broadcasted_iota(jnp.int32, (B_BLK, B_BLK), 1)
    return k1 <= q1, two


def _lane_pick(ref_rows, h):
    lane = lax.broadcasted_iota(jnp.int32, ref_rows.shape, 1)
    return jnp.sum(jnp.where(lane == h, ref_rows, 0.0), axis=1, keepdims=True)


B_ROWS = 2048


def _attn_schedule(nb, sb, block):
    way = 16

    def run(items):
        for at in range(0, len(items), way):
            _round_robin([block(*it) for it in items[at:at + way]])

    run([(si, 0, True) for si in range(sb)])
    if nb == 1:
        return
    per = max(1, way // sb)
    lead = 1 + (nb - 1) % per
    if lead > 1:
        run([(si, i, False) for i in range(1, lead) for si in range(sb)])

    def step(it, carry):
        run([(si, lead + it * per + u, False) for u in range(per) for si in range(sb)])
        return carry

    lax.fori_loop(0, (nb - lead) // per, step, 0)


def _attn_rows(i, first):
    if first:
        return pl.ds(0, B_BLK), pl.ds(0, B_BLK)
    rows = pl.ds(pl.multiple_of(i * B_BLK, B_BLK), B_BLK)
    return rows, pl.ds(pl.multiple_of((i - 1) * B_BLK, B_BLK), 2 * B_BLK)


def _attn_fwd(qkv, name):
    ns, ln, _ = qkv.shape
    nb = ln // B_BLK
    sb = B_ROWS // ln
    scale = B_DH ** -0.5

    def body(q_ref, k_ref, v_ref, o_ref, lse_ref):
        h = pl.program_id(1)
        mask1, mask2 = _attn_masks()
        lane = lax.broadcasted_iota(jnp.int32, (B_BLK, B_HEADS), 1)

        @pl.when(h == 0)
        def _():
            lse_ref[...] = jnp.zeros_like(lse_ref)

        def block(si, i, first):
            rows, win = _attn_rows(i, first)
            mask = mask1 if first else mask2
            sc = jnp.where(mask, _dot(q_ref[si, rows, :], k_ref[si, win, :], 1, 1) * scale, -1e30)
            yield
            m = jnp.max(sc, axis=1, keepdims=True)
            p = jnp.exp(sc - m)
            l = jnp.sum(p, axis=1, keepdims=True)
            pv = _dot(p, v_ref[si, win, :], 1, 0)
            yield
            o_ref[si, rows, :] = (pv / l).astype(BF16)
            lse_ref[si, rows, :] = jnp.where(lane == h, m + jnp.log(l), lse_ref[si, rows, :])

        _attn_schedule(nb, sb, block)

    head = lambda off: pl.BlockSpec((sb, ln, B_DH), lambda s, h: (s, 0, off + h))
    return pl.pallas_call(
        body, name=name, grid=(ns // sb, B_HEADS),
        in_specs=[head(0), head(B_HEADS), head(2 * B_HEADS)],
        out_specs=[head(0), pl.BlockSpec((sb, ln, B_HEADS), lambda s, h: (s, 0, 0))],
        out_shape=[jax.ShapeDtypeStruct((ns, ln, B_W), BF16), jax.ShapeDtypeStruct((ns, ln, B_HEADS), F32)],
        compiler_params=_params(("parallel", "arbitrary")),
    )(qkv, qkv, qkv)


def _attn_bwd(qkv, d_o, lse_joint, delta, name):
    ns, ln, _ = qkv.shape
    nb = ln // B_BLK
    sb = B_ROWS // ln
    scale = B_DH ** -0.5

    def body(q_ref, k_ref, v_ref, do_ref, lj_ref, dl_ref, dq_ref, dk_out, dv_out, dk_ref, dv_ref):
        h = pl.program_id(1)
        mask1, mask2 = _attn_masks()
        dk_ref[...] = jnp.zeros_like(dk_ref)
        dv_ref[...] = jnp.zeros_like(dv_ref)

        def block(si, i, first):
            rows, win = _attn_rows(i, first)
            mask = mask1 if first else mask2
            q = q_ref[si, rows, :]
            d_out = do_ref[si, rows, :]
            l_col = _lane_pick(lj_ref[si, rows, :], h)
            d_col = _lane_pick(dl_ref[si, rows, :], h)
            sc = _dot(q, k_ref[si, win, :], 1, 1) * scale
            d_p = _dot(d_out, v_ref[si, win, :], 1, 1)
            yield
            p = jnp.exp(jnp.where(mask, sc - l_col, -1e30))
            d_s = p * (d_p - d_col) * scale
            d_q = _dot(d_s, k_ref[si, win, :], 1, 0)
            d_k = _dot(d_s, q, 0, 0)
            d_v = _dot(p, d_out, 0, 0)
            yield
            dq_ref[si, rows, :] = d_q.astype(BF16)
            dk_ref[si, win, :] += d_k
            dv_ref[si, win, :] += d_v

        _attn_schedule(nb, sb, block)
        dk_out[...] = dk_ref[...].astype(BF16)
        dv_out[...] = dv_ref[...].astype(BF16)

    head = lambda off: pl.BlockSpec((sb, ln, B_DH), lambda s, h: (s, 0, off + h))
    small = pl.BlockSpec((sb, ln, B_HEADS), lambda s, h: (s, 0, 0))
    return pl.pallas_call(
        body, name=name, grid=(ns // sb, B_HEADS),
        in_specs=[head(0), head(B_HEADS), head(2 * B_HEADS), head(0), small, small],
        out_specs=[head(0)] * 3,
        out_shape=[jax.ShapeDtypeStruct((ns, ln, B_W), BF16)] * 3,
        scratch_shapes=[pltpu.VMEM((sb, ln, B_DH), F32)] * 2,
        compiler_params=_params(("parallel", "parallel")),
    )(qkv, qkv, qkv, d_o, lse_joint, delta)


def _merge_weights(lse_refs):
    ls = [r[...] for r in lse_refs]
    m = jnp.maximum(jnp.maximum(ls[0], ls[1]), ls[2])
    es = [jnp.exp(l - m) for l in ls]
    tot = es[0] + es[1] + es[2]
    return [e / tot for e in es], m + jnp.log(tot)


def _merge_fwd(outs, lses, proj0, tm=512):
    t = outs[0].shape[0]

    def body(o0, o1, o2, l0, l1, l2, z_ref, og_ref):
        wts, _ = _merge_weights((l0, l1, l2))

        def one_head(h):
            cols = slice(h * B_DH, (h + 1) * B_DH)
            w0, w1, w2 = (jnp.broadcast_to(w[:, h:h + 1], (tm, B_DH)) for w in wts)
            yield
            o = w0 * o0[:, cols] + w1 * o1[:, cols] + w2 * o2[:, cols]
            og_ref[:, cols] = (o * _silu(z_ref[:, cols].astype(F32))).astype(BF16)

        _round_robin([one_head(h) for h in range(B_HEADS)])

    wide = pl.BlockSpec((tm, B_W), lambda i: (i, 0))
    small = pl.BlockSpec((tm, B_HEADS), lambda i: (i, 0))
    return pl.pallas_call(
        body, name="merge_fwd", grid=(t // tm,),
        in_specs=[wide] * 3 + [small] * 3 + [pl.BlockSpec((tm, B_W), lambda i: (i, 3))],
        out_specs=wide, out_shape=jax.ShapeDtypeStruct((t, B_W), BF16),
        compiler_params=_params(("parallel",)),
    )(*outs, *lses, proj0)


def _merge_bwd(outs, lses, proj0, d_og, tm=512):
    t = outs[0].shape[0]

    def body(o0, o1, o2, l0, l1, l2, z_ref, dog_ref, do_ref, lj_ref, dl_ref, dz_ref):
        wts, lj = _merge_weights((l0, l1, l2))
        lj_ref[...] = lj
        lane = lax.broadcasted_iota(jnp.int32, (tm, B_HEADS), 1)
        sums = [None] * B_HEADS

        def one_head(h):
            cols = slice(h * B_DH, (h + 1) * B_DH)
            w0, w1, w2 = (jnp.broadcast_to(w[:, h:h + 1], (tm, B_DH)) for w in wts)
            yield
            o = w0 * o0[:, cols] + w1 * o1[:, cols] + w2 * o2[:, cols]
            z = z_ref[:, cols].astype(F32)
            d_g = dog_ref[:, cols].astype(F32)
            gate, gate_slope = _silu_and_slope(z)
            d_out = d_g * gate
            dz_ref[:, cols] = (d_g * o * gate_slope).astype(BF16)
            do_ref[:, cols] = d_out.astype(BF16)
            sums[h] = jnp.sum(d_out * o, axis=1, keepdims=True)
            yield

        _round_robin([one_head(h) for h in range(B_HEADS)])
        delta = jnp.zeros((tm, B_HEADS), F32)
        for h in range(B_HEADS):
            delta = jnp.where(lane == h, sums[h], delta)
        dl_ref[...] = delta

    wide = pl.BlockSpec((tm, B_W), lambda i: (i, 0))
    small = pl.BlockSpec((tm, B_HEADS), lambda i: (i, 0))
    return pl.pallas_call(
        body, name="merge_bwd", grid=(t // tm,),
        in_specs=[wide] * 3 + [small] * 3 + [pl.BlockSpec((tm, B_W), lambda i: (i, 3)), wide],
        out_specs=[wide, small, small, wide],
        out_shape=[jax.ShapeDtypeStruct((t, B_W), BF16), jax.ShapeDtypeStruct((t, B_HEADS), F32),
                   jax.ShapeDtypeStruct((t, B_HEADS), F32), jax.ShapeDtypeStruct((t, B_W), BF16)],
        compiler_params=_params(("parallel",)),
    )(*outs, *lses, proj0, d_og)


def _adamw(w, g, m, v, name):
    r, c = w.shape
    tr = r
    for cand in (256, 128, 64, 32, 16, 8):
        if r % cand == 0:
            tr = cand
            break

    def body(w_ref, g_ref, m_ref, v_ref, d_ref, nm_ref, nv_ref):
        gv = g_ref[...]
        nm = ADAM_B1 * m_ref[...] + (1.0 - ADAM_B1) * gv
        nv = ADAM_B2 * v_ref[...] + (1.0 - ADAM_B2) * (gv * gv)
        m_hat = nm / (1.0 - ADAM_B1 ** ADAM_STEP)
        v_hat = nv / (1.0 - ADAM_B2 ** ADAM_STEP)
        d_ref[...] = -ADAM_LR * (m_hat / (jnp.sqrt(v_hat) + ADAM_EPS) + ADAM_WD * w_ref[...])
        nm_ref[...] = nm
        nv_ref[...] = nv

    blk = pl.BlockSpec((tr, c), lambda i: (i, 0))
    return pl.pallas_call(
        body, name=name, grid=(r // tr,), in_specs=[blk] * 4, out_specs=[blk] * 3,
        out_shape=[jax.ShapeDtypeStruct((r, c), F32)] * 3,
        compiler_params=_params(("parallel",)),
    )(w, g, m, v)


def _adam_update(w, gv, m, v):
    nm = ADAM_B1 * m + (1.0 - ADAM_B1) * gv
    nv = ADAM_B2 * v + (1.0 - ADAM_B2) * (gv * gv)
    m_hat = nm / (1.0 - ADAM_B1 ** ADAM_STEP)
    v_hat = nv / (1.0 - ADAM_B2 ** ADAM_STEP)
    return -ADAM_LR * (m_hat / (jnp.sqrt(v_hat) + ADAM_EPS) + ADAM_WD * w), nm, nv


def _adamw_shard(w, mine, theirs, m, v, half_index, name, tr=128):
    _, r, c = w.shape
    nhb = (r // 2) // tr

    def body(c_ref, w_ref, mine_ref, theirs_ref, m_ref, v_ref, g_ref, d_ref, nm_ref, nv_ref):
        is_mine = (pl.program_id(0) // nhb) == c_ref[0]
        gv = jnp.where(is_mine, mine_ref[...], theirs_ref[...])
        d, nm, nv = _adam_update(w_ref[...], gv, m_ref[...], v_ref[...])
        g_ref[...] = gv
        d_ref[...] = d
        nm_ref[...] = nm
        nv_ref[...] = nv

    full = pl.BlockSpec((None, tr, c), lambda i, cc: (0, i, 0))
    half = pl.BlockSpec((tr, c), lambda i, cc: (i % nhb, 0))
    return pl.pallas_call(
        body, name=name,
        grid_spec=pltpu.PrefetchScalarGridSpec(
            num_scalar_prefetch=1, grid=(2 * nhb,),
            in_specs=[full, half, half, full, full], out_specs=[full] * 4),
        out_shape=[jax.ShapeDtypeStruct(w.shape, F32)] * 4,
        compiler_params=_params(("parallel",), 40 * 1024 * 1024),
    )(half_index, w, mine, theirs, m, v)


def _adamw_shard_cols(w, mine, theirs, m, v, half_index, name, steps=20):
    c, _, r = w.shape
    tc = c // steps
    assert tc * steps == c

    def body(c_ref, w_ref, mine_ref, theirs_ref, m_ref, v_ref, g_ref, d_ref, nm_ref, nv_ref):
        first = jnp.where(c_ref[0] == 0, mine_ref[...], theirs_ref[...])
        second = jnp.where(c_ref[0] == 0, theirs_ref[...], mine_ref[...])
        for lo, gv in ((0, first), (r // 2, second)):
            cols = slice(lo, lo + r // 2)
            d, nm, nv = _adam_update(w_ref[:, :, cols], gv, m_ref[:, :, cols], v_ref[:, :, cols])
            g_ref[:, :, cols] = gv
            d_ref[:, :, cols] = d
            nm_ref[:, :, cols] = nm
            nv_ref[:, :, cols] = nv

    full = pl.BlockSpec((tc, 1, r), lambda i, cc: (i, 0, 0))
    half = pl.BlockSpec((tc, 1, r // 2), lambda i, cc: (i, 0, 0))
    return pl.pallas_call(
        body, name=name,
        grid_spec=pltpu.PrefetchScalarGridSpec(
            num_scalar_prefetch=1, grid=(steps,),
            in_specs=[full, half, half, full, full], out_specs=[full] * 4),
        out_shape=[jax.ShapeDtypeStruct(w.shape, F32)] * 4,
        compiler_params=_params(("parallel",), 40 * 1024 * 1024),
    )(half_index, w, mine, theirs, m, v)


def _pair_sum(own, other, half_index, name, tr=256):
    _, r, c = own.shape
    rh = r // 2
    tr = min(tr, rh)
    nrb = rh // tr

    def body(c_ref, own_ref, oth_ref, out_ref):
        out_ref[...] = (own_ref[...] + oth_ref[...].astype(F32)).astype(BF16)

    return pl.pallas_call(
        body, name=name,
        grid_spec=pltpu.PrefetchScalarGridSpec(
            num_scalar_prefetch=1, grid=(N_CHIPS, nrb),
            in_specs=[pl.BlockSpec((None, tr, c), lambda k, i, cc: (k, cc[0] * nrb + i, 0)),
                      pl.BlockSpec((None, tr, c), lambda k, i, cc: (k, i, 0))],
            out_specs=pl.BlockSpec((None, tr, c), lambda k, i, cc: (k, i, 0))),
        out_shape=jax.ShapeDtypeStruct((N_CHIPS, rh, c), BF16),
        compiler_params=_params(("parallel", "parallel")),
    )(half_index, own, other)


def _chip_sum(sums, others, chip_index, name, tr=256):
    _, r, c = sums.shape
    tr = min(tr, r)

    def body(k_ref, own_ref, oth_ref, out_ref):
        acc = own_ref[...].astype(F32)
        for j in range(N_CHIPS - 1):
            acc = acc + oth_ref[j].astype(F32)
        out_ref[...] = acc

    return pl.pallas_call(
        body, name=name,
        grid_spec=pltpu.PrefetchScalarGridSpec(
            num_scalar_prefetch=1, grid=(r // tr,),
            in_specs=[pl.BlockSpec((None, tr, c), lambda i, kk: (kk[0], i, 0)),
                      pl.BlockSpec((N_CHIPS - 1, tr, c), lambda i, kk: (0, i, 0))],
            out_specs=pl.BlockSpec((tr, c), lambda i, kk: (i, 0))),
        out_shape=jax.ShapeDtypeStruct((r, c), F32),
        compiler_params=_params(("parallel",)),
    )(chip_index, sums, others)


HBM = pl.BlockSpec(memory_space=pltpu.HBM)


def _place():
    x, y, c = lax.axis_index("x"), lax.axis_index("y"), lax.axis_index("c")
    chips = [(1 - x, y), (x, 1 - y), (1 - x, 1 - y)]
    return x, y, c, chips


def _sibling_forward(land):
    def body(in_ref, out_ref, send, recv):
        x, y, c, chips = _place()
        rh = out_ref.shape[1] // 2
        cps = []
        for j, (px, py) in enumerate(chips):
            slot = out_ref.at[2 * px + py, pl.ds(c * rh, rh)]
            cp = pltpu.make_async_remote_copy(
                src_ref=slot, dst_ref=slot, send_sem=send.at[j], recv_sem=recv.at[j],
                device_id=(x, y, 1 - c), device_id_type=MESH)
            cp.start()
            cps.append(cp)
        for j, (px, py) in enumerate(chips):
            slot = out_ref.at[2 * px + py, pl.ds((1 - c) * rh, rh)]
            pltpu.make_async_remote_copy(
                src_ref=slot, dst_ref=slot, send_sem=send.at[j], recv_sem=recv.at[j],
                device_id=(x, y, 1 - c), device_id_type=MESH).wait_recv()
        for cp in cps:
            cp.wait_send()

    return pl.pallas_call(
        body, name="first_weights_sibling_forward", in_specs=[HBM], out_specs=HBM,
        out_shape=jax.ShapeDtypeStruct(land.shape, land.dtype), input_output_aliases={0: 0},
        scratch_shapes=[pltpu.SemaphoreType.DMA((3,)), pltpu.SemaphoreType.DMA((3,))],
    )(land)


def _sibling_swap(halves, name):
    na = len(halves)

    def body(*refs):
        ins, outs = refs[:na], refs[na:2 * na]
        send, recv = refs[2 * na:]
        x, y, c, _ = _place()
        cps = []
        for i in range(na):
            cp = pltpu.make_async_remote_copy(
                src_ref=ins[i], dst_ref=outs[i], send_sem=send.at[i], recv_sem=recv.at[i],
                device_id=(x, y, 1 - c), device_id_type=MESH)
            cp.start()
            cps.append(cp)
        for cp in cps:
            cp.wait()

    out_shape = [jax.ShapeDtypeStruct(h.shape, h.dtype) for h in halves]
    return pl.pallas_call(
        body, name=name, in_specs=[HBM] * na, out_specs=[HBM] * na, out_shape=out_shape,
        scratch_shapes=[pltpu.SemaphoreType.DMA((na,)), pltpu.SemaphoreType.DMA((na,))],
    )(*halves)


SEM = pl.BlockSpec(memory_space=pltpu.SEMAPHORE)
ANY = pl.BlockSpec(memory_space=pl.ANY)
EFFECT = pltpu.SideEffectType.DATAFLOW_SIDE_EFFECTING


def _split_copy_start(name, plan, srcs, lands, after):
    ns, nl = len(srcs), len(lands)

    def body(*refs):
        src_refs, land_refs = refs[:ns], refs[ns:ns + nl]
        send, recv = refs[ns + nl + 1], refs[ns + nl + 2]
        token = refs[-1]
        outgoing, _ = plan(src_refs, land_refs)
        for src, dst, dev, si, ri in outgoing:
            pltpu.make_async_remote_copy(src_ref=src, dst_ref=dst, send_sem=send.at[si], recv_sem=recv.at[ri],
                                         device_id=dev, device_id_type=MESH).start()
        token[...] = jnp.zeros_like(token)

    n_out, n_in = plan.counts
    thru = [pltpu.HBM(a.shape, a.dtype) for a in list(srcs) + list(lands)]
    res = pl.pallas_call(
        body, name=name,
        out_shape=[pltpu.SemaphoreType.DMA((n_out,)), pltpu.SemaphoreType.DMA((n_in,))] + thru
        + [jax.ShapeDtypeStruct((8, 128), F32)],
        in_specs=[HBM] * (ns + nl) + [ANY],
        out_specs=[SEM, SEM] + [HBM] * (ns + nl) + [pl.BlockSpec(memory_space=pltpu.VMEM)],
        input_output_aliases={i: 2 + i for i in range(ns + nl)},
        compiler_params=pltpu.CompilerParams(has_side_effects=EFFECT),
    )(*[pltpu.with_memory_space_constraint(a, pltpu.HBM) for a in list(srcs) + list(lands)], after)
    return res[0], res[1], res[2:2 + ns], res[2 + ns:2 + ns + nl], res[-1]


def _split_copy_wait(name, plan, send, recv, srcs, lands, after):
    ns, nl = len(srcs), len(lands)
    after = list(after) if isinstance(after, (list, tuple)) else [after]

    def body(*refs):
        src_refs, land_refs = refs[:ns], refs[ns:ns + nl]
        send_ref, recv_ref = refs[ns + nl], refs[ns + nl + 1]
        outgoing, arrivals = plan(src_refs, land_refs)
        for src, dst, dev, si, ri in outgoing:
            pltpu.make_async_remote_copy(src_ref=src, dst_ref=dst, send_sem=send_ref.at[si], recv_sem=recv_ref.at[ri],
                                         device_id=dev, device_id_type=MESH).wait_send()
        for view, ri in arrivals:
            pltpu.make_async_remote_copy(src_ref=view, dst_ref=view, send_sem=send_ref.at[0], recv_sem=recv_ref.at[ri],
                                         device_id=_place()[:3], device_id_type=MESH).wait_recv()

    thru = [pltpu.HBM(a.shape, a.dtype) for a in list(srcs) + list(lands)]
    res = pl.pallas_call(
        body, name=name, out_shape=thru,
        in_specs=[HBM] * (ns + nl) + [SEM, SEM] + [ANY] * len(after), out_specs=[HBM] * (ns + nl),
        input_output_aliases={i: i for i in range(ns + nl)},
        compiler_params=pltpu.CompilerParams(has_side_effects=EFFECT),
    )(*srcs, *lands, send, recv, *after)
    return res[:ns], res[ns:]


def _gather_plan(n_arrays):
    def plan(src_refs, land_refs):
        x, y, c, chips = _place()
        me = 2 * x + y
        outgoing, arrivals = [], []
        for i in range(n_arrays):
            rh = src_refs[i].shape[0] // 2
            mine = pl.ds(c * rh, rh)
            for j, (px, py) in enumerate(chips):
                for delta in range(2):
                    tc = c ^ delta
                    outgoing.append((src_refs[i].at[mine], land_refs[i].at[me, mine], (px, py, tc),
                                     6 * i + 2 * j + delta, 6 * i + 2 * j + delta))
                    theirs = pl.ds(tc * rh, rh)
                    arrivals.append((land_refs[i].at[2 * px + py, theirs], 6 * i + 2 * j + delta))
        return outgoing, arrivals

    plan.counts = (6 * n_arrays, 6 * n_arrays)
    return plan


def _first_gather_plan():
    def plan(src_refs, land_refs):
        x, y, c, chips = _place()
        me = 2 * x + y
        rh = src_refs[0].shape[0] // 2
        mine = pl.ds(c * rh, rh)
        outgoing, arrivals = [], []
        for j, (px, py) in enumerate(chips):
            outgoing.append((src_refs[0].at[mine], land_refs[0].at[me, mine], (px, py, c), j, j))
            arrivals.append((land_refs[0].at[2 * px + py, mine], j))
            outgoing.append((src_refs[1], land_refs[1].at[me], (px, py, c), 3 + j, 3 + j))
            arrivals.append((land_refs[1].at[2 * px + py], 3 + j))
        return outgoing, arrivals

    plan.counts = (6, 6)
    return plan


def _exchange_plan(n_arrays):
    def plan(src_refs, land_refs):
        x, y, c, chips = _place()
        outgoing, arrivals = [], []
        for i in range(n_arrays):
            for j, (px, py) in enumerate(chips):
                outgoing.append((src_refs[i].at[2 * px + py], land_refs[i].at[j], (px, py, c), 3 * i + j, 3 * i + j))
                arrivals.append((land_refs[i].at[j], 3 * i + j))
        return outgoing, arrivals

    plan.counts = (3 * n_arrays, 3 * n_arrays)
    return plan


def _small_allreduce(vec):
    r, cdim = vec.shape
    n_dev = 8

    def body(v_ref, out_ref, buf, send, recv):
        x, y, c, _ = _place()
        me = 4 * x + 2 * y + c
        buf[me] = v_ref[...]
        cps = []
        for k in range(1, n_dev):
            dx, dy, dc = (k >> 2) & 1, (k >> 1) & 1, k & 1
            peer = (x ^ dx, y ^ dy, c ^ dc)
            cp = pltpu.make_async_remote_copy(
                src_ref=v_ref, dst_ref=buf.at[me], send_sem=send.at[k - 1], recv_sem=recv.at[k - 1],
                device_id=peer, device_id_type=MESH)
            cp.start()
            cps.append(cp)
        for k in range(1, n_dev):
            dx, dy, dc = (k >> 2) & 1, (k >> 1) & 1, k & 1
            src = 4 * (x ^ dx) + 2 * (y ^ dy) + (c ^ dc)
            slot = buf.at[src]
            pltpu.make_async_remote_copy(
                src_ref=slot, dst_ref=slot, send_sem=send.at[k - 1], recv_sem=recv.at[k - 1],
                device_id=(x ^ dx, y ^ dy, c ^ dc), device_id_type=MESH).wait_recv()
        for cp in cps:
            cp.wait_send()
        acc = buf[0]
        for k in range(1, n_dev):
            acc = acc + buf[k]
        out_ref[...] = acc

    vm = pl.BlockSpec(memory_space=pltpu.VMEM)
    return pl.pallas_call(
        body, name="small_allreduce", in_specs=[vm], out_specs=vm,
        out_shape=jax.ShapeDtypeStruct((r, cdim), F32),
        scratch_shapes=[pltpu.VMEM((n_dev, r, cdim), F32), pltpu.SemaphoreType.DMA((n_dev - 1,)),
                        pltpu.SemaphoreType.DMA((n_dev - 1,))],
    )(vec)


def _a_cols_to_head_major(w):
    lead = w.shape[:-1]
    q = w[..., :A_QK].reshape(lead + (A_HEADS, A_DK))
    k = w[..., A_QK:2 * A_QK].reshape(lead + (A_HEADS, A_DK))
    v = w[..., 2 * A_QK:2 * A_QK + A_VW].reshape(lead + (A_HEADS, A_DV))
    z = w[..., 2 * A_QK + A_VW:].reshape(lead + (A_HEADS, A_DV))
    return jnp.concatenate([q, k, v, z], axis=-1).reshape(lead + (A_HEADS * A_HEAD_COLS,))


def _a_cols_from_head_major(w):
    lead = w.shape[:-1]
    w = w.reshape(lead + (A_HEADS, A_HEAD_COLS))
    parts = [w[..., :A_DK], w[..., A_DK:2 * A_DK], w[..., 2 * A_DK:2 * A_DK + A_DV], w[..., 2 * A_DK + A_DV:]]
    return jnp.concatenate([p.reshape(lead + (-1,)) for p in parts], axis=-1)


def _conv_cols_to_head_major(w):
    lead = w.shape[:-1]
    q = w[..., :A_QK].reshape(lead + (A_HEADS, A_DK))
    k = w[..., A_QK:2 * A_QK].reshape(lead + (A_HEADS, A_DK))
    v = w[..., 2 * A_QK:].reshape(lead + (A_HEADS, A_DV))
    return jnp.concatenate([q, k, v], axis=-1).reshape(lead + (A_HEADS * A_CONV_COLS,))


def _conv_cols_from_head_major(w):
    lead = w.shape[:-1]
    w = w.reshape(lead + (A_HEADS, A_CONV_COLS))
    parts = [w[..., :A_DK], w[..., A_DK:2 * A_DK], w[..., 2 * A_DK:]]
    return jnp.concatenate([p.reshape(lead + (-1,)) for p in parts], axis=-1)


def _to_stream(a, bn, d):
    rest = a.shape[1:]
    s = a.shape[0] // bn
    a = a.reshape((bn, s // d, d) + rest)
    a = jnp.swapaxes(a, 1, 2)
    return a.reshape((bn * d, s // d) + rest)


def _from_stream(a, bn, d):
    rest = a.shape[2:]
    ln = a.shape[1]
    a = a.reshape((bn, d, ln) + rest)
    a = jnp.swapaxes(a, 1, 2)
    return a.reshape((bn * ln * d,) + rest)


B_SUB = 512
B_SHARD_BLOCKS = (3 * B_GROUPS * B_W + B_W) // N_CHIPS // B_SUB


def _b_block(gi, jj):
    nb = (B_GROUPS * (jj // 2) + gi) * 2 + jj % 2
    return nb // B_SHARD_BLOCKS, nb % B_SHARD_BLOCKS


def _shard_major(g, ncols):
    r = g.shape[0]
    return jnp.swapaxes(g.reshape(r, N_CHIPS, ncols), 0, 1)


def _pack_rows(items):
    rows, offs = [], []
    at = 0
    for a in items:
        flat = a.reshape(-1).astype(F32)
        nr = -(-flat.shape[0] // 1024) * 8
        flat = jnp.pad(flat, (0, nr * 128 - flat.shape[0]))
        rows.append(flat.reshape(nr, 128))
        offs.append((at, nr, a.shape))
        at += nr
    return jnp.concatenate(rows, axis=0), offs


def _unpack_rows(packed, offs):
    out = []
    for at, nr, shape in offs:
        size = int(np.prod(shape)) if len(shape) else 1
        out.append(packed[at:at + nr].reshape(-1)[:size].reshape(shape))
    return out


def _local_step(x, positions, loss_target, norm_g, a_log, a_dt_bias, a_norm_g, b_q_norm_g, b_k_norm_g,
                start_token, first_weights, late_weights, b_grads_ready, a_grads_ready):
    bn, s, d = x.shape
    t = bn * s
    n_chunks = s // A_CHUNK
    x0 = x.reshape(t, d)
    h0 = _rms_fwd(x0, norm_g[0:1] + start_token, "rms0_fwd")
    inv_freq = ROPE_THETA ** (-jnp.arange(0, ROPE_DIMS, 2, dtype=F32) / ROPE_DIMS)
    freq_row = jnp.concatenate([inv_freq, inv_freq, jnp.zeros((128 - ROPE_DIMS,), F32)]).reshape(1, 128)
    posf = jnp.broadcast_to(positions.astype(F32).reshape(t, 1), (t, 128)) + start_token
    tabs = _rope_tables(posf, freq_row)
    tabs_s = [tabs if dil == 1 else [_to_stream(tb, bn, dil).reshape(t, 128) for tb in tabs] for dil in B_DIL]
    wa_in, conv_w, late_token = first_weights([h0] + [tb for ts in tabs_s for tb in ts])
    wa_main = _a_cols_to_head_major(wa_in[:, :A_MAIN])
    wa_tail = jnp.pad(wa_in[:, A_MAIN:], ((0, 0), (0, 128 - 2 * A_HEADS))) + late_token.astype(BF16)
    cw_hm = _conv_cols_to_head_major(conv_w)

    proj_a = _matmul(h0, wa_main, "nn", F32, "a_in_main")
    tail_a = _matmul(h0, wa_tail, "nn", F32, "a_in_tail")
    tail_t = jnp.swapaxes(tail_a[:, :2 * A_HEADS].reshape(bn, s, 2 * A_HEADS), 1, 2)
    tail_t = tail_t.reshape(bn, 2 * A_HEADS, n_chunks, A_CHUNK)
    beta, gc = _gdn_prep(tail_t, a_log[0], a_dt_bias[0])
    proj_a3 = proj_a.reshape(bn, s, A_MAIN)
    og_a, oraw_a, states, t_mats, conv_y = _gdn_fwd(proj_a3, cw_hm, beta, gc, a_norm_g)
    wa_out, wb_in, wb_out = late_weights(og_a)
    b_cols = [4 * B_W] + [3 * B_W] * (B_GROUPS - 1)
    x1, h1 = _out_proj(og_a.reshape(t, A_VW), wa_out, x0, "a_out", norm_g=norm_g[1:2])

    h1_s, proj_b, qkv_b, o_b, lse_b = [], [], [], [], []
    for gi, dil in enumerate(B_DIL):
        hs = h1 if dil == 1 else _to_stream(h1, bn, dil).reshape(t, d)
        ts = tabs_s[gi]
        pj = _matmul(hs, wb_in, "nn", BF16, f"b_in_g{gi}", tm=2048, tn=B_SUB, n=b_cols[gi], b_spec=pl.BlockSpec(
            (None, d, B_SUB), lambda i, j, kk, gi=gi: (_b_block(gi, j)[0], kk, _b_block(gi, j)[1])))
        qkv = _qk_prep(pj, *ts, b_q_norm_g[0, gi:gi + 1], b_k_norm_g[0, gi:gi + 1], f"qk_prep_g{gi}")
        o_s, lse_s = _attn_fwd(qkv.reshape(bn * dil, s // dil, 3 * B_W), f"attn_fwd_g{gi}")
        h1_s.append(hs), proj_b.append(pj), qkv_b.append(qkv)
        o_b.append(o_s.reshape(t, B_W) if dil == 1 else _from_stream(o_s, bn, dil))
        lse_b.append(lse_s.reshape(t, B_HEADS) if dil == 1 else _from_stream(lse_s, bn, dil))
    og_b = _merge_fwd(o_b, lse_b, proj_b[0])
    d_x2, loss_parts = _out_proj(og_b, wb_out, x1, "b_out_loss", target=loss_target.reshape(t, d))
    loss_local = jnp.sum(loss_parts)

    d_x2b = d_x2.astype(BF16)
    g_wb_out = _matmul(og_b, d_x2b, "tn", F32, "b_out_dw")
    d_og_b = _matmul(d_x2b, wb_out, "nt", BF16, "b_out_dx")
    d_o, lse_joint, delta, d_z = _merge_bwd(o_b, lse_b, proj_b[0], d_og_b)
    d_h1, g_qn, g_kn = [], [], []
    g_wb_in = lax.empty(wb_in.shape, F32)
    for gi, dil in enumerate(B_DIL):
        if dil == 1:
            do_s, lj_s, dl_s = d_o, lse_joint, delta
        else:
            do_s, lj_s, dl_s = (_to_stream(a, bn, dil).reshape(t, -1) for a in (d_o, lse_joint, delta))
        ns, ln = bn * dil, s // dil
        dq, dk, dv = _attn_bwd(qkv_b[gi].reshape(ns, ln, 3 * B_W), do_s.reshape(ns, ln, B_W),
                               lj_s.reshape(ns, ln, B_HEADS), dl_s.reshape(ns, ln, B_HEADS), f"attn_bwd_g{gi}")
        d_pj, d_gain = _qk_prep_bwd(proj_b[gi], *tabs_s[gi], b_q_norm_g[0, gi:gi + 1], b_k_norm_g[0, gi:gi + 1],
                                    dq.reshape(t, B_W), dk.reshape(t, B_W), dv.reshape(t, B_W),
                                    d_z if gi == 0 else None, f"qk_prep_bwd_g{gi}")
        g_wb_in = _matmul(h1_s[gi], d_pj, "tn", F32, f"b_in_dw_g{gi}", tn=B_SUB, tk=2048, into=(g_wb_in, pl.BlockSpec(
            (None, d, B_SUB), lambda i, j, kk, gi=gi: (_b_block(gi, j)[0], i, _b_block(gi, j)[1]))))
        dh = _matmul(d_pj, wb_in, "nt", BF16, f"b_in_dx_g{gi}", tm=2048, tk=B_SUB, n=d, b_spec=pl.BlockSpec(
            (None, d, B_SUB), lambda i, j, kk, gi=gi: (_b_block(gi, kk)[0], j, _b_block(gi, kk)[1])))
        d_h1.append(dh if dil == 1 else _from_stream(dh.reshape(ns, ln, d), bn, dil))
        g_qn.append(d_gain[0]), g_kn.append(d_gain[1])
    d_x1, g_norm1 = _rms_bwd(x1, norm_g[1:2], d_h1, d_x2, "rms1_bwd")

    d_x1b = d_x1.astype(BF16)
    g_wa_out = _matmul(og_a.reshape(t, A_VW), d_x1b, "tn", F32, "a_out_dw")
    b_token = b_grads_ready(g_wb_in, g_wb_out, g_wa_out)
    d_og_a = _matmul(d_x1b, wa_out, "nt", BF16, "a_out_dx")
    d_pa, d_gc, d_beta, d_cw, d_ng = _gdn_bwd(proj_a3, cw_hm, beta, gc, a_norm_g + b_token, oraw_a, states,
                                              t_mats, conv_y, d_og_a.reshape(bn, s, A_VW))
    d_tail_t, d_alog, d_dtb = _gdn_prep_bwd(tail_t, a_log[0], a_dt_bias[0], d_gc, d_beta)
    d_tail = jnp.swapaxes(d_tail_t.reshape(bn, 2 * A_HEADS, s), 1, 2).reshape(t, 2 * A_HEADS)
    d_tail = jnp.pad(d_tail, ((0, 0), (0, 128 - 2 * A_HEADS))).astype(BF16)
    d_pa = d_pa.reshape(t, A_MAIN)
    g_wa_main = _matmul(h0, d_pa, "tn", F32, "a_in_dw_main")
    g_wa_tail = _matmul(h0, d_tail, "tn", F32, "a_in_dw_tail")
    g_wa_in = jnp.concatenate([_a_cols_from_head_major(g_wa_main), g_wa_tail[:, :2 * A_HEADS]], axis=1)
    a_token = a_grads_ready(g_wa_in)
    d_h0t = _matmul(d_tail + a_token.astype(BF16), wa_tail, "nt", F32, "a_in_dx_tail")
    d_x0, g_norm0 = _in_proj_bwd(d_pa, wa_main, d_h0t, x0, norm_g[0:1], d_x1, "a_in_dx_rms0_bwd")

    gfull = {
        "norm_g": jnp.concatenate([g_norm0, g_norm1], axis=0), "a_w_in": g_wa_in,
        "a_conv_w": _conv_cols_from_head_major(jnp.sum(d_cw, axis=0)),
        "a_log": jnp.sum(d_alog[:, :, 0], axis=0), "a_dt_bias": jnp.sum(d_dtb[:, :, 0], axis=0),
        "a_norm_g": jnp.sum(d_ng[:, :, 0, :], axis=(0, 1)), "a_w_out": g_wa_out, "b_w_in": g_wb_in,
        "b_q_norm_g": jnp.stack(g_qn), "b_k_norm_g": jnp.stack(g_kn), "b_w_out": g_wb_out}
    return loss_local, d_x0.reshape(bn, s, d), gfull


def kernel(x, positions, norm_g, a_w_in, a_conv_w, a_log, a_dt_bias, a_norm_g, a_w_out, b_w_in, b_q_norm_g, b_k_norm_g, b_w_out, loss_target, m_norm_g, m_a_w_in, m_a_conv_w, m_a_log, m_a_dt_bias, m_a_norm_g, m_a_w_out, m_b_w_in, m_b_q_norm_g, m_b_k_norm_g, m_b_w_out, v_norm_g, v_a_w_in, v_a_conv_w, v_a_log, v_a_dt_bias, v_a_norm_g, v_a_w_out, v_b_w_in, v_b_q_norm_g, v_b_k_norm_g, v_b_w_out):
    d = x.shape[2]
    my_c = lax.axis_index("c")
    my_chip = 2 * lax.axis_index("x") + lax.axis_index("y")

    half_index = jnp.reshape(my_c, (1,)).astype(jnp.int32)
    chip_index = jnp.reshape(my_chip, (1,)).astype(jnp.int32)
    def landing(shard):
        return lax.dynamic_update_slice(lax.empty((N_CHIPS,) + shard.shape, shard.dtype), shard[None],
                                        (my_chip,) + (0,) * shard.ndim)

    first_shards = [a_w_in[0].astype(BF16), a_conv_w[0]]
    first_plan = _first_gather_plan()
    first = _split_copy_start("first_weights_start", first_plan, first_shards,
                              [landing(s) for s in first_shards], half_index)
    pending = {}
    late_shards = [(w[0] + first[4][0, 0]).astype(BF16) for w in (a_w_out, b_w_in, b_w_out)]
    late_lands = [landing(s) for s in late_shards]

    def first_weights(after):
        _, (ga_in, g_conv) = _split_copy_wait("first_weights_wait", first_plan, *first[:4],
                                              list(after) + late_lands)
        ga_in = _sibling_forward(ga_in)
        wa_in = jnp.concatenate([ga_in[k] for k in range(N_CHIPS)], axis=1)
        conv_w = jnp.concatenate([g_conv[k] for k in range(N_CHIPS)], axis=1)
        plan = _gather_plan(len(late_shards))
        pending["late"] = (plan,) + tuple(_split_copy_start(
            "late_weights_start", plan, late_shards, late_lands, conv_w))
        return wa_in, conv_w, pending["late"][5][0, 0]

    def late_weights(after):
        plan, send, recv, srcs, lands, _ = pending["late"]
        _, (ga_out, gb_in, gb_out) = _split_copy_wait("late_weights_wait", plan, send, recv, srcs, lands, after)
        return ga_out.reshape(A_VW, d), gb_in, gb_out.reshape(B_W, d)

    def reduce_to_chip_sums(mats, tag):
        half = lambda g: lax.dynamic_slice_in_dim(g, (1 - my_c) * (g.shape[1] // 2), g.shape[1] // 2, axis=1)
        recv_sib = _sibling_swap([half(g).astype(BF16) for g in mats], f"grad_{tag}_sibling_swap")
        return [_pair_sum(g, r, half_index, f"grad_{tag}_pair_sum_{i}") for i, (g, r) in enumerate(zip(mats, recv_sib))]

    def start_exchange(tag, mats):
        sums = reduce_to_chip_sums(mats, tag)
        lands = [lax.empty((N_CHIPS - 1,) + s.shape[1:], BF16) for s in sums]
        plan = _exchange_plan(len(mats))
        pending[tag] = (plan,) + tuple(_split_copy_start(f"grad_{tag}_exchange_start", plan, sums, lands, chip_index))
        return pending[tag][5][0, 0]

    def finish_exchange(tag, after):
        plan, send, recv, srcs, lands, _ = pending[tag]
        return _split_copy_wait(f"grad_{tag}_exchange_wait", plan, send, recv, srcs, lands, after)

    def b_grads_ready(g_wb_in, g_wb_out, g_wa_out):
        return start_exchange("b", [g_wb_in, g_wb_out.reshape(N_CHIPS, -1, d), g_wa_out.reshape(N_CHIPS, -1, d)])

    def a_grads_ready(g_wa_in):
        return start_exchange("a", [_shard_major(g_wa_in, a_w_in.shape[2])])

    loss_local, d_x0, gfull = _local_step(x, positions, loss_target, norm_g, a_log, a_dt_bias, a_norm_g,
                                          b_q_norm_g, b_k_norm_g, first[4][0, 0], first_weights, late_weights,
                                          b_grads_ready, a_grads_ready)

    small = [gfull["norm_g"], gfull["a_conv_w"], gfull["a_log"], gfull["a_dt_bias"], gfull["a_norm_g"],
             gfull["b_q_norm_g"], gfull["b_k_norm_g"], loss_local]
    packed, offs = _pack_rows(small)
    reduced = _small_allreduce(packed)
    g_norm, g_conv_all, g_alog, g_dtb, g_ang, g_q, g_k, loss = _unpack_rows(reduced, offs)
    g_conv_mine = lax.dynamic_slice_in_dim(g_conv_all, my_chip * a_conv_w.shape[2], a_conv_w.shape[2], axis=1)

    b_sums, b_received = finish_exchange("b", d_x0)
    a_sums, a_received = finish_exchange("a", reduced)
    chip_sums = [a_sums[0], b_sums[2], b_sums[0], b_sums[1]]
    received = [a_received[0], b_received[2], b_received[0], b_received[1]]
    halves = [_chip_sum(s, r, chip_index, f"grad_chip_sum_{i}") for i, (s, r) in enumerate(zip(chip_sums, received))]
    theirs = _sibling_swap(halves, "grad_sibling_join")
    big = ("a_w_in", "a_w_out", "b_w_in", "b_w_out")
    big_halves = dict(zip(big, zip(halves, theirs)))

    grads = {
        "norm_g": g_norm, "a_conv_w": g_conv_mine[None], "a_log": g_alog[None], "a_dt_bias": g_dtb[None],
        "a_norm_g": g_ang[None], "b_q_norm_g": g_q[None], "b_k_norm_g": g_k[None]}
    weights = {"norm_g": norm_g, "a_w_in": a_w_in, "a_conv_w": a_conv_w, "a_log": a_log, "a_dt_bias": a_dt_bias,
               "a_norm_g": a_norm_g, "a_w_out": a_w_out, "b_w_in": b_w_in, "b_q_norm_g": b_q_norm_g,
               "b_k_norm_g": b_k_norm_g, "b_w_out": b_w_out}
    m_in = {"norm_g": m_norm_g, "a_w_in": m_a_w_in, "a_conv_w": m_a_conv_w, "a_log": m_a_log,
            "a_dt_bias": m_a_dt_bias, "a_norm_g": m_a_norm_g, "a_w_out": m_a_w_out, "b_w_in": m_b_w_in,
            "b_q_norm_g": m_b_q_norm_g, "b_k_norm_g": m_b_k_norm_g, "b_w_out": m_b_w_out}
    v_in = {"norm_g": v_norm_g, "a_w_in": v_a_w_in, "a_conv_w": v_a_conv_w, "a_log": v_a_log,
            "a_dt_bias": v_a_dt_bias, "a_norm_g": v_a_norm_g, "a_w_out": v_a_w_out, "b_w_in": v_b_w_in,
            "b_q_norm_g": v_b_q_norm_g, "b_k_norm_g": v_b_k_norm_g, "b_w_out": v_b_w_out}
    names = list(weights)

    delta_w, new_m, new_v = {}, {}, {}
    for nm in big:
        mine, other = big_halves[nm]
        if weights[nm].shape[2] % 128:
            cols = lambda a: jnp.transpose(a, (2, 0, 1))
            half_cols = lambda a: jnp.transpose(a)[:, None, :]
            outs = _adamw_shard_cols(cols(weights[nm]), half_cols(mine), half_cols(other), cols(m_in[nm]),
                                     cols(v_in[nm]), half_index, f"adamw_{nm}")
            outs = [jnp.transpose(o, (1, 2, 0)) for o in outs]
        else:
            outs = _adamw_shard(weights[nm], mine, other, m_in[nm], v_in[nm], half_index, f"adamw_{nm}")
        grads[nm], delta_w[nm], new_m[nm], new_v[nm] = outs
    small_names = [nm for nm in names if nm not in big]
    packs = [_pack_rows([src[nm] for nm in small_names]) for src in (weights, grads, m_in, v_in)]
    offs = packs[0][1]
    dl, m2, v2 = _adamw(packs[0][0], packs[1][0], packs[2][0], packs[3][0], "adamw_small")
    for nm, a, b, c2 in zip(small_names, _unpack_rows(dl, offs), _unpack_rows(m2, offs), _unpack_rows(v2, offs)):
        delta_w[nm], new_m[nm], new_v[nm] = a, b, c2

    return (loss, d_x0, *[grads[nm] for nm in names], *[delta_w[nm] for nm in names],
            *[new_m[nm] for nm in names], *[new_v[nm] for nm in names])
```

```python
import jax
import jax.numpy as jnp
import numpy as np
from jax import lax
from jax.experimental import pallas as pl
from jax.experimental.pallas import tpu as pltpu

F32 = jnp.float32
BF16 = jnp.bfloat16
MESH = pl.DeviceIdType.MESH

EPS = 1e-6
A_HEADS = 8
A_DK = 128
A_DV = 256
A_QK = A_HEADS * A_DK
A_VW = A_HEADS * A_DV
A_MAIN = 2 * A_QK + 2 * A_VW
A_HEAD_COLS = 2 * A_DK + 2 * A_DV
A_CONV_COLS = 2 * A_DK + A_DV
A_CHUNK = 64
A_CONV = 4
B_GROUPS = 3
B_HEADS = 8
B_DH = 128
B_W = B_HEADS * B_DH
B_DIL = (1, 4, 16)
B_BLK = 128
ROPE_THETA = 500000.0
ROPE_DIMS = B_DH // 4
ADAM_LR, ADAM_B1, ADAM_B2, ADAM_EPS, ADAM_WD, ADAM_STEP = 0.001, 0.9, 0.999, 1e-08, 0.01, 10
N_CHIPS = 4
VMEM_BIG = 56 * 1024 * 1024


def _params(sem=None, vmem=None):
    return pltpu.CompilerParams(dimension_semantics=sem, vmem_limit_bytes=vmem)


def _dot(a, b, ca, cb):
    return lax.dot_general(a.astype(BF16), b.astype(BF16), (((ca,), (cb,)), ((), ())),
                           preferred_element_type=F32)


def _split3(a):
    hi = a.astype(BF16)
    r = a - hi.astype(F32)
    mid = r.astype(BF16)
    lo = (r - mid.astype(F32)).astype(BF16)
    return hi, mid, lo


def _sigmoid(y):
    return 1.0 / (1.0 + jnp.exp(-y))


def _silu(y):
    return y * _sigmoid(y)


def _silu_and_slope(y):
    s = _sigmoid(y)
    return y * s, s * (1.0 + y * (1.0 - s))


def _matmul(a, b, mode, out_dtype, name, res=None, tm=1024, tn=1024, tk=1024, n=None, b_spec=None, into=None):
    m, k = a.shape[::-1] if mode == "tn" else a.shape
    if n is None:
        n = b.shape[0] if mode == "nt" else b.shape[1]
    tm, tn, tk = min(tm, m), min(tn, n), min(tk, k)
    assert m % tm == 0 and n % tn == 0 and k % tk == 0, (name, a.shape, b.shape)
    nk = k // tk
    dims = {"nn": ((1,), (0,)), "nt": ((1,), (1,)), "tn": ((0,), (0,))}[mode]

    def body(*refs):
        a_ref, b_ref = refs[0], refs[1]
        r_ref = refs[2] if res is not None else None
        o_ref = refs[2 + (res is not None) + (into is not None)]
        prod = lax.dot_general(a_ref[...], b_ref[...], (dims, ((), ())), preferred_element_type=F32)

        def finish(r):
            if res is not None:
                r = r + r_ref[...]
            o_ref[...] = r.astype(out_dtype)

        if nk == 1:
            finish(prod)
            return
        acc = refs[-1]
        kk = pl.program_id(2)

        @pl.when(kk == 0)
        def _():
            acc[...] = prod

        @pl.when((kk > 0) & (kk < nk - 1))
        def _():
            acc[...] += prod

        @pl.when(kk == nk - 1)
        def _():
            finish(acc[...] + prod)

    a_spec = pl.BlockSpec((tm, tk), lambda i, j, kk: (i, kk))
    if mode == "tn":
        a_spec = pl.BlockSpec((tk, tm), lambda i, j, kk: (kk, i))
    if b_spec is None and mode == "nt":
        b_spec = pl.BlockSpec((tn, tk), lambda i, j, kk: (j, kk))
    elif b_spec is None:
        b_spec = pl.BlockSpec((tk, tn), lambda i, j, kk: (kk, j))
    in_specs = [a_spec, b_spec]
    args = [a, b]
    if res is not None:
        in_specs.append(pl.BlockSpec((tm, tn), lambda i, j, kk: (i, j)))
        args.append(res)
    out_spec = pl.BlockSpec((tm, tn), lambda i, j, kk: (i, j))
    out_shape = jax.ShapeDtypeStruct((m, n), out_dtype)
    aliases = {}
    if into is not None:
        assert res is None
        buf, out_spec = into
        out_shape = jax.ShapeDtypeStruct(buf.shape, buf.dtype)
        in_specs.append(ANY)
        args.append(buf)
        aliases = {2: 0}
    return pl.pallas_call(
        body, name=name, grid=(m // tm, n // tn, nk),
        in_specs=in_specs, out_specs=out_spec, out_shape=out_shape, input_output_aliases=aliases,
        scratch_shapes=[pltpu.VMEM((tm, tn), F32)] if nk > 1 else [],
        compiler_params=_params(("parallel", "parallel", "arbitrary"), 48 * 1024 * 1024),
    )(*args)


def _rms_fwd(x, g, name, tm=512):
    t, d = x.shape

    def body(x_ref, g_ref, h_ref):
        xv = x_ref[...]
        r = lax.rsqrt(jnp.mean(xv * xv, axis=-1, keepdims=True) + EPS)
        h_ref[...] = (xv * r * g_ref[...]).astype(BF16)

    return pl.pallas_call(
        body, name=name, grid=(t // tm,),
        in_specs=[pl.BlockSpec((tm, d), lambda i: (i, 0)), pl.BlockSpec((1, d), lambda i: (0, 0))],
        out_specs=pl.BlockSpec((tm, d), lambda i: (i, 0)),
        out_shape=jax.ShapeDtypeStruct((t, d), BF16),
        compiler_params=_params(("parallel",)),
    )(x, g)


def _rms_bwd(x, g, dhs, dres, name, tm=512):
    t, d = x.shape
    n_dh = len(dhs)

    def body(*refs):
        x_ref, g_ref = refs[0], refs[1]
        dh_refs = refs[2:2 + n_dh]
        dres_ref, dx_ref, dg_ref = refs[2 + n_dh:]
        i = pl.program_id(0)

        @pl.when(i == 0)
        def _():
            dg_ref[...] = jnp.zeros_like(dg_ref)

        xv = x_ref[...]
        r = lax.rsqrt(jnp.mean(xv * xv, axis=-1, keepdims=True) + EPS)
        xh = xv * r
        dh = dh_refs[0][...].astype(F32)
        for ref in dh_refs[1:]:
            dh = dh + ref[...].astype(F32)
        dg_ref[0:1, :] += jnp.sum(dh * xh, axis=0, keepdims=True)
        dxh = dh * g_ref[...]
        dx = r * (dxh - xh * jnp.mean(dxh * xh, axis=-1, keepdims=True))
        dx_ref[...] = dx + dres_ref[...]

    row = pl.BlockSpec((tm, d), lambda i: (i, 0))
    dx, dg = pl.pallas_call(
        body, name=name, grid=(t // tm,),
        in_specs=[row, pl.BlockSpec((1, d), lambda i: (0, 0))] + [row] * n_dh + [row],
        out_specs=[row, pl.BlockSpec((8, d), lambda i: (0, 0))],
        out_shape=[jax.ShapeDtypeStruct((t, d), F32), jax.ShapeDtypeStruct((8, d), F32)],
        compiler_params=_params(("arbitrary",)),
    )(x, g, *dhs, dres)
    return dx, dg[0:1]


def _in_proj_bwd(dp, w, dh_more, x, g, dres, name, tm=512, tk=1024):
    t, k = dp.shape
    d = w.shape[0]
    nk = k // tk

    def body(dp_ref, w_ref, more_ref, x_ref, g_ref, dres_ref, dx_ref, dg_ref, acc):
        i, kk = pl.program_id(0), pl.program_id(1)

        @pl.when((i == 0) & (kk == 0))
        def _():
            dg_ref[...] = jnp.zeros_like(dg_ref)

        prod = lax.dot_general(dp_ref[...], w_ref[...], (((1,), (1,)), ((), ())), preferred_element_type=F32)

        @pl.when(kk == 0)
        def _():
            acc[...] = prod

        @pl.when((kk > 0) & (kk < nk - 1))
        def _():
            acc[...] += prod

        @pl.when(kk == nk - 1)
        def _():
            dh = acc[...] + prod + more_ref[...]
            xv = x_ref[...]
            r = lax.rsqrt(jnp.mean(xv * xv, axis=-1, keepdims=True) + EPS)
            xh = xv * r
            dg_ref[0:1, :] += jnp.sum(dh * xh, axis=0, keepdims=True)
            dxh = dh * g_ref[...]
            dx_ref[...] = r * (dxh - xh * jnp.mean(dxh * xh, axis=-1, keepdims=True)) + dres_ref[...]

    row = pl.BlockSpec((tm, d), lambda i, kk: (i, 0))
    dx, dg = pl.pallas_call(
        body, name=name, grid=(t // tm, nk),
        in_specs=[pl.BlockSpec((tm, tk), lambda i, kk: (i, kk)), pl.BlockSpec((d, tk), lambda i, kk: (0, kk)),
                  row, row, pl.BlockSpec((1, d), lambda i, kk: (0, 0)), row],
        out_specs=[row, pl.BlockSpec((8, d), lambda i, kk: (0, 0))],
        out_shape=[jax.ShapeDtypeStruct((t, d), F32), jax.ShapeDtypeStruct((8, d), F32)],
        scratch_shapes=[pltpu.VMEM((tm, d), F32)],
        compiler_params=_params(("arbitrary", "arbitrary"), 48 * 1024 * 1024),
    )(dp, w, dh_more, x, g, dres)
    return dx, dg[0:1]


def _out_proj(a, w, res, name, norm_g=None, target=None, tm=512):
    t, k = a.shape
    d = w.shape[1]
    nb = t // tm

    def body(a_ref, w_ref, r_ref, x_ref, o1_ref, o2_ref):
        y = jnp.dot(a_ref[...], w_ref[...], preferred_element_type=F32) + r_ref[...]
        if norm_g is not None:
            o1_ref[...] = y
            r = lax.rsqrt(jnp.mean(y * y, axis=-1, keepdims=True) + EPS)
            o2_ref[...] = (y * r * x_ref[...]).astype(BF16)
        else:
            e = y - x_ref[...]
            o1_ref[...] = e * (1.0 / d)
            s = jnp.sum(jnp.sum(e * e, axis=1, keepdims=True), axis=0, keepdims=True) * (0.5 / d)
            o2_ref[...] = jnp.broadcast_to(s, (8, 128))

    row = pl.BlockSpec((tm, d), lambda i: (i, 0))
    if norm_g is not None:
        extra, extra_spec = norm_g, pl.BlockSpec((1, d), lambda i: (0, 0))
        out2_spec, out2_shape = row, jax.ShapeDtypeStruct((t, d), BF16)
    else:
        extra, extra_spec = target, row
        out2_spec = pl.BlockSpec((None, 8, 128), lambda i: (i, 0, 0))
        out2_shape = jax.ShapeDtypeStruct((nb, 8, 128), F32)
    o1, o2 = pl.pallas_call(
        body, name=name, grid=(nb,),
        in_specs=[pl.BlockSpec((tm, k), lambda i: (i, 0)), pl.BlockSpec((k, d), lambda i: (0, 0)), row, extra_spec],
        out_specs=[row, out2_spec], out_shape=[jax.ShapeDtypeStruct((t, d), F32), out2_shape],
        compiler_params=_params(("parallel",), 48 * 1024 * 1024),
    )(a, w, res, extra)
    return (o1, o2) if norm_g is not None else (o1, o2[:, 0, 0])


def _softplus(x):
    t = jnp.exp(-jnp.abs(x))
    return jnp.maximum(x, 0.0) + jnp.where(t < 1e-3, t * (1.0 - 0.5 * t), jnp.log(1.0 + t))


def _tri(rows_le_cols):
    r = lax.broadcasted_iota(jnp.int32, (A_CHUNK, A_CHUNK), 0)
    c = lax.broadcasted_iota(jnp.int32, (A_CHUNK, A_CHUNK), 1)
    return jnp.where((r <= c) if rows_le_cols else (r >= c), 1.0, 0.0).astype(BF16)


def _dot_exact_rhs(a, ones_bf16):
    dn = (((1,), (0,)), ((), ()))
    hi, mid, lo = _split3(a)
    out = lax.dot_general(hi, ones_bf16, dn, preferred_element_type=F32)
    out = out + lax.dot_general(mid, ones_bf16, dn, preferred_element_type=F32)
    return out + lax.dot_general(lo, ones_bf16, dn, preferred_element_type=F32)


def _gdn_prep(tail_t, a_log, dt_bias):
    bn, _, n, c = tail_t.shape

    def body(t_ref, alog_ref, dtb_ref, beta_ref, gc_ref):
        upper = _tri(True)
        for h in range(A_HEADS):
            beta_ref[h] = _sigmoid(t_ref[h])
            ea = jnp.exp(jnp.full((n, c), alog_ref[h], F32))
            g = -ea * _softplus(t_ref[A_HEADS + h] + dtb_ref[h])
            gc_ref[h] = _dot_exact_rhs(g, upper)

    smem = pl.BlockSpec(memory_space=pltpu.SMEM)
    blk = pl.BlockSpec((None, A_HEADS, n, c), lambda b: (b, 0, 0, 0))
    return pl.pallas_call(
        body, name="gdn_prep", grid=(bn,),
        in_specs=[pl.BlockSpec((None, 2 * A_HEADS, n, c), lambda b: (b, 0, 0, 0)), smem, smem],
        out_specs=[blk, blk],
        out_shape=[jax.ShapeDtypeStruct((bn, A_HEADS, n, c), F32)] * 2,
        compiler_params=_params(("parallel",)),
    )(tail_t, a_log, dt_bias)


def _gdn_prep_bwd(tail_t, a_log, dt_bias, d_gc, d_beta):
    bn, _, n, c = tail_t.shape

    def body(t_ref, alog_ref, dtb_ref, dgc_ref, dbeta_ref, dt_ref, dal_ref, ddt_ref):
        lower = _tri(False)
        for h in range(A_HEADS):
            beta = _sigmoid(t_ref[h])
            dt_ref[h] = dbeta_ref[h] * beta * (1.0 - beta)
            dg = _dot_exact_rhs(dgc_ref[h], lower)
            ea = jnp.exp(jnp.full((n, c), alog_ref[h], F32))
            xa = t_ref[A_HEADS + h] + dtb_ref[h]
            g = -ea * _softplus(xa)
            dxa = -ea * dg * _sigmoid(xa)
            dt_ref[A_HEADS + h] = dxa
            s1 = jnp.sum(jnp.sum(g * dg, axis=1, keepdims=True), axis=0, keepdims=True)
            s2 = jnp.sum(jnp.sum(dxa, axis=1, keepdims=True), axis=0, keepdims=True)
            dal_ref[h:h + 1, :] = jnp.broadcast_to(s1, (1, 128))
            ddt_ref[h:h + 1, :] = jnp.broadcast_to(s2, (1, 128))

    smem = pl.BlockSpec(memory_space=pltpu.SMEM)
    blk8 = pl.BlockSpec((None, A_HEADS, n, c), lambda b: (b, 0, 0, 0))
    blk16 = pl.BlockSpec((None, 2 * A_HEADS, n, c), lambda b: (b, 0, 0, 0))
    sm = pl.BlockSpec((None, A_HEADS, 128), lambda b: (b, 0, 0))
    return pl.pallas_call(
        body, name="gdn_prep_bwd", grid=(bn,),
        in_specs=[blk16, smem, smem, blk8, blk8],
        out_specs=[blk16, sm, sm],
        out_shape=[jax.ShapeDtypeStruct((bn, 2 * A_HEADS, n, c), F32),
                   jax.ShapeDtypeStruct((bn, A_HEADS, 128), F32),
                   jax.ShapeDtypeStruct((bn, A_HEADS, 128), F32)],
        compiler_params=_params(("parallel",)),
    )(tail_t, a_log, dt_bias, d_gc, d_beta)


HALO = 8


def _conv_taps(xw, w):
    y = w[A_CONV - 1:A_CONV, :] * xw
    for j in range(1, A_CONV):
        y = y + w[A_CONV - 1 - j:A_CONV - j, :] * pltpu.roll(xw, j, 0)
    return y[HALO:, :]


def _row_to_col(row, eye):
    c = eye.shape[0]
    return jnp.sum(jnp.where(eye, jnp.broadcast_to(row, (c, c)), 0.0), axis=1, keepdims=True)


def _col_to_row(col, eye):
    c = eye.shape[0]
    return jnp.sum(jnp.where(eye, jnp.broadcast_to(col, (c, c)), 0.0), axis=0, keepdims=True)


def _unit_lower_inverse(a, ri, ci):
    eye = jnp.where(ri == ci, 1.0, 0.0)
    a8 = jnp.where((ri >> 3) == (ci >> 3), a, 0.0)
    a2 = _dot(a8, a8, 1, 0)
    yield
    a4 = _dot(a2, a2, 1, 0)
    t = eye - a8
    t = t + _dot(t, a2, 1, 0)
    yield
    t = t + _dot(t, a4, 1, 0)
    yield
    for sh in (3, 4, 5):
        off = jnp.where(((ri >> (sh + 1)) == (ci >> (sh + 1))) & ((ri >> sh) != (ci >> sh)), a, 0.0)
        left = _dot(t, off, 1, 0)
        yield
        t = t - _dot(left, t, 1, 0)
        yield
    return t


def _round_robin(gens):
    live = list(gens)
    while live:
        nxt = []
        for g in live:
            try:
                next(g)
                nxt.append(g)
            except StopIteration:
                pass
        live = nxt


def _gdn_chunk_core(q, k, v, g_row, b_row, t_mat, ri, ci):
    eye = ri == ci
    g_col = _row_to_col(g_row, eye)
    b_col = _row_to_col(b_row, eye)
    causal = ri >= ci
    strict = ri > ci
    dec = jnp.where(causal, jnp.exp(jnp.where(causal, g_col - g_row, 0.0)), 0.0)
    gam = jnp.exp(g_col)
    g_last = g_row[:, A_CHUNK - 1:A_CHUNK]
    gam_last = jnp.exp(g_last)
    e = jnp.exp(g_last - g_col)
    kb = k * b_col
    bv = v * b_col
    kbg = kb * gam
    q16, k16, kb16 = q.astype(BF16), k.astype(BF16), kb.astype(BF16)
    kk = _dot(kb16, k16, 1, 1)
    p = _dot(q16, k16, 1, 1) * dec
    yield
    a_mat = jnp.where(strict, kk * dec, 0.0)
    if t_mat is None:
        t_mat = yield from _unit_lower_inverse(a_mat, ri, ci)
    t16 = t_mat.astype(BF16)
    u = _dot(t16, bv, 1, 0)
    w = _dot(t16, kbg, 1, 0)
    yield
    return dict(eye=eye, g_col=g_col, b_col=b_col, dec=dec, strict=strict, causal=causal, gam=gam,
                gam_last=gam_last, e=e, kb=kb, bv=bv, kbg=kbg, a_mat=a_mat, t_mat=t_mat, u=u, w=w, p=p,
                qg=q * gam, kd=k * e, q16=q16, k16=k16, kb16=kb16, t16=t16)


A_SEQ_BLK = 256
A_BLK_CHUNKS = A_SEQ_BLK // A_CHUNK


def _gdn_halo(proj_hm):
    bn, s, w = proj_hm.shape
    last = proj_hm.reshape(bn, s // A_SEQ_BLK, A_SEQ_BLK, w)[:, :, A_SEQ_BLK - HALO:, :]
    return jnp.concatenate([jnp.zeros((bn, 1, HALO, w), proj_hm.dtype), last[:, :-1]], axis=1)


def _gdn_window(x_ref, halo_ref, ci, first, lo):
    if first:
        return jnp.concatenate([halo_ref[:, lo:lo + A_CONV_COLS], x_ref[0:A_CHUNK, lo:lo + A_CONV_COLS]], axis=0)
    start = pl.multiple_of(ci * A_CHUNK - HALO, HALO)
    return x_ref[pl.ds(start, A_CHUNK + HALO), lo:lo + A_CONV_COLS]


def _gdn_chunk_prep(xw, cw, y=None):
    if y is None:
        y = _conv_taps(xw, cw)
    a, slope = _silu_and_slope(y)
    aq, ak, v = a[:, 0:A_DK], a[:, A_DK:2 * A_DK], a[:, 2 * A_DK:]
    rq = lax.rsqrt(jnp.sum(aq * aq, axis=1, keepdims=True) + EPS)
    rk = lax.rsqrt(jnp.sum(ak * ak, axis=1, keepdims=True) + EPS)
    return dict(xw=xw, y=y, slope=slope, aq=aq, ak=ak, rq=rq, rk=rk, q=aq * rq * (A_DK ** -0.5), k=ak * rk, v=v)


def _gdn_fwd(proj_hm, cw_hm, beta, gc, norm_g, hp=8):
    bn, s, _ = proj_hm.shape
    n = s // A_CHUNK
    nsb = s // A_SEQ_BLK
    halo = _gdn_halo(proj_hm)

    def body(x_ref, halo_ref, cw_ref, beta_ref, gc_ref, ng_ref, og_ref, oraw_ref, st_ref, t_ref, y_ref, state):
        first_chunk = pl.program_id(2) * A_BLK_CHUNKS
        ri = lax.broadcasted_iota(jnp.int32, (A_CHUNK, A_CHUNK), 0)
        ci_ = lax.broadcasted_iota(jnp.int32, (A_CHUNK, A_CHUNK), 1)
        ng = ng_ref[...]

        @pl.when(pl.program_id(2) == 0)
        def _():
            state[...] = jnp.zeros_like(state)

        def one_head(hh, ci, first, rows):
            lo = hh * A_HEAD_COLS
            cw = cw_ref[:, hh * A_CONV_COLS:(hh + 1) * A_CONV_COLS]
            cin = _gdn_chunk_prep(_gdn_window(x_ref, halo_ref, ci, first, lo), cw)
            y_ref[rows, hh * A_CONV_COLS:(hh + 1) * A_CONV_COLS] = cin["y"]
            seq_chunk = pl.ds(first_chunk + ci, 1)
            core = yield from _gdn_chunk_core(cin["q"], cin["k"], cin["v"], gc_ref[hh, seq_chunk, :],
                                              beta_ref[hh, seq_chunk, :], None, ri, ci_)
            st = state[hh]
            st_ref[hh, ci] = st.astype(BF16)
            t_ref[hh, ci] = core["t_mat"]
            st16 = st.astype(BF16)
            vn = core["u"] - _dot(core["w"], st16, 1, 0)
            qs = _dot(core["qg"], st16, 1, 0)
            yield
            vn16 = vn.astype(BF16)
            o = qs + _dot(core["p"], vn16, 1, 0)
            state[hh] = st * core["gam_last"] + _dot(core["kd"], vn16, 0, 0)
            yield
            ocols = slice(hh * A_DV, (hh + 1) * A_DV)
            oraw_ref[rows, ocols] = o
            r = lax.rsqrt(jnp.mean(o * o, axis=1, keepdims=True) + EPS)
            z = x_ref[rows, lo + A_CONV_COLS:lo + A_HEAD_COLS]
            og_ref[rows, ocols] = (o * r * ng * _silu(z)).astype(BF16)

        def chunk(ci, first):
            rows = pl.ds(0 if first else pl.multiple_of(ci * A_CHUNK, A_CHUNK), A_CHUNK)
            _round_robin([one_head(hh, ci, first, rows) for hh in range(hp)])

        chunk(0, True)
        lax.fori_loop(1, A_BLK_CHUNKS, lambda i, c: (chunk(i, False), c)[1], 0)

    small = pl.BlockSpec((None, hp, n, A_CHUNK), lambda b, h, j: (b, h, 0, 0))
    return pl.pallas_call(
        body, name="gdn_fwd", grid=(bn, A_HEADS // hp, nsb),
        in_specs=[pl.BlockSpec((None, A_SEQ_BLK, hp * A_HEAD_COLS), lambda b, h, j: (b, j, h)),
                  pl.BlockSpec((None, None, HALO, hp * A_HEAD_COLS), lambda b, h, j: (b, j, 0, h)),
                  pl.BlockSpec((A_CONV, hp * A_CONV_COLS), lambda b, h, j: (0, h)),
                  small, small,
                  pl.BlockSpec((1, A_DV), lambda b, h, j: (0, 0))],
        out_specs=[pl.BlockSpec((None, A_SEQ_BLK, hp * A_DV), lambda b, h, j: (b, j, h)),
                   pl.BlockSpec((None, A_SEQ_BLK, hp * A_DV), lambda b, h, j: (b, j, h)),
                   pl.BlockSpec((None, hp, A_BLK_CHUNKS, A_DK, A_DV), lambda b, h, j: (b, h, j, 0, 0)),
                   pl.BlockSpec((None, hp, A_BLK_CHUNKS, A_CHUNK, A_CHUNK), lambda b, h, j: (b, h, j, 0, 0)),
                   pl.BlockSpec((None, A_SEQ_BLK, hp * A_CONV_COLS), lambda b, h, j: (b, j, h))],
        out_shape=[jax.ShapeDtypeStruct((bn, s, A_VW), BF16),
                   jax.ShapeDtypeStruct((bn, s, A_VW), F32),
                   jax.ShapeDtypeStruct((bn, A_HEADS, n, A_DK, A_DV), BF16),
                   jax.ShapeDtypeStruct((bn, A_HEADS, n, A_CHUNK, A_CHUNK), F32),
                   jax.ShapeDtypeStruct((bn, s, A_HEADS * A_CONV_COLS), F32)],
        scratch_shapes=[pltpu.VMEM((hp, A_DK, A_DV), F32)],
        compiler_params=_params(("parallel", "parallel", "arbitrary"), VMEM_BIG),
    )(proj_hm, halo, cw_hm, beta, gc, norm_g)


def _gdn_bwd(proj_hm, cw_hm, beta, gc, norm_g, oraw, states, t_mats, conv_y, dog, hp=4):
    bn, s, _ = proj_hm.shape
    n = s // A_CHUNK
    nsb = s // A_SEQ_BLK
    halo = _gdn_halo(proj_hm)

    def body(x_ref, halo_ref, cw_ref, beta_ref, gc_ref, ng_ref, oraw_ref, st_ref, t_ref, y_ref, dog_ref,
             dx_ref, dgc_ref, dbeta_ref, dcw_ref, dng_ref, dstate, dy_next, shifted):
        first_chunk = (nsb - 1 - pl.program_id(2)) * A_BLK_CHUNKS
        ri = lax.broadcasted_iota(jnp.int32, (A_CHUNK, A_CHUNK), 0)
        ci_ = lax.broadcasted_iota(jnp.int32, (A_CHUNK, A_CHUNK), 1)
        lane = lax.broadcasted_iota(jnp.int32, (1, A_CHUNK), 1)
        ng = ng_ref[...]

        @pl.when(pl.program_id(2) == 0)
        def _():
            dstate[...] = jnp.zeros_like(dstate)
            dy_next[...] = jnp.zeros_like(dy_next)
            dcw_ref[...] = jnp.zeros_like(dcw_ref)
            dng_ref[...] = jnp.zeros_like(dng_ref)

        def one_head(hh, ci, first, rows):
            lo = hh * A_HEAD_COLS
            ccols = slice(hh * A_CONV_COLS, (hh + 1) * A_CONV_COLS)
            ocols = slice(hh * A_DV, (hh + 1) * A_DV)
            cw = cw_ref[:, ccols]
            cin = _gdn_chunk_prep(_gdn_window(x_ref, halo_ref, ci, first, lo), cw, y_ref[rows, ccols])
            q, k, v = cin["q"], cin["k"], cin["v"]
            seq_chunk = pl.ds(first_chunk + ci, 1)
            cr = yield from _gdn_chunk_core(q, k, v, gc_ref[hh, seq_chunk, :], beta_ref[hh, seq_chunk, :],
                                            t_ref[hh, ci], ri, ci_)
            eye, dec, gam, e = cr["eye"], cr["dec"], cr["gam"], cr["e"]
            b_col, t_mat, u, w, p = cr["b_col"], cr["t_mat"], cr["u"], cr["w"], cr["p"]
            st = st_ref[hh, ci]
            ds_out = dstate[hh]

            o = oraw_ref[rows, ocols]
            z = x_ref[rows, lo + A_CONV_COLS:lo + A_HEAD_COLS]
            d_og = dog_ref[rows, ocols].astype(F32)
            r = lax.rsqrt(jnp.mean(o * o, axis=1, keepdims=True) + EPS)
            oh = o * r
            gate, gate_slope = _silu_and_slope(z)
            d_on = d_og * gate
            dz = d_og * oh * ng * gate_slope
            dng_ref[hh, 0:1, :] += jnp.sum(d_on * oh, axis=0, keepdims=True)
            d_oh = d_on * ng
            d_o = r * (d_oh - oh * jnp.mean(d_oh * oh, axis=1, keepdims=True))

            st16, ds16, do16, w16 = st.astype(BF16), ds_out.astype(BF16), d_o.astype(BF16), w.astype(BF16)
            q16, k16, t16 = cr["q16"], cr["k16"], cr["t16"]
            vn = u - _dot(w16, st16, 1, 0)
            d_vn = _dot(p, do16, 0, 0) + _dot(cr["kd"], ds16, 1, 0)
            d_qg = _dot(do16, st16, 1, 1)
            qgdo = _dot(cr["qg"], do16, 0, 0)
            yield
            vn16, dvn16 = vn.astype(BF16), d_vn.astype(BF16)
            d_p = jnp.where(cr["causal"], _dot(do16, vn16, 1, 1), 0.0)
            d_kd = _dot(vn16, ds16, 1, 1)
            d_gam_last = jnp.sum(jnp.sum(st.astype(F32) * ds_out, axis=1, keepdims=True), axis=0, keepdims=True)
            d_w = -_dot(dvn16, st16, 1, 1)
            dstate[hh] = qgdo + ds_out * cr["gam_last"] - _dot(w16, dvn16, 0, 0)
            d_bv = _dot(t16, dvn16, 0, 0)
            yield
            d_kbg = _dot(t16, d_w, 0, 0)
            n_p = (d_p * dec).astype(BF16)
            d_q = _dot(n_p, k16, 1, 0) + d_qg * gam
            npq = _dot(n_p, q16, 0, 0)
            yield
            d_a = jnp.where(cr["strict"], -(_dot(d_bv, u, 1, 1) + _dot(d_kbg, w16, 1, 1)), 0.0)
            yield
            m_a = (d_a * dec).astype(BF16)
            d_kb = _dot(m_a, k16, 1, 0) + d_kbg * gam
            d_k = (_dot(m_a, cr["kb16"], 0, 0) + npq + d_kd * e + d_kb * b_col)
            yield
            d_v = d_bv * b_col
            d_beta_col = (jnp.sum(d_bv * v, axis=1, keepdims=True)
                          + jnp.sum(d_kb * k, axis=1, keepdims=True))
            gterm = d_a * cr["a_mat"] + d_p * p
            d_e = jnp.sum(d_kd * k, axis=1, keepdims=True) * e
            d_g_col = (jnp.sum(gterm, axis=1, keepdims=True)
                       + (jnp.sum(d_qg * q, axis=1, keepdims=True)
                          + jnp.sum(d_kbg * cr["kb"], axis=1, keepdims=True)) * gam
                       - d_e)
            d_g_last = jnp.sum(d_e, axis=0, keepdims=True) + d_gam_last * cr["gam_last"]
            d_g_row = (_col_to_row(d_g_col, eye) - jnp.sum(gterm, axis=0, keepdims=True)
                       + jnp.where(lane == A_CHUNK - 1, d_g_last, 0.0))
            dgc_ref[hh, seq_chunk, :] = d_g_row
            dbeta_ref[hh, seq_chunk, :] = _col_to_row(d_beta_col, eye)

            qh = cin["aq"] * cin["rq"]
            kh = cin["ak"] * cin["rk"]
            d_qh = d_q * (A_DK ** -0.5)
            d_aq = cin["rq"] * (d_qh - qh * jnp.sum(d_qh * qh, axis=1, keepdims=True))
            d_ak = cin["rk"] * (d_k - kh * jnp.sum(d_k * kh, axis=1, keepdims=True))
            d_y = jnp.concatenate([d_aq, d_ak, d_v], axis=1) * cin["slope"]
            shifted[hh, 0, 0:A_CHUNK, :] = d_y
            shifted[hh, 0, A_CHUNK:A_CHUNK + HALO, :] = dy_next[hh]
            shifted[hh, 1, 0:A_CHUNK + HALO, :] = cin["xw"]
            d_x = cw[A_CONV - 1:A_CONV, :] * d_y
            for j in range(1, A_CONV):
                d_x = d_x + cw[A_CONV - 1 - j:A_CONV - j, :] * shifted[hh, 0, j:j + A_CHUNK, :]
            for j in range(A_CONV):
                xs = shifted[hh, 1, HALO - j:HALO - j + A_CHUNK, :]
                dcw_ref[A_CONV - 1 - j:A_CONV - j, ccols] += jnp.sum(d_y * xs, axis=0, keepdims=True)
            dy_next[hh] = d_y[0:HALO, :]
            dx_ref[rows, lo:lo + A_CONV_COLS] = d_x.astype(BF16)
            dx_ref[rows, lo + A_CONV_COLS:lo + A_HEAD_COLS] = dz.astype(BF16)

        def chunk(ci, first):
            rows = pl.ds(0 if first else pl.multiple_of(ci * A_CHUNK, A_CHUNK), A_CHUNK)
            _round_robin([one_head(hh, ci, first, rows) for hh in range(hp)])

        lax.fori_loop(0, A_BLK_CHUNKS - 1, lambda i, c: (chunk(A_BLK_CHUNKS - 1 - i, False), c)[1], 0)
        chunk(0, True)

    rev = lambda j: nsb - 1 - j
    small = pl.BlockSpec((None, hp, n, A_CHUNK), lambda b, h, j: (b, h, 0, 0))
    wide = pl.BlockSpec((None, A_SEQ_BLK, hp * A_HEAD_COLS), lambda b, h, j: (b, rev(j), h))
    val = pl.BlockSpec((None, A_SEQ_BLK, hp * A_DV), lambda b, h, j: (b, rev(j), h))
    return pl.pallas_call(
        body, name="gdn_bwd", grid=(bn, A_HEADS // hp, nsb),
        in_specs=[wide,
                  pl.BlockSpec((None, None, HALO, hp * A_HEAD_COLS), lambda b, h, j: (b, rev(j), 0, h)),
                  pl.BlockSpec((A_CONV, hp * A_CONV_COLS), lambda b, h, j: (0, h)),
                  small, small,
                  pl.BlockSpec((1, A_DV), lambda b, h, j: (0, 0)),
                  val,
                  pl.BlockSpec((None, hp, A_BLK_CHUNKS, A_DK, A_DV), lambda b, h, j: (b, h, rev(j), 0, 0)),
                  pl.BlockSpec((None, hp, A_BLK_CHUNKS, A_CHUNK, A_CHUNK), lambda b, h, j: (b, h, rev(j), 0, 0)),
                  pl.BlockSpec((None, A_SEQ_BLK, hp * A_CONV_COLS), lambda b, h, j: (b, rev(j), h)),
                  val],
        out_specs=[wide, small, small,
                   pl.BlockSpec((None, A_CONV, hp * A_CONV_COLS), lambda b, h, j: (b, 0, h)),
                   pl.BlockSpec((None, hp, 8, A_DV), lambda b, h, j: (b, h, 0, 0))],
        out_shape=[jax.ShapeDtypeStruct((bn, s, A_HEADS * A_HEAD_COLS), BF16),
                   jax.ShapeDtypeStruct((bn, A_HEADS, n, A_CHUNK), F32),
                   jax.ShapeDtypeStruct((bn, A_HEADS, n, A_CHUNK), F32),
                   jax.ShapeDtypeStruct((bn, A_CONV, A_HEADS * A_CONV_COLS), F32),
                   jax.ShapeDtypeStruct((bn, A_HEADS, 8, A_DV), F32)],
        scratch_shapes=[pltpu.VMEM((hp, A_DK, A_DV), F32), pltpu.VMEM((hp, HALO, A_CONV_COLS), F32),
                        pltpu.VMEM((hp, 2, A_CHUNK + 2 * HALO, A_CONV_COLS), F32)],
        compiler_params=_params(("parallel", "parallel", "arbitrary"), VMEM_BIG),
    )(proj_hm, halo, cw_hm, beta, gc, norm_g, oraw, states, t_mats, conv_y, dog)


def _rope_tables(posf, inv_freq_row):
    t = posf.shape[0]
    tm = 512

    def body(p_ref, f_ref, c_ref, sa_ref, sb_ref):
        ang = p_ref[...] * f_ref[...]
        lane = lax.broadcasted_iota(jnp.int32, ang.shape, 1)
        half = ROPE_DIMS // 2
        c_ref[...] = jnp.where(lane < ROPE_DIMS, jnp.cos(ang), 1.0)
        sn = jnp.sin(ang)
        sa_ref[...] = jnp.where(lane < half, -sn, 0.0)
        sb_ref[...] = jnp.where((lane >= half) & (lane < ROPE_DIMS), sn, 0.0)

    row = pl.BlockSpec((tm, 128), lambda i: (i, 0))
    return pl.pallas_call(
        body, name="rope_tables", grid=(t // tm,),
        in_specs=[row, pl.BlockSpec((1, 128), lambda i: (0, 0))], out_specs=[row] * 3,
        out_shape=[jax.ShapeDtypeStruct((t, 128), F32)] * 3,
        compiler_params=_params(("parallel",)),
    )(posf, inv_freq_row)


def _qk_prep(proj, c, sa, sb, qg, kg, name, tm=512):
    t = proj.shape[0]

    def body(x_ref, c_ref, sa_ref, sb_ref, qg_ref, kg_ref, o_ref):
        cc, s1, s2 = c_ref[...], sa_ref[...], sb_ref[...]
        half = ROPE_DIMS // 2

        def one_head(lo, g):
            xv = x_ref[:, lo:lo + B_DH].astype(F32)
            ms = jnp.mean(xv * xv, axis=1, keepdims=True)
            yield
            xn = xv * lax.rsqrt(ms + EPS) * g
            r1, r2 = pltpu.roll(xn, 128 - half, 1), pltpu.roll(xn, half, 1)
            yield
            o_ref[:, lo:lo + B_DH] = (xn * cc + r1 * s1 + r2 * s2).astype(BF16)

        for which, g_ref in ((0, qg_ref), (1, kg_ref)):
            g = g_ref[...]
            _round_robin([one_head(which * B_W + h * B_DH, g) for h in range(B_HEADS)])
        o_ref[:, 2 * B_W:3 * B_W] = x_ref[:, 2 * B_W:3 * B_W]

    tab = pl.BlockSpec((tm, 128), lambda i: (i, 0))
    gain = pl.BlockSpec((1, B_DH), lambda i: (0, 0))
    return pl.pallas_call(
        body, name=name, grid=(t // tm,),
        in_specs=[pl.BlockSpec((tm, 3 * B_W), lambda i: (i, 0)), tab, tab, tab, gain, gain],
        out_specs=pl.BlockSpec((tm, 3 * B_W), lambda i: (i, 0)),
        out_shape=jax.ShapeDtypeStruct((t, 3 * B_W), BF16),
        compiler_params=_params(("parallel",), 40 * 1024 * 1024),
    )(proj, c, sa, sb, qg, kg)


def _qk_prep_bwd(proj, c, sa, sb, qg, kg, dq, dk, dv, dz, name, tm=512):
    t = proj.shape[0]
    out_w = 3 * B_W + (B_W if dz is not None else 0)

    def body(*refs):
        x_ref, c_ref, sa_ref, sb_ref, qg_ref, kg_ref, dq_ref, dk_ref, dv_ref = refs[:9]
        if dz is not None:
            dz_ref, o_ref, dgain_ref = refs[9:]
        else:
            o_ref, dgain_ref = refs[9:]
        i = pl.program_id(0)

        @pl.when(i == 0)
        def _():
            dgain_ref[...] = jnp.zeros_like(dgain_ref)

        cc, s1, s2 = c_ref[...], sa_ref[...], sb_ref[...]
        half = ROPE_DIMS // 2

        def one_head(which, h, g, d_ref, parts):
            lo = which * B_W + h * B_DH
            xv = x_ref[:, lo:lo + B_DH].astype(F32)
            d_out = d_ref[:, h * B_DH:(h + 1) * B_DH].astype(F32)
            ms = jnp.mean(xv * xv, axis=1, keepdims=True)
            r1, r2 = pltpu.roll(d_out * s1, half, 1), pltpu.roll(d_out * s2, 128 - half, 1)
            yield
            r = lax.rsqrt(ms + EPS)
            xh = xv * r
            d_xn = d_out * cc + r1 + r2
            parts.append(jnp.sum(d_xn * xh, axis=0, keepdims=True))
            d_xh = d_xn * g
            dot = jnp.mean(d_xh * xh, axis=1, keepdims=True)
            yield
            o_ref[:, lo:lo + B_DH] = (r * (d_xh - xh * dot)).astype(BF16)

        for which, g_ref, d_ref in ((0, qg_ref, dq_ref), (1, kg_ref, dk_ref)):
            parts = []
            _round_robin([one_head(which, h, g_ref[...], d_ref, parts) for h in range(B_HEADS)])
            acc = parts[0]
            for part in parts[1:]:
                acc = acc + part
            dgain_ref[which:which + 1, :] += acc
        o_ref[:, 2 * B_W:3 * B_W] = dv_ref[...]
        if dz is not None:
            o_ref[:, 3 * B_W:4 * B_W] = dz_ref[...]

    tab = pl.BlockSpec((tm, 128), lambda i: (i, 0))
    gain = pl.BlockSpec((1, B_DH), lambda i: (0, 0))
    grad = pl.BlockSpec((tm, B_W), lambda i: (i, 0))
    in_specs = [pl.BlockSpec((tm, 2 * B_W), lambda i: (i, 0)), tab, tab, tab, gain, gain, grad, grad, grad]
    args = [proj, c, sa, sb, qg, kg, dq, dk, dv]
    if dz is not None:
        in_specs.append(grad)
        args.append(dz)
    return pl.pallas_call(
        body, name=name, grid=(t // tm,), in_specs=in_specs,
        out_specs=[pl.BlockSpec((tm, out_w), lambda i: (i, 0)), pl.BlockSpec((8, B_DH), lambda i: (0, 0))],
        out_shape=[jax.ShapeDtypeStruct((t, out_w), BF16), jax.ShapeDtypeStruct((8, B_DH), F32)],
        compiler_params=_params(("arbitrary",), 40 * 1024 * 1024),
    )(*args)


def _attn_masks():
    qi = lax.broadcasted_iota(jnp.int32, (B_BLK, 2 * B_BLK), 0)
    kj = lax.broadcasted_iota(jnp.int32, (B_BLK, 2 * B_BLK), 1)
    two = (kj >= qi) & (kj <= qi + B_BLK)
    q1 = lax.broadcasted_iota(jnp.int32, (B_BLK, B_BLK), 0)
    k1 = lax.broadcasted_iota(jnp.int32, (B_BLK, B_BLK), 1)
    return k1 <= q1, two


def _lane_pick(ref_rows, h):
    lane = lax.broadcasted_iota(jnp.int32, ref_rows.shape, 1)
    return jnp.sum(jnp.where(lane == h, ref_rows, 0.0), axis=1, keepdims=True)


B_ROWS = 2048


def _attn_schedule(nb, sb, block):
    way = 16

    def run(items):
        for at in range(0, len(items), way):
            _round_robin([block(*it) for it in items[at:at + way]])

    run([(si, 0, True) for si in range(sb)])
    if nb == 1:
        return
    per = max(1, way // sb)
    lead = 1 + (nb - 1) % per
    if lead > 1:
        run([(si, i, False) for i in range(1, lead) for si in range(sb)])

    def step(it, carry):
        run([(si, lead + it * per + u, False) for u in range(per) for si in range(sb)])
        return carry

    lax.fori_loop(0, (nb - lead) // per, step, 0)


def _attn_rows(i, first):
    if first:
        return pl.ds(0, B_BLK), pl.ds(0, B_BLK)
    rows = pl.ds(pl.multiple_of(i * B_BLK, B_BLK), B_BLK)
    return rows, pl.ds(pl.multiple_of((i - 1) * B_BLK, B_BLK), 2 * B_BLK)


def _attn_fwd(qkv, name):
    ns, ln, _ = qkv.shape
    nb = ln // B_BLK
    sb = B_ROWS // ln
    scale = B_DH ** -0.5

    def body(q_ref, k_ref, v_ref, o_ref, lse_ref):
        h = pl.program_id(1)
        mask1, mask2 = _attn_masks()
        lane = lax.broadcasted_iota(jnp.int32, (B_BLK, B_HEADS), 1)

        @pl.when(h == 0)
        def _():
            lse_ref[...] = jnp.zeros_like(lse_ref)

        def block(si, i, first):
            rows, win = _attn_rows(i, first)
            mask = mask1 if first else mask2
            sc = jnp.where(mask, _dot(q_ref[si, rows, :], k_ref[si, win, :], 1, 1) * scale, -1e30)
            yield
            m = jnp.max(sc, axis=1, keepdims=True)
            p = jnp.exp(sc - m)
            l = jnp.sum(p, axis=1, keepdims=True)
            pv = _dot(p, v_ref[si, win, :], 1, 0)
            yield
            o_ref[si, rows, :] = (pv / l).astype(BF16)
            lse_ref[si, rows, :] = jnp.where(lane == h, m + jnp.log(l), lse_ref[si, rows, :])

        _attn_schedule(nb, sb, block)

    head = lambda off: pl.BlockSpec((sb, ln, B_DH), lambda s, h: (s, 0, off + h))
    return pl.pallas_call(
        body, name=name, grid=(ns // sb, B_HEADS),
        in_specs=[head(0), head(B_HEADS), head(2 * B_HEADS)],
        out_specs=[head(0), pl.BlockSpec((sb, ln, B_HEADS), lambda s, h: (s, 0, 0))],
        out_shape=[jax.ShapeDtypeStruct((ns, ln, B_W), BF16), jax.ShapeDtypeStruct((ns, ln, B_HEADS), F32)],
        compiler_params=_params(("parallel", "arbitrary")),
    )(qkv, qkv, qkv)


def _attn_bwd(qkv, d_o, lse_joint, delta, name):
    ns, ln, _ = qkv.shape
    nb = ln // B_BLK
    sb = B_ROWS // ln
    scale = B_DH ** -0.5

    def body(q_ref, k_ref, v_ref, do_ref, lj_ref, dl_ref, dq_ref, dk_out, dv_out, dk_ref, dv_ref):
        h = pl.program_id(1)
        mask1, mask2 = _attn_masks()
        dk_ref[...] = jnp.zeros_like(dk_ref)
        dv_ref[...] = jnp.zeros_like(dv_ref)

        def block(si, i, first):
            rows, win = _attn_rows(i, first)
            mask = mask1 if first else mask2
            q = q_ref[si, rows, :]
            d_out = do_ref[si, rows, :]
            l_col = _lane_pick(lj_ref[si, rows, :], h)
            d_col = _lane_pick(dl_ref[si, rows, :], h)
            sc = _dot(q, k_ref[si, win, :], 1, 1) * scale
            d_p = _dot(d_out, v_ref[si, win, :], 1, 1)
            yield
            p = jnp.exp(jnp.where(mask, sc - l_col, -1e30))
            d_s = p * (d_p - d_col) * scale
            d_q = _dot(d_s, k_ref[si, win, :], 1, 0)
            d_k = _dot(d_s, q, 0, 0)
            d_v = _dot(p, d_out, 0, 0)
            yield
            dq_ref[si, rows, :] = d_q.astype(BF16)
            dk_ref[si, win, :] += d_k
            dv_ref[si, win, :] += d_v

        _attn_schedule(nb, sb, block)
        dk_out[...] = dk_ref[...].astype(BF16)
        dv_out[...] = dv_ref[...].astype(BF16)

    head = lambda off: pl.BlockSpec((sb, ln, B_DH), lambda s, h: (s, 0, off + h))
    small = pl.BlockSpec((sb, ln, B_HEADS), lambda s, h: (s, 0, 0))
    return pl.pallas_call(
        body, name=name, grid=(ns // sb, B_HEADS),
        in_specs=[head(0), head(B_HEADS), head(2 * B_HEADS), head(0), small, small],
        out_specs=[head(0)] * 3,
        out_shape=[jax.ShapeDtypeStruct((ns, ln, B_W), BF16)] * 3,
        scratch_shapes=[pltpu.VMEM((sb, ln, B_DH), F32)] * 2,
        compiler_params=_params(("parallel", "parallel")),
    )(qkv, qkv, qkv, d_o, lse_joint, delta)


def _merge_weights(lse_refs):
    ls = [r[...] for r in lse_refs]
    m = jnp.maximum(jnp.maximum(ls[0], ls[1]), ls[2])
    es = [jnp.exp(l - m) for l in ls]
    tot = es[0] + es[1] + es[2]
    return [e / tot for e in es], m + jnp.log(tot)


def _merge_fwd(outs, lses, proj0, tm=512):
    t = outs[0].shape[0]

    def body(o0, o1, o2, l0, l1, l2, z_ref, og_ref):
        wts, _ = _merge_weights((l0, l1, l2))

        def one_head(h):
            cols = slice(h * B_DH, (h + 1) * B_DH)
            w0, w1, w2 = (jnp.broadcast_to(w[:, h:h + 1], (tm, B_DH)) for w in wts)
            yield
            o = w0 * o0[:, cols] + w1 * o1[:, cols] + w2 * o2[:, cols]
            og_ref[:, cols] = (o * _silu(z_ref[:, cols].astype(F32))).astype(BF16)

        _round_robin([one_head(h) for h in range(B_HEADS)])

    wide = pl.BlockSpec((tm, B_W), lambda i: (i, 0))
    small = pl.BlockSpec((tm, B_HEADS), lambda i: (i, 0))
    return pl.pallas_call(
        body, name="merge_fwd", grid=(t // tm,),
        in_specs=[wide] * 3 + [small] * 3 + [pl.BlockSpec((tm, B_W), lambda i: (i, 3))],
        out_specs=wide, out_shape=jax.ShapeDtypeStruct((t, B_W), BF16),
        compiler_params=_params(("parallel",)),
    )(*outs, *lses, proj0)


def _merge_bwd(outs, lses, proj0, d_og, tm=512):
    t = outs[0].shape[0]

    def body(o0, o1, o2, l0, l1, l2, z_ref, dog_ref, do_ref, lj_ref, dl_ref, dz_ref):
        wts, lj = _merge_weights((l0, l1, l2))
        lj_ref[...] = lj
        lane = lax.broadcasted_iota(jnp.int32, (tm, B_HEADS), 1)
        sums = [None] * B_HEADS

        def one_head(h):
            cols = slice(h * B_DH, (h + 1) * B_DH)
            w0, w1, w2 = (jnp.broadcast_to(w[:, h:h + 1], (tm, B_DH)) for w in wts)
            yield
            o = w0 * o0[:, cols] + w1 * o1[:, cols] + w2 * o2[:, cols]
            z = z_ref[:, cols].astype(F32)
            d_g = dog_ref[:, cols].astype(F32)
            gate, gate_slope = _silu_and_slope(z)
            d_out = d_g * gate
            dz_ref[:, cols] = (d_g * o * gate_slope).astype(BF16)
            do_ref[:, cols] = d_out.astype(BF16)
            sums[h] = jnp.sum(d_out * o, axis=1, keepdims=True)
            yield

        _round_robin([one_head(h) for h in range(B_HEADS)])
        delta = jnp.zeros((tm, B_HEADS), F32)
        for h in range(B_HEADS):
            delta = jnp.where(lane == h, sums[h], delta)
        dl_ref[...] = delta

    wide = pl.BlockSpec((tm, B_W), lambda i: (i, 0))
    small = pl.BlockSpec((tm, B_HEADS), lambda i: (i, 0))
    return pl.pallas_call(
        body, name="merge_bwd", grid=(t // tm,),
        in_specs=[wide] * 3 + [small] * 3 + [pl.BlockSpec((tm, B_W), lambda i: (i, 3)), wide],
        out_specs=[wide, small, small, wide],
        out_shape=[jax.ShapeDtypeStruct((t, B_W), BF16), jax.ShapeDtypeStruct((t, B_HEADS), F32),
                   jax.ShapeDtypeStruct((t, B_HEADS), F32), jax.ShapeDtypeStruct((t, B_W), BF16)],
        compiler_params=_params(("parallel",)),
    )(*outs, *lses, proj0, d_og)


def _adamw(w, g, m, v, name):
    r, c = w.shape
    tr = r
    for cand in (256, 128, 64, 32, 16, 8):
        if r % cand == 0:
            tr = cand
            break

    def body(w_ref, g_ref, m_ref, v_ref, d_ref, nm_ref, nv_ref):
        gv = g_ref[...]
        nm = ADAM_B1 * m_ref[...] + (1.0 - ADAM_B1) * gv
        nv = ADAM_B2 * v_ref[...] + (1.0 - ADAM_B2) * (gv * gv)
        m_hat = nm / (1.0 - ADAM_B1 ** ADAM_STEP)
        v_hat = nv / (1.0 - ADAM_B2 ** ADAM_STEP)
        d_ref[...] = -ADAM_LR * (m_hat / (jnp.sqrt(v_hat) + ADAM_EPS) + ADAM_WD * w_ref[...])
        nm_ref[...] = nm
        nv_ref[...] = nv

    blk = pl.BlockSpec((tr, c), lambda i: (i, 0))
    return pl.pallas_call(
        body, name=name, grid=(r // tr,), in_specs=[blk] * 4, out_specs=[blk] * 3,
        out_shape=[jax.ShapeDtypeStruct((r, c), F32)] * 3,
        compiler_params=_params(("parallel",)),
    )(w, g, m, v)


def _adam_update(w, gv, m, v):
    nm = ADAM_B1 * m + (1.0 - ADAM_B1) * gv
    nv = ADAM_B2 * v + (1.0 - ADAM_B2) * (gv * gv)
    m_hat = nm / (1.0 - ADAM_B1 ** ADAM_STEP)
    v_hat = nv / (1.0 - ADAM_B2 ** ADAM_STEP)
    return -ADAM_LR * (m_hat / (jnp.sqrt(v_hat) + ADAM_EPS) + ADAM_WD * w), nm, nv


def _adamw_shard(w, mine, theirs, m, v, half_index, name, tr=128):
    _, r, c = w.shape
    nhb = (r // 2) // tr

    def body(c_ref, w_ref, mine_ref, theirs_ref, m_ref, v_ref, g_ref, d_ref, nm_ref, nv_ref):
        is_mine = (pl.program_id(0) // nhb) == c_ref[0]
        gv = jnp.where(is_mine, mine_ref[...], theirs_ref[...])
        d, nm, nv = _adam_update(w_ref[...], gv, m_ref[...], v_ref[...])
        g_ref[...] = gv
        d_ref[...] = d
        nm_ref[...] = nm
        nv_ref[...] = nv

    full = pl.BlockSpec((None, tr, c), lambda i, cc: (0, i, 0))
    half = pl.BlockSpec((tr, c), lambda i, cc: (i % nhb, 0))
    return pl.pallas_call(
        body, name=name,
        grid_spec=pltpu.PrefetchScalarGridSpec(
            num_scalar_prefetch=1, grid=(2 * nhb,),
            in_specs=[full, half, half, full, full], out_specs=[full] * 4),
        out_shape=[jax.ShapeDtypeStruct(w.shape, F32)] * 4,
        compiler_params=_params(("parallel",), 40 * 1024 * 1024),
    )(half_index, w, mine, theirs, m, v)


def _adamw_shard_cols(w, mine, theirs, m, v, half_index, name, steps=20):
    c, _, r = w.shape
    tc = c // steps
    assert tc * steps == c

    def body(c_ref, w_ref, mine_ref, theirs_ref, m_ref, v_ref, g_ref, d_ref, nm_ref, nv_ref):
        first = jnp.where(c_ref[0] == 0, mine_ref[...], theirs_ref[...])
        second = jnp.where(c_ref[0] == 0, theirs_ref[...], mine_ref[...])
        for lo, gv in ((0, first), (r // 2, second)):
            cols = slice(lo, lo + r // 2)
            d, nm, nv = _adam_update(w_ref[:, :, cols], gv, m_ref[:, :, cols], v_ref[:, :, cols])
            g_ref[:, :, cols] = gv
            d_ref[:, :, cols] = d
            nm_ref[:, :, cols] = nm
            nv_ref[:, :, cols] = nv

    full = pl.BlockSpec((tc, 1, r), lambda i, cc: (i, 0, 0))
    half = pl.BlockSpec((tc, 1, r // 2), lambda i, cc: (i, 0, 0))
    return pl.pallas_call(
        body, name=name,
        grid_spec=pltpu.PrefetchScalarGridSpec(
            num_scalar_prefetch=1, grid=(steps,),
            in_specs=[full, half, half, full, full], out_specs=[full] * 4),
        out_shape=[jax.ShapeDtypeStruct(w.shape, F32)] * 4,
        compiler_params=_params(("parallel",), 40 * 1024 * 1024),
    )(half_index, w, mine, theirs, m, v)


def _pair_sum(own, other, half_index, name, tr=256):
    _, r, c = own.shape
    rh = r // 2
    tr = min(tr, rh)
    nrb = rh // tr

    def body(c_ref, own_ref, oth_ref, out_ref):
        out_ref[...] = (own_ref[...] + oth_ref[...].astype(F32)).astype(BF16)

    return pl.pallas_call(
        body, name=name,
        grid_spec=pltpu.PrefetchScalarGridSpec(
            num_scalar_prefetch=1, grid=(N_CHIPS, nrb),
            in_specs=[pl.BlockSpec((None, tr, c), lambda k, i, cc: (k, cc[0] * nrb + i, 0)),
                      pl.BlockSpec((None, tr, c), lambda k, i, cc: (k, i, 0))],
            out_specs=pl.BlockSpec((None, tr, c), lambda k, i, cc: (k, i, 0))),
        out_shape=jax.ShapeDtypeStruct((N_CHIPS, rh, c), BF16),
        compiler_params=_params(("parallel", "parallel")),
    )(half_index, own, other)


def _chip_sum(sums, others, chip_index, name, tr=256):
    _, r, c = sums.shape
    tr = min(tr, r)

    def body(k_ref, own_ref, oth_ref, out_ref):
        acc = own_ref[...].astype(F32)
        for j in range(N_CHIPS - 1):
            acc = acc + oth_ref[j].astype(F32)
        out_ref[...] = acc

    return pl.pallas_call(
        body, name=name,
        grid_spec=pltpu.PrefetchScalarGridSpec(
            num_scalar_prefetch=1, grid=(r // tr,),
            in_specs=[pl.BlockSpec((None, tr, c), lambda i, kk: (kk[0], i, 0)),
                      pl.BlockSpec((N_CHIPS - 1, tr, c), lambda i, kk: (0, i, 0))],
            out_specs=pl.BlockSpec((tr, c), lambda i, kk: (i, 0))),
        out_shape=jax.ShapeDtypeStruct((r, c), F32),
        compiler_params=_params(("parallel",)),
    )(chip_index, sums, others)


HBM = pl.BlockSpec(memory_space=pltpu.HBM)


def _place():
    x, y, c = lax.axis_index("x"), lax.axis_index("y"), lax.axis_index("c")
    chips = [(1 - x, y), (x, 1 - y), (1 - x, 1 - y)]
    return x, y, c, chips


def _sibling_forward(land):
    def body(in_ref, out_ref, send, recv):
        x, y, c, chips = _place()
        rh = out_ref.shape[1] // 2
        cps = []
        for j, (px, py) in enumerate(chips):
            slot = out_ref.at[2 * px + py, pl.ds(c * rh, rh)]
            cp = pltpu.make_async_remote_copy(
                src_ref=slot, dst_ref=slot, send_sem=send.at[j], recv_sem=recv.at[j],
                device_id=(x, y, 1 - c), device_id_type=MESH)
            cp.start()
            cps.append(cp)
        for j, (px, py) in enumerate(chips):
            slot = out_ref.at[2 * px + py, pl.ds((1 - c) * rh, rh)]
            pltpu.make_async_remote_copy(
                src_ref=slot, dst_ref=slot, send_sem=send.at[j], recv_sem=recv.at[j],
                device_id=(x, y, 1 - c), device_id_type=MESH).wait_recv()
        for cp in cps:
            cp.wait_send()

    return pl.pallas_call(
        body, name="first_weights_sibling_forward", in_specs=[HBM], out_specs=HBM,
        out_shape=jax.ShapeDtypeStruct(land.shape, land.dtype), input_output_aliases={0: 0},
        scratch_shapes=[pltpu.SemaphoreType.DMA((3,)), pltpu.SemaphoreType.DMA((3,))],
    )(land)


def _sibling_swap(halves, name):
    na = len(halves)

    def body(*refs):
        ins, outs = refs[:na], refs[na:2 * na]
        send, recv = refs[2 * na:]
        x, y, c, _ = _place()
        cps = []
        for i in range(na):
            cp = pltpu.make_async_remote_copy(
                src_ref=ins[i], dst_ref=outs[i], send_sem=send.at[i], recv_sem=recv.at[i],
                device_id=(x, y, 1 - c), device_id_type=MESH)
            cp.start()
            cps.append(cp)
        for cp in cps:
            cp.wait()

    out_shape = [jax.ShapeDtypeStruct(h.shape, h.dtype) for h in halves]
    return pl.pallas_call(
        body, name=name, in_specs=[HBM] * na, out_specs=[HBM] * na, out_shape=out_shape,
        scratch_shapes=[pltpu.SemaphoreType.DMA((na,)), pltpu.SemaphoreType.DMA((na,))],
    )(*halves)


SEM = pl.BlockSpec(memory_space=pltpu.SEMAPHORE)
ANY = pl.BlockSpec(memory_space=pl.ANY)
EFFECT = pltpu.SideEffectType.DATAFLOW_SIDE_EFFECTING


def _split_copy_start(name, plan, srcs, lands, after):
    ns, nl = len(srcs), len(lands)

    def body(*refs):
        src_refs, land_refs = refs[:ns], refs[ns:ns + nl]
        send, recv = refs[ns + nl + 1], refs[ns + nl + 2]
        token = refs[-1]
        outgoing, _ = plan(src_refs, land_refs)
        for src, dst, dev, si, ri in outgoing:
            pltpu.make_async_remote_copy(src_ref=src, dst_ref=dst, send_sem=send.at[si], recv_sem=recv.at[ri],
                                         device_id=dev, device_id_type=MESH).start()
        token[...] = jnp.zeros_like(token)

    n_out, n_in = plan.counts
    thru = [pltpu.HBM(a.shape, a.dtype) for a in list(srcs) + list(lands)]
    res = pl.pallas_call(
        body, name=name,
        out_shape=[pltpu.SemaphoreType.DMA((n_out,)), pltpu.SemaphoreType.DMA((n_in,))] + thru
        + [jax.ShapeDtypeStruct((8, 128), F32)],
        in_specs=[HBM] * (ns + nl) + [ANY],
        out_specs=[SEM, SEM] + [HBM] * (ns + nl) + [pl.BlockSpec(memory_space=pltpu.VMEM)],
        input_output_aliases={i: 2 + i for i in range(ns + nl)},
        compiler_params=pltpu.CompilerParams(has_side_effects=EFFECT),
    )(*[pltpu.with_memory_space_constraint(a, pltpu.HBM) for a in list(srcs) + list(lands)], after)
    return res[0], res[1], res[2:2 + ns], res[2 + ns:2 + ns + nl], res[-1]


def _split_copy_wait(name, plan, send, recv, srcs, lands, after):
    ns, nl = len(srcs), len(lands)
    after = list(after) if isinstance(after, (list, tuple)) else [after]

    def body(*refs):
        src_refs, land_refs = refs[:ns], refs[ns:ns + nl]
        send_ref, recv_ref = refs[ns + nl], refs[ns + nl + 1]
        outgoing, arrivals = plan(src_refs, land_refs)
        for src, dst, dev, si, ri in outgoing:
            pltpu.make_async_remote_copy(src_ref=src, dst_ref=dst, send_sem=send_ref.at[si], recv_sem=recv_ref.at[ri],
                                         device_id=dev, device_id_type=MESH).wait_send()
        for view, ri in arrivals:
            pltpu.make_async_remote_copy(src_ref=view, dst_ref=view, send_sem=send_ref.at[0], recv_sem=recv_ref.at[ri],
                                         device_id=_place()[:3], device_id_type=MESH).wait_recv()

    thru = [pltpu.HBM(a.shape, a.dtype) for a in list(srcs) + list(lands)]
    res = pl.pallas_call(
        body, name=name, out_shape=thru,
        in_specs=[HBM] * (ns + nl) + [SEM, SEM] + [ANY] * len(after), out_specs=[HBM] * (ns + nl),
        input_output_aliases={i: i for i in range(ns + nl)},
        compiler_params=pltpu.CompilerParams(has_side_effects=EFFECT),
    )(*srcs, *lands, send, recv, *after)
    return res[:ns], res[ns:]


def _gather_plan(n_arrays):
    def plan(src_refs, land_refs):
        x, y, c, chips = _place()
        me = 2 * x + y
        outgoing, arrivals = [], []
        for i in range(n_arrays):
            rh = src_refs[i].shape[0] // 2
            mine = pl.ds(c * rh, rh)
            for j, (px, py) in enumerate(chips):
                for delta in range(2):
                    tc = c ^ delta
                    outgoing.append((src_refs[i].at[mine], land_refs[i].at[me, mine], (px, py, tc),
                                     6 * i + 2 * j + delta, 6 * i + 2 * j + delta))
                    theirs = pl.ds(tc * rh, rh)
                    arrivals.append((land_refs[i].at[2 * px + py, theirs], 6 * i + 2 * j + delta))
        return outgoing, arrivals

    plan.counts = (6 * n_arrays, 6 * n_arrays)
    return plan


def _first_gather_plan():
    def plan(src_refs, land_refs):
        x, y, c, chips = _place()
        me = 2 * x + y
        rh = src_refs[0].shape[0] // 2
        mine = pl.ds(c * rh, rh)
        outgoing, arrivals = [], []
        for j, (px, py) in enumerate(chips):
            outgoing.append((src_refs[0].at[mine], land_refs[0].at[me, mine], (px, py, c), j, j))
            arrivals.append((land_refs[0].at[2 * px + py, mine], j))
            outgoing.append((src_refs[1], land_refs[1].at[me], (px, py, c), 3 + j, 3 + j))
            arrivals.append((land_refs[1].at[2 * px + py], 3 + j))
        return outgoing, arrivals

    plan.counts = (6, 6)
    return plan


def _exchange_plan(n_arrays):
    def plan(src_refs, land_refs):
        x, y, c, chips = _place()
        outgoing, arrivals = [], []
        for i in range(n_arrays):
            for j, (px, py) in enumerate(chips):
                outgoing.append((src_refs[i].at[2 * px + py], land_refs[i].at[j], (px, py, c), 3 * i + j, 3 * i + j))
                arrivals.append((land_refs[i].at[j], 3 * i + j))
        return outgoing, arrivals

    plan.counts = (3 * n_arrays, 3 * n_arrays)
    return plan


def _small_allreduce(vec):
    r, cdim = vec.shape
    n_dev = 8

    def body(v_ref, out_ref, buf, send, recv):
        x, y, c, _ = _place()
        me = 4 * x + 2 * y + c
        buf[me] = v_ref[...]
        cps = []
        for k in range(1, n_dev):
            dx, dy, dc = (k >> 2) & 1, (k >> 1) & 1, k & 1
            peer = (x ^ dx, y ^ dy, c ^ dc)
            cp = pltpu.make_async_remote_copy(
                src_ref=v_ref, dst_ref=buf.at[me], send_sem=send.at[k - 1], recv_sem=recv.at[k - 1],
                device_id=peer, device_id_type=MESH)
            cp.start()
            cps.append(cp)
        for k in range(1, n_dev):
            dx, dy, dc = (k >> 2) & 1, (k >> 1) & 1, k & 1
            src = 4 * (x ^ dx) + 2 * (y ^ dy) + (c ^ dc)
            slot = buf.at[src]
            pltpu.make_async_remote_copy(
                src_ref=slot, dst_ref=slot, send_sem=send.at[k - 1], recv_sem=recv.at[k - 1],
                device_id=(x ^ dx, y ^ dy, c ^ dc), device_id_type=MESH).wait_recv()
        for cp in cps:
            cp.wait_send()
        acc = buf[0]
        for k in range(1, n_dev):
            acc = acc + buf[k]
        out_ref[...] = acc

    vm = pl.BlockSpec(memory_space=pltpu.VMEM)
    return pl.pallas_call(
        body, name="small_allreduce", in_specs=[vm], out_specs=vm,
        out_shape=jax.ShapeDtypeStruct((r, cdim), F32),
        scratch_shapes=[pltpu.VMEM((n_dev, r, cdim), F32), pltpu.SemaphoreType.DMA((n_dev - 1,)),
                        pltpu.SemaphoreType.DMA((n_dev - 1,))],
    )(vec)


def _a_cols_to_head_major(w):
    lead = w.shape[:-1]
    q = w[..., :A_QK].reshape(lead + (A_HEADS, A_DK))
    k = w[..., A_QK:2 * A_QK].reshape(lead + (A_HEADS, A_DK))
    v = w[..., 2 * A_QK:2 * A_QK + A_VW].reshape(lead + (A_HEADS, A_DV))
    z = w[..., 2 * A_QK + A_VW:].reshape(lead + (A_HEADS, A_DV))
    return jnp.concatenate([q, k, v, z], axis=-1).reshape(lead + (A_HEADS * A_HEAD_COLS,))


def _a_cols_from_head_major(w):
    lead = w.shape[:-1]
    w = w.reshape(lead + (A_HEADS, A_HEAD_COLS))
    parts = [w[..., :A_DK], w[..., A_DK:2 * A_DK], w[..., 2 * A_DK:2 * A_DK + A_DV], w[..., 2 * A_DK + A_DV:]]
    return jnp.concatenate([p.reshape(lead + (-1,)) for p in parts], axis=-1)


def _conv_cols_to_head_major(w):
    lead = w.shape[:-1]
    q = w[..., :A_QK].reshape(lead + (A_HEADS, A_DK))
    k = w[..., A_QK:2 * A_QK].reshape(lead + (A_HEADS, A_DK))
    v = w[..., 2 * A_QK:].reshape(lead + (A_HEADS, A_DV))
    return jnp.concatenate([q, k, v], axis=-1).reshape(lead + (A_HEADS * A_CONV_COLS,))


def _conv_cols_from_head_major(w):
    lead = w.shape[:-1]
    w = w.reshape(lead + (A_HEADS, A_CONV_COLS))
    parts = [w[..., :A_DK], w[..., A_DK:2 * A_DK], w[..., 2 * A_DK:]]
    return jnp.concatenate([p.reshape(lead + (-1,)) for p in parts], axis=-1)


def _to_stream(a, bn, d):
    rest = a.shape[1:]
    s = a.shape[0] // bn
    a = a.reshape((bn, s // d, d) + rest)
    a = jnp.swapaxes(a, 1, 2)
    return a.reshape((bn * d, s // d) + rest)


def _from_stream(a, bn, d):
    rest = a.shape[2:]
    ln = a.shape[1]
    a = a.reshape((bn, d, ln) + rest)
    a = jnp.swapaxes(a, 1, 2)
    return a.reshape((bn * ln * d,) + rest)


B_SUB = 512
B_SHARD_BLOCKS = (3 * B_GROUPS * B_W + B_W) // N_CHIPS // B_SUB


def _b_block(gi, jj):
    nb = (B_GROUPS * (jj // 2) + gi) * 2 + jj % 2
    return nb // B_SHARD_BLOCKS, nb % B_SHARD_BLOCKS


def _shard_major(g, ncols):
    r = g.shape[0]
    return jnp.swapaxes(g.reshape(r, N_CHIPS, ncols), 0, 1)


def _pack_rows(items):
    rows, offs = [], []
    at = 0
    for a in items:
        flat = a.reshape(-1).astype(F32)
        nr = -(-flat.shape[0] // 1024) * 8
        flat = jnp.pad(flat, (0, nr * 128 - flat.shape[0]))
        rows.append(flat.reshape(nr, 128))
        offs.append((at, nr, a.shape))
        at += nr
    return jnp.concatenate(rows, axis=0), offs


def _unpack_rows(packed, offs):
    out = []
    for at, nr, shape in offs:
        size = int(np.prod(shape)) if len(shape) else 1
        out.append(packed[at:at + nr].reshape(-1)[:size].reshape(shape))
    return out


def _local_step(x, positions, loss_target, norm_g, a_log, a_dt_bias, a_norm_g, b_q_norm_g, b_k_norm_g,
                start_token, first_weights, late_weights, b_grads_ready, a_grads_ready):
    bn, s, d = x.shape
    t = bn * s
    n_chunks = s // A_CHUNK
    x0 = x.reshape(t, d)
    h0 = _rms_fwd(x0, norm_g[0:1] + start_token, "rms0_fwd")
    inv_freq = ROPE_THETA ** (-jnp.arange(0, ROPE_DIMS, 2, dtype=F32) / ROPE_DIMS)
    freq_row = jnp.concatenate([inv_freq, inv_freq, jnp.zeros((128 - ROPE_DIMS,), F32)]).reshape(1, 128)
    posf = jnp.broadcast_to(positions.astype(F32).reshape(t, 1), (t, 128)) + start_token
    tabs = _rope_tables(posf, freq_row)
    tabs_s = [tabs if dil == 1 else [_to_stream(tb, bn, dil).reshape(t, 128) for tb in tabs] for dil in B_DIL]
    wa_in, conv_w, late_token = first_weights([h0] + [tb for ts in tabs_s for tb in ts])
    wa_main = _a_cols_to_head_major(wa_in[:, :A_MAIN])
    wa_tail = jnp.pad(wa_in[:, A_MAIN:], ((0, 0), (0, 128 - 2 * A_HEADS))) + late_token.astype(BF16)
    cw_hm = _conv_cols_to_head_major(conv_w)

    proj_a = _matmul(h0, wa_main, "nn", F32, "a_in_main")
    tail_a = _matmul(h0, wa_tail, "nn", F32, "a_in_tail")
    tail_t = jnp.swapaxes(tail_a[:, :2 * A_HEADS].reshape(bn, s, 2 * A_HEADS), 1, 2)
    tail_t = tail_t.reshape(bn, 2 * A_HEADS, n_chunks, A_CHUNK)
    beta, gc = _gdn_prep(tail_t, a_log[0], a_dt_bias[0])
    proj_a3 = proj_a.reshape(bn, s, A_MAIN)
    og_a, oraw_a, states, t_mats, conv_y = _gdn_fwd(proj_a3, cw_hm, beta, gc, a_norm_g)
    wa_out, wb_in, wb_out = late_weights(og_a)
    b_cols = [4 * B_W] + [3 * B_W] * (B_GROUPS - 1)
    x1, h1 = _out_proj(og_a.reshape(t, A_VW), wa_out, x0, "a_out", norm_g=norm_g[1:2])

    h1_s, proj_b, qkv_b, o_b, lse_b = [], [], [], [], []
    for gi, dil in enumerate(B_DIL):
        hs = h1 if dil == 1 else _to_stream(h1, bn, dil).reshape(t, d)
        ts = tabs_s[gi]
        pj = _matmul(hs, wb_in, "nn", BF16, f"b_in_g{gi}", tm=2048, tn=B_SUB, n=b_cols[gi], b_spec=pl.BlockSpec(
            (None, d, B_SUB), lambda i, j, kk, gi=gi: (_b_block(gi, j)[0], kk, _b_block(gi, j)[1])))
        qkv = _qk_prep(pj, *ts, b_q_norm_g[0, gi:gi + 1], b_k_norm_g[0, gi:gi + 1], f"qk_prep_g{gi}")
        o_s, lse_s = _attn_fwd(qkv.reshape(bn * dil, s // dil, 3 * B_W), f"attn_fwd_g{gi}")
        h1_s.append(hs), proj_b.append(pj), qkv_b.append(qkv)
        o_b.append(o_s.reshape(t, B_W) if dil == 1 else _from_stream(o_s, bn, dil))
        lse_b.append(lse_s.reshape(t, B_HEADS) if dil == 1 else _from_stream(lse_s, bn, dil))
    og_b = _merge_fwd(o_b, lse_b, proj_b[0])
    d_x2, loss_parts = _out_proj(og_b, wb_out, x1, "b_out_loss", target=loss_target.reshape(t, d))
    loss_local = jnp.sum(loss_parts)

    d_x2b = d_x2.astype(BF16)
    g_wb_out = _matmul(og_b, d_x2b, "tn", F32, "b_out_dw")
    d_og_b = _matmul(d_x2b, wb_out, "nt", BF16, "b_out_dx")
    d_o, lse_joint, delta, d_z = _merge_bwd(o_b, lse_b, proj_b[0], d_og_b)
    d_h1, g_qn, g_kn = [], [], []
    g_wb_in = lax.empty(wb_in.shape, F32)
    for gi, dil in enumerate(B_DIL):
        if dil == 1:
            do_s, lj_s, dl_s = d_o, lse_joint, delta
        else:
            do_s, lj_s, dl_s = (_to_stream(a, bn, dil).reshape(t, -1) for a in (d_o, lse_joint, delta))
        ns, ln = bn * dil, s // dil
        dq, dk, dv = _attn_bwd(qkv_b[gi].reshape(ns, ln, 3 * B_W), do_s.reshape(ns, ln, B_W),
                               lj_s.reshape(ns, ln, B_HEADS), dl_s.reshape(ns, ln, B_HEADS), f"attn_bwd_g{gi}")
        d_pj, d_gain = _qk_prep_bwd(proj_b[gi], *tabs_s[gi], b_q_norm_g[0, gi:gi + 1], b_k_norm_g[0, gi:gi + 1],
                                    dq.reshape(t, B_W), dk.reshape(t, B_W), dv.reshape(t, B_W),
                                    d_z if gi == 0 else None, f"qk_prep_bwd_g{gi}")
        g_wb_in = _matmul(h1_s[gi], d_pj, "tn", F32, f"b_in_dw_g{gi}", tn=B_SUB, tk=2048, into=(g_wb_in, pl.BlockSpec(
            (None, d, B_SUB), lambda i, j, kk, gi=gi: (_b_block(gi, j)[0], i, _b_block(gi, j)[1]))))
        dh = _matmul(d_pj, wb_in, "nt", BF16, f"b_in_dx_g{gi}", tm=2048, tk=B_SUB, n=d, b_spec=pl.BlockSpec(
            (None, d, B_SUB), lambda i, j, kk, gi=gi: (_b_block(gi, kk)[0], j, _b_block(gi, kk)[1])))
        d_h1.append(dh if dil == 1 else _from_stream(dh.reshape(ns, ln, d), bn, dil))
        g_qn.append(d_gain[0]), g_kn.append(d_gain[1])
    d_x1, g_norm1 = _rms_bwd(x1, norm_g[1:2], d_h1, d_x2, "rms1_bwd")

    d_x1b = d_x1.astype(BF16)
    g_wa_out = _matmul(og_a.reshape(t, A_VW), d_x1b, "tn", F32, "a_out_dw")
    b_token = b_grads_ready(g_wb_in, g_wb_out, g_wa_out)
    d_og_a = _matmul(d_x1b, wa_out, "nt", BF16, "a_out_dx")
    d_pa, d_gc, d_beta, d_cw, d_ng = _gdn_bwd(proj_a3, cw_hm, beta, gc, a_norm_g + b_token, oraw_a, states,
                                              t_mats, conv_y, d_og_a.reshape(bn, s, A_VW))
    d_tail_t, d_alog, d_dtb = _gdn_prep_bwd(tail_t, a_log[0], a_dt_bias[0], d_gc, d_beta)
    d_tail = jnp.swapaxes(d_tail_t.reshape(bn, 2 * A_HEADS, s), 1, 2).reshape(t, 2 * A_HEADS)
    d_tail = jnp.pad(d_tail, ((0, 0), (0, 128 - 2 * A_HEADS))).astype(BF16)
    d_pa = d_pa.reshape(t, A_MAIN)
    g_wa_main = _matmul(h0, d_pa, "tn", F32, "a_in_dw_main")
    g_wa_tail = _matmul(h0, d_tail, "tn", F32, "a_in_dw_tail")
    g_wa_in = jnp.concatenate([_a_cols_from_head_major(g_wa_main), g_wa_tail[:, :2 * A_HEADS]], axis=1)
    a_token = a_grads_ready(g_wa_in)
    d_h0t = _matmul(d_tail + a_token.astype(BF16), wa_tail, "nt", F32, "a_in_dx_tail")
    d_x0, g_norm0 = _in_proj_bwd(d_pa, wa_main, d_h0t, x0, norm_g[0:1], d_x1, "a_in_dx_rms0_bwd")

    gfull = {
        "norm_g": jnp.concatenate([g_norm0, g_norm1], axis=0), "a_w_in": g_wa_in,
        "a_conv_w": _conv_cols_from_head_major(jnp.sum(d_cw, axis=0)),
        "a_log": jnp.sum(d_alog[:, :, 0], axis=0), "a_dt_bias": jnp.sum(d_dtb[:, :, 0], axis=0),
        "a_norm_g": jnp.sum(d_ng[:, :, 0, :], axis=(0, 1)), "a_w_out": g_wa_out, "b_w_in": g_wb_in,
        "b_q_norm_g": jnp.stack(g_qn), "b_k_norm_g": jnp.stack(g_kn), "b_w_out": g_wb_out}
    return loss_local, d_x0.reshape(bn, s, d), gfull


def kernel(x, positions, norm_g, a_w_in, a_conv_w, a_log, a_dt_bias, a_norm_g, a_w_out, b_w_in, b_q_norm_g, b_k_norm_g, b_w_out, loss_target, m_norm_g, m_a_w_in, m_a_conv_w, m_a_log, m_a_dt_bias, m_a_norm_g, m_a_w_out, m_b_w_in, m_b_q_norm_g, m_b_k_norm_g, m_b_w_out, v_norm_g, v_a_w_in, v_a_conv_w, v_a_log, v_a_dt_bias, v_a_norm_g, v_a_w_out, v_b_w_in, v_b_q_norm_g, v_b_k_norm_g, v_b_w_out):
    d = x.shape[2]
    my_c = lax.axis_index("c")
    my_chip = 2 * lax.axis_index("x") + lax.axis_index("y")

    half_index = jnp.reshape(my_c, (1,)).astype(jnp.int32)
    chip_index = jnp.reshape(my_chip, (1,)).astype(jnp.int32)
    def landing(shard):
        return lax.dynamic_update_slice(lax.empty((N_CHIPS,) + shard.shape, shard.dtype), shard[None],
                                        (my_chip,) + (0,) * shard.ndim)

    first_shards = [a_w_in[0].astype(BF16), a_conv_w[0]]
    first_plan = _first_gather_plan()
    first = _split_copy_start("first_weights_start", first_plan, first_shards,
                              [landing(s) for s in first_shards], half_index)
    pending = {}
    late_shards = [(w[0] + first[4][0, 0]).astype(BF16) for w in (a_w_out, b_w_in, b_w_out)]
    late_lands = [landing(s) for s in late_shards]

    def first_weights(after):
        _, (ga_in, g_conv) = _split_copy_wait("first_weights_wait", first_plan, *first[:4],
                                              list(after) + late_lands)
        ga_in = _sibling_forward(ga_in)
        wa_in = jnp.concatenate([ga_in[k] for k in range(N_CHIPS)], axis=1)
        conv_w = jnp.concatenate([g_conv[k] for k in range(N_CHIPS)], axis=1)
        plan = _gather_plan(len(late_shards))
        pending["late"] = (plan,) + tuple(_split_copy_start(
            "late_weights_start", plan, late_shards, late_lands, conv_w))
        return wa_in, conv_w, pending["late"][5][0, 0]

    def late_weights(after):
        plan, send, recv, srcs, lands, _ = pending["late"]
        _, (ga_out, gb_in, gb_out) = _split_copy_wait("late_weights_wait", plan, send, recv, srcs, lands, after)
        return ga_out.reshape(A_VW, d), gb_in, gb_out.reshape(B_W, d)

    def reduce_to_chip_sums(mats, tag):
        half = lambda g: lax.dynamic_slice_in_dim(g, (1 - my_c) * (g.shape[1] // 2), g.shape[1] // 2, axis=1)
        recv_sib = _sibling_swap([half(g).astype(BF16) for g in mats], f"grad_{tag}_sibling_swap")
        return [_pair_sum(g, r, half_index, f"grad_{tag}_pair_sum_{i}") for i, (g, r) in enumerate(zip(mats, recv_sib))]

    def start_exchange(tag, mats):
        sums = reduce_to_chip_sums(mats, tag)
        lands = [lax.empty((N_CHIPS - 1,) + s.shape[1:], BF16) for s in sums]
        plan = _exchange_plan(len(mats))
        pending[tag] = (plan,) + tuple(_split_copy_start(f"grad_{tag}_exchange_start", plan, sums, lands, chip_index))
        return pending[tag][5][0, 0]

    def finish_exchange(tag, after):
        plan, send, recv, srcs, lands, _ = pending[tag]
        return _split_copy_wait(f"grad_{tag}_exchange_wait", plan, send, recv, srcs, lands, after)

    def b_grads_ready(g_wb_in, g_wb_out, g_wa_out):
        return start_exchange("b", [g_wb_in, g_wb_out.reshape(N_CHIPS, -1, d), g_wa_out.reshape(N_CHIPS, -1, d)])

    def a_grads_ready(g_wa_in):
        return start_exchange("a", [_shard_major(g_wa_in, a_w_in.shape[2])])

    loss_local, d_x0, gfull = _local_step(x, positions, loss_target, norm_g, a_log, a_dt_bias, a_norm_g,
                                          b_q_norm_g, b_k_norm_g, first[4][0, 0], first_weights, late_weights,
                                          b_grads_ready, a_grads_ready)

    small = [gfull["norm_g"], gfull["a_conv_w"], gfull["a_log"], gfull["a_dt_bias"], gfull["a_norm_g"],
             gfull["b_q_norm_g"], gfull["b_k_norm_g"], loss_local]
    packed, offs = _pack_rows(small)
    reduced = _small_allreduce(packed)
    g_norm, g_conv_all, g_alog, g_dtb, g_ang, g_q, g_k, loss = _unpack_rows(reduced, offs)
    g_conv_mine = lax.dynamic_slice_in_dim(g_conv_all, my_chip * a_conv_w.shape[2], a_conv_w.shape[2], axis=1)

    b_sums, b_received = finish_exchange("b", d_x0)
    a_sums, a_received = finish_exchange("a", reduced)
    chip_sums = [a_sums[0], b_sums[2], b_sums[0], b_sums[1]]
    received = [a_received[0], b_received[2], b_received[0], b_received[1]]
    halves = [_chip_sum(s, r, chip_index, f"grad_chip_sum_{i}") for i, (s, r) in enumerate(zip(chip_sums, received))]
    theirs = _sibling_swap(halves, "grad_sibling_join")
    big = ("a_w_in", "a_w_out", "b_w_in", "b_w_out")
    big_halves = dict(zip(big, zip(halves, theirs)))

    grads = {
        "norm_g": g_norm, "a_conv_w": g_conv_mine[None], "a_log": g_alog[None], "a_dt_bias": g_dtb[None],
        "a_norm_g": g_ang[None], "b_q_norm_g": g_q[None], "b_k_norm_g": g_k[None]}
    weights = {"norm_g": norm_g, "a_w_in": a_w_in, "a_conv_w": a_conv_w, "a_log": a_log, "a_dt_bias": a_dt_bias,
               "a_norm_g": a_norm_g, "a_w_out": a_w_out, "b_w_in": b_w_in, "b_q_norm_g": b_q_norm_g,
               "b_k_norm_g": b_k_norm_g, "b_w_out": b_w_out}
    m_in = {"norm_g": m_norm_g, "a_w_in": m_a_w_in, "a_conv_w": m_a_conv_w, "a_log": m_a_log,
            "a_dt_bias": m_a_dt_bias, "a_norm_g": m_a_norm_g, "a_w_out": m_a_w_out, "b_w_in": m_b_w_in,
            "b_q_norm_g": m_b_q_norm_g, "b_k_norm_g": m_b_k_norm_g, "b_w_out": m_b_w_out}
    v_in = {"norm_g": v_norm_g, "a_w_in": v_a_w_in, "a_conv_w": v_a_conv_w, "a_log": v_a_log,
            "a_dt_bias": v_a_dt_bias, "a_norm_g": v_a_norm_g, "a_w_out": v_a_w_out, "b_w_in": v_b_w_in,
            "b_q_norm_g": v_b_q_norm_g, "b_k_norm_g": v_b_k_norm_g, "b_w_out": v_b_w_out}
    names = list(weights)

    delta_w, new_m, new_v = {}, {}, {}
    for nm in big:
        mine, other = big_halves[nm]
        if weights[nm].shape[2] % 128:
            cols = lambda a: jnp.transpose(a, (2, 0, 1))
            half_cols = lambda a: jnp.transpose(a)[:, None, :]
            outs = _adamw_shard_cols(cols(weights[nm]), half_cols(mine), half_cols(other), cols(m_in[nm]),
                                     cols(v_in[nm]), half_index, f"adamw_{nm}")
            outs = [jnp.transpose(o, (1, 2, 0)) for o in outs]
        else:
            outs = _adamw_shard(weights[nm], mine, other, m_in[nm], v_in[nm], half_index, f"adamw_{nm}")
        grads[nm], delta_w[nm], new_m[nm], new_v[nm] = outs
    small_names = [nm for nm in names if nm not in big]
    packs = [_pack_rows([src[nm] for nm in small_names]) for src in (weights, grads, m_in, v_in)]
    offs = packs[0][1]
    dl, m2, v2 = _adamw(packs[0][0], packs[1][0], packs[2][0], packs[3][0], "adamw_small")
    for nm, a, b, c2 in zip(small_names, _unpack_rows(dl, offs), _unpack_rows(m2, offs), _unpack_rows(v2, offs)):
        delta_w[nm], new_m[nm], new_v[nm] = a, b, c2

    return (loss, d_x0, *[grads[nm] for nm in names], *[delta_w[nm] for nm in names],
            *[new_m[nm] for nm in names], *[new_v[nm] for nm in names])
```

```python
import jax
import jax.numpy as jnp
import numpy as np
from jax import lax
from jax.experimental import pallas as pl
from jax.experimental.pallas import tpu as pltpu

F32 = jnp.float32
BF16 = jnp.bfloat16
MESH = pl.DeviceIdType.MESH

EPS = 1e-6
A_HEADS = 8
A_DK = 128
A_DV = 256
A_QK = A_HEADS * A_DK
A_VW = A_HEADS * A_DV
A_MAIN = 2 * A_QK + 2 * A_VW
A_HEAD_COLS = 2 * A_DK + 2 * A_DV
A_CONV_COLS = 2 * A_DK + A_DV
A_CHUNK = 64
A_CONV = 4
B_GROUPS = 3
B_HEADS = 8
B_DH = 128
B_W = B_HEADS * B_DH
B_DIL = (1, 4, 16)
B_BLK = 128
ROPE_THETA = 500000.0
ROPE_DIMS = B_DH // 4
ADAM_LR, ADAM_B1, ADAM_B2, ADAM_EPS, ADAM_WD, ADAM_STEP = 0.001, 0.9, 0.999, 1e-08, 0.01, 10
N_CHIPS = 4
VMEM_BIG = 56 * 1024 * 1024


def _params(sem=None, vmem=None):
    return pltpu.CompilerParams(dimension_semantics=sem, vmem_limit_bytes=vmem)


def _dot(a, b, ca, cb):
    return lax.dot_general(a.astype(BF16), b.astype(BF16), (((ca,), (cb,)), ((), ())),
                           preferred_element_type=F32)


def _split3(a):
    hi = a.astype(BF16)
    r = a - hi.astype(F32)
    mid = r.astype(BF16)
    lo = (r - mid.astype(F32)).astype(BF16)
    return hi, mid, lo


def _sigmoid(y):
    return 1.0 / (1.0 + jnp.exp(-y))


def _silu(y):
    return y * _sigmoid(y)


def _silu_and_slope(y):
    s = _sigmoid(y)
    return y * s, s * (1.0 + y * (1.0 - s))


def _matmul(a, b, mode, out_dtype, name, res=None, tm=1024, tn=1024, tk=1024, n=None, b_spec=None, into=None):
    m, k = a.shape[::-1] if mode == "tn" else a.shape
    if n is None:
        n = b.shape[0] if mode == "nt" else b.shape[1]
    tm, tn, tk = min(tm, m), min(tn, n), min(tk, k)
    assert m % tm == 0 and n % tn == 0 and k % tk == 0, (name, a.shape, b.shape)
    nk = k // tk
    dims = {"nn": ((1,), (0,)), "nt": ((1,), (1,)), "tn": ((0,), (0,))}[mode]

    def body(*refs):
        a_ref, b_ref = refs[0], refs[1]
        r_ref = refs[2] if res is not None else None
        o_ref = refs[2 + (res is not None) + (into is not None)]
        prod = lax.dot_general(a_ref[...], b_ref[...], (dims, ((), ())), preferred_element_type=F32)

        def finish(r):
            if res is not None:
                r = r + r_ref[...]
            o_ref[...] = r.astype(out_dtype)

        if nk == 1:
            finish(prod)
            return
        acc = refs[-1]
        kk = pl.program_id(2)

        @pl.when(kk == 0)
        def _():
            acc[...] = prod

        @pl.when((kk > 0) & (kk < nk - 1))
        def _():
            acc[...] += prod

        @pl.when(kk == nk - 1)
        def _():
            finish(acc[...] + prod)

    a_spec = pl.BlockSpec((tm, tk), lambda i, j, kk: (i, kk))
    if mode == "tn":
        a_spec = pl.BlockSpec((tk, tm), lambda i, j, kk: (kk, i))
    if b_spec is None and mode == "nt":
        b_spec = pl.BlockSpec((tn, tk), lambda i, j, kk: (j, kk))
    elif b_spec is None:
        b_spec = pl.BlockSpec((tk, tn), lambda i, j, kk: (kk, j))
    in_specs = [a_spec, b_spec]
    args = [a, b]
    if res is not None:
        in_specs.append(pl.BlockSpec((tm, tn), lambda i, j, kk: (i, j)))
        args.append(res)
    out_spec = pl.BlockSpec((tm, tn), lambda i, j, kk: (i, j))
    out_shape = jax.ShapeDtypeStruct((m, n), out_dtype)
    aliases = {}
    if into is not None:
        assert res is None
        buf, out_spec = into
        out_shape = jax.ShapeDtypeStruct(buf.shape, buf.dtype)
        in_specs.append(ANY)
        args.append(buf)
        aliases = {2: 0}
    return pl.pallas_call(
        body, name=name, grid=(m // tm, n // tn, nk),
        in_specs=in_specs, out_specs=out_spec, out_shape=out_shape, input_output_aliases=aliases,
        scratch_shapes=[pltpu.VMEM((tm, tn), F32)] if nk > 1 else [],
        compiler_params=_params(("parallel", "parallel", "arbitrary"), 48 * 1024 * 1024),
    )(*args)


def _rms_fwd(x, g, name, tm=512):
    t, d = x.shape

    def body(x_ref, g_ref, h_ref):
        xv = x_ref[...]
        r = lax.rsqrt(jnp.mean(xv * xv, axis=-1, keepdims=True) + EPS)
        h_ref[...] = (xv * r * g_ref[...]).astype(BF16)

    return pl.pallas_call(
        body, name=name, grid=(t // tm,),
        in_specs=[pl.BlockSpec((tm, d), lambda i: (i, 0)), pl.BlockSpec((1, d), lambda i: (0, 0))],
        out_specs=pl.BlockSpec((tm, d), lambda i: (i, 0)),
        out_shape=jax.ShapeDtypeStruct((t, d), BF16),
        compiler_params=_params(("parallel",)),
    )(x, g)


def _rms_bwd(x, g, dhs, dres, name, tm=512):
    t, d = x.shape
    n_dh = len(dhs)

    def body(*refs):
        x_ref, g_ref = refs[0], refs[1]
        dh_refs = refs[2:2 + n_dh]
        dres_ref, dx_ref, dg_ref = refs[2 + n_dh:]
        i = pl.program_id(0)

        @pl.when(i == 0)
        def _():
            dg_ref[...] = jnp.zeros_like(dg_ref)

        xv = x_ref[...]
        r = lax.rsqrt(jnp.mean(xv * xv, axis=-1, keepdims=True) + EPS)
        xh = xv * r
        dh = dh_refs[0][...].astype(F32)
        for ref in dh_refs[1:]:
            dh = dh + ref[...].astype(F32)
        dg_ref[0:1, :] += jnp.sum(dh * xh, axis=0, keepdims=True)
        dxh = dh * g_ref[...]
        dx = r * (dxh - xh * jnp.mean(dxh * xh, axis=-1, keepdims=True))
        dx_ref[...] = dx + dres_ref[...]

    row = pl.BlockSpec((tm, d), lambda i: (i, 0))
    dx, dg = pl.pallas_call(
        body, name=name, grid=(t // tm,),
        in_specs=[row, pl.BlockSpec((1, d), lambda i: (0, 0))] + [row] * n_dh + [row],
        out_specs=[row, pl.BlockSpec((8, d), lambda i: (0, 0))],
        out_shape=[jax.ShapeDtypeStruct((t, d), F32), jax.ShapeDtypeStruct((8, d), F32)],
        compiler_params=_params(("arbitrary",)),
    )(x, g, *dhs, dres)
    return dx, dg[0:1]


def _in_proj_bwd(dp, w, dh_more, x, g, dres, name, tm=512, tk=2048):
    t, k = dp.shape
    d = w.shape[0]
    nk = k // tk

    def body(dp_ref, w_ref, more_ref, x_ref, g_ref, dres_ref, dx_ref, dg_ref, acc):
        i, kk = pl.program_id(0), pl.program_id(1)

        @pl.when((i == 0) & (kk == 0))
        def _():
            dg_ref[...] = jnp.zeros_like(dg_ref)

        prod = lax.dot_general(dp_ref[...], w_ref[...], (((1,), (1,)), ((), ())), preferred_element_type=F32)

        @pl.when(kk == 0)
        def _():
            acc[...] = prod

        @pl.when((kk > 0) & (kk < nk - 1))
        def _():
            acc[...] += prod

        @pl.when(kk == nk - 1)
        def _():
            dh = acc[...] + prod + more_ref[...]
            xv = x_ref[...]
            r = lax.rsqrt(jnp.mean(xv * xv, axis=-1, keepdims=True) + EPS)
            xh = xv * r
            dg_ref[0:1, :] += jnp.sum(dh * xh, axis=0, keepdims=True)
            dxh = dh * g_ref[...]
            dx_ref[...] = r * (dxh - xh * jnp.mean(dxh * xh, axis=-1, keepdims=True)) + dres_ref[...]

    row = pl.BlockSpec((tm, d), lambda i, kk: (i, 0))
    dx, dg = pl.pallas_call(
        body, name=name, grid=(t // tm, nk),
        in_specs=[pl.BlockSpec((tm, tk), lambda i, kk: (i, kk)), pl.BlockSpec((d, tk), lambda i, kk: (0, kk)),
                  row, row, pl.BlockSpec((1, d), lambda i, kk: (0, 0)), row],
        out_specs=[row, pl.BlockSpec((8, d), lambda i, kk: (0, 0))],
        out_shape=[jax.ShapeDtypeStruct((t, d), F32), jax.ShapeDtypeStruct((8, d), F32)],
        scratch_shapes=[pltpu.VMEM((tm, d), F32)],
        compiler_params=_params(("arbitrary", "arbitrary"), 48 * 1024 * 1024),
    )(dp, w, dh_more, x, g, dres)
    return dx, dg[0:1]


def _out_proj(a, w, res, name, norm_g=None, target=None, tm=512):
    t, k = a.shape
    d = w.shape[1]
    nb = t // tm

    def body(a_ref, w_ref, r_ref, x_ref, o1_ref, o2_ref):
        y = jnp.dot(a_ref[...], w_ref[...], preferred_element_type=F32) + r_ref[...]
        if norm_g is not None:
            o1_ref[...] = y
            r = lax.rsqrt(jnp.mean(y * y, axis=-1, keepdims=True) + EPS)
            o2_ref[...] = (y * r * x_ref[...]).astype(BF16)
        else:
            e = y - x_ref[...]
            o1_ref[...] = e * (1.0 / d)
            s = jnp.sum(jnp.sum(e * e, axis=1, keepdims=True), axis=0, keepdims=True) * (0.5 / d)
            o2_ref[...] = jnp.broadcast_to(s, (8, 128))

    row = pl.BlockSpec((tm, d), lambda i: (i, 0))
    if norm_g is not None:
        extra, extra_spec = norm_g, pl.BlockSpec((1, d), lambda i: (0, 0))
        out2_spec, out2_shape = row, jax.ShapeDtypeStruct((t, d), BF16)
    else:
        extra, extra_spec = target, row
        out2_spec = pl.BlockSpec((None, 8, 128), lambda i: (i, 0, 0))
        out2_shape = jax.ShapeDtypeStruct((nb, 8, 128), F32)
    o1, o2 = pl.pallas_call(
        body, name=name, grid=(nb,),
        in_specs=[pl.BlockSpec((tm, k), lambda i: (i, 0)), pl.BlockSpec((k, d), lambda i: (0, 0)), row, extra_spec],
        out_specs=[row, out2_spec], out_shape=[jax.ShapeDtypeStruct((t, d), F32), out2_shape],
        compiler_params=_params(("parallel",), 48 * 1024 * 1024),
    )(a, w, res, extra)
    return (o1, o2) if norm_g is not None else (o1, o2[:, 0, 0])


def _softplus(x):
    t = jnp.exp(-jnp.abs(x))
    return jnp.maximum(x, 0.0) + jnp.where(t < 1e-3, t * (1.0 - 0.5 * t), jnp.log(1.0 + t))


def _tri(rows_le_cols):
    r = lax.broadcasted_iota(jnp.int32, (A_CHUNK, A_CHUNK), 0)
    c = lax.broadcasted_iota(jnp.int32, (A_CHUNK, A_CHUNK), 1)
    return jnp.where((r <= c) if rows_le_cols else (r >= c), 1.0, 0.0).astype(BF16)


def _dot_exact_rhs(a, ones_bf16):
    dn = (((1,), (0,)), ((), ()))
    hi, mid, lo = _split3(a)
    out = lax.dot_general(hi, ones_bf16, dn, preferred_element_type=F32)
    out = out + lax.dot_general(mid, ones_bf16, dn, preferred_element_type=F32)
    return out + lax.dot_general(lo, ones_bf16, dn, preferred_element_type=F32)


def _gdn_prep(tail_t, a_log, dt_bias):
    bn, _, n, c = tail_t.shape

    def body(t_ref, alog_ref, dtb_ref, beta_ref, gc_ref):
        upper = _tri(True)
        for h in range(A_HEADS):
            beta_ref[h] = _sigmoid(t_ref[h])
            ea = jnp.exp(jnp.full((n, c), alog_ref[h], F32))
            g = -ea * _softplus(t_ref[A_HEADS + h] + dtb_ref[h])
            gc_ref[h] = _dot_exact_rhs(g, upper)

    smem = pl.BlockSpec(memory_space=pltpu.SMEM)
    blk = pl.BlockSpec((None, A_HEADS, n, c), lambda b: (b, 0, 0, 0))
    return pl.pallas_call(
        body, name="gdn_prep", grid=(bn,),
        in_specs=[pl.BlockSpec((None, 2 * A_HEADS, n, c), lambda b: (b, 0, 0, 0)), smem, smem],
        out_specs=[blk, blk],
        out_shape=[jax.ShapeDtypeStruct((bn, A_HEADS, n, c), F32)] * 2,
        compiler_params=_params(("parallel",)),
    )(tail_t, a_log, dt_bias)


def _gdn_prep_bwd(tail_t, a_log, dt_bias, d_gc, d_beta):
    bn, _, n, c = tail_t.shape

    def body(t_ref, alog_ref, dtb_ref, dgc_ref, dbeta_ref, dt_ref, dal_ref, ddt_ref):
        lower = _tri(False)
        for h in range(A_HEADS):
            beta = _sigmoid(t_ref[h])
            dt_ref[h] = dbeta_ref[h] * beta * (1.0 - beta)
            dg = _dot_exact_rhs(dgc_ref[h], lower)
            ea = jnp.exp(jnp.full((n, c), alog_ref[h], F32))
            xa = t_ref[A_HEADS + h] + dtb_ref[h]
            g = -ea * _softplus(xa)
            dxa = -ea * dg * _sigmoid(xa)
            dt_ref[A_HEADS + h] = dxa
            s1 = jnp.sum(jnp.sum(g * dg, axis=1, keepdims=True), axis=0, keepdims=True)
            s2 = jnp.sum(jnp.sum(dxa, axis=1, keepdims=True), axis=0, keepdims=True)
            dal_ref[h:h + 1, :] = jnp.broadcast_to(s1, (1, 128))
            ddt_ref[h:h + 1, :] = jnp.broadcast_to(s2, (1, 128))

    smem = pl.BlockSpec(memory_space=pltpu.SMEM)
    blk8 = pl.BlockSpec((None, A_HEADS, n, c), lambda b: (b, 0, 0, 0))
    blk16 = pl.BlockSpec((None, 2 * A_HEADS, n, c), lambda b: (b, 0, 0, 0))
    sm = pl.BlockSpec((None, A_HEADS, 128), lambda b: (b, 0, 0))
    return pl.pallas_call(
        body, name="gdn_prep_bwd", grid=(bn,),
        in_specs=[blk16, smem, smem, blk8, blk8],
        out_specs=[blk16, sm, sm],
        out_shape=[jax.ShapeDtypeStruct((bn, 2 * A_HEADS, n, c), F32),
                   jax.ShapeDtypeStruct((bn, A_HEADS, 128), F32),
                   jax.ShapeDtypeStruct((bn, A_HEADS, 128), F32)],
        compiler_params=_params(("parallel",)),
    )(tail_t, a_log, dt_bias, d_gc, d_beta)


HALO = 8


def _conv_taps(xw, w):
    y = w[A_CONV - 1:A_CONV, :] * xw
    for j in range(1, A_CONV):
        y = y + w[A_CONV - 1 - j:A_CONV - j, :] * pltpu.roll(xw, j, 0)
    return y[HALO:, :]


def _row_to_col(row, eye):
    c = eye.shape[0]
    return jnp.sum(jnp.where(eye, jnp.broadcast_to(row, (c, c)), 0.0), axis=1, keepdims=True)


def _col_to_row(col, eye):
    c = eye.shape[0]
    return jnp.sum(jnp.where(eye, jnp.broadcast_to(col, (c, c)), 0.0), axis=0, keepdims=True)


def _unit_lower_inverse(a, ri, ci):
    eye = jnp.where(ri == ci, 1.0, 0.0)
    a8 = jnp.where((ri >> 3) == (ci >> 3), a, 0.0)
    a2 = _dot(a8, a8, 1, 0)
    yield
    a4 = _dot(a2, a2, 1, 0)
    t = eye - a8
    t = t + _dot(t, a2, 1, 0)
    yield
    t = t + _dot(t, a4, 1, 0)
    yield
    for sh in (3, 4, 5):
        off = jnp.where(((ri >> (sh + 1)) == (ci >> (sh + 1))) & ((ri >> sh) != (ci >> sh)), a, 0.0)
        left = _dot(t, off, 1, 0)
        yield
        t = t - _dot(left, t, 1, 0)
        yield
    return t


def _round_robin(gens):
    live = list(gens)
    while live:
        nxt = []
        for g in live:
            try:
                next(g)
                nxt.append(g)
            except StopIteration:
                pass
        live = nxt


def _gdn_chunk_core(q, k, v, g_row, b_row, t_mat, ri, ci):
    eye = ri == ci
    g_col = _row_to_col(g_row, eye)
    b_col = _row_to_col(b_row, eye)
    causal = ri >= ci
    strict = ri > ci
    dec = jnp.where(causal, jnp.exp(jnp.where(causal, g_col - g_row, 0.0)), 0.0)
    gam = jnp.exp(g_col)
    g_last = g_row[:, A_CHUNK - 1:A_CHUNK]
    gam_last = jnp.exp(g_last)
    e = jnp.exp(g_last - g_col)
    kb = k * b_col
    bv = v * b_col
    kbg = kb * gam
    q16, k16, kb16 = q.astype(BF16), k.astype(BF16), kb.astype(BF16)
    kk = _dot(kb16, k16, 1, 1)
    p = _dot(q16, k16, 1, 1) * dec
    yield
    a_mat = jnp.where(strict, kk * dec, 0.0)
    if t_mat is None:
        t_mat = yield from _unit_lower_inverse(a_mat, ri, ci)
    t16 = t_mat.astype(BF16)
    u = _dot(t16, bv, 1, 0)
    w = _dot(t16, kbg, 1, 0)
    yield
    return dict(eye=eye, g_col=g_col, b_col=b_col, dec=dec, strict=strict, causal=causal, gam=gam,
                gam_last=gam_last, e=e, kb=kb, bv=bv, kbg=kbg, a_mat=a_mat, t_mat=t_mat, u=u, w=w, p=p,
                qg=q * gam, kd=k * e, q16=q16, k16=k16, kb16=kb16, t16=t16)


A_SEQ_BLK = 256
A_BLK_CHUNKS = A_SEQ_BLK // A_CHUNK


def _gdn_halo(proj_hm):
    bn, s, w = proj_hm.shape
    last = proj_hm.reshape(bn, s // A_SEQ_BLK, A_SEQ_BLK, w)[:, :, A_SEQ_BLK - HALO:, :]
    return jnp.concatenate([jnp.zeros((bn, 1, HALO, w), proj_hm.dtype), last[:, :-1]], axis=1)


def _gdn_window(x_ref, halo_ref, ci, first, lo):
    if first:
        return jnp.concatenate([halo_ref[:, lo:lo + A_CONV_COLS], x_ref[0:A_CHUNK, lo:lo + A_CONV_COLS]], axis=0)
    start = pl.multiple_of(ci * A_CHUNK - HALO, HALO)
    return x_ref[pl.ds(start, A_CHUNK + HALO), lo:lo + A_CONV_COLS]


def _gdn_chunk_prep(xw, cw, y=None):
    if y is None:
        y = _conv_taps(xw, cw)
    a, slope = _silu_and_slope(y)
    aq, ak, v = a[:, 0:A_DK], a[:, A_DK:2 * A_DK], a[:, 2 * A_DK:]
    rq = lax.rsqrt(jnp.sum(aq * aq, axis=1, keepdims=True) + EPS)
    rk = lax.rsqrt(jnp.sum(ak * ak, axis=1, keepdims=True) + EPS)
    return dict(xw=xw, y=y, slope=slope, aq=aq, ak=ak, rq=rq, rk=rk, q=aq * rq * (A_DK ** -0.5), k=ak * rk, v=v)


def _gdn_fwd(proj_hm, cw_hm, beta, gc, norm_g, hp=8):
    bn, s, _ = proj_hm.shape
    n = s // A_CHUNK
    nsb = s // A_SEQ_BLK
    halo = _gdn_halo(proj_hm)

    def body(x_ref, halo_ref, cw_ref, beta_ref, gc_ref, ng_ref, og_ref, oraw_ref, st_ref, t_ref, y_ref, state):
        first_chunk = pl.program_id(2) * A_BLK_CHUNKS
        ri = lax.broadcasted_iota(jnp.int32, (A_CHUNK, A_CHUNK), 0)
        ci_ = lax.broadcasted_iota(jnp.int32, (A_CHUNK, A_CHUNK), 1)
        ng = ng_ref[...]

        @pl.when(pl.program_id(2) == 0)
        def _():
            state[...] = jnp.zeros_like(state)

        def one_head(hh, ci, first, rows):
            lo = hh * A_HEAD_COLS
            cw = cw_ref[:, hh * A_CONV_COLS:(hh + 1) * A_CONV_COLS]
            cin = _gdn_chunk_prep(_gdn_window(x_ref, halo_ref, ci, first, lo), cw)
            y_ref[rows, hh * A_CONV_COLS:(hh + 1) * A_CONV_COLS] = cin["y"]
            seq_chunk = pl.ds(first_chunk + ci, 1)
            core = yield from _gdn_chunk_core(cin["q"], cin["k"], cin["v"], gc_ref[hh, seq_chunk, :],
                                              beta_ref[hh, seq_chunk, :], None, ri, ci_)
            st = state[hh]
            st_ref[hh, ci] = st
            t_ref[hh, ci] = core["t_mat"]
            st16 = st.astype(BF16)
            vn = core["u"] - _dot(core["w"], st16, 1, 0)
            qs = _dot(core["qg"], st16, 1, 0)
            yield
            vn16 = vn.astype(BF16)
            o = qs + _dot(core["p"], vn16, 1, 0)
            state[hh] = st * core["gam_last"] + _dot(core["kd"], vn16, 0, 0)
            yield
            ocols = slice(hh * A_DV, (hh + 1) * A_DV)
            oraw_ref[rows, ocols] = o
            r = lax.rsqrt(jnp.mean(o * o, axis=1, keepdims=True) + EPS)
            z = x_ref[rows, lo + A_CONV_COLS:lo + A_HEAD_COLS]
            og_ref[rows, ocols] = (o * r * ng * _silu(z)).astype(BF16)

        def chunk(ci, first):
            rows = pl.ds(0 if first else pl.multiple_of(ci * A_CHUNK, A_CHUNK), A_CHUNK)
            _round_robin([one_head(hh, ci, first, rows) for hh in range(hp)])

        chunk(0, True)
        lax.fori_loop(1, A_BLK_CHUNKS, lambda i, c: (chunk(i, False), c)[1], 0)

    small = pl.BlockSpec((None, hp, n, A_CHUNK), lambda b, h, j: (b, h, 0, 0))
    return pl.pallas_call(
        body, name="gdn_fwd", grid=(bn, A_HEADS // hp, nsb),
        in_specs=[pl.BlockSpec((None, A_SEQ_BLK, hp * A_HEAD_COLS), lambda b, h, j: (b, j, h)),
                  pl.BlockSpec((None, None, HALO, hp * A_HEAD_COLS), lambda b, h, j: (b, j, 0, h)),
                  pl.BlockSpec((A_CONV, hp * A_CONV_COLS), lambda b, h, j: (0, h)),
                  small, small,
                  pl.BlockSpec((1, A_DV), lambda b, h, j: (0, 0))],
        out_specs=[pl.BlockSpec((None, A_SEQ_BLK, hp * A_DV), lambda b, h, j: (b, j, h)),
                   pl.BlockSpec((None, A_SEQ_BLK, hp * A_DV), lambda b, h, j: (b, j, h)),
                   pl.BlockSpec((None, hp, A_BLK_CHUNKS, A_DK, A_DV), lambda b, h, j: (b, h, j, 0, 0)),
                   pl.BlockSpec((None, hp, A_BLK_CHUNKS, A_CHUNK, A_CHUNK), lambda b, h, j: (b, h, j, 0, 0)),
                   pl.BlockSpec((None, A_SEQ_BLK, hp * A_CONV_COLS), lambda b, h, j: (b, j, h))],
        out_shape=[jax.ShapeDtypeStruct((bn, s, A_VW), BF16),
                   jax.ShapeDtypeStruct((bn, s, A_VW), F32),
                   jax.ShapeDtypeStruct((bn, A_HEADS, n, A_DK, A_DV), F32),
                   jax.ShapeDtypeStruct((bn, A_HEADS, n, A_CHUNK, A_CHUNK), F32),
                   jax.ShapeDtypeStruct((bn, s, A_HEADS * A_CONV_COLS), F32)],
        scratch_shapes=[pltpu.VMEM((hp, A_DK, A_DV), F32)],
        compiler_params=_params(("parallel", "parallel", "arbitrary"), VMEM_BIG),
    )(proj_hm, halo, cw_hm, beta, gc, norm_g)


def _gdn_bwd(proj_hm, cw_hm, beta, gc, norm_g, oraw, states, t_mats, conv_y, dog, hp=4):
    bn, s, _ = proj_hm.shape
    n = s // A_CHUNK
    nsb = s // A_SEQ_BLK
    halo = _gdn_halo(proj_hm)

    def body(x_ref, halo_ref, cw_ref, beta_ref, gc_ref, ng_ref, oraw_ref, st_ref, t_ref, y_ref, dog_ref,
             dx_ref, dgc_ref, dbeta_ref, dcw_ref, dng_ref, dstate, dy_next, shifted):
        first_chunk = (nsb - 1 - pl.program_id(2)) * A_BLK_CHUNKS
        ri = lax.broadcasted_iota(jnp.int32, (A_CHUNK, A_CHUNK), 0)
        ci_ = lax.broadcasted_iota(jnp.int32, (A_CHUNK, A_CHUNK), 1)
        lane = lax.broadcasted_iota(jnp.int32, (1, A_CHUNK), 1)
        ng = ng_ref[...]

        @pl.when(pl.program_id(2) == 0)
        def _():
            dstate[...] = jnp.zeros_like(dstate)
            dy_next[...] = jnp.zeros_like(dy_next)
            dcw_ref[...] = jnp.zeros_like(dcw_ref)
            dng_ref[...] = jnp.zeros_like(dng_ref)

        def one_head(hh, ci, first, rows):
            lo = hh * A_HEAD_COLS
            ccols = slice(hh * A_CONV_COLS, (hh + 1) * A_CONV_COLS)
            ocols = slice(hh * A_DV, (hh + 1) * A_DV)
            cw = cw_ref[:, ccols]
            cin = _gdn_chunk_prep(_gdn_window(x_ref, halo_ref, ci, first, lo), cw, y_ref[rows, ccols])
            q, k, v = cin["q"], cin["k"], cin["v"]
            seq_chunk = pl.ds(first_chunk + ci, 1)
            cr = yield from _gdn_chunk_core(q, k, v, gc_ref[hh, seq_chunk, :], beta_ref[hh, seq_chunk, :],
                                            t_ref[hh, ci], ri, ci_)
            eye, dec, gam, e = cr["eye"], cr["dec"], cr["gam"], cr["e"]
            b_col, t_mat, u, w, p = cr["b_col"], cr["t_mat"], cr["u"], cr["w"], cr["p"]
            st = st_ref[hh, ci]
            ds_out = dstate[hh]

            o = oraw_ref[rows, ocols]
            z = x_ref[rows, lo + A_CONV_COLS:lo + A_HEAD_COLS]
            d_og = dog_ref[rows, ocols].astype(F32)
            r = lax.rsqrt(jnp.mean(o * o, axis=1, keepdims=True) + EPS)
            oh = o * r
            gate, gate_slope = _silu_and_slope(z)
            d_on = d_og * gate
            dz = d_og * oh * ng * gate_slope
            dng_ref[hh, 0:1, :] += jnp.sum(d_on * oh, axis=0, keepdims=True)
            d_oh = d_on * ng
            d_o = r * (d_oh - oh * jnp.mean(d_oh * oh, axis=1, keepdims=True))

            st16, ds16, do16, w16 = st.astype(BF16), ds_out.astype(BF16), d_o.astype(BF16), w.astype(BF16)
            q16, k16, t16 = cr["q16"], cr["k16"], cr["t16"]
            vn = u - _dot(w16, st16, 1, 0)
            d_vn = _dot(p, do16, 0, 0) + _dot(cr["kd"], ds16, 1, 0)
            d_qg = _dot(do16, st16, 1, 1)
            qgdo = _dot(cr["qg"], do16, 0, 0)
            yield
            vn16, dvn16 = vn.astype(BF16), d_vn.astype(BF16)
            d_p = jnp.where(cr["causal"], _dot(do16, vn16, 1, 1), 0.0)
            d_kd = _dot(vn16, ds16, 1, 1)
            d_gam_last = jnp.sum(jnp.sum(st * ds_out, axis=1, keepdims=True), axis=0, keepdims=True)
            d_w = -_dot(dvn16, st16, 1, 1)
            dstate[hh] = qgdo + ds_out * cr["gam_last"] - _dot(w16, dvn16, 0, 0)
            d_bv = _dot(t16, dvn16, 0, 0)
            yield
            d_kbg = _dot(t16, d_w, 0, 0)
            n_p = (d_p * dec).astype(BF16)
            d_q = _dot(n_p, k16, 1, 0) + d_qg * gam
            npq = _dot(n_p, q16, 0, 0)
            yield
            d_a = jnp.where(cr["strict"], -(_dot(d_bv, u, 1, 1) + _dot(d_kbg, w16, 1, 1)), 0.0)
            yield
            m_a = (d_a * dec).astype(BF16)
            d_kb = _dot(m_a, k16, 1, 0) + d_kbg * gam
            d_k = (_dot(m_a, cr["kb16"], 0, 0) + npq + d_kd * e + d_kb * b_col)
            yield
            d_v = d_bv * b_col
            d_beta_col = (jnp.sum(d_bv * v, axis=1, keepdims=True)
                          + jnp.sum(d_kb * k, axis=1, keepdims=True))
            gterm = d_a * cr["a_mat"] + d_p * p
            d_e = jnp.sum(d_kd * k, axis=1, keepdims=True) * e
            d_g_col = (jnp.sum(gterm, axis=1, keepdims=True)
                       + (jnp.sum(d_qg * q, axis=1, keepdims=True)
                          + jnp.sum(d_kbg * cr["kb"], axis=1, keepdims=True)) * gam
                       - d_e)
            d_g_last = jnp.sum(d_e, axis=0, keepdims=True) + d_gam_last * cr["gam_last"]
            d_g_row = (_col_to_row(d_g_col, eye) - jnp.sum(gterm, axis=0, keepdims=True)
                       + jnp.where(lane == A_CHUNK - 1, d_g_last, 0.0))
            dgc_ref[hh, seq_chunk, :] = d_g_row
            dbeta_ref[hh, seq_chunk, :] = _col_to_row(d_beta_col, eye)

            qh = cin["aq"] * cin["rq"]
            kh = cin["ak"] * cin["rk"]
            d_qh = d_q * (A_DK ** -0.5)
            d_aq = cin["rq"] * (d_qh - qh * jnp.sum(d_qh * qh, axis=1, keepdims=True))
            d_ak = cin["rk"] * (d_k - kh * jnp.sum(d_k * kh, axis=1, keepdims=True))
            d_y = jnp.concatenate([d_aq, d_ak, d_v], axis=1) * cin["slope"]
            shifted[hh, 0, 0:A_CHUNK, :] = d_y
            shifted[hh, 0, A_CHUNK:A_CHUNK + HALO, :] = dy_next[hh]
            shifted[hh, 1, 0:A_CHUNK + HALO, :] = cin["xw"]
            d_x = cw[A_CONV - 1:A_CONV, :] * d_y
            for j in range(1, A_CONV):
                d_x = d_x + cw[A_CONV - 1 - j:A_CONV - j, :] * shifted[hh, 0, j:j + A_CHUNK, :]
            for j in range(A_CONV):
                xs = shifted[hh, 1, HALO - j:HALO - j + A_CHUNK, :]
                dcw_ref[A_CONV - 1 - j:A_CONV - j, ccols] += jnp.sum(d_y * xs, axis=0, keepdims=True)
            dy_next[hh] = d_y[0:HALO, :]
            dx_ref[rows, lo:lo + A_CONV_COLS] = d_x.astype(BF16)
            dx_ref[rows, lo + A_CONV_COLS:lo + A_HEAD_COLS] = dz.astype(BF16)

        def chunk(ci, first):
            rows = pl.ds(0 if first else pl.multiple_of(ci * A_CHUNK, A_CHUNK), A_CHUNK)
            _round_robin([one_head(hh, ci, first, rows) for hh in range(hp)])

        lax.fori_loop(0, A_BLK_CHUNKS - 1, lambda i, c: (chunk(A_BLK_CHUNKS - 1 - i, False), c)[1], 0)
        chunk(0, True)

    rev = lambda j: nsb - 1 - j
    small = pl.BlockSpec((None, hp, n, A_CHUNK), lambda b, h, j: (b, h, 0, 0))
    wide = pl.BlockSpec((None, A_SEQ_BLK, hp * A_HEAD_COLS), lambda b, h, j: (b, rev(j), h))
    val = pl.BlockSpec((None, A_SEQ_BLK, hp * A_DV), lambda b, h, j: (b, rev(j), h))
    return pl.pallas_call(
        body, name="gdn_bwd", grid=(bn, A_HEADS // hp, nsb),
        in_specs=[wide,
                  pl.BlockSpec((None, None, HALO, hp * A_HEAD_COLS), lambda b, h, j: (b, rev(j), 0, h)),
                  pl.BlockSpec((A_CONV, hp * A_CONV_COLS), lambda b, h, j: (0, h)),
                  small, small,
                  pl.BlockSpec((1, A_DV), lambda b, h, j: (0, 0)),
                  val,
                  pl.BlockSpec((None, hp, A_BLK_CHUNKS, A_DK, A_DV), lambda b, h, j: (b, h, rev(j), 0, 0)),
                  pl.BlockSpec((None, hp, A_BLK_CHUNKS, A_CHUNK, A_CHUNK), lambda b, h, j: (b, h, rev(j), 0, 0)),
                  pl.BlockSpec((None, A_SEQ_BLK, hp * A_CONV_COLS), lambda b, h, j: (b, rev(j), h)),
                  val],
        out_specs=[wide, small, small,
                   pl.BlockSpec((None, A_CONV, hp * A_CONV_COLS), lambda b, h, j: (b, 0, h)),
                   pl.BlockSpec((None, hp, 8, A_DV), lambda b, h, j: (b, h, 0, 0))],
        out_shape=[jax.ShapeDtypeStruct((bn, s, A_HEADS * A_HEAD_COLS), BF16),
                   jax.ShapeDtypeStruct((bn, A_HEADS, n, A_CHUNK), F32),
                   jax.ShapeDtypeStruct((bn, A_HEADS, n, A_CHUNK), F32),
                   jax.ShapeDtypeStruct((bn, A_CONV, A_HEADS * A_CONV_COLS), F32),
                   jax.ShapeDtypeStruct((bn, A_HEADS, 8, A_DV), F32)],
        scratch_shapes=[pltpu.VMEM((hp, A_DK, A_DV), F32), pltpu.VMEM((hp, HALO, A_CONV_COLS), F32),
                        pltpu.VMEM((hp, 2, A_CHUNK + 2 * HALO, A_CONV_COLS), F32)],
        compiler_params=_params(("parallel", "parallel", "arbitrary"), VMEM_BIG),
    )(proj_hm, halo, cw_hm, beta, gc, norm_g, oraw, states, t_mats, conv_y, dog)


def _rope_tables(posf, inv_freq_row):
    t = posf.shape[0]
    tm = 512

    def body(p_ref, f_ref, c_ref, sa_ref, sb_ref):
        ang = p_ref[...] * f_ref[...]
        lane = lax.broadcasted_iota(jnp.int32, ang.shape, 1)
        half = ROPE_DIMS // 2
        c_ref[...] = jnp.where(lane < ROPE_DIMS, jnp.cos(ang), 1.0)
        sn = jnp.sin(ang)
        sa_ref[...] = jnp.where(lane < half, -sn, 0.0)
        sb_ref[...] = jnp.where((lane >= half) & (lane < ROPE_DIMS), sn, 0.0)

    row = pl.BlockSpec((tm, 128), lambda i: (i, 0))
    return pl.pallas_call(
        body, name="rope_tables", grid=(t // tm,),
        in_specs=[row, pl.BlockSpec((1, 128), lambda i: (0, 0))], out_specs=[row] * 3,
        out_shape=[jax.ShapeDtypeStruct((t, 128), F32)] * 3,
        compiler_params=_params(("parallel",)),
    )(posf, inv_freq_row)


def _qk_prep(proj, c, sa, sb, qg, kg, name, tm=512):
    t = proj.shape[0]

    def body(x_ref, c_ref, sa_ref, sb_ref, qg_ref, kg_ref, o_ref):
        cc, s1, s2 = c_ref[...], sa_ref[...], sb_ref[...]
        half = ROPE_DIMS // 2

        def one_head(lo, g):
            xv = x_ref[:, lo:lo + B_DH].astype(F32)
            ms = jnp.mean(xv * xv, axis=1, keepdims=True)
            yield
            xn = xv * lax.rsqrt(ms + EPS) * g
            r1, r2 = pltpu.roll(xn, 128 - half, 1), pltpu.roll(xn, half, 1)
            yield
            o_ref[:, lo:lo + B_DH] = (xn * cc + r1 * s1 + r2 * s2).astype(BF16)

        for which, g_ref in ((0, qg_ref), (1, kg_ref)):
            g = g_ref[...]
            _round_robin([one_head(which * B_W + h * B_DH, g) for h in range(B_HEADS)])
        o_ref[:, 2 * B_W:3 * B_W] = x_ref[:, 2 * B_W:3 * B_W]

    tab = pl.BlockSpec((tm, 128), lambda i: (i, 0))
    gain = pl.BlockSpec((1, B_DH), lambda i: (0, 0))
    return pl.pallas_call(
        body, name=name, grid=(t // tm,),
        in_specs=[pl.BlockSpec((tm, 3 * B_W), lambda i: (i, 0)), tab, tab, tab, gain, gain],
        out_specs=pl.BlockSpec((tm, 3 * B_W), lambda i: (i, 0)),
        out_shape=jax.ShapeDtypeStruct((t, 3 * B_W), BF16),
        compiler_params=_params(("parallel",), 40 * 1024 * 1024),
    )(proj, c, sa, sb, qg, kg)


def _qk_prep_bwd(proj, c, sa, sb, qg, kg, dq, dk, dv, dz, name, tm=512):
    t = proj.shape[0]
    out_w = 3 * B_W + (B_W if dz is not None else 0)

    def body(*refs):
        x_ref, c_ref, sa_ref, sb_ref, qg_ref, kg_ref, dq_ref, dk_ref, dv_ref = refs[:9]
        if dz is not None:
            dz_ref, o_ref, dgain_ref = refs[9:]
        else:
            o_ref, dgain_ref = refs[9:]
        i = pl.program_id(0)

        @pl.when(i == 0)
        def _():
            dgain_ref[...] = jnp.zeros_like(dgain_ref)

        cc, s1, s2 = c_ref[...], sa_ref[...], sb_ref[...]
        half = ROPE_DIMS // 2

        def one_head(which, h, g, d_ref, parts):
            lo = which * B_W + h * B_DH
            xv = x_ref[:, lo:lo + B_DH].astype(F32)
            d_out = d_ref[:, h * B_DH:(h + 1) * B_DH].astype(F32)
            ms = jnp.mean(xv * xv, axis=1, keepdims=True)
            r1, r2 = pltpu.roll(d_out * s1, half, 1), pltpu.roll(d_out * s2, 128 - half, 1)
            yield
            r = lax.rsqrt(ms + EPS)
            xh = xv * r
            d_xn = d_out * cc + r1 + r2
            parts.append(jnp.sum(d_xn * xh, axis=0, keepdims=True))
            d_xh = d_xn * g
            dot = jnp.mean(d_xh * xh, axis=1, keepdims=True)
            yield
            o_ref[:, lo:lo + B_DH] = (r * (d_xh - xh * dot)).astype(BF16)

        for which, g_ref, d_ref in ((0, qg_ref, dq_ref), (1, kg_ref, dk_ref)):
            parts = []
            _round_robin([one_head(which, h, g_ref[...], d_ref, parts) for h in range(B_HEADS)])
            acc = parts[0]
            for part in parts[1:]:
                acc = acc + part
            dgain_ref[which:which + 1, :] += acc
        o_ref[:, 2 * B_W:3 * B_W] = dv_ref[...]
        if dz is not None:
            o_ref[:, 3 * B_W:4 * B_W] = dz_ref[...]

    tab = pl.BlockSpec((tm, 128), lambda i: (i, 0))
    gain = pl.BlockSpec((1, B_DH), lambda i: (0, 0))
    grad = pl.BlockSpec((tm, B_W), lambda i: (i, 0))
    in_specs = [pl.BlockSpec((tm, 2 * B_W), lambda i: (i, 0)), tab, tab, tab, gain, gain, grad, grad, grad]
    args = [proj, c, sa, sb, qg, kg, dq, dk, dv]
    if dz is not None:
        in_specs.append(grad)
        args.append(dz)
    return pl.pallas_call(
        body, name=name, grid=(t // tm,), in_specs=in_specs,
        out_specs=[pl.BlockSpec((tm, out_w), lambda i: (i, 0)), pl.BlockSpec((8, B_DH), lambda i: (0, 0))],
        out_shape=[jax.ShapeDtypeStruct((t, out_w), BF16), jax.ShapeDtypeStruct((8, B_DH), F32)],
        compiler_params=_params(("arbitrary",), 40 * 1024 * 1024),
    )(*args)


def _attn_masks():
    qi = lax.broadcasted_iota(jnp.int32, (B_BLK, 2 * B_BLK), 0)
    kj = lax.broadcasted_iota(jnp.int32, (B_BLK, 2 * B_BLK), 1)
    two = (kj >= qi) & (kj <= qi + B_BLK)
    q1 = lax.broadcasted_iota(jnp.int32, (B_BLK, B_BLK), 0)
    k1 = lax.broadcasted_iota(jnp.int32, (B_BLK, B_BLK), 1)
    return k1 <= q1, two


def _lane_pick(ref_rows, h):
    lane = lax.broadcasted_iota(jnp.int32, ref_rows.shape, 1)
    return jnp.sum(jnp.where(lane == h, ref_rows, 0.0), axis=1, keepdims=True)


B_ROWS = 2048


def _attn_schedule(nb, sb, block):
    way = 16

    def run(items):
        for at in range(0, len(items), way):
            _round_robin([block(*it) for it in items[at:at + way]])

    run([(si, 0, True) for si in range(sb)])
    if nb == 1:
        return
    per = max(1, way // sb)
    lead = 1 + (nb - 1) % per
    if lead > 1:
        run([(si, i, False) for i in range(1, lead) for si in range(sb)])

    def step(it, carry):
        run([(si, lead + it * per + u, False) for u in range(per) for si in range(sb)])
        return carry

    lax.fori_loop(0, (nb - lead) // per, step, 0)


def _attn_rows(i, first):
    if first:
        return pl.ds(0, B_BLK), pl.ds(0, B_BLK)
    rows = pl.ds(pl.multiple_of(i * B_BLK, B_BLK), B_BLK)
    return rows, pl.ds(pl.multiple_of((i - 1) * B_BLK, B_BLK), 2 * B_BLK)


def _attn_fwd(qkv, name):
    ns, ln, _ = qkv.shape
    nb = ln // B_BLK
    sb = B_ROWS // ln
    scale = B_DH ** -0.5

    def body(q_ref, k_ref, v_ref, o_ref, lse_ref):
        h = pl.program_id(1)
        mask1, mask2 = _attn_masks()
        lane = lax.broadcasted_iota(jnp.int32, (B_BLK, B_HEADS), 1)

        @pl.when(h == 0)
        def _():
            lse_ref[...] = jnp.zeros_like(lse_ref)

        def block(si, i, first):
            rows, win = _attn_rows(i, first)
            mask = mask1 if first else mask2
            sc = jnp.where(mask, _dot(q_ref[si, rows, :], k_ref[si, win, :], 1, 1) * scale, -1e30)
            yield
            m = jnp.max(sc, axis=1, keepdims=True)
            p = jnp.exp(sc - m)
            l = jnp.sum(p, axis=1, keepdims=True)
            pv = _dot(p, v_ref[si, win, :], 1, 0)
            yield
            o_ref[si, rows, :] = (pv / l).astype(BF16)
            lse_ref[si, rows, :] = jnp.where(lane == h, m + jnp.log(l), lse_ref[si, rows, :])

        _attn_schedule(nb, sb, block)

    head = lambda off: pl.BlockSpec((sb, ln, B_DH), lambda s, h: (s, 0, off + h))
    return pl.pallas_call(
        body, name=name, grid=(ns // sb, B_HEADS),
        in_specs=[head(0), head(B_HEADS), head(2 * B_HEADS)],
        out_specs=[head(0), pl.BlockSpec((sb, ln, B_HEADS), lambda s, h: (s, 0, 0))],
        out_shape=[jax.ShapeDtypeStruct((ns, ln, B_W), BF16), jax.ShapeDtypeStruct((ns, ln, B_HEADS), F32)],
        compiler_params=_params(("parallel", "arbitrary")),
    )(qkv, qkv, qkv)


def _attn_bwd(qkv, d_o, lse_joint, delta, name):
    ns, ln, _ = qkv.shape
    nb = ln // B_BLK
    sb = B_ROWS // ln
    scale = B_DH ** -0.5

    def body(q_ref, k_ref, v_ref, do_ref, lj_ref, dl_ref, dq_ref, dk_out, dv_out, dk_ref, dv_ref):
        h = pl.program_id(1)
        mask1, mask2 = _attn_masks()
        dk_ref[...] = jnp.zeros_like(dk_ref)
        dv_ref[...] = jnp.zeros_like(dv_ref)

        def block(si, i, first):
            rows, win = _attn_rows(i, first)
            mask = mask1 if first else mask2
            q = q_ref[si, rows, :]
            d_out = do_ref[si, rows, :]
            l_col = _lane_pick(lj_ref[si, rows, :], h)
            d_col = _lane_pick(dl_ref[si, rows, :], h)
            sc = _dot(q, k_ref[si, win, :], 1, 1) * scale
            d_p = _dot(d_out, v_ref[si, win, :], 1, 1)
            yield
            p = jnp.exp(jnp.where(mask, sc - l_col, -1e30))
            d_s = p * (d_p - d_col) * scale
            d_q = _dot(d_s, k_ref[si, win, :], 1, 0)
            d_k = _dot(d_s, q, 0, 0)
            d_v = _dot(p, d_out, 0, 0)
            yield
            dq_ref[si, rows, :] = d_q.astype(BF16)
            dk_ref[si, win, :] += d_k
            dv_ref[si, win, :] += d_v

        _attn_schedule(nb, sb, block)
        dk_out[...] = dk_ref[...].astype(BF16)
        dv_out[...] = dv_ref[...].astype(BF16)

    head = lambda off: pl.BlockSpec((sb, ln, B_DH), lambda s, h: (s, 0, off + h))
    small = pl.BlockSpec((sb, ln, B_HEADS), lambda s, h: (s, 0, 0))
    return pl.pallas_call(
        body, name=name, grid=(ns // sb, B_HEADS),
        in_specs=[head(0), head(B_HEADS), head(2 * B_HEADS), head(0), small, small],
        out_specs=[head(0)] * 3,
        out_shape=[jax.ShapeDtypeStruct((ns, ln, B_W), BF16)] * 3,
        scratch_shapes=[pltpu.VMEM((sb, ln, B_DH), F32)] * 2,
        compiler_params=_params(("parallel", "parallel")),
    )(qkv, qkv, qkv, d_o, lse_joint, delta)


def _merge_weights(lse_refs):
    ls = [r[...] for r in lse_refs]
    m = jnp.maximum(jnp.maximum(ls[0], ls[1]), ls[2])
    es = [jnp.exp(l - m) for l in ls]
    tot = es[0] + es[1] + es[2]
    return [e / tot for e in es], m + jnp.log(tot)


def _merge_fwd(outs, lses, proj0, tm=512):
    t = outs[0].shape[0]

    def body(o0, o1, o2, l0, l1, l2, z_ref, og_ref):
        wts, _ = _merge_weights((l0, l1, l2))

        def one_head(h):
            cols = slice(h * B_DH, (h + 1) * B_DH)
            w0, w1, w2 = (jnp.broadcast_to(w[:, h:h + 1], (tm, B_DH)) for w in wts)
            yield
            o = w0 * o0[:, cols] + w1 * o1[:, cols] + w2 * o2[:, cols]
            og_ref[:, cols] = (o * _silu(z_ref[:, cols].astype(F32))).astype(BF16)

        _round_robin([one_head(h) for h in range(B_HEADS)])

    wide = pl.BlockSpec((tm, B_W), lambda i: (i, 0))
    small = pl.BlockSpec((tm, B_HEADS), lambda i: (i, 0))
    return pl.pallas_call(
        body, name="merge_fwd", grid=(t // tm,),
        in_specs=[wide] * 3 + [small] * 3 + [pl.BlockSpec((tm, B_W), lambda i: (i, 3))],
        out_specs=wide, out_shape=jax.ShapeDtypeStruct((t, B_W), BF16),
        compiler_params=_params(("parallel",)),
    )(*outs, *lses, proj0)


def _merge_bwd(outs, lses, proj0, d_og, tm=512):
    t = outs[0].shape[0]

    def body(o0, o1, o2, l0, l1, l2, z_ref, dog_ref, do_ref, lj_ref, dl_ref, dz_ref):
        wts, lj = _merge_weights((l0, l1, l2))
        lj_ref[...] = lj
        lane = lax.broadcasted_iota(jnp.int32, (tm, B_HEADS), 1)
        sums = [None] * B_HEADS

        def one_head(h):
            cols = slice(h * B_DH, (h + 1) * B_DH)
            w0, w1, w2 = (jnp.broadcast_to(w[:, h:h + 1], (tm, B_DH)) for w in wts)
            yield
            o = w0 * o0[:, cols] + w1 * o1[:, cols] + w2 * o2[:, cols]
            z = z_ref[:, cols].astype(F32)
            d_g = dog_ref[:, cols].astype(F32)
            gate, gate_slope = _silu_and_slope(z)
            d_out = d_g * gate
            dz_ref[:, cols] = (d_g * o * gate_slope).astype(BF16)
            do_ref[:, cols] = d_out.astype(BF16)
            sums[h] = jnp.sum(d_out * o, axis=1, keepdims=True)
            yield

        _round_robin([one_head(h) for h in range(B_HEADS)])
        delta = jnp.zeros((tm, B_HEADS), F32)
        for h in range(B_HEADS):
            delta = jnp.where(lane == h, sums[h], delta)
        dl_ref[...] = delta

    wide = pl.BlockSpec((tm, B_W), lambda i: (i, 0))
    small = pl.BlockSpec((tm, B_HEADS), lambda i: (i, 0))
    return pl.pallas_call(
        body, name="merge_bwd", grid=(t // tm,),
        in_specs=[wide] * 3 + [small] * 3 + [pl.BlockSpec((tm, B_W), lambda i: (i, 3)), wide],
        out_specs=[wide, small, small, wide],
        out_shape=[jax.ShapeDtypeStruct((t, B_W), BF16), jax.ShapeDtypeStruct((t, B_HEADS), F32),
                   jax.ShapeDtypeStruct((t, B_HEADS), F32), jax.ShapeDtypeStruct((t, B_W), BF16)],
        compiler_params=_params(("parallel",)),
    )(*outs, *lses, proj0, d_og)


def _adamw(w, g, m, v, name):
    r, c = w.shape
    tr = r
    for cand in (256, 128, 64, 32, 16, 8):
        if r % cand == 0:
            tr = cand
            break

    def body(w_ref, g_ref, m_ref, v_ref, d_ref, nm_ref, nv_ref):
        gv = g_ref[...]
        nm = ADAM_B1 * m_ref[...] + (1.0 - ADAM_B1) * gv
        nv = ADAM_B2 * v_ref[...] + (1.0 - ADAM_B2) * (gv * gv)
        m_hat = nm / (1.0 - ADAM_B1 ** ADAM_STEP)
        v_hat = nv / (1.0 - ADAM_B2 ** ADAM_STEP)
        d_ref[...] = -ADAM_LR * (m_hat / (jnp.sqrt(v_hat) + ADAM_EPS) + ADAM_WD * w_ref[...])
        nm_ref[...] = nm
        nv_ref[...] = nv

    blk = pl.BlockSpec((tr, c), lambda i: (i, 0))
    return pl.pallas_call(
        body, name=name, grid=(r // tr,), in_specs=[blk] * 4, out_specs=[blk] * 3,
        out_shape=[jax.ShapeDtypeStruct((r, c), F32)] * 3,
        compiler_params=_params(("parallel",)),
    )(w, g, m, v)


def _adam_update(w, gv, m, v):
    nm = ADAM_B1 * m + (1.0 - ADAM_B1) * gv
    nv = ADAM_B2 * v + (1.0 - ADAM_B2) * (gv * gv)
    m_hat = nm / (1.0 - ADAM_B1 ** ADAM_STEP)
    v_hat = nv / (1.0 - ADAM_B2 ** ADAM_STEP)
    return -ADAM_LR * (m_hat / (jnp.sqrt(v_hat) + ADAM_EPS) + ADAM_WD * w), nm, nv


def _adamw_shard(w, mine, theirs, m, v, half_index, name, tr=128):
    _, r, c = w.shape
    nhb = (r // 2) // tr

    def body(c_ref, w_ref, mine_ref, theirs_ref, m_ref, v_ref, g_ref, d_ref, nm_ref, nv_ref):
        is_mine = (pl.program_id(0) // nhb) == c_ref[0]
        gv = jnp.where(is_mine, mine_ref[...], theirs_ref[...])
        d, nm, nv = _adam_update(w_ref[...], gv, m_ref[...], v_ref[...])
        g_ref[...] = gv
        d_ref[...] = d
        nm_ref[...] = nm
        nv_ref[...] = nv

    full = pl.BlockSpec((None, tr, c), lambda i, cc: (0, i, 0))
    half = pl.BlockSpec((tr, c), lambda i, cc: (i % nhb, 0))
    return pl.pallas_call(
        body, name=name,
        grid_spec=pltpu.PrefetchScalarGridSpec(
            num_scalar_prefetch=1, grid=(2 * nhb,),
            in_specs=[full, half, half, full, full], out_specs=[full] * 4),
        out_shape=[jax.ShapeDtypeStruct(w.shape, F32)] * 4,
        compiler_params=_params(("parallel",), 40 * 1024 * 1024),
    )(half_index, w, mine, theirs, m, v)


def _adamw_shard_cols(w, mine, theirs, m, v, half_index, name, steps=20):
    c, _, r = w.shape
    tc = c // steps
    assert tc * steps == c

    def body(c_ref, w_ref, mine_ref, theirs_ref, m_ref, v_ref, g_ref, d_ref, nm_ref, nv_ref):
        first = jnp.where(c_ref[0] == 0, mine_ref[...], theirs_ref[...])
        second = jnp.where(c_ref[0] == 0, theirs_ref[...], mine_ref[...])
        for lo, gv in ((0, first), (r // 2, second)):
            cols = slice(lo, lo + r // 2)
            d, nm, nv = _adam_update(w_ref[:, :, cols], gv, m_ref[:, :, cols], v_ref[:, :, cols])
            g_ref[:, :, cols] = gv
            d_ref[:, :, cols] = d
            nm_ref[:, :, cols] = nm
            nv_ref[:, :, cols] = nv

    full = pl.BlockSpec((tc, 1, r), lambda i, cc: (i, 0, 0))
    half = pl.BlockSpec((tc, 1, r // 2), lambda i, cc: (i, 0, 0))
    return pl.pallas_call(
        body, name=name,
        grid_spec=pltpu.PrefetchScalarGridSpec(
            num_scalar_prefetch=1, grid=(steps,),
            in_specs=[full, half, half, full, full], out_specs=[full] * 4),
        out_shape=[jax.ShapeDtypeStruct(w.shape, F32)] * 4,
        compiler_params=_params(("parallel",), 40 * 1024 * 1024),
    )(half_index, w, mine, theirs, m, v)


def _pair_sum(own, other, half_index, name, tr=256):
    _, r, c = own.shape
    rh = r // 2
    tr = min(tr, rh)
    nrb = rh // tr

    def body(c_ref, own_ref, oth_ref, out_ref):
        out_ref[...] = (own_ref[...] + oth_ref[...].astype(F32)).astype(BF16)

    return pl.pallas_call(
        body, name=name,
        grid_spec=pltpu.PrefetchScalarGridSpec(
            num_scalar_prefetch=1, grid=(N_CHIPS, nrb),
            in_specs=[pl.BlockSpec((None, tr, c), lambda k, i, cc: (k, cc[0] * nrb + i, 0)),
                      pl.BlockSpec((None, tr, c), lambda k, i, cc: (k, i, 0))],
            out_specs=pl.BlockSpec((None, tr, c), lambda k, i, cc: (k, i, 0))),
        out_shape=jax.ShapeDtypeStruct((N_CHIPS, rh, c), BF16),
        compiler_params=_params(("parallel", "parallel")),
    )(half_index, own, other)


def _chip_sum(sums, others, chip_index, name, tr=256):
    _, r, c = sums.shape
    tr = min(tr, r)

    def body(k_ref, own_ref, oth_ref, out_ref):
        acc = own_ref[...].astype(F32)
        for j in range(N_CHIPS - 1):
            acc = acc + oth_ref[j].astype(F32)
        out_ref[...] = acc

    return pl.pallas_call(
        body, name=name,
        grid_spec=pltpu.PrefetchScalarGridSpec(
            num_scalar_prefetch=1, grid=(r // tr,),
            in_specs=[pl.BlockSpec((None, tr, c), lambda i, kk: (kk[0], i, 0)),
                      pl.BlockSpec((N_CHIPS - 1, tr, c), lambda i, kk: (0, i, 0))],
            out_specs=pl.BlockSpec((tr, c), lambda i, kk: (i, 0))),
        out_shape=jax.ShapeDtypeStruct((r, c), F32),
        compiler_params=_params(("parallel",)),
    )(chip_index, sums, others)


HBM = pl.BlockSpec(memory_space=pltpu.HBM)


def _place():
    x, y, c = lax.axis_index("x"), lax.axis_index("y"), lax.axis_index("c")
    chips = [(1 - x, y), (x, 1 - y), (1 - x, 1 - y)]
    return x, y, c, chips


def _sibling_forward(land):
    def body(in_ref, out_ref, send, recv):
        x, y, c, chips = _place()
        rh = out_ref.shape[1] // 2
        cps = []
        for j, (px, py) in enumerate(chips):
            slot = out_ref.at[2 * px + py, pl.ds(c * rh, rh)]
            cp = pltpu.make_async_remote_copy(
                src_ref=slot, dst_ref=slot, send_sem=send.at[j], recv_sem=recv.at[j],
                device_id=(x, y, 1 - c), device_id_type=MESH)
            cp.start()
            cps.append(cp)
        for j, (px, py) in enumerate(chips):
            slot = out_ref.at[2 * px + py, pl.ds((1 - c) * rh, rh)]
            pltpu.make_async_remote_copy(
                src_ref=slot, dst_ref=slot, send_sem=send.at[j], recv_sem=recv.at[j],
                device_id=(x, y, 1 - c), device_id_type=MESH).wait_recv()
        for cp in cps:
            cp.wait_send()

    return pl.pallas_call(
        body, name="first_weights_sibling_forward", in_specs=[HBM], out_specs=HBM,
        out_shape=jax.ShapeDtypeStruct(land.shape, land.dtype), input_output_aliases={0: 0},
        scratch_shapes=[pltpu.SemaphoreType.DMA((3,)), pltpu.SemaphoreType.DMA((3,))],
    )(land)


def _sibling_swap(halves, name):
    na = len(halves)

    def body(*refs):
        ins, outs = refs[:na], refs[na:2 * na]
        send, recv = refs[2 * na:]
        x, y, c, _ = _place()
        cps = []
        for i in range(na):
            cp = pltpu.make_async_remote_copy(
                src_ref=ins[i], dst_ref=outs[i], send_sem=send.at[i], recv_sem=recv.at[i],
                device_id=(x, y, 1 - c), device_id_type=MESH)
            cp.start()
            cps.append(cp)
        for cp in cps:
            cp.wait()

    out_shape = [jax.ShapeDtypeStruct(h.shape, h.dtype) for h in halves]
    return pl.pallas_call(
        body, name=name, in_specs=[HBM] * na, out_specs=[HBM] * na, out_shape=out_shape,
        scratch_shapes=[pltpu.SemaphoreType.DMA((na,)), pltpu.SemaphoreType.DMA((na,))],
    )(*halves)


SEM = pl.BlockSpec(memory_space=pltpu.SEMAPHORE)
ANY = pl.BlockSpec(memory_space=pl.ANY)
EFFECT = pltpu.SideEffectType.DATAFLOW_SIDE_EFFECTING


def _split_copy_start(name, plan, srcs, lands, after):
    ns, nl = len(srcs), len(lands)

    def body(*refs):
        src_refs, land_refs = refs[:ns], refs[ns:ns + nl]
        send, recv = refs[ns + nl + 1], refs[ns + nl + 2]
        token = refs[-1]
        outgoing, _ = plan(src_refs, land_refs)
        for src, dst, dev, si, ri in outgoing:
            pltpu.make_async_remote_copy(src_ref=src, dst_ref=dst, send_sem=send.at[si], recv_sem=recv.at[ri],
                                         device_id=dev, device_id_type=MESH).start()
        token[...] = jnp.zeros_like(token)

    n_out, n_in = plan.counts
    thru = [pltpu.HBM(a.shape, a.dtype) for a in list(srcs) + list(lands)]
    res = pl.pallas_call(
        body, name=name,
        out_shape=[pltpu.SemaphoreType.DMA((n_out,)), pltpu.SemaphoreType.DMA((n_in,))] + thru
        + [jax.ShapeDtypeStruct((8, 128), F32)],
        in_specs=[HBM] * (ns + nl) + [ANY],
        out_specs=[SEM, SEM] + [HBM] * (ns + nl) + [pl.BlockSpec(memory_space=pltpu.VMEM)],
        input_output_aliases={i: 2 + i for i in range(ns + nl)},
        compiler_params=pltpu.CompilerParams(has_side_effects=EFFECT),
    )(*[pltpu.with_memory_space_constraint(a, pltpu.HBM) for a in list(srcs) + list(lands)], after)
    return res[0], res[1], res[2:2 + ns], res[2 + ns:2 + ns + nl], res[-1]


def _split_copy_wait(name, plan, send, recv, srcs, lands, after):
    ns, nl = len(srcs), len(lands)
    after = list(after) if isinstance(after, (list, tuple)) else [after]

    def body(*refs):
        src_refs, land_refs = refs[:ns], refs[ns:ns + nl]
        send_ref, recv_ref = refs[ns + nl], refs[ns + nl + 1]
        outgoing, arrivals = plan(src_refs, land_refs)
        for src, dst, dev, si, ri in outgoing:
            pltpu.make_async_remote_copy(src_ref=src, dst_ref=dst, send_sem=send_ref.at[si], recv_sem=recv_ref.at[ri],
                                         device_id=dev, device_id_type=MESH).wait_send()
        for view, ri in arrivals:
            pltpu.make_async_remote_copy(src_ref=view, dst_ref=view, send_sem=send_ref.at[0], recv_sem=recv_ref.at[ri],
                                         device_id=_place()[:3], device_id_type=MESH).wait_recv()

    thru = [pltpu.HBM(a.shape, a.dtype) for a in list(srcs) + list(lands)]
    res = pl.pallas_call(
        body, name=name, out_shape=thru,
        in_specs=[HBM] * (ns + nl) + [SEM, SEM] + [ANY] * len(after), out_specs=[HBM] * (ns + nl),
        input_output_aliases={i: i for i in range(ns + nl)},
        compiler_params=pltpu.CompilerParams(has_side_effects=EFFECT),
    )(*srcs, *lands, send, recv, *after)
    return res[:ns], res[ns:]


def _gather_plan(n_arrays):
    def plan(src_refs, land_refs):
        x, y, c, chips = _place()
        me = 2 * x + y
        outgoing, arrivals = [], []
        for i in range(n_arrays):
            rh = src_refs[i].shape[0] // 2
            mine = pl.ds(c * rh, rh)
            for j, (px, py) in enumerate(chips):
                for delta in range(2):
                    tc = c ^ delta
                    outgoing.append((src_refs[i].at[mine], land_refs[i].at[me, mine], (px, py, tc),
                                     6 * i + 2 * j + delta, 6 * i + 2 * j + delta))
                    theirs = pl.ds(tc * rh, rh)
                    arrivals.append((land_refs[i].at[2 * px + py, theirs], 6 * i + 2 * j + delta))
        return outgoing, arrivals

    plan.counts = (6 * n_arrays, 6 * n_arrays)
    return plan


def _first_gather_plan():
    def plan(src_refs, land_refs):
        x, y, c, chips = _place()
        me = 2 * x + y
        rh = src_refs[0].shape[0] // 2
        mine = pl.ds(c * rh, rh)
        outgoing, arrivals = [], []
        for j, (px, py) in enumerate(chips):
            outgoing.append((src_refs[0].at[mine], land_refs[0].at[me, mine], (px, py, c), j, j))
            arrivals.append((land_refs[0].at[2 * px + py, mine], j))
            outgoing.append((src_refs[1], land_refs[1].at[me], (px, py, c), 3 + j, 3 + j))
            arrivals.append((land_refs[1].at[2 * px + py], 3 + j))
        return outgoing, arrivals

    plan.counts = (6, 6)
    return plan


def _exchange_plan(n_arrays):
    def plan(src_refs, land_refs):
        x, y, c, chips = _place()
        outgoing, arrivals = [], []
        for i in range(n_arrays):
            for j, (px, py) in enumerate(chips):
                outgoing.append((src_refs[i].at[2 * px + py], land_refs[i].at[j], (px, py, c), 3 * i + j, 3 * i + j))
                arrivals.append((land_refs[i].at[j], 3 * i + j))
        return outgoing, arrivals

    plan.counts = (3 * n_arrays, 3 * n_arrays)
    return plan


def _small_allreduce(vec):
    r, cdim = vec.shape
    n_dev = 8

    def body(v_ref, out_ref, buf, send, recv):
        x, y, c, _ = _place()
        me = 4 * x + 2 * y + c
        buf[me] = v_ref[...]
        cps = []
        for k in range(1, n_dev):
            dx, dy, dc = (k >> 2) & 1, (k >> 1) & 1, k & 1
            peer = (x ^ dx, y ^ dy, c ^ dc)
            cp = pltpu.make_async_remote_copy(
                src_ref=v_ref, dst_ref=buf.at[me], send_sem=send.at[k - 1], recv_sem=recv.at[k - 1],
                device_id=peer, device_id_type=MESH)
            cp.start()
            cps.append(cp)
        for k in range(1, n_dev):
            dx, dy, dc = (k >> 2) & 1, (k >> 1) & 1, k & 1
            src = 4 * (x ^ dx) + 2 * (y ^ dy) + (c ^ dc)
            slot = buf.at[src]
            pltpu.make_async_remote_copy(
                src_ref=slot, dst_ref=slot, send_sem=send.at[k - 1], recv_sem=recv.at[k - 1],
                device_id=(x ^ dx, y ^ dy, c ^ dc), device_id_type=MESH).wait_recv()
        for cp in cps:
            cp.wait_send()
        acc = buf[0]
        for k in range(1, n_dev):
            acc = acc + buf[k]
        out_ref[...] = acc

    vm = pl.BlockSpec(memory_space=pltpu.VMEM)
    return pl.pallas_call(
        body, name="small_allreduce", in_specs=[vm], out_specs=vm,
        out_shape=jax.ShapeDtypeStruct((r, cdim), F32),
        scratch_shapes=[pltpu.VMEM((n_dev, r, cdim), F32), pltpu.SemaphoreType.DMA((n_dev - 1,)),
                        pltpu.SemaphoreType.DMA((n_dev - 1,))],
    )(vec)


def _a_cols_to_head_major(w):
    lead = w.shape[:-1]
    q = w[..., :A_QK].reshape(lead + (A_HEADS, A_DK))
    k = w[..., A_QK:2 * A_QK].reshape(lead + (A_HEADS, A_DK))
    v = w[..., 2 * A_QK:2 * A_QK + A_VW].reshape(lead + (A_HEADS, A_DV))
    z = w[..., 2 * A_QK + A_VW:].reshape(lead + (A_HEADS, A_DV))
    return jnp.concatenate([q, k, v, z], axis=-1).reshape(lead + (A_HEADS * A_HEAD_COLS,))


def _a_cols_from_head_major(w):
    lead = w.shape[:-1]
    w = w.reshape(lead + (A_HEADS, A_HEAD_COLS))
    parts = [w[..., :A_DK], w[..., A_DK:2 * A_DK], w[..., 2 * A_DK:2 * A_DK + A_DV], w[..., 2 * A_DK + A_DV:]]
    return jnp.concatenate([p.reshape(lead + (-1,)) for p in parts], axis=-1)


def _conv_cols_to_head_major(w):
    lead = w.shape[:-1]
    q = w[..., :A_QK].reshape(lead + (A_HEADS, A_DK))
    k = w[..., A_QK:2 * A_QK].reshape(lead + (A_HEADS, A_DK))
    v = w[..., 2 * A_QK:].reshape(lead + (A_HEADS, A_DV))
    return jnp.concatenate([q, k, v], axis=-1).reshape(lead + (A_HEADS * A_CONV_COLS,))


def _conv_cols_from_head_major(w):
    lead = w.shape[:-1]
    w = w.reshape(lead + (A_HEADS, A_CONV_COLS))
    parts = [w[..., :A_DK], w[..., A_DK:2 * A_DK], w[..., 2 * A_DK:]]
    return jnp.concatenate([p.reshape(lead + (-1,)) for p in parts], axis=-1)


def _to_stream(a, bn, d):
    rest = a.shape[1:]
    s = a.shape[0] // bn
    a = a.reshape((bn, s // d, d) + rest)
    a = jnp.swapaxes(a, 1, 2)
    return a.reshape((bn * d, s // d) + rest)


def _from_stream(a, bn, d):
    rest = a.shape[2:]
    ln = a.shape[1]
    a = a.reshape((bn, d, ln) + rest)
    a = jnp.swapaxes(a, 1, 2)
    return a.reshape((bn * ln * d,) + rest)


B_SUB = 512
B_SHARD_BLOCKS = (3 * B_GROUPS * B_W + B_W) // N_CHIPS // B_SUB


def _b_block(gi, jj):
    nb = (B_GROUPS * (jj // 2) + gi) * 2 + jj % 2
    return nb // B_SHARD_BLOCKS, nb % B_SHARD_BLOCKS


def _shard_major(g, ncols):
    r = g.shape[0]
    return jnp.swapaxes(g.reshape(r, N_CHIPS, ncols), 0, 1)


def _pack_rows(items):
    rows, offs = [], []
    at = 0
    for a in items:
        flat = a.reshape(-1).astype(F32)
        nr = -(-flat.shape[0] // 1024) * 8
        flat = jnp.pad(flat, (0, nr * 128 - flat.shape[0]))
        rows.append(flat.reshape(nr, 128))
        offs.append((at, nr, a.shape))
        at += nr
    return jnp.concatenate(rows, axis=0), offs


def _unpack_rows(packed, offs):
    out = []
    for at, nr, shape in offs:
        size = int(np.prod(shape)) if len(shape) else 1
        out.append(packed[at:at + nr].reshape(-1)[:size].reshape(shape))
    return out


def _local_step(x, positions, loss_target, norm_g, a_log, a_dt_bias, a_norm_g, b_q_norm_g, b_k_norm_g,
                start_token, first_weights, late_weights, b_grads_ready, a_grads_ready):
    bn, s, d = x.shape
    t = bn * s
    n_chunks = s // A_CHUNK
    x0 = x.reshape(t, d)
    h0 = _rms_fwd(x0, norm_g[0:1] + start_token, "rms0_fwd")
    inv_freq = ROPE_THETA ** (-jnp.arange(0, ROPE_DIMS, 2, dtype=F32) / ROPE_DIMS)
    freq_row = jnp.concatenate([inv_freq, inv_freq, jnp.zeros((128 - ROPE_DIMS,), F32)]).reshape(1, 128)
    posf = jnp.broadcast_to(positions.astype(F32).reshape(t, 1), (t, 128)) + start_token
    tabs = _rope_tables(posf, freq_row)
    tabs_s = [tabs if dil == 1 else [_to_stream(tb, bn, dil).reshape(t, 128) for tb in tabs] for dil in B_DIL]
    wa_in, conv_w, late_token = first_weights([h0] + [tb for ts in tabs_s for tb in ts])
    wa_main = _a_cols_to_head_major(wa_in[:, :A_MAIN])
    wa_tail = jnp.pad(wa_in[:, A_MAIN:], ((0, 0), (0, 128 - 2 * A_HEADS))) + late_token.astype(BF16)
    cw_hm = _conv_cols_to_head_major(conv_w)

    proj_a = _matmul(h0, wa_main, "nn", F32, "a_in_main", tm=2048)
    tail_a = _matmul(h0, wa_tail, "nn", F32, "a_in_tail")
    tail_t = jnp.swapaxes(tail_a[:, :2 * A_HEADS].reshape(bn, s, 2 * A_HEADS), 1, 2)
    tail_t = tail_t.reshape(bn, 2 * A_HEADS, n_chunks, A_CHUNK)
    beta, gc = _gdn_prep(tail_t, a_log[0], a_dt_bias[0])
    proj_a3 = proj_a.reshape(bn, s, A_MAIN)
    og_a, oraw_a, states, t_mats, conv_y = _gdn_fwd(proj_a3, cw_hm, beta, gc, a_norm_g)
    wa_out, wb_in, wb_out = late_weights(og_a)
    b_cols = [4 * B_W] + [3 * B_W] * (B_GROUPS - 1)
    x1, h1 = _out_proj(og_a.reshape(t, A_VW), wa_out, x0, "a_out", norm_g=norm_g[1:2])

    h1_s, proj_b, qkv_b, o_b, lse_b = [], [], [], [], []
    for gi, dil in enumerate(B_DIL):
        hs = h1 if dil == 1 else _to_stream(h1, bn, dil).reshape(t, d)
        ts = tabs_s[gi]
        pj = _matmul(hs, wb_in, "nn", BF16, f"b_in_g{gi}", tm=2048, tn=B_SUB, n=b_cols[gi], b_spec=pl.BlockSpec(
            (None, d, B_SUB), lambda i, j, kk, gi=gi: (_b_block(gi, j)[0], kk, _b_block(gi, j)[1])))
        qkv = _qk_prep(pj, *ts, b_q_norm_g[0, gi:gi + 1], b_k_norm_g[0, gi:gi + 1], f"qk_prep_g{gi}")
        o_s, lse_s = _attn_fwd(qkv.reshape(bn * dil, s // dil, 3 * B_W), f"attn_fwd_g{gi}")
        h1_s.append(hs), proj_b.append(pj), qkv_b.append(qkv)
        o_b.append(o_s.reshape(t, B_W) if dil == 1 else _from_stream(o_s, bn, dil))
        lse_b.append(lse_s.reshape(t, B_HEADS) if dil == 1 else _from_stream(lse_s, bn, dil))
    og_b = _merge_fwd(o_b, lse_b, proj_b[0])
    d_x2, loss_parts = _out_proj(og_b, wb_out, x1, "b_out_loss", target=loss_target.reshape(t, d))
    loss_local = jnp.sum(loss_parts)

    d_x2b = d_x2.astype(BF16)
    g_wb_out = _matmul(og_b, d_x2b, "tn", F32, "b_out_dw")
    d_og_b = _matmul(d_x2b, wb_out, "nt", BF16, "b_out_dx")
    d_o, lse_joint, delta, d_z = _merge_bwd(o_b, lse_b, proj_b[0], d_og_b)
    d_h1, g_qn, g_kn = [], [], []
    g_wb_in = lax.empty(wb_in.shape, F32)
    for gi, dil in enumerate(B_DIL):
        if dil == 1:
            do_s, lj_s, dl_s = d_o, lse_joint, delta
        else:
            do_s, lj_s, dl_s = (_to_stream(a, bn, dil).reshape(t, -1) for a in (d_o, lse_joint, delta))
        ns, ln = bn * dil, s // dil
        dq, dk, dv = _attn_bwd(qkv_b[gi].reshape(ns, ln, 3 * B_W), do_s.reshape(ns, ln, B_W),
                               lj_s.reshape(ns, ln, B_HEADS), dl_s.reshape(ns, ln, B_HEADS), f"attn_bwd_g{gi}")
        d_pj, d_gain = _qk_prep_bwd(proj_b[gi], *tabs_s[gi], b_q_norm_g[0, gi:gi + 1], b_k_norm_g[0, gi:gi + 1],
                                    dq.reshape(t, B_W), dk.reshape(t, B_W), dv.reshape(t, B_W),
                                    d_z if gi == 0 else None, f"qk_prep_bwd_g{gi}")
        g_wb_in = _matmul(h1_s[gi], d_pj, "tn", F32, f"b_in_dw_g{gi}", tn=B_SUB, tk=2048, into=(g_wb_in, pl.BlockSpec(
            (None, d, B_SUB), lambda i, j, kk, gi=gi: (_b_block(gi, j)[0], i, _b_block(gi, j)[1]))))
        dh = _matmul(d_pj, wb_in, "nt", BF16, f"b_in_dx_g{gi}", tm=2048, tk=B_SUB, n=d, b_spec=pl.BlockSpec(
            (None, d, B_SUB), lambda i, j, kk, gi=gi: (_b_block(gi, kk)[0], j, _b_block(gi, kk)[1])))
        d_h1.append(dh if dil == 1 else _from_stream(dh.reshape(ns, ln, d), bn, dil))
        g_qn.append(d_gain[0]), g_kn.append(d_gain[1])
    d_x1, g_norm1 = _rms_bwd(x1, norm_g[1:2], d_h1, d_x2, "rms1_bwd")

    d_x1b = d_x1.astype(BF16)
    g_wa_out = _matmul(og_a.reshape(t, A_VW), d_x1b, "tn", F32, "a_out_dw", tk=2048)
    b_token = b_grads_ready(g_wb_in, g_wb_out, g_wa_out)
    d_og_a = _matmul(d_x1b, wa_out, "nt", BF16, "a_out_dx")
    d_pa, d_gc, d_beta, d_cw, d_ng = _gdn_bwd(proj_a3, cw_hm, beta, gc, a_norm_g + b_token, oraw_a, states,
                                              t_mats, conv_y, d_og_a.reshape(bn, s, A_VW))
    d_tail_t, d_alog, d_dtb = _gdn_prep_bwd(tail_t, a_log[0], a_dt_bias[0], d_gc, d_beta)
    d_tail = jnp.swapaxes(d_tail_t.reshape(bn, 2 * A_HEADS, s), 1, 2).reshape(t, 2 * A_HEADS)
    d_tail = jnp.pad(d_tail, ((0, 0), (0, 128 - 2 * A_HEADS))).astype(BF16)
    d_pa = d_pa.reshape(t, A_MAIN)
    g_wa_main = _matmul(h0, d_pa, "tn", F32, "a_in_dw_main", tk=2048)
    g_wa_tail = _matmul(h0, d_tail, "tn", F32, "a_in_dw_tail")
    g_wa_in = jnp.concatenate([_a_cols_from_head_major(g_wa_main), g_wa_tail[:, :2 * A_HEADS]], axis=1)
    a_token = a_grads_ready(g_wa_in)
    d_h0t = _matmul(d_tail + a_token.astype(BF16), wa_tail, "nt", F32, "a_in_dx_tail")
    d_x0, g_norm0 = _in_proj_bwd(d_pa, wa_main, d_h0t, x0, norm_g[0:1], d_x1, "a_in_dx_rms0_bwd")

    gfull = {
        "norm_g": jnp.concatenate([g_norm0, g_norm1], axis=0), "a_w_in": g_wa_in,
        "a_conv_w": _conv_cols_from_head_major(jnp.sum(d_cw, axis=0)),
        "a_log": jnp.sum(d_alog[:, :, 0], axis=0), "a_dt_bias": jnp.sum(d_dtb[:, :, 0], axis=0),
        "a_norm_g": jnp.sum(d_ng[:, :, 0, :], axis=(0, 1)), "a_w_out": g_wa_out, "b_w_in": g_wb_in,
        "b_q_norm_g": jnp.stack(g_qn), "b_k_norm_g": jnp.stack(g_kn), "b_w_out": g_wb_out}
    return loss_local, d_x0.reshape(bn, s, d), gfull


def kernel(x, positions, norm_g, a_w_in, a_conv_w, a_log, a_dt_bias, a_norm_g, a_w_out, b_w_in, b_q_norm_g, b_k_norm_g, b_w_out, loss_target, m_norm_g, m_a_w_in, m_a_conv_w, m_a_log, m_a_dt_bias, m_a_norm_g, m_a_w_out, m_b_w_in, m_b_q_norm_g, m_b_k_norm_g, m_b_w_out, v_norm_g, v_a_w_in, v_a_conv_w, v_a_log, v_a_dt_bias, v_a_norm_g, v_a_w_out, v_b_w_in, v_b_q_norm_g, v_b_k_norm_g, v_b_w_out):
    d = x.shape[2]
    my_c = lax.axis_index("c")
    my_chip = 2 * lax.axis_index("x") + lax.axis_index("y")

    half_index = jnp.reshape(my_c, (1,)).astype(jnp.int32)
    chip_index = jnp.reshape(my_chip, (1,)).astype(jnp.int32)
    def landing(shard):
        return lax.dynamic_update_slice(lax.empty((N_CHIPS,) + shard.shape, shard.dtype), shard[None],
                                        (my_chip,) + (0,) * shard.ndim)

    first_shards = [a_w_in[0].astype(BF16), a_conv_w[0]]
    first_plan = _first_gather_plan()
    first = _split_copy_start("first_weights_start", first_plan, first_shards,
                              [landing(s) for s in first_shards], half_index)
    pending = {}
    late_shards = [(w[0] + first[4][0, 0]).astype(BF16) for w in (a_w_out, b_w_in, b_w_out)]
    late_lands = [landing(s) for s in late_shards]

    def first_weights(after):
        _, (ga_in, g_conv) = _split_copy_wait("first_weights_wait", first_plan, *first[:4],
                                              list(after) + late_lands)
        ga_in = _sibling_forward(ga_in)
        wa_in = jnp.concatenate([ga_in[k] for k in range(N_CHIPS)], axis=1)
        conv_w = jnp.concatenate([g_conv[k] for k in range(N_CHIPS)], axis=1)
        plan = _gather_plan(len(late_shards))
        pending["late"] = (plan,) + tuple(_split_copy_start(
            "late_weights_start", plan, late_shards, late_lands, conv_w))
        return wa_in, conv_w, pending["late"][5][0, 0]

    def late_weights(after):
        plan, send, recv, srcs, lands, _ = pending["late"]
        _, (ga_out, gb_in, gb_out) = _split_copy_wait("late_weights_wait", plan, send, recv, srcs, lands, after)
        return ga_out.reshape(A_VW, d), gb_in, gb_out.reshape(B_W, d)

    def reduce_to_chip_sums(mats, tag):
        half = lambda g: lax.dynamic_slice_in_dim(g, (1 - my_c) * (g.shape[1] // 2), g.shape[1] // 2, axis=1)
        recv_sib = _sibling_swap([half(g).astype(BF16) for g in mats], f"grad_{tag}_sibling_swap")
        return [_pair_sum(g, r, half_index, f"grad_{tag}_pair_sum_{i}") for i, (g, r) in enumerate(zip(mats, recv_sib))]

    def start_exchange(tag, mats):
        sums = reduce_to_chip_sums(mats, tag)
        lands = [lax.empty((N_CHIPS - 1,) + s.shape[1:], BF16) for s in sums]
        plan = _exchange_plan(len(mats))
        pending[tag] = (plan,) + tuple(_split_copy_start(f"grad_{tag}_exchange_start", plan, sums, lands, chip_index))
        return pending[tag][5][0, 0]

    def finish_exchange(tag, after):
        plan, send, recv, srcs, lands, _ = pending[tag]
        return _split_copy_wait(f"grad_{tag}_exchange_wait", plan, send, recv, srcs, lands, after)

    def b_grads_ready(g_wb_in, g_wb_out, g_wa_out):
        return start_exchange("b", [g_wb_in, g_wb_out.reshape(N_CHIPS, -1, d), g_wa_out.reshape(N_CHIPS, -1, d)])

    def a_grads_ready(g_wa_in):
        return start_exchange("a", [_shard_major(g_wa_in, a_w_in.shape[2])])

    loss_local, d_x0, gfull = _local_step(x, positions, loss_target, norm_g, a_log, a_dt_bias, a_norm_g,
                                          b_q_norm_g, b_k_norm_g, first[4][0, 0], first_weights, late_weights,
                                          b_grads_ready, a_grads_ready)

    small = [gfull["norm_g"], gfull["a_conv_w"], gfull["a_log"], gfull["a_dt_bias"], gfull["a_norm_g"],
             gfull["b_q_norm_g"], gfull["b_k_norm_g"], loss_local]
    packed, offs = _pack_rows(small)
    reduced = _small_allreduce(packed)
    g_norm, g_conv_all, g_alog, g_dtb, g_ang, g_q, g_k, loss = _unpack_rows(reduced, offs)
    g_conv_mine = lax.dynamic_slice_in_dim(g_conv_all, my_chip * a_conv_w.shape[2], a_conv_w.shape[2], axis=1)

    b_sums, b_received = finish_exchange("b", d_x0)
    a_sums, a_received = finish_exchange("a", reduced)
    chip_sums = [a_sums[0], b_sums[2], b_sums[0], b_sums[1]]
    received = [a_received[0], b_received[2], b_received[0], b_received[1]]
    halves = [_chip_sum(s, r, chip_index, f"grad_chip_sum_{i}") for i, (s, r) in enumerate(zip(chip_sums, received))]
    theirs = _sibling_swap(halves, "grad_sibling_join")
    big = ("a_w_in", "a_w_out", "b_w_in", "b_w_out")
    big_halves = dict(zip(big, zip(halves, theirs)))

    grads = {
        "norm_g": g_norm, "a_conv_w": g_conv_mine[None], "a_log": g_alog[None], "a_dt_bias": g_dtb[None],
        "a_norm_g": g_ang[None], "b_q_norm_g": g_q[None], "b_k_norm_g": g_k[None]}
    weights = {"norm_g": norm_g, "a_w_in": a_w_in, "a_conv_w": a_conv_w, "a_log": a_log, "a_dt_bias": a_dt_bias,
               "a_norm_g": a_norm_g, "a_w_out": a_w_out, "b_w_in": b_w_in, "b_q_norm_g": b_q_norm_g,
               "b_k_norm_g": b_k_norm_g, "b_w_out": b_w_out}
    m_in = {"norm_g": m_norm_g, "a_w_in": m_a_w_in, "a_conv_w": m_a_conv_w, "a_log": m_a_log,
            "a_dt_bias": m_a_dt_bias, "a_norm_g": m_a_norm_g, "a_w_out": m_a_w_out, "b_w_in": m_b_w_in,
            "b_q_norm_g": m_b_q_norm_g, "b_k_norm_g": m_b_k_norm_g, "b_w_out": m_b_w_out}
    v_in = {"norm_g": v_norm_g, "a_w_in": v_a_w_in, "a_conv_w": v_a_conv_w, "a_log": v_a_log,
            "a_dt_bias": v_a_dt_bias, "a_norm_g": v_a_norm_g, "a_w_out": v_a_w_out, "b_w_in": v_b_w_in,
            "b_q_norm_g": v_b_q_norm_g, "b_k_norm_g": v_b_k_norm_g, "b_w_out": v_b_w_out}
    names = list(weights)

    delta_w, new_m, new_v = {}, {}, {}
    for nm in big:
        mine, other = big_halves[nm]
        if weights[nm].shape[2] % 128:
            cols = lambda a: jnp.transpose(a, (2, 0, 1))
            half_cols = lambda a: jnp.transpose(a)[:, None, :]
            outs = _adamw_shard_cols(cols(weights[nm]), half_cols(mine), half_cols(other), cols(m_in[nm]),
                                     cols(v_in[nm]), half_index, f"adamw_{nm}")
            outs = [jnp.transpose(o, (1, 2, 0)) for o in outs]
        else:
            outs = _adamw_shard(weights[nm], mine, other, m_in[nm], v_in[nm], half_index, f"adamw_{nm}")
        grads[nm], delta_w[nm], new_m[nm], new_v[nm] = outs
    small_names = [nm for nm in names if nm not in big]
    packs = [_pack_rows([src[nm] for nm in small_names]) for src in (weights, grads, m_in, v_in)]
    offs = packs[0][1]
    dl, m2, v2 = _adamw(packs[0][0], packs[1][0], packs[2][0], packs[3][0], "adamw_small")
    for nm, a, b, c2 in zip(small_names, _unpack_rows(dl, offs), _unpack_rows(m2, offs), _unpack_rows(v2, offs)):
        delta_w[nm], new_m[nm], new_v[nm] = a, b, c2

    return (loss, d_x0, *[grads[nm] for nm in names], *[delta_w[nm] for nm in names],
            *[new_m[nm] for nm in names], *[new_v[nm] for nm in names])
```

```python
import jax
import jax.numpy as jnp
import numpy as np
from jax import lax
from jax.experimental import pallas as pl
from jax.experimental.pallas import tpu as pltpu

F32 = jnp.float32
BF16 = jnp.bfloat16
MESH = pl.DeviceIdType.MESH

EPS = 1e-6
A_HEADS = 8
A_DK = 128
A_DV = 256
A_QK = A_HEADS * A_DK
A_VW = A_HEADS * A_DV
A_MAIN = 2 * A_QK + 2 * A_VW
A_HEAD_COLS = 2 * A_DK + 2 * A_DV
A_CONV_COLS = 2 * A_DK + A_DV
A_CHUNK = 64
A_CONV = 4
B_GROUPS = 3
B_HEADS = 8
B_DH = 128
B_W = B_HEADS * B_DH
B_DIL = (1, 4, 16)
B_BLK = 128
ROPE_THETA = 500000.0
ROPE_DIMS = B_DH // 4
ADAM_LR, ADAM_B1, ADAM_B2, ADAM_EPS, ADAM_WD, ADAM_STEP = 0.001, 0.9, 0.999, 1e-08, 0.01, 10
N_CHIPS = 4
VMEM_BIG = 56 * 1024 * 1024
DW_K = 4096


def _params(sem=None, vmem=None):
    return pltpu.CompilerParams(dimension_semantics=sem, vmem_limit_bytes=vmem)


def _dot(a, b, ca, cb):
    return lax.dot_general(a.astype(BF16), b.astype(BF16), (((ca,), (cb,)), ((), ())),
                           preferred_element_type=F32)


def _split3(a):
    hi = a.astype(BF16)
    r = a - hi.astype(F32)
    mid = r.astype(BF16)
    lo = (r - mid.astype(F32)).astype(BF16)
    return hi, mid, lo


def _sigmoid(y):
    return 1.0 / (1.0 + jnp.exp(-y))


def _silu(y):
    return y * _sigmoid(y)


def _silu_and_slope(y):
    s = _sigmoid(y)
    return y * s, s * (1.0 + y * (1.0 - s))


def _matmul(a, b, mode, out_dtype, name, res=None, tm=1024, tn=1024, tk=1024, n=None, b_spec=None, into=None):
    m, k = a.shape[::-1] if mode == "tn" else a.shape
    if n is None:
        n = b.shape[0] if mode == "nt" else b.shape[1]
    tm, tn, tk = min(tm, m), min(tn, n), min(tk, k)
    assert m % tm == 0 and n % tn == 0 and k % tk == 0, (name, a.shape, b.shape)
    nk = k // tk
    dims = {"nn": ((1,), (0,)), "nt": ((1,), (1,)), "tn": ((0,), (0,))}[mode]

    def body(*refs):
        a_ref, b_ref = refs[0], refs[1]
        r_ref = refs[2] if res is not None else None
        o_ref = refs[2 + (res is not None) + (into is not None)]
        prod = lax.dot_general(a_ref[...], b_ref[...], (dims, ((), ())), preferred_element_type=F32)

        def finish(r):
            if res is not None:
                r = r + r_ref[...]
            o_ref[...] = r.astype(out_dtype)

        if nk == 1:
            finish(prod)
            return
        acc = refs[-1]
        kk = pl.program_id(2)

        @pl.when(kk == 0)
        def _():
            acc[...] = prod

        @pl.when((kk > 0) & (kk < nk - 1))
        def _():
            acc[...] += prod

        @pl.when(kk == nk - 1)
        def _():
            finish(acc[...] + prod)

    a_spec = pl.BlockSpec((tm, tk), lambda i, j, kk: (i, kk))
    if mode == "tn":
        a_spec = pl.BlockSpec((tk, tm), lambda i, j, kk: (kk, i))
    if b_spec is None and mode == "nt":
        b_spec = pl.BlockSpec((tn, tk), lambda i, j, kk: (j, kk))
    elif b_spec is None:
        b_spec = pl.BlockSpec((tk, tn), lambda i, j, kk: (kk, j))
    in_specs = [a_spec, b_spec]
    args = [a, b]
    if res is not None:
        in_specs.append(pl.BlockSpec((tm, tn), lambda i, j, kk: (i, j)))
        args.append(res)
    out_spec = pl.BlockSpec((tm, tn), lambda i, j, kk: (i, j))
    out_shape = jax.ShapeDtypeStruct((m, n), out_dtype)
    aliases = {}
    if into is not None:
        assert res is None
        buf, out_spec = into
        out_shape = jax.ShapeDtypeStruct(buf.shape, buf.dtype)
        in_specs.append(ANY)
        args.append(buf)
        aliases = {2: 0}
    return pl.pallas_call(
        body, name=name, grid=(m // tm, n // tn, nk),
        in_specs=in_specs, out_specs=out_spec, out_shape=out_shape, input_output_aliases=aliases,
        scratch_shapes=[pltpu.VMEM((tm, tn), F32)] if nk > 1 else [],
        compiler_params=_params(("parallel", "parallel", "arbitrary"), 48 * 1024 * 1024),
    )(*args)


def _rms_fwd(x, g, name, tm=512):
    t, d = x.shape

    def body(x_ref, g_ref, h_ref):
        xv = x_ref[...]
        r = lax.rsqrt(jnp.mean(xv * xv, axis=-1, keepdims=True) + EPS)
        h_ref[...] = (xv * r * g_ref[...]).astype(BF16)

    return pl.pallas_call(
        body, name=name, grid=(t // tm,),
        in_specs=[pl.BlockSpec((tm, d), lambda i: (i, 0)), pl.BlockSpec((1, d), lambda i: (0, 0))],
        out_specs=pl.BlockSpec((tm, d), lambda i: (i, 0)),
        out_shape=jax.ShapeDtypeStruct((t, d), BF16),
        compiler_params=_params(("parallel",)),
    )(x, g)


def _rms_bwd(x, g, dhs, dres, name, tm=512):
    t, d = x.shape
    n_dh = len(dhs)

    def body(*refs):
        x_ref, g_ref = refs[0], refs[1]
        dh_refs = refs[2:2 + n_dh]
        dres_ref, dx_ref, dg_ref = refs[2 + n_dh:]
        i = pl.program_id(0)

        @pl.when(i == 0)
        def _():
            dg_ref[...] = jnp.zeros_like(dg_ref)

        xv = x_ref[...]
        r = lax.rsqrt(jnp.mean(xv * xv, axis=-1, keepdims=True) + EPS)
        xh = xv * r
        dh = dh_refs[0][...].astype(F32)
        for ref in dh_refs[1:]:
            dh = dh + ref[...].astype(F32)
        dg_ref[0:1, :] += jnp.sum(dh * xh, axis=0, keepdims=True)
        dxh = dh * g_ref[...]
        dx = r * (dxh - xh * jnp.mean(dxh * xh, axis=-1, keepdims=True))
        dx_ref[...] = dx + dres_ref[...]

    row = pl.BlockSpec((tm, d), lambda i: (i, 0))
    dx, dg = pl.pallas_call(
        body, name=name, grid=(t // tm,),
        in_specs=[row, pl.BlockSpec((1, d), lambda i: (0, 0))] + [row] * n_dh + [row],
        out_specs=[row, pl.BlockSpec((8, d), lambda i: (0, 0))],
        out_shape=[jax.ShapeDtypeStruct((t, d), F32), jax.ShapeDtypeStruct((8, d), F32)],
        compiler_params=_params(("arbitrary",)),
    )(x, g, *dhs, dres)
    return dx, dg[0:1]


def _in_proj_bwd(dp, w, dh_more, x, g, dres, name, tm=512, tk=2048):
    t, k = dp.shape
    d = w.shape[0]
    nk = k // tk

    def body(dp_ref, w_ref, more_ref, x_ref, g_ref, dres_ref, dx_ref, dg_ref, acc):
        i, kk = pl.program_id(0), pl.program_id(1)

        @pl.when((i == 0) & (kk == 0))
        def _():
            dg_ref[...] = jnp.zeros_like(dg_ref)

        prod = lax.dot_general(dp_ref[...], w_ref[...], (((1,), (1,)), ((), ())), preferred_element_type=F32)

        @pl.when(kk == 0)
        def _():
            acc[...] = prod

        @pl.when((kk > 0) & (kk < nk - 1))
        def _():
            acc[...] += prod

        @pl.when(kk == nk - 1)
        def _():
            dh = acc[...] + prod + more_ref[...]
            xv = x_ref[...]
            r = lax.rsqrt(jnp.mean(xv * xv, axis=-1, keepdims=True) + EPS)
            xh = xv * r
            dg_ref[0:1, :] += jnp.sum(dh * xh, axis=0, keepdims=True)
            dxh = dh * g_ref[...]
            dx_ref[...] = r * (dxh - xh * jnp.mean(dxh * xh, axis=-1, keepdims=True)) + dres_ref[...]

    row = pl.BlockSpec((tm, d), lambda i, kk: (i, 0))
    dx, dg = pl.pallas_call(
        body, name=name, grid=(t // tm, nk),
        in_specs=[pl.BlockSpec((tm, tk), lambda i, kk: (i, kk)), pl.BlockSpec((d, tk), lambda i, kk: (0, kk)),
                  row, row, pl.BlockSpec((1, d), lambda i, kk: (0, 0)), row],
        out_specs=[row, pl.BlockSpec((8, d), lambda i, kk: (0, 0))],
        out_shape=[jax.ShapeDtypeStruct((t, d), F32), jax.ShapeDtypeStruct((8, d), F32)],
        scratch_shapes=[pltpu.VMEM((tm, d), F32)],
        compiler_params=_params(("arbitrary", "arbitrary"), 48 * 1024 * 1024),
    )(dp, w, dh_more, x, g, dres)
    return dx, dg[0:1]


def _out_proj(a, w, res, name, norm_g=None, target=None, tm=512):
    t, k = a.shape
    d = w.shape[1]
    nb = t // tm

    def body(a_ref, w_ref, r_ref, x_ref, o1_ref, o2_ref):
        y = jnp.dot(a_ref[...], w_ref[...], preferred_element_type=F32) + r_ref[...]
        if norm_g is not None:
            o1_ref[...] = y
            r = lax.rsqrt(jnp.mean(y * y, axis=-1, keepdims=True) + EPS)
            o2_ref[...] = (y * r * x_ref[...]).astype(BF16)
        else:
            e = y - x_ref[...]
            o1_ref[...] = e * (1.0 / d)
            s = jnp.sum(jnp.sum(e * e, axis=1, keepdims=True), axis=0, keepdims=True) * (0.5 / d)
            o2_ref[...] = jnp.broadcast_to(s, (8, 128))

    row = pl.BlockSpec((tm, d), lambda i: (i, 0))
    if norm_g is not None:
        extra, extra_spec = norm_g, pl.BlockSpec((1, d), lambda i: (0, 0))
        out2_spec, out2_shape = row, jax.ShapeDtypeStruct((t, d), BF16)
    else:
        extra, extra_spec = target, row
        out2_spec = pl.BlockSpec((None, 8, 128), lambda i: (i, 0, 0))
        out2_shape = jax.ShapeDtypeStruct((nb, 8, 128), F32)
    o1, o2 = pl.pallas_call(
        body, name=name, grid=(nb,),
        in_specs=[pl.BlockSpec((tm, k), lambda i: (i, 0)), pl.BlockSpec((k, d), lambda i: (0, 0)), row, extra_spec],
        out_specs=[row, out2_spec], out_shape=[jax.ShapeDtypeStruct((t, d), F32), out2_shape],
        compiler_params=_params(("parallel",), 48 * 1024 * 1024),
    )(a, w, res, extra)
    return (o1, o2) if norm_g is not None else (o1, o2[:, 0, 0])


def _softplus(x):
    t = jnp.exp(-jnp.abs(x))
    return jnp.maximum(x, 0.0) + jnp.where(t < 1e-3, t * (1.0 - 0.5 * t), jnp.log(1.0 + t))


def _tri(rows_le_cols):
    r = lax.broadcasted_iota(jnp.int32, (A_CHUNK, A_CHUNK), 0)
    c = lax.broadcasted_iota(jnp.int32, (A_CHUNK, A_CHUNK), 1)
    return jnp.where((r <= c) if rows_le_cols else (r >= c), 1.0, 0.0).astype(BF16)


def _dot_exact_rhs(a, ones_bf16):
    dn = (((1,), (0,)), ((), ()))
    hi, mid, lo = _split3(a)
    out = lax.dot_general(hi, ones_bf16, dn, preferred_element_type=F32)
    out = out + lax.dot_general(mid, ones_bf16, dn, preferred_element_type=F32)
    return out + lax.dot_general(lo, ones_bf16, dn, preferred_element_type=F32)


def _gdn_prep(tail_t, a_log, dt_bias):
    bn, _, n, c = tail_t.shape

    def body(t_ref, alog_ref, dtb_ref, beta_ref, gc_ref):
        upper = _tri(True)
        for h in range(A_HEADS):
            beta_ref[h] = _sigmoid(t_ref[h])
            ea = jnp.exp(jnp.full((n, c), alog_ref[h], F32))
            g = -ea * _softplus(t_ref[A_HEADS + h] + dtb_ref[h])
            gc_ref[h] = _dot_exact_rhs(g, upper)

    smem = pl.BlockSpec(memory_space=pltpu.SMEM)
    blk = pl.BlockSpec((None, A_HEADS, n, c), lambda b: (b, 0, 0, 0))
    return pl.pallas_call(
        body, name="gdn_prep", grid=(bn,),
        in_specs=[pl.BlockSpec((None, 2 * A_HEADS, n, c), lambda b: (b, 0, 0, 0)), smem, smem],
        out_specs=[blk, blk],
        out_shape=[jax.ShapeDtypeStruct((bn, A_HEADS, n, c), F32)] * 2,
        compiler_params=_params(("parallel",)),
    )(tail_t, a_log, dt_bias)


def _gdn_prep_bwd(tail_t, a_log, dt_bias, d_gc, d_beta):
    bn, _, n, c = tail_t.shape

    def body(t_ref, alog_ref, dtb_ref, dgc_ref, dbeta_ref, dt_ref, dal_ref, ddt_ref):
        lower = _tri(False)
        for h in range(A_HEADS):
            beta = _sigmoid(t_ref[h])
            dt_ref[h] = dbeta_ref[h] * beta * (1.0 - beta)
            dg = _dot_exact_rhs(dgc_ref[h], lower)
            ea = jnp.exp(jnp.full((n, c), alog_ref[h], F32))
            xa = t_ref[A_HEADS + h] + dtb_ref[h]
            g = -ea * _softplus(xa)
            dxa = -ea * dg * _sigmoid(xa)
            dt_ref[A_HEADS + h] = dxa
            s1 = jnp.sum(jnp.sum(g * dg, axis=1, keepdims=True), axis=0, keepdims=True)
            s2 = jnp.sum(jnp.sum(dxa, axis=1, keepdims=True), axis=0, keepdims=True)
            dal_ref[h:h + 1, :] = jnp.broadcast_to(s1, (1, 128))
            ddt_ref[h:h + 1, :] = jnp.broadcast_to(s2, (1, 128))

    smem = pl.BlockSpec(memory_space=pltpu.SMEM)
    blk8 = pl.BlockSpec((None, A_HEADS, n, c), lambda b: (b, 0, 0, 0))
    blk16 = pl.BlockSpec((None, 2 * A_HEADS, n, c), lambda b: (b, 0, 0, 0))
    sm = pl.BlockSpec((None, A_HEADS, 128), lambda b: (b, 0, 0))
    return pl.pallas_call(
        body, name="gdn_prep_bwd", grid=(bn,),
        in_specs=[blk16, smem, smem, blk8, blk8],
        out_specs=[blk16, sm, sm],
        out_shape=[jax.ShapeDtypeStruct((bn, 2 * A_HEADS, n, c), F32),
                   jax.ShapeDtypeStruct((bn, A_HEADS, 128), F32),
                   jax.ShapeDtypeStruct((bn, A_HEADS, 128), F32)],
        compiler_params=_params(("parallel",)),
    )(tail_t, a_log, dt_bias, d_gc, d_beta)


HALO = 8


def _conv_taps(xw, w):
    y = w[A_CONV - 1:A_CONV, :] * xw
    for j in range(1, A_CONV):
        y = y + w[A_CONV - 1 - j:A_CONV - j, :] * pltpu.roll(xw, j, 0)
    return y[HALO:, :]


def _row_to_col(row, eye):
    c = eye.shape[0]
    return jnp.sum(jnp.where(eye, jnp.broadcast_to(row, (c, c)), 0.0), axis=1, keepdims=True)


def _col_to_row(col, eye):
    c = eye.shape[0]
    return jnp.sum(jnp.where(eye, jnp.broadcast_to(col, (c, c)), 0.0), axis=0, keepdims=True)


def _unit_lower_inverse(a, ri, ci):
    eye = jnp.where(ri == ci, 1.0, 0.0)
    a8 = jnp.where((ri >> 3) == (ci >> 3), a, 0.0)
    a2 = _dot(a8, a8, 1, 0)
    yield
    a4 = _dot(a2, a2, 1, 0)
    t = eye - a8
    t = t + _dot(t, a2, 1, 0)
    yield
    t = t + _dot(t, a4, 1, 0)
    yield
    for sh in (3, 4, 5):
        off = jnp.where(((ri >> (sh + 1)) == (ci >> (sh + 1))) & ((ri >> sh) != (ci >> sh)), a, 0.0)
        left = _dot(t, off, 1, 0)
        yield
        t = t - _dot(left, t, 1, 0)
        yield
    return t


def _round_robin(gens):
    live = list(gens)
    while live:
        nxt = []
        for g in live:
            try:
                next(g)
                nxt.append(g)
            except StopIteration:
                pass
        live = nxt


def _gdn_chunk_core(q, k, v, g_row, b_row, t_mat, ri, ci):
    eye = ri == ci
    g_col = _row_to_col(g_row, eye)
    b_col = _row_to_col(b_row, eye)
    causal = ri >= ci
    strict = ri > ci
    dec = jnp.where(causal, jnp.exp(jnp.where(causal, g_col - g_row, 0.0)), 0.0)
    gam = jnp.exp(g_col)
    g_last = g_row[:, A_CHUNK - 1:A_CHUNK]
    gam_last = jnp.exp(g_last)
    e = jnp.exp(g_last - g_col)
    kb = k * b_col
    bv = v * b_col
    kbg = kb * gam
    q16, k16, kb16 = q.astype(BF16), k.astype(BF16), kb.astype(BF16)
    kk = _dot(kb16, k16, 1, 1)
    p = _dot(q16, k16, 1, 1) * dec
    yield
    a_mat = jnp.where(strict, kk * dec, 0.0)
    if t_mat is None:
        t_mat = yield from _unit_lower_inverse(a_mat, ri, ci)
    t16 = t_mat.astype(BF16)
    u = _dot(t16, bv, 1, 0)
    w = _dot(t16, kbg, 1, 0)
    yield
    return dict(eye=eye, g_col=g_col, b_col=b_col, dec=dec, strict=strict, causal=causal, gam=gam,
                gam_last=gam_last, e=e, kb=kb, bv=bv, kbg=kbg, a_mat=a_mat, t_mat=t_mat, u=u, w=w, p=p,
                qg=q * gam, kd=k * e, q16=q16, k16=k16, kb16=kb16, t16=t16)


A_SEQ_BLK = 256
A_BLK_CHUNKS = A_SEQ_BLK // A_CHUNK


def _gdn_halo(proj_hm):
    bn, s, w = proj_hm.shape
    last = proj_hm.reshape(bn, s // A_SEQ_BLK, A_SEQ_BLK, w)[:, :, A_SEQ_BLK - HALO:, :]
    return jnp.concatenate([jnp.zeros((bn, 1, HALO, w), proj_hm.dtype), last[:, :-1]], axis=1)


def _gdn_window(x_ref, halo_ref, ci, first, lo):
    if first:
        return jnp.concatenate([halo_ref[:, lo:lo + A_CONV_COLS], x_ref[0:A_CHUNK, lo:lo + A_CONV_COLS]], axis=0)
    start = pl.multiple_of(ci * A_CHUNK - HALO, HALO)
    return x_ref[pl.ds(start, A_CHUNK + HALO), lo:lo + A_CONV_COLS]


def _gdn_chunk_prep(xw, cw, y=None):
    if y is None:
        y = _conv_taps(xw, cw)
    a, slope = _silu_and_slope(y)
    aq, ak, v = a[:, 0:A_DK], a[:, A_DK:2 * A_DK], a[:, 2 * A_DK:]
    rq = lax.rsqrt(jnp.sum(aq * aq, axis=1, keepdims=True) + EPS)
    rk = lax.rsqrt(jnp.sum(ak * ak, axis=1, keepdims=True) + EPS)
    return dict(xw=xw, y=y, slope=slope, aq=aq, ak=ak, rq=rq, rk=rk, q=aq * rq * (A_DK ** -0.5), k=ak * rk, v=v)


def _gdn_fwd(proj_hm, cw_hm, beta, gc, norm_g, hp=8):
    bn, s, _ = proj_hm.shape
    n = s // A_CHUNK
    nsb = s // A_SEQ_BLK
    halo = _gdn_halo(proj_hm)

    def body(x_ref, halo_ref, cw_ref, beta_ref, gc_ref, ng_ref, og_ref, oraw_ref, st_ref, t_ref, y_ref, state):
        first_chunk = pl.program_id(2) * A_BLK_CHUNKS
        ri = lax.broadcasted_iota(jnp.int32, (A_CHUNK, A_CHUNK), 0)
        ci_ = lax.broadcasted_iota(jnp.int32, (A_CHUNK, A_CHUNK), 1)
        ng = ng_ref[...]

        @pl.when(pl.program_id(2) == 0)
        def _():
            state[...] = jnp.zeros_like(state)

        def one_head(hh, ci, first, rows):
            lo = hh * A_HEAD_COLS
            cw = cw_ref[:, hh * A_CONV_COLS:(hh + 1) * A_CONV_COLS]
            cin = _gdn_chunk_prep(_gdn_window(x_ref, halo_ref, ci, first, lo), cw)
            y_ref[rows, hh * A_CONV_COLS:(hh + 1) * A_CONV_COLS] = cin["y"]
            seq_chunk = pl.ds(first_chunk + ci, 1)
            core = yield from _gdn_chunk_core(cin["q"], cin["k"], cin["v"], gc_ref[hh, seq_chunk, :],
                                              beta_ref[hh, seq_chunk, :], None, ri, ci_)
            st = state[hh]
            st_ref[hh, ci] = st
            t_ref[hh, ci] = core["t_mat"]
            st16 = st.astype(BF16)
            vn = core["u"] - _dot(core["w"], st16, 1, 0)
            qs = _dot(core["qg"], st16, 1, 0)
            yield
            vn16 = vn.astype(BF16)
            o = qs + _dot(core["p"], vn16, 1, 0)
            state[hh] = st * core["gam_last"] + _dot(core["kd"], vn16, 0, 0)
            yield
            ocols = slice(hh * A_DV, (hh + 1) * A_DV)
            oraw_ref[rows, ocols] = o
            r = lax.rsqrt(jnp.mean(o * o, axis=1, keepdims=True) + EPS)
            z = x_ref[rows, lo + A_CONV_COLS:lo + A_HEAD_COLS]
            og_ref[rows, ocols] = (o * r * ng * _silu(z)).astype(BF16)

        def chunk(ci, first):
            rows = pl.ds(0 if first else pl.multiple_of(ci * A_CHUNK, A_CHUNK), A_CHUNK)
            _round_robin([one_head(hh, ci, first, rows) for hh in range(hp)])

        chunk(0, True)
        lax.fori_loop(1, A_BLK_CHUNKS, lambda i, c: (chunk(i, False), c)[1], 0)

    small = pl.BlockSpec((None, hp, n, A_CHUNK), lambda b, h, j: (b, h, 0, 0))
    return pl.pallas_call(
        body, name="gdn_fwd", grid=(bn, A_HEADS // hp, nsb),
        in_specs=[pl.BlockSpec((None, A_SEQ_BLK, hp * A_HEAD_COLS), lambda b, h, j: (b, j, h)),
                  pl.BlockSpec((None, None, HALO, hp * A_HEAD_COLS), lambda b, h, j: (b, j, 0, h)),
                  pl.BlockSpec((A_CONV, hp * A_CONV_COLS), lambda b, h, j: (0, h)),
                  small, small,
                  pl.BlockSpec((1, A_DV), lambda b, h, j: (0, 0))],
        out_specs=[pl.BlockSpec((None, A_SEQ_BLK, hp * A_DV), lambda b, h, j: (b, j, h)),
                   pl.BlockSpec((None, A_SEQ_BLK, hp * A_DV), lambda b, h, j: (b, j, h)),
                   pl.BlockSpec((None, hp, A_BLK_CHUNKS, A_DK, A_DV), lambda b, h, j: (b, h, j, 0, 0)),
                   pl.BlockSpec((None, hp, A_BLK_CHUNKS, A_CHUNK, A_CHUNK), lambda b, h, j: (b, h, j, 0, 0)),
                   pl.BlockSpec((None, A_SEQ_BLK, hp * A_CONV_COLS), lambda b, h, j: (b, j, h))],
        out_shape=[jax.ShapeDtypeStruct((bn, s, A_VW), BF16),
                   jax.ShapeDtypeStruct((bn, s, A_VW), F32),
                   jax.ShapeDtypeStruct((bn, A_HEADS, n, A_DK, A_DV), F32),
                   jax.ShapeDtypeStruct((bn, A_HEADS, n, A_CHUNK, A_CHUNK), F32),
                   jax.ShapeDtypeStruct((bn, s, A_HEADS * A_CONV_COLS), F32)],
        scratch_shapes=[pltpu.VMEM((hp, A_DK, A_DV), F32)],
        compiler_params=_params(("parallel", "parallel", "arbitrary"), VMEM_BIG),
    )(proj_hm, halo, cw_hm, beta, gc, norm_g)


def _gdn_bwd(proj_hm, cw_hm, beta, gc, norm_g, oraw, states, t_mats, conv_y, dog, hp=4):
    bn, s, _ = proj_hm.shape
    n = s // A_CHUNK
    nsb = s // A_SEQ_BLK
    halo = _gdn_halo(proj_hm)

    def body(x_ref, halo_ref, cw_ref, beta_ref, gc_ref, ng_ref, oraw_ref, st_ref, t_ref, y_ref, dog_ref,
             dx_ref, dgc_ref, dbeta_ref, dcw_ref, dng_ref, dstate, dy_next, shifted):
        first_chunk = (nsb - 1 - pl.program_id(2)) * A_BLK_CHUNKS
        ri = lax.broadcasted_iota(jnp.int32, (A_CHUNK, A_CHUNK), 0)
        ci_ = lax.broadcasted_iota(jnp.int32, (A_CHUNK, A_CHUNK), 1)
        lane = lax.broadcasted_iota(jnp.int32, (1, A_CHUNK), 1)
        ng = ng_ref[...]

        @pl.when(pl.program_id(2) == 0)
        def _():
            dstate[...] = jnp.zeros_like(dstate)
            dy_next[...] = jnp.zeros_like(dy_next)
            dcw_ref[...] = jnp.zeros_like(dcw_ref)
            dng_ref[...] = jnp.zeros_like(dng_ref)

        def one_head(hh, ci, first, rows):
            lo = hh * A_HEAD_COLS
            ccols = slice(hh * A_CONV_COLS, (hh + 1) * A_CONV_COLS)
            ocols = slice(hh * A_DV, (hh + 1) * A_DV)
            cw = cw_ref[:, ccols]
            cin = _gdn_chunk_prep(_gdn_window(x_ref, halo_ref, ci, first, lo), cw, y_ref[rows, ccols])
            q, k, v = cin["q"], cin["k"], cin["v"]
            seq_chunk = pl.ds(first_chunk + ci, 1)
            cr = yield from _gdn_chunk_core(q, k, v, gc_ref[hh, seq_chunk, :], beta_ref[hh, seq_chunk, :],
                                            t_ref[hh, ci], ri, ci_)
            eye, dec, gam, e = cr["eye"], cr["dec"], cr["gam"], cr["e"]
            b_col, t_mat, u, w, p = cr["b_col"], cr["t_mat"], cr["u"], cr["w"], cr["p"]
            st = st_ref[hh, ci]
            ds_out = dstate[hh]

            o = oraw_ref[rows, ocols]
            z = x_ref[rows, lo + A_CONV_COLS:lo + A_HEAD_COLS]
            d_og = dog_ref[rows, ocols].astype(F32)
            r = lax.rsqrt(jnp.mean(o * o, axis=1, keepdims=True) + EPS)
            oh = o * r
            gate, gate_slope = _silu_and_slope(z)
            d_on = d_og * gate
            dz = d_og * oh * ng * gate_slope
            dng_ref[hh, 0:1, :] += jnp.sum(d_on * oh, axis=0, keepdims=True)
            d_oh = d_on * ng
            d_o = r * (d_oh - oh * jnp.mean(d_oh * oh, axis=1, keepdims=True))

            st16, ds16, do16, w16 = st.astype(BF16), ds_out.astype(BF16), d_o.astype(BF16), w.astype(BF16)
            q16, k16, t16 = cr["q16"], cr["k16"], cr["t16"]
            vn = u - _dot(w16, st16, 1, 0)
            d_vn = _dot(p, do16, 0, 0) + _dot(cr["kd"], ds16, 1, 0)
            d_qg = _dot(do16, st16, 1, 1)
            qgdo = _dot(cr["qg"], do16, 0, 0)
            yield
            vn16, dvn16 = vn.astype(BF16), d_vn.astype(BF16)
            d_p = jnp.where(cr["causal"], _dot(do16, vn16, 1, 1), 0.0)
            d_kd = _dot(vn16, ds16, 1, 1)
            d_gam_last = jnp.sum(jnp.sum(st * ds_out, axis=1, keepdims=True), axis=0, keepdims=True)
            d_w = -_dot(dvn16, st16, 1, 1)
            dstate[hh] = qgdo + ds_out * cr["gam_last"] - _dot(w16, dvn16, 0, 0)
            d_bv = _dot(t16, dvn16, 0, 0)
            yield
            d_kbg = _dot(t16, d_w, 0, 0)
            n_p = (d_p * dec).astype(BF16)
            d_q = _dot(n_p, k16, 1, 0) + d_qg * gam
            npq = _dot(n_p, q16, 0, 0)
            yield
            d_a = jnp.where(cr["strict"], -(_dot(d_bv, u, 1, 1) + _dot(d_kbg, w16, 1, 1)), 0.0)
            yield
            m_a = (d_a * dec).astype(BF16)
            d_kb = _dot(m_a, k16, 1, 0) + d_kbg * gam
            d_k = (_dot(m_a, cr["kb16"], 0, 0) + npq + d_kd * e + d_kb * b_col)
            yield
            d_v = d_bv * b_col
            d_beta_col = (jnp.sum(d_bv * v, axis=1, keepdims=True)
                          + jnp.sum(d_kb * k, axis=1, keepdims=True))
            gterm = d_a * cr["a_mat"] + d_p * p
            d_e = jnp.sum(d_kd * k, axis=1, keepdims=True) * e
            d_g_col = (jnp.sum(gterm, axis=1, keepdims=True)
                       + (jnp.sum(d_qg * q, axis=1, keepdims=True)
                          + jnp.sum(d_kbg * cr["kb"], axis=1, keepdims=True)) * gam
                       - d_e)
            d_g_last = jnp.sum(d_e, axis=0, keepdims=True) + d_gam_last * cr["gam_last"]
            d_g_row = (_col_to_row(d_g_col, eye) - jnp.sum(gterm, axis=0, keepdims=True)
                       + jnp.where(lane == A_CHUNK - 1, d_g_last, 0.0))
            dgc_ref[hh, seq_chunk, :] = d_g_row
            dbeta_ref[hh, seq_chunk, :] = _col_to_row(d_beta_col, eye)

            qh = cin["aq"] * cin["rq"]
            kh = cin["ak"] * cin["rk"]
            d_qh = d_q * (A_DK ** -0.5)
            d_aq = cin["rq"] * (d_qh - qh * jnp.sum(d_qh * qh, axis=1, keepdims=True))
            d_ak = cin["rk"] * (d_k - kh * jnp.sum(d_k * kh, axis=1, keepdims=True))
            d_y = jnp.concatenate([d_aq, d_ak, d_v], axis=1) * cin["slope"]
            shifted[hh, 0, 0:A_CHUNK, :] = d_y
            shifted[hh, 0, A_CHUNK:A_CHUNK + HALO, :] = dy_next[hh]
            shifted[hh, 1, 0:A_CHUNK + HALO, :] = cin["xw"]
            d_x = cw[A_CONV - 1:A_CONV, :] * d_y
            for j in range(1, A_CONV):
                d_x = d_x + cw[A_CONV - 1 - j:A_CONV - j, :] * shifted[hh, 0, j:j + A_CHUNK, :]
            for j in range(A_CONV):
                xs = shifted[hh, 1, HALO - j:HALO - j + A_CHUNK, :]
                dcw_ref[A_CONV - 1 - j:A_CONV - j, ccols] += jnp.sum(d_y * xs, axis=0, keepdims=True)
            dy_next[hh] = d_y[0:HALO, :]
            dx_ref[rows, lo:lo + A_CONV_COLS] = d_x.astype(BF16)
            dx_ref[rows, lo + A_CONV_COLS:lo + A_HEAD_COLS] = dz.astype(BF16)

        def chunk(ci, first):
            rows = pl.ds(0 if first else pl.multiple_of(ci * A_CHUNK, A_CHUNK), A_CHUNK)
            _round_robin([one_head(hh, ci, first, rows) for hh in range(hp)])

        lax.fori_loop(0, A_BLK_CHUNKS - 1, lambda i, c: (chunk(A_BLK_CHUNKS - 1 - i, False), c)[1], 0)
        chunk(0, True)

    rev = lambda j: nsb - 1 - j
    small = pl.BlockSpec((None, hp, n, A_CHUNK), lambda b, h, j: (b, h, 0, 0))
    wide = pl.BlockSpec((None, A_SEQ_BLK, hp * A_HEAD_COLS), lambda b, h, j: (b, rev(j), h))
    val = pl.BlockSpec((None, A_SEQ_BLK, hp * A_DV), lambda b, h, j: (b, rev(j), h))
    return pl.pallas_call(
        body, name="gdn_bwd", grid=(bn, A_HEADS // hp, nsb),
        in_specs=[wide,
                  pl.BlockSpec((None, None, HALO, hp * A_HEAD_COLS), lambda b, h, j: (b, rev(j), 0, h)),
                  pl.BlockSpec((A_CONV, hp * A_CONV_COLS), lambda b, h, j: (0, h)),
                  small, small,
                  pl.BlockSpec((1, A_DV), lambda b, h, j: (0, 0)),
                  val,
                  pl.BlockSpec((None, hp, A_BLK_CHUNKS, A_DK, A_DV), lambda b, h, j: (b, h, rev(j), 0, 0)),
                  pl.BlockSpec((None, hp, A_BLK_CHUNKS, A_CHUNK, A_CHUNK), lambda b, h, j: (b, h, rev(j), 0, 0)),
                  pl.BlockSpec((None, A_SEQ_BLK, hp * A_CONV_COLS), lambda b, h, j: (b, rev(j), h)),
                  val],
        out_specs=[wide, small, small,
                   pl.BlockSpec((None, A_CONV, hp * A_CONV_COLS), lambda b, h, j: (b, 0, h)),
                   pl.BlockSpec((None, hp, 8, A_DV), lambda b, h, j: (b, h, 0, 0))],
        out_shape=[jax.ShapeDtypeStruct((bn, s, A_HEADS * A_HEAD_COLS), BF16),
                   jax.ShapeDtypeStruct((bn, A_HEADS, n, A_CHUNK), F32),
                   jax.ShapeDtypeStruct((bn, A_HEADS, n, A_CHUNK), F32),
                   jax.ShapeDtypeStruct((bn, A_CONV, A_HEADS * A_CONV_COLS), F32),
                   jax.ShapeDtypeStruct((bn, A_HEADS, 8, A_DV), F32)],
        scratch_shapes=[pltpu.VMEM((hp, A_DK, A_DV), F32), pltpu.VMEM((hp, HALO, A_CONV_COLS), F32),
                        pltpu.VMEM((hp, 2, A_CHUNK + 2 * HALO, A_CONV_COLS), F32)],
        compiler_params=_params(("parallel", "parallel", "arbitrary"), VMEM_BIG),
    )(proj_hm, halo, cw_hm, beta, gc, norm_g, oraw, states, t_mats, conv_y, dog)


def _rope_tables(posf, inv_freq_row):
    t = posf.shape[0]
    tm = 512

    def body(p_ref, f_ref, c_ref, sa_ref, sb_ref):
        ang = p_ref[...] * f_ref[...]
        lane = lax.broadcasted_iota(jnp.int32, ang.shape, 1)
        half = ROPE_DIMS // 2
        c_ref[...] = jnp.where(lane < ROPE_DIMS, jnp.cos(ang), 1.0)
        sn = jnp.sin(ang)
        sa_ref[...] = jnp.where(lane < half, -sn, 0.0)
        sb_ref[...] = jnp.where((lane >= half) & (lane < ROPE_DIMS), sn, 0.0)

    row = pl.BlockSpec((tm, 128), lambda i: (i, 0))
    return pl.pallas_call(
        body, name="rope_tables", grid=(t // tm,),
        in_specs=[row, pl.BlockSpec((1, 128), lambda i: (0, 0))], out_specs=[row] * 3,
        out_shape=[jax.ShapeDtypeStruct((t, 128), F32)] * 3,
        compiler_params=_params(("parallel",)),
    )(posf, inv_freq_row)


def _qk_prep(proj, c, sa, sb, qg, kg, name, tm=512):
    t = proj.shape[0]

    def body(x_ref, c_ref, sa_ref, sb_ref, qg_ref, kg_ref, o_ref):
        cc, s1, s2 = c_ref[...], sa_ref[...], sb_ref[...]
        half = ROPE_DIMS // 2

        def one_head(lo, g):
            xv = x_ref[:, lo:lo + B_DH].astype(F32)
            ms = jnp.mean(xv * xv, axis=1, keepdims=True)
            yield
            xn = xv * lax.rsqrt(ms + EPS) * g
            r1, r2 = pltpu.roll(xn, 128 - half, 1), pltpu.roll(xn, half, 1)
            yield
            o_ref[:, lo:lo + B_DH] = (xn * cc + r1 * s1 + r2 * s2).astype(BF16)

        for which, g_ref in ((0, qg_ref), (1, kg_ref)):
            g = g_ref[...]
            _round_robin([one_head(which * B_W + h * B_DH, g) for h in range(B_HEADS)])
        o_ref[:, 2 * B_W:3 * B_W] = x_ref[:, 2 * B_W:3 * B_W]

    tab = pl.BlockSpec((tm, 128), lambda i: (i, 0))
    gain = pl.BlockSpec((1, B_DH), lambda i: (0, 0))
    return pl.pallas_call(
        body, name=name, grid=(t // tm,),
        in_specs=[pl.BlockSpec((tm, 3 * B_W), lambda i: (i, 0)), tab, tab, tab, gain, gain],
        out_specs=pl.BlockSpec((tm, 3 * B_W), lambda i: (i, 0)),
        out_shape=jax.ShapeDtypeStruct((t, 3 * B_W), BF16),
        compiler_params=_params(("parallel",), 40 * 1024 * 1024),
    )(proj, c, sa, sb, qg, kg)


def _qk_prep_bwd(proj, c, sa, sb, qg, kg, dq, dk, dv, dz, name, tm=512):
    t = proj.shape[0]
    out_w = 3 * B_W + (B_W if dz is not None else 0)

    def body(*refs):
        x_ref, c_ref, sa_ref, sb_ref, qg_ref, kg_ref, dq_ref, dk_ref, dv_ref = refs[:9]
        if dz is not None:
            dz_ref, o_ref, dgain_ref = refs[9:]
        else:
            o_ref, dgain_ref = refs[9:]
        i = pl.program_id(0)

        @pl.when(i == 0)
        def _():
            dgain_ref[...] = jnp.zeros_like(dgain_ref)

        cc, s1, s2 = c_ref[...], sa_ref[...], sb_ref[...]
        half = ROPE_DIMS // 2

        def one_head(which, h, g, d_ref, parts):
            lo = which * B_W + h * B_DH
            xv = x_ref[:, lo:lo + B_DH].astype(F32)
            d_out = d_ref[:, h * B_DH:(h + 1) * B_DH].astype(F32)
            ms = jnp.mean(xv * xv, axis=1, keepdims=True)
            r1, r2 = pltpu.roll(d_out * s1, half, 1), pltpu.roll(d_out * s2, 128 - half, 1)
            yield
            r = lax.rsqrt(ms + EPS)
            xh = xv * r
            d_xn = d_out * cc + r1 + r2
            parts.append(jnp.sum(d_xn * xh, axis=0, keepdims=True))
            d_xh = d_xn * g
            dot = jnp.mean(d_xh * xh, axis=1, keepdims=True)
            yield
            o_ref[:, lo:lo + B_DH] = (r * (d_xh - xh * dot)).astype(BF16)

        for which, g_ref, d_ref in ((0, qg_ref, dq_ref), (1, kg_ref, dk_ref)):
            parts = []
            _round_robin([one_head(which, h, g_ref[...], d_ref, parts) for h in range(B_HEADS)])
            acc = parts[0]
            for part in parts[1:]:
                acc = acc + part
            dgain_ref[which:which + 1, :] += acc
        o_ref[:, 2 * B_W:3 * B_W] = dv_ref[...]
        if dz is not None:
            o_ref[:, 3 * B_W:4 * B_W] = dz_ref[...]

    tab = pl.BlockSpec((tm, 128), lambda i: (i, 0))
    gain = pl.BlockSpec((1, B_DH), lambda i: (0, 0))
    grad = pl.BlockSpec((tm, B_W), lambda i: (i, 0))
    in_specs = [pl.BlockSpec((tm, 2 * B_W), lambda i: (i, 0)), tab, tab, tab, gain, gain, grad, grad, grad]
    args = [proj, c, sa, sb, qg, kg, dq, dk, dv]
    if dz is not None:
        in_specs.append(grad)
        args.append(dz)
    return pl.pallas_call(
        body, name=name, grid=(t // tm,), in_specs=in_specs,
        out_specs=[pl.BlockSpec((tm, out_w), lambda i: (i, 0)), pl.BlockSpec((8, B_DH), lambda i: (0, 0))],
        out_shape=[jax.ShapeDtypeStruct((t, out_w), BF16), jax.ShapeDtypeStruct((8, B_DH), F32)],
        compiler_params=_params(("arbitrary",), 40 * 1024 * 1024),
    )(*args)


def _attn_masks():
    qi = lax.broadcasted_iota(jnp.int32, (B_BLK, 2 * B_BLK), 0)
    kj = lax.broadcasted_iota(jnp.int32, (B_BLK, 2 * B_BLK), 1)
    two = (kj >= qi) & (kj <= qi + B_BLK)
    q1 = lax.broadcasted_iota(jnp.int32, (B_BLK, B_BLK), 0)
    k1 = lax.broadcasted_iota(jnp.int32, (B_BLK, B_BLK), 1)
    return k1 <= q1, two


def _lane_pick(ref_rows, h):
    lane = lax.broadcasted_iota(jnp.int32, ref_rows.shape, 1)
    return jnp.sum(jnp.where(lane == h, ref_rows, 0.0), axis=1, keepdims=True)


B_ROWS = 2048


def _attn_schedule(nb, sb, block):
    way = 16

    def run(items):
        for at in range(0, len(items), way):
            _round_robin([block(*it) for it in items[at:at + way]])

    run([(si, 0, True) for si in range(sb)])
    if nb == 1:
        return
    per = max(1, way // sb)
    lead = 1 + (nb - 1) % per
    if lead > 1:
        run([(si, i, False) for i in range(1, lead) for si in range(sb)])

    def step(it, carry):
        run([(si, lead + it * per + u, False) for u in range(per) for si in range(sb)])
        return carry

    lax.fori_loop(0, (nb - lead) // per, step, 0)


def _attn_rows(i, first):
    if first:
        return pl.ds(0, B_BLK), pl.ds(0, B_BLK)
    rows = pl.ds(pl.multiple_of(i * B_BLK, B_BLK), B_BLK)
    return rows, pl.ds(pl.multiple_of((i - 1) * B_BLK, B_BLK), 2 * B_BLK)


def _attn_fwd(qkv, name):
    ns, ln, _ = qkv.shape
    nb = ln // B_BLK
    sb = B_ROWS // ln
    scale = B_DH ** -0.5

    def body(q_ref, k_ref, v_ref, o_ref, lse_ref):
        h = pl.program_id(1)
        mask1, mask2 = _attn_masks()
        lane = lax.broadcasted_iota(jnp.int32, (B_BLK, B_HEADS), 1)

        @pl.when(h == 0)
        def _():
            lse_ref[...] = jnp.zeros_like(lse_ref)

        def block(si, i, first):
            rows, win = _attn_rows(i, first)
            mask = mask1 if first else mask2
            sc = jnp.where(mask, _dot(q_ref[si, rows, :], k_ref[si, win, :], 1, 1) * scale, -1e30)
            yield
            m = jnp.max(sc, axis=1, keepdims=True)
            p = jnp.exp(sc - m)
            l = jnp.sum(p, axis=1, keepdims=True)
            pv = _dot(p, v_ref[si, win, :], 1, 0)
            yield
            o_ref[si, rows, :] = (pv / l).astype(BF16)
            lse_ref[si, rows, :] = jnp.where(lane == h, m + jnp.log(l), lse_ref[si, rows, :])

        _attn_schedule(nb, sb, block)

    head = lambda off: pl.BlockSpec((sb, ln, B_DH), lambda s, h: (s, 0, off + h))
    return pl.pallas_call(
        body, name=name, grid=(ns // sb, B_HEADS),
        in_specs=[head(0), head(B_HEADS), head(2 * B_HEADS)],
        out_specs=[head(0), pl.BlockSpec((sb, ln, B_HEADS), lambda s, h: (s, 0, 0))],
        out_shape=[jax.ShapeDtypeStruct((ns, ln, B_W), BF16), jax.ShapeDtypeStruct((ns, ln, B_HEADS), F32)],
        compiler_params=_params(("parallel", "arbitrary")),
    )(qkv, qkv, qkv)


def _attn_bwd(qkv, d_o, lse_joint, delta, name):
    ns, ln, _ = qkv.shape
    nb = ln // B_BLK
    sb = B_ROWS // ln
    scale = B_DH ** -0.5

    def body(q_ref, k_ref, v_ref, do_ref, lj_ref, dl_ref, dq_ref, dk_out, dv_out, dk_ref, dv_ref):
        h = pl.program_id(1)
        mask1, mask2 = _attn_masks()
        dk_ref[...] = jnp.zeros_like(dk_ref)
        dv_ref[...] = jnp.zeros_like(dv_ref)

        def block(si, i, first):
            rows, win = _attn_rows(i, first)
            mask = mask1 if first else mask2
            q = q_ref[si, rows, :]
            d_out = do_ref[si, rows, :]
            l_col = _lane_pick(lj_ref[si, rows, :], h)
            d_col = _lane_pick(dl_ref[si, rows, :], h)
            sc = _dot(q, k_ref[si, win, :], 1, 1) * scale
            d_p = _dot(d_out, v_ref[si, win, :], 1, 1)
            yield
            p = jnp.exp(jnp.where(mask, sc - l_col, -1e30))
            d_s = p * (d_p - d_col) * scale
            d_q = _dot(d_s, k_ref[si, win, :], 1, 0)
            d_k = _dot(d_s, q, 0, 0)
            d_v = _dot(p, d_out, 0, 0)
            yield
            dq_ref[si, rows, :] = d_q.astype(BF16)
            dk_ref[si, win, :] += d_k
            dv_ref[si, win, :] += d_v

        _attn_schedule(nb, sb, block)
        dk_out[...] = dk_ref[...].astype(BF16)
        dv_out[...] = dv_ref[...].astype(BF16)

    head = lambda off: pl.BlockSpec((sb, ln, B_DH), lambda s, h: (s, 0, off + h))
    small = pl.BlockSpec((sb, ln, B_HEADS), lambda s, h: (s, 0, 0))
    return pl.pallas_call(
        body, name=name, grid=(ns // sb, B_HEADS),
        in_specs=[head(0), head(B_HEADS), head(2 * B_HEADS), head(0), small, small],
        out_specs=[head(0)] * 3,
        out_shape=[jax.ShapeDtypeStruct((ns, ln, B_W), BF16)] * 3,
        scratch_shapes=[pltpu.VMEM((sb, ln, B_DH), F32)] * 2,
        compiler_params=_params(("parallel", "parallel")),
    )(qkv, qkv, qkv, d_o, lse_joint, delta)


def _merge_weights(lse_refs):
    ls = [r[...] for r in lse_refs]
    m = jnp.maximum(jnp.maximum(ls[0], ls[1]), ls[2])
    es = [jnp.exp(l - m) for l in ls]
    tot = es[0] + es[1] + es[2]
    return [e / tot for e in es], m + jnp.log(tot)


def _merge_fwd(outs, lses, proj0, tm=512):
    t = outs[0].shape[0]

    def body(o0, o1, o2, l0, l1, l2, z_ref, og_ref):
        wts, _ = _merge_weights((l0, l1, l2))

        def one_head(h):
            cols = slice(h * B_DH, (h + 1) * B_DH)
            w0, w1, w2 = (jnp.broadcast_to(w[:, h:h + 1], (tm, B_DH)) for w in wts)
            yield
            o = w0 * o0[:, cols] + w1 * o1[:, cols] + w2 * o2[:, cols]
            og_ref[:, cols] = (o * _silu(z_ref[:, cols].astype(F32))).astype(BF16)

        _round_robin([one_head(h) for h in range(B_HEADS)])

    wide = pl.BlockSpec((tm, B_W), lambda i: (i, 0))
    small = pl.BlockSpec((tm, B_HEADS), lambda i: (i, 0))
    return pl.pallas_call(
        body, name="merge_fwd", grid=(t // tm,),
        in_specs=[wide] * 3 + [small] * 3 + [pl.BlockSpec((tm, B_W), lambda i: (i, 3))],
        out_specs=wide, out_shape=jax.ShapeDtypeStruct((t, B_W), BF16),
        compiler_params=_params(("parallel",)),
    )(*outs, *lses, proj0)


def _merge_bwd(outs, lses, proj0, d_og, tm=512):
    t = outs[0].shape[0]

    def body(o0, o1, o2, l0, l1, l2, z_ref, dog_ref, do_ref, lj_ref, dl_ref, dz_ref):
        wts, lj = _merge_weights((l0, l1, l2))
        lj_ref[...] = lj
        lane = lax.broadcasted_iota(jnp.int32, (tm, B_HEADS), 1)
        sums = [None] * B_HEADS

        def one_head(h):
            cols = slice(h * B_DH, (h + 1) * B_DH)
            w0, w1, w2 = (jnp.broadcast_to(w[:, h:h + 1], (tm, B_DH)) for w in wts)
            yield
            o = w0 * o0[:, cols] + w1 * o1[:, cols] + w2 * o2[:, cols]
            z = z_ref[:, cols].astype(F32)
            d_g = dog_ref[:, cols].astype(F32)
            gate, gate_slope = _silu_and_slope(z)
            d_out = d_g * gate
            dz_ref[:, cols] = (d_g * o * gate_slope).astype(BF16)
            do_ref[:, cols] = d_out.astype(BF16)
            sums[h] = jnp.sum(d_out * o, axis=1, keepdims=True)
            yield

        _round_robin([one_head(h) for h in range(B_HEADS)])
        delta = jnp.zeros((tm, B_HEADS), F32)
        for h in range(B_HEADS):
            delta = jnp.where(lane == h, sums[h], delta)
        dl_ref[...] = delta

    wide = pl.BlockSpec((tm, B_W), lambda i: (i, 0))
    small = pl.BlockSpec((tm, B_HEADS), lambda i: (i, 0))
    return pl.pallas_call(
        body, name="merge_bwd", grid=(t // tm,),
        in_specs=[wide] * 3 + [small] * 3 + [pl.BlockSpec((tm, B_W), lambda i: (i, 3)), wide],
        out_specs=[wide, small, small, wide],
        out_shape=[jax.ShapeDtypeStruct((t, B_W), BF16), jax.ShapeDtypeStruct((t, B_HEADS), F32),
                   jax.ShapeDtypeStruct((t, B_HEADS), F32), jax.ShapeDtypeStruct((t, B_W), BF16)],
        compiler_params=_params(("parallel",)),
    )(*outs, *lses, proj0, d_og)


def _adamw(w, g, m, v, name):
    r, c = w.shape
    tr = r
    for cand in (256, 128, 64, 32, 16, 8):
        if r % cand == 0:
            tr = cand
            break

    def body(w_ref, g_ref, m_ref, v_ref, d_ref, nm_ref, nv_ref):
        gv = g_ref[...]
        nm = ADAM_B1 * m_ref[...] + (1.0 - ADAM_B1) * gv
        nv = ADAM_B2 * v_ref[...] + (1.0 - ADAM_B2) * (gv * gv)
        m_hat = nm / (1.0 - ADAM_B1 ** ADAM_STEP)
        v_hat = nv / (1.0 - ADAM_B2 ** ADAM_STEP)
        d_ref[...] = -ADAM_LR * (m_hat / (jnp.sqrt(v_hat) + ADAM_EPS) + ADAM_WD * w_ref[...])
        nm_ref[...] = nm
        nv_ref[...] = nv

    blk = pl.BlockSpec((tr, c), lambda i: (i, 0))
    return pl.pallas_call(
        body, name=name, grid=(r // tr,), in_specs=[blk] * 4, out_specs=[blk] * 3,
        out_shape=[jax.ShapeDtypeStruct((r, c), F32)] * 3,
        compiler_params=_params(("parallel",)),
    )(w, g, m, v)


def _adam_update(w, gv, m, v):
    nm = ADAM_B1 * m + (1.0 - ADAM_B1) * gv
    nv = ADAM_B2 * v + (1.0 - ADAM_B2) * (gv * gv)
    m_hat = nm / (1.0 - ADAM_B1 ** ADAM_STEP)
    v_hat = nv / (1.0 - ADAM_B2 ** ADAM_STEP)
    return -ADAM_LR * (m_hat / (jnp.sqrt(v_hat) + ADAM_EPS) + ADAM_WD * w), nm, nv


def _adamw_shard(w, mine, theirs, m, v, half_index, name, tr=128):
    _, r, c = w.shape
    nhb = (r // 2) // tr

    def body(c_ref, w_ref, mine_ref, theirs_ref, m_ref, v_ref, g_ref, d_ref, nm_ref, nv_ref):
        is_mine = (pl.program_id(0) // nhb) == c_ref[0]
        gv = jnp.where(is_mine, mine_ref[...], theirs_ref[...])
        d, nm, nv = _adam_update(w_ref[...], gv, m_ref[...], v_ref[...])
        g_ref[...] = gv
        d_ref[...] = d
        nm_ref[...] = nm
        nv_ref[...] = nv

    full = pl.BlockSpec((None, tr, c), lambda i, cc: (0, i, 0))
    half = pl.BlockSpec((tr, c), lambda i, cc: (i % nhb, 0))
    return pl.pallas_call(
        body, name=name,
        grid_spec=pltpu.PrefetchScalarGridSpec(
            num_scalar_prefetch=1, grid=(2 * nhb,),
            in_specs=[full, half, half, full, full], out_specs=[full] * 4),
        out_shape=[jax.ShapeDtypeStruct(w.shape, F32)] * 4,
        compiler_params=_params(("parallel",), 40 * 1024 * 1024),
    )(half_index, w, mine, theirs, m, v)


def _adamw_shard_cols(w, mine, theirs, m, v, half_index, name, steps=20):
    c, _, r = w.shape
    tc = c // steps
    assert tc * steps == c

    def body(c_ref, w_ref, mine_ref, theirs_ref, m_ref, v_ref, g_ref, d_ref, nm_ref, nv_ref):
        first = jnp.where(c_ref[0] == 0, mine_ref[...], theirs_ref[...])
        second = jnp.where(c_ref[0] == 0, theirs_ref[...], mine_ref[...])
        for lo, gv in ((0, first), (r // 2, second)):
            cols = slice(lo, lo + r // 2)
            d, nm, nv = _adam_update(w_ref[:, :, cols], gv, m_ref[:, :, cols], v_ref[:, :, cols])
            g_ref[:, :, cols] = gv
            d_ref[:, :, cols] = d
            nm_ref[:, :, cols] = nm
            nv_ref[:, :, cols] = nv

    full = pl.BlockSpec((tc, 1, r), lambda i, cc: (i, 0, 0))
    half = pl.BlockSpec((tc, 1, r // 2), lambda i, cc: (i, 0, 0))
    return pl.pallas_call(
        body, name=name,
        grid_spec=pltpu.PrefetchScalarGridSpec(
            num_scalar_prefetch=1, grid=(steps,),
            in_specs=[full, half, half, full, full], out_specs=[full] * 4),
        out_shape=[jax.ShapeDtypeStruct(w.shape, F32)] * 4,
        compiler_params=_params(("parallel",), 40 * 1024 * 1024),
    )(half_index, w, mine, theirs, m, v)


def _pair_sum(own, other, half_index, name, tr=256):
    _, r, c = own.shape
    rh = r // 2
    tr = min(tr, rh)
    nrb = rh // tr

    def body(c_ref, own_ref, oth_ref, out_ref):
        out_ref[...] = (own_ref[...] + oth_ref[...].astype(F32)).astype(BF16)

    return pl.pallas_call(
        body, name=name,
        grid_spec=pltpu.PrefetchScalarGridSpec(
            num_scalar_prefetch=1, grid=(N_CHIPS, nrb),
            in_specs=[pl.BlockSpec((None, tr, c), lambda k, i, cc: (k, cc[0] * nrb + i, 0)),
                      pl.BlockSpec((None, tr, c), lambda k, i, cc: (k, i, 0))],
            out_specs=pl.BlockSpec((None, tr, c), lambda k, i, cc: (k, i, 0))),
        out_shape=jax.ShapeDtypeStruct((N_CHIPS, rh, c), BF16),
        compiler_params=_params(("parallel", "parallel")),
    )(half_index, own, other)


def _chip_sum(sums, others, chip_index, name, tr=256):
    _, r, c = sums.shape
    tr = min(tr, r)

    def body(k_ref, own_ref, oth_ref, out_ref):
        acc = own_ref[...].astype(F32)
        for j in range(N_CHIPS - 1):
            acc = acc + oth_ref[j].astype(F32)
        out_ref[...] = acc

    return pl.pallas_call(
        body, name=name,
        grid_spec=pltpu.PrefetchScalarGridSpec(
            num_scalar_prefetch=1, grid=(r // tr,),
            in_specs=[pl.BlockSpec((None, tr, c), lambda i, kk: (kk[0], i, 0)),
                      pl.BlockSpec((N_CHIPS - 1, tr, c), lambda i, kk: (0, i, 0))],
            out_specs=pl.BlockSpec((tr, c), lambda i, kk: (i, 0))),
        out_shape=jax.ShapeDtypeStruct((r, c), F32),
        compiler_params=_params(("parallel",)),
    )(chip_index, sums, others)


HBM = pl.BlockSpec(memory_space=pltpu.HBM)


def _place():
    x, y, c = lax.axis_index("x"), lax.axis_index("y"), lax.axis_index("c")
    chips = [(1 - x, y), (x, 1 - y), (1 - x, 1 - y)]
    return x, y, c, chips


def _sibling_forward(land):
    def body(in_ref, out_ref, send, recv):
        x, y, c, chips = _place()
        rh = out_ref.shape[1] // 2
        cps = []
        for j, (px, py) in enumerate(chips):
            slot = out_ref.at[2 * px + py, pl.ds(c * rh, rh)]
            cp = pltpu.make_async_remote_copy(
                src_ref=slot, dst_ref=slot, send_sem=send.at[j], recv_sem=recv.at[j],
                device_id=(x, y, 1 - c), device_id_type=MESH)
            cp.start()
            cps.append(cp)
        for j, (px, py) in enumerate(chips):
            slot = out_ref.at[2 * px + py, pl.ds((1 - c) * rh, rh)]
            pltpu.make_async_remote_copy(
                src_ref=slot, dst_ref=slot, send_sem=send.at[j], recv_sem=recv.at[j],
                device_id=(x, y, 1 - c), device_id_type=MESH).wait_recv()
        for cp in cps:
            cp.wait_send()

    return pl.pallas_call(
        body, name="first_weights_sibling_forward", in_specs=[HBM], out_specs=HBM,
        out_shape=jax.ShapeDtypeStruct(land.shape, land.dtype), input_output_aliases={0: 0},
        scratch_shapes=[pltpu.SemaphoreType.DMA((3,)), pltpu.SemaphoreType.DMA((3,))],
    )(land)


def _sibling_swap(halves, name):
    na = len(halves)

    def body(*refs):
        ins, outs = refs[:na], refs[na:2 * na]
        send, recv = refs[2 * na:]
        x, y, c, _ = _place()
        cps = []
        for i in range(na):
            cp = pltpu.make_async_remote_copy(
                src_ref=ins[i], dst_ref=outs[i], send_sem=send.at[i], recv_sem=recv.at[i],
                device_id=(x, y, 1 - c), device_id_type=MESH)
            cp.start()
            cps.append(cp)
        for cp in cps:
            cp.wait()

    out_shape = [jax.ShapeDtypeStruct(h.shape, h.dtype) for h in halves]
    return pl.pallas_call(
        body, name=name, in_specs=[HBM] * na, out_specs=[HBM] * na, out_shape=out_shape,
        scratch_shapes=[pltpu.SemaphoreType.DMA((na,)), pltpu.SemaphoreType.DMA((na,))],
    )(*halves)


SEM = pl.BlockSpec(memory_space=pltpu.SEMAPHORE)
ANY = pl.BlockSpec(memory_space=pl.ANY)
EFFECT = pltpu.SideEffectType.DATAFLOW_SIDE_EFFECTING


def _split_copy_start(name, plan, srcs, lands, after):
    ns, nl = len(srcs), len(lands)

    def body(*refs):
        src_refs, land_refs = refs[:ns], refs[ns:ns + nl]
        send, recv = refs[ns + nl + 1], refs[ns + nl + 2]
        token = refs[-1]
        outgoing, _ = plan(src_refs, land_refs)
        for src, dst, dev, si, ri in outgoing:
            pltpu.make_async_remote_copy(src_ref=src, dst_ref=dst, send_sem=send.at[si], recv_sem=recv.at[ri],
                                         device_id=dev, device_id_type=MESH).start()
        token[...] = jnp.zeros_like(token)

    n_out, n_in = plan.counts
    thru = [pltpu.HBM(a.shape, a.dtype) for a in list(srcs) + list(lands)]
    res = pl.pallas_call(
        body, name=name,
        out_shape=[pltpu.SemaphoreType.DMA((n_out,)), pltpu.SemaphoreType.DMA((n_in,))] + thru
        + [jax.ShapeDtypeStruct((8, 128), F32)],
        in_specs=[HBM] * (ns + nl) + [ANY],
        out_specs=[SEM, SEM] + [HBM] * (ns + nl) + [pl.BlockSpec(memory_space=pltpu.VMEM)],
        input_output_aliases={i: 2 + i for i in range(ns + nl)},
        compiler_params=pltpu.CompilerParams(has_side_effects=EFFECT),
    )(*[pltpu.with_memory_space_constraint(a, pltpu.HBM) for a in list(srcs) + list(lands)], after)
    return res[0], res[1], res[2:2 + ns], res[2 + ns:2 + ns + nl], res[-1]


def _split_copy_wait(name, plan, send, recv, srcs, lands, after):
    ns, nl = len(srcs), len(lands)
    after = list(after) if isinstance(after, (list, tuple)) else [after]

    def body(*refs):
        src_refs, land_refs = refs[:ns], refs[ns:ns + nl]
        send_ref, recv_ref = refs[ns + nl], refs[ns + nl + 1]
        outgoing, arrivals = plan(src_refs, land_refs)
        for src, dst, dev, si, ri in outgoing:
            pltpu.make_async_remote_copy(src_ref=src, dst_ref=dst, send_sem=send_ref.at[si], recv_sem=recv_ref.at[ri],
                                         device_id=dev, device_id_type=MESH).wait_send()
        for view, ri in arrivals:
            pltpu.make_async_remote_copy(src_ref=view, dst_ref=view, send_sem=send_ref.at[0], recv_sem=recv_ref.at[ri],
                                         device_id=_place()[:3], device_id_type=MESH).wait_recv()

    thru = [pltpu.HBM(a.shape, a.dtype) for a in list(srcs) + list(lands)]
    res = pl.pallas_call(
        body, name=name, out_shape=thru,
        in_specs=[HBM] * (ns + nl) + [SEM, SEM] + [ANY] * len(after), out_specs=[HBM] * (ns + nl),
        input_output_aliases={i: i for i in range(ns + nl)},
        compiler_params=pltpu.CompilerParams(has_side_effects=EFFECT),
    )(*srcs, *lands, send, recv, *after)
    return res[:ns], res[ns:]


def _gather_plan(n_arrays):
    def plan(src_refs, land_refs):
        x, y, c, chips = _place()
        me = 2 * x + y
        outgoing, arrivals = [], []
        for i in range(n_arrays):
            rh = src_refs[i].shape[0] // 2
            mine = pl.ds(c * rh, rh)
            for j, (px, py) in enumerate(chips):
                for delta in range(2):
                    tc = c ^ delta
                    outgoing.append((src_refs[i].at[mine], land_refs[i].at[me, mine], (px, py, tc),
                                     6 * i + 2 * j + delta, 6 * i + 2 * j + delta))
                    theirs = pl.ds(tc * rh, rh)
                    arrivals.append((land_refs[i].at[2 * px + py, theirs], 6 * i + 2 * j + delta))
        return outgoing, arrivals

    plan.counts = (6 * n_arrays, 6 * n_arrays)
    return plan


def _first_gather_plan():
    def plan(src_refs, land_refs):
        x, y, c, chips = _place()
        me = 2 * x + y
        rh = src_refs[0].shape[0] // 2
        mine = pl.ds(c * rh, rh)
        outgoing, arrivals = [], []
        for j, (px, py) in enumerate(chips):
            outgoing.append((src_refs[0].at[mine], land_refs[0].at[me, mine], (px, py, c), j, j))
            arrivals.append((land_refs[0].at[2 * px + py, mine], j))
            outgoing.append((src_refs[1], land_refs[1].at[me], (px, py, c), 3 + j, 3 + j))
            arrivals.append((land_refs[1].at[2 * px + py], 3 + j))
        return outgoing, arrivals

    plan.counts = (6, 6)
    return plan


def _exchange_plan(n_arrays):
    def plan(src_refs, land_refs):
        x, y, c, chips = _place()
        outgoing, arrivals = [], []
        for i in range(n_arrays):
            for j, (px, py) in enumerate(chips):
                outgoing.append((src_refs[i].at[2 * px + py], land_refs[i].at[j], (px, py, c), 3 * i + j, 3 * i + j))
                arrivals.append((land_refs[i].at[j], 3 * i + j))
        return outgoing, arrivals

    plan.counts = (3 * n_arrays, 3 * n_arrays)
    return plan


def _small_allreduce(vec):
    r, cdim = vec.shape
    n_dev = 8

    def body(v_ref, out_ref, buf, send, recv):
        x, y, c, _ = _place()
        me = 4 * x + 2 * y + c
        buf[me] = v_ref[...]
        cps = []
        for k in range(1, n_dev):
            dx, dy, dc = (k >> 2) & 1, (k >> 1) & 1, k & 1
            peer = (x ^ dx, y ^ dy, c ^ dc)
            cp = pltpu.make_async_remote_copy(
                src_ref=v_ref, dst_ref=buf.at[me], send_sem=send.at[k - 1], recv_sem=recv.at[k - 1],
                device_id=peer, device_id_type=MESH)
            cp.start()
            cps.append(cp)
        for k in range(1, n_dev):
            dx, dy, dc = (k >> 2) & 1, (k >> 1) & 1, k & 1
            src = 4 * (x ^ dx) + 2 * (y ^ dy) + (c ^ dc)
            slot = buf.at[src]
            pltpu.make_async_remote_copy(
                src_ref=slot, dst_ref=slot, send_sem=send.at[k - 1], recv_sem=recv.at[k - 1],
                device_id=(x ^ dx, y ^ dy, c ^ dc), device_id_type=MESH).wait_recv()
        for cp in cps:
            cp.wait_send()
        acc = buf[0]
        for k in range(1, n_dev):
            acc = acc + buf[k]
        out_ref[...] = acc

    vm = pl.BlockSpec(memory_space=pltpu.VMEM)
    return pl.pallas_call(
        body, name="small_allreduce", in_specs=[vm], out_specs=vm,
        out_shape=jax.ShapeDtypeStruct((r, cdim), F32),
        scratch_shapes=[pltpu.VMEM((n_dev, r, cdim), F32), pltpu.SemaphoreType.DMA((n_dev - 1,)),
                        pltpu.SemaphoreType.DMA((n_dev - 1,))],
    )(vec)


def _a_cols_to_head_major(w):
    lead = w.shape[:-1]
    q = w[..., :A_QK].reshape(lead + (A_HEADS, A_DK))
    k = w[..., A_QK:2 * A_QK].reshape(lead + (A_HEADS, A_DK))
    v = w[..., 2 * A_QK:2 * A_QK + A_VW].reshape(lead + (A_HEADS, A_DV))
    z = w[..., 2 * A_QK + A_VW:].reshape(lead + (A_HEADS, A_DV))
    return jnp.concatenate([q, k, v, z], axis=-1).reshape(lead + (A_HEADS * A_HEAD_COLS,))


def _a_cols_from_head_major(w):
    lead = w.shape[:-1]
    w = w.reshape(lead + (A_HEADS, A_HEAD_COLS))
    parts = [w[..., :A_DK], w[..., A_DK:2 * A_DK], w[..., 2 * A_DK:2 * A_DK + A_DV], w[..., 2 * A_DK + A_DV:]]
    return jnp.concatenate([p.reshape(lead + (-1,)) for p in parts], axis=-1)


def _conv_cols_to_head_major(w):
    lead = w.shape[:-1]
    q = w[..., :A_QK].reshape(lead + (A_HEADS, A_DK))
    k = w[..., A_QK:2 * A_QK].reshape(lead + (A_HEADS, A_DK))
    v = w[..., 2 * A_QK:].reshape(lead + (A_HEADS, A_DV))
    return jnp.concatenate([q, k, v], axis=-1).reshape(lead + (A_HEADS * A_CONV_COLS,))


def _conv_cols_from_head_major(w):
    lead = w.shape[:-1]
    w = w.reshape(lead + (A_HEADS, A_CONV_COLS))
    parts = [w[..., :A_DK], w[..., A_DK:2 * A_DK], w[..., 2 * A_DK:]]
    return jnp.concatenate([p.reshape(lead + (-1,)) for p in parts], axis=-1)


def _to_stream(a, bn, d):
    rest = a.shape[1:]
    s = a.shape[0] // bn
    a = a.reshape((bn, s // d, d) + rest)
    a = jnp.swapaxes(a, 1, 2)
    return a.reshape((bn * d, s // d) + rest)


def _from_stream(a, bn, d):
    rest = a.shape[2:]
    ln = a.shape[1]
    a = a.reshape((bn, d, ln) + rest)
    a = jnp.swapaxes(a, 1, 2)
    return a.reshape((bn * ln * d,) + rest)


B_SUB = 512
B_SHARD_BLOCKS = (3 * B_GROUPS * B_W + B_W) // N_CHIPS // B_SUB


def _b_block(gi, jj):
    nb = (B_GROUPS * (jj // 2) + gi) * 2 + jj % 2
    return nb // B_SHARD_BLOCKS, nb % B_SHARD_BLOCKS


def _shard_major(g, ncols):
    r = g.shape[0]
    return jnp.swapaxes(g.reshape(r, N_CHIPS, ncols), 0, 1)


def _pack_rows(items):
    rows, offs = [], []
    at = 0
    for a in items:
        flat = a.reshape(-1).astype(F32)
        nr = -(-flat.shape[0] // 1024) * 8
        flat = jnp.pad(flat, (0, nr * 128 - flat.shape[0]))
        rows.append(flat.reshape(nr, 128))
        offs.append((at, nr, a.shape))
        at += nr
    return jnp.concatenate(rows, axis=0), offs


def _unpack_rows(packed, offs):
    out = []
    for at, nr, shape in offs:
        size = int(np.prod(shape)) if len(shape) else 1
        out.append(packed[at:at + nr].reshape(-1)[:size].reshape(shape))
    return out


def _local_step(x, positions, loss_target, norm_g, a_log, a_dt_bias, a_norm_g, b_q_norm_g, b_k_norm_g,
                start_token, first_weights, late_weights, b_grads_ready, a_grads_ready):
    bn, s, d = x.shape
    t = bn * s
    n_chunks = s // A_CHUNK
    x0 = x.reshape(t, d)
    h0 = _rms_fwd(x0, norm_g[0:1] + start_token, "rms0_fwd")
    inv_freq = ROPE_THETA ** (-jnp.arange(0, ROPE_DIMS, 2, dtype=F32) / ROPE_DIMS)
    freq_row = jnp.concatenate([inv_freq, inv_freq, jnp.zeros((128 - ROPE_DIMS,), F32)]).reshape(1, 128)
    posf = jnp.broadcast_to(positions.astype(F32).reshape(t, 1), (t, 128)) + start_token
    tabs = _rope_tables(posf, freq_row)
    tabs_s = [tabs if dil == 1 else [_to_stream(tb, bn, dil).reshape(t, 128) for tb in tabs] for dil in B_DIL]
    wa_in, conv_w, late_token = first_weights([h0] + [tb for ts in tabs_s for tb in ts])
    wa_main = _a_cols_to_head_major(wa_in[:, :A_MAIN])
    wa_tail = jnp.pad(wa_in[:, A_MAIN:], ((0, 0), (0, 128 - 2 * A_HEADS))) + late_token.astype(BF16)
    cw_hm = _conv_cols_to_head_major(conv_w)

    proj_a = _matmul(h0, wa_main, "nn", F32, "a_in_main", tm=2048)
    tail_a = _matmul(h0, wa_tail, "nn", F32, "a_in_tail")
    tail_t = jnp.swapaxes(tail_a[:, :2 * A_HEADS].reshape(bn, s, 2 * A_HEADS), 1, 2)
    tail_t = tail_t.reshape(bn, 2 * A_HEADS, n_chunks, A_CHUNK)
    beta, gc = _gdn_prep(tail_t, a_log[0], a_dt_bias[0])
    proj_a3 = proj_a.reshape(bn, s, A_MAIN)
    og_a, oraw_a, states, t_mats, conv_y = _gdn_fwd(proj_a3, cw_hm, beta, gc, a_norm_g)
    wa_out, wb_in, wb_out = late_weights(og_a)
    b_cols = [4 * B_W] + [3 * B_W] * (B_GROUPS - 1)
    x1, h1 = _out_proj(og_a.reshape(t, A_VW), wa_out, x0, "a_out", norm_g=norm_g[1:2])

    h1_s, proj_b, qkv_b, o_b, lse_b = [], [], [], [], []
    for gi, dil in enumerate(B_DIL):
        hs = h1 if dil == 1 else _to_stream(h1, bn, dil).reshape(t, d)
        ts = tabs_s[gi]
        pj = _matmul(hs, wb_in, "nn", BF16, f"b_in_g{gi}", tm=2048, tn=B_SUB, n=b_cols[gi], b_spec=pl.BlockSpec(
            (None, d, B_SUB), lambda i, j, kk, gi=gi: (_b_block(gi, j)[0], kk, _b_block(gi, j)[1])))
        qkv = _qk_prep(pj, *ts, b_q_norm_g[0, gi:gi + 1], b_k_norm_g[0, gi:gi + 1], f"qk_prep_g{gi}")
        o_s, lse_s = _attn_fwd(qkv.reshape(bn * dil, s // dil, 3 * B_W), f"attn_fwd_g{gi}")
        h1_s.append(hs), proj_b.append(pj), qkv_b.append(qkv)
        o_b.append(o_s.reshape(t, B_W) if dil == 1 else _from_stream(o_s, bn, dil))
        lse_b.append(lse_s.reshape(t, B_HEADS) if dil == 1 else _from_stream(lse_s, bn, dil))
    og_b = _merge_fwd(o_b, lse_b, proj_b[0])
    d_x2, loss_parts = _out_proj(og_b, wb_out, x1, "b_out_loss", target=loss_target.reshape(t, d))
    loss_local = jnp.sum(loss_parts)

    d_x2b = d_x2.astype(BF16)
    g_wb_out = _matmul(og_b, d_x2b, "tn", F32, "b_out_dw", tk=DW_K)
    d_og_b = _matmul(d_x2b, wb_out, "nt", BF16, "b_out_dx")
    d_o, lse_joint, delta, d_z = _merge_bwd(o_b, lse_b, proj_b[0], d_og_b)
    d_h1, g_qn, g_kn = [], [], []
    g_wb_in = lax.empty(wb_in.shape, F32)
    for gi, dil in enumerate(B_DIL):
        if dil == 1:
            do_s, lj_s, dl_s = d_o, lse_joint, delta
        else:
            do_s, lj_s, dl_s = (_to_stream(a, bn, dil).reshape(t, -1) for a in (d_o, lse_joint, delta))
        ns, ln = bn * dil, s // dil
        dq, dk, dv = _attn_bwd(qkv_b[gi].reshape(ns, ln, 3 * B_W), do_s.reshape(ns, ln, B_W),
                               lj_s.reshape(ns, ln, B_HEADS), dl_s.reshape(ns, ln, B_HEADS), f"attn_bwd_g{gi}")
        d_pj, d_gain = _qk_prep_bwd(proj_b[gi], *tabs_s[gi], b_q_norm_g[0, gi:gi + 1], b_k_norm_g[0, gi:gi + 1],
                                    dq.reshape(t, B_W), dk.reshape(t, B_W), dv.reshape(t, B_W),
                                    d_z if gi == 0 else None, f"qk_prep_bwd_g{gi}")
        g_wb_in = _matmul(h1_s[gi], d_pj, "tn", F32, f"b_in_dw_g{gi}", tn=B_SUB, tk=DW_K, into=(g_wb_in, pl.BlockSpec(
            (None, d, B_SUB), lambda i, j, kk, gi=gi: (_b_block(gi, j)[0], i, _b_block(gi, j)[1]))))
        dh = _matmul(d_pj, wb_in, "nt", BF16, f"b_in_dx_g{gi}", tm=2048, tk=B_SUB, n=d, b_spec=pl.BlockSpec(
            (None, d, B_SUB), lambda i, j, kk, gi=gi: (_b_block(gi, kk)[0], j, _b_block(gi, kk)[1])))
        d_h1.append(dh if dil == 1 else _from_stream(dh.reshape(ns, ln, d), bn, dil))
        g_qn.append(d_gain[0]), g_kn.append(d_gain[1])
    d_x1, g_norm1 = _rms_bwd(x1, norm_g[1:2], d_h1, d_x2, "rms1_bwd")

    d_x1b = d_x1.astype(BF16)
    g_wa_out = _matmul(og_a.reshape(t, A_VW), d_x1b, "tn", F32, "a_out_dw", tk=DW_K)
    b_token = b_grads_ready(g_wb_in, g_wb_out, g_wa_out)
    d_og_a = _matmul(d_x1b, wa_out, "nt", BF16, "a_out_dx")
    d_pa, d_gc, d_beta, d_cw, d_ng = _gdn_bwd(proj_a3, cw_hm, beta, gc, a_norm_g + b_token, oraw_a, states,
                                              t_mats, conv_y, d_og_a.reshape(bn, s, A_VW))
    d_tail_t, d_alog, d_dtb = _gdn_prep_bwd(tail_t, a_log[0], a_dt_bias[0], d_gc, d_beta)
    d_tail = jnp.swapaxes(d_tail_t.reshape(bn, 2 * A_HEADS, s), 1, 2).reshape(t, 2 * A_HEADS)
    d_tail = jnp.pad(d_tail, ((0, 0), (0, 128 - 2 * A_HEADS))).astype(BF16)
    d_pa = d_pa.reshape(t, A_MAIN)
    g_wa_main = _matmul(h0, d_pa, "tn", F32, "a_in_dw_main", tk=DW_K)
    g_wa_tail = _matmul(h0, d_tail, "tn", F32, "a_in_dw_tail", tk=DW_K)
    g_wa_in = jnp.concatenate([_a_cols_from_head_major(g_wa_main), g_wa_tail[:, :2 * A_HEADS]], axis=1)
    a_token = a_grads_ready(g_wa_in)
    d_h0t = _matmul(d_tail + a_token.astype(BF16), wa_tail, "nt", F32, "a_in_dx_tail")
    d_x0, g_norm0 = _in_proj_bwd(d_pa, wa_main, d_h0t, x0, norm_g[0:1], d_x1, "a_in_dx_rms0_bwd")

    gfull = {
        "norm_g": jnp.concatenate([g_norm0, g_norm1], axis=0), "a_w_in": g_wa_in,
        "a_conv_w": _conv_cols_from_head_major(jnp.sum(d_cw, axis=0)),
        "a_log": jnp.sum(d_alog[:, :, 0], axis=0), "a_dt_bias": jnp.sum(d_dtb[:, :, 0], axis=0),
        "a_norm_g": jnp.sum(d_ng[:, :, 0, :], axis=(0, 1)), "a_w_out": g_wa_out, "b_w_in": g_wb_in,
        "b_q_norm_g": jnp.stack(g_qn), "b_k_norm_g": jnp.stack(g_kn), "b_w_out": g_wb_out}
    return loss_local, d_x0.reshape(bn, s, d), gfull


def kernel(x, positions, norm_g, a_w_in, a_conv_w, a_log, a_dt_bias, a_norm_g, a_w_out, b_w_in, b_q_norm_g, b_k_norm_g, b_w_out, loss_target, m_norm_g, m_a_w_in, m_a_conv_w, m_a_log, m_a_dt_bias, m_a_norm_g, m_a_w_out, m_b_w_in, m_b_q_norm_g, m_b_k_norm_g, m_b_w_out, v_norm_g, v_a_w_in, v_a_conv_w, v_a_log, v_a_dt_bias, v_a_norm_g, v_a_w_out, v_b_w_in, v_b_q_norm_g, v_b_k_norm_g, v_b_w_out):
    d = x.shape[2]
    my_c = lax.axis_index("c")
    my_chip = 2 * lax.axis_index("x") + lax.axis_index("y")

    half_index = jnp.reshape(my_c, (1,)).astype(jnp.int32)
    chip_index = jnp.reshape(my_chip, (1,)).astype(jnp.int32)
    def landing(shard):
        return lax.dynamic_update_slice(lax.empty((N_CHIPS,) + shard.shape, shard.dtype), shard[None],
                                        (my_chip,) + (0,) * shard.ndim)

    first_shards = [a_w_in[0].astype(BF16), a_conv_w[0]]
    first_plan = _first_gather_plan()
    first = _split_copy_start("first_weights_start", first_plan, first_shards,
                              [landing(s) for s in first_shards], half_index)
    pending = {}
    late_shards = [(w[0] + first[4][0, 0]).astype(BF16) for w in (a_w_out, b_w_in, b_w_out)]
    late_lands = [landing(s) for s in late_shards]

    def first_weights(after):
        _, (ga_in, g_conv) = _split_copy_wait("first_weights_wait", first_plan, *first[:4],
                                              list(after) + late_lands)
        ga_in = _sibling_forward(ga_in)
        wa_in = jnp.concatenate([ga_in[k] for k in range(N_CHIPS)], axis=1)
        conv_w = jnp.concatenate([g_conv[k] for k in range(N_CHIPS)], axis=1)
        plan = _gather_plan(len(late_shards))
        pending["late"] = (plan,) + tuple(_split_copy_start(
            "late_weights_start", plan, late_shards, late_lands, conv_w))
        return wa_in, conv_w, pending["late"][5][0, 0]

    def late_weights(after):
        plan, send, recv, srcs, lands, _ = pending["late"]
        _, (ga_out, gb_in, gb_out) = _split_copy_wait("late_weights_wait", plan, send, recv, srcs, lands, after)
        return ga_out.reshape(A_VW, d), gb_in, gb_out.reshape(B_W, d)

    def reduce_to_chip_sums(mats, tag):
        half = lambda g: lax.dynamic_slice_in_dim(g, (1 - my_c) * (g.shape[1] // 2), g.shape[1] // 2, axis=1)
        recv_sib = _sibling_swap([half(g).astype(BF16) for g in mats], f"grad_{tag}_sibling_swap")
        return [_pair_sum(g, r, half_index, f"grad_{tag}_pair_sum_{i}") for i, (g, r) in enumerate(zip(mats, recv_sib))]

    def start_exchange(tag, mats):
        sums = reduce_to_chip_sums(mats, tag)
        lands = [lax.empty((N_CHIPS - 1,) + s.shape[1:], BF16) for s in sums]
        plan = _exchange_plan(len(mats))
        pending[tag] = (plan,) + tuple(_split_copy_start(f"grad_{tag}_exchange_start", plan, sums, lands, chip_index))
        return pending[tag][5][0, 0]

    def finish_exchange(tag, after):
        plan, send, recv, srcs, lands, _ = pending[tag]
        return _split_copy_wait(f"grad_{tag}_exchange_wait", plan, send, recv, srcs, lands, after)

    def b_grads_ready(g_wb_in, g_wb_out, g_wa_out):
        return start_exchange("b", [g_wb_in, g_wb_out.reshape(N_CHIPS, -1, d), g_wa_out.reshape(N_CHIPS, -1, d)])

    def a_grads_ready(g_wa_in):
        return start_exchange("a", [_shard_major(g_wa_in, a_w_in.shape[2])])

    loss_local, d_x0, gfull = _local_step(x, positions, loss_target, norm_g, a_log, a_dt_bias, a_norm_g,
                                          b_q_norm_g, b_k_norm_g, first[4][0, 0], first_weights, late_weights,
                                          b_grads_ready, a_grads_ready)

    small = [gfull["norm_g"], gfull["a_conv_w"], gfull["a_log"], gfull["a_dt_bias"], gfull["a_norm_g"],
             gfull["b_q_norm_g"], gfull["b_k_norm_g"], loss_local]
    packed, offs = _pack_rows(small)
    reduced = _small_allreduce(packed)
    g_norm, g_conv_all, g_alog, g_dtb, g_ang, g_q, g_k, loss = _unpack_rows(reduced, offs)
    g_conv_mine = lax.dynamic_slice_in_dim(g_conv_all, my_chip * a_conv_w.shape[2], a_conv_w.shape[2], axis=1)

    b_sums, b_received = finish_exchange("b", d_x0)
    a_sums, a_received = finish_exchange("a", reduced)
    chip_sums = [a_sums[0], b_sums[2], b_sums[0], b_sums[1]]
    received = [a_received[0], b_received[2], b_received[0], b_received[1]]
    halves = [_chip_sum(s, r, chip_index, f"grad_chip_sum_{i}") for i, (s, r) in enumerate(zip(chip_sums, received))]
    theirs = _sibling_swap(halves, "grad_sibling_join")
    big = ("a_w_in", "a_w_out", "b_w_in", "b_w_out")
    big_halves = dict(zip(big, zip(halves, theirs)))

    grads = {
        "norm_g": g_norm, "a_conv_w": g_conv_mine[None], "a_log": g_alog[None], "a_dt_bias": g_dtb[None],
        "a_norm_g": g_ang[None], "b_q_norm_g": g_q[None], "b_k_norm_g": g_k[None]}
    weights = {"norm_g": norm_g, "a_w_in": a_w_in, "a_conv_w": a_conv_w, "a_log": a_log, "a_dt_bias": a_dt_bias,
               "a_norm_g": a_norm_g, "a_w_out": a_w_out, "b_w_in": b_w_in, "b_q_norm_g": b_q_norm_g,
               "b_k_norm_g": b_k_norm_g, "b_w_out": b_w_out}
    m_in = {"norm_g": m_norm_g, "a_w_in": m_a_w_in, "a_conv_w": m_a_conv_w, "a_log": m_a_log,
            "a_dt_bias": m_a_dt_bias, "a_norm_g": m_a_norm_g, "a_w_out": m_a_w_out, "b_w_in": m_b_w_in,
            "b_q_norm_g": m_b_q_norm_g, "b_k_norm_g": m_b_k_norm_g, "b_w_out": m_b_w_out}
    v_in = {"norm_g": v_norm_g, "a_w_in": v_a_w_in, "a_conv_w": v_a_conv_w, "a_log": v_a_log,
            "a_dt_bias": v_a_dt_bias, "a_norm_g": v_a_norm_g, "a_w_out": v_a_w_out, "b_w_in": v_b_w_in,
            "b_q_norm_g": v_b_q_norm_g, "b_k_norm_g": v_b_k_norm_g, "b_w_out": v_b_w_out}
    names = list(weights)

    delta_w, new_m, new_v = {}, {}, {}
    for nm in big:
        mine, other = big_halves[nm]
        if weights[nm].shape[2] % 128:
            cols = lambda a: jnp.transpose(a, (2, 0, 1))
            half_cols = lambda a: jnp.transpose(a)[:, None, :]
            outs = _adamw_shard_cols(cols(weights[nm]), half_cols(mine), half_cols(other), cols(m_in[nm]),
                                     cols(v_in[nm]), half_index, f"adamw_{nm}")
            outs = [jnp.transpose(o, (1, 2, 0)) for o in outs]
        else:
            outs = _adamw_shard(weights[nm], mine, other, m_in[nm], v_in[nm], half_index, f"adamw_{nm}")
        grads[nm], delta_w[nm], new_m[nm], new_v[nm] = outs
    small_names = [nm for nm in names if nm not in big]
    packs = [_pack_rows([src[nm] for nm in small_names]) for src in (weights, grads, m_in, v_in)]
    offs = packs[0][1]
    dl, m2, v2 = _adamw(packs[0][0], packs[1][0], packs[2][0], packs[3][0], "adamw_small")
    for nm, a, b, c2 in zip(small_names, _unpack_rows(dl, offs), _unpack_rows(m2, offs), _unpack_rows(v2, offs)):
        delta_w[nm], new_m[nm], new_v[nm] = a, b, c2

    return (loss, d_x0, *[grads[nm] for nm in names], *[delta_w[nm] for nm in names],
            *[new_m[nm] for nm in names], *[new_v[nm] for nm in names])
```

```python
import jax
import jax.numpy as jnp
import numpy as np
from jax import lax
from jax.experimental import pallas as pl
from jax.experimental.pallas import tpu as pltpu

F32 = jnp.float32
BF16 = jnp.bfloat16
MESH = pl.DeviceIdType.MESH

EPS = 1e-6
A_HEADS = 8
A_DK = 128
A_DV = 256
A_QK = A_HEADS * A_DK
A_VW = A_HEADS * A_DV
A_MAIN = 2 * A_QK + 2 * A_VW
A_HEAD_COLS = 2 * A_DK + 2 * A_DV
A_CONV_COLS = 2 * A_DK + A_DV
A_CHUNK = 64
A_CONV = 4
B_GROUPS = 3
B_HEADS = 8
B_DH = 128
B_W = B_HEADS * B_DH
B_DIL = (1, 4, 16)
B_BLK = 128
ROPE_THETA = 500000.0
ROPE_DIMS = B_DH // 4
ADAM_LR, ADAM_B1, ADAM_B2, ADAM_EPS, ADAM_WD, ADAM_STEP = 0.001, 0.9, 0.999, 1e-08, 0.01, 10
N_CHIPS = 4
VMEM_BIG = 56 * 1024 * 1024
DW_K = 4096


def _params(sem=None, vmem=None):
    return pltpu.CompilerParams(dimension_semantics=sem, vmem_limit_bytes=vmem)


def _dot(a, b, ca, cb):
    return lax.dot_general(a.astype(BF16), b.astype(BF16), (((ca,), (cb,)), ((), ())),
                           preferred_element_type=F32)


def _split3(a):
    hi = a.astype(BF16)
    r = a - hi.astype(F32)
    mid = r.astype(BF16)
    lo = (r - mid.astype(F32)).astype(BF16)
    return hi, mid, lo


def _sigmoid(y):
    return 1.0 / (1.0 + jnp.exp(-y))


def _silu(y):
    return y * _sigmoid(y)


def _silu_and_slope(y):
    s = _sigmoid(y)
    return y * s, s * (1.0 + y * (1.0 - s))


def _matmul(a, b, mode, out_dtype, name, res=None, tm=1024, tn=1024, tk=1024, n=None, b_spec=None, into=None):
    m, k = a.shape[::-1] if mode == "tn" else a.shape
    if n is None:
        n = b.shape[0] if mode == "nt" else b.shape[1]
    tm, tn, tk = min(tm, m), min(tn, n), min(tk, k)
    assert m % tm == 0 and n % tn == 0 and k % tk == 0, (name, a.shape, b.shape)
    nk = k // tk
    dims = {"nn": ((1,), (0,)), "nt": ((1,), (1,)), "tn": ((0,), (0,))}[mode]

    def body(*refs):
        a_ref, b_ref = refs[0], refs[1]
        r_ref = refs[2] if res is not None else None
        o_ref = refs[2 + (res is not None) + (into is not None)]
        prod = lax.dot_general(a_ref[...], b_ref[...], (dims, ((), ())), preferred_element_type=F32)

        def finish(r):
            if res is not None:
                r = r + r_ref[...]
            o_ref[...] = r.astype(out_dtype)

        if nk == 1:
            finish(prod)
            return
        acc = refs[-1]
        kk = pl.program_id(2)

        @pl.when(kk == 0)
        def _():
            acc[...] = prod

        @pl.when((kk > 0) & (kk < nk - 1))
        def _():
            acc[...] += prod

        @pl.when(kk == nk - 1)
        def _():
            finish(acc[...] + prod)

    a_spec = pl.BlockSpec((tm, tk), lambda i, j, kk: (i, kk))
    if mode == "tn":
        a_spec = pl.BlockSpec((tk, tm), lambda i, j, kk: (kk, i))
    if b_spec is None and mode == "nt":
        b_spec = pl.BlockSpec((tn, tk), lambda i, j, kk: (j, kk))
    elif b_spec is None:
        b_spec = pl.BlockSpec((tk, tn), lambda i, j, kk: (kk, j))
    in_specs = [a_spec, b_spec]
    args = [a, b]
    if res is not None:
        in_specs.append(pl.BlockSpec((tm, tn), lambda i, j, kk: (i, j)))
        args.append(res)
    out_spec = pl.BlockSpec((tm, tn), lambda i, j, kk: (i, j))
    out_shape = jax.ShapeDtypeStruct((m, n), out_dtype)
    aliases = {}
    if into is not None:
        assert res is None
        buf, out_spec = into
        out_shape = jax.ShapeDtypeStruct(buf.shape, buf.dtype)
        in_specs.append(ANY)
        args.append(buf)
        aliases = {2: 0}
    return pl.pallas_call(
        body, name=name, grid=(m // tm, n // tn, nk),
        in_specs=in_specs, out_specs=out_spec, out_shape=out_shape, input_output_aliases=aliases,
        scratch_shapes=[pltpu.VMEM((tm, tn), F32)] if nk > 1 else [],
        compiler_params=_params(("parallel", "parallel", "arbitrary"), 48 * 1024 * 1024),
    )(*args)


def _rms_fwd(x, g, name, tm=512):
    t, d = x.shape

    def body(x_ref, g_ref, h_ref):
        xv = x_ref[...]
        r = lax.rsqrt(jnp.mean(xv * xv, axis=-1, keepdims=True) + EPS)
        h_ref[...] = (xv * r * g_ref[...]).astype(BF16)

    return pl.pallas_call(
        body, name=name, grid=(t // tm,),
        in_specs=[pl.BlockSpec((tm, d), lambda i: (i, 0)), pl.BlockSpec((1, d), lambda i: (0, 0))],
        out_specs=pl.BlockSpec((tm, d), lambda i: (i, 0)),
        out_shape=jax.ShapeDtypeStruct((t, d), BF16),
        compiler_params=_params(("parallel",)),
    )(x, g)


def _rms_bwd(x, g, dhs, dres, name, tm=512):
    t, d = x.shape
    n_dh = len(dhs)

    def body(*refs):
        x_ref, g_ref = refs[0], refs[1]
        dh_refs = refs[2:2 + n_dh]
        dres_ref, dx_ref, dg_ref = refs[2 + n_dh:]
        i = pl.program_id(0)

        @pl.when(i == 0)
        def _():
            dg_ref[...] = jnp.zeros_like(dg_ref)

        xv = x_ref[...]
        r = lax.rsqrt(jnp.mean(xv * xv, axis=-1, keepdims=True) + EPS)
        xh = xv * r
        dh = dh_refs[0][...].astype(F32)
        for ref in dh_refs[1:]:
            dh = dh + ref[...].astype(F32)
        dg_ref[0:1, :] += jnp.sum(dh * xh, axis=0, keepdims=True)
        dxh = dh * g_ref[...]
        dx = r * (dxh - xh * jnp.mean(dxh * xh, axis=-1, keepdims=True))
        dx_ref[...] = dx + dres_ref[...]

    row = pl.BlockSpec((tm, d), lambda i: (i, 0))
    dx, dg = pl.pallas_call(
        body, name=name, grid=(t // tm,),
        in_specs=[row, pl.BlockSpec((1, d), lambda i: (0, 0))] + [row] * n_dh + [row],
        out_specs=[row, pl.BlockSpec((8, d), lambda i: (0, 0))],
        out_shape=[jax.ShapeDtypeStruct((t, d), F32), jax.ShapeDtypeStruct((8, d), F32)],
        compiler_params=_params(("arbitrary",)),
    )(x, g, *dhs, dres)
    return dx, dg[0:1]


def _in_proj_bwd(dp, w, dh_more, x, g, dres, name, tm=512, tk=2048):
    t, k = dp.shape
    d = w.shape[0]
    nk = k // tk

    def body(dp_ref, w_ref, more_ref, x_ref, g_ref, dres_ref, dx_ref, dg_ref, acc):
        i, kk = pl.program_id(0), pl.program_id(1)

        @pl.when((i == 0) & (kk == 0))
        def _():
            dg_ref[...] = jnp.zeros_like(dg_ref)

        prod = lax.dot_general(dp_ref[...], w_ref[...], (((1,), (1,)), ((), ())), preferred_element_type=F32)

        @pl.when(kk == 0)
        def _():
            acc[...] = prod

        @pl.when((kk > 0) & (kk < nk - 1))
        def _():
            acc[...] += prod

        @pl.when(kk == nk - 1)
        def _():
            dh = acc[...] + prod + more_ref[...]
            xv = x_ref[...]
            r = lax.rsqrt(jnp.mean(xv * xv, axis=-1, keepdims=True) + EPS)
            xh = xv * r
            dg_ref[0:1, :] += jnp.sum(dh * xh, axis=0, keepdims=True)
            dxh = dh * g_ref[...]
            dx_ref[...] = r * (dxh - xh * jnp.mean(dxh * xh, axis=-1, keepdims=True)) + dres_ref[...]

    row = pl.BlockSpec((tm, d), lambda i, kk: (i, 0))
    dx, dg = pl.pallas_call(
        body, name=name, grid=(t // tm, nk),
        in_specs=[pl.BlockSpec((tm, tk), lambda i, kk: (i, kk)), pl.BlockSpec((d, tk), lambda i, kk: (0, kk)),
                  row, row, pl.BlockSpec((1, d), lambda i, kk: (0, 0)), row],
        out_specs=[row, pl.BlockSpec((8, d), lambda i, kk: (0, 0))],
        out_shape=[jax.ShapeDtypeStruct((t, d), F32), jax.ShapeDtypeStruct((8, d), F32)],
        scratch_shapes=[pltpu.VMEM((tm, d), F32)],
        compiler_params=_params(("arbitrary", "arbitrary"), 48 * 1024 * 1024),
    )(dp, w, dh_more, x, g, dres)
    return dx, dg[0:1]


def _out_proj(a, w, res, name, norm_g=None, target=None, tm=512):
    t, k = a.shape
    d = w.shape[1]
    nb = t // tm

    def body(a_ref, w_ref, r_ref, x_ref, o1_ref, o2_ref):
        y = jnp.dot(a_ref[...], w_ref[...], preferred_element_type=F32) + r_ref[...]
        if norm_g is not None:
            o1_ref[...] = y
            r = lax.rsqrt(jnp.mean(y * y, axis=-1, keepdims=True) + EPS)
            o2_ref[...] = (y * r * x_ref[...]).astype(BF16)
        else:
            e = y - x_ref[...]
            o1_ref[...] = e * (1.0 / d)
            s = jnp.sum(jnp.sum(e * e, axis=1, keepdims=True), axis=0, keepdims=True) * (0.5 / d)
            o2_ref[...] = jnp.broadcast_to(s, (8, 128))

    row = pl.BlockSpec((tm, d), lambda i: (i, 0))
    if norm_g is not None:
        extra, extra_spec = norm_g, pl.BlockSpec((1, d), lambda i: (0, 0))
        out2_spec, out2_shape = row, jax.ShapeDtypeStruct((t, d), BF16)
    else:
        extra, extra_spec = target, row
        out2_spec = pl.BlockSpec((None, 8, 128), lambda i: (i, 0, 0))
        out2_shape = jax.ShapeDtypeStruct((nb, 8, 128), F32)
    o1, o2 = pl.pallas_call(
        body, name=name, grid=(nb,),
        in_specs=[pl.BlockSpec((tm, k), lambda i: (i, 0)), pl.BlockSpec((k, d), lambda i: (0, 0)), row, extra_spec],
        out_specs=[row, out2_spec], out_shape=[jax.ShapeDtypeStruct((t, d), F32), out2_shape],
        compiler_params=_params(("parallel",), 48 * 1024 * 1024),
    )(a, w, res, extra)
    return (o1, o2) if norm_g is not None else (o1, o2[:, 0, 0])


def _softplus(x):
    t = jnp.exp(-jnp.abs(x))
    return jnp.maximum(x, 0.0) + jnp.where(t < 1e-3, t * (1.0 - 0.5 * t), jnp.log(1.0 + t))


def _tri(rows_le_cols):
    r = lax.broadcasted_iota(jnp.int32, (A_CHUNK, A_CHUNK), 0)
    c = lax.broadcasted_iota(jnp.int32, (A_CHUNK, A_CHUNK), 1)
    return jnp.where((r <= c) if rows_le_cols else (r >= c), 1.0, 0.0).astype(BF16)


def _dot_exact_rhs(a, ones_bf16):
    dn = (((1,), (0,)), ((), ()))
    hi, mid, lo = _split3(a)
    out = lax.dot_general(hi, ones_bf16, dn, preferred_element_type=F32)
    out = out + lax.dot_general(mid, ones_bf16, dn, preferred_element_type=F32)
    return out + lax.dot_general(lo, ones_bf16, dn, preferred_element_type=F32)


def _gdn_prep(tail_t, a_log, dt_bias):
    bn, _, n, c = tail_t.shape

    def body(t_ref, alog_ref, dtb_ref, beta_ref, gc_ref):
        upper = _tri(True)
        for h in range(A_HEADS):
            beta_ref[h] = _sigmoid(t_ref[h])
            ea = jnp.exp(jnp.full((n, c), alog_ref[h], F32))
            g = -ea * _softplus(t_ref[A_HEADS + h] + dtb_ref[h])
            gc_ref[h] = _dot_exact_rhs(g, upper)

    smem = pl.BlockSpec(memory_space=pltpu.SMEM)
    blk = pl.BlockSpec((None, A_HEADS, n, c), lambda b: (b, 0, 0, 0))
    return pl.pallas_call(
        body, name="gdn_prep", grid=(bn,),
        in_specs=[pl.BlockSpec((None, 2 * A_HEADS, n, c), lambda b: (b, 0, 0, 0)), smem, smem],
        out_specs=[blk, blk],
        out_shape=[jax.ShapeDtypeStruct((bn, A_HEADS, n, c), F32)] * 2,
        compiler_params=_params(("parallel",)),
    )(tail_t, a_log, dt_bias)


def _gdn_prep_bwd(tail_t, a_log, dt_bias, d_gc, d_beta):
    bn, _, n, c = tail_t.shape

    def body(t_ref, alog_ref, dtb_ref, dgc_ref, dbeta_ref, dt_ref, dal_ref, ddt_ref):
        lower = _tri(False)
        for h in range(A_HEADS):
            beta = _sigmoid(t_ref[h])
            dt_ref[h] = dbeta_ref[h] * beta * (1.0 - beta)
            dg = _dot_exact_rhs(dgc_ref[h], lower)
            ea = jnp.exp(jnp.full((n, c), alog_ref[h], F32))
            xa = t_ref[A_HEADS + h] + dtb_ref[h]
            g = -ea * _softplus(xa)
            dxa = -ea * dg * _sigmoid(xa)
            dt_ref[A_HEADS + h] = dxa
            s1 = jnp.sum(jnp.sum(g * dg, axis=1, keepdims=True), axis=0, keepdims=True)
            s2 = jnp.sum(jnp.sum(dxa, axis=1, keepdims=True), axis=0, keepdims=True)
            dal_ref[h:h + 1, :] = jnp.broadcast_to(s1, (1, 128))
            ddt_ref[h:h + 1, :] = jnp.broadcast_to(s2, (1, 128))

    smem = pl.BlockSpec(memory_space=pltpu.SMEM)
    blk8 = pl.BlockSpec((None, A_HEADS, n, c), lambda b: (b, 0, 0, 0))
    blk16 = pl.BlockSpec((None, 2 * A_HEADS, n, c), lambda b: (b, 0, 0, 0))
    sm = pl.BlockSpec((None, A_HEADS, 128), lambda b: (b, 0, 0))
    return pl.pallas_call(
        body, name="gdn_prep_bwd", grid=(bn,),
        in_specs=[blk16, smem, smem, blk8, blk8],
        out_specs=[blk16, sm, sm],
        out_shape=[jax.ShapeDtypeStruct((bn, 2 * A_HEADS, n, c), F32),
                   jax.ShapeDtypeStruct((bn, A_HEADS, 128), F32),
                   jax.ShapeDtypeStruct((bn, A_HEADS, 128), F32)],
        compiler_params=_params(("parallel",)),
    )(tail_t, a_log, dt_bias, d_gc, d_beta)


HALO = 8


def _conv_taps(xw, w):
    y = w[A_CONV - 1:A_CONV, :] * xw
    for j in range(1, A_CONV):
        y = y + w[A_CONV - 1 - j:A_CONV - j, :] * pltpu.roll(xw, j, 0)
    return y[HALO:, :]


def _row_to_col(row, eye):
    c = eye.shape[0]
    return jnp.sum(jnp.where(eye, jnp.broadcast_to(row, (c, c)), 0.0), axis=1, keepdims=True)


def _col_to_row(col, eye):
    c = eye.shape[0]
    return jnp.sum(jnp.where(eye, jnp.broadcast_to(col, (c, c)), 0.0), axis=0, keepdims=True)


def _unit_lower_inverse(a, ri, ci):
    eye = jnp.where(ri == ci, 1.0, 0.0)
    a8 = jnp.where((ri >> 3) == (ci >> 3), a, 0.0)
    a2 = _dot(a8, a8, 1, 0)
    yield
    a4 = _dot(a2, a2, 1, 0)
    t = eye - a8
    t = t + _dot(t, a2, 1, 0)
    yield
    t = t + _dot(t, a4, 1, 0)
    yield
    for sh in (3, 4, 5):
        off = jnp.where(((ri >> (sh + 1)) == (ci >> (sh + 1))) & ((ri >> sh) != (ci >> sh)), a, 0.0)
        left = _dot(t, off, 1, 0)
        yield
        t = t - _dot(left, t, 1, 0)
        yield
    return t


def _round_robin(gens):
    live = list(gens)
    while live:
        nxt = []
        for g in live:
            try:
                next(g)
                nxt.append(g)
            except StopIteration:
                pass
        live = nxt


def _gdn_chunk_core(q, k, v, g_row, b_row, t_mat, ri, ci):
    eye = ri == ci
    g_col = _row_to_col(g_row, eye)
    b_col = _row_to_col(b_row, eye)
    causal = ri >= ci
    strict = ri > ci
    dec = jnp.where(causal, jnp.exp(jnp.where(causal, g_col - g_row, 0.0)), 0.0)
    gam = jnp.exp(g_col)
    g_last = g_row[:, A_CHUNK - 1:A_CHUNK]
    gam_last = jnp.exp(g_last)
    e = jnp.exp(g_last - g_col)
    kb = k * b_col
    bv = v * b_col
    kbg = kb * gam
    q16, k16, kb16 = q.astype(BF16), k.astype(BF16), kb.astype(BF16)
    kk = _dot(kb16, k16, 1, 1)
    p = _dot(q16, k16, 1, 1) * dec
    yield
    a_mat = jnp.where(strict, kk * dec, 0.0)
    if t_mat is None:
        t_mat = yield from _unit_lower_inverse(a_mat, ri, ci)
    t16 = t_mat.astype(BF16)
    u = _dot(t16, bv, 1, 0)
    w = _dot(t16, kbg, 1, 0)
    yield
    return dict(eye=eye, g_col=g_col, b_col=b_col, dec=dec, strict=strict, causal=causal, gam=gam,
                gam_last=gam_last, e=e, kb=kb, bv=bv, kbg=kbg, a_mat=a_mat, t_mat=t_mat, u=u, w=w, p=p,
                qg=q * gam, kd=k * e, q16=q16, k16=k16, kb16=kb16, t16=t16)


A_SEQ_BLK = 256
A_BLK_CHUNKS = A_SEQ_BLK // A_CHUNK


def _gdn_halo(proj_hm):
    bn, s, w = proj_hm.shape
    last = proj_hm.reshape(bn, s // A_SEQ_BLK, A_SEQ_BLK, w)[:, :, A_SEQ_BLK - HALO:, :]
    return jnp.concatenate([jnp.zeros((bn, 1, HALO, w), proj_hm.dtype), last[:, :-1]], axis=1)


def _gdn_window(x_ref, halo_ref, ci, first, lo):
    if first:
        return jnp.concatenate([halo_ref[:, lo:lo + A_CONV_COLS], x_ref[0:A_CHUNK, lo:lo + A_CONV_COLS]], axis=0)
    start = pl.multiple_of(ci * A_CHUNK - HALO, HALO)
    return x_ref[pl.ds(start, A_CHUNK + HALO), lo:lo + A_CONV_COLS]


def _gdn_chunk_prep(xw, cw, y=None):
    if y is None:
        y = _conv_taps(xw, cw)
    a, slope = _silu_and_slope(y)
    aq, ak, v = a[:, 0:A_DK], a[:, A_DK:2 * A_DK], a[:, 2 * A_DK:]
    rq = lax.rsqrt(jnp.sum(aq * aq, axis=1, keepdims=True) + EPS)
    rk = lax.rsqrt(jnp.sum(ak * ak, axis=1, keepdims=True) + EPS)
    return dict(xw=xw, y=y, slope=slope, aq=aq, ak=ak, rq=rq, rk=rk, q=aq * rq * (A_DK ** -0.5), k=ak * rk, v=v)


def _gdn_fwd(proj_hm, cw_hm, beta, gc, norm_g, hp=8):
    bn, s, _ = proj_hm.shape
    n = s // A_CHUNK
    nsb = s // A_SEQ_BLK
    halo = _gdn_halo(proj_hm)

    def body(x_ref, halo_ref, cw_ref, beta_ref, gc_ref, ng_ref, og_ref, oraw_ref, st_ref, t_ref, y_ref, state):
        first_chunk = pl.program_id(2) * A_BLK_CHUNKS
        ri = lax.broadcasted_iota(jnp.int32, (A_CHUNK, A_CHUNK), 0)
        ci_ = lax.broadcasted_iota(jnp.int32, (A_CHUNK, A_CHUNK), 1)
        ng = ng_ref[...]

        @pl.when(pl.program_id(2) == 0)
        def _():
            state[...] = jnp.zeros_like(state)

        def one_head(hh, ci, first, rows):
            lo = hh * A_HEAD_COLS
            cw = cw_ref[:, hh * A_CONV_COLS:(hh + 1) * A_CONV_COLS]
            cin = _gdn_chunk_prep(_gdn_window(x_ref, halo_ref, ci, first, lo), cw)
            y_ref[rows, hh * A_CONV_COLS:(hh + 1) * A_CONV_COLS] = cin["y"]
            seq_chunk = pl.ds(first_chunk + ci, 1)
            core = yield from _gdn_chunk_core(cin["q"], cin["k"], cin["v"], gc_ref[hh, seq_chunk, :],
                                              beta_ref[hh, seq_chunk, :], None, ri, ci_)
            st = state[hh]
            st_ref[hh, ci] = st
            t_ref[hh, ci] = core["t_mat"]
            st16 = st.astype(BF16)
            vn = core["u"] - _dot(core["w"], st16, 1, 0)
            qs = _dot(core["qg"], st16, 1, 0)
            yield
            vn16 = vn.astype(BF16)
            o = qs + _dot(core["p"], vn16, 1, 0)
            state[hh] = st * core["gam_last"] + _dot(core["kd"], vn16, 0, 0)
            yield
            ocols = slice(hh * A_DV, (hh + 1) * A_DV)
            oraw_ref[rows, ocols] = o
            r = lax.rsqrt(jnp.mean(o * o, axis=1, keepdims=True) + EPS)
            z = x_ref[rows, lo + A_CONV_COLS:lo + A_HEAD_COLS]
            og_ref[rows, ocols] = (o * r * ng * _silu(z)).astype(BF16)

        def chunk(ci, first):
            rows = pl.ds(0 if first else pl.multiple_of(ci * A_CHUNK, A_CHUNK), A_CHUNK)
            _round_robin([one_head(hh, ci, first, rows) for hh in range(hp)])

        chunk(0, True)
        lax.fori_loop(1, A_BLK_CHUNKS, lambda i, c: (chunk(i, False), c)[1], 0)

    small = pl.BlockSpec((None, hp, n, A_CHUNK), lambda b, h, j: (b, h, 0, 0))
    return pl.pallas_call(
        body, name="gdn_fwd", grid=(bn, A_HEADS // hp, nsb),
        in_specs=[pl.BlockSpec((None, A_SEQ_BLK, hp * A_HEAD_COLS), lambda b, h, j: (b, j, h)),
                  pl.BlockSpec((None, None, HALO, hp * A_HEAD_COLS), lambda b, h, j: (b, j, 0, h)),
                  pl.BlockSpec((A_CONV, hp * A_CONV_COLS), lambda b, h, j: (0, h)),
                  small, small,
                  pl.BlockSpec((1, A_DV), lambda b, h, j: (0, 0))],
        out_specs=[pl.BlockSpec((None, A_SEQ_BLK, hp * A_DV), lambda b, h, j: (b, j, h)),
                   pl.BlockSpec((None, A_SEQ_BLK, hp * A_DV), lambda b, h, j: (b, j, h)),
                   pl.BlockSpec((None, hp, A_BLK_CHUNKS, A_DK, A_DV), lambda b, h, j: (b, h, j, 0, 0)),
                   pl.BlockSpec((None, hp, A_BLK_CHUNKS, A_CHUNK, A_CHUNK), lambda b, h, j: (b, h, j, 0, 0)),
                   pl.BlockSpec((None, A_SEQ_BLK, hp * A_CONV_COLS), lambda b, h, j: (b, j, h))],
        out_shape=[jax.ShapeDtypeStruct((bn, s, A_VW), BF16),
                   jax.ShapeDtypeStruct((bn, s, A_VW), F32),
                   jax.ShapeDtypeStruct((bn, A_HEADS, n, A_DK, A_DV), F32),
                   jax.ShapeDtypeStruct((bn, A_HEADS, n, A_CHUNK, A_CHUNK), F32),
                   jax.ShapeDtypeStruct((bn, s, A_HEADS * A_CONV_COLS), F32)],
        scratch_shapes=[pltpu.VMEM((hp, A_DK, A_DV), F32)],
        compiler_params=_params(("parallel", "parallel", "arbitrary"), VMEM_BIG),
    )(proj_hm, halo, cw_hm, beta, gc, norm_g)


def _gdn_bwd(proj_hm, cw_hm, beta, gc, norm_g, oraw, states, t_mats, conv_y, dog, hp=4):
    bn, s, _ = proj_hm.shape
    n = s // A_CHUNK
    nsb = s // A_SEQ_BLK
    halo = _gdn_halo(proj_hm)

    def body(x_ref, halo_ref, cw_ref, beta_ref, gc_ref, ng_ref, oraw_ref, st_ref, t_ref, y_ref, dog_ref,
             dx_ref, dgc_ref, dbeta_ref, dcw_ref, dng_ref, dstate, dy_next, shifted):
        first_chunk = (nsb - 1 - pl.program_id(2)) * A_BLK_CHUNKS
        ri = lax.broadcasted_iota(jnp.int32, (A_CHUNK, A_CHUNK), 0)
        ci_ = lax.broadcasted_iota(jnp.int32, (A_CHUNK, A_CHUNK), 1)
        lane = lax.broadcasted_iota(jnp.int32, (1, A_CHUNK), 1)
        ng = ng_ref[...]

        @pl.when(pl.program_id(2) == 0)
        def _():
            dstate[...] = jnp.zeros_like(dstate)
            dy_next[...] = jnp.zeros_like(dy_next)
            dcw_ref[...] = jnp.zeros_like(dcw_ref)
            dng_ref[...] = jnp.zeros_like(dng_ref)

        def one_head(hh, ci, first, rows):
            lo = hh * A_HEAD_COLS
            ccols = slice(hh * A_CONV_COLS, (hh + 1) * A_CONV_COLS)
            ocols = slice(hh * A_DV, (hh + 1) * A_DV)
            cw = cw_ref[:, ccols]
            cin = _gdn_chunk_prep(_gdn_window(x_ref, halo_ref, ci, first, lo), cw, y_ref[rows, ccols])
            q, k, v = cin["q"], cin["k"], cin["v"]
            seq_chunk = pl.ds(first_chunk + ci, 1)
            cr = yield from _gdn_chunk_core(q, k, v, gc_ref[hh, seq_chunk, :], beta_ref[hh, seq_chunk, :],
                                            t_ref[hh, ci], ri, ci_)
            eye, dec, gam, e = cr["eye"], cr["dec"], cr["gam"], cr["e"]
            b_col, t_mat, u, w, p = cr["b_col"], cr["t_mat"], cr["u"], cr["w"], cr["p"]
            st = st_ref[hh, ci]
            ds_out = dstate[hh]

            o = oraw_ref[rows, ocols]
            z = x_ref[rows, lo + A_CONV_COLS:lo + A_HEAD_COLS]
            d_og = dog_ref[rows, ocols].astype(F32)
            r = lax.rsqrt(jnp.mean(o * o, axis=1, keepdims=True) + EPS)
            oh = o * r
            gate, gate_slope = _silu_and_slope(z)
            d_on = d_og * gate
            dz = d_og * oh * ng * gate_slope
            dng_ref[hh, 0:1, :] += jnp.sum(d_on * oh, axis=0, keepdims=True)
            d_oh = d_on * ng
            d_o = r * (d_oh - oh * jnp.mean(d_oh * oh, axis=1, keepdims=True))

            st16, ds16, do16, w16 = st.astype(BF16), ds_out.astype(BF16), d_o.astype(BF16), w.astype(BF16)
            q16, k16, t16 = cr["q16"], cr["k16"], cr["t16"]
            vn = u - _dot(w16, st16, 1, 0)
            d_vn = _dot(p, do16, 0, 0) + _dot(cr["kd"], ds16, 1, 0)
            d_qg = _dot(do16, st16, 1, 1)
            qgdo = _dot(cr["qg"], do16, 0, 0)
            yield
            vn16, dvn16 = vn.astype(BF16), d_vn.astype(BF16)
            d_p = jnp.where(cr["causal"], _dot(do16, vn16, 1, 1), 0.0)
            d_kd = _dot(vn16, ds16, 1, 1)
            d_gam_last = jnp.sum(jnp.sum(st * ds_out, axis=1, keepdims=True), axis=0, keepdims=True)
            d_w = -_dot(dvn16, st16, 1, 1)
            dstate[hh] = qgdo + ds_out * cr["gam_last"] - _dot(w16, dvn16, 0, 0)
            d_bv = _dot(t16, dvn16, 0, 0)
            yield
            d_kbg = _dot(t16, d_w, 0, 0)
            n_p = (d_p * dec).astype(BF16)
            d_q = _dot(n_p, k16, 1, 0) + d_qg * gam
            npq = _dot(n_p, q16, 0, 0)
            yield
            d_a = jnp.where(cr["strict"], -(_dot(d_bv, u, 1, 1) + _dot(d_kbg, w16, 1, 1)), 0.0)
            yield
            m_a = (d_a * dec).astype(BF16)
            d_kb = _dot(m_a, k16, 1, 0) + d_kbg * gam
            d_k = (_dot(m_a, cr["kb16"], 0, 0) + npq + d_kd * e + d_kb * b_col)
            yield
            d_v = d_bv * b_col
            d_beta_col = (jnp.sum(d_bv * v, axis=1, keepdims=True)
                          + jnp.sum(d_kb * k, axis=1, keepdims=True))
            gterm = d_a * cr["a_mat"] + d_p * p
            d_e = jnp.sum(d_kd * k, axis=1, keepdims=True) * e
            d_g_col = (jnp.sum(gterm, axis=1, keepdims=True)
                       + (jnp.sum(d_qg * q, axis=1, keepdims=True)
                          + jnp.sum(d_kbg * cr["kb"], axis=1, keepdims=True)) * gam
                       - d_e)
            d_g_last = jnp.sum(d_e, axis=0, keepdims=True) + d_gam_last * cr["gam_last"]
            d_g_row = (_col_to_row(d_g_col, eye) - jnp.sum(gterm, axis=0, keepdims=True)
                       + jnp.where(lane == A_CHUNK - 1, d_g_last, 0.0))
            dgc_ref[hh, seq_chunk, :] = d_g_row
            dbeta_ref[hh, seq_chunk, :] = _col_to_row(d_beta_col, eye)

            qh = cin["aq"] * cin["rq"]
            kh = cin["ak"] * cin["rk"]
            d_qh = d_q * (A_DK ** -0.5)
            d_aq = cin["rq"] * (d_qh - qh * jnp.sum(d_qh * qh, axis=1, keepdims=True))
            d_ak = cin["rk"] * (d_k - kh * jnp.sum(d_k * kh, axis=1, keepdims=True))
            d_y = jnp.concatenate([d_aq, d_ak, d_v], axis=1) * cin["slope"]
            shifted[hh, 0, 0:A_CHUNK, :] = d_y
            shifted[hh, 0, A_CHUNK:A_CHUNK + HALO, :] = dy_next[hh]
            shifted[hh, 1, 0:A_CHUNK + HALO, :] = cin["xw"]
            d_x = cw[A_CONV - 1:A_CONV, :] * d_y
            for j in range(1, A_CONV):
                d_x = d_x + cw[A_CONV - 1 - j:A_CONV - j, :] * shifted[hh, 0, j:j + A_CHUNK, :]
            for j in range(A_CONV):
                xs = shifted[hh, 1, HALO - j:HALO - j + A_CHUNK, :]
                dcw_ref[A_CONV - 1 - j:A_CONV - j, ccols] += jnp.sum(d_y * xs, axis=0, keepdims=True)
            dy_next[hh] = d_y[0:HALO, :]
            dx_ref[rows, lo:lo + A_CONV_COLS] = d_x.astype(BF16)
            dx_ref[rows, lo + A_CONV_COLS:lo + A_HEAD_COLS] = dz.astype(BF16)

        def chunk(ci, first):
            rows = pl.ds(0 if first else pl.multiple_of(ci * A_CHUNK, A_CHUNK), A_CHUNK)
            _round_robin([one_head(hh, ci, first, rows) for hh in range(hp)])

        lax.fori_loop(0, A_BLK_CHUNKS - 1, lambda i, c: (chunk(A_BLK_CHUNKS - 1 - i, False), c)[1], 0)
        chunk(0, True)

    rev = lambda j: nsb - 1 - j
    small = pl.BlockSpec((None, hp, n, A_CHUNK), lambda b, h, j: (b, h, 0, 0))
    wide = pl.BlockSpec((None, A_SEQ_BLK, hp * A_HEAD_COLS), lambda b, h, j: (b, rev(j), h))
    val = pl.BlockSpec((None, A_SEQ_BLK, hp * A_DV), lambda b, h, j: (b, rev(j), h))
    return pl.pallas_call(
        body, name="gdn_bwd", grid=(bn, A_HEADS // hp, nsb),
        in_specs=[wide,
                  pl.BlockSpec((None, None, HALO, hp * A_HEAD_COLS), lambda b, h, j: (b, rev(j), 0, h)),
                  pl.BlockSpec((A_CONV, hp * A_CONV_COLS), lambda b, h, j: (0, h)),
                  small, small,
                  pl.BlockSpec((1, A_DV), lambda b, h, j: (0, 0)),
                  val,
                  pl.BlockSpec((None, hp, A_BLK_CHUNKS, A_DK, A_DV), lambda b, h, j: (b, h, rev(j), 0, 0)),
                  pl.BlockSpec((None, hp, A_BLK_CHUNKS, A_CHUNK, A_CHUNK), lambda b, h, j: (b, h, rev(j), 0, 0)),
                  pl.BlockSpec((None, A_SEQ_BLK, hp * A_CONV_COLS), lambda b, h, j: (b, rev(j), h)),
                  val],
        out_specs=[wide, small, small,
                   pl.BlockSpec((None, A_CONV, hp * A_CONV_COLS), lambda b, h, j: (b, 0, h)),
                   pl.BlockSpec((None, hp, 8, A_DV), lambda b, h, j: (b, h, 0, 0))],
        out_shape=[jax.ShapeDtypeStruct((bn, s, A_HEADS * A_HEAD_COLS), BF16),
                   jax.ShapeDtypeStruct((bn, A_HEADS, n, A_CHUNK), F32),
                   jax.ShapeDtypeStruct((bn, A_HEADS, n, A_CHUNK), F32),
                   jax.ShapeDtypeStruct((bn, A_CONV, A_HEADS * A_CONV_COLS), F32),
                   jax.ShapeDtypeStruct((bn, A_HEADS, 8, A_DV), F32)],
        scratch_shapes=[pltpu.VMEM((hp, A_DK, A_DV), F32), pltpu.VMEM((hp, HALO, A_CONV_COLS), F32),
                        pltpu.VMEM((hp, 2, A_CHUNK + 2 * HALO, A_CONV_COLS), F32)],
        compiler_params=_params(("parallel", "parallel", "arbitrary"), VMEM_BIG),
    )(proj_hm, halo, cw_hm, beta, gc, norm_g, oraw, states, t_mats, conv_y, dog)


def _rope_tables(posf, inv_freq_row):
    t = posf.shape[0]
    tm = 512

    def body(p_ref, f_ref, c_ref, sa_ref, sb_ref):
        ang = p_ref[...] * f_ref[...]
        lane = lax.broadcasted_iota(jnp.int32, ang.shape, 1)
        half = ROPE_DIMS // 2
        c_ref[...] = jnp.where(lane < ROPE_DIMS, jnp.cos(ang), 1.0)
        sn = jnp.sin(ang)
        sa_ref[...] = jnp.where(lane < half, -sn, 0.0)
        sb_ref[...] = jnp.where((lane >= half) & (lane < ROPE_DIMS), sn, 0.0)

    row = pl.BlockSpec((tm, 128), lambda i: (i, 0))
    return pl.pallas_call(
        body, name="rope_tables", grid=(t // tm,),
        in_specs=[row, pl.BlockSpec((1, 128), lambda i: (0, 0))], out_specs=[row] * 3,
        out_shape=[jax.ShapeDtypeStruct((t, 128), F32)] * 3,
        compiler_params=_params(("parallel",)),
    )(posf, inv_freq_row)


def _qk_prep(proj, c, sa, sb, qg, kg, name, tm=512):
    t = proj.shape[0]

    def body(x_ref, c_ref, sa_ref, sb_ref, qg_ref, kg_ref, o_ref):
        cc, s1, s2 = c_ref[...], sa_ref[...], sb_ref[...]
        half = ROPE_DIMS // 2

        def one_head(lo, g):
            xv = x_ref[:, lo:lo + B_DH].astype(F32)
            ms = jnp.mean(xv * xv, axis=1, keepdims=True)
            yield
            xn = xv * lax.rsqrt(ms + EPS) * g
            r1, r2 = pltpu.roll(xn, 128 - half, 1), pltpu.roll(xn, half, 1)
            yield
            o_ref[:, lo:lo + B_DH] = (xn * cc + r1 * s1 + r2 * s2).astype(BF16)

        for which, g_ref in ((0, qg_ref), (1, kg_ref)):
            g = g_ref[...]
            _round_robin([one_head(which * B_W + h * B_DH, g) for h in range(B_HEADS)])
        o_ref[:, 2 * B_W:3 * B_W] = x_ref[:, 2 * B_W:3 * B_W]

    tab = pl.BlockSpec((tm, 128), lambda i: (i, 0))
    gain = pl.BlockSpec((1, B_DH), lambda i: (0, 0))
    return pl.pallas_call(
        body, name=name, grid=(t // tm,),
        in_specs=[pl.BlockSpec((tm, 3 * B_W), lambda i: (i, 0)), tab, tab, tab, gain, gain],
        out_specs=pl.BlockSpec((tm, 3 * B_W), lambda i: (i, 0)),
        out_shape=jax.ShapeDtypeStruct((t, 3 * B_W), BF16),
        compiler_params=_params(("parallel",), 40 * 1024 * 1024),
    )(proj, c, sa, sb, qg, kg)


def _qk_prep_bwd(proj, c, sa, sb, qg, kg, dq, dk, dv, dz, name, tm=512):
    t = proj.shape[0]
    out_w = 3 * B_W + (B_W if dz is not None else 0)

    def body(*refs):
        x_ref, c_ref, sa_ref, sb_ref, qg_ref, kg_ref, dq_ref, dk_ref, dv_ref = refs[:9]
        if dz is not None:
            dz_ref, o_ref, dgain_ref = refs[9:]
        else:
            o_ref, dgain_ref = refs[9:]
        i = pl.program_id(0)

        @pl.when(i == 0)
        def _():
            dgain_ref[...] = jnp.zeros_like(dgain_ref)

        cc, s1, s2 = c_ref[...], sa_ref[...], sb_ref[...]
        half = ROPE_DIMS // 2

        def one_head(which, h, g, d_ref, parts):
            lo = which * B_W + h * B_DH
            xv = x_ref[:, lo:lo + B_DH].astype(F32)
            d_out = d_ref[:, h * B_DH:(h + 1) * B_DH].astype(F32)
            ms = jnp.mean(xv * xv, axis=1, keepdims=True)
            r1, r2 = pltpu.roll(d_out * s1, half, 1), pltpu.roll(d_out * s2, 128 - half, 1)
            yield
            r = lax.rsqrt(ms + EPS)
            xh = xv * r
            d_xn = d_out * cc + r1 + r2
            parts.append(jnp.sum(d_xn * xh, axis=0, keepdims=True))
            d_xh = d_xn * g
            dot = jnp.mean(d_xh * xh, axis=1, keepdims=True)
            yield
            o_ref[:, lo:lo + B_DH] = (r * (d_xh - xh * dot)).astype(BF16)

        for which, g_ref, d_ref in ((0, qg_ref, dq_ref), (1, kg_ref, dk_ref)):
            parts = []
            _round_robin([one_head(which, h, g_ref[...], d_ref, parts) for h in range(B_HEADS)])
            acc = parts[0]
            for part in parts[1:]:
                acc = acc + part
            dgain_ref[which:which + 1, :] += acc
        o_ref[:, 2 * B_W:3 * B_W] = dv_ref[...]
        if dz is not None:
            o_ref[:, 3 * B_W:4 * B_W] = dz_ref[...]

    tab = pl.BlockSpec((tm, 128), lambda i: (i, 0))
    gain = pl.BlockSpec((1, B_DH), lambda i: (0, 0))
    grad = pl.BlockSpec((tm, B_W), lambda i: (i, 0))
    in_specs = [pl.BlockSpec((tm, 2 * B_W), lambda i: (i, 0)), tab, tab, tab, gain, gain, grad, grad, grad]
    args = [proj, c, sa, sb, qg, kg, dq, dk, dv]
    if dz is not None:
        in_specs.append(grad)
        args.append(dz)
    return pl.pallas_call(
        body, name=name, grid=(t // tm,), in_specs=in_specs,
        out_specs=[pl.BlockSpec((tm, out_w), lambda i: (i, 0)), pl.BlockSpec((8, B_DH), lambda i: (0, 0))],
        out_shape=[jax.ShapeDtypeStruct((t, out_w), BF16), jax.ShapeDtypeStruct((8, B_DH), F32)],
        compiler_params=_params(("arbitrary",), 40 * 1024 * 1024),
    )(*args)


def _attn_masks():
    qi = lax.broadcasted_iota(jnp.int32, (B_BLK, 2 * B_BLK), 0)
    kj = lax.broadcasted_iota(jnp.int32, (B_BLK, 2 * B_BLK), 1)
    two = (kj >= qi) & (kj <= qi + B_BLK)
    q1 = lax.broadcasted_iota(jnp.int32, (B_BLK, B_BLK), 0)
    k1 = lax.broadcasted_iota(jnp.int32, (B_BLK, B_BLK), 1)
    return k1 <= q1, two


def _lane_pick(ref_rows, h):
    lane = lax.broadcasted_iota(jnp.int32, ref_rows.shape, 1)
    return jnp.sum(jnp.where(lane == h, ref_rows, 0.0), axis=1, keepdims=True)


B_ROWS = 2048


def _attn_schedule(nb, sb, block):
    way = 16

    def run(items):
        for at in range(0, len(items), way):
            _round_robin([block(*it) for it in items[at:at + way]])

    run([(si, 0, True) for si in range(sb)])
    if nb == 1:
        return
    per = max(1, way // sb)
    lead = 1 + (nb - 1) % per
    if lead > 1:
        run([(si, i, False) for i in range(1, lead) for si in range(sb)])

    def step(it, carry):
        run([(si, lead + it * per + u, False) for u in range(per) for si in range(sb)])
        return carry

    lax.fori_loop(0, (nb - lead) // per, step, 0)


def _attn_rows(i, first):
    if first:
        return pl.ds(0, B_BLK), pl.ds(0, B_BLK)
    rows = pl.ds(pl.multiple_of(i * B_BLK, B_BLK), B_BLK)
    return rows, pl.ds(pl.multiple_of((i - 1) * B_BLK, B_BLK), 2 * B_BLK)


def _attn_fwd(qkv, name):
    ns, ln, _ = qkv.shape
    nb = ln // B_BLK
    sb = B_ROWS // ln
    scale = B_DH ** -0.5

    def body(q_ref, k_ref, v_ref, o_ref, lse_ref):
        h = pl.program_id(1)
        mask1, mask2 = _attn_masks()
        lane = lax.broadcasted_iota(jnp.int32, (B_BLK, B_HEADS), 1)

        @pl.when(h == 0)
        def _():
            lse_ref[...] = jnp.zeros_like(lse_ref)

        def block(si, i, first):
            rows, win = _attn_rows(i, first)
            mask = mask1 if first else mask2
            sc = jnp.where(mask, _dot(q_ref[si, rows, :], k_ref[si, win, :], 1, 1) * scale, -1e30)
            yield
            m = jnp.max(sc, axis=1, keepdims=True)
            p = jnp.exp(sc - m)
            l = jnp.sum(p, axis=1, keepdims=True)
            pv = _dot(p, v_ref[si, win, :], 1, 0)
            yield
            o_ref[si, rows, :] = (pv / l).astype(BF16)
            lse_ref[si, rows, :] = jnp.where(lane == h, m + jnp.log(l), lse_ref[si, rows, :])

        _attn_schedule(nb, sb, block)

    head = lambda off: pl.BlockSpec((sb, ln, B_DH), lambda s, h: (s, 0, off + h))
    return pl.pallas_call(
        body, name=name, grid=(ns // sb, B_HEADS),
        in_specs=[head(0), head(B_HEADS), head(2 * B_HEADS)],
        out_specs=[head(0), pl.BlockSpec((sb, ln, B_HEADS), lambda s, h: (s, 0, 0))],
        out_shape=[jax.ShapeDtypeStruct((ns, ln, B_W), BF16), jax.ShapeDtypeStruct((ns, ln, B_HEADS), F32)],
        compiler_params=_params(("parallel", "arbitrary")),
    )(qkv, qkv, qkv)


def _attn_bwd(qkv, d_o, lse_joint, delta, name):
    ns, ln, _ = qkv.shape
    nb = ln // B_BLK
    sb = B_ROWS // ln
    scale = B_DH ** -0.5

    def body(q_ref, k_ref, v_ref, do_ref, lj_ref, dl_ref, dq_ref, dk_out, dv_out, dk_ref, dv_ref):
        h = pl.program_id(1)
        mask1, mask2 = _attn_masks()
        dk_ref[...] = jnp.zeros_like(dk_ref)
        dv_ref[...] = jnp.zeros_like(dv_ref)

        def block(si, i, first):
            rows, win = _attn_rows(i, first)
            mask = mask1 if first else mask2
            q = q_ref[si, rows, :]
            d_out = do_ref[si, rows, :]
            l_col = _lane_pick(lj_ref[si, rows, :], h)
            d_col = _lane_pick(dl_ref[si, rows, :], h)
            sc = _dot(q, k_ref[si, win, :], 1, 1) * scale
            d_p = _dot(d_out, v_ref[si, win, :], 1, 1)
            yield
            p = jnp.exp(jnp.where(mask, sc - l_col, -1e30))
            d_s = p * (d_p - d_col) * scale
            d_q = _dot(d_s, k_ref[si, win, :], 1, 0)
            d_k = _dot(d_s, q, 0, 0)
            d_v = _dot(p, d_out, 0, 0)
            yield
            dq_ref[si, rows, :] = d_q.astype(BF16)
            dk_ref[si, win, :] += d_k
            dv_ref[si, win, :] += d_v

        _attn_schedule(nb, sb, block)
        dk_out[...] = dk_ref[...].astype(BF16)
        dv_out[...] = dv_ref[...].astype(BF16)

    head = lambda off: pl.BlockSpec((sb, ln, B_DH), lambda s, h: (s, 0, off + h))
    small = pl.BlockSpec((sb, ln, B_HEADS), lambda s, h: (s, 0, 0))
    return pl.pallas_call(
        body, name=name, grid=(ns // sb, B_HEADS),
        in_specs=[head(0), head(B_HEADS), head(2 * B_HEADS), head(0), small, small],
        out_specs=[head(0)] * 3,
        out_shape=[jax.ShapeDtypeStruct((ns, ln, B_W), BF16)] * 3,
        scratch_shapes=[pltpu.VMEM((sb, ln, B_DH), F32)] * 2,
        compiler_params=_params(("parallel", "parallel")),
    )(qkv, qkv, qkv, d_o, lse_joint, delta)


def _merge_weights(lse_refs):
    ls = [r[...] for r in lse_refs]
    m = jnp.maximum(jnp.maximum(ls[0], ls[1]), ls[2])
    es = [jnp.exp(l - m) for l in ls]
    tot = es[0] + es[1] + es[2]
    return [e / tot for e in es], m + jnp.log(tot)


def _merge_fwd(outs, lses, proj0, tm=512):
    t = outs[0].shape[0]

    def body(o0, o1, o2, l0, l1, l2, z_ref, og_ref):
        wts, _ = _merge_weights((l0, l1, l2))

        def one_head(h):
            cols = slice(h * B_DH, (h + 1) * B_DH)
            w0, w1, w2 = (jnp.broadcast_to(w[:, h:h + 1], (tm, B_DH)) for w in wts)
            yield
            o = w0 * o0[:, cols] + w1 * o1[:, cols] + w2 * o2[:, cols]
            og_ref[:, cols] = (o * _silu(z_ref[:, cols].astype(F32))).astype(BF16)

        _round_robin([one_head(h) for h in range(B_HEADS)])

    wide = pl.BlockSpec((tm, B_W), lambda i: (i, 0))
    small = pl.BlockSpec((tm, B_HEADS), lambda i: (i, 0))
    return pl.pallas_call(
        body, name="merge_fwd", grid=(t // tm,),
        in_specs=[wide] * 3 + [small] * 3 + [pl.BlockSpec((tm, B_W), lambda i: (i, 3))],
        out_specs=wide, out_shape=jax.ShapeDtypeStruct((t, B_W), BF16),
        compiler_params=_params(("parallel",)),
    )(*outs, *lses, proj0)


def _merge_bwd(outs, lses, proj0, d_og, tm=512):
    t = outs[0].shape[0]

    def body(o0, o1, o2, l0, l1, l2, z_ref, dog_ref, do_ref, lj_ref, dl_ref, dz_ref):
        wts, lj = _merge_weights((l0, l1, l2))
        lj_ref[...] = lj
        lane = lax.broadcasted_iota(jnp.int32, (tm, B_HEADS), 1)
        sums = [None] * B_HEADS

        def one_head(h):
            cols = slice(h * B_DH, (h + 1) * B_DH)
            w0, w1, w2 = (jnp.broadcast_to(w[:, h:h + 1], (tm, B_DH)) for w in wts)
            yield
            o = w0 * o0[:, cols] + w1 * o1[:, cols] + w2 * o2[:, cols]
            z = z_ref[:, cols].astype(F32)
            d_g = dog_ref[:, cols].astype(F32)
            gate, gate_slope = _silu_and_slope(z)
            d_out = d_g * gate
            dz_ref[:, cols] = (d_g * o * gate_slope).astype(BF16)
            do_ref[:, cols] = d_out.astype(BF16)
            sums[h] = jnp.sum(d_out * o, axis=1, keepdims=True)
            yield

        _round_robin([one_head(h) for h in range(B_HEADS)])
        delta = jnp.zeros((tm, B_HEADS), F32)
        for h in range(B_HEADS):
            delta = jnp.where(lane == h, sums[h], delta)
        dl_ref[...] = delta

    wide = pl.BlockSpec((tm, B_W), lambda i: (i, 0))
    small = pl.BlockSpec((tm, B_HEADS), lambda i: (i, 0))
    return pl.pallas_call(
        body, name="merge_bwd", grid=(t // tm,),
        in_specs=[wide] * 3 + [small] * 3 + [pl.BlockSpec((tm, B_W), lambda i: (i, 3)), wide],
        out_specs=[wide, small, small, wide],
        out_shape=[jax.ShapeDtypeStruct((t, B_W), BF16), jax.ShapeDtypeStruct((t, B_HEADS), F32),
                   jax.ShapeDtypeStruct((t, B_HEADS), F32), jax.ShapeDtypeStruct((t, B_W), BF16)],
        compiler_params=_params(("parallel",)),
    )(*outs, *lses, proj0, d_og)


def _adamw(w, g, m, v, name):
    r, c = w.shape
    tr = r
    for cand in (256, 128, 64, 32, 16, 8):
        if r % cand == 0:
            tr = cand
            break

    def body(w_ref, g_ref, m_ref, v_ref, d_ref, nm_ref, nv_ref):
        gv = g_ref[...]
        nm = ADAM_B1 * m_ref[...] + (1.0 - ADAM_B1) * gv
        nv = ADAM_B2 * v_ref[...] + (1.0 - ADAM_B2) * (gv * gv)
        m_hat = nm / (1.0 - ADAM_B1 ** ADAM_STEP)
        v_hat = nv / (1.0 - ADAM_B2 ** ADAM_STEP)
        d_ref[...] = -ADAM_LR * (m_hat / (jnp.sqrt(v_hat) + ADAM_EPS) + ADAM_WD * w_ref[...])
        nm_ref[...] = nm
        nv_ref[...] = nv

    blk = pl.BlockSpec((tr, c), lambda i: (i, 0))
    return pl.pallas_call(
        body, name=name, grid=(r // tr,), in_specs=[blk] * 4, out_specs=[blk] * 3,
        out_shape=[jax.ShapeDtypeStruct((r, c), F32)] * 3,
        compiler_params=_params(("parallel",)),
    )(w, g, m, v)


def _adam_update(w, gv, m, v):
    nm = ADAM_B1 * m + (1.0 - ADAM_B1) * gv
    nv = ADAM_B2 * v + (1.0 - ADAM_B2) * (gv * gv)
    m_hat = nm / (1.0 - ADAM_B1 ** ADAM_STEP)
    v_hat = nv / (1.0 - ADAM_B2 ** ADAM_STEP)
    return -ADAM_LR * (m_hat / (jnp.sqrt(v_hat) + ADAM_EPS) + ADAM_WD * w), nm, nv


def _adamw_shard(w, mine, theirs, m, v, half_index, name, tr=128):
    _, r, c = w.shape
    nhb = (r // 2) // tr

    def body(c_ref, w_ref, mine_ref, theirs_ref, m_ref, v_ref, g_ref, d_ref, nm_ref, nv_ref):
        is_mine = (pl.program_id(0) // nhb) == c_ref[0]
        gv = jnp.where(is_mine, mine_ref[...], theirs_ref[...])
        d, nm, nv = _adam_update(w_ref[...], gv, m_ref[...], v_ref[...])
        g_ref[...] = gv
        d_ref[...] = d
        nm_ref[...] = nm
        nv_ref[...] = nv

    full = pl.BlockSpec((None, tr, c), lambda i, cc: (0, i, 0))
    half = pl.BlockSpec((tr, c), lambda i, cc: (i % nhb, 0))
    return pl.pallas_call(
        body, name=name,
        grid_spec=pltpu.PrefetchScalarGridSpec(
            num_scalar_prefetch=1, grid=(2 * nhb,),
            in_specs=[full, half, half, full, full], out_specs=[full] * 4),
        out_shape=[jax.ShapeDtypeStruct(w.shape, F32)] * 4,
        compiler_params=_params(("parallel",), 40 * 1024 * 1024),
    )(half_index, w, mine, theirs, m, v)


def _adamw_shard_cols(w, mine, theirs, m, v, half_index, name, steps=20):
    c, _, r = w.shape
    tc = c // steps
    assert tc * steps == c

    def body(c_ref, w_ref, mine_ref, theirs_ref, m_ref, v_ref, g_ref, d_ref, nm_ref, nv_ref):
        first = jnp.where(c_ref[0] == 0, mine_ref[...], theirs_ref[...])
        second = jnp.where(c_ref[0] == 0, theirs_ref[...], mine_ref[...])
        for lo, gv in ((0, first), (r // 2, second)):
            cols = slice(lo, lo + r // 2)
            d, nm, nv = _adam_update(w_ref[:, :, cols], gv, m_ref[:, :, cols], v_ref[:, :, cols])
            g_ref[:, :, cols] = gv
            d_ref[:, :, cols] = d
            nm_ref[:, :, cols] = nm
            nv_ref[:, :, cols] = nv

    full = pl.BlockSpec((tc, 1, r), lambda i, cc: (i, 0, 0))
    half = pl.BlockSpec((tc, 1, r // 2), lambda i, cc: (i, 0, 0))
    return pl.pallas_call(
        body, name=name,
        grid_spec=pltpu.PrefetchScalarGridSpec(
            num_scalar_prefetch=1, grid=(steps,),
            in_specs=[full, half, half, full, full], out_specs=[full] * 4),
        out_shape=[jax.ShapeDtypeStruct(w.shape, F32)] * 4,
        compiler_params=_params(("parallel",), 40 * 1024 * 1024),
    )(half_index, w, mine, theirs, m, v)


def _pair_sum(own, other, half_index, name, tr=256):
    _, r, c = own.shape
    rh = r // 2
    tr = min(tr, rh)
    nrb = rh // tr

    def body(c_ref, own_ref, oth_ref, out_ref):
        out_ref[...] = (own_ref[...] + oth_ref[...].astype(F32)).astype(BF16)

    return pl.pallas_call(
        body, name=name,
        grid_spec=pltpu.PrefetchScalarGridSpec(
            num_scalar_prefetch=1, grid=(N_CHIPS, nrb),
            in_specs=[pl.BlockSpec((None, tr, c), lambda k, i, cc: (k, cc[0] * nrb + i, 0)),
                      pl.BlockSpec((None, tr, c), lambda k, i, cc: (k, i, 0))],
            out_specs=pl.BlockSpec((None, tr, c), lambda k, i, cc: (k, i, 0))),
        out_shape=jax.ShapeDtypeStruct((N_CHIPS, rh, c), BF16),
        compiler_params=_params(("parallel", "parallel")),
    )(half_index, own, other)


def _chip_sum(sums, others, chip_index, name, tr=256):
    _, r, c = sums.shape
    tr = min(tr, r)

    def body(k_ref, own_ref, oth_ref, out_ref):
        acc = own_ref[...].astype(F32)
        for j in range(N_CHIPS - 1):
            acc = acc + oth_ref[j].astype(F32)
        out_ref[...] = acc

    return pl.pallas_call(
        body, name=name,
        grid_spec=pltpu.PrefetchScalarGridSpec(
            num_scalar_prefetch=1, grid=(r // tr,),
            in_specs=[pl.BlockSpec((None, tr, c), lambda i, kk: (kk[0], i, 0)),
                      pl.BlockSpec((N_CHIPS - 1, tr, c), lambda i, kk: (0, i, 0))],
            out_specs=pl.BlockSpec((tr, c), lambda i, kk: (i, 0))),
        out_shape=jax.ShapeDtypeStruct((r, c), F32),
        compiler_params=_params(("parallel",)),
    )(chip_index, sums, others)


HBM = pl.BlockSpec(memory_space=pltpu.HBM)


def _place():
    x, y, c = lax.axis_index("x"), lax.axis_index("y"), lax.axis_index("c")
    chips = [(1 - x, y), (x, 1 - y), (1 - x, 1 - y)]
    return x, y, c, chips


def _sibling_forward(land):
    def body(in_ref, out_ref, send, recv):
        x, y, c, chips = _place()
        rh = out_ref.shape[1] // 2
        cps = []
        for j, (px, py) in enumerate(chips):
            slot = out_ref.at[2 * px + py, pl.ds(c * rh, rh)]
            cp = pltpu.make_async_remote_copy(
                src_ref=slot, dst_ref=slot, send_sem=send.at[j], recv_sem=recv.at[j],
                device_id=(x, y, 1 - c), device_id_type=MESH)
            cp.start()
            cps.append(cp)
        for j, (px, py) in enumerate(chips):
            slot = out_ref.at[2 * px + py, pl.ds((1 - c) * rh, rh)]
            pltpu.make_async_remote_copy(
                src_ref=slot, dst_ref=slot, send_sem=send.at[j], recv_sem=recv.at[j],
                device_id=(x, y, 1 - c), device_id_type=MESH).wait_recv()
        for cp in cps:
            cp.wait_send()

    return pl.pallas_call(
        body, name="first_weights_sibling_forward", in_specs=[HBM], out_specs=HBM,
        out_shape=jax.ShapeDtypeStruct(land.shape, land.dtype), input_output_aliases={0: 0},
        scratch_shapes=[pltpu.SemaphoreType.DMA((3,)), pltpu.SemaphoreType.DMA((3,))],
    )(land)


def _sibling_swap(halves, name):
    na = len(halves)

    def body(*refs):
        ins, outs = refs[:na], refs[na:2 * na]
        send, recv = refs[2 * na:]
        x, y, c, _ = _place()
        cps = []
        for i in range(na):
            cp = pltpu.make_async_remote_copy(
                src_ref=ins[i], dst_ref=outs[i], send_sem=send.at[i], recv_sem=recv.at[i],
                device_id=(x, y, 1 - c), device_id_type=MESH)
            cp.start()
            cps.append(cp)
        for cp in cps:
            cp.wait()

    out_shape = [jax.ShapeDtypeStruct(h.shape, h.dtype) for h in halves]
    return pl.pallas_call(
        body, name=name, in_specs=[HBM] * na, out_specs=[HBM] * na, out_shape=out_shape,
        scratch_shapes=[pltpu.SemaphoreType.DMA((na,)), pltpu.SemaphoreType.DMA((na,))],
    )(*halves)


SEM = pl.BlockSpec(memory_space=pltpu.SEMAPHORE)
ANY = pl.BlockSpec(memory_space=pl.ANY)
EFFECT = pltpu.SideEffectType.DATAFLOW_SIDE_EFFECTING


def _split_copy_start(name, plan, srcs, lands, after):
    ns, nl = len(srcs), len(lands)

    def body(*refs):
        src_refs, land_refs = refs[:ns], refs[ns:ns + nl]
        send, recv = refs[ns + nl + 1], refs[ns + nl + 2]
        token = refs[-1]
        outgoing, _ = plan(src_refs, land_refs)
        for src, dst, dev, si, ri in outgoing:
            pltpu.make_async_remote_copy(src_ref=src, dst_ref=dst, send_sem=send.at[si], recv_sem=recv.at[ri],
                                         device_id=dev, device_id_type=MESH).start()
        token[...] = jnp.zeros_like(token)

    n_out, n_in = plan.counts
    thru = [pltpu.HBM(a.shape, a.dtype) for a in list(srcs) + list(lands)]
    res = pl.pallas_call(
        body, name=name,
        out_shape=[pltpu.SemaphoreType.DMA((n_out,)), pltpu.SemaphoreType.DMA((n_in,))] + thru
        + [jax.ShapeDtypeStruct((8, 128), F32)],
        in_specs=[HBM] * (ns + nl) + [ANY],
        out_specs=[SEM, SEM] + [HBM] * (ns + nl) + [pl.BlockSpec(memory_space=pltpu.VMEM)],
        input_output_aliases={i: 2 + i for i in range(ns + nl)},
        compiler_params=pltpu.CompilerParams(has_side_effects=EFFECT),
    )(*[pltpu.with_memory_space_constraint(a, pltpu.HBM) for a in list(srcs) + list(lands)], after)
    return res[0], res[1], res[2:2 + ns], res[2 + ns:2 + ns + nl], res[-1]


def _split_copy_wait(name, plan, send, recv, srcs, lands, after):
    ns, nl = len(srcs), len(lands)
    after = list(after) if isinstance(after, (list, tuple)) else [after]

    def body(*refs):
        src_refs, land_refs = refs[:ns], refs[ns:ns + nl]
        send_ref, recv_ref = refs[ns + nl], refs[ns + nl + 1]
        outgoing, arrivals = plan(src_refs, land_refs)
        for src, dst, dev, si, ri in outgoing:
            pltpu.make_async_remote_copy(src_ref=src, dst_ref=dst, send_sem=send_ref.at[si], recv_sem=recv_ref.at[ri],
                                         device_id=dev, device_id_type=MESH).wait_send()
        for view, ri in arrivals:
            pltpu.make_async_remote_copy(src_ref=view, dst_ref=view, send_sem=send_ref.at[0], recv_sem=recv_ref.at[ri],
                                         device_id=_place()[:3], device_id_type=MESH).wait_recv()

    thru = [pltpu.HBM(a.shape, a.dtype) for a in list(srcs) + list(lands)]
    res = pl.pallas_call(
        body, name=name, out_shape=thru,
        in_specs=[HBM] * (ns + nl) + [SEM, SEM] + [ANY] * len(after), out_specs=[HBM] * (ns + nl),
        input_output_aliases={i: i for i in range(ns + nl)},
        compiler_params=pltpu.CompilerParams(has_side_effects=EFFECT),
    )(*srcs, *lands, send, recv, *after)
    return res[:ns], res[ns:]


def _gather_plan(n_arrays):
    def plan(src_refs, land_refs):
        x, y, c, chips = _place()
        me = 2 * x + y
        outgoing, arrivals = [], []
        for i in range(n_arrays):
            rh = src_refs[i].shape[0] // 2
            mine = pl.ds(c * rh, rh)
            for j, (px, py) in enumerate(chips):
                for delta in range(2):
                    tc = c ^ delta
                    outgoing.append((src_refs[i].at[mine], land_refs[i].at[me, mine], (px, py, tc),
                                     6 * i + 2 * j + delta, 6 * i + 2 * j + delta))
                    theirs = pl.ds(tc * rh, rh)
                    arrivals.append((land_refs[i].at[2 * px + py, theirs], 6 * i + 2 * j + delta))
        return outgoing, arrivals

    plan.counts = (6 * n_arrays, 6 * n_arrays)
    return plan


def _first_gather_plan():
    def plan(src_refs, land_refs):
        x, y, c, chips = _place()
        me = 2 * x + y
        rh = src_refs[0].shape[0] // 2
        mine = pl.ds(c * rh, rh)
        outgoing, arrivals = [], []
        for j, (px, py) in enumerate(chips):
            outgoing.append((src_refs[0].at[mine], land_refs[0].at[me, mine], (px, py, c), j, j))
            arrivals.append((land_refs[0].at[2 * px + py, mine], j))
            outgoing.append((src_refs[1], land_refs[1].at[me], (px, py, c), 3 + j, 3 + j))
            arrivals.append((land_refs[1].at[2 * px + py], 3 + j))
        return outgoing, arrivals

    plan.counts = (6, 6)
    return plan


def _exchange_plan(n_arrays):
    def plan(src_refs, land_refs):
        x, y, c, chips = _place()
        outgoing, arrivals = [], []
        for i in range(n_arrays):
            for j, (px, py) in enumerate(chips):
                outgoing.append((src_refs[i].at[2 * px + py], land_refs[i].at[j], (px, py, c), 3 * i + j, 3 * i + j))
                arrivals.append((land_refs[i].at[j], 3 * i + j))
        return outgoing, arrivals

    plan.counts = (3 * n_arrays, 3 * n_arrays)
    return plan


def _small_allreduce(vec):
    r, cdim = vec.shape
    n_dev = 8

    def body(v_ref, out_ref, buf, send, recv):
        x, y, c, _ = _place()
        me = 4 * x + 2 * y + c
        buf[me] = v_ref[...]
        cps = []
        for k in range(1, n_dev):
            dx, dy, dc = (k >> 2) & 1, (k >> 1) & 1, k & 1
            peer = (x ^ dx, y ^ dy, c ^ dc)
            cp = pltpu.make_async_remote_copy(
                src_ref=v_ref, dst_ref=buf.at[me], send_sem=send.at[k - 1], recv_sem=recv.at[k - 1],
                device_id=peer, device_id_type=MESH)
            cp.start()
            cps.append(cp)
        for k in range(1, n_dev):
            dx, dy, dc = (k >> 2) & 1, (k >> 1) & 1, k & 1
            src = 4 * (x ^ dx) + 2 * (y ^ dy) + (c ^ dc)
            slot = buf.at[src]
            pltpu.make_async_remote_copy(
                src_ref=slot, dst_ref=slot, send_sem=send.at[k - 1], recv_sem=recv.at[k - 1],
                device_id=(x ^ dx, y ^ dy, c ^ dc), device_id_type=MESH).wait_recv()
        for cp in cps:
            cp.wait_send()
        acc = buf[0]
        for k in range(1, n_dev):
            acc = acc + buf[k]
        out_ref[...] = acc

    vm = pl.BlockSpec(memory_space=pltpu.VMEM)
    return pl.pallas_call(
        body, name="small_allreduce", in_specs=[vm], out_specs=vm,
        out_shape=jax.ShapeDtypeStruct((r, cdim), F32),
        scratch_shapes=[pltpu.VMEM((n_dev, r, cdim), F32), pltpu.SemaphoreType.DMA((n_dev - 1,)),
                        pltpu.SemaphoreType.DMA((n_dev - 1,))],
    )(vec)


def _a_cols_to_head_major(w):
    lead = w.shape[:-1]
    q = w[..., :A_QK].reshape(lead + (A_HEADS, A_DK))
    k = w[..., A_QK:2 * A_QK].reshape(lead + (A_HEADS, A_DK))
    v = w[..., 2 * A_QK:2 * A_QK + A_VW].reshape(lead + (A_HEADS, A_DV))
    z = w[..., 2 * A_QK + A_VW:].reshape(lead + (A_HEADS, A_DV))
    return jnp.concatenate([q, k, v, z], axis=-1).reshape(lead + (A_HEADS * A_HEAD_COLS,))


def _a_cols_from_head_major(w):
    lead = w.shape[:-1]
    w = w.reshape(lead + (A_HEADS, A_HEAD_COLS))
    parts = [w[..., :A_DK], w[..., A_DK:2 * A_DK], w[..., 2 * A_DK:2 * A_DK + A_DV], w[..., 2 * A_DK + A_DV:]]
    return jnp.concatenate([p.reshape(lead + (-1,)) for p in parts], axis=-1)


def _conv_cols_to_head_major(w):
    lead = w.shape[:-1]
    q = w[..., :A_QK].reshape(lead + (A_HEADS, A_DK))
    k = w[..., A_QK:2 * A_QK].reshape(lead + (A_HEADS, A_DK))
    v = w[..., 2 * A_QK:].reshape(lead + (A_HEADS, A_DV))
    return jnp.concatenate([q, k, v], axis=-1).reshape(lead + (A_HEADS * A_CONV_COLS,))


def _conv_cols_from_head_major(w):
    lead = w.shape[:-1]
    w = w.reshape(lead + (A_HEADS, A_CONV_COLS))
    parts = [w[..., :A_DK], w[..., A_DK:2 * A_DK], w[..., 2 * A_DK:]]
    return jnp.concatenate([p.reshape(lead + (-1,)) for p in parts], axis=-1)


def _to_stream(a, bn, d):
    rest = a.shape[1:]
    s = a.shape[0] // bn
    a = a.reshape((bn, s // d, d) + rest)
    a = jnp.swapaxes(a, 1, 2)
    return a.reshape((bn * d, s // d) + rest)


def _from_stream(a, bn, d):
    rest = a.shape[2:]
    ln = a.shape[1]
    a = a.reshape((bn, d, ln) + rest)
    a = jnp.swapaxes(a, 1, 2)
    return a.reshape((bn * ln * d,) + rest)


B_SUB = 512
B_SHARD_BLOCKS = (3 * B_GROUPS * B_W + B_W) // N_CHIPS // B_SUB


def _b_block(gi, jj):
    nb = (B_GROUPS * (jj // 2) + gi) * 2 + jj % 2
    return nb // B_SHARD_BLOCKS, nb % B_SHARD_BLOCKS


def _shard_major(g, ncols):
    r = g.shape[0]
    return jnp.swapaxes(g.reshape(r, N_CHIPS, ncols), 0, 1)


def _pack_rows(items):
    rows, offs = [], []
    at = 0
    for a in items:
        flat = a.reshape(-1).astype(F32)
        nr = -(-flat.shape[0] // 1024) * 8
        flat = jnp.pad(flat, (0, nr * 128 - flat.shape[0]))
        rows.append(flat.reshape(nr, 128))
        offs.append((at, nr, a.shape))
        at += nr
    return jnp.concatenate(rows, axis=0), offs


def _unpack_rows(packed, offs):
    out = []
    for at, nr, shape in offs:
        size = int(np.prod(shape)) if len(shape) else 1
        out.append(packed[at:at + nr].reshape(-1)[:size].reshape(shape))
    return out


def _local_step(x, positions, loss_target, norm_g, a_log, a_dt_bias, a_norm_g, b_q_norm_g, b_k_norm_g,
                start_token, first_weights, late_weights, b_grads_ready, a_grads_ready):
    bn, s, d = x.shape
    t = bn * s
    n_chunks = s // A_CHUNK
    x0 = x.reshape(t, d)
    h0 = _rms_fwd(x0, norm_g[0:1] + start_token, "rms0_fwd")
    inv_freq = ROPE_THETA ** (-jnp.arange(0, ROPE_DIMS, 2, dtype=F32) / ROPE_DIMS)
    freq_row = jnp.concatenate([inv_freq, inv_freq, jnp.zeros((128 - ROPE_DIMS,), F32)]).reshape(1, 128)
    posf = jnp.broadcast_to(positions.astype(F32).reshape(t, 1), (t, 128)) + start_token
    tabs = _rope_tables(posf, freq_row)
    tabs_s = [tabs if dil == 1 else [_to_stream(tb, bn, dil).reshape(t, 128) for tb in tabs] for dil in B_DIL]
    wa_in, conv_w, late_token = first_weights([h0] + [tb for ts in tabs_s for tb in ts])
    wa_main = _a_cols_to_head_major(wa_in[:, :A_MAIN])
    wa_tail = jnp.pad(wa_in[:, A_MAIN:], ((0, 0), (0, 128 - 2 * A_HEADS))) + late_token.astype(BF16)
    cw_hm = _conv_cols_to_head_major(conv_w)

    proj_a = _matmul(h0, wa_main, "nn", F32, "a_in_main", tm=2048)
    tail_a = _matmul(h0, wa_tail, "nn", F32, "a_in_tail")
    tail_t = jnp.swapaxes(tail_a[:, :2 * A_HEADS].reshape(bn, s, 2 * A_HEADS), 1, 2)
    tail_t = tail_t.reshape(bn, 2 * A_HEADS, n_chunks, A_CHUNK)
    beta, gc = _gdn_prep(tail_t, a_log[0], a_dt_bias[0])
    proj_a3 = proj_a.reshape(bn, s, A_MAIN)
    og_a, oraw_a, states, t_mats, conv_y = _gdn_fwd(proj_a3, cw_hm, beta, gc, a_norm_g)
    wa_out, wb_in, wb_out = late_weights(og_a)
    b_cols = [4 * B_W] + [3 * B_W] * (B_GROUPS - 1)
    x1, h1 = _out_proj(og_a.reshape(t, A_VW), wa_out, x0, "a_out", norm_g=norm_g[1:2])

    h1_s, proj_b, qkv_b, o_b, lse_b = [], [], [], [], []
    for gi, dil in enumerate(B_DIL):
        hs = h1 if dil == 1 else _to_stream(h1, bn, dil).reshape(t, d)
        ts = tabs_s[gi]
        pj = _matmul(hs, wb_in, "nn", BF16, f"b_in_g{gi}", tm=4096, tn=B_SUB, n=b_cols[gi], b_spec=pl.BlockSpec(
            (None, d, B_SUB), lambda i, j, kk, gi=gi: (_b_block(gi, j)[0], kk, _b_block(gi, j)[1])))
        qkv = _qk_prep(pj, *ts, b_q_norm_g[0, gi:gi + 1], b_k_norm_g[0, gi:gi + 1], f"qk_prep_g{gi}")
        o_s, lse_s = _attn_fwd(qkv.reshape(bn * dil, s // dil, 3 * B_W), f"attn_fwd_g{gi}")
        h1_s.append(hs), proj_b.append(pj), qkv_b.append(qkv)
        o_b.append(o_s.reshape(t, B_W) if dil == 1 else _from_stream(o_s, bn, dil))
        lse_b.append(lse_s.reshape(t, B_HEADS) if dil == 1 else _from_stream(lse_s, bn, dil))
    og_b = _merge_fwd(o_b, lse_b, proj_b[0])
    d_x2, loss_parts = _out_proj(og_b, wb_out, x1, "b_out_loss", target=loss_target.reshape(t, d))
    loss_local = jnp.sum(loss_parts)

    d_x2b = d_x2.astype(BF16)
    g_wb_out = _matmul(og_b, d_x2b, "tn", F32, "b_out_dw")
    d_og_b = _matmul(d_x2b, wb_out, "nt", BF16, "b_out_dx")
    d_o, lse_joint, delta, d_z = _merge_bwd(o_b, lse_b, proj_b[0], d_og_b)
    d_h1, g_qn, g_kn = [], [], []
    g_wb_in = lax.empty(wb_in.shape, F32)
    for gi, dil in enumerate(B_DIL):
        if dil == 1:
            do_s, lj_s, dl_s = d_o, lse_joint, delta
        else:
            do_s, lj_s, dl_s = (_to_stream(a, bn, dil).reshape(t, -1) for a in (d_o, lse_joint, delta))
        ns, ln = bn * dil, s // dil
        dq, dk, dv = _attn_bwd(qkv_b[gi].reshape(ns, ln, 3 * B_W), do_s.reshape(ns, ln, B_W),
                               lj_s.reshape(ns, ln, B_HEADS), dl_s.reshape(ns, ln, B_HEADS), f"attn_bwd_g{gi}")
        d_pj, d_gain = _qk_prep_bwd(proj_b[gi], *tabs_s[gi], b_q_norm_g[0, gi:gi + 1], b_k_norm_g[0, gi:gi + 1],
                                    dq.reshape(t, B_W), dk.reshape(t, B_W), dv.reshape(t, B_W),
                                    d_z if gi == 0 else None, f"qk_prep_bwd_g{gi}")
        g_wb_in = _matmul(h1_s[gi], d_pj, "tn", F32, f"b_in_dw_g{gi}", tn=B_SUB, tk=DW_K, into=(g_wb_in, pl.BlockSpec(
            (None, d, B_SUB), lambda i, j, kk, gi=gi: (_b_block(gi, j)[0], i, _b_block(gi, j)[1]))))
        dh = _matmul(d_pj, wb_in, "nt", BF16, f"b_in_dx_g{gi}", tm=2048, tk=B_SUB, n=d, b_spec=pl.BlockSpec(
            (None, d, B_SUB), lambda i, j, kk, gi=gi: (_b_block(gi, kk)[0], j, _b_block(gi, kk)[1])))
        d_h1.append(dh if dil == 1 else _from_stream(dh.reshape(ns, ln, d), bn, dil))
        g_qn.append(d_gain[0]), g_kn.append(d_gain[1])
    d_x1, g_norm1 = _rms_bwd(x1, norm_g[1:2], d_h1, d_x2, "rms1_bwd")

    d_x1b = d_x1.astype(BF16)
    g_wa_out = _matmul(og_a.reshape(t, A_VW), d_x1b, "tn", F32, "a_out_dw", tk=DW_K)
    b_token = b_grads_ready(g_wb_in, g_wb_out, g_wa_out)
    d_og_a = _matmul(d_x1b, wa_out, "nt", BF16, "a_out_dx")
    d_pa, d_gc, d_beta, d_cw, d_ng = _gdn_bwd(proj_a3, cw_hm, beta, gc, a_norm_g + b_token, oraw_a, states,
                                              t_mats, conv_y, d_og_a.reshape(bn, s, A_VW))
    d_tail_t, d_alog, d_dtb = _gdn_prep_bwd(tail_t, a_log[0], a_dt_bias[0], d_gc, d_beta)
    d_tail = jnp.swapaxes(d_tail_t.reshape(bn, 2 * A_HEADS, s), 1, 2).reshape(t, 2 * A_HEADS)
    d_tail = jnp.pad(d_tail, ((0, 0), (0, 128 - 2 * A_HEADS))).astype(BF16)
    d_pa = d_pa.reshape(t, A_MAIN)
    g_wa_main = _matmul(h0, d_pa, "tn", F32, "a_in_dw_main", tk=DW_K)
    g_wa_tail = _matmul(h0, d_tail, "tn", F32, "a_in_dw_tail", tk=DW_K)
    g_wa_in = jnp.concatenate([_a_cols_from_head_major(g_wa_main), g_wa_tail[:, :2 * A_HEADS]], axis=1)
    a_token = a_grads_ready(g_wa_in)
    d_h0t = _matmul(d_tail + a_token.astype(BF16), wa_tail, "nt", F32, "a_in_dx_tail")
    d_x0, g_norm0 = _in_proj_bwd(d_pa, wa_main, d_h0t, x0, norm_g[0:1], d_x1, "a_in_dx_rms0_bwd")

    gfull = {
        "norm_g": jnp.concatenate([g_norm0, g_norm1], axis=0), "a_w_in": g_wa_in,
        "a_conv_w": _conv_cols_from_head_major(jnp.sum(d_cw, axis=0)),
        "a_log": jnp.sum(d_alog[:, :, 0], axis=0), "a_dt_bias": jnp.sum(d_dtb[:, :, 0], axis=0),
        "a_norm_g": jnp.sum(d_ng[:, :, 0, :], axis=(0, 1)), "a_w_out": g_wa_out, "b_w_in": g_wb_in,
        "b_q_norm_g": jnp.stack(g_qn), "b_k_norm_g": jnp.stack(g_kn), "b_w_out": g_wb_out}
    return loss_local, d_x0.reshape(bn, s, d), gfull


def kernel(x, positions, norm_g, a_w_in, a_conv_w, a_log, a_dt_bias, a_norm_g, a_w_out, b_w_in, b_q_norm_g, b_k_norm_g, b_w_out, loss_target, m_norm_g, m_a_w_in, m_a_conv_w, m_a_log, m_a_dt_bias, m_a_norm_g, m_a_w_out, m_b_w_in, m_b_q_norm_g, m_b_k_norm_g, m_b_w_out, v_norm_g, v_a_w_in, v_a_conv_w, v_a_log, v_a_dt_bias, v_a_norm_g, v_a_w_out, v_b_w_in, v_b_q_norm_g, v_b_k_norm_g, v_b_w_out):
    d = x.shape[2]
    my_c = lax.axis_index("c")
    my_chip = 2 * lax.axis_index("x") + lax.axis_index("y")

    half_index = jnp.reshape(my_c, (1,)).astype(jnp.int32)
    chip_index = jnp.reshape(my_chip, (1,)).astype(jnp.int32)
    def landing(shard):
        return lax.dynamic_update_slice(lax.empty((N_CHIPS,) + shard.shape, shard.dtype), shard[None],
                                        (my_chip,) + (0,) * shard.ndim)

    first_shards = [a_w_in[0].astype(BF16), a_conv_w[0]]
    first_plan = _first_gather_plan()
    first = _split_copy_start("first_weights_start", first_plan, first_shards,
                              [landing(s) for s in first_shards], half_index)
    pending = {}
    late_shards = [(w[0] + first[4][0, 0]).astype(BF16) for w in (a_w_out, b_w_in, b_w_out)]
    late_lands = [landing(s) for s in late_shards]

    def first_weights(after):
        _, (ga_in, g_conv) = _split_copy_wait("first_weights_wait", first_plan, *first[:4],
                                              list(after) + late_lands)
        ga_in = _sibling_forward(ga_in)
        wa_in = jnp.concatenate([ga_in[k] for k in range(N_CHIPS)], axis=1)
        conv_w = jnp.concatenate([g_conv[k] for k in range(N_CHIPS)], axis=1)
        plan = _gather_plan(len(late_shards))
        pending["late"] = (plan,) + tuple(_split_copy_start(
            "late_weights_start", plan, late_shards, late_lands, conv_w))
        return wa_in, conv_w, pending["late"][5][0, 0]

    def late_weights(after):
        plan, send, recv, srcs, lands, _ = pending["late"]
        _, (ga_out, gb_in, gb_out) = _split_copy_wait("late_weights_wait", plan, send, recv, srcs, lands, after)
        return ga_out.reshape(A_VW, d), gb_in, gb_out.reshape(B_W, d)

    def reduce_to_chip_sums(mats, tag):
        half = lambda g: lax.dynamic_slice_in_dim(g, (1 - my_c) * (g.shape[1] // 2), g.shape[1] // 2, axis=1)
        recv_sib = _sibling_swap([half(g).astype(BF16) for g in mats], f"grad_{tag}_sibling_swap")
        return [_pair_sum(g, r, half_index, f"grad_{tag}_pair_sum_{i}") for i, (g, r) in enumerate(zip(mats, recv_sib))]

    def start_exchange(tag, mats):
        sums = reduce_to_chip_sums(mats, tag)
        lands = [lax.empty((N_CHIPS - 1,) + s.shape[1:], BF16) for s in sums]
        plan = _exchange_plan(len(mats))
        pending[tag] = (plan,) + tuple(_split_copy_start(f"grad_{tag}_exchange_start", plan, sums, lands, chip_index))
        return pending[tag][5][0, 0]

    def finish_exchange(tag, after):
        plan, send, recv, srcs, lands, _ = pending[tag]
        return _split_copy_wait(f"grad_{tag}_exchange_wait", plan, send, recv, srcs, lands, after)

    def b_grads_ready(g_wb_in, g_wb_out, g_wa_out):
        return start_exchange("b", [g_wb_in, g_wb_out.reshape(N_CHIPS, -1, d), g_wa_out.reshape(N_CHIPS, -1, d)])

    def a_grads_ready(g_wa_in):
        return start_exchange("a", [_shard_major(g_wa_in, a_w_in.shape[2])])

    loss_local, d_x0, gfull = _local_step(x, positions, loss_target, norm_g, a_log, a_dt_bias, a_norm_g,
                                          b_q_norm_g, b_k_norm_g, first[4][0, 0], first_weights, late_weights,
                                          b_grads_ready, a_grads_ready)

    small = [gfull["norm_g"], gfull["a_conv_w"], gfull["a_log"], gfull["a_dt_bias"], gfull["a_norm_g"],
             gfull["b_q_norm_g"], gfull["b_k_norm_g"], loss_local]
    packed, offs = _pack_rows(small)
    reduced = _small_allreduce(packed)
    g_norm, g_conv_all, g_alog, g_dtb, g_ang, g_q, g_k, loss = _unpack_rows(reduced, offs)
    g_conv_mine = lax.dynamic_slice_in_dim(g_conv_all, my_chip * a_conv_w.shape[2], a_conv_w.shape[2], axis=1)

    b_sums, b_received = finish_exchange("b", d_x0)
    a_sums, a_received = finish_exchange("a", reduced)
    chip_sums = [a_sums[0], b_sums[2], b_sums[0], b_sums[1]]
    received = [a_received[0], b_received[2], b_received[0], b_received[1]]
    halves = [_chip_sum(s, r, chip_index, f"grad_chip_sum_{i}") for i, (s, r) in enumerate(zip(chip_sums, received))]
    theirs = _sibling_swap(halves, "grad_sibling_join")
    big = ("a_w_in", "a_w_out", "b_w_in", "b_w_out")
    big_halves = dict(zip(big, zip(halves, theirs)))

    grads = {
        "norm_g": g_norm, "a_conv_w": g_conv_mine[None], "a_log": g_alog[None], "a_dt_bias": g_dtb[None],
        "a_norm_g": g_ang[None], "b_q_norm_g": g_q[None], "b_k_norm_g": g_k[None]}
    weights = {"norm_g": norm_g, "a_w_in": a_w_in, "a_conv_w": a_conv_w, "a_log": a_log, "a_dt_bias": a_dt_bias,
               "a_norm_g": a_norm_g, "a_w_out": a_w_out, "b_w_in": b_w_in, "b_q_norm_g": b_q_norm_g,
               "b_k_norm_g": b_k_norm_g, "b_w_out": b_w_out}
    m_in = {"norm_g": m_norm_g, "a_w_in": m_a_w_in, "a_conv_w": m_a_conv_w, "a_log": m_a_log,
            "a_dt_bias": m_a_dt_bias, "a_norm_g": m_a_norm_g, "a_w_out": m_a_w_out, "b_w_in": m_b_w_in,
            "b_q_norm_g": m_b_q_norm_g, "b_k_norm_g": m_b_k_norm_g, "b_w_out": m_b_w_out}
    v_in = {"norm_g": v_norm_g, "a_w_in": v_a_w_in, "a_conv_w": v_a_conv_w, "a_log": v_a_log,
            "a_dt_bias": v_a_dt_bias, "a_norm_g": v_a_norm_g, "a_w_out": v_a_w_out, "b_w_in": v_b_w_in,
            "b_q_norm_g": v_b_q_norm_g, "b_k_norm_g": v_b_k_norm_g, "b_w_out": v_b_w_out}
    names = list(weights)

    delta_w, new_m, new_v = {}, {}, {}
    for nm in big:
        mine, other = big_halves[nm]
        if weights[nm].shape[2] % 128:
            cols = lambda a: jnp.transpose(a, (2, 0, 1))
            half_cols = lambda a: jnp.transpose(a)[:, None, :]
            outs = _adamw_shard_cols(cols(weights[nm]), half_cols(mine), half_cols(other), cols(m_in[nm]),
                                     cols(v_in[nm]), half_index, f"adamw_{nm}")
            outs = [jnp.transpose(o, (1, 2, 0)) for o in outs]
        else:
            outs = _adamw_shard(weights[nm], mine, other, m_in[nm], v_in[nm], half_index, f"adamw_{nm}")
        grads[nm], delta_w[nm], new_m[nm], new_v[nm] = outs
    small_names = [nm for nm in names if nm not in big]
    packs = [_pack_rows([src[nm] for nm in small_names]) for src in (weights, grads, m_in, v_in)]
    offs = packs[0][1]
    dl, m2, v2 = _adamw(packs[0][0], packs[1][0], packs[2][0], packs[3][0], "adamw_small")
    for nm, a, b, c2 in zip(small_names, _unpack_rows(dl, offs), _unpack_rows(m2, offs), _unpack_rows(v2, offs)):
        delta_w[nm], new_m[nm], new_v[nm] = a, b, c2

    return (loss, d_x0, *[grads[nm] for nm in names], *[delta_w[nm] for nm in names],
            *[new_m[nm] for nm in names], *[new_v[nm] for nm in names])
```

```python
import jax
import jax.numpy as jnp
import numpy as np
from jax import lax
from jax.experimental import pallas as pl
from jax.experimental.pallas import tpu as pltpu

F32 = jnp.float32
BF16 = jnp.bfloat16
MESH = pl.DeviceIdType.MESH

EPS = 1e-6
A_HEADS = 8
A_DK = 128
A_DV = 256
A_QK = A_HEADS * A_DK
A_VW = A_HEADS * A_DV
A_MAIN = 2 * A_QK + 2 * A_VW
A_HEAD_COLS = 2 * A_DK + 2 * A_DV
A_CONV_COLS = 2 * A_DK + A_DV
A_CHUNK = 64
A_CONV = 4
B_GROUPS = 3
B_HEADS = 8
B_DH = 128
B_W = B_HEADS * B_DH
B_DIL = (1, 4, 16)
B_BLK = 128
ROPE_THETA = 500000.0
ROPE_DIMS = B_DH // 4
ADAM_LR, ADAM_B1, ADAM_B2, ADAM_EPS, ADAM_WD, ADAM_STEP = 0.001, 0.9, 0.999, 1e-08, 0.01, 10
N_CHIPS = 4
VMEM_BIG = 56 * 1024 * 1024
DW_K = 4096


def _params(sem=None, vmem=None):
    return pltpu.CompilerParams(dimension_semantics=sem, vmem_limit_bytes=vmem)


def _dot(a, b, ca, cb):
    return lax.dot_general(a.astype(BF16), b.astype(BF16), (((ca,), (cb,)), ((), ())),
                           preferred_element_type=F32)


def _split3(a):
    hi = a.astype(BF16)
    r = a - hi.astype(F32)
    mid = r.astype(BF16)
    lo = (r - mid.astype(F32)).astype(BF16)
    return hi, mid, lo


def _sigmoid(y):
    return 1.0 / (1.0 + jnp.exp(-y))


def _silu(y):
    return y * _sigmoid(y)


def _silu_and_slope(y):
    s = _sigmoid(y)
    return y * s, s * (1.0 + y * (1.0 - s))


def _matmul(a, b, mode, out_dtype, name, res=None, tm=1024, tn=1024, tk=1024, n=None, b_spec=None, into=None):
    m, k = a.shape[::-1] if mode == "tn" else a.shape
    if n is None:
        n = b.shape[0] if mode == "nt" else b.shape[1]
    tm, tn, tk = min(tm, m), min(tn, n), min(tk, k)
    assert m % tm == 0 and n % tn == 0 and k % tk == 0, (name, a.shape, b.shape)
    nk = k // tk
    dims = {"nn": ((1,), (0,)), "nt": ((1,), (1,)), "tn": ((0,), (0,))}[mode]

    def body(*refs):
        a_ref, b_ref = refs[0], refs[1]
        r_ref = refs[2] if res is not None else None
        o_ref = refs[2 + (res is not None) + (into is not None)]
        prod = lax.dot_general(a_ref[...], b_ref[...], (dims, ((), ())), preferred_element_type=F32)

        def finish(r):
            if res is not None:
                r = r + r_ref[...]
            o_ref[...] = r.astype(out_dtype)

        if nk == 1:
            finish(prod)
            return
        acc = refs[-1]
        kk = pl.program_id(2)

        @pl.when(kk == 0)
        def _():
            acc[...] = prod

        @pl.when((kk > 0) & (kk < nk - 1))
        def _():
            acc[...] += prod

        @pl.when(kk == nk - 1)
        def _():
            finish(acc[...] + prod)

    a_spec = pl.BlockSpec((tm, tk), lambda i, j, kk: (i, kk))
    if mode == "tn":
        a_spec = pl.BlockSpec((tk, tm), lambda i, j, kk: (kk, i))
    if b_spec is None and mode == "nt":
        b_spec = pl.BlockSpec((tn, tk), lambda i, j, kk: (j, kk))
    elif b_spec is None:
        b_spec = pl.BlockSpec((tk, tn), lambda i, j, kk: (kk, j))
    in_specs = [a_spec, b_spec]
    args = [a, b]
    if res is not None:
        in_specs.append(pl.BlockSpec((tm, tn), lambda i, j, kk: (i, j)))
        args.append(res)
    out_spec = pl.BlockSpec((tm, tn), lambda i, j, kk: (i, j))
    out_shape = jax.ShapeDtypeStruct((m, n), out_dtype)
    aliases = {}
    if into is not None:
        assert res is None
        buf, out_spec = into
        out_shape = jax.ShapeDtypeStruct(buf.shape, buf.dtype)
        in_specs.append(ANY)
        args.append(buf)
        aliases = {2: 0}
    return pl.pallas_call(
        body, name=name, grid=(m // tm, n // tn, nk),
        in_specs=in_specs, out_specs=out_spec, out_shape=out_shape, input_output_aliases=aliases,
        scratch_shapes=[pltpu.VMEM((tm, tn), F32)] if nk > 1 else [],
        compiler_params=_params(("parallel", "parallel", "arbitrary"), 48 * 1024 * 1024),
    )(*args)


def _rms_fwd(x, g, name, tm=512):
    t, d = x.shape

    def body(x_ref, g_ref, h_ref):
        xv = x_ref[...]
        r = lax.rsqrt(jnp.mean(xv * xv, axis=-1, keepdims=True) + EPS)
        h_ref[...] = (xv * r * g_ref[...]).astype(BF16)

    return pl.pallas_call(
        body, name=name, grid=(t // tm,),
        in_specs=[pl.BlockSpec((tm, d), lambda i: (i, 0)), pl.BlockSpec((1, d), lambda i: (0, 0))],
        out_specs=pl.BlockSpec((tm, d), lambda i: (i, 0)),
        out_shape=jax.ShapeDtypeStruct((t, d), BF16),
        compiler_params=_params(("parallel",)),
    )(x, g)


def _rms_bwd(x, g, dhs, dres, name, tm=512):
    t, d = x.shape
    n_dh = len(dhs)

    def body(*refs):
        x_ref, g_ref = refs[0], refs[1]
        dh_refs = refs[2:2 + n_dh]
        dres_ref, dx_ref, dg_ref = refs[2 + n_dh:]
        i = pl.program_id(0)

        @pl.when(i == 0)
        def _():
            dg_ref[...] = jnp.zeros_like(dg_ref)

        xv = x_ref[...]
        r = lax.rsqrt(jnp.mean(xv * xv, axis=-1, keepdims=True) + EPS)
        xh = xv * r
        dh = dh_refs[0][...].astype(F32)
        for ref in dh_refs[1:]:
            dh = dh + ref[...].astype(F32)
        dg_ref[0:1, :] += jnp.sum(dh * xh, axis=0, keepdims=True)
        dxh = dh * g_ref[...]
        dx = r * (dxh - xh * jnp.mean(dxh * xh, axis=-1, keepdims=True))
        dx_ref[...] = dx + dres_ref[...]

    row = pl.BlockSpec((tm, d), lambda i: (i, 0))
    dx, dg = pl.pallas_call(
        body, name=name, grid=(t // tm,),
        in_specs=[row, pl.BlockSpec((1, d), lambda i: (0, 0))] + [row] * n_dh + [row],
        out_specs=[row, pl.BlockSpec((8, d), lambda i: (0, 0))],
        out_shape=[jax.ShapeDtypeStruct((t, d), F32), jax.ShapeDtypeStruct((8, d), F32)],
        compiler_params=_params(("arbitrary",)),
    )(x, g, *dhs, dres)
    return dx, dg[0:1]


def _in_proj_bwd(dp, w, dh_more, x, g, dres, name, tm=512, tk=3072):
    t, k = dp.shape
    d = w.shape[0]
    nk = k // tk

    def body(dp_ref, w_ref, more_ref, x_ref, g_ref, dres_ref, dx_ref, dg_ref, acc):
        i, kk = pl.program_id(0), pl.program_id(1)

        @pl.when((i == 0) & (kk == 0))
        def _():
            dg_ref[...] = jnp.zeros_like(dg_ref)

        prod = lax.dot_general(dp_ref[...], w_ref[...], (((1,), (1,)), ((), ())), preferred_element_type=F32)

        @pl.when(kk == 0)
        def _():
            acc[...] = prod

        @pl.when((kk > 0) & (kk < nk - 1))
        def _():
            acc[...] += prod

        @pl.when(kk == nk - 1)
        def _():
            dh = acc[...] + prod + more_ref[...]
            xv = x_ref[...]
            r = lax.rsqrt(jnp.mean(xv * xv, axis=-1, keepdims=True) + EPS)
            xh = xv * r
            dg_ref[0:1, :] += jnp.sum(dh * xh, axis=0, keepdims=True)
            dxh = dh * g_ref[...]
            dx_ref[...] = r * (dxh - xh * jnp.mean(dxh * xh, axis=-1, keepdims=True)) + dres_ref[...]

    row = pl.BlockSpec((tm, d), lambda i, kk: (i, 0))
    dx, dg = pl.pallas_call(
        body, name=name, grid=(t // tm, nk),
        in_specs=[pl.BlockSpec((tm, tk), lambda i, kk: (i, kk)), pl.BlockSpec((d, tk), lambda i, kk: (0, kk)),
                  row, row, pl.BlockSpec((1, d), lambda i, kk: (0, 0)), row],
        out_specs=[row, pl.BlockSpec((8, d), lambda i, kk: (0, 0))],
        out_shape=[jax.ShapeDtypeStruct((t, d), F32), jax.ShapeDtypeStruct((8, d), F32)],
        scratch_shapes=[pltpu.VMEM((tm, d), F32)],
        compiler_params=_params(("arbitrary", "arbitrary"), 48 * 1024 * 1024),
    )(dp, w, dh_more, x, g, dres)
    return dx, dg[0:1]


def _out_proj(a, w, res, name, norm_g=None, target=None, tm=512):
    t, k = a.shape
    d = w.shape[1]
    nb = t // tm

    def body(a_ref, w_ref, r_ref, x_ref, o1_ref, o2_ref):
        y = jnp.dot(a_ref[...], w_ref[...], preferred_element_type=F32) + r_ref[...]
        if norm_g is not None:
            o1_ref[...] = y
            r = lax.rsqrt(jnp.mean(y * y, axis=-1, keepdims=True) + EPS)
            o2_ref[...] = (y * r * x_ref[...]).astype(BF16)
        else:
            e = y - x_ref[...]
            o1_ref[...] = e * (1.0 / d)
            s = jnp.sum(jnp.sum(e * e, axis=1, keepdims=True), axis=0, keepdims=True) * (0.5 / d)
            o2_ref[...] = jnp.broadcast_to(s, (8, 128))

    row = pl.BlockSpec((tm, d), lambda i: (i, 0))
    if norm_g is not None:
        extra, extra_spec = norm_g, pl.BlockSpec((1, d), lambda i: (0, 0))
        out2_spec, out2_shape = row, jax.ShapeDtypeStruct((t, d), BF16)
    else:
        extra, extra_spec = target, row
        out2_spec = pl.BlockSpec((None, 8, 128), lambda i: (i, 0, 0))
        out2_shape = jax.ShapeDtypeStruct((nb, 8, 128), F32)
    o1, o2 = pl.pallas_call(
        body, name=name, grid=(nb,),
        in_specs=[pl.BlockSpec((tm, k), lambda i: (i, 0)), pl.BlockSpec((k, d), lambda i: (0, 0)), row, extra_spec],
        out_specs=[row, out2_spec], out_shape=[jax.ShapeDtypeStruct((t, d), F32), out2_shape],
        compiler_params=_params(("parallel",), 48 * 1024 * 1024),
    )(a, w, res, extra)
    return (o1, o2) if norm_g is not None else (o1, o2[:, 0, 0])


def _softplus(x):
    t = jnp.exp(-jnp.abs(x))
    return jnp.maximum(x, 0.0) + jnp.where(t < 1e-3, t * (1.0 - 0.5 * t), jnp.log(1.0 + t))


def _tri(rows_le_cols):
    r = lax.broadcasted_iota(jnp.int32, (A_CHUNK, A_CHUNK), 0)
    c = lax.broadcasted_iota(jnp.int32, (A_CHUNK, A_CHUNK), 1)
    return jnp.where((r <= c) if rows_le_cols else (r >= c), 1.0, 0.0).astype(BF16)


def _dot_exact_rhs(a, ones_bf16):
    dn = (((1,), (0,)), ((), ()))
    hi, mid, lo = _split3(a)
    out = lax.dot_general(hi, ones_bf16, dn, preferred_element_type=F32)
    out = out + lax.dot_general(mid, ones_bf16, dn, preferred_element_type=F32)
    return out + lax.dot_general(lo, ones_bf16, dn, preferred_element_type=F32)


def _gdn_prep(tail_t, a_log, dt_bias):
    bn, _, n, c = tail_t.shape

    def body(t_ref, alog_ref, dtb_ref, beta_ref, gc_ref):
        upper = _tri(True)
        for h in range(A_HEADS):
            beta_ref[h] = _sigmoid(t_ref[h])
            ea = jnp.exp(jnp.full((n, c), alog_ref[h], F32))
            g = -ea * _softplus(t_ref[A_HEADS + h] + dtb_ref[h])
            gc_ref[h] = _dot_exact_rhs(g, upper)

    smem = pl.BlockSpec(memory_space=pltpu.SMEM)
    blk = pl.BlockSpec((None, A_HEADS, n, c), lambda b: (b, 0, 0, 0))
    return pl.pallas_call(
        body, name="gdn_prep", grid=(bn,),
        in_specs=[pl.BlockSpec((None, 2 * A_HEADS, n, c), lambda b: (b, 0, 0, 0)), smem, smem],
        out_specs=[blk, blk],
        out_shape=[jax.ShapeDtypeStruct((bn, A_HEADS, n, c), F32)] * 2,
        compiler_params=_params(("parallel",)),
    )(tail_t, a_log, dt_bias)


def _gdn_prep_bwd(tail_t, a_log, dt_bias, d_gc, d_beta):
    bn, _, n, c = tail_t.shape

    def body(t_ref, alog_ref, dtb_ref, dgc_ref, dbeta_ref, dt_ref, dal_ref, ddt_ref):
        lower = _tri(False)
        for h in range(A_HEADS):
            beta = _sigmoid(t_ref[h])
            dt_ref[h] = dbeta_ref[h] * beta * (1.0 - beta)
            dg = _dot_exact_rhs(dgc_ref[h], lower)
            ea = jnp.exp(jnp.full((n, c), alog_ref[h], F32))
            xa = t_ref[A_HEADS + h] + dtb_ref[h]
            g = -ea * _softplus(xa)
            dxa = -ea * dg * _sigmoid(xa)
            dt_ref[A_HEADS + h] = dxa
            s1 = jnp.sum(jnp.sum(g * dg, axis=1, keepdims=True), axis=0, keepdims=True)
            s2 = jnp.sum(jnp.sum(dxa, axis=1, keepdims=True), axis=0, keepdims=True)
            dal_ref[h:h + 1, :] = jnp.broadcast_to(s1, (1, 128))
            ddt_ref[h:h + 1, :] = jnp.broadcast_to(s2, (1, 128))

    smem = pl.BlockSpec(memory_space=pltpu.SMEM)
    blk8 = pl.BlockSpec((None, A_HEADS, n, c), lambda b: (b, 0, 0, 0))
    blk16 = pl.BlockSpec((None, 2 * A_HEADS, n, c), lambda b: (b, 0, 0, 0))
    sm = pl.BlockSpec((None, A_HEADS, 128), lambda b: (b, 0, 0))
    return pl.pallas_call(
        body, name="gdn_prep_bwd", grid=(bn,),
        in_specs=[blk16, smem, smem, blk8, blk8],
        out_specs=[blk16, sm, sm],
        out_shape=[jax.ShapeDtypeStruct((bn, 2 * A_HEADS, n, c), F32),
                   jax.ShapeDtypeStruct((bn, A_HEADS, 128), F32),
                   jax.ShapeDtypeStruct((bn, A_HEADS, 128), F32)],
        compiler_params=_params(("parallel",)),
    )(tail_t, a_log, dt_bias, d_gc, d_beta)


HALO = 8


def _conv_taps(xw, w):
    y = w[A_CONV - 1:A_CONV, :] * xw
    for j in range(1, A_CONV):
        y = y + w[A_CONV - 1 - j:A_CONV - j, :] * pltpu.roll(xw, j, 0)
    return y[HALO:, :]


def _row_to_col(row, eye):
    c = eye.shape[0]
    return jnp.sum(jnp.where(eye, jnp.broadcast_to(row, (c, c)), 0.0), axis=1, keepdims=True)


def _col_to_row(col, eye):
    c = eye.shape[0]
    return jnp.sum(jnp.where(eye, jnp.broadcast_to(col, (c, c)), 0.0), axis=0, keepdims=True)


def _unit_lower_inverse(a, ri, ci):
    eye = jnp.where(ri == ci, 1.0, 0.0)
    a8 = jnp.where((ri >> 3) == (ci >> 3), a, 0.0)
    a2 = _dot(a8, a8, 1, 0)
    yield
    a4 = _dot(a2, a2, 1, 0)
    t = eye - a8
    t = t + _dot(t, a2, 1, 0)
    yield
    t = t + _dot(t, a4, 1, 0)
    yield
    for sh in (3, 4, 5):
        off = jnp.where(((ri >> (sh + 1)) == (ci >> (sh + 1))) & ((ri >> sh) != (ci >> sh)), a, 0.0)
        left = _dot(t, off, 1, 0)
        yield
        t = t - _dot(left, t, 1, 0)
        yield
    return t


def _round_robin(gens):
    live = list(gens)
    while live:
        nxt = []
        for g in live:
            try:
                next(g)
                nxt.append(g)
            except StopIteration:
                pass
        live = nxt


def _gdn_chunk_core(q, k, v, g_row, b_row, t_mat, ri, ci):
    eye = ri == ci
    g_col = _row_to_col(g_row, eye)
    b_col = _row_to_col(b_row, eye)
    causal = ri >= ci
    strict = ri > ci
    dec = jnp.where(causal, jnp.exp(jnp.where(causal, g_col - g_row, 0.0)), 0.0)
    gam = jnp.exp(g_col)
    g_last = g_row[:, A_CHUNK - 1:A_CHUNK]
    gam_last = jnp.exp(g_last)
    e = jnp.exp(g_last - g_col)
    kb = k * b_col
    bv = v * b_col
    kbg = kb * gam
    q16, k16, kb16 = q.astype(BF16), k.astype(BF16), kb.astype(BF16)
    kk = _dot(kb16, k16, 1, 1)
    p = _dot(q16, k16, 1, 1) * dec
    yield
    a_mat = jnp.where(strict, kk * dec, 0.0)
    if t_mat is None:
        t_mat = yield from _unit_lower_inverse(a_mat, ri, ci)
    t16 = t_mat.astype(BF16)
    u = _dot(t16, bv, 1, 0)
    w = _dot(t16, kbg, 1, 0)
    yield
    return dict(eye=eye, g_col=g_col, b_col=b_col, dec=dec, strict=strict, causal=causal, gam=gam,
                gam_last=gam_last, e=e, kb=kb, bv=bv, kbg=kbg, a_mat=a_mat, t_mat=t_mat, u=u, w=w, p=p,
                qg=q * gam, kd=k * e, q16=q16, k16=k16, kb16=kb16, t16=t16)


A_SEQ_BLK = 256
A_BLK_CHUNKS = A_SEQ_BLK // A_CHUNK


def _gdn_halo(proj_hm):
    bn, s, w = proj_hm.shape
    last = proj_hm.reshape(bn, s // A_SEQ_BLK, A_SEQ_BLK, w)[:, :, A_SEQ_BLK - HALO:, :]
    return jnp.concatenate([jnp.zeros((bn, 1, HALO, w), proj_hm.dtype), last[:, :-1]], axis=1)


def _gdn_window(x_ref, halo_ref, ci, first, lo):
    if first:
        return jnp.concatenate([halo_ref[:, lo:lo + A_CONV_COLS], x_ref[0:A_CHUNK, lo:lo + A_CONV_COLS]], axis=0)
    start = pl.multiple_of(ci * A_CHUNK - HALO, HALO)
    return x_ref[pl.ds(start, A_CHUNK + HALO), lo:lo + A_CONV_COLS]


def _gdn_chunk_prep(xw, cw, y=None):
    if y is None:
        y = _conv_taps(xw, cw)
    a, slope = _silu_and_slope(y)
    aq, ak, v = a[:, 0:A_DK], a[:, A_DK:2 * A_DK], a[:, 2 * A_DK:]
    rq = lax.rsqrt(jnp.sum(aq * aq, axis=1, keepdims=True) + EPS)
    rk = lax.rsqrt(jnp.sum(ak * ak, axis=1, keepdims=True) + EPS)
    return dict(xw=xw, y=y, slope=slope, aq=aq, ak=ak, rq=rq, rk=rk, q=aq * rq * (A_DK ** -0.5), k=ak * rk, v=v)


def _gdn_fwd(proj_hm, cw_hm, beta, gc, norm_g, hp=8):
    bn, s, _ = proj_hm.shape
    n = s // A_CHUNK
    nsb = s // A_SEQ_BLK
    halo = _gdn_halo(proj_hm)

    def body(x_ref, halo_ref, cw_ref, beta_ref, gc_ref, ng_ref, og_ref, oraw_ref, st_ref, t_ref, y_ref, state):
        first_chunk = pl.program_id(2) * A_BLK_CHUNKS
        ri = lax.broadcasted_iota(jnp.int32, (A_CHUNK, A_CHUNK), 0)
        ci_ = lax.broadcasted_iota(jnp.int32, (A_CHUNK, A_CHUNK), 1)
        ng = ng_ref[...]

        @pl.when(pl.program_id(2) == 0)
        def _():
            state[...] = jnp.zeros_like(state)

        def one_head(hh, ci, first, rows):
            lo = hh * A_HEAD_COLS
            cw = cw_ref[:, hh * A_CONV_COLS:(hh + 1) * A_CONV_COLS]
            cin = _gdn_chunk_prep(_gdn_window(x_ref, halo_ref, ci, first, lo), cw)
            y_ref[rows, hh * A_CONV_COLS:(hh + 1) * A_CONV_COLS] = cin["y"]
            seq_chunk = pl.ds(first_chunk + ci, 1)
            core = yield from _gdn_chunk_core(cin["q"], cin["k"], cin["v"], gc_ref[hh, seq_chunk, :],
                                              beta_ref[hh, seq_chunk, :], None, ri, ci_)
            st = state[hh]
            st_ref[hh, ci] = st
            t_ref[hh, ci] = core["t_mat"]
            st16 = st.astype(BF16)
            vn = core["u"] - _dot(core["w"], st16, 1, 0)
            qs = _dot(core["qg"], st16, 1, 0)
            yield
            vn16 = vn.astype(BF16)
            o = qs + _dot(core["p"], vn16, 1, 0)
            state[hh] = st * core["gam_last"] + _dot(core["kd"], vn16, 0, 0)
            yield
            ocols = slice(hh * A_DV, (hh + 1) * A_DV)
            oraw_ref[rows, ocols] = o
            r = lax.rsqrt(jnp.mean(o * o, axis=1, keepdims=True) + EPS)
            z = x_ref[rows, lo + A_CONV_COLS:lo + A_HEAD_COLS]
            og_ref[rows, ocols] = (o * r * ng * _silu(z)).astype(BF16)

        def chunk(ci, first):
            rows = pl.ds(0 if first else pl.multiple_of(ci * A_CHUNK, A_CHUNK), A_CHUNK)
            _round_robin([one_head(hh, ci, first, rows) for hh in range(hp)])

        chunk(0, True)
        lax.fori_loop(1, A_BLK_CHUNKS, lambda i, c: (chunk(i, False), c)[1], 0)

    small = pl.BlockSpec((None, hp, n, A_CHUNK), lambda b, h, j: (b, h, 0, 0))
    return pl.pallas_call(
        body, name="gdn_fwd", grid=(bn, A_HEADS // hp, nsb),
        in_specs=[pl.BlockSpec((None, A_SEQ_BLK, hp * A_HEAD_COLS), lambda b, h, j: (b, j, h)),
                  pl.BlockSpec((None, None, HALO, hp * A_HEAD_COLS), lambda b, h, j: (b, j, 0, h)),
                  pl.BlockSpec((A_CONV, hp * A_CONV_COLS), lambda b, h, j: (0, h)),
                  small, small,
                  pl.BlockSpec((1, A_DV), lambda b, h, j: (0, 0))],
        out_specs=[pl.BlockSpec((None, A_SEQ_BLK, hp * A_DV), lambda b, h, j: (b, j, h)),
                   pl.BlockSpec((None, A_SEQ_BLK, hp * A_DV), lambda b, h, j: (b, j, h)),
                   pl.BlockSpec((None, hp, A_BLK_CHUNKS, A_DK, A_DV), lambda b, h, j: (b, h, j, 0, 0)),
                   pl.BlockSpec((None, hp, A_BLK_CHUNKS, A_CHUNK, A_CHUNK), lambda b, h, j: (b, h, j, 0, 0)),
                   pl.BlockSpec((None, A_SEQ_BLK, hp * A_CONV_COLS), lambda b, h, j: (b, j, h))],
        out_shape=[jax.ShapeDtypeStruct((bn, s, A_VW), BF16),
                   jax.ShapeDtypeStruct((bn, s, A_VW), F32),
                   jax.ShapeDtypeStruct((bn, A_HEADS, n, A_DK, A_DV), F32),
                   jax.ShapeDtypeStruct((bn, A_HEADS, n, A_CHUNK, A_CHUNK), F32),
                   jax.ShapeDtypeStruct((bn, s, A_HEADS * A_CONV_COLS), F32)],
        scratch_shapes=[pltpu.VMEM((hp, A_DK, A_DV), F32)],
        compiler_params=_params(("parallel", "parallel", "arbitrary"), VMEM_BIG),
    )(proj_hm, halo, cw_hm, beta, gc, norm_g)


def _gdn_bwd(proj_hm, cw_hm, beta, gc, norm_g, oraw, states, t_mats, conv_y, dog, hp=4):
    bn, s, _ = proj_hm.shape
    n = s // A_CHUNK
    nsb = s // A_SEQ_BLK
    halo = _gdn_halo(proj_hm)

    def body(x_ref, halo_ref, cw_ref, beta_ref, gc_ref, ng_ref, oraw_ref, st_ref, t_ref, y_ref, dog_ref,
             dx_ref, dgc_ref, dbeta_ref, dcw_ref, dng_ref, dstate, dy_next, shifted):
        first_chunk = (nsb - 1 - pl.program_id(2)) * A_BLK_CHUNKS
        ri = lax.broadcasted_iota(jnp.int32, (A_CHUNK, A_CHUNK), 0)
        ci_ = lax.broadcasted_iota(jnp.int32, (A_CHUNK, A_CHUNK), 1)
        lane = lax.broadcasted_iota(jnp.int32, (1, A_CHUNK), 1)
        ng = ng_ref[...]

        @pl.when(pl.program_id(2) == 0)
        def _():
            dstate[...] = jnp.zeros_like(dstate)
            dy_next[...] = jnp.zeros_like(dy_next)
            dcw_ref[...] = jnp.zeros_like(dcw_ref)
            dng_ref[...] = jnp.zeros_like(dng_ref)

        def one_head(hh, ci, first, rows):
            lo = hh * A_HEAD_COLS
            ccols = slice(hh * A_CONV_COLS, (hh + 1) * A_CONV_COLS)
            ocols = slice(hh * A_DV, (hh + 1) * A_DV)
            cw = cw_ref[:, ccols]
            cin = _gdn_chunk_prep(_gdn_window(x_ref, halo_ref, ci, first, lo), cw, y_ref[rows, ccols])
            q, k, v = cin["q"], cin["k"], cin["v"]
            seq_chunk = pl.ds(first_chunk + ci, 1)
            cr = yield from _gdn_chunk_core(q, k, v, gc_ref[hh, seq_chunk, :], beta_ref[hh, seq_chunk, :],
                                            t_ref[hh, ci], ri, ci_)
            eye, dec, gam, e = cr["eye"], cr["dec"], cr["gam"], cr["e"]
            b_col, t_mat, u, w, p = cr["b_col"], cr["t_mat"], cr["u"], cr["w"], cr["p"]
            st = st_ref[hh, ci]
            ds_out = dstate[hh]

            o = oraw_ref[rows, ocols]
            z = x_ref[rows, lo + A_CONV_COLS:lo + A_HEAD_COLS]
            d_og = dog_ref[rows, ocols].astype(F32)
            r = lax.rsqrt(jnp.mean(o * o, axis=1, keepdims=True) + EPS)
            oh = o * r
            gate, gate_slope = _silu_and_slope(z)
            d_on = d_og * gate
            dz = d_og * oh * ng * gate_slope
            dng_ref[hh, 0:1, :] += jnp.sum(d_on * oh, axis=0, keepdims=True)
            d_oh = d_on * ng
            d_o = r * (d_oh - oh * jnp.mean(d_oh * oh, axis=1, keepdims=True))

            st16, ds16, do16, w16 = st.astype(BF16), ds_out.astype(BF16), d_o.astype(BF16), w.astype(BF16)
            q16, k16, t16 = cr["q16"], cr["k16"], cr["t16"]
            vn = u - _dot(w16, st16, 1, 0)
            d_vn = _dot(p, do16, 0, 0) + _dot(cr["kd"], ds16, 1, 0)
            d_qg = _dot(do16, st16, 1, 1)
            qgdo = _dot(cr["qg"], do16, 0, 0)
            yield
            vn16, dvn16 = vn.astype(BF16), d_vn.astype(BF16)
            d_p = jnp.where(cr["causal"], _dot(do16, vn16, 1, 1), 0.0)
            d_kd = _dot(vn16, ds16, 1, 1)
            d_gam_last = jnp.sum(jnp.sum(st * ds_out, axis=1, keepdims=True), axis=0, keepdims=True)
            d_w = -_dot(dvn16, st16, 1, 1)
            dstate[hh] = qgdo + ds_out * cr["gam_last"] - _dot(w16, dvn16, 0, 0)
            d_bv = _dot(t16, dvn16, 0, 0)
            yield
            d_kbg = _dot(t16, d_w, 0, 0)
            n_p = (d_p * dec).astype(BF16)
            d_q = _dot(n_p, k16, 1, 0) + d_qg * gam
            npq = _dot(n_p, q16, 0, 0)
            yield
            d_a = jnp.where(cr["strict"], -(_dot(d_bv, u, 1, 1) + _dot(d_kbg, w16, 1, 1)), 0.0)
            yield
            m_a = (d_a * dec).astype(BF16)
            d_kb = _dot(m_a, k16, 1, 0) + d_kbg * gam
            d_k = (_dot(m_a, cr["kb16"], 0, 0) + npq + d_kd * e + d_kb * b_col)
            yield
            d_v = d_bv * b_col
            d_beta_col = (jnp.sum(d_bv * v, axis=1, keepdims=True)
                          + jnp.sum(d_kb * k, axis=1, keepdims=True))
            gterm = d_a * cr["a_mat"] + d_p * p
            d_e = jnp.sum(d_kd * k, axis=1, keepdims=True) * e
            d_g_col = (jnp.sum(gterm, axis=1, keepdims=True)
                       + (jnp.sum(d_qg * q, axis=1, keepdims=True)
                          + jnp.sum(d_kbg * cr["kb"], axis=1, keepdims=True)) * gam
                       - d_e)
            d_g_last = jnp.sum(d_e, axis=0, keepdims=True) + d_gam_last * cr["gam_last"]
            d_g_row = (_col_to_row(d_g_col, eye) - jnp.sum(gterm, axis=0, keepdims=True)
                       + jnp.where(lane == A_CHUNK - 1, d_g_last, 0.0))
            dgc_ref[hh, seq_chunk, :] = d_g_row
            dbeta_ref[hh, seq_chunk, :] = _col_to_row(d_beta_col, eye)

            qh = cin["aq"] * cin["rq"]
            kh = cin["ak"] * cin["rk"]
            d_qh = d_q * (A_DK ** -0.5)
            d_aq = cin["rq"] * (d_qh - qh * jnp.sum(d_qh * qh, axis=1, keepdims=True))
            d_ak = cin["rk"] * (d_k - kh * jnp.sum(d_k * kh, axis=1, keepdims=True))
            d_y = jnp.concatenate([d_aq, d_ak, d_v], axis=1) * cin["slope"]
            shifted[hh, 0, 0:A_CHUNK, :] = d_y
            shifted[hh, 0, A_CHUNK:A_CHUNK + HALO, :] = dy_next[hh]
            shifted[hh, 1, 0:A_CHUNK + HALO, :] = cin["xw"]
            d_x = cw[A_CONV - 1:A_CONV, :] * d_y
            for j in range(1, A_CONV):
                d_x = d_x + cw[A_CONV - 1 - j:A_CONV - j, :] * shifted[hh, 0, j:j + A_CHUNK, :]
            for j in range(A_CONV):
                xs = shifted[hh, 1, HALO - j:HALO - j + A_CHUNK, :]
                dcw_ref[A_CONV - 1 - j:A_CONV - j, ccols] += jnp.sum(d_y * xs, axis=0, keepdims=True)
            dy_next[hh] = d_y[0:HALO, :]
            dx_ref[rows, lo:lo + A_CONV_COLS] = d_x.astype(BF16)
            dx_ref[rows, lo + A_CONV_COLS:lo + A_HEAD_COLS] = dz.astype(BF16)

        def chunk(ci, first):
            rows = pl.ds(0 if first else pl.multiple_of(ci * A_CHUNK, A_CHUNK), A_CHUNK)
            _round_robin([one_head(hh, ci, first, rows) for hh in range(hp)])

        lax.fori_loop(0, A_BLK_CHUNKS - 1, lambda i, c: (chunk(A_BLK_CHUNKS - 1 - i, False), c)[1], 0)
        chunk(0, True)

    rev = lambda j: nsb - 1 - j
    small = pl.BlockSpec((None, hp, n, A_CHUNK), lambda b, h, j: (b, h, 0, 0))
    wide = pl.BlockSpec((None, A_SEQ_BLK, hp * A_HEAD_COLS), lambda b, h, j: (b, rev(j), h))
    val = pl.BlockSpec((None, A_SEQ_BLK, hp * A_DV), lambda b, h, j: (b, rev(j), h))
    return pl.pallas_call(
        body, name="gdn_bwd", grid=(bn, A_HEADS // hp, nsb),
        in_specs=[wide,
                  pl.BlockSpec((None, None, HALO, hp * A_HEAD_COLS), lambda b, h, j: (b, rev(j), 0, h)),
                  pl.BlockSpec((A_CONV, hp * A_CONV_COLS), lambda b, h, j: (0, h)),
                  small, small,
                  pl.BlockSpec((1, A_DV), lambda b, h, j: (0, 0)),
                  val,
                  pl.BlockSpec((None, hp, A_BLK_CHUNKS, A_DK, A_DV), lambda b, h, j: (b, h, rev(j), 0, 0)),
                  pl.BlockSpec((None, hp, A_BLK_CHUNKS, A_CHUNK, A_CHUNK), lambda b, h, j: (b, h, rev(j), 0, 0)),
                  pl.BlockSpec((None, A_SEQ_BLK, hp * A_CONV_COLS), lambda b, h, j: (b, rev(j), h)),
                  val],
        out_specs=[wide, small, small,
                   pl.BlockSpec((None, A_CONV, hp * A_CONV_COLS), lambda b, h, j: (b, 0, h)),
                   pl.BlockSpec((None, hp, 8, A_DV), lambda b, h, j: (b, h, 0, 0))],
        out_shape=[jax.ShapeDtypeStruct((bn, s, A_HEADS * A_HEAD_COLS), BF16),
                   jax.ShapeDtypeStruct((bn, A_HEADS, n, A_CHUNK), F32),
                   jax.ShapeDtypeStruct((bn, A_HEADS, n, A_CHUNK), F32),
                   jax.ShapeDtypeStruct((bn, A_CONV, A_HEADS * A_CONV_COLS), F32),
                   jax.ShapeDtypeStruct((bn, A_HEADS, 8, A_DV), F32)],
        scratch_shapes=[pltpu.VMEM((hp, A_DK, A_DV), F32), pltpu.VMEM((hp, HALO, A_CONV_COLS), F32),
                        pltpu.VMEM((hp, 2, A_CHUNK + 2 * HALO, A_CONV_COLS), F32)],
        compiler_params=_params(("parallel", "parallel", "arbitrary"), VMEM_BIG),
    )(proj_hm, halo, cw_hm, beta, gc, norm_g, oraw, states, t_mats, conv_y, dog)


def _rope_tables(posf, inv_freq_row):
    t = posf.shape[0]
    tm = 512

    def body(p_ref, f_ref, c_ref, sa_ref, sb_ref):
        ang = p_ref[...] * f_ref[...]
        lane = lax.broadcasted_iota(jnp.int32, ang.shape, 1)
        half = ROPE_DIMS // 2
        c_ref[...] = jnp.where(lane < ROPE_DIMS, jnp.cos(ang), 1.0)
        sn = jnp.sin(ang)
        sa_ref[...] = jnp.where(lane < half, -sn, 0.0)
        sb_ref[...] = jnp.where((lane >= half) & (lane < ROPE_DIMS), sn, 0.0)

    row = pl.BlockSpec((tm, 128), lambda i: (i, 0))
    return pl.pallas_call(
        body, name="rope_tables", grid=(t // tm,),
        in_specs=[row, pl.BlockSpec((1, 128), lambda i: (0, 0))], out_specs=[row] * 3,
        out_shape=[jax.ShapeDtypeStruct((t, 128), F32)] * 3,
        compiler_params=_params(("parallel",)),
    )(posf, inv_freq_row)


def _qk_prep(proj, c, sa, sb, qg, kg, name, tm=512):
    t = proj.shape[0]

    def body(x_ref, c_ref, sa_ref, sb_ref, qg_ref, kg_ref, o_ref):
        cc, s1, s2 = c_ref[...], sa_ref[...], sb_ref[...]
        half = ROPE_DIMS // 2

        def one_head(lo, g):
            xv = x_ref[:, lo:lo + B_DH].astype(F32)
            ms = jnp.mean(xv * xv, axis=1, keepdims=True)
            yield
            xn = xv * lax.rsqrt(ms + EPS) * g
            r1, r2 = pltpu.roll(xn, 128 - half, 1), pltpu.roll(xn, half, 1)
            yield
            o_ref[:, lo:lo + B_DH] = (xn * cc + r1 * s1 + r2 * s2).astype(BF16)

        for which, g_ref in ((0, qg_ref), (1, kg_ref)):
            g = g_ref[...]
            _round_robin([one_head(which * B_W + h * B_DH, g) for h in range(B_HEADS)])
        o_ref[:, 2 * B_W:3 * B_W] = x_ref[:, 2 * B_W:3 * B_W]

    tab = pl.BlockSpec((tm, 128), lambda i: (i, 0))
    gain = pl.BlockSpec((1, B_DH), lambda i: (0, 0))
    return pl.pallas_call(
        body, name=name, grid=(t // tm,),
        in_specs=[pl.BlockSpec((tm, 3 * B_W), lambda i: (i, 0)), tab, tab, tab, gain, gain],
        out_specs=pl.BlockSpec((tm, 3 * B_W), lambda i: (i, 0)),
        out_shape=jax.ShapeDtypeStruct((t, 3 * B_W), BF16),
        compiler_params=_params(("parallel",), 40 * 1024 * 1024),
    )(proj, c, sa, sb, qg, kg)


def _qk_prep_bwd(proj, c, sa, sb, qg, kg, dq, dk, dv, dz, name, tm=512):
    t = proj.shape[0]
    out_w = 3 * B_W + (B_W if dz is not None else 0)

    def body(*refs):
        x_ref, c_ref, sa_ref, sb_ref, qg_ref, kg_ref, dq_ref, dk_ref, dv_ref = refs[:9]
        if dz is not None:
            dz_ref, o_ref, dgain_ref = refs[9:]
        else:
            o_ref, dgain_ref = refs[9:]
        i = pl.program_id(0)

        @pl.when(i == 0)
        def _():
            dgain_ref[...] = jnp.zeros_like(dgain_ref)

        cc, s1, s2 = c_ref[...], sa_ref[...], sb_ref[...]
        half = ROPE_DIMS // 2

        def one_head(which, h, g, d_ref, parts):
            lo = which * B_W + h * B_DH
            xv = x_ref[:, lo:lo + B_DH].astype(F32)
            d_out = d_ref[:, h * B_DH:(h + 1) * B_DH].astype(F32)
            ms = jnp.mean(xv * xv, axis=1, keepdims=True)
            r1, r2 = pltpu.roll(d_out * s1, half, 1), pltpu.roll(d_out * s2, 128 - half, 1)
            yield
            r = lax.rsqrt(ms + EPS)
            xh = xv * r
            d_xn = d_out * cc + r1 + r2
            parts.append(jnp.sum(d_xn * xh, axis=0, keepdims=True))
            d_xh = d_xn * g
            dot = jnp.mean(d_xh * xh, axis=1, keepdims=True)
            yield
            o_ref[:, lo:lo + B_DH] = (r * (d_xh - xh * dot)).astype(BF16)

        for which, g_ref, d_ref in ((0, qg_ref, dq_ref), (1, kg_ref, dk_ref)):
            parts = []
            _round_robin([one_head(which, h, g_ref[...], d_ref, parts) for h in range(B_HEADS)])
            acc = parts[0]
            for part in parts[1:]:
                acc = acc + part
            dgain_ref[which:which + 1, :] += acc
        o_ref[:, 2 * B_W:3 * B_W] = dv_ref[...]
        if dz is not None:
            o_ref[:, 3 * B_W:4 * B_W] = dz_ref[...]

    tab = pl.BlockSpec((tm, 128), lambda i: (i, 0))
    gain = pl.BlockSpec((1, B_DH), lambda i: (0, 0))
    grad = pl.BlockSpec((tm, B_W), lambda i: (i, 0))
    in_specs = [pl.BlockSpec((tm, 2 * B_W), lambda i: (i, 0)), tab, tab, tab, gain, gain, grad, grad, grad]
    args = [proj, c, sa, sb, qg, kg, dq, dk, dv]
    if dz is not None:
        in_specs.append(grad)
        args.append(dz)
    return pl.pallas_call(
        body, name=name, grid=(t // tm,), in_specs=in_specs,
        out_specs=[pl.BlockSpec((tm, out_w), lambda i: (i, 0)), pl.BlockSpec((8, B_DH), lambda i: (0, 0))],
        out_shape=[jax.ShapeDtypeStruct((t, out_w), BF16), jax.ShapeDtypeStruct((8, B_DH), F32)],
        compiler_params=_params(("arbitrary",), 40 * 1024 * 1024),
    )(*args)


def _attn_masks():
    qi = lax.broadcasted_iota(jnp.int32, (B_BLK, 2 * B_BLK), 0)
    kj = lax.broadcasted_iota(jnp.int32, (B_BLK, 2 * B_BLK), 1)
    two = (kj >= qi) & (kj <= qi + B_BLK)
    q1 = lax.broadcasted_iota(jnp.int32, (B_BLK, B_BLK), 0)
    k1 = lax.broadcasted_iota(jnp.int32, (B_BLK, B_BLK), 1)
    return k1 <= q1, two


def _lane_pick(ref_rows, h):
    lane = lax.broadcasted_iota(jnp.int32, ref_rows.shape, 1)
    return jnp.sum(jnp.where(lane == h, ref_rows, 0.0), axis=1, keepdims=True)


B_ROWS = 2048


def _attn_schedule(nb, sb, block):
    way = 16

    def run(items):
        for at in range(0, len(items), way):
            _round_robin([block(*it) for it in items[at:at + way]])

    run([(si, 0, True) for si in range(sb)])
    if nb == 1:
        return
    per = max(1, way // sb)
    lead = 1 + (nb - 1) % per
    if lead > 1:
        run([(si, i, False) for i in range(1, lead) for si in range(sb)])

    def step(it, carry):
        run([(si, lead + it * per + u, False) for u in range(per) for si in range(sb)])
        return carry

    lax.fori_loop(0, (nb - lead) // per, step, 0)


def _attn_rows(i, first):
    if first:
        return pl.ds(0, B_BLK), pl.ds(0, B_BLK)
    rows = pl.ds(pl.multiple_of(i * B_BLK, B_BLK), B_BLK)
    return rows, pl.ds(pl.multiple_of((i - 1) * B_BLK, B_BLK), 2 * B_BLK)


def _attn_fwd(qkv, name):
    ns, ln, _ = qkv.shape
    nb = ln // B_BLK
    sb = B_ROWS // ln
    scale = B_DH ** -0.5

    def body(q_ref, k_ref, v_ref, o_ref, lse_ref):
        h = pl.program_id(1)
        mask1, mask2 = _attn_masks()
        lane = lax.broadcasted_iota(jnp.int32, (B_BLK, B_HEADS), 1)

        @pl.when(h == 0)
        def _():
            lse_ref[...] = jnp.zeros_like(lse_ref)

        def block(si, i, first):
            rows, win = _attn_rows(i, first)
            mask = mask1 if first else mask2
            sc = jnp.where(mask, _dot(q_ref[si, rows, :], k_ref[si, win, :], 1, 1) * scale, -1e30)
            yield
            m = jnp.max(sc, axis=1, keepdims=True)
            p = jnp.exp(sc - m)
            l = jnp.sum(p, axis=1, keepdims=True)
            pv = _dot(p, v_ref[si, win, :], 1, 0)
            yield
            o_ref[si, rows, :] = (pv / l).astype(BF16)
            lse_ref[si, rows, :] = jnp.where(lane == h, m + jnp.log(l), lse_ref[si, rows, :])

        _attn_schedule(nb, sb, block)

    head = lambda off: pl.BlockSpec((sb, ln, B_DH), lambda s, h: (s, 0, off + h))
    return pl.pallas_call(
        body, name=name, grid=(ns // sb, B_HEADS),
        in_specs=[head(0), head(B_HEADS), head(2 * B_HEADS)],
        out_specs=[head(0), pl.BlockSpec((sb, ln, B_HEADS), lambda s, h: (s, 0, 0))],
        out_shape=[jax.ShapeDtypeStruct((ns, ln, B_W), BF16), jax.ShapeDtypeStruct((ns, ln, B_HEADS), F32)],
        compiler_params=_params(("parallel", "arbitrary")),
    )(qkv, qkv, qkv)


def _attn_bwd(qkv, d_o, lse_joint, delta, name):
    ns, ln, _ = qkv.shape
    nb = ln // B_BLK
    sb = B_ROWS // ln
    scale = B_DH ** -0.5

    def body(q_ref, k_ref, v_ref, do_ref, lj_ref, dl_ref, dq_ref, dk_out, dv_out, dk_ref, dv_ref):
        h = pl.program_id(1)
        mask1, mask2 = _attn_masks()
        dk_ref[...] = jnp.zeros_like(dk_ref)
        dv_ref[...] = jnp.zeros_like(dv_ref)

        def block(si, i, first):
            rows, win = _attn_rows(i, first)
            mask = mask1 if first else mask2
            q = q_ref[si, rows, :]
            d_out = do_ref[si, rows, :]
            l_col = _lane_pick(lj_ref[si, rows, :], h)
            d_col = _lane_pick(dl_ref[si, rows, :], h)
            sc = _dot(q, k_ref[si, win, :], 1, 1) * scale
            d_p = _dot(d_out, v_ref[si, win, :], 1, 1)
            yield
            p = jnp.exp(jnp.where(mask, sc - l_col, -1e30))
            d_s = p * (d_p - d_col) * scale
            d_q = _dot(d_s, k_ref[si, win, :], 1, 0)
            d_k = _dot(d_s, q, 0, 0)
            d_v = _dot(p, d_out, 0, 0)
            yield
            dq_ref[si, rows, :] = d_q.astype(BF16)
            dk_ref[si, win, :] += d_k
            dv_ref[si, win, :] += d_v

        _attn_schedule(nb, sb, block)
        dk_out[...] = dk_ref[...].astype(BF16)
        dv_out[...] = dv_ref[...].astype(BF16)

    head = lambda off: pl.BlockSpec((sb, ln, B_DH), lambda s, h: (s, 0, off + h))
    small = pl.BlockSpec((sb, ln, B_HEADS), lambda s, h: (s, 0, 0))
    return pl.pallas_call(
        body, name=name, grid=(ns // sb, B_HEADS),
        in_specs=[head(0), head(B_HEADS), head(2 * B_HEADS), head(0), small, small],
        out_specs=[head(0)] * 3,
        out_shape=[jax.ShapeDtypeStruct((ns, ln, B_W), BF16)] * 3,
        scratch_shapes=[pltpu.VMEM((sb, ln, B_DH), F32)] * 2,
        compiler_params=_params(("parallel", "parallel")),
    )(qkv, qkv, qkv, d_o, lse_joint, delta)


def _merge_weights(lse_refs):
    ls = [r[...] for r in lse_refs]
    m = jnp.maximum(jnp.maximum(ls[0], ls[1]), ls[2])
    es = [jnp.exp(l - m) for l in ls]
    tot = es[0] + es[1] + es[2]
    return [e / tot for e in es], m + jnp.log(tot)


def _merge_fwd(outs, lses, proj0, tm=512):
    t = outs[0].shape[0]

    def body(o0, o1, o2, l0, l1, l2, z_ref, og_ref):
        wts, _ = _merge_weights((l0, l1, l2))

        def one_head(h):
            cols = slice(h * B_DH, (h + 1) * B_DH)
            w0, w1, w2 = (jnp.broadcast_to(w[:, h:h + 1], (tm, B_DH)) for w in wts)
            yield
            o = w0 * o0[:, cols] + w1 * o1[:, cols] + w2 * o2[:, cols]
            og_ref[:, cols] = (o * _silu(z_ref[:, cols].astype(F32))).astype(BF16)

        _round_robin([one_head(h) for h in range(B_HEADS)])

    wide = pl.BlockSpec((tm, B_W), lambda i: (i, 0))
    small = pl.BlockSpec((tm, B_HEADS), lambda i: (i, 0))
    return pl.pallas_call(
        body, name="merge_fwd", grid=(t // tm,),
        in_specs=[wide] * 3 + [small] * 3 + [pl.BlockSpec((tm, B_W), lambda i: (i, 3))],
        out_specs=wide, out_shape=jax.ShapeDtypeStruct((t, B_W), BF16),
        compiler_params=_params(("parallel",)),
    )(*outs, *lses, proj0)


def _merge_bwd(outs, lses, proj0, d_og, tm=512):
    t = outs[0].shape[0]

    def body(o0, o1, o2, l0, l1, l2, z_ref, dog_ref, do_ref, lj_ref, dl_ref, dz_ref):
        wts, lj = _merge_weights((l0, l1, l2))
        lj_ref[...] = lj
        lane = lax.broadcasted_iota(jnp.int32, (tm, B_HEADS), 1)
        sums = [None] * B_HEADS

        def one_head(h):
            cols = slice(h * B_DH, (h + 1) * B_DH)
            w0, w1, w2 = (jnp.broadcast_to(w[:, h:h + 1], (tm, B_DH)) for w in wts)
            yield
            o = w0 * o0[:, cols] + w1 * o1[:, cols] + w2 * o2[:, cols]
            z = z_ref[:, cols].astype(F32)
            d_g = dog_ref[:, cols].astype(F32)
            gate, gate_slope = _silu_and_slope(z)
            d_out = d_g * gate
            dz_ref[:, cols] = (d_g * o * gate_slope).astype(BF16)
            do_ref[:, cols] = d_out.astype(BF16)
            sums[h] = jnp.sum(d_out * o, axis=1, keepdims=True)
            yield

        _round_robin([one_head(h) for h in range(B_HEADS)])
        delta = jnp.zeros((tm, B_HEADS), F32)
        for h in range(B_HEADS):
            delta = jnp.where(lane == h, sums[h], delta)
        dl_ref[...] = delta

    wide = pl.BlockSpec((tm, B_W), lambda i: (i, 0))
    small = pl.BlockSpec((tm, B_HEADS), lambda i: (i, 0))
    return pl.pallas_call(
        body, name="merge_bwd", grid=(t // tm,),
        in_specs=[wide] * 3 + [small] * 3 + [pl.BlockSpec((tm, B_W), lambda i: (i, 3)), wide],
        out_specs=[wide, small, small, wide],
        out_shape=[jax.ShapeDtypeStruct((t, B_W), BF16), jax.ShapeDtypeStruct((t, B_HEADS), F32),
                   jax.ShapeDtypeStruct((t, B_HEADS), F32), jax.ShapeDtypeStruct((t, B_W), BF16)],
        compiler_params=_params(("parallel",)),
    )(*outs, *lses, proj0, d_og)


def _adamw(w, g, m, v, name):
    r, c = w.shape
    tr = r
    for cand in (256, 128, 64, 32, 16, 8):
        if r % cand == 0:
            tr = cand
            break

    def body(w_ref, g_ref, m_ref, v_ref, d_ref, nm_ref, nv_ref):
        gv = g_ref[...]
        nm = ADAM_B1 * m_ref[...] + (1.0 - ADAM_B1) * gv
        nv = ADAM_B2 * v_ref[...] + (1.0 - ADAM_B2) * (gv * gv)
        m_hat = nm / (1.0 - ADAM_B1 ** ADAM_STEP)
        v_hat = nv / (1.0 - ADAM_B2 ** ADAM_STEP)
        d_ref[...] = -ADAM_LR * (m_hat / (jnp.sqrt(v_hat) + ADAM_EPS) + ADAM_WD * w_ref[...])
        nm_ref[...] = nm
        nv_ref[...] = nv

    blk = pl.BlockSpec((tr, c), lambda i: (i, 0))
    return pl.pallas_call(
        body, name=name, grid=(r // tr,), in_specs=[blk] * 4, out_specs=[blk] * 3,
        out_shape=[jax.ShapeDtypeStruct((r, c), F32)] * 3,
        compiler_params=_params(("parallel",)),
    )(w, g, m, v)


def _adam_update(w, gv, m, v):
    nm = ADAM_B1 * m + (1.0 - ADAM_B1) * gv
    nv = ADAM_B2 * v + (1.0 - ADAM_B2) * (gv * gv)
    m_hat = nm / (1.0 - ADAM_B1 ** ADAM_STEP)
    v_hat = nv / (1.0 - ADAM_B2 ** ADAM_STEP)
    return -ADAM_LR * (m_hat / (jnp.sqrt(v_hat) + ADAM_EPS) + ADAM_WD * w), nm, nv


def _adamw_shard(w, mine, theirs, m, v, half_index, name, tr=128):
    _, r, c = w.shape
    nhb = (r // 2) // tr

    def body(c_ref, w_ref, mine_ref, theirs_ref, m_ref, v_ref, g_ref, d_ref, nm_ref, nv_ref):
        is_mine = (pl.program_id(0) // nhb) == c_ref[0]
        gv = jnp.where(is_mine, mine_ref[...], theirs_ref[...])
        d, nm, nv = _adam_update(w_ref[...], gv, m_ref[...], v_ref[...])
        g_ref[...] = gv
        d_ref[...] = d
        nm_ref[...] = nm
        nv_ref[...] = nv

    full = pl.BlockSpec((None, tr, c), lambda i, cc: (0, i, 0))
    half = pl.BlockSpec((tr, c), lambda i, cc: (i % nhb, 0))
    return pl.pallas_call(
        body, name=name,
        grid_spec=pltpu.PrefetchScalarGridSpec(
            num_scalar_prefetch=1, grid=(2 * nhb,),
            in_specs=[full, half, half, full, full], out_specs=[full] * 4),
        out_shape=[jax.ShapeDtypeStruct(w.shape, F32)] * 4,
        compiler_params=_params(("parallel",), 40 * 1024 * 1024),
    )(half_index, w, mine, theirs, m, v)


def _adamw_shard_cols(w, mine, theirs, m, v, half_index, name, steps=20):
    c, _, r = w.shape
    tc = c // steps
    assert tc * steps == c

    def body(c_ref, w_ref, mine_ref, theirs_ref, m_ref, v_ref, g_ref, d_ref, nm_ref, nv_ref):
        first = jnp.where(c_ref[0] == 0, mine_ref[...], theirs_ref[...])
        second = jnp.where(c_ref[0] == 0, theirs_ref[...], mine_ref[...])
        for lo, gv in ((0, first), (r // 2, second)):
            cols = slice(lo, lo + r // 2)
            d, nm, nv = _adam_update(w_ref[:, :, cols], gv, m_ref[:, :, cols], v_ref[:, :, cols])
            g_ref[:, :, cols] = gv
            d_ref[:, :, cols] = d
            nm_ref[:, :, cols] = nm
            nv_ref[:, :, cols] = nv

    full = pl.BlockSpec((tc, 1, r), lambda i, cc: (i, 0, 0))
    half = pl.BlockSpec((tc, 1, r // 2), lambda i, cc: (i, 0, 0))
    return pl.pallas_call(
        body, name=name,
        grid_spec=pltpu.PrefetchScalarGridSpec(
            num_scalar_prefetch=1, grid=(steps,),
            in_specs=[full, half, half, full, full], out_specs=[full] * 4),
        out_shape=[jax.ShapeDtypeStruct(w.shape, F32)] * 4,
        compiler_params=_params(("parallel",), 40 * 1024 * 1024),
    )(half_index, w, mine, theirs, m, v)


def _pair_sum(own, other, half_index, name, tr=256):
    _, r, c = own.shape
    rh = r // 2
    tr = min(tr, rh)
    nrb = rh // tr

    def body(c_ref, own_ref, oth_ref, out_ref):
        out_ref[...] = (own_ref[...] + oth_ref[...].astype(F32)).astype(BF16)

    return pl.pallas_call(
        body, name=name,
        grid_spec=pltpu.PrefetchScalarGridSpec(
            num_scalar_prefetch=1, grid=(N_CHIPS, nrb),
            in_specs=[pl.BlockSpec((None, tr, c), lambda k, i, cc: (k, cc[0] * nrb + i, 0)),
                      pl.BlockSpec((None, tr, c), lambda k, i, cc: (k, i, 0))],
            out_specs=pl.BlockSpec((None, tr, c), lambda k, i, cc: (k, i, 0))),
        out_shape=jax.ShapeDtypeStruct((N_CHIPS, rh, c), BF16),
        compiler_params=_params(("parallel", "parallel")),
    )(half_index, own, other)


def _chip_sum(sums, others, chip_index, name, tr=256):
    _, r, c = sums.shape
    tr = min(tr, r)

    def body(k_ref, own_ref, oth_ref, out_ref):
        acc = own_ref[...].astype(F32)
        for j in range(N_CHIPS - 1):
            acc = acc + oth_ref[j].astype(F32)
        out_ref[...] = acc

    return pl.pallas_call(
        body, name=name,
        grid_spec=pltpu.PrefetchScalarGridSpec(
            num_scalar_prefetch=1, grid=(r // tr,),
            in_specs=[pl.BlockSpec((None, tr, c), lambda i, kk: (kk[0], i, 0)),
                      pl.BlockSpec((N_CHIPS - 1, tr, c), lambda i, kk: (0, i, 0))],
            out_specs=pl.BlockSpec((tr, c), lambda i, kk: (i, 0))),
        out_shape=jax.ShapeDtypeStruct((r, c), F32),
        compiler_params=_params(("parallel",)),
    )(chip_index, sums, others)


HBM = pl.BlockSpec(memory_space=pltpu.HBM)


def _place():
    x, y, c = lax.axis_index("x"), lax.axis_index("y"), lax.axis_index("c")
    chips = [(1 - x, y), (x, 1 - y), (1 - x, 1 - y)]
    return x, y, c, chips


def _sibling_forward(land):
    def body(in_ref, out_ref, send, recv):
        x, y, c, chips = _place()
        rh = out_ref.shape[1] // 2
        cps = []
        for j, (px, py) in enumerate(chips):
            slot = out_ref.at[2 * px + py, pl.ds(c * rh, rh)]
            cp = pltpu.make_async_remote_copy(
                src_ref=slot, dst_ref=slot, send_sem=send.at[j], recv_sem=recv.at[j],
                device_id=(x, y, 1 - c), device_id_type=MESH)
            cp.start()
            cps.append(cp)
        for j, (px, py) in enumerate(chips):
            slot = out_ref.at[2 * px + py, pl.ds((1 - c) * rh, rh)]
            pltpu.make_async_remote_copy(
                src_ref=slot, dst_ref=slot, send_sem=send.at[j], recv_sem=recv.at[j],
                device_id=(x, y, 1 - c), device_id_type=MESH).wait_recv()
        for cp in cps:
            cp.wait_send()

    return pl.pallas_call(
        body, name="first_weights_sibling_forward", in_specs=[HBM], out_specs=HBM,
        out_shape=jax.ShapeDtypeStruct(land.shape, land.dtype), input_output_aliases={0: 0},
        scratch_shapes=[pltpu.SemaphoreType.DMA((3,)), pltpu.SemaphoreType.DMA((3,))],
    )(land)


def _sibling_swap(halves, name):
    na = len(halves)

    def body(*refs):
        ins, outs = refs[:na], refs[na:2 * na]
        send, recv = refs[2 * na:]
        x, y, c, _ = _place()
        cps = []
        for i in range(na):
            cp = pltpu.make_async_remote_copy(
                src_ref=ins[i], dst_ref=outs[i], send_sem=send.at[i], recv_sem=recv.at[i],
                device_id=(x, y, 1 - c), device_id_type=MESH)
            cp.start()
            cps.append(cp)
        for cp in cps:
            cp.wait()

    out_shape = [jax.ShapeDtypeStruct(h.shape, h.dtype) for h in halves]
    return pl.pallas_call(
        body, name=name, in_specs=[HBM] * na, out_specs=[HBM] * na, out_shape=out_shape,
        scratch_shapes=[pltpu.SemaphoreType.DMA((na,)), pltpu.SemaphoreType.DMA((na,))],
    )(*halves)


SEM = pl.BlockSpec(memory_space=pltpu.SEMAPHORE)
ANY = pl.BlockSpec(memory_space=pl.ANY)
EFFECT = pltpu.SideEffectType.DATAFLOW_SIDE_EFFECTING


def _split_copy_start(name, plan, srcs, lands, after):
    ns, nl = len(srcs), len(lands)

    def body(*refs):
        src_refs, land_refs = refs[:ns], refs[ns:ns + nl]
        send, recv = refs[ns + nl + 1], refs[ns + nl + 2]
        token = refs[-1]
        outgoing, _ = plan(src_refs, land_refs)
        for src, dst, dev, si, ri in outgoing:
            pltpu.make_async_remote_copy(src_ref=src, dst_ref=dst, send_sem=send.at[si], recv_sem=recv.at[ri],
                                         device_id=dev, device_id_type=MESH).start()
        token[...] = jnp.zeros_like(token)

    n_out, n_in = plan.counts
    thru = [pltpu.HBM(a.shape, a.dtype) for a in list(srcs) + list(lands)]
    res = pl.pallas_call(
        body, name=name,
        out_shape=[pltpu.SemaphoreType.DMA((n_out,)), pltpu.SemaphoreType.DMA((n_in,))] + thru
        + [jax.ShapeDtypeStruct((8, 128), F32)],
        in_specs=[HBM] * (ns + nl) + [ANY],
        out_specs=[SEM, SEM] + [HBM] * (ns + nl) + [pl.BlockSpec(memory_space=pltpu.VMEM)],
        input_output_aliases={i: 2 + i for i in range(ns + nl)},
        compiler_params=pltpu.CompilerParams(has_side_effects=EFFECT),
    )(*[pltpu.with_memory_space_constraint(a, pltpu.HBM) for a in list(srcs) + list(lands)], after)
    return res[0], res[1], res[2:2 + ns], res[2 + ns:2 + ns + nl], res[-1]


def _split_copy_wait(name, plan, send, recv, srcs, lands, after):
    ns, nl = len(srcs), len(lands)
    after = list(after) if isinstance(after, (list, tuple)) else [after]

    def body(*refs):
        src_refs, land_refs = refs[:ns], refs[ns:ns + nl]
        send_ref, recv_ref = refs[ns + nl], refs[ns + nl + 1]
        outgoing, arrivals = plan(src_refs, land_refs)
        for src, dst, dev, si, ri in outgoing:
            pltpu.make_async_remote_copy(src_ref=src, dst_ref=dst, send_sem=send_ref.at[si], recv_sem=recv_ref.at[ri],
                                         device_id=dev, device_id_type=MESH).wait_send()
        for view, ri in arrivals:
            pltpu.make_async_remote_copy(src_ref=view, dst_ref=view, send_sem=send_ref.at[0], recv_sem=recv_ref.at[ri],
                                         device_id=_place()[:3], device_id_type=MESH).wait_recv()

    thru = [pltpu.HBM(a.shape, a.dtype) for a in list(srcs) + list(lands)]
    res = pl.pallas_call(
        body, name=name, out_shape=thru,
        in_specs=[HBM] * (ns + nl) + [SEM, SEM] + [ANY] * len(after), out_specs=[HBM] * (ns + nl),
        input_output_aliases={i: i for i in range(ns + nl)},
        compiler_params=pltpu.CompilerParams(has_side_effects=EFFECT),
    )(*srcs, *lands, send, recv, *after)
    return res[:ns], res[ns:]


def _gather_plan(n_arrays):
    def plan(src_refs, land_refs):
        x, y, c, chips = _place()
        me = 2 * x + y
        outgoing, arrivals = [], []
        for i in range(n_arrays):
            rh = src_refs[i].shape[0] // 2
            mine = pl.ds(c * rh, rh)
            for j, (px, py) in enumerate(chips):
                for delta in range(2):
                    tc = c ^ delta
                    outgoing.append((src_refs[i].at[mine], land_refs[i].at[me, mine], (px, py, tc),
                                     6 * i + 2 * j + delta, 6 * i + 2 * j + delta))
                    theirs = pl.ds(tc * rh, rh)
                    arrivals.append((land_refs[i].at[2 * px + py, theirs], 6 * i + 2 * j + delta))
        return outgoing, arrivals

    plan.counts = (6 * n_arrays, 6 * n_arrays)
    return plan


def _first_gather_plan():
    def plan(src_refs, land_refs):
        x, y, c, chips = _place()
        me = 2 * x + y
        rh = src_refs[0].shape[0] // 2
        mine = pl.ds(c * rh, rh)
        outgoing, arrivals = [], []
        for j, (px, py) in enumerate(chips):
            outgoing.append((src_refs[0].at[mine], land_refs[0].at[me, mine], (px, py, c), j, j))
            arrivals.append((land_refs[0].at[2 * px + py, mine], j))
            outgoing.append((src_refs[1], land_refs[1].at[me], (px, py, c), 3 + j, 3 + j))
            arrivals.append((land_refs[1].at[2 * px + py], 3 + j))
        return outgoing, arrivals

    plan.counts = (6, 6)
    return plan


def _exchange_plan(n_arrays):
    def plan(src_refs, land_refs):
        x, y, c, chips = _place()
        outgoing, arrivals = [], []
        for i in range(n_arrays):
            for j, (px, py) in enumerate(chips):
                outgoing.append((src_refs[i].at[2 * px + py], land_refs[i].at[j], (px, py, c), 3 * i + j, 3 * i + j))
                arrivals.append((land_refs[i].at[j], 3 * i + j))
        return outgoing, arrivals

    plan.counts = (3 * n_arrays, 3 * n_arrays)
    return plan


def _small_allreduce(vec):
    r, cdim = vec.shape
    n_dev = 8

    def body(v_ref, out_ref, buf, send, recv):
        x, y, c, _ = _place()
        me = 4 * x + 2 * y + c
        buf[me] = v_ref[...]
        cps = []
        for k in range(1, n_dev):
            dx, dy, dc = (k >> 2) & 1, (k >> 1) & 1, k & 1
            peer = (x ^ dx, y ^ dy, c ^ dc)
            cp = pltpu.make_async_remote_copy(
                src_ref=v_ref, dst_ref=buf.at[me], send_sem=send.at[k - 1], recv_sem=recv.at[k - 1],
                device_id=peer, device_id_type=MESH)
            cp.start()
            cps.append(cp)
        for k in range(1, n_dev):
            dx, dy, dc = (k >> 2) & 1, (k >> 1) & 1, k & 1
            src = 4 * (x ^ dx) + 2 * (y ^ dy) + (c ^ dc)
            slot = buf.at[src]
            pltpu.make_async_remote_copy(
                src_ref=slot, dst_ref=slot, send_sem=send.at[k - 1], recv_sem=recv.at[k - 1],
                device_id=(x ^ dx, y ^ dy, c ^ dc), device_id_type=MESH).wait_recv()
        for cp in cps:
            cp.wait_send()
        acc = buf[0]
        for k in range(1, n_dev):
            acc = acc + buf[k]
        out_ref[...] = acc

    vm = pl.BlockSpec(memory_space=pltpu.VMEM)
    return pl.pallas_call(
        body, name="small_allreduce", in_specs=[vm], out_specs=vm,
        out_shape=jax.ShapeDtypeStruct((r, cdim), F32),
        scratch_shapes=[pltpu.VMEM((n_dev, r, cdim), F32), pltpu.SemaphoreType.DMA((n_dev - 1,)),
                        pltpu.SemaphoreType.DMA((n_dev - 1,))],
    )(vec)


def _a_cols_to_head_major(w):
    lead = w.shape[:-1]
    q = w[..., :A_QK].reshape(lead + (A_HEADS, A_DK))
    k = w[..., A_QK:2 * A_QK].reshape(lead + (A_HEADS, A_DK))
    v = w[..., 2 * A_QK:2 * A_QK + A_VW].reshape(lead + (A_HEADS, A_DV))
    z = w[..., 2 * A_QK + A_VW:].reshape(lead + (A_HEADS, A_DV))
    return jnp.concatenate([q, k, v, z], axis=-1).reshape(lead + (A_HEADS * A_HEAD_COLS,))


def _a_cols_from_head_major(w):
    lead = w.shape[:-1]
    w = w.reshape(lead + (A_HEADS, A_HEAD_COLS))
    parts = [w[..., :A_DK], w[..., A_DK:2 * A_DK], w[..., 2 * A_DK:2 * A_DK + A_DV], w[..., 2 * A_DK + A_DV:]]
    return jnp.concatenate([p.reshape(lead + (-1,)) for p in parts], axis=-1)


def _conv_cols_to_head_major(w):
    lead = w.shape[:-1]
    q = w[..., :A_QK].reshape(lead + (A_HEADS, A_DK))
    k = w[..., A_QK:2 * A_QK].reshape(lead + (A_HEADS, A_DK))
    v = w[..., 2 * A_QK:].reshape(lead + (A_HEADS, A_DV))
    return jnp.concatenate([q, k, v], axis=-1).reshape(lead + (A_HEADS * A_CONV_COLS,))


def _conv_cols_from_head_major(w):
    lead = w.shape[:-1]
    w = w.reshape(lead + (A_HEADS, A_CONV_COLS))
    parts = [w[..., :A_DK], w[..., A_DK:2 * A_DK], w[..., 2 * A_DK:]]
    return jnp.concatenate([p.reshape(lead + (-1,)) for p in parts], axis=-1)


def _to_stream(a, bn, d):
    rest = a.shape[1:]
    s = a.shape[0] // bn
    a = a.reshape((bn, s // d, d) + rest)
    a = jnp.swapaxes(a, 1, 2)
    return a.reshape((bn * d, s // d) + rest)


def _from_stream(a, bn, d):
    rest = a.shape[2:]
    ln = a.shape[1]
    a = a.reshape((bn, d, ln) + rest)
    a = jnp.swapaxes(a, 1, 2)
    return a.reshape((bn * ln * d,) + rest)


B_SUB = 512
B_SHARD_BLOCKS = (3 * B_GROUPS * B_W + B_W) // N_CHIPS // B_SUB


def _b_block(gi, jj):
    nb = (B_GROUPS * (jj // 2) + gi) * 2 + jj % 2
    return nb // B_SHARD_BLOCKS, nb % B_SHARD_BLOCKS


def _shard_major(g, ncols):
    r = g.shape[0]
    return jnp.swapaxes(g.reshape(r, N_CHIPS, ncols), 0, 1)


def _pack_rows(items):
    rows, offs = [], []
    at = 0
    for a in items:
        flat = a.reshape(-1).astype(F32)
        nr = -(-flat.shape[0] // 1024) * 8
        flat = jnp.pad(flat, (0, nr * 128 - flat.shape[0]))
        rows.append(flat.reshape(nr, 128))
        offs.append((at, nr, a.shape))
        at += nr
    return jnp.concatenate(rows, axis=0), offs


def _unpack_rows(packed, offs):
    out = []
    for at, nr, shape in offs:
        size = int(np.prod(shape)) if len(shape) else 1
        out.append(packed[at:at + nr].reshape(-1)[:size].reshape(shape))
    return out


def _local_step(x, positions, loss_target, norm_g, a_log, a_dt_bias, a_norm_g, b_q_norm_g, b_k_norm_g,
                start_token, first_weights, late_weights, b_grads_ready, a_grads_ready):
    bn, s, d = x.shape
    t = bn * s
    n_chunks = s // A_CHUNK
    x0 = x.reshape(t, d)
    h0 = _rms_fwd(x0, norm_g[0:1] + start_token, "rms0_fwd")
    inv_freq = ROPE_THETA ** (-jnp.arange(0, ROPE_DIMS, 2, dtype=F32) / ROPE_DIMS)
    freq_row = jnp.concatenate([inv_freq, inv_freq, jnp.zeros((128 - ROPE_DIMS,), F32)]).reshape(1, 128)
    posf = jnp.broadcast_to(positions.astype(F32).reshape(t, 1), (t, 128)) + start_token
    tabs = _rope_tables(posf, freq_row)
    tabs_s = [tabs if dil == 1 else [_to_stream(tb, bn, dil).reshape(t, 128) for tb in tabs] for dil in B_DIL]
    wa_in, conv_w, late_token = first_weights([h0] + [tb for ts in tabs_s for tb in ts])
    wa_main = _a_cols_to_head_major(wa_in[:, :A_MAIN])
    wa_tail = jnp.pad(wa_in[:, A_MAIN:], ((0, 0), (0, 128 - 2 * A_HEADS))) + late_token.astype(BF16)
    cw_hm = _conv_cols_to_head_major(conv_w)

    proj_a = _matmul(h0, wa_main, "nn", F32, "a_in_main", tm=2048)
    tail_a = _matmul(h0, wa_tail, "nn", F32, "a_in_tail")
    tail_t = jnp.swapaxes(tail_a[:, :2 * A_HEADS].reshape(bn, s, 2 * A_HEADS), 1, 2)
    tail_t = tail_t.reshape(bn, 2 * A_HEADS, n_chunks, A_CHUNK)
    beta, gc = _gdn_prep(tail_t, a_log[0], a_dt_bias[0])
    proj_a3 = proj_a.reshape(bn, s, A_MAIN)
    og_a, oraw_a, states, t_mats, conv_y = _gdn_fwd(proj_a3, cw_hm, beta, gc, a_norm_g)
    wa_out, wb_in, wb_out = late_weights(og_a)
    b_cols = [4 * B_W] + [3 * B_W] * (B_GROUPS - 1)
    x1, h1 = _out_proj(og_a.reshape(t, A_VW), wa_out, x0, "a_out", norm_g=norm_g[1:2])

    h1_s, proj_b, qkv_b, o_b, lse_b = [], [], [], [], []
    for gi, dil in enumerate(B_DIL):
        hs = h1 if dil == 1 else _to_stream(h1, bn, dil).reshape(t, d)
        ts = tabs_s[gi]
        pj = _matmul(hs, wb_in, "nn", BF16, f"b_in_g{gi}", tm=4096, tn=B_SUB, n=b_cols[gi], b_spec=pl.BlockSpec(
            (None, d, B_SUB), lambda i, j, kk, gi=gi: (_b_block(gi, j)[0], kk, _b_block(gi, j)[1])))
        qkv = _qk_prep(pj, *ts, b_q_norm_g[0, gi:gi + 1], b_k_norm_g[0, gi:gi + 1], f"qk_prep_g{gi}")
        o_s, lse_s = _attn_fwd(qkv.reshape(bn * dil, s // dil, 3 * B_W), f"attn_fwd_g{gi}")
        h1_s.append(hs), proj_b.append(pj), qkv_b.append(qkv)
        o_b.append(o_s.reshape(t, B_W) if dil == 1 else _from_stream(o_s, bn, dil))
        lse_b.append(lse_s.reshape(t, B_HEADS) if dil == 1 else _from_stream(lse_s, bn, dil))
    og_b = _merge_fwd(o_b, lse_b, proj_b[0])
    d_x2, loss_parts = _out_proj(og_b, wb_out, x1, "b_out_loss", target=loss_target.reshape(t, d))
    loss_local = jnp.sum(loss_parts)

    d_x2b = d_x2.astype(BF16)
    g_wb_out = _matmul(og_b, d_x2b, "tn", F32, "b_out_dw")
    d_og_b = _matmul(d_x2b, wb_out, "nt", BF16, "b_out_dx")
    d_o, lse_joint, delta, d_z = _merge_bwd(o_b, lse_b, proj_b[0], d_og_b)
    d_h1, g_qn, g_kn = [], [], []
    g_wb_in = lax.empty(wb_in.shape, F32)
    for gi, dil in enumerate(B_DIL):
        if dil == 1:
            do_s, lj_s, dl_s = d_o, lse_joint, delta
        else:
            do_s, lj_s, dl_s = (_to_stream(a, bn, dil).reshape(t, -1) for a in (d_o, lse_joint, delta))
        ns, ln = bn * dil, s // dil
        dq, dk, dv = _attn_bwd(qkv_b[gi].reshape(ns, ln, 3 * B_W), do_s.reshape(ns, ln, B_W),
                               lj_s.reshape(ns, ln, B_HEADS), dl_s.reshape(ns, ln, B_HEADS), f"attn_bwd_g{gi}")
        d_pj, d_gain = _qk_prep_bwd(proj_b[gi], *tabs_s[gi], b_q_norm_g[0, gi:gi + 1], b_k_norm_g[0, gi:gi + 1],
                                    dq.reshape(t, B_W), dk.reshape(t, B_W), dv.reshape(t, B_W),
                                    d_z if gi == 0 else None, f"qk_prep_bwd_g{gi}")
        g_wb_in = _matmul(h1_s[gi], d_pj, "tn", F32, f"b_in_dw_g{gi}", tn=B_SUB, tk=DW_K, into=(g_wb_in, pl.BlockSpec(
            (None, d, B_SUB), lambda i, j, kk, gi=gi: (_b_block(gi, j)[0], i, _b_block(gi, j)[1]))))
        dh = _matmul(d_pj, wb_in, "nt", BF16, f"b_in_dx_g{gi}", tm=2048, tk=B_SUB, n=d, b_spec=pl.BlockSpec(
            (None, d, B_SUB), lambda i, j, kk, gi=gi: (_b_block(gi, kk)[0], j, _b_block(gi, kk)[1])))
        d_h1.append(dh if dil == 1 else _from_stream(dh.reshape(ns, ln, d), bn, dil))
        g_qn.append(d_gain[0]), g_kn.append(d_gain[1])
    d_x1, g_norm1 = _rms_bwd(x1, norm_g[1:2], d_h1, d_x2, "rms1_bwd")

    d_x1b = d_x1.astype(BF16)
    g_wa_out = _matmul(og_a.reshape(t, A_VW), d_x1b, "tn", F32, "a_out_dw", tk=DW_K)
    b_token = b_grads_ready(g_wb_in, g_wb_out, g_wa_out)
    d_og_a = _matmul(d_x1b, wa_out, "nt", BF16, "a_out_dx")
    d_pa, d_gc, d_beta, d_cw, d_ng = _gdn_bwd(proj_a3, cw_hm, beta, gc, a_norm_g + b_token, oraw_a, states,
                                              t_mats, conv_y, d_og_a.reshape(bn, s, A_VW))
    d_tail_t, d_alog, d_dtb = _gdn_prep_bwd(tail_t, a_log[0], a_dt_bias[0], d_gc, d_beta)
    d_tail = jnp.swapaxes(d_tail_t.reshape(bn, 2 * A_HEADS, s), 1, 2).reshape(t, 2 * A_HEADS)
    d_tail = jnp.pad(d_tail, ((0, 0), (0, 128 - 2 * A_HEADS))).astype(BF16)
    d_pa = d_pa.reshape(t, A_MAIN)
    g_wa_main = _matmul(h0, d_pa, "tn", F32, "a_in_dw_main", tk=DW_K)
    g_wa_tail = _matmul(h0, d_tail, "tn", F32, "a_in_dw_tail", tk=DW_K)
    g_wa_in = jnp.concatenate([_a_cols_from_head_major(g_wa_main), g_wa_tail[:, :2 * A_HEADS]], axis=1)
    a_token = a_grads_ready(g_wa_in)
    d_h0t = _matmul(d_tail + a_token.astype(BF16), wa_tail, "nt", F32, "a_in_dx_tail")
    d_x0, g_norm0 = _in_proj_bwd(d_pa, wa_main, d_h0t, x0, norm_g[0:1], d_x1, "a_in_dx_rms0_bwd")

    gfull = {
        "norm_g": jnp.concatenate([g_norm0, g_norm1], axis=0), "a_w_in": g_wa_in,
        "a_conv_w": _conv_cols_from_head_major(jnp.sum(d_cw, axis=0)),
        "a_log": jnp.sum(d_alog[:, :, 0], axis=0), "a_dt_bias": jnp.sum(d_dtb[:, :, 0], axis=0),
        "a_norm_g": jnp.sum(d_ng[:, :, 0, :], axis=(0, 1)), "a_w_out": g_wa_out, "b_w_in": g_wb_in,
        "b_q_norm_g": jnp.stack(g_qn), "b_k_norm_g": jnp.stack(g_kn), "b_w_out": g_wb_out}
    return loss_local, d_x0.reshape(bn, s, d), gfull


def kernel(x, positions, norm_g, a_w_in, a_conv_w, a_log, a_dt_bias, a_norm_g, a_w_out, b_w_in, b_q_norm_g, b_k_norm_g, b_w_out, loss_target, m_norm_g, m_a_w_in, m_a_conv_w, m_a_log, m_a_dt_bias, m_a_norm_g, m_a_w_out, m_b_w_in, m_b_q_norm_g, m_b_k_norm_g, m_b_w_out, v_norm_g, v_a_w_in, v_a_conv_w, v_a_log, v_a_dt_bias, v_a_norm_g, v_a_w_out, v_b_w_in, v_b_q_norm_g, v_b_k_norm_g, v_b_w_out):
    d = x.shape[2]
    my_c = lax.axis_index("c")
    my_chip = 2 * lax.axis_index("x") + lax.axis_index("y")

    half_index = jnp.reshape(my_c, (1,)).astype(jnp.int32)
    chip_index = jnp.reshape(my_chip, (1,)).astype(jnp.int32)
    def landing(shard):
        return lax.dynamic_update_slice(lax.empty((N_CHIPS,) + shard.shape, shard.dtype), shard[None],
                                        (my_chip,) + (0,) * shard.ndim)

    first_shards = [a_w_in[0].astype(BF16), a_conv_w[0]]
    first_plan = _first_gather_plan()
    first = _split_copy_start("first_weights_start", first_plan, first_shards,
                              [landing(s) for s in first_shards], half_index)
    pending = {}
    late_shards = [(w[0] + first[4][0, 0]).astype(BF16) for w in (a_w_out, b_w_in, b_w_out)]
    late_lands = [landing(s) for s in late_shards]

    def first_weights(after):
        _, (ga_in, g_conv) = _split_copy_wait("first_weights_wait", first_plan, *first[:4],
                                              list(after) + late_lands)
        ga_in = _sibling_forward(ga_in)
        wa_in = jnp.concatenate([ga_in[k] for k in range(N_CHIPS)], axis=1)
        conv_w = jnp.concatenate([g_conv[k] for k in range(N_CHIPS)], axis=1)
        plan = _gather_plan(len(late_shards))
        pending["late"] = (plan,) + tuple(_split_copy_start(
            "late_weights_start", plan, late_shards, late_lands, conv_w))
        return wa_in, conv_w, pending["late"][5][0, 0]

    def late_weights(after):
        plan, send, recv, srcs, lands, _ = pending["late"]
        _, (ga_out, gb_in, gb_out) = _split_copy_wait("late_weights_wait", plan, send, recv, srcs, lands, after)
        return ga_out.reshape(A_VW, d), gb_in, gb_out.reshape(B_W, d)

    def reduce_to_chip_sums(mats, tag):
        half = lambda g: lax.dynamic_slice_in_dim(g, (1 - my_c) * (g.shape[1] // 2), g.shape[1] // 2, axis=1)
        recv_sib = _sibling_swap([half(g).astype(BF16) for g in mats], f"grad_{tag}_sibling_swap")
        return [_pair_sum(g, r, half_index, f"grad_{tag}_pair_sum_{i}") for i, (g, r) in enumerate(zip(mats, recv_sib))]

    def start_exchange(tag, mats):
        sums = reduce_to_chip_sums(mats, tag)
        lands = [lax.empty((N_CHIPS - 1,) + s.shape[1:], BF16) for s in sums]
        plan = _exchange_plan(len(mats))
        pending[tag] = (plan,) + tuple(_split_copy_start(f"grad_{tag}_exchange_start", plan, sums, lands, chip_index))
        return pending[tag][5][0, 0]

    def finish_exchange(tag, after):
        plan, send, recv, srcs, lands, _ = pending[tag]
        return _split_copy_wait(f"grad_{tag}_exchange_wait", plan, send, recv, srcs, lands, after)

    def b_grads_ready(g_wb_in, g_wb_out, g_wa_out):
        return start_exchange("b", [g_wb_in, g_wb_out.reshape(N_CHIPS, -1, d), g_wa_out.reshape(N_CHIPS, -1, d)])

    def a_grads_ready(g_wa_in):
        return start_exchange("a", [_shard_major(g_wa_in, a_w_in.shape[2])])

    loss_local, d_x0, gfull = _local_step(x, positions, loss_target, norm_g, a_log, a_dt_bias, a_norm_g,
                                          b_q_norm_g, b_k_norm_g, first[4][0, 0], first_weights, late_weights,
                                          b_grads_ready, a_grads_ready)

    small = [gfull["norm_g"], gfull["a_conv_w"], gfull["a_log"], gfull["a_dt_bias"], gfull["a_norm_g"],
             gfull["b_q_norm_g"], gfull["b_k_norm_g"], loss_local]
    packed, offs = _pack_rows(small)
    reduced = _small_allreduce(packed)
    g_norm, g_conv_all, g_alog, g_dtb, g_ang, g_q, g_k, loss = _unpack_rows(reduced, offs)
    g_conv_mine = lax.dynamic_slice_in_dim(g_conv_all, my_chip * a_conv_w.shape[2], a_conv_w.shape[2], axis=1)

    b_sums, b_received = finish_exchange("b", d_x0)
    a_sums, a_received = finish_exchange("a", reduced)
    chip_sums = [a_sums[0], b_sums[2], b_sums[0], b_sums[1]]
    received = [a_received[0], b_received[2], b_received[0], b_received[1]]
    halves = [_chip_sum(s, r, chip_index, f"grad_chip_sum_{i}") for i, (s, r) in enumerate(zip(chip_sums, received))]
    theirs = _sibling_swap(halves, "grad_sibling_join")
    big = ("a_w_in", "a_w_out", "b_w_in", "b_w_out")
    big_halves = dict(zip(big, zip(halves, theirs)))

    grads = {
        "norm_g": g_norm, "a_conv_w": g_conv_mine[None], "a_log": g_alog[None], "a_dt_bias": g_dtb[None],
        "a_norm_g": g_ang[None], "b_q_norm_g": g_q[None], "b_k_norm_g": g_k[None]}
    weights = {"norm_g": norm_g, "a_w_in": a_w_in, "a_conv_w": a_conv_w, "a_log": a_log, "a_dt_bias": a_dt_bias,
               "a_norm_g": a_norm_g, "a_w_out": a_w_out, "b_w_in": b_w_in, "b_q_norm_g": b_q_norm_g,
               "b_k_norm_g": b_k_norm_g, "b_w_out": b_w_out}
    m_in = {"norm_g": m_norm_g, "a_w_in": m_a_w_in, "a_conv_w": m_a_conv_w, "a_log": m_a_log,
            "a_dt_bias": m_a_dt_bias, "a_norm_g": m_a_norm_g, "a_w_out": m_a_w_out, "b_w_in": m_b_w_in,
            "b_q_norm_g": m_b_q_norm_g, "b_k_norm_g": m_b_k_norm_g, "b_w_out": m_b_w_out}
    v_in = {"norm_g": v_norm_g, "a_w_in": v_a_w_in, "a_conv_w": v_a_conv_w, "a_log": v_a_log,
            "a_dt_bias": v_a_dt_bias, "a_norm_g": v_a_norm_g, "a_w_out": v_a_w_out, "b_w_in": v_b_w_in,
            "b_q_norm_g": v_b_q_norm_g, "b_k_norm_g": v_b_k_norm_g, "b_w_out": v_b_w_out}
    names = list(weights)

    delta_w, new_m, new_v = {}, {}, {}
    for nm in big:
        mine, other = big_halves[nm]
        if weights[nm].shape[2] % 128:
            cols = lambda a: jnp.transpose(a, (2, 0, 1))
            half_cols = lambda a: jnp.transpose(a)[:, None, :]
            outs = _adamw_shard_cols(cols(weights[nm]), half_cols(mine), half_cols(other), cols(m_in[nm]),
                                     cols(v_in[nm]), half_index, f"adamw_{nm}")
            outs = [jnp.transpose(o, (1, 2, 0)) for o in outs]
        else:
            outs = _adamw_shard(weights[nm], mine, other, m_in[nm], v_in[nm], half_index, f"adamw_{nm}")
        grads[nm], delta_w[nm], new_m[nm], new_v[nm] = outs
    small_names = [nm for nm in names if nm not in big]
    packs = [_pack_rows([src[nm] for nm in small_names]) for src in (weights, grads, m_in, v_in)]
    offs = packs[0][1]
    dl, m2, v2 = _adamw(packs[0][0], packs[1][0], packs[2][0], packs[3][0], "adamw_small")
    for nm, a, b, c2 in zip(small_names, _unpack_rows(dl, offs), _unpack_rows(m2, offs), _unpack_rows(v2, offs)):
        delta_w[nm], new_m[nm], new_v[nm] = a, b, c2

    return (loss, d_x0, *[grads[nm] for nm in names], *[delta_w[nm] for nm in names],
            *[new_m[nm] for nm in names], *[new_v[nm] for nm in names])
```

```python
import jax
import jax.numpy as jnp
import numpy as np
from jax import lax
from jax.experimental import pallas as pl
from jax.experimental.pallas import tpu as pltpu

F32 = jnp.float32
BF16 = jnp.bfloat16
MESH = pl.DeviceIdType.MESH

EPS = 1e-6
A_HEADS = 8
A_DK = 128
A_DV = 256
A_QK = A_HEADS * A_DK
A_VW = A_HEADS * A_DV
A_MAIN = 2 * A_QK + 2 * A_VW
A_HEAD_COLS = 2 * A_DK + 2 * A_DV
A_CONV_COLS = 2 * A_DK + A_DV
A_CHUNK = 64
A_CONV = 4
B_GROUPS = 3
B_HEADS = 8
B_DH = 128
B_W = B_HEADS * B_DH
B_DIL = (1, 4, 16)
B_BLK = 128
ROPE_THETA = 500000.0
ROPE_DIMS = B_DH // 4
ADAM_LR, ADAM_B1, ADAM_B2, ADAM_EPS, ADAM_WD, ADAM_STEP = 0.001, 0.9, 0.999, 1e-08, 0.01, 10
N_CHIPS = 4
VMEM_BIG = 56 * 1024 * 1024
DW_K = 4096


def _params(sem=None, vmem=None):
    return pltpu.CompilerParams(dimension_semantics=sem, vmem_limit_bytes=vmem)


def _dot(a, b, ca, cb):
    return lax.dot_general(a.astype(BF16), b.astype(BF16), (((ca,), (cb,)), ((), ())),
                           preferred_element_type=F32)


def _split3(a):
    hi = a.astype(BF16)
    r = a - hi.astype(F32)
    mid = r.astype(BF16)
    lo = (r - mid.astype(F32)).astype(BF16)
    return hi, mid, lo


def _sigmoid(y):
    return 1.0 / (1.0 + jnp.exp(-y))


def _silu(y):
    return y * _sigmoid(y)


def _silu_and_slope(y):
    s = _sigmoid(y)
    return y * s, s * (1.0 + y * (1.0 - s))


def _matmul(a, b, mode, out_dtype, name, res=None, tm=1024, tn=1024, tk=1024, n=None, b_spec=None, into=None):
    m, k = a.shape[::-1] if mode == "tn" else a.shape
    if n is None:
        n = b.shape[0] if mode == "nt" else b.shape[1]
    tm, tn, tk = min(tm, m), min(tn, n), min(tk, k)
    assert m % tm == 0 and n % tn == 0 and k % tk == 0, (name, a.shape, b.shape)
    nk = k // tk
    dims = {"nn": ((1,), (0,)), "nt": ((1,), (1,)), "tn": ((0,), (0,))}[mode]

    def body(*refs):
        a_ref, b_ref = refs[0], refs[1]
        r_ref = refs[2] if res is not None else None
        o_ref = refs[2 + (res is not None) + (into is not None)]
        prod = lax.dot_general(a_ref[...], b_ref[...], (dims, ((), ())), preferred_element_type=F32)

        def finish(r):
            if res is not None:
                r = r + r_ref[...]
            o_ref[...] = r.astype(out_dtype)

        if nk == 1:
            finish(prod)
            return
        acc = refs[-1]
        kk = pl.program_id(2)

        @pl.when(kk == 0)
        def _():
            acc[...] = prod

        @pl.when((kk > 0) & (kk < nk - 1))
        def _():
            acc[...] += prod

        @pl.when(kk == nk - 1)
        def _():
            finish(acc[...] + prod)

    a_spec = pl.BlockSpec((tm, tk), lambda i, j, kk: (i, kk))
    if mode == "tn":
        a_spec = pl.BlockSpec((tk, tm), lambda i, j, kk: (kk, i))
    if b_spec is None and mode == "nt":
        b_spec = pl.BlockSpec((tn, tk), lambda i, j, kk: (j, kk))
    elif b_spec is None:
        b_spec = pl.BlockSpec((tk, tn), lambda i, j, kk: (kk, j))
    in_specs = [a_spec, b_spec]
    args = [a, b]
    if res is not None:
        in_specs.append(pl.BlockSpec((tm, tn), lambda i, j, kk: (i, j)))
        args.append(res)
    out_spec = pl.BlockSpec((tm, tn), lambda i, j, kk: (i, j))
    out_shape = jax.ShapeDtypeStruct((m, n), out_dtype)
    aliases = {}
    if into is not None:
        assert res is None
        buf, out_spec = into
        out_shape = jax.ShapeDtypeStruct(buf.shape, buf.dtype)
        in_specs.append(ANY)
        args.append(buf)
        aliases = {2: 0}
    return pl.pallas_call(
        body, name=name, grid=(m // tm, n // tn, nk),
        in_specs=in_specs, out_specs=out_spec, out_shape=out_shape, input_output_aliases=aliases,
        scratch_shapes=[pltpu.VMEM((tm, tn), F32)] if nk > 1 else [],
        compiler_params=_params(("parallel", "parallel", "arbitrary"), 48 * 1024 * 1024),
    )(*args)


def _rms_fwd(x, g, name, tm=512):
    t, d = x.shape

    def body(x_ref, g_ref, h_ref):
        xv = x_ref[...]
        r = lax.rsqrt(jnp.mean(xv * xv, axis=-1, keepdims=True) + EPS)
        h_ref[...] = (xv * r * g_ref[...]).astype(BF16)

    return pl.pallas_call(
        body, name=name, grid=(t // tm,),
        in_specs=[pl.BlockSpec((tm, d), lambda i: (i, 0)), pl.BlockSpec((1, d), lambda i: (0, 0))],
        out_specs=pl.BlockSpec((tm, d), lambda i: (i, 0)),
        out_shape=jax.ShapeDtypeStruct((t, d), BF16),
        compiler_params=_params(("parallel",)),
    )(x, g)


def _rms_bwd(x, g, dhs, dres, name, tm=512):
    t, d = x.shape
    n_dh = len(dhs)

    def body(*refs):
        x_ref, g_ref = refs[0], refs[1]
        dh_refs = refs[2:2 + n_dh]
        dres_ref, dx_ref, dg_ref = refs[2 + n_dh:]
        i = pl.program_id(0)

        @pl.when(i == 0)
        def _():
            dg_ref[...] = jnp.zeros_like(dg_ref)

        xv = x_ref[...]
        r = lax.rsqrt(jnp.mean(xv * xv, axis=-1, keepdims=True) + EPS)
        xh = xv * r
        dh = dh_refs[0][...].astype(F32)
        for ref in dh_refs[1:]:
            dh = dh + ref[...].astype(F32)
        dg_ref[0:1, :] += jnp.sum(dh * xh, axis=0, keepdims=True)
        dxh = dh * g_ref[...]
        dx = r * (dxh - xh * jnp.mean(dxh * xh, axis=-1, keepdims=True))
        dx_ref[...] = dx + dres_ref[...]

    row = pl.BlockSpec((tm, d), lambda i: (i, 0))
    dx, dg = pl.pallas_call(
        body, name=name, grid=(t // tm,),
        in_specs=[row, pl.BlockSpec((1, d), lambda i: (0, 0))] + [row] * n_dh + [row],
        out_specs=[row, pl.BlockSpec((8, d), lambda i: (0, 0))],
        out_shape=[jax.ShapeDtypeStruct((t, d), F32), jax.ShapeDtypeStruct((8, d), F32)],
        compiler_params=_params(("arbitrary",)),
    )(x, g, *dhs, dres)
    return dx, dg[0:1]


def _b_in_dx(dp, w_shards, gi, name, tm=2048):
    t, k = dp.shape
    d = w_shards.shape[1]
    nk = k // (2 * B_SUB)
    dn = (((1,), (1,)), ((), ()))

    def body(a_ref, w0_ref, w1_ref, o_ref, acc):
        kk = pl.program_id(1)
        prod = (lax.dot_general(a_ref[:, 0:B_SUB], w0_ref[...], dn, preferred_element_type=F32)
                + lax.dot_general(a_ref[:, B_SUB:2 * B_SUB], w1_ref[...], dn, preferred_element_type=F32))

        @pl.when(kk == 0)
        def _():
            acc[...] = prod

        @pl.when((kk > 0) & (kk < nk - 1))
        def _():
            acc[...] += prod

        @pl.when(kk == nk - 1)
        def _():
            o_ref[...] = (acc[...] + prod).astype(BF16)

    w_block = lambda part: pl.BlockSpec(
        (None, d, B_SUB), lambda i, kk: (_b_block(gi, 2 * kk + part)[0], 0, _b_block(gi, 2 * kk + part)[1]))
    return pl.pallas_call(
        body, name=name, grid=(t // tm, nk),
        in_specs=[pl.BlockSpec((tm, 2 * B_SUB), lambda i, kk: (i, kk)), w_block(0), w_block(1)],
        out_specs=pl.BlockSpec((tm, d), lambda i, kk: (i, 0)),
        out_shape=jax.ShapeDtypeStruct((t, d), BF16),
        scratch_shapes=[pltpu.VMEM((tm, d), F32)],
        compiler_params=_params(("parallel", "arbitrary"), 48 * 1024 * 1024),
    )(dp, w_shards, w_shards)


def _in_proj_bwd(dp, w, dh_more, x, g, dres, name, tm=512, tk=3072):
    t, k = dp.shape
    d = w.shape[0]
    nk = k // tk

    def body(dp_ref, w_ref, more_ref, x_ref, g_ref, dres_ref, dx_ref, dg_ref, acc):
        i, kk = pl.program_id(0), pl.program_id(1)

        @pl.when((i == 0) & (kk == 0))
        def _():
            dg_ref[...] = jnp.zeros_like(dg_ref)

        prod = lax.dot_general(dp_ref[...], w_ref[...], (((1,), (1,)), ((), ())), preferred_element_type=F32)

        @pl.when(kk == 0)
        def _():
            acc[...] = prod

        @pl.when((kk > 0) & (kk < nk - 1))
        def _():
            acc[...] += prod

        @pl.when(kk == nk - 1)
        def _():
            dh = acc[...] + prod + more_ref[...]
            xv = x_ref[...]
            r = lax.rsqrt(jnp.mean(xv * xv, axis=-1, keepdims=True) + EPS)
            xh = xv * r
            dg_ref[0:1, :] += jnp.sum(dh * xh, axis=0, keepdims=True)
            dxh = dh * g_ref[...]
            dx_ref[...] = r * (dxh - xh * jnp.mean(dxh * xh, axis=-1, keepdims=True)) + dres_ref[...]

    row = pl.BlockSpec((tm, d), lambda i, kk: (i, 0))
    dx, dg = pl.pallas_call(
        body, name=name, grid=(t // tm, nk),
        in_specs=[pl.BlockSpec((tm, tk), lambda i, kk: (i, kk)), pl.BlockSpec((d, tk), lambda i, kk: (0, kk)),
                  row, row, pl.BlockSpec((1, d), lambda i, kk: (0, 0)), row],
        out_specs=[row, pl.BlockSpec((8, d), lambda i, kk: (0, 0))],
        out_shape=[jax.ShapeDtypeStruct((t, d), F32), jax.ShapeDtypeStruct((8, d), F32)],
        scratch_shapes=[pltpu.VMEM((tm, d), F32)],
        compiler_params=_params(("arbitrary", "arbitrary"), 48 * 1024 * 1024),
    )(dp, w, dh_more, x, g, dres)
    return dx, dg[0:1]


def _out_proj(a, w, res, name, norm_g=None, target=None, tm=512):
    t, k = a.shape
    d = w.shape[1]
    nb = t // tm

    def body(a_ref, w_ref, r_ref, x_ref, o1_ref, o2_ref):
        y = jnp.dot(a_ref[...], w_ref[...], preferred_element_type=F32) + r_ref[...]
        if norm_g is not None:
            o1_ref[...] = y
            r = lax.rsqrt(jnp.mean(y * y, axis=-1, keepdims=True) + EPS)
            o2_ref[...] = (y * r * x_ref[...]).astype(BF16)
        else:
            e = y - x_ref[...]
            o1_ref[...] = e * (1.0 / d)
            s = jnp.sum(jnp.sum(e * e, axis=1, keepdims=True), axis=0, keepdims=True) * (0.5 / d)
            o2_ref[...] = jnp.broadcast_to(s, (8, 128))

    row = pl.BlockSpec((tm, d), lambda i: (i, 0))
    if norm_g is not None:
        extra, extra_spec = norm_g, pl.BlockSpec((1, d), lambda i: (0, 0))
        out2_spec, out2_shape = row, jax.ShapeDtypeStruct((t, d), BF16)
    else:
        extra, extra_spec = target, row
        out2_spec = pl.BlockSpec((None, 8, 128), lambda i: (i, 0, 0))
        out2_shape = jax.ShapeDtypeStruct((nb, 8, 128), F32)
    o1, o2 = pl.pallas_call(
        body, name=name, grid=(nb,),
        in_specs=[pl.BlockSpec((tm, k), lambda i: (i, 0)), pl.BlockSpec((k, d), lambda i: (0, 0)), row, extra_spec],
        out_specs=[row, out2_spec], out_shape=[jax.ShapeDtypeStruct((t, d), F32), out2_shape],
        compiler_params=_params(("parallel",), 48 * 1024 * 1024),
    )(a, w, res, extra)
    return (o1, o2) if norm_g is not None else (o1, o2[:, 0, 0])


def _softplus(x):
    t = jnp.exp(-jnp.abs(x))
    return jnp.maximum(x, 0.0) + jnp.where(t < 1e-3, t * (1.0 - 0.5 * t), jnp.log(1.0 + t))


def _tri(rows_le_cols):
    r = lax.broadcasted_iota(jnp.int32, (A_CHUNK, A_CHUNK), 0)
    c = lax.broadcasted_iota(jnp.int32, (A_CHUNK, A_CHUNK), 1)
    return jnp.where((r <= c) if rows_le_cols else (r >= c), 1.0, 0.0).astype(BF16)


def _dot_exact_rhs(a, ones_bf16):
    dn = (((1,), (0,)), ((), ()))
    hi, mid, lo = _split3(a)
    out = lax.dot_general(hi, ones_bf16, dn, preferred_element_type=F32)
    out = out + lax.dot_general(mid, ones_bf16, dn, preferred_element_type=F32)
    return out + lax.dot_general(lo, ones_bf16, dn, preferred_element_type=F32)


def _gdn_prep(tail_t, a_log, dt_bias):
    bn, _, n, c = tail_t.shape

    def body(t_ref, alog_ref, dtb_ref, beta_ref, gc_ref):
        upper = _tri(True)
        for h in range(A_HEADS):
            beta_ref[h] = _sigmoid(t_ref[h])
            ea = jnp.exp(jnp.full((n, c), alog_ref[h], F32))
            g = -ea * _softplus(t_ref[A_HEADS + h] + dtb_ref[h])
            gc_ref[h] = _dot_exact_rhs(g, upper)

    smem = pl.BlockSpec(memory_space=pltpu.SMEM)
    blk = pl.BlockSpec((None, A_HEADS, n, c), lambda b: (b, 0, 0, 0))
    return pl.pallas_call(
        body, name="gdn_prep", grid=(bn,),
        in_specs=[pl.BlockSpec((None, 2 * A_HEADS, n, c), lambda b: (b, 0, 0, 0)), smem, smem],
        out_specs=[blk, blk],
        out_shape=[jax.ShapeDtypeStruct((bn, A_HEADS, n, c), F32)] * 2,
        compiler_params=_params(("parallel",)),
    )(tail_t, a_log, dt_bias)


def _gdn_prep_bwd(tail_t, a_log, dt_bias, d_gc, d_beta):
    bn, _, n, c = tail_t.shape

    def body(t_ref, alog_ref, dtb_ref, dgc_ref, dbeta_ref, dt_ref, dal_ref, ddt_ref):
        lower = _tri(False)
        for h in range(A_HEADS):
            beta = _sigmoid(t_ref[h])
            dt_ref[h] = dbeta_ref[h] * beta * (1.0 - beta)
            dg = _dot_exact_rhs(dgc_ref[h], lower)
            ea = jnp.exp(jnp.full((n, c), alog_ref[h], F32))
            xa = t_ref[A_HEADS + h] + dtb_ref[h]
            g = -ea * _softplus(xa)
            dxa = -ea * dg * _sigmoid(xa)
            dt_ref[A_HEADS + h] = dxa
            s1 = jnp.sum(jnp.sum(g * dg, axis=1, keepdims=True), axis=0, keepdims=True)
            s2 = jnp.sum(jnp.sum(dxa, axis=1, keepdims=True), axis=0, keepdims=True)
            dal_ref[h:h + 1, :] = jnp.broadcast_to(s1, (1, 128))
            ddt_ref[h:h + 1, :] = jnp.broadcast_to(s2, (1, 128))

    smem = pl.BlockSpec(memory_space=pltpu.SMEM)
    blk8 = pl.BlockSpec((None, A_HEADS, n, c), lambda b: (b, 0, 0, 0))
    blk16 = pl.BlockSpec((None, 2 * A_HEADS, n, c), lambda b: (b, 0, 0, 0))
    sm = pl.BlockSpec((None, A_HEADS, 128), lambda b: (b, 0, 0))
    return pl.pallas_call(
        body, name="gdn_prep_bwd", grid=(bn,),
        in_specs=[blk16, smem, smem, blk8, blk8],
        out_specs=[blk16, sm, sm],
        out_shape=[jax.ShapeDtypeStruct((bn, 2 * A_HEADS, n, c), F32),
                   jax.ShapeDtypeStruct((bn, A_HEADS, 128), F32),
                   jax.ShapeDtypeStruct((bn, A_HEADS, 128), F32)],
        compiler_params=_params(("parallel",)),
    )(tail_t, a_log, dt_bias, d_gc, d_beta)


HALO = 8


def _conv_taps(xw, w):
    y = w[A_CONV - 1:A_CONV, :] * xw
    for j in range(1, A_CONV):
        y = y + w[A_CONV - 1 - j:A_CONV - j, :] * pltpu.roll(xw, j, 0)
    return y[HALO:, :]


def _row_to_col(row, eye):
    c = eye.shape[0]
    return jnp.sum(jnp.where(eye, jnp.broadcast_to(row, (c, c)), 0.0), axis=1, keepdims=True)


def _col_to_row(col, eye):
    c = eye.shape[0]
    return jnp.sum(jnp.where(eye, jnp.broadcast_to(col, (c, c)), 0.0), axis=0, keepdims=True)


def _unit_lower_inverse(a, ri, ci):
    eye = jnp.where(ri == ci, 1.0, 0.0)
    a8 = jnp.where((ri >> 3) == (ci >> 3), a, 0.0)
    a2 = _dot(a8, a8, 1, 0)
    yield
    a4 = _dot(a2, a2, 1, 0)
    t = eye - a8
    t = t + _dot(t, a2, 1, 0)
    yield
    t = t + _dot(t, a4, 1, 0)
    yield
    for sh in (3, 4, 5):
        off = jnp.where(((ri >> (sh + 1)) == (ci >> (sh + 1))) & ((ri >> sh) != (ci >> sh)), a, 0.0)
        left = _dot(t, off, 1, 0)
        yield
        t = t - _dot(left, t, 1, 0)
        yield
    return t


def _round_robin(gens):
    live = list(gens)
    while live:
        nxt = []
        for g in live:
            try:
                next(g)
                nxt.append(g)
            except StopIteration:
                pass
        live = nxt


def _gdn_chunk_core(q, k, v, g_row, b_row, t_mat, ri, ci):
    eye = ri == ci
    g_col = _row_to_col(g_row, eye)
    b_col = _row_to_col(b_row, eye)
    causal = ri >= ci
    strict = ri > ci
    dec = jnp.where(causal, jnp.exp(jnp.where(causal, g_col - g_row, 0.0)), 0.0)
    gam = jnp.exp(g_col)
    g_last = g_row[:, A_CHUNK - 1:A_CHUNK]
    gam_last = jnp.exp(g_last)
    e = jnp.exp(g_last - g_col)
    kb = k * b_col
    bv = v * b_col
    kbg = kb * gam
    q16, k16, kb16 = q.astype(BF16), k.astype(BF16), kb.astype(BF16)
    kk = _dot(kb16, k16, 1, 1)
    p = _dot(q16, k16, 1, 1) * dec
    yield
    a_mat = jnp.where(strict, kk * dec, 0.0)
    if t_mat is None:
        t_mat = yield from _unit_lower_inverse(a_mat, ri, ci)
    t16 = t_mat.astype(BF16)
    u = _dot(t16, bv, 1, 0)
    w = _dot(t16, kbg, 1, 0)
    yield
    return dict(eye=eye, g_col=g_col, b_col=b_col, dec=dec, strict=strict, causal=causal, gam=gam,
                gam_last=gam_last, e=e, kb=kb, bv=bv, kbg=kbg, a_mat=a_mat, t_mat=t_mat, u=u, w=w, p=p,
                qg=q * gam, kd=k * e, q16=q16, k16=k16, kb16=kb16, t16=t16)


A_SEQ_BLK = 256
A_BLK_CHUNKS = A_SEQ_BLK // A_CHUNK


def _gdn_halo(proj_hm):
    bn, s, w = proj_hm.shape
    last = proj_hm.reshape(bn, s // A_SEQ_BLK, A_SEQ_BLK, w)[:, :, A_SEQ_BLK - HALO:, :]
    return jnp.concatenate([jnp.zeros((bn, 1, HALO, w), proj_hm.dtype), last[:, :-1]], axis=1)


def _gdn_window(x_ref, halo_ref, ci, first, lo):
    if first:
        return jnp.concatenate([halo_ref[:, lo:lo + A_CONV_COLS], x_ref[0:A_CHUNK, lo:lo + A_CONV_COLS]], axis=0)
    start = pl.multiple_of(ci * A_CHUNK - HALO, HALO)
    return x_ref[pl.ds(start, A_CHUNK + HALO), lo:lo + A_CONV_COLS]


def _gdn_chunk_prep(xw, cw, y=None):
    if y is None:
        y = _conv_taps(xw, cw)
    a, slope = _silu_and_slope(y)
    aq, ak, v = a[:, 0:A_DK], a[:, A_DK:2 * A_DK], a[:, 2 * A_DK:]
    rq = lax.rsqrt(jnp.sum(aq * aq, axis=1, keepdims=True) + EPS)
    rk = lax.rsqrt(jnp.sum(ak * ak, axis=1, keepdims=True) + EPS)
    return dict(xw=xw, y=y, slope=slope, aq=aq, ak=ak, rq=rq, rk=rk, q=aq * rq * (A_DK ** -0.5), k=ak * rk, v=v)


def _gdn_fwd(proj_hm, cw_hm, beta, gc, norm_g, hp=8):
    bn, s, _ = proj_hm.shape
    n = s // A_CHUNK
    nsb = s // A_SEQ_BLK
    halo = _gdn_halo(proj_hm)

    def body(x_ref, halo_ref, cw_ref, beta_ref, gc_ref, ng_ref, og_ref, oraw_ref, st_ref, t_ref, y_ref, state):
        first_chunk = pl.program_id(2) * A_BLK_CHUNKS
        ri = lax.broadcasted_iota(jnp.int32, (A_CHUNK, A_CHUNK), 0)
        ci_ = lax.broadcasted_iota(jnp.int32, (A_CHUNK, A_CHUNK), 1)
        ng = ng_ref[...]

        @pl.when(pl.program_id(2) == 0)
        def _():
            state[...] = jnp.zeros_like(state)

        def one_head(hh, ci, first, rows):
            lo = hh * A_HEAD_COLS
            cw = cw_ref[:, hh * A_CONV_COLS:(hh + 1) * A_CONV_COLS]
            cin = _gdn_chunk_prep(_gdn_window(x_ref, halo_ref, ci, first, lo), cw)
            y_ref[rows, hh * A_CONV_COLS:(hh + 1) * A_CONV_COLS] = cin["y"]
            seq_chunk = pl.ds(first_chunk + ci, 1)
            core = yield from _gdn_chunk_core(cin["q"], cin["k"], cin["v"], gc_ref[hh, seq_chunk, :],
                                              beta_ref[hh, seq_chunk, :], None, ri, ci_)
            st = state[hh]
            st_ref[hh, ci] = st
            t_ref[hh, ci] = core["t_mat"]
            st16 = st.astype(BF16)
            vn = core["u"] - _dot(core["w"], st16, 1, 0)
            qs = _dot(core["qg"], st16, 1, 0)
            yield
            vn16 = vn.astype(BF16)
            o = qs + _dot(core["p"], vn16, 1, 0)
            state[hh] = st * core["gam_last"] + _dot(core["kd"], vn16, 0, 0)
            yield
            ocols = slice(hh * A_DV, (hh + 1) * A_DV)
            oraw_ref[rows, ocols] = o
            r = lax.rsqrt(jnp.mean(o * o, axis=1, keepdims=True) + EPS)
            z = x_ref[rows, lo + A_CONV_COLS:lo + A_HEAD_COLS]
            og_ref[rows, ocols] = (o * r * ng * _silu(z)).astype(BF16)

        def chunk(ci, first):
            rows = pl.ds(0 if first else pl.multiple_of(ci * A_CHUNK, A_CHUNK), A_CHUNK)
            _round_robin([one_head(hh, ci, first, rows) for hh in range(hp)])

        chunk(0, True)
        lax.fori_loop(1, A_BLK_CHUNKS, lambda i, c: (chunk(i, False), c)[1], 0)

    small = pl.BlockSpec((None, hp, n, A_CHUNK), lambda b, h, j: (b, h, 0, 0))
    return pl.pallas_call(
        body, name="gdn_fwd", grid=(bn, A_HEADS // hp, nsb),
        in_specs=[pl.BlockSpec((None, A_SEQ_BLK, hp * A_HEAD_COLS), lambda b, h, j: (b, j, h)),
                  pl.BlockSpec((None, None, HALO, hp * A_HEAD_COLS), lambda b, h, j: (b, j, 0, h)),
                  pl.BlockSpec((A_CONV, hp * A_CONV_COLS), lambda b, h, j: (0, h)),
                  small, small,
                  pl.BlockSpec((1, A_DV), lambda b, h, j: (0, 0))],
        out_specs=[pl.BlockSpec((None, A_SEQ_BLK, hp * A_DV), lambda b, h, j: (b, j, h)),
                   pl.BlockSpec((None, A_SEQ_BLK, hp * A_DV), lambda b, h, j: (b, j, h)),
                   pl.BlockSpec((None, hp, A_BLK_CHUNKS, A_DK, A_DV), lambda b, h, j: (b, h, j, 0, 0)),
                   pl.BlockSpec((None, hp, A_BLK_CHUNKS, A_CHUNK, A_CHUNK), lambda b, h, j: (b, h, j, 0, 0)),
                   pl.BlockSpec((None, A_SEQ_BLK, hp * A_CONV_COLS), lambda b, h, j: (b, j, h))],
        out_shape=[jax.ShapeDtypeStruct((bn, s, A_VW), BF16),
                   jax.ShapeDtypeStruct((bn, s, A_VW), F32),
                   jax.ShapeDtypeStruct((bn, A_HEADS, n, A_DK, A_DV), F32),
                   jax.ShapeDtypeStruct((bn, A_HEADS, n, A_CHUNK, A_CHUNK), F32),
                   jax.ShapeDtypeStruct((bn, s, A_HEADS * A_CONV_COLS), F32)],
        scratch_shapes=[pltpu.VMEM((hp, A_DK, A_DV), F32)],
        compiler_params=_params(("parallel", "parallel", "arbitrary"), VMEM_BIG),
    )(proj_hm, halo, cw_hm, beta, gc, norm_g)


def _gdn_bwd(proj_hm, cw_hm, beta, gc, norm_g, oraw, states, t_mats, conv_y, dog, hp=4):
    bn, s, _ = proj_hm.shape
    n = s // A_CHUNK
    nsb = s // A_SEQ_BLK
    halo = _gdn_halo(proj_hm)

    def body(x_ref, halo_ref, cw_ref, beta_ref, gc_ref, ng_ref, oraw_ref, st_ref, t_ref, y_ref, dog_ref,
             dx_ref, dgc_ref, dbeta_ref, dcw_ref, dng_ref, dstate, dy_next, shifted):
        first_chunk = (nsb - 1 - pl.program_id(2)) * A_BLK_CHUNKS
        ri = lax.broadcasted_iota(jnp.int32, (A_CHUNK, A_CHUNK), 0)
        ci_ = lax.broadcasted_iota(jnp.int32, (A_CHUNK, A_CHUNK), 1)
        lane = lax.broadcasted_iota(jnp.int32, (1, A_CHUNK), 1)
        ng = ng_ref[...]

        @pl.when(pl.program_id(2) == 0)
        def _():
            dstate[...] = jnp.zeros_like(dstate)
            dy_next[...] = jnp.zeros_like(dy_next)
            dcw_ref[...] = jnp.zeros_like(dcw_ref)
            dng_ref[...] = jnp.zeros_like(dng_ref)

        def one_head(hh, ci, first, rows):
            lo = hh * A_HEAD_COLS
            ccols = slice(hh * A_CONV_COLS, (hh + 1) * A_CONV_COLS)
            ocols = slice(hh * A_DV, (hh + 1) * A_DV)
            cw = cw_ref[:, ccols]
            cin = _gdn_chunk_prep(_gdn_window(x_ref, halo_ref, ci, first, lo), cw, y_ref[rows, ccols])
            q, k, v = cin["q"], cin["k"], cin["v"]
            seq_chunk = pl.ds(first_chunk + ci, 1)
            cr = yield from _gdn_chunk_core(q, k, v, gc_ref[hh, seq_chunk, :], beta_ref[hh, seq_chunk, :],
                                            t_ref[hh, ci], ri, ci_)
            eye, dec, gam, e = cr["eye"], cr["dec"], cr["gam"], cr["e"]
            b_col, t_mat, u, w, p = cr["b_col"], cr["t_mat"], cr["u"], cr["w"], cr["p"]
            st = st_ref[hh, ci]
            ds_out = dstate[hh]

            o = oraw_ref[rows, ocols]
            z = x_ref[rows, lo + A_CONV_COLS:lo + A_HEAD_COLS]
            d_og = dog_ref[rows, ocols].astype(F32)
            r = lax.rsqrt(jnp.mean(o * o, axis=1, keepdims=True) + EPS)
            oh = o * r
            gate, gate_slope = _silu_and_slope(z)
            d_on = d_og * gate
            dz = d_og * oh * ng * gate_slope
            dng_ref[hh, 0:1, :] += jnp.sum(d_on * oh, axis=0, keepdims=True)
            d_oh = d_on * ng
            d_o = r * (d_oh - oh * jnp.mean(d_oh * oh, axis=1, keepdims=True))

            st16, ds16, do16, w16 = st.astype(BF16), ds_out.astype(BF16), d_o.astype(BF16), w.astype(BF16)
            q16, k16, t16 = cr["q16"], cr["k16"], cr["t16"]
            vn = u - _dot(w16, st16, 1, 0)
            d_vn = _dot(p, do16, 0, 0) + _dot(cr["kd"], ds16, 1, 0)
            d_qg = _dot(do16, st16, 1, 1)
            qgdo = _dot(cr["qg"], do16, 0, 0)
            yield
            vn16, dvn16 = vn.astype(BF16), d_vn.astype(BF16)
            d_p = jnp.where(cr["causal"], _dot(do16, vn16, 1, 1), 0.0)
            d_kd = _dot(vn16, ds16, 1, 1)
            d_gam_last = jnp.sum(jnp.sum(st * ds_out, axis=1, keepdims=True), axis=0, keepdims=True)
            d_w = -_dot(dvn16, st16, 1, 1)
            dstate[hh] = qgdo + ds_out * cr["gam_last"] - _dot(w16, dvn16, 0, 0)
            d_bv = _dot(t16, dvn16, 0, 0)
            yield
            d_kbg = _dot(t16, d_w, 0, 0)
            n_p = (d_p * dec).astype(BF16)
            d_q = _dot(n_p, k16, 1, 0) + d_qg * gam
            npq = _dot(n_p, q16, 0, 0)
            yield
            d_a = jnp.where(cr["strict"], -(_dot(d_bv, u, 1, 1) + _dot(d_kbg, w16, 1, 1)), 0.0)
            yield
            m_a = (d_a * dec).astype(BF16)
            d_kb = _dot(m_a, k16, 1, 0) + d_kbg * gam
            d_k = (_dot(m_a, cr["kb16"], 0, 0) + npq + d_kd * e + d_kb * b_col)
            yield
            d_v = d_bv * b_col
            d_beta_col = (jnp.sum(d_bv * v, axis=1, keepdims=True)
                          + jnp.sum(d_kb * k, axis=1, keepdims=True))
            gterm = d_a * cr["a_mat"] + d_p * p
            d_e = jnp.sum(d_kd * k, axis=1, keepdims=True) * e
            d_g_col = (jnp.sum(gterm, axis=1, keepdims=True)
                       + (jnp.sum(d_qg * q, axis=1, keepdims=True)
                          + jnp.sum(d_kbg * cr["kb"], axis=1, keepdims=True)) * gam
                       - d_e)
            d_g_last = jnp.sum(d_e, axis=0, keepdims=True) + d_gam_last * cr["gam_last"]
            d_g_row = (_col_to_row(d_g_col, eye) - jnp.sum(gterm, axis=0, keepdims=True)
                       + jnp.where(lane == A_CHUNK - 1, d_g_last, 0.0))
            dgc_ref[hh, seq_chunk, :] = d_g_row
            dbeta_ref[hh, seq_chunk, :] = _col_to_row(d_beta_col, eye)

            qh = cin["aq"] * cin["rq"]
            kh = cin["ak"] * cin["rk"]
            d_qh = d_q * (A_DK ** -0.5)
            d_aq = cin["rq"] * (d_qh - qh * jnp.sum(d_qh * qh, axis=1, keepdims=True))
            d_ak = cin["rk"] * (d_k - kh * jnp.sum(d_k * kh, axis=1, keepdims=True))
            d_y = jnp.concatenate([d_aq, d_ak, d_v], axis=1) * cin["slope"]
            shifted[hh, 0, 0:A_CHUNK, :] = d_y
            shifted[hh, 0, A_CHUNK:A_CHUNK + HALO, :] = dy_next[hh]
            shifted[hh, 1, 0:A_CHUNK + HALO, :] = cin["xw"]
            d_x = cw[A_CONV - 1:A_CONV, :] * d_y
            for j in range(1, A_CONV):
                d_x = d_x + cw[A_CONV - 1 - j:A_CONV - j, :] * shifted[hh, 0, j:j + A_CHUNK, :]
            for j in range(A_CONV):
                xs = shifted[hh, 1, HALO - j:HALO - j + A_CHUNK, :]
                dcw_ref[A_CONV - 1 - j:A_CONV - j, ccols] += jnp.sum(d_y * xs, axis=0, keepdims=True)
            dy_next[hh] = d_y[0:HALO, :]
            dx_ref[rows, lo:lo + A_CONV_COLS] = d_x.astype(BF16)
            dx_ref[rows, lo + A_CONV_COLS:lo + A_HEAD_COLS] = dz.astype(BF16)

        def chunk(ci, first):
            rows = pl.ds(0 if first else pl.multiple_of(ci * A_CHUNK, A_CHUNK), A_CHUNK)
            _round_robin([one_head(hh, ci, first, rows) for hh in range(hp)])

        lax.fori_loop(0, A_BLK_CHUNKS - 1, lambda i, c: (chunk(A_BLK_CHUNKS - 1 - i, False), c)[1], 0)
        chunk(0, True)

    rev = lambda j: nsb - 1 - j
    small = pl.BlockSpec((None, hp, n, A_CHUNK), lambda b, h, j: (b, h, 0, 0))
    wide = pl.BlockSpec((None, A_SEQ_BLK, hp * A_HEAD_COLS), lambda b, h, j: (b, rev(j), h))
    val = pl.BlockSpec((None, A_SEQ_BLK, hp * A_DV), lambda b, h, j: (b, rev(j), h))
    return pl.pallas_call(
        body, name="gdn_bwd", grid=(bn, A_HEADS // hp, nsb),
        in_specs=[wide,
                  pl.BlockSpec((None, None, HALO, hp * A_HEAD_COLS), lambda b, h, j: (b, rev(j), 0, h)),
                  pl.BlockSpec((A_CONV, hp * A_CONV_COLS), lambda b, h, j: (0, h)),
                  small, small,
                  pl.BlockSpec((1, A_DV), lambda b, h, j: (0, 0)),
                  val,
                  pl.BlockSpec((None, hp, A_BLK_CHUNKS, A_DK, A_DV), lambda b, h, j: (b, h, rev(j), 0, 0)),
                  pl.BlockSpec((None, hp, A_BLK_CHUNKS, A_CHUNK, A_CHUNK), lambda b, h, j: (b, h, rev(j), 0, 0)),
                  pl.BlockSpec((None, A_SEQ_BLK, hp * A_CONV_COLS), lambda b, h, j: (b, rev(j), h)),
                  val],
        out_specs=[wide, small, small,
                   pl.BlockSpec((None, A_CONV, hp * A_CONV_COLS), lambda b, h, j: (b, 0, h)),
                   pl.BlockSpec((None, hp, 8, A_DV), lambda b, h, j: (b, h, 0, 0))],
        out_shape=[jax.ShapeDtypeStruct((bn, s, A_HEADS * A_HEAD_COLS), BF16),
                   jax.ShapeDtypeStruct((bn, A_HEADS, n, A_CHUNK), F32),
                   jax.ShapeDtypeStruct((bn, A_HEADS, n, A_CHUNK), F32),
                   jax.ShapeDtypeStruct((bn, A_CONV, A_HEADS * A_CONV_COLS), F32),
                   jax.ShapeDtypeStruct((bn, A_HEADS, 8, A_DV), F32)],
        scratch_shapes=[pltpu.VMEM((hp, A_DK, A_DV), F32), pltpu.VMEM((hp, HALO, A_CONV_COLS), F32),
                        pltpu.VMEM((hp, 2, A_CHUNK + 2 * HALO, A_CONV_COLS), F32)],
        compiler_params=_params(("parallel", "parallel", "arbitrary"), VMEM_BIG),
    )(proj_hm, halo, cw_hm, beta, gc, norm_g, oraw, states, t_mats, conv_y, dog)


def _rope_tables(posf, inv_freq_row):
    t = posf.shape[0]
    tm = 512

    def body(p_ref, f_ref, c_ref, sa_ref, sb_ref):
        ang = p_ref[...] * f_ref[...]
        lane = lax.broadcasted_iota(jnp.int32, ang.shape, 1)
        half = ROPE_DIMS // 2
        c_ref[...] = jnp.where(lane < ROPE_DIMS, jnp.cos(ang), 1.0)
        sn = jnp.sin(ang)
        sa_ref[...] = jnp.where(lane < half, -sn, 0.0)
        sb_ref[...] = jnp.where((lane >= half) & (lane < ROPE_DIMS), sn, 0.0)

    row = pl.BlockSpec((tm, 128), lambda i: (i, 0))
    return pl.pallas_call(
        body, name="rope_tables", grid=(t // tm,),
        in_specs=[row, pl.BlockSpec((1, 128), lambda i: (0, 0))], out_specs=[row] * 3,
        out_shape=[jax.ShapeDtypeStruct((t, 128), F32)] * 3,
        compiler_params=_params(("parallel",)),
    )(posf, inv_freq_row)


def _qk_prep(proj, c, sa, sb, qg, kg, name, tm=512):
    t = proj.shape[0]

    def body(x_ref, c_ref, sa_ref, sb_ref, qg_ref, kg_ref, o_ref):
        cc, s1, s2 = c_ref[...], sa_ref[...], sb_ref[...]
        half = ROPE_DIMS // 2

        def one_head(lo, g):
            xv = x_ref[:, lo:lo + B_DH].astype(F32)
            ms = jnp.mean(xv * xv, axis=1, keepdims=True)
            yield
            xn = xv * lax.rsqrt(ms + EPS) * g
            r1, r2 = pltpu.roll(xn, 128 - half, 1), pltpu.roll(xn, half, 1)
            yield
            o_ref[:, lo:lo + B_DH] = (xn * cc + r1 * s1 + r2 * s2).astype(BF16)

        for which, g_ref in ((0, qg_ref), (1, kg_ref)):
            g = g_ref[...]
            _round_robin([one_head(which * B_W + h * B_DH, g) for h in range(B_HEADS)])
        o_ref[:, 2 * B_W:3 * B_W] = x_ref[:, 2 * B_W:3 * B_W]

    tab = pl.BlockSpec((tm, 128), lambda i: (i, 0))
    gain = pl.BlockSpec((1, B_DH), lambda i: (0, 0))
    return pl.pallas_call(
        body, name=name, grid=(t // tm,),
        in_specs=[pl.BlockSpec((tm, 3 * B_W), lambda i: (i, 0)), tab, tab, tab, gain, gain],
        out_specs=pl.BlockSpec((tm, 3 * B_W), lambda i: (i, 0)),
        out_shape=jax.ShapeDtypeStruct((t, 3 * B_W), BF16),
        compiler_params=_params(("parallel",), 40 * 1024 * 1024),
    )(proj, c, sa, sb, qg, kg)


def _qk_prep_bwd(proj, c, sa, sb, qg, kg, dq, dk, dv, dz, name, tm=512):
    t = proj.shape[0]
    out_w = 3 * B_W + (B_W if dz is not None else 0)

    def body(*refs):
        x_ref, c_ref, sa_ref, sb_ref, qg_ref, kg_ref, dq_ref, dk_ref, dv_ref = refs[:9]
        if dz is not None:
            dz_ref, o_ref, dgain_ref = refs[9:]
        else:
            o_ref, dgain_ref = refs[9:]
        i = pl.program_id(0)

        @pl.when(i == 0)
        def _():
            dgain_ref[...] = jnp.zeros_like(dgain_ref)

        cc, s1, s2 = c_ref[...], sa_ref[...], sb_ref[...]
        half = ROPE_DIMS // 2

        def one_head(which, h, g, d_ref, parts):
            lo = which * B_W + h * B_DH
            xv = x_ref[:, lo:lo + B_DH].astype(F32)
            d_out = d_ref[:, h * B_DH:(h + 1) * B_DH].astype(F32)
            ms = jnp.mean(xv * xv, axis=1, keepdims=True)
            r1, r2 = pltpu.roll(d_out * s1, half, 1), pltpu.roll(d_out * s2, 128 - half, 1)
            yield
            r = lax.rsqrt(ms + EPS)
            xh = xv * r
            d_xn = d_out * cc + r1 + r2
            parts.append(jnp.sum(d_xn * xh, axis=0, keepdims=True))
            d_xh = d_xn * g
            dot = jnp.mean(d_xh * xh, axis=1, keepdims=True)
            yield
            o_ref[:, lo:lo + B_DH] = (r * (d_xh - xh * dot)).astype(BF16)

        for which, g_ref, d_ref in ((0, qg_ref, dq_ref), (1, kg_ref, dk_ref)):
            parts = []
            _round_robin([one_head(which, h, g_ref[...], d_ref, parts) for h in range(B_HEADS)])
            acc = parts[0]
            for part in parts[1:]:
                acc = acc + part
            dgain_ref[which:which + 1, :] += acc
        o_ref[:, 2 * B_W:3 * B_W] = dv_ref[...]
        if dz is not None:
            o_ref[:, 3 * B_W:4 * B_W] = dz_ref[...]

    tab = pl.BlockSpec((tm, 128), lambda i: (i, 0))
    gain = pl.BlockSpec((1, B_DH), lambda i: (0, 0))
    grad = pl.BlockSpec((tm, B_W), lambda i: (i, 0))
    in_specs = [pl.BlockSpec((tm, 2 * B_W), lambda i: (i, 0)), tab, tab, tab, gain, gain, grad, grad, grad]
    args = [proj, c, sa, sb, qg, kg, dq, dk, dv]
    if dz is not None:
        in_specs.append(grad)
        args.append(dz)
    return pl.pallas_call(
        body, name=name, grid=(t // tm,), in_specs=in_specs,
        out_specs=[pl.BlockSpec((tm, out_w), lambda i: (i, 0)), pl.BlockSpec((8, B_DH), lambda i: (0, 0))],
        out_shape=[jax.ShapeDtypeStruct((t, out_w), BF16), jax.ShapeDtypeStruct((8, B_DH), F32)],
        compiler_params=_params(("arbitrary",), 40 * 1024 * 1024),
    )(*args)


def _attn_masks():
    qi = lax.broadcasted_iota(jnp.int32, (B_BLK, 2 * B_BLK), 0)
    kj = lax.broadcasted_iota(jnp.int32, (B_BLK, 2 * B_BLK), 1)
    two = (kj >= qi) & (kj <= qi + B_BLK)
    q1 = lax.broadcasted_iota(jnp.int32, (B_BLK, B_BLK), 0)
    k1 = lax.broadcasted_iota(jnp.int32, (B_BLK, B_BLK), 1)
    return k1 <= q1, two


def _lane_pick(ref_rows, h):
    lane = lax.broadcasted_iota(jnp.int32, ref_rows.shape, 1)
    return jnp.sum(jnp.where(lane == h, ref_rows, 0.0), axis=1, keepdims=True)


B_ROWS = 2048


def _attn_schedule(nb, sb, block):
    way = 16

    def run(items):
        for at in range(0, len(items), way):
            _round_robin([block(*it) for it in items[at:at + way]])

    run([(si, 0, True) for si in range(sb)])
    if nb == 1:
        return
    per = max(1, way // sb)
    lead = 1 + (nb - 1) % per
    if lead > 1:
        run([(si, i, False) for i in range(1, lead) for si in range(sb)])

    def step(it, carry):
        run([(si, lead + it * per + u, False) for u in range(per) for si in range(sb)])
        return carry

    lax.fori_loop(0, (nb - lead) // per, step, 0)


def _attn_rows(i, first):
    if first:
        return pl.ds(0, B_BLK), pl.ds(0, B_BLK)
    rows = pl.ds(pl.multiple_of(i * B_BLK, B_BLK), B_BLK)
    return rows, pl.ds(pl.multiple_of((i - 1) * B_BLK, B_BLK), 2 * B_BLK)


def _attn_fwd(qkv, name):
    ns, ln, _ = qkv.shape
    nb = ln // B_BLK
    sb = B_ROWS // ln
    scale = B_DH ** -0.5

    def body(q_ref, k_ref, v_ref, o_ref, lse_ref):
        h = pl.program_id(1)
        mask1, mask2 = _attn_masks()
        lane = lax.broadcasted_iota(jnp.int32, (B_BLK, B_HEADS), 1)

        @pl.when(h == 0)
        def _():
            lse_ref[...] = jnp.zeros_like(lse_ref)

        def block(si, i, first):
            rows, win = _attn_rows(i, first)
            mask = mask1 if first else mask2
            sc = jnp.where(mask, _dot(q_ref[si, rows, :], k_ref[si, win, :], 1, 1) * scale, -1e30)
            yield
            m = jnp.max(sc, axis=1, keepdims=True)
            p = jnp.exp(sc - m)
            l = jnp.sum(p, axis=1, keepdims=True)
            pv = _dot(p, v_ref[si, win, :], 1, 0)
            yield
            o_ref[si, rows, :] = (pv / l).astype(BF16)
            lse_ref[si, rows, :] = jnp.where(lane == h, m + jnp.log(l), lse_ref[si, rows, :])

        _attn_schedule(nb, sb, block)

    head = lambda off: pl.BlockSpec((sb, ln, B_DH), lambda s, h: (s, 0, off + h))
    return pl.pallas_call(
        body, name=name, grid=(ns // sb, B_HEADS),
        in_specs=[head(0), head(B_HEADS), head(2 * B_HEADS)],
        out_specs=[head(0), pl.BlockSpec((sb, ln, B_HEADS), lambda s, h: (s, 0, 0))],
        out_shape=[jax.ShapeDtypeStruct((ns, ln, B_W), BF16), jax.ShapeDtypeStruct((ns, ln, B_HEADS), F32)],
        compiler_params=_params(("parallel", "arbitrary")),
    )(qkv, qkv, qkv)


def _attn_bwd(qkv, d_o, lse_joint, delta, name):
    ns, ln, _ = qkv.shape
    nb = ln // B_BLK
    sb = B_ROWS // ln
    scale = B_DH ** -0.5

    def body(q_ref, k_ref, v_ref, do_ref, lj_ref, dl_ref, dq_ref, dk_out, dv_out, dk_ref, dv_ref):
        h = pl.program_id(1)
        mask1, mask2 = _attn_masks()
        dk_ref[...] = jnp.zeros_like(dk_ref)
        dv_ref[...] = jnp.zeros_like(dv_ref)

        def block(si, i, first):
            rows, win = _attn_rows(i, first)
            mask = mask1 if first else mask2
            q = q_ref[si, rows, :]
            d_out = do_ref[si, rows, :]
            l_col = _lane_pick(lj_ref[si, rows, :], h)
            d_col = _lane_pick(dl_ref[si, rows, :], h)
            sc = _dot(q, k_ref[si, win, :], 1, 1) * scale
            d_p = _dot(d_out, v_ref[si, win, :], 1, 1)
            yield
            p = jnp.exp(jnp.where(mask, sc - l_col, -1e30))
            d_s = p * (d_p - d_col) * scale
            d_q = _dot(d_s, k_ref[si, win, :], 1, 0)
            d_k = _dot(d_s, q, 0, 0)
            d_v = _dot(p, d_out, 0, 0)
            yield
            dq_ref[si, rows, :] = d_q.astype(BF16)
            dk_ref[si, win, :] += d_k
            dv_ref[si, win, :] += d_v

        _attn_schedule(nb, sb, block)
        dk_out[...] = dk_ref[...].astype(BF16)
        dv_out[...] = dv_ref[...].astype(BF16)

    head = lambda off: pl.BlockSpec((sb, ln, B_DH), lambda s, h: (s, 0, off + h))
    small = pl.BlockSpec((sb, ln, B_HEADS), lambda s, h: (s, 0, 0))
    return pl.pallas_call(
        body, name=name, grid=(ns // sb, B_HEADS),
        in_specs=[head(0), head(B_HEADS), head(2 * B_HEADS), head(0), small, small],
        out_specs=[head(0)] * 3,
        out_shape=[jax.ShapeDtypeStruct((ns, ln, B_W), BF16)] * 3,
        scratch_shapes=[pltpu.VMEM((sb, ln, B_DH), F32)] * 2,
        compiler_params=_params(("parallel", "parallel")),
    )(qkv, qkv, qkv, d_o, lse_joint, delta)


def _merge_weights(lse_refs):
    ls = [r[...] for r in lse_refs]
    m = jnp.maximum(jnp.maximum(ls[0], ls[1]), ls[2])
    es = [jnp.exp(l - m) for l in ls]
    tot = es[0] + es[1] + es[2]
    return [e / tot for e in es], m + jnp.log(tot)


def _merge_fwd(outs, lses, proj0, tm=512):
    t = outs[0].shape[0]

    def body(o0, o1, o2, l0, l1, l2, z_ref, og_ref):
        wts, _ = _merge_weights((l0, l1, l2))

        def one_head(h):
            cols = slice(h * B_DH, (h + 1) * B_DH)
            w0, w1, w2 = (jnp.broadcast_to(w[:, h:h + 1], (tm, B_DH)) for w in wts)
            yield
            o = w0 * o0[:, cols] + w1 * o1[:, cols] + w2 * o2[:, cols]
            og_ref[:, cols] = (o * _silu(z_ref[:, cols].astype(F32))).astype(BF16)

        _round_robin([one_head(h) for h in range(B_HEADS)])

    wide = pl.BlockSpec((tm, B_W), lambda i: (i, 0))
    small = pl.BlockSpec((tm, B_HEADS), lambda i: (i, 0))
    return pl.pallas_call(
        body, name="merge_fwd", grid=(t // tm,),
        in_specs=[wide] * 3 + [small] * 3 + [pl.BlockSpec((tm, B_W), lambda i: (i, 3))],
        out_specs=wide, out_shape=jax.ShapeDtypeStruct((t, B_W), BF16),
        compiler_params=_params(("parallel",)),
    )(*outs, *lses, proj0)


def _merge_bwd(outs, lses, proj0, d_og, tm=512):
    t = outs[0].shape[0]

    def body(o0, o1, o2, l0, l1, l2, z_ref, dog_ref, do_ref, lj_ref, dl_ref, dz_ref):
        wts, lj = _merge_weights((l0, l1, l2))
        lj_ref[...] = lj
        lane = lax.broadcasted_iota(jnp.int32, (tm, B_HEADS), 1)
        sums = [None] * B_HEADS

        def one_head(h):
            cols = slice(h * B_DH, (h + 1) * B_DH)
            w0, w1, w2 = (jnp.broadcast_to(w[:, h:h + 1], (tm, B_DH)) for w in wts)
            yield
            o = w0 * o0[:, cols] + w1 * o1[:, cols] + w2 * o2[:, cols]
            z = z_ref[:, cols].astype(F32)
            d_g = dog_ref[:, cols].astype(F32)
            gate, gate_slope = _silu_and_slope(z)
            d_out = d_g * gate
            dz_ref[:, cols] = (d_g * o * gate_slope).astype(BF16)
            do_ref[:, cols] = d_out.astype(BF16)
            sums[h] = jnp.sum(d_out * o, axis=1, keepdims=True)
            yield

        _round_robin([one_head(h) for h in range(B_HEADS)])
        delta = jnp.zeros((tm, B_HEADS), F32)
        for h in range(B_HEADS):
            delta = jnp.where(lane == h, sums[h], delta)
        dl_ref[...] = delta

    wide = pl.BlockSpec((tm, B_W), lambda i: (i, 0))
    small = pl.BlockSpec((tm, B_HEADS), lambda i: (i, 0))
    return pl.pallas_call(
        body, name="merge_bwd", grid=(t // tm,),
        in_specs=[wide] * 3 + [small] * 3 + [pl.BlockSpec((tm, B_W), lambda i: (i, 3)), wide],
        out_specs=[wide, small, small, wide],
        out_shape=[jax.ShapeDtypeStruct((t, B_W), BF16), jax.ShapeDtypeStruct((t, B_HEADS), F32),
                   jax.ShapeDtypeStruct((t, B_HEADS), F32), jax.ShapeDtypeStruct((t, B_W), BF16)],
        compiler_params=_params(("parallel",)),
    )(*outs, *lses, proj0, d_og)


def _adamw(w, g, m, v, name):
    r, c = w.shape
    tr = r
    for cand in (256, 128, 64, 32, 16, 8):
        if r % cand == 0:
            tr = cand
            break

    def body(w_ref, g_ref, m_ref, v_ref, d_ref, nm_ref, nv_ref):
        gv = g_ref[...]
        nm = ADAM_B1 * m_ref[...] + (1.0 - ADAM_B1) * gv
        nv = ADAM_B2 * v_ref[...] + (1.0 - ADAM_B2) * (gv * gv)
        m_hat = nm / (1.0 - ADAM_B1 ** ADAM_STEP)
        v_hat = nv / (1.0 - ADAM_B2 ** ADAM_STEP)
        d_ref[...] = -ADAM_LR * (m_hat / (jnp.sqrt(v_hat) + ADAM_EPS) + ADAM_WD * w_ref[...])
        nm_ref[...] = nm
        nv_ref[...] = nv

    blk = pl.BlockSpec((tr, c), lambda i: (i, 0))
    return pl.pallas_call(
        body, name=name, grid=(r // tr,), in_specs=[blk] * 4, out_specs=[blk] * 3,
        out_shape=[jax.ShapeDtypeStruct((r, c), F32)] * 3,
        compiler_params=_params(("parallel",)),
    )(w, g, m, v)


def _adam_update(w, gv, m, v):
    nm = ADAM_B1 * m + (1.0 - ADAM_B1) * gv
    nv = ADAM_B2 * v + (1.0 - ADAM_B2) * (gv * gv)
    m_hat = nm / (1.0 - ADAM_B1 ** ADAM_STEP)
    v_hat = nv / (1.0 - ADAM_B2 ** ADAM_STEP)
    return -ADAM_LR * (m_hat / (jnp.sqrt(v_hat) + ADAM_EPS) + ADAM_WD * w), nm, nv


def _adamw_shard(w, mine, theirs, m, v, half_index, name, tr=128):
    _, r, c = w.shape
    nhb = (r // 2) // tr

    def body(c_ref, w_ref, mine_ref, theirs_ref, m_ref, v_ref, g_ref, d_ref, nm_ref, nv_ref):
        is_mine = (pl.program_id(0) // nhb) == c_ref[0]
        gv = jnp.where(is_mine, mine_ref[...], theirs_ref[...])
        d, nm, nv = _adam_update(w_ref[...], gv, m_ref[...], v_ref[...])
        g_ref[...] = gv
        d_ref[...] = d
        nm_ref[...] = nm
        nv_ref[...] = nv

    full = pl.BlockSpec((None, tr, c), lambda i, cc: (0, i, 0))
    half = pl.BlockSpec((tr, c), lambda i, cc: (i % nhb, 0))
    return pl.pallas_call(
        body, name=name,
        grid_spec=pltpu.PrefetchScalarGridSpec(
            num_scalar_prefetch=1, grid=(2 * nhb,),
            in_specs=[full, half, half, full, full], out_specs=[full] * 4),
        out_shape=[jax.ShapeDtypeStruct(w.shape, F32)] * 4,
        compiler_params=_params(("parallel",), 40 * 1024 * 1024),
    )(half_index, w, mine, theirs, m, v)


def _adamw_shard_cols(w, mine, theirs, m, v, half_index, name, steps=20):
    c, _, r = w.shape
    tc = c // steps
    assert tc * steps == c

    def body(c_ref, w_ref, mine_ref, theirs_ref, m_ref, v_ref, g_ref, d_ref, nm_ref, nv_ref):
        first = jnp.where(c_ref[0] == 0, mine_ref[...], theirs_ref[...])
        second = jnp.where(c_ref[0] == 0, theirs_ref[...], mine_ref[...])
        for lo, gv in ((0, first), (r // 2, second)):
            cols = slice(lo, lo + r // 2)
            d, nm, nv = _adam_update(w_ref[:, :, cols], gv, m_ref[:, :, cols], v_ref[:, :, cols])
            g_ref[:, :, cols] = gv
            d_ref[:, :, cols] = d
            nm_ref[:, :, cols] = nm
            nv_ref[:, :, cols] = nv

    full = pl.BlockSpec((tc, 1, r), lambda i, cc: (i, 0, 0))
    half = pl.BlockSpec((tc, 1, r // 2), lambda i, cc: (i, 0, 0))
    return pl.pallas_call(
        body, name=name,
        grid_spec=pltpu.PrefetchScalarGridSpec(
            num_scalar_prefetch=1, grid=(steps,),
            in_specs=[full, half, half, full, full], out_specs=[full] * 4),
        out_shape=[jax.ShapeDtypeStruct(w.shape, F32)] * 4,
        compiler_params=_params(("parallel",), 40 * 1024 * 1024),
    )(half_index, w, mine, theirs, m, v)


def _pair_sum(own, other, half_index, name, tr=256):
    _, r, c = own.shape
    rh = r // 2
    tr = min(tr, rh)
    nrb = rh // tr

    def body(c_ref, own_ref, oth_ref, out_ref):
        out_ref[...] = (own_ref[...] + oth_ref[...].astype(F32)).astype(BF16)

    return pl.pallas_call(
        body, name=name,
        grid_spec=pltpu.PrefetchScalarGridSpec(
            num_scalar_prefetch=1, grid=(N_CHIPS, nrb),
            in_specs=[pl.BlockSpec((None, tr, c), lambda k, i, cc: (k, cc[0] * nrb + i, 0)),
                      pl.BlockSpec((None, tr, c), lambda k, i, cc: (k, i, 0))],
            out_specs=pl.BlockSpec((None, tr, c), lambda k, i, cc: (k, i, 0))),
        out_shape=jax.ShapeDtypeStruct((N_CHIPS, rh, c), BF16),
        compiler_params=_params(("parallel", "parallel")),
    )(half_index, own, other)


def _chip_sum(sums, others, chip_index, name, tr=256):
    _, r, c = sums.shape
    tr = min(tr, r)

    def body(k_ref, own_ref, oth_ref, out_ref):
        acc = own_ref[...].astype(F32)
        for j in range(N_CHIPS - 1):
            acc = acc + oth_ref[j].astype(F32)
        out_ref[...] = acc

    return pl.pallas_call(
        body, name=name,
        grid_spec=pltpu.PrefetchScalarGridSpec(
            num_scalar_prefetch=1, grid=(r // tr,),
            in_specs=[pl.BlockSpec((None, tr, c), lambda i, kk: (kk[0], i, 0)),
                      pl.BlockSpec((N_CHIPS - 1, tr, c), lambda i, kk: (0, i, 0))],
            out_specs=pl.BlockSpec((tr, c), lambda i, kk: (i, 0))),
        out_shape=jax.ShapeDtypeStruct((r, c), F32),
        compiler_params=_params(("parallel",)),
    )(chip_index, sums, others)


HBM = pl.BlockSpec(memory_space=pltpu.HBM)


def _place():
    x, y, c = lax.axis_index("x"), lax.axis_index("y"), lax.axis_index("c")
    chips = [(1 - x, y), (x, 1 - y), (1 - x, 1 - y)]
    return x, y, c, chips


def _sibling_forward(land):
    def body(in_ref, out_ref, send, recv):
        x, y, c, chips = _place()
        rh = out_ref.shape[1] // 2
        cps = []
        for j, (px, py) in enumerate(chips):
            slot = out_ref.at[2 * px + py, pl.ds(c * rh, rh)]
            cp = pltpu.make_async_remote_copy(
                src_ref=slot, dst_ref=slot, send_sem=send.at[j], recv_sem=recv.at[j],
                device_id=(x, y, 1 - c), device_id_type=MESH)
            cp.start()
            cps.append(cp)
        for j, (px, py) in enumerate(chips):
            slot = out_ref.at[2 * px + py, pl.ds((1 - c) * rh, rh)]
            pltpu.make_async_remote_copy(
                src_ref=slot, dst_ref=slot, send_sem=send.at[j], recv_sem=recv.at[j],
                device_id=(x, y, 1 - c), device_id_type=MESH).wait_recv()
        for cp in cps:
            cp.wait_send()

    return pl.pallas_call(
        body, name="first_weights_sibling_forward", in_specs=[HBM], out_specs=HBM,
        out_shape=jax.ShapeDtypeStruct(land.shape, land.dtype), input_output_aliases={0: 0},
        scratch_shapes=[pltpu.SemaphoreType.DMA((3,)), pltpu.SemaphoreType.DMA((3,))],
    )(land)


def _sibling_swap(halves, name):
    na = len(halves)

    def body(*refs):
        ins, outs = refs[:na], refs[na:2 * na]
        send, recv = refs[2 * na:]
        x, y, c, _ = _place()
        cps = []
        for i in range(na):
            cp = pltpu.make_async_remote_copy(
                src_ref=ins[i], dst_ref=outs[i], send_sem=send.at[i], recv_sem=recv.at[i],
                device_id=(x, y, 1 - c), device_id_type=MESH)
            cp.start()
            cps.append(cp)
        for cp in cps:
            cp.wait()

    out_shape = [jax.ShapeDtypeStruct(h.shape, h.dtype) for h in halves]
    return pl.pallas_call(
        body, name=name, in_specs=[HBM] * na, out_specs=[HBM] * na, out_shape=out_shape,
        scratch_shapes=[pltpu.SemaphoreType.DMA((na,)), pltpu.SemaphoreType.DMA((na,))],
    )(*halves)


SEM = pl.BlockSpec(memory_space=pltpu.SEMAPHORE)
ANY = pl.BlockSpec(memory_space=pl.ANY)
EFFECT = pltpu.SideEffectType.DATAFLOW_SIDE_EFFECTING


def _split_copy_start(name, plan, srcs, lands, after):
    ns, nl = len(srcs), len(lands)

    def body(*refs):
        src_refs, land_refs = refs[:ns], refs[ns:ns + nl]
        send, recv = refs[ns + nl + 1], refs[ns + nl + 2]
        token = refs[-1]
        outgoing, _ = plan(src_refs, land_refs)
        for src, dst, dev, si, ri in outgoing:
            pltpu.make_async_remote_copy(src_ref=src, dst_ref=dst, send_sem=send.at[si], recv_sem=recv.at[ri],
                                         device_id=dev, device_id_type=MESH).start()
        token[...] = jnp.zeros_like(token)

    n_out, n_in = plan.counts
    thru = [pltpu.HBM(a.shape, a.dtype) for a in list(srcs) + list(lands)]
    res = pl.pallas_call(
        body, name=name,
        out_shape=[pltpu.SemaphoreType.DMA((n_out,)), pltpu.SemaphoreType.DMA((n_in,))] + thru
        + [jax.ShapeDtypeStruct((8, 128), F32)],
        in_specs=[HBM] * (ns + nl) + [ANY],
        out_specs=[SEM, SEM] + [HBM] * (ns + nl) + [pl.BlockSpec(memory_space=pltpu.VMEM)],
        input_output_aliases={i: 2 + i for i in range(ns + nl)},
        compiler_params=pltpu.CompilerParams(has_side_effects=EFFECT),
    )(*[pltpu.with_memory_space_constraint(a, pltpu.HBM) for a in list(srcs) + list(lands)], after)
    return res[0], res[1], res[2:2 + ns], res[2 + ns:2 + ns + nl], res[-1]


def _split_copy_wait(name, plan, send, recv, srcs, lands, after):
    ns, nl = len(srcs), len(lands)
    after = list(after) if isinstance(after, (list, tuple)) else [after]

    def body(*refs):
        src_refs, land_refs = refs[:ns], refs[ns:ns + nl]
        send_ref, recv_ref = refs[ns + nl], refs[ns + nl + 1]
        outgoing, arrivals = plan(src_refs, land_refs)
        for src, dst, dev, si, ri in outgoing:
            pltpu.make_async_remote_copy(src_ref=src, dst_ref=dst, send_sem=send_ref.at[si], recv_sem=recv_ref.at[ri],
                                         device_id=dev, device_id_type=MESH).wait_send()
        for view, ri in arrivals:
            pltpu.make_async_remote_copy(src_ref=view, dst_ref=view, send_sem=send_ref.at[0], recv_sem=recv_ref.at[ri],
                                         device_id=_place()[:3], device_id_type=MESH).wait_recv()

    thru = [pltpu.HBM(a.shape, a.dtype) for a in list(srcs) + list(lands)]
    res = pl.pallas_call(
        body, name=name, out_shape=thru,
        in_specs=[HBM] * (ns + nl) + [SEM, SEM] + [ANY] * len(after), out_specs=[HBM] * (ns + nl),
        input_output_aliases={i: i for i in range(ns + nl)},
        compiler_params=pltpu.CompilerParams(has_side_effects=EFFECT),
    )(*srcs, *lands, send, recv, *after)
    return res[:ns], res[ns:]


def _gather_plan(n_arrays):
    def plan(src_refs, land_refs):
        x, y, c, chips = _place()
        me = 2 * x + y
        outgoing, arrivals = [], []
        for i in range(n_arrays):
            rh = src_refs[i].shape[0] // 2
            mine = pl.ds(c * rh, rh)
            for j, (px, py) in enumerate(chips):
                for delta in range(2):
                    tc = c ^ delta
                    outgoing.append((src_refs[i].at[mine], land_refs[i].at[me, mine], (px, py, tc),
                                     6 * i + 2 * j + delta, 6 * i + 2 * j + delta))
                    theirs = pl.ds(tc * rh, rh)
                    arrivals.append((land_refs[i].at[2 * px + py, theirs], 6 * i + 2 * j + delta))
        return outgoing, arrivals

    plan.counts = (6 * n_arrays, 6 * n_arrays)
    return plan


def _first_gather_plan():
    def plan(src_refs, land_refs):
        x, y, c, chips = _place()
        me = 2 * x + y
        rh = src_refs[0].shape[0] // 2
        mine = pl.ds(c * rh, rh)
        outgoing, arrivals = [], []
        for j, (px, py) in enumerate(chips):
            outgoing.append((src_refs[0].at[mine], land_refs[0].at[me, mine], (px, py, c), j, j))
            arrivals.append((land_refs[0].at[2 * px + py, mine], j))
            outgoing.append((src_refs[1], land_refs[1].at[me], (px, py, c), 3 + j, 3 + j))
            arrivals.append((land_refs[1].at[2 * px + py], 3 + j))
        return outgoing, arrivals

    plan.counts = (6, 6)
    return plan


def _exchange_plan(n_arrays):
    def plan(src_refs, land_refs):
        x, y, c, chips = _place()
        outgoing, arrivals = [], []
        for i in range(n_arrays):
            for j, (px, py) in enumerate(chips):
                outgoing.append((src_refs[i].at[2 * px + py], land_refs[i].at[j], (px, py, c), 3 * i + j, 3 * i + j))
                arrivals.append((land_refs[i].at[j], 3 * i + j))
        return outgoing, arrivals

    plan.counts = (3 * n_arrays, 3 * n_arrays)
    return plan


def _small_allreduce(vec):
    r, cdim = vec.shape
    n_dev = 8

    def body(v_ref, out_ref, buf, send, recv):
        x, y, c, _ = _place()
        me = 4 * x + 2 * y + c
        buf[me] = v_ref[...]
        cps = []
        for k in range(1, n_dev):
            dx, dy, dc = (k >> 2) & 1, (k >> 1) & 1, k & 1
            peer = (x ^ dx, y ^ dy, c ^ dc)
            cp = pltpu.make_async_remote_copy(
                src_ref=v_ref, dst_ref=buf.at[me], send_sem=send.at[k - 1], recv_sem=recv.at[k - 1],
                device_id=peer, device_id_type=MESH)
            cp.start()
            cps.append(cp)
        for k in range(1, n_dev):
            dx, dy, dc = (k >> 2) & 1, (k >> 1) & 1, k & 1
            src = 4 * (x ^ dx) + 2 * (y ^ dy) + (c ^ dc)
            slot = buf.at[src]
            pltpu.make_async_remote_copy(
                src_ref=slot, dst_ref=slot, send_sem=send.at[k - 1], recv_sem=recv.at[k - 1],
                device_id=(x ^ dx, y ^ dy, c ^ dc), device_id_type=MESH).wait_recv()
        for cp in cps:
            cp.wait_send()
        acc = buf[0]
        for k in range(1, n_dev):
            acc = acc + buf[k]
        out_ref[...] = acc

    vm = pl.BlockSpec(memory_space=pltpu.VMEM)
    return pl.pallas_call(
        body, name="small_allreduce", in_specs=[vm], out_specs=vm,
        out_shape=jax.ShapeDtypeStruct((r, cdim), F32),
        scratch_shapes=[pltpu.VMEM((n_dev, r, cdim), F32), pltpu.SemaphoreType.DMA((n_dev - 1,)),
                        pltpu.SemaphoreType.DMA((n_dev - 1,))],
    )(vec)


def _a_cols_to_head_major(w):
    lead = w.shape[:-1]
    q = w[..., :A_QK].reshape(lead + (A_HEADS, A_DK))
    k = w[..., A_QK:2 * A_QK].reshape(lead + (A_HEADS, A_DK))
    v = w[..., 2 * A_QK:2 * A_QK + A_VW].reshape(lead + (A_HEADS, A_DV))
    z = w[..., 2 * A_QK + A_VW:].reshape(lead + (A_HEADS, A_DV))
    return jnp.concatenate([q, k, v, z], axis=-1).reshape(lead + (A_HEADS * A_HEAD_COLS,))


def _a_cols_from_head_major(w):
    lead = w.shape[:-1]
    w = w.reshape(lead + (A_HEADS, A_HEAD_COLS))
    parts = [w[..., :A_DK], w[..., A_DK:2 * A_DK], w[..., 2 * A_DK:2 * A_DK + A_DV], w[..., 2 * A_DK + A_DV:]]
    return jnp.concatenate([p.reshape(lead + (-1,)) for p in parts], axis=-1)


def _conv_cols_to_head_major(w):
    lead = w.shape[:-1]
    q = w[..., :A_QK].reshape(lead + (A_HEADS, A_DK))
    k = w[..., A_QK:2 * A_QK].reshape(lead + (A_HEADS, A_DK))
    v = w[..., 2 * A_QK:].reshape(lead + (A_HEADS, A_DV))
    return jnp.concatenate([q, k, v], axis=-1).reshape(lead + (A_HEADS * A_CONV_COLS,))


def _conv_cols_from_head_major(w):
    lead = w.shape[:-1]
    w = w.reshape(lead + (A_HEADS, A_CONV_COLS))
    parts = [w[..., :A_DK], w[..., A_DK:2 * A_DK], w[..., 2 * A_DK:]]
    return jnp.concatenate([p.reshape(lead + (-1,)) for p in parts], axis=-1)


def _to_stream(a, bn, d):
    rest = a.shape[1:]
    s = a.shape[0] // bn
    a = a.reshape((bn, s // d, d) + rest)
    a = jnp.swapaxes(a, 1, 2)
    return a.reshape((bn * d, s // d) + rest)


def _from_stream(a, bn, d):
    rest = a.shape[2:]
    ln = a.shape[1]
    a = a.reshape((bn, d, ln) + rest)
    a = jnp.swapaxes(a, 1, 2)
    return a.reshape((bn * ln * d,) + rest)


B_SUB = 512
B_SHARD_BLOCKS = (3 * B_GROUPS * B_W + B_W) // N_CHIPS // B_SUB


def _b_block(gi, jj):
    nb = (B_GROUPS * (jj // 2) + gi) * 2 + jj % 2
    return nb // B_SHARD_BLOCKS, nb % B_SHARD_BLOCKS


def _shard_major(g, ncols):
    r = g.shape[0]
    return jnp.swapaxes(g.reshape(r, N_CHIPS, ncols), 0, 1)


def _pack_rows(items):
    rows, offs = [], []
    at = 0
    for a in items:
        flat = a.reshape(-1).astype(F32)
        nr = -(-flat.shape[0] // 1024) * 8
        flat = jnp.pad(flat, (0, nr * 128 - flat.shape[0]))
        rows.append(flat.reshape(nr, 128))
        offs.append((at, nr, a.shape))
        at += nr
    return jnp.concatenate(rows, axis=0), offs


def _unpack_rows(packed, offs):
    out = []
    for at, nr, shape in offs:
        size = int(np.prod(shape)) if len(shape) else 1
        out.append(packed[at:at + nr].reshape(-1)[:size].reshape(shape))
    return out


def _local_step(x, positions, loss_target, norm_g, a_log, a_dt_bias, a_norm_g, b_q_norm_g, b_k_norm_g,
                start_token, first_weights, late_weights, b_grads_ready, a_grads_ready):
    bn, s, d = x.shape
    t = bn * s
    n_chunks = s // A_CHUNK
    x0 = x.reshape(t, d)
    h0 = _rms_fwd(x0, norm_g[0:1] + start_token, "rms0_fwd")
    inv_freq = ROPE_THETA ** (-jnp.arange(0, ROPE_DIMS, 2, dtype=F32) / ROPE_DIMS)
    freq_row = jnp.concatenate([inv_freq, inv_freq, jnp.zeros((128 - ROPE_DIMS,), F32)]).reshape(1, 128)
    posf = jnp.broadcast_to(positions.astype(F32).reshape(t, 1), (t, 128)) + start_token
    tabs = _rope_tables(posf, freq_row)
    tabs_s = [tabs if dil == 1 else [_to_stream(tb, bn, dil).reshape(t, 128) for tb in tabs] for dil in B_DIL]
    wa_in, conv_w, late_token = first_weights([h0] + [tb for ts in tabs_s for tb in ts])
    wa_main = _a_cols_to_head_major(wa_in[:, :A_MAIN])
    wa_tail = jnp.pad(wa_in[:, A_MAIN:], ((0, 0), (0, 128 - 2 * A_HEADS))) + late_token.astype(BF16)
    cw_hm = _conv_cols_to_head_major(conv_w)

    proj_a = _matmul(h0, wa_main, "nn", F32, "a_in_main", tm=2048)
    tail_a = _matmul(h0, wa_tail, "nn", F32, "a_in_tail")
    tail_t = jnp.swapaxes(tail_a[:, :2 * A_HEADS].reshape(bn, s, 2 * A_HEADS), 1, 2)
    tail_t = tail_t.reshape(bn, 2 * A_HEADS, n_chunks, A_CHUNK)
    beta, gc = _gdn_prep(tail_t, a_log[0], a_dt_bias[0])
    proj_a3 = proj_a.reshape(bn, s, A_MAIN)
    og_a, oraw_a, states, t_mats, conv_y = _gdn_fwd(proj_a3, cw_hm, beta, gc, a_norm_g)
    wa_out, wb_in, wb_out = late_weights(og_a)
    b_cols = [4 * B_W] + [3 * B_W] * (B_GROUPS - 1)
    x1, h1 = _out_proj(og_a.reshape(t, A_VW), wa_out, x0, "a_out", norm_g=norm_g[1:2])

    h1_s, proj_b, qkv_b, o_b, lse_b = [], [], [], [], []
    for gi, dil in enumerate(B_DIL):
        hs = h1 if dil == 1 else _to_stream(h1, bn, dil).reshape(t, d)
        ts = tabs_s[gi]
        pj = _matmul(hs, wb_in, "nn", BF16, f"b_in_g{gi}", tm=4096, tn=B_SUB, n=b_cols[gi], b_spec=pl.BlockSpec(
            (None, d, B_SUB), lambda i, j, kk, gi=gi: (_b_block(gi, j)[0], kk, _b_block(gi, j)[1])))
        qkv = _qk_prep(pj, *ts, b_q_norm_g[0, gi:gi + 1], b_k_norm_g[0, gi:gi + 1], f"qk_prep_g{gi}")
        o_s, lse_s = _attn_fwd(qkv.reshape(bn * dil, s // dil, 3 * B_W), f"attn_fwd_g{gi}")
        h1_s.append(hs), proj_b.append(pj), qkv_b.append(qkv)
        o_b.append(o_s.reshape(t, B_W) if dil == 1 else _from_stream(o_s, bn, dil))
        lse_b.append(lse_s.reshape(t, B_HEADS) if dil == 1 else _from_stream(lse_s, bn, dil))
    og_b = _merge_fwd(o_b, lse_b, proj_b[0])
    d_x2, loss_parts = _out_proj(og_b, wb_out, x1, "b_out_loss", target=loss_target.reshape(t, d))
    loss_local = jnp.sum(loss_parts)

    d_x2b = d_x2.astype(BF16)
    g_wb_out = _matmul(og_b, d_x2b, "tn", F32, "b_out_dw")
    d_og_b = _matmul(d_x2b, wb_out, "nt", BF16, "b_out_dx")
    d_o, lse_joint, delta, d_z = _merge_bwd(o_b, lse_b, proj_b[0], d_og_b)
    d_h1, g_qn, g_kn = [], [], []
    g_wb_in = lax.empty(wb_in.shape, F32)
    for gi, dil in enumerate(B_DIL):
        if dil == 1:
            do_s, lj_s, dl_s = d_o, lse_joint, delta
        else:
            do_s, lj_s, dl_s = (_to_stream(a, bn, dil).reshape(t, -1) for a in (d_o, lse_joint, delta))
        ns, ln = bn * dil, s // dil
        dq, dk, dv = _attn_bwd(qkv_b[gi].reshape(ns, ln, 3 * B_W), do_s.reshape(ns, ln, B_W),
                               lj_s.reshape(ns, ln, B_HEADS), dl_s.reshape(ns, ln, B_HEADS), f"attn_bwd_g{gi}")
        d_pj, d_gain = _qk_prep_bwd(proj_b[gi], *tabs_s[gi], b_q_norm_g[0, gi:gi + 1], b_k_norm_g[0, gi:gi + 1],
                                    dq.reshape(t, B_W), dk.reshape(t, B_W), dv.reshape(t, B_W),
                                    d_z if gi == 0 else None, f"qk_prep_bwd_g{gi}")
        g_wb_in = _matmul(h1_s[gi], d_pj, "tn", F32, f"b_in_dw_g{gi}", tn=B_SUB, tk=DW_K, into=(g_wb_in, pl.BlockSpec(
            (None, d, B_SUB), lambda i, j, kk, gi=gi: (_b_block(gi, j)[0], i, _b_block(gi, j)[1]))))
        dh = _b_in_dx(d_pj, wb_in, gi, f"b_in_dx_g{gi}")
        d_h1.append(dh if dil == 1 else _from_stream(dh.reshape(ns, ln, d), bn, dil))
        g_qn.append(d_gain[0]), g_kn.append(d_gain[1])
    d_x1, g_norm1 = _rms_bwd(x1, norm_g[1:2], d_h1, d_x2, "rms1_bwd")

    d_x1b = d_x1.astype(BF16)
    g_wa_out = _matmul(og_a.reshape(t, A_VW), d_x1b, "tn", F32, "a_out_dw", tk=DW_K)
    b_token = b_grads_ready(g_wb_in, g_wb_out, g_wa_out)
    d_og_a = _matmul(d_x1b, wa_out, "nt", BF16, "a_out_dx")
    d_pa, d_gc, d_beta, d_cw, d_ng = _gdn_bwd(proj_a3, cw_hm, beta, gc, a_norm_g + b_token, oraw_a, states,
                                              t_mats, conv_y, d_og_a.reshape(bn, s, A_VW))
    d_tail_t, d_alog, d_dtb = _gdn_prep_bwd(tail_t, a_log[0], a_dt_bias[0], d_gc, d_beta)
    d_tail = jnp.swapaxes(d_tail_t.reshape(bn, 2 * A_HEADS, s), 1, 2).reshape(t, 2 * A_HEADS)
    d_tail = jnp.pad(d_tail, ((0, 0), (0, 128 - 2 * A_HEADS))).astype(BF16)
    d_pa = d_pa.reshape(t, A_MAIN)
    g_wa_main = _matmul(h0, d_pa, "tn", F32, "a_in_dw_main", tk=DW_K)
    g_wa_tail = _matmul(h0, d_tail, "tn", F32, "a_in_dw_tail", tk=DW_K)
    g_wa_in = jnp.concatenate([_a_cols_from_head_major(g_wa_main), g_wa_tail[:, :2 * A_HEADS]], axis=1)
    a_token = a_grads_ready(g_wa_in)
    d_h0t = _matmul(d_tail + a_token.astype(BF16), wa_tail, "nt", F32, "a_in_dx_tail")
    d_x0, g_norm0 = _in_proj_bwd(d_pa, wa_main, d_h0t, x0, norm_g[0:1], d_x1, "a_in_dx_rms0_bwd")

    gfull = {
        "norm_g": jnp.concatenate([g_norm0, g_norm1], axis=0), "a_w_in": g_wa_in,
        "a_conv_w": _conv_cols_from_head_major(jnp.sum(d_cw, axis=0)),
        "a_log": jnp.sum(d_alog[:, :, 0], axis=0), "a_dt_bias": jnp.sum(d_dtb[:, :, 0], axis=0),
        "a_norm_g": jnp.sum(d_ng[:, :, 0, :], axis=(0, 1)), "a_w_out": g_wa_out, "b_w_in": g_wb_in,
        "b_q_norm_g": jnp.stack(g_qn), "b_k_norm_g": jnp.stack(g_kn), "b_w_out": g_wb_out}
    return loss_local, d_x0.reshape(bn, s, d), gfull


def kernel(x, positions, norm_g, a_w_in, a_conv_w, a_log, a_dt_bias, a_norm_g, a_w_out, b_w_in, b_q_norm_g, b_k_norm_g, b_w_out, loss_target, m_norm_g, m_a_w_in, m_a_conv_w, m_a_log, m_a_dt_bias, m_a_norm_g, m_a_w_out, m_b_w_in, m_b_q_norm_g, m_b_k_norm_g, m_b_w_out, v_norm_g, v_a_w_in, v_a_conv_w, v_a_log, v_a_dt_bias, v_a_norm_g, v_a_w_out, v_b_w_in, v_b_q_norm_g, v_b_k_norm_g, v_b_w_out):
    d = x.shape[2]
    my_c = lax.axis_index("c")
    my_chip = 2 * lax.axis_index("x") + lax.axis_index("y")

    half_index = jnp.reshape(my_c, (1,)).astype(jnp.int32)
    chip_index = jnp.reshape(my_chip, (1,)).astype(jnp.int32)
    def landing(shard):
        return lax.dynamic_update_slice(lax.empty((N_CHIPS,) + shard.shape, shard.dtype), shard[None],
                                        (my_chip,) + (0,) * shard.ndim)

    first_shards = [a_w_in[0].astype(BF16), a_conv_w[0]]
    first_plan = _first_gather_plan()
    first = _split_copy_start("first_weights_start", first_plan, first_shards,
                              [landing(s) for s in first_shards], half_index)
    pending = {}
    late_shards = [(w[0] + first[4][0, 0]).astype(BF16) for w in (a_w_out, b_w_in, b_w_out)]
    late_lands = [landing(s) for s in late_shards]

    def first_weights(after):
        _, (ga_in, g_conv) = _split_copy_wait("first_weights_wait", first_plan, *first[:4],
                                              list(after) + late_lands)
        ga_in = _sibling_forward(ga_in)
        wa_in = jnp.concatenate([ga_in[k] for k in range(N_CHIPS)], axis=1)
        conv_w = jnp.concatenate([g_conv[k] for k in range(N_CHIPS)], axis=1)
        plan = _gather_plan(len(late_shards))
        pending["late"] = (plan,) + tuple(_split_copy_start(
            "late_weights_start", plan, late_shards, late_lands, conv_w))
        return wa_in, conv_w, pending["late"][5][0, 0]

    def late_weights(after):
        plan, send, recv, srcs, lands, _ = pending["late"]
        _, (ga_out, gb_in, gb_out) = _split_copy_wait("late_weights_wait", plan, send, recv, srcs, lands, after)
        return ga_out.reshape(A_VW, d), gb_in, gb_out.reshape(B_W, d)

    def reduce_to_chip_sums(mats, tag):
        half = lambda g: lax.dynamic_slice_in_dim(g, (1 - my_c) * (g.shape[1] // 2), g.shape[1] // 2, axis=1)
        recv_sib = _sibling_swap([half(g).astype(BF16) for g in mats], f"grad_{tag}_sibling_swap")
        return [_pair_sum(g, r, half_index, f"grad_{tag}_pair_sum_{i}") for i, (g, r) in enumerate(zip(mats, recv_sib))]

    def start_exchange(tag, mats):
        sums = reduce_to_chip_sums(mats, tag)
        lands = [lax.empty((N_CHIPS - 1,) + s.shape[1:], BF16) for s in sums]
        plan = _exchange_plan(len(mats))
        pending[tag] = (plan,) + tuple(_split_copy_start(f"grad_{tag}_exchange_start", plan, sums, lands, chip_index))
        return pending[tag][5][0, 0]

    def finish_exchange(tag, after):
        plan, send, recv, srcs, lands, _ = pending[tag]
        return _split_copy_wait(f"grad_{tag}_exchange_wait", plan, send, recv, srcs, lands, after)

    def b_grads_ready(g_wb_in, g_wb_out, g_wa_out):
        return start_exchange("b", [g_wb_in, g_wb_out.reshape(N_CHIPS, -1, d), g_wa_out.reshape(N_CHIPS, -1, d)])

    def a_grads_ready(g_wa_in):
        return start_exchange("a", [_shard_major(g_wa_in, a_w_in.shape[2])])

    loss_local, d_x0, gfull = _local_step(x, positions, loss_target, norm_g, a_log, a_dt_bias, a_norm_g,
                                          b_q_norm_g, b_k_norm_g, first[4][0, 0], first_weights, late_weights,
                                          b_grads_ready, a_grads_ready)

    small = [gfull["norm_g"], gfull["a_conv_w"], gfull["a_log"], gfull["a_dt_bias"], gfull["a_norm_g"],
             gfull["b_q_norm_g"], gfull["b_k_norm_g"], loss_local]
    packed, offs = _pack_rows(small)
    reduced = _small_allreduce(packed)
    g_norm, g_conv_all, g_alog, g_dtb, g_ang, g_q, g_k, loss = _unpack_rows(reduced, offs)
    g_conv_mine = lax.dynamic_slice_in_dim(g_conv_all, my_chip * a_conv_w.shape[2], a_conv_w.shape[2], axis=1)

    b_sums, b_received = finish_exchange("b", d_x0)
    a_sums, a_received = finish_exchange("a", reduced)
    chip_sums = [a_sums[0], b_sums[2], b_sums[0], b_sums[1]]
    received = [a_received[0], b_received[2], b_received[0], b_received[1]]
    halves = [_chip_sum(s, r, chip_index, f"grad_chip_sum_{i}") for i, (s, r) in enumerate(zip(chip_sums, received))]
    theirs = _sibling_swap(halves, "grad_sibling_join")
    big = ("a_w_in", "a_w_out", "b_w_in", "b_w_out")
    big_halves = dict(zip(big, zip(halves, theirs)))

    grads = {
        "norm_g": g_norm, "a_conv_w": g_conv_mine[None], "a_log": g_alog[None], "a_dt_bias": g_dtb[None],
        "a_norm_g": g_ang[None], "b_q_norm_g": g_q[None], "b_k_norm_g": g_k[None]}
    weights = {"norm_g": norm_g, "a_w_in": a_w_in, "a_conv_w": a_conv_w, "a_log": a_log, "a_dt_bias": a_dt_bias,
               "a_norm_g": a_norm_g, "a_w_out": a_w_out, "b_w_in": b_w_in, "b_q_norm_g": b_q_norm_g,
               "b_k_norm_g": b_k_norm_g, "b_w_out": b_w_out}
    m_in = {"norm_g": m_norm_g, "a_w_in": m_a_w_in, "a_conv_w": m_a_conv_w, "a_log": m_a_log,
            "a_dt_bias": m_a_dt_bias, "a_norm_g": m_a_norm_g, "a_w_out": m_a_w_out, "b_w_in": m_b_w_in,
            "b_q_norm_g": m_b_q_norm_g, "b_k_norm_g": m_b_k_norm_g, "b_w_out": m_b_w_out}
    v_in = {"norm_g": v_norm_g, "a_w_in": v_a_w_in, "a_conv_w": v_a_conv_w, "a_log": v_a_log,
            "a_dt_bias": v_a_dt_bias, "a_norm_g": v_a_norm_g, "a_w_out": v_a_w_out, "b_w_in": v_b_w_in,
            "b_q_norm_g": v_b_q_norm_g, "b_k_norm_g": v_b_k_norm_g, "b_w_out": v_b_w_out}
    names = list(weights)

    delta_w, new_m, new_v = {}, {}, {}
    for nm in big:
        mine, other = big_halves[nm]
        if weights[nm].shape[2] % 128:
            cols = lambda a: jnp.transpose(a, (2, 0, 1))
            half_cols = lambda a: jnp.transpose(a)[:, None, :]
            outs = _adamw_shard_cols(cols(weights[nm]), half_cols(mine), half_cols(other), cols(m_in[nm]),
                                     cols(v_in[nm]), half_index, f"adamw_{nm}")
            outs = [jnp.transpose(o, (1, 2, 0)) for o in outs]
        else:
            outs = _adamw_shard(weights[nm], mine, other, m_in[nm], v_in[nm], half_index, f"adamw_{nm}")
        grads[nm], delta_w[nm], new_m[nm], new_v[nm] = outs
    small_names = [nm for nm in names if nm not in big]
    packs = [_pack_rows([src[nm] for nm in small_names]) for src in (weights, grads, m_in, v_in)]
    offs = packs[0][1]
    dl, m2, v2 = _adamw(packs[0][0], packs[1][0], packs[2][0], packs[3][0], "adamw_small")
    for nm, a, b, c2 in zip(small_names, _unpack_rows(dl, offs), _unpack_rows(m2, offs), _unpack_rows(v2, offs)):
        delta_w[nm], new_m[nm], new_v[nm] = a, b, c2

    return (loss, d_x0, *[grads[nm] for nm in names], *[delta_w[nm] for nm in names],
            *[new_m[nm] for nm in names], *[new_v[nm] for nm in names])
```

```python
import jax
import jax.numpy as jnp
import numpy as np
from jax import lax
from jax.experimental import pallas as pl
from jax.experimental.pallas import tpu as pltpu

F32 = jnp.float32
BF16 = jnp.bfloat16
MESH = pl.DeviceIdType.MESH

EPS = 1e-6
A_HEADS = 8
A_DK = 128
A_DV = 256
A_QK = A_HEADS * A_DK
A_VW = A_HEADS * A_DV
A_MAIN = 2 * A_QK + 2 * A_VW
A_HEAD_COLS = 2 * A_DK + 2 * A_DV
A_CONV_COLS = 2 * A_DK + A_DV
A_CHUNK = 64
A_CONV = 4
B_GROUPS = 3
B_HEADS = 8
B_DH = 128
B_W = B_HEADS * B_DH
B_DIL = (1, 4, 16)
B_BLK = 128
ROPE_THETA = 500000.0
ROPE_DIMS = B_DH // 4
ADAM_LR, ADAM_B1, ADAM_B2, ADAM_EPS, ADAM_WD, ADAM_STEP = 0.001, 0.9, 0.999, 1e-08, 0.01, 10
N_CHIPS = 4
VMEM_BIG = 56 * 1024 * 1024
DW_K = 4096


def _params(sem=None, vmem=None):
    return pltpu.CompilerParams(dimension_semantics=sem, vmem_limit_bytes=vmem)


def _dot(a, b, ca, cb):
    return lax.dot_general(a.astype(BF16), b.astype(BF16), (((ca,), (cb,)), ((), ())),
                           preferred_element_type=F32)


def _split3(a):
    hi = a.astype(BF16)
    r = a - hi.astype(F32)
    mid = r.astype(BF16)
    lo = (r - mid.astype(F32)).astype(BF16)
    return hi, mid, lo


def _sigmoid(y):
    return 1.0 / (1.0 + jnp.exp(-y))


def _silu(y):
    return y * _sigmoid(y)


def _silu_and_slope(y):
    s = _sigmoid(y)
    return y * s, s * (1.0 + y * (1.0 - s))


def _matmul(a, b, mode, out_dtype, name, res=None, tm=1024, tn=1024, tk=1024, n=None, b_spec=None, into=None):
    m, k = a.shape[::-1] if mode == "tn" else a.shape
    if n is None:
        n = b.shape[0] if mode == "nt" else b.shape[1]
    tm, tn, tk = min(tm, m), min(tn, n), min(tk, k)
    assert m % tm == 0 and n % tn == 0 and k % tk == 0, (name, a.shape, b.shape)
    nk = k // tk
    dims = {"nn": ((1,), (0,)), "nt": ((1,), (1,)), "tn": ((0,), (0,))}[mode]

    def body(*refs):
        a_ref, b_ref = refs[0], refs[1]
        r_ref = refs[2] if res is not None else None
        o_ref = refs[2 + (res is not None) + (into is not None)]
        prod = lax.dot_general(a_ref[...], b_ref[...], (dims, ((), ())), preferred_element_type=F32)

        def finish(r):
            if res is not None:
                r = r + r_ref[...]
            o_ref[...] = r.astype(out_dtype)

        if nk == 1:
            finish(prod)
            return
        acc = refs[-1]
        kk = pl.program_id(2)

        @pl.when(kk == 0)
        def _():
            acc[...] = prod

        @pl.when((kk > 0) & (kk < nk - 1))
        def _():
            acc[...] += prod

        @pl.when(kk == nk - 1)
        def _():
            finish(acc[...] + prod)

    a_spec = pl.BlockSpec((tm, tk), lambda i, j, kk: (i, kk))
    if mode == "tn":
        a_spec = pl.BlockSpec((tk, tm), lambda i, j, kk: (kk, i))
    if b_spec is None and mode == "nt":
        b_spec = pl.BlockSpec((tn, tk), lambda i, j, kk: (j, kk))
    elif b_spec is None:
        b_spec = pl.BlockSpec((tk, tn), lambda i, j, kk: (kk, j))
    in_specs = [a_spec, b_spec]
    args = [a, b]
    if res is not None:
        in_specs.append(pl.BlockSpec((tm, tn), lambda i, j, kk: (i, j)))
        args.append(res)
    out_spec = pl.BlockSpec((tm, tn), lambda i, j, kk: (i, j))
    out_shape = jax.ShapeDtypeStruct((m, n), out_dtype)
    aliases = {}
    if into is not None:
        assert res is None
        buf, out_spec = into
        out_shape = jax.ShapeDtypeStruct(buf.shape, buf.dtype)
        in_specs.append(ANY)
        args.append(buf)
        aliases = {2: 0}
    return pl.pallas_call(
        body, name=name, grid=(m // tm, n // tn, nk),
        in_specs=in_specs, out_specs=out_spec, out_shape=out_shape, input_output_aliases=aliases,
        scratch_shapes=[pltpu.VMEM((tm, tn), F32)] if nk > 1 else [],
        compiler_params=_params(("parallel", "parallel", "arbitrary"), 48 * 1024 * 1024),
    )(*args)


def _rms_fwd(x, g, name, tm=512):
    t, d = x.shape

    def body(x_ref, g_ref, h_ref):
        xv = x_ref[...]
        r = lax.rsqrt(jnp.mean(xv * xv, axis=-1, keepdims=True) + EPS)
        h_ref[...] = (xv * r * g_ref[...]).astype(BF16)

    return pl.pallas_call(
        body, name=name, grid=(t // tm,),
        in_specs=[pl.BlockSpec((tm, d), lambda i: (i, 0)), pl.BlockSpec((1, d), lambda i: (0, 0))],
        out_specs=pl.BlockSpec((tm, d), lambda i: (i, 0)),
        out_shape=jax.ShapeDtypeStruct((t, d), BF16),
        compiler_params=_params(("parallel",)),
    )(x, g)


def _rms_bwd(x, g, dhs, dres, name, tm=512):
    t, d = x.shape
    n_dh = len(dhs)

    def body(*refs):
        x_ref, g_ref = refs[0], refs[1]
        dh_refs = refs[2:2 + n_dh]
        dres_ref, dx_ref, dg_ref = refs[2 + n_dh:]
        i = pl.program_id(0)

        @pl.when(i == 0)
        def _():
            dg_ref[...] = jnp.zeros_like(dg_ref)

        xv = x_ref[...]
        r = lax.rsqrt(jnp.mean(xv * xv, axis=-1, keepdims=True) + EPS)
        xh = xv * r
        dh = dh_refs[0][...].astype(F32)
        for ref in dh_refs[1:]:
            dh = dh + ref[...].astype(F32)
        dg_ref[0:1, :] += jnp.sum(dh * xh, axis=0, keepdims=True)
        dxh = dh * g_ref[...]
        dx = r * (dxh - xh * jnp.mean(dxh * xh, axis=-1, keepdims=True))
        dx_ref[...] = dx + dres_ref[...]

    row = pl.BlockSpec((tm, d), lambda i: (i, 0))
    dx, dg = pl.pallas_call(
        body, name=name, grid=(t // tm,),
        in_specs=[row, pl.BlockSpec((1, d), lambda i: (0, 0))] + [row] * n_dh + [row],
        out_specs=[row, pl.BlockSpec((8, d), lambda i: (0, 0))],
        out_shape=[jax.ShapeDtypeStruct((t, d), F32), jax.ShapeDtypeStruct((8, d), F32)],
        compiler_params=_params(("arbitrary",)),
    )(x, g, *dhs, dres)
    return dx, dg[0:1]


def _b_in_dx(dp, w_shards, gi, name, tm=2048):
    t, k = dp.shape
    d = w_shards.shape[1]
    nk = k // (2 * B_SUB)
    dn = (((1,), (1,)), ((), ()))

    def body(a_ref, w0_ref, w1_ref, o_ref, acc):
        kk = pl.program_id(1)
        prod = (lax.dot_general(a_ref[:, 0:B_SUB], w0_ref[...], dn, preferred_element_type=F32)
                + lax.dot_general(a_ref[:, B_SUB:2 * B_SUB], w1_ref[...], dn, preferred_element_type=F32))

        @pl.when(kk == 0)
        def _():
            acc[...] = prod

        @pl.when((kk > 0) & (kk < nk - 1))
        def _():
            acc[...] += prod

        @pl.when(kk == nk - 1)
        def _():
            o_ref[...] = (acc[...] + prod).astype(BF16)

    w_block = lambda part: pl.BlockSpec(
        (None, d, B_SUB), lambda i, kk: (_b_block(gi, 2 * kk + part)[0], 0, _b_block(gi, 2 * kk + part)[1]))
    return pl.pallas_call(
        body, name=name, grid=(t // tm, nk),
        in_specs=[pl.BlockSpec((tm, 2 * B_SUB), lambda i, kk: (i, kk)), w_block(0), w_block(1)],
        out_specs=pl.BlockSpec((tm, d), lambda i, kk: (i, 0)),
        out_shape=jax.ShapeDtypeStruct((t, d), BF16),
        scratch_shapes=[pltpu.VMEM((tm, d), F32)],
        compiler_params=_params(("parallel", "arbitrary"), 48 * 1024 * 1024),
    )(dp, w_shards, w_shards)


def _in_proj_bwd(dp, w, dh_more, x, g, dres, name, tm=512, tk=3072):
    t, k = dp.shape
    d = w.shape[0]
    nk = k // tk

    def body(dp_ref, w_ref, more_ref, x_ref, g_ref, dres_ref, dx_ref, dg_ref, acc):
        i, kk = pl.program_id(0), pl.program_id(1)

        @pl.when((i == 0) & (kk == 0))
        def _():
            dg_ref[...] = jnp.zeros_like(dg_ref)

        prod = lax.dot_general(dp_ref[...], w_ref[...], (((1,), (1,)), ((), ())), preferred_element_type=F32)

        @pl.when(kk == 0)
        def _():
            acc[...] = prod

        @pl.when((kk > 0) & (kk < nk - 1))
        def _():
            acc[...] += prod

        @pl.when(kk == nk - 1)
        def _():
            dh = acc[...] + prod + more_ref[...]
            xv = x_ref[...]
            r = lax.rsqrt(jnp.mean(xv * xv, axis=-1, keepdims=True) + EPS)
            xh = xv * r
            dg_ref[0:1, :] += jnp.sum(dh * xh, axis=0, keepdims=True)
            dxh = dh * g_ref[...]
            dx_ref[...] = r * (dxh - xh * jnp.mean(dxh * xh, axis=-1, keepdims=True)) + dres_ref[...]

    row = pl.BlockSpec((tm, d), lambda i, kk: (i, 0))
    dx, dg = pl.pallas_call(
        body, name=name, grid=(t // tm, nk),
        in_specs=[pl.BlockSpec((tm, tk), lambda i, kk: (i, kk)), pl.BlockSpec((d, tk), lambda i, kk: (0, kk)),
                  row, row, pl.BlockSpec((1, d), lambda i, kk: (0, 0)), row],
        out_specs=[row, pl.BlockSpec((8, d), lambda i, kk: (0, 0))],
        out_shape=[jax.ShapeDtypeStruct((t, d), F32), jax.ShapeDtypeStruct((8, d), F32)],
        scratch_shapes=[pltpu.VMEM((tm, d), F32)],
        compiler_params=_params(("arbitrary", "arbitrary"), 48 * 1024 * 1024),
    )(dp, w, dh_more, x, g, dres)
    return dx, dg[0:1]


def _out_proj(a, w, res, name, norm_g=None, target=None, tm=512):
    t, k = a.shape
    d = w.shape[1]
    nb = t // tm

    def body(a_ref, w_ref, r_ref, x_ref, o1_ref, o2_ref):
        y = jnp.dot(a_ref[...], w_ref[...], preferred_element_type=F32) + r_ref[...]
        if norm_g is not None:
            o1_ref[...] = y
            r = lax.rsqrt(jnp.mean(y * y, axis=-1, keepdims=True) + EPS)
            o2_ref[...] = (y * r * x_ref[...]).astype(BF16)
        else:
            e = y - x_ref[...]
            o1_ref[...] = e * (1.0 / d)
            s = jnp.sum(jnp.sum(e * e, axis=1, keepdims=True), axis=0, keepdims=True) * (0.5 / d)
            o2_ref[...] = jnp.broadcast_to(s, (8, 128))

    row = pl.BlockSpec((tm, d), lambda i: (i, 0))
    if norm_g is not None:
        extra, extra_spec = norm_g, pl.BlockSpec((1, d), lambda i: (0, 0))
        out2_spec, out2_shape = row, jax.ShapeDtypeStruct((t, d), BF16)
    else:
        extra, extra_spec = target, row
        out2_spec = pl.BlockSpec((None, 8, 128), lambda i: (i, 0, 0))
        out2_shape = jax.ShapeDtypeStruct((nb, 8, 128), F32)
    o1, o2 = pl.pallas_call(
        body, name=name, grid=(nb,),
        in_specs=[pl.BlockSpec((tm, k), lambda i: (i, 0)), pl.BlockSpec((k, d), lambda i: (0, 0)), row, extra_spec],
        out_specs=[row, out2_spec], out_shape=[jax.ShapeDtypeStruct((t, d), F32), out2_shape],
        compiler_params=_params(("parallel",), 48 * 1024 * 1024),
    )(a, w, res, extra)
    return (o1, o2) if norm_g is not None else (o1, o2[:, 0, 0])


def _softplus(x):
    t = jnp.exp(-jnp.abs(x))
    return jnp.maximum(x, 0.0) + jnp.where(t < 1e-3, t * (1.0 - 0.5 * t), jnp.log(1.0 + t))


def _tri(rows_le_cols):
    r = lax.broadcasted_iota(jnp.int32, (A_CHUNK, A_CHUNK), 0)
    c = lax.broadcasted_iota(jnp.int32, (A_CHUNK, A_CHUNK), 1)
    return jnp.where((r <= c) if rows_le_cols else (r >= c), 1.0, 0.0).astype(BF16)


def _dot_exact_rhs(a, ones_bf16):
    dn = (((1,), (0,)), ((), ()))
    hi, mid, lo = _split3(a)
    out = lax.dot_general(hi, ones_bf16, dn, preferred_element_type=F32)
    out = out + lax.dot_general(mid, ones_bf16, dn, preferred_element_type=F32)
    return out + lax.dot_general(lo, ones_bf16, dn, preferred_element_type=F32)


def _gdn_prep(tail_t, a_log, dt_bias):
    bn, _, n, c = tail_t.shape

    def body(t_ref, alog_ref, dtb_ref, beta_ref, gc_ref):
        upper = _tri(True)
        for h in range(A_HEADS):
            beta_ref[h] = _sigmoid(t_ref[h])
            ea = jnp.exp(jnp.full((n, c), alog_ref[h], F32))
            g = -ea * _softplus(t_ref[A_HEADS + h] + dtb_ref[h])
            gc_ref[h] = _dot_exact_rhs(g, upper)

    smem = pl.BlockSpec(memory_space=pltpu.SMEM)
    blk = pl.BlockSpec((None, A_HEADS, n, c), lambda b: (b, 0, 0, 0))
    return pl.pallas_call(
        body, name="gdn_prep", grid=(bn,),
        in_specs=[pl.BlockSpec((None, 2 * A_HEADS, n, c), lambda b: (b, 0, 0, 0)), smem, smem],
        out_specs=[blk, blk],
        out_shape=[jax.ShapeDtypeStruct((bn, A_HEADS, n, c), F32)] * 2,
        compiler_params=_params(("parallel",)),
    )(tail_t, a_log, dt_bias)


def _gdn_prep_bwd(tail_t, a_log, dt_bias, d_gc, d_beta):
    bn, _, n, c = tail_t.shape

    def body(t_ref, alog_ref, dtb_ref, dgc_ref, dbeta_ref, dt_ref, dal_ref, ddt_ref):
        lower = _tri(False)
        for h in range(A_HEADS):
            beta = _sigmoid(t_ref[h])
            dt_ref[h] = dbeta_ref[h] * beta * (1.0 - beta)
            dg = _dot_exact_rhs(dgc_ref[h], lower)
            ea = jnp.exp(jnp.full((n, c), alog_ref[h], F32))
            xa = t_ref[A_HEADS + h] + dtb_ref[h]
            g = -ea * _softplus(xa)
            dxa = -ea * dg * _sigmoid(xa)
            dt_ref[A_HEADS + h] = dxa
            s1 = jnp.sum(jnp.sum(g * dg, axis=1, keepdims=True), axis=0, keepdims=True)
            s2 = jnp.sum(jnp.sum(dxa, axis=1, keepdims=True), axis=0, keepdims=True)
            dal_ref[h:h + 1, :] = jnp.broadcast_to(s1, (1, 128))
            ddt_ref[h:h + 1, :] = jnp.broadcast_to(s2, (1, 128))

    smem = pl.BlockSpec(memory_space=pltpu.SMEM)
    blk8 = pl.BlockSpec((None, A_HEADS, n, c), lambda b: (b, 0, 0, 0))
    blk16 = pl.BlockSpec((None, 2 * A_HEADS, n, c), lambda b: (b, 0, 0, 0))
    sm = pl.BlockSpec((None, A_HEADS, 128), lambda b: (b, 0, 0))
    return pl.pallas_call(
        body, name="gdn_prep_bwd", grid=(bn,),
        in_specs=[blk16, smem, smem, blk8, blk8],
        out_specs=[blk16, sm, sm],
        out_shape=[jax.ShapeDtypeStruct((bn, 2 * A_HEADS, n, c), F32),
                   jax.ShapeDtypeStruct((bn, A_HEADS, 128), F32),
                   jax.ShapeDtypeStruct((bn, A_HEADS, 128), F32)],
        compiler_params=_params(("parallel",)),
    )(tail_t, a_log, dt_bias, d_gc, d_beta)


HALO = 8


def _conv_taps(xw, w):
    y = w[A_CONV - 1:A_CONV, :] * xw
    for j in range(1, A_CONV):
        y = y + w[A_CONV - 1 - j:A_CONV - j, :] * pltpu.roll(xw, j, 0)
    return y[HALO:, :]


def _row_to_col(row, eye):
    c = eye.shape[0]
    return jnp.sum(jnp.where(eye, jnp.broadcast_to(row, (c, c)), 0.0), axis=1, keepdims=True)


def _col_to_row(col, eye):
    c = eye.shape[0]
    return jnp.sum(jnp.where(eye, jnp.broadcast_to(col, (c, c)), 0.0), axis=0, keepdims=True)


def _unit_lower_inverse(a, ri, ci):
    eye = jnp.where(ri == ci, 1.0, 0.0)
    a8 = jnp.where((ri >> 3) == (ci >> 3), a, 0.0)
    a2 = _dot(a8, a8, 1, 0)
    yield
    a4 = _dot(a2, a2, 1, 0)
    t = eye - a8
    t = t + _dot(t, a2, 1, 0)
    yield
    t = t + _dot(t, a4, 1, 0)
    yield
    for sh in (3, 4, 5):
        off = jnp.where(((ri >> (sh + 1)) == (ci >> (sh + 1))) & ((ri >> sh) != (ci >> sh)), a, 0.0)
        left = _dot(t, off, 1, 0)
        yield
        t = t - _dot(left, t, 1, 0)
        yield
    return t


def _round_robin(gens):
    live = list(gens)
    while live:
        nxt = []
        for g in live:
            try:
                next(g)
                nxt.append(g)
            except StopIteration:
                pass
        live = nxt


def _gdn_chunk_core(q, k, v, g_row, b_row, t_mat, ri, ci):
    eye = ri == ci
    g_col = _row_to_col(g_row, eye)
    b_col = _row_to_col(b_row, eye)
    causal = ri >= ci
    strict = ri > ci
    dec = jnp.where(causal, jnp.exp(jnp.where(causal, g_col - g_row, 0.0)), 0.0)
    gam = jnp.exp(g_col)
    g_last = g_row[:, A_CHUNK - 1:A_CHUNK]
    gam_last = jnp.exp(g_last)
    e = jnp.exp(g_last - g_col)
    kb = k * b_col
    bv = v * b_col
    kbg = kb * gam
    q16, k16, kb16 = q.astype(BF16), k.astype(BF16), kb.astype(BF16)
    kk = _dot(kb16, k16, 1, 1)
    p = _dot(q16, k16, 1, 1) * dec
    yield
    a_mat = jnp.where(strict, kk * dec, 0.0)
    if t_mat is None:
        t_mat = yield from _unit_lower_inverse(a_mat, ri, ci)
    t16 = t_mat.astype(BF16)
    u = _dot(t16, bv, 1, 0)
    w = _dot(t16, kbg, 1, 0)
    yield
    return dict(eye=eye, g_col=g_col, b_col=b_col, dec=dec, strict=strict, causal=causal, gam=gam,
                gam_last=gam_last, e=e, kb=kb, bv=bv, kbg=kbg, a_mat=a_mat, t_mat=t_mat, u=u, w=w, p=p,
                qg=q * gam, kd=k * e, q16=q16, k16=k16, kb16=kb16, t16=t16)


A_SEQ_BLK = 256
A_BLK_CHUNKS = A_SEQ_BLK // A_CHUNK


def _gdn_halo(proj_hm):
    bn, s, w = proj_hm.shape
    last = proj_hm.reshape(bn, s // A_SEQ_BLK, A_SEQ_BLK, w)[:, :, A_SEQ_BLK - HALO:, :]
    return jnp.concatenate([jnp.zeros((bn, 1, HALO, w), proj_hm.dtype), last[:, :-1]], axis=1)


def _gdn_window(x_ref, halo_ref, ci, first, lo):
    if first:
        return jnp.concatenate([halo_ref[:, lo:lo + A_CONV_COLS], x_ref[0:A_CHUNK, lo:lo + A_CONV_COLS]], axis=0)
    start = pl.multiple_of(ci * A_CHUNK - HALO, HALO)
    return x_ref[pl.ds(start, A_CHUNK + HALO), lo:lo + A_CONV_COLS]


def _gdn_chunk_prep(xw, cw, y=None):
    if y is None:
        y = _conv_taps(xw, cw)
    a, slope = _silu_and_slope(y)
    aq, ak, v = a[:, 0:A_DK], a[:, A_DK:2 * A_DK], a[:, 2 * A_DK:]
    rq = lax.rsqrt(jnp.sum(aq * aq, axis=1, keepdims=True) + EPS)
    rk = lax.rsqrt(jnp.sum(ak * ak, axis=1, keepdims=True) + EPS)
    return dict(xw=xw, y=y, slope=slope, aq=aq, ak=ak, rq=rq, rk=rk, q=aq * rq * (A_DK ** -0.5), k=ak * rk, v=v)


def _gdn_fwd(proj_hm, cw_hm, beta, gc, norm_g, hp=8):
    bn, s, _ = proj_hm.shape
    n = s // A_CHUNK
    nsb = s // A_SEQ_BLK
    halo = _gdn_halo(proj_hm)

    def body(x_ref, halo_ref, cw_ref, beta_ref, gc_ref, ng_ref, og_ref, oraw_ref, st_ref, t_ref, y_ref, state):
        first_chunk = pl.program_id(2) * A_BLK_CHUNKS
        ri = lax.broadcasted_iota(jnp.int32, (A_CHUNK, A_CHUNK), 0)
        ci_ = lax.broadcasted_iota(jnp.int32, (A_CHUNK, A_CHUNK), 1)
        ng = ng_ref[...]

        @pl.when(pl.program_id(2) == 0)
        def _():
            state[...] = jnp.zeros_like(state)

        def one_head(hh, ci, first, rows):
            lo = hh * A_HEAD_COLS
            cw = cw_ref[:, hh * A_CONV_COLS:(hh + 1) * A_CONV_COLS]
            cin = _gdn_chunk_prep(_gdn_window(x_ref, halo_ref, ci, first, lo), cw)
            y_ref[rows, hh * A_CONV_COLS:(hh + 1) * A_CONV_COLS] = cin["y"]
            seq_chunk = pl.ds(first_chunk + ci, 1)
            core = yield from _gdn_chunk_core(cin["q"], cin["k"], cin["v"], gc_ref[hh, seq_chunk, :],
                                              beta_ref[hh, seq_chunk, :], None, ri, ci_)
            st = state[hh]
            st_ref[hh, ci] = st
            t_ref[hh, ci] = core["t_mat"]
            st16 = st.astype(BF16)
            vn = core["u"] - _dot(core["w"], st16, 1, 0)
            qs = _dot(core["qg"], st16, 1, 0)
            yield
            vn16 = vn.astype(BF16)
            o = qs + _dot(core["p"], vn16, 1, 0)
            state[hh] = st * core["gam_last"] + _dot(core["kd"], vn16, 0, 0)
            yield
            ocols = slice(hh * A_DV, (hh + 1) * A_DV)
            oraw_ref[rows, ocols] = o
            r = lax.rsqrt(jnp.mean(o * o, axis=1, keepdims=True) + EPS)
            z = x_ref[rows, lo + A_CONV_COLS:lo + A_HEAD_COLS]
            og_ref[rows, ocols] = (o * r * ng * _silu(z)).astype(BF16)

        def chunk(ci, first):
            rows = pl.ds(0 if first else pl.multiple_of(ci * A_CHUNK, A_CHUNK), A_CHUNK)
            _round_robin([one_head(hh, ci, first, rows) for hh in range(hp)])

        chunk(0, True)
        lax.fori_loop(1, A_BLK_CHUNKS, lambda i, c: (chunk(i, False), c)[1], 0)

    small = pl.BlockSpec((None, hp, n, A_CHUNK), lambda b, h, j: (b, h, 0, 0))
    return pl.pallas_call(
        body, name="gdn_fwd", grid=(bn, A_HEADS // hp, nsb),
        in_specs=[pl.BlockSpec((None, A_SEQ_BLK, hp * A_HEAD_COLS), lambda b, h, j: (b, j, h)),
                  pl.BlockSpec((None, None, HALO, hp * A_HEAD_COLS), lambda b, h, j: (b, j, 0, h)),
                  pl.BlockSpec((A_CONV, hp * A_CONV_COLS), lambda b, h, j: (0, h)),
                  small, small,
                  pl.BlockSpec((1, A_DV), lambda b, h, j: (0, 0))],
        out_specs=[pl.BlockSpec((None, A_SEQ_BLK, hp * A_DV), lambda b, h, j: (b, j, h)),
                   pl.BlockSpec((None, A_SEQ_BLK, hp * A_DV), lambda b, h, j: (b, j, h)),
                   pl.BlockSpec((None, hp, A_BLK_CHUNKS, A_DK, A_DV), lambda b, h, j: (b, h, j, 0, 0)),
                   pl.BlockSpec((None, hp, A_BLK_CHUNKS, A_CHUNK, A_CHUNK), lambda b, h, j: (b, h, j, 0, 0)),
                   pl.BlockSpec((None, A_SEQ_BLK, hp * A_CONV_COLS), lambda b, h, j: (b, j, h))],
        out_shape=[jax.ShapeDtypeStruct((bn, s, A_VW), BF16),
                   jax.ShapeDtypeStruct((bn, s, A_VW), F32),
                   jax.ShapeDtypeStruct((bn, A_HEADS, n, A_DK, A_DV), F32),
                   jax.ShapeDtypeStruct((bn, A_HEADS, n, A_CHUNK, A_CHUNK), F32),
                   jax.ShapeDtypeStruct((bn, s, A_HEADS * A_CONV_COLS), F32)],
        scratch_shapes=[pltpu.VMEM((hp, A_DK, A_DV), F32)],
        compiler_params=_params(("parallel", "parallel", "arbitrary"), VMEM_BIG),
    )(proj_hm, halo, cw_hm, beta, gc, norm_g)


def _gdn_bwd(proj_hm, cw_hm, beta, gc, norm_g, oraw, states, t_mats, conv_y, dog, hp=4):
    bn, s, _ = proj_hm.shape
    n = s // A_CHUNK
    nsb = s // A_SEQ_BLK
    halo = _gdn_halo(proj_hm)

    def body(x_ref, halo_ref, cw_ref, beta_ref, gc_ref, ng_ref, oraw_ref, st_ref, t_ref, y_ref, dog_ref,
             dx_ref, dgc_ref, dbeta_ref, dcw_ref, dng_ref, dstate, dy_next, shifted):
        first_chunk = (nsb - 1 - pl.program_id(2)) * A_BLK_CHUNKS
        ri = lax.broadcasted_iota(jnp.int32, (A_CHUNK, A_CHUNK), 0)
        ci_ = lax.broadcasted_iota(jnp.int32, (A_CHUNK, A_CHUNK), 1)
        lane = lax.broadcasted_iota(jnp.int32, (1, A_CHUNK), 1)
        ng = ng_ref[...]

        @pl.when(pl.program_id(2) == 0)
        def _():
            dstate[...] = jnp.zeros_like(dstate)
            dy_next[...] = jnp.zeros_like(dy_next)
            dcw_ref[...] = jnp.zeros_like(dcw_ref)
            dng_ref[...] = jnp.zeros_like(dng_ref)

        def one_head(hh, ci, first, rows):
            lo = hh * A_HEAD_COLS
            ccols = slice(hh * A_CONV_COLS, (hh + 1) * A_CONV_COLS)
            ocols = slice(hh * A_DV, (hh + 1) * A_DV)
            cw = cw_ref[:, ccols]
            cin = _gdn_chunk_prep(_gdn_window(x_ref, halo_ref, ci, first, lo), cw, y_ref[rows, ccols])
            q, k, v = cin["q"], cin["k"], cin["v"]
            seq_chunk = pl.ds(first_chunk + ci, 1)
            cr = yield from _gdn_chunk_core(q, k, v, gc_ref[hh, seq_chunk, :], beta_ref[hh, seq_chunk, :],
                                            t_ref[hh, ci], ri, ci_)
            eye, dec, gam, e = cr["eye"], cr["dec"], cr["gam"], cr["e"]
            b_col, t_mat, u, w, p = cr["b_col"], cr["t_mat"], cr["u"], cr["w"], cr["p"]
            st = st_ref[hh, ci]
            ds_out = dstate[hh]

            o = oraw_ref[rows, ocols]
            z = x_ref[rows, lo + A_CONV_COLS:lo + A_HEAD_COLS]
            d_og = dog_ref[rows, ocols].astype(F32)
            r = lax.rsqrt(jnp.mean(o * o, axis=1, keepdims=True) + EPS)
            oh = o * r
            gate, gate_slope = _silu_and_slope(z)
            d_on = d_og * gate
            dz = d_og * oh * ng * gate_slope
            dng_ref[hh, 0:1, :] += jnp.sum(d_on * oh, axis=0, keepdims=True)
            d_oh = d_on * ng
            d_o = r * (d_oh - oh * jnp.mean(d_oh * oh, axis=1, keepdims=True))

            st16, ds16, do16, w16 = st.astype(BF16), ds_out.astype(BF16), d_o.astype(BF16), w.astype(BF16)
            q16, k16, t16 = cr["q16"], cr["k16"], cr["t16"]
            vn = u - _dot(w16, st16, 1, 0)
            d_vn = _dot(p, do16, 0, 0) + _dot(cr["kd"], ds16, 1, 0)
            d_qg = _dot(do16, st16, 1, 1)
            qgdo = _dot(cr["qg"], do16, 0, 0)
            yield
            vn16, dvn16 = vn.astype(BF16), d_vn.astype(BF16)
            d_p = jnp.where(cr["causal"], _dot(do16, vn16, 1, 1), 0.0)
            d_kd = _dot(vn16, ds16, 1, 1)
            d_gam_last = jnp.sum(jnp.sum(st * ds_out, axis=1, keepdims=True), axis=0, keepdims=True)
            d_w = -_dot(dvn16, st16, 1, 1)
            dstate[hh] = qgdo + ds_out * cr["gam_last"] - _dot(w16, dvn16, 0, 0)
            d_bv = _dot(t16, dvn16, 0, 0)
            yield
            d_kbg = _dot(t16, d_w, 0, 0)
            n_p = (d_p * dec).astype(BF16)
            d_q = _dot(n_p, k16, 1, 0) + d_qg * gam
            npq = _dot(n_p, q16, 0, 0)
            yield
            d_a = jnp.where(cr["strict"], -(_dot(d_bv, u, 1, 1) + _dot(d_kbg, w16, 1, 1)), 0.0)
            yield
            m_a = (d_a * dec).astype(BF16)
            d_kb = _dot(m_a, k16, 1, 0) + d_kbg * gam
            d_k = (_dot(m_a, cr["kb16"], 0, 0) + npq + d_kd * e + d_kb * b_col)
            yield
            d_v = d_bv * b_col
            d_beta_col = (jnp.sum(d_bv * v, axis=1, keepdims=True)
                          + jnp.sum(d_kb * k, axis=1, keepdims=True))
            gterm = d_a * cr["a_mat"] + d_p * p
            d_e = jnp.sum(d_kd * k, axis=1, keepdims=True) * e
            d_g_col = (jnp.sum(gterm, axis=1, keepdims=True)
                       + (jnp.sum(d_qg * q, axis=1, keepdims=True)
                          + jnp.sum(d_kbg * cr["kb"], axis=1, keepdims=True)) * gam
                       - d_e)
            d_g_last = jnp.sum(d_e, axis=0, keepdims=True) + d_gam_last * cr["gam_last"]
            d_g_row = (_col_to_row(d_g_col, eye) - jnp.sum(gterm, axis=0, keepdims=True)
                       + jnp.where(lane == A_CHUNK - 1, d_g_last, 0.0))
            dgc_ref[hh, seq_chunk, :] = d_g_row
            dbeta_ref[hh, seq_chunk, :] = _col_to_row(d_beta_col, eye)

            qh = cin["aq"] * cin["rq"]
            kh = cin["ak"] * cin["rk"]
            d_qh = d_q * (A_DK ** -0.5)
            d_aq = cin["rq"] * (d_qh - qh * jnp.sum(d_qh * qh, axis=1, keepdims=True))
            d_ak = cin["rk"] * (d_k - kh * jnp.sum(d_k * kh, axis=1, keepdims=True))
            d_y = jnp.concatenate([d_aq, d_ak, d_v], axis=1) * cin["slope"]
            shifted[hh, 0, 0:A_CHUNK, :] = d_y
            shifted[hh, 0, A_CHUNK:A_CHUNK + HALO, :] = dy_next[hh]
            shifted[hh, 1, 0:A_CHUNK + HALO, :] = cin["xw"]
            d_x = cw[A_CONV - 1:A_CONV, :] * d_y
            for j in range(1, A_CONV):
                d_x = d_x + cw[A_CONV - 1 - j:A_CONV - j, :] * shifted[hh, 0, j:j + A_CHUNK, :]
            for j in range(A_CONV):
                xs = shifted[hh, 1, HALO - j:HALO - j + A_CHUNK, :]
                dcw_ref[A_CONV - 1 - j:A_CONV - j, ccols] += jnp.sum(d_y * xs, axis=0, keepdims=True)
            dy_next[hh] = d_y[0:HALO, :]
            dx_ref[rows, lo:lo + A_CONV_COLS] = d_x.astype(BF16)
            dx_ref[rows, lo + A_CONV_COLS:lo + A_HEAD_COLS] = dz.astype(BF16)

        def chunk(ci, first):
            rows = pl.ds(0 if first else pl.multiple_of(ci * A_CHUNK, A_CHUNK), A_CHUNK)
            _round_robin([one_head(hh, ci, first, rows) for hh in range(hp)])

        lax.fori_loop(0, A_BLK_CHUNKS - 1, lambda i, c: (chunk(A_BLK_CHUNKS - 1 - i, False), c)[1], 0)
        chunk(0, True)

    rev = lambda j: nsb - 1 - j
    small = pl.BlockSpec((None, hp, n, A_CHUNK), lambda b, h, j: (b, h, 0, 0))
    wide = pl.BlockSpec((None, A_SEQ_BLK, hp * A_HEAD_COLS), lambda b, h, j: (b, rev(j), h))
    val = pl.BlockSpec((None, A_SEQ_BLK, hp * A_DV), lambda b, h, j: (b, rev(j), h))
    return pl.pallas_call(
        body, name="gdn_bwd", grid=(bn, A_HEADS // hp, nsb),
        in_specs=[wide,
                  pl.BlockSpec((None, None, HALO, hp * A_HEAD_COLS), lambda b, h, j: (b, rev(j), 0, h)),
                  pl.BlockSpec((A_CONV, hp * A_CONV_COLS), lambda b, h, j: (0, h)),
                  small, small,
                  pl.BlockSpec((1, A_DV), lambda b, h, j: (0, 0)),
                  val,
                  pl.BlockSpec((None, hp, A_BLK_CHUNKS, A_DK, A_DV), lambda b, h, j: (b, h, rev(j), 0, 0)),
                  pl.BlockSpec((None, hp, A_BLK_CHUNKS, A_CHUNK, A_CHUNK), lambda b, h, j: (b, h, rev(j), 0, 0)),
                  pl.BlockSpec((None, A_SEQ_BLK, hp * A_CONV_COLS), lambda b, h, j: (b, rev(j), h)),
                  val],
        out_specs=[wide, small, small,
                   pl.BlockSpec((None, A_CONV, hp * A_CONV_COLS), lambda b, h, j: (b, 0, h)),
                   pl.BlockSpec((None, hp, 8, A_DV), lambda b, h, j: (b, h, 0, 0))],
        out_shape=[jax.ShapeDtypeStruct((bn, s, A_HEADS * A_HEAD_COLS), BF16),
                   jax.ShapeDtypeStruct((bn, A_HEADS, n, A_CHUNK), F32),
                   jax.ShapeDtypeStruct((bn, A_HEADS, n, A_CHUNK), F32),
                   jax.ShapeDtypeStruct((bn, A_CONV, A_HEADS * A_CONV_COLS), F32),
                   jax.ShapeDtypeStruct((bn, A_HEADS, 8, A_DV), F32)],
        scratch_shapes=[pltpu.VMEM((hp, A_DK, A_DV), F32), pltpu.VMEM((hp, HALO, A_CONV_COLS), F32),
                        pltpu.VMEM((hp, 2, A_CHUNK + 2 * HALO, A_CONV_COLS), F32)],
        compiler_params=_params(("parallel", "parallel", "arbitrary"), VMEM_BIG),
    )(proj_hm, halo, cw_hm, beta, gc, norm_g, oraw, states, t_mats, conv_y, dog)


def _rope_tables(posf, inv_freq_row):
    t = posf.shape[0]
    tm = 512

    def body(p_ref, f_ref, c_ref, sa_ref, sb_ref):
        ang = p_ref[...] * f_ref[...]
        lane = lax.broadcasted_iota(jnp.int32, ang.shape, 1)
        half = ROPE_DIMS // 2
        c_ref[...] = jnp.where(lane < ROPE_DIMS, jnp.cos(ang), 1.0)
        sn = jnp.sin(ang)
        sa_ref[...] = jnp.where(lane < half, -sn, 0.0)
        sb_ref[...] = jnp.where((lane >= half) & (lane < ROPE_DIMS), sn, 0.0)

    row = pl.BlockSpec((tm, 128), lambda i: (i, 0))
    return pl.pallas_call(
        body, name="rope_tables", grid=(t // tm,),
        in_specs=[row, pl.BlockSpec((1, 128), lambda i: (0, 0))], out_specs=[row] * 3,
        out_shape=[jax.ShapeDtypeStruct((t, 128), F32)] * 3,
        compiler_params=_params(("parallel",)),
    )(posf, inv_freq_row)


def _qk_prep(proj, c, sa, sb, qg, kg, name, tm=512):
    t = proj.shape[0]

    def body(x_ref, c_ref, sa_ref, sb_ref, qg_ref, kg_ref, o_ref):
        cc, s1, s2 = c_ref[...], sa_ref[...], sb_ref[...]
        half = ROPE_DIMS // 2

        def one_head(lo, g):
            xv = x_ref[:, lo:lo + B_DH].astype(F32)
            ms = jnp.mean(xv * xv, axis=1, keepdims=True)
            yield
            xn = xv * lax.rsqrt(ms + EPS) * g
            r1, r2 = pltpu.roll(xn, 128 - half, 1), pltpu.roll(xn, half, 1)
            yield
            o_ref[:, lo:lo + B_DH] = (xn * cc + r1 * s1 + r2 * s2).astype(BF16)

        for which, g_ref in ((0, qg_ref), (1, kg_ref)):
            g = g_ref[...]
            _round_robin([one_head(which * B_W + h * B_DH, g) for h in range(B_HEADS)])
        o_ref[:, 2 * B_W:3 * B_W] = x_ref[:, 2 * B_W:3 * B_W]

    tab = pl.BlockSpec((tm, 128), lambda i: (i, 0))
    gain = pl.BlockSpec((1, B_DH), lambda i: (0, 0))
    return pl.pallas_call(
        body, name=name, grid=(t // tm,),
        in_specs=[pl.BlockSpec((tm, 3 * B_W), lambda i: (i, 0)), tab, tab, tab, gain, gain],
        out_specs=pl.BlockSpec((tm, 3 * B_W), lambda i: (i, 0)),
        out_shape=jax.ShapeDtypeStruct((t, 3 * B_W), BF16),
        compiler_params=_params(("parallel",), 40 * 1024 * 1024),
    )(proj, c, sa, sb, qg, kg)


def _qk_prep_bwd(proj, c, sa, sb, qg, kg, dq, dk, dv, dz, name, tm=512):
    t = proj.shape[0]
    out_w = 3 * B_W + (B_W if dz is not None else 0)

    def body(*refs):
        x_ref, c_ref, sa_ref, sb_ref, qg_ref, kg_ref, dq_ref, dk_ref, dv_ref = refs[:9]
        if dz is not None:
            dz_ref, o_ref, dgain_ref = refs[9:]
        else:
            o_ref, dgain_ref = refs[9:]
        i = pl.program_id(0)

        @pl.when(i == 0)
        def _():
            dgain_ref[...] = jnp.zeros_like(dgain_ref)

        cc, s1, s2 = c_ref[...], sa_ref[...], sb_ref[...]
        half = ROPE_DIMS // 2

        def one_head(which, h, g, d_ref, parts):
            lo = which * B_W + h * B_DH
            xv = x_ref[:, lo:lo + B_DH].astype(F32)
            d_out = d_ref[:, h * B_DH:(h + 1) * B_DH].astype(F32)
            ms = jnp.mean(xv * xv, axis=1, keepdims=True)
            r1, r2 = pltpu.roll(d_out * s1, half, 1), pltpu.roll(d_out * s2, 128 - half, 1)
            yield
            r = lax.rsqrt(ms + EPS)
            xh = xv * r
            d_xn = d_out * cc + r1 + r2
            parts.append(jnp.sum(d_xn * xh, axis=0, keepdims=True))
            d_xh = d_xn * g
            dot = jnp.mean(d_xh * xh, axis=1, keepdims=True)
            yield
            o_ref[:, lo:lo + B_DH] = (r * (d_xh - xh * dot)).astype(BF16)

        for which, g_ref, d_ref in ((0, qg_ref, dq_ref), (1, kg_ref, dk_ref)):
            parts = []
            _round_robin([one_head(which, h, g_ref[...], d_ref, parts) for h in range(B_HEADS)])
            acc = parts[0]
            for part in parts[1:]:
                acc = acc + part
            dgain_ref[which:which + 1, :] += acc
        o_ref[:, 2 * B_W:3 * B_W] = dv_ref[...]
        if dz is not None:
            o_ref[:, 3 * B_W:4 * B_W] = dz_ref[...]

    tab = pl.BlockSpec((tm, 128), lambda i: (i, 0))
    gain = pl.BlockSpec((1, B_DH), lambda i: (0, 0))
    grad = pl.BlockSpec((tm, B_W), lambda i: (i, 0))
    in_specs = [pl.BlockSpec((tm, 2 * B_W), lambda i: (i, 0)), tab, tab, tab, gain, gain, grad, grad, grad]
    args = [proj, c, sa, sb, qg, kg, dq, dk, dv]
    if dz is not None:
        in_specs.append(grad)
        args.append(dz)
    return pl.pallas_call(
        body, name=name, grid=(t // tm,), in_specs=in_specs,
        out_specs=[pl.BlockSpec((tm, out_w), lambda i: (i, 0)), pl.BlockSpec((8, B_DH), lambda i: (0, 0))],
        out_shape=[jax.ShapeDtypeStruct((t, out_w), BF16), jax.ShapeDtypeStruct((8, B_DH), F32)],
        compiler_params=_params(("arbitrary",), 40 * 1024 * 1024),
    )(*args)


def _attn_masks():
    qi = lax.broadcasted_iota(jnp.int32, (B_BLK, 2 * B_BLK), 0)
    kj = lax.broadcasted_iota(jnp.int32, (B_BLK, 2 * B_BLK), 1)
    two = (kj >= qi) & (kj <= qi + B_BLK)
    q1 = lax.broadcasted_iota(jnp.int32, (B_BLK, B_BLK), 0)
    k1 = lax.broadcasted_iota(jnp.int32, (B_BLK, B_BLK), 1)
    return k1 <= q1, two


def _lane_pick(ref_rows, h):
    lane = lax.broadcasted_iota(jnp.int32, ref_rows.shape, 1)
    return jnp.sum(jnp.where(lane == h, ref_rows, 0.0), axis=1, keepdims=True)


B_ROWS = 4096


def _attn_schedule(nb, sb, block):
    way = 16

    def run(items):
        for at in range(0, len(items), way):
            _round_robin([block(*it) for it in items[at:at + way]])

    run([(si, 0, True) for si in range(sb)])
    if nb == 1:
        return
    per = max(1, way // sb)
    lead = 1 + (nb - 1) % per
    if lead > 1:
        run([(si, i, False) for i in range(1, lead) for si in range(sb)])

    def step(it, carry):
        run([(si, lead + it * per + u, False) for u in range(per) for si in range(sb)])
        return carry

    lax.fori_loop(0, (nb - lead) // per, step, 0)


def _attn_rows(i, first):
    if first:
        return pl.ds(0, B_BLK), pl.ds(0, B_BLK)
    rows = pl.ds(pl.multiple_of(i * B_BLK, B_BLK), B_BLK)
    return rows, pl.ds(pl.multiple_of((i - 1) * B_BLK, B_BLK), 2 * B_BLK)


def _attn_fwd(qkv, name):
    ns, ln, _ = qkv.shape
    nb = ln // B_BLK
    sb = min(B_ROWS // ln, ns)
    scale = B_DH ** -0.5

    def body(q_ref, k_ref, v_ref, o_ref, lse_ref):
        h = pl.program_id(1)
        mask1, mask2 = _attn_masks()
        lane = lax.broadcasted_iota(jnp.int32, (B_BLK, B_HEADS), 1)

        @pl.when(h == 0)
        def _():
            lse_ref[...] = jnp.zeros_like(lse_ref)

        def block(si, i, first):
            rows, win = _attn_rows(i, first)
            mask = mask1 if first else mask2
            sc = jnp.where(mask, _dot(q_ref[si, rows, :], k_ref[si, win, :], 1, 1) * scale, -1e30)
            yield
            m = jnp.max(sc, axis=1, keepdims=True)
            p = jnp.exp(sc - m)
            l = jnp.sum(p, axis=1, keepdims=True)
            pv = _dot(p, v_ref[si, win, :], 1, 0)
            yield
            o_ref[si, rows, :] = (pv / l).astype(BF16)
            lse_ref[si, rows, :] = jnp.where(lane == h, m + jnp.log(l), lse_ref[si, rows, :])

        _attn_schedule(nb, sb, block)

    head = lambda off: pl.BlockSpec((sb, ln, B_DH), lambda s, h: (s, 0, off + h))
    return pl.pallas_call(
        body, name=name, grid=(ns // sb, B_HEADS),
        in_specs=[head(0), head(B_HEADS), head(2 * B_HEADS)],
        out_specs=[head(0), pl.BlockSpec((sb, ln, B_HEADS), lambda s, h: (s, 0, 0))],
        out_shape=[jax.ShapeDtypeStruct((ns, ln, B_W), BF16), jax.ShapeDtypeStruct((ns, ln, B_HEADS), F32)],
        compiler_params=_params(("parallel", "arbitrary")),
    )(qkv, qkv, qkv)


def _attn_bwd(qkv, d_o, lse_joint, delta, name):
    ns, ln, _ = qkv.shape
    nb = ln // B_BLK
    sb = min(B_ROWS // ln, ns)
    scale = B_DH ** -0.5

    def body(q_ref, k_ref, v_ref, do_ref, lj_ref, dl_ref, dq_ref, dk_out, dv_out, dk_ref, dv_ref):
        h = pl.program_id(1)
        mask1, mask2 = _attn_masks()
        dk_ref[...] = jnp.zeros_like(dk_ref)
        dv_ref[...] = jnp.zeros_like(dv_ref)

        def block(si, i, first):
            rows, win = _attn_rows(i, first)
            mask = mask1 if first else mask2
            q = q_ref[si, rows, :]
            d_out = do_ref[si, rows, :]
            l_col = _lane_pick(lj_ref[si, rows, :], h)
            d_col = _lane_pick(dl_ref[si, rows, :], h)
            sc = _dot(q, k_ref[si, win, :], 1, 1) * scale
            d_p = _dot(d_out, v_ref[si, win, :], 1, 1)
            yield
            p = jnp.exp(jnp.where(mask, sc - l_col, -1e30))
            d_s = p * (d_p - d_col) * scale
            d_q = _dot(d_s, k_ref[si, win, :], 1, 0)
            d_k = _dot(d_s, q, 0, 0)
            d_v = _dot(p, d_out, 0, 0)
            yield
            dq_ref[si, rows, :] = d_q.astype(BF16)
            dk_ref[si, win, :] += d_k
            dv_ref[si, win, :] += d_v

        _attn_schedule(nb, sb, block)
        dk_out[...] = dk_ref[...].astype(BF16)
        dv_out[...] = dv_ref[...].astype(BF16)

    head = lambda off: pl.BlockSpec((sb, ln, B_DH), lambda s, h: (s, 0, off + h))
    small = pl.BlockSpec((sb, ln, B_HEADS), lambda s, h: (s, 0, 0))
    return pl.pallas_call(
        body, name=name, grid=(ns // sb, B_HEADS),
        in_specs=[head(0), head(B_HEADS), head(2 * B_HEADS), head(0), small, small],
        out_specs=[head(0)] * 3,
        out_shape=[jax.ShapeDtypeStruct((ns, ln, B_W), BF16)] * 3,
        scratch_shapes=[pltpu.VMEM((sb, ln, B_DH), F32)] * 2,
        compiler_params=_params(("parallel", "parallel")),
    )(qkv, qkv, qkv, d_o, lse_joint, delta)


def _merge_weights(lse_refs):
    ls = [r[...] for r in lse_refs]
    m = jnp.maximum(jnp.maximum(ls[0], ls[1]), ls[2])
    es = [jnp.exp(l - m) for l in ls]
    tot = es[0] + es[1] + es[2]
    return [e / tot for e in es], m + jnp.log(tot)


def _merge_fwd(outs, lses, proj0, tm=512):
    t = outs[0].shape[0]

    def body(o0, o1, o2, l0, l1, l2, z_ref, og_ref):
        wts, _ = _merge_weights((l0, l1, l2))

        def one_head(h):
            cols = slice(h * B_DH, (h + 1) * B_DH)
            w0, w1, w2 = (jnp.broadcast_to(w[:, h:h + 1], (tm, B_DH)) for w in wts)
            yield
            o = w0 * o0[:, cols] + w1 * o1[:, cols] + w2 * o2[:, cols]
            og_ref[:, cols] = (o * _silu(z_ref[:, cols].astype(F32))).astype(BF16)

        _round_robin([one_head(h) for h in range(B_HEADS)])

    wide = pl.BlockSpec((tm, B_W), lambda i: (i, 0))
    small = pl.BlockSpec((tm, B_HEADS), lambda i: (i, 0))
    return pl.pallas_call(
        body, name="merge_fwd", grid=(t // tm,),
        in_specs=[wide] * 3 + [small] * 3 + [pl.BlockSpec((tm, B_W), lambda i: (i, 3))],
        out_specs=wide, out_shape=jax.ShapeDtypeStruct((t, B_W), BF16),
        compiler_params=_params(("parallel",)),
    )(*outs, *lses, proj0)


def _merge_bwd(outs, lses, proj0, d_og, tm=512):
    t = outs[0].shape[0]

    def body(o0, o1, o2, l0, l1, l2, z_ref, dog_ref, do_ref, lj_ref, dl_ref, dz_ref):
        wts, lj = _merge_weights((l0, l1, l2))
        lj_ref[...] = lj
        lane = lax.broadcasted_iota(jnp.int32, (tm, B_HEADS), 1)
        sums = [None] * B_HEADS

        def one_head(h):
            cols = slice(h * B_DH, (h + 1) * B_DH)
            w0, w1, w2 = (jnp.broadcast_to(w[:, h:h + 1], (tm, B_DH)) for w in wts)
            yield
            o = w0 * o0[:, cols] + w1 * o1[:, cols] + w2 * o2[:, cols]
            z = z_ref[:, cols].astype(F32)
            d_g = dog_ref[:, cols].astype(F32)
            gate, gate_slope = _silu_and_slope(z)
            d_out = d_g * gate
            dz_ref[:, cols] = (d_g * o * gate_slope).astype(BF16)
            do_ref[:, cols] = d_out.astype(BF16)
            sums[h] = jnp.sum(d_out * o, axis=1, keepdims=True)
            yield

        _round_robin([one_head(h) for h in range(B_HEADS)])
        delta = jnp.zeros((tm, B_HEADS), F32)
        for h in range(B_HEADS):
            delta = jnp.where(lane == h, sums[h], delta)
        dl_ref[...] = delta

    wide = pl.BlockSpec((tm, B_W), lambda i: (i, 0))
    small = pl.BlockSpec((tm, B_HEADS), lambda i: (i, 0))
    return pl.pallas_call(
        body, name="merge_bwd", grid=(t // tm,),
        in_specs=[wide] * 3 + [small] * 3 + [pl.BlockSpec((tm, B_W), lambda i: (i, 3)), wide],
        out_specs=[wide, small, small, wide],
        out_shape=[jax.ShapeDtypeStruct((t, B_W), BF16), jax.ShapeDtypeStruct((t, B_HEADS), F32),
                   jax.ShapeDtypeStruct((t, B_HEADS), F32), jax.ShapeDtypeStruct((t, B_W), BF16)],
        compiler_params=_params(("parallel",)),
    )(*outs, *lses, proj0, d_og)


def _adamw(w, g, m, v, name):
    r, c = w.shape
    tr = r
    for cand in (256, 128, 64, 32, 16, 8):
        if r % cand == 0:
            tr = cand
            break

    def body(w_ref, g_ref, m_ref, v_ref, d_ref, nm_ref, nv_ref):
        gv = g_ref[...]
        nm = ADAM_B1 * m_ref[...] + (1.0 - ADAM_B1) * gv
        nv = ADAM_B2 * v_ref[...] + (1.0 - ADAM_B2) * (gv * gv)
        m_hat = nm / (1.0 - ADAM_B1 ** ADAM_STEP)
        v_hat = nv / (1.0 - ADAM_B2 ** ADAM_STEP)
        d_ref[...] = -ADAM_LR * (m_hat / (jnp.sqrt(v_hat) + ADAM_EPS) + ADAM_WD * w_ref[...])
        nm_ref[...] = nm
        nv_ref[...] = nv

    blk = pl.BlockSpec((tr, c), lambda i: (i, 0))
    return pl.pallas_call(
        body, name=name, grid=(r // tr,), in_specs=[blk] * 4, out_specs=[blk] * 3,
        out_shape=[jax.ShapeDtypeStruct((r, c), F32)] * 3,
        compiler_params=_params(("parallel",)),
    )(w, g, m, v)


def _adam_update(w, gv, m, v):
    nm = ADAM_B1 * m + (1.0 - ADAM_B1) * gv
    nv = ADAM_B2 * v + (1.0 - ADAM_B2) * (gv * gv)
    m_hat = nm / (1.0 - ADAM_B1 ** ADAM_STEP)
    v_hat = nv / (1.0 - ADAM_B2 ** ADAM_STEP)
    return -ADAM_LR * (m_hat / (jnp.sqrt(v_hat) + ADAM_EPS) + ADAM_WD * w), nm, nv


def _adamw_shard(w, mine, theirs, m, v, half_index, name, tr=128):
    _, r, c = w.shape
    nhb = (r // 2) // tr

    def body(c_ref, w_ref, mine_ref, theirs_ref, m_ref, v_ref, g_ref, d_ref, nm_ref, nv_ref):
        is_mine = (pl.program_id(0) // nhb) == c_ref[0]
        gv = jnp.where(is_mine, mine_ref[...], theirs_ref[...])
        d, nm, nv = _adam_update(w_ref[...], gv, m_ref[...], v_ref[...])
        g_ref[...] = gv
        d_ref[...] = d
        nm_ref[...] = nm
        nv_ref[...] = nv

    full = pl.BlockSpec((None, tr, c), lambda i, cc: (0, i, 0))
    half = pl.BlockSpec((tr, c), lambda i, cc: (i % nhb, 0))
    return pl.pallas_call(
        body, name=name,
        grid_spec=pltpu.PrefetchScalarGridSpec(
            num_scalar_prefetch=1, grid=(2 * nhb,),
            in_specs=[full, half, half, full, full], out_specs=[full] * 4),
        out_shape=[jax.ShapeDtypeStruct(w.shape, F32)] * 4,
        compiler_params=_params(("parallel",), 40 * 1024 * 1024),
    )(half_index, w, mine, theirs, m, v)


def _adamw_shard_cols(w, mine, theirs, m, v, half_index, name, steps=20):
    c, _, r = w.shape
    tc = c // steps
    assert tc * steps == c

    def body(c_ref, w_ref, mine_ref, theirs_ref, m_ref, v_ref, g_ref, d_ref, nm_ref, nv_ref):
        first = jnp.where(c_ref[0] == 0, mine_ref[...], theirs_ref[...])
        second = jnp.where(c_ref[0] == 0, theirs_ref[...], mine_ref[...])
        for lo, gv in ((0, first), (r // 2, second)):
            cols = slice(lo, lo + r // 2)
            d, nm, nv = _adam_update(w_ref[:, :, cols], gv, m_ref[:, :, cols], v_ref[:, :, cols])
            g_ref[:, :, cols] = gv
            d_ref[:, :, cols] = d
            nm_ref[:, :, cols] = nm
            nv_ref[:, :, cols] = nv

    full = pl.BlockSpec((tc, 1, r), lambda i, cc: (i, 0, 0))
    half = pl.BlockSpec((tc, 1, r // 2), lambda i, cc: (i, 0, 0))
    return pl.pallas_call(
        body, name=name,
        grid_spec=pltpu.PrefetchScalarGridSpec(
            num_scalar_prefetch=1, grid=(steps,),
            in_specs=[full, half, half, full, full], out_specs=[full] * 4),
        out_shape=[jax.ShapeDtypeStruct(w.shape, F32)] * 4,
        compiler_params=_params(("parallel",), 40 * 1024 * 1024),
    )(half_index, w, mine, theirs, m, v)


def _pair_sum(own, other, half_index, name, tr=256):
    _, r, c = own.shape
    rh = r // 2
    tr = min(tr, rh)
    nrb = rh // tr

    def body(c_ref, own_ref, oth_ref, out_ref):
        out_ref[...] = (own_ref[...] + oth_ref[...].astype(F32)).astype(BF16)

    return pl.pallas_call(
        body, name=name,
        grid_spec=pltpu.PrefetchScalarGridSpec(
            num_scalar_prefetch=1, grid=(N_CHIPS, nrb),
            in_specs=[pl.BlockSpec((None, tr, c), lambda k, i, cc: (k, cc[0] * nrb + i, 0)),
                      pl.BlockSpec((None, tr, c), lambda k, i, cc: (k, i, 0))],
            out_specs=pl.BlockSpec((None, tr, c), lambda k, i, cc: (k, i, 0))),
        out_shape=jax.ShapeDtypeStruct((N_CHIPS, rh, c), BF16),
        compiler_params=_params(("parallel", "parallel")),
    )(half_index, own, other)


def _chip_sum(sums, others, chip_index, name, tr=256):
    _, r, c = sums.shape
    tr = min(tr, r)

    def body(k_ref, own_ref, oth_ref, out_ref):
        acc = own_ref[...].astype(F32)
        for j in range(N_CHIPS - 1):
            acc = acc + oth_ref[j].astype(F32)
        out_ref[...] = acc

    return pl.pallas_call(
        body, name=name,
        grid_spec=pltpu.PrefetchScalarGridSpec(
            num_scalar_prefetch=1, grid=(r // tr,),
            in_specs=[pl.BlockSpec((None, tr, c), lambda i, kk: (kk[0], i, 0)),
                      pl.BlockSpec((N_CHIPS - 1, tr, c), lambda i, kk: (0, i, 0))],
            out_specs=pl.BlockSpec((tr, c), lambda i, kk: (i, 0))),
        out_shape=jax.ShapeDtypeStruct((r, c), F32),
        compiler_params=_params(("parallel",)),
    )(chip_index, sums, others)


HBM = pl.BlockSpec(memory_space=pltpu.HBM)


def _place():
    x, y, c = lax.axis_index("x"), lax.axis_index("y"), lax.axis_index("c")
    chips = [(1 - x, y), (x, 1 - y), (1 - x, 1 - y)]
    return x, y, c, chips


def _sibling_forward(land):
    def body(in_ref, out_ref, send, recv):
        x, y, c, chips = _place()
        rh = out_ref.shape[1] // 2
        cps = []
        for j, (px, py) in enumerate(chips):
            slot = out_ref.at[2 * px + py, pl.ds(c * rh, rh)]
            cp = pltpu.make_async_remote_copy(
                src_ref=slot, dst_ref=slot, send_sem=send.at[j], recv_sem=recv.at[j],
                device_id=(x, y, 1 - c), device_id_type=MESH)
            cp.start()
            cps.append(cp)
        for j, (px, py) in enumerate(chips):
            slot = out_ref.at[2 * px + py, pl.ds((1 - c) * rh, rh)]
            pltpu.make_async_remote_copy(
                src_ref=slot, dst_ref=slot, send_sem=send.at[j], recv_sem=recv.at[j],
                device_id=(x, y, 1 - c), device_id_type=MESH).wait_recv()
        for cp in cps:
            cp.wait_send()

    return pl.pallas_call(
        body, name="first_weights_sibling_forward", in_specs=[HBM], out_specs=HBM,
        out_shape=jax.ShapeDtypeStruct(land.shape, land.dtype), input_output_aliases={0: 0},
        scratch_shapes=[pltpu.SemaphoreType.DMA((3,)), pltpu.SemaphoreType.DMA((3,))],
    )(land)


def _sibling_swap(halves, name):
    na = len(halves)

    def body(*refs):
        ins, outs = refs[:na], refs[na:2 * na]
        send, recv = refs[2 * na:]
        x, y, c, _ = _place()
        cps = []
        for i in range(na):
            cp = pltpu.make_async_remote_copy(
                src_ref=ins[i], dst_ref=outs[i], send_sem=send.at[i], recv_sem=recv.at[i],
                device_id=(x, y, 1 - c), device_id_type=MESH)
            cp.start()
            cps.append(cp)
        for cp in cps:
            cp.wait()

    out_shape = [jax.ShapeDtypeStruct(h.shape, h.dtype) for h in halves]
    return pl.pallas_call(
        body, name=name, in_specs=[HBM] * na, out_specs=[HBM] * na, out_shape=out_shape,
        scratch_shapes=[pltpu.SemaphoreType.DMA((na,)), pltpu.SemaphoreType.DMA((na,))],
    )(*halves)


SEM = pl.BlockSpec(memory_space=pltpu.SEMAPHORE)
ANY = pl.BlockSpec(memory_space=pl.ANY)
EFFECT = pltpu.SideEffectType.DATAFLOW_SIDE_EFFECTING


def _split_copy_start(name, plan, srcs, lands, after):
    ns, nl = len(srcs), len(lands)

    def body(*refs):
        src_refs, land_refs = refs[:ns], refs[ns:ns + nl]
        send, recv = refs[ns + nl + 1], refs[ns + nl + 2]
        token = refs[-1]
        outgoing, _ = plan(src_refs, land_refs)
        for src, dst, dev, si, ri in outgoing:
            pltpu.make_async_remote_copy(src_ref=src, dst_ref=dst, send_sem=send.at[si], recv_sem=recv.at[ri],
                                         device_id=dev, device_id_type=MESH).start()
        token[...] = jnp.zeros_like(token)

    n_out, n_in = plan.counts
    thru = [pltpu.HBM(a.shape, a.dtype) for a in list(srcs) + list(lands)]
    res = pl.pallas_call(
        body, name=name,
        out_shape=[pltpu.SemaphoreType.DMA((n_out,)), pltpu.SemaphoreType.DMA((n_in,))] + thru
        + [jax.ShapeDtypeStruct((8, 128), F32)],
        in_specs=[HBM] * (ns + nl) + [ANY],
        out_specs=[SEM, SEM] + [HBM] * (ns + nl) + [pl.BlockSpec(memory_space=pltpu.VMEM)],
        input_output_aliases={i: 2 + i for i in range(ns + nl)},
        compiler_params=pltpu.CompilerParams(has_side_effects=EFFECT),
    )(*[pltpu.with_memory_space_constraint(a, pltpu.HBM) for a in list(srcs) + list(lands)], after)
    return res[0], res[1], res[2:2 + ns], res[2 + ns:2 + ns + nl], res[-1]


def _split_copy_wait(name, plan, send, recv, srcs, lands, after):
    ns, nl = len(srcs), len(lands)
    after = list(after) if isinstance(after, (list, tuple)) else [after]

    def body(*refs):
        src_refs, land_refs = refs[:ns], refs[ns:ns + nl]
        send_ref, recv_ref = refs[ns + nl], refs[ns + nl + 1]
        outgoing, arrivals = plan(src_refs, land_refs)
        for src, dst, dev, si, ri in outgoing:
            pltpu.make_async_remote_copy(src_ref=src, dst_ref=dst, send_sem=send_ref.at[si], recv_sem=recv_ref.at[ri],
                                         device_id=dev, device_id_type=MESH).wait_send()
        for view, ri in arrivals:
            pltpu.make_async_remote_copy(src_ref=view, dst_ref=view, send_sem=send_ref.at[0], recv_sem=recv_ref.at[ri],
                                         device_id=_place()[:3], device_id_type=MESH).wait_recv()

    thru = [pltpu.HBM(a.shape, a.dtype) for a in list(srcs) + list(lands)]
    res = pl.pallas_call(
        body, name=name, out_shape=thru,
        in_specs=[HBM] * (ns + nl) + [SEM, SEM] + [ANY] * len(after), out_specs=[HBM] * (ns + nl),
        input_output_aliases={i: i for i in range(ns + nl)},
        compiler_params=pltpu.CompilerParams(has_side_effects=EFFECT),
    )(*srcs, *lands, send, recv, *after)
    return res[:ns], res[ns:]


def _gather_plan(n_arrays):
    def plan(src_refs, land_refs):
        x, y, c, chips = _place()
        me = 2 * x + y
        outgoing, arrivals = [], []
        for i in range(n_arrays):
            rh = src_refs[i].shape[0] // 2
            mine = pl.ds(c * rh, rh)
            for j, (px, py) in enumerate(chips):
                for delta in range(2):
                    tc = c ^ delta
                    outgoing.append((src_refs[i].at[mine], land_refs[i].at[me, mine], (px, py, tc),
                                     6 * i + 2 * j + delta, 6 * i + 2 * j + delta))
                    theirs = pl.ds(tc * rh, rh)
                    arrivals.append((land_refs[i].at[2 * px + py, theirs], 6 * i + 2 * j + delta))
        return outgoing, arrivals

    plan.counts = (6 * n_arrays, 6 * n_arrays)
    return plan


def _first_gather_plan():
    def plan(src_refs, land_refs):
        x, y, c, chips = _place()
        me = 2 * x + y
        rh = src_refs[0].shape[0] // 2
        mine = pl.ds(c * rh, rh)
        outgoing, arrivals = [], []
        for j, (px, py) in enumerate(chips):
            outgoing.append((src_refs[0].at[mine], land_refs[0].at[me, mine], (px, py, c), j, j))
            arrivals.append((land_refs[0].at[2 * px + py, mine], j))
            outgoing.append((src_refs[1], land_refs[1].at[me], (px, py, c), 3 + j, 3 + j))
            arrivals.append((land_refs[1].at[2 * px + py], 3 + j))
        return outgoing, arrivals

    plan.counts = (6, 6)
    return plan


def _exchange_plan(n_arrays):
    def plan(src_refs, land_refs):
        x, y, c, chips = _place()
        outgoing, arrivals = [], []
        for i in range(n_arrays):
            for j, (px, py) in enumerate(chips):
                outgoing.append((src_refs[i].at[2 * px + py], land_refs[i].at[j], (px, py, c), 3 * i + j, 3 * i + j))
                arrivals.append((land_refs[i].at[j], 3 * i + j))
        return outgoing, arrivals

    plan.counts = (3 * n_arrays, 3 * n_arrays)
    return plan


def _small_allreduce(vec):
    r, cdim = vec.shape
    n_dev = 8

    def body(v_ref, out_ref, buf, send, recv):
        x, y, c, _ = _place()
        me = 4 * x + 2 * y + c
        buf[me] = v_ref[...]
        cps = []
        for k in range(1, n_dev):
            dx, dy, dc = (k >> 2) & 1, (k >> 1) & 1, k & 1
            peer = (x ^ dx, y ^ dy, c ^ dc)
            cp = pltpu.make_async_remote_copy(
                src_ref=v_ref, dst_ref=buf.at[me], send_sem=send.at[k - 1], recv_sem=recv.at[k - 1],
                device_id=peer, device_id_type=MESH)
            cp.start()
            cps.append(cp)
        for k in range(1, n_dev):
            dx, dy, dc = (k >> 2) & 1, (k >> 1) & 1, k & 1
            src = 4 * (x ^ dx) + 2 * (y ^ dy) + (c ^ dc)
            slot = buf.at[src]
            pltpu.make_async_remote_copy(
                src_ref=slot, dst_ref=slot, send_sem=send.at[k - 1], recv_sem=recv.at[k - 1],
                device_id=(x ^ dx, y ^ dy, c ^ dc), device_id_type=MESH).wait_recv()
        for cp in cps:
            cp.wait_send()
        acc = buf[0]
        for k in range(1, n_dev):
            acc = acc + buf[k]
        out_ref[...] = acc

    vm = pl.BlockSpec(memory_space=pltpu.VMEM)
    return pl.pallas_call(
        body, name="small_allreduce", in_specs=[vm], out_specs=vm,
        out_shape=jax.ShapeDtypeStruct((r, cdim), F32),
        scratch_shapes=[pltpu.VMEM((n_dev, r, cdim), F32), pltpu.SemaphoreType.DMA((n_dev - 1,)),
                        pltpu.SemaphoreType.DMA((n_dev - 1,))],
    )(vec)


def _a_cols_to_head_major(w):
    lead = w.shape[:-1]
    q = w[..., :A_QK].reshape(lead + (A_HEADS, A_DK))
    k = w[..., A_QK:2 * A_QK].reshape(lead + (A_HEADS, A_DK))
    v = w[..., 2 * A_QK:2 * A_QK + A_VW].reshape(lead + (A_HEADS, A_DV))
    z = w[..., 2 * A_QK + A_VW:].reshape(lead + (A_HEADS, A_DV))
    return jnp.concatenate([q, k, v, z], axis=-1).reshape(lead + (A_HEADS * A_HEAD_COLS,))


def _a_cols_from_head_major(w):
    lead = w.shape[:-1]
    w = w.reshape(lead + (A_HEADS, A_HEAD_COLS))
    parts = [w[..., :A_DK], w[..., A_DK:2 * A_DK], w[..., 2 * A_DK:2 * A_DK + A_DV], w[..., 2 * A_DK + A_DV:]]
    return jnp.concatenate([p.reshape(lead + (-1,)) for p in parts], axis=-1)


def _conv_cols_to_head_major(w):
    lead = w.shape[:-1]
    q = w[..., :A_QK].reshape(lead + (A_HEADS, A_DK))
    k = w[..., A_QK:2 * A_QK].reshape(lead + (A_HEADS, A_DK))
    v = w[..., 2 * A_QK:].reshape(lead + (A_HEADS, A_DV))
    return jnp.concatenate([q, k, v], axis=-1).reshape(lead + (A_HEADS * A_CONV_COLS,))


def _conv_cols_from_head_major(w):
    lead = w.shape[:-1]
    w = w.reshape(lead + (A_HEADS, A_CONV_COLS))
    parts = [w[..., :A_DK], w[..., A_DK:2 * A_DK], w[..., 2 * A_DK:]]
    return jnp.concatenate([p.reshape(lead + (-1,)) for p in parts], axis=-1)


def _to_stream(a, bn, d):
    rest = a.shape[1:]
    s = a.shape[0] // bn
    a = a.reshape((bn, s // d, d) + rest)
    a = jnp.swapaxes(a, 1, 2)
    return a.reshape((bn * d, s // d) + rest)


def _from_stream(a, bn, d):
    rest = a.shape[2:]
    ln = a.shape[1]
    a = a.reshape((bn, d, ln) + rest)
    a = jnp.swapaxes(a, 1, 2)
    return a.reshape((bn * ln * d,) + rest)


B_SUB = 512
B_SHARD_BLOCKS = (3 * B_GROUPS * B_W + B_W) // N_CHIPS // B_SUB


def _b_block(gi, jj):
    nb = (B_GROUPS * (jj // 2) + gi) * 2 + jj % 2
    return nb // B_SHARD_BLOCKS, nb % B_SHARD_BLOCKS


def _shard_major(g, ncols):
    r = g.shape[0]
    return jnp.swapaxes(g.reshape(r, N_CHIPS, ncols), 0, 1)


def _pack_rows(items):
    rows, offs = [], []
    at = 0
    for a in items:
        flat = a.reshape(-1).astype(F32)
        nr = -(-flat.shape[0] // 1024) * 8
        flat = jnp.pad(flat, (0, nr * 128 - flat.shape[0]))
        rows.append(flat.reshape(nr, 128))
        offs.append((at, nr, a.shape))
        at += nr
    return jnp.concatenate(rows, axis=0), offs


def _unpack_rows(packed, offs):
    out = []
    for at, nr, shape in offs:
        size = int(np.prod(shape)) if len(shape) else 1
        out.append(packed[at:at + nr].reshape(-1)[:size].reshape(shape))
    return out


def _local_step(x, positions, loss_target, norm_g, a_log, a_dt_bias, a_norm_g, b_q_norm_g, b_k_norm_g,
                start_token, first_weights, late_weights, b_grads_ready, a_grads_ready):
    bn, s, d = x.shape
    t = bn * s
    n_chunks = s // A_CHUNK
    x0 = x.reshape(t, d)
    h0 = _rms_fwd(x0, norm_g[0:1] + start_token, "rms0_fwd")
    inv_freq = ROPE_THETA ** (-jnp.arange(0, ROPE_DIMS, 2, dtype=F32) / ROPE_DIMS)
    freq_row = jnp.concatenate([inv_freq, inv_freq, jnp.zeros((128 - ROPE_DIMS,), F32)]).reshape(1, 128)
    posf = jnp.broadcast_to(positions.astype(F32).reshape(t, 1), (t, 128)) + start_token
    tabs = _rope_tables(posf, freq_row)
    tabs_s = [tabs if dil == 1 else [_to_stream(tb, bn, dil).reshape(t, 128) for tb in tabs] for dil in B_DIL]
    wa_in, conv_w, late_token = first_weights([h0] + [tb for ts in tabs_s for tb in ts])
    wa_main = _a_cols_to_head_major(wa_in[:, :A_MAIN])
    wa_tail = jnp.pad(wa_in[:, A_MAIN:], ((0, 0), (0, 128 - 2 * A_HEADS))) + late_token.astype(BF16)
    cw_hm = _conv_cols_to_head_major(conv_w)

    proj_a = _matmul(h0, wa_main, "nn", F32, "a_in_main", tm=2048)
    tail_a = _matmul(h0, wa_tail, "nn", F32, "a_in_tail")
    tail_t = jnp.swapaxes(tail_a[:, :2 * A_HEADS].reshape(bn, s, 2 * A_HEADS), 1, 2)
    tail_t = tail_t.reshape(bn, 2 * A_HEADS, n_chunks, A_CHUNK)
    beta, gc = _gdn_prep(tail_t, a_log[0], a_dt_bias[0])
    proj_a3 = proj_a.reshape(bn, s, A_MAIN)
    og_a, oraw_a, states, t_mats, conv_y = _gdn_fwd(proj_a3, cw_hm, beta, gc, a_norm_g)
    wa_out, wb_in, wb_out = late_weights(og_a)
    b_cols = [4 * B_W] + [3 * B_W] * (B_GROUPS - 1)
    x1, h1 = _out_proj(og_a.reshape(t, A_VW), wa_out, x0, "a_out", norm_g=norm_g[1:2])

    h1_s, proj_b, qkv_b, o_b, lse_b = [], [], [], [], []
    for gi, dil in enumerate(B_DIL):
        hs = h1 if dil == 1 else _to_stream(h1, bn, dil).reshape(t, d)
        ts = tabs_s[gi]
        pj = _matmul(hs, wb_in, "nn", BF16, f"b_in_g{gi}", tm=4096, tn=B_SUB, n=b_cols[gi], b_spec=pl.BlockSpec(
            (None, d, B_SUB), lambda i, j, kk, gi=gi: (_b_block(gi, j)[0], kk, _b_block(gi, j)[1])))
        qkv = _qk_prep(pj, *ts, b_q_norm_g[0, gi:gi + 1], b_k_norm_g[0, gi:gi + 1], f"qk_prep_g{gi}")
        o_s, lse_s = _attn_fwd(qkv.reshape(bn * dil, s // dil, 3 * B_W), f"attn_fwd_g{gi}")
        h1_s.append(hs), proj_b.append(pj), qkv_b.append(qkv)
        o_b.append(o_s.reshape(t, B_W) if dil == 1 else _from_stream(o_s, bn, dil))
        lse_b.append(lse_s.reshape(t, B_HEADS) if dil == 1 else _from_stream(lse_s, bn, dil))
    og_b = _merge_fwd(o_b, lse_b, proj_b[0])
    d_x2, loss_parts = _out_proj(og_b, wb_out, x1, "b_out_loss", target=loss_target.reshape(t, d))
    loss_local = jnp.sum(loss_parts)

    d_x2b = d_x2.astype(BF16)
    g_wb_out = _matmul(og_b, d_x2b, "tn", F32, "b_out_dw")
    d_og_b = _matmul(d_x2b, wb_out, "nt", BF16, "b_out_dx")
    d_o, lse_joint, delta, d_z = _merge_bwd(o_b, lse_b, proj_b[0], d_og_b)
    d_h1, g_qn, g_kn = [], [], []
    g_wb_in = lax.empty(wb_in.shape, F32)
    for gi, dil in enumerate(B_DIL):
        if dil == 1:
            do_s, lj_s, dl_s = d_o, lse_joint, delta
        else:
            do_s, lj_s, dl_s = (_to_stream(a, bn, dil).reshape(t, -1) for a in (d_o, lse_joint, delta))
        ns, ln = bn * dil, s // dil
        dq, dk, dv = _attn_bwd(qkv_b[gi].reshape(ns, ln, 3 * B_W), do_s.reshape(ns, ln, B_W),
                               lj_s.reshape(ns, ln, B_HEADS), dl_s.reshape(ns, ln, B_HEADS), f"attn_bwd_g{gi}")
        d_pj, d_gain = _qk_prep_bwd(proj_b[gi], *tabs_s[gi], b_q_norm_g[0, gi:gi + 1], b_k_norm_g[0, gi:gi + 1],
                                    dq.reshape(t, B_W), dk.reshape(t, B_W), dv.reshape(t, B_W),
                                    d_z if gi == 0 else None, f"qk_prep_bwd_g{gi}")
        g_wb_in = _matmul(h1_s[gi], d_pj, "tn", F32, f"b_in_dw_g{gi}", tn=B_SUB, tk=DW_K, into=(g_wb_in, pl.BlockSpec(
            (None, d, B_SUB), lambda i, j, kk, gi=gi: (_b_block(gi, j)[0], i, _b_block(gi, j)[1]))))
        dh = _b_in_dx(d_pj, wb_in, gi, f"b_in_dx_g{gi}")
        d_h1.append(dh if dil == 1 else _from_stream(dh.reshape(ns, ln, d), bn, dil))
        g_qn.append(d_gain[0]), g_kn.append(d_gain[1])
    d_x1, g_norm1 = _rms_bwd(x1, norm_g[1:2], d_h1, d_x2, "rms1_bwd")

    d_x1b = d_x1.astype(BF16)
    g_wa_out = _matmul(og_a.reshape(t, A_VW), d_x1b, "tn", F32, "a_out_dw", tk=DW_K)
    b_token = b_grads_ready(g_wb_in, g_wb_out, g_wa_out)
    d_og_a = _matmul(d_x1b, wa_out, "nt", BF16, "a_out_dx")
    d_pa, d_gc, d_beta, d_cw, d_ng = _gdn_bwd(proj_a3, cw_hm, beta, gc, a_norm_g + b_token, oraw_a, states,
                                              t_mats, conv_y, d_og_a.reshape(bn, s, A_VW))
    d_tail_t, d_alog, d_dtb = _gdn_prep_bwd(tail_t, a_log[0], a_dt_bias[0], d_gc, d_beta)
    d_tail = jnp.swapaxes(d_tail_t.reshape(bn, 2 * A_HEADS, s), 1, 2).reshape(t, 2 * A_HEADS)
    d_tail = jnp.pad(d_tail, ((0, 0), (0, 128 - 2 * A_HEADS))).astype(BF16)
    d_pa = d_pa.reshape(t, A_MAIN)
    g_wa_main = _matmul(h0, d_pa, "tn", F32, "a_in_dw_main", tk=DW_K)
    g_wa_tail = _matmul(h0, d_tail, "tn", F32, "a_in_dw_tail", tk=DW_K)
    g_wa_in = jnp.concatenate([_a_cols_from_head_major(g_wa_main), g_wa_tail[:, :2 * A_HEADS]], axis=1)
    a_token = a_grads_ready(g_wa_in)
    d_h0t = _matmul(d_tail + a_token.astype(BF16), wa_tail, "nt", F32, "a_in_dx_tail")
    d_x0, g_norm0 = _in_proj_bwd(d_pa, wa_main, d_h0t, x0, norm_g[0:1], d_x1, "a_in_dx_rms0_bwd")

    gfull = {
        "norm_g": jnp.concatenate([g_norm0, g_norm1], axis=0), "a_w_in": g_wa_in,
        "a_conv_w": _conv_cols_from_head_major(jnp.sum(d_cw, axis=0)),
        "a_log": jnp.sum(d_alog[:, :, 0], axis=0), "a_dt_bias": jnp.sum(d_dtb[:, :, 0], axis=0),
        "a_norm_g": jnp.sum(d_ng[:, :, 0, :], axis=(0, 1)), "a_w_out": g_wa_out, "b_w_in": g_wb_in,
        "b_q_norm_g": jnp.stack(g_qn), "b_k_norm_g": jnp.stack(g_kn), "b_w_out": g_wb_out}
    return loss_local, d_x0.reshape(bn, s, d), gfull


def kernel(x, positions, norm_g, a_w_in, a_conv_w, a_log, a_dt_bias, a_norm_g, a_w_out, b_w_in, b_q_norm_g, b_k_norm_g, b_w_out, loss_target, m_norm_g, m_a_w_in, m_a_conv_w, m_a_log, m_a_dt_bias, m_a_norm_g, m_a_w_out, m_b_w_in, m_b_q_norm_g, m_b_k_norm_g, m_b_w_out, v_norm_g, v_a_w_in, v_a_conv_w, v_a_log, v_a_dt_bias, v_a_norm_g, v_a_w_out, v_b_w_in, v_b_q_norm_g, v_b_k_norm_g, v_b_w_out):
    d = x.shape[2]
    my_c = lax.axis_index("c")
    my_chip = 2 * lax.axis_index("x") + lax.axis_index("y")

    half_index = jnp.reshape(my_c, (1,)).astype(jnp.int32)
    chip_index = jnp.reshape(my_chip, (1,)).astype(jnp.int32)
    def landing(shard):
        return lax.dynamic_update_slice(lax.empty((N_CHIPS,) + shard.shape, shard.dtype), shard[None],
                                        (my_chip,) + (0,) * shard.ndim)

    first_shards = [a_w_in[0].astype(BF16), a_conv_w[0]]
    first_plan = _first_gather_plan()
    first = _split_copy_start("first_weights_start", first_plan, first_shards,
                              [landing(s) for s in first_shards], half_index)
    pending = {}
    late_shards = [(w[0] + first[4][0, 0]).astype(BF16) for w in (a_w_out, b_w_in, b_w_out)]
    late_lands = [landing(s) for s in late_shards]

    def first_weights(after):
        _, (ga_in, g_conv) = _split_copy_wait("first_weights_wait", first_plan, *first[:4],
                                              list(after) + late_lands)
        ga_in = _sibling_forward(ga_in)
        wa_in = jnp.concatenate([ga_in[k] for k in range(N_CHIPS)], axis=1)
        conv_w = jnp.concatenate([g_conv[k] for k in range(N_CHIPS)], axis=1)
        plan = _gather_plan(len(late_shards))
        pending["late"] = (plan,) + tuple(_split_copy_start(
            "late_weights_start", plan, late_shards, late_lands, conv_w))
        return wa_in, conv_w, pending["late"][5][0, 0]

    def late_weights(after):
        plan, send, recv, srcs, lands, _ = pending["late"]
        _, (ga_out, gb_in, gb_out) = _split_copy_wait("late_weights_wait", plan, send, recv, srcs, lands, after)
        return ga_out.reshape(A_VW, d), gb_in, gb_out.reshape(B_W, d)

    def reduce_to_chip_sums(mats, tag):
        half = lambda g: lax.dynamic_slice_in_dim(g, (1 - my_c) * (g.shape[1] // 2), g.shape[1] // 2, axis=1)
        recv_sib = _sibling_swap([half(g).astype(BF16) for g in mats], f"grad_{tag}_sibling_swap")
        return [_pair_sum(g, r, half_index, f"grad_{tag}_pair_sum_{i}") for i, (g, r) in enumerate(zip(mats, recv_sib))]

    def start_exchange(tag, mats):
        sums = reduce_to_chip_sums(mats, tag)
        lands = [lax.empty((N_CHIPS - 1,) + s.shape[1:], BF16) for s in sums]
        plan = _exchange_plan(len(mats))
        pending[tag] = (plan,) + tuple(_split_copy_start(f"grad_{tag}_exchange_start", plan, sums, lands, chip_index))
        return pending[tag][5][0, 0]

    def finish_exchange(tag, after):
        plan, send, recv, srcs, lands, _ = pending[tag]
        return _split_copy_wait(f"grad_{tag}_exchange_wait", plan, send, recv, srcs, lands, after)

    def b_grads_ready(g_wb_in, g_wb_out, g_wa_out):
        return start_exchange("b", [g_wb_in, g_wb_out.reshape(N_CHIPS, -1, d), g_wa_out.reshape(N_CHIPS, -1, d)])

    def a_grads_ready(g_wa_in):
        return start_exchange("a", [_shard_major(g_wa_in, a_w_in.shape[2])])

    loss_local, d_x0, gfull = _local_step(x, positions, loss_target, norm_g, a_log, a_dt_bias, a_norm_g,
                                          b_q_norm_g, b_k_norm_g, first[4][0, 0], first_weights, late_weights,
                                          b_grads_ready, a_grads_ready)

    small = [gfull["norm_g"], gfull["a_conv_w"], gfull["a_log"], gfull["a_dt_bias"], gfull["a_norm_g"],
             gfull["b_q_norm_g"], gfull["b_k_norm_g"], loss_local]
    packed, offs = _pack_rows(small)
    reduced = _small_allreduce(packed)
    g_norm, g_conv_all, g_alog, g_dtb, g_ang, g_q, g_k, loss = _unpack_rows(reduced, offs)
    g_conv_mine = lax.dynamic_slice_in_dim(g_conv_all, my_chip * a_conv_w.shape[2], a_conv_w.shape[2], axis=1)

    b_sums, b_received = finish_exchange("b", d_x0)
    a_sums, a_received = finish_exchange("a", reduced)
    chip_sums = [a_sums[0], b_sums[2], b_sums[0], b_sums[1]]
    received = [a_received[0], b_received[2], b_received[0], b_received[1]]
    halves = [_chip_sum(s, r, chip_index, f"grad_chip_sum_{i}") for i, (s, r) in enumerate(zip(chip_sums, received))]
    theirs = _sibling_swap(halves, "grad_sibling_join")
    big = ("a_w_in", "a_w_out", "b_w_in", "b_w_out")
    big_halves = dict(zip(big, zip(halves, theirs)))

    grads = {
        "norm_g": g_norm, "a_conv_w": g_conv_mine[None], "a_log": g_alog[None], "a_dt_bias": g_dtb[None],
        "a_norm_g": g_ang[None], "b_q_norm_g": g_q[None], "b_k_norm_g": g_k[None]}
    weights = {"norm_g": norm_g, "a_w_in": a_w_in, "a_conv_w": a_conv_w, "a_log": a_log, "a_dt_bias": a_dt_bias,
               "a_norm_g": a_norm_g, "a_w_out": a_w_out, "b_w_in": b_w_in, "b_q_norm_g": b_q_norm_g,
               "b_k_norm_g": b_k_norm_g, "b_w_out": b_w_out}
    m_in = {"norm_g": m_norm_g, "a_w_in": m_a_w_in, "a_conv_w": m_a_conv_w, "a_log": m_a_log,
            "a_dt_bias": m_a_dt_bias, "a_norm_g": m_a_norm_g, "a_w_out": m_a_w_out, "b_w_in": m_b_w_in,
            "b_q_norm_g": m_b_q_norm_g, "b_k_norm_g": m_b_k_norm_g, "b_w_out": m_b_w_out}
    v_in = {"norm_g": v_norm_g, "a_w_in": v_a_w_in, "a_conv_w": v_a_conv_w, "a_log": v_a_log,
            "a_dt_bias": v_a_dt_bias, "a_norm_g": v_a_norm_g, "a_w_out": v_a_w_out, "b_w_in": v_b_w_in,
            "b_q_norm_g": v_b_q_norm_g, "b_k_norm_g": v_b_k_norm_g, "b_w_out": v_b_w_out}
    names = list(weights)

    delta_w, new_m, new_v = {}, {}, {}
    for nm in big:
        mine, other = big_halves[nm]
        if weights[nm].shape[2] % 128:
            cols = lambda a: jnp.transpose(a, (2, 0, 1))
            half_cols = lambda a: jnp.transpose(a)[:, None, :]
            outs = _adamw_shard_cols(cols(weights[nm]), half_cols(mine), half_cols(other), cols(m_in[nm]),
                                     cols(v_in[nm]), half_index, f"adamw_{nm}")
            outs = [jnp.transpose(o, (1, 2, 0)) for o in outs]
        else:
            outs = _adamw_shard(weights[nm], mine, other, m_in[nm], v_in[nm], half_index, f"adamw_{nm}")
        grads[nm], delta_w[nm], new_m[nm], new_v[nm] = outs
    small_names = [nm for nm in names if nm not in big]
    packs = [_pack_rows([src[nm] for nm in small_names]) for src in (weights, grads, m_in, v_in)]
    offs = packs[0][1]
    dl, m2, v2 = _adamw(packs[0][0], packs[1][0], packs[2][0], packs[3][0], "adamw_small")
    for nm, a, b, c2 in zip(small_names, _unpack_rows(dl, offs), _unpack_rows(m2, offs), _unpack_rows(v2, offs)):
        delta_w[nm], new_m[nm], new_v[nm] = a, b, c2

    return (loss, d_x0, *[grads[nm] for nm in names], *[delta_w[nm] for nm in names],
            *[new_m[nm] for nm in names], *[new_v[nm] for nm in names])
```
